```python
import jax, jax.numpy as jnp
from jax import lax
import numpy as np

D_MODEL = 1024
BATCH = 8
SEQ = 4096
DEPTH = 2

CHUNK = 128
A_GROUPS = 4
A_GROUP_DIM = 128
A_WIDTH = A_GROUPS * A_GROUP_DIM
DIL_WINDOWS = (128, 512, 2048)
DIL_RATES = (1, 4, 16)
N_DIL = 3
B_HEADS = 8
B_HEAD_DIM = 64
B_WIDTH = B_HEADS * B_HEAD_DIM
ROPE_DIM = B_HEAD_DIM // 4
ROPE_THETA = 500000.0
C_WIDTH = 512
C_KERNEL = 31
D_WIDTH = 512
D_KERNEL = 3
D_FF = -(-8 * D_MODEL // (3 * 256)) * 256
EPS = 1e-6
NEG_INF = -1e30

AB_IN = 2 * A_WIDTH + 3 * N_DIL * B_HEADS * B_HEAD_DIM
AB_OUT = A_WIDTH + B_WIDTH
CD_IN = 2 * C_WIDTH + 3 * D_WIDTH
CD_OUT = C_WIDTH + D_WIDTH
N_AB = (DEPTH + 1) // 2
N_CD = DEPTH // 2

kernel_name = 'hybrid_gmlp_dilattn_conformer_shortconv'


def rms_norm(x, g):
    xf = x.astype(jnp.float32)
    y = xf * lax.rsqrt(jnp.mean(xf * xf, axis=-1, keepdims=True) + EPS)
    return (y * g.astype(jnp.float32)).astype(x.dtype)


def layer_norm(x, g, b):
    xf = x.astype(jnp.float32)
    xc = xf - jnp.mean(xf, axis=-1, keepdims=True)
    y = xc * lax.rsqrt(jnp.mean(xc * xc, axis=-1, keepdims=True) + EPS)
    return (y * g.astype(jnp.float32) + b.astype(jnp.float32)).astype(x.dtype)


def apply_rope(x, pos):
    half = ROPE_DIM // 2
    inv_freq = ROPE_THETA ** (-jnp.arange(half, dtype=jnp.float32) * 2.0 / ROPE_DIM)
    ang = pos[:, None] * inv_freq[None, :]
    cos = jnp.cos(ang)[None, :, None, :]
    sin = jnp.sin(ang)[None, :, None, :]
    xf = x.astype(jnp.float32)
    x1 = xf[..., :half]
    x2 = xf[..., half:ROPE_DIM]
    out = jnp.concatenate([x1 * cos - x2 * sin, x1 * sin + x2 * cos, xf[..., ROPE_DIM:]], axis=-1)
    return out.astype(x.dtype)


def causal_depthwise_conv(x, w):
    k, c = w.shape
    return lax.conv_general_dilated(
        x, w.astype(x.dtype)[:, None, :], window_strides=(1,), padding=[(k - 1, 0)],
        dimension_numbers=('NWC', 'WIO', 'NWC'), feature_group_count=c)


def chunked_spatial_gating(u, v, w_s, b_s):
    bn, s, _ = u.shape
    nc = s // CHUNK
    vg = v.reshape(bn, nc, CHUNK, A_GROUPS, A_GROUP_DIM)
    causal = jnp.tril(jnp.ones((CHUNK, CHUNK), dtype=bool))
    w = jnp.where(causal[None], w_s, 0.0).astype(v.dtype)
    mixed = jnp.einsum('gts,bnsgc->bntgc', w, vg) + b_s.T.astype(v.dtype)[None, None, :, :, None]
    return u * mixed.reshape(bn, s, A_WIDTH)


def dilated_group_attention(q, k, v, rate, n_back):
    bn, s, h, dh = q.shape
    span = rate * n_back
    s_pad = -(-s // span) * span
    length = s_pad // rate
    nb = length // n_back

    def to_blocks(t):
        t = jnp.pad(t, ((0, 0), (0, s_pad - s), (0, 0), (0, 0)))
        t = t.reshape(bn, length, rate, h, dh).transpose(0, 2, 1, 3, 4)
        return t.reshape(bn, rate, nb, n_back, h, dh)

    def with_prev(t):
        prev = jnp.pad(t[:, :, :-1], ((0, 0), (0, 0), (1, 0), (0, 0), (0, 0), (0, 0)))
        return jnp.concatenate([prev, t], axis=3)

    qb = to_blocks(q)
    kk = with_prev(to_blocks(k))
    vv = with_prev(to_blocks(v))
    scores = jnp.einsum('brnqhd,brnkhd->brnhqk', qb.astype(jnp.float32),
                        kk.astype(jnp.float32)) * (dh ** -0.5)
    qi = jnp.arange(n_back)[:, None]
    kj = jnp.arange(2 * n_back)[None, :]
    dist = qi + n_back - kj
    band = (dist >= 0) & (dist <= n_back)
    exists = (jnp.arange(nb)[:, None, None] > 0) | (kj[None] >= n_back)
    mask = band[None] & exists
    scores = jnp.where(mask[None, None, :, None], scores, NEG_INF)
    lse = jax.nn.logsumexp(scores, axis=-1)
    probs = jnp.exp(scores - lse[..., None])
    out = jnp.einsum('brnhqk,brnkhd->brnqhd', probs, vv.astype(jnp.float32))
    out = out.reshape(bn, rate, length, h, dh).transpose(0, 2, 1, 3, 4).reshape(bn, s_pad, h, dh)[:, :s]
    lse = lse.transpose(0, 1, 2, 4, 3).reshape(bn, rate, length, h)
    lse = lse.transpose(0, 2, 1, 3).reshape(bn, s_pad, h)[:, :s]
    return out, lse


def mixer_ab(h, w_in, sgu_g, sgu_b, sgu_w, sgu_bias, q_g, k_g, w_out):
    bn, s, _ = h.shape
    proj = h @ w_in
    a = jax.nn.gelu(proj[..., :2 * A_WIDTH], approximate=False)
    u, v = a[..., :A_WIDTH], a[..., A_WIDTH:]
    a_out = chunked_spatial_gating(u, layer_norm(v, sgu_g, sgu_b), sgu_w, sgu_bias)
    qkv = proj[..., 2 * A_WIDTH:].reshape(bn, s, 3, N_DIL, B_HEADS, B_HEAD_DIM)
    pos = jnp.arange(s, dtype=jnp.float32)
    outs, lses = [], []
    for g in range(N_DIL):
        q = apply_rope(rms_norm(qkv[:, :, 0, g], q_g[g]), pos)
        k = apply_rope(rms_norm(qkv[:, :, 1, g], k_g[g]), pos)
        o, l = dilated_group_attention(q, k, qkv[:, :, 2, g], DIL_RATES[g], DIL_WINDOWS[g] // DIL_RATES[g])
        outs.append(o)
        lses.append(l)
    wts = jax.nn.softmax(jnp.stack(lses), axis=0)
    b_out = jnp.sum(wts[..., None] * jnp.stack(outs), axis=0).astype(h.dtype).reshape(bn, s, B_WIDTH)
    return jnp.concatenate([a_out, b_out], axis=-1) @ w_out


def mixer_cd(h, w_in, conv_c_w, conv_c_b, c_ln_g, c_ln_b, conv_d_w, w_out):
    proj = h @ w_in
    c = proj[..., :C_WIDTH] * jax.nn.sigmoid(proj[..., C_WIDTH:2 * C_WIDTH])
    c = causal_depthwise_conv(c, conv_c_w) + conv_c_b.astype(c.dtype)
    c = jax.nn.silu(layer_norm(c, c_ln_g, c_ln_b))
    gate_b, gate_c, hv = jnp.split(proj[..., 2 * C_WIDTH:], 3, axis=-1)
    d = gate_b * causal_depthwise_conv(gate_c * hv, conv_d_w)
    return jnp.concatenate([c, d], axis=-1) @ w_out


def swiglu(h, w_gate, w_up, w_down):
    return (jax.nn.silu(h @ w_gate) * (h @ w_up)) @ w_down


def _fwd_setup_inputs(seed: int = 0) -> dict:
    key = jax.random.key(seed)
    ks = jax.random.split(key, 24)
    f32 = jnp.float32

    def nrm(k, shape, scale):
        return jax.random.normal(k, shape, f32) * scale

    def gain(k, shape):
        return 1.0 + 0.05 * jax.random.normal(k, shape, f32)

    return {
        'x': jax.random.normal(ks[0], (BATCH, SEQ, D_MODEL), f32),
        'ab_norm_g': gain(ks[1], (N_AB, D_MODEL)),
        'ab_w_in': nrm(ks[2], (N_AB, D_MODEL, AB_IN), D_MODEL ** -0.5),
        'sgu_norm_g': gain(ks[3], (N_AB, A_WIDTH)),
        'sgu_norm_b': nrm(ks[4], (N_AB, A_WIDTH), 0.02),
        'sgu_w': nrm(ks[5], (N_AB, A_GROUPS, CHUNK, CHUNK), 0.5 * CHUNK ** -0.5),
        'sgu_bias': gain(ks[6], (N_AB, A_GROUPS, CHUNK)),
        'q_norm_g': gain(ks[7], (N_AB, N_DIL, B_HEAD_DIM)),
        'k_norm_g': gain(ks[8], (N_AB, N_DIL, B_HEAD_DIM)),
        'ab_w_out': nrm(ks[9], (N_AB, AB_OUT, D_MODEL), AB_OUT ** -0.5),
        'cd_norm_g': gain(ks[10], (N_CD, D_MODEL)),
        'cd_w_in': nrm(ks[11], (N_CD, D_MODEL, CD_IN), D_MODEL ** -0.5),
        'conv_c_w': nrm(ks[12], (N_CD, C_KERNEL, C_WIDTH), C_KERNEL ** -0.5),
        'conv_c_b': nrm(ks[13], (N_CD, C_WIDTH), 0.02),
        'c_ln_g': gain(ks[14], (N_CD, C_WIDTH)),
        'c_ln_b': nrm(ks[15], (N_CD, C_WIDTH), 0.02),
        'conv_d_w': nrm(ks[16], (N_CD, D_KERNEL, D_WIDTH), D_KERNEL ** -0.5),
        'cd_w_out': nrm(ks[17], (N_CD, CD_OUT, D_MODEL), CD_OUT ** -0.5),
        'ffn_norm_g': gain(ks[18], (DEPTH, D_MODEL)),
        'ffn_w_gate': nrm(ks[19], (DEPTH, D_MODEL, D_FF), D_MODEL ** -0.5),
        'ffn_w_up': nrm(ks[20], (DEPTH, D_MODEL, D_FF), D_MODEL ** -0.5),
        'ffn_w_down': nrm(ks[21], (DEPTH, D_FF, D_MODEL), D_FF ** -0.5),
    }


def _fwd_reference(x, ab_norm_g, ab_w_in, sgu_norm_g, sgu_norm_b, sgu_w, sgu_bias, q_norm_g, k_norm_g,
              ab_w_out, cd_norm_g, cd_w_in, conv_c_w, conv_c_b, c_ln_g, c_ln_b, conv_d_w, cd_w_out,
              ffn_norm_g, ffn_w_gate, ffn_w_up, ffn_w_down):
    for layer in range(DEPTH):
        i = layer // 2
        if layer % 2 == 0:
            h = rms_norm(x, ab_norm_g[i])
            x = x + mixer_ab(h, ab_w_in[i], sgu_norm_g[i], sgu_norm_b[i], sgu_w[i], sgu_bias[i],
                             q_norm_g[i], k_norm_g[i], ab_w_out[i])
        else:
            h = rms_norm(x, cd_norm_g[i])
            x = x + mixer_cd(h, cd_w_in[i], conv_c_w[i], conv_c_b[i], c_ln_g[i], c_ln_b[i],
                             conv_d_w[i], cd_w_out[i])
        h = rms_norm(x, ffn_norm_g[layer])
        x = x + swiglu(h, ffn_w_gate[layer], ffn_w_up[layer], ffn_w_down[layer])
    return x


import jax as _jax
import jax.numpy as _jnp

TWIN_FORMAT = 'train_step'
FWD_PARAMS = ['x', 'ab_norm_g', 'ab_w_in', 'sgu_norm_g', 'sgu_norm_b', 'sgu_w', 'sgu_bias', 'q_norm_g', 'k_norm_g', 'ab_w_out', 'cd_norm_g', 'cd_w_in', 'conv_c_w', 'conv_c_b', 'c_ln_g', 'c_ln_b', 'conv_d_w', 'cd_w_out', 'ffn_norm_g', 'ffn_w_gate', 'ffn_w_up', 'ffn_w_down']
TWIN_WEIGHTS = ['ab_norm_g', 'ab_w_in', 'sgu_norm_g', 'sgu_norm_b', 'sgu_w', 'sgu_bias', 'q_norm_g', 'k_norm_g', 'ab_w_out', 'cd_norm_g', 'cd_w_in', 'conv_c_w', 'conv_c_b', 'c_ln_g', 'c_ln_b', 'conv_d_w', 'cd_w_out', 'ffn_norm_g', 'ffn_w_gate', 'ffn_w_up', 'ffn_w_down']
TWIN_DIFF_INPUT = 'x'
TWIN_INPUTS = ['x', 'ab_norm_g', 'ab_w_in', 'sgu_norm_g', 'sgu_norm_b', 'sgu_w', 'sgu_bias', 'q_norm_g', 'k_norm_g', 'ab_w_out', 'cd_norm_g', 'cd_w_in', 'conv_c_w', 'conv_c_b', 'c_ln_g', 'c_ln_b', 'conv_d_w', 'cd_w_out', 'ffn_norm_g', 'ffn_w_gate', 'ffn_w_up', 'ffn_w_down', 'loss_target', 'm_ab_norm_g', 'm_ab_w_in', 'm_sgu_norm_g', 'm_sgu_norm_b', 'm_sgu_w', 'm_sgu_bias', 'm_q_norm_g', 'm_k_norm_g', 'm_ab_w_out', 'm_cd_norm_g', 'm_cd_w_in', 'm_conv_c_w', 'm_conv_c_b', 'm_c_ln_g', 'm_c_ln_b', 'm_conv_d_w', 'm_cd_w_out', 'm_ffn_norm_g', 'm_ffn_w_gate', 'm_ffn_w_up', 'm_ffn_w_down', 'v_ab_norm_g', 'v_ab_w_in', 'v_sgu_norm_g', 'v_sgu_norm_b', 'v_sgu_w', 'v_sgu_bias', 'v_q_norm_g', 'v_k_norm_g', 'v_ab_w_out', 'v_cd_norm_g', 'v_cd_w_in', 'v_conv_c_w', 'v_conv_c_b', 'v_c_ln_g', 'v_c_ln_b', 'v_conv_d_w', 'v_cd_w_out', 'v_ffn_norm_g', 'v_ffn_w_gate', 'v_ffn_w_up', 'v_ffn_w_down']
TWIN_OUTPUTS = ['loss', 'grad_x', 'grad_ab_norm_g', 'grad_ab_w_in', 'grad_sgu_norm_g', 'grad_sgu_norm_b', 'grad_sgu_w', 'grad_sgu_bias', 'grad_q_norm_g', 'grad_k_norm_g', 'grad_ab_w_out', 'grad_cd_norm_g', 'grad_cd_w_in', 'grad_conv_c_w', 'grad_conv_c_b', 'grad_c_ln_g', 'grad_c_ln_b', 'grad_conv_d_w', 'grad_cd_w_out', 'grad_ffn_norm_g', 'grad_ffn_w_gate', 'grad_ffn_w_up', 'grad_ffn_w_down', 'delta_ab_norm_g', 'delta_ab_w_in', 'delta_sgu_norm_g', 'delta_sgu_norm_b', 'delta_sgu_w', 'delta_sgu_bias', 'delta_q_norm_g', 'delta_k_norm_g', 'delta_ab_w_out', 'delta_cd_norm_g', 'delta_cd_w_in', 'delta_conv_c_w', 'delta_conv_c_b', 'delta_c_ln_g', 'delta_c_ln_b', 'delta_conv_d_w', 'delta_cd_w_out', 'delta_ffn_norm_g', 'delta_ffn_w_gate', 'delta_ffn_w_up', 'delta_ffn_w_down', 'new_m_ab_norm_g', 'new_m_ab_w_in', 'new_m_sgu_norm_g', 'new_m_sgu_norm_b', 'new_m_sgu_w', 'new_m_sgu_bias', 'new_m_q_norm_g', 'new_m_k_norm_g', 'new_m_ab_w_out', 'new_m_cd_norm_g', 'new_m_cd_w_in', 'new_m_conv_c_w', 'new_m_conv_c_b', 'new_m_c_ln_g', 'new_m_c_ln_b', 'new_m_conv_d_w', 'new_m_cd_w_out', 'new_m_ffn_norm_g', 'new_m_ffn_w_gate', 'new_m_ffn_w_up', 'new_m_ffn_w_down', 'new_v_ab_norm_g', 'new_v_ab_w_in', 'new_v_sgu_norm_g', 'new_v_sgu_norm_b', 'new_v_sgu_w', 'new_v_sgu_bias', 'new_v_q_norm_g', 'new_v_k_norm_g', 'new_v_ab_w_out', 'new_v_cd_norm_g', 'new_v_cd_w_in', 'new_v_conv_c_w', 'new_v_conv_c_b', 'new_v_c_ln_g', 'new_v_c_ln_b', 'new_v_conv_d_w', 'new_v_cd_w_out', 'new_v_ffn_norm_g', 'new_v_ffn_w_gate', 'new_v_ffn_w_up', 'new_v_ffn_w_down']
TWIN_LEAF_KINDS = {'loss': 'loss', 'grad_x': 'grad_x', 'grad_ab_norm_g': 'grad_w', 'grad_ab_w_in': 'grad_w', 'grad_sgu_norm_g': 'grad_w', 'grad_sgu_norm_b': 'grad_w', 'grad_sgu_w': 'grad_w', 'grad_sgu_bias': 'grad_w', 'grad_q_norm_g': 'grad_w', 'grad_k_norm_g': 'grad_w', 'grad_ab_w_out': 'grad_w', 'grad_cd_norm_g': 'grad_w', 'grad_cd_w_in': 'grad_w', 'grad_conv_c_w': 'grad_w', 'grad_conv_c_b': 'grad_w', 'grad_c_ln_g': 'grad_w', 'grad_c_ln_b': 'grad_w', 'grad_conv_d_w': 'grad_w', 'grad_cd_w_out': 'grad_w', 'grad_ffn_norm_g': 'grad_w', 'grad_ffn_w_gate': 'grad_w', 'grad_ffn_w_up': 'grad_w', 'grad_ffn_w_down': 'grad_w', 'delta_ab_norm_g': 'delta_w', 'delta_ab_w_in': 'delta_w', 'delta_sgu_norm_g': 'delta_w', 'delta_sgu_norm_b': 'delta_w', 'delta_sgu_w': 'delta_w', 'delta_sgu_bias': 'delta_w', 'delta_q_norm_g': 'delta_w', 'delta_k_norm_g': 'delta_w', 'delta_ab_w_out': 'delta_w', 'delta_cd_norm_g': 'delta_w', 'delta_cd_w_in': 'delta_w', 'delta_conv_c_w': 'delta_w', 'delta_conv_c_b': 'delta_w', 'delta_c_ln_g': 'delta_w', 'delta_c_ln_b': 'delta_w', 'delta_conv_d_w': 'delta_w', 'delta_cd_w_out': 'delta_w', 'delta_ffn_norm_g': 'delta_w', 'delta_ffn_w_gate': 'delta_w', 'delta_ffn_w_up': 'delta_w', 'delta_ffn_w_down': 'delta_w', 'new_m_ab_norm_g': 'new_m', 'new_m_ab_w_in': 'new_m', 'new_m_sgu_norm_g': 'new_m', 'new_m_sgu_norm_b': 'new_m', 'new_m_sgu_w': 'new_m', 'new_m_sgu_bias': 'new_m', 'new_m_q_norm_g': 'new_m', 'new_m_k_norm_g': 'new_m', 'new_m_ab_w_out': 'new_m', 'new_m_cd_norm_g': 'new_m', 'new_m_cd_w_in': 'new_m', 'new_m_conv_c_w': 'new_m', 'new_m_conv_c_b': 'new_m', 'new_m_c_ln_g': 'new_m', 'new_m_c_ln_b': 'new_m', 'new_m_conv_d_w': 'new_m', 'new_m_cd_w_out': 'new_m', 'new_m_ffn_norm_g': 'new_m', 'new_m_ffn_w_gate': 'new_m', 'new_m_ffn_w_up': 'new_m', 'new_m_ffn_w_down': 'new_m', 'new_v_ab_norm_g': 'new_v', 'new_v_ab_w_in': 'new_v', 'new_v_sgu_norm_g': 'new_v', 'new_v_sgu_norm_b': 'new_v', 'new_v_sgu_w': 'new_v', 'new_v_sgu_bias': 'new_v', 'new_v_q_norm_g': 'new_v', 'new_v_k_norm_g': 'new_v', 'new_v_ab_w_out': 'new_v', 'new_v_cd_norm_g': 'new_v', 'new_v_cd_w_in': 'new_v', 'new_v_conv_c_w': 'new_v', 'new_v_conv_c_b': 'new_v', 'new_v_c_ln_g': 'new_v', 'new_v_c_ln_b': 'new_v', 'new_v_conv_d_w': 'new_v', 'new_v_cd_w_out': 'new_v', 'new_v_ffn_norm_g': 'new_v', 'new_v_ffn_w_gate': 'new_v', 'new_v_ffn_w_up': 'new_v', 'new_v_ffn_w_down': 'new_v'}


def _forward(args):
    return _fwd_reference(*[args[k] for k in FWD_PARAMS])


def _output_shape():
    out = _jax.eval_shape(lambda: _forward(_fwd_setup_inputs(0)))
    return out.shape, out.dtype

N_MICROBATCH = 1
ADAM_LR = 0.001
ADAM_B1 = 0.9
ADAM_B2 = 0.999
ADAM_EPS = 1e-08
ADAM_WD = 0.01
ADAM_STEP = 10
PER_EXAMPLE_BATCH_AXIS = {'x': 0, 'loss_target': 0}
SHARED_INPUTS = []
_WEIGHT_DTYPES = {'ab_norm_g': _jnp.float32, 'ab_w_in': _jnp.float32, 'sgu_norm_g': _jnp.float32, 'sgu_norm_b': _jnp.float32, 'sgu_w': _jnp.float32, 'sgu_bias': _jnp.float32, 'q_norm_g': _jnp.float32, 'k_norm_g': _jnp.float32, 'ab_w_out': _jnp.float32, 'cd_norm_g': _jnp.float32, 'cd_w_in': _jnp.float32, 'conv_c_w': _jnp.float32, 'conv_c_b': _jnp.float32, 'c_ln_g': _jnp.float32, 'c_ln_b': _jnp.float32, 'conv_d_w': _jnp.float32, 'cd_w_out': _jnp.float32, 'ffn_norm_g': _jnp.float32, 'ffn_w_gate': _jnp.float32, 'ffn_w_up': _jnp.float32, 'ffn_w_down': _jnp.float32}
MOMENT_SCALE = {'ab_norm_g': 7.963523e+00, 'ab_w_in': 2.961015e-01, 'sgu_norm_g': 1.424647e+00, 'sgu_norm_b': 5.949403e-01, 'sgu_w': 5.742847e-01, 'sgu_bias': 1.290752e+01, 'q_norm_g': 5.557424e-01, 'k_norm_g': 5.527109e-01, 'ab_w_out': 2.011038e+00, 'cd_norm_g': 4.760943e+01, 'cd_w_in': 1.039927e+00, 'conv_c_w': 7.624547e-01, 'conv_c_b': 8.577930e+00, 'c_ln_g': 1.494959e+01, 'c_ln_b': 1.031832e+01, 'conv_d_w': 1.859644e+01, 'cd_w_out': 1.370671e+00, 'ffn_norm_g': 2.473348e+01, 'ffn_w_gate': 4.691302e-01, 'ffn_w_up': 3.389844e-01, 'ffn_w_down': 5.412167e-01}


def _to_microbatches(a, axis):
    t = _jnp.moveaxis(a, axis, 0)
    t = t.reshape((N_MICROBATCH, t.shape[0] // N_MICROBATCH) + t.shape[1:])
    return _jnp.moveaxis(t, 1, axis + 1)


def setup_inputs(seed: int = 0) -> dict:
    inp = _fwd_setup_inputs(seed)
    key = _jax.random.fold_in(_jax.random.key(seed), 7919)
    shape, _ = _output_shape()
    out = dict(inp)
    out["loss_target"] = _jax.random.normal(_jax.random.fold_in(key, 0), shape, _jnp.float32)
    for i, name in enumerate(TWIN_WEIGHTS):
        w = inp[name].astype(_jnp.float32)
        if MOMENT_SCALE is None:
            s = _jnp.sqrt(_jnp.mean(_jnp.square(w)) + 1e-30)
        else:
            s = MOMENT_SCALE[name]
        km, kv = _jax.random.split(_jax.random.fold_in(key, i + 1))
        out[name] = w
        out["m_" + name] = s * _jax.random.normal(km, w.shape, _jnp.float32)
        out["v_" + name] = (s * s) * _jax.random.uniform(kv, w.shape, _jnp.float32, 0.5, 1.5)
    if N_MICROBATCH > 1:
        for name, axis in PER_EXAMPLE_BATCH_AXIS.items():
            out[name] = _to_microbatches(out[name], axis)
    return {'x': out['x'], 'ab_norm_g': out['ab_norm_g'], 'ab_w_in': out['ab_w_in'], 'sgu_norm_g': out['sgu_norm_g'], 'sgu_norm_b': out['sgu_norm_b'], 'sgu_w': out['sgu_w'], 'sgu_bias': out['sgu_bias'], 'q_norm_g': out['q_norm_g'], 'k_norm_g': out['k_norm_g'], 'ab_w_out': out['ab_w_out'], 'cd_norm_g': out['cd_norm_g'], 'cd_w_in': out['cd_w_in'], 'conv_c_w': out['conv_c_w'], 'conv_c_b': out['conv_c_b'], 'c_ln_g': out['c_ln_g'], 'c_ln_b': out['c_ln_b'], 'conv_d_w': out['conv_d_w'], 'cd_w_out': out['cd_w_out'], 'ffn_norm_g': out['ffn_norm_g'], 'ffn_w_gate': out['ffn_w_gate'], 'ffn_w_up': out['ffn_w_up'], 'ffn_w_down': out['ffn_w_down'], 'loss_target': out['loss_target'], 'm_ab_norm_g': out['m_ab_norm_g'], 'm_ab_w_in': out['m_ab_w_in'], 'm_sgu_norm_g': out['m_sgu_norm_g'], 'm_sgu_norm_b': out['m_sgu_norm_b'], 'm_sgu_w': out['m_sgu_w'], 'm_sgu_bias': out['m_sgu_bias'], 'm_q_norm_g': out['m_q_norm_g'], 'm_k_norm_g': out['m_k_norm_g'], 'm_ab_w_out': out['m_ab_w_out'], 'm_cd_norm_g': out['m_cd_norm_g'], 'm_cd_w_in': out['m_cd_w_in'], 'm_conv_c_w': out['m_conv_c_w'], 'm_conv_c_b': out['m_conv_c_b'], 'm_c_ln_g': out['m_c_ln_g'], 'm_c_ln_b': out['m_c_ln_b'], 'm_conv_d_w': out['m_conv_d_w'], 'm_cd_w_out': out['m_cd_w_out'], 'm_ffn_norm_g': out['m_ffn_norm_g'], 'm_ffn_w_gate': out['m_ffn_w_gate'], 'm_ffn_w_up': out['m_ffn_w_up'], 'm_ffn_w_down': out['m_ffn_w_down'], 'v_ab_norm_g': out['v_ab_norm_g'], 'v_ab_w_in': out['v_ab_w_in'], 'v_sgu_norm_g': out['v_sgu_norm_g'], 'v_sgu_norm_b': out['v_sgu_norm_b'], 'v_sgu_w': out['v_sgu_w'], 'v_sgu_bias': out['v_sgu_bias'], 'v_q_norm_g': out['v_q_norm_g'], 'v_k_norm_g': out['v_k_norm_g'], 'v_ab_w_out': out['v_ab_w_out'], 'v_cd_norm_g': out['v_cd_norm_g'], 'v_cd_w_in': out['v_cd_w_in'], 'v_conv_c_w': out['v_conv_c_w'], 'v_conv_c_b': out['v_conv_c_b'], 'v_c_ln_g': out['v_c_ln_g'], 'v_c_ln_b': out['v_c_ln_b'], 'v_conv_d_w': out['v_conv_d_w'], 'v_cd_w_out': out['v_cd_w_out'], 'v_ffn_norm_g': out['v_ffn_norm_g'], 'v_ffn_w_gate': out['v_ffn_w_gate'], 'v_ffn_w_up': out['v_ffn_w_up'], 'v_ffn_w_down': out['v_ffn_w_down']}


def _loss(weights, diff, rest, loss_target):
    with _jax.named_scope("forward"):
        args = {**rest, TWIN_DIFF_INPUT: diff, **{k: w.astype(_WEIGHT_DTYPES[k]) for k, w in weights.items()}}
        y = _forward(args)
    with _jax.named_scope("loss_head"):
        err = _jnp.square(y.astype(_jnp.float32) - loss_target)
        return 0.5 * _jnp.sum(_jnp.mean(err, axis=-1)) if err.ndim else 0.5 * err


def _adamw(w, g, m, v):
    m = ADAM_B1 * m + (1.0 - ADAM_B1) * g
    v = ADAM_B2 * v + (1.0 - ADAM_B2) * _jnp.square(g)
    m_hat = m / (1.0 - ADAM_B1 ** ADAM_STEP)
    v_hat = v / (1.0 - ADAM_B2 ** ADAM_STEP)
    delta = -ADAM_LR * (m_hat / (_jnp.sqrt(v_hat) + ADAM_EPS) + ADAM_WD * w)
    return delta, m, v


def reference(x, ab_norm_g, ab_w_in, sgu_norm_g, sgu_norm_b, sgu_w, sgu_bias, q_norm_g, k_norm_g, ab_w_out, cd_norm_g, cd_w_in, conv_c_w, conv_c_b, c_ln_g, c_ln_b, conv_d_w, cd_w_out, ffn_norm_g, ffn_w_gate, ffn_w_up, ffn_w_down, loss_target, m_ab_norm_g, m_ab_w_in, m_sgu_norm_g, m_sgu_norm_b, m_sgu_w, m_sgu_bias, m_q_norm_g, m_k_norm_g, m_ab_w_out, m_cd_norm_g, m_cd_w_in, m_conv_c_w, m_conv_c_b, m_c_ln_g, m_c_ln_b, m_conv_d_w, m_cd_w_out, m_ffn_norm_g, m_ffn_w_gate, m_ffn_w_up, m_ffn_w_down, v_ab_norm_g, v_ab_w_in, v_sgu_norm_g, v_sgu_norm_b, v_sgu_w, v_sgu_bias, v_q_norm_g, v_k_norm_g, v_ab_w_out, v_cd_norm_g, v_cd_w_in, v_conv_c_w, v_conv_c_b, v_c_ln_g, v_c_ln_b, v_conv_d_w, v_cd_w_out, v_ffn_norm_g, v_ffn_w_gate, v_ffn_w_up, v_ffn_w_down):
    given = dict(x=x, ab_norm_g=ab_norm_g, ab_w_in=ab_w_in, sgu_norm_g=sgu_norm_g, sgu_norm_b=sgu_norm_b, sgu_w=sgu_w, sgu_bias=sgu_bias, q_norm_g=q_norm_g, k_norm_g=k_norm_g, ab_w_out=ab_w_out, cd_norm_g=cd_norm_g, cd_w_in=cd_w_in, conv_c_w=conv_c_w, conv_c_b=conv_c_b, c_ln_g=c_ln_g, c_ln_b=c_ln_b, conv_d_w=conv_d_w, cd_w_out=cd_w_out, ffn_norm_g=ffn_norm_g, ffn_w_gate=ffn_w_gate, ffn_w_up=ffn_w_up, ffn_w_down=ffn_w_down, loss_target=loss_target, m_ab_norm_g=m_ab_norm_g, m_ab_w_in=m_ab_w_in, m_sgu_norm_g=m_sgu_norm_g, m_sgu_norm_b=m_sgu_norm_b, m_sgu_w=m_sgu_w, m_sgu_bias=m_sgu_bias, m_q_norm_g=m_q_norm_g, m_k_norm_g=m_k_norm_g, m_ab_w_out=m_ab_w_out, m_cd_norm_g=m_cd_norm_g, m_cd_w_in=m_cd_w_in, m_conv_c_w=m_conv_c_w, m_conv_c_b=m_conv_c_b, m_c_ln_g=m_c_ln_g, m_c_ln_b=m_c_ln_b, m_conv_d_w=m_conv_d_w, m_cd_w_out=m_cd_w_out, m_ffn_norm_g=m_ffn_norm_g, m_ffn_w_gate=m_ffn_w_gate, m_ffn_w_up=m_ffn_w_up, m_ffn_w_down=m_ffn_w_down, v_ab_norm_g=v_ab_norm_g, v_ab_w_in=v_ab_w_in, v_sgu_norm_g=v_sgu_norm_g, v_sgu_norm_b=v_sgu_norm_b, v_sgu_w=v_sgu_w, v_sgu_bias=v_sgu_bias, v_q_norm_g=v_q_norm_g, v_k_norm_g=v_k_norm_g, v_ab_w_out=v_ab_w_out, v_cd_norm_g=v_cd_norm_g, v_cd_w_in=v_cd_w_in, v_conv_c_w=v_conv_c_w, v_conv_c_b=v_conv_c_b, v_c_ln_g=v_c_ln_g, v_c_ln_b=v_c_ln_b, v_conv_d_w=v_conv_d_w, v_cd_w_out=v_cd_w_out, v_ffn_norm_g=v_ffn_norm_g, v_ffn_w_gate=v_ffn_w_gate, v_ffn_w_up=v_ffn_w_up, v_ffn_w_down=v_ffn_w_down)
    weights = {n: given[n] for n in TWIN_WEIGHTS}
    shared = {n: given[n] for n in SHARED_INPUTS}
    per_example = {n: given[n] for n in ['x']}
    grad_fn = _jax.value_and_grad(_loss, argnums=(0, 1))

    def one_microbatch(ex, loss_target):
        ex = dict(ex)
        diff = ex.pop(TWIN_DIFF_INPUT)
        return grad_fn(weights, diff, {**shared, **ex}, loss_target)

    if N_MICROBATCH == 1:
        loss, (grad_w, grad_x) = one_microbatch(per_example, given["loss_target"])
    else:
        def body(carry, xs):
            loss_sum, grad_sum = carry
            l_k, (gw_k, gx_k) = one_microbatch(xs[0], xs[1])
            with _jax.named_scope("update"):
                return (loss_sum + l_k, _jax.tree.map(_jnp.add, grad_sum, gw_k)), gx_k

        init = (_jnp.zeros((), _jnp.float32), _jax.tree.map(_jnp.zeros_like, weights))
        (loss, grad_w), grad_x = _jax.lax.scan(body, init, (per_example, given["loss_target"]))
    with _jax.named_scope("update"):
        delta_w, new_m, new_v = {}, {}, {}
        for n in TWIN_WEIGHTS:
            delta_w[n], new_m[n], new_v[n] = _adamw(weights[n], grad_w[n], given["m_" + n], given["v_" + n])
    return (loss, grad_x, *[grad_w[n] for n in TWIN_WEIGHTS], *[delta_w[n] for n in TWIN_WEIGHTS],
            *[new_m[n] for n in TWIN_WEIGHTS], *[new_v[n] for n in TWIN_WEIGHTS])
```

```python
import functools

import jax
import jax.numpy as jnp
from jax import lax
from jax.experimental import pallas as pl
from jax.experimental.pallas import tpu as pltpu

F32 = jnp.float32
BF16 = jnp.bfloat16

T = 4096
D = 1024
NDEV = 8
EPS = 1e-6
NEG_INF = -1e30
DFF = 2816
AB_IN = 5632
CD_IN = 2560
HEAD = 64
PAIR = 128
NPAIR = 4
NBACK = 128
DIL_RATES = (1, 4, 16)
ROPE_HALF = 8
ROPE_THETA = 500000.0
CONV_C_TAPS = 31
CONV_D_TAPS = 3
HALO = 32

ADAM_LR = 0.001
ADAM_B1 = 0.9
ADAM_B2 = 0.999
ADAM_EPS = 1e-08
ADAM_WD = 0.01
ADAM_STEP = 10
ADAM_C1 = 1.0 / (1.0 - ADAM_B1 ** ADAM_STEP)
ADAM_C2 = 1.0 / (1.0 - ADAM_B2 ** ADAM_STEP)

VMEM_LIMIT_MB = 48
MESH = pl.DeviceIdType.MESH


def _cparams(ngrid, vmem_mb=VMEM_LIMIT_MB):
    return pltpu.CompilerParams(dimension_semantics=("arbitrary",) * ngrid,
                                vmem_limit_bytes=vmem_mb * 1024 * 1024)


def _pick(n, options):
    for o in options:
        if n % o == 0:
            return o
    raise ValueError(f"no tile for {n} in {options}")


def _sds(shape, dtype):
    return jax.ShapeDtypeStruct(shape, dtype)


def _sigmoid(x):
    return 1.0 / (1.0 + jnp.exp(-x))


def _gelu(z):
    return 0.5 * z * (1.0 + lax.erf(z * 0.7071067811865476))


def _gelu_grad(z):
    return 0.5 * (1.0 + lax.erf(z * 0.7071067811865476)) + z * jnp.exp(-0.5 * z * z) * 0.3989422804014327


def _mm_nt(a, wt, name, out_dtype=BF16, tm=1024):
    M, K = a.shape
    N = wt.shape[0]
    tn = _pick(N, (512, 256))

    def body(a_ref, w_ref, o_ref):
        o_ref[...] = lax.dot_general(a_ref[...], w_ref[...], (((1,), (1,)), ((), ())),
                                     preferred_element_type=F32).astype(o_ref.dtype)

    return pl.pallas_call(
        body, name=name, grid=(M // tm, N // tn),
        in_specs=[pl.BlockSpec((tm, K), lambda i, j: (i, 0)), pl.BlockSpec((tn, K), lambda i, j: (j, 0))],
        out_specs=pl.BlockSpec((tm, tn), lambda i, j: (i, j)),
        out_shape=_sds((M, N), out_dtype), compiler_params=_cparams(2))(a, wt)


def _mm_nn(a, w, name, resid=None, out_dtype=F32, tm=1024):
    M, K = a.shape
    N = w.shape[1]
    tk = _pick(K, (1408, 1280, 1024, 512))
    nk = K // tk
    has_resid = resid is not None

    def body(*refs):
        a_ref, w_ref = refs[0], refs[1]
        r_ref = refs[2] if has_resid else None
        o_ref, acc = refs[-2], refs[-1]
        k = pl.program_id(1)

        @pl.when(k == 0)
        def _():
            acc[...] = jnp.zeros_like(acc)

        acc[...] += jnp.dot(a_ref[...], w_ref[...], preferred_element_type=F32)

        @pl.when(k == nk - 1)
        def _():
            v = acc[...]
            if has_resid:
                v = v + r_ref[...]
            o_ref[...] = v.astype(o_ref.dtype)

    in_specs = [pl.BlockSpec((tm, tk), lambda i, k: (i, k)), pl.BlockSpec((tk, N), lambda i, k: (k, 0))]
    args = [a, w]
    if has_resid:
        in_specs.append(pl.BlockSpec((tm, N), lambda i, k: (i, 0)))
        args.append(resid)
    return pl.pallas_call(
        body, name=name, grid=(M // tm, nk), in_specs=in_specs,
        out_specs=pl.BlockSpec((tm, N), lambda i, k: (i, 0)),
        out_shape=_sds((M, N), out_dtype), scratch_shapes=[pltpu.VMEM((tm, N), F32)],
        compiler_params=_cparams(2))(*args)


def _mm_tn(a, b, name, out_dtype=BF16, tt=512):
    Tt, M = a.shape
    N = b.shape[1]
    tn = _pick(M, (1408, 1280, 1024, 512))
    nt = Tt // tt

    def body(a_ref, b_ref, o_ref, acc):
        t = pl.program_id(1)

        @pl.when(t == 0)
        def _():
            acc[...] = jnp.zeros_like(acc)

        acc[...] += lax.dot_general(a_ref[...], b_ref[...], (((0,), (0,)), ((), ())),
                                    preferred_element_type=F32)

        @pl.when(t == nt - 1)
        def _():
            o_ref[...] = acc[...].astype(o_ref.dtype)

    return pl.pallas_call(
        body, name=name, grid=(M // tn, nt),
        in_specs=[pl.BlockSpec((tt, tn), lambda j, t: (t, j)), pl.BlockSpec((tt, N), lambda j, t: (t, 0))],
        out_specs=pl.BlockSpec((tn, N), lambda j, t: (j, 0)),
        out_shape=_sds((M, N), out_dtype), scratch_shapes=[pltpu.VMEM((tn, N), F32)],
        compiler_params=_cparams(2))(a, b)


def _rms_fwd(x, g, name, tm=512):
    def body(x_ref, g_ref, h_ref):
        xf = x_ref[...]
        r = lax.rsqrt(jnp.mean(xf * xf, axis=-1, keepdims=True) + EPS)
        h_ref[...] = (xf * r * g_ref[...]).astype(BF16)

    return pl.pallas_call(
        body, name=name, grid=(T // tm,),
        in_specs=[pl.BlockSpec((tm, D), lambda i: (i, 0)), pl.BlockSpec((1, D), lambda i: (0, 0))],
        out_specs=pl.BlockSpec((tm, D), lambda i: (i, 0)),
        out_shape=_sds((T, D), BF16), compiler_params=_cparams(1))(x, g)


def _rms_bwd(x, g, dh, dres, name, tm=512):
    def body(x_ref, g_ref, dh_ref, dres_ref, dx_ref, dxb_ref, dg_ref):
        i = pl.program_id(0)
        xf = x_ref[...]
        r = lax.rsqrt(jnp.mean(xf * xf, axis=-1, keepdims=True) + EPS)
        xhat = xf * r
        dhf = dh_ref[...].astype(F32)

        @pl.when(i == 0)
        def _():
            dg_ref[...] = jnp.zeros_like(dg_ref)

        dg_ref[...] += jnp.sum(dhf * xhat, axis=0, keepdims=True)
        dxh = dhf * g_ref[...]
        dx = r * (dxh - xhat * jnp.mean(dxh * xhat, axis=-1, keepdims=True))
        tot = dres_ref[...] + dx
        dx_ref[...] = tot
        dxb_ref[...] = tot.astype(BF16)

    row = pl.BlockSpec((tm, D), lambda i: (i, 0))
    vec = pl.BlockSpec((1, D), lambda i: (0, 0))
    return pl.pallas_call(
        body, name=name, grid=(T // tm,), in_specs=[row, vec, row, row],
        out_specs=[row, row, vec],
        out_shape=[_sds((T, D), F32), _sds((T, D), BF16), _sds((1, D), F32)],
        compiler_params=_cparams(1))(x, g, dh, dres)


def _loss_fwd_bwd(y, tgt, tm=512):
    def body(y_ref, t_ref, dy_ref, dyb_ref, l_ref):
        i = pl.program_id(0)
        d = y_ref[...] - t_ref[...]

        @pl.when(i == 0)
        def _():
            l_ref[...] = jnp.zeros_like(l_ref)

        l_ref[...] += jnp.sum(d * d, axis=0, keepdims=True) * (0.5 / D)
        dy = d * (1.0 / D)
        dy_ref[...] = dy
        dyb_ref[...] = dy.astype(BF16)

    row = pl.BlockSpec((tm, D), lambda i: (i, 0))
    vec = pl.BlockSpec((1, D), lambda i: (0, 0))
    return pl.pallas_call(
        body, name="loss", grid=(T // tm,), in_specs=[row, row], out_specs=[row, row, vec],
        out_shape=[_sds((T, D), F32), _sds((T, D), BF16), _sds((1, D), F32)],
        compiler_params=_cparams(1))(y, tgt)


def _swiglu_fwd(p, name, tm=512):
    def body(g_ref, u_ref, o_ref):
        g = g_ref[...].astype(F32)
        o_ref[...] = (g * _sigmoid(g) * u_ref[...].astype(F32)).astype(BF16)

    return pl.pallas_call(
        body, name=name, grid=(T // tm,),
        in_specs=[pl.BlockSpec((tm, DFF), lambda i: (i, 0)), pl.BlockSpec((tm, DFF), lambda i: (i, 1))],
        out_specs=pl.BlockSpec((tm, DFF), lambda i: (i, 0)),
        out_shape=_sds((T, DFF), BF16), compiler_params=_cparams(1))(p, p)


def _swiglu_bwd(p, dact, name, tm=512):
    def body(g_ref, u_ref, da_ref, o_ref):
        g = g_ref[...].astype(F32)
        u = u_ref[...].astype(F32)
        da = da_ref[...].astype(F32)
        sg = _sigmoid(g)
        o_ref[:, 0:DFF] = (da * u * sg * (1.0 + g * (1.0 - sg))).astype(BF16)
        o_ref[:, DFF:2 * DFF] = (da * g * sg).astype(BF16)

    return pl.pallas_call(
        body, name=name, grid=(T // tm,),
        in_specs=[pl.BlockSpec((tm, DFF), lambda i: (i, 0)), pl.BlockSpec((tm, DFF), lambda i: (i, 1)),
                  pl.BlockSpec((tm, DFF), lambda i: (i, 0))],
        out_specs=pl.BlockSpec((tm, 2 * DFF), lambda i: (i, 0)),
        out_shape=_sds((T, 2 * DFF), BF16), compiler_params=_cparams(1))(p, p, dact)


def _tril_mask():
    r = lax.broadcasted_iota(jnp.int32, (128, 128), 0)
    c = lax.broadcasted_iota(jnp.int32, (128, 128), 1)
    return r >= c


def _mix_a_fwd(pab, sgu_g, sgu_b, sgu_w, sgu_bias3, tm=512):
    def body(zu_ref, zv_ref, g_ref, b_ref, w_ref, bias_ref, o_ref):
        u = _gelu(zu_ref[...].astype(F32))
        v = _gelu(zv_ref[...].astype(F32))
        mu = jnp.mean(v, axis=-1, keepdims=True)
        vc = v - mu
        rstd = lax.rsqrt(jnp.mean(vc * vc, axis=-1, keepdims=True) + EPS)
        vn = (vc * rstd * g_ref[...] + b_ref[...]).astype(BF16)
        tri = _tril_mask()
        for gi in range(4):
            wg = jnp.where(tri, w_ref[gi], 0.0).astype(BF16)
            bg = bias_ref[gi]
            for c in range(tm // 128):
                rs, cs = slice(c * 128, (c + 1) * 128), slice(gi * 128, (gi + 1) * 128)
                mixed = jnp.dot(wg, vn[rs, cs], preferred_element_type=F32) + bg
                o_ref[rs, cs] = (u[rs, cs] * mixed).astype(BF16)

    half = pl.BlockSpec((tm, 512), lambda i: (i, 0))
    return pl.pallas_call(
        body, name="mix_a_fwd", grid=(T // tm,),
        in_specs=[half, pl.BlockSpec((tm, 512), lambda i: (i, 1)),
                  pl.BlockSpec((1, 512), lambda i: (0, 0)), pl.BlockSpec((1, 512), lambda i: (0, 0)),
                  pl.BlockSpec((4, 128, 128), lambda i: (0, 0, 0)), pl.BlockSpec((4, 128, 1), lambda i: (0, 0, 0))],
        out_specs=half, out_shape=_sds((T, D), BF16), compiler_params=_cparams(1),
    )(pab, pab, sgu_g, sgu_b, sgu_w, sgu_bias3)


def _rope_tables():
    pos = jnp.arange(T, dtype=F32)
    inv_freq = ROPE_THETA ** (-jnp.arange(ROPE_HALF, dtype=F32) * 2.0 / (2 * ROPE_HALF))
    ang = pos[:, None] * inv_freq[None, :]
    cos, sin = jnp.cos(ang), jnp.sin(ang)
    z8 = jnp.zeros((T, ROPE_HALF), F32)
    rest = HEAD - 2 * ROPE_HALF
    c64 = jnp.concatenate([cos, cos, jnp.ones((T, rest), F32)], axis=1)
    s1 = jnp.concatenate([z8, sin, jnp.zeros((T, rest), F32)], axis=1)
    s2 = jnp.concatenate([-sin, z8, jnp.zeros((T, rest), F32)], axis=1)
    return jnp.tile(c64, (1, 2)), jnp.tile(s1, (1, 2)), jnp.tile(s2, (1, 2))


def _lo_mask(shape):
    return lax.broadcasted_iota(jnp.int32, shape, 1) < HEAD


def _seg_mean(x, lo):
    s_all = jnp.sum(x, axis=-1, keepdims=True)
    s_lo = jnp.sum(jnp.where(lo, x, 0.0), axis=-1, keepdims=True)
    return jnp.where(lo, s_lo, s_all - s_lo) * (1.0 / HEAD)


def _rope(n, c, s1, s2):
    return n * c + pltpu.roll(n, ROPE_HALF, 1) * s1 + pltpu.roll(n, PAIR - ROPE_HALF, 1) * s2


def _rope_t(dy, c, s1, s2):
    return dy * c - pltpu.roll(dy, PAIR - ROPE_HALF, 1) * s2 - pltpu.roll(dy, ROPE_HALF, 1) * s1


def _prep_fwd(pab, qg, kg, tabs, tm=512):
    def body(p_ref, qg_ref, kg_ref, c_ref, s1_ref, s2_ref, *outs):
        lo = _lo_mask((tm, PAIR))
        c, s1, s2 = c_ref[...], s1_ref[...], s2_ref[...]
        for g in range(3):
            qn_ref, kn_ref, v_ref = outs[3 * g:3 * g + 3]
            for p in range(NPAIR):
                for which, gains, dst in ((0, qg_ref, qn_ref), (1, kg_ref, kn_ref)):
                    col = (2 + 3 * which + g) * 512 + p * PAIR
                    xr = p_ref[:, col:col + PAIR].astype(F32)
                    rinv = lax.rsqrt(_seg_mean(xr * xr, lo) + EPS)
                    dst[p] = _rope(xr * rinv * gains[g:g + 1, :], c, s1, s2)
                col = (8 + g) * 512 + p * PAIR
                v_ref[p] = p_ref[:, col:col + PAIR].astype(F32)

    pm = pl.BlockSpec((NPAIR, tm, PAIR), lambda i: (0, i, 0))
    tab = pl.BlockSpec((tm, PAIR), lambda i: (i, 0))
    gain = pl.BlockSpec((3, PAIR), lambda i: (0, 0))
    return pl.pallas_call(
        body, name="prep_fwd", grid=(T // tm,),
        in_specs=[pl.BlockSpec((tm, AB_IN), lambda i: (i, 0)), gain, gain, tab, tab, tab],
        out_specs=[pm] * 9, out_shape=[_sds((NPAIR, T, PAIR), F32)] * 9,
        compiler_params=_cparams(1))(pab, qg, kg, *tabs)


def _res_index(it, rate):
    window = NBACK * rate
    b = it // rate
    rho = it % rate
    start = b * window + rho
    startp = jnp.maximum(start - window, rho)
    kmin = jnp.where(b > 0, 0, NBACK)
    return start, startp, kmin


def _rows(start, rate):
    if rate == 1:
        return pl.ds(pl.multiple_of(start, NBACK), NBACK)
    return pl.ds(start, NBACK, stride=rate)


def _band():
    qi = lax.broadcasted_iota(jnp.int32, (NBACK, 2 * NBACK), 0)
    kj = lax.broadcasted_iota(jnp.int32, (NBACK, 2 * NBACK), 1)
    dist = qi + NBACK - kj
    return (dist >= 0) & (dist <= NBACK), kj


def _attn_fwd(qn, kn, v, rate, name):
    def body(q_ref, k_ref, v_ref, o_ref, l_ref):
        lo = _lo_mask((NBACK, PAIR))
        band, kj = _band()

        def step(it, carry):
            start, startp, kmin = _res_index(it, rate)
            q = q_ref[_rows(start, rate), :]
            kcat = jnp.concatenate([k_ref[_rows(startp, rate), :], k_ref[_rows(start, rate), :]], axis=0).astype(BF16)
            vcat = jnp.concatenate([v_ref[_rows(startp, rate), :], v_ref[_rows(start, rate), :]], axis=0).astype(BF16)
            ok = band & (kj >= kmin)
            outs, lses = [], []
            for h in range(2):
                hm = lo if h == 0 else jnp.logical_not(lo)
                qh = jnp.where(hm, q, 0.0).astype(BF16)
                s = lax.dot_general(qh, kcat, (((1,), (1,)), ((), ())), preferred_element_type=F32) * (HEAD ** -0.5)
                s = jnp.where(ok, s, NEG_INF)
                m = jnp.max(s, axis=-1, keepdims=True)
                pr = jnp.exp(s - m)
                l = jnp.sum(pr, axis=-1, keepdims=True)
                outs.append(jnp.dot(pr.astype(BF16), vcat, preferred_element_type=F32) / l)
                lses.append(m + jnp.log(l))
            o_ref[_rows(start, rate), :] = jnp.where(lo, outs[0], outs[1])
            l_ref[_rows(start, rate), :] = jnp.where(lo, lses[0], lses[1])
            return carry

        lax.fori_loop(0, T // NBACK, step, 0)

    pm = pl.BlockSpec((None, T, PAIR), lambda p: (p, 0, 0))
    return pl.pallas_call(
        body, name=name, grid=(NPAIR,), in_specs=[pm, pm, pm], out_specs=[pm, pm],
        out_shape=[_sds((NPAIR, T, PAIR), F32)] * 2, compiler_params=_cparams(1))(qn, kn, v)


def _merge_fwd(cat_ab, outs, lses, tm=512):
    def body(cat_in, o0, o1, o2, l0, l1, l2, cat_ref, lse_ref):
        del cat_in
        for p in range(NPAIR):
            a0, a1, a2 = l0[p], l1[p], l2[p]
            m = jnp.maximum(jnp.maximum(a0, a1), a2)
            w0, w1, w2 = jnp.exp(a0 - m), jnp.exp(a1 - m), jnp.exp(a2 - m)
            s = w0 + w1 + w2
            b = (w0 * o0[p] + w1 * o1[p] + w2 * o2[p]) / s
            cat_ref[:, p * PAIR:(p + 1) * PAIR] = b.astype(BF16)
            lse_ref[p] = m + jnp.log(s)

    pm = pl.BlockSpec((NPAIR, tm, PAIR), lambda i: (0, i, 0))
    return pl.pallas_call(
        body, name="merge_fwd", grid=(T // tm,),
        in_specs=[pl.BlockSpec(memory_space=pl.ANY)] + [pm] * 6,
        out_specs=[pl.BlockSpec((tm, 512), lambda i: (i, 1)), pm],
        out_shape=[_sds((T, D), BF16), _sds((NPAIR, T, PAIR), F32)],
        input_output_aliases={0: 0}, compiler_params=_cparams(1))(cat_ab, *outs, *lses)


def _b_pre_bwd(dcat, cat, tm=512):
    def body(db_ref, b_ref, dbp_ref, e_ref):
        lo = _lo_mask((tm, PAIR))
        for p in range(NPAIR):
            db = db_ref[:, p * PAIR:(p + 1) * PAIR].astype(F32)
            b = b_ref[:, p * PAIR:(p + 1) * PAIR].astype(F32)
            dbp_ref[p] = db
            e_ref[p] = _seg_mean(db * b, lo) * float(HEAD)

    pm = pl.BlockSpec((NPAIR, tm, PAIR), lambda i: (0, i, 0))
    right = pl.BlockSpec((tm, 512), lambda i: (i, 1))
    return pl.pallas_call(
        body, name="b_pre_bwd", grid=(T // tm,), in_specs=[right, right], out_specs=[pm, pm],
        out_shape=[_sds((NPAIR, T, PAIR), F32)] * 2, compiler_params=_cparams(1))(dcat, cat)


def _attn_bwd(qn, kn, v, dbp, e, lse, rate, name):
    def body(q_ref, k_ref, v_ref, db_ref, e_ref, lse_ref, dq_ref, dk_ref, dv_ref):
        lo = _lo_mask((NBACK, PAIR))
        band, kj = _band()
        dk_ref[...] = jnp.zeros_like(dk_ref)
        dv_ref[...] = jnp.zeros_like(dv_ref)
        scale = HEAD ** -0.5
        nt = (((1,), (1,)), ((), ()))
        tn = (((0,), (0,)), ((), ()))

        def step(it, carry):
            start, startp, kmin = _res_index(it, rate)
            rq, rp = _rows(start, rate), _rows(startp, rate)
            q = q_ref[rq, :]
            db = db_ref[rq, :]
            ev = e_ref[rq, :]
            ls = lse_ref[rq, :]
            kcat = jnp.concatenate([k_ref[rp, :], k_ref[rq, :]], axis=0).astype(BF16)
            vcat = jnp.concatenate([v_ref[rp, :], v_ref[rq, :]], axis=0).astype(BF16)
            ok = band & (kj >= kmin)
            dqs = []
            dkc = jnp.zeros((2 * NBACK, PAIR), F32)
            dvc = jnp.zeros((2 * NBACK, PAIR), F32)
            for h in range(2):
                hm = lo if h == 0 else jnp.logical_not(lo)
                qh = jnp.where(hm, q, 0.0).astype(BF16)
                dbh = jnp.where(hm, db, 0.0).astype(BF16)
                s = lax.dot_general(qh, kcat, nt, preferred_element_type=F32) * scale
                s = jnp.where(ok, s, NEG_INF)
                col = slice(h * HEAD, h * HEAD + 1)
                pt = jnp.exp(s - ls[:, col])
                dp = lax.dot_general(dbh, vcat, nt, preferred_element_type=F32)
                ds = (pt * (dp - ev[:, col])).astype(BF16)
                dqs.append(jnp.dot(ds, kcat, preferred_element_type=F32) * scale)
                dkc = dkc + lax.dot_general(ds, qh, tn, preferred_element_type=F32) * scale
                dvc = dvc + lax.dot_general(pt.astype(BF16), dbh, tn, preferred_element_type=F32)
            dq_ref[rq, :] = jnp.where(lo, dqs[0], dqs[1])
            dk_ref[rp, :] += dkc[0:NBACK]
            dk_ref[rq, :] += dkc[NBACK:]
            dv_ref[rp, :] += dvc[0:NBACK]
            dv_ref[rq, :] += dvc[NBACK:]
            return carry

        lax.fori_loop(0, T // NBACK, step, 0)

    pm = pl.BlockSpec((None, T, PAIR), lambda p: (p, 0, 0))
    return pl.pallas_call(
        body, name=name, grid=(NPAIR,), in_specs=[pm] * 6, out_specs=[pm] * 3,
        out_shape=[_sds((NPAIR, T, PAIR), F32)] * 3, compiler_params=_cparams(1, 56))(qn, kn, v, dbp, e, lse)


def _ab_in_bwd(pab, dcat, sgu_g, sgu_b, sgu_w, sgu_bias3, qg, kg, tabs, dqkv, tm=256):
    def body(p_ref, dcat_ref, g_ref, b_ref, w_ref, bias_ref, qg_ref, kg_ref, c_ref, s1_ref, s2_ref, *rest):
        dq_refs = rest[0:9]
        o_ref, dwm_ref, dbias_ref, dsg_ref, dsb_ref, dgain_ref = rest[9:]
        i = pl.program_id(0)

        @pl.when(i == 0)
        def _():
            dwm_ref[...] = jnp.zeros_like(dwm_ref)
            dbias_ref[...] = jnp.zeros_like(dbias_ref)
            dsg_ref[...] = jnp.zeros_like(dsg_ref)
            dsb_ref[...] = jnp.zeros_like(dsb_ref)
            dgain_ref[...] = jnp.zeros_like(dgain_ref)

        zu = p_ref[:, 0:512].astype(F32)
        zv = p_ref[:, 512:1024].astype(F32)
        u = _gelu(zu)
        v = _gelu(zv)
        mu = jnp.mean(v, axis=-1, keepdims=True)
        vc = v - mu
        rstd = lax.rsqrt(jnp.mean(vc * vc, axis=-1, keepdims=True) + EPS)
        xhat = vc * rstd
        vn = (xhat * g_ref[...] + b_ref[...]).astype(BF16)
        da = dcat_ref[...].astype(F32)
        tri = _tril_mask()
        du_parts = [[None] * 4 for _ in range(tm // 128)]
        dvn_parts = [[None] * 4 for _ in range(tm // 128)]
        for gi in range(4):
            wg = jnp.where(tri, w_ref[gi], 0.0).astype(BF16)
            bg = bias_ref[gi]
            for c in range(tm // 128):
                rs, cs = slice(c * 128, (c + 1) * 128), slice(gi * 128, (gi + 1) * 128)
                vblk = vn[rs, cs]
                mixed = jnp.dot(wg, vblk, preferred_element_type=F32) + bg
                dab = da[rs, cs]
                du_parts[c][gi] = dab * mixed
                dmixed = dab * u[rs, cs]
                dmb = dmixed.astype(BF16)
                dvn_parts[c][gi] = lax.dot_general(wg, dmb, (((0,), (0,)), ((), ())), preferred_element_type=F32)
                dwm = lax.dot_general(dmb, vblk, (((1,), (1,)), ((), ())), preferred_element_type=F32)
                dwm_ref[gi] += jnp.where(tri, dwm, 0.0)
                dbias_ref[gi] += dmixed
        du = jnp.concatenate([jnp.concatenate(r, axis=1) for r in du_parts], axis=0)
        dvn = jnp.concatenate([jnp.concatenate(r, axis=1) for r in dvn_parts], axis=0)
        dsg_ref[...] += jnp.sum(dvn * xhat, axis=0, keepdims=True)
        dsb_ref[...] += jnp.sum(dvn, axis=0, keepdims=True)
        dxh = dvn * g_ref[...]
        dv = rstd * (dxh - jnp.mean(dxh, axis=-1, keepdims=True)
                     - xhat * jnp.mean(dxh * xhat, axis=-1, keepdims=True))
        o_ref[:, 0:512] = (du * _gelu_grad(zu)).astype(BF16)
        o_ref[:, 512:1024] = (dv * _gelu_grad(zv)).astype(BF16)

        lo = _lo_mask((tm, PAIR))
        c, s1, s2 = c_ref[...], s1_ref[...], s2_ref[...]
        for g in range(3):
            dq_ref, dk_ref, dv_ref = dq_refs[3 * g:3 * g + 3]
            for p in range(NPAIR):
                for which, gains, src in ((0, qg_ref, dq_ref), (1, kg_ref, dk_ref)):
                    col = (2 + 3 * which + g) * 512 + p * PAIR
                    xr = p_ref[:, col:col + PAIR].astype(F32)
                    rinv = lax.rsqrt(_seg_mean(xr * xr, lo) + EPS)
                    xh = xr * rinv
                    dn = _rope_t(src[p], c, s1, s2)
                    row = 2 * g + which
                    dgain_ref[row:row + 1, :] += jnp.sum(dn * xh, axis=0, keepdims=True)
                    dxh2 = dn * gains[g:g + 1, :]
                    dx = rinv * (dxh2 - xh * _seg_mean(dxh2 * xh, lo))
                    o_ref[:, col:col + PAIR] = dx.astype(BF16)
                col = (8 + g) * 512 + p * PAIR
                o_ref[:, col:col + PAIR] = dv_ref[p].astype(BF16)

    pm = pl.BlockSpec((NPAIR, tm, PAIR), lambda i: (0, i, 0))
    tab = pl.BlockSpec((tm, PAIR), lambda i: (i, 0))
    gain = pl.BlockSpec((3, PAIR), lambda i: (0, 0))
    vec = pl.BlockSpec((1, 512), lambda i: (0, 0))
    full = pl.BlockSpec((tm, AB_IN), lambda i: (i, 0))
    w4 = pl.BlockSpec((4, 128, 128), lambda i: (0, 0, 0))
    return pl.pallas_call(
        body, name="ab_in_bwd", grid=(T // tm,),
        in_specs=[full, pl.BlockSpec((tm, 512), lambda i: (i, 0)), vec, vec, w4,
                  pl.BlockSpec((4, 128, 1), lambda i: (0, 0, 0)), gain, gain, tab, tab, tab] + [pm] * 9,
        out_specs=[full, w4, w4, vec, vec, pl.BlockSpec((8, PAIR), lambda i: (0, 0))],
        out_shape=[_sds((T, AB_IN), BF16), _sds((4, 128, 128), F32), _sds((4, 128, 128), F32),
                   _sds((1, 512), F32), _sds((1, 512), F32), _sds((8, PAIR), F32)],
        compiler_params=_cparams(1))(pab, dcat, sgu_g, sgu_b, sgu_w, sgu_bias3, qg, kg, *tabs, *dqkv)


def _ln_stats(x):
    mu = jnp.mean(x, axis=-1, keepdims=True)
    xc = x - mu
    rstd = lax.rsqrt(jnp.mean(xc * xc, axis=-1, keepdims=True) + EPS)
    return xc * rstd, rstd


def _cd_fwd(pcd, cw, cb, lg, lb, dw, tm=512):
    per = tm // HALO

    def body(p_ref, h_ref, cw_ref, cb_ref, lg_ref, lb_ref, dw_ref, cat_ref, c0_ref, c1_ref, dd_ref, y_ref, buf, buf2):
        i = pl.program_id(0)
        live = jnp.where(i > 0, 1.0, 0.0)
        a = p_ref[:, 0:512].astype(F32)
        gt = p_ref[:, 512:1024].astype(F32)
        gb = p_ref[:, 1024:1536].astype(F32)
        gc = p_ref[:, 1536:2048].astype(F32)
        hv = p_ref[:, 2048:2560].astype(F32)
        c0 = a * _sigmoid(gt)
        dd = gc * hv
        buf[0:HALO, :] = h_ref[:, 0:512].astype(F32) * _sigmoid(h_ref[:, 512:1024].astype(F32)) * live
        buf[HALO:, :] = c0
        buf2[0:HALO, :] = h_ref[:, 1536:2048].astype(F32) * h_ref[:, 2048:2560].astype(F32) * live
        buf2[HALO:, :] = dd
        acc = jnp.broadcast_to(cb_ref[...], (tm, 512))
        for j in range(CONV_C_TAPS):
            acc = acc + cw_ref[j:j + 1, :] * buf[pl.ds(HALO - (CONV_C_TAPS - 1) + j, tm), :]
        xhat, _ = _ln_stats(acc)
        c2 = xhat * lg_ref[...] + lb_ref[...]
        y = jnp.zeros((tm, 512), F32)
        for j in range(CONV_D_TAPS):
            y = y + dw_ref[j:j + 1, :] * buf2[pl.ds(HALO - (CONV_D_TAPS - 1) + j, tm), :]
        cat_ref[:, 0:512] = (c2 * _sigmoid(c2)).astype(BF16)
        cat_ref[:, 512:1024] = (gb * y).astype(BF16)
        c0_ref[...] = c0.astype(BF16)
        c1_ref[...] = acc
        dd_ref[...] = dd.astype(BF16)
        y_ref[...] = y.astype(BF16)

    half = pl.BlockSpec((tm, 512), lambda i: (i, 0))
    vec = pl.BlockSpec((1, 512), lambda i: (0, 0))
    return pl.pallas_call(
        body, name="cd_fwd", grid=(T // tm,),
        in_specs=[pl.BlockSpec((tm, CD_IN), lambda i: (i, 0)),
                  pl.BlockSpec((HALO, CD_IN), lambda i: (jnp.maximum(i * per - 1, 0), 0)),
                  pl.BlockSpec((32, 512), lambda i: (0, 0)), vec, vec, vec, pl.BlockSpec((8, 512), lambda i: (0, 0))],
        out_specs=[pl.BlockSpec((tm, D), lambda i: (i, 0)), half, half, half, half],
        out_shape=[_sds((T, D), BF16), _sds((T, 512), BF16), _sds((T, 512), F32), _sds((T, 512), BF16),
                   _sds((T, 512), BF16)],
        scratch_shapes=[pltpu.VMEM((HALO + tm, 512), F32), pltpu.VMEM((HALO + tm, 512), F32)],
        compiler_params=_cparams(1))(pcd, pcd, cw, cb, lg, lb, dw)


def _cd_bwd_pw(dcat, c1, pcd, y, lg, lb, tm=512):
    def body(dcat_ref, c1_ref, gb_ref, y_ref, lg_ref, lb_ref, dc1_ref, dy3_ref, dgb_ref, dlg_ref, dlb_ref, dcb_ref):
        i = pl.program_id(0)

        @pl.when(i == 0)
        def _():
            dlg_ref[...] = jnp.zeros_like(dlg_ref)
            dlb_ref[...] = jnp.zeros_like(dlb_ref)
            dcb_ref[...] = jnp.zeros_like(dcb_ref)

        dc = dcat_ref[:, 0:512].astype(F32)
        ddo = dcat_ref[:, 512:1024].astype(F32)
        xhat, rstd = _ln_stats(c1_ref[...])
        c2 = xhat * lg_ref[...] + lb_ref[...]
        sg = _sigmoid(c2)
        dc2 = dc * sg * (1.0 + c2 * (1.0 - sg))
        dlg_ref[...] += jnp.sum(dc2 * xhat, axis=0, keepdims=True)
        dlb_ref[...] += jnp.sum(dc2, axis=0, keepdims=True)
        dxh = dc2 * lg_ref[...]
        dc1 = rstd * (dxh - jnp.mean(dxh, axis=-1, keepdims=True)
                      - xhat * jnp.mean(dxh * xhat, axis=-1, keepdims=True))
        dcb_ref[...] += jnp.sum(dc1, axis=0, keepdims=True)
        dc1_ref[...] = dc1
        dgb_ref[...] = (ddo * y_ref[...].astype(F32)).astype(BF16)
        dy3_ref[...] = ddo * gb_ref[...].astype(F32)

    half = pl.BlockSpec((tm, 512), lambda i: (i, 0))
    vec = pl.BlockSpec((1, 512), lambda i: (0, 0))
    return pl.pallas_call(
        body, name="cd_bwd_pw", grid=(T // tm,),
        in_specs=[pl.BlockSpec((tm, D), lambda i: (i, 0)), half, pl.BlockSpec((tm, 512), lambda i: (i, 2)), half,
                  vec, vec],
        out_specs=[half, half, half, vec, vec, vec],
        out_shape=[_sds((T, 512), F32), _sds((T, 512), F32), _sds((T, 512), BF16),
                   _sds((1, 512), F32), _sds((1, 512), F32), _sds((1, 512), F32)],
        compiler_params=_cparams(1))(dcat, c1, pcd, y, lg, lb)


def _cd_bwd_conv(pcd, dc1, dy3, c0, dd, dgb, cw, dw, tm=512):
    per = tm // HALO
    nblk = T // tm
    last32 = T // HALO - 1

    def body(p_ref, dc1_ref, dc1n_ref, dy3_ref, dy3n_ref, c0_ref, c0p_ref, dd_ref, ddp_ref, dgb_ref, cw_ref, dw_ref,
             o_ref, dcw_ref, ddw_ref, dbuf, cbuf, d3buf, ddbuf):
        i = pl.program_id(0)
        has_prev = jnp.where(i > 0, 1.0, 0.0)
        has_next = jnp.where(i < nblk - 1, 1.0, 0.0)

        @pl.when(i == 0)
        def _():
            dcw_ref[...] = jnp.zeros_like(dcw_ref)
            ddw_ref[...] = jnp.zeros_like(ddw_ref)

        dc1 = dc1_ref[...]
        dy3 = dy3_ref[...]
        dbuf[0:tm, :] = dc1
        dbuf[tm:, :] = dc1n_ref[...] * has_next
        d3buf[0:tm, :] = dy3
        d3buf[tm:, :] = dy3n_ref[...] * has_next
        cbuf[0:HALO, :] = c0p_ref[...].astype(F32) * has_prev
        cbuf[HALO:, :] = c0_ref[...].astype(F32)
        ddbuf[0:HALO, :] = ddp_ref[...].astype(F32) * has_prev
        ddbuf[HALO:, :] = dd_ref[...].astype(F32)

        dc0 = jnp.zeros((tm, 512), F32)
        for j in range(CONV_C_TAPS):
            dc0 = dc0 + cw_ref[j:j + 1, :] * dbuf[pl.ds(CONV_C_TAPS - 1 - j, tm), :]
            dcw_ref[j:j + 1, :] += jnp.sum(dc1 * cbuf[pl.ds(HALO - (CONV_C_TAPS - 1) + j, tm), :], axis=0, keepdims=True)
        ddd = jnp.zeros((tm, 512), F32)
        for j in range(CONV_D_TAPS):
            ddd = ddd + dw_ref[j:j + 1, :] * d3buf[pl.ds(CONV_D_TAPS - 1 - j, tm), :]
            ddw_ref[j:j + 1, :] += jnp.sum(dy3 * ddbuf[pl.ds(HALO - (CONV_D_TAPS - 1) + j, tm), :], axis=0, keepdims=True)

        a = p_ref[:, 0:512].astype(F32)
        gt = p_ref[:, 512:1024].astype(F32)
        gc = p_ref[:, 1536:2048].astype(F32)
        hv = p_ref[:, 2048:2560].astype(F32)
        sg = _sigmoid(gt)
        o_ref[:, 0:512] = (dc0 * sg).astype(BF16)
        o_ref[:, 512:1024] = (dc0 * a * sg * (1.0 - sg)).astype(BF16)
        o_ref[:, 1024:1536] = dgb_ref[...]
        o_ref[:, 1536:2048] = (ddd * hv).astype(BF16)
        o_ref[:, 2048:2560] = (ddd * gc).astype(BF16)

    half = pl.BlockSpec((tm, 512), lambda i: (i, 0))
    nxt = pl.BlockSpec((HALO, 512), lambda i: (jnp.minimum((i + 1) * per, last32), 0))
    prv = pl.BlockSpec((HALO, 512), lambda i: (jnp.maximum(i * per - 1, 0), 0))
    full = pl.BlockSpec((tm, CD_IN), lambda i: (i, 0))
    return pl.pallas_call(
        body, name="cd_bwd_conv", grid=(nblk,),
        in_specs=[full, half, nxt, half, nxt, half, prv, half, prv, half,
                  pl.BlockSpec((32, 512), lambda i: (0, 0)), pl.BlockSpec((8, 512), lambda i: (0, 0))],
        out_specs=[full, pl.BlockSpec((32, 512), lambda i: (0, 0)), pl.BlockSpec((8, 512), lambda i: (0, 0))],
        out_shape=[_sds((T, CD_IN), BF16), _sds((32, 512), F32), _sds((8, 512), F32)],
        scratch_shapes=[pltpu.VMEM((tm + HALO, 512), F32), pltpu.VMEM((HALO + tm, 512), F32),
                        pltpu.VMEM((tm + HALO, 512), F32), pltpu.VMEM((HALO + tm, 512), F32)],
        compiler_params=_cparams(1))(pcd, dc1, dc1, dy3, dy3, c0, c0, dd, dd, dgb, cw, dw)


def _local_step(x, tgt, W):
    tabs = _rope_tables()
    qg = jnp.tile(W["q_norm_g"], (1, 2))
    kg = jnp.tile(W["k_norm_g"], (1, 2))
    bias3 = W["sgu_bias"].reshape(4, 128, 1)
    G = {}

    h0 = _rms_fwd(x, W["ab_norm_g"], "rms_fwd_ab")
    pab = _mm_nt(h0, W["wt_ab_in"], "mm_ab_in")
    cat_ab = _mix_a_fwd(pab, W["sgu_norm_g"], W["sgu_norm_b"], W["sgu_w"], bias3)
    qkv = _prep_fwd(pab, qg, kg, tabs)
    outs, lses = [], []
    for g, rate in enumerate(DIL_RATES):
        o, l = _attn_fwd(qkv[3 * g], qkv[3 * g + 1], qkv[3 * g + 2], rate, f"attn_fwd_{g}")
        outs.append(o)
        lses.append(l)
    cat_ab, lse = _merge_fwd(cat_ab, outs, lses)
    x1 = _mm_nn(cat_ab, W["w_ab_out"], "mm_ab_out", resid=x)

    def ffn_fwd(xin, layer):
        h = _rms_fwd(xin, W["ffn_norm_g"][layer:layer + 1], f"rms_fwd_ffn{layer}")
        pf = _mm_nt(h, W["wt_ffn_in"][layer], f"mm_ffn_in{layer}")
        act = _swiglu_fwd(pf, f"swiglu_fwd{layer}")
        xout = _mm_nn(act, W["w_ffn_down"][layer], f"mm_ffn_down{layer}", resid=xin)
        return xout, (h, pf, act)

    x2, ffn0 = ffn_fwd(x1, 0)
    h2 = _rms_fwd(x2, W["cd_norm_g"], "rms_fwd_cd")
    pcd = _mm_nt(h2, W["wt_cd_in"], "mm_cd_in")
    cat_cd, c0, c1, dd, yv = _cd_fwd(pcd, W["conv_c_w32"], W["conv_c_b"], W["c_ln_g"], W["c_ln_b"], W["conv_d_w8"])
    x3 = _mm_nn(cat_cd, W["w_cd_out"], "mm_cd_out", resid=x2)
    x4, ffn1 = ffn_fwd(x3, 1)
    dy, dyb, loss_cols = _loss_fwd_bwd(x4, tgt)

    def ffn_bwd(xin, saved, dres, dresb, layer):
        h, pf, act = saved
        G[f"w_ffn_down{layer}"] = _mm_tn(act, dresb, f"mm_g_ffn_down{layer}")
        dact = _mm_nt(dresb, W["w_ffn_down"][layer], f"mm_d_act{layer}")
        dpf = _swiglu_bwd(pf, dact, f"swiglu_bwd{layer}")
        G[f"wt_ffn_in{layer}"] = _mm_tn(dpf, h, f"mm_g_ffn_in{layer}")
        dh = _mm_nn(dpf, W["wt_ffn_in"][layer], f"mm_d_h_ffn{layer}")
        dx, dxb, dg = _rms_bwd(xin, W["ffn_norm_g"][layer:layer + 1], dh, dres, f"rms_bwd_ffn{layer}")
        G[f"ffn_norm_g{layer}"] = dg
        return dx, dxb

    dx3, dx3b = ffn_bwd(x3, ffn1, dy, dyb, 1)

    G["w_cd_out"] = _mm_tn(cat_cd, dx3b, "mm_g_cd_out")
    dcat_cd = _mm_nt(dx3b, W["w_cd_out"], "mm_d_cat_cd")
    dc1, dy3, dgb, G["c_ln_g"], G["c_ln_b"], G["conv_c_b"] = _cd_bwd_pw(dcat_cd, c1, pcd, yv, W["c_ln_g"], W["c_ln_b"])
    dpcd, G["conv_c_w32"], G["conv_d_w8"] = _cd_bwd_conv(pcd, dc1, dy3, c0, dd, dgb, W["conv_c_w32"], W["conv_d_w8"])
    G["wt_cd_in"] = _mm_tn(dpcd, h2, "mm_g_cd_in")
    dh2 = _mm_nn(dpcd, W["wt_cd_in"], "mm_d_h_cd")
    dx2, dx2b, G["cd_norm_g"] = _rms_bwd(x2, W["cd_norm_g"], dh2, dx3, "rms_bwd_cd")

    dx1, dx1b = ffn_bwd(x1, ffn0, dx2, dx2b, 0)

    G["w_ab_out"] = _mm_tn(cat_ab, dx1b, "mm_g_ab_out")
    dcat_ab = _mm_nt(dx1b, W["w_ab_out"], "mm_d_cat_ab")
    dbp, e = _b_pre_bwd(dcat_ab, cat_ab)
    dqkv = []
    for g, rate in enumerate(DIL_RATES):
        dqkv += _attn_bwd(qkv[3 * g], qkv[3 * g + 1], qkv[3 * g + 2], dbp, e, lse, rate, f"attn_bwd_{g}")
    dpab, G["sgu_w"], dbias_part, G["sgu_norm_g"], G["sgu_norm_b"], dgain = _ab_in_bwd(
        pab, dcat_ab, W["sgu_norm_g"], W["sgu_norm_b"], W["sgu_w"], bias3, qg, kg, tabs, dqkv)
    G["sgu_bias"] = jnp.sum(dbias_part, axis=-1)
    dgain = dgain[0:6, 0:HEAD] + dgain[0:6, HEAD:PAIR]
    G["q_norm_g"] = dgain[0::2]
    G["k_norm_g"] = dgain[1::2]
    G["wt_ab_in"] = _mm_tn(dpab, h0, "mm_g_ab_in")
    dh0 = _mm_nn(dpab, W["wt_ab_in"], "mm_d_h_ab")
    grad_x, _, G["ab_norm_g"] = _rms_bwd(x, W["ab_norm_g"], dh0, dx1, "rms_bwd_ab")
    return loss_cols, grad_x, G


def _my_place():
    return lax.axis_index("x"), lax.axis_index("y"), lax.axis_index("c")


def _dev_index(px, py, pc):
    return 4 * px + 2 * py + pc


def _flip(place, k):
    x, y, c = place
    return (1 - x if k & 4 else x, 1 - y if k & 2 else y, 1 - c if k & 1 else c)


HBM_SPEC = pl.BlockSpec(memory_space=pl.ANY)


def _gather_weights(ab_in_t, ab_out, cd_in_t, cd_out, gate_t, up_t, down, small):
    r_small = small.shape[0]
    out_shape = [
        _sds((NDEV,) + ab_in_t.shape, BF16), _sds((NDEV,) + ab_out.shape, BF16),
        _sds((NDEV,) + cd_in_t.shape, BF16), _sds((NDEV,) + cd_out.shape, BF16),
        _sds((2, NDEV) + gate_t.shape[1:], BF16), _sds((2, NDEV) + gate_t.shape[1:], BF16),
        _sds((NDEV,) + down.shape[1:], BF16), _sds((NDEV,) + down.shape[1:], BF16),
        _sds((NDEV, r_small, 128), F32),
    ]
    n_items = 11

    def body(ab_in_r, ab_out_r, cd_in_r, cd_out_r, gate_r, up_r, down_r, small_r,
             o_ab_in, o_ab_out, o_cd_in, o_cd_out, o_ffn_in0, o_ffn_in1, o_down0, o_down1, o_small,
             send_sems, recv_sems, loc_sems):
        x, y, c = _my_place()
        me = (x, y, c)
        sib = (x, y, 1 - c)
        chips = [(1 - x, y), (x, 1 - y), (1 - x, 1 - y)]
        items = [
            (ab_in_r, lambda d: o_ab_in.at[d]),
            (small_r, lambda d: o_small.at[d]),
            (ab_out_r, lambda d: o_ab_out.at[d]),
            (gate_r.at[0], lambda d: o_ffn_in0.at[0, d]),
            (up_r.at[0], lambda d: o_ffn_in0.at[1, d]),
            (down_r.at[0], lambda d: o_down0.at[d]),
            (cd_in_r, lambda d: o_cd_in.at[d]),
            (cd_out_r, lambda d: o_cd_out.at[d]),
            (gate_r.at[1], lambda d: o_ffn_in1.at[0, d]),
            (up_r.at[1], lambda d: o_ffn_in1.at[1, d]),
            (down_r.at[1], lambda d: o_down1.at[d]),
        ]

        def rcopy(it, k, src, dst, to):
            return pltpu.make_async_remote_copy(src_ref=src, dst_ref=dst, send_sem=send_sems.at[it, k],
                                                recv_sem=recv_sems.at[it, k], device_id=to, device_id_type=MESH)

        started = []
        locals_ = []
        for it, (src, dst) in enumerate(items):
            mine = dst(_dev_index(*me))
            lc = pltpu.make_async_copy(src, mine, loc_sems.at[it])
            lc.start()
            locals_.append(lc)
            first = [rcopy(it, 0, src, mine, sib)]
            first += [rcopy(it, 1 + j, src, mine, (*chip, c)) for j, chip in enumerate(chips)]
            for cp in first:
                cp.start()
            started += first
        for it, (src, dst) in enumerate(items):
            for j, chip in enumerate(chips):
                blk = dst(_dev_index(*chip, c))
                rcopy(it, 1 + j, blk, blk, me).wait_recv()
                fwd = rcopy(it, 4 + j, blk, blk, sib)
                fwd.start()
                started.append(fwd)
        for it, (src, dst) in enumerate(items):
            blk = dst(_dev_index(x, y, 1 - c))
            rcopy(it, 0, blk, blk, me).wait_recv()
            for j, chip in enumerate(chips):
                blk = dst(_dev_index(*chip, 1 - c))
                rcopy(it, 4 + j, blk, blk, me).wait_recv()
        for cp in started:
            cp.wait_send()
        for lc in locals_:
            lc.wait()

    return pl.pallas_call(
        body, name="gather_weights", in_specs=[HBM_SPEC] * 8, out_specs=[HBM_SPEC] * 9, out_shape=out_shape,
        scratch_shapes=[pltpu.SemaphoreType.DMA((n_items, 7)), pltpu.SemaphoreType.DMA((n_items, 7)),
                        pltpu.SemaphoreType.DMA((n_items,))],
    )(ab_in_t, ab_out, cd_in_t, cd_out, gate_t, up_t, down, small)


def _scatter_grads(g_ab_in, g_ab_out, g_cd_in, g_cd_out, g_ffn_in0, g_ffn_in1, g_down0, g_down1):
    out_shape = [
        _sds(g_ab_in.shape, BF16), _sds(g_ab_out.shape, BF16), _sds(g_cd_in.shape, BF16), _sds(g_cd_out.shape, BF16),
        _sds((2,) + g_ffn_in0.shape, BF16), _sds((2,) + g_down0.shape, BF16),
    ]
    n_items = 10

    def body(ab_in_r, ab_out_r, cd_in_r, cd_out_r, ffn0_r, ffn1_r, down0_r, down1_r,
             l_ab_in, l_ab_out, l_cd_in, l_cd_out, l_ffn_in, l_down, send_sems, recv_sems, loc_sems):
        me = _my_place()
        mi = _dev_index(*me)
        items = [
            (lambda j: down1_r.at[j], lambda s: l_down.at[1, s]),
            (lambda j: ffn1_r.at[0, j], lambda s: l_ffn_in.at[1, 0, s]),
            (lambda j: ffn1_r.at[1, j], lambda s: l_ffn_in.at[1, 1, s]),
            (lambda j: cd_out_r.at[j], lambda s: l_cd_out.at[s]),
            (lambda j: cd_in_r.at[j], lambda s: l_cd_in.at[s]),
            (lambda j: down0_r.at[j], lambda s: l_down.at[0, s]),
            (lambda j: ffn0_r.at[0, j], lambda s: l_ffn_in.at[0, 0, s]),
            (lambda j: ffn0_r.at[1, j], lambda s: l_ffn_in.at[0, 1, s]),
            (lambda j: ab_out_r.at[j], lambda s: l_ab_out.at[s]),
            (lambda j: ab_in_r.at[j], lambda s: l_ab_in.at[s]),
        ]
        copies = []
        for it, (src, dst) in enumerate(items):
            lc = pltpu.make_async_copy(src(mi), dst(mi), loc_sems.at[it])
            lc.start()
            copies.append(lc)
            for k in range(1, NDEV):
                peer = _flip(me, k)
                cp = pltpu.make_async_remote_copy(
                    src_ref=src(_dev_index(*peer)), dst_ref=dst(mi), send_sem=send_sems.at[it, k - 1],
                    recv_sem=recv_sems.at[it, k - 1], device_id=peer, device_id_type=MESH)
                cp.start()
                copies.append(cp)
        for cp in copies:
            cp.wait()

    return pl.pallas_call(
        body, name="scatter_grads", in_specs=[HBM_SPEC] * 8, out_specs=[HBM_SPEC] * 6, out_shape=out_shape,
        scratch_shapes=[pltpu.SemaphoreType.DMA((n_items, 7)), pltpu.SemaphoreType.DMA((n_items, 7)),
                        pltpu.SemaphoreType.DMA((n_items,))],
    )(g_ab_in, g_ab_out, g_cd_in, g_cd_out, g_ffn_in0, g_ffn_in1, g_down0, g_down1)


def _allreduce_small(packed):
    rows = packed.shape[0]

    def body(x_ref, o_ref, land, send_sems, recv_sems):
        me = _my_place()
        mi = _dev_index(*me)
        copies = []
        for k in range(1, NDEV):
            cp = pltpu.make_async_remote_copy(
                src_ref=x_ref, dst_ref=land.at[mi], send_sem=send_sems.at[k - 1], recv_sem=recv_sems.at[k - 1],
                device_id=_flip(me, k), device_id_type=MESH)
            cp.start()
            copies.append(cp)
        land[mi] = x_ref[...]
        for cp in copies:
            cp.wait()
        acc = land[0]
        for d in range(1, NDEV):
            acc = acc + land[d]
        o_ref[...] = acc

    return pl.pallas_call(
        body, name="allreduce_small", out_shape=_sds((rows, 128), F32),
        in_specs=[pl.BlockSpec(memory_space=pltpu.VMEM)], out_specs=pl.BlockSpec(memory_space=pltpu.VMEM),
        scratch_shapes=[pltpu.VMEM((NDEV, rows, 128), F32), pltpu.SemaphoreType.DMA((7,)),
                        pltpu.SemaphoreType.DMA((7,))],
    )(packed)


def _adam_math(w, g, m, v):
    m2 = ADAM_B1 * m + (1.0 - ADAM_B1) * g
    v2 = ADAM_B2 * v + (1.0 - ADAM_B2) * (g * g)
    delta = -ADAM_LR * ((m2 * ADAM_C1) / (jnp.sqrt(v2 * ADAM_C2) + ADAM_EPS) + ADAM_WD * w)
    return delta, m2, v2


def _adam_t(land, sel, w, m, v, name, tc=256):
    L, _, _, R, _ = land.shape

    def body(l_ref, w_ref, m_ref, v_ref, g_out, d_out, m_out, v_out):
        s = l_ref[0].astype(F32)
        for d in range(1, NDEV):
            s = s + l_ref[d].astype(F32)
        g = s.T
        delta, m2, v2 = _adam_math(w_ref[...], g, m_ref[...], v_ref[...])
        g_out[...] = g
        d_out[...] = delta
        m_out[...] = m2
        v_out[...] = v2

    wspec = pl.BlockSpec((None, tc, R), lambda l, i: (l, i, 0))
    return pl.pallas_call(
        body, name=name, grid=(L, D // tc),
        in_specs=[pl.BlockSpec((None, None, NDEV, R, tc), lambda l, i: (l, sel, 0, 0, i)), wspec, wspec, wspec],
        out_specs=[wspec] * 4, out_shape=[_sds((L, D, R), F32)] * 4, compiler_params=_cparams(2))(land, w, m, v)


def _adam_n(land, w, m, v, name, tc=512):
    L, _, R, _ = land.shape

    def body(l_ref, w_ref, m_ref, v_ref, g_out, d_out, m_out, v_out):
        g = l_ref[0].astype(F32)
        for d in range(1, NDEV):
            g = g + l_ref[d].astype(F32)
        delta, m2, v2 = _adam_math(w_ref[...], g, m_ref[...], v_ref[...])
        g_out[...] = g
        d_out[...] = delta
        m_out[...] = m2
        v_out[...] = v2

    wspec = pl.BlockSpec((None, R, tc), lambda l, i: (l, 0, i))
    return pl.pallas_call(
        body, name=name, grid=(L, D // tc),
        in_specs=[pl.BlockSpec((None, NDEV, R, tc), lambda l, i: (l, 0, 0, i)), wspec, wspec, wspec],
        out_specs=[wspec] * 4, out_shape=[_sds((L, R, D), F32)] * 4, compiler_params=_cparams(2))(land, w, m, v)


def _adam_small(ws, gs, ms, vs):
    n = len(ws)

    def body(*refs):
        w_r, g_r, m_r, v_r = refs[0:n], refs[n:2 * n], refs[2 * n:3 * n], refs[3 * n:4 * n]
        d_o, m_o, v_o = refs[4 * n:5 * n], refs[5 * n:6 * n], refs[6 * n:7 * n]
        for i in range(n):
            delta, m2, v2 = _adam_math(w_r[i][...], g_r[i][...], m_r[i][...], v_r[i][...])
            d_o[i][...] = delta
            m_o[i][...] = m2
            v_o[i][...] = v2

    vm = pl.BlockSpec(memory_space=pltpu.VMEM)
    shapes = [_sds(w.shape, F32) for w in ws]
    outs = pl.pallas_call(body, name="adam_small", in_specs=[vm] * (4 * n), out_specs=[vm] * (3 * n),
                          out_shape=shapes * 3)(*ws, *gs, *ms, *vs)
    return outs[0:n], outs[n:2 * n], outs[2 * n:3 * n]


WEIGHT_NAMES = ("ab_norm_g", "ab_w_in", "sgu_norm_g", "sgu_norm_b", "sgu_w", "sgu_bias", "q_norm_g", "k_norm_g",
                "ab_w_out", "cd_norm_g", "cd_w_in", "conv_c_w", "conv_c_b", "c_ln_g", "c_ln_b", "conv_d_w",
                "cd_w_out", "ffn_norm_g", "ffn_w_gate", "ffn_w_up", "ffn_w_down")
SMALL_2D = (("ab_norm_g", (1, 1024)), ("sgu_norm_g", (1, 512)), ("sgu_norm_b", (1, 512)), ("sgu_w", (512, 128)),
            ("sgu_bias", (4, 128)), ("q_norm_g", (3, 64)), ("k_norm_g", (3, 64)), ("cd_norm_g", (1, 128)),
            ("conv_c_w", (31, 64)), ("conv_c_b", (1, 64)), ("c_ln_g", (1, 64)), ("c_ln_b", (1, 64)),
            ("conv_d_w", (3, 64)), ("ffn_norm_g", (2, 1024)))
SHARD_C = 64


def _pack_rows(parts, rows):
    flat = jnp.concatenate([p.reshape(-1) for p in parts])
    return jnp.pad(flat, (0, rows * 128 - flat.shape[0])).reshape(rows, 128)


def kernel(x, ab_norm_g, ab_w_in, sgu_norm_g, sgu_norm_b, sgu_w, sgu_bias, q_norm_g, k_norm_g, ab_w_out, cd_norm_g, cd_w_in, conv_c_w, conv_c_b, c_ln_g, c_ln_b, conv_d_w, cd_w_out, ffn_norm_g, ffn_w_gate, ffn_w_up, ffn_w_down, loss_target, m_ab_norm_g, m_ab_w_in, m_sgu_norm_g, m_sgu_norm_b, m_sgu_w, m_sgu_bias, m_q_norm_g, m_k_norm_g, m_ab_w_out, m_cd_norm_g, m_cd_w_in, m_conv_c_w, m_conv_c_b, m_c_ln_g, m_c_ln_b, m_conv_d_w, m_cd_w_out, m_ffn_norm_g, m_ffn_w_gate, m_ffn_w_up, m_ffn_w_down, v_ab_norm_g, v_ab_w_in, v_sgu_norm_g, v_sgu_norm_b, v_sgu_w, v_sgu_bias, v_q_norm_g, v_k_norm_g, v_ab_w_out, v_cd_norm_g, v_cd_w_in, v_conv_c_w, v_conv_c_b, v_c_ln_g, v_c_ln_b, v_conv_d_w, v_cd_w_out, v_ffn_norm_g, v_ffn_w_gate, v_ffn_w_up, v_ffn_w_down):
    w = dict(zip(WEIGHT_NAMES, (ab_norm_g, ab_w_in, sgu_norm_g, sgu_norm_b, sgu_w, sgu_bias, q_norm_g, k_norm_g, ab_w_out, cd_norm_g, cd_w_in, conv_c_w, conv_c_b, c_ln_g, c_ln_b, conv_d_w, cd_w_out, ffn_norm_g, ffn_w_gate, ffn_w_up, ffn_w_down)))
    m = dict(zip(WEIGHT_NAMES, (m_ab_norm_g, m_ab_w_in, m_sgu_norm_g, m_sgu_norm_b, m_sgu_w, m_sgu_bias, m_q_norm_g, m_k_norm_g, m_ab_w_out, m_cd_norm_g, m_cd_w_in, m_conv_c_w, m_conv_c_b, m_c_ln_g, m_c_ln_b, m_conv_d_w, m_cd_w_out, m_ffn_norm_g, m_ffn_w_gate, m_ffn_w_up, m_ffn_w_down)))
    v = dict(zip(WEIGHT_NAMES, (v_ab_norm_g, v_ab_w_in, v_sgu_norm_g, v_sgu_norm_b, v_sgu_w, v_sgu_bias, v_q_norm_g, v_k_norm_g, v_ab_w_out, v_cd_norm_g, v_cd_w_in, v_conv_c_w, v_conv_c_b, v_c_ln_g, v_c_ln_b, v_conv_d_w, v_cd_w_out, v_ffn_norm_g, v_ffn_w_gate, v_ffn_w_up, v_ffn_w_down)))
    me = _dev_index(*_my_place())

    small_local = _pack_rows([w["cd_norm_g"], w["conv_c_w"], w["conv_c_b"], w["c_ln_g"], w["c_ln_b"], w["conv_d_w"]], 24)
    (o_ab_in, o_ab_out, o_cd_in, o_cd_out, o_ffn_in0, o_ffn_in1, o_down0, o_down1, o_small) = _gather_weights(
        w["ab_w_in"][0].T.astype(BF16), w["ab_w_out"][0].astype(BF16),
        w["cd_w_in"][0].T.astype(BF16), w["cd_w_out"][0].astype(BF16),
        jnp.swapaxes(w["ffn_w_gate"], 1, 2).astype(BF16), jnp.swapaxes(w["ffn_w_up"], 1, 2).astype(BF16),
        w["ffn_w_down"].astype(BF16), small_local)
    flat = o_small.reshape(NDEV, 24 * 128)

    def chan(lo, taps):
        return flat[:, lo:lo + taps * SHARD_C].reshape(NDEV, taps, SHARD_C).transpose(1, 0, 2).reshape(taps, 512)

    W = {
        "wt_ab_in": o_ab_in.reshape(AB_IN, D), "w_ab_out": o_ab_out.reshape(D, D),
        "wt_cd_in": o_cd_in.reshape(CD_IN, D), "w_cd_out": o_cd_out.reshape(D, D),
        "wt_ffn_in": [o_ffn_in0.reshape(2 * DFF, D), o_ffn_in1.reshape(2 * DFF, D)],
        "w_ffn_down": [o_down0.reshape(DFF, D), o_down1.reshape(DFF, D)],
        "ab_norm_g": w["ab_norm_g"], "sgu_norm_g": w["sgu_norm_g"], "sgu_norm_b": w["sgu_norm_b"],
        "sgu_w": w["sgu_w"][0], "sgu_bias": w["sgu_bias"][0], "q_norm_g": w["q_norm_g"][0],
        "k_norm_g": w["k_norm_g"][0], "ffn_norm_g": w["ffn_norm_g"],
        "cd_norm_g": flat[:, 0:128].reshape(1, D),
        "conv_c_w32": jnp.pad(chan(128, CONV_C_TAPS), ((0, 1), (0, 0))),
        "conv_c_b": chan(2112, 1), "c_ln_g": chan(2176, 1), "c_ln_b": chan(2240, 1),
        "conv_d_w8": jnp.pad(chan(2304, CONV_D_TAPS), ((0, 8 - CONV_D_TAPS), (0, 0))),
    }

    loss_cols, grad_x, G = _local_step(x[0], loss_target[0], W)
    loss = lax.psum(jnp.sum(loss_cols), ("x", "y", "c"))

    l_ab_in, l_ab_out, l_cd_in, l_cd_out, l_ffn_in, l_down = _scatter_grads(
        G["wt_ab_in"].reshape(NDEV, AB_IN // NDEV, D), G["w_ab_out"].reshape(NDEV, D // NDEV, D),
        G["wt_cd_in"].reshape(NDEV, CD_IN // NDEV, D), G["w_cd_out"].reshape(NDEV, D // NDEV, D),
        G["wt_ffn_in0"].reshape(2, NDEV, DFF // NDEV, D), G["wt_ffn_in1"].reshape(2, NDEV, DFF // NDEV, D),
        G["w_ffn_down0"].reshape(NDEV, DFF // NDEV, D), G["w_ffn_down1"].reshape(NDEV, DFF // NDEV, D))
    small_parts = [G["ab_norm_g"], G["sgu_norm_g"], G["sgu_norm_b"], G["sgu_w"], G["sgu_bias"], G["q_norm_g"],
                   G["k_norm_g"], G["cd_norm_g"], G["conv_c_w32"][:CONV_C_TAPS], G["conv_c_b"], G["c_ln_g"],
                   G["c_ln_b"], G["conv_d_w8"][:CONV_D_TAPS], G["ffn_norm_g0"], G["ffn_norm_g1"]]
    sizes = [p.size for p in small_parts]
    red = _allreduce_small(_pack_rows(small_parts, 712)).reshape(-1)
    offs = [0]
    for s in sizes:
        offs.append(offs[-1] + s)
    seg = [red[offs[i]:offs[i + 1]] for i in range(len(sizes))]

    def own_channels(full, taps):
        return lax.dynamic_slice_in_dim(full.reshape(taps, 512), me * SHARD_C, SHARD_C, axis=1)

    g_small = {
        "ab_norm_g": seg[0].reshape(1, 1024), "sgu_norm_g": seg[1].reshape(1, 512), "sgu_norm_b": seg[2].reshape(1, 512),
        "sgu_w": seg[3].reshape(512, 128), "sgu_bias": seg[4].reshape(4, 128), "q_norm_g": seg[5].reshape(3, 64),
        "k_norm_g": seg[6].reshape(3, 64),
        "cd_norm_g": lax.dynamic_slice_in_dim(seg[7].reshape(1, D), me * (D // NDEV), D // NDEV, axis=1),
        "conv_c_w": own_channels(seg[8], CONV_C_TAPS), "conv_c_b": own_channels(seg[9], 1),
        "c_ln_g": own_channels(seg[10], 1), "c_ln_b": own_channels(seg[11], 1),
        "conv_d_w": own_channels(seg[12], CONV_D_TAPS),
        "ffn_norm_g": jnp.concatenate([seg[13].reshape(1, D), seg[14].reshape(1, D)], axis=0),
    }

    grads, deltas, new_m, new_v = {}, {}, {}, {}
    names2d = [n for n, _ in SMALL_2D]
    d_s, m_s, v_s = _adam_small([w[n].reshape(s) for n, s in SMALL_2D], [g_small[n] for n in names2d],
                                [m[n].reshape(s) for n, s in SMALL_2D], [v[n].reshape(s) for n, s in SMALL_2D])
    for i, n in enumerate(names2d):
        shape = w[n].shape
        grads[n], deltas[n] = g_small[n].reshape(shape), d_s[i].reshape(shape)
        new_m[n], new_v[n] = m_s[i].reshape(shape), v_s[i].reshape(shape)

    def put(name, res):
        grads[name], deltas[name], new_m[name], new_v[name] = res

    put("ab_w_in", _adam_t(l_ab_in.reshape((1, 1) + l_ab_in.shape), 0, w["ab_w_in"], m["ab_w_in"], v["ab_w_in"], "adam_ab_in"))
    put("cd_w_in", _adam_t(l_cd_in.reshape((1, 1) + l_cd_in.shape), 0, w["cd_w_in"], m["cd_w_in"], v["cd_w_in"], "adam_cd_in"))
    put("ffn_w_gate", _adam_t(l_ffn_in, 0, w["ffn_w_gate"], m["ffn_w_gate"], v["ffn_w_gate"], "adam_ffn_gate"))
    put("ffn_w_up", _adam_t(l_ffn_in, 1, w["ffn_w_up"], m["ffn_w_up"], v["ffn_w_up"], "adam_ffn_up"))
    put("ab_w_out", _adam_n(l_ab_out.reshape((1,) + l_ab_out.shape), w["ab_w_out"], m["ab_w_out"], v["ab_w_out"], "adam_ab_out"))
    put("cd_w_out", _adam_n(l_cd_out.reshape((1,) + l_cd_out.shape), w["cd_w_out"], m["cd_w_out"], v["cd_w_out"], "adam_cd_out"))
    put("ffn_w_down", _adam_n(l_down, w["ffn_w_down"], m["ffn_w_down"], v["ffn_w_down"], "adam_ffn_down"))

    return (loss, grad_x[None], *[grads[n] for n in WEIGHT_NAMES], *[deltas[n] for n in WEIGHT_NAMES],
            *[new_m[n] for n in WEIGHT_NAMES], *[new_v[n] for n in WEIGHT_NAMES])
```

```python
import functools

import jax
import jax.numpy as jnp
from jax import lax
from jax.experimental import pallas as pl
from jax.experimental.pallas import tpu as pltpu

F32 = jnp.float32
BF16 = jnp.bfloat16

T = 4096
D = 1024
NDEV = 8
EPS = 1e-6
NEG_INF = -1e30
DFF = 2816
AB_IN = 5632
CD_IN = 2560
HEAD = 64
PAIR = 128
NPAIR = 4
NBACK = 128
DIL_RATES = (1, 4, 16)
ROPE_HALF = 8
ROPE_THETA = 500000.0
CONV_C_TAPS = 31
CONV_D_TAPS = 3
HALO = 32

ADAM_LR = 0.001
ADAM_B1 = 0.9
ADAM_B2 = 0.999
ADAM_EPS = 1e-08
ADAM_WD = 0.01
ADAM_STEP = 10
ADAM_C1 = 1.0 / (1.0 - ADAM_B1 ** ADAM_STEP)
ADAM_C2 = 1.0 / (1.0 - ADAM_B2 ** ADAM_STEP)

VMEM_LIMIT_MB = 48
MESH = pl.DeviceIdType.MESH
HBM_SPEC = pl.BlockSpec(memory_space=pl.ANY)


def _cparams(ngrid, vmem_mb=VMEM_LIMIT_MB):
    return pltpu.CompilerParams(dimension_semantics=("arbitrary",) * ngrid,
                                vmem_limit_bytes=vmem_mb * 1024 * 1024)


def _pick(n, options):
    for o in options:
        if n % o == 0:
            return o
    raise ValueError(f"no tile for {n} in {options}")


def _sds(shape, dtype):
    return jax.ShapeDtypeStruct(shape, dtype)


def _sigmoid(x):
    return 1.0 / (1.0 + jnp.exp(-x))


def _gelu(z):
    return 0.5 * z * (1.0 + lax.erf(z * 0.7071067811865476))


def _gelu_grad(z):
    return 0.5 * (1.0 + lax.erf(z * 0.7071067811865476)) + z * jnp.exp(-0.5 * z * z) * 0.3989422804014327


def _mm_nt(a, wt, name, out_dtype=BF16, tm=1024, dep=None):
    M, K = a.shape
    N = wt.shape[0]
    tn = _pick(N, (512, 256))

    def body(a_ref, w_ref, *rest):
        o_ref = rest[-1]
        o_ref[...] = lax.dot_general(a_ref[...], w_ref[...], (((1,), (1,)), ((), ())),
                                     preferred_element_type=F32).astype(o_ref.dtype)

    in_specs = [pl.BlockSpec((tm, K), lambda i, j: (i, 0)), pl.BlockSpec((tn, K), lambda i, j: (j, 0))]
    args = [a, wt]
    if dep is not None:
        in_specs.append(HBM_SPEC)
        args.append(dep)
    return pl.pallas_call(
        body, name=name, grid=(M // tm, N // tn), in_specs=in_specs,
        out_specs=pl.BlockSpec((tm, tn), lambda i, j: (i, j)),
        out_shape=_sds((M, N), out_dtype), compiler_params=_cparams(2))(*args)


def _mm_nn(a, w, name, resid=None, out_dtype=F32, tm=1024, dep=None):
    M, K = a.shape
    N = w.shape[1]
    tk = _pick(K, (1408, 1280, 1024, 512))
    nk = K // tk
    has_resid = resid is not None

    def body(*refs):
        a_ref, w_ref = refs[0], refs[1]
        r_ref = refs[2] if has_resid else None
        o_ref, acc = refs[-2], refs[-1]
        k = pl.program_id(1)

        @pl.when(k == 0)
        def _():
            acc[...] = jnp.zeros_like(acc)

        acc[...] += jnp.dot(a_ref[...], w_ref[...], preferred_element_type=F32)

        @pl.when(k == nk - 1)
        def _():
            v = acc[...]
            if has_resid:
                v = v + r_ref[...]
            o_ref[...] = v.astype(o_ref.dtype)

    in_specs = [pl.BlockSpec((tm, tk), lambda i, k: (i, k)), pl.BlockSpec((tk, N), lambda i, k: (k, 0))]
    args = [a, w]
    if has_resid:
        in_specs.append(pl.BlockSpec((tm, N), lambda i, k: (i, 0)))
        args.append(resid)
    if dep is not None:
        in_specs.append(HBM_SPEC)
        args.append(dep)
    return pl.pallas_call(
        body, name=name, grid=(M // tm, nk), in_specs=in_specs,
        out_specs=pl.BlockSpec((tm, N), lambda i, k: (i, 0)),
        out_shape=_sds((M, N), out_dtype), scratch_shapes=[pltpu.VMEM((tm, N), F32)],
        compiler_params=_cparams(2))(*args)


def _mm_tn(a, b, name, out_dtype=BF16, tt=512):
    Tt, M = a.shape
    N = b.shape[1]
    tn = _pick(M, (1408, 1280, 1024, 512))
    nt = Tt // tt

    def body(a_ref, b_ref, o_ref, acc):
        t = pl.program_id(1)

        @pl.when(t == 0)
        def _():
            acc[...] = jnp.zeros_like(acc)

        acc[...] += lax.dot_general(a_ref[...], b_ref[...], (((0,), (0,)), ((), ())),
                                    preferred_element_type=F32)

        @pl.when(t == nt - 1)
        def _():
            o_ref[...] = acc[...].astype(o_ref.dtype)

    return pl.pallas_call(
        body, name=name, grid=(M // tn, nt),
        in_specs=[pl.BlockSpec((tt, tn), lambda j, t: (t, j)), pl.BlockSpec((tt, N), lambda j, t: (t, 0))],
        out_specs=pl.BlockSpec((tn, N), lambda j, t: (j, 0)),
        out_shape=_sds((M, N), out_dtype), scratch_shapes=[pltpu.VMEM((tn, N), F32)],
        compiler_params=_cparams(2))(a, b)


def _rms_fwd(x, g, name, tm=512):
    def body(x_ref, g_ref, h_ref):
        xf = x_ref[...]
        r = lax.rsqrt(jnp.mean(xf * xf, axis=-1, keepdims=True) + EPS)
        h_ref[...] = (xf * r * g_ref[...]).astype(BF16)

    return pl.pallas_call(
        body, name=name, grid=(T // tm,),
        in_specs=[pl.BlockSpec((tm, D), lambda i: (i, 0)), pl.BlockSpec((1, D), lambda i: (0, 0))],
        out_specs=pl.BlockSpec((tm, D), lambda i: (i, 0)),
        out_shape=_sds((T, D), BF16), compiler_params=_cparams(1))(x, g)


def _rms_bwd(x, g, dh, dres, name, tm=512):
    def body(x_ref, g_ref, dh_ref, dres_ref, dx_ref, dxb_ref, dg_ref):
        i = pl.program_id(0)
        xf = x_ref[...]
        r = lax.rsqrt(jnp.mean(xf * xf, axis=-1, keepdims=True) + EPS)
        xhat = xf * r
        dhf = dh_ref[...].astype(F32)

        @pl.when(i == 0)
        def _():
            dg_ref[...] = jnp.zeros_like(dg_ref)

        dg_ref[...] += jnp.sum(dhf * xhat, axis=0, keepdims=True)
        dxh = dhf * g_ref[...]
        dx = r * (dxh - xhat * jnp.mean(dxh * xhat, axis=-1, keepdims=True))
        tot = dres_ref[...] + dx
        dx_ref[...] = tot
        dxb_ref[...] = tot.astype(BF16)

    row = pl.BlockSpec((tm, D), lambda i: (i, 0))
    vec = pl.BlockSpec((1, D), lambda i: (0, 0))
    return pl.pallas_call(
        body, name=name, grid=(T // tm,), in_specs=[row, vec, row, row],
        out_specs=[row, row, vec],
        out_shape=[_sds((T, D), F32), _sds((T, D), BF16), _sds((1, D), F32)],
        compiler_params=_cparams(1))(x, g, dh, dres)


def _loss_fwd_bwd(y, tgt, tm=512):
    def body(y_ref, t_ref, dy_ref, dyb_ref, l_ref):
        i = pl.program_id(0)
        d = y_ref[...] - t_ref[...]

        @pl.when(i == 0)
        def _():
            l_ref[...] = jnp.zeros_like(l_ref)

        l_ref[...] += jnp.sum(d * d, axis=0, keepdims=True) * (0.5 / D)
        dy = d * (1.0 / D)
        dy_ref[...] = dy
        dyb_ref[...] = dy.astype(BF16)

    row = pl.BlockSpec((tm, D), lambda i: (i, 0))
    vec = pl.BlockSpec((1, D), lambda i: (0, 0))
    return pl.pallas_call(
        body, name="loss", grid=(T // tm,), in_specs=[row, row], out_specs=[row, row, vec],
        out_shape=[_sds((T, D), F32), _sds((T, D), BF16), _sds((1, D), F32)],
        compiler_params=_cparams(1))(y, tgt)


def _swiglu_fwd(p, name, tm=512):
    def body(g_ref, u_ref, o_ref):
        g = g_ref[...].astype(F32)
        o_ref[...] = (g * _sigmoid(g) * u_ref[...].astype(F32)).astype(BF16)

    return pl.pallas_call(
        body, name=name, grid=(T // tm,),
        in_specs=[pl.BlockSpec((tm, DFF), lambda i: (i, 0)), pl.BlockSpec((tm, DFF), lambda i: (i, 1))],
        out_specs=pl.BlockSpec((tm, DFF), lambda i: (i, 0)),
        out_shape=_sds((T, DFF), BF16), compiler_params=_cparams(1))(p, p)


def _swiglu_bwd(p, dact, name, tm=512):
    def body(g_ref, u_ref, da_ref, o_ref):
        g = g_ref[...].astype(F32)
        u = u_ref[...].astype(F32)
        da = da_ref[...].astype(F32)
        sg = _sigmoid(g)
        o_ref[:, 0:DFF] = (da * u * sg * (1.0 + g * (1.0 - sg))).astype(BF16)
        o_ref[:, DFF:2 * DFF] = (da * g * sg).astype(BF16)

    return pl.pallas_call(
        body, name=name, grid=(T // tm,),
        in_specs=[pl.BlockSpec((tm, DFF), lambda i: (i, 0)), pl.BlockSpec((tm, DFF), lambda i: (i, 1)),
                  pl.BlockSpec((tm, DFF), lambda i: (i, 0))],
        out_specs=pl.BlockSpec((tm, 2 * DFF), lambda i: (i, 0)),
        out_shape=_sds((T, 2 * DFF), BF16), compiler_params=_cparams(1))(p, p, dact)


def _tril_mask():
    r = lax.broadcasted_iota(jnp.int32, (128, 128), 0)
    c = lax.broadcasted_iota(jnp.int32, (128, 128), 1)
    return r >= c


def _mix_a_fwd(pab, sgu_g, sgu_b, sgu_w, sgu_bias3, tm=512):
    def body(zu_ref, zv_ref, g_ref, b_ref, w_ref, bias_ref, o_ref):
        u = _gelu(zu_ref[...].astype(F32))
        v = _gelu(zv_ref[...].astype(F32))
        mu = jnp.mean(v, axis=-1, keepdims=True)
        vc = v - mu
        rstd = lax.rsqrt(jnp.mean(vc * vc, axis=-1, keepdims=True) + EPS)
        vn = (vc * rstd * g_ref[...] + b_ref[...]).astype(BF16)
        tri = _tril_mask()
        for gi in range(4):
            wg = jnp.where(tri, w_ref[gi], 0.0).astype(BF16)
            bg = bias_ref[gi]
            for c in range(tm // 128):
                rs, cs = slice(c * 128, (c + 1) * 128), slice(gi * 128, (gi + 1) * 128)
                mixed = jnp.dot(wg, vn[rs, cs], preferred_element_type=F32) + bg
                o_ref[rs, cs] = (u[rs, cs] * mixed).astype(BF16)

    half = pl.BlockSpec((tm, 512), lambda i: (i, 0))
    return pl.pallas_call(
        body, name="mix_a_fwd", grid=(T // tm,),
        in_specs=[half, pl.BlockSpec((tm, 512), lambda i: (i, 1)),
                  pl.BlockSpec((1, 512), lambda i: (0, 0)), pl.BlockSpec((1, 512), lambda i: (0, 0)),
                  pl.BlockSpec((4, 128, 128), lambda i: (0, 0, 0)), pl.BlockSpec((4, 128, 1), lambda i: (0, 0, 0))],
        out_specs=half, out_shape=_sds((T, D), BF16), compiler_params=_cparams(1),
    )(pab, pab, sgu_g, sgu_b, sgu_w, sgu_bias3)


def _rope_tables():
    pos = jnp.arange(T, dtype=F32)
    inv_freq = ROPE_THETA ** (-jnp.arange(ROPE_HALF, dtype=F32) * 2.0 / (2 * ROPE_HALF))
    ang = pos[:, None] * inv_freq[None, :]
    cos, sin = jnp.cos(ang), jnp.sin(ang)
    z8 = jnp.zeros((T, ROPE_HALF), F32)
    rest = HEAD - 2 * ROPE_HALF
    c64 = jnp.concatenate([cos, cos, jnp.ones((T, rest), F32)], axis=1)
    s1 = jnp.concatenate([z8, sin, jnp.zeros((T, rest), F32)], axis=1)
    s2 = jnp.concatenate([-sin, z8, jnp.zeros((T, rest), F32)], axis=1)
    return jnp.tile(c64, (1, 2)), jnp.tile(s1, (1, 2)), jnp.tile(s2, (1, 2))


def _lo_mask(shape):
    return lax.broadcasted_iota(jnp.int32, shape, 1) < HEAD


def _seg_mean(x, lo):
    s_all = jnp.sum(x, axis=-1, keepdims=True)
    s_lo = jnp.sum(jnp.where(lo, x, 0.0), axis=-1, keepdims=True)
    return jnp.where(lo, s_lo, s_all - s_lo) * (1.0 / HEAD)


def _rope(n, c, s1, s2):
    return n * c + pltpu.roll(n, ROPE_HALF, 1) * s1 + pltpu.roll(n, PAIR - ROPE_HALF, 1) * s2


def _rope_t(dy, c, s1, s2):
    return dy * c - pltpu.roll(dy, PAIR - ROPE_HALF, 1) * s2 - pltpu.roll(dy, ROPE_HALF, 1) * s1


def _prep_fwd(pab, qg, kg, tabs, tm=512):
    def body(p_ref, qg_ref, kg_ref, c_ref, s1_ref, s2_ref, *outs):
        lo = _lo_mask((tm, PAIR))
        c, s1, s2 = c_ref[...], s1_ref[...], s2_ref[...]
        for g in range(3):
            qn_ref, kn_ref, v_ref = outs[3 * g:3 * g + 3]
            for p in range(NPAIR):
                for which, gains, dst in ((0, qg_ref, qn_ref), (1, kg_ref, kn_ref)):
                    col = (2 + 3 * which + g) * 512 + p * PAIR
                    xr = p_ref[:, col:col + PAIR].astype(F32)
                    rinv = lax.rsqrt(_seg_mean(xr * xr, lo) + EPS)
                    dst[p] = _rope(xr * rinv * gains[g:g + 1, :], c, s1, s2)
                col = (8 + g) * 512 + p * PAIR
                v_ref[p] = p_ref[:, col:col + PAIR].astype(F32)

    pm = pl.BlockSpec((NPAIR, tm, PAIR), lambda i: (0, i, 0))
    tab = pl.BlockSpec((tm, PAIR), lambda i: (i, 0))
    gain = pl.BlockSpec((3, PAIR), lambda i: (0, 0))
    return pl.pallas_call(
        body, name="prep_fwd", grid=(T // tm,),
        in_specs=[pl.BlockSpec((tm, AB_IN), lambda i: (i, 0)), gain, gain, tab, tab, tab],
        out_specs=[pm] * 9, out_shape=[_sds((NPAIR, T, PAIR), F32)] * 9,
        compiler_params=_cparams(1))(pab, qg, kg, *tabs)


def _res_index(it, rate):
    window = NBACK * rate
    b = it // rate
    rho = it % rate
    start = b * window + rho
    startp = jnp.maximum(start - window, rho)
    kmin = jnp.where(b > 0, 0, NBACK)
    return start, startp, kmin


def _rows(start, rate):
    if rate == 1:
        return pl.ds(pl.multiple_of(start, NBACK), NBACK)
    return pl.ds(start, NBACK, stride=rate)


def _band():
    qi = lax.broadcasted_iota(jnp.int32, (NBACK, 2 * NBACK), 0)
    kj = lax.broadcasted_iota(jnp.int32, (NBACK, 2 * NBACK), 1)
    dist = qi + NBACK - kj
    return (dist >= 0) & (dist <= NBACK), kj


def _attn_fwd(qn, kn, v, rate, name):
    def body(q_ref, k_ref, v_ref, o_ref, l_ref):
        lo = _lo_mask((NBACK, PAIR))
        band, kj = _band()

        def step(it, carry):
            start, startp, kmin = _res_index(it, rate)
            q = q_ref[_rows(start, rate), :]
            kcat = jnp.concatenate([k_ref[_rows(startp, rate), :], k_ref[_rows(start, rate), :]], axis=0).astype(BF16)
            vcat = jnp.concatenate([v_ref[_rows(startp, rate), :], v_ref[_rows(start, rate), :]], axis=0).astype(BF16)
            ok = band & (kj >= kmin)
            outs, lses = [], []
            for h in range(2):
                hm = lo if h == 0 else jnp.logical_not(lo)
                qh = jnp.where(hm, q, 0.0).astype(BF16)
                s = lax.dot_general(qh, kcat, (((1,), (1,)), ((), ())), preferred_element_type=F32) * (HEAD ** -0.5)
                s = jnp.where(ok, s, NEG_INF)
                m = jnp.max(s, axis=-1, keepdims=True)
                pr = jnp.exp(s - m)
                l = jnp.sum(pr, axis=-1, keepdims=True)
                outs.append(jnp.dot(pr.astype(BF16), vcat, preferred_element_type=F32) / l)
                lses.append(m + jnp.log(l))
            o_ref[_rows(start, rate), :] = jnp.where(lo, outs[0], outs[1])
            l_ref[_rows(start, rate), :] = jnp.where(lo, lses[0], lses[1])
            return carry

        lax.fori_loop(0, T // NBACK, step, 0)

    pm = pl.BlockSpec((None, T, PAIR), lambda p: (p, 0, 0))
    return pl.pallas_call(
        body, name=name, grid=(NPAIR,), in_specs=[pm, pm, pm], out_specs=[pm, pm],
        out_shape=[_sds((NPAIR, T, PAIR), F32)] * 2, compiler_params=_cparams(1))(qn, kn, v)


def _merge_fwd(cat_ab, outs, lses, tm=512):
    def body(cat_in, o0, o1, o2, l0, l1, l2, cat_ref, lse_ref):
        del cat_in
        for p in range(NPAIR):
            a0, a1, a2 = l0[p], l1[p], l2[p]
            m = jnp.maximum(jnp.maximum(a0, a1), a2)
            w0, w1, w2 = jnp.exp(a0 - m), jnp.exp(a1 - m), jnp.exp(a2 - m)
            s = w0 + w1 + w2
            b = (w0 * o0[p] + w1 * o1[p] + w2 * o2[p]) / s
            cat_ref[:, p * PAIR:(p + 1) * PAIR] = b.astype(BF16)
            lse_ref[p] = m + jnp.log(s)

    pm = pl.BlockSpec((NPAIR, tm, PAIR), lambda i: (0, i, 0))
    return pl.pallas_call(
        body, name="merge_fwd", grid=(T // tm,),
        in_specs=[pl.BlockSpec(memory_space=pl.ANY)] + [pm] * 6,
        out_specs=[pl.BlockSpec((tm, 512), lambda i: (i, 1)), pm],
        out_shape=[_sds((T, D), BF16), _sds((NPAIR, T, PAIR), F32)],
        input_output_aliases={0: 0}, compiler_params=_cparams(1))(cat_ab, *outs, *lses)


def _b_pre_bwd(dcat, cat, tm=512):
    def body(db_ref, b_ref, dbp_ref, e_ref):
        lo = _lo_mask((tm, PAIR))
        for p in range(NPAIR):
            db = db_ref[:, p * PAIR:(p + 1) * PAIR].astype(F32)
            b = b_ref[:, p * PAIR:(p + 1) * PAIR].astype(F32)
            dbp_ref[p] = db
            e_ref[p] = _seg_mean(db * b, lo) * float(HEAD)

    pm = pl.BlockSpec((NPAIR, tm, PAIR), lambda i: (0, i, 0))
    right = pl.BlockSpec((tm, 512), lambda i: (i, 1))
    return pl.pallas_call(
        body, name="b_pre_bwd", grid=(T // tm,), in_specs=[right, right], out_specs=[pm, pm],
        out_shape=[_sds((NPAIR, T, PAIR), F32)] * 2, compiler_params=_cparams(1))(dcat, cat)


def _attn_bwd(qn, kn, v, dbp, e, lse, rate, name):
    def body(q_ref, k_ref, v_ref, db_ref, e_ref, lse_ref, dq_ref, dk_ref, dv_ref):
        lo = _lo_mask((NBACK, PAIR))
        band, kj = _band()
        dk_ref[...] = jnp.zeros_like(dk_ref)
        dv_ref[...] = jnp.zeros_like(dv_ref)
        scale = HEAD ** -0.5
        nt = (((1,), (1,)), ((), ()))
        tn = (((0,), (0,)), ((), ()))

        def step(it, carry):
            start, startp, kmin = _res_index(it, rate)
            rq, rp = _rows(start, rate), _rows(startp, rate)
            q = q_ref[rq, :]
            db = db_ref[rq, :]
            ev = e_ref[rq, :]
            ls = lse_ref[rq, :]
            kcat = jnp.concatenate([k_ref[rp, :], k_ref[rq, :]], axis=0).astype(BF16)
            vcat = jnp.concatenate([v_ref[rp, :], v_ref[rq, :]], axis=0).astype(BF16)
            ok = band & (kj >= kmin)
            dqs = []
            dkc = jnp.zeros((2 * NBACK, PAIR), F32)
            dvc = jnp.zeros((2 * NBACK, PAIR), F32)
            for h in range(2):
                hm = lo if h == 0 else jnp.logical_not(lo)
                qh = jnp.where(hm, q, 0.0).astype(BF16)
                dbh = jnp.where(hm, db, 0.0).astype(BF16)
                s = lax.dot_general(qh, kcat, nt, preferred_element_type=F32) * scale
                s = jnp.where(ok, s, NEG_INF)
                col = slice(h * HEAD, h * HEAD + 1)
                pt = jnp.exp(s - ls[:, col])
                dp = lax.dot_general(dbh, vcat, nt, preferred_element_type=F32)
                ds = (pt * (dp - ev[:, col])).astype(BF16)
                dqs.append(jnp.dot(ds, kcat, preferred_element_type=F32) * scale)
                dkc = dkc + lax.dot_general(ds, qh, tn, preferred_element_type=F32) * scale
                dvc = dvc + lax.dot_general(pt.astype(BF16), dbh, tn, preferred_element_type=F32)
            dq_ref[rq, :] = jnp.where(lo, dqs[0], dqs[1])
            dk_ref[rp, :] += dkc[0:NBACK]
            dk_ref[rq, :] += dkc[NBACK:]
            dv_ref[rp, :] += dvc[0:NBACK]
            dv_ref[rq, :] += dvc[NBACK:]
            return carry

        lax.fori_loop(0, T // NBACK, step, 0)

    pm = pl.BlockSpec((None, T, PAIR), lambda p: (p, 0, 0))
    return pl.pallas_call(
        body, name=name, grid=(NPAIR,), in_specs=[pm] * 6, out_specs=[pm] * 3,
        out_shape=[_sds((NPAIR, T, PAIR), F32)] * 3, compiler_params=_cparams(1, 56))(qn, kn, v, dbp, e, lse)


def _ab_in_bwd(pab, dcat, sgu_g, sgu_b, sgu_w, sgu_bias3, qg, kg, tabs, dqkv, tm=256):
    def body(p_ref, dcat_ref, g_ref, b_ref, w_ref, bias_ref, qg_ref, kg_ref, c_ref, s1_ref, s2_ref, *rest):
        dq_refs = rest[0:9]
        o_ref, dwm_ref, dbias_ref, dsg_ref, dsb_ref, dgain_ref = rest[9:]
        i = pl.program_id(0)

        @pl.when(i == 0)
        def _():
            dwm_ref[...] = jnp.zeros_like(dwm_ref)
            dbias_ref[...] = jnp.zeros_like(dbias_ref)
            dsg_ref[...] = jnp.zeros_like(dsg_ref)
            dsb_ref[...] = jnp.zeros_like(dsb_ref)
            dgain_ref[...] = jnp.zeros_like(dgain_ref)

        zu = p_ref[:, 0:512].astype(F32)
        zv = p_ref[:, 512:1024].astype(F32)
        u = _gelu(zu)
        v = _gelu(zv)
        mu = jnp.mean(v, axis=-1, keepdims=True)
        vc = v - mu
        rstd = lax.rsqrt(jnp.mean(vc * vc, axis=-1, keepdims=True) + EPS)
        xhat = vc * rstd
        vn = (xhat * g_ref[...] + b_ref[...]).astype(BF16)
        da = dcat_ref[...].astype(F32)
        tri = _tril_mask()
        du_parts = [[None] * 4 for _ in range(tm // 128)]
        dvn_parts = [[None] * 4 for _ in range(tm // 128)]
        for gi in range(4):
            wg = jnp.where(tri, w_ref[gi], 0.0).astype(BF16)
            bg = bias_ref[gi]
            for c in range(tm // 128):
                rs, cs = slice(c * 128, (c + 1) * 128), slice(gi * 128, (gi + 1) * 128)
                vblk = vn[rs, cs]
                mixed = jnp.dot(wg, vblk, preferred_element_type=F32) + bg
                dab = da[rs, cs]
                du_parts[c][gi] = dab * mixed
                dmixed = dab * u[rs, cs]
                dmb = dmixed.astype(BF16)
                dvn_parts[c][gi] = lax.dot_general(wg, dmb, (((0,), (0,)), ((), ())), preferred_element_type=F32)
                dwm = lax.dot_general(dmb, vblk, (((1,), (1,)), ((), ())), preferred_element_type=F32)
                dwm_ref[gi] += jnp.where(tri, dwm, 0.0)
                dbias_ref[gi] += dmixed
        du = jnp.concatenate([jnp.concatenate(r, axis=1) for r in du_parts], axis=0)
        dvn = jnp.concatenate([jnp.concatenate(r, axis=1) for r in dvn_parts], axis=0)
        dsg_ref[...] += jnp.sum(dvn * xhat, axis=0, keepdims=True)
        dsb_ref[...] += jnp.sum(dvn, axis=0, keepdims=True)
        dxh = dvn * g_ref[...]
        dv = rstd * (dxh - jnp.mean(dxh, axis=-1, keepdims=True)
                     - xhat * jnp.mean(dxh * xhat, axis=-1, keepdims=True))
        o_ref[:, 0:512] = (du * _gelu_grad(zu)).astype(BF16)
        o_ref[:, 512:1024] = (dv * _gelu_grad(zv)).astype(BF16)

        lo = _lo_mask((tm, PAIR))
        c, s1, s2 = c_ref[...], s1_ref[...], s2_ref[...]
        for g in range(3):
            dq_ref, dk_ref, dv_ref = dq_refs[3 * g:3 * g + 3]
            for p in range(NPAIR):
                for which, gains, src in ((0, qg_ref, dq_ref), (1, kg_ref, dk_ref)):
                    col = (2 + 3 * which + g) * 512 + p * PAIR
                    xr = p_ref[:, col:col + PAIR].astype(F32)
                    rinv = lax.rsqrt(_seg_mean(xr * xr, lo) + EPS)
                    xh = xr * rinv
                    dn = _rope_t(src[p], c, s1, s2)
                    row = 2 * g + which
                    dgain_ref[row:row + 1, :] += jnp.sum(dn * xh, axis=0, keepdims=True)
                    dxh2 = dn * gains[g:g + 1, :]
                    dx = rinv * (dxh2 - xh * _seg_mean(dxh2 * xh, lo))
                    o_ref[:, col:col + PAIR] = dx.astype(BF16)
                col = (8 + g) * 512 + p * PAIR
                o_ref[:, col:col + PAIR] = dv_ref[p].astype(BF16)

    pm = pl.BlockSpec((NPAIR, tm, PAIR), lambda i: (0, i, 0))
    tab = pl.BlockSpec((tm, PAIR), lambda i: (i, 0))
    gain = pl.BlockSpec((3, PAIR), lambda i: (0, 0))
    vec = pl.BlockSpec((1, 512), lambda i: (0, 0))
    full = pl.BlockSpec((tm, AB_IN), lambda i: (i, 0))
    w4 = pl.BlockSpec((4, 128, 128), lambda i: (0, 0, 0))
    return pl.pallas_call(
        body, name="ab_in_bwd", grid=(T // tm,),
        in_specs=[full, pl.BlockSpec((tm, 512), lambda i: (i, 0)), vec, vec, w4,
                  pl.BlockSpec((4, 128, 1), lambda i: (0, 0, 0)), gain, gain, tab, tab, tab] + [pm] * 9,
        out_specs=[full, w4, w4, vec, vec, pl.BlockSpec((8, PAIR), lambda i: (0, 0))],
        out_shape=[_sds((T, AB_IN), BF16), _sds((4, 128, 128), F32), _sds((4, 128, 128), F32),
                   _sds((1, 512), F32), _sds((1, 512), F32), _sds((8, PAIR), F32)],
        compiler_params=_cparams(1))(pab, dcat, sgu_g, sgu_b, sgu_w, sgu_bias3, qg, kg, *tabs, *dqkv)


def _ln_stats(x):
    mu = jnp.mean(x, axis=-1, keepdims=True)
    xc = x - mu
    rstd = lax.rsqrt(jnp.mean(xc * xc, axis=-1, keepdims=True) + EPS)
    return xc * rstd, rstd


def _cd_fwd(pcd, cw, cb, lg, lb, dw, tm=512):
    per = tm // HALO

    def body(p_ref, h_ref, cw_ref, cb_ref, lg_ref, lb_ref, dw_ref, cat_ref, c0_ref, c1_ref, dd_ref, y_ref, buf, buf2):
        i = pl.program_id(0)
        live = jnp.where(i > 0, 1.0, 0.0)
        a = p_ref[:, 0:512].astype(F32)
        gt = p_ref[:, 512:1024].astype(F32)
        gb = p_ref[:, 1024:1536].astype(F32)
        gc = p_ref[:, 1536:2048].astype(F32)
        hv = p_ref[:, 2048:2560].astype(F32)
        c0 = a * _sigmoid(gt)
        dd = gc * hv
        buf[0:HALO, :] = h_ref[:, 0:512].astype(F32) * _sigmoid(h_ref[:, 512:1024].astype(F32)) * live
        buf[HALO:, :] = c0
        buf2[0:HALO, :] = h_ref[:, 1536:2048].astype(F32) * h_ref[:, 2048:2560].astype(F32) * live
        buf2[HALO:, :] = dd
        acc = jnp.broadcast_to(cb_ref[...], (tm, 512))
        for j in range(CONV_C_TAPS):
            acc = acc + cw_ref[j:j + 1, :] * buf[pl.ds(HALO - (CONV_C_TAPS - 1) + j, tm), :]
        xhat, _ = _ln_stats(acc)
        c2 = xhat * lg_ref[...] + lb_ref[...]
        y = jnp.zeros((tm, 512), F32)
        for j in range(CONV_D_TAPS):
            y = y + dw_ref[j:j + 1, :] * buf2[pl.ds(HALO - (CONV_D_TAPS - 1) + j, tm), :]
        cat_ref[:, 0:512] = (c2 * _sigmoid(c2)).astype(BF16)
        cat_ref[:, 512:1024] = (gb * y).astype(BF16)
        c0_ref[...] = c0.astype(BF16)
        c1_ref[...] = acc
        dd_ref[...] = dd.astype(BF16)
        y_ref[...] = y.astype(BF16)

    half = pl.BlockSpec((tm, 512), lambda i: (i, 0))
    vec = pl.BlockSpec((1, 512), lambda i: (0, 0))
    return pl.pallas_call(
        body, name="cd_fwd", grid=(T // tm,),
        in_specs=[pl.BlockSpec((tm, CD_IN), lambda i: (i, 0)),
                  pl.BlockSpec((HALO, CD_IN), lambda i: (jnp.maximum(i * per - 1, 0), 0)),
                  pl.BlockSpec((32, 512), lambda i: (0, 0)), vec, vec, vec, pl.BlockSpec((8, 512), lambda i: (0, 0))],
        out_specs=[pl.BlockSpec((tm, D), lambda i: (i, 0)), half, half, half, half],
        out_shape=[_sds((T, D), BF16), _sds((T, 512), BF16), _sds((T, 512), F32), _sds((T, 512), BF16),
                   _sds((T, 512), BF16)],
        scratch_shapes=[pltpu.VMEM((HALO + tm, 512), F32), pltpu.VMEM((HALO + tm, 512), F32)],
        compiler_params=_cparams(1))(pcd, pcd, cw, cb, lg, lb, dw)


def _cd_bwd_pw(dcat, c1, pcd, y, lg, lb, tm=512):
    def body(dcat_ref, c1_ref, gb_ref, y_ref, lg_ref, lb_ref, dc1_ref, dy3_ref, dgb_ref, dlg_ref, dlb_ref, dcb_ref):
        i = pl.program_id(0)

        @pl.when(i == 0)
        def _():
            dlg_ref[...] = jnp.zeros_like(dlg_ref)
            dlb_ref[...] = jnp.zeros_like(dlb_ref)
            dcb_ref[...] = jnp.zeros_like(dcb_ref)

        dc = dcat_ref[:, 0:512].astype(F32)
        ddo = dcat_ref[:, 512:1024].astype(F32)
        xhat, rstd = _ln_stats(c1_ref[...])
        c2 = xhat * lg_ref[...] + lb_ref[...]
        sg = _sigmoid(c2)
        dc2 = dc * sg * (1.0 + c2 * (1.0 - sg))
        dlg_ref[...] += jnp.sum(dc2 * xhat, axis=0, keepdims=True)
        dlb_ref[...] += jnp.sum(dc2, axis=0, keepdims=True)
        dxh = dc2 * lg_ref[...]
        dc1 = rstd * (dxh - jnp.mean(dxh, axis=-1, keepdims=True)
                      - xhat * jnp.mean(dxh * xhat, axis=-1, keepdims=True))
        dcb_ref[...] += jnp.sum(dc1, axis=0, keepdims=True)
        dc1_ref[...] = dc1
        dgb_ref[...] = (ddo * y_ref[...].astype(F32)).astype(BF16)
        dy3_ref[...] = ddo * gb_ref[...].astype(F32)

    half = pl.BlockSpec((tm, 512), lambda i: (i, 0))
    vec = pl.BlockSpec((1, 512), lambda i: (0, 0))
    return pl.pallas_call(
        body, name="cd_bwd_pw", grid=(T // tm,),
        in_specs=[pl.BlockSpec((tm, D), lambda i: (i, 0)), half, pl.BlockSpec((tm, 512), lambda i: (i, 2)), half,
                  vec, vec],
        out_specs=[half, half, half, vec, vec, vec],
        out_shape=[_sds((T, 512), F32), _sds((T, 512), F32), _sds((T, 512), BF16),
                   _sds((1, 512), F32), _sds((1, 512), F32), _sds((1, 512), F32)],
        compiler_params=_cparams(1))(dcat, c1, pcd, y, lg, lb)


def _cd_bwd_conv(pcd, dc1, dy3, c0, dd, dgb, cw, dw, tm=512):
    per = tm // HALO
    nblk = T // tm
    last32 = T // HALO - 1

    def body(p_ref, dc1_ref, dc1n_ref, dy3_ref, dy3n_ref, c0_ref, c0p_ref, dd_ref, ddp_ref, dgb_ref, cw_ref, dw_ref,
             o_ref, dcw_ref, ddw_ref, dbuf, cbuf, d3buf, ddbuf):
        i = pl.program_id(0)
        has_prev = jnp.where(i > 0, 1.0, 0.0)
        has_next = jnp.where(i < nblk - 1, 1.0, 0.0)

        @pl.when(i == 0)
        def _():
            dcw_ref[...] = jnp.zeros_like(dcw_ref)
            ddw_ref[...] = jnp.zeros_like(ddw_ref)

        dc1 = dc1_ref[...]
        dy3 = dy3_ref[...]
        dbuf[0:tm, :] = dc1
        dbuf[tm:, :] = dc1n_ref[...] * has_next
        d3buf[0:tm, :] = dy3
        d3buf[tm:, :] = dy3n_ref[...] * has_next
        cbuf[0:HALO, :] = c0p_ref[...].astype(F32) * has_prev
        cbuf[HALO:, :] = c0_ref[...].astype(F32)
        ddbuf[0:HALO, :] = ddp_ref[...].astype(F32) * has_prev
        ddbuf[HALO:, :] = dd_ref[...].astype(F32)

        dc0 = jnp.zeros((tm, 512), F32)
        for j in range(CONV_C_TAPS):
            dc0 = dc0 + cw_ref[j:j + 1, :] * dbuf[pl.ds(CONV_C_TAPS - 1 - j, tm), :]
            dcw_ref[j:j + 1, :] += jnp.sum(dc1 * cbuf[pl.ds(HALO - (CONV_C_TAPS - 1) + j, tm), :], axis=0, keepdims=True)
        ddd = jnp.zeros((tm, 512), F32)
        for j in range(CONV_D_TAPS):
            ddd = ddd + dw_ref[j:j + 1, :] * d3buf[pl.ds(CONV_D_TAPS - 1 - j, tm), :]
            ddw_ref[j:j + 1, :] += jnp.sum(dy3 * ddbuf[pl.ds(HALO - (CONV_D_TAPS - 1) + j, tm), :], axis=0, keepdims=True)

        a = p_ref[:, 0:512].astype(F32)
        gt = p_ref[:, 512:1024].astype(F32)
        gc = p_ref[:, 1536:2048].astype(F32)
        hv = p_ref[:, 2048:2560].astype(F32)
        sg = _sigmoid(gt)
        o_ref[:, 0:512] = (dc0 * sg).astype(BF16)
        o_ref[:, 512:1024] = (dc0 * a * sg * (1.0 - sg)).astype(BF16)
        o_ref[:, 1024:1536] = dgb_ref[...]
        o_ref[:, 1536:2048] = (ddd * hv).astype(BF16)
        o_ref[:, 2048:2560] = (ddd * gc).astype(BF16)

    half = pl.BlockSpec((tm, 512), lambda i: (i, 0))
    nxt = pl.BlockSpec((HALO, 512), lambda i: (jnp.minimum((i + 1) * per, last32), 0))
    prv = pl.BlockSpec((HALO, 512), lambda i: (jnp.maximum(i * per - 1, 0), 0))
    full = pl.BlockSpec((tm, CD_IN), lambda i: (i, 0))
    return pl.pallas_call(
        body, name="cd_bwd_conv", grid=(nblk,),
        in_specs=[full, half, nxt, half, nxt, half, prv, half, prv, half,
                  pl.BlockSpec((32, 512), lambda i: (0, 0)), pl.BlockSpec((8, 512), lambda i: (0, 0))],
        out_specs=[full, pl.BlockSpec((32, 512), lambda i: (0, 0)), pl.BlockSpec((8, 512), lambda i: (0, 0))],
        out_shape=[_sds((T, CD_IN), BF16), _sds((32, 512), F32), _sds((8, 512), F32)],
        scratch_shapes=[pltpu.VMEM((tm + HALO, 512), F32), pltpu.VMEM((HALO + tm, 512), F32),
                        pltpu.VMEM((tm + HALO, 512), F32), pltpu.VMEM((HALO + tm, 512), F32)],
        compiler_params=_cparams(1))(pcd, dc1, dc1, dy3, dy3, c0, c0, dd, dd, dgb, cw, dw)


def _local_step(x, tgt, W, fetch=None, on_grad=None):
    W = dict(W)
    if fetch is None:
        fetch = lambda stage, after: {}
    if on_grad is None:
        on_grad = lambda key, arr: None
    tabs = _rope_tables()
    qg = jnp.tile(W["q_norm_g"], (1, 2))
    kg = jnp.tile(W["k_norm_g"], (1, 2))
    bias3 = W["sgu_bias"].reshape(4, 128, 1)
    G = {}

    h0 = _rms_fwd(x, W["ab_norm_g"], "rms_fwd_ab")
    pab = _mm_nt(h0, W["wt_ab_in"], "mm_ab_in", dep=W.get("dep0"))
    cat_ab = _mix_a_fwd(pab, W["sgu_norm_g"], W["sgu_norm_b"], W["sgu_w"], bias3)
    qkv = _prep_fwd(pab, qg, kg, tabs)
    outs, lses = [], []
    for g, rate in enumerate(DIL_RATES):
        o, l = _attn_fwd(qkv[3 * g], qkv[3 * g + 1], qkv[3 * g + 2], rate, f"attn_fwd_{g}")
        outs.append(o)
        lses.append(l)
    cat_ab, lse = _merge_fwd(cat_ab, outs, lses)
    W.update(fetch(1, lse))
    x1 = _mm_nn(cat_ab, W["w_ab_out"], "mm_ab_out", resid=x)

    def ffn_fwd(xin, layer):
        h = _rms_fwd(xin, W["ffn_norm_g"][layer:layer + 1], f"rms_fwd_ffn{layer}")
        pf = _mm_nt(h, W[f"wt_ffn_in{layer}"], f"mm_ffn_in{layer}")
        act = _swiglu_fwd(pf, f"swiglu_fwd{layer}")
        xout = _mm_nn(act, W[f"w_ffn_down{layer}"], f"mm_ffn_down{layer}", resid=xin)
        return xout, (h, pf, act)

    x2, ffn0 = ffn_fwd(x1, 0)
    h2 = _rms_fwd(x2, W["cd_norm_g"], "rms_fwd_cd")
    W.update(fetch(2, h2))
    pcd = _mm_nt(h2, W["wt_cd_in"], "mm_cd_in")
    cat_cd, c0, c1, dd, yv = _cd_fwd(pcd, W["conv_c_w32"], W["conv_c_b"], W["c_ln_g"], W["c_ln_b"], W["conv_d_w8"])
    x3 = _mm_nn(cat_cd, W["w_cd_out"], "mm_cd_out", resid=x2)
    x4, ffn1 = ffn_fwd(x3, 1)
    dy, dyb, loss_cols = _loss_fwd_bwd(x4, tgt)

    def ffn_bwd(xin, saved, dres, dresb, layer):
        h, pf, act = saved
        G[f"w_ffn_down{layer}"] = _mm_tn(act, dresb, f"mm_g_ffn_down{layer}")
        dep = on_grad(f"w_ffn_down{layer}", G[f"w_ffn_down{layer}"])
        dact = _mm_nt(dresb, W[f"w_ffn_down{layer}"], f"mm_d_act{layer}", dep=dep)
        dpf = _swiglu_bwd(pf, dact, f"swiglu_bwd{layer}")
        G[f"wt_ffn_in{layer}"] = _mm_tn(dpf, h, f"mm_g_ffn_in{layer}")
        dep = on_grad(f"wt_ffn_in{layer}", G[f"wt_ffn_in{layer}"])
        dh = _mm_nn(dpf, W[f"wt_ffn_in{layer}"], f"mm_d_h_ffn{layer}", dep=dep)
        dx, dxb, dg = _rms_bwd(xin, W["ffn_norm_g"][layer:layer + 1], dh, dres, f"rms_bwd_ffn{layer}")
        G[f"ffn_norm_g{layer}"] = dg
        return dx, dxb

    dx3, dx3b = ffn_bwd(x3, ffn1, dy, dyb, 1)

    G["w_cd_out"] = _mm_tn(cat_cd, dx3b, "mm_g_cd_out")
    dep = on_grad("w_cd_out", G["w_cd_out"])
    dcat_cd = _mm_nt(dx3b, W["w_cd_out"], "mm_d_cat_cd", dep=dep)
    dc1, dy3, dgb, G["c_ln_g"], G["c_ln_b"], G["conv_c_b"] = _cd_bwd_pw(dcat_cd, c1, pcd, yv, W["c_ln_g"], W["c_ln_b"])
    dpcd, G["conv_c_w32"], G["conv_d_w8"] = _cd_bwd_conv(pcd, dc1, dy3, c0, dd, dgb, W["conv_c_w32"], W["conv_d_w8"])
    G["wt_cd_in"] = _mm_tn(dpcd, h2, "mm_g_cd_in")
    dep = on_grad("wt_cd_in", G["wt_cd_in"])
    dh2 = _mm_nn(dpcd, W["wt_cd_in"], "mm_d_h_cd", dep=dep)
    dx2, dx2b, G["cd_norm_g"] = _rms_bwd(x2, W["cd_norm_g"], dh2, dx3, "rms_bwd_cd")

    dx1, dx1b = ffn_bwd(x1, ffn0, dx2, dx2b, 0)

    G["w_ab_out"] = _mm_tn(cat_ab, dx1b, "mm_g_ab_out")
    dep = on_grad("w_ab_out", G["w_ab_out"])
    dcat_ab = _mm_nt(dx1b, W["w_ab_out"], "mm_d_cat_ab", dep=dep)
    dbp, e = _b_pre_bwd(dcat_ab, cat_ab)
    dqkv = []
    for g, rate in enumerate(DIL_RATES):
        dqkv += _attn_bwd(qkv[3 * g], qkv[3 * g + 1], qkv[3 * g + 2], dbp, e, lse, rate, f"attn_bwd_{g}")
    dpab, G["sgu_w"], dbias_part, G["sgu_norm_g"], G["sgu_norm_b"], dgain = _ab_in_bwd(
        pab, dcat_ab, W["sgu_norm_g"], W["sgu_norm_b"], W["sgu_w"], bias3, qg, kg, tabs, dqkv)
    G["sgu_bias"] = jnp.sum(dbias_part, axis=-1)
    dgain = dgain[0:6, 0:HEAD] + dgain[0:6, HEAD:PAIR]
    G["q_norm_g"] = dgain[0::2]
    G["k_norm_g"] = dgain[1::2]
    G["wt_ab_in"] = _mm_tn(dpab, h0, "mm_g_ab_in")
    dep = on_grad("wt_ab_in", G["wt_ab_in"])
    dh0 = _mm_nn(dpab, W["wt_ab_in"], "mm_d_h_ab", dep=dep)
    grad_x, _, G["ab_norm_g"] = _rms_bwd(x, W["ab_norm_g"], dh0, dx1, "rms_bwd_ab")
    return loss_cols, grad_x, G


def _my_place():
    return lax.axis_index("x"), lax.axis_index("y"), lax.axis_index("c")


def _dev_index(px, py, pc):
    return 4 * px + 2 * py + pc


def _flip(place, k):
    x, y, c = place
    return (1 - x if k & 4 else x, 1 - y if k & 2 else y, 1 - c if k & 1 else c)


def _gather_first(ab_in_t, small):
    out_shape = [_sds((NDEV,) + ab_in_t.shape, BF16), _sds((NDEV,) + small.shape, F32)]
    n_items = 2

    def body(ab_in_r, small_r, o_ab_in, o_small, send_sems, recv_sems, loc_sems):
        x, y, c = _my_place()
        me = (x, y, c)
        sib = (x, y, 1 - c)
        chips = [(1 - x, y), (x, 1 - y), (1 - x, 1 - y)]
        items = [(ab_in_r, lambda d: o_ab_in.at[d]), (small_r, lambda d: o_small.at[d])]

        def rcopy(it, k, src, dst, to):
            return pltpu.make_async_remote_copy(src_ref=src, dst_ref=dst, send_sem=send_sems.at[it, k],
                                                recv_sem=recv_sems.at[it, k], device_id=to, device_id_type=MESH)

        started = []
        locals_ = []
        for it, (src, dst) in enumerate(items):
            mine = dst(_dev_index(*me))
            lc = pltpu.make_async_copy(src, mine, loc_sems.at[it])
            lc.start()
            locals_.append(lc)
            first = [rcopy(it, 0, src, mine, sib)]
            first += [rcopy(it, 1 + j, src, mine, (*chip, c)) for j, chip in enumerate(chips)]
            for cp in first:
                cp.start()
            started += first
        for it, (src, dst) in enumerate(items):
            for j, chip in enumerate(chips):
                blk = dst(_dev_index(*chip, c))
                rcopy(it, 1 + j, blk, blk, me).wait_recv()
                fwd = rcopy(it, 4 + j, blk, blk, sib)
                fwd.start()
                started.append(fwd)
        for it, (src, dst) in enumerate(items):
            blk = dst(_dev_index(x, y, 1 - c))
            rcopy(it, 0, blk, blk, me).wait_recv()
            for j, chip in enumerate(chips):
                blk = dst(_dev_index(*chip, 1 - c))
                rcopy(it, 4 + j, blk, blk, me).wait_recv()
        for cp in started:
            cp.wait_send()
        for lc in locals_:
            lc.wait()

    return pl.pallas_call(
        body, name="gather_first", in_specs=[HBM_SPEC] * 2, out_specs=[HBM_SPEC] * 2, out_shape=out_shape,
        scratch_shapes=[pltpu.SemaphoreType.DMA((n_items, 7)), pltpu.SemaphoreType.DMA((n_items, 7)),
                        pltpu.SemaphoreType.DMA((n_items,))],
    )(ab_in_t, small)


HBM_ONLY = pl.BlockSpec(memory_space=pltpu.HBM)
SEM_SPEC = pl.BlockSpec(memory_space=pltpu.SEMAPHORE)
IN_FLIGHT = pltpu.CompilerParams(has_side_effects=pltpu.SideEffectType.DATAFLOW_SIDE_EFFECTING)


def _in_hbm(a):
    return pltpu.with_memory_space_constraint(a, pltpu.HBM)


def _exchange_start(name, srcs, land_shapes, items, dep=None):
    ns, nl, ni = len(srcs), len(land_shapes), len(items)
    lands = [lax.empty(s, BF16) for s in land_shapes]

    def body(*refs):
        S, L = refs[0:ns], refs[ns:ns + nl]
        first_out = ns + nl + (0 if dep is None else 1)
        send_sems, recv_sems, token = refs[first_out], refs[first_out + 1], refs[-1]
        me = _my_place()
        mi = _dev_index(*me)
        for i, (src, dst) in enumerate(items):
            for k in range(1, NDEV):
                peer = _flip(me, k)
                pltpu.make_async_remote_copy(
                    src_ref=src(S, _dev_index(*peer)), dst_ref=dst(L, mi), send_sem=send_sems.at[7 * i + k - 1],
                    recv_sem=recv_sems.at[7 * i + k - 1], device_id=peer, device_id_type=MESH).start()
        token[...] = jnp.zeros_like(token)

    thru = [pltpu.HBM(a.shape, a.dtype) for a in srcs] + [pltpu.HBM(s, BF16) for s in land_shapes]
    args = [_in_hbm(a) for a in srcs] + [_in_hbm(a) for a in lands]
    in_specs = [HBM_ONLY] * (ns + nl)
    if dep is not None:
        args.append(dep)
        in_specs.append(HBM_SPEC)
    outs = pl.pallas_call(
        body, name=name, in_specs=in_specs,
        out_shape=(pltpu.SemaphoreType.DMA((7 * ni,)), pltpu.SemaphoreType.DMA((7 * ni,)), *thru, _sds((8, 128), F32)),
        out_specs=(SEM_SPEC, SEM_SPEC, *[HBM_ONLY] * (ns + nl), pl.BlockSpec(memory_space=pltpu.VMEM)),
        input_output_aliases={j: 2 + j for j in range(ns + nl)}, compiler_params=IN_FLIGHT)(*args)
    return dict(send=outs[0], recv=outs[1], srcs=list(outs[2:2 + ns]), lands=list(outs[2 + ns:2 + ns + nl]),
                token=outs[-1], items=items)


def _exchange_wait(name, states, after):
    counts = [(len(st["srcs"]), len(st["lands"]), len(st["items"])) for st in states]
    n_items = sum(c[2] for c in counts)
    n_arrays = sum(c[0] + c[1] for c in counts)

    def body(*refs):
        loc_sems = refs[-1]
        me = _my_place()
        mi = _dev_index(*me)
        pos = 0
        sem_pos = n_arrays
        it = 0
        locals_ = []
        for st, (ns, nl, ni) in zip(states, counts):
            S, L = refs[pos:pos + ns], refs[pos + ns:pos + ns + nl]
            send_sems, recv_sems = refs[sem_pos], refs[sem_pos + 1]
            pos += ns + nl
            sem_pos += 2
            for i, (src, dst) in enumerate(st["items"]):
                lc = pltpu.make_async_copy(src(S, mi), dst(L, mi), loc_sems.at[it])
                lc.start()
                locals_.append(lc)
                it += 1
                for k in range(1, NDEV):
                    cp = pltpu.make_async_remote_copy(
                        src_ref=src(S, mi), dst_ref=dst(L, mi), send_sem=send_sems.at[7 * i + k - 1],
                        recv_sem=recv_sems.at[7 * i + k - 1], device_id=me, device_id_type=MESH)
                    cp.wait_send()
                    cp.wait_recv()
        for lc in locals_:
            lc.wait()

    arrays, sems = [], []
    for st in states:
        arrays += st["srcs"] + st["lands"]
        sems += [st["send"], st["recv"]]
    outs = pl.pallas_call(
        body, name=name, in_specs=[HBM_ONLY] * n_arrays + [SEM_SPEC] * len(sems) + [HBM_SPEC],
        out_shape=tuple(pltpu.HBM(a.shape, a.dtype) for a in arrays), out_specs=tuple([HBM_ONLY] * n_arrays),
        input_output_aliases={j: j for j in range(n_arrays)},
        scratch_shapes=[pltpu.SemaphoreType.DMA((n_items,))], compiler_params=IN_FLIGHT)(*arrays, *sems, after)
    lands, pos = [], 0
    for ns, nl, _ in counts:
        lands.append(list(outs[pos + ns:pos + ns + nl]))
        pos += ns + nl
    return lands


def _allreduce_small(packed):
    rows = packed.shape[0]

    def body(x_ref, o_ref, land, send_sems, recv_sems):
        me = _my_place()
        mi = _dev_index(*me)
        copies = []
        for k in range(1, NDEV):
            cp = pltpu.make_async_remote_copy(
                src_ref=x_ref, dst_ref=land.at[mi], send_sem=send_sems.at[k - 1], recv_sem=recv_sems.at[k - 1],
                device_id=_flip(me, k), device_id_type=MESH)
            cp.start()
            copies.append(cp)
        land[mi] = x_ref[...]
        for cp in copies:
            cp.wait()
        acc = land[0]
        for d in range(1, NDEV):
            acc = acc + land[d]
        o_ref[...] = acc

    return pl.pallas_call(
        body, name="allreduce_small", out_shape=_sds((rows, 128), F32),
        in_specs=[pl.BlockSpec(memory_space=pltpu.VMEM)], out_specs=pl.BlockSpec(memory_space=pltpu.VMEM),
        scratch_shapes=[pltpu.VMEM((NDEV, rows, 128), F32), pltpu.SemaphoreType.DMA((7,)),
                        pltpu.SemaphoreType.DMA((7,))],
    )(packed)


def _adam_math(w, g, m, v):
    m2 = ADAM_B1 * m + (1.0 - ADAM_B1) * g
    v2 = ADAM_B2 * v + (1.0 - ADAM_B2) * (g * g)
    delta = -ADAM_LR * ((m2 * ADAM_C1) / (jnp.sqrt(v2 * ADAM_C2) + ADAM_EPS) + ADAM_WD * w)
    return delta, m2, v2


def _adam_layer(land, sel, w, m, v, layer, name, transposed, prev=None):
    R = land.shape[2]
    tc = 256 if transposed else 512

    def body(l_ref, w_ref, m_ref, v_ref, *rest):
        g_out, d_out, m_out, v_out = rest[-4:]
        g = l_ref[0].astype(F32)
        for d in range(1, NDEV):
            g = g + l_ref[d].astype(F32)
        if transposed:
            g = g.T
        delta, m2, v2 = _adam_math(w_ref[...], g, m_ref[...], v_ref[...])
        g_out[...] = g
        d_out[...] = delta
        m_out[...] = m2
        v_out[...] = v2

    if transposed:
        wspec = pl.BlockSpec((None, tc, R), lambda i: (layer, i, 0))
    else:
        wspec = pl.BlockSpec((None, R, tc), lambda i: (layer, 0, i))
    in_specs = [pl.BlockSpec((None, NDEV, R, tc), lambda i: (sel, 0, 0, i)), wspec, wspec, wspec]
    args = [land, w, m, v]
    aliases = {}
    if prev is not None:
        in_specs += [HBM_SPEC] * 4
        args += list(prev)
        aliases = {4 + j: j for j in range(4)}
    return pl.pallas_call(
        body, name=name, grid=(D // tc,), in_specs=in_specs, out_specs=[wspec] * 4,
        out_shape=[_sds(w.shape, F32)] * 4, input_output_aliases=aliases, compiler_params=_cparams(1))(*args)


def _adam_stacked(lands, sel, w, m, v, name, transposed):
    res = None
    for layer, land in enumerate(lands):
        res = _adam_layer(land, sel, w, m, v, layer, f"{name}{layer}", transposed, prev=res)
    return res


def _adam_small(ws, gs, ms, vs):
    n = len(ws)

    def body(*refs):
        w_r, g_r, m_r, v_r = refs[0:n], refs[n:2 * n], refs[2 * n:3 * n], refs[3 * n:4 * n]
        d_o, m_o, v_o = refs[4 * n:5 * n], refs[5 * n:6 * n], refs[6 * n:7 * n]
        for i in range(n):
            delta, m2, v2 = _adam_math(w_r[i][...], g_r[i][...], m_r[i][...], v_r[i][...])
            d_o[i][...] = delta
            m_o[i][...] = m2
            v_o[i][...] = v2

    vm = pl.BlockSpec(memory_space=pltpu.VMEM)
    shapes = [_sds(w.shape, F32) for w in ws]
    outs = pl.pallas_call(body, name="adam_small", in_specs=[vm] * (4 * n), out_specs=[vm] * (3 * n),
                          out_shape=shapes * 3)(*ws, *gs, *ms, *vs)
    return outs[0:n], outs[n:2 * n], outs[2 * n:3 * n]


WEIGHT_NAMES = ("ab_norm_g", "ab_w_in", "sgu_norm_g", "sgu_norm_b", "sgu_w", "sgu_bias", "q_norm_g", "k_norm_g",
                "ab_w_out", "cd_norm_g", "cd_w_in", "conv_c_w", "conv_c_b", "c_ln_g", "c_ln_b", "conv_d_w",
                "cd_w_out", "ffn_norm_g", "ffn_w_gate", "ffn_w_up", "ffn_w_down")
SMALL_2D = (("ab_norm_g", (1, 1024)), ("sgu_norm_g", (1, 512)), ("sgu_norm_b", (1, 512)), ("sgu_w", (512, 128)),
            ("sgu_bias", (4, 128)), ("q_norm_g", (3, 64)), ("k_norm_g", (3, 64)), ("cd_norm_g", (1, 128)),
            ("conv_c_w", (31, 64)), ("conv_c_b", (1, 64)), ("c_ln_g", (1, 64)), ("c_ln_b", (1, 64)),
            ("conv_d_w", (3, 64)), ("ffn_norm_g", (2, 1024)))
SHARD_C = 64


def _pack_rows(parts, rows):
    flat = jnp.concatenate([p.reshape(-1) for p in parts])
    return jnp.pad(flat, (0, rows * 128 - flat.shape[0])).reshape(rows, 128)


def kernel(x, ab_norm_g, ab_w_in, sgu_norm_g, sgu_norm_b, sgu_w, sgu_bias, q_norm_g, k_norm_g, ab_w_out, cd_norm_g, cd_w_in, conv_c_w, conv_c_b, c_ln_g, c_ln_b, conv_d_w, cd_w_out, ffn_norm_g, ffn_w_gate, ffn_w_up, ffn_w_down, loss_target, m_ab_norm_g, m_ab_w_in, m_sgu_norm_g, m_sgu_norm_b, m_sgu_w, m_sgu_bias, m_q_norm_g, m_k_norm_g, m_ab_w_out, m_cd_norm_g, m_cd_w_in, m_conv_c_w, m_conv_c_b, m_c_ln_g, m_c_ln_b, m_conv_d_w, m_cd_w_out, m_ffn_norm_g, m_ffn_w_gate, m_ffn_w_up, m_ffn_w_down, v_ab_norm_g, v_ab_w_in, v_sgu_norm_g, v_sgu_norm_b, v_sgu_w, v_sgu_bias, v_q_norm_g, v_k_norm_g, v_ab_w_out, v_cd_norm_g, v_cd_w_in, v_conv_c_w, v_conv_c_b, v_c_ln_g, v_c_ln_b, v_conv_d_w, v_cd_w_out, v_ffn_norm_g, v_ffn_w_gate, v_ffn_w_up, v_ffn_w_down):
    w = dict(zip(WEIGHT_NAMES, (ab_norm_g, ab_w_in, sgu_norm_g, sgu_norm_b, sgu_w, sgu_bias, q_norm_g, k_norm_g, ab_w_out, cd_norm_g, cd_w_in, conv_c_w, conv_c_b, c_ln_g, c_ln_b, conv_d_w, cd_w_out, ffn_norm_g, ffn_w_gate, ffn_w_up, ffn_w_down)))
    m = dict(zip(WEIGHT_NAMES, (m_ab_norm_g, m_ab_w_in, m_sgu_norm_g, m_sgu_norm_b, m_sgu_w, m_sgu_bias, m_q_norm_g, m_k_norm_g, m_ab_w_out, m_cd_norm_g, m_cd_w_in, m_conv_c_w, m_conv_c_b, m_c_ln_g, m_c_ln_b, m_conv_d_w, m_cd_w_out, m_ffn_norm_g, m_ffn_w_gate, m_ffn_w_up, m_ffn_w_down)))
    v = dict(zip(WEIGHT_NAMES, (v_ab_norm_g, v_ab_w_in, v_sgu_norm_g, v_sgu_norm_b, v_sgu_w, v_sgu_bias, v_q_norm_g, v_k_norm_g, v_ab_w_out, v_cd_norm_g, v_cd_w_in, v_conv_c_w, v_conv_c_b, v_c_ln_g, v_c_ln_b, v_conv_d_w, v_cd_w_out, v_ffn_norm_g, v_ffn_w_gate, v_ffn_w_up, v_ffn_w_down)))
    me = _dev_index(*_my_place())

    small_local = _pack_rows([w["cd_norm_g"], w["conv_c_w"], w["conv_c_b"], w["c_ln_g"], w["c_ln_b"], w["conv_d_w"]], 24)
    o_ab_in, o_small = _gather_first(w["ab_w_in"][0].T.astype(BF16), small_local)
    r_ff = DFF // NDEV
    one = lambda a: (lambda S, j: S[a])
    slot = lambda b: (lambda L, s: L[b].at[s])
    slot2 = lambda b, part: (lambda L, s: L[b].at[part, s])
    gather1 = _exchange_start(
        "gather1_start",
        [w["ab_w_out"][0].astype(BF16), w["ffn_w_gate"][0].T.astype(BF16), w["ffn_w_up"][0].T.astype(BF16),
         w["ffn_w_down"][0].astype(BF16)],
        [(NDEV, D // NDEV, D), (2, NDEV, r_ff, D), (NDEV, r_ff, D)],
        [(one(0), slot(0)), (one(1), slot2(1, 0)), (one(2), slot2(1, 1)), (one(3), slot(2))], dep=o_small)
    gather2 = _exchange_start(
        "gather2_start",
        [w["cd_w_in"][0].T.astype(BF16), w["cd_w_out"][0].astype(BF16), w["ffn_w_gate"][1].T.astype(BF16),
         w["ffn_w_up"][1].T.astype(BF16), w["ffn_w_down"][1].astype(BF16)],
        [(NDEV, CD_IN // NDEV, D), (NDEV, D // NDEV, D), (2, NDEV, r_ff, D), (NDEV, r_ff, D)],
        [(one(0), slot(0)), (one(1), slot(1)), (one(2), slot2(2, 0)), (one(3), slot2(2, 1)), (one(4), slot(3))],
        dep=gather1["token"])

    def fetch(stage, after):
        if stage == 1:
            l_out, l_ffn, l_down = _exchange_wait("gather1_wait", [gather1], after)[0]
            return {"w_ab_out": l_out.reshape(D, D), "wt_ffn_in0": l_ffn.reshape(2 * DFF, D),
                    "w_ffn_down0": l_down.reshape(DFF, D)}
        l_in, l_out, l_ffn, l_down = _exchange_wait("gather2_wait", [gather2], after)[0]
        return {"wt_cd_in": l_in.reshape(CD_IN, D), "w_cd_out": l_out.reshape(D, D),
                "wt_ffn_in1": l_ffn.reshape(2 * DFF, D), "w_ffn_down1": l_down.reshape(DFF, D)}

    scatters = {}

    def on_grad(key, arr):
        if key.startswith("wt_ffn_in"):
            src, land = arr.reshape(2, NDEV, r_ff, D), (2, NDEV, r_ff, D)
            items = [(lambda S, j: S[0].at[0, j], slot2(0, 0)), (lambda S, j: S[0].at[1, j], slot2(0, 1))]
        else:
            rows = arr.shape[0] // NDEV
            src, land = arr.reshape(NDEV, rows, D), (1, NDEV, rows, D)
            items = [(lambda S, j: S[0].at[j], slot2(0, 0))]
        scatters[key] = _exchange_start(f"scatter_{key}_start", [src], [land], items)
        return scatters[key]["token"]

    flat = o_small.reshape(NDEV, 24 * 128)

    def chan(lo, taps):
        return flat[:, lo:lo + taps * SHARD_C].reshape(NDEV, taps, SHARD_C).transpose(1, 0, 2).reshape(taps, 512)

    W = {
        "wt_ab_in": o_ab_in.reshape(AB_IN, D), "dep0": gather2["token"],
        "ab_norm_g": w["ab_norm_g"], "sgu_norm_g": w["sgu_norm_g"], "sgu_norm_b": w["sgu_norm_b"],
        "sgu_w": w["sgu_w"][0], "sgu_bias": w["sgu_bias"][0], "q_norm_g": w["q_norm_g"][0],
        "k_norm_g": w["k_norm_g"][0], "ffn_norm_g": w["ffn_norm_g"],
        "cd_norm_g": flat[:, 0:128].reshape(1, D),
        "conv_c_w32": jnp.pad(chan(128, CONV_C_TAPS), ((0, 1), (0, 0))),
        "conv_c_b": chan(2112, 1), "c_ln_g": chan(2176, 1), "c_ln_b": chan(2240, 1),
        "conv_d_w8": jnp.pad(chan(2304, CONV_D_TAPS), ((0, 8 - CONV_D_TAPS), (0, 0))),
    }

    loss_cols, grad_x, G = _local_step(x[0], loss_target[0], W, fetch, on_grad)
    loss = lax.psum(jnp.sum(loss_cols), ("x", "y", "c"))

    early = ["w_ffn_down1", "wt_ffn_in1", "w_cd_out", "wt_cd_in", "w_ffn_down0", "wt_ffn_in0", "w_ab_out"]
    landed = dict(zip(early, _exchange_wait("scatter_wait_early", [scatters[k] for k in early], grad_x)))
    small_parts = [G["ab_norm_g"], G["sgu_norm_g"], G["sgu_norm_b"], G["sgu_w"], G["sgu_bias"], G["q_norm_g"],
                   G["k_norm_g"], G["cd_norm_g"], G["conv_c_w32"][:CONV_C_TAPS], G["conv_c_b"], G["c_ln_g"],
                   G["c_ln_b"], G["conv_d_w8"][:CONV_D_TAPS], G["ffn_norm_g0"], G["ffn_norm_g1"]]
    sizes = [p.size for p in small_parts]
    red = _allreduce_small(_pack_rows(small_parts, 712)).reshape(-1)
    offs = [0]
    for s in sizes:
        offs.append(offs[-1] + s)
    seg = [red[offs[i]:offs[i + 1]] for i in range(len(sizes))]

    def own_channels(full, taps):
        return lax.dynamic_slice_in_dim(full.reshape(taps, 512), me * SHARD_C, SHARD_C, axis=1)

    g_small = {
        "ab_norm_g": seg[0].reshape(1, 1024), "sgu_norm_g": seg[1].reshape(1, 512), "sgu_norm_b": seg[2].reshape(1, 512),
        "sgu_w": seg[3].reshape(512, 128), "sgu_bias": seg[4].reshape(4, 128), "q_norm_g": seg[5].reshape(3, 64),
        "k_norm_g": seg[6].reshape(3, 64),
        "cd_norm_g": lax.dynamic_slice_in_dim(seg[7].reshape(1, D), me * (D // NDEV), D // NDEV, axis=1),
        "conv_c_w": own_channels(seg[8], CONV_C_TAPS), "conv_c_b": own_channels(seg[9], 1),
        "c_ln_g": own_channels(seg[10], 1), "c_ln_b": own_channels(seg[11], 1),
        "conv_d_w": own_channels(seg[12], CONV_D_TAPS),
        "ffn_norm_g": jnp.concatenate([seg[13].reshape(1, D), seg[14].reshape(1, D)], axis=0),
    }

    grads, deltas, new_m, new_v = {}, {}, {}, {}
    names2d = [n for n, _ in SMALL_2D]
    d_s, m_s, v_s = _adam_small([w[n].reshape(s) for n, s in SMALL_2D], [g_small[n] for n in names2d],
                                [m[n].reshape(s) for n, s in SMALL_2D], [v[n].reshape(s) for n, s in SMALL_2D])
    for i, n in enumerate(names2d):
        shape = w[n].shape
        grads[n], deltas[n] = g_small[n].reshape(shape), d_s[i].reshape(shape)
        new_m[n], new_v[n] = m_s[i].reshape(shape), v_s[i].reshape(shape)

    def put(name, res):
        grads[name], deltas[name], new_m[name], new_v[name] = res

    def adam(name, lands, sel, transposed):
        put(name, _adam_stacked(lands, sel, w[name], m[name], v[name], f"adam_{name}", transposed))

    ffn_in_lands = [landed["wt_ffn_in0"][0], landed["wt_ffn_in1"][0]]
    adam("cd_w_in", landed["wt_cd_in"], 0, True)
    adam("ffn_w_gate", ffn_in_lands, 0, True)
    adam("ffn_w_up", ffn_in_lands, 1, True)
    adam("ab_w_out", landed["w_ab_out"], 0, False)
    adam("cd_w_out", landed["w_cd_out"], 0, False)
    adam("ffn_w_down", [landed["w_ffn_down0"][0], landed["w_ffn_down1"][0]], 0, False)
    last = _exchange_wait("scatter_wait_last", [scatters["wt_ab_in"]], deltas["ffn_w_down"])[0]
    adam("ab_w_in", last, 0, True)

    return (loss, grad_x[None], *[grads[n] for n in WEIGHT_NAMES], *[deltas[n] for n in WEIGHT_NAMES],
            *[new_m[n] for n in WEIGHT_NAMES], *[new_v[n] for n in WEIGHT_NAMES])
```

```python
import functools

import jax
import jax.numpy as jnp
from jax import lax
from jax.experimental import pallas as pl
from jax.experimental.pallas import tpu as pltpu

F32 = jnp.float32
BF16 = jnp.bfloat16

T = 4096
D = 1024
NDEV = 8
EPS = 1e-6
NEG_INF = -1e30
DFF = 2816
AB_IN = 5632
CD_IN = 2560
HEAD = 64
PAIR = 128
NPAIR = 4
NBACK = 128
DIL_RATES = (1, 4, 16)
ROPE_HALF = 8
ROPE_THETA = 500000.0
CONV_C_TAPS = 31
CONV_D_TAPS = 3
HALO = 32
ATTN_BWD_UNROLL = 2

ADAM_LR = 0.001
ADAM_B1 = 0.9
ADAM_B2 = 0.999
ADAM_EPS = 1e-08
ADAM_WD = 0.01
ADAM_STEP = 10
ADAM_C1 = 1.0 / (1.0 - ADAM_B1 ** ADAM_STEP)
ADAM_C2 = 1.0 / (1.0 - ADAM_B2 ** ADAM_STEP)

VMEM_LIMIT_MB = 48
MESH = pl.DeviceIdType.MESH
HBM_SPEC = pl.BlockSpec(memory_space=pl.ANY)


def _cparams(ngrid, vmem_mb=VMEM_LIMIT_MB):
    return pltpu.CompilerParams(dimension_semantics=("arbitrary",) * ngrid,
                                vmem_limit_bytes=vmem_mb * 1024 * 1024)


def _pick(n, options):
    for o in options:
        if n % o == 0:
            return o
    raise ValueError(f"no tile for {n} in {options}")


def _sds(shape, dtype):
    return jax.ShapeDtypeStruct(shape, dtype)


def _sigmoid(x):
    return 1.0 / (1.0 + jnp.exp(-x))


def _gelu(z):
    return 0.5 * z * (1.0 + lax.erf(z * 0.7071067811865476))


def _gelu_grad(z):
    return 0.5 * (1.0 + lax.erf(z * 0.7071067811865476)) + z * jnp.exp(-0.5 * z * z) * 0.3989422804014327


def _mm_nt(a, wt, name, out_dtype=BF16, tm=1024, dep=None):
    M, K = a.shape
    N = wt.shape[0]
    tn = _pick(N, (512, 256))

    def body(a_ref, w_ref, *rest):
        o_ref = rest[-1]
        o_ref[...] = lax.dot_general(a_ref[...], w_ref[...], (((1,), (1,)), ((), ())),
                                     preferred_element_type=F32).astype(o_ref.dtype)

    in_specs = [pl.BlockSpec((tm, K), lambda i, j: (i, 0)), pl.BlockSpec((tn, K), lambda i, j: (j, 0))]
    args = [a, wt]
    if dep is not None:
        in_specs.append(HBM_SPEC)
        args.append(dep)
    return pl.pallas_call(
        body, name=name, grid=(M // tm, N // tn), in_specs=in_specs,
        out_specs=pl.BlockSpec((tm, tn), lambda i, j: (i, j)),
        out_shape=_sds((M, N), out_dtype), compiler_params=_cparams(2))(*args)


EPI_ROWS = 256


def _mm_nn(a, w, name, mode="plain", resid=None, gain=None, tgt=None, x=None, dres=None, out_dtype=F32, dep=None):
    parts = a.shape[0] if a.ndim == 3 else 1
    M, Kp = a.shape[-2], a.shape[-1]
    N = w.shape[1]
    tk = _pick(Kp, (1408, 1280, 1024, 512))
    kper = Kp // tk
    nk = parts * kper
    tm = 512 if mode == "rms_bwd" else 1024
    n_in = 2 + sum(t is not None for t in (resid, gain, tgt, x, dres, dep))

    def body(*refs):
        a_ref, w_ref = refs[0], refs[1]
        named = dict(zip([n for n, t in (("resid", resid), ("gain", gain), ("tgt", tgt), ("x", x), ("dres", dres))
                          if t is not None], refs[2:]))
        outs, acc = refs[n_in:-1], refs[-1]
        i, k = pl.program_id(0), pl.program_id(1)

        @pl.when(k == 0)
        def _():
            acc[...] = jnp.zeros_like(acc)

        acc[...] += jnp.dot(a_ref[...], w_ref[...], preferred_element_type=F32)

        if mode in ("loss", "rms_bwd"):
            @pl.when((k == 0) & (i == 0))
            def _():
                outs[2][...] = jnp.zeros_like(outs[2])

        @pl.when(k == nk - 1)
        def _():
            for r0 in range(0, tm, EPI_ROWS):
                rows = slice(r0, r0 + EPI_ROWS)
                v = acc[rows, :]
                if resid is not None:
                    v = v + named["resid"][rows, :]
                if mode == "plain":
                    outs[0][rows, :] = v.astype(outs[0].dtype)
                elif mode == "rms":
                    outs[0][rows, :] = v
                    r = lax.rsqrt(jnp.mean(v * v, axis=-1, keepdims=True) + EPS)
                    outs[1][rows, :] = (v * r * named["gain"][...]).astype(BF16)
                elif mode == "loss":
                    d = v - named["tgt"][rows, :]
                    outs[2][...] += jnp.sum(d * d, axis=0, keepdims=True) * (0.5 / N)
                    dy = d * (1.0 / N)
                    outs[0][rows, :] = dy
                    outs[1][rows, :] = dy.astype(BF16)
                else:
                    xf = named["x"][rows, :]
                    r = lax.rsqrt(jnp.mean(xf * xf, axis=-1, keepdims=True) + EPS)
                    xhat = xf * r
                    outs[2][...] += jnp.sum(v * xhat, axis=0, keepdims=True)
                    dxh = v * named["gain"][...]
                    tot = named["dres"][rows, :] + r * (dxh - xhat * jnp.mean(dxh * xhat, axis=-1, keepdims=True))
                    outs[0][rows, :] = tot
                    outs[1][rows, :] = tot.astype(BF16)

    row = pl.BlockSpec((tm, N), lambda i, k: (i, 0))
    vec = pl.BlockSpec((1, N), lambda i, k: (0, 0))
    if a.ndim == 3:
        a_spec = pl.BlockSpec((None, tm, tk), lambda i, k: (k // kper, i, k % kper))
    else:
        a_spec = pl.BlockSpec((tm, tk), lambda i, k: (i, k))
    in_specs = [a_spec, pl.BlockSpec((tk, N), lambda i, k: (k, 0))]
    args = [a, w]
    for t, spec in ((resid, row), (gain, vec), (tgt, row), (x, row), (dres, row), (dep, HBM_SPEC)):
        if t is not None:
            in_specs.append(spec)
            args.append(t)
    if mode == "plain":
        out_specs, out_shape = [row], [_sds((M, N), out_dtype)]
    elif mode == "rms":
        out_specs, out_shape = [row, row], [_sds((M, N), F32), _sds((M, N), BF16)]
    else:
        out_specs, out_shape = [row, row, vec], [_sds((M, N), F32), _sds((M, N), BF16), _sds((1, N), F32)]
    res = pl.pallas_call(
        body, name=name, grid=(M // tm, nk), in_specs=in_specs, out_specs=out_specs, out_shape=out_shape,
        scratch_shapes=[pltpu.VMEM((tm, N), F32)], compiler_params=_cparams(2))(*args)
    return res[0] if mode == "plain" else res


def _mm_tn(a, b, name, out_dtype=BF16, tt=512):
    parts = a.shape[0] if a.ndim == 3 else 1
    Tt, Mp = a.shape[-2], a.shape[-1]
    N = b.shape[1]
    tn = _pick(Mp, (1408, 1280, 1024, 512))
    jper = Mp // tn
    nt = Tt // tt

    def body(a_ref, b_ref, o_ref, acc):
        t = pl.program_id(1)

        @pl.when(t == 0)
        def _():
            acc[...] = jnp.zeros_like(acc)

        acc[...] += lax.dot_general(a_ref[...], b_ref[...], (((0,), (0,)), ((), ())),
                                    preferred_element_type=F32)

        @pl.when(t == nt - 1)
        def _():
            o_ref[...] = acc[...].astype(o_ref.dtype)

    if a.ndim == 3:
        a_spec = pl.BlockSpec((None, tt, tn), lambda j, t: (j // jper, t, j % jper))
    else:
        a_spec = pl.BlockSpec((tt, tn), lambda j, t: (t, j))
    return pl.pallas_call(
        body, name=name, grid=(parts * jper, nt),
        in_specs=[a_spec, pl.BlockSpec((tt, N), lambda j, t: (t, 0))],
        out_specs=pl.BlockSpec((tn, N), lambda j, t: (j, 0)),
        out_shape=_sds((parts * Mp, N), out_dtype), scratch_shapes=[pltpu.VMEM((tn, N), F32)],
        compiler_params=_cparams(2))(a, b)


def _ffn_in(h, wt_in, name, tm=1024, tn=256):
    nj = DFF // tn

    def body(h_ref, wg_ref, wu_ref, p_ref, act_ref):
        nt = (((1,), (1,)), ((), ()))
        g = lax.dot_general(h_ref[...], wg_ref[...], nt, preferred_element_type=F32)
        u = lax.dot_general(h_ref[...], wu_ref[...], nt, preferred_element_type=F32)
        p_ref[0] = g.astype(BF16)
        p_ref[1] = u.astype(BF16)
        act_ref[...] = (g * _sigmoid(g) * u).astype(BF16)

    return pl.pallas_call(
        body, name=name, grid=(T // tm, nj),
        in_specs=[pl.BlockSpec((tm, D), lambda i, j: (i, 0)), pl.BlockSpec((tn, D), lambda i, j: (j, 0)),
                  pl.BlockSpec((tn, D), lambda i, j: (j + nj, 0))],
        out_specs=[pl.BlockSpec((2, tm, tn), lambda i, j: (0, i, j)), pl.BlockSpec((tm, tn), lambda i, j: (i, j))],
        out_shape=[_sds((2, T, DFF), BF16), _sds((T, DFF), BF16)], compiler_params=_cparams(2))(h, wt_in, wt_in)


def _ffn_dact(dyb, w_down, p3, name, tm=1024, tn=256, dep=None):
    def body(dy_ref, w_ref, p_ref, *rest):
        o_ref = rest[-1]
        da = lax.dot_general(dy_ref[...], w_ref[...], (((1,), (1,)), ((), ())), preferred_element_type=F32)
        g = p_ref[0].astype(F32)
        u = p_ref[1].astype(F32)
        sg = _sigmoid(g)
        o_ref[0] = (da * u * sg * (1.0 + g * (1.0 - sg))).astype(BF16)
        o_ref[1] = (da * g * sg).astype(BF16)

    pspec = pl.BlockSpec((2, tm, tn), lambda i, j: (0, i, j))
    in_specs = [pl.BlockSpec((tm, D), lambda i, j: (i, 0)), pl.BlockSpec((tn, D), lambda i, j: (j, 0)), pspec]
    args = [dyb, w_down, p3]
    if dep is not None:
        in_specs.append(HBM_SPEC)
        args.append(dep)
    return pl.pallas_call(
        body, name=name, grid=(T // tm, DFF // tn), in_specs=in_specs, out_specs=pspec,
        out_shape=_sds((2, T, DFF), BF16), compiler_params=_cparams(2))(*args)


def _rms_fwd(x, g, name, tm=512):
    def body(x_ref, g_ref, h_ref):
        xf = x_ref[...]
        r = lax.rsqrt(jnp.mean(xf * xf, axis=-1, keepdims=True) + EPS)
        h_ref[...] = (xf * r * g_ref[...]).astype(BF16)

    return pl.pallas_call(
        body, name=name, grid=(T // tm,),
        in_specs=[pl.BlockSpec((tm, D), lambda i: (i, 0)), pl.BlockSpec((1, D), lambda i: (0, 0))],
        out_specs=pl.BlockSpec((tm, D), lambda i: (i, 0)),
        out_shape=_sds((T, D), BF16), compiler_params=_cparams(1))(x, g)


def _tril_mask():
    r = lax.broadcasted_iota(jnp.int32, (128, 128), 0)
    c = lax.broadcasted_iota(jnp.int32, (128, 128), 1)
    return r >= c


def _mix_a_fwd(pab, sgu_g, sgu_b, sgu_w, sgu_bias3, tm=512):
    def body(zu_ref, zv_ref, g_ref, b_ref, w_ref, bias_ref, o_ref):
        u = _gelu(zu_ref[...].astype(F32))
        v = _gelu(zv_ref[...].astype(F32))
        mu = jnp.mean(v, axis=-1, keepdims=True)
        vc = v - mu
        rstd = lax.rsqrt(jnp.mean(vc * vc, axis=-1, keepdims=True) + EPS)
        vn = (vc * rstd * g_ref[...] + b_ref[...]).astype(BF16)
        tri = _tril_mask()
        for gi in range(4):
            wg = jnp.where(tri, w_ref[gi], 0.0).astype(BF16)
            bg = bias_ref[gi]
            for c in range(tm // 128):
                rs, cs = slice(c * 128, (c + 1) * 128), slice(gi * 128, (gi + 1) * 128)
                mixed = jnp.dot(wg, vn[rs, cs], preferred_element_type=F32) + bg
                o_ref[rs, cs] = (u[rs, cs] * mixed).astype(BF16)

    half = pl.BlockSpec((tm, 512), lambda i: (i, 0))
    return pl.pallas_call(
        body, name="mix_a_fwd", grid=(T // tm,),
        in_specs=[half, pl.BlockSpec((tm, 512), lambda i: (i, 1)),
                  pl.BlockSpec((1, 512), lambda i: (0, 0)), pl.BlockSpec((1, 512), lambda i: (0, 0)),
                  pl.BlockSpec((4, 128, 128), lambda i: (0, 0, 0)), pl.BlockSpec((4, 128, 1), lambda i: (0, 0, 0))],
        out_specs=half, out_shape=_sds((T, D), BF16), compiler_params=_cparams(1),
    )(pab, pab, sgu_g, sgu_b, sgu_w, sgu_bias3)


def _rope_tables():
    pos = jnp.arange(T, dtype=F32)
    inv_freq = ROPE_THETA ** (-jnp.arange(ROPE_HALF, dtype=F32) * 2.0 / (2 * ROPE_HALF))
    ang = pos[:, None] * inv_freq[None, :]
    cos, sin = jnp.cos(ang), jnp.sin(ang)
    z8 = jnp.zeros((T, ROPE_HALF), F32)
    rest = HEAD - 2 * ROPE_HALF
    c64 = jnp.concatenate([cos, cos, jnp.ones((T, rest), F32)], axis=1)
    s1 = jnp.concatenate([z8, sin, jnp.zeros((T, rest), F32)], axis=1)
    s2 = jnp.concatenate([-sin, z8, jnp.zeros((T, rest), F32)], axis=1)
    return jnp.tile(c64, (1, 2)), jnp.tile(s1, (1, 2)), jnp.tile(s2, (1, 2))


def _lo_mask(shape):
    return lax.broadcasted_iota(jnp.int32, shape, 1) < HEAD


def _seg_mean(x, lo):
    s_all = jnp.sum(x, axis=-1, keepdims=True)
    s_lo = jnp.sum(jnp.where(lo, x, 0.0), axis=-1, keepdims=True)
    return jnp.where(lo, s_lo, s_all - s_lo) * (1.0 / HEAD)


def _rope(n, c, s1, s2):
    return n * c + pltpu.roll(n, ROPE_HALF, 1) * s1 + pltpu.roll(n, PAIR - ROPE_HALF, 1) * s2


def _rope_t(dy, c, s1, s2):
    return dy * c - pltpu.roll(dy, PAIR - ROPE_HALF, 1) * s2 - pltpu.roll(dy, ROPE_HALF, 1) * s1


def _prep_fwd(pab, qg, kg, tabs, tm=512):
    def body(p_ref, qg_ref, kg_ref, c_ref, s1_ref, s2_ref, *outs):
        lo = _lo_mask((tm, PAIR))
        c, s1, s2 = c_ref[...], s1_ref[...], s2_ref[...]
        for g in range(3):
            qn_ref, kn_ref, v_ref = outs[3 * g:3 * g + 3]
            for p in range(NPAIR):
                for which, gains, dst in ((0, qg_ref, qn_ref), (1, kg_ref, kn_ref)):
                    col = (2 + 3 * which + g) * 512 + p * PAIR
                    xr = p_ref[:, col:col + PAIR].astype(F32)
                    rinv = lax.rsqrt(_seg_mean(xr * xr, lo) + EPS)
                    dst[p] = _rope(xr * rinv * gains[g:g + 1, :], c, s1, s2)
                col = (8 + g) * 512 + p * PAIR
                v_ref[p] = p_ref[:, col:col + PAIR].astype(F32)

    pm = pl.BlockSpec((NPAIR, tm, PAIR), lambda i: (0, i, 0))
    tab = pl.BlockSpec((tm, PAIR), lambda i: (i, 0))
    gain = pl.BlockSpec((3, PAIR), lambda i: (0, 0))
    return pl.pallas_call(
        body, name="prep_fwd", grid=(T // tm,),
        in_specs=[pl.BlockSpec((tm, AB_IN), lambda i: (i, 0)), gain, gain, tab, tab, tab],
        out_specs=[pm] * 9, out_shape=[_sds((NPAIR, T, PAIR), F32)] * 9,
        compiler_params=_cparams(1))(pab, qg, kg, *tabs)


def _res_index(it, rate):
    window = NBACK * rate
    b = it // rate
    rho = it % rate
    start = b * window + rho
    startp = jnp.maximum(start - window, rho)
    kmin = jnp.where(b > 0, 0, NBACK)
    return start, startp, kmin


def _rows(start, rate):
    if rate == 1:
        return pl.ds(pl.multiple_of(start, NBACK), NBACK)
    return pl.ds(start, NBACK, stride=rate)


def _band():
    qi = lax.broadcasted_iota(jnp.int32, (NBACK, 2 * NBACK), 0)
    kj = lax.broadcasted_iota(jnp.int32, (NBACK, 2 * NBACK), 1)
    dist = qi + NBACK - kj
    return (dist >= 0) & (dist <= NBACK), kj


def _attn_fwd(qn, kn, v, rate, name):
    def body(q_ref, k_ref, v_ref, o_ref, l_ref):
        lo = _lo_mask((NBACK, PAIR))
        band, kj = _band()

        def step(it, carry):
            start, startp, kmin = _res_index(it, rate)
            q = q_ref[_rows(start, rate), :]
            kcat = jnp.concatenate([k_ref[_rows(startp, rate), :], k_ref[_rows(start, rate), :]], axis=0).astype(BF16)
            vcat = jnp.concatenate([v_ref[_rows(startp, rate), :], v_ref[_rows(start, rate), :]], axis=0).astype(BF16)
            ok = band & (kj >= kmin)
            q2 = jnp.concatenate([jnp.where(lo, q, 0.0), jnp.where(lo, 0.0, q)], axis=0).astype(BF16)
            s = lax.dot_general(q2, kcat, (((1,), (1,)), ((), ())), preferred_element_type=F32) * (HEAD ** -0.5)
            s = jnp.where(jnp.concatenate([ok, ok], axis=0), s, NEG_INF)
            m = jnp.max(s, axis=-1, keepdims=True)
            pr = jnp.exp(s - m)
            l = jnp.sum(pr, axis=-1, keepdims=True)
            o2 = jnp.dot(pr.astype(BF16), vcat, preferred_element_type=F32) / l
            ls = m + jnp.log(l)
            o_ref[_rows(start, rate), :] = jnp.where(lo, o2[0:NBACK], o2[NBACK:])
            l_ref[_rows(start, rate), :] = jnp.where(lo, ls[0:NBACK], ls[NBACK:])
            return carry

        lax.fori_loop(0, T // NBACK, step, 0, unroll=4)

    pm = pl.BlockSpec((None, T, PAIR), lambda p: (p, 0, 0))
    return pl.pallas_call(
        body, name=name, grid=(NPAIR,), in_specs=[pm, pm, pm], out_specs=[pm, pm],
        out_shape=[_sds((NPAIR, T, PAIR), F32)] * 2, compiler_params=_cparams(1))(qn, kn, v)


def _merge_fwd(cat_ab, outs, lses, tm=512):
    def body(cat_in, o0, o1, o2, l0, l1, l2, cat_ref, lse_ref):
        del cat_in
        for p in range(NPAIR):
            a0, a1, a2 = l0[p], l1[p], l2[p]
            m = jnp.maximum(jnp.maximum(a0, a1), a2)
            w0, w1, w2 = jnp.exp(a0 - m), jnp.exp(a1 - m), jnp.exp(a2 - m)
            s = w0 + w1 + w2
            b = (w0 * o0[p] + w1 * o1[p] + w2 * o2[p]) / s
            cat_ref[:, p * PAIR:(p + 1) * PAIR] = b.astype(BF16)
            lse_ref[p] = m + jnp.log(s)

    pm = pl.BlockSpec((NPAIR, tm, PAIR), lambda i: (0, i, 0))
    return pl.pallas_call(
        body, name="merge_fwd", grid=(T // tm,),
        in_specs=[pl.BlockSpec(memory_space=pl.ANY)] + [pm] * 6,
        out_specs=[pl.BlockSpec((tm, 512), lambda i: (i, 1)), pm],
        out_shape=[_sds((T, D), BF16), _sds((NPAIR, T, PAIR), F32)],
        input_output_aliases={0: 0}, compiler_params=_cparams(1))(cat_ab, *outs, *lses)


def _b_pre_bwd(dcat, cat, tm=512):
    def body(db_ref, b_ref, dbp_ref, e_ref):
        lo = _lo_mask((tm, PAIR))
        for p in range(NPAIR):
            db = db_ref[:, p * PAIR:(p + 1) * PAIR].astype(F32)
            b = b_ref[:, p * PAIR:(p + 1) * PAIR].astype(F32)
            dbp_ref[p] = db
            e_ref[p] = _seg_mean(db * b, lo) * float(HEAD)

    pm = pl.BlockSpec((NPAIR, tm, PAIR), lambda i: (0, i, 0))
    right = pl.BlockSpec((tm, 512), lambda i: (i, 1))
    return pl.pallas_call(
        body, name="b_pre_bwd", grid=(T // tm,), in_specs=[right, right], out_specs=[pm, pm],
        out_shape=[_sds((NPAIR, T, PAIR), F32)] * 2, compiler_params=_cparams(1))(dcat, cat)


def _attn_bwd(qn, kn, v, dbp, e, lse, rate, name):
    def body(q_ref, k_ref, v_ref, db_ref, e_ref, lse_ref, dq_ref, dk_ref, dv_ref):
        lo = _lo_mask((NBACK, PAIR))
        band, kj = _band()
        scale = HEAD ** -0.5
        nt = (((1,), (1,)), ((), ()))
        tn = (((0,), (0,)), ((), ()))
        window = NBACK * rate
        nblk = T // window

        def one(it, carry):
            dk_carry, dv_carry = carry
            rho = it // nblk
            b = it % nblk
            start = b * window + rho
            rq = _rows(start, rate)
            rp = _rows(jnp.maximum(start - window, rho), rate)
            kmin = jnp.where(b > 0, 0, NBACK)
            q = q_ref[rq, :]
            db = db_ref[rq, :]
            ev = e_ref[rq, :]
            ls = lse_ref[rq, :]
            kcat = jnp.concatenate([k_ref[rp, :], k_ref[rq, :]], axis=0).astype(BF16)
            vcat = jnp.concatenate([v_ref[rp, :], v_ref[rq, :]], axis=0).astype(BF16)
            ok = band & (kj >= kmin)
            ok2 = jnp.concatenate([ok, ok], axis=0)
            q2 = jnp.concatenate([jnp.where(lo, q, 0.0), jnp.where(lo, 0.0, q)], axis=0).astype(BF16)
            db2 = jnp.concatenate([jnp.where(lo, db, 0.0), jnp.where(lo, 0.0, db)], axis=0).astype(BF16)
            ls2 = jnp.concatenate([ls[:, 0:1], ls[:, HEAD:HEAD + 1]], axis=0)
            ev2 = jnp.concatenate([ev[:, 0:1], ev[:, HEAD:HEAD + 1]], axis=0)
            s = lax.dot_general(q2, kcat, nt, preferred_element_type=F32) * scale
            s = jnp.where(ok2, s, NEG_INF)
            pt = jnp.exp(s - ls2)
            dp = lax.dot_general(db2, vcat, nt, preferred_element_type=F32)
            ds = (pt * (dp - ev2)).astype(BF16)
            dq2 = jnp.dot(ds, kcat, preferred_element_type=F32) * scale
            dkc = lax.dot_general(ds, q2, tn, preferred_element_type=F32) * scale
            dvc = lax.dot_general(pt.astype(BF16), db2, tn, preferred_element_type=F32)
            dq_ref[rq, :] = jnp.where(lo, dq2[0:NBACK], dq2[NBACK:])
            dk_ref[rp, :] = dk_carry + dkc[0:NBACK]
            dk_ref[rq, :] = dkc[NBACK:]
            dv_ref[rp, :] = dv_carry + dvc[0:NBACK]
            dv_ref[rq, :] = dvc[NBACK:]
            return dkc[NBACK:], dvc[NBACK:]

        def step(i, carry):
            for u in range(ATTN_BWD_UNROLL):
                carry = one(i * ATTN_BWD_UNROLL + u, carry)
            return carry

        zero = jnp.zeros((NBACK, PAIR), F32)
        lax.fori_loop(0, T // NBACK // ATTN_BWD_UNROLL, step, (zero, zero))

    pm = pl.BlockSpec((None, T, PAIR), lambda p: (p, 0, 0))
    return pl.pallas_call(
        body, name=name, grid=(NPAIR,), in_specs=[pm] * 6, out_specs=[pm] * 3,
        out_shape=[_sds((NPAIR, T, PAIR), F32)] * 3, compiler_params=_cparams(1, 56))(qn, kn, v, dbp, e, lse)


def _ab_in_bwd(pab, dcat, sgu_g, sgu_b, sgu_w, sgu_bias3, qg, kg, tabs, dqkv, tm=256):
    def body(p_ref, dcat_ref, g_ref, b_ref, w_ref, bias_ref, qg_ref, kg_ref, c_ref, s1_ref, s2_ref, *rest):
        dq_refs = rest[0:9]
        o_ref, dwm_ref, dbias_ref, dsg_ref, dsb_ref, dgain_ref = rest[9:]
        i = pl.program_id(0)

        @pl.when(i == 0)
        def _():
            dwm_ref[...] = jnp.zeros_like(dwm_ref)
            dbias_ref[...] = jnp.zeros_like(dbias_ref)
            dsg_ref[...] = jnp.zeros_like(dsg_ref)
            dsb_ref[...] = jnp.zeros_like(dsb_ref)
            dgain_ref[...] = jnp.zeros_like(dgain_ref)

        zu = p_ref[:, 0:512].astype(F32)
        zv = p_ref[:, 512:1024].astype(F32)
        u = _gelu(zu)
        v = _gelu(zv)
        mu = jnp.mean(v, axis=-1, keepdims=True)
        vc = v - mu
        rstd = lax.rsqrt(jnp.mean(vc * vc, axis=-1, keepdims=True) + EPS)
        xhat = vc * rstd
        vn = (xhat * g_ref[...] + b_ref[...]).astype(BF16)
        da = dcat_ref[...].astype(F32)
        tri = _tril_mask()
        du_parts = [[None] * 4 for _ in range(tm // 128)]
        dvn_parts = [[None] * 4 for _ in range(tm // 128)]
        for gi in range(4):
            wg = jnp.where(tri, w_ref[gi], 0.0).astype(BF16)
            bg = bias_ref[gi]
            for c in range(tm // 128):
                rs, cs = slice(c * 128, (c + 1) * 128), slice(gi * 128, (gi + 1) * 128)
                vblk = vn[rs, cs]
                mixed = jnp.dot(wg, vblk, preferred_element_type=F32) + bg
                dab = da[rs, cs]
                du_parts[c][gi] = dab * mixed
                dmixed = dab * u[rs, cs]
                dmb = dmixed.astype(BF16)
                dvn_parts[c][gi] = lax.dot_general(wg, dmb, (((0,), (0,)), ((), ())), preferred_element_type=F32)
                dwm = lax.dot_general(dmb, vblk, (((1,), (1,)), ((), ())), preferred_element_type=F32)
                dwm_ref[gi] += jnp.where(tri, dwm, 0.0)
                dbias_ref[gi] += dmixed
        du = jnp.concatenate([jnp.concatenate(r, axis=1) for r in du_parts], axis=0)
        dvn = jnp.concatenate([jnp.concatenate(r, axis=1) for r in dvn_parts], axis=0)
        dsg_ref[...] += jnp.sum(dvn * xhat, axis=0, keepdims=True)
        dsb_ref[...] += jnp.sum(dvn, axis=0, keepdims=True)
        dxh = dvn * g_ref[...]
        dv = rstd * (dxh - jnp.mean(dxh, axis=-1, keepdims=True)
                     - xhat * jnp.mean(dxh * xhat, axis=-1, keepdims=True))
        o_ref[:, 0:512] = (du * _gelu_grad(zu)).astype(BF16)
        o_ref[:, 512:1024] = (dv * _gelu_grad(zv)).astype(BF16)

        lo = _lo_mask((tm, PAIR))
        c, s1, s2 = c_ref[...], s1_ref[...], s2_ref[...]
        for g in range(3):
            dq_ref, dk_ref, dv_ref = dq_refs[3 * g:3 * g + 3]
            for p in range(NPAIR):
                for which, gains, src in ((0, qg_ref, dq_ref), (1, kg_ref, dk_ref)):
                    col = (2 + 3 * which + g) * 512 + p * PAIR
                    xr = p_ref[:, col:col + PAIR].astype(F32)
                    rinv = lax.rsqrt(_seg_mean(xr * xr, lo) + EPS)
                    xh = xr * rinv
                    dn = _rope_t(src[p], c, s1, s2)
                    row = 2 * g + which
                    dgain_ref[row:row + 1, :] += jnp.sum(dn * xh, axis=0, keepdims=True)
                    dxh2 = dn * gains[g:g + 1, :]
                    dx = rinv * (dxh2 - xh * _seg_mean(dxh2 * xh, lo))
                    o_ref[:, col:col + PAIR] = dx.astype(BF16)
                col = (8 + g) * 512 + p * PAIR
                o_ref[:, col:col + PAIR] = dv_ref[p].astype(BF16)

    pm = pl.BlockSpec((NPAIR, tm, PAIR), lambda i: (0, i, 0))
    tab = pl.BlockSpec((tm, PAIR), lambda i: (i, 0))
    gain = pl.BlockSpec((3, PAIR), lambda i: (0, 0))
    vec = pl.BlockSpec((1, 512), lambda i: (0, 0))
    full = pl.BlockSpec((tm, AB_IN), lambda i: (i, 0))
    w4 = pl.BlockSpec((4, 128, 128), lambda i: (0, 0, 0))
    return pl.pallas_call(
        body, name="ab_in_bwd", grid=(T // tm,),
        in_specs=[full, pl.BlockSpec((tm, 512), lambda i: (i, 0)), vec, vec, w4,
                  pl.BlockSpec((4, 128, 1), lambda i: (0, 0, 0)), gain, gain, tab, tab, tab] + [pm] * 9,
        out_specs=[full, w4, w4, vec, vec, pl.BlockSpec((8, PAIR), lambda i: (0, 0))],
        out_shape=[_sds((T, AB_IN), BF16), _sds((4, 128, 128), F32), _sds((4, 128, 128), F32),
                   _sds((1, 512), F32), _sds((1, 512), F32), _sds((8, PAIR), F32)],
        compiler_params=_cparams(1))(pab, dcat, sgu_g, sgu_b, sgu_w, sgu_bias3, qg, kg, *tabs, *dqkv)


def _ln_stats(x):
    mu = jnp.mean(x, axis=-1, keepdims=True)
    xc = x - mu
    rstd = lax.rsqrt(jnp.mean(xc * xc, axis=-1, keepdims=True) + EPS)
    return xc * rstd, rstd


def _cd_fwd(pcd, cw, cb, lg, lb, dw, tm=512):
    per = tm // HALO

    def body(p_ref, h_ref, cw_ref, cb_ref, lg_ref, lb_ref, dw_ref, cat_ref, c0_ref, c1_ref, dd_ref, y_ref, buf, buf2):
        i = pl.program_id(0)
        live = jnp.where(i > 0, 1.0, 0.0)
        a = p_ref[:, 0:512].astype(F32)
        gt = p_ref[:, 512:1024].astype(F32)
        gb = p_ref[:, 1024:1536].astype(F32)
        gc = p_ref[:, 1536:2048].astype(F32)
        hv = p_ref[:, 2048:2560].astype(F32)
        c0 = a * _sigmoid(gt)
        dd = gc * hv
        buf[0:HALO, :] = h_ref[:, 0:512].astype(F32) * _sigmoid(h_ref[:, 512:1024].astype(F32)) * live
        buf[HALO:, :] = c0
        buf2[0:HALO, :] = h_ref[:, 1536:2048].astype(F32) * h_ref[:, 2048:2560].astype(F32) * live
        buf2[HALO:, :] = dd
        acc = jnp.broadcast_to(cb_ref[...], (tm, 512))
        for j in range(CONV_C_TAPS):
            acc = acc + cw_ref[j:j + 1, :] * buf[pl.ds(HALO - (CONV_C_TAPS - 1) + j, tm), :]
        xhat, _ = _ln_stats(acc)
        c2 = xhat * lg_ref[...] + lb_ref[...]
        y = jnp.zeros((tm, 512), F32)
        for j in range(CONV_D_TAPS):
            y = y + dw_ref[j:j + 1, :] * buf2[pl.ds(HALO - (CONV_D_TAPS - 1) + j, tm), :]
        cat_ref[:, 0:512] = (c2 * _sigmoid(c2)).astype(BF16)
        cat_ref[:, 512:1024] = (gb * y).astype(BF16)
        c0_ref[...] = c0.astype(BF16)
        c1_ref[...] = acc
        dd_ref[...] = dd.astype(BF16)
        y_ref[...] = y.astype(BF16)

    half = pl.BlockSpec((tm, 512), lambda i: (i, 0))
    vec = pl.BlockSpec((1, 512), lambda i: (0, 0))
    return pl.pallas_call(
        body, name="cd_fwd", grid=(T // tm,),
        in_specs=[pl.BlockSpec((tm, CD_IN), lambda i: (i, 0)),
                  pl.BlockSpec((HALO, CD_IN), lambda i: (jnp.maximum(i * per - 1, 0), 0)),
                  pl.BlockSpec((32, 512), lambda i: (0, 0)), vec, vec, vec, pl.BlockSpec((8, 512), lambda i: (0, 0))],
        out_specs=[pl.BlockSpec((tm, D), lambda i: (i, 0)), half, half, half, half],
        out_shape=[_sds((T, D), BF16), _sds((T, 512), BF16), _sds((T, 512), F32), _sds((T, 512), BF16),
                   _sds((T, 512), BF16)],
        scratch_shapes=[pltpu.VMEM((HALO + tm, 512), F32), pltpu.VMEM((HALO + tm, 512), F32)],
        compiler_params=_cparams(1))(pcd, pcd, cw, cb, lg, lb, dw)


def _cd_bwd_pw(dcat, c1, pcd, y, lg, lb, tm=512):
    def body(dcat_ref, c1_ref, gb_ref, y_ref, lg_ref, lb_ref, dc1_ref, dy3_ref, dgb_ref, dlg_ref, dlb_ref, dcb_ref):
        i = pl.program_id(0)

        @pl.when(i == 0)
        def _():
            dlg_ref[...] = jnp.zeros_like(dlg_ref)
            dlb_ref[...] = jnp.zeros_like(dlb_ref)
            dcb_ref[...] = jnp.zeros_like(dcb_ref)

        dc = dcat_ref[:, 0:512].astype(F32)
        ddo = dcat_ref[:, 512:1024].astype(F32)
        xhat, rstd = _ln_stats(c1_ref[...])
        c2 = xhat * lg_ref[...] + lb_ref[...]
        sg = _sigmoid(c2)
        dc2 = dc * sg * (1.0 + c2 * (1.0 - sg))
        dlg_ref[...] += jnp.sum(dc2 * xhat, axis=0, keepdims=True)
        dlb_ref[...] += jnp.sum(dc2, axis=0, keepdims=True)
        dxh = dc2 * lg_ref[...]
        dc1 = rstd * (dxh - jnp.mean(dxh, axis=-1, keepdims=True)
                      - xhat * jnp.mean(dxh * xhat, axis=-1, keepdims=True))
        dcb_ref[...] += jnp.sum(dc1, axis=0, keepdims=True)
        dc1_ref[...] = dc1
        dgb_ref[...] = (ddo * y_ref[...].astype(F32)).astype(BF16)
        dy3_ref[...] = ddo * gb_ref[...].astype(F32)

    half = pl.BlockSpec((tm, 512), lambda i: (i, 0))
    vec = pl.BlockSpec((1, 512), lambda i: (0, 0))
    return pl.pallas_call(
        body, name="cd_bwd_pw", grid=(T // tm,),
        in_specs=[pl.BlockSpec((tm, D), lambda i: (i, 0)), half, pl.BlockSpec((tm, 512), lambda i: (i, 2)), half,
                  vec, vec],
        out_specs=[half, half, half, vec, vec, vec],
        out_shape=[_sds((T, 512), F32), _sds((T, 512), F32), _sds((T, 512), BF16),
                   _sds((1, 512), F32), _sds((1, 512), F32), _sds((1, 512), F32)],
        compiler_params=_cparams(1))(dcat, c1, pcd, y, lg, lb)


def _cd_bwd_conv(pcd, dc1, dy3, c0, dd, dgb, cw, dw, tm=512):
    per = tm // HALO
    nblk = T // tm
    last32 = T // HALO - 1

    def body(p_ref, dc1_ref, dc1n_ref, dy3_ref, dy3n_ref, c0_ref, c0p_ref, dd_ref, ddp_ref, dgb_ref, cw_ref, dw_ref,
             o_ref, dcw_ref, ddw_ref, dbuf, cbuf, d3buf, ddbuf):
        i = pl.program_id(0)
        has_prev = jnp.where(i > 0, 1.0, 0.0)
        has_next = jnp.where(i < nblk - 1, 1.0, 0.0)

        @pl.when(i == 0)
        def _():
            dcw_ref[...] = jnp.zeros_like(dcw_ref)
            ddw_ref[...] = jnp.zeros_like(ddw_ref)

        dc1 = dc1_ref[...]
        dy3 = dy3_ref[...]
        dbuf[0:tm, :] = dc1
        dbuf[tm:, :] = dc1n_ref[...] * has_next
        d3buf[0:tm, :] = dy3
        d3buf[tm:, :] = dy3n_ref[...] * has_next
        cbuf[0:HALO, :] = c0p_ref[...].astype(F32) * has_prev
        cbuf[HALO:, :] = c0_ref[...].astype(F32)
        ddbuf[0:HALO, :] = ddp_ref[...].astype(F32) * has_prev
        ddbuf[HALO:, :] = dd_ref[...].astype(F32)

        dc0 = jnp.zeros((tm, 512), F32)
        for j in range(CONV_C_TAPS):
            dc0 = dc0 + cw_ref[j:j + 1, :] * dbuf[pl.ds(CONV_C_TAPS - 1 - j, tm), :]
            dcw_ref[j:j + 1, :] += jnp.sum(dc1 * cbuf[pl.ds(HALO - (CONV_C_TAPS - 1) + j, tm), :], axis=0, keepdims=True)
        ddd = jnp.zeros((tm, 512), F32)
        for j in range(CONV_D_TAPS):
            ddd = ddd + dw_ref[j:j + 1, :] * d3buf[pl.ds(CONV_D_TAPS - 1 - j, tm), :]
            ddw_ref[j:j + 1, :] += jnp.sum(dy3 * ddbuf[pl.ds(HALO - (CONV_D_TAPS - 1) + j, tm), :], axis=0, keepdims=True)

        a = p_ref[:, 0:512].astype(F32)
        gt = p_ref[:, 512:1024].astype(F32)
        gc = p_ref[:, 1536:2048].astype(F32)
        hv = p_ref[:, 2048:2560].astype(F32)
        sg = _sigmoid(gt)
        o_ref[:, 0:512] = (dc0 * sg).astype(BF16)
        o_ref[:, 512:1024] = (dc0 * a * sg * (1.0 - sg)).astype(BF16)
        o_ref[:, 1024:1536] = dgb_ref[...]
        o_ref[:, 1536:2048] = (ddd * hv).astype(BF16)
        o_ref[:, 2048:2560] = (ddd * gc).astype(BF16)

    half = pl.BlockSpec((tm, 512), lambda i: (i, 0))
    nxt = pl.BlockSpec((HALO, 512), lambda i: (jnp.minimum((i + 1) * per, last32), 0))
    prv = pl.BlockSpec((HALO, 512), lambda i: (jnp.maximum(i * per - 1, 0), 0))
    full = pl.BlockSpec((tm, CD_IN), lambda i: (i, 0))
    return pl.pallas_call(
        body, name="cd_bwd_conv", grid=(nblk,),
        in_specs=[full, half, nxt, half, nxt, half, prv, half, prv, half,
                  pl.BlockSpec((32, 512), lambda i: (0, 0)), pl.BlockSpec((8, 512), lambda i: (0, 0))],
        out_specs=[full, pl.BlockSpec((32, 512), lambda i: (0, 0)), pl.BlockSpec((8, 512), lambda i: (0, 0))],
        out_shape=[_sds((T, CD_IN), BF16), _sds((32, 512), F32), _sds((8, 512), F32)],
        scratch_shapes=[pltpu.VMEM((tm + HALO, 512), F32), pltpu.VMEM((HALO + tm, 512), F32),
                        pltpu.VMEM((tm + HALO, 512), F32), pltpu.VMEM((HALO + tm, 512), F32)],
        compiler_params=_cparams(1))(pcd, dc1, dc1, dy3, dy3, c0, c0, dd, dd, dgb, cw, dw)


def _local_step(x, tgt, W, fetch=None, on_grad=None):
    W = dict(W)
    if fetch is None:
        fetch = lambda stage, after: {}
    if on_grad is None:
        on_grad = lambda key, arr: None
    tabs = _rope_tables()
    qg = jnp.tile(W["q_norm_g"], (1, 2))
    kg = jnp.tile(W["k_norm_g"], (1, 2))
    bias3 = W["sgu_bias"].reshape(4, 128, 1)
    G = {}

    h0 = _rms_fwd(x, W["ab_norm_g"], "rms_fwd_ab")
    pab = _mm_nt(h0, W["wt_ab_in"], "mm_ab_in", dep=W.get("dep0"))
    cat_ab = _mix_a_fwd(pab, W["sgu_norm_g"], W["sgu_norm_b"], W["sgu_w"], bias3)
    qkv = _prep_fwd(pab, qg, kg, tabs)
    outs, lses = [], []
    for g, rate in enumerate(DIL_RATES):
        o, l = _attn_fwd(qkv[3 * g], qkv[3 * g + 1], qkv[3 * g + 2], rate, f"attn_fwd_{g}")
        outs.append(o)
        lses.append(l)
        if g == 0:
            W.update(fetch(0, o))
    cat_ab, lse = _merge_fwd(cat_ab, outs, lses)
    W.update(fetch(1, lse))
    x1, h1 = _mm_nn(cat_ab, W["w_ab_out"], "mm_ab_out", mode="rms", resid=x, gain=W["ffn_norm_g"][0:1])
    pf0, act0 = _ffn_in(h1, W["wt_ffn_in0"], "ffn_in0")
    x2, h2 = _mm_nn(act0, W["w_ffn_down0"], "mm_ffn_down0", mode="rms", resid=x1, gain=W["cd_norm_g"])
    W.update(fetch(2, h2))
    pcd = _mm_nt(h2, W["wt_cd_in"], "mm_cd_in")
    cat_cd, c0, c1, dd, yv = _cd_fwd(pcd, W["conv_c_w32"], W["conv_c_b"], W["c_ln_g"], W["c_ln_b"], W["conv_d_w8"])
    x3, h3 = _mm_nn(cat_cd, W["w_cd_out"], "mm_cd_out", mode="rms", resid=x2, gain=W["ffn_norm_g"][1:2])
    pf1, act1 = _ffn_in(h3, W["wt_ffn_in1"], "ffn_in1")
    dy, dyb, loss_cols = _mm_nn(act1, W["w_ffn_down1"], "mm_ffn_down1", mode="loss", resid=x3, tgt=tgt)

    def ffn_bwd(xin, h, pf, act, dres, dresb, layer):
        G[f"w_ffn_down{layer}"] = _mm_tn(act, dresb, f"mm_g_ffn_down{layer}")
        dep = on_grad(f"w_ffn_down{layer}", G[f"w_ffn_down{layer}"])
        dpf = _ffn_dact(dresb, W[f"w_ffn_down{layer}"], pf, f"ffn_dact{layer}", dep=dep)
        G[f"wt_ffn_in{layer}"] = _mm_tn(dpf, h, f"mm_g_ffn_in{layer}")
        dep = on_grad(f"wt_ffn_in{layer}", G[f"wt_ffn_in{layer}"])
        dx, dxb, G[f"ffn_norm_g{layer}"] = _mm_nn(
            dpf, W[f"wt_ffn_in{layer}"], f"mm_d_h_ffn{layer}", mode="rms_bwd", x=xin,
            gain=W["ffn_norm_g"][layer:layer + 1], dres=dres, dep=dep)
        return dx, dxb

    dx3, dx3b = ffn_bwd(x3, h3, pf1, act1, dy, dyb, 1)

    G["w_cd_out"] = _mm_tn(cat_cd, dx3b, "mm_g_cd_out")
    dep = on_grad("w_cd_out", G["w_cd_out"])
    dcat_cd = _mm_nt(dx3b, W["w_cd_out"], "mm_d_cat_cd", dep=dep)
    dc1, dy3, dgb, G["c_ln_g"], G["c_ln_b"], G["conv_c_b"] = _cd_bwd_pw(dcat_cd, c1, pcd, yv, W["c_ln_g"], W["c_ln_b"])
    dpcd, G["conv_c_w32"], G["conv_d_w8"] = _cd_bwd_conv(pcd, dc1, dy3, c0, dd, dgb, W["conv_c_w32"], W["conv_d_w8"])
    G["wt_cd_in"] = _mm_tn(dpcd, h2, "mm_g_cd_in")
    dep = on_grad("wt_cd_in", G["wt_cd_in"])
    dx2, dx2b, G["cd_norm_g"] = _mm_nn(dpcd, W["wt_cd_in"], "mm_d_h_cd", mode="rms_bwd", x=x2, gain=W["cd_norm_g"],
                                       dres=dx3, dep=dep)

    dx1, dx1b = ffn_bwd(x1, h1, pf0, act0, dx2, dx2b, 0)

    G["w_ab_out"] = _mm_tn(cat_ab, dx1b, "mm_g_ab_out")
    dep = on_grad("w_ab_out", G["w_ab_out"])
    dcat_ab = _mm_nt(dx1b, W["w_ab_out"], "mm_d_cat_ab", dep=dep)
    dbp, e = _b_pre_bwd(dcat_ab, cat_ab)
    dqkv = []
    for g, rate in enumerate(DIL_RATES):
        dqkv += _attn_bwd(qkv[3 * g], qkv[3 * g + 1], qkv[3 * g + 2], dbp, e, lse, rate, f"attn_bwd_{g}")
    dpab, G["sgu_w"], dbias_part, G["sgu_norm_g"], G["sgu_norm_b"], dgain = _ab_in_bwd(
        pab, dcat_ab, W["sgu_norm_g"], W["sgu_norm_b"], W["sgu_w"], bias3, qg, kg, tabs, dqkv)
    G["sgu_bias"] = jnp.sum(dbias_part, axis=-1)
    dgain = dgain[0:6, 0:HEAD] + dgain[0:6, HEAD:PAIR]
    G["q_norm_g"] = dgain[0::2]
    G["k_norm_g"] = dgain[1::2]
    G["wt_ab_in"] = _mm_tn(dpab, h0, "mm_g_ab_in")
    dep = on_grad("wt_ab_in", G["wt_ab_in"])
    grad_x, _, G["ab_norm_g"] = _mm_nn(dpab, W["wt_ab_in"], "mm_d_h_ab", mode="rms_bwd", x=x, gain=W["ab_norm_g"],
                                       dres=dx1, dep=dep)
    return loss_cols, grad_x, G


def _my_place():
    return lax.axis_index("x"), lax.axis_index("y"), lax.axis_index("c")


def _dev_index(px, py, pc):
    return 4 * px + 2 * py + pc


def _flip(place, k):
    x, y, c = place
    return (1 - x if k & 4 else x, 1 - y if k & 2 else y, 1 - c if k & 1 else c)


def _gather_first(ab_in_t, small):
    out_shape = [_sds((NDEV,) + ab_in_t.shape, BF16), _sds((NDEV,) + small.shape, F32)]
    n_items = 2

    def body(ab_in_r, small_r, o_ab_in, o_small, send_sems, recv_sems, loc_sems):
        x, y, c = _my_place()
        me = (x, y, c)
        sib = (x, y, 1 - c)
        chips = [(1 - x, y), (x, 1 - y), (1 - x, 1 - y)]
        items = [(ab_in_r, lambda d: o_ab_in.at[d]), (small_r, lambda d: o_small.at[d])]

        def rcopy(it, k, src, dst, to):
            return pltpu.make_async_remote_copy(src_ref=src, dst_ref=dst, send_sem=send_sems.at[it, k],
                                                recv_sem=recv_sems.at[it, k], device_id=to, device_id_type=MESH)

        started = []
        locals_ = []
        for it, (src, dst) in enumerate(items):
            mine = dst(_dev_index(*me))
            lc = pltpu.make_async_copy(src, mine, loc_sems.at[it])
            lc.start()
            locals_.append(lc)
            first = [rcopy(it, 0, src, mine, sib)]
            first += [rcopy(it, 1 + j, src, mine, (*chip, c)) for j, chip in enumerate(chips)]
            for cp in first:
                cp.start()
            started += first
        for it, (src, dst) in enumerate(items):
            for j, chip in enumerate(chips):
                blk = dst(_dev_index(*chip, c))
                rcopy(it, 1 + j, blk, blk, me).wait_recv()
                fwd = rcopy(it, 4 + j, blk, blk, sib)
                fwd.start()
                started.append(fwd)
        for it, (src, dst) in enumerate(items):
            blk = dst(_dev_index(x, y, 1 - c))
            rcopy(it, 0, blk, blk, me).wait_recv()
            for j, chip in enumerate(chips):
                blk = dst(_dev_index(*chip, 1 - c))
                rcopy(it, 4 + j, blk, blk, me).wait_recv()
        for cp in started:
            cp.wait_send()
        for lc in locals_:
            lc.wait()

    return pl.pallas_call(
        body, name="gather_first", in_specs=[HBM_SPEC] * 2, out_specs=[HBM_SPEC] * 2, out_shape=out_shape,
        scratch_shapes=[pltpu.SemaphoreType.DMA((n_items, 7)), pltpu.SemaphoreType.DMA((n_items, 7)),
                        pltpu.SemaphoreType.DMA((n_items,))],
    )(ab_in_t, small)


HBM_ONLY = pl.BlockSpec(memory_space=pltpu.HBM)
SEM_SPEC = pl.BlockSpec(memory_space=pltpu.SEMAPHORE)
IN_FLIGHT = pltpu.CompilerParams(has_side_effects=pltpu.SideEffectType.DATAFLOW_SIDE_EFFECTING)


def _in_hbm(a):
    return pltpu.with_memory_space_constraint(a, pltpu.HBM)


def _exchange_start(name, srcs, land_shapes, items, dep=None, land_dtype=BF16):
    ns, nl, ni = len(srcs), len(land_shapes), len(items)
    lands = [lax.empty(s, land_dtype) for s in land_shapes]

    def body(*refs):
        S, L = refs[0:ns], refs[ns:ns + nl]
        first_out = ns + nl + (0 if dep is None else 1)
        send_sems, recv_sems, token = refs[first_out], refs[first_out + 1], refs[-1]
        me = _my_place()
        mi = _dev_index(*me)
        for i, (src, dst) in enumerate(items):
            for k in range(1, NDEV):
                peer = _flip(me, k)
                pltpu.make_async_remote_copy(
                    src_ref=src(S, _dev_index(*peer)), dst_ref=dst(L, mi), send_sem=send_sems.at[7 * i + k - 1],
                    recv_sem=recv_sems.at[7 * i + k - 1], device_id=peer, device_id_type=MESH).start()
        token[...] = jnp.zeros_like(token)

    thru = [pltpu.HBM(a.shape, a.dtype) for a in srcs] + [pltpu.HBM(s, land_dtype) for s in land_shapes]
    args = [_in_hbm(a) for a in srcs] + [_in_hbm(a) for a in lands]
    in_specs = [HBM_ONLY] * (ns + nl)
    if dep is not None:
        args.append(dep)
        in_specs.append(HBM_SPEC)
    outs = pl.pallas_call(
        body, name=name, in_specs=in_specs,
        out_shape=(pltpu.SemaphoreType.DMA((7 * ni,)), pltpu.SemaphoreType.DMA((7 * ni,)), *thru, _sds((8, 128), F32)),
        out_specs=(SEM_SPEC, SEM_SPEC, *[HBM_ONLY] * (ns + nl), pl.BlockSpec(memory_space=pltpu.VMEM)),
        input_output_aliases={j: 2 + j for j in range(ns + nl)}, compiler_params=IN_FLIGHT)(*args)
    return dict(send=outs[0], recv=outs[1], srcs=list(outs[2:2 + ns]), lands=list(outs[2 + ns:2 + ns + nl]),
                token=outs[-1], items=items)


def _exchange_wait(name, states, after):
    counts = [(len(st["srcs"]), len(st["lands"]), len(st["items"])) for st in states]
    n_items = sum(c[2] for c in counts)
    n_arrays = sum(c[0] + c[1] for c in counts)

    def body(*refs):
        loc_sems = refs[-1]
        me = _my_place()
        mi = _dev_index(*me)
        pos = 0
        sem_pos = n_arrays
        it = 0
        locals_ = []
        for st, (ns, nl, ni) in zip(states, counts):
            S, L = refs[pos:pos + ns], refs[pos + ns:pos + ns + nl]
            send_sems, recv_sems = refs[sem_pos], refs[sem_pos + 1]
            pos += ns + nl
            sem_pos += 2
            for i, (src, dst) in enumerate(st["items"]):
                lc = pltpu.make_async_copy(src(S, mi), dst(L, mi), loc_sems.at[it])
                lc.start()
                locals_.append(lc)
                it += 1
                for k in range(1, NDEV):
                    cp = pltpu.make_async_remote_copy(
                        src_ref=src(S, mi), dst_ref=dst(L, mi), send_sem=send_sems.at[7 * i + k - 1],
                        recv_sem=recv_sems.at[7 * i + k - 1], device_id=me, device_id_type=MESH)
                    cp.wait_send()
                    cp.wait_recv()
        for lc in locals_:
            lc.wait()

    arrays, sems = [], []
    for st in states:
        arrays += st["srcs"] + st["lands"]
        sems += [st["send"], st["recv"]]
    outs = pl.pallas_call(
        body, name=name, in_specs=[HBM_ONLY] * n_arrays + [SEM_SPEC] * len(sems) + [HBM_SPEC],
        out_shape=tuple(pltpu.HBM(a.shape, a.dtype) for a in arrays), out_specs=tuple([HBM_ONLY] * n_arrays),
        input_output_aliases={j: j for j in range(n_arrays)},
        scratch_shapes=[pltpu.SemaphoreType.DMA((n_items,))], compiler_params=IN_FLIGHT)(*arrays, *sems, after)
    lands, pos = [], 0
    for ns, nl, _ in counts:
        lands.append(list(outs[pos + ns:pos + ns + nl]))
        pos += ns + nl
    return lands


def _sum_slots(land):
    def body(l_ref, o_ref):
        acc = l_ref[0]
        for d in range(1, NDEV):
            acc = acc + l_ref[d]
        o_ref[...] = acc

    vm = pl.BlockSpec(memory_space=pltpu.VMEM)
    return pl.pallas_call(body, name="sum_small", out_shape=_sds(land.shape[1:], F32), in_specs=[vm], out_specs=vm)(land)


def _adam_math(w, g, m, v):
    m2 = ADAM_B1 * m + (1.0 - ADAM_B1) * g
    v2 = ADAM_B2 * v + (1.0 - ADAM_B2) * (g * g)
    delta = -ADAM_LR * ((m2 * ADAM_C1) / (jnp.sqrt(v2 * ADAM_C2) + ADAM_EPS) + ADAM_WD * w)
    return delta, m2, v2


def _adam_layer(land, sel, w, m, v, layer, name, transposed, prev=None):
    R = land.shape[2]
    tc = 256 if transposed else 512

    def body(l_ref, w_ref, m_ref, v_ref, *rest):
        g_out, d_out, m_out, v_out = rest[-4:]
        g = l_ref[0].astype(F32)
        for d in range(1, NDEV):
            g = g + l_ref[d].astype(F32)
        if transposed:
            g = g.T
        delta, m2, v2 = _adam_math(w_ref[...], g, m_ref[...], v_ref[...])
        g_out[...] = g
        d_out[...] = delta
        m_out[...] = m2
        v_out[...] = v2

    if transposed:
        wspec = pl.BlockSpec((None, tc, R), lambda i: (layer, i, 0))
    else:
        wspec = pl.BlockSpec((None, R, tc), lambda i: (layer, 0, i))
    in_specs = [pl.BlockSpec((None, NDEV, R, tc), lambda i: (sel, 0, 0, i)), wspec, wspec, wspec]
    args = [land, w, m, v]
    aliases = {}
    if prev is not None:
        in_specs += [HBM_SPEC] * 4
        args += list(prev)
        aliases = {4 + j: j for j in range(4)}
    return pl.pallas_call(
        body, name=name, grid=(D // tc,), in_specs=in_specs, out_specs=[wspec] * 4,
        out_shape=[_sds(w.shape, F32)] * 4, input_output_aliases=aliases, compiler_params=_cparams(1))(*args)


def _adam_stacked(lands, sel, w, m, v, name, transposed):
    res = None
    for layer, land in enumerate(lands):
        res = _adam_layer(land, sel, w, m, v, layer, f"{name}{layer}", transposed, prev=res)
    return res


def _adam_small(ws, gs, ms, vs):
    n = len(ws)

    def body(*refs):
        w_r, g_r, m_r, v_r = refs[0:n], refs[n:2 * n], refs[2 * n:3 * n], refs[3 * n:4 * n]
        d_o, m_o, v_o = refs[4 * n:5 * n], refs[5 * n:6 * n], refs[6 * n:7 * n]
        for i in range(n):
            delta, m2, v2 = _adam_math(w_r[i][...], g_r[i][...], m_r[i][...], v_r[i][...])
            d_o[i][...] = delta
            m_o[i][...] = m2
            v_o[i][...] = v2

    vm = pl.BlockSpec(memory_space=pltpu.VMEM)
    shapes = [_sds(w.shape, F32) for w in ws]
    outs = pl.pallas_call(body, name="adam_small", in_specs=[vm] * (4 * n), out_specs=[vm] * (3 * n),
                          out_shape=shapes * 3)(*ws, *gs, *ms, *vs)
    return outs[0:n], outs[n:2 * n], outs[2 * n:3 * n]


WEIGHT_NAMES = ("ab_norm_g", "ab_w_in", "sgu_norm_g", "sgu_norm_b", "sgu_w", "sgu_bias", "q_norm_g", "k_norm_g",
                "ab_w_out", "cd_norm_g", "cd_w_in", "conv_c_w", "conv_c_b", "c_ln_g", "c_ln_b", "conv_d_w",
                "cd_w_out", "ffn_norm_g", "ffn_w_gate", "ffn_w_up", "ffn_w_down")
SMALL_2D = (("ab_norm_g", (1, 1024)), ("sgu_norm_g", (1, 512)), ("sgu_norm_b", (1, 512)), ("sgu_w", (512, 128)),
            ("sgu_bias", (4, 128)), ("q_norm_g", (3, 64)), ("k_norm_g", (3, 64)), ("cd_norm_g", (1, 128)),
            ("conv_c_w", (31, 64)), ("conv_c_b", (1, 64)), ("c_ln_g", (1, 64)), ("c_ln_b", (1, 64)),
            ("conv_d_w", (3, 64)), ("ffn_norm_g", (2, 1024)))
SHARD_C = 64


def _pack_rows(parts, rows):
    flat = jnp.concatenate([p.reshape(-1) for p in parts])
    return jnp.pad(flat, (0, rows * 128 - flat.shape[0])).reshape(rows, 128)


def kernel(x, ab_norm_g, ab_w_in, sgu_norm_g, sgu_norm_b, sgu_w, sgu_bias, q_norm_g, k_norm_g, ab_w_out, cd_norm_g, cd_w_in, conv_c_w, conv_c_b, c_ln_g, c_ln_b, conv_d_w, cd_w_out, ffn_norm_g, ffn_w_gate, ffn_w_up, ffn_w_down, loss_target, m_ab_norm_g, m_ab_w_in, m_sgu_norm_g, m_sgu_norm_b, m_sgu_w, m_sgu_bias, m_q_norm_g, m_k_norm_g, m_ab_w_out, m_cd_norm_g, m_cd_w_in, m_conv_c_w, m_conv_c_b, m_c_ln_g, m_c_ln_b, m_conv_d_w, m_cd_w_out, m_ffn_norm_g, m_ffn_w_gate, m_ffn_w_up, m_ffn_w_down, v_ab_norm_g, v_ab_w_in, v_sgu_norm_g, v_sgu_norm_b, v_sgu_w, v_sgu_bias, v_q_norm_g, v_k_norm_g, v_ab_w_out, v_cd_norm_g, v_cd_w_in, v_conv_c_w, v_conv_c_b, v_c_ln_g, v_c_ln_b, v_conv_d_w, v_cd_w_out, v_ffn_norm_g, v_ffn_w_gate, v_ffn_w_up, v_ffn_w_down):
    w = dict(zip(WEIGHT_NAMES, (ab_norm_g, ab_w_in, sgu_norm_g, sgu_norm_b, sgu_w, sgu_bias, q_norm_g, k_norm_g, ab_w_out, cd_norm_g, cd_w_in, conv_c_w, conv_c_b, c_ln_g, c_ln_b, conv_d_w, cd_w_out, ffn_norm_g, ffn_w_gate, ffn_w_up, ffn_w_down)))
    m = dict(zip(WEIGHT_NAMES, (m_ab_norm_g, m_ab_w_in, m_sgu_norm_g, m_sgu_norm_b, m_sgu_w, m_sgu_bias, m_q_norm_g, m_k_norm_g, m_ab_w_out, m_cd_norm_g, m_cd_w_in, m_conv_c_w, m_conv_c_b, m_c_ln_g, m_c_ln_b, m_conv_d_w, m_cd_w_out, m_ffn_norm_g, m_ffn_w_gate, m_ffn_w_up, m_ffn_w_down)))
    v = dict(zip(WEIGHT_NAMES, (v_ab_norm_g, v_ab_w_in, v_sgu_norm_g, v_sgu_norm_b, v_sgu_w, v_sgu_bias, v_q_norm_g, v_k_norm_g, v_ab_w_out, v_cd_norm_g, v_cd_w_in, v_conv_c_w, v_conv_c_b, v_c_ln_g, v_c_ln_b, v_conv_d_w, v_cd_w_out, v_ffn_norm_g, v_ffn_w_gate, v_ffn_w_up, v_ffn_w_down)))
    me = _dev_index(*_my_place())

    small_local = _pack_rows([w["cd_norm_g"], w["conv_c_w"], w["conv_c_b"], w["c_ln_g"], w["c_ln_b"], w["conv_d_w"]], 24)
    o_ab_in, o_small = _gather_first(w["ab_w_in"][0].T.astype(BF16), small_local)
    r_ff = DFF // NDEV
    one = lambda a: (lambda S, j: S[a])
    slot = lambda b: (lambda L, s: L[b].at[s])
    slot2 = lambda b, part: (lambda L, s: L[b].at[part, s])
    gather1 = _exchange_start(
        "gather1_start",
        [w["ab_w_out"][0].astype(BF16), w["ffn_w_gate"][0].T.astype(BF16), w["ffn_w_up"][0].T.astype(BF16),
         w["ffn_w_down"][0].astype(BF16)],
        [(NDEV, D // NDEV, D), (2, NDEV, r_ff, D), (NDEV, r_ff, D)],
        [(one(0), slot(0)), (one(1), slot2(1, 0)), (one(2), slot2(1, 1)), (one(3), slot(2))], dep=o_small)
    gathers = {}

    def fetch(stage, after):
        if stage == 0:
            gathers[2] = _exchange_start(
                "gather2_start",
                [w["cd_w_in"][0].T.astype(BF16), w["cd_w_out"][0].astype(BF16), w["ffn_w_gate"][1].T.astype(BF16),
                 w["ffn_w_up"][1].T.astype(BF16), w["ffn_w_down"][1].astype(BF16)],
                [(NDEV, CD_IN // NDEV, D), (NDEV, D // NDEV, D), (2, NDEV, r_ff, D), (NDEV, r_ff, D)],
                [(one(0), slot(0)), (one(1), slot(1)), (one(2), slot2(2, 0)), (one(3), slot2(2, 1)),
                 (one(4), slot(3))], dep=after)
            return {}
        if stage == 1:
            l_out, l_ffn, l_down = _exchange_wait("gather1_wait", [gather1], after)[0]
            return {"w_ab_out": l_out.reshape(D, D), "wt_ffn_in0": l_ffn.reshape(2 * DFF, D),
                    "w_ffn_down0": l_down.reshape(DFF, D)}
        l_in, l_out, l_ffn, l_down = _exchange_wait("gather2_wait", [gathers[2]], after)[0]
        return {"wt_cd_in": l_in.reshape(CD_IN, D), "w_cd_out": l_out.reshape(D, D),
                "wt_ffn_in1": l_ffn.reshape(2 * DFF, D), "w_ffn_down1": l_down.reshape(DFF, D)}

    scatters = {}

    def on_grad(key, arr):
        if key.startswith("wt_ffn_in"):
            src, land = arr.reshape(2, NDEV, r_ff, D), (2, NDEV, r_ff, D)
            items = [(lambda S, j: S[0].at[0, j], slot2(0, 0)), (lambda S, j: S[0].at[1, j], slot2(0, 1))]
        else:
            rows = arr.shape[0] // NDEV
            src, land = arr.reshape(NDEV, rows, D), (1, NDEV, rows, D)
            items = [(lambda S, j: S[0].at[j], slot2(0, 0))]
        scatters[key] = _exchange_start(f"scatter_{key}_start", [src], [land], items)
        return scatters[key]["token"]

    flat = o_small.reshape(NDEV, 24 * 128)

    def chan(lo, taps):
        return flat[:, lo:lo + taps * SHARD_C].reshape(NDEV, taps, SHARD_C).transpose(1, 0, 2).reshape(taps, 512)

    W = {
        "wt_ab_in": o_ab_in.reshape(AB_IN, D), "dep0": gather1["token"],
        "ab_norm_g": w["ab_norm_g"], "sgu_norm_g": w["sgu_norm_g"], "sgu_norm_b": w["sgu_norm_b"],
        "sgu_w": w["sgu_w"][0], "sgu_bias": w["sgu_bias"][0], "q_norm_g": w["q_norm_g"][0],
        "k_norm_g": w["k_norm_g"][0], "ffn_norm_g": w["ffn_norm_g"],
        "cd_norm_g": flat[:, 0:128].reshape(1, D),
        "conv_c_w32": jnp.pad(chan(128, CONV_C_TAPS), ((0, 1), (0, 0))),
        "conv_c_b": chan(2112, 1), "c_ln_g": chan(2176, 1), "c_ln_b": chan(2240, 1),
        "conv_d_w8": jnp.pad(chan(2304, CONV_D_TAPS), ((0, 8 - CONV_D_TAPS), (0, 0))),
    }

    loss_cols, grad_x, G = _local_step(x[0], loss_target[0], W, fetch, on_grad)
    loss = lax.psum(jnp.sum(loss_cols), ("x", "y", "c"))

    small_parts = [G["ab_norm_g"], G["sgu_norm_g"], G["sgu_norm_b"], G["sgu_w"], G["sgu_bias"], G["q_norm_g"],
                   G["k_norm_g"], G["cd_norm_g"], G["conv_c_w32"][:CONV_C_TAPS], G["conv_c_b"], G["c_ln_g"],
                   G["c_ln_b"], G["conv_d_w8"][:CONV_D_TAPS], G["ffn_norm_g0"], G["ffn_norm_g1"]]
    sizes = [p.size for p in small_parts]
    small_rows = 712
    small = _exchange_start("small_start", [_pack_rows(small_parts, small_rows)], [(NDEV, small_rows, 128)],
                            [(one(0), slot(0))], land_dtype=F32)
    early = ["w_ffn_down1", "wt_ffn_in1", "w_cd_out", "wt_cd_in", "w_ffn_down0", "wt_ffn_in0", "w_ab_out"]
    landed = dict(zip(early, _exchange_wait("scatter_wait_early", [scatters[k] for k in early], small["token"])))

    grads, deltas, new_m, new_v = {}, {}, {}, {}

    def put(name, res):
        grads[name], deltas[name], new_m[name], new_v[name] = res

    def adam(name, lands, sel, transposed):
        put(name, _adam_stacked(lands, sel, w[name], m[name], v[name], f"adam_{name}", transposed))

    ffn_in_lands = [landed["wt_ffn_in0"][0], landed["wt_ffn_in1"][0]]
    adam("cd_w_in", landed["wt_cd_in"], 0, True)
    adam("ffn_w_gate", ffn_in_lands, 0, True)
    adam("ffn_w_up", ffn_in_lands, 1, True)
    adam("ab_w_out", landed["w_ab_out"], 0, False)
    adam("cd_w_out", landed["w_cd_out"], 0, False)
    adam("ffn_w_down", [landed["w_ffn_down0"][0], landed["w_ffn_down1"][0]], 0, False)

    small_land = _exchange_wait("small_wait", [small], deltas["ffn_w_down"])[0][0]
    red = _sum_slots(small_land).reshape(-1)
    offs = [0]
    for s in sizes:
        offs.append(offs[-1] + s)
    seg = [red[offs[i]:offs[i + 1]] for i in range(len(sizes))]

    def own_channels(full, taps):
        return lax.dynamic_slice_in_dim(full.reshape(taps, 512), me * SHARD_C, SHARD_C, axis=1)

    g_small = {
        "ab_norm_g": seg[0].reshape(1, 1024), "sgu_norm_g": seg[1].reshape(1, 512), "sgu_norm_b": seg[2].reshape(1, 512),
        "sgu_w": seg[3].reshape(512, 128), "sgu_bias": seg[4].reshape(4, 128), "q_norm_g": seg[5].reshape(3, 64),
        "k_norm_g": seg[6].reshape(3, 64),
        "cd_norm_g": lax.dynamic_slice_in_dim(seg[7].reshape(1, D), me * (D // NDEV), D // NDEV, axis=1),
        "conv_c_w": own_channels(seg[8], CONV_C_TAPS), "conv_c_b": own_channels(seg[9], 1),
        "c_ln_g": own_channels(seg[10], 1), "c_ln_b": own_channels(seg[11], 1),
        "conv_d_w": own_channels(seg[12], CONV_D_TAPS),
        "ffn_norm_g": jnp.concatenate([seg[13].reshape(1, D), seg[14].reshape(1, D)], axis=0),
    }

    names2d = [n for n, _ in SMALL_2D]
    d_s, m_s, v_s = _adam_small([w[n].reshape(s) for n, s in SMALL_2D], [g_small[n] for n in names2d],
                                [m[n].reshape(s) for n, s in SMALL_2D], [v[n].reshape(s) for n, s in SMALL_2D])
    for i, n in enumerate(names2d):
        shape = w[n].shape
        grads[n], deltas[n] = g_small[n].reshape(shape), d_s[i].reshape(shape)
        new_m[n], new_v[n] = m_s[i].reshape(shape), v_s[i].reshape(shape)

    last = _exchange_wait("scatter_wait_last", [scatters["wt_ab_in"]], d_s[0])[0]
    adam("ab_w_in", last, 0, True)

    return (loss, grad_x[None], *[grads[n] for n in WEIGHT_NAMES], *[deltas[n] for n in WEIGHT_NAMES],
            *[new_m[n] for n in WEIGHT_NAMES], *[new_v[n] for n in WEIGHT_NAMES])
```

```python
import functools

import jax
import jax.numpy as jnp
from jax import lax
from jax.experimental import pallas as pl
from jax.experimental.pallas import tpu as pltpu

F32 = jnp.float32
BF16 = jnp.bfloat16

T = 4096
D = 1024
NDEV = 8
EPS = 1e-6
NEG_INF = -1e30
DFF = 2816
AB_IN = 5632
CD_IN = 2560
HEAD = 64
PAIR = 128
NPAIR = 4
NBACK = 128
DIL_RATES = (1, 4, 16)
ROPE_HALF = 8
ROPE_THETA = 500000.0
CONV_C_TAPS = 31
CONV_D_TAPS = 3
HALO = 32
ATTN_BWD_UNROLL = 2

ADAM_LR = 0.001
ADAM_B1 = 0.9
ADAM_B2 = 0.999
ADAM_EPS = 1e-08
ADAM_WD = 0.01
ADAM_STEP = 10
ADAM_C1 = 1.0 / (1.0 - ADAM_B1 ** ADAM_STEP)
ADAM_C2 = 1.0 / (1.0 - ADAM_B2 ** ADAM_STEP)

VMEM_LIMIT_MB = 48
MESH = pl.DeviceIdType.MESH
HBM_SPEC = pl.BlockSpec(memory_space=pl.ANY)


def _cparams(ngrid, vmem_mb=VMEM_LIMIT_MB):
    return pltpu.CompilerParams(dimension_semantics=("arbitrary",) * ngrid,
                                vmem_limit_bytes=vmem_mb * 1024 * 1024)


def _pick(n, options):
    for o in options:
        if n % o == 0:
            return o
    raise ValueError(f"no tile for {n} in {options}")


def _sds(shape, dtype):
    return jax.ShapeDtypeStruct(shape, dtype)


def _sigmoid(x):
    return 1.0 / (1.0 + jnp.exp(-x))


def _gelu(z):
    return 0.5 * z * (1.0 + lax.erf(z * 0.7071067811865476))


def _gelu_grad(z):
    return 0.5 * (1.0 + lax.erf(z * 0.7071067811865476)) + z * jnp.exp(-0.5 * z * z) * 0.3989422804014327


def _mm_nt(a, wt, name, out_dtype=BF16, tm=1024, dep=None):
    M, K = a.shape
    N = wt.shape[0]
    tn = _pick(N, (512, 256))

    def body(a_ref, w_ref, *rest):
        o_ref = rest[-1]
        o_ref[...] = lax.dot_general(a_ref[...], w_ref[...], (((1,), (1,)), ((), ())),
                                     preferred_element_type=F32).astype(o_ref.dtype)

    in_specs = [pl.BlockSpec((tm, K), lambda i, j: (i, 0)), pl.BlockSpec((tn, K), lambda i, j: (j, 0))]
    args = [a, wt]
    if dep is not None:
        in_specs.append(HBM_SPEC)
        args.append(dep)
    return pl.pallas_call(
        body, name=name, grid=(M // tm, N // tn), in_specs=in_specs,
        out_specs=pl.BlockSpec((tm, tn), lambda i, j: (i, j)),
        out_shape=_sds((M, N), out_dtype), compiler_params=_cparams(2))(*args)


EPI_ROWS = 256


def _mm_nn(a, w, name, mode="plain", resid=None, gain=None, tgt=None, x=None, dres=None, out_dtype=F32, dep=None):
    parts = a.shape[0] if a.ndim == 3 else 1
    M, Kp = a.shape[-2], a.shape[-1]
    N = w.shape[1]
    tk = _pick(Kp, (1408, 1280, 1024, 512))
    kper = Kp // tk
    nk = parts * kper
    tm = 512 if mode == "rms_bwd" else 1024
    n_in = 2 + sum(t is not None for t in (resid, gain, tgt, x, dres, dep))

    def body(*refs):
        a_ref, w_ref = refs[0], refs[1]
        named = dict(zip([n for n, t in (("resid", resid), ("gain", gain), ("tgt", tgt), ("x", x), ("dres", dres))
                          if t is not None], refs[2:]))
        outs, acc = refs[n_in:-1], refs[-1]
        i, k = pl.program_id(0), pl.program_id(1)

        @pl.when(k == 0)
        def _():
            acc[...] = jnp.zeros_like(acc)

        acc[...] += jnp.dot(a_ref[...], w_ref[...], preferred_element_type=F32)

        if mode in ("loss", "rms_bwd"):
            @pl.when((k == 0) & (i == 0))
            def _():
                outs[2][...] = jnp.zeros_like(outs[2])

        @pl.when(k == nk - 1)
        def _():
            for r0 in range(0, tm, EPI_ROWS):
                rows = slice(r0, r0 + EPI_ROWS)
                v = acc[rows, :]
                if resid is not None:
                    v = v + named["resid"][rows, :]
                if mode == "plain":
                    outs[0][rows, :] = v.astype(outs[0].dtype)
                elif mode == "rms":
                    outs[0][rows, :] = v
                    r = lax.rsqrt(jnp.mean(v * v, axis=-1, keepdims=True) + EPS)
                    outs[1][rows, :] = (v * r * named["gain"][...]).astype(BF16)
                elif mode == "loss":
                    d = v - named["tgt"][rows, :]
                    outs[2][...] += jnp.sum(d * d, axis=0, keepdims=True) * (0.5 / N)
                    dy = d * (1.0 / N)
                    outs[0][rows, :] = dy
                    outs[1][rows, :] = dy.astype(BF16)
                else:
                    xf = named["x"][rows, :]
                    r = lax.rsqrt(jnp.mean(xf * xf, axis=-1, keepdims=True) + EPS)
                    xhat = xf * r
                    outs[2][...] += jnp.sum(v * xhat, axis=0, keepdims=True)
                    dxh = v * named["gain"][...]
                    tot = named["dres"][rows, :] + r * (dxh - xhat * jnp.mean(dxh * xhat, axis=-1, keepdims=True))
                    outs[0][rows, :] = tot
                    outs[1][rows, :] = tot.astype(BF16)

    row = pl.BlockSpec((tm, N), lambda i, k: (i, 0))
    vec = pl.BlockSpec((1, N), lambda i, k: (0, 0))
    if a.ndim == 3:
        a_spec = pl.BlockSpec((None, tm, tk), lambda i, k: (k // kper, i, k % kper))
    else:
        a_spec = pl.BlockSpec((tm, tk), lambda i, k: (i, k))
    in_specs = [a_spec, pl.BlockSpec((tk, N), lambda i, k: (k, 0))]
    args = [a, w]
    for t, spec in ((resid, row), (gain, vec), (tgt, row), (x, row), (dres, row), (dep, HBM_SPEC)):
        if t is not None:
            in_specs.append(spec)
            args.append(t)
    if mode == "plain":
        out_specs, out_shape = [row], [_sds((M, N), out_dtype)]
    elif mode == "rms":
        out_specs, out_shape = [row, row], [_sds((M, N), F32), _sds((M, N), BF16)]
    else:
        out_specs, out_shape = [row, row, vec], [_sds((M, N), F32), _sds((M, N), BF16), _sds((1, N), F32)]
    res = pl.pallas_call(
        body, name=name, grid=(M // tm, nk), in_specs=in_specs, out_specs=out_specs, out_shape=out_shape,
        scratch_shapes=[pltpu.VMEM((tm, N), F32)], compiler_params=_cparams(2))(*args)
    return res[0] if mode == "plain" else res


def _mm_tn(a, b, name, out_dtype=BF16, tt=512):
    parts = a.shape[0] if a.ndim == 3 else 1
    Tt, Mp = a.shape[-2], a.shape[-1]
    N = b.shape[1]
    tn = _pick(Mp, (1408, 1280, 1024, 512))
    jper = Mp // tn
    nt = Tt // tt

    def body(a_ref, b_ref, o_ref, acc):
        t = pl.program_id(1)

        @pl.when(t == 0)
        def _():
            acc[...] = jnp.zeros_like(acc)

        acc[...] += lax.dot_general(a_ref[...], b_ref[...], (((0,), (0,)), ((), ())),
                                    preferred_element_type=F32)

        @pl.when(t == nt - 1)
        def _():
            o_ref[...] = acc[...].astype(o_ref.dtype)

    if a.ndim == 3:
        a_spec = pl.BlockSpec((None, tt, tn), lambda j, t: (j // jper, t, j % jper))
    else:
        a_spec = pl.BlockSpec((tt, tn), lambda j, t: (t, j))
    return pl.pallas_call(
        body, name=name, grid=(parts * jper, nt),
        in_specs=[a_spec, pl.BlockSpec((tt, N), lambda j, t: (t, 0))],
        out_specs=pl.BlockSpec((tn, N), lambda j, t: (j, 0)),
        out_shape=_sds((parts * Mp, N), out_dtype), scratch_shapes=[pltpu.VMEM((tn, N), F32)],
        compiler_params=_cparams(2))(a, b)


def _ffn_in(h, wt_in, name, tm=1024, tn=256):
    nj = DFF // tn

    def body(h_ref, wg_ref, wu_ref, p_ref, act_ref):
        nt = (((1,), (1,)), ((), ()))
        g = lax.dot_general(h_ref[...], wg_ref[...], nt, preferred_element_type=F32)
        u = lax.dot_general(h_ref[...], wu_ref[...], nt, preferred_element_type=F32)
        p_ref[0] = g.astype(BF16)
        p_ref[1] = u.astype(BF16)
        act_ref[...] = (g * _sigmoid(g) * u).astype(BF16)

    return pl.pallas_call(
        body, name=name, grid=(T // tm, nj),
        in_specs=[pl.BlockSpec((tm, D), lambda i, j: (i, 0)), pl.BlockSpec((tn, D), lambda i, j: (j, 0)),
                  pl.BlockSpec((tn, D), lambda i, j: (j + nj, 0))],
        out_specs=[pl.BlockSpec((2, tm, tn), lambda i, j: (0, i, j)), pl.BlockSpec((tm, tn), lambda i, j: (i, j))],
        out_shape=[_sds((2, T, DFF), BF16), _sds((T, DFF), BF16)], compiler_params=_cparams(2))(h, wt_in, wt_in)


def _ffn_dact(dyb, w_down, p3, name, tm=1024, tn=256, dep=None):
    def body(dy_ref, w_ref, p_ref, *rest):
        o_ref = rest[-1]
        da = lax.dot_general(dy_ref[...], w_ref[...], (((1,), (1,)), ((), ())), preferred_element_type=F32)
        g = p_ref[0].astype(F32)
        u = p_ref[1].astype(F32)
        sg = _sigmoid(g)
        o_ref[0] = (da * u * sg * (1.0 + g * (1.0 - sg))).astype(BF16)
        o_ref[1] = (da * g * sg).astype(BF16)

    pspec = pl.BlockSpec((2, tm, tn), lambda i, j: (0, i, j))
    in_specs = [pl.BlockSpec((tm, D), lambda i, j: (i, 0)), pl.BlockSpec((tn, D), lambda i, j: (j, 0)), pspec]
    args = [dyb, w_down, p3]
    if dep is not None:
        in_specs.append(HBM_SPEC)
        args.append(dep)
    return pl.pallas_call(
        body, name=name, grid=(T // tm, DFF // tn), in_specs=in_specs, out_specs=pspec,
        out_shape=_sds((2, T, DFF), BF16), compiler_params=_cparams(2))(*args)


def _rms_fwd(x, g, name, tm=512):
    def body(x_ref, g_ref, h_ref):
        xf = x_ref[...]
        r = lax.rsqrt(jnp.mean(xf * xf, axis=-1, keepdims=True) + EPS)
        h_ref[...] = (xf * r * g_ref[...]).astype(BF16)

    return pl.pallas_call(
        body, name=name, grid=(T // tm,),
        in_specs=[pl.BlockSpec((tm, D), lambda i: (i, 0)), pl.BlockSpec((1, D), lambda i: (0, 0))],
        out_specs=pl.BlockSpec((tm, D), lambda i: (i, 0)),
        out_shape=_sds((T, D), BF16), compiler_params=_cparams(1))(x, g)


def _tril_mask():
    r = lax.broadcasted_iota(jnp.int32, (128, 128), 0)
    c = lax.broadcasted_iota(jnp.int32, (128, 128), 1)
    return r >= c


def _mix_a_fwd(pab, sgu_g, sgu_b, sgu_w, sgu_bias3, tm=512):
    def body(zu_ref, zv_ref, g_ref, b_ref, w_ref, bias_ref, o_ref):
        u = _gelu(zu_ref[...].astype(F32))
        v = _gelu(zv_ref[...].astype(F32))
        mu = jnp.mean(v, axis=-1, keepdims=True)
        vc = v - mu
        rstd = lax.rsqrt(jnp.mean(vc * vc, axis=-1, keepdims=True) + EPS)
        vn = (vc * rstd * g_ref[...] + b_ref[...]).astype(BF16)
        tri = _tril_mask()
        for gi in range(4):
            wg = jnp.where(tri, w_ref[gi], 0.0).astype(BF16)
            bg = bias_ref[gi]
            for c in range(tm // 128):
                rs, cs = slice(c * 128, (c + 1) * 128), slice(gi * 128, (gi + 1) * 128)
                mixed = jnp.dot(wg, vn[rs, cs], preferred_element_type=F32) + bg
                o_ref[rs, cs] = (u[rs, cs] * mixed).astype(BF16)

    half = pl.BlockSpec((tm, 512), lambda i: (i, 0))
    return pl.pallas_call(
        body, name="mix_a_fwd", grid=(T // tm,),
        in_specs=[half, pl.BlockSpec((tm, 512), lambda i: (i, 1)),
                  pl.BlockSpec((1, 512), lambda i: (0, 0)), pl.BlockSpec((1, 512), lambda i: (0, 0)),
                  pl.BlockSpec((4, 128, 128), lambda i: (0, 0, 0)), pl.BlockSpec((4, 128, 1), lambda i: (0, 0, 0))],
        out_specs=half, out_shape=_sds((T, D), BF16), compiler_params=_cparams(1),
    )(pab, pab, sgu_g, sgu_b, sgu_w, sgu_bias3)


def _rope_tables():
    pos = jnp.arange(T, dtype=F32)
    inv_freq = ROPE_THETA ** (-jnp.arange(ROPE_HALF, dtype=F32) * 2.0 / (2 * ROPE_HALF))
    ang = pos[:, None] * inv_freq[None, :]
    cos, sin = jnp.cos(ang), jnp.sin(ang)
    z8 = jnp.zeros((T, ROPE_HALF), F32)
    rest = HEAD - 2 * ROPE_HALF
    c64 = jnp.concatenate([cos, cos, jnp.ones((T, rest), F32)], axis=1)
    s1 = jnp.concatenate([z8, sin, jnp.zeros((T, rest), F32)], axis=1)
    s2 = jnp.concatenate([-sin, z8, jnp.zeros((T, rest), F32)], axis=1)
    return jnp.tile(c64, (1, 2)), jnp.tile(s1, (1, 2)), jnp.tile(s2, (1, 2))


def _lo_mask(shape):
    return lax.broadcasted_iota(jnp.int32, shape, 1) < HEAD


def _seg_mean(x, lo):
    s_all = jnp.sum(x, axis=-1, keepdims=True)
    s_lo = jnp.sum(jnp.where(lo, x, 0.0), axis=-1, keepdims=True)
    return jnp.where(lo, s_lo, s_all - s_lo) * (1.0 / HEAD)


def _rope(n, c, s1, s2):
    return n * c + pltpu.roll(n, ROPE_HALF, 1) * s1 + pltpu.roll(n, PAIR - ROPE_HALF, 1) * s2


def _rope_t(dy, c, s1, s2):
    return dy * c - pltpu.roll(dy, PAIR - ROPE_HALF, 1) * s2 - pltpu.roll(dy, ROPE_HALF, 1) * s1


def _prep_fwd(pab, qg, kg, tabs, tm=512):
    def body(p_ref, qg_ref, kg_ref, c_ref, s1_ref, s2_ref, *outs):
        lo = _lo_mask((tm, PAIR))
        c, s1, s2 = c_ref[...], s1_ref[...], s2_ref[...]
        for g in range(3):
            qn_ref, kn_ref, v_ref = outs[3 * g:3 * g + 3]
            for p in range(NPAIR):
                for which, gains, dst in ((0, qg_ref, qn_ref), (1, kg_ref, kn_ref)):
                    col = (2 + 3 * which + g) * 512 + p * PAIR
                    xr = p_ref[:, col:col + PAIR].astype(F32)
                    rinv = lax.rsqrt(_seg_mean(xr * xr, lo) + EPS)
                    dst[p] = _rope(xr * rinv * gains[g:g + 1, :], c, s1, s2)
                col = (8 + g) * 512 + p * PAIR
                v_ref[p] = p_ref[:, col:col + PAIR].astype(F32)

    pm = pl.BlockSpec((NPAIR, tm, PAIR), lambda i: (0, i, 0))
    tab = pl.BlockSpec((tm, PAIR), lambda i: (i, 0))
    gain = pl.BlockSpec((3, PAIR), lambda i: (0, 0))
    return pl.pallas_call(
        body, name="prep_fwd", grid=(T // tm,),
        in_specs=[pl.BlockSpec((tm, AB_IN), lambda i: (i, 0)), gain, gain, tab, tab, tab],
        out_specs=[pm] * 9, out_shape=[_sds((NPAIR, T, PAIR), F32)] * 9,
        compiler_params=_cparams(1))(pab, qg, kg, *tabs)


def _res_index(it, rate):
    window = NBACK * rate
    b = it // rate
    rho = it % rate
    start = b * window + rho
    startp = jnp.maximum(start - window, rho)
    kmin = jnp.where(b > 0, 0, NBACK)
    return start, startp, kmin


def _rows(start, rate):
    if rate == 1:
        return pl.ds(pl.multiple_of(start, NBACK), NBACK)
    return pl.ds(start, NBACK, stride=rate)


def _band():
    qi = lax.broadcasted_iota(jnp.int32, (NBACK, 2 * NBACK), 0)
    kj = lax.broadcasted_iota(jnp.int32, (NBACK, 2 * NBACK), 1)
    dist = qi + NBACK - kj
    return (dist >= 0) & (dist <= NBACK), kj


def _attn_fwd(qn, kn, v, rate, name, dep=None):
    def body(q_ref, k_ref, v_ref, *rest):
        o_ref, l_ref = rest[-2:]
        lo = _lo_mask((NBACK, PAIR))
        band, kj = _band()

        def step(it, carry):
            start, startp, kmin = _res_index(it, rate)
            q = q_ref[_rows(start, rate), :]
            kcat = jnp.concatenate([k_ref[_rows(startp, rate), :], k_ref[_rows(start, rate), :]], axis=0).astype(BF16)
            vcat = jnp.concatenate([v_ref[_rows(startp, rate), :], v_ref[_rows(start, rate), :]], axis=0).astype(BF16)
            ok = band & (kj >= kmin)
            q2 = jnp.concatenate([jnp.where(lo, q, 0.0), jnp.where(lo, 0.0, q)], axis=0).astype(BF16)
            s = lax.dot_general(q2, kcat, (((1,), (1,)), ((), ())), preferred_element_type=F32) * (HEAD ** -0.5)
            s = jnp.where(jnp.concatenate([ok, ok], axis=0), s, NEG_INF)
            m = jnp.max(s, axis=-1, keepdims=True)
            pr = jnp.exp(s - m)
            l = jnp.sum(pr, axis=-1, keepdims=True)
            o2 = jnp.dot(pr.astype(BF16), vcat, preferred_element_type=F32) / l
            ls = m + jnp.log(l)
            o_ref[_rows(start, rate), :] = jnp.where(lo, o2[0:NBACK], o2[NBACK:])
            l_ref[_rows(start, rate), :] = jnp.where(lo, ls[0:NBACK], ls[NBACK:])
            return carry

        lax.fori_loop(0, T // NBACK, step, 0, unroll=4)

    pm = pl.BlockSpec((None, T, PAIR), lambda p: (p, 0, 0))
    in_specs, args = [pm, pm, pm], [qn, kn, v]
    if dep is not None:
        in_specs.append(HBM_SPEC)
        args.append(dep)
    return pl.pallas_call(
        body, name=name, grid=(NPAIR,), in_specs=in_specs, out_specs=[pm, pm],
        out_shape=[_sds((NPAIR, T, PAIR), F32)] * 2, compiler_params=_cparams(1))(*args)


def _merge_fwd(cat_ab, outs, lses, tm=512):
    def body(cat_in, o0, o1, o2, l0, l1, l2, cat_ref, lse_ref):
        del cat_in
        for p in range(NPAIR):
            a0, a1, a2 = l0[p], l1[p], l2[p]
            m = jnp.maximum(jnp.maximum(a0, a1), a2)
            w0, w1, w2 = jnp.exp(a0 - m), jnp.exp(a1 - m), jnp.exp(a2 - m)
            s = w0 + w1 + w2
            b = (w0 * o0[p] + w1 * o1[p] + w2 * o2[p]) / s
            cat_ref[:, p * PAIR:(p + 1) * PAIR] = b.astype(BF16)
            lse_ref[p] = m + jnp.log(s)

    pm = pl.BlockSpec((NPAIR, tm, PAIR), lambda i: (0, i, 0))
    return pl.pallas_call(
        body, name="merge_fwd", grid=(T // tm,),
        in_specs=[pl.BlockSpec(memory_space=pl.ANY)] + [pm] * 6,
        out_specs=[pl.BlockSpec((tm, 512), lambda i: (i, 1)), pm],
        out_shape=[_sds((T, D), BF16), _sds((NPAIR, T, PAIR), F32)],
        input_output_aliases={0: 0}, compiler_params=_cparams(1))(cat_ab, *outs, *lses)


def _b_pre_bwd(dcat, cat, tm=512):
    def body(db_ref, b_ref, dbp_ref, e_ref):
        lo = _lo_mask((tm, PAIR))
        for p in range(NPAIR):
            db = db_ref[:, p * PAIR:(p + 1) * PAIR].astype(F32)
            b = b_ref[:, p * PAIR:(p + 1) * PAIR].astype(F32)
            dbp_ref[p] = db
            e_ref[p] = _seg_mean(db * b, lo) * float(HEAD)

    pm = pl.BlockSpec((NPAIR, tm, PAIR), lambda i: (0, i, 0))
    right = pl.BlockSpec((tm, 512), lambda i: (i, 1))
    return pl.pallas_call(
        body, name="b_pre_bwd", grid=(T // tm,), in_specs=[right, right], out_specs=[pm, pm],
        out_shape=[_sds((NPAIR, T, PAIR), F32)] * 2, compiler_params=_cparams(1))(dcat, cat)


def _attn_bwd(qn, kn, v, dbp, e, lse, rate, name):
    def body(q_ref, k_ref, v_ref, db_ref, e_ref, lse_ref, dq_ref, dk_ref, dv_ref):
        lo = _lo_mask((NBACK, PAIR))
        band, kj = _band()
        scale = HEAD ** -0.5
        nt = (((1,), (1,)), ((), ()))
        tn = (((0,), (0,)), ((), ()))
        window = NBACK * rate
        nblk = T // window

        def one(it, carry):
            dk_carry, dv_carry = carry
            rho = it // nblk
            b = it % nblk
            start = b * window + rho
            rq = _rows(start, rate)
            rp = _rows(jnp.maximum(start - window, rho), rate)
            kmin = jnp.where(b > 0, 0, NBACK)
            q = q_ref[rq, :]
            db = db_ref[rq, :]
            ev = e_ref[rq, :]
            ls = lse_ref[rq, :]
            kcat = jnp.concatenate([k_ref[rp, :], k_ref[rq, :]], axis=0).astype(BF16)
            vcat = jnp.concatenate([v_ref[rp, :], v_ref[rq, :]], axis=0).astype(BF16)
            ok = band & (kj >= kmin)
            ok2 = jnp.concatenate([ok, ok], axis=0)
            q2 = jnp.concatenate([jnp.where(lo, q, 0.0), jnp.where(lo, 0.0, q)], axis=0).astype(BF16)
            db2 = jnp.concatenate([jnp.where(lo, db, 0.0), jnp.where(lo, 0.0, db)], axis=0).astype(BF16)
            ls2 = jnp.concatenate([ls[:, 0:1], ls[:, HEAD:HEAD + 1]], axis=0)
            ev2 = jnp.concatenate([ev[:, 0:1], ev[:, HEAD:HEAD + 1]], axis=0)
            s = lax.dot_general(q2, kcat, nt, preferred_element_type=F32) * scale
            s = jnp.where(ok2, s, NEG_INF)
            pt = jnp.exp(s - ls2)
            dp = lax.dot_general(db2, vcat, nt, preferred_element_type=F32)
            ds = (pt * (dp - ev2)).astype(BF16)
            dq2 = jnp.dot(ds, kcat, preferred_element_type=F32) * scale
            dkc = lax.dot_general(ds, q2, tn, preferred_element_type=F32) * scale
            dvc = lax.dot_general(pt.astype(BF16), db2, tn, preferred_element_type=F32)
            dq_ref[rq, :] = jnp.where(lo, dq2[0:NBACK], dq2[NBACK:])
            dk_ref[rp, :] = dk_carry + dkc[0:NBACK]
            dk_ref[rq, :] = dkc[NBACK:]
            dv_ref[rp, :] = dv_carry + dvc[0:NBACK]
            dv_ref[rq, :] = dvc[NBACK:]
            return dkc[NBACK:], dvc[NBACK:]

        def step(i, carry):
            for u in range(ATTN_BWD_UNROLL):
                carry = one(i * ATTN_BWD_UNROLL + u, carry)
            return carry

        zero = jnp.zeros((NBACK, PAIR), F32)
        lax.fori_loop(0, T // NBACK // ATTN_BWD_UNROLL, step, (zero, zero))

    pm = pl.BlockSpec((None, T, PAIR), lambda p: (p, 0, 0))
    return pl.pallas_call(
        body, name=name, grid=(NPAIR,), in_specs=[pm] * 6, out_specs=[pm] * 3,
        out_shape=[_sds((NPAIR, T, PAIR), F32)] * 3, compiler_params=_cparams(1, 56))(qn, kn, v, dbp, e, lse)


def _ab_in_bwd(pab, dcat, sgu_g, sgu_b, sgu_w, sgu_bias3, qg, kg, tabs, dqkv, tm=256):
    def body(p_ref, dcat_ref, g_ref, b_ref, w_ref, bias_ref, qg_ref, kg_ref, c_ref, s1_ref, s2_ref, *rest):
        dq_refs = rest[0:9]
        o_ref, dwm_ref, dbias_ref, dsg_ref, dsb_ref, dgain_ref = rest[9:]
        i = pl.program_id(0)

        @pl.when(i == 0)
        def _():
            dwm_ref[...] = jnp.zeros_like(dwm_ref)
            dbias_ref[...] = jnp.zeros_like(dbias_ref)
            dsg_ref[...] = jnp.zeros_like(dsg_ref)
            dsb_ref[...] = jnp.zeros_like(dsb_ref)
            dgain_ref[...] = jnp.zeros_like(dgain_ref)

        zu = p_ref[:, 0:512].astype(F32)
        zv = p_ref[:, 512:1024].astype(F32)
        u = _gelu(zu)
        v = _gelu(zv)
        mu = jnp.mean(v, axis=-1, keepdims=True)
        vc = v - mu
        rstd = lax.rsqrt(jnp.mean(vc * vc, axis=-1, keepdims=True) + EPS)
        xhat = vc * rstd
        vn = (xhat * g_ref[...] + b_ref[...]).astype(BF16)
        da = dcat_ref[...].astype(F32)
        tri = _tril_mask()
        du_parts = [[None] * 4 for _ in range(tm // 128)]
        dvn_parts = [[None] * 4 for _ in range(tm // 128)]
        for gi in range(4):
            wg = jnp.where(tri, w_ref[gi], 0.0).astype(BF16)
            bg = bias_ref[gi]
            for c in range(tm // 128):
                rs, cs = slice(c * 128, (c + 1) * 128), slice(gi * 128, (gi + 1) * 128)
                vblk = vn[rs, cs]
                mixed = jnp.dot(wg, vblk, preferred_element_type=F32) + bg
                dab = da[rs, cs]
                du_parts[c][gi] = dab * mixed
                dmixed = dab * u[rs, cs]
                dmb = dmixed.astype(BF16)
                dvn_parts[c][gi] = lax.dot_general(wg, dmb, (((0,), (0,)), ((), ())), preferred_element_type=F32)
                dwm = lax.dot_general(dmb, vblk, (((1,), (1,)), ((), ())), preferred_element_type=F32)
                dwm_ref[gi] += jnp.where(tri, dwm, 0.0)
                dbias_ref[gi] += dmixed
        du = jnp.concatenate([jnp.concatenate(r, axis=1) for r in du_parts], axis=0)
        dvn = jnp.concatenate([jnp.concatenate(r, axis=1) for r in dvn_parts], axis=0)
        dsg_ref[...] += jnp.sum(dvn * xhat, axis=0, keepdims=True)
        dsb_ref[...] += jnp.sum(dvn, axis=0, keepdims=True)
        dxh = dvn * g_ref[...]
        dv = rstd * (dxh - jnp.mean(dxh, axis=-1, keepdims=True)
                     - xhat * jnp.mean(dxh * xhat, axis=-1, keepdims=True))
        o_ref[:, 0:512] = (du * _gelu_grad(zu)).astype(BF16)
        o_ref[:, 512:1024] = (dv * _gelu_grad(zv)).astype(BF16)

        lo = _lo_mask((tm, PAIR))
        c, s1, s2 = c_ref[...], s1_ref[...], s2_ref[...]
        for g in range(3):
            dq_ref, dk_ref, dv_ref = dq_refs[3 * g:3 * g + 3]
            for p in range(NPAIR):
                for which, gains, src in ((0, qg_ref, dq_ref), (1, kg_ref, dk_ref)):
                    col = (2 + 3 * which + g) * 512 + p * PAIR
                    xr = p_ref[:, col:col + PAIR].astype(F32)
                    rinv = lax.rsqrt(_seg_mean(xr * xr, lo) + EPS)
                    xh = xr * rinv
                    dn = _rope_t(src[p], c, s1, s2)
                    row = 2 * g + which
                    dgain_ref[row:row + 1, :] += jnp.sum(dn * xh, axis=0, keepdims=True)
                    dxh2 = dn * gains[g:g + 1, :]
                    dx = rinv * (dxh2 - xh * _seg_mean(dxh2 * xh, lo))
                    o_ref[:, col:col + PAIR] = dx.astype(BF16)
                col = (8 + g) * 512 + p * PAIR
                o_ref[:, col:col + PAIR] = dv_ref[p].astype(BF16)

    pm = pl.BlockSpec((NPAIR, tm, PAIR), lambda i: (0, i, 0))
    tab = pl.BlockSpec((tm, PAIR), lambda i: (i, 0))
    gain = pl.BlockSpec((3, PAIR), lambda i: (0, 0))
    vec = pl.BlockSpec((1, 512), lambda i: (0, 0))
    full = pl.BlockSpec((tm, AB_IN), lambda i: (i, 0))
    w4 = pl.BlockSpec((4, 128, 128), lambda i: (0, 0, 0))
    return pl.pallas_call(
        body, name="ab_in_bwd", grid=(T // tm,),
        in_specs=[full, pl.BlockSpec((tm, 512), lambda i: (i, 0)), vec, vec, w4,
                  pl.BlockSpec((4, 128, 1), lambda i: (0, 0, 0)), gain, gain, tab, tab, tab] + [pm] * 9,
        out_specs=[full, w4, w4, vec, vec, pl.BlockSpec((8, PAIR), lambda i: (0, 0))],
        out_shape=[_sds((T, AB_IN), BF16), _sds((4, 128, 128), F32), _sds((4, 128, 128), F32),
                   _sds((1, 512), F32), _sds((1, 512), F32), _sds((8, PAIR), F32)],
        compiler_params=_cparams(1))(pab, dcat, sgu_g, sgu_b, sgu_w, sgu_bias3, qg, kg, *tabs, *dqkv)


def _ln_stats(x):
    mu = jnp.mean(x, axis=-1, keepdims=True)
    xc = x - mu
    rstd = lax.rsqrt(jnp.mean(xc * xc, axis=-1, keepdims=True) + EPS)
    return xc * rstd, rstd


def _cd_fwd(pcd, cw, cb, lg, lb, dw, tm=512):
    per = tm // HALO

    def body(p_ref, h_ref, cw_ref, cb_ref, lg_ref, lb_ref, dw_ref, cat_ref, c0_ref, c1_ref, dd_ref, y_ref, buf, buf2):
        i = pl.program_id(0)
        live = jnp.where(i > 0, 1.0, 0.0)
        a = p_ref[:, 0:512].astype(F32)
        gt = p_ref[:, 512:1024].astype(F32)
        gb = p_ref[:, 1024:1536].astype(F32)
        gc = p_ref[:, 1536:2048].astype(F32)
        hv = p_ref[:, 2048:2560].astype(F32)
        c0 = a * _sigmoid(gt)
        dd = gc * hv
        buf[0:HALO, :] = h_ref[:, 0:512].astype(F32) * _sigmoid(h_ref[:, 512:1024].astype(F32)) * live
        buf[HALO:, :] = c0
        buf2[0:HALO, :] = h_ref[:, 1536:2048].astype(F32) * h_ref[:, 2048:2560].astype(F32) * live
        buf2[HALO:, :] = dd
        acc = jnp.broadcast_to(cb_ref[...], (tm, 512))
        for j in range(CONV_C_TAPS):
            acc = acc + cw_ref[j:j + 1, :] * buf[pl.ds(HALO - (CONV_C_TAPS - 1) + j, tm), :]
        xhat, _ = _ln_stats(acc)
        c2 = xhat * lg_ref[...] + lb_ref[...]
        y = jnp.zeros((tm, 512), F32)
        for j in range(CONV_D_TAPS):
            y = y + dw_ref[j:j + 1, :] * buf2[pl.ds(HALO - (CONV_D_TAPS - 1) + j, tm), :]
        cat_ref[:, 0:512] = (c2 * _sigmoid(c2)).astype(BF16)
        cat_ref[:, 512:1024] = (gb * y).astype(BF16)
        c0_ref[...] = c0.astype(BF16)
        c1_ref[...] = acc
        dd_ref[...] = dd.astype(BF16)
        y_ref[...] = y.astype(BF16)

    half = pl.BlockSpec((tm, 512), lambda i: (i, 0))
    vec = pl.BlockSpec((1, 512), lambda i: (0, 0))
    return pl.pallas_call(
        body, name="cd_fwd", grid=(T // tm,),
        in_specs=[pl.BlockSpec((tm, CD_IN), lambda i: (i, 0)),
                  pl.BlockSpec((HALO, CD_IN), lambda i: (jnp.maximum(i * per - 1, 0), 0)),
                  pl.BlockSpec((32, 512), lambda i: (0, 0)), vec, vec, vec, pl.BlockSpec((8, 512), lambda i: (0, 0))],
        out_specs=[pl.BlockSpec((tm, D), lambda i: (i, 0)), half, half, half, half],
        out_shape=[_sds((T, D), BF16), _sds((T, 512), BF16), _sds((T, 512), F32), _sds((T, 512), BF16),
                   _sds((T, 512), BF16)],
        scratch_shapes=[pltpu.VMEM((HALO + tm, 512), F32), pltpu.VMEM((HALO + tm, 512), F32)],
        compiler_params=_cparams(1))(pcd, pcd, cw, cb, lg, lb, dw)


def _cd_bwd_pw(dcat, c1, pcd, y, lg, lb, tm=512):
    def body(dcat_ref, c1_ref, gb_ref, y_ref, lg_ref, lb_ref, dc1_ref, dy3_ref, dgb_ref, dlg_ref, dlb_ref, dcb_ref):
        i = pl.program_id(0)

        @pl.when(i == 0)
        def _():
            dlg_ref[...] = jnp.zeros_like(dlg_ref)
            dlb_ref[...] = jnp.zeros_like(dlb_ref)
            dcb_ref[...] = jnp.zeros_like(dcb_ref)

        dc = dcat_ref[:, 0:512].astype(F32)
        ddo = dcat_ref[:, 512:1024].astype(F32)
        xhat, rstd = _ln_stats(c1_ref[...])
        c2 = xhat * lg_ref[...] + lb_ref[...]
        sg = _sigmoid(c2)
        dc2 = dc * sg * (1.0 + c2 * (1.0 - sg))
        dlg_ref[...] += jnp.sum(dc2 * xhat, axis=0, keepdims=True)
        dlb_ref[...] += jnp.sum(dc2, axis=0, keepdims=True)
        dxh = dc2 * lg_ref[...]
        dc1 = rstd * (dxh - jnp.mean(dxh, axis=-1, keepdims=True)
                      - xhat * jnp.mean(dxh * xhat, axis=-1, keepdims=True))
        dcb_ref[...] += jnp.sum(dc1, axis=0, keepdims=True)
        dc1_ref[...] = dc1
        dgb_ref[...] = (ddo * y_ref[...].astype(F32)).astype(BF16)
        dy3_ref[...] = ddo * gb_ref[...].astype(F32)

    half = pl.BlockSpec((tm, 512), lambda i: (i, 0))
    vec = pl.BlockSpec((1, 512), lambda i: (0, 0))
    return pl.pallas_call(
        body, name="cd_bwd_pw", grid=(T // tm,),
        in_specs=[pl.BlockSpec((tm, D), lambda i: (i, 0)), half, pl.BlockSpec((tm, 512), lambda i: (i, 2)), half,
                  vec, vec],
        out_specs=[half, half, half, vec, vec, vec],
        out_shape=[_sds((T, 512), F32), _sds((T, 512), F32), _sds((T, 512), BF16),
                   _sds((1, 512), F32), _sds((1, 512), F32), _sds((1, 512), F32)],
        compiler_params=_cparams(1))(dcat, c1, pcd, y, lg, lb)


def _cd_bwd_conv(pcd, dc1, dy3, c0, dd, dgb, cw, dw, tm=512):
    per = tm // HALO
    nblk = T // tm
    last32 = T // HALO - 1

    def body(p_ref, dc1_ref, dc1n_ref, dy3_ref, dy3n_ref, c0_ref, c0p_ref, dd_ref, ddp_ref, dgb_ref, cw_ref, dw_ref,
             o_ref, dcw_ref, ddw_ref, dbuf, cbuf, d3buf, ddbuf):
        i = pl.program_id(0)
        has_prev = jnp.where(i > 0, 1.0, 0.0)
        has_next = jnp.where(i < nblk - 1, 1.0, 0.0)

        @pl.when(i == 0)
        def _():
            dcw_ref[...] = jnp.zeros_like(dcw_ref)
            ddw_ref[...] = jnp.zeros_like(ddw_ref)

        dc1 = dc1_ref[...]
        dy3 = dy3_ref[...]
        dbuf[0:tm, :] = dc1
        dbuf[tm:, :] = dc1n_ref[...] * has_next
        d3buf[0:tm, :] = dy3
        d3buf[tm:, :] = dy3n_ref[...] * has_next
        cbuf[0:HALO, :] = c0p_ref[...].astype(F32) * has_prev
        cbuf[HALO:, :] = c0_ref[...].astype(F32)
        ddbuf[0:HALO, :] = ddp_ref[...].astype(F32) * has_prev
        ddbuf[HALO:, :] = dd_ref[...].astype(F32)

        dc0 = jnp.zeros((tm, 512), F32)
        for j in range(CONV_C_TAPS):
            dc0 = dc0 + cw_ref[j:j + 1, :] * dbuf[pl.ds(CONV_C_TAPS - 1 - j, tm), :]
            dcw_ref[j:j + 1, :] += jnp.sum(dc1 * cbuf[pl.ds(HALO - (CONV_C_TAPS - 1) + j, tm), :], axis=0, keepdims=True)
        ddd = jnp.zeros((tm, 512), F32)
        for j in range(CONV_D_TAPS):
            ddd = ddd + dw_ref[j:j + 1, :] * d3buf[pl.ds(CONV_D_TAPS - 1 - j, tm), :]
            ddw_ref[j:j + 1, :] += jnp.sum(dy3 * ddbuf[pl.ds(HALO - (CONV_D_TAPS - 1) + j, tm), :], axis=0, keepdims=True)

        a = p_ref[:, 0:512].astype(F32)
        gt = p_ref[:, 512:1024].astype(F32)
        gc = p_ref[:, 1536:2048].astype(F32)
        hv = p_ref[:, 2048:2560].astype(F32)
        sg = _sigmoid(gt)
        o_ref[:, 0:512] = (dc0 * sg).astype(BF16)
        o_ref[:, 512:1024] = (dc0 * a * sg * (1.0 - sg)).astype(BF16)
        o_ref[:, 1024:1536] = dgb_ref[...]
        o_ref[:, 1536:2048] = (ddd * hv).astype(BF16)
        o_ref[:, 2048:2560] = (ddd * gc).astype(BF16)

    half = pl.BlockSpec((tm, 512), lambda i: (i, 0))
    nxt = pl.BlockSpec((HALO, 512), lambda i: (jnp.minimum((i + 1) * per, last32), 0))
    prv = pl.BlockSpec((HALO, 512), lambda i: (jnp.maximum(i * per - 1, 0), 0))
    full = pl.BlockSpec((tm, CD_IN), lambda i: (i, 0))
    return pl.pallas_call(
        body, name="cd_bwd_conv", grid=(nblk,),
        in_specs=[full, half, nxt, half, nxt, half, prv, half, prv, half,
                  pl.BlockSpec((32, 512), lambda i: (0, 0)), pl.BlockSpec((8, 512), lambda i: (0, 0))],
        out_specs=[full, pl.BlockSpec((32, 512), lambda i: (0, 0)), pl.BlockSpec((8, 512), lambda i: (0, 0))],
        out_shape=[_sds((T, CD_IN), BF16), _sds((32, 512), F32), _sds((8, 512), F32)],
        scratch_shapes=[pltpu.VMEM((tm + HALO, 512), F32), pltpu.VMEM((HALO + tm, 512), F32),
                        pltpu.VMEM((tm + HALO, 512), F32), pltpu.VMEM((HALO + tm, 512), F32)],
        compiler_params=_cparams(1))(pcd, dc1, dc1, dy3, dy3, c0, c0, dd, dd, dgb, cw, dw)


def _local_step(x, tgt, W, fetch=None, on_grad=None):
    W = dict(W)
    if fetch is None:
        fetch = lambda stage, after: {}
    if on_grad is None:
        on_grad = lambda key, arr: None
    tabs = _rope_tables()
    qg = jnp.tile(W["q_norm_g"], (1, 2))
    kg = jnp.tile(W["k_norm_g"], (1, 2))
    bias3 = W["sgu_bias"].reshape(4, 128, 1)
    G = {}

    h0 = _rms_fwd(x, W["ab_norm_g"], "rms_fwd_ab")
    pab = _mm_nt(h0, W["wt_ab_in"], "mm_ab_in", dep=W.get("dep0"))
    cat_ab = _mix_a_fwd(pab, W["sgu_norm_g"], W["sgu_norm_b"], W["sgu_w"], bias3)
    qkv = _prep_fwd(pab, qg, kg, tabs)
    outs, lses = [], []
    for g, rate in enumerate(DIL_RATES):
        o, l = _attn_fwd(qkv[3 * g], qkv[3 * g + 1], qkv[3 * g + 2], rate, f"attn_fwd_{g}", dep=W.get(f"dep_attn{g}"))
        outs.append(o)
        lses.append(l)
        W.update(fetch(f"attn{g}", o))
    cat_ab, lse = _merge_fwd(cat_ab, outs, lses)
    W.update(fetch("ab_out", lse))
    x1, h1 = _mm_nn(cat_ab, W["w_ab_out"], "mm_ab_out", mode="rms", resid=x, gain=W["ffn_norm_g"][0:1])
    pf0, act0 = _ffn_in(h1, W["wt_ffn_in0"], "ffn_in0")
    W.update(fetch("ffn_down0", act0))
    x2, h2 = _mm_nn(act0, W["w_ffn_down0"], "mm_ffn_down0", mode="rms", resid=x1, gain=W["cd_norm_g"],
                    dep=W.get("dep_down0"))
    W.update(fetch("cd_in", h2))
    pcd = _mm_nt(h2, W["wt_cd_in"], "mm_cd_in")
    cat_cd, c0, c1, dd, yv = _cd_fwd(pcd, W["conv_c_w32"], W["conv_c_b"], W["c_ln_g"], W["c_ln_b"], W["conv_d_w8"])
    x3, h3 = _mm_nn(cat_cd, W["w_cd_out"], "mm_cd_out", mode="rms", resid=x2, gain=W["ffn_norm_g"][1:2])
    pf1, act1 = _ffn_in(h3, W["wt_ffn_in1"], "ffn_in1")
    dy, dyb, loss_cols = _mm_nn(act1, W["w_ffn_down1"], "mm_ffn_down1", mode="loss", resid=x3, tgt=tgt)

    def ffn_bwd(xin, h, pf, act, dres, dresb, layer):
        G[f"w_ffn_down{layer}"] = _mm_tn(act, dresb, f"mm_g_ffn_down{layer}")
        dep = on_grad(f"w_ffn_down{layer}", G[f"w_ffn_down{layer}"])
        dpf = _ffn_dact(dresb, W[f"w_ffn_down{layer}"], pf, f"ffn_dact{layer}", dep=dep)
        G[f"wt_ffn_in{layer}"] = _mm_tn(dpf, h, f"mm_g_ffn_in{layer}")
        dep = on_grad(f"wt_ffn_in{layer}", G[f"wt_ffn_in{layer}"])
        dx, dxb, G[f"ffn_norm_g{layer}"] = _mm_nn(
            dpf, W[f"wt_ffn_in{layer}"], f"mm_d_h_ffn{layer}", mode="rms_bwd", x=xin,
            gain=W["ffn_norm_g"][layer:layer + 1], dres=dres, dep=dep)
        return dx, dxb

    dx3, dx3b = ffn_bwd(x3, h3, pf1, act1, dy, dyb, 1)

    G["w_cd_out"] = _mm_tn(cat_cd, dx3b, "mm_g_cd_out")
    dep = on_grad("w_cd_out", G["w_cd_out"])
    dcat_cd = _mm_nt(dx3b, W["w_cd_out"], "mm_d_cat_cd", dep=dep)
    dc1, dy3, dgb, G["c_ln_g"], G["c_ln_b"], G["conv_c_b"] = _cd_bwd_pw(dcat_cd, c1, pcd, yv, W["c_ln_g"], W["c_ln_b"])
    dpcd, G["conv_c_w32"], G["conv_d_w8"] = _cd_bwd_conv(pcd, dc1, dy3, c0, dd, dgb, W["conv_c_w32"], W["conv_d_w8"])
    G["wt_cd_in"] = _mm_tn(dpcd, h2, "mm_g_cd_in")
    dep = on_grad("wt_cd_in", G["wt_cd_in"])
    dx2, dx2b, G["cd_norm_g"] = _mm_nn(dpcd, W["wt_cd_in"], "mm_d_h_cd", mode="rms_bwd", x=x2, gain=W["cd_norm_g"],
                                       dres=dx3, dep=dep)

    dx1, dx1b = ffn_bwd(x1, h1, pf0, act0, dx2, dx2b, 0)

    G["w_ab_out"] = _mm_tn(cat_ab, dx1b, "mm_g_ab_out")
    dep = on_grad("w_ab_out", G["w_ab_out"])
    dcat_ab = _mm_nt(dx1b, W["w_ab_out"], "mm_d_cat_ab", dep=dep)
    dbp, e = _b_pre_bwd(dcat_ab, cat_ab)
    dqkv = []
    for g, rate in enumerate(DIL_RATES):
        dqkv += _attn_bwd(qkv[3 * g], qkv[3 * g + 1], qkv[3 * g + 2], dbp, e, lse, rate, f"attn_bwd_{g}")
    dpab, G["sgu_w"], dbias_part, G["sgu_norm_g"], G["sgu_norm_b"], dgain = _ab_in_bwd(
        pab, dcat_ab, W["sgu_norm_g"], W["sgu_norm_b"], W["sgu_w"], bias3, qg, kg, tabs, dqkv)
    G["sgu_bias"] = jnp.sum(dbias_part, axis=-1)
    dgain = dgain[0:6, 0:HEAD] + dgain[0:6, HEAD:PAIR]
    G["q_norm_g"] = dgain[0::2]
    G["k_norm_g"] = dgain[1::2]
    G["wt_ab_in"] = _mm_tn(dpab, h0, "mm_g_ab_in")
    dep = on_grad("wt_ab_in", G["wt_ab_in"])
    grad_x, _, G["ab_norm_g"] = _mm_nn(dpab, W["wt_ab_in"], "mm_d_h_ab", mode="rms_bwd", x=x, gain=W["ab_norm_g"],
                                       dres=dx1, dep=dep)
    return loss_cols, grad_x, G


def _my_place():
    return lax.axis_index("x"), lax.axis_index("y"), lax.axis_index("c")


def _dev_index(px, py, pc):
    return 4 * px + 2 * py + pc


def _flip(place, k):
    x, y, c = place
    return (1 - x if k & 4 else x, 1 - y if k & 2 else y, 1 - c if k & 1 else c)


def _gather_first(ab_in_t, small):
    out_shape = [_sds((NDEV,) + ab_in_t.shape, BF16), _sds((NDEV,) + small.shape, F32)]
    n_items = 2

    def body(ab_in_r, small_r, o_ab_in, o_small, send_sems, recv_sems, loc_sems):
        x, y, c = _my_place()
        me = (x, y, c)
        sib = (x, y, 1 - c)
        chips = [(1 - x, y), (x, 1 - y), (1 - x, 1 - y)]
        items = [(ab_in_r, lambda d: o_ab_in.at[d]), (small_r, lambda d: o_small.at[d])]

        def rcopy(it, k, src, dst, to):
            return pltpu.make_async_remote_copy(src_ref=src, dst_ref=dst, send_sem=send_sems.at[it, k],
                                                recv_sem=recv_sems.at[it, k], device_id=to, device_id_type=MESH)

        started = []
        locals_ = []
        for it, (src, dst) in enumerate(items):
            mine = dst(_dev_index(*me))
            lc = pltpu.make_async_copy(src, mine, loc_sems.at[it])
            lc.start()
            locals_.append(lc)
            first = [rcopy(it, 0, src, mine, sib)]
            first += [rcopy(it, 1 + j, src, mine, (*chip, c)) for j, chip in enumerate(chips)]
            for cp in first:
                cp.start()
            started += first
        for it, (src, dst) in enumerate(items):
            for j, chip in enumerate(chips):
                blk = dst(_dev_index(*chip, c))
                rcopy(it, 1 + j, blk, blk, me).wait_recv()
                fwd = rcopy(it, 4 + j, blk, blk, sib)
                fwd.start()
                started.append(fwd)
        for it, (src, dst) in enumerate(items):
            blk = dst(_dev_index(x, y, 1 - c))
            rcopy(it, 0, blk, blk, me).wait_recv()
            for j, chip in enumerate(chips):
                blk = dst(_dev_index(*chip, 1 - c))
                rcopy(it, 4 + j, blk, blk, me).wait_recv()
        for cp in started:
            cp.wait_send()
        for lc in locals_:
            lc.wait()

    return pl.pallas_call(
        body, name="gather_first", in_specs=[HBM_SPEC] * 2, out_specs=[HBM_SPEC] * 2, out_shape=out_shape,
        scratch_shapes=[pltpu.SemaphoreType.DMA((n_items, 7)), pltpu.SemaphoreType.DMA((n_items, 7)),
                        pltpu.SemaphoreType.DMA((n_items,))],
    )(ab_in_t, small)


HBM_ONLY = pl.BlockSpec(memory_space=pltpu.HBM)
SEM_SPEC = pl.BlockSpec(memory_space=pltpu.SEMAPHORE)
IN_FLIGHT = pltpu.CompilerParams(has_side_effects=pltpu.SideEffectType.DATAFLOW_SIDE_EFFECTING)


def _in_hbm(a):
    return pltpu.with_memory_space_constraint(a, pltpu.HBM)


def _exchange_start(name, srcs, land_shapes, items, dep=None, land_dtype=BF16):
    ns, nl, ni = len(srcs), len(land_shapes), len(items)
    lands = [lax.empty(s, land_dtype) for s in land_shapes]

    def body(*refs):
        S, L = refs[0:ns], refs[ns:ns + nl]
        first_out = ns + nl + (0 if dep is None else 1)
        send_sems, recv_sems, token = refs[first_out], refs[first_out + 1], refs[-1]
        me = _my_place()
        mi = _dev_index(*me)
        for i, (src, dst) in enumerate(items):
            for k in range(1, NDEV):
                peer = _flip(me, k)
                pltpu.make_async_remote_copy(
                    src_ref=src(S, _dev_index(*peer)), dst_ref=dst(L, mi), send_sem=send_sems.at[7 * i + k - 1],
                    recv_sem=recv_sems.at[7 * i + k - 1], device_id=peer, device_id_type=MESH).start()
        token[...] = jnp.zeros_like(token)

    thru = [pltpu.HBM(a.shape, a.dtype) for a in srcs] + [pltpu.HBM(s, land_dtype) for s in land_shapes]
    args = [_in_hbm(a) for a in srcs] + [_in_hbm(a) for a in lands]
    in_specs = [HBM_ONLY] * (ns + nl)
    if dep is not None:
        args.append(dep)
        in_specs.append(HBM_SPEC)
    outs = pl.pallas_call(
        body, name=name, in_specs=in_specs,
        out_shape=(pltpu.SemaphoreType.DMA((7 * ni,)), pltpu.SemaphoreType.DMA((7 * ni,)), *thru, _sds((8, 128), F32)),
        out_specs=(SEM_SPEC, SEM_SPEC, *[HBM_ONLY] * (ns + nl), pl.BlockSpec(memory_space=pltpu.VMEM)),
        input_output_aliases={j: 2 + j for j in range(ns + nl)}, compiler_params=IN_FLIGHT)(*args)
    return dict(send=outs[0], recv=outs[1], srcs=list(outs[2:2 + ns]), lands=list(outs[2 + ns:2 + ns + nl]),
                token=outs[-1], items=items)


def _exchange_wait(name, states, after):
    counts = [(len(st["srcs"]), len(st["lands"]), len(st["items"])) for st in states]
    n_items = sum(c[2] for c in counts)
    n_arrays = sum(c[0] + c[1] for c in counts)

    def body(*refs):
        loc_sems = refs[-1]
        me = _my_place()
        mi = _dev_index(*me)
        pos = 0
        sem_pos = n_arrays
        it = 0
        locals_ = []
        for st, (ns, nl, ni) in zip(states, counts):
            S, L = refs[pos:pos + ns], refs[pos + ns:pos + ns + nl]
            send_sems, recv_sems = refs[sem_pos], refs[sem_pos + 1]
            pos += ns + nl
            sem_pos += 2
            for i, (src, dst) in enumerate(st["items"]):
                lc = pltpu.make_async_copy(src(S, mi), dst(L, mi), loc_sems.at[it])
                lc.start()
                locals_.append(lc)
                it += 1
                for k in range(1, NDEV):
                    cp = pltpu.make_async_remote_copy(
                        src_ref=src(S, mi), dst_ref=dst(L, mi), send_sem=send_sems.at[7 * i + k - 1],
                        recv_sem=recv_sems.at[7 * i + k - 1], device_id=me, device_id_type=MESH)
                    cp.wait_send()
                    cp.wait_recv()
        for lc in locals_:
            lc.wait()

    arrays, sems = [], []
    for st in states:
        arrays += st["srcs"] + st["lands"]
        sems += [st["send"], st["recv"]]
    outs = pl.pallas_call(
        body, name=name, in_specs=[HBM_ONLY] * n_arrays + [SEM_SPEC] * len(sems) + [HBM_SPEC],
        out_shape=tuple(pltpu.HBM(a.shape, a.dtype) for a in arrays), out_specs=tuple([HBM_ONLY] * n_arrays),
        input_output_aliases={j: j for j in range(n_arrays)},
        scratch_shapes=[pltpu.SemaphoreType.DMA((n_items,))], compiler_params=IN_FLIGHT)(*arrays, *sems, after)
    lands, pos = [], 0
    for ns, nl, _ in counts:
        lands.append(list(outs[pos + ns:pos + ns + nl]))
        pos += ns + nl
    return lands


def _place_and_neighbours():
    x, y, c = _my_place()
    return (x, y, c), (x, y, 1 - c), [(1 - x, y), (x, 1 - y), (1 - x, 1 - y)]


def _gather_start(name, srcs, land_shapes, items, dep=None):
    ns, nl, ni = len(srcs), len(land_shapes), len(items)
    lands = [lax.empty(s, BF16) for s in land_shapes]

    def body(*refs):
        S, L = refs[0:ns], refs[ns:ns + nl]
        first_out = ns + nl + (0 if dep is None else 1)
        send_sems, recv_sems, token = refs[first_out], refs[first_out + 1], refs[-1]
        me, sib, chips = _place_and_neighbours()
        mi = _dev_index(*me)
        for i, (src, dst) in enumerate(items):
            for k, to in enumerate([sib] + [(*chip, me[2]) for chip in chips]):
                pltpu.make_async_remote_copy(
                    src_ref=src(S), dst_ref=dst(L, mi), send_sem=send_sems.at[4 * i + k],
                    recv_sem=recv_sems.at[4 * i + k], device_id=to, device_id_type=MESH).start()
        token[...] = jnp.zeros_like(token)

    thru = [pltpu.HBM(a.shape, a.dtype) for a in srcs] + [pltpu.HBM(s, BF16) for s in land_shapes]
    args = [_in_hbm(a) for a in srcs] + [_in_hbm(a) for a in lands]
    in_specs = [HBM_ONLY] * (ns + nl)
    if dep is not None:
        args.append(dep)
        in_specs.append(HBM_SPEC)
    outs = pl.pallas_call(
        body, name=name, in_specs=in_specs,
        out_shape=(pltpu.SemaphoreType.DMA((4 * ni,)), pltpu.SemaphoreType.DMA((4 * ni,)), *thru, _sds((8, 128), F32)),
        out_specs=(SEM_SPEC, SEM_SPEC, *[HBM_ONLY] * (ns + nl), pl.BlockSpec(memory_space=pltpu.VMEM)),
        input_output_aliases={j: 2 + j for j in range(ns + nl)}, compiler_params=IN_FLIGHT)(*args)
    return dict(send=outs[0], recv=outs[1], srcs=list(outs[2:2 + ns]), lands=list(outs[2 + ns:2 + ns + nl]),
                token=outs[-1], items=items)


def _gather_forward(name, st, after):
    nl, ni = len(st["lands"]), len(st["items"])

    def body(*refs):
        L, recv_sems = refs[0:nl], refs[nl]
        fwd_send, fwd_recv, token = refs[2 * nl + 2], refs[2 * nl + 3], refs[-1]
        me, sib, chips = _place_and_neighbours()
        for i, (_, dst) in enumerate(st["items"]):
            for j, chip in enumerate(chips):
                blk = dst(L, _dev_index(*chip, me[2]))
                pltpu.make_async_remote_copy(
                    src_ref=blk, dst_ref=blk, send_sem=fwd_send.at[3 * i + j], recv_sem=recv_sems.at[4 * i + 1 + j],
                    device_id=me, device_id_type=MESH).wait_recv()
                pltpu.make_async_remote_copy(
                    src_ref=blk, dst_ref=blk, send_sem=fwd_send.at[3 * i + j], recv_sem=fwd_recv.at[3 * i + j],
                    device_id=sib, device_id_type=MESH).start()
        token[...] = jnp.zeros_like(token)

    outs = pl.pallas_call(
        body, name=name, in_specs=[HBM_ONLY] * nl + [SEM_SPEC, HBM_SPEC],
        out_shape=(*[pltpu.HBM(a.shape, a.dtype) for a in st["lands"]], pltpu.SemaphoreType.DMA((3 * ni,)),
                   pltpu.SemaphoreType.DMA((3 * ni,)), _sds((8, 128), F32)),
        out_specs=(*[HBM_ONLY] * nl, SEM_SPEC, SEM_SPEC, pl.BlockSpec(memory_space=pltpu.VMEM)),
        input_output_aliases={j: j for j in range(nl)}, compiler_params=IN_FLIGHT)(*st["lands"], st["recv"], after)
    return dict(st, lands=list(outs[0:nl]), fwd_send=outs[nl], fwd_recv=outs[nl + 1], token=outs[-1])


def _gather_wait(name, st, after):
    ns, nl, ni = len(st["srcs"]), len(st["lands"]), len(st["items"])

    def body(*refs):
        S, L = refs[0:ns], refs[ns:ns + nl]
        send_sems, recv_sems, fwd_send, fwd_recv = refs[ns + nl:ns + nl + 4]
        loc_sems = refs[-1]
        me, sib, chips = _place_and_neighbours()
        mi = _dev_index(*me)
        locals_ = []
        for i, (src, dst) in enumerate(st["items"]):
            lc = pltpu.make_async_copy(src(S), dst(L, mi), loc_sems.at[i])
            lc.start()
            locals_.append(lc)
            mine = dst(L, mi)
            for k in range(4):
                pltpu.make_async_remote_copy(
                    src_ref=src(S), dst_ref=mine, send_sem=send_sems.at[4 * i + k], recv_sem=recv_sems.at[4 * i + k],
                    device_id=me, device_id_type=MESH).wait_send()
            pltpu.make_async_remote_copy(
                src_ref=src(S), dst_ref=mine, send_sem=send_sems.at[4 * i], recv_sem=recv_sems.at[4 * i],
                device_id=me, device_id_type=MESH).wait_recv()
            for j in range(3):
                cp = pltpu.make_async_remote_copy(
                    src_ref=mine, dst_ref=mine, send_sem=fwd_send.at[3 * i + j], recv_sem=fwd_recv.at[3 * i + j],
                    device_id=me, device_id_type=MESH)
                cp.wait_send()
                cp.wait_recv()
        for lc in locals_:
            lc.wait()

    arrays = st["srcs"] + st["lands"]
    outs = pl.pallas_call(
        body, name=name, in_specs=[HBM_ONLY] * (ns + nl) + [SEM_SPEC] * 4 + [HBM_SPEC],
        out_shape=tuple(pltpu.HBM(a.shape, a.dtype) for a in arrays), out_specs=tuple([HBM_ONLY] * (ns + nl)),
        input_output_aliases={j: j for j in range(ns + nl)}, scratch_shapes=[pltpu.SemaphoreType.DMA((ni,))],
        compiler_params=IN_FLIGHT)(*arrays, st["send"], st["recv"], st["fwd_send"], st["fwd_recv"], after)
    return list(outs[ns:ns + nl])


def _sum_slots(land):
    def body(l_ref, o_ref):
        acc = l_ref[0]
        for d in range(1, NDEV):
            acc = acc + l_ref[d]
        o_ref[...] = acc

    vm = pl.BlockSpec(memory_space=pltpu.VMEM)
    return pl.pallas_call(body, name="sum_small", out_shape=_sds(land.shape[1:], F32), in_specs=[vm], out_specs=vm)(land)


def _adam_math(w, g, m, v):
    m2 = ADAM_B1 * m + (1.0 - ADAM_B1) * g
    v2 = ADAM_B2 * v + (1.0 - ADAM_B2) * (g * g)
    delta = -ADAM_LR * ((m2 * ADAM_C1) / (jnp.sqrt(v2 * ADAM_C2) + ADAM_EPS) + ADAM_WD * w)
    return delta, m2, v2


def _adam_layer(land, sel, w, m, v, layer, name, prev=None, tc=512):
    R = land.shape[2]

    def body(l_ref, w_ref, m_ref, v_ref, *rest):
        g_out, d_out, m_out, v_out = rest[-4:]
        g = l_ref[0].astype(F32)
        for d in range(1, NDEV):
            g = g + l_ref[d].astype(F32)
        delta, m2, v2 = _adam_math(w_ref[...], g, m_ref[...], v_ref[...])
        g_out[...] = g
        d_out[...] = delta
        m_out[...] = m2
        v_out[...] = v2

    wspec = pl.BlockSpec((None, R, tc), lambda i: (layer, 0, i))
    in_specs = [pl.BlockSpec((None, NDEV, R, tc), lambda i: (sel, 0, 0, i)), wspec, wspec, wspec]
    args = [land, w, m, v]
    aliases = {}
    if prev is not None:
        in_specs += [HBM_SPEC] * 4
        args += list(prev)
        aliases = {4 + j: j for j in range(4)}
    return pl.pallas_call(
        body, name=name, grid=(D // tc,), in_specs=in_specs, out_specs=[wspec] * 4,
        out_shape=[_sds(w.shape, F32)] * 4, input_output_aliases=aliases, compiler_params=_cparams(1))(*args)


def _adam_stacked(lands, sel, w, m, v, name):
    res = None
    for layer, land in enumerate(lands):
        res = _adam_layer(land, sel, w, m, v, layer, f"{name}{layer}", prev=res)
    return res


def _adam_small(ws, gs, ms, vs):
    n = len(ws)

    def body(*refs):
        w_r, g_r, m_r, v_r = refs[0:n], refs[n:2 * n], refs[2 * n:3 * n], refs[3 * n:4 * n]
        d_o, m_o, v_o = refs[4 * n:5 * n], refs[5 * n:6 * n], refs[6 * n:7 * n]
        for i in range(n):
            delta, m2, v2 = _adam_math(w_r[i][...], g_r[i][...], m_r[i][...], v_r[i][...])
            d_o[i][...] = delta
            m_o[i][...] = m2
            v_o[i][...] = v2

    vm = pl.BlockSpec(memory_space=pltpu.VMEM)
    shapes = [_sds(w.shape, F32) for w in ws]
    outs = pl.pallas_call(body, name="adam_small", in_specs=[vm] * (4 * n), out_specs=[vm] * (3 * n),
                          out_shape=shapes * 3)(*ws, *gs, *ms, *vs)
    return outs[0:n], outs[n:2 * n], outs[2 * n:3 * n]


WEIGHT_NAMES = ("ab_norm_g", "ab_w_in", "sgu_norm_g", "sgu_norm_b", "sgu_w", "sgu_bias", "q_norm_g", "k_norm_g",
                "ab_w_out", "cd_norm_g", "cd_w_in", "conv_c_w", "conv_c_b", "c_ln_g", "c_ln_b", "conv_d_w",
                "cd_w_out", "ffn_norm_g", "ffn_w_gate", "ffn_w_up", "ffn_w_down")
SMALL_2D = (("ab_norm_g", (1, 1024)), ("sgu_norm_g", (1, 512)), ("sgu_norm_b", (1, 512)), ("sgu_w", (512, 128)),
            ("sgu_bias", (4, 128)), ("q_norm_g", (3, 64)), ("k_norm_g", (3, 64)), ("cd_norm_g", (1, 128)),
            ("conv_c_w", (31, 64)), ("conv_c_b", (1, 64)), ("c_ln_g", (1, 64)), ("c_ln_b", (1, 64)),
            ("conv_d_w", (3, 64)), ("ffn_norm_g", (2, 1024)))
SHARD_C = 64


def _pack_rows(parts, rows):
    flat = jnp.concatenate([p.reshape(-1) for p in parts])
    return jnp.pad(flat, (0, rows * 128 - flat.shape[0])).reshape(rows, 128)


def kernel(x, ab_norm_g, ab_w_in, sgu_norm_g, sgu_norm_b, sgu_w, sgu_bias, q_norm_g, k_norm_g, ab_w_out, cd_norm_g, cd_w_in, conv_c_w, conv_c_b, c_ln_g, c_ln_b, conv_d_w, cd_w_out, ffn_norm_g, ffn_w_gate, ffn_w_up, ffn_w_down, loss_target, m_ab_norm_g, m_ab_w_in, m_sgu_norm_g, m_sgu_norm_b, m_sgu_w, m_sgu_bias, m_q_norm_g, m_k_norm_g, m_ab_w_out, m_cd_norm_g, m_cd_w_in, m_conv_c_w, m_conv_c_b, m_c_ln_g, m_c_ln_b, m_conv_d_w, m_cd_w_out, m_ffn_norm_g, m_ffn_w_gate, m_ffn_w_up, m_ffn_w_down, v_ab_norm_g, v_ab_w_in, v_sgu_norm_g, v_sgu_norm_b, v_sgu_w, v_sgu_bias, v_q_norm_g, v_k_norm_g, v_ab_w_out, v_cd_norm_g, v_cd_w_in, v_conv_c_w, v_conv_c_b, v_c_ln_g, v_c_ln_b, v_conv_d_w, v_cd_w_out, v_ffn_norm_g, v_ffn_w_gate, v_ffn_w_up, v_ffn_w_down):
    w = dict(zip(WEIGHT_NAMES, (ab_norm_g, ab_w_in, sgu_norm_g, sgu_norm_b, sgu_w, sgu_bias, q_norm_g, k_norm_g, ab_w_out, cd_norm_g, cd_w_in, conv_c_w, conv_c_b, c_ln_g, c_ln_b, conv_d_w, cd_w_out, ffn_norm_g, ffn_w_gate, ffn_w_up, ffn_w_down)))
    m = dict(zip(WEIGHT_NAMES, (m_ab_norm_g, m_ab_w_in, m_sgu_norm_g, m_sgu_norm_b, m_sgu_w, m_sgu_bias, m_q_norm_g, m_k_norm_g, m_ab_w_out, m_cd_norm_g, m_cd_w_in, m_conv_c_w, m_conv_c_b, m_c_ln_g, m_c_ln_b, m_conv_d_w, m_cd_w_out, m_ffn_norm_g, m_ffn_w_gate, m_ffn_w_up, m_ffn_w_down)))
    v = dict(zip(WEIGHT_NAMES, (v_ab_norm_g, v_ab_w_in, v_sgu_norm_g, v_sgu_norm_b, v_sgu_w, v_sgu_bias, v_q_norm_g, v_k_norm_g, v_ab_w_out, v_cd_norm_g, v_cd_w_in, v_conv_c_w, v_conv_c_b, v_c_ln_g, v_c_ln_b, v_conv_d_w, v_cd_w_out, v_ffn_norm_g, v_ffn_w_gate, v_ffn_w_up, v_ffn_w_down)))
    me = _dev_index(*_my_place())

    small_local = _pack_rows([w["cd_norm_g"], w["conv_c_w"], w["conv_c_b"], w["c_ln_g"], w["c_ln_b"], w["conv_d_w"]], 24)
    o_ab_in, o_small = _gather_first(w["ab_w_in"][0].T.astype(BF16), small_local)
    r_ff = DFF // NDEV
    one = lambda a: (lambda S, j: S[a])
    slot = lambda b: (lambda L, s: L[b].at[s])
    slot2 = lambda b, part: (lambda L, s: L[b].at[part, s])
    shard = lambda a: (lambda S: S[a])
    gathers = {1: _gather_start(
        "gather1_start",
        [w["ab_w_out"][0].astype(BF16), w["ffn_w_gate"][0].T.astype(BF16), w["ffn_w_up"][0].T.astype(BF16),
         w["ffn_w_down"][0].astype(BF16)],
        [(NDEV, D // NDEV, D), (2, NDEV, r_ff, D), (NDEV, r_ff, D)],
        [(shard(0), slot(0)), (shard(1), slot2(1, 0)), (shard(2), slot2(1, 1)), (shard(3), slot(2))], dep=o_small)}

    def fetch(stage, after):
        if stage == "attn0":
            gathers[2] = _gather_start(
                "gather2_start",
                [w["cd_w_in"][0].T.astype(BF16), w["cd_w_out"][0].astype(BF16), w["ffn_w_gate"][1].T.astype(BF16),
                 w["ffn_w_up"][1].T.astype(BF16), w["ffn_w_down"][1].astype(BF16)],
                [(NDEV, CD_IN // NDEV, D), (NDEV, D // NDEV, D), (2, NDEV, r_ff, D), (NDEV, r_ff, D)],
                [(shard(0), slot(0)), (shard(1), slot(1)), (shard(2), slot2(2, 0)), (shard(3), slot2(2, 1)),
                 (shard(4), slot(3))], dep=after)
            return {"dep_attn1": gathers[2]["token"]}
        if stage == "attn1":
            gathers[1] = _gather_forward("gather1_forward", gathers[1], after)
            return {"dep_attn2": gathers[1]["token"]}
        if stage == "ab_out":
            l_out, l_ffn, l_down = _gather_wait("gather1_wait", gathers[1], after)
            return {"w_ab_out": l_out.reshape(D, D), "wt_ffn_in0": l_ffn.reshape(2 * DFF, D),
                    "w_ffn_down0": l_down.reshape(DFF, D)}
        if stage == "ffn_down0":
            gathers[2] = _gather_forward("gather2_forward", gathers[2], after)
            return {"dep_down0": gathers[2]["token"]}
        if stage == "cd_in":
            l_in, l_out, l_ffn, l_down = _gather_wait("gather2_wait", gathers[2], after)
            return {"wt_cd_in": l_in.reshape(CD_IN, D), "w_cd_out": l_out.reshape(D, D),
                    "wt_ffn_in1": l_ffn.reshape(2 * DFF, D), "w_ffn_down1": l_down.reshape(DFF, D)}
        return {}

    scatters = {}

    def on_grad(key, arr):
        if key.startswith("wt_ffn_in"):
            src, land = arr.reshape(2, NDEV, r_ff, D), (2, NDEV, r_ff, D)
            items = [(lambda S, j: S[0].at[0, j], slot2(0, 0)), (lambda S, j: S[0].at[1, j], slot2(0, 1))]
        else:
            rows = arr.shape[0] // NDEV
            src, land = arr.reshape(NDEV, rows, D), (1, NDEV, rows, D)
            items = [(lambda S, j: S[0].at[j], slot2(0, 0))]
        scatters[key] = _exchange_start(f"scatter_{key}_start", [src], [land], items)
        return scatters[key]["token"]

    flat = o_small.reshape(NDEV, 24 * 128)

    def chan(lo, taps):
        return flat[:, lo:lo + taps * SHARD_C].reshape(NDEV, taps, SHARD_C).transpose(1, 0, 2).reshape(taps, 512)

    W = {
        "wt_ab_in": o_ab_in.reshape(AB_IN, D), "dep0": gathers[1]["token"],
        "ab_norm_g": w["ab_norm_g"], "sgu_norm_g": w["sgu_norm_g"], "sgu_norm_b": w["sgu_norm_b"],
        "sgu_w": w["sgu_w"][0], "sgu_bias": w["sgu_bias"][0], "q_norm_g": w["q_norm_g"][0],
        "k_norm_g": w["k_norm_g"][0], "ffn_norm_g": w["ffn_norm_g"],
        "cd_norm_g": flat[:, 0:128].reshape(1, D),
        "conv_c_w32": jnp.pad(chan(128, CONV_C_TAPS), ((0, 1), (0, 0))),
        "conv_c_b": chan(2112, 1), "c_ln_g": chan(2176, 1), "c_ln_b": chan(2240, 1),
        "conv_d_w8": jnp.pad(chan(2304, CONV_D_TAPS), ((0, 8 - CONV_D_TAPS), (0, 0))),
    }

    loss_cols, grad_x, G = _local_step(x[0], loss_target[0], W, fetch, on_grad)
    loss = lax.psum(jnp.sum(loss_cols), ("x", "y", "c"))

    small_parts = [G["ab_norm_g"], G["sgu_norm_g"], G["sgu_norm_b"], G["sgu_w"], G["sgu_bias"], G["q_norm_g"],
                   G["k_norm_g"], G["cd_norm_g"], G["conv_c_w32"][:CONV_C_TAPS], G["conv_c_b"], G["c_ln_g"],
                   G["c_ln_b"], G["conv_d_w8"][:CONV_D_TAPS], G["ffn_norm_g0"], G["ffn_norm_g1"]]
    sizes = [p.size for p in small_parts]
    small_rows = 712
    small = _exchange_start("small_start", [_pack_rows(small_parts, small_rows)], [(NDEV, small_rows, 128)],
                            [(one(0), slot(0))], land_dtype=F32)
    early = ["w_ffn_down1", "wt_ffn_in1", "w_cd_out", "wt_cd_in", "w_ffn_down0", "wt_ffn_in0", "w_ab_out"]
    landed = dict(zip(early, _exchange_wait("scatter_wait_early", [scatters[k] for k in early], small["token"])))

    grads, deltas, new_m, new_v = {}, {}, {}, {}

    def put(name, res):
        grads[name], deltas[name], new_m[name], new_v[name] = res

    def adam(name, lands, sel, transposed):
        flip = (lambda a: jnp.swapaxes(a, 1, 2)) if transposed else (lambda a: a)
        res = _adam_stacked(lands, sel, flip(w[name]), flip(m[name]), flip(v[name]), f"adam_{name}")
        put(name, [flip(r) for r in res])

    ffn_in_lands = [landed["wt_ffn_in0"][0], landed["wt_ffn_in1"][0]]
    adam("cd_w_in", landed["wt_cd_in"], 0, True)
    adam("ffn_w_gate", ffn_in_lands, 0, True)
    adam("ffn_w_up", ffn_in_lands, 1, True)
    adam("ab_w_out", landed["w_ab_out"], 0, False)
    adam("cd_w_out", landed["w_cd_out"], 0, False)
    adam("ffn_w_down", [landed["w_ffn_down0"][0], landed["w_ffn_down1"][0]], 0, False)

    small_land = _exchange_wait("small_wait", [small], deltas["ffn_w_down"])[0][0]
    red = _sum_slots(small_land).reshape(-1)
    offs = [0]
    for s in sizes:
        offs.append(offs[-1] + s)
    seg = [red[offs[i]:offs[i + 1]] for i in range(len(sizes))]

    def own_channels(full, taps):
        return lax.dynamic_slice_in_dim(full.reshape(taps, 512), me * SHARD_C, SHARD_C, axis=1)

    g_small = {
        "ab_norm_g": seg[0].reshape(1, 1024), "sgu_norm_g": seg[1].reshape(1, 512), "sgu_norm_b": seg[2].reshape(1, 512),
        "sgu_w": seg[3].reshape(512, 128), "sgu_bias": seg[4].reshape(4, 128), "q_norm_g": seg[5].reshape(3, 64),
        "k_norm_g": seg[6].reshape(3, 64),
        "cd_norm_g": lax.dynamic_slice_in_dim(seg[7].reshape(1, D), me * (D // NDEV), D // NDEV, axis=1),
        "conv_c_w": own_channels(seg[8], CONV_C_TAPS), "conv_c_b": own_channels(seg[9], 1),
        "c_ln_g": own_channels(seg[10], 1), "c_ln_b": own_channels(seg[11], 1),
        "conv_d_w": own_channels(seg[12], CONV_D_TAPS),
        "ffn_norm_g": jnp.concatenate([seg[13].reshape(1, D), seg[14].reshape(1, D)], axis=0),
    }

    names2d = [n for n, _ in SMALL_2D]
    d_s, m_s, v_s = _adam_small([w[n].reshape(s) for n, s in SMALL_2D], [g_small[n] for n in names2d],
                                [m[n].reshape(s) for n, s in SMALL_2D], [v[n].reshape(s) for n, s in SMALL_2D])
    for i, n in enumerate(names2d):
        shape = w[n].shape
        grads[n], deltas[n] = g_small[n].reshape(shape), d_s[i].reshape(shape)
        new_m[n], new_v[n] = m_s[i].reshape(shape), v_s[i].reshape(shape)

    last = _exchange_wait("scatter_wait_last", [scatters["wt_ab_in"]], d_s[0])[0]
    adam("ab_w_in", last, 0, True)

    return (loss, grad_x[None], *[grads[n] for n in WEIGHT_NAMES], *[deltas[n] for n in WEIGHT_NAMES],
            *[new_m[n] for n in WEIGHT_NAMES], *[new_v[n] for n in WEIGHT_NAMES])
```

```python
import functools

import jax
import jax.numpy as jnp
from jax import lax
from jax.experimental import pallas as pl
from jax.experimental.pallas import tpu as pltpu

F32 = jnp.float32
BF16 = jnp.bfloat16

T = 4096
D = 1024
NDEV = 8
EPS = 1e-6
NEG_INF = -1e30
DFF = 2816
AB_IN = 5632
CD_IN = 2560
HEAD = 64
PAIR = 128
NPAIR = 4
NBACK = 128
DIL_RATES = (1, 4, 16)
ROPE_HALF = 8
ROPE_THETA = 500000.0
CONV_C_TAPS = 31
CONV_D_TAPS = 3
HALO = 32
ATTN_BWD_UNROLL = 2

ADAM_LR = 0.001
ADAM_B1 = 0.9
ADAM_B2 = 0.999
ADAM_EPS = 1e-08
ADAM_WD = 0.01
ADAM_STEP = 10
ADAM_C1 = 1.0 / (1.0 - ADAM_B1 ** ADAM_STEP)
ADAM_C2 = 1.0 / (1.0 - ADAM_B2 ** ADAM_STEP)

VMEM_LIMIT_MB = 48
MESH = pl.DeviceIdType.MESH
HBM_SPEC = pl.BlockSpec(memory_space=pl.ANY)


def _cparams(ngrid, vmem_mb=VMEM_LIMIT_MB):
    return pltpu.CompilerParams(dimension_semantics=("arbitrary",) * ngrid,
                                vmem_limit_bytes=vmem_mb * 1024 * 1024)


def _pick(n, options):
    for o in options:
        if n % o == 0:
            return o
    raise ValueError(f"no tile for {n} in {options}")


def _sds(shape, dtype):
    return jax.ShapeDtypeStruct(shape, dtype)


def _sigmoid(x):
    return 1.0 / (1.0 + jnp.exp(-x))


def _gelu(z):
    return 0.5 * z * (1.0 + lax.erf(z * 0.7071067811865476))


def _gelu_grad(z):
    return 0.5 * (1.0 + lax.erf(z * 0.7071067811865476)) + z * jnp.exp(-0.5 * z * z) * 0.3989422804014327


def _mm_nt(a, wt, name, out_dtype=BF16, tm=1024, dep=None):
    M, K = a.shape
    N = wt.shape[0]
    tn = _pick(N, (512, 256))

    def body(a_ref, w_ref, *rest):
        o_ref = rest[-1]
        o_ref[...] = lax.dot_general(a_ref[...], w_ref[...], (((1,), (1,)), ((), ())),
                                     preferred_element_type=F32).astype(o_ref.dtype)

    in_specs = [pl.BlockSpec((tm, K), lambda i, j: (i, 0)), pl.BlockSpec((tn, K), lambda i, j: (j, 0))]
    args = [a, wt]
    if dep is not None:
        in_specs.append(HBM_SPEC)
        args.append(dep)
    return pl.pallas_call(
        body, name=name, grid=(M // tm, N // tn), in_specs=in_specs,
        out_specs=pl.BlockSpec((tm, tn), lambda i, j: (i, j)),
        out_shape=_sds((M, N), out_dtype), compiler_params=_cparams(2))(*args)


EPI_ROWS = 256


def _mm_nn(a, w, name, mode="plain", resid=None, gain=None, tgt=None, x=None, dres=None, out_dtype=F32, dep=None):
    parts = a.shape[0] if a.ndim == 3 else 1
    M, Kp = a.shape[-2], a.shape[-1]
    N = w.shape[1]
    tk = _pick(Kp, (1408, 1280, 1024, 512))
    kper = Kp // tk
    nk = parts * kper
    tm = 512 if mode == "rms_bwd" else 1024
    n_in = 2 + sum(t is not None for t in (resid, gain, tgt, x, dres, dep))

    def body(*refs):
        a_ref, w_ref = refs[0], refs[1]
        named = dict(zip([n for n, t in (("resid", resid), ("gain", gain), ("tgt", tgt), ("x", x), ("dres", dres))
                          if t is not None], refs[2:]))
        outs, acc = refs[n_in:-1], refs[-1]
        i, k = pl.program_id(0), pl.program_id(1)

        @pl.when(k == 0)
        def _():
            acc[...] = jnp.zeros_like(acc)

        acc[...] += jnp.dot(a_ref[...], w_ref[...], preferred_element_type=F32)

        if mode in ("loss", "rms_bwd"):
            @pl.when((k == 0) & (i == 0))
            def _():
                outs[2][...] = jnp.zeros_like(outs[2])

        @pl.when(k == nk - 1)
        def _():
            for r0 in range(0, tm, EPI_ROWS):
                rows = slice(r0, r0 + EPI_ROWS)
                v = acc[rows, :]
                if resid is not None:
                    v = v + named["resid"][rows, :]
                if mode == "plain":
                    outs[0][rows, :] = v.astype(outs[0].dtype)
                elif mode == "rms":
                    outs[0][rows, :] = v
                    r = lax.rsqrt(jnp.mean(v * v, axis=-1, keepdims=True) + EPS)
                    outs[1][rows, :] = (v * r * named["gain"][...]).astype(BF16)
                elif mode == "loss":
                    d = v - named["tgt"][rows, :]
                    outs[2][...] += jnp.sum(d * d, axis=0, keepdims=True) * (0.5 / N)
                    dy = d * (1.0 / N)
                    outs[0][rows, :] = dy
                    outs[1][rows, :] = dy.astype(BF16)
                else:
                    xf = named["x"][rows, :]
                    r = lax.rsqrt(jnp.mean(xf * xf, axis=-1, keepdims=True) + EPS)
                    xhat = xf * r
                    outs[2][...] += jnp.sum(v * xhat, axis=0, keepdims=True)
                    dxh = v * named["gain"][...]
                    tot = named["dres"][rows, :] + r * (dxh - xhat * jnp.mean(dxh * xhat, axis=-1, keepdims=True))
                    outs[0][rows, :] = tot
                    outs[1][rows, :] = tot.astype(BF16)

    row = pl.BlockSpec((tm, N), lambda i, k: (i, 0))
    vec = pl.BlockSpec((1, N), lambda i, k: (0, 0))
    if a.ndim == 3:
        a_spec = pl.BlockSpec((None, tm, tk), lambda i, k: (k // kper, i, k % kper))
    else:
        a_spec = pl.BlockSpec((tm, tk), lambda i, k: (i, k))
    in_specs = [a_spec, pl.BlockSpec((tk, N), lambda i, k: (k, 0))]
    args = [a, w]
    for t, spec in ((resid, row), (gain, vec), (tgt, row), (x, row), (dres, row), (dep, HBM_SPEC)):
        if t is not None:
            in_specs.append(spec)
            args.append(t)
    if mode == "plain":
        out_specs, out_shape = [row], [_sds((M, N), out_dtype)]
    elif mode == "rms":
        out_specs, out_shape = [row, row], [_sds((M, N), F32), _sds((M, N), BF16)]
    else:
        out_specs, out_shape = [row, row, vec], [_sds((M, N), F32), _sds((M, N), BF16), _sds((1, N), F32)]
    res = pl.pallas_call(
        body, name=name, grid=(M // tm, nk), in_specs=in_specs, out_specs=out_specs, out_shape=out_shape,
        scratch_shapes=[pltpu.VMEM((tm, N), F32)], compiler_params=_cparams(2))(*args)
    return res[0] if mode == "plain" else res


def _mm_tn(a, b, name, out_dtype=BF16, tt=512):
    parts = a.shape[0] if a.ndim == 3 else 1
    Tt, Mp = a.shape[-2], a.shape[-1]
    N = b.shape[1]
    tn = _pick(Mp, (1408, 1280, 1024, 512))
    jper = Mp // tn
    nt = Tt // tt

    def body(a_ref, b_ref, o_ref, acc):
        t = pl.program_id(1)

        @pl.when(t == 0)
        def _():
            acc[...] = jnp.zeros_like(acc)

        acc[...] += lax.dot_general(a_ref[...], b_ref[...], (((0,), (0,)), ((), ())),
                                    preferred_element_type=F32)

        @pl.when(t == nt - 1)
        def _():
            o_ref[...] = acc[...].astype(o_ref.dtype)

    if a.ndim == 3:
        a_spec = pl.BlockSpec((None, tt, tn), lambda j, t: (j // jper, t, j % jper))
    else:
        a_spec = pl.BlockSpec((tt, tn), lambda j, t: (t, j))
    return pl.pallas_call(
        body, name=name, grid=(parts * jper, nt),
        in_specs=[a_spec, pl.BlockSpec((tt, N), lambda j, t: (t, 0))],
        out_specs=pl.BlockSpec((tn, N), lambda j, t: (j, 0)),
        out_shape=_sds((parts * Mp, N), out_dtype), scratch_shapes=[pltpu.VMEM((tn, N), F32)],
        compiler_params=_cparams(2))(a, b)


def _ffn_in(h, wt_in, name, tm=1024, tn=256):
    nj = DFF // tn

    def body(h_ref, wg_ref, wu_ref, p_ref, act_ref):
        nt = (((1,), (1,)), ((), ()))
        g = lax.dot_general(h_ref[...], wg_ref[...], nt, preferred_element_type=F32)
        u = lax.dot_general(h_ref[...], wu_ref[...], nt, preferred_element_type=F32)
        p_ref[0] = g.astype(BF16)
        p_ref[1] = u.astype(BF16)
        act_ref[...] = (g * _sigmoid(g) * u).astype(BF16)

    return pl.pallas_call(
        body, name=name, grid=(T // tm, nj),
        in_specs=[pl.BlockSpec((tm, D), lambda i, j: (i, 0)), pl.BlockSpec((tn, D), lambda i, j: (j, 0)),
                  pl.BlockSpec((tn, D), lambda i, j: (j + nj, 0))],
        out_specs=[pl.BlockSpec((2, tm, tn), lambda i, j: (0, i, j)), pl.BlockSpec((tm, tn), lambda i, j: (i, j))],
        out_shape=[_sds((2, T, DFF), BF16), _sds((T, DFF), BF16)], compiler_params=_cparams(2))(h, wt_in, wt_in)


def _ffn_dact(dyb, w_down, p3, name, tm=1024, tn=256, dep=None):
    def body(dy_ref, w_ref, p_ref, *rest):
        o_ref = rest[-1]
        da = lax.dot_general(dy_ref[...], w_ref[...], (((1,), (1,)), ((), ())), preferred_element_type=F32)
        g = p_ref[0].astype(F32)
        u = p_ref[1].astype(F32)
        sg = _sigmoid(g)
        o_ref[0] = (da * u * sg * (1.0 + g * (1.0 - sg))).astype(BF16)
        o_ref[1] = (da * g * sg).astype(BF16)

    pspec = pl.BlockSpec((2, tm, tn), lambda i, j: (0, i, j))
    in_specs = [pl.BlockSpec((tm, D), lambda i, j: (i, 0)), pl.BlockSpec((tn, D), lambda i, j: (j, 0)), pspec]
    args = [dyb, w_down, p3]
    if dep is not None:
        in_specs.append(HBM_SPEC)
        args.append(dep)
    return pl.pallas_call(
        body, name=name, grid=(T // tm, DFF // tn), in_specs=in_specs, out_specs=pspec,
        out_shape=_sds((2, T, DFF), BF16), compiler_params=_cparams(2))(*args)


def _rms_fwd(x, g, name, tm=512):
    def body(x_ref, g_ref, h_ref):
        xf = x_ref[...]
        r = lax.rsqrt(jnp.mean(xf * xf, axis=-1, keepdims=True) + EPS)
        h_ref[...] = (xf * r * g_ref[...]).astype(BF16)

    return pl.pallas_call(
        body, name=name, grid=(T // tm,),
        in_specs=[pl.BlockSpec((tm, D), lambda i: (i, 0)), pl.BlockSpec((1, D), lambda i: (0, 0))],
        out_specs=pl.BlockSpec((tm, D), lambda i: (i, 0)),
        out_shape=_sds((T, D), BF16), compiler_params=_cparams(1))(x, g)


def _tril_mask():
    r = lax.broadcasted_iota(jnp.int32, (128, 128), 0)
    c = lax.broadcasted_iota(jnp.int32, (128, 128), 1)
    return r >= c


def _mix_a_fwd(pab, sgu_g, sgu_b, sgu_w, sgu_bias3, tm=512):
    def body(zu_ref, zv_ref, g_ref, b_ref, w_ref, bias_ref, o_ref):
        u = _gelu(zu_ref[...].astype(F32))
        v = _gelu(zv_ref[...].astype(F32))
        mu = jnp.mean(v, axis=-1, keepdims=True)
        vc = v - mu
        rstd = lax.rsqrt(jnp.mean(vc * vc, axis=-1, keepdims=True) + EPS)
        vn = (vc * rstd * g_ref[...] + b_ref[...]).astype(BF16)
        tri = _tril_mask()
        for gi in range(4):
            wg = jnp.where(tri, w_ref[gi], 0.0).astype(BF16)
            bg = bias_ref[gi]
            for c in range(tm // 128):
                rs, cs = slice(c * 128, (c + 1) * 128), slice(gi * 128, (gi + 1) * 128)
                mixed = jnp.dot(wg, vn[rs, cs], preferred_element_type=F32) + bg
                o_ref[rs, cs] = (u[rs, cs] * mixed).astype(BF16)

    half = pl.BlockSpec((tm, 512), lambda i: (i, 0))
    return pl.pallas_call(
        body, name="mix_a_fwd", grid=(T // tm,),
        in_specs=[half, pl.BlockSpec((tm, 512), lambda i: (i, 1)),
                  pl.BlockSpec((1, 512), lambda i: (0, 0)), pl.BlockSpec((1, 512), lambda i: (0, 0)),
                  pl.BlockSpec((4, 128, 128), lambda i: (0, 0, 0)), pl.BlockSpec((4, 128, 1), lambda i: (0, 0, 0))],
        out_specs=half, out_shape=_sds((T, D), BF16), compiler_params=_cparams(1),
    )(pab, pab, sgu_g, sgu_b, sgu_w, sgu_bias3)


def _rope_tables():
    pos = jnp.arange(T, dtype=F32)
    inv_freq = ROPE_THETA ** (-jnp.arange(ROPE_HALF, dtype=F32) * 2.0 / (2 * ROPE_HALF))
    ang = pos[:, None] * inv_freq[None, :]
    cos, sin = jnp.cos(ang), jnp.sin(ang)
    z8 = jnp.zeros((T, ROPE_HALF), F32)
    rest = HEAD - 2 * ROPE_HALF
    c64 = jnp.concatenate([cos, cos, jnp.ones((T, rest), F32)], axis=1)
    s1 = jnp.concatenate([z8, sin, jnp.zeros((T, rest), F32)], axis=1)
    s2 = jnp.concatenate([-sin, z8, jnp.zeros((T, rest), F32)], axis=1)
    return jnp.tile(c64, (1, 2)), jnp.tile(s1, (1, 2)), jnp.tile(s2, (1, 2))


def _lo_mask(shape):
    return lax.broadcasted_iota(jnp.int32, shape, 1) < HEAD


def _seg_mean(x, lo):
    s_all = jnp.sum(x, axis=-1, keepdims=True)
    s_lo = jnp.sum(jnp.where(lo, x, 0.0), axis=-1, keepdims=True)
    return jnp.where(lo, s_lo, s_all - s_lo) * (1.0 / HEAD)


def _rope(n, c, s1, s2):
    return n * c + pltpu.roll(n, ROPE_HALF, 1) * s1 + pltpu.roll(n, PAIR - ROPE_HALF, 1) * s2


def _rope_t(dy, c, s1, s2):
    return dy * c - pltpu.roll(dy, PAIR - ROPE_HALF, 1) * s2 - pltpu.roll(dy, ROPE_HALF, 1) * s1


def _prep_fwd(pab, qg, kg, tabs, tm=512):
    def body(p_ref, qg_ref, kg_ref, c_ref, s1_ref, s2_ref, *outs):
        lo = _lo_mask((tm, PAIR))
        c, s1, s2 = c_ref[...], s1_ref[...], s2_ref[...]
        for g in range(3):
            qn_ref, kn_ref, v_ref = outs[3 * g:3 * g + 3]
            for p in range(NPAIR):
                for which, gains, dst in ((0, qg_ref, qn_ref), (1, kg_ref, kn_ref)):
                    col = (2 + 3 * which + g) * 512 + p * PAIR
                    xr = p_ref[:, col:col + PAIR].astype(F32)
                    rinv = lax.rsqrt(_seg_mean(xr * xr, lo) + EPS)
                    dst[p] = _rope(xr * rinv * gains[g:g + 1, :], c, s1, s2)
                col = (8 + g) * 512 + p * PAIR
                v_ref[p] = p_ref[:, col:col + PAIR].astype(F32)

    pm = pl.BlockSpec((NPAIR, tm, PAIR), lambda i: (0, i, 0))
    tab = pl.BlockSpec((tm, PAIR), lambda i: (i, 0))
    gain = pl.BlockSpec((3, PAIR), lambda i: (0, 0))
    return pl.pallas_call(
        body, name="prep_fwd", grid=(T // tm,),
        in_specs=[pl.BlockSpec((tm, AB_IN), lambda i: (i, 0)), gain, gain, tab, tab, tab],
        out_specs=[pm] * 9, out_shape=[_sds((NPAIR, T, PAIR), F32)] * 9,
        compiler_params=_cparams(1))(pab, qg, kg, *tabs)


def _res_index(it, rate):
    window = NBACK * rate
    b = it // rate
    rho = it % rate
    start = b * window + rho
    startp = jnp.maximum(start - window, rho)
    kmin = jnp.where(b > 0, 0, NBACK)
    return start, startp, kmin


def _rows(start, rate):
    if rate == 1:
        return pl.ds(pl.multiple_of(start, NBACK), NBACK)
    return pl.ds(start, NBACK, stride=rate)


def _band():
    qi = lax.broadcasted_iota(jnp.int32, (NBACK, 2 * NBACK), 0)
    kj = lax.broadcasted_iota(jnp.int32, (NBACK, 2 * NBACK), 1)
    dist = qi + NBACK - kj
    return (dist >= 0) & (dist <= NBACK), kj


def _attn_fwd(qn, kn, v, rate, name, dep=None):
    def body(q_ref, k_ref, v_ref, *rest):
        o_ref, l_ref = rest[-2:]
        lo = _lo_mask((NBACK, PAIR))
        band, kj = _band()

        def step(it, carry):
            start, startp, kmin = _res_index(it, rate)
            q = q_ref[_rows(start, rate), :]
            kcat = jnp.concatenate([k_ref[_rows(startp, rate), :], k_ref[_rows(start, rate), :]], axis=0).astype(BF16)
            vcat = jnp.concatenate([v_ref[_rows(startp, rate), :], v_ref[_rows(start, rate), :]], axis=0).astype(BF16)
            ok = band & (kj >= kmin)
            q2 = jnp.concatenate([jnp.where(lo, q, 0.0), jnp.where(lo, 0.0, q)], axis=0).astype(BF16)
            s = lax.dot_general(q2, kcat, (((1,), (1,)), ((), ())), preferred_element_type=F32) * (HEAD ** -0.5)
            s = jnp.where(jnp.concatenate([ok, ok], axis=0), s, NEG_INF)
            m = jnp.max(s, axis=-1, keepdims=True)
            pr = jnp.exp(s - m)
            l = jnp.sum(pr, axis=-1, keepdims=True)
            o2 = jnp.dot(pr.astype(BF16), vcat, preferred_element_type=F32) / l
            ls = m + jnp.log(l)
            o_ref[_rows(start, rate), :] = jnp.where(lo, o2[0:NBACK], o2[NBACK:])
            l_ref[_rows(start, rate), :] = jnp.where(lo, ls[0:NBACK], ls[NBACK:])
            return carry

        lax.fori_loop(0, T // NBACK, step, 0, unroll=4)

    pm = pl.BlockSpec((None, T, PAIR), lambda p: (p, 0, 0))
    in_specs, args = [pm, pm, pm], [qn, kn, v]
    if dep is not None:
        in_specs.append(HBM_SPEC)
        args.append(dep)
    return pl.pallas_call(
        body, name=name, grid=(NPAIR,), in_specs=in_specs, out_specs=[pm, pm],
        out_shape=[_sds((NPAIR, T, PAIR), F32)] * 2, compiler_params=_cparams(1))(*args)


def _merge_fwd(cat_ab, outs, lses, tm=512):
    def body(cat_in, o0, o1, o2, l0, l1, l2, cat_ref, lse_ref):
        del cat_in
        for p in range(NPAIR):
            a0, a1, a2 = l0[p], l1[p], l2[p]
            m = jnp.maximum(jnp.maximum(a0, a1), a2)
            w0, w1, w2 = jnp.exp(a0 - m), jnp.exp(a1 - m), jnp.exp(a2 - m)
            s = w0 + w1 + w2
            b = (w0 * o0[p] + w1 * o1[p] + w2 * o2[p]) / s
            cat_ref[:, p * PAIR:(p + 1) * PAIR] = b.astype(BF16)
            lse_ref[p] = m + jnp.log(s)

    pm = pl.BlockSpec((NPAIR, tm, PAIR), lambda i: (0, i, 0))
    return pl.pallas_call(
        body, name="merge_fwd", grid=(T // tm,),
        in_specs=[pl.BlockSpec(memory_space=pl.ANY)] + [pm] * 6,
        out_specs=[pl.BlockSpec((tm, 512), lambda i: (i, 1)), pm],
        out_shape=[_sds((T, D), BF16), _sds((NPAIR, T, PAIR), F32)],
        input_output_aliases={0: 0}, compiler_params=_cparams(1))(cat_ab, *outs, *lses)


def _b_pre_bwd(dcat, cat, tm=512):
    def body(db_ref, b_ref, dbp_ref, e_ref):
        lo = _lo_mask((tm, PAIR))
        for p in range(NPAIR):
            db = db_ref[:, p * PAIR:(p + 1) * PAIR].astype(F32)
            b = b_ref[:, p * PAIR:(p + 1) * PAIR].astype(F32)
            dbp_ref[p] = db
            e_ref[p] = _seg_mean(db * b, lo) * float(HEAD)

    pm = pl.BlockSpec((NPAIR, tm, PAIR), lambda i: (0, i, 0))
    right = pl.BlockSpec((tm, 512), lambda i: (i, 1))
    return pl.pallas_call(
        body, name="b_pre_bwd", grid=(T // tm,), in_specs=[right, right], out_specs=[pm, pm],
        out_shape=[_sds((NPAIR, T, PAIR), F32)] * 2, compiler_params=_cparams(1))(dcat, cat)


def _attn_bwd(qn, kn, v, dbp, e, lse, rate, name):
    def body(q_ref, k_ref, v_ref, db_ref, e_ref, lse_ref, dq_ref, dk_ref, dv_ref):
        lo = _lo_mask((NBACK, PAIR))
        band, kj = _band()
        scale = HEAD ** -0.5
        nt = (((1,), (1,)), ((), ()))
        tn = (((0,), (0,)), ((), ()))
        window = NBACK * rate
        nblk = T // window

        def one(it, carry):
            dk_carry, dv_carry = carry
            rho = it // nblk
            b = it % nblk
            start = b * window + rho
            rq = _rows(start, rate)
            rp = _rows(jnp.maximum(start - window, rho), rate)
            kmin = jnp.where(b > 0, 0, NBACK)
            q = q_ref[rq, :]
            db = db_ref[rq, :]
            ev = e_ref[rq, :]
            ls = lse_ref[rq, :]
            kcat = jnp.concatenate([k_ref[rp, :], k_ref[rq, :]], axis=0).astype(BF16)
            vcat = jnp.concatenate([v_ref[rp, :], v_ref[rq, :]], axis=0).astype(BF16)
            ok = band & (kj >= kmin)
            ok2 = jnp.concatenate([ok, ok], axis=0)
            q2 = jnp.concatenate([jnp.where(lo, q, 0.0), jnp.where(lo, 0.0, q)], axis=0).astype(BF16)
            db2 = jnp.concatenate([jnp.where(lo, db, 0.0), jnp.where(lo, 0.0, db)], axis=0).astype(BF16)
            ls2 = jnp.concatenate([ls[:, 0:1], ls[:, HEAD:HEAD + 1]], axis=0)
            ev2 = jnp.concatenate([ev[:, 0:1], ev[:, HEAD:HEAD + 1]], axis=0)
            s = lax.dot_general(q2, kcat, nt, preferred_element_type=F32) * scale
            s = jnp.where(ok2, s, NEG_INF)
            pt = jnp.exp(s - ls2)
            dp = lax.dot_general(db2, vcat, nt, preferred_element_type=F32)
            ds = (pt * (dp - ev2)).astype(BF16)
            dq2 = jnp.dot(ds, kcat, preferred_element_type=F32) * scale
            dkc = lax.dot_general(ds, q2, tn, preferred_element_type=F32) * scale
            dvc = lax.dot_general(pt.astype(BF16), db2, tn, preferred_element_type=F32)
            dq_ref[rq, :] = jnp.where(lo, dq2[0:NBACK], dq2[NBACK:])
            dk_ref[rp, :] = dk_carry + dkc[0:NBACK]
            dk_ref[rq, :] = dkc[NBACK:]
            dv_ref[rp, :] = dv_carry + dvc[0:NBACK]
            dv_ref[rq, :] = dvc[NBACK:]
            return dkc[NBACK:], dvc[NBACK:]

        def step(i, carry):
            for u in range(ATTN_BWD_UNROLL):
                carry = one(i * ATTN_BWD_UNROLL + u, carry)
            return carry

        zero = jnp.zeros((NBACK, PAIR), F32)
        lax.fori_loop(0, T // NBACK // ATTN_BWD_UNROLL, step, (zero, zero))

    pm = pl.BlockSpec((None, T, PAIR), lambda p: (p, 0, 0))
    return pl.pallas_call(
        body, name=name, grid=(NPAIR,), in_specs=[pm] * 6, out_specs=[pm] * 3,
        out_shape=[_sds((NPAIR, T, PAIR), F32)] * 3, compiler_params=_cparams(1, 56))(qn, kn, v, dbp, e, lse)


def _ab_in_bwd(pab, dcat, sgu_g, sgu_b, sgu_w, sgu_bias3, qg, kg, tabs, dqkv, tm=256):
    def body(p_ref, dcat_ref, g_ref, b_ref, w_ref, bias_ref, qg_ref, kg_ref, c_ref, s1_ref, s2_ref, *rest):
        dq_refs = rest[0:9]
        o_ref, dwm_ref, dbias_ref, dsg_ref, dsb_ref, dgain_ref = rest[9:]
        i = pl.program_id(0)

        @pl.when(i == 0)
        def _():
            dwm_ref[...] = jnp.zeros_like(dwm_ref)
            dbias_ref[...] = jnp.zeros_like(dbias_ref)
            dsg_ref[...] = jnp.zeros_like(dsg_ref)
            dsb_ref[...] = jnp.zeros_like(dsb_ref)
            dgain_ref[...] = jnp.zeros_like(dgain_ref)

        zu = p_ref[:, 0:512].astype(F32)
        zv = p_ref[:, 512:1024].astype(F32)
        u = _gelu(zu)
        v = _gelu(zv)
        mu = jnp.mean(v, axis=-1, keepdims=True)
        vc = v - mu
        rstd = lax.rsqrt(jnp.mean(vc * vc, axis=-1, keepdims=True) + EPS)
        xhat = vc * rstd
        vn = (xhat * g_ref[...] + b_ref[...]).astype(BF16)
        da = dcat_ref[...].astype(F32)
        tri = _tril_mask()
        du_parts = [[None] * 4 for _ in range(tm // 128)]
        dvn_parts = [[None] * 4 for _ in range(tm // 128)]
        for gi in range(4):
            wg = jnp.where(tri, w_ref[gi], 0.0).astype(BF16)
            bg = bias_ref[gi]
            for c in range(tm // 128):
                rs, cs = slice(c * 128, (c + 1) * 128), slice(gi * 128, (gi + 1) * 128)
                vblk = vn[rs, cs]
                mixed = jnp.dot(wg, vblk, preferred_element_type=F32) + bg
                dab = da[rs, cs]
                du_parts[c][gi] = dab * mixed
                dmixed = dab * u[rs, cs]
                dmb = dmixed.astype(BF16)
                dvn_parts[c][gi] = lax.dot_general(wg, dmb, (((0,), (0,)), ((), ())), preferred_element_type=F32)
                dwm = lax.dot_general(dmb, vblk, (((1,), (1,)), ((), ())), preferred_element_type=F32)
                dwm_ref[gi] += jnp.where(tri, dwm, 0.0)
                dbias_ref[gi] += dmixed
        du = jnp.concatenate([jnp.concatenate(r, axis=1) for r in du_parts], axis=0)
        dvn = jnp.concatenate([jnp.concatenate(r, axis=1) for r in dvn_parts], axis=0)
        dsg_ref[...] += jnp.sum(dvn * xhat, axis=0, keepdims=True)
        dsb_ref[...] += jnp.sum(dvn, axis=0, keepdims=True)
        dxh = dvn * g_ref[...]
        dv = rstd * (dxh - jnp.mean(dxh, axis=-1, keepdims=True)
                     - xhat * jnp.mean(dxh * xhat, axis=-1, keepdims=True))
        o_ref[:, 0:512] = (du * _gelu_grad(zu)).astype(BF16)
        o_ref[:, 512:1024] = (dv * _gelu_grad(zv)).astype(BF16)

        lo = _lo_mask((tm, PAIR))
        c, s1, s2 = c_ref[...], s1_ref[...], s2_ref[...]
        for g in range(3):
            dq_ref, dk_ref, dv_ref = dq_refs[3 * g:3 * g + 3]
            for p in range(NPAIR):
                for which, gains, src in ((0, qg_ref, dq_ref), (1, kg_ref, dk_ref)):
                    col = (2 + 3 * which + g) * 512 + p * PAIR
                    xr = p_ref[:, col:col + PAIR].astype(F32)
                    rinv = lax.rsqrt(_seg_mean(xr * xr, lo) + EPS)
                    xh = xr * rinv
                    dn = _rope_t(src[p], c, s1, s2)
                    row = 2 * g + which
                    dgain_ref[row:row + 1, :] += jnp.sum(dn * xh, axis=0, keepdims=True)
                    dxh2 = dn * gains[g:g + 1, :]
                    dx = rinv * (dxh2 - xh * _seg_mean(dxh2 * xh, lo))
                    o_ref[:, col:col + PAIR] = dx.astype(BF16)
                col = (8 + g) * 512 + p * PAIR
                o_ref[:, col:col + PAIR] = dv_ref[p].astype(BF16)

    pm = pl.BlockSpec((NPAIR, tm, PAIR), lambda i: (0, i, 0))
    tab = pl.BlockSpec((tm, PAIR), lambda i: (i, 0))
    gain = pl.BlockSpec((3, PAIR), lambda i: (0, 0))
    vec = pl.BlockSpec((1, 512), lambda i: (0, 0))
    full = pl.BlockSpec((tm, AB_IN), lambda i: (i, 0))
    w4 = pl.BlockSpec((4, 128, 128), lambda i: (0, 0, 0))
    return pl.pallas_call(
        body, name="ab_in_bwd", grid=(T // tm,),
        in_specs=[full, pl.BlockSpec((tm, 512), lambda i: (i, 0)), vec, vec, w4,
                  pl.BlockSpec((4, 128, 1), lambda i: (0, 0, 0)), gain, gain, tab, tab, tab] + [pm] * 9,
        out_specs=[full, w4, w4, vec, vec, pl.BlockSpec((8, PAIR), lambda i: (0, 0))],
        out_shape=[_sds((T, AB_IN), BF16), _sds((4, 128, 128), F32), _sds((4, 128, 128), F32),
                   _sds((1, 512), F32), _sds((1, 512), F32), _sds((8, PAIR), F32)],
        compiler_params=_cparams(1))(pab, dcat, sgu_g, sgu_b, sgu_w, sgu_bias3, qg, kg, *tabs, *dqkv)


def _ln_stats(x):
    mu = jnp.mean(x, axis=-1, keepdims=True)
    xc = x - mu
    rstd = lax.rsqrt(jnp.mean(xc * xc, axis=-1, keepdims=True) + EPS)
    return xc * rstd, rstd


def _cd_fwd(pcd, cw, cb, lg, lb, dw, tm=512):
    per = tm // HALO

    def body(p_ref, h_ref, cw_ref, cb_ref, lg_ref, lb_ref, dw_ref, cat_ref, c0_ref, c1_ref, dd_ref, y_ref, buf, buf2):
        i = pl.program_id(0)
        live = jnp.where(i > 0, 1.0, 0.0)
        a = p_ref[:, 0:512].astype(F32)
        gt = p_ref[:, 512:1024].astype(F32)
        gb = p_ref[:, 1024:1536].astype(F32)
        gc = p_ref[:, 1536:2048].astype(F32)
        hv = p_ref[:, 2048:2560].astype(F32)
        c0 = a * _sigmoid(gt)
        dd = gc * hv
        buf[0:HALO, :] = h_ref[:, 0:512].astype(F32) * _sigmoid(h_ref[:, 512:1024].astype(F32)) * live
        buf[HALO:, :] = c0
        buf2[0:HALO, :] = h_ref[:, 1536:2048].astype(F32) * h_ref[:, 2048:2560].astype(F32) * live
        buf2[HALO:, :] = dd
        acc = jnp.broadcast_to(cb_ref[...], (tm, 512))
        for j in range(CONV_C_TAPS):
            acc = acc + cw_ref[j:j + 1, :] * buf[pl.ds(HALO - (CONV_C_TAPS - 1) + j, tm), :]
        xhat, _ = _ln_stats(acc)
        c2 = xhat * lg_ref[...] + lb_ref[...]
        y = jnp.zeros((tm, 512), F32)
        for j in range(CONV_D_TAPS):
            y = y + dw_ref[j:j + 1, :] * buf2[pl.ds(HALO - (CONV_D_TAPS - 1) + j, tm), :]
        cat_ref[:, 0:512] = (c2 * _sigmoid(c2)).astype(BF16)
        cat_ref[:, 512:1024] = (gb * y).astype(BF16)
        c0_ref[...] = c0.astype(BF16)
        c1_ref[...] = acc
        dd_ref[...] = dd.astype(BF16)
        y_ref[...] = y.astype(BF16)

    half = pl.BlockSpec((tm, 512), lambda i: (i, 0))
    vec = pl.BlockSpec((1, 512), lambda i: (0, 0))
    return pl.pallas_call(
        body, name="cd_fwd", grid=(T // tm,),
        in_specs=[pl.BlockSpec((tm, CD_IN), lambda i: (i, 0)),
                  pl.BlockSpec((HALO, CD_IN), lambda i: (jnp.maximum(i * per - 1, 0), 0)),
                  pl.BlockSpec((32, 512), lambda i: (0, 0)), vec, vec, vec, pl.BlockSpec((8, 512), lambda i: (0, 0))],
        out_specs=[pl.BlockSpec((tm, D), lambda i: (i, 0)), half, half, half, half],
        out_shape=[_sds((T, D), BF16), _sds((T, 512), BF16), _sds((T, 512), F32), _sds((T, 512), BF16),
                   _sds((T, 512), BF16)],
        scratch_shapes=[pltpu.VMEM((HALO + tm, 512), F32), pltpu.VMEM((HALO + tm, 512), F32)],
        compiler_params=_cparams(1))(pcd, pcd, cw, cb, lg, lb, dw)


def _cd_bwd_pw(dcat, c1, pcd, y, lg, lb, tm=512):
    def body(dcat_ref, c1_ref, gb_ref, y_ref, lg_ref, lb_ref, dc1_ref, dy3_ref, dgb_ref, dlg_ref, dlb_ref, dcb_ref):
        i = pl.program_id(0)

        @pl.when(i == 0)
        def _():
            dlg_ref[...] = jnp.zeros_like(dlg_ref)
            dlb_ref[...] = jnp.zeros_like(dlb_ref)
            dcb_ref[...] = jnp.zeros_like(dcb_ref)

        dc = dcat_ref[:, 0:512].astype(F32)
        ddo = dcat_ref[:, 512:1024].astype(F32)
        xhat, rstd = _ln_stats(c1_ref[...])
        c2 = xhat * lg_ref[...] + lb_ref[...]
        sg = _sigmoid(c2)
        dc2 = dc * sg * (1.0 + c2 * (1.0 - sg))
        dlg_ref[...] += jnp.sum(dc2 * xhat, axis=0, keepdims=True)
        dlb_ref[...] += jnp.sum(dc2, axis=0, keepdims=True)
        dxh = dc2 * lg_ref[...]
        dc1 = rstd * (dxh - jnp.mean(dxh, axis=-1, keepdims=True)
                      - xhat * jnp.mean(dxh * xhat, axis=-1, keepdims=True))
        dcb_ref[...] += jnp.sum(dc1, axis=0, keepdims=True)
        dc1_ref[...] = dc1
        dgb_ref[...] = (ddo * y_ref[...].astype(F32)).astype(BF16)
        dy3_ref[...] = ddo * gb_ref[...].astype(F32)

    half = pl.BlockSpec((tm, 512), lambda i: (i, 0))
    vec = pl.BlockSpec((1, 512), lambda i: (0, 0))
    return pl.pallas_call(
        body, name="cd_bwd_pw", grid=(T // tm,),
        in_specs=[pl.BlockSpec((tm, D), lambda i: (i, 0)), half, pl.BlockSpec((tm, 512), lambda i: (i, 2)), half,
                  vec, vec],
        out_specs=[half, half, half, vec, vec, vec],
        out_shape=[_sds((T, 512), F32), _sds((T, 512), F32), _sds((T, 512), BF16),
                   _sds((1, 512), F32), _sds((1, 512), F32), _sds((1, 512), F32)],
        compiler_params=_cparams(1))(dcat, c1, pcd, y, lg, lb)


def _cd_bwd_conv(pcd, dc1, dy3, c0, dd, dgb, cw, dw, tm=512):
    per = tm // HALO
    nblk = T // tm
    last32 = T // HALO - 1

    def body(p_ref, dc1_ref, dc1n_ref, dy3_ref, dy3n_ref, c0_ref, c0p_ref, dd_ref, ddp_ref, dgb_ref, cw_ref, dw_ref,
             o_ref, dcw_ref, ddw_ref, dbuf, cbuf, d3buf, ddbuf):
        i = pl.program_id(0)
        has_prev = jnp.where(i > 0, 1.0, 0.0)
        has_next = jnp.where(i < nblk - 1, 1.0, 0.0)

        @pl.when(i == 0)
        def _():
            dcw_ref[...] = jnp.zeros_like(dcw_ref)
            ddw_ref[...] = jnp.zeros_like(ddw_ref)

        dc1 = dc1_ref[...]
        dy3 = dy3_ref[...]
        dbuf[0:tm, :] = dc1
        dbuf[tm:, :] = dc1n_ref[...] * has_next
        d3buf[0:tm, :] = dy3
        d3buf[tm:, :] = dy3n_ref[...] * has_next
        cbuf[0:HALO, :] = c0p_ref[...].astype(F32) * has_prev
        cbuf[HALO:, :] = c0_ref[...].astype(F32)
        ddbuf[0:HALO, :] = ddp_ref[...].astype(F32) * has_prev
        ddbuf[HALO:, :] = dd_ref[...].astype(F32)

        dc0 = jnp.zeros((tm, 512), F32)
        for j in range(CONV_C_TAPS):
            dc0 = dc0 + cw_ref[j:j + 1, :] * dbuf[pl.ds(CONV_C_TAPS - 1 - j, tm), :]
            dcw_ref[j:j + 1, :] += jnp.sum(dc1 * cbuf[pl.ds(HALO - (CONV_C_TAPS - 1) + j, tm), :], axis=0, keepdims=True)
        ddd = jnp.zeros((tm, 512), F32)
        for j in range(CONV_D_TAPS):
            ddd = ddd + dw_ref[j:j + 1, :] * d3buf[pl.ds(CONV_D_TAPS - 1 - j, tm), :]
            ddw_ref[j:j + 1, :] += jnp.sum(dy3 * ddbuf[pl.ds(HALO - (CONV_D_TAPS - 1) + j, tm), :], axis=0, keepdims=True)

        a = p_ref[:, 0:512].astype(F32)
        gt = p_ref[:, 512:1024].astype(F32)
        gc = p_ref[:, 1536:2048].astype(F32)
        hv = p_ref[:, 2048:2560].astype(F32)
        sg = _sigmoid(gt)
        o_ref[:, 0:512] = (dc0 * sg).astype(BF16)
        o_ref[:, 512:1024] = (dc0 * a * sg * (1.0 - sg)).astype(BF16)
        o_ref[:, 1024:1536] = dgb_ref[...]
        o_ref[:, 1536:2048] = (ddd * hv).astype(BF16)
        o_ref[:, 2048:2560] = (ddd * gc).astype(BF16)

    half = pl.BlockSpec((tm, 512), lambda i: (i, 0))
    nxt = pl.BlockSpec((HALO, 512), lambda i: (jnp.minimum((i + 1) * per, last32), 0))
    prv = pl.BlockSpec((HALO, 512), lambda i: (jnp.maximum(i * per - 1, 0), 0))
    full = pl.BlockSpec((tm, CD_IN), lambda i: (i, 0))
    return pl.pallas_call(
        body, name="cd_bwd_conv", grid=(nblk,),
        in_specs=[full, half, nxt, half, nxt, half, prv, half, prv, half,
                  pl.BlockSpec((32, 512), lambda i: (0, 0)), pl.BlockSpec((8, 512), lambda i: (0, 0))],
        out_specs=[full, pl.BlockSpec((32, 512), lambda i: (0, 0)), pl.BlockSpec((8, 512), lambda i: (0, 0))],
        out_shape=[_sds((T, CD_IN), BF16), _sds((32, 512), F32), _sds((8, 512), F32)],
        scratch_shapes=[pltpu.VMEM((tm + HALO, 512), F32), pltpu.VMEM((HALO + tm, 512), F32),
                        pltpu.VMEM((tm + HALO, 512), F32), pltpu.VMEM((HALO + tm, 512), F32)],
        compiler_params=_cparams(1))(pcd, dc1, dc1, dy3, dy3, c0, c0, dd, dd, dgb, cw, dw)


def _local_step(x, tgt, W, fetch=None, on_grad=None):
    W = dict(W)
    if fetch is None:
        fetch = lambda stage, after: {}
    if on_grad is None:
        on_grad = lambda key, arr: None
    tabs = _rope_tables()
    qg = jnp.tile(W["q_norm_g"], (1, 2))
    kg = jnp.tile(W["k_norm_g"], (1, 2))
    bias3 = W["sgu_bias"].reshape(4, 128, 1)
    G = {}

    h0 = _rms_fwd(x, W["ab_norm_g"], "rms_fwd_ab")
    pab = _mm_nt(h0, W["wt_ab_in"], "mm_ab_in", dep=W.get("dep0"))
    cat_ab = _mix_a_fwd(pab, W["sgu_norm_g"], W["sgu_norm_b"], W["sgu_w"], bias3)
    qkv = _prep_fwd(pab, qg, kg, tabs)
    outs, lses = [], []
    for g, rate in enumerate(DIL_RATES):
        o, l = _attn_fwd(qkv[3 * g], qkv[3 * g + 1], qkv[3 * g + 2], rate, f"attn_fwd_{g}", dep=W.get(f"dep_attn{g}"))
        outs.append(o)
        lses.append(l)
        W.update(fetch(f"attn{g}", o))
    cat_ab, lse = _merge_fwd(cat_ab, outs, lses)
    W.update(fetch("ab_out", lse))
    x1, h1 = _mm_nn(cat_ab, W["w_ab_out"], "mm_ab_out", mode="rms", resid=x, gain=W["ffn_norm_g"][0:1])
    pf0, act0 = _ffn_in(h1, W["wt_ffn_in0"], "ffn_in0")
    W.update(fetch("ffn_down0", act0))
    x2, h2 = _mm_nn(act0, W["w_ffn_down0"], "mm_ffn_down0", mode="rms", resid=x1, gain=W["cd_norm_g"],
                    dep=W.get("dep_down0"))
    W.update(fetch("cd_in", h2))
    pcd = _mm_nt(h2, W["wt_cd_in"], "mm_cd_in")
    cat_cd, c0, c1, dd, yv = _cd_fwd(pcd, W["conv_c_w32"], W["conv_c_b"], W["c_ln_g"], W["c_ln_b"], W["conv_d_w8"])
    x3, h3 = _mm_nn(cat_cd, W["w_cd_out"], "mm_cd_out", mode="rms", resid=x2, gain=W["ffn_norm_g"][1:2])
    pf1, act1 = _ffn_in(h3, W["wt_ffn_in1"], "ffn_in1")
    dy, dyb, loss_cols = _mm_nn(act1, W["w_ffn_down1"], "mm_ffn_down1", mode="loss", resid=x3, tgt=tgt)

    def ffn_bwd(xin, h, pf, act, dres, dresb, layer):
        G[f"w_ffn_down{layer}"] = _mm_tn(act, dresb, f"mm_g_ffn_down{layer}")
        dep = on_grad(f"w_ffn_down{layer}", G[f"w_ffn_down{layer}"])
        dpf = _ffn_dact(dresb, W[f"w_ffn_down{layer}"], pf, f"ffn_dact{layer}", dep=dep)
        G[f"wt_ffn_in{layer}"] = _mm_tn(dpf, h, f"mm_g_ffn_in{layer}")
        dep = on_grad(f"wt_ffn_in{layer}", G[f"wt_ffn_in{layer}"])
        dx, dxb, G[f"ffn_norm_g{layer}"] = _mm_nn(
            dpf, W[f"wt_ffn_in{layer}"], f"mm_d_h_ffn{layer}", mode="rms_bwd", x=xin,
            gain=W["ffn_norm_g"][layer:layer + 1], dres=dres, dep=dep)
        return dx, dxb

    dx3, dx3b = ffn_bwd(x3, h3, pf1, act1, dy, dyb, 1)

    G["w_cd_out"] = _mm_tn(cat_cd, dx3b, "mm_g_cd_out")
    dep = on_grad("w_cd_out", G["w_cd_out"])
    dcat_cd = _mm_nt(dx3b, W["w_cd_out"], "mm_d_cat_cd", dep=dep)
    dc1, dy3, dgb, G["c_ln_g"], G["c_ln_b"], G["conv_c_b"] = _cd_bwd_pw(dcat_cd, c1, pcd, yv, W["c_ln_g"], W["c_ln_b"])
    dpcd, G["conv_c_w32"], G["conv_d_w8"] = _cd_bwd_conv(pcd, dc1, dy3, c0, dd, dgb, W["conv_c_w32"], W["conv_d_w8"])
    G["wt_cd_in"] = _mm_tn(dpcd, h2, "mm_g_cd_in")
    dep = on_grad("wt_cd_in", G["wt_cd_in"])
    dx2, dx2b, G["cd_norm_g"] = _mm_nn(dpcd, W["wt_cd_in"], "mm_d_h_cd", mode="rms_bwd", x=x2, gain=W["cd_norm_g"],
                                       dres=dx3, dep=dep)

    dx1, dx1b = ffn_bwd(x1, h1, pf0, act0, dx2, dx2b, 0)

    G["w_ab_out"] = _mm_tn(cat_ab, dx1b, "mm_g_ab_out")
    dep = on_grad("w_ab_out", G["w_ab_out"])
    dcat_ab = _mm_nt(dx1b, W["w_ab_out"], "mm_d_cat_ab", dep=dep)
    dbp, e = _b_pre_bwd(dcat_ab, cat_ab)
    dqkv = []
    for g, rate in enumerate(DIL_RATES):
        dqkv += _attn_bwd(qkv[3 * g], qkv[3 * g + 1], qkv[3 * g + 2], dbp, e, lse, rate, f"attn_bwd_{g}")
    dpab, G["sgu_w"], dbias_part, G["sgu_norm_g"], G["sgu_norm_b"], dgain = _ab_in_bwd(
        pab, dcat_ab, W["sgu_norm_g"], W["sgu_norm_b"], W["sgu_w"], bias3, qg, kg, tabs, dqkv)
    G["sgu_bias"] = jnp.sum(dbias_part, axis=-1)
    dgain = dgain[0:6, 0:HEAD] + dgain[0:6, HEAD:PAIR]
    G["q_norm_g"] = dgain[0::2]
    G["k_norm_g"] = dgain[1::2]
    G["wt_ab_in"] = _mm_tn(dpab, h0, "mm_g_ab_in")
    dep = on_grad("wt_ab_in", G["wt_ab_in"])
    grad_x, _, G["ab_norm_g"] = _mm_nn(dpab, W["wt_ab_in"], "mm_d_h_ab", mode="rms_bwd", x=x, gain=W["ab_norm_g"],
                                       dres=dx1, dep=dep)
    return loss_cols, grad_x, G


def _my_place():
    return lax.axis_index("x"), lax.axis_index("y"), lax.axis_index("c")


def _dev_index(px, py, pc):
    return 4 * px + 2 * py + pc


def _flip(place, k):
    x, y, c = place
    return (1 - x if k & 4 else x, 1 - y if k & 2 else y, 1 - c if k & 1 else c)


def _landing(shape, dtype, own):
    buf = lax.empty(shape, dtype)
    for lead, part in own:
        buf = lax.dynamic_update_slice(buf, part.reshape((1,) * len(lead) + part.shape),
                                       tuple(lead) + (0,) * part.ndim)
    return buf


def _gather_first(ab_in_t, small, me_index):
    lands = [_landing((NDEV,) + ab_in_t.shape, BF16, [((me_index,), ab_in_t)]),
             _landing((NDEV,) + small.shape, F32, [((me_index,), small)])]
    n_items = 2

    def body(ab_in_r, small_r, l_ab_in, l_small, o_ab_in, o_small, send_sems, recv_sems):
        del l_ab_in, l_small
        x, y, c = _my_place()
        me = (x, y, c)
        sib = (x, y, 1 - c)
        chips = [(1 - x, y), (x, 1 - y), (1 - x, 1 - y)]
        items = [(ab_in_r, lambda d: o_ab_in.at[d]), (small_r, lambda d: o_small.at[d])]

        def rcopy(it, k, src, dst, to):
            return pltpu.make_async_remote_copy(src_ref=src, dst_ref=dst, send_sem=send_sems.at[it, k],
                                                recv_sem=recv_sems.at[it, k], device_id=to, device_id_type=MESH)

        started = []
        for it, (src, dst) in enumerate(items):
            mine = dst(_dev_index(*me))
            first = [rcopy(it, 0, src, mine, sib)]
            first += [rcopy(it, 1 + j, src, mine, (*chip, c)) for j, chip in enumerate(chips)]
            for cp in first:
                cp.start()
            started += first
        for it, (src, dst) in enumerate(items):
            for j, chip in enumerate(chips):
                blk = dst(_dev_index(*chip, c))
                rcopy(it, 1 + j, blk, blk, me).wait_recv()
                fwd = rcopy(it, 4 + j, blk, blk, sib)
                fwd.start()
                started.append(fwd)
        for it, (src, dst) in enumerate(items):
            blk = dst(_dev_index(x, y, 1 - c))
            rcopy(it, 0, blk, blk, me).wait_recv()
            for j, chip in enumerate(chips):
                blk = dst(_dev_index(*chip, 1 - c))
                rcopy(it, 4 + j, blk, blk, me).wait_recv()
        for cp in started:
            cp.wait_send()

    return pl.pallas_call(
        body, name="gather_first", in_specs=[HBM_SPEC] * 4, out_specs=[HBM_SPEC] * 2,
        out_shape=[_sds(a.shape, a.dtype) for a in lands], input_output_aliases={2: 0, 3: 1},
        scratch_shapes=[pltpu.SemaphoreType.DMA((n_items, 7)), pltpu.SemaphoreType.DMA((n_items, 7))],
    )(ab_in_t, small, *lands)


HBM_ONLY = pl.BlockSpec(memory_space=pltpu.HBM)
SEM_SPEC = pl.BlockSpec(memory_space=pltpu.SEMAPHORE)
IN_FLIGHT = pltpu.CompilerParams(has_side_effects=pltpu.SideEffectType.DATAFLOW_SIDE_EFFECTING)


def _in_hbm(a):
    return pltpu.with_memory_space_constraint(a, pltpu.HBM)


def _exchange_start(name, srcs, lands, items, dep=None):
    ns, nl, ni = len(srcs), len(lands), len(items)

    def body(*refs):
        S, L = refs[0:ns], refs[ns:ns + nl]
        first_out = ns + nl + (0 if dep is None else 1)
        send_sems, recv_sems, token = refs[first_out], refs[first_out + 1], refs[-1]
        me = _my_place()
        mi = _dev_index(*me)
        for i, (src, dst) in enumerate(items):
            for k in range(1, NDEV):
                peer = _flip(me, k)
                pltpu.make_async_remote_copy(
                    src_ref=src(S, _dev_index(*peer)), dst_ref=dst(L, mi), send_sem=send_sems.at[7 * i + k - 1],
                    recv_sem=recv_sems.at[7 * i + k - 1], device_id=peer, device_id_type=MESH).start()
        token[...] = jnp.zeros_like(token)

    thru = [pltpu.HBM(a.shape, a.dtype) for a in list(srcs) + list(lands)]
    args = [_in_hbm(a) for a in srcs] + [_in_hbm(a) for a in lands]
    in_specs = [HBM_ONLY] * (ns + nl)
    if dep is not None:
        args.append(dep)
        in_specs.append(HBM_SPEC)
    outs = pl.pallas_call(
        body, name=name, in_specs=in_specs,
        out_shape=(pltpu.SemaphoreType.DMA((7 * ni,)), pltpu.SemaphoreType.DMA((7 * ni,)), *thru, _sds((8, 128), F32)),
        out_specs=(SEM_SPEC, SEM_SPEC, *[HBM_ONLY] * (ns + nl), pl.BlockSpec(memory_space=pltpu.VMEM)),
        input_output_aliases={j: 2 + j for j in range(ns + nl)}, compiler_params=IN_FLIGHT)(*args)
    return dict(send=outs[0], recv=outs[1], srcs=list(outs[2:2 + ns]), lands=list(outs[2 + ns:2 + ns + nl]),
                token=outs[-1], items=items)


def _exchange_wait(name, states, after):
    counts = [(len(st["srcs"]), len(st["lands"]), len(st["items"])) for st in states]
    n_arrays = sum(c[0] + c[1] for c in counts)

    def body(*refs):
        me = _my_place()
        mi = _dev_index(*me)
        pos = 0
        sem_pos = n_arrays
        for st, (ns, nl, ni) in zip(states, counts):
            S, L = refs[pos:pos + ns], refs[pos + ns:pos + ns + nl]
            send_sems, recv_sems = refs[sem_pos], refs[sem_pos + 1]
            pos += ns + nl
            sem_pos += 2
            for i, (src, dst) in enumerate(st["items"]):
                for k in range(1, NDEV):
                    cp = pltpu.make_async_remote_copy(
                        src_ref=src(S, mi), dst_ref=dst(L, mi), send_sem=send_sems.at[7 * i + k - 1],
                        recv_sem=recv_sems.at[7 * i + k - 1], device_id=me, device_id_type=MESH)
                    cp.wait_send()
                    cp.wait_recv()

    arrays, sems = [], []
    for st in states:
        arrays += st["srcs"] + st["lands"]
        sems += [st["send"], st["recv"]]
    outs = pl.pallas_call(
        body, name=name, in_specs=[HBM_ONLY] * n_arrays + [SEM_SPEC] * len(sems) + [HBM_SPEC],
        out_shape=tuple(pltpu.HBM(a.shape, a.dtype) for a in arrays), out_specs=tuple([HBM_ONLY] * n_arrays),
        input_output_aliases={j: j for j in range(n_arrays)}, compiler_params=IN_FLIGHT)(*arrays, *sems, after)
    lands, pos = [], 0
    for ns, nl, _ in counts:
        lands.append(list(outs[pos + ns:pos + ns + nl]))
        pos += ns + nl
    return lands


def _place_and_neighbours():
    x, y, c = _my_place()
    return (x, y, c), (x, y, 1 - c), [(1 - x, y), (x, 1 - y), (1 - x, 1 - y)]


def _gather_start(name, srcs, lands, items, dep=None):
    ns, nl, ni = len(srcs), len(lands), len(items)

    def body(*refs):
        S, L = refs[0:ns], refs[ns:ns + nl]
        first_out = ns + nl + (0 if dep is None else 1)
        send_sems, recv_sems, token = refs[first_out], refs[first_out + 1], refs[-1]
        me, sib, chips = _place_and_neighbours()
        mi = _dev_index(*me)
        for i, (src, dst) in enumerate(items):
            for k, to in enumerate([sib] + [(*chip, me[2]) for chip in chips]):
                pltpu.make_async_remote_copy(
                    src_ref=src(S), dst_ref=dst(L, mi), send_sem=send_sems.at[4 * i + k],
                    recv_sem=recv_sems.at[4 * i + k], device_id=to, device_id_type=MESH).start()
        token[...] = jnp.zeros_like(token)

    thru = [pltpu.HBM(a.shape, a.dtype) for a in list(srcs) + list(lands)]
    args = [_in_hbm(a) for a in srcs] + [_in_hbm(a) for a in lands]
    in_specs = [HBM_ONLY] * (ns + nl)
    if dep is not None:
        args.append(dep)
        in_specs.append(HBM_SPEC)
    outs = pl.pallas_call(
        body, name=name, in_specs=in_specs,
        out_shape=(pltpu.SemaphoreType.DMA((4 * ni,)), pltpu.SemaphoreType.DMA((4 * ni,)), *thru, _sds((8, 128), F32)),
        out_specs=(SEM_SPEC, SEM_SPEC, *[HBM_ONLY] * (ns + nl), pl.BlockSpec(memory_space=pltpu.VMEM)),
        input_output_aliases={j: 2 + j for j in range(ns + nl)}, compiler_params=IN_FLIGHT)(*args)
    return dict(send=outs[0], recv=outs[1], srcs=list(outs[2:2 + ns]), lands=list(outs[2 + ns:2 + ns + nl]),
                token=outs[-1], items=items)


def _gather_forward(name, st, after):
    nl, ni = len(st["lands"]), len(st["items"])

    def body(*refs):
        L, recv_sems = refs[0:nl], refs[nl]
        fwd_send, fwd_recv, token = refs[2 * nl + 2], refs[2 * nl + 3], refs[-1]
        me, sib, chips = _place_and_neighbours()
        for i, (_, dst) in enumerate(st["items"]):
            for j, chip in enumerate(chips):
                blk = dst(L, _dev_index(*chip, me[2]))
                pltpu.make_async_remote_copy(
                    src_ref=blk, dst_ref=blk, send_sem=fwd_send.at[3 * i + j], recv_sem=recv_sems.at[4 * i + 1 + j],
                    device_id=me, device_id_type=MESH).wait_recv()
                pltpu.make_async_remote_copy(
                    src_ref=blk, dst_ref=blk, send_sem=fwd_send.at[3 * i + j], recv_sem=fwd_recv.at[3 * i + j],
                    device_id=sib, device_id_type=MESH).start()
        token[...] = jnp.zeros_like(token)

    outs = pl.pallas_call(
        body, name=name, in_specs=[HBM_ONLY] * nl + [SEM_SPEC, HBM_SPEC],
        out_shape=(*[pltpu.HBM(a.shape, a.dtype) for a in st["lands"]], pltpu.SemaphoreType.DMA((3 * ni,)),
                   pltpu.SemaphoreType.DMA((3 * ni,)), _sds((8, 128), F32)),
        out_specs=(*[HBM_ONLY] * nl, SEM_SPEC, SEM_SPEC, pl.BlockSpec(memory_space=pltpu.VMEM)),
        input_output_aliases={j: j for j in range(nl)}, compiler_params=IN_FLIGHT)(*st["lands"], st["recv"], after)
    return dict(st, lands=list(outs[0:nl]), fwd_send=outs[nl], fwd_recv=outs[nl + 1], token=outs[-1])


def _gather_wait(name, st, after):
    ns, nl, ni = len(st["srcs"]), len(st["lands"]), len(st["items"])

    def body(*refs):
        S, L = refs[0:ns], refs[ns:ns + nl]
        send_sems, recv_sems, fwd_send, fwd_recv = refs[ns + nl:ns + nl + 4]
        me, sib, chips = _place_and_neighbours()
        mi = _dev_index(*me)
        for i, (src, dst) in enumerate(st["items"]):
            mine = dst(L, mi)
            for k in range(4):
                pltpu.make_async_remote_copy(
                    src_ref=src(S), dst_ref=mine, send_sem=send_sems.at[4 * i + k], recv_sem=recv_sems.at[4 * i + k],
                    device_id=me, device_id_type=MESH).wait_send()
            pltpu.make_async_remote_copy(
                src_ref=src(S), dst_ref=mine, send_sem=send_sems.at[4 * i], recv_sem=recv_sems.at[4 * i],
                device_id=me, device_id_type=MESH).wait_recv()
            for j in range(3):
                cp = pltpu.make_async_remote_copy(
                    src_ref=mine, dst_ref=mine, send_sem=fwd_send.at[3 * i + j], recv_sem=fwd_recv.at[3 * i + j],
                    device_id=me, device_id_type=MESH)
                cp.wait_send()
                cp.wait_recv()

    arrays = st["srcs"] + st["lands"]
    outs = pl.pallas_call(
        body, name=name, in_specs=[HBM_ONLY] * (ns + nl) + [SEM_SPEC] * 4 + [HBM_SPEC],
        out_shape=tuple(pltpu.HBM(a.shape, a.dtype) for a in arrays), out_specs=tuple([HBM_ONLY] * (ns + nl)),
        input_output_aliases={j: j for j in range(ns + nl)},
        compiler_params=IN_FLIGHT)(*arrays, st["send"], st["recv"], st["fwd_send"], st["fwd_recv"], after)
    return list(outs[ns:ns + nl])


def _sum_slots(land):
    def body(l_ref, o_ref):
        acc = l_ref[0]
        for d in range(1, NDEV):
            acc = acc + l_ref[d]
        o_ref[...] = acc

    vm = pl.BlockSpec(memory_space=pltpu.VMEM)
    return pl.pallas_call(body, name="sum_small", out_shape=_sds(land.shape[1:], F32), in_specs=[vm], out_specs=vm)(land)


def _adam_math(w, g, m, v):
    m2 = ADAM_B1 * m + (1.0 - ADAM_B1) * g
    v2 = ADAM_B2 * v + (1.0 - ADAM_B2) * (g * g)
    delta = -ADAM_LR * ((m2 * ADAM_C1) / (jnp.sqrt(v2 * ADAM_C2) + ADAM_EPS) + ADAM_WD * w)
    return delta, m2, v2


def _adam_layer(land, sel, w, m, v, layer, name, prev=None, tc=512):
    R = land.shape[2]

    def body(l_ref, w_ref, m_ref, v_ref, *rest):
        g_out, d_out, m_out, v_out = rest[-4:]
        g = l_ref[0].astype(F32)
        for d in range(1, NDEV):
            g = g + l_ref[d].astype(F32)
        delta, m2, v2 = _adam_math(w_ref[...], g, m_ref[...], v_ref[...])
        g_out[...] = g
        d_out[...] = delta
        m_out[...] = m2
        v_out[...] = v2

    wspec = pl.BlockSpec((None, R, tc), lambda i: (layer, 0, i))
    in_specs = [pl.BlockSpec((None, NDEV, R, tc), lambda i: (sel, 0, 0, i)), wspec, wspec, wspec]
    args = [land, w, m, v]
    aliases = {}
    if prev is not None:
        in_specs += [HBM_SPEC] * 4
        args += list(prev)
        aliases = {4 + j: j for j in range(4)}
    return pl.pallas_call(
        body, name=name, grid=(D // tc,), in_specs=in_specs, out_specs=[wspec] * 4,
        out_shape=[_sds(w.shape, F32)] * 4, input_output_aliases=aliases, compiler_params=_cparams(1))(*args)


def _adam_stacked(lands, sel, w, m, v, name):
    res = None
    for layer, land in enumerate(lands):
        res = _adam_layer(land, sel, w, m, v, layer, f"{name}{layer}", prev=res)
    return res


def _adam_small(ws, gs, ms, vs):
    n = len(ws)

    def body(*refs):
        w_r, g_r, m_r, v_r = refs[0:n], refs[n:2 * n], refs[2 * n:3 * n], refs[3 * n:4 * n]
        d_o, m_o, v_o = refs[4 * n:5 * n], refs[5 * n:6 * n], refs[6 * n:7 * n]
        for i in range(n):
            delta, m2, v2 = _adam_math(w_r[i][...], g_r[i][...], m_r[i][...], v_r[i][...])
            d_o[i][...] = delta
            m_o[i][...] = m2
            v_o[i][...] = v2

    vm = pl.BlockSpec(memory_space=pltpu.VMEM)
    shapes = [_sds(w.shape, F32) for w in ws]
    outs = pl.pallas_call(body, name="adam_small", in_specs=[vm] * (4 * n), out_specs=[vm] * (3 * n),
                          out_shape=shapes * 3)(*ws, *gs, *ms, *vs)
    return outs[0:n], outs[n:2 * n], outs[2 * n:3 * n]


WEIGHT_NAMES = ("ab_norm_g", "ab_w_in", "sgu_norm_g", "sgu_norm_b", "sgu_w", "sgu_bias", "q_norm_g", "k_norm_g",
                "ab_w_out", "cd_norm_g", "cd_w_in", "conv_c_w", "conv_c_b", "c_ln_g", "c_ln_b", "conv_d_w",
                "cd_w_out", "ffn_norm_g", "ffn_w_gate", "ffn_w_up", "ffn_w_down")
SMALL_2D = (("ab_norm_g", (1, 1024)), ("sgu_norm_g", (1, 512)), ("sgu_norm_b", (1, 512)), ("sgu_w", (512, 128)),
            ("sgu_bias", (4, 128)), ("q_norm_g", (3, 64)), ("k_norm_g", (3, 64)), ("cd_norm_g", (1, 128)),
            ("conv_c_w", (31, 64)), ("conv_c_b", (1, 64)), ("c_ln_g", (1, 64)), ("c_ln_b", (1, 64)),
            ("conv_d_w", (3, 64)), ("ffn_norm_g", (2, 1024)))
SHARD_C = 64


def _pack_rows(parts, rows):
    flat = jnp.concatenate([p.reshape(-1) for p in parts])
    return jnp.pad(flat, (0, rows * 128 - flat.shape[0])).reshape(rows, 128)


def kernel(x, ab_norm_g, ab_w_in, sgu_norm_g, sgu_norm_b, sgu_w, sgu_bias, q_norm_g, k_norm_g, ab_w_out, cd_norm_g, cd_w_in, conv_c_w, conv_c_b, c_ln_g, c_ln_b, conv_d_w, cd_w_out, ffn_norm_g, ffn_w_gate, ffn_w_up, ffn_w_down, loss_target, m_ab_norm_g, m_ab_w_in, m_sgu_norm_g, m_sgu_norm_b, m_sgu_w, m_sgu_bias, m_q_norm_g, m_k_norm_g, m_ab_w_out, m_cd_norm_g, m_cd_w_in, m_conv_c_w, m_conv_c_b, m_c_ln_g, m_c_ln_b, m_conv_d_w, m_cd_w_out, m_ffn_norm_g, m_ffn_w_gate, m_ffn_w_up, m_ffn_w_down, v_ab_norm_g, v_ab_w_in, v_sgu_norm_g, v_sgu_norm_b, v_sgu_w, v_sgu_bias, v_q_norm_g, v_k_norm_g, v_ab_w_out, v_cd_norm_g, v_cd_w_in, v_conv_c_w, v_conv_c_b, v_c_ln_g, v_c_ln_b, v_conv_d_w, v_cd_w_out, v_ffn_norm_g, v_ffn_w_gate, v_ffn_w_up, v_ffn_w_down):
    w = dict(zip(WEIGHT_NAMES, (ab_norm_g, ab_w_in, sgu_norm_g, sgu_norm_b, sgu_w, sgu_bias, q_norm_g, k_norm_g, ab_w_out, cd_norm_g, cd_w_in, conv_c_w, conv_c_b, c_ln_g, c_ln_b, conv_d_w, cd_w_out, ffn_norm_g, ffn_w_gate, ffn_w_up, ffn_w_down)))
    m = dict(zip(WEIGHT_NAMES, (m_ab_norm_g, m_ab_w_in, m_sgu_norm_g, m_sgu_norm_b, m_sgu_w, m_sgu_bias, m_q_norm_g, m_k_norm_g, m_ab_w_out, m_cd_norm_g, m_cd_w_in, m_conv_c_w, m_conv_c_b, m_c_ln_g, m_c_ln_b, m_conv_d_w, m_cd_w_out, m_ffn_norm_g, m_ffn_w_gate, m_ffn_w_up, m_ffn_w_down)))
    v = dict(zip(WEIGHT_NAMES, (v_ab_norm_g, v_ab_w_in, v_sgu_norm_g, v_sgu_norm_b, v_sgu_w, v_sgu_bias, v_q_norm_g, v_k_norm_g, v_ab_w_out, v_cd_norm_g, v_cd_w_in, v_conv_c_w, v_conv_c_b, v_c_ln_g, v_c_ln_b, v_conv_d_w, v_cd_w_out, v_ffn_norm_g, v_ffn_w_gate, v_ffn_w_up, v_ffn_w_down)))
    me = _dev_index(*_my_place())

    small_local = _pack_rows([w["cd_norm_g"], w["conv_c_w"], w["conv_c_b"], w["c_ln_g"], w["c_ln_b"], w["conv_d_w"]], 24)
    o_ab_in, o_small = _gather_first(w["ab_w_in"][0].T.astype(BF16), small_local, me)
    r_ff = DFF // NDEV
    one = lambda a: (lambda S, j: S[a])
    slot = lambda b: (lambda L, s: L[b].at[s])
    slot2 = lambda b, part: (lambda L, s: L[b].at[part, s])
    shard = lambda a: (lambda S: S[a])

    def layer_shards(layer):
        return (w["ffn_w_gate"][layer].T.astype(BF16), w["ffn_w_up"][layer].T.astype(BF16),
                w["ffn_w_down"][layer].astype(BF16))

    def gathered(own):
        return _landing((NDEV,) + own.shape, BF16, [((me,), own)])

    def gathered2(a, b):
        return _landing((2, NDEV) + a.shape, BF16, [((0, me), a), ((1, me), b)])

    ab_out_s = w["ab_w_out"][0].astype(BF16)
    gate0, up0, down0 = layer_shards(0)
    gathers = {1: _gather_start(
        "gather1_start", [ab_out_s, gate0, up0, down0], [gathered(ab_out_s), gathered2(gate0, up0), gathered(down0)],
        [(shard(0), slot(0)), (shard(1), slot2(1, 0)), (shard(2), slot2(1, 1)), (shard(3), slot(2))], dep=o_small)}

    def fetch(stage, after):
        if stage == "attn0":
            cd_in_s, cd_out_s = w["cd_w_in"][0].T.astype(BF16), w["cd_w_out"][0].astype(BF16)
            gate1, up1, down1 = layer_shards(1)
            gathers[2] = _gather_start(
                "gather2_start", [cd_in_s, cd_out_s, gate1, up1, down1],
                [gathered(cd_in_s), gathered(cd_out_s), gathered2(gate1, up1), gathered(down1)],
                [(shard(0), slot(0)), (shard(1), slot(1)), (shard(2), slot2(2, 0)), (shard(3), slot2(2, 1)),
                 (shard(4), slot(3))], dep=after)
            return {"dep_attn1": gathers[2]["token"]}
        if stage == "attn1":
            gathers[1] = _gather_forward("gather1_forward", gathers[1], after)
            return {"dep_attn2": gathers[1]["token"]}
        if stage == "ab_out":
            l_out, l_ffn, l_down = _gather_wait("gather1_wait", gathers[1], after)
            return {"w_ab_out": l_out.reshape(D, D), "wt_ffn_in0": l_ffn.reshape(2 * DFF, D),
                    "w_ffn_down0": l_down.reshape(DFF, D)}
        if stage == "ffn_down0":
            gathers[2] = _gather_forward("gather2_forward", gathers[2], after)
            return {"dep_down0": gathers[2]["token"]}
        if stage == "cd_in":
            l_in, l_out, l_ffn, l_down = _gather_wait("gather2_wait", gathers[2], after)
            return {"wt_cd_in": l_in.reshape(CD_IN, D), "w_cd_out": l_out.reshape(D, D),
                    "wt_ffn_in1": l_ffn.reshape(2 * DFF, D), "w_ffn_down1": l_down.reshape(DFF, D)}
        return {}

    scatters = {}

    def on_grad(key, arr):
        if key.startswith("wt_ffn_in"):
            src = arr.reshape(2, NDEV, r_ff, D)
            own = lax.dynamic_slice_in_dim(src, me, 1, axis=1)
            land = lax.dynamic_update_slice(lax.empty(src.shape, BF16), own, (0, me, 0, 0))
            items = [(lambda S, j: S[0].at[0, j], slot2(0, 0)), (lambda S, j: S[0].at[1, j], slot2(0, 1))]
        else:
            rows = arr.shape[0] // NDEV
            src = arr.reshape(NDEV, rows, D)
            land = _landing((1, NDEV, rows, D), BF16, [((0, me), lax.dynamic_index_in_dim(src, me, 0, keepdims=False))])
            items = [(lambda S, j: S[0].at[j], slot2(0, 0))]
        scatters[key] = _exchange_start(f"scatter_{key}_start", [src], [land], items)
        return scatters[key]["token"]

    flat = o_small.reshape(NDEV, 24 * 128)

    def chan(lo, taps):
        return flat[:, lo:lo + taps * SHARD_C].reshape(NDEV, taps, SHARD_C).transpose(1, 0, 2).reshape(taps, 512)

    W = {
        "wt_ab_in": o_ab_in.reshape(AB_IN, D), "dep0": gathers[1]["token"],
        "ab_norm_g": w["ab_norm_g"], "sgu_norm_g": w["sgu_norm_g"], "sgu_norm_b": w["sgu_norm_b"],
        "sgu_w": w["sgu_w"][0], "sgu_bias": w["sgu_bias"][0], "q_norm_g": w["q_norm_g"][0],
        "k_norm_g": w["k_norm_g"][0], "ffn_norm_g": w["ffn_norm_g"],
        "cd_norm_g": flat[:, 0:128].reshape(1, D),
        "conv_c_w32": jnp.pad(chan(128, CONV_C_TAPS), ((0, 1), (0, 0))),
        "conv_c_b": chan(2112, 1), "c_ln_g": chan(2176, 1), "c_ln_b": chan(2240, 1),
        "conv_d_w8": jnp.pad(chan(2304, CONV_D_TAPS), ((0, 8 - CONV_D_TAPS), (0, 0))),
    }

    loss_cols, grad_x, G = _local_step(x[0], loss_target[0], W, fetch, on_grad)
    loss = lax.psum(jnp.sum(loss_cols), ("x", "y", "c"))

    small_parts = [G["ab_norm_g"], G["sgu_norm_g"], G["sgu_norm_b"], G["sgu_w"], G["sgu_bias"], G["q_norm_g"],
                   G["k_norm_g"], G["cd_norm_g"], G["conv_c_w32"][:CONV_C_TAPS], G["conv_c_b"], G["c_ln_g"],
                   G["c_ln_b"], G["conv_d_w8"][:CONV_D_TAPS], G["ffn_norm_g0"], G["ffn_norm_g1"]]
    sizes = [p.size for p in small_parts]
    small_rows = 712
    packed = _pack_rows(small_parts, small_rows)
    small = _exchange_start("small_start", [packed], [_landing((NDEV, small_rows, 128), F32, [((me,), packed)])],
                            [(one(0), slot(0))])
    early = ["w_ffn_down1", "wt_ffn_in1", "w_cd_out", "wt_cd_in", "w_ffn_down0", "wt_ffn_in0", "w_ab_out"]
    landed = dict(zip(early, _exchange_wait("scatter_wait_early", [scatters[k] for k in early], small["token"])))

    grads, deltas, new_m, new_v = {}, {}, {}, {}

    def put(name, res):
        grads[name], deltas[name], new_m[name], new_v[name] = res

    def adam(name, lands, sel, transposed):
        flip = (lambda a: jnp.swapaxes(a, 1, 2)) if transposed else (lambda a: a)
        res = _adam_stacked(lands, sel, flip(w[name]), flip(m[name]), flip(v[name]), f"adam_{name}")
        put(name, [flip(r) for r in res])

    ffn_in_lands = [landed["wt_ffn_in0"][0], landed["wt_ffn_in1"][0]]
    adam("cd_w_in", landed["wt_cd_in"], 0, True)
    adam("ffn_w_gate", ffn_in_lands, 0, True)
    adam("ffn_w_up", ffn_in_lands, 1, True)
    adam("ab_w_out", landed["w_ab_out"], 0, False)
    adam("cd_w_out", landed["w_cd_out"], 0, False)
    adam("ffn_w_down", [landed["w_ffn_down0"][0], landed["w_ffn_down1"][0]], 0, False)

    small_land = _exchange_wait("small_wait", [small], deltas["ffn_w_down"])[0][0]
    red = _sum_slots(small_land).reshape(-1)
    offs = [0]
    for s in sizes:
        offs.append(offs[-1] + s)
    seg = [red[offs[i]:offs[i + 1]] for i in range(len(sizes))]

    def own_channels(full, taps):
        return lax.dynamic_slice_in_dim(full.reshape(taps, 512), me * SHARD_C, SHARD_C, axis=1)

    g_small = {
        "ab_norm_g": seg[0].reshape(1, 1024), "sgu_norm_g": seg[1].reshape(1, 512), "sgu_norm_b": seg[2].reshape(1, 512),
        "sgu_w": seg[3].reshape(512, 128), "sgu_bias": seg[4].reshape(4, 128), "q_norm_g": seg[5].reshape(3, 64),
        "k_norm_g": seg[6].reshape(3, 64),
        "cd_norm_g": lax.dynamic_slice_in_dim(seg[7].reshape(1, D), me * (D // NDEV), D // NDEV, axis=1),
        "conv_c_w": own_channels(seg[8], CONV_C_TAPS), "conv_c_b": own_channels(seg[9], 1),
        "c_ln_g": own_channels(seg[10], 1), "c_ln_b": own_channels(seg[11], 1),
        "conv_d_w": own_channels(seg[12], CONV_D_TAPS),
        "ffn_norm_g": jnp.concatenate([seg[13].reshape(1, D), seg[14].reshape(1, D)], axis=0),
    }

    names2d = [n for n, _ in SMALL_2D]
    d_s, m_s, v_s = _adam_small([w[n].reshape(s) for n, s in SMALL_2D], [g_small[n] for n in names2d],
                                [m[n].reshape(s) for n, s in SMALL_2D], [v[n].reshape(s) for n, s in SMALL_2D])
    for i, n in enumerate(names2d):
        shape = w[n].shape
        grads[n], deltas[n] = g_small[n].reshape(shape), d_s[i].reshape(shape)
        new_m[n], new_v[n] = m_s[i].reshape(shape), v_s[i].reshape(shape)

    last = _exchange_wait("scatter_wait_last", [scatters["wt_ab_in"]], d_s[0])[0]
    adam("ab_w_in", last, 0, True)

    return (loss, grad_x[None], *[grads[n] for n in WEIGHT_NAMES], *[deltas[n] for n in WEIGHT_NAMES],
            *[new_m[n] for n in WEIGHT_NAMES], *[new_v[n] for n in WEIGHT_NAMES])
```

```python
import functools

import jax
import jax.numpy as jnp
from jax import lax
from jax.experimental import pallas as pl
from jax.experimental.pallas import tpu as pltpu

F32 = jnp.float32
BF16 = jnp.bfloat16

T = 4096
D = 1024
NDEV = 8
EPS = 1e-6
NEG_INF = -1e30
DFF = 2816
AB_IN = 5632
CD_IN = 2560
HEAD = 64
PAIR = 128
NPAIR = 4
NBACK = 128
DIL_RATES = (1, 4, 16)
ROPE_HALF = 8
ROPE_THETA = 500000.0
CONV_C_TAPS = 31
CONV_D_TAPS = 3
HALO = 32
ATTN_BWD_UNROLL = 2

ADAM_LR = 0.001
ADAM_B1 = 0.9
ADAM_B2 = 0.999
ADAM_EPS = 1e-08
ADAM_WD = 0.01
ADAM_STEP = 10
ADAM_C1 = 1.0 / (1.0 - ADAM_B1 ** ADAM_STEP)
ADAM_C2 = 1.0 / (1.0 - ADAM_B2 ** ADAM_STEP)

VMEM_LIMIT_MB = 48
MESH = pl.DeviceIdType.MESH
HBM_SPEC = pl.BlockSpec(memory_space=pl.ANY)


def _cparams(ngrid, vmem_mb=VMEM_LIMIT_MB):
    return pltpu.CompilerParams(dimension_semantics=("arbitrary",) * ngrid,
                                vmem_limit_bytes=vmem_mb * 1024 * 1024)


def _pick(n, options):
    for o in options:
        if n % o == 0:
            return o
    raise ValueError(f"no tile for {n} in {options}")


def _sds(shape, dtype):
    return jax.ShapeDtypeStruct(shape, dtype)


def _sigmoid(x):
    return 1.0 / (1.0 + jnp.exp(-x))


def _sigmoid_bf16(x):
    return 0.5 * jnp.tanh(0.5 * x) + 0.5


def _gelu(z):
    return 0.5 * z * (1.0 + lax.erf(z * 0.7071067811865476))


def _gelu_grad(z):
    return 0.5 * (1.0 + lax.erf(z * 0.7071067811865476)) + z * jnp.exp(-0.5 * z * z) * 0.3989422804014327


def _mm_nt(a, wt, name, out_dtype=BF16, tm=2048, dep=None):
    M, K = a.shape
    N = wt.shape[0]
    tn = _pick(N, (512, 256))

    def body(a_ref, w_ref, *rest):
        o_ref = rest[-1]
        o_ref[...] = lax.dot_general(a_ref[...], w_ref[...], (((1,), (1,)), ((), ())),
                                     preferred_element_type=F32).astype(o_ref.dtype)

    in_specs = [pl.BlockSpec((tm, K), lambda i, j: (i, 0)), pl.BlockSpec((tn, K), lambda i, j: (j, 0))]
    args = [a, wt]
    if dep is not None:
        in_specs.append(HBM_SPEC)
        args.append(dep)
    return pl.pallas_call(
        body, name=name, grid=(M // tm, N // tn), in_specs=in_specs,
        out_specs=pl.BlockSpec((tm, tn), lambda i, j: (i, j)),
        out_shape=_sds((M, N), out_dtype), compiler_params=_cparams(2))(*args)


EPI_ROWS = 256


def _mm_nn(a, w, name, mode="plain", resid=None, gain=None, tgt=None, x=None, dres=None, out_dtype=F32, dep=None):
    parts = a.shape[0] if a.ndim == 3 else 1
    M, Kp = a.shape[-2], a.shape[-1]
    N = w.shape[1]
    tk = _pick(Kp, (1408, 1280, 1024, 512))
    kper = Kp // tk
    nk = parts * kper
    tm = 512 if mode == "rms_bwd" else 1024
    n_in = 2 + sum(t is not None for t in (resid, gain, tgt, x, dres, dep))

    def body(*refs):
        a_ref, w_ref = refs[0], refs[1]
        named = dict(zip([n for n, t in (("resid", resid), ("gain", gain), ("tgt", tgt), ("x", x), ("dres", dres))
                          if t is not None], refs[2:]))
        outs, acc = refs[n_in:-1], refs[-1]
        i, k = pl.program_id(0), pl.program_id(1)

        @pl.when(k == 0)
        def _():
            acc[...] = jnp.zeros_like(acc)

        acc[...] += jnp.dot(a_ref[...], w_ref[...], preferred_element_type=F32)

        if mode in ("loss", "rms_bwd"):
            @pl.when((k == 0) & (i == 0))
            def _():
                outs[2][...] = jnp.zeros_like(outs[2])

        @pl.when(k == nk - 1)
        def _():
            for r0 in range(0, tm, EPI_ROWS):
                rows = slice(r0, r0 + EPI_ROWS)
                v = acc[rows, :]
                if resid is not None:
                    v = v + named["resid"][rows, :]
                if mode == "plain":
                    outs[0][rows, :] = v.astype(outs[0].dtype)
                elif mode == "rms":
                    outs[0][rows, :] = v
                    r = lax.rsqrt(jnp.mean(v * v, axis=-1, keepdims=True) + EPS)
                    outs[1][rows, :] = (v * r * named["gain"][...]).astype(BF16)
                elif mode == "loss":
                    d = v - named["tgt"][rows, :]
                    outs[2][...] += jnp.sum(d * d, axis=0, keepdims=True) * (0.5 / N)
                    dy = d * (1.0 / N)
                    outs[0][rows, :] = dy
                    outs[1][rows, :] = dy.astype(BF16)
                else:
                    xf = named["x"][rows, :]
                    r = lax.rsqrt(jnp.mean(xf * xf, axis=-1, keepdims=True) + EPS)
                    xhat = xf * r
                    outs[2][...] += jnp.sum(v * xhat, axis=0, keepdims=True)
                    dxh = v * named["gain"][...]
                    tot = named["dres"][rows, :] + r * (dxh - xhat * jnp.mean(dxh * xhat, axis=-1, keepdims=True))
                    outs[0][rows, :] = tot
                    outs[1][rows, :] = tot.astype(BF16)

    row = pl.BlockSpec((tm, N), lambda i, k: (i, 0))
    vec = pl.BlockSpec((1, N), lambda i, k: (0, 0))
    if a.ndim == 3:
        a_spec = pl.BlockSpec((None, tm, tk), lambda i, k: (k // kper, i, k % kper))
    else:
        a_spec = pl.BlockSpec((tm, tk), lambda i, k: (i, k))
    in_specs = [a_spec, pl.BlockSpec((tk, N), lambda i, k: (k, 0))]
    args = [a, w]
    for t, spec in ((resid, row), (gain, vec), (tgt, row), (x, row), (dres, row), (dep, HBM_SPEC)):
        if t is not None:
            in_specs.append(spec)
            args.append(t)
    if mode == "plain":
        out_specs, out_shape = [row], [_sds((M, N), out_dtype)]
    elif mode == "rms":
        out_specs, out_shape = [row, row], [_sds((M, N), F32), _sds((M, N), BF16)]
    else:
        out_specs, out_shape = [row, row, vec], [_sds((M, N), F32), _sds((M, N), BF16), _sds((1, N), F32)]
    res = pl.pallas_call(
        body, name=name, grid=(M // tm, nk), in_specs=in_specs, out_specs=out_specs, out_shape=out_shape,
        scratch_shapes=[pltpu.VMEM((tm, N), F32)], compiler_params=_cparams(2))(*args)
    return res[0] if mode == "plain" else res


def _mm_tn(a, b, name, out_dtype=BF16, tt=512):
    parts = a.shape[0] if a.ndim == 3 else 1
    Tt, Mp = a.shape[-2], a.shape[-1]
    N = b.shape[1]
    tn = _pick(Mp, (1408, 1280, 1024, 512))
    jper = Mp // tn
    nt = Tt // tt

    def body(a_ref, b_ref, o_ref, acc):
        t = pl.program_id(1)

        @pl.when(t == 0)
        def _():
            acc[...] = jnp.zeros_like(acc)

        acc[...] += lax.dot_general(a_ref[...], b_ref[...], (((0,), (0,)), ((), ())),
                                    preferred_element_type=F32)

        @pl.when(t == nt - 1)
        def _():
            o_ref[...] = acc[...].astype(o_ref.dtype)

    if a.ndim == 3:
        a_spec = pl.BlockSpec((None, tt, tn), lambda j, t: (j // jper, t, j % jper))
    else:
        a_spec = pl.BlockSpec((tt, tn), lambda j, t: (t, j))
    return pl.pallas_call(
        body, name=name, grid=(parts * jper, nt),
        in_specs=[a_spec, pl.BlockSpec((tt, N), lambda j, t: (t, 0))],
        out_specs=pl.BlockSpec((tn, N), lambda j, t: (j, 0)),
        out_shape=_sds((parts * Mp, N), out_dtype), scratch_shapes=[pltpu.VMEM((tn, N), F32)],
        compiler_params=_cparams(2))(a, b)


def _ffn_in(h, wt_in, name, tm=2048, tn=256):
    nj = DFF // tn

    def body(h_ref, wg_ref, wu_ref, p_ref, act_ref):
        nt = (((1,), (1,)), ((), ()))
        g = lax.dot_general(h_ref[...], wg_ref[...], nt, preferred_element_type=F32).astype(BF16)
        u = lax.dot_general(h_ref[...], wu_ref[...], nt, preferred_element_type=F32).astype(BF16)
        p_ref[0] = g
        p_ref[1] = u
        act_ref[...] = g * _sigmoid_bf16(g) * u

    return pl.pallas_call(
        body, name=name, grid=(T // tm, nj),
        in_specs=[pl.BlockSpec((tm, D), lambda i, j: (i, 0)), pl.BlockSpec((tn, D), lambda i, j: (j, 0)),
                  pl.BlockSpec((tn, D), lambda i, j: (j + nj, 0))],
        out_specs=[pl.BlockSpec((2, tm, tn), lambda i, j: (0, i, j)), pl.BlockSpec((tm, tn), lambda i, j: (i, j))],
        out_shape=[_sds((2, T, DFF), BF16), _sds((T, DFF), BF16)], compiler_params=_cparams(2))(h, wt_in, wt_in)


def _ffn_dact(dyb, w_down, p3, name, tm=2048, tn=256, dep=None):
    def body(dy_ref, w_ref, p_ref, *rest):
        o_ref = rest[-1]
        da = lax.dot_general(dy_ref[...], w_ref[...], (((1,), (1,)), ((), ())),
                             preferred_element_type=F32).astype(BF16)
        g = p_ref[0]
        u = p_ref[1]
        sg = _sigmoid_bf16(g)
        gs = g * sg
        o_ref[0] = (da * u) * (sg + gs * (1.0 - sg))
        o_ref[1] = da * gs

    pspec = pl.BlockSpec((2, tm, tn), lambda i, j: (0, i, j))
    in_specs = [pl.BlockSpec((tm, D), lambda i, j: (i, 0)), pl.BlockSpec((tn, D), lambda i, j: (j, 0)), pspec]
    args = [dyb, w_down, p3]
    if dep is not None:
        in_specs.append(HBM_SPEC)
        args.append(dep)
    return pl.pallas_call(
        body, name=name, grid=(T // tm, DFF // tn), in_specs=in_specs, out_specs=pspec,
        out_shape=_sds((2, T, DFF), BF16), compiler_params=_cparams(2))(*args)


def _rms_fwd(x, g, name, tm=512):
    def body(x_ref, g_ref, h_ref):
        xf = x_ref[...]
        r = lax.rsqrt(jnp.mean(xf * xf, axis=-1, keepdims=True) + EPS)
        h_ref[...] = (xf * r * g_ref[...]).astype(BF16)

    return pl.pallas_call(
        body, name=name, grid=(T // tm,),
        in_specs=[pl.BlockSpec((tm, D), lambda i: (i, 0)), pl.BlockSpec((1, D), lambda i: (0, 0))],
        out_specs=pl.BlockSpec((tm, D), lambda i: (i, 0)),
        out_shape=_sds((T, D), BF16), compiler_params=_cparams(1))(x, g)


def _tril_mask():
    r = lax.broadcasted_iota(jnp.int32, (128, 128), 0)
    c = lax.broadcasted_iota(jnp.int32, (128, 128), 1)
    return r >= c


def _mix_a_fwd(pab, sgu_g, sgu_b, sgu_w, sgu_bias3, tm=512):
    def body(zu_ref, zv_ref, g_ref, b_ref, w_ref, bias_ref, o_ref):
        u = _gelu(zu_ref[...].astype(F32))
        v = _gelu(zv_ref[...].astype(F32))
        mu = jnp.mean(v, axis=-1, keepdims=True)
        vc = v - mu
        rstd = lax.rsqrt(jnp.mean(vc * vc, axis=-1, keepdims=True) + EPS)
        vn = (vc * rstd * g_ref[...] + b_ref[...]).astype(BF16)
        tri = _tril_mask()
        for gi in range(4):
            wg = jnp.where(tri, w_ref[gi], 0.0).astype(BF16)
            bg = bias_ref[gi]
            for c in range(tm // 128):
                rs, cs = slice(c * 128, (c + 1) * 128), slice(gi * 128, (gi + 1) * 128)
                mixed = jnp.dot(wg, vn[rs, cs], preferred_element_type=F32) + bg
                o_ref[rs, cs] = (u[rs, cs] * mixed).astype(BF16)

    half = pl.BlockSpec((tm, 512), lambda i: (i, 0))
    return pl.pallas_call(
        body, name="mix_a_fwd", grid=(T // tm,),
        in_specs=[half, pl.BlockSpec((tm, 512), lambda i: (i, 1)),
                  pl.BlockSpec((1, 512), lambda i: (0, 0)), pl.BlockSpec((1, 512), lambda i: (0, 0)),
                  pl.BlockSpec((4, 128, 128), lambda i: (0, 0, 0)), pl.BlockSpec((4, 128, 1), lambda i: (0, 0, 0))],
        out_specs=half, out_shape=_sds((T, D), BF16), compiler_params=_cparams(1),
    )(pab, pab, sgu_g, sgu_b, sgu_w, sgu_bias3)


def _rope_tables():
    pos = jnp.arange(T, dtype=F32)
    inv_freq = ROPE_THETA ** (-jnp.arange(ROPE_HALF, dtype=F32) * 2.0 / (2 * ROPE_HALF))
    ang = pos[:, None] * inv_freq[None, :]
    cos, sin = jnp.cos(ang), jnp.sin(ang)
    z8 = jnp.zeros((T, ROPE_HALF), F32)
    rest = HEAD - 2 * ROPE_HALF
    c64 = jnp.concatenate([cos, cos, jnp.ones((T, rest), F32)], axis=1)
    s1 = jnp.concatenate([z8, sin, jnp.zeros((T, rest), F32)], axis=1)
    s2 = jnp.concatenate([-sin, z8, jnp.zeros((T, rest), F32)], axis=1)
    return jnp.tile(c64, (1, 2)), jnp.tile(s1, (1, 2)), jnp.tile(s2, (1, 2))


def _lo_mask(shape):
    return lax.broadcasted_iota(jnp.int32, shape, 1) < HEAD


def _seg_mean(x, lo):
    s_all = jnp.sum(x, axis=-1, keepdims=True)
    s_lo = jnp.sum(jnp.where(lo, x, 0.0), axis=-1, keepdims=True)
    return jnp.where(lo, s_lo, s_all - s_lo) * (1.0 / HEAD)


def _rope(n, c, s1, s2):
    return n * c + pltpu.roll(n, ROPE_HALF, 1) * s1 + pltpu.roll(n, PAIR - ROPE_HALF, 1) * s2


def _rope_t(dy, c, s1, s2):
    return dy * c - pltpu.roll(dy, PAIR - ROPE_HALF, 1) * s2 - pltpu.roll(dy, ROPE_HALF, 1) * s1


def _prep_fwd(pab, qg, kg, tabs, tm=512):
    def body(p_ref, qg_ref, kg_ref, c_ref, s1_ref, s2_ref, *outs):
        lo = _lo_mask((tm, PAIR))
        c, s1, s2 = c_ref[...], s1_ref[...], s2_ref[...]
        for g in range(3):
            qn_ref, kn_ref, v_ref = outs[3 * g:3 * g + 3]
            for p in range(NPAIR):
                for which, gains, dst in ((0, qg_ref, qn_ref), (1, kg_ref, kn_ref)):
                    col = (2 + 3 * which + g) * 512 + p * PAIR
                    xr = p_ref[:, col:col + PAIR].astype(F32)
                    rinv = lax.rsqrt(_seg_mean(xr * xr, lo) + EPS)
                    dst[p] = _rope(xr * rinv * gains[g:g + 1, :], c, s1, s2)
                col = (8 + g) * 512 + p * PAIR
                v_ref[p] = p_ref[:, col:col + PAIR].astype(F32)

    pm = pl.BlockSpec((NPAIR, tm, PAIR), lambda i: (0, i, 0))
    tab = pl.BlockSpec((tm, PAIR), lambda i: (i, 0))
    gain = pl.BlockSpec((3, PAIR), lambda i: (0, 0))
    return pl.pallas_call(
        body, name="prep_fwd", grid=(T // tm,),
        in_specs=[pl.BlockSpec((tm, AB_IN), lambda i: (i, 0)), gain, gain, tab, tab, tab],
        out_specs=[pm] * 9, out_shape=[_sds((NPAIR, T, PAIR), F32)] * 9,
        compiler_params=_cparams(1))(pab, qg, kg, *tabs)


def _res_index(it, rate):
    window = NBACK * rate
    b = it // rate
    rho = it % rate
    start = b * window + rho
    startp = jnp.maximum(start - window, rho)
    kmin = jnp.where(b > 0, 0, NBACK)
    return start, startp, kmin


def _rows(start, rate):
    if rate == 1:
        return pl.ds(pl.multiple_of(start, NBACK), NBACK)
    return pl.ds(start, NBACK, stride=rate)


def _band():
    qi = lax.broadcasted_iota(jnp.int32, (NBACK, 2 * NBACK), 0)
    kj = lax.broadcasted_iota(jnp.int32, (NBACK, 2 * NBACK), 1)
    dist = qi + NBACK - kj
    return (dist >= 0) & (dist <= NBACK), kj


def _attn_fwd(qn, kn, v, rate, name, dep=None):
    def body(q_ref, k_ref, v_ref, *rest):
        o_ref, l_ref = rest[-2:]
        lo = _lo_mask((NBACK, PAIR))
        band, kj = _band()

        def step(it, carry):
            start, startp, kmin = _res_index(it, rate)
            q = q_ref[_rows(start, rate), :]
            kcat = jnp.concatenate([k_ref[_rows(startp, rate), :], k_ref[_rows(start, rate), :]], axis=0).astype(BF16)
            vcat = jnp.concatenate([v_ref[_rows(startp, rate), :], v_ref[_rows(start, rate), :]], axis=0).astype(BF16)
            ok = band & (kj >= kmin)
            q2 = jnp.concatenate([jnp.where(lo, q, 0.0), jnp.where(lo, 0.0, q)], axis=0).astype(BF16)
            s = lax.dot_general(q2, kcat, (((1,), (1,)), ((), ())), preferred_element_type=F32) * (HEAD ** -0.5)
            s = jnp.where(jnp.concatenate([ok, ok], axis=0), s, NEG_INF)
            m = jnp.max(s, axis=-1, keepdims=True)
            pr = jnp.exp(s - m)
            l = jnp.sum(pr, axis=-1, keepdims=True)
            o2 = jnp.dot(pr.astype(BF16), vcat, preferred_element_type=F32) / l
            ls = m + jnp.log(l)
            o_ref[_rows(start, rate), :] = jnp.where(lo, o2[0:NBACK], o2[NBACK:])
            l_ref[_rows(start, rate), :] = jnp.where(lo, ls[0:NBACK], ls[NBACK:])
            return carry

        lax.fori_loop(0, T // NBACK, step, 0, unroll=4)

    pm = pl.BlockSpec((None, T, PAIR), lambda p: (p, 0, 0))
    in_specs, args = [pm, pm, pm], [qn, kn, v]
    if dep is not None:
        in_specs.append(HBM_SPEC)
        args.append(dep)
    return pl.pallas_call(
        body, name=name, grid=(NPAIR,), in_specs=in_specs, out_specs=[pm, pm],
        out_shape=[_sds((NPAIR, T, PAIR), F32)] * 2, compiler_params=_cparams(1))(*args)


def _merge_fwd(cat_ab, outs, lses, tm=512):
    def body(cat_in, o0, o1, o2, l0, l1, l2, cat_ref, lse_ref):
        del cat_in
        for p in range(NPAIR):
            a0, a1, a2 = l0[p], l1[p], l2[p]
            m = jnp.maximum(jnp.maximum(a0, a1), a2)
            w0, w1, w2 = jnp.exp(a0 - m), jnp.exp(a1 - m), jnp.exp(a2 - m)
            s = w0 + w1 + w2
            b = (w0 * o0[p] + w1 * o1[p] + w2 * o2[p]) / s
            cat_ref[:, p * PAIR:(p + 1) * PAIR] = b.astype(BF16)
            lse_ref[p] = m + jnp.log(s)

    pm = pl.BlockSpec((NPAIR, tm, PAIR), lambda i: (0, i, 0))
    return pl.pallas_call(
        body, name="merge_fwd", grid=(T // tm,),
        in_specs=[pl.BlockSpec(memory_space=pl.ANY)] + [pm] * 6,
        out_specs=[pl.BlockSpec((tm, 512), lambda i: (i, 1)), pm],
        out_shape=[_sds((T, D), BF16), _sds((NPAIR, T, PAIR), F32)],
        input_output_aliases={0: 0}, compiler_params=_cparams(1))(cat_ab, *outs, *lses)


def _b_pre_bwd(dcat, cat, tm=512):
    def body(db_ref, b_ref, dbp_ref, e_ref):
        lo = _lo_mask((tm, PAIR))
        for p in range(NPAIR):
            db = db_ref[:, p * PAIR:(p + 1) * PAIR].astype(F32)
            b = b_ref[:, p * PAIR:(p + 1) * PAIR].astype(F32)
            dbp_ref[p] = db
            e_ref[p] = _seg_mean(db * b, lo) * float(HEAD)

    pm = pl.BlockSpec((NPAIR, tm, PAIR), lambda i: (0, i, 0))
    right = pl.BlockSpec((tm, 512), lambda i: (i, 1))
    return pl.pallas_call(
        body, name="b_pre_bwd", grid=(T // tm,), in_specs=[right, right], out_specs=[pm, pm],
        out_shape=[_sds((NPAIR, T, PAIR), F32)] * 2, compiler_params=_cparams(1))(dcat, cat)


def _attn_bwd(qn, kn, v, dbp, e, lse, rate, name):
    def body(q_ref, k_ref, v_ref, db_ref, e_ref, lse_ref, dq_ref, dk_ref, dv_ref):
        lo = _lo_mask((NBACK, PAIR))
        band, kj = _band()
        scale = HEAD ** -0.5
        nt = (((1,), (1,)), ((), ()))
        tn = (((0,), (0,)), ((), ()))
        window = NBACK * rate
        nblk = T // window

        def one(it, carry):
            dk_carry, dv_carry = carry
            rho = it // nblk
            b = it % nblk
            start = b * window + rho
            rq = _rows(start, rate)
            rp = _rows(jnp.maximum(start - window, rho), rate)
            kmin = jnp.where(b > 0, 0, NBACK)
            q = q_ref[rq, :]
            db = db_ref[rq, :]
            ev = e_ref[rq, :]
            ls = lse_ref[rq, :]
            kcat = jnp.concatenate([k_ref[rp, :], k_ref[rq, :]], axis=0).astype(BF16)
            vcat = jnp.concatenate([v_ref[rp, :], v_ref[rq, :]], axis=0).astype(BF16)
            ok = band & (kj >= kmin)
            ok2 = jnp.concatenate([ok, ok], axis=0)
            q2 = jnp.concatenate([jnp.where(lo, q, 0.0), jnp.where(lo, 0.0, q)], axis=0).astype(BF16)
            db2 = jnp.concatenate([jnp.where(lo, db, 0.0), jnp.where(lo, 0.0, db)], axis=0).astype(BF16)
            ls2 = jnp.concatenate([ls[:, 0:1], ls[:, HEAD:HEAD + 1]], axis=0)
            ev2 = jnp.concatenate([ev[:, 0:1], ev[:, HEAD:HEAD + 1]], axis=0)
            s = lax.dot_general(q2, kcat, nt, preferred_element_type=F32) * scale
            s = jnp.where(ok2, s, NEG_INF)
            pt = jnp.exp(s - ls2)
            dp = lax.dot_general(db2, vcat, nt, preferred_element_type=F32)
            ds = (pt * (dp - ev2)).astype(BF16)
            dq2 = jnp.dot(ds, kcat, preferred_element_type=F32) * scale
            dkc = lax.dot_general(ds, q2, tn, preferred_element_type=F32) * scale
            dvc = lax.dot_general(pt.astype(BF16), db2, tn, preferred_element_type=F32)
            dq_ref[rq, :] = jnp.where(lo, dq2[0:NBACK], dq2[NBACK:])
            dk_ref[rp, :] = dk_carry + dkc[0:NBACK]
            dk_ref[rq, :] = dkc[NBACK:]
            dv_ref[rp, :] = dv_carry + dvc[0:NBACK]
            dv_ref[rq, :] = dvc[NBACK:]
            return dkc[NBACK:], dvc[NBACK:]

        def step(i, carry):
            for u in range(ATTN_BWD_UNROLL):
                carry = one(i * ATTN_BWD_UNROLL + u, carry)
            return carry

        zero = jnp.zeros((NBACK, PAIR), F32)
        lax.fori_loop(0, T // NBACK // ATTN_BWD_UNROLL, step, (zero, zero))

    pm = pl.BlockSpec((None, T, PAIR), lambda p: (p, 0, 0))
    return pl.pallas_call(
        body, name=name, grid=(NPAIR,), in_specs=[pm] * 6, out_specs=[pm] * 3,
        out_shape=[_sds((NPAIR, T, PAIR), F32)] * 3, compiler_params=_cparams(1, 56))(qn, kn, v, dbp, e, lse)


def _ab_in_bwd(pab, dcat, sgu_g, sgu_b, sgu_w, sgu_bias3, qg, kg, tabs, dqkv, tm=256):
    def body(p_ref, dcat_ref, g_ref, b_ref, w_ref, bias_ref, qg_ref, kg_ref, c_ref, s1_ref, s2_ref, *rest):
        dq_refs = rest[0:9]
        o_ref, dwm_ref, dbias_ref, dsg_ref, dsb_ref, dgain_ref = rest[9:]
        i = pl.program_id(0)

        @pl.when(i == 0)
        def _():
            dwm_ref[...] = jnp.zeros_like(dwm_ref)
            dbias_ref[...] = jnp.zeros_like(dbias_ref)
            dsg_ref[...] = jnp.zeros_like(dsg_ref)
            dsb_ref[...] = jnp.zeros_like(dsb_ref)
            dgain_ref[...] = jnp.zeros_like(dgain_ref)

        zu = p_ref[:, 0:512].astype(F32)
        zv = p_ref[:, 512:1024].astype(F32)
        u = _gelu(zu)
        v = _gelu(zv)
        mu = jnp.mean(v, axis=-1, keepdims=True)
        vc = v - mu
        rstd = lax.rsqrt(jnp.mean(vc * vc, axis=-1, keepdims=True) + EPS)
        xhat = vc * rstd
        vn = (xhat * g_ref[...] + b_ref[...]).astype(BF16)
        da = dcat_ref[...].astype(F32)
        tri = _tril_mask()
        du_parts = [[None] * 4 for _ in range(tm // 128)]
        dvn_parts = [[None] * 4 for _ in range(tm // 128)]
        for gi in range(4):
            wg = jnp.where(tri, w_ref[gi], 0.0).astype(BF16)
            bg = bias_ref[gi]
            for c in range(tm // 128):
                rs, cs = slice(c * 128, (c + 1) * 128), slice(gi * 128, (gi + 1) * 128)
                vblk = vn[rs, cs]
                mixed = jnp.dot(wg, vblk, preferred_element_type=F32) + bg
                dab = da[rs, cs]
                du_parts[c][gi] = dab * mixed
                dmixed = dab * u[rs, cs]
                dmb = dmixed.astype(BF16)
                dvn_parts[c][gi] = lax.dot_general(wg, dmb, (((0,), (0,)), ((), ())), preferred_element_type=F32)
                dwm = lax.dot_general(dmb, vblk, (((1,), (1,)), ((), ())), preferred_element_type=F32)
                dwm_ref[gi] += jnp.where(tri, dwm, 0.0)
                dbias_ref[gi] += dmixed
        du = jnp.concatenate([jnp.concatenate(r, axis=1) for r in du_parts], axis=0)
        dvn = jnp.concatenate([jnp.concatenate(r, axis=1) for r in dvn_parts], axis=0)
        dsg_ref[...] += jnp.sum(dvn * xhat, axis=0, keepdims=True)
        dsb_ref[...] += jnp.sum(dvn, axis=0, keepdims=True)
        dxh = dvn * g_ref[...]
        dv = rstd * (dxh - jnp.mean(dxh, axis=-1, keepdims=True)
                     - xhat * jnp.mean(dxh * xhat, axis=-1, keepdims=True))
        o_ref[:, 0:512] = (du * _gelu_grad(zu)).astype(BF16)
        o_ref[:, 512:1024] = (dv * _gelu_grad(zv)).astype(BF16)

        lo = _lo_mask((tm, PAIR))
        c, s1, s2 = c_ref[...], s1_ref[...], s2_ref[...]
        for g in range(3):
            dq_ref, dk_ref, dv_ref = dq_refs[3 * g:3 * g + 3]
            for p in range(NPAIR):
                for which, gains, src in ((0, qg_ref, dq_ref), (1, kg_ref, dk_ref)):
                    col = (2 + 3 * which + g) * 512 + p * PAIR
                    xr = p_ref[:, col:col + PAIR].astype(F32)
                    rinv = lax.rsqrt(_seg_mean(xr * xr, lo) + EPS)
                    xh = xr * rinv
                    dn = _rope_t(src[p], c, s1, s2)
                    row = 2 * g + which
                    dgain_ref[row:row + 1, :] += jnp.sum(dn * xh, axis=0, keepdims=True)
                    dxh2 = dn * gains[g:g + 1, :]
                    dx = rinv * (dxh2 - xh * _seg_mean(dxh2 * xh, lo))
                    o_ref[:, col:col + PAIR] = dx.astype(BF16)
                col = (8 + g) * 512 + p * PAIR
                o_ref[:, col:col + PAIR] = dv_ref[p].astype(BF16)

    pm = pl.BlockSpec((NPAIR, tm, PAIR), lambda i: (0, i, 0))
    tab = pl.BlockSpec((tm, PAIR), lambda i: (i, 0))
    gain = pl.BlockSpec((3, PAIR), lambda i: (0, 0))
    vec = pl.BlockSpec((1, 512), lambda i: (0, 0))
    full = pl.BlockSpec((tm, AB_IN), lambda i: (i, 0))
    w4 = pl.BlockSpec((4, 128, 128), lambda i: (0, 0, 0))
    return pl.pallas_call(
        body, name="ab_in_bwd", grid=(T // tm,),
        in_specs=[full, pl.BlockSpec((tm, 512), lambda i: (i, 0)), vec, vec, w4,
                  pl.BlockSpec((4, 128, 1), lambda i: (0, 0, 0)), gain, gain, tab, tab, tab] + [pm] * 9,
        out_specs=[full, w4, w4, vec, vec, pl.BlockSpec((8, PAIR), lambda i: (0, 0))],
        out_shape=[_sds((T, AB_IN), BF16), _sds((4, 128, 128), F32), _sds((4, 128, 128), F32),
                   _sds((1, 512), F32), _sds((1, 512), F32), _sds((8, PAIR), F32)],
        compiler_params=_cparams(1))(pab, dcat, sgu_g, sgu_b, sgu_w, sgu_bias3, qg, kg, *tabs, *dqkv)


def _ln_stats(x):
    mu = jnp.mean(x, axis=-1, keepdims=True)
    xc = x - mu
    rstd = lax.rsqrt(jnp.mean(xc * xc, axis=-1, keepdims=True) + EPS)
    return xc * rstd, rstd


CONV_RC = 64


def _shifted_copies(src, dst, tm):
    dst[0] = src[...]
    for b in range(1, 8):
        dst[b, 0:tm + HALO - 8, :] = src[pl.ds(b, tm + HALO - 8), :]


def _cd_fwd(pcd, cw, cb, lg, lb, dw, tm=512):
    per = tm // HALO

    def body(p_ref, h_ref, cw_ref, cb_ref, lg_ref, lb_ref, dw_ref, cat_ref, c0_ref, c1_ref, dd_ref, y_ref,
             buf, buf2, sb):
        i = pl.program_id(0)
        live = jnp.where(i > 0, 1.0, 0.0)
        a = p_ref[:, 0:512].astype(F32)
        gt = p_ref[:, 512:1024].astype(F32)
        gb = p_ref[:, 1024:1536].astype(F32)
        gc = p_ref[:, 1536:2048].astype(F32)
        hv = p_ref[:, 2048:2560].astype(F32)
        c0 = a * _sigmoid(gt)
        dd = gc * hv
        buf[0:HALO, :] = h_ref[:, 0:512].astype(F32) * _sigmoid(h_ref[:, 512:1024].astype(F32)) * live
        buf[HALO:, :] = c0
        buf2[0:HALO, :] = h_ref[:, 1536:2048].astype(F32) * h_ref[:, 2048:2560].astype(F32) * live
        buf2[HALO:, :] = dd
        c0_ref[...] = c0.astype(BF16)
        dd_ref[...] = dd.astype(BF16)
        _shifted_copies(buf, sb, tm)

        def conv_rows(r, carry):
            base = pl.multiple_of(r * CONV_RC, CONV_RC)
            for c in range(4):
                lanes = slice(c * 128, (c + 1) * 128)
                acc = jnp.broadcast_to(cb_ref[:, lanes], (CONV_RC, 128))
                for j in range(CONV_C_TAPS):
                    a8, b8 = divmod(HALO - (CONV_C_TAPS - 1) + j, 8)
                    acc = acc + cw_ref[j:j + 1, lanes] * sb[b8, pl.ds(base + 8 * a8, CONV_RC), lanes]
                c1_ref[pl.ds(base, CONV_RC), lanes] = acc
            return carry

        lax.fori_loop(0, tm // CONV_RC, conv_rows, 0)
        xhat, _ = _ln_stats(c1_ref[...])
        c2 = xhat * lg_ref[...] + lb_ref[...]
        y = jnp.zeros((tm, 512), F32)
        for j in range(CONV_D_TAPS):
            y = y + dw_ref[j:j + 1, :] * buf2[pl.ds(HALO - (CONV_D_TAPS - 1) + j, tm), :]
        cat_ref[:, 0:512] = (c2 * _sigmoid(c2)).astype(BF16)
        cat_ref[:, 512:1024] = (gb * y).astype(BF16)
        y_ref[...] = y.astype(BF16)

    half = pl.BlockSpec((tm, 512), lambda i: (i, 0))
    vec = pl.BlockSpec((1, 512), lambda i: (0, 0))
    return pl.pallas_call(
        body, name="cd_fwd", grid=(T // tm,),
        in_specs=[pl.BlockSpec((tm, CD_IN), lambda i: (i, 0)),
                  pl.BlockSpec((HALO, CD_IN), lambda i: (jnp.maximum(i * per - 1, 0), 0)),
                  pl.BlockSpec((32, 512), lambda i: (0, 0)), vec, vec, vec, pl.BlockSpec((8, 512), lambda i: (0, 0))],
        out_specs=[pl.BlockSpec((tm, D), lambda i: (i, 0)), half, half, half, half],
        out_shape=[_sds((T, D), BF16), _sds((T, 512), BF16), _sds((T, 512), F32), _sds((T, 512), BF16),
                   _sds((T, 512), BF16)],
        scratch_shapes=[pltpu.VMEM((HALO + tm, 512), F32), pltpu.VMEM((HALO + tm, 512), F32),
                        pltpu.VMEM((8, HALO + tm, 512), F32)],
        compiler_params=_cparams(1))(pcd, pcd, cw, cb, lg, lb, dw)


def _cd_bwd_pw(dcat, c1, pcd, y, lg, lb, tm=512):
    def body(dcat_ref, c1_ref, gb_ref, y_ref, lg_ref, lb_ref, dc1_ref, dy3_ref, dgb_ref, dlg_ref, dlb_ref, dcb_ref):
        i = pl.program_id(0)

        @pl.when(i == 0)
        def _():
            dlg_ref[...] = jnp.zeros_like(dlg_ref)
            dlb_ref[...] = jnp.zeros_like(dlb_ref)
            dcb_ref[...] = jnp.zeros_like(dcb_ref)

        dc = dcat_ref[:, 0:512].astype(F32)
        ddo = dcat_ref[:, 512:1024].astype(F32)
        xhat, rstd = _ln_stats(c1_ref[...])
        c2 = xhat * lg_ref[...] + lb_ref[...]
        sg = _sigmoid(c2)
        dc2 = dc * sg * (1.0 + c2 * (1.0 - sg))
        dlg_ref[...] += jnp.sum(dc2 * xhat, axis=0, keepdims=True)
        dlb_ref[...] += jnp.sum(dc2, axis=0, keepdims=True)
        dxh = dc2 * lg_ref[...]
        dc1 = rstd * (dxh - jnp.mean(dxh, axis=-1, keepdims=True)
                      - xhat * jnp.mean(dxh * xhat, axis=-1, keepdims=True))
        dcb_ref[...] += jnp.sum(dc1, axis=0, keepdims=True)
        dc1_ref[...] = dc1
        dgb_ref[...] = (ddo * y_ref[...].astype(F32)).astype(BF16)
        dy3_ref[...] = ddo * gb_ref[...].astype(F32)

    half = pl.BlockSpec((tm, 512), lambda i: (i, 0))
    vec = pl.BlockSpec((1, 512), lambda i: (0, 0))
    return pl.pallas_call(
        body, name="cd_bwd_pw", grid=(T // tm,),
        in_specs=[pl.BlockSpec((tm, D), lambda i: (i, 0)), half, pl.BlockSpec((tm, 512), lambda i: (i, 2)), half,
                  vec, vec],
        out_specs=[half, half, half, vec, vec, vec],
        out_shape=[_sds((T, 512), F32), _sds((T, 512), F32), _sds((T, 512), BF16),
                   _sds((1, 512), F32), _sds((1, 512), F32), _sds((1, 512), F32)],
        compiler_params=_cparams(1))(dcat, c1, pcd, y, lg, lb)


def _cd_bwd_conv(pcd, dc1, dy3, c0, dd, dgb, cw, dw, tm=256):
    per = tm // HALO
    nblk = T // tm
    last32 = T // HALO - 1

    def body(p_ref, dc1_ref, dc1n_ref, dy3_ref, dy3n_ref, c0_ref, c0p_ref, dd_ref, ddp_ref, dgb_ref, cw_ref, dw_ref,
             o_ref, dcw_ref, ddw_ref, dbuf, cbuf, d3buf, ddbuf, sd, sc, dc0_buf):
        i = pl.program_id(0)
        has_prev = jnp.where(i > 0, 1.0, 0.0)
        has_next = jnp.where(i < nblk - 1, 1.0, 0.0)

        @pl.when(i == 0)
        def _():
            dcw_ref[...] = jnp.zeros_like(dcw_ref)
            ddw_ref[...] = jnp.zeros_like(ddw_ref)

        dc1 = dc1_ref[...]
        dy3 = dy3_ref[...]
        dbuf[0:tm, :] = dc1
        dbuf[tm:, :] = dc1n_ref[...] * has_next
        d3buf[0:tm, :] = dy3
        d3buf[tm:, :] = dy3n_ref[...] * has_next
        cbuf[0:HALO, :] = c0p_ref[...].astype(F32) * has_prev
        cbuf[HALO:, :] = c0_ref[...].astype(F32)
        ddbuf[0:HALO, :] = ddp_ref[...].astype(F32) * has_prev
        ddbuf[HALO:, :] = dd_ref[...].astype(F32)

        _shifted_copies(dbuf, sd, tm)
        _shifted_copies(cbuf, sc, tm)
        n_tiles = tm // CONV_RC

        def dc0_rows(r, carry):
            base = pl.multiple_of(r * CONV_RC, CONV_RC)
            for c in range(4):
                lanes = slice(c * 128, (c + 1) * 128)
                acc = jnp.zeros((CONV_RC, 128), F32)
                for j in range(CONV_C_TAPS):
                    a8, b8 = divmod(CONV_C_TAPS - 1 - j, 8)
                    acc = acc + cw_ref[j:j + 1, lanes] * sd[b8, pl.ds(base + 8 * a8, CONV_RC), lanes]
                dc0_buf[pl.ds(base, CONV_RC), lanes] = acc
            return carry

        lax.fori_loop(0, n_tiles, dc0_rows, 0)

        for c in range(4):
            lanes = slice(c * 128, (c + 1) * 128)
            for j0 in range(0, CONV_C_TAPS, 8):
                taps = list(range(j0, min(j0 + 8, CONV_C_TAPS)))

                def dw_rows(r, accs, lanes=lanes, taps=taps):
                    base = pl.multiple_of(r * CONV_RC, CONV_RC)
                    d = dbuf[pl.ds(base, CONV_RC), lanes]
                    out = []
                    for acc, j in zip(accs, taps):
                        a8, b8 = divmod(HALO - (CONV_C_TAPS - 1) + j, 8)
                        prod = d * sc[b8, pl.ds(base + 8 * a8, CONV_RC), lanes]
                        out.append(acc + jnp.sum(prod.reshape(CONV_RC // 8, 8, 128), axis=0))
                    return tuple(out)

                accs = lax.fori_loop(0, n_tiles, dw_rows, tuple(jnp.zeros((8, 128), F32) for _ in taps))
                for acc, j in zip(accs, taps):
                    dcw_ref[j:j + 1, lanes] += jnp.sum(acc, axis=0, keepdims=True)

        dc0 = dc0_buf[...]
        ddd = jnp.zeros((tm, 512), F32)
        for j in range(CONV_D_TAPS):
            ddd = ddd + dw_ref[j:j + 1, :] * d3buf[pl.ds(CONV_D_TAPS - 1 - j, tm), :]
            ddw_ref[j:j + 1, :] += jnp.sum(dy3 * ddbuf[pl.ds(HALO - (CONV_D_TAPS - 1) + j, tm), :], axis=0, keepdims=True)

        a = p_ref[:, 0:512].astype(F32)
        gt = p_ref[:, 512:1024].astype(F32)
        gc = p_ref[:, 1536:2048].astype(F32)
        hv = p_ref[:, 2048:2560].astype(F32)
        sg = _sigmoid(gt)
        o_ref[:, 0:512] = (dc0 * sg).astype(BF16)
        o_ref[:, 512:1024] = (dc0 * a * sg * (1.0 - sg)).astype(BF16)
        o_ref[:, 1024:1536] = dgb_ref[...]
        o_ref[:, 1536:2048] = (ddd * hv).astype(BF16)
        o_ref[:, 2048:2560] = (ddd * gc).astype(BF16)

    half = pl.BlockSpec((tm, 512), lambda i: (i, 0))
    nxt = pl.BlockSpec((HALO, 512), lambda i: (jnp.minimum((i + 1) * per, last32), 0))
    prv = pl.BlockSpec((HALO, 512), lambda i: (jnp.maximum(i * per - 1, 0), 0))
    full = pl.BlockSpec((tm, CD_IN), lambda i: (i, 0))
    return pl.pallas_call(
        body, name="cd_bwd_conv", grid=(nblk,),
        in_specs=[full, half, nxt, half, nxt, half, prv, half, prv, half,
                  pl.BlockSpec((32, 512), lambda i: (0, 0)), pl.BlockSpec((8, 512), lambda i: (0, 0))],
        out_specs=[full, pl.BlockSpec((32, 512), lambda i: (0, 0)), pl.BlockSpec((8, 512), lambda i: (0, 0))],
        out_shape=[_sds((T, CD_IN), BF16), _sds((32, 512), F32), _sds((8, 512), F32)],
        scratch_shapes=[pltpu.VMEM((tm + HALO, 512), F32), pltpu.VMEM((HALO + tm, 512), F32),
                        pltpu.VMEM((tm + HALO, 512), F32), pltpu.VMEM((HALO + tm, 512), F32),
                        pltpu.VMEM((8, tm + HALO, 512), F32), pltpu.VMEM((8, HALO + tm, 512), F32),
                        pltpu.VMEM((tm, 512), F32)],
        compiler_params=_cparams(1))(pcd, dc1, dc1, dy3, dy3, c0, c0, dd, dd, dgb, cw, dw)


def _local_step(x, tgt, W, fetch=None, on_grad=None):
    W = dict(W)
    if fetch is None:
        fetch = lambda stage, after: {}
    if on_grad is None:
        on_grad = lambda key, arr: None
    tabs = _rope_tables()
    qg = jnp.tile(W["q_norm_g"], (1, 2))
    kg = jnp.tile(W["k_norm_g"], (1, 2))
    bias3 = W["sgu_bias"].reshape(4, 128, 1)
    G = {}

    h0 = _rms_fwd(x, W["ab_norm_g"], "rms_fwd_ab")
    pab = _mm_nt(h0, W["wt_ab_in"], "mm_ab_in", dep=W.get("dep0"))
    cat_ab = _mix_a_fwd(pab, W["sgu_norm_g"], W["sgu_norm_b"], W["sgu_w"], bias3)
    qkv = _prep_fwd(pab, qg, kg, tabs)
    outs, lses = [], []
    for g, rate in enumerate(DIL_RATES):
        o, l = _attn_fwd(qkv[3 * g], qkv[3 * g + 1], qkv[3 * g + 2], rate, f"attn_fwd_{g}", dep=W.get(f"dep_attn{g}"))
        outs.append(o)
        lses.append(l)
        W.update(fetch(f"attn{g}", o))
    cat_ab, lse = _merge_fwd(cat_ab, outs, lses)
    W.update(fetch("ab_out", lse))
    x1, h1 = _mm_nn(cat_ab, W["w_ab_out"], "mm_ab_out", mode="rms", resid=x, gain=W["ffn_norm_g"][0:1])
    pf0, act0 = _ffn_in(h1, W["wt_ffn_in0"], "ffn_in0")
    W.update(fetch("ffn_down0", act0))
    x2, h2 = _mm_nn(act0, W["w_ffn_down0"], "mm_ffn_down0", mode="rms", resid=x1, gain=W["cd_norm_g"],
                    dep=W.get("dep_down0"))
    W.update(fetch("cd_in", h2))
    pcd = _mm_nt(h2, W["wt_cd_in"], "mm_cd_in")
    cat_cd, c0, c1, dd, yv = _cd_fwd(pcd, W["conv_c_w32"], W["conv_c_b"], W["c_ln_g"], W["c_ln_b"], W["conv_d_w8"])
    x3, h3 = _mm_nn(cat_cd, W["w_cd_out"], "mm_cd_out", mode="rms", resid=x2, gain=W["ffn_norm_g"][1:2])
    pf1, act1 = _ffn_in(h3, W["wt_ffn_in1"], "ffn_in1")
    dy, dyb, loss_cols = _mm_nn(act1, W["w_ffn_down1"], "mm_ffn_down1", mode="loss", resid=x3, tgt=tgt)

    def ffn_bwd(xin, h, pf, act, dres, dresb, layer):
        G[f"w_ffn_down{layer}"] = _mm_tn(act, dresb, f"mm_g_ffn_down{layer}")
        dep = on_grad(f"w_ffn_down{layer}", G[f"w_ffn_down{layer}"])
        dpf = _ffn_dact(dresb, W[f"w_ffn_down{layer}"], pf, f"ffn_dact{layer}", dep=dep)
        G[f"wt_ffn_in{layer}"] = _mm_tn(dpf, h, f"mm_g_ffn_in{layer}")
        dep = on_grad(f"wt_ffn_in{layer}", G[f"wt_ffn_in{layer}"])
        dx, dxb, G[f"ffn_norm_g{layer}"] = _mm_nn(
            dpf, W[f"wt_ffn_in{layer}"], f"mm_d_h_ffn{layer}", mode="rms_bwd", x=xin,
            gain=W["ffn_norm_g"][layer:layer + 1], dres=dres, dep=dep)
        return dx, dxb

    dx3, dx3b = ffn_bwd(x3, h3, pf1, act1, dy, dyb, 1)

    G["w_cd_out"] = _mm_tn(cat_cd, dx3b, "mm_g_cd_out")
    dep = on_grad("w_cd_out", G["w_cd_out"])
    dcat_cd = _mm_nt(dx3b, W["w_cd_out"], "mm_d_cat_cd", dep=dep)
    dc1, dy3, dgb, G["c_ln_g"], G["c_ln_b"], G["conv_c_b"] = _cd_bwd_pw(dcat_cd, c1, pcd, yv, W["c_ln_g"], W["c_ln_b"])
    dpcd, G["conv_c_w32"], G["conv_d_w8"] = _cd_bwd_conv(pcd, dc1, dy3, c0, dd, dgb, W["conv_c_w32"], W["conv_d_w8"])
    G["wt_cd_in"] = _mm_tn(dpcd, h2, "mm_g_cd_in")
    dep = on_grad("wt_cd_in", G["wt_cd_in"])
    dx2, dx2b, G["cd_norm_g"] = _mm_nn(dpcd, W["wt_cd_in"], "mm_d_h_cd", mode="rms_bwd", x=x2, gain=W["cd_norm_g"],
                                       dres=dx3, dep=dep)

    dx1, dx1b = ffn_bwd(x1, h1, pf0, act0, dx2, dx2b, 0)

    G["w_ab_out"] = _mm_tn(cat_ab, dx1b, "mm_g_ab_out")
    dep = on_grad("w_ab_out", G["w_ab_out"])
    dcat_ab = _mm_nt(dx1b, W["w_ab_out"], "mm_d_cat_ab", dep=dep)
    dbp, e = _b_pre_bwd(dcat_ab, cat_ab)
    dqkv = []
    for g, rate in enumerate(DIL_RATES):
        dqkv += _attn_bwd(qkv[3 * g], qkv[3 * g + 1], qkv[3 * g + 2], dbp, e, lse, rate, f"attn_bwd_{g}")
    dpab, G["sgu_w"], dbias_part, G["sgu_norm_g"], G["sgu_norm_b"], dgain = _ab_in_bwd(
        pab, dcat_ab, W["sgu_norm_g"], W["sgu_norm_b"], W["sgu_w"], bias3, qg, kg, tabs, dqkv)
    G["sgu_bias"] = jnp.sum(dbias_part, axis=-1)
    dgain = dgain[0:6, 0:HEAD] + dgain[0:6, HEAD:PAIR]
    G["q_norm_g"] = dgain[0::2]
    G["k_norm_g"] = dgain[1::2]
    G["wt_ab_in"] = _mm_tn(dpab, h0, "mm_g_ab_in")
    dep = on_grad("wt_ab_in", G["wt_ab_in"])
    grad_x, _, G["ab_norm_g"] = _mm_nn(dpab, W["wt_ab_in"], "mm_d_h_ab", mode="rms_bwd", x=x, gain=W["ab_norm_g"],
                                       dres=dx1, dep=dep)
    return loss_cols, grad_x, G


def _my_place():
    return lax.axis_index("x"), lax.axis_index("y"), lax.axis_index("c")


def _dev_index(px, py, pc):
    return 4 * px + 2 * py + pc


def _flip(place, k):
    x, y, c = place
    return (1 - x if k & 4 else x, 1 - y if k & 2 else y, 1 - c if k & 1 else c)


def _landing(shape, dtype, own):
    buf = lax.empty(shape, dtype)
    for lead, part in own:
        buf = lax.dynamic_update_slice(buf, part.reshape((1,) * len(lead) + part.shape),
                                       tuple(lead) + (0,) * part.ndim)
    return buf


def _gather_first(ab_in_t, small, me_index):
    lands = [_landing((NDEV,) + ab_in_t.shape, BF16, [((me_index,), ab_in_t)]),
             _landing((NDEV,) + small.shape, F32, [((me_index,), small)])]
    n_items = 2

    def body(ab_in_r, small_r, l_ab_in, l_small, o_ab_in, o_small, send_sems, recv_sems):
        del l_ab_in, l_small
        x, y, c = _my_place()
        me = (x, y, c)
        sib = (x, y, 1 - c)
        chips = [(1 - x, y), (x, 1 - y), (1 - x, 1 - y)]
        items = [(ab_in_r, lambda d: o_ab_in.at[d]), (small_r, lambda d: o_small.at[d])]

        def rcopy(it, k, src, dst, to):
            return pltpu.make_async_remote_copy(src_ref=src, dst_ref=dst, send_sem=send_sems.at[it, k],
                                                recv_sem=recv_sems.at[it, k], device_id=to, device_id_type=MESH)

        started = []
        for it, (src, dst) in enumerate(items):
            mine = dst(_dev_index(*me))
            first = [rcopy(it, 0, src, mine, sib)]
            first += [rcopy(it, 1 + j, src, mine, (*chip, c)) for j, chip in enumerate(chips)]
            for cp in first:
                cp.start()
            started += first
        for it, (src, dst) in enumerate(items):
            for j, chip in enumerate(chips):
                blk = dst(_dev_index(*chip, c))
                rcopy(it, 1 + j, blk, blk, me).wait_recv()
                fwd = rcopy(it, 4 + j, blk, blk, sib)
                fwd.start()
                started.append(fwd)
        for it, (src, dst) in enumerate(items):
            blk = dst(_dev_index(x, y, 1 - c))
            rcopy(it, 0, blk, blk, me).wait_recv()
            for j, chip in enumerate(chips):
                blk = dst(_dev_index(*chip, 1 - c))
                rcopy(it, 4 + j, blk, blk, me).wait_recv()
        for cp in started:
            cp.wait_send()

    return pl.pallas_call(
        body, name="gather_first", in_specs=[HBM_SPEC] * 4, out_specs=[HBM_SPEC] * 2,
        out_shape=[_sds(a.shape, a.dtype) for a in lands], input_output_aliases={2: 0, 3: 1},
        scratch_shapes=[pltpu.SemaphoreType.DMA((n_items, 7)), pltpu.SemaphoreType.DMA((n_items, 7))],
    )(ab_in_t, small, *lands)


HBM_ONLY = pl.BlockSpec(memory_space=pltpu.HBM)
SEM_SPEC = pl.BlockSpec(memory_space=pltpu.SEMAPHORE)
IN_FLIGHT = pltpu.CompilerParams(has_side_effects=pltpu.SideEffectType.DATAFLOW_SIDE_EFFECTING)


def _in_hbm(a):
    return pltpu.with_memory_space_constraint(a, pltpu.HBM)


def _exchange_start(name, srcs, lands, items, dep=None):
    ns, nl, ni = len(srcs), len(lands), len(items)

    def body(*refs):
        S, L = refs[0:ns], refs[ns:ns + nl]
        first_out = ns + nl + (0 if dep is None else 1)
        send_sems, recv_sems, token = refs[first_out], refs[first_out + 1], refs[-1]
        me = _my_place()
        mi = _dev_index(*me)
        for i, (src, dst) in enumerate(items):
            for k in range(1, NDEV):
                peer = _flip(me, k)
                pltpu.make_async_remote_copy(
                    src_ref=src(S, _dev_index(*peer)), dst_ref=dst(L, mi), send_sem=send_sems.at[7 * i + k - 1],
                    recv_sem=recv_sems.at[7 * i + k - 1], device_id=peer, device_id_type=MESH).start()
        token[...] = jnp.zeros_like(token)

    thru = [pltpu.HBM(a.shape, a.dtype) for a in list(srcs) + list(lands)]
    args = [_in_hbm(a) for a in srcs] + [_in_hbm(a) for a in lands]
    in_specs = [HBM_ONLY] * (ns + nl)
    if dep is not None:
        args.append(dep)
        in_specs.append(HBM_SPEC)
    outs = pl.pallas_call(
        body, name=name, in_specs=in_specs,
        out_shape=(pltpu.SemaphoreType.DMA((7 * ni,)), pltpu.SemaphoreType.DMA((7 * ni,)), *thru, _sds((8, 128), F32)),
        out_specs=(SEM_SPEC, SEM_SPEC, *[HBM_ONLY] * (ns + nl), pl.BlockSpec(memory_space=pltpu.VMEM)),
        input_output_aliases={j: 2 + j for j in range(ns + nl)}, compiler_params=IN_FLIGHT)(*args)
    return dict(send=outs[0], recv=outs[1], srcs=list(outs[2:2 + ns]), lands=list(outs[2 + ns:2 + ns + nl]),
                token=outs[-1], items=items)


def _exchange_wait(name, states, after):
    counts = [(len(st["srcs"]), len(st["lands"]), len(st["items"])) for st in states]
    n_arrays = sum(c[0] + c[1] for c in counts)

    def body(*refs):
        me = _my_place()
        mi = _dev_index(*me)
        pos = 0
        sem_pos = n_arrays
        for st, (ns, nl, ni) in zip(states, counts):
            S, L = refs[pos:pos + ns], refs[pos + ns:pos + ns + nl]
            send_sems, recv_sems = refs[sem_pos], refs[sem_pos + 1]
            pos += ns + nl
            sem_pos += 2
            for i, (src, dst) in enumerate(st["items"]):
                for k in range(1, NDEV):
                    cp = pltpu.make_async_remote_copy(
                        src_ref=src(S, mi), dst_ref=dst(L, mi), send_sem=send_sems.at[7 * i + k - 1],
                        recv_sem=recv_sems.at[7 * i + k - 1], device_id=me, device_id_type=MESH)
                    cp.wait_send()
                    cp.wait_recv()

    arrays, sems = [], []
    for st in states:
        arrays += st["srcs"] + st["lands"]
        sems += [st["send"], st["recv"]]
    outs = pl.pallas_call(
        body, name=name, in_specs=[HBM_ONLY] * n_arrays + [SEM_SPEC] * len(sems) + [HBM_SPEC],
        out_shape=tuple(pltpu.HBM(a.shape, a.dtype) for a in arrays), out_specs=tuple([HBM_ONLY] * n_arrays),
        input_output_aliases={j: j for j in range(n_arrays)}, compiler_params=IN_FLIGHT)(*arrays, *sems, after)
    lands, pos = [], 0
    for ns, nl, _ in counts:
        lands.append(list(outs[pos + ns:pos + ns + nl]))
        pos += ns + nl
    return lands


def _place_and_neighbours():
    x, y, c = _my_place()
    return (x, y, c), (x, y, 1 - c), [(1 - x, y), (x, 1 - y), (1 - x, 1 - y)]


def _gather_start(name, srcs, lands, items, dep=None):
    ns, nl, ni = len(srcs), len(lands), len(items)

    def body(*refs):
        S, L = refs[0:ns], refs[ns:ns + nl]
        first_out = ns + nl + (0 if dep is None else 1)
        send_sems, recv_sems, token = refs[first_out], refs[first_out + 1], refs[-1]
        me, sib, chips = _place_and_neighbours()
        mi = _dev_index(*me)
        for i, (src, dst) in enumerate(items):
            for k, to in enumerate([sib] + [(*chip, me[2]) for chip in chips]):
                pltpu.make_async_remote_copy(
                    src_ref=src(S), dst_ref=dst(L, mi), send_sem=send_sems.at[4 * i + k],
                    recv_sem=recv_sems.at[4 * i + k], device_id=to, device_id_type=MESH).start()
        token[...] = jnp.zeros_like(token)

    thru = [pltpu.HBM(a.shape, a.dtype) for a in list(srcs) + list(lands)]
    args = [_in_hbm(a) for a in srcs] + [_in_hbm(a) for a in lands]
    in_specs = [HBM_ONLY] * (ns + nl)
    if dep is not None:
        args.append(dep)
        in_specs.append(HBM_SPEC)
    outs = pl.pallas_call(
        body, name=name, in_specs=in_specs,
        out_shape=(pltpu.SemaphoreType.DMA((4 * ni,)), pltpu.SemaphoreType.DMA((4 * ni,)), *thru, _sds((8, 128), F32)),
        out_specs=(SEM_SPEC, SEM_SPEC, *[HBM_ONLY] * (ns + nl), pl.BlockSpec(memory_space=pltpu.VMEM)),
        input_output_aliases={j: 2 + j for j in range(ns + nl)}, compiler_params=IN_FLIGHT)(*args)
    return dict(send=outs[0], recv=outs[1], srcs=list(outs[2:2 + ns]), lands=list(outs[2 + ns:2 + ns + nl]),
                token=outs[-1], items=items)


def _gather_forward(name, st, after):
    nl, ni = len(st["lands"]), len(st["items"])

    def body(*refs):
        L, recv_sems = refs[0:nl], refs[nl]
        fwd_send, fwd_recv, token = refs[2 * nl + 2], refs[2 * nl + 3], refs[-1]
        me, sib, chips = _place_and_neighbours()
        for i, (_, dst) in enumerate(st["items"]):
            for j, chip in enumerate(chips):
                blk = dst(L, _dev_index(*chip, me[2]))
                pltpu.make_async_remote_copy(
                    src_ref=blk, dst_ref=blk, send_sem=fwd_send.at[3 * i + j], recv_sem=recv_sems.at[4 * i + 1 + j],
                    device_id=me, device_id_type=MESH).wait_recv()
                pltpu.make_async_remote_copy(
                    src_ref=blk, dst_ref=blk, send_sem=fwd_send.at[3 * i + j], recv_sem=fwd_recv.at[3 * i + j],
                    device_id=sib, device_id_type=MESH).start()
        token[...] = jnp.zeros_like(token)

    outs = pl.pallas_call(
        body, name=name, in_specs=[HBM_ONLY] * nl + [SEM_SPEC, HBM_SPEC],
        out_shape=(*[pltpu.HBM(a.shape, a.dtype) for a in st["lands"]], pltpu.SemaphoreType.DMA((3 * ni,)),
                   pltpu.SemaphoreType.DMA((3 * ni,)), _sds((8, 128), F32)),
        out_specs=(*[HBM_ONLY] * nl, SEM_SPEC, SEM_SPEC, pl.BlockSpec(memory_space=pltpu.VMEM)),
        input_output_aliases={j: j for j in range(nl)}, compiler_params=IN_FLIGHT)(*st["lands"], st["recv"], after)
    return dict(st, lands=list(outs[0:nl]), fwd_send=outs[nl], fwd_recv=outs[nl + 1], token=outs[-1])


def _gather_wait(name, st, after):
    ns, nl, ni = len(st["srcs"]), len(st["lands"]), len(st["items"])

    def body(*refs):
        S, L = refs[0:ns], refs[ns:ns + nl]
        send_sems, recv_sems, fwd_send, fwd_recv = refs[ns + nl:ns + nl + 4]
        me, sib, chips = _place_and_neighbours()
        mi = _dev_index(*me)
        for i, (src, dst) in enumerate(st["items"]):
            mine = dst(L, mi)
            for k in range(4):
                pltpu.make_async_remote_copy(
                    src_ref=src(S), dst_ref=mine, send_sem=send_sems.at[4 * i + k], recv_sem=recv_sems.at[4 * i + k],
                    device_id=me, device_id_type=MESH).wait_send()
            pltpu.make_async_remote_copy(
                src_ref=src(S), dst_ref=mine, send_sem=send_sems.at[4 * i], recv_sem=recv_sems.at[4 * i],
                device_id=me, device_id_type=MESH).wait_recv()
            for j in range(3):
                cp = pltpu.make_async_remote_copy(
                    src_ref=mine, dst_ref=mine, send_sem=fwd_send.at[3 * i + j], recv_sem=fwd_recv.at[3 * i + j],
                    device_id=me, device_id_type=MESH)
                cp.wait_send()
                cp.wait_recv()

    arrays = st["srcs"] + st["lands"]
    outs = pl.pallas_call(
        body, name=name, in_specs=[HBM_ONLY] * (ns + nl) + [SEM_SPEC] * 4 + [HBM_SPEC],
        out_shape=tuple(pltpu.HBM(a.shape, a.dtype) for a in arrays), out_specs=tuple([HBM_ONLY] * (ns + nl)),
        input_output_aliases={j: j for j in range(ns + nl)},
        compiler_params=IN_FLIGHT)(*arrays, st["send"], st["recv"], st["fwd_send"], st["fwd_recv"], after)
    return list(outs[ns:ns + nl])


def _sum_slots(land):
    def body(l_ref, o_ref):
        acc = l_ref[0]
        for d in range(1, NDEV):
            acc = acc + l_ref[d]
        o_ref[...] = acc

    vm = pl.BlockSpec(memory_space=pltpu.VMEM)
    return pl.pallas_call(body, name="sum_small", out_shape=_sds(land.shape[1:], F32), in_specs=[vm], out_specs=vm)(land)


def _adam_math(w, g, m, v):
    m2 = ADAM_B1 * m + (1.0 - ADAM_B1) * g
    v2 = ADAM_B2 * v + (1.0 - ADAM_B2) * (g * g)
    delta = -ADAM_LR * ((m2 * ADAM_C1) / (jnp.sqrt(v2 * ADAM_C2) + ADAM_EPS) + ADAM_WD * w)
    return delta, m2, v2


def _adam_layer(land, sel, w, m, v, layer, name, prev=None, tc=512):
    R = land.shape[2]

    def body(l_ref, w_ref, m_ref, v_ref, *rest):
        g_out, d_out, m_out, v_out = rest[-4:]
        g = l_ref[0].astype(F32)
        for d in range(1, NDEV):
            g = g + l_ref[d].astype(F32)
        delta, m2, v2 = _adam_math(w_ref[...], g, m_ref[...], v_ref[...])
        g_out[...] = g
        d_out[...] = delta
        m_out[...] = m2
        v_out[...] = v2

    wspec = pl.BlockSpec((None, R, tc), lambda i: (layer, 0, i))
    in_specs = [pl.BlockSpec((None, NDEV, R, tc), lambda i: (sel, 0, 0, i)), wspec, wspec, wspec]
    args = [land, w, m, v]
    aliases = {}
    if prev is not None:
        in_specs += [HBM_SPEC] * 4
        args += list(prev)
        aliases = {4 + j: j for j in range(4)}
    return pl.pallas_call(
        body, name=name, grid=(D // tc,), in_specs=in_specs, out_specs=[wspec] * 4,
        out_shape=[_sds(w.shape, F32)] * 4, input_output_aliases=aliases, compiler_params=_cparams(1))(*args)


def _adam_stacked(lands, sel, w, m, v, name):
    res = None
    for layer, land in enumerate(lands):
        res = _adam_layer(land, sel, w, m, v, layer, f"{name}{layer}", prev=res)
    return res


def _adam_small(ws, gs, ms, vs):
    n = len(ws)

    def body(*refs):
        w_r, g_r, m_r, v_r = refs[0:n], refs[n:2 * n], refs[2 * n:3 * n], refs[3 * n:4 * n]
        d_o, m_o, v_o = refs[4 * n:5 * n], refs[5 * n:6 * n], refs[6 * n:7 * n]
        for i in range(n):
            delta, m2, v2 = _adam_math(w_r[i][...], g_r[i][...], m_r[i][...], v_r[i][...])
            d_o[i][...] = delta
            m_o[i][...] = m2
            v_o[i][...] = v2

    vm = pl.BlockSpec(memory_space=pltpu.VMEM)
    shapes = [_sds(w.shape, F32) for w in ws]
    outs = pl.pallas_call(body, name="adam_small", in_specs=[vm] * (4 * n), out_specs=[vm] * (3 * n),
                          out_shape=shapes * 3)(*ws, *gs, *ms, *vs)
    return outs[0:n], outs[n:2 * n], outs[2 * n:3 * n]


WEIGHT_NAMES = ("ab_norm_g", "ab_w_in", "sgu_norm_g", "sgu_norm_b", "sgu_w", "sgu_bias", "q_norm_g", "k_norm_g",
                "ab_w_out", "cd_norm_g", "cd_w_in", "conv_c_w", "conv_c_b", "c_ln_g", "c_ln_b", "conv_d_w",
                "cd_w_out", "ffn_norm_g", "ffn_w_gate", "ffn_w_up", "ffn_w_down")
SMALL_2D = (("ab_norm_g", (1, 1024)), ("sgu_norm_g", (1, 512)), ("sgu_norm_b", (1, 512)), ("sgu_w", (512, 128)),
            ("sgu_bias", (4, 128)), ("q_norm_g", (3, 64)), ("k_norm_g", (3, 64)), ("cd_norm_g", (1, 128)),
            ("conv_c_w", (31, 64)), ("conv_c_b", (1, 64)), ("c_ln_g", (1, 64)), ("c_ln_b", (1, 64)),
            ("conv_d_w", (3, 64)), ("ffn_norm_g", (2, 1024)))
SHARD_C = 64


def _pack_rows(parts, rows):
    flat = jnp.concatenate([p.reshape(-1) for p in parts])
    return jnp.pad(flat, (0, rows * 128 - flat.shape[0])).reshape(rows, 128)


def kernel(x, ab_norm_g, ab_w_in, sgu_norm_g, sgu_norm_b, sgu_w, sgu_bias, q_norm_g, k_norm_g, ab_w_out, cd_norm_g, cd_w_in, conv_c_w, conv_c_b, c_ln_g, c_ln_b, conv_d_w, cd_w_out, ffn_norm_g, ffn_w_gate, ffn_w_up, ffn_w_down, loss_target, m_ab_norm_g, m_ab_w_in, m_sgu_norm_g, m_sgu_norm_b, m_sgu_w, m_sgu_bias, m_q_norm_g, m_k_norm_g, m_ab_w_out, m_cd_norm_g, m_cd_w_in, m_conv_c_w, m_conv_c_b, m_c_ln_g, m_c_ln_b, m_conv_d_w, m_cd_w_out, m_ffn_norm_g, m_ffn_w_gate, m_ffn_w_up, m_ffn_w_down, v_ab_norm_g, v_ab_w_in, v_sgu_norm_g, v_sgu_norm_b, v_sgu_w, v_sgu_bias, v_q_norm_g, v_k_norm_g, v_ab_w_out, v_cd_norm_g, v_cd_w_in, v_conv_c_w, v_conv_c_b, v_c_ln_g, v_c_ln_b, v_conv_d_w, v_cd_w_out, v_ffn_norm_g, v_ffn_w_gate, v_ffn_w_up, v_ffn_w_down):
    w = dict(zip(WEIGHT_NAMES, (ab_norm_g, ab_w_in, sgu_norm_g, sgu_norm_b, sgu_w, sgu_bias, q_norm_g, k_norm_g, ab_w_out, cd_norm_g, cd_w_in, conv_c_w, conv_c_b, c_ln_g, c_ln_b, conv_d_w, cd_w_out, ffn_norm_g, ffn_w_gate, ffn_w_up, ffn_w_down)))
    m = dict(zip(WEIGHT_NAMES, (m_ab_norm_g, m_ab_w_in, m_sgu_norm_g, m_sgu_norm_b, m_sgu_w, m_sgu_bias, m_q_norm_g, m_k_norm_g, m_ab_w_out, m_cd_norm_g, m_cd_w_in, m_conv_c_w, m_conv_c_b, m_c_ln_g, m_c_ln_b, m_conv_d_w, m_cd_w_out, m_ffn_norm_g, m_ffn_w_gate, m_ffn_w_up, m_ffn_w_down)))
    v = dict(zip(WEIGHT_NAMES, (v_ab_norm_g, v_ab_w_in, v_sgu_norm_g, v_sgu_norm_b, v_sgu_w, v_sgu_bias, v_q_norm_g, v_k_norm_g, v_ab_w_out, v_cd_norm_g, v_cd_w_in, v_conv_c_w, v_conv_c_b, v_c_ln_g, v_c_ln_b, v_conv_d_w, v_cd_w_out, v_ffn_norm_g, v_ffn_w_gate, v_ffn_w_up, v_ffn_w_down)))
    me = _dev_index(*_my_place())

    small_local = _pack_rows([w["cd_norm_g"], w["conv_c_w"], w["conv_c_b"], w["c_ln_g"], w["c_ln_b"], w["conv_d_w"]], 24)
    o_ab_in, o_small = _gather_first(w["ab_w_in"][0].T.astype(BF16), small_local, me)
    r_ff = DFF // NDEV
    one = lambda a: (lambda S, j: S[a])
    slot = lambda b: (lambda L, s: L[b].at[s])
    slot2 = lambda b, part: (lambda L, s: L[b].at[part, s])
    shard = lambda a: (lambda S: S[a])

    def layer_shards(layer):
        return (w["ffn_w_gate"][layer].T.astype(BF16), w["ffn_w_up"][layer].T.astype(BF16),
                w["ffn_w_down"][layer].astype(BF16))

    def gathered(own):
        return _landing((NDEV,) + own.shape, BF16, [((me,), own)])

    def gathered2(a, b):
        return _landing((2, NDEV) + a.shape, BF16, [((0, me), a), ((1, me), b)])

    ab_out_s = w["ab_w_out"][0].astype(BF16)
    gate0, up0, down0 = layer_shards(0)
    gathers = {1: _gather_start(
        "gather1_start", [ab_out_s, gate0, up0, down0], [gathered(ab_out_s), gathered2(gate0, up0), gathered(down0)],
        [(shard(0), slot(0)), (shard(1), slot2(1, 0)), (shard(2), slot2(1, 1)), (shard(3), slot(2))], dep=o_small)}

    def fetch(stage, after):
        if stage == "attn0":
            cd_in_s, cd_out_s = w["cd_w_in"][0].T.astype(BF16), w["cd_w_out"][0].astype(BF16)
            gate1, up1, down1 = layer_shards(1)
            gathers[2] = _gather_start(
                "gather2_start", [cd_in_s, cd_out_s, gate1, up1, down1],
                [gathered(cd_in_s), gathered(cd_out_s), gathered2(gate1, up1), gathered(down1)],
                [(shard(0), slot(0)), (shard(1), slot(1)), (shard(2), slot2(2, 0)), (shard(3), slot2(2, 1)),
                 (shard(4), slot(3))], dep=after)
            return {"dep_attn1": gathers[2]["token"]}
        if stage == "attn1":
            gathers[1] = _gather_forward("gather1_forward", gathers[1], after)
            return {"dep_attn2": gathers[1]["token"]}
        if stage == "ab_out":
            l_out, l_ffn, l_down = _gather_wait("gather1_wait", gathers[1], after)
            return {"w_ab_out": l_out.reshape(D, D), "wt_ffn_in0": l_ffn.reshape(2 * DFF, D),
                    "w_ffn_down0": l_down.reshape(DFF, D)}
        if stage == "ffn_down0":
            gathers[2] = _gather_forward("gather2_forward", gathers[2], after)
            return {"dep_down0": gathers[2]["token"]}
        if stage == "cd_in":
            l_in, l_out, l_ffn, l_down = _gather_wait("gather2_wait", gathers[2], after)
            return {"wt_cd_in": l_in.reshape(CD_IN, D), "w_cd_out": l_out.reshape(D, D),
                    "wt_ffn_in1": l_ffn.reshape(2 * DFF, D), "w_ffn_down1": l_down.reshape(DFF, D)}
        return {}

    scatters = {}

    def on_grad(key, arr):
        if key.startswith("wt_ffn_in"):
            src = arr.reshape(2, NDEV, r_ff, D)
            own = lax.dynamic_slice_in_dim(src, me, 1, axis=1)
            land = lax.dynamic_update_slice(lax.empty(src.shape, BF16), own, (0, me, 0, 0))
            items = [(lambda S, j: S[0].at[0, j], slot2(0, 0)), (lambda S, j: S[0].at[1, j], slot2(0, 1))]
        else:
            rows = arr.shape[0] // NDEV
            src = arr.reshape(NDEV, rows, D)
            land = _landing((1, NDEV, rows, D), BF16, [((0, me), lax.dynamic_index_in_dim(src, me, 0, keepdims=False))])
            items = [(lambda S, j: S[0].at[j], slot2(0, 0))]
        scatters[key] = _exchange_start(f"scatter_{key}_start", [src], [land], items)
        return scatters[key]["token"]

    flat = o_small.reshape(NDEV, 24 * 128)

    def chan(lo, taps):
        return flat[:, lo:lo + taps * SHARD_C].reshape(NDEV, taps, SHARD_C).transpose(1, 0, 2).reshape(taps, 512)

    W = {
        "wt_ab_in": o_ab_in.reshape(AB_IN, D), "dep0": gathers[1]["token"],
        "ab_norm_g": w["ab_norm_g"], "sgu_norm_g": w["sgu_norm_g"], "sgu_norm_b": w["sgu_norm_b"],
        "sgu_w": w["sgu_w"][0], "sgu_bias": w["sgu_bias"][0], "q_norm_g": w["q_norm_g"][0],
        "k_norm_g": w["k_norm_g"][0], "ffn_norm_g": w["ffn_norm_g"],
        "cd_norm_g": flat[:, 0:128].reshape(1, D),
        "conv_c_w32": jnp.pad(chan(128, CONV_C_TAPS), ((0, 1), (0, 0))),
        "conv_c_b": chan(2112, 1), "c_ln_g": chan(2176, 1), "c_ln_b": chan(2240, 1),
        "conv_d_w8": jnp.pad(chan(2304, CONV_D_TAPS), ((0, 8 - CONV_D_TAPS), (0, 0))),
    }

    loss_cols, grad_x, G = _local_step(x[0], loss_target[0], W, fetch, on_grad)
    loss = lax.psum(jnp.sum(loss_cols), ("x", "y", "c"))

    small_parts = [G["ab_norm_g"], G["sgu_norm_g"], G["sgu_norm_b"], G["sgu_w"], G["sgu_bias"], G["q_norm_g"],
                   G["k_norm_g"], G["cd_norm_g"], G["conv_c_w32"][:CONV_C_TAPS], G["conv_c_b"], G["c_ln_g"],
                   G["c_ln_b"], G["conv_d_w8"][:CONV_D_TAPS], G["ffn_norm_g0"], G["ffn_norm_g1"]]
    sizes = [p.size for p in small_parts]
    small_rows = 712
    packed = _pack_rows(small_parts, small_rows)
    small = _exchange_start("small_start", [packed], [_landing((NDEV, small_rows, 128), F32, [((me,), packed)])],
                            [(one(0), slot(0))])
    early = ["w_ffn_down1", "wt_ffn_in1", "w_cd_out", "wt_cd_in", "w_ffn_down0", "wt_ffn_in0", "w_ab_out"]
    landed = dict(zip(early, _exchange_wait("scatter_wait_early", [scatters[k] for k in early], small["token"])))

    grads, deltas, new_m, new_v = {}, {}, {}, {}

    def put(name, res):
        grads[name], deltas[name], new_m[name], new_v[name] = res

    def adam(name, lands, sel, transposed):
        flip = (lambda a: jnp.swapaxes(a, 1, 2)) if transposed else (lambda a: a)
        res = _adam_stacked(lands, sel, flip(w[name]), flip(m[name]), flip(v[name]), f"adam_{name}")
        put(name, [flip(r) for r in res])

    ffn_in_lands = [landed["wt_ffn_in0"][0], landed["wt_ffn_in1"][0]]
    adam("cd_w_in", landed["wt_cd_in"], 0, True)
    adam("ffn_w_gate", ffn_in_lands, 0, True)
    adam("ffn_w_up", ffn_in_lands, 1, True)
    adam("ab_w_out", landed["w_ab_out"], 0, False)
    adam("cd_w_out", landed["w_cd_out"], 0, False)
    adam("ffn_w_down", [landed["w_ffn_down0"][0], landed["w_ffn_down1"][0]], 0, False)

    small_land = _exchange_wait("small_wait", [small], deltas["ffn_w_down"])[0][0]
    red = _sum_slots(small_land).reshape(-1)
    offs = [0]
    for s in sizes:
        offs.append(offs[-1] + s)
    seg = [red[offs[i]:offs[i + 1]] for i in range(len(sizes))]

    def own_channels(full, taps):
        return lax.dynamic_slice_in_dim(full.reshape(taps, 512), me * SHARD_C, SHARD_C, axis=1)

    g_small = {
        "ab_norm_g": seg[0].reshape(1, 1024), "sgu_norm_g": seg[1].reshape(1, 512), "sgu_norm_b": seg[2].reshape(1, 512),
        "sgu_w": seg[3].reshape(512, 128), "sgu_bias": seg[4].reshape(4, 128), "q_norm_g": seg[5].reshape(3, 64),
        "k_norm_g": seg[6].reshape(3, 64),
        "cd_norm_g": lax.dynamic_slice_in_dim(seg[7].reshape(1, D), me * (D // NDEV), D // NDEV, axis=1),
        "conv_c_w": own_channels(seg[8], CONV_C_TAPS), "conv_c_b": own_channels(seg[9], 1),
        "c_ln_g": own_channels(seg[10], 1), "c_ln_b": own_channels(seg[11], 1),
        "conv_d_w": own_channels(seg[12], CONV_D_TAPS),
        "ffn_norm_g": jnp.concatenate([seg[13].reshape(1, D), seg[14].reshape(1, D)], axis=0),
    }

    names2d = [n for n, _ in SMALL_2D]
    d_s, m_s, v_s = _adam_small([w[n].reshape(s) for n, s in SMALL_2D], [g_small[n] for n in names2d],
                                [m[n].reshape(s) for n, s in SMALL_2D], [v[n].reshape(s) for n, s in SMALL_2D])
    for i, n in enumerate(names2d):
        shape = w[n].shape
        grads[n], deltas[n] = g_small[n].reshape(shape), d_s[i].reshape(shape)
        new_m[n], new_v[n] = m_s[i].reshape(shape), v_s[i].reshape(shape)

    last = _exchange_wait("scatter_wait_last", [scatters["wt_ab_in"]], d_s[0])[0]
    adam("ab_w_in", last, 0, True)

    return (loss, grad_x[None], *[grads[n] for n in WEIGHT_NAMES], *[deltas[n] for n in WEIGHT_NAMES],
            *[new_m[n] for n in WEIGHT_NAMES], *[new_v[n] for n in WEIGHT_NAMES])
```

```python
import functools

import jax
import jax.numpy as jnp
import numpy as np
from jax import lax
from jax.experimental import pallas as pl
from jax.experimental.pallas import tpu as pltpu

F32 = jnp.float32
BF16 = jnp.bfloat16

T = 4096
D = 1024
NDEV = 8
EPS = 1e-6
NEG_INF = -1e30
DFF = 2816
AB_IN = 5632
CD_IN = 2560
HEAD = 64
PAIR = 128
NPAIR = 4
NBACK = 128
DIL_RATES = (1, 4, 16)
ROPE_HALF = 8
ROPE_THETA = 500000.0
CONV_C_TAPS = 31
CONV_D_TAPS = 3
HALO = 32
ATTN_BWD_UNROLL = 4

ADAM_LR = 0.001
ADAM_B1 = 0.9
ADAM_B2 = 0.999
ADAM_EPS = 1e-08
ADAM_WD = 0.01
ADAM_STEP = 10
ADAM_C1 = 1.0 / (1.0 - ADAM_B1 ** ADAM_STEP)
ADAM_C2 = 1.0 / (1.0 - ADAM_B2 ** ADAM_STEP)

VMEM_LIMIT_MB = 48
MESH = pl.DeviceIdType.MESH
HBM_SPEC = pl.BlockSpec(memory_space=pl.ANY)


def _cparams(ngrid, vmem_mb=VMEM_LIMIT_MB):
    return pltpu.CompilerParams(dimension_semantics=("arbitrary",) * ngrid,
                                vmem_limit_bytes=vmem_mb * 1024 * 1024)


def _pick(n, options):
    for o in options:
        if n % o == 0:
            return o
    raise ValueError(f"no tile for {n} in {options}")


def _sds(shape, dtype):
    return jax.ShapeDtypeStruct(shape, dtype)


def _sigmoid(x):
    return 1.0 / (1.0 + jnp.exp(-x))


def _sigmoid_bf16(x):
    return 0.5 * jnp.tanh(0.5 * x) + 0.5


def _gelu(z):
    return 0.5 * z * (1.0 + lax.erf(z * 0.7071067811865476))


def _gelu_grad(z):
    return 0.5 * (1.0 + lax.erf(z * 0.7071067811865476)) + z * jnp.exp(-0.5 * z * z) * 0.3989422804014327


def _mm_nt(a, wt, name, out_dtype=BF16, tm=2048, dep=None):
    M, K = a.shape
    N = wt.shape[0]
    tn = _pick(N, (512, 256))

    def body(a_ref, w_ref, *rest):
        o_ref = rest[-1]
        o_ref[...] = lax.dot_general(a_ref[...], w_ref[...], (((1,), (1,)), ((), ())),
                                     preferred_element_type=F32).astype(o_ref.dtype)

    in_specs = [pl.BlockSpec((tm, K), lambda i, j: (i, 0)), pl.BlockSpec((tn, K), lambda i, j: (j, 0))]
    args = [a, wt]
    if dep is not None:
        in_specs.append(HBM_SPEC)
        args.append(dep)
    return pl.pallas_call(
        body, name=name, grid=(M // tm, N // tn), in_specs=in_specs,
        out_specs=pl.BlockSpec((tm, tn), lambda i, j: (i, j)),
        out_shape=_sds((M, N), out_dtype), compiler_params=_cparams(2))(*args)


EPI_ROWS = 256


def _mm_nn(a, w, name, mode="plain", resid=None, gain=None, tgt=None, x=None, dres=None, out_dtype=F32, dep=None):
    parts = a.shape[0] if a.ndim == 3 else 1
    M, Kp = a.shape[-2], a.shape[-1]
    N = w.shape[1]
    tk = _pick(Kp, (1408, 1280, 1024, 512))
    kper = Kp // tk
    nk = parts * kper
    tm = 512 if mode == "rms_bwd" else 1024
    n_in = 2 + sum(t is not None for t in (resid, gain, tgt, x, dres, dep))

    def body(*refs):
        a_ref, w_ref = refs[0], refs[1]
        named = dict(zip([n for n, t in (("resid", resid), ("gain", gain), ("tgt", tgt), ("x", x), ("dres", dres))
                          if t is not None], refs[2:]))
        outs, acc = refs[n_in:-1], refs[-1]
        i, k = pl.program_id(0), pl.program_id(1)

        @pl.when(k == 0)
        def _():
            acc[...] = jnp.zeros_like(acc)

        acc[...] += jnp.dot(a_ref[...], w_ref[...], preferred_element_type=F32)

        if mode in ("loss", "rms_bwd"):
            @pl.when((k == 0) & (i == 0))
            def _():
                outs[2][...] = jnp.zeros_like(outs[2])

        @pl.when(k == nk - 1)
        def _():
            for r0 in range(0, tm, EPI_ROWS):
                rows = slice(r0, r0 + EPI_ROWS)
                v = acc[rows, :]
                if resid is not None:
                    v = v + named["resid"][rows, :]
                if mode == "plain":
                    outs[0][rows, :] = v.astype(outs[0].dtype)
                elif mode == "rms":
                    outs[0][rows, :] = v
                    r = lax.rsqrt(jnp.mean(v * v, axis=-1, keepdims=True) + EPS)
                    outs[1][rows, :] = (v * r * named["gain"][...]).astype(BF16)
                elif mode == "loss":
                    d = v - named["tgt"][rows, :]
                    outs[2][...] += jnp.sum(d * d, axis=0, keepdims=True) * (0.5 / N)
                    dy = d * (1.0 / N)
                    outs[0][rows, :] = dy
                    outs[1][rows, :] = dy.astype(BF16)
                else:
                    xf = named["x"][rows, :]
                    r = lax.rsqrt(jnp.mean(xf * xf, axis=-1, keepdims=True) + EPS)
                    xhat = xf * r
                    outs[2][...] += jnp.sum(v * xhat, axis=0, keepdims=True)
                    dxh = v * named["gain"][...]
                    tot = named["dres"][rows, :] + r * (dxh - xhat * jnp.mean(dxh * xhat, axis=-1, keepdims=True))
                    outs[0][rows, :] = tot
                    outs[1][rows, :] = tot.astype(BF16)

    row = pl.BlockSpec((tm, N), lambda i, k: (i, 0))
    vec = pl.BlockSpec((1, N), lambda i, k: (0, 0))
    if a.ndim == 3:
        a_spec = pl.BlockSpec((None, tm, tk), lambda i, k: (k // kper, i, k % kper))
    else:
        a_spec = pl.BlockSpec((tm, tk), lambda i, k: (i, k))
    in_specs = [a_spec, pl.BlockSpec((tk, N), lambda i, k: (k, 0))]
    args = [a, w]
    for t, spec in ((resid, row), (gain, vec), (tgt, row), (x, row), (dres, row), (dep, HBM_SPEC)):
        if t is not None:
            in_specs.append(spec)
            args.append(t)
    if mode == "plain":
        out_specs, out_shape = [row], [_sds((M, N), out_dtype)]
    elif mode == "rms":
        out_specs, out_shape = [row, row], [_sds((M, N), F32), _sds((M, N), BF16)]
    else:
        out_specs, out_shape = [row, row, vec], [_sds((M, N), F32), _sds((M, N), BF16), _sds((1, N), F32)]
    res = pl.pallas_call(
        body, name=name, grid=(M // tm, nk), in_specs=in_specs, out_specs=out_specs, out_shape=out_shape,
        scratch_shapes=[pltpu.VMEM((tm, N), F32)], compiler_params=_cparams(2))(*args)
    return res[0] if mode == "plain" else res


def _mm_tn(a, b, name, out_dtype=BF16, tt=1024):
    parts = a.shape[0] if a.ndim == 3 else 1
    Tt, Mp = a.shape[-2], a.shape[-1]
    N = b.shape[1]
    tn = _pick(Mp, (1408, 1280, 1024, 512))
    jper = Mp // tn
    nt = Tt // tt

    def body(a_ref, b_ref, o_ref, acc):
        t = pl.program_id(1)

        @pl.when(t == 0)
        def _():
            acc[...] = jnp.zeros_like(acc)

        acc[...] += lax.dot_general(a_ref[...], b_ref[...], (((0,), (0,)), ((), ())),
                                    preferred_element_type=F32)

        @pl.when(t == nt - 1)
        def _():
            o_ref[...] = acc[...].astype(o_ref.dtype)

    if a.ndim == 3:
        a_spec = pl.BlockSpec((None, tt, tn), lambda j, t: (j // jper, t, j % jper))
    else:
        a_spec = pl.BlockSpec((tt, tn), lambda j, t: (t, j))
    return pl.pallas_call(
        body, name=name, grid=(parts * jper, nt),
        in_specs=[a_spec, pl.BlockSpec((tt, N), lambda j, t: (t, 0))],
        out_specs=pl.BlockSpec((tn, N), lambda j, t: (j, 0)),
        out_shape=_sds((parts * Mp, N), out_dtype), scratch_shapes=[pltpu.VMEM((tn, N), F32)],
        compiler_params=_cparams(2))(a, b)


def _ffn_in(h, wt_in, name, tm=2048, tn=256):
    nj = DFF // tn

    def body(h_ref, wg_ref, wu_ref, p_ref, act_ref):
        nt = (((1,), (1,)), ((), ()))
        g = lax.dot_general(h_ref[...], wg_ref[...], nt, preferred_element_type=F32).astype(BF16)
        u = lax.dot_general(h_ref[...], wu_ref[...], nt, preferred_element_type=F32).astype(BF16)
        p_ref[0] = g
        p_ref[1] = u
        act_ref[...] = g * _sigmoid_bf16(g) * u

    return pl.pallas_call(
        body, name=name, grid=(T // tm, nj),
        in_specs=[pl.BlockSpec((tm, D), lambda i, j: (i, 0)), pl.BlockSpec((tn, D), lambda i, j: (j, 0)),
                  pl.BlockSpec((tn, D), lambda i, j: (j + nj, 0))],
        out_specs=[pl.BlockSpec((2, tm, tn), lambda i, j: (0, i, j)), pl.BlockSpec((tm, tn), lambda i, j: (i, j))],
        out_shape=[_sds((2, T, DFF), BF16), _sds((T, DFF), BF16)], compiler_params=_cparams(2))(h, wt_in, wt_in)


def _ffn_dact(dyb, w_down, p3, name, tm=2048, tn=256, dep=None):
    def body(dy_ref, w_ref, p_ref, *rest):
        o_ref = rest[-1]
        da = lax.dot_general(dy_ref[...], w_ref[...], (((1,), (1,)), ((), ())),
                             preferred_element_type=F32).astype(BF16)
        g = p_ref[0]
        u = p_ref[1]
        sg = _sigmoid_bf16(g)
        gs = g * sg
        o_ref[0] = (da * u) * (sg + gs * (1.0 - sg))
        o_ref[1] = da * gs

    pspec = pl.BlockSpec((2, tm, tn), lambda i, j: (0, i, j))
    in_specs = [pl.BlockSpec((tm, D), lambda i, j: (i, 0)), pl.BlockSpec((tn, D), lambda i, j: (j, 0)), pspec]
    args = [dyb, w_down, p3]
    if dep is not None:
        in_specs.append(HBM_SPEC)
        args.append(dep)
    return pl.pallas_call(
        body, name=name, grid=(T // tm, DFF // tn), in_specs=in_specs, out_specs=pspec,
        out_shape=_sds((2, T, DFF), BF16), compiler_params=_cparams(2))(*args)


def _rms_fwd(x, g, name, tm=512):
    def body(x_ref, g_ref, h_ref):
        xf = x_ref[...]
        r = lax.rsqrt(jnp.mean(xf * xf, axis=-1, keepdims=True) + EPS)
        h_ref[...] = (xf * r * g_ref[...]).astype(BF16)

    return pl.pallas_call(
        body, name=name, grid=(T // tm,),
        in_specs=[pl.BlockSpec((tm, D), lambda i: (i, 0)), pl.BlockSpec((1, D), lambda i: (0, 0))],
        out_specs=pl.BlockSpec((tm, D), lambda i: (i, 0)),
        out_shape=_sds((T, D), BF16), compiler_params=_cparams(1))(x, g)


def _tril_mask():
    r = lax.broadcasted_iota(jnp.int32, (128, 128), 0)
    c = lax.broadcasted_iota(jnp.int32, (128, 128), 1)
    return r >= c


def _mix_a_fwd(pab, sgu_g, sgu_b, sgu_w, sgu_bias3, tm=512):
    def body(zu_ref, zv_ref, g_ref, b_ref, w_ref, bias_ref, o_ref):
        u = _gelu(zu_ref[...].astype(F32))
        v = _gelu(zv_ref[...].astype(F32))
        mu = jnp.mean(v, axis=-1, keepdims=True)
        vc = v - mu
        rstd = lax.rsqrt(jnp.mean(vc * vc, axis=-1, keepdims=True) + EPS)
        vn = (vc * rstd * g_ref[...] + b_ref[...]).astype(BF16)
        tri = _tril_mask()
        for gi in range(4):
            wg = jnp.where(tri, w_ref[gi], 0.0).astype(BF16)
            bg = bias_ref[gi]
            for c in range(tm // 128):
                rs, cs = slice(c * 128, (c + 1) * 128), slice(gi * 128, (gi + 1) * 128)
                mixed = jnp.dot(wg, vn[rs, cs], preferred_element_type=F32) + bg
                o_ref[rs, cs] = (u[rs, cs] * mixed).astype(BF16)

    half = pl.BlockSpec((tm, 512), lambda i: (i, 0))
    return pl.pallas_call(
        body, name="mix_a_fwd", grid=(T // tm,),
        in_specs=[half, pl.BlockSpec((tm, 512), lambda i: (i, 1)),
                  pl.BlockSpec((1, 512), lambda i: (0, 0)), pl.BlockSpec((1, 512), lambda i: (0, 0)),
                  pl.BlockSpec((4, 128, 128), lambda i: (0, 0, 0)), pl.BlockSpec((4, 128, 1), lambda i: (0, 0, 0))],
        out_specs=half, out_shape=_sds((T, D), BF16), compiler_params=_cparams(1),
    )(pab, pab, sgu_g, sgu_b, sgu_w, sgu_bias3)


def _rope_tables():
    pos = np.arange(T, dtype=np.float32)
    inv_freq = np.float32(ROPE_THETA) ** (-np.arange(ROPE_HALF, dtype=np.float32) * np.float32(2.0 / (2 * ROPE_HALF)))
    ang = (pos[:, None] * inv_freq[None, :]).astype(np.float32)
    cos, sin = np.cos(ang), np.sin(ang)
    z8 = np.zeros((T, ROPE_HALF), np.float32)
    rest = np.zeros((T, HEAD - 2 * ROPE_HALF), np.float32)
    c64 = np.concatenate([cos, cos, rest + 1.0], axis=1)
    s1 = np.concatenate([z8, sin, rest], axis=1)
    s2 = np.concatenate([-sin, z8, rest], axis=1)
    return tuple(jnp.asarray(np.tile(t, (1, 2)).astype(np.float32)) for t in (c64, s1, s2))


def _lo_mask(shape):
    return lax.broadcasted_iota(jnp.int32, shape, 1) < HEAD


def _seg_mean(x, lo):
    s_all = jnp.sum(x, axis=-1, keepdims=True)
    s_lo = jnp.sum(jnp.where(lo, x, 0.0), axis=-1, keepdims=True)
    return jnp.where(lo, s_lo, s_all - s_lo) * (1.0 / HEAD)


def _rope(n, c, s1, s2):
    return n * c + pltpu.roll(n, ROPE_HALF, 1) * s1 + pltpu.roll(n, PAIR - ROPE_HALF, 1) * s2


def _rope_t(dy, c, s1, s2):
    return dy * c - pltpu.roll(dy, PAIR - ROPE_HALF, 1) * s2 - pltpu.roll(dy, ROPE_HALF, 1) * s1


def _prep_fwd(pab, qg, kg, tabs, tm=512):
    def body(p_ref, qg_ref, kg_ref, c_ref, s1_ref, s2_ref, *outs):
        lo = _lo_mask((tm, PAIR))
        c, s1, s2 = c_ref[...], s1_ref[...], s2_ref[...]
        for g in range(3):
            qn_ref, kn_ref, v_ref = outs[3 * g:3 * g + 3]
            for p in range(NPAIR):
                for which, gains, dst in ((0, qg_ref, qn_ref), (1, kg_ref, kn_ref)):
                    col = (2 + 3 * which + g) * 512 + p * PAIR
                    xr = p_ref[:, col:col + PAIR].astype(F32)
                    rinv = lax.rsqrt(_seg_mean(xr * xr, lo) + EPS)
                    dst[p] = _rope(xr * rinv * gains[g:g + 1, :], c, s1, s2)
                col = (8 + g) * 512 + p * PAIR
                v_ref[p] = p_ref[:, col:col + PAIR].astype(F32)

    pm = pl.BlockSpec((NPAIR, tm, PAIR), lambda i: (0, i, 0))
    tab = pl.BlockSpec((tm, PAIR), lambda i: (i, 0))
    gain = pl.BlockSpec((3, PAIR), lambda i: (0, 0))
    return pl.pallas_call(
        body, name="prep_fwd", grid=(T // tm,),
        in_specs=[pl.BlockSpec((tm, AB_IN), lambda i: (i, 0)), gain, gain, tab, tab, tab],
        out_specs=[pm] * 9, out_shape=[_sds((NPAIR, T, PAIR), F32)] * 9,
        compiler_params=_cparams(1))(pab, qg, kg, *tabs)


def _res_index(it, rate):
    window = NBACK * rate
    b = it // rate
    rho = it % rate
    start = b * window + rho
    startp = jnp.maximum(start - window, rho)
    kmin = jnp.where(b > 0, 0, NBACK)
    return start, startp, kmin


def _rows(start, rate):
    if rate == 1:
        return pl.ds(pl.multiple_of(start, NBACK), NBACK)
    return pl.ds(start, NBACK, stride=rate)


def _band():
    qi = lax.broadcasted_iota(jnp.int32, (NBACK, 2 * NBACK), 0)
    kj = lax.broadcasted_iota(jnp.int32, (NBACK, 2 * NBACK), 1)
    dist = qi + NBACK - kj
    return (dist >= 0) & (dist <= NBACK), kj


def _attn_fwd(qn, kn, v, rate, name, dep=None):
    def body(q_ref, k_ref, v_ref, *rest):
        o_ref, l_ref = rest[-2:]
        lo = _lo_mask((NBACK, PAIR))
        band, kj = _band()

        def step(it, carry):
            start, startp, kmin = _res_index(it, rate)
            q = q_ref[_rows(start, rate), :]
            kcat = jnp.concatenate([k_ref[_rows(startp, rate), :], k_ref[_rows(start, rate), :]], axis=0).astype(BF16)
            vcat = jnp.concatenate([v_ref[_rows(startp, rate), :], v_ref[_rows(start, rate), :]], axis=0).astype(BF16)
            ok = band & (kj >= kmin)
            q2 = jnp.concatenate([jnp.where(lo, q, 0.0), jnp.where(lo, 0.0, q)], axis=0).astype(BF16)
            s = lax.dot_general(q2, kcat, (((1,), (1,)), ((), ())), preferred_element_type=F32) * (HEAD ** -0.5)
            s = jnp.where(jnp.concatenate([ok, ok], axis=0), s, NEG_INF)
            m = jnp.max(s, axis=-1, keepdims=True)
            pr = jnp.exp(s - m)
            l = jnp.sum(pr, axis=-1, keepdims=True)
            o2 = jnp.dot(pr.astype(BF16), vcat, preferred_element_type=F32) / l
            ls = m + jnp.log(l)
            o_ref[_rows(start, rate), :] = jnp.where(lo, o2[0:NBACK], o2[NBACK:])
            l_ref[_rows(start, rate), :] = jnp.where(lo, ls[0:NBACK], ls[NBACK:])
            return carry

        lax.fori_loop(0, T // NBACK, step, 0, unroll=4)

    pm = pl.BlockSpec((None, T, PAIR), lambda p: (p, 0, 0))
    in_specs, args = [pm, pm, pm], [qn, kn, v]
    if dep is not None:
        in_specs.append(HBM_SPEC)
        args.append(dep)
    return pl.pallas_call(
        body, name=name, grid=(NPAIR,), in_specs=in_specs, out_specs=[pm, pm],
        out_shape=[_sds((NPAIR, T, PAIR), F32)] * 2, compiler_params=_cparams(1))(*args)


def _merge_fwd(cat_ab, outs, lses, tm=512):
    def body(cat_in, o0, o1, o2, l0, l1, l2, cat_ref, lse_ref):
        del cat_in
        for p in range(NPAIR):
            a0, a1, a2 = l0[p], l1[p], l2[p]
            m = jnp.maximum(jnp.maximum(a0, a1), a2)
            w0, w1, w2 = jnp.exp(a0 - m), jnp.exp(a1 - m), jnp.exp(a2 - m)
            s = w0 + w1 + w2
            b = (w0 * o0[p] + w1 * o1[p] + w2 * o2[p]) / s
            cat_ref[:, p * PAIR:(p + 1) * PAIR] = b.astype(BF16)
            lse_ref[p] = m + jnp.log(s)

    pm = pl.BlockSpec((NPAIR, tm, PAIR), lambda i: (0, i, 0))
    return pl.pallas_call(
        body, name="merge_fwd", grid=(T // tm,),
        in_specs=[pl.BlockSpec(memory_space=pl.ANY)] + [pm] * 6,
        out_specs=[pl.BlockSpec((tm, 512), lambda i: (i, 1)), pm],
        out_shape=[_sds((T, D), BF16), _sds((NPAIR, T, PAIR), F32)],
        input_output_aliases={0: 0}, compiler_params=_cparams(1))(cat_ab, *outs, *lses)


def _b_pre_bwd(dcat, cat, tm=512):
    def body(db_ref, b_ref, dbp_ref, e_ref):
        lo = _lo_mask((tm, PAIR))
        for p in range(NPAIR):
            db = db_ref[:, p * PAIR:(p + 1) * PAIR].astype(F32)
            b = b_ref[:, p * PAIR:(p + 1) * PAIR].astype(F32)
            dbp_ref[p] = db
            e_ref[p] = _seg_mean(db * b, lo) * float(HEAD)

    pm = pl.BlockSpec((NPAIR, tm, PAIR), lambda i: (0, i, 0))
    right = pl.BlockSpec((tm, 512), lambda i: (i, 1))
    return pl.pallas_call(
        body, name="b_pre_bwd", grid=(T // tm,), in_specs=[right, right], out_specs=[pm, pm],
        out_shape=[_sds((NPAIR, T, PAIR), F32)] * 2, compiler_params=_cparams(1))(dcat, cat)


def _attn_bwd(qn, kn, v, dbp, e, lse, rate, name):
    def body(q_ref, k_ref, v_ref, db_ref, e_ref, lse_ref, dq_ref, dk_ref, dv_ref):
        lo = _lo_mask((NBACK, PAIR))
        band, kj = _band()
        scale = HEAD ** -0.5
        nt = (((1,), (1,)), ((), ()))
        tn = (((0,), (0,)), ((), ()))
        window = NBACK * rate
        nblk = T // window

        def one(it, carry):
            dk_carry, dv_carry = carry
            rho = it // nblk
            b = it % nblk
            start = b * window + rho
            rq = _rows(start, rate)
            rp = _rows(jnp.maximum(start - window, rho), rate)
            kmin = jnp.where(b > 0, 0, NBACK)
            q = q_ref[rq, :]
            db = db_ref[rq, :]
            ev = e_ref[rq, :]
            ls = lse_ref[rq, :]
            kcat = jnp.concatenate([k_ref[rp, :], k_ref[rq, :]], axis=0).astype(BF16)
            vcat = jnp.concatenate([v_ref[rp, :], v_ref[rq, :]], axis=0).astype(BF16)
            ok = band & (kj >= kmin)
            ok2 = jnp.concatenate([ok, ok], axis=0)
            q2 = jnp.concatenate([jnp.where(lo, q, 0.0), jnp.where(lo, 0.0, q)], axis=0).astype(BF16)
            db2 = jnp.concatenate([jnp.where(lo, db, 0.0), jnp.where(lo, 0.0, db)], axis=0).astype(BF16)
            ls2 = jnp.concatenate([ls[:, 0:1], ls[:, HEAD:HEAD + 1]], axis=0)
            ev2 = jnp.concatenate([ev[:, 0:1], ev[:, HEAD:HEAD + 1]], axis=0)
            s = lax.dot_general(q2, kcat, nt, preferred_element_type=F32) * scale
            s = jnp.where(ok2, s, NEG_INF)
            pt = jnp.exp(s - ls2)
            dp = lax.dot_general(db2, vcat, nt, preferred_element_type=F32)
            ds = (pt * (dp - ev2)).astype(BF16)
            dq2 = jnp.dot(ds, kcat, preferred_element_type=F32) * scale
            dkc = lax.dot_general(ds, q2, tn, preferred_element_type=F32) * scale
            dvc = lax.dot_general(pt.astype(BF16), db2, tn, preferred_element_type=F32)
            dq_ref[rq, :] = jnp.where(lo, dq2[0:NBACK], dq2[NBACK:])
            dk_ref[rp, :] = dk_carry + dkc[0:NBACK]
            dk_ref[rq, :] = dkc[NBACK:]
            dv_ref[rp, :] = dv_carry + dvc[0:NBACK]
            dv_ref[rq, :] = dvc[NBACK:]
            return dkc[NBACK:], dvc[NBACK:]

        def step(i, carry):
            for u in range(ATTN_BWD_UNROLL):
                carry = one(i * ATTN_BWD_UNROLL + u, carry)
            return carry

        zero = jnp.zeros((NBACK, PAIR), F32)
        lax.fori_loop(0, T // NBACK // ATTN_BWD_UNROLL, step, (zero, zero))

    pm = pl.BlockSpec((None, T, PAIR), lambda p: (p, 0, 0))
    return pl.pallas_call(
        body, name=name, grid=(NPAIR,), in_specs=[pm] * 6, out_specs=[pm] * 3,
        out_shape=[_sds((NPAIR, T, PAIR), F32)] * 3, compiler_params=_cparams(1, 56))(qn, kn, v, dbp, e, lse)


def _ab_in_bwd(pab, dcat, sgu_g, sgu_b, sgu_w, sgu_bias3, qg, kg, tabs, dqkv, tm=256):
    def body(p_ref, dcat_ref, g_ref, b_ref, w_ref, bias_ref, qg_ref, kg_ref, c_ref, s1_ref, s2_ref, *rest):
        dq_refs = rest[0:9]
        o_ref, dwm_ref, dbias_ref, dsg_ref, dsb_ref, dgain_ref = rest[9:]
        i = pl.program_id(0)

        @pl.when(i == 0)
        def _():
            dwm_ref[...] = jnp.zeros_like(dwm_ref)
            dbias_ref[...] = jnp.zeros_like(dbias_ref)
            dsg_ref[...] = jnp.zeros_like(dsg_ref)
            dsb_ref[...] = jnp.zeros_like(dsb_ref)
            dgain_ref[...] = jnp.zeros_like(dgain_ref)

        zu = p_ref[:, 0:512].astype(F32)
        zv = p_ref[:, 512:1024].astype(F32)
        u = _gelu(zu)
        v = _gelu(zv)
        mu = jnp.mean(v, axis=-1, keepdims=True)
        vc = v - mu
        rstd = lax.rsqrt(jnp.mean(vc * vc, axis=-1, keepdims=True) + EPS)
        xhat = vc * rstd
        vn = (xhat * g_ref[...] + b_ref[...]).astype(BF16)
        da = dcat_ref[...].astype(F32)
        tri = _tril_mask()
        du_parts = [[None] * 4 for _ in range(tm // 128)]
        dvn_parts = [[None] * 4 for _ in range(tm // 128)]
        for gi in range(4):
            wg = jnp.where(tri, w_ref[gi], 0.0).astype(BF16)
            bg = bias_ref[gi]
            for c in range(tm // 128):
                rs, cs = slice(c * 128, (c + 1) * 128), slice(gi * 128, (gi + 1) * 128)
                vblk = vn[rs, cs]
                mixed = jnp.dot(wg, vblk, preferred_element_type=F32) + bg
                dab = da[rs, cs]
                du_parts[c][gi] = dab * mixed
                dmixed = dab * u[rs, cs]
                dmb = dmixed.astype(BF16)
                dvn_parts[c][gi] = lax.dot_general(wg, dmb, (((0,), (0,)), ((), ())), preferred_element_type=F32)
                dwm = lax.dot_general(dmb, vblk, (((1,), (1,)), ((), ())), preferred_element_type=F32)
                dwm_ref[gi] += jnp.where(tri, dwm, 0.0)
                dbias_ref[gi] += dmixed
        du = jnp.concatenate([jnp.concatenate(r, axis=1) for r in du_parts], axis=0)
        dvn = jnp.concatenate([jnp.concatenate(r, axis=1) for r in dvn_parts], axis=0)
        dsg_ref[...] += jnp.sum(dvn * xhat, axis=0, keepdims=True)
        dsb_ref[...] += jnp.sum(dvn, axis=0, keepdims=True)
        dxh = dvn * g_ref[...]
        dv = rstd * (dxh - jnp.mean(dxh, axis=-1, keepdims=True)
                     - xhat * jnp.mean(dxh * xhat, axis=-1, keepdims=True))
        o_ref[:, 0:512] = (du * _gelu_grad(zu)).astype(BF16)
        o_ref[:, 512:1024] = (dv * _gelu_grad(zv)).astype(BF16)

        lo = _lo_mask((tm, PAIR))
        c, s1, s2 = c_ref[...], s1_ref[...], s2_ref[...]
        for g in range(3):
            dq_ref, dk_ref, dv_ref = dq_refs[3 * g:3 * g + 3]
            for p in range(NPAIR):
                for which, gains, src in ((0, qg_ref, dq_ref), (1, kg_ref, dk_ref)):
                    col = (2 + 3 * which + g) * 512 + p * PAIR
                    xr = p_ref[:, col:col + PAIR].astype(F32)
                    rinv = lax.rsqrt(_seg_mean(xr * xr, lo) + EPS)
                    xh = xr * rinv
                    dn = _rope_t(src[p], c, s1, s2)
                    row = 2 * g + which
                    dgain_ref[row:row + 1, :] += jnp.sum(dn * xh, axis=0, keepdims=True)
                    dxh2 = dn * gains[g:g + 1, :]
                    dx = rinv * (dxh2 - xh * _seg_mean(dxh2 * xh, lo))
                    o_ref[:, col:col + PAIR] = dx.astype(BF16)
                col = (8 + g) * 512 + p * PAIR
                o_ref[:, col:col + PAIR] = dv_ref[p].astype(BF16)

    pm = pl.BlockSpec((NPAIR, tm, PAIR), lambda i: (0, i, 0))
    tab = pl.BlockSpec((tm, PAIR), lambda i: (i, 0))
    gain = pl.BlockSpec((3, PAIR), lambda i: (0, 0))
    vec = pl.BlockSpec((1, 512), lambda i: (0, 0))
    full = pl.BlockSpec((tm, AB_IN), lambda i: (i, 0))
    w4 = pl.BlockSpec((4, 128, 128), lambda i: (0, 0, 0))
    return pl.pallas_call(
        body, name="ab_in_bwd", grid=(T // tm,),
        in_specs=[full, pl.BlockSpec((tm, 512), lambda i: (i, 0)), vec, vec, w4,
                  pl.BlockSpec((4, 128, 1), lambda i: (0, 0, 0)), gain, gain, tab, tab, tab] + [pm] * 9,
        out_specs=[full, w4, w4, vec, vec, pl.BlockSpec((8, PAIR), lambda i: (0, 0))],
        out_shape=[_sds((T, AB_IN), BF16), _sds((4, 128, 128), F32), _sds((4, 128, 128), F32),
                   _sds((1, 512), F32), _sds((1, 512), F32), _sds((8, PAIR), F32)],
        compiler_params=_cparams(1))(pab, dcat, sgu_g, sgu_b, sgu_w, sgu_bias3, qg, kg, *tabs, *dqkv)


def _ln_stats(x):
    mu = jnp.mean(x, axis=-1, keepdims=True)
    xc = x - mu
    rstd = lax.rsqrt(jnp.mean(xc * xc, axis=-1, keepdims=True) + EPS)
    return xc * rstd, rstd


CONV_RC = 64


def _shifted_copies(src, dst, tm):
    dst[0] = src[...]
    for b in range(1, 8):
        dst[b, 0:tm + HALO - 8, :] = src[pl.ds(b, tm + HALO - 8), :]


def _cd_fwd(pcd, cw, cb, lg, lb, dw, tm=512):
    per = tm // HALO

    def body(p_ref, h_ref, cw_ref, cb_ref, lg_ref, lb_ref, dw_ref, cat_ref, c0_ref, c1_ref, dd_ref, y_ref,
             buf, buf2, sb):
        i = pl.program_id(0)
        live = jnp.where(i > 0, 1.0, 0.0)
        a = p_ref[:, 0:512].astype(F32)
        gt = p_ref[:, 512:1024].astype(F32)
        gb = p_ref[:, 1024:1536].astype(F32)
        gc = p_ref[:, 1536:2048].astype(F32)
        hv = p_ref[:, 2048:2560].astype(F32)
        c0 = a * _sigmoid(gt)
        dd = gc * hv
        buf[0:HALO, :] = h_ref[:, 0:512].astype(F32) * _sigmoid(h_ref[:, 512:1024].astype(F32)) * live
        buf[HALO:, :] = c0
        buf2[0:HALO, :] = h_ref[:, 1536:2048].astype(F32) * h_ref[:, 2048:2560].astype(F32) * live
        buf2[HALO:, :] = dd
        c0_ref[...] = c0.astype(BF16)
        dd_ref[...] = dd.astype(BF16)
        _shifted_copies(buf, sb, tm)

        def conv_rows(r, carry):
            base = pl.multiple_of(r * CONV_RC, CONV_RC)
            for c in range(4):
                lanes = slice(c * 128, (c + 1) * 128)
                acc = jnp.broadcast_to(cb_ref[:, lanes], (CONV_RC, 128))
                for j in range(CONV_C_TAPS):
                    a8, b8 = divmod(HALO - (CONV_C_TAPS - 1) + j, 8)
                    acc = acc + cw_ref[j:j + 1, lanes] * sb[b8, pl.ds(base + 8 * a8, CONV_RC), lanes]
                c1_ref[pl.ds(base, CONV_RC), lanes] = acc
            return carry

        lax.fori_loop(0, tm // CONV_RC, conv_rows, 0)
        xhat, _ = _ln_stats(c1_ref[...])
        c2 = xhat * lg_ref[...] + lb_ref[...]
        y = jnp.zeros((tm, 512), F32)
        for j in range(CONV_D_TAPS):
            y = y + dw_ref[j:j + 1, :] * buf2[pl.ds(HALO - (CONV_D_TAPS - 1) + j, tm), :]
        cat_ref[:, 0:512] = (c2 * _sigmoid(c2)).astype(BF16)
        cat_ref[:, 512:1024] = (gb * y).astype(BF16)
        y_ref[...] = y.astype(BF16)

    half = pl.BlockSpec((tm, 512), lambda i: (i, 0))
    vec = pl.BlockSpec((1, 512), lambda i: (0, 0))
    return pl.pallas_call(
        body, name="cd_fwd", grid=(T // tm,),
        in_specs=[pl.BlockSpec((tm, CD_IN), lambda i: (i, 0)),
                  pl.BlockSpec((HALO, CD_IN), lambda i: (jnp.maximum(i * per - 1, 0), 0)),
                  pl.BlockSpec((32, 512), lambda i: (0, 0)), vec, vec, vec, pl.BlockSpec((8, 512), lambda i: (0, 0))],
        out_specs=[pl.BlockSpec((tm, D), lambda i: (i, 0)), half, half, half, half],
        out_shape=[_sds((T, D), BF16), _sds((T, 512), BF16), _sds((T, 512), F32), _sds((T, 512), BF16),
                   _sds((T, 512), BF16)],
        scratch_shapes=[pltpu.VMEM((HALO + tm, 512), F32), pltpu.VMEM((HALO + tm, 512), F32),
                        pltpu.VMEM((8, HALO + tm, 512), F32)],
        compiler_params=_cparams(1))(pcd, pcd, cw, cb, lg, lb, dw)


def _cd_bwd_pw(dcat, c1, pcd, y, lg, lb, tm=512):
    def body(dcat_ref, c1_ref, gb_ref, y_ref, lg_ref, lb_ref, dc1_ref, dy3_ref, dgb_ref, dlg_ref, dlb_ref, dcb_ref):
        i = pl.program_id(0)

        @pl.when(i == 0)
        def _():
            dlg_ref[...] = jnp.zeros_like(dlg_ref)
            dlb_ref[...] = jnp.zeros_like(dlb_ref)
            dcb_ref[...] = jnp.zeros_like(dcb_ref)

        dc = dcat_ref[:, 0:512].astype(F32)
        ddo = dcat_ref[:, 512:1024].astype(F32)
        xhat, rstd = _ln_stats(c1_ref[...])
        c2 = xhat * lg_ref[...] + lb_ref[...]
        sg = _sigmoid(c2)
        dc2 = dc * sg * (1.0 + c2 * (1.0 - sg))
        dlg_ref[...] += jnp.sum(dc2 * xhat, axis=0, keepdims=True)
        dlb_ref[...] += jnp.sum(dc2, axis=0, keepdims=True)
        dxh = dc2 * lg_ref[...]
        dc1 = rstd * (dxh - jnp.mean(dxh, axis=-1, keepdims=True)
                      - xhat * jnp.mean(dxh * xhat, axis=-1, keepdims=True))
        dcb_ref[...] += jnp.sum(dc1, axis=0, keepdims=True)
        dc1_ref[...] = dc1
        dgb_ref[...] = (ddo * y_ref[...].astype(F32)).astype(BF16)
        dy3_ref[...] = ddo * gb_ref[...].astype(F32)

    half = pl.BlockSpec((tm, 512), lambda i: (i, 0))
    vec = pl.BlockSpec((1, 512), lambda i: (0, 0))
    return pl.pallas_call(
        body, name="cd_bwd_pw", grid=(T // tm,),
        in_specs=[pl.BlockSpec((tm, D), lambda i: (i, 0)), half, pl.BlockSpec((tm, 512), lambda i: (i, 2)), half,
                  vec, vec],
        out_specs=[half, half, half, vec, vec, vec],
        out_shape=[_sds((T, 512), F32), _sds((T, 512), F32), _sds((T, 512), BF16),
                   _sds((1, 512), F32), _sds((1, 512), F32), _sds((1, 512), F32)],
        compiler_params=_cparams(1))(dcat, c1, pcd, y, lg, lb)


def _cd_bwd_conv(pcd, dc1, dy3, c0, dd, dgb, cw, dw, tm=256):
    per = tm // HALO
    nblk = T // tm
    last32 = T // HALO - 1

    def body(p_ref, dc1_ref, dc1n_ref, dy3_ref, dy3n_ref, c0_ref, c0p_ref, dd_ref, ddp_ref, dgb_ref, cw_ref, dw_ref,
             o_ref, dcw_ref, ddw_ref, dbuf, cbuf, d3buf, ddbuf, sd, sc, dc0_buf):
        i = pl.program_id(0)
        has_prev = jnp.where(i > 0, 1.0, 0.0)
        has_next = jnp.where(i < nblk - 1, 1.0, 0.0)

        @pl.when(i == 0)
        def _():
            dcw_ref[...] = jnp.zeros_like(dcw_ref)
            ddw_ref[...] = jnp.zeros_like(ddw_ref)

        dc1 = dc1_ref[...]
        dy3 = dy3_ref[...]
        dbuf[0:tm, :] = dc1
        dbuf[tm:, :] = dc1n_ref[...] * has_next
        d3buf[0:tm, :] = dy3
        d3buf[tm:, :] = dy3n_ref[...] * has_next
        cbuf[0:HALO, :] = c0p_ref[...].astype(F32) * has_prev
        cbuf[HALO:, :] = c0_ref[...].astype(F32)
        ddbuf[0:HALO, :] = ddp_ref[...].astype(F32) * has_prev
        ddbuf[HALO:, :] = dd_ref[...].astype(F32)

        _shifted_copies(dbuf, sd, tm)
        _shifted_copies(cbuf, sc, tm)
        n_tiles = tm // CONV_RC

        def dc0_rows(r, carry):
            base = pl.multiple_of(r * CONV_RC, CONV_RC)
            for c in range(4):
                lanes = slice(c * 128, (c + 1) * 128)
                acc = jnp.zeros((CONV_RC, 128), F32)
                for j in range(CONV_C_TAPS):
                    a8, b8 = divmod(CONV_C_TAPS - 1 - j, 8)
                    acc = acc + cw_ref[j:j + 1, lanes] * sd[b8, pl.ds(base + 8 * a8, CONV_RC), lanes]
                dc0_buf[pl.ds(base, CONV_RC), lanes] = acc
            return carry

        lax.fori_loop(0, n_tiles, dc0_rows, 0)

        for c in range(4):
            lanes = slice(c * 128, (c + 1) * 128)
            for j0 in range(0, CONV_C_TAPS, 8):
                taps = list(range(j0, min(j0 + 8, CONV_C_TAPS)))

                def dw_rows(r, accs, lanes=lanes, taps=taps):
                    base = pl.multiple_of(r * CONV_RC, CONV_RC)
                    d = dbuf[pl.ds(base, CONV_RC), lanes]
                    out = []
                    for acc, j in zip(accs, taps):
                        a8, b8 = divmod(HALO - (CONV_C_TAPS - 1) + j, 8)
                        prod = d * sc[b8, pl.ds(base + 8 * a8, CONV_RC), lanes]
                        out.append(acc + jnp.sum(prod.reshape(CONV_RC // 8, 8, 128), axis=0))
                    return tuple(out)

                accs = lax.fori_loop(0, n_tiles, dw_rows, tuple(jnp.zeros((8, 128), F32) for _ in taps))
                for acc, j in zip(accs, taps):
                    dcw_ref[j:j + 1, lanes] += jnp.sum(acc, axis=0, keepdims=True)

        dc0 = dc0_buf[...]
        ddd = jnp.zeros((tm, 512), F32)
        for j in range(CONV_D_TAPS):
            ddd = ddd + dw_ref[j:j + 1, :] * d3buf[pl.ds(CONV_D_TAPS - 1 - j, tm), :]
            ddw_ref[j:j + 1, :] += jnp.sum(dy3 * ddbuf[pl.ds(HALO - (CONV_D_TAPS - 1) + j, tm), :], axis=0, keepdims=True)

        a = p_ref[:, 0:512].astype(F32)
        gt = p_ref[:, 512:1024].astype(F32)
        gc = p_ref[:, 1536:2048].astype(F32)
        hv = p_ref[:, 2048:2560].astype(F32)
        sg = _sigmoid(gt)
        o_ref[:, 0:512] = (dc0 * sg).astype(BF16)
        o_ref[:, 512:1024] = (dc0 * a * sg * (1.0 - sg)).astype(BF16)
        o_ref[:, 1024:1536] = dgb_ref[...]
        o_ref[:, 1536:2048] = (ddd * hv).astype(BF16)
        o_ref[:, 2048:2560] = (ddd * gc).astype(BF16)

    half = pl.BlockSpec((tm, 512), lambda i: (i, 0))
    nxt = pl.BlockSpec((HALO, 512), lambda i: (jnp.minimum((i + 1) * per, last32), 0))
    prv = pl.BlockSpec((HALO, 512), lambda i: (jnp.maximum(i * per - 1, 0), 0))
    full = pl.BlockSpec((tm, CD_IN), lambda i: (i, 0))
    return pl.pallas_call(
        body, name="cd_bwd_conv", grid=(nblk,),
        in_specs=[full, half, nxt, half, nxt, half, prv, half, prv, half,
                  pl.BlockSpec((32, 512), lambda i: (0, 0)), pl.BlockSpec((8, 512), lambda i: (0, 0))],
        out_specs=[full, pl.BlockSpec((32, 512), lambda i: (0, 0)), pl.BlockSpec((8, 512), lambda i: (0, 0))],
        out_shape=[_sds((T, CD_IN), BF16), _sds((32, 512), F32), _sds((8, 512), F32)],
        scratch_shapes=[pltpu.VMEM((tm + HALO, 512), F32), pltpu.VMEM((HALO + tm, 512), F32),
                        pltpu.VMEM((tm + HALO, 512), F32), pltpu.VMEM((HALO + tm, 512), F32),
                        pltpu.VMEM((8, tm + HALO, 512), F32), pltpu.VMEM((8, HALO + tm, 512), F32),
                        pltpu.VMEM((tm, 512), F32)],
        compiler_params=_cparams(1))(pcd, dc1, dc1, dy3, dy3, c0, c0, dd, dd, dgb, cw, dw)


def _local_step(x, tgt, W, fetch=None, on_grad=None):
    W = dict(W)
    if fetch is None:
        fetch = lambda stage, after: {}
    if on_grad is None:
        on_grad = lambda key, arr: None
    tabs = _rope_tables()
    qg = jnp.tile(W["q_norm_g"], (1, 2))
    kg = jnp.tile(W["k_norm_g"], (1, 2))
    bias3 = W["sgu_bias"].reshape(4, 128, 1)
    G = {}

    h0 = _rms_fwd(x, W["ab_norm_g"], "rms_fwd_ab")
    pab = _mm_nt(h0, W["wt_ab_in"], "mm_ab_in", dep=W.get("dep0"))
    cat_ab = _mix_a_fwd(pab, W["sgu_norm_g"], W["sgu_norm_b"], W["sgu_w"], bias3)
    qkv = _prep_fwd(pab, qg, kg, tabs)
    outs, lses = [], []
    for g, rate in enumerate(DIL_RATES):
        o, l = _attn_fwd(qkv[3 * g], qkv[3 * g + 1], qkv[3 * g + 2], rate, f"attn_fwd_{g}", dep=W.get(f"dep_attn{g}"))
        outs.append(o)
        lses.append(l)
        W.update(fetch(f"attn{g}", o))
    cat_ab, lse = _merge_fwd(cat_ab, outs, lses)
    W.update(fetch("ab_out", lse))
    x1, h1 = _mm_nn(cat_ab, W["w_ab_out"], "mm_ab_out", mode="rms", resid=x, gain=W["ffn_norm_g"][0:1])
    pf0, act0 = _ffn_in(h1, W["wt_ffn_in0"], "ffn_in0")
    W.update(fetch("ffn_down0", act0))
    x2, h2 = _mm_nn(act0, W["w_ffn_down0"], "mm_ffn_down0", mode="rms", resid=x1, gain=W["cd_norm_g"],
                    dep=W.get("dep_down0"))
    W.update(fetch("cd_in", h2))
    pcd = _mm_nt(h2, W["wt_cd_in"], "mm_cd_in")
    cat_cd, c0, c1, dd, yv = _cd_fwd(pcd, W["conv_c_w32"], W["conv_c_b"], W["c_ln_g"], W["c_ln_b"], W["conv_d_w8"])
    x3, h3 = _mm_nn(cat_cd, W["w_cd_out"], "mm_cd_out", mode="rms", resid=x2, gain=W["ffn_norm_g"][1:2])
    pf1, act1 = _ffn_in(h3, W["wt_ffn_in1"], "ffn_in1")
    dy, dyb, loss_cols = _mm_nn(act1, W["w_ffn_down1"], "mm_ffn_down1", mode="loss", resid=x3, tgt=tgt)

    def ffn_bwd(xin, h, pf, act, dres, dresb, layer):
        G[f"w_ffn_down{layer}"] = _mm_tn(act, dresb, f"mm_g_ffn_down{layer}")
        dep = on_grad(f"w_ffn_down{layer}", G[f"w_ffn_down{layer}"])
        dpf = _ffn_dact(dresb, W[f"w_ffn_down{layer}"], pf, f"ffn_dact{layer}", dep=dep)
        G[f"wt_ffn_in{layer}"] = _mm_tn(dpf, h, f"mm_g_ffn_in{layer}")
        dep = on_grad(f"wt_ffn_in{layer}", G[f"wt_ffn_in{layer}"])
        dx, dxb, G[f"ffn_norm_g{layer}"] = _mm_nn(
            dpf, W[f"wt_ffn_in{layer}"], f"mm_d_h_ffn{layer}", mode="rms_bwd", x=xin,
            gain=W["ffn_norm_g"][layer:layer + 1], dres=dres, dep=dep)
        return dx, dxb

    dx3, dx3b = ffn_bwd(x3, h3, pf1, act1, dy, dyb, 1)

    G["w_cd_out"] = _mm_tn(cat_cd, dx3b, "mm_g_cd_out")
    dep = on_grad("w_cd_out", G["w_cd_out"])
    dcat_cd = _mm_nt(dx3b, W["w_cd_out"], "mm_d_cat_cd", dep=dep)
    dc1, dy3, dgb, G["c_ln_g"], G["c_ln_b"], G["conv_c_b"] = _cd_bwd_pw(dcat_cd, c1, pcd, yv, W["c_ln_g"], W["c_ln_b"])
    dpcd, G["conv_c_w32"], G["conv_d_w8"] = _cd_bwd_conv(pcd, dc1, dy3, c0, dd, dgb, W["conv_c_w32"], W["conv_d_w8"])
    G["wt_cd_in"] = _mm_tn(dpcd, h2, "mm_g_cd_in")
    dep = on_grad("wt_cd_in", G["wt_cd_in"])
    dx2, dx2b, G["cd_norm_g"] = _mm_nn(dpcd, W["wt_cd_in"], "mm_d_h_cd", mode="rms_bwd", x=x2, gain=W["cd_norm_g"],
                                       dres=dx3, dep=dep)

    dx1, dx1b = ffn_bwd(x1, h1, pf0, act0, dx2, dx2b, 0)

    G["w_ab_out"] = _mm_tn(cat_ab, dx1b, "mm_g_ab_out")
    dep = on_grad("w_ab_out", G["w_ab_out"])
    dcat_ab = _mm_nt(dx1b, W["w_ab_out"], "mm_d_cat_ab", dep=dep)
    dbp, e = _b_pre_bwd(dcat_ab, cat_ab)
    dqkv = []
    for g, rate in enumerate(DIL_RATES):
        dqkv += _attn_bwd(qkv[3 * g], qkv[3 * g + 1], qkv[3 * g + 2], dbp, e, lse, rate, f"attn_bwd_{g}")
    dpab, G["sgu_w"], dbias_part, G["sgu_norm_g"], G["sgu_norm_b"], dgain = _ab_in_bwd(
        pab, dcat_ab, W["sgu_norm_g"], W["sgu_norm_b"], W["sgu_w"], bias3, qg, kg, tabs, dqkv)
    G["sgu_bias"] = jnp.sum(dbias_part, axis=-1)
    dgain = dgain[0:6, 0:HEAD] + dgain[0:6, HEAD:PAIR]
    G["q_norm_g"] = dgain[0::2]
    G["k_norm_g"] = dgain[1::2]
    G["wt_ab_in"] = _mm_tn(dpab, h0, "mm_g_ab_in")
    dep = on_grad("wt_ab_in", G["wt_ab_in"])
    grad_x, _, G["ab_norm_g"] = _mm_nn(dpab, W["wt_ab_in"], "mm_d_h_ab", mode="rms_bwd", x=x, gain=W["ab_norm_g"],
                                       dres=dx1, dep=dep)
    return loss_cols, grad_x, G


def _my_place():
    return lax.axis_index("x"), lax.axis_index("y"), lax.axis_index("c")


def _dev_index(px, py, pc):
    return 4 * px + 2 * py + pc


def _flip(place, k):
    x, y, c = place
    return (1 - x if k & 4 else x, 1 - y if k & 2 else y, 1 - c if k & 1 else c)


def _landing(shape, dtype, own):
    buf = lax.empty(shape, dtype)
    for lead, part in own:
        buf = lax.dynamic_update_slice(buf, part.reshape((1,) * len(lead) + part.shape),
                                       tuple(lead) + (0,) * part.ndim)
    return buf


def _gather_first(ab_in_t, small, me_index):
    lands = [_landing((NDEV,) + ab_in_t.shape, BF16, [((me_index,), ab_in_t)]),
             _landing((NDEV,) + small.shape, F32, [((me_index,), small)])]
    n_items = 2

    def body(ab_in_r, small_r, l_ab_in, l_small, o_ab_in, o_small, send_sems, recv_sems):
        del l_ab_in, l_small
        x, y, c = _my_place()
        me = (x, y, c)
        sib = (x, y, 1 - c)
        chips = [(1 - x, y), (x, 1 - y), (1 - x, 1 - y)]
        items = [(ab_in_r, lambda d: o_ab_in.at[d]), (small_r, lambda d: o_small.at[d])]

        def rcopy(it, k, src, dst, to):
            return pltpu.make_async_remote_copy(src_ref=src, dst_ref=dst, send_sem=send_sems.at[it, k],
                                                recv_sem=recv_sems.at[it, k], device_id=to, device_id_type=MESH)

        started = []
        for it, (src, dst) in enumerate(items):
            mine = dst(_dev_index(*me))
            first = [rcopy(it, 0, src, mine, sib)]
            first += [rcopy(it, 1 + j, src, mine, (*chip, c)) for j, chip in enumerate(chips)]
            for cp in first:
                cp.start()
            started += first
        for it, (src, dst) in enumerate(items):
            for j, chip in enumerate(chips):
                blk = dst(_dev_index(*chip, c))
                rcopy(it, 1 + j, blk, blk, me).wait_recv()
                fwd = rcopy(it, 4 + j, blk, blk, sib)
                fwd.start()
                started.append(fwd)
        for it, (src, dst) in enumerate(items):
            blk = dst(_dev_index(x, y, 1 - c))
            rcopy(it, 0, blk, blk, me).wait_recv()
            for j, chip in enumerate(chips):
                blk = dst(_dev_index(*chip, 1 - c))
                rcopy(it, 4 + j, blk, blk, me).wait_recv()
        for cp in started:
            cp.wait_send()

    return pl.pallas_call(
        body, name="gather_first", in_specs=[HBM_SPEC] * 4, out_specs=[HBM_SPEC] * 2,
        out_shape=[_sds(a.shape, a.dtype) for a in lands], input_output_aliases={2: 0, 3: 1},
        scratch_shapes=[pltpu.SemaphoreType.DMA((n_items, 7)), pltpu.SemaphoreType.DMA((n_items, 7))],
    )(ab_in_t, small, *lands)


HBM_ONLY = pl.BlockSpec(memory_space=pltpu.HBM)
SEM_SPEC = pl.BlockSpec(memory_space=pltpu.SEMAPHORE)
IN_FLIGHT = pltpu.CompilerParams(has_side_effects=pltpu.SideEffectType.DATAFLOW_SIDE_EFFECTING)


def _in_hbm(a):
    return pltpu.with_memory_space_constraint(a, pltpu.HBM)


def _exchange_start(name, srcs, lands, items, dep=None):
    ns, nl, ni = len(srcs), len(lands), len(items)

    def body(*refs):
        S, L = refs[0:ns], refs[ns:ns + nl]
        first_out = ns + nl + (0 if dep is None else 1)
        send_sems, recv_sems, token = refs[first_out], refs[first_out + 1], refs[-1]
        me = _my_place()
        mi = _dev_index(*me)
        for i, (src, dst) in enumerate(items):
            for k in range(1, NDEV):
                peer = _flip(me, k)
                pltpu.make_async_remote_copy(
                    src_ref=src(S, _dev_index(*peer)), dst_ref=dst(L, mi), send_sem=send_sems.at[7 * i + k - 1],
                    recv_sem=recv_sems.at[7 * i + k - 1], device_id=peer, device_id_type=MESH).start()
        token[...] = jnp.zeros_like(token)

    thru = [pltpu.HBM(a.shape, a.dtype) for a in list(srcs) + list(lands)]
    args = [_in_hbm(a) for a in srcs] + [_in_hbm(a) for a in lands]
    in_specs = [HBM_ONLY] * (ns + nl)
    if dep is not None:
        args.append(dep)
        in_specs.append(HBM_SPEC)
    outs = pl.pallas_call(
        body, name=name, in_specs=in_specs,
        out_shape=(pltpu.SemaphoreType.DMA((7 * ni,)), pltpu.SemaphoreType.DMA((7 * ni,)), *thru, _sds((8, 128), F32)),
        out_specs=(SEM_SPEC, SEM_SPEC, *[HBM_ONLY] * (ns + nl), pl.BlockSpec(memory_space=pltpu.VMEM)),
        input_output_aliases={j: 2 + j for j in range(ns + nl)}, compiler_params=IN_FLIGHT)(*args)
    return dict(send=outs[0], recv=outs[1], srcs=list(outs[2:2 + ns]), lands=list(outs[2 + ns:2 + ns + nl]),
                token=outs[-1], items=items)


def _exchange_wait(name, states, after):
    counts = [(len(st["srcs"]), len(st["lands"]), len(st["items"])) for st in states]
    n_arrays = sum(c[0] + c[1] for c in counts)

    def body(*refs):
        me = _my_place()
        mi = _dev_index(*me)
        pos = 0
        sem_pos = n_arrays
        for st, (ns, nl, ni) in zip(states, counts):
            S, L = refs[pos:pos + ns], refs[pos + ns:pos + ns + nl]
            send_sems, recv_sems = refs[sem_pos], refs[sem_pos + 1]
            pos += ns + nl
            sem_pos += 2
            for i, (src, dst) in enumerate(st["items"]):
                for k in range(1, NDEV):
                    cp = pltpu.make_async_remote_copy(
                        src_ref=src(S, mi), dst_ref=dst(L, mi), send_sem=send_sems.at[7 * i + k - 1],
                        recv_sem=recv_sems.at[7 * i + k - 1], device_id=me, device_id_type=MESH)
                    cp.wait_send()
                    cp.wait_recv()

    arrays, sems = [], []
    for st in states:
        arrays += st["srcs"] + st["lands"]
        sems += [st["send"], st["recv"]]
    outs = pl.pallas_call(
        body, name=name, in_specs=[HBM_ONLY] * n_arrays + [SEM_SPEC] * len(sems) + [HBM_SPEC],
        out_shape=tuple(pltpu.HBM(a.shape, a.dtype) for a in arrays), out_specs=tuple([HBM_ONLY] * n_arrays),
        input_output_aliases={j: j for j in range(n_arrays)}, compiler_params=IN_FLIGHT)(*arrays, *sems, after)
    lands, pos = [], 0
    for ns, nl, _ in counts:
        lands.append(list(outs[pos + ns:pos + ns + nl]))
        pos += ns + nl
    return lands


def _place_and_neighbours():
    x, y, c = _my_place()
    return (x, y, c), (x, y, 1 - c), [(1 - x, y), (x, 1 - y), (1 - x, 1 - y)]


def _gather_start(name, srcs, lands, items, dep=None):
    ns, nl, ni = len(srcs), len(lands), len(items)

    def body(*refs):
        S, L = refs[0:ns], refs[ns:ns + nl]
        first_out = ns + nl + (0 if dep is None else 1)
        send_sems, recv_sems, token = refs[first_out], refs[first_out + 1], refs[-1]
        me, sib, chips = _place_and_neighbours()
        mi = _dev_index(*me)
        for i, (src, dst) in enumerate(items):
            for k, to in enumerate([sib] + [(*chip, me[2]) for chip in chips]):
                pltpu.make_async_remote_copy(
                    src_ref=src(S), dst_ref=dst(L, mi), send_sem=send_sems.at[4 * i + k],
                    recv_sem=recv_sems.at[4 * i + k], device_id=to, device_id_type=MESH).start()
        token[...] = jnp.zeros_like(token)

    thru = [pltpu.HBM(a.shape, a.dtype) for a in list(srcs) + list(lands)]
    args = [_in_hbm(a) for a in srcs] + [_in_hbm(a) for a in lands]
    in_specs = [HBM_ONLY] * (ns + nl)
    if dep is not None:
        args.append(dep)
        in_specs.append(HBM_SPEC)
    outs = pl.pallas_call(
        body, name=name, in_specs=in_specs,
        out_shape=(pltpu.SemaphoreType.DMA((4 * ni,)), pltpu.SemaphoreType.DMA((4 * ni,)), *thru, _sds((8, 128), F32)),
        out_specs=(SEM_SPEC, SEM_SPEC, *[HBM_ONLY] * (ns + nl), pl.BlockSpec(memory_space=pltpu.VMEM)),
        input_output_aliases={j: 2 + j for j in range(ns + nl)}, compiler_params=IN_FLIGHT)(*args)
    return dict(send=outs[0], recv=outs[1], srcs=list(outs[2:2 + ns]), lands=list(outs[2 + ns:2 + ns + nl]),
                token=outs[-1], items=items)


def _gather_forward(name, st, after):
    nl, ni = len(st["lands"]), len(st["items"])

    def body(*refs):
        L, recv_sems = refs[0:nl], refs[nl]
        fwd_send, fwd_recv, token = refs[2 * nl + 2], refs[2 * nl + 3], refs[-1]
        me, sib, chips = _place_and_neighbours()
        for i, (_, dst) in enumerate(st["items"]):
            for j, chip in enumerate(chips):
                blk = dst(L, _dev_index(*chip, me[2]))
                pltpu.make_async_remote_copy(
                    src_ref=blk, dst_ref=blk, send_sem=fwd_send.at[3 * i + j], recv_sem=recv_sems.at[4 * i + 1 + j],
                    device_id=me, device_id_type=MESH).wait_recv()
                pltpu.make_async_remote_copy(
                    src_ref=blk, dst_ref=blk, send_sem=fwd_send.at[3 * i + j], recv_sem=fwd_recv.at[3 * i + j],
                    device_id=sib, device_id_type=MESH).start()
        token[...] = jnp.zeros_like(token)

    outs = pl.pallas_call(
        body, name=name, in_specs=[HBM_ONLY] * nl + [SEM_SPEC, HBM_SPEC],
        out_shape=(*[pltpu.HBM(a.shape, a.dtype) for a in st["lands"]], pltpu.SemaphoreType.DMA((3 * ni,)),
                   pltpu.SemaphoreType.DMA((3 * ni,)), _sds((8, 128), F32)),
        out_specs=(*[HBM_ONLY] * nl, SEM_SPEC, SEM_SPEC, pl.BlockSpec(memory_space=pltpu.VMEM)),
        input_output_aliases={j: j for j in range(nl)}, compiler_params=IN_FLIGHT)(*st["lands"], st["recv"], after)
    return dict(st, lands=list(outs[0:nl]), fwd_send=outs[nl], fwd_recv=outs[nl + 1], token=outs[-1])


def _gather_wait(name, st, after):
    ns, nl, ni = len(st["srcs"]), len(st["lands"]), len(st["items"])

    def body(*refs):
        S, L = refs[0:ns], refs[ns:ns + nl]
        send_sems, recv_sems, fwd_send, fwd_recv = refs[ns + nl:ns + nl + 4]
        me, sib, chips = _place_and_neighbours()
        mi = _dev_index(*me)
        for i, (src, dst) in enumerate(st["items"]):
            mine = dst(L, mi)
            for k in range(4):
                pltpu.make_async_remote_copy(
                    src_ref=src(S), dst_ref=mine, send_sem=send_sems.at[4 * i + k], recv_sem=recv_sems.at[4 * i + k],
                    device_id=me, device_id_type=MESH).wait_send()
            pltpu.make_async_remote_copy(
                src_ref=src(S), dst_ref=mine, send_sem=send_sems.at[4 * i], recv_sem=recv_sems.at[4 * i],
                device_id=me, device_id_type=MESH).wait_recv()
            for j in range(3):
                cp = pltpu.make_async_remote_copy(
                    src_ref=mine, dst_ref=mine, send_sem=fwd_send.at[3 * i + j], recv_sem=fwd_recv.at[3 * i + j],
                    device_id=me, device_id_type=MESH)
                cp.wait_send()
                cp.wait_recv()

    arrays = st["srcs"] + st["lands"]
    outs = pl.pallas_call(
        body, name=name, in_specs=[HBM_ONLY] * (ns + nl) + [SEM_SPEC] * 4 + [HBM_SPEC],
        out_shape=tuple(pltpu.HBM(a.shape, a.dtype) for a in arrays), out_specs=tuple([HBM_ONLY] * (ns + nl)),
        input_output_aliases={j: j for j in range(ns + nl)},
        compiler_params=IN_FLIGHT)(*arrays, st["send"], st["recv"], st["fwd_send"], st["fwd_recv"], after)
    return list(outs[ns:ns + nl])


def _sum_slots(land):
    def body(l_ref, o_ref):
        acc = l_ref[0]
        for d in range(1, NDEV):
            acc = acc + l_ref[d]
        o_ref[...] = acc

    vm = pl.BlockSpec(memory_space=pltpu.VMEM)
    return pl.pallas_call(body, name="sum_small", out_shape=_sds(land.shape[1:], F32), in_specs=[vm], out_specs=vm)(land)


def _adam_math(w, g, m, v):
    m2 = ADAM_B1 * m + (1.0 - ADAM_B1) * g
    v2 = ADAM_B2 * v + (1.0 - ADAM_B2) * (g * g)
    delta = -ADAM_LR * ((m2 * ADAM_C1) / (jnp.sqrt(v2 * ADAM_C2) + ADAM_EPS) + ADAM_WD * w)
    return delta, m2, v2


def _adam_layer(land, sel, w, m, v, layer, name, prev=None, tc=512):
    R = land.shape[2]

    def body(l_ref, w_ref, m_ref, v_ref, *rest):
        g_out, d_out, m_out, v_out = rest[-4:]
        g = l_ref[0].astype(F32)
        for d in range(1, NDEV):
            g = g + l_ref[d].astype(F32)
        delta, m2, v2 = _adam_math(w_ref[...], g, m_ref[...], v_ref[...])
        g_out[...] = g
        d_out[...] = delta
        m_out[...] = m2
        v_out[...] = v2

    wspec = pl.BlockSpec((None, R, tc), lambda i: (layer, 0, i))
    in_specs = [pl.BlockSpec((None, NDEV, R, tc), lambda i: (sel, 0, 0, i)), wspec, wspec, wspec]
    args = [land, w, m, v]
    aliases = {}
    if prev is not None:
        in_specs += [HBM_SPEC] * 4
        args += list(prev)
        aliases = {4 + j: j for j in range(4)}
    return pl.pallas_call(
        body, name=name, grid=(D // tc,), in_specs=in_specs, out_specs=[wspec] * 4,
        out_shape=[_sds(w.shape, F32)] * 4, input_output_aliases=aliases, compiler_params=_cparams(1))(*args)


def _adam_stacked(lands, sel, w, m, v, name):
    res = None
    for layer, land in enumerate(lands):
        res = _adam_layer(land, sel, w, m, v, layer, f"{name}{layer}", prev=res)
    return res


def _adam_small(ws, gs, ms, vs):
    n = len(ws)

    def body(*refs):
        w_r, g_r, m_r, v_r = refs[0:n], refs[n:2 * n], refs[2 * n:3 * n], refs[3 * n:4 * n]
        d_o, m_o, v_o = refs[4 * n:5 * n], refs[5 * n:6 * n], refs[6 * n:7 * n]
        for i in range(n):
            delta, m2, v2 = _adam_math(w_r[i][...], g_r[i][...], m_r[i][...], v_r[i][...])
            d_o[i][...] = delta
            m_o[i][...] = m2
            v_o[i][...] = v2

    vm = pl.BlockSpec(memory_space=pltpu.VMEM)
    shapes = [_sds(w.shape, F32) for w in ws]
    outs = pl.pallas_call(body, name="adam_small", in_specs=[vm] * (4 * n), out_specs=[vm] * (3 * n),
                          out_shape=shapes * 3)(*ws, *gs, *ms, *vs)
    return outs[0:n], outs[n:2 * n], outs[2 * n:3 * n]


WEIGHT_NAMES = ("ab_norm_g", "ab_w_in", "sgu_norm_g", "sgu_norm_b", "sgu_w", "sgu_bias", "q_norm_g", "k_norm_g",
                "ab_w_out", "cd_norm_g", "cd_w_in", "conv_c_w", "conv_c_b", "c_ln_g", "c_ln_b", "conv_d_w",
                "cd_w_out", "ffn_norm_g", "ffn_w_gate", "ffn_w_up", "ffn_w_down")
SMALL_2D = (("ab_norm_g", (1, 1024)), ("sgu_norm_g", (1, 512)), ("sgu_norm_b", (1, 512)), ("sgu_w", (512, 128)),
            ("sgu_bias", (4, 128)), ("q_norm_g", (3, 64)), ("k_norm_g", (3, 64)), ("cd_norm_g", (1, 128)),
            ("conv_c_w", (31, 64)), ("conv_c_b", (1, 64)), ("c_ln_g", (1, 64)), ("c_ln_b", (1, 64)),
            ("conv_d_w", (3, 64)), ("ffn_norm_g", (2, 1024)))
SHARD_C = 64


def _pack_rows(parts, rows):
    flat = jnp.concatenate([p.reshape(-1) for p in parts])
    return jnp.pad(flat, (0, rows * 128 - flat.shape[0])).reshape(rows, 128)


def kernel(x, ab_norm_g, ab_w_in, sgu_norm_g, sgu_norm_b, sgu_w, sgu_bias, q_norm_g, k_norm_g, ab_w_out, cd_norm_g, cd_w_in, conv_c_w, conv_c_b, c_ln_g, c_ln_b, conv_d_w, cd_w_out, ffn_norm_g, ffn_w_gate, ffn_w_up, ffn_w_down, loss_target, m_ab_norm_g, m_ab_w_in, m_sgu_norm_g, m_sgu_norm_b, m_sgu_w, m_sgu_bias, m_q_norm_g, m_k_norm_g, m_ab_w_out, m_cd_norm_g, m_cd_w_in, m_conv_c_w, m_conv_c_b, m_c_ln_g, m_c_ln_b, m_conv_d_w, m_cd_w_out, m_ffn_norm_g, m_ffn_w_gate, m_ffn_w_up, m_ffn_w_down, v_ab_norm_g, v_ab_w_in, v_sgu_norm_g, v_sgu_norm_b, v_sgu_w, v_sgu_bias, v_q_norm_g, v_k_norm_g, v_ab_w_out, v_cd_norm_g, v_cd_w_in, v_conv_c_w, v_conv_c_b, v_c_ln_g, v_c_ln_b, v_conv_d_w, v_cd_w_out, v_ffn_norm_g, v_ffn_w_gate, v_ffn_w_up, v_ffn_w_down):
    w = dict(zip(WEIGHT_NAMES, (ab_norm_g, ab_w_in, sgu_norm_g, sgu_norm_b, sgu_w, sgu_bias, q_norm_g, k_norm_g, ab_w_out, cd_norm_g, cd_w_in, conv_c_w, conv_c_b, c_ln_g, c_ln_b, conv_d_w, cd_w_out, ffn_norm_g, ffn_w_gate, ffn_w_up, ffn_w_down)))
    m = dict(zip(WEIGHT_NAMES, (m_ab_norm_g, m_ab_w_in, m_sgu_norm_g, m_sgu_norm_b, m_sgu_w, m_sgu_bias, m_q_norm_g, m_k_norm_g, m_ab_w_out, m_cd_norm_g, m_cd_w_in, m_conv_c_w, m_conv_c_b, m_c_ln_g, m_c_ln_b, m_conv_d_w, m_cd_w_out, m_ffn_norm_g, m_ffn_w_gate, m_ffn_w_up, m_ffn_w_down)))
    v = dict(zip(WEIGHT_NAMES, (v_ab_norm_g, v_ab_w_in, v_sgu_norm_g, v_sgu_norm_b, v_sgu_w, v_sgu_bias, v_q_norm_g, v_k_norm_g, v_ab_w_out, v_cd_norm_g, v_cd_w_in, v_conv_c_w, v_conv_c_b, v_c_ln_g, v_c_ln_b, v_conv_d_w, v_cd_w_out, v_ffn_norm_g, v_ffn_w_gate, v_ffn_w_up, v_ffn_w_down)))
    me = _dev_index(*_my_place())

    small_local = _pack_rows([w["cd_norm_g"], w["conv_c_w"], w["conv_c_b"], w["c_ln_g"], w["c_ln_b"], w["conv_d_w"]], 24)
    o_ab_in, o_small = _gather_first(w["ab_w_in"][0].T.astype(BF16), small_local, me)
    r_ff = DFF // NDEV
    one = lambda a: (lambda S, j: S[a])
    slot = lambda b: (lambda L, s: L[b].at[s])
    slot2 = lambda b, part: (lambda L, s: L[b].at[part, s])
    shard = lambda a: (lambda S: S[a])

    def layer_shards(layer):
        return (w["ffn_w_gate"][layer].T.astype(BF16), w["ffn_w_up"][layer].T.astype(BF16),
                w["ffn_w_down"][layer].astype(BF16))

    def gathered(own):
        return _landing((NDEV,) + own.shape, BF16, [((me,), own)])

    def gathered2(a, b):
        return _landing((2, NDEV) + a.shape, BF16, [((0, me), a), ((1, me), b)])

    ab_out_s = w["ab_w_out"][0].astype(BF16)
    gate0, up0, down0 = layer_shards(0)
    gathers = {1: _gather_start(
        "gather1_start", [ab_out_s, gate0, up0, down0], [gathered(ab_out_s), gathered2(gate0, up0), gathered(down0)],
        [(shard(0), slot(0)), (shard(1), slot2(1, 0)), (shard(2), slot2(1, 1)), (shard(3), slot(2))], dep=o_small)}

    def fetch(stage, after):
        if stage == "attn0":
            cd_in_s, cd_out_s = w["cd_w_in"][0].T.astype(BF16), w["cd_w_out"][0].astype(BF16)
            gate1, up1, down1 = layer_shards(1)
            gathers[2] = _gather_start(
                "gather2_start", [cd_in_s, cd_out_s, gate1, up1, down1],
                [gathered(cd_in_s), gathered(cd_out_s), gathered2(gate1, up1), gathered(down1)],
                [(shard(0), slot(0)), (shard(1), slot(1)), (shard(2), slot2(2, 0)), (shard(3), slot2(2, 1)),
                 (shard(4), slot(3))], dep=after)
            return {"dep_attn1": gathers[2]["token"]}
        if stage == "attn1":
            gathers[1] = _gather_forward("gather1_forward", gathers[1], after)
            return {"dep_attn2": gathers[1]["token"]}
        if stage == "ab_out":
            l_out, l_ffn, l_down = _gather_wait("gather1_wait", gathers[1], after)
            return {"w_ab_out": l_out.reshape(D, D), "wt_ffn_in0": l_ffn.reshape(2 * DFF, D),
                    "w_ffn_down0": l_down.reshape(DFF, D)}
        if stage == "ffn_down0":
            gathers[2] = _gather_forward("gather2_forward", gathers[2], after)
            return {"dep_down0": gathers[2]["token"]}
        if stage == "cd_in":
            l_in, l_out, l_ffn, l_down = _gather_wait("gather2_wait", gathers[2], after)
            return {"wt_cd_in": l_in.reshape(CD_IN, D), "w_cd_out": l_out.reshape(D, D),
                    "wt_ffn_in1": l_ffn.reshape(2 * DFF, D), "w_ffn_down1": l_down.reshape(DFF, D)}
        return {}

    scatters = {}
    rides_with = {"w_ffn_down1": "wt_ffn_in1", "w_cd_out": "wt_cd_in", "w_ffn_down0": "wt_ffn_in0",
                  "w_ab_out": "wt_ab_in"}
    held = {}

    def on_grad(key, arr):
        if key in rides_with:
            held[rides_with[key]] = (key, arr)
            return None
        group = ([held.pop(key)] if key in held else []) + [(key, arr)]
        srcs, lands, items = [], [], []
        for n, (k, a) in enumerate(group):
            if k.startswith("wt_ffn_in"):
                src = a.reshape(2, NDEV, r_ff, D)
                own = lax.dynamic_slice_in_dim(src, me, 1, axis=1)
                lands.append(lax.dynamic_update_slice(lax.empty(src.shape, BF16), own, (0, me, 0, 0)))
                items += [((lambda S, j, n=n: S[n].at[0, j]), slot2(n, 0)), ((lambda S, j, n=n: S[n].at[1, j]), slot2(n, 1))]
            else:
                rows = a.shape[0] // NDEV
                src = a.reshape(NDEV, rows, D)
                own = lax.dynamic_index_in_dim(src, me, 0, keepdims=False)
                lands.append(_landing((1, NDEV, rows, D), BF16, [((0, me), own)]))
                items.append(((lambda S, j, n=n: S[n].at[j]), slot2(n, 0)))
            srcs.append(src)
        st = _exchange_start(f"scatter_{key}_start", srcs, lands, items)
        scatters[key] = (st, [k for k, _ in group])
        return st["token"]

    flat = o_small.reshape(NDEV, 24 * 128)

    def chan(lo, taps):
        return flat[:, lo:lo + taps * SHARD_C].reshape(NDEV, taps, SHARD_C).transpose(1, 0, 2).reshape(taps, 512)

    W = {
        "wt_ab_in": o_ab_in.reshape(AB_IN, D), "dep0": gathers[1]["token"],
        "ab_norm_g": w["ab_norm_g"], "sgu_norm_g": w["sgu_norm_g"], "sgu_norm_b": w["sgu_norm_b"],
        "sgu_w": w["sgu_w"][0], "sgu_bias": w["sgu_bias"][0], "q_norm_g": w["q_norm_g"][0],
        "k_norm_g": w["k_norm_g"][0], "ffn_norm_g": w["ffn_norm_g"],
        "cd_norm_g": flat[:, 0:128].reshape(1, D),
        "conv_c_w32": jnp.pad(chan(128, CONV_C_TAPS), ((0, 1), (0, 0))),
        "conv_c_b": chan(2112, 1), "c_ln_g": chan(2176, 1), "c_ln_b": chan(2240, 1),
        "conv_d_w8": jnp.pad(chan(2304, CONV_D_TAPS), ((0, 8 - CONV_D_TAPS), (0, 0))),
    }

    loss_cols, grad_x, G = _local_step(x[0], loss_target[0], W, fetch, on_grad)
    loss = lax.psum(jnp.sum(loss_cols), ("x", "y", "c"))

    small_parts = [G["ab_norm_g"], G["sgu_norm_g"], G["sgu_norm_b"], G["sgu_w"], G["sgu_bias"], G["q_norm_g"],
                   G["k_norm_g"], G["cd_norm_g"], G["conv_c_w32"][:CONV_C_TAPS], G["conv_c_b"], G["c_ln_g"],
                   G["c_ln_b"], G["conv_d_w8"][:CONV_D_TAPS], G["ffn_norm_g0"], G["ffn_norm_g1"]]
    sizes = [p.size for p in small_parts]
    small_rows = 712
    packed = _pack_rows(small_parts, small_rows)
    small = _exchange_start("small_start", [packed], [_landing((NDEV, small_rows, 128), F32, [((me,), packed)])],
                            [(one(0), slot(0))])
    landed = {}

    def wait_scatters(name, group_keys, after):
        res = _exchange_wait(name, [scatters[gk][0] for gk in group_keys], after)
        for gk, lands in zip(group_keys, res):
            landed.update(zip(scatters[gk][1], lands))

    wait_scatters("scatter_wait_early", ["wt_ffn_in1", "wt_cd_in", "wt_ffn_in0"], small["token"])

    grads, deltas, new_m, new_v = {}, {}, {}, {}

    def put(name, res):
        grads[name], deltas[name], new_m[name], new_v[name] = res

    def adam(name, lands, sel, transposed):
        flip = (lambda a: jnp.swapaxes(a, 1, 2)) if transposed else (lambda a: a)
        res = _adam_stacked(lands, sel, flip(w[name]), flip(m[name]), flip(v[name]), f"adam_{name}")
        put(name, [flip(r) for r in res])

    ffn_in_lands = [landed["wt_ffn_in0"], landed["wt_ffn_in1"]]
    adam("cd_w_in", [landed["wt_cd_in"]], 0, True)
    adam("ffn_w_gate", ffn_in_lands, 0, True)
    adam("ffn_w_up", ffn_in_lands, 1, True)
    adam("cd_w_out", [landed["w_cd_out"]], 0, False)
    adam("ffn_w_down", [landed["w_ffn_down0"], landed["w_ffn_down1"]], 0, False)

    small_land = _exchange_wait("small_wait", [small], deltas["ffn_w_down"])[0][0]
    red = _sum_slots(small_land).reshape(-1)
    offs = [0]
    for s in sizes:
        offs.append(offs[-1] + s)
    seg = [red[offs[i]:offs[i + 1]] for i in range(len(sizes))]

    def own_channels(full, taps):
        return lax.dynamic_slice_in_dim(full.reshape(taps, 512), me * SHARD_C, SHARD_C, axis=1)

    g_small = {
        "ab_norm_g": seg[0].reshape(1, 1024), "sgu_norm_g": seg[1].reshape(1, 512), "sgu_norm_b": seg[2].reshape(1, 512),
        "sgu_w": seg[3].reshape(512, 128), "sgu_bias": seg[4].reshape(4, 128), "q_norm_g": seg[5].reshape(3, 64),
        "k_norm_g": seg[6].reshape(3, 64),
        "cd_norm_g": lax.dynamic_slice_in_dim(seg[7].reshape(1, D), me * (D // NDEV), D // NDEV, axis=1),
        "conv_c_w": own_channels(seg[8], CONV_C_TAPS), "conv_c_b": own_channels(seg[9], 1),
        "c_ln_g": own_channels(seg[10], 1), "c_ln_b": own_channels(seg[11], 1),
        "conv_d_w": own_channels(seg[12], CONV_D_TAPS),
        "ffn_norm_g": jnp.concatenate([seg[13].reshape(1, D), seg[14].reshape(1, D)], axis=0),
    }

    names2d = [n for n, _ in SMALL_2D]
    d_s, m_s, v_s = _adam_small([w[n].reshape(s) for n, s in SMALL_2D], [g_small[n] for n in names2d],
                                [m[n].reshape(s) for n, s in SMALL_2D], [v[n].reshape(s) for n, s in SMALL_2D])
    for i, n in enumerate(names2d):
        shape = w[n].shape
        grads[n], deltas[n] = g_small[n].reshape(shape), d_s[i].reshape(shape)
        new_m[n], new_v[n] = m_s[i].reshape(shape), v_s[i].reshape(shape)

    wait_scatters("scatter_wait_last", ["wt_ab_in"], d_s[0])
    adam("ab_w_out", [landed["w_ab_out"]], 0, False)
    adam("ab_w_in", [landed["wt_ab_in"]], 0, True)

    return (loss, grad_x[None], *[grads[n] for n in WEIGHT_NAMES], *[deltas[n] for n in WEIGHT_NAMES],
            *[new_m[n] for n in WEIGHT_NAMES], *[new_v[n] for n in WEIGHT_NAMES])
```

```python
import functools

import jax
import jax.numpy as jnp
import numpy as np
from jax import lax
from jax.experimental import pallas as pl
from jax.experimental.pallas import tpu as pltpu

F32 = jnp.float32
BF16 = jnp.bfloat16

T = 4096
D = 1024
NDEV = 8
EPS = 1e-6
NEG_INF = -1e30
DFF = 2816
AB_IN = 5632
CD_IN = 2560
HEAD = 64
PAIR = 128
NPAIR = 4
NBACK = 128
DIL_RATES = (1, 4, 16)
ROPE_HALF = 8
ROPE_THETA = 500000.0
CONV_C_TAPS = 31
CONV_D_TAPS = 3
HALO = 32
ATTN_BWD_UNROLL = 4

ADAM_LR = 0.001
ADAM_B1 = 0.9
ADAM_B2 = 0.999
ADAM_EPS = 1e-08
ADAM_WD = 0.01
ADAM_STEP = 10
ADAM_C1 = 1.0 / (1.0 - ADAM_B1 ** ADAM_STEP)
ADAM_C2 = 1.0 / (1.0 - ADAM_B2 ** ADAM_STEP)

VMEM_LIMIT_MB = 48
MESH = pl.DeviceIdType.MESH
HBM_SPEC = pl.BlockSpec(memory_space=pl.ANY)


def _cparams(ngrid, vmem_mb=VMEM_LIMIT_MB):
    return pltpu.CompilerParams(dimension_semantics=("arbitrary",) * ngrid,
                                vmem_limit_bytes=vmem_mb * 1024 * 1024)


def _pick(n, options):
    for o in options:
        if n % o == 0:
            return o
    raise ValueError(f"no tile for {n} in {options}")


def _sds(shape, dtype):
    return jax.ShapeDtypeStruct(shape, dtype)


def _sigmoid(x):
    return 1.0 / (1.0 + jnp.exp(-x))


def _sigmoid_bf16(x):
    return 0.5 * jnp.tanh(0.5 * x) + 0.5


def _gelu(z):
    return 0.5 * z * (1.0 + lax.erf(z * 0.7071067811865476))


def _gelu_grad(z):
    return 0.5 * (1.0 + lax.erf(z * 0.7071067811865476)) + z * jnp.exp(-0.5 * z * z) * 0.3989422804014327


def _mm_nt(a, wt, name, out_dtype=BF16, tm=2048, dep=None):
    M, K = a.shape
    N = wt.shape[0]
    tn = _pick(N, (512, 256))

    def body(a_ref, w_ref, *rest):
        o_ref = rest[-1]
        o_ref[...] = lax.dot_general(a_ref[...], w_ref[...], (((1,), (1,)), ((), ())),
                                     preferred_element_type=F32).astype(o_ref.dtype)

    in_specs = [pl.BlockSpec((tm, K), lambda i, j: (i, 0)), pl.BlockSpec((tn, K), lambda i, j: (j, 0))]
    args = [a, wt]
    if dep is not None:
        in_specs.append(HBM_SPEC)
        args.append(dep)
    return pl.pallas_call(
        body, name=name, grid=(M // tm, N // tn), in_specs=in_specs,
        out_specs=pl.BlockSpec((tm, tn), lambda i, j: (i, j)),
        out_shape=_sds((M, N), out_dtype), compiler_params=_cparams(2))(*args)


EPI_ROWS = 256


def _mm_nn(a, w, name, mode, resid, gain=None, tgt=None, dep=None, tm=1024):
    M, K = a.shape
    N = w.shape[1]
    tk = _pick(K, (1408, 1280, 1024, 512))
    nk = K // tk
    side = gain if mode == "rms" else tgt

    def body(a_ref, w_ref, resid_ref, side_ref, *rest):
        outs, acc = rest[-3 if mode == "rms" else -4:-1], rest[-1]
        i, k = pl.program_id(0), pl.program_id(1)

        @pl.when(k == 0)
        def _():
            acc[...] = jnp.zeros_like(acc)

        acc[...] += jnp.dot(a_ref[...], w_ref[...], preferred_element_type=F32)

        if mode == "loss":
            @pl.when((k == 0) & (i == 0))
            def _():
                outs[2][...] = jnp.zeros_like(outs[2])

        @pl.when(k == nk - 1)
        def _():
            for r0 in range(0, tm, EPI_ROWS):
                rows = slice(r0, r0 + EPI_ROWS)
                v = acc[rows, :] + resid_ref[rows, :]
                if mode == "rms":
                    outs[0][rows, :] = v
                    r = lax.rsqrt(jnp.mean(v * v, axis=-1, keepdims=True) + EPS)
                    outs[1][rows, :] = (v * r * side_ref[...]).astype(BF16)
                else:
                    d = v - side_ref[rows, :]
                    outs[2][...] += jnp.sum(d * d, axis=0, keepdims=True) * (0.5 / N)
                    dy = d * (1.0 / N)
                    outs[0][rows, :] = dy
                    outs[1][rows, :] = dy.astype(BF16)

    row = pl.BlockSpec((tm, N), lambda i, k: (i, 0))
    vec = pl.BlockSpec((1, N), lambda i, k: (0, 0))
    in_specs = [pl.BlockSpec((tm, tk), lambda i, k: (i, k)), pl.BlockSpec((tk, N), lambda i, k: (k, 0)), row,
                vec if mode == "rms" else row]
    args = [a, w, resid, side]
    if dep is not None:
        in_specs.append(HBM_SPEC)
        args.append(dep)
    if mode == "rms":
        out_specs, out_shape = [row, row], [_sds((M, N), F32), _sds((M, N), BF16)]
    else:
        out_specs, out_shape = [row, row, vec], [_sds((M, N), F32), _sds((M, N), BF16), _sds((1, N), F32)]
    return pl.pallas_call(
        body, name=name, grid=(M // tm, nk), in_specs=in_specs, out_specs=out_specs, out_shape=out_shape,
        scratch_shapes=[pltpu.VMEM((tm, N), F32)], compiler_params=_cparams(2))(*args)


def _mm_dh_rms_bwd(a, w, x, gain, dres, name, dep=None, tm=512):
    parts = a.shape[0] if a.ndim == 3 else 1
    M, Kp = a.shape[-2], a.shape[-1]
    N = w.shape[1]
    tk = _pick(Kp, (1408, 1280, 1024, 512))
    kper = Kp // tk
    nk = parts * kper
    nblk = M // tm
    er = tm // nk
    assert nblk % 2 == 0 and er % 8 == 0

    def body(a_ref, w_ref, x_ref, g_ref, dres_ref, *rest):
        dx_ref, dxb_ref, dg_ref, acc0, acc1 = rest[-5:]
        i, k = pl.program_id(0), pl.program_id(1)

        def matmul(acc):
            d = jnp.dot(a_ref[...], w_ref[...], preferred_element_type=F32)
            acc[...] = jnp.where(k == 0, d, d + acc[...])

        def finish(acc):
            rows = pl.ds(pl.multiple_of(k * er, er), er)
            v = acc[rows, :]
            xf = x_ref[rows, :]
            r = lax.rsqrt(jnp.mean(xf * xf, axis=-1, keepdims=True) + EPS)
            xhat = xf * r
            dg_ref[...] += jnp.sum(v * xhat, axis=0, keepdims=True)
            dxh = v * g_ref[...]
            tot = dres_ref[rows, :] + r * (dxh - xhat * jnp.mean(dxh * xhat, axis=-1, keepdims=True))
            dx_ref[rows, :] = tot
            dxb_ref[rows, :] = tot.astype(BF16)

        @pl.when(i == 0)
        def _():
            @pl.when(k == 0)
            def _():
                dg_ref[...] = jnp.zeros_like(dg_ref)
            matmul(acc0)

        @pl.when((i > 0) & (i < nblk) & (i % 2 == 1))
        def _():
            matmul(acc1)
            finish(acc0)

        @pl.when((i > 0) & (i < nblk) & (i % 2 == 0))
        def _():
            matmul(acc0)
            finish(acc1)

        @pl.when(i == nblk)
        def _():
            finish(acc1)

    last = nblk - 1
    row = pl.BlockSpec((tm, N), lambda i, k: (jnp.maximum(i - 1, 0), 0))
    vec = pl.BlockSpec((1, N), lambda i, k: (0, 0))
    if a.ndim == 3:
        a_spec = pl.BlockSpec((None, tm, tk), lambda i, k: (k // kper, jnp.minimum(i, last), k % kper))
    else:
        a_spec = pl.BlockSpec((tm, tk), lambda i, k: (jnp.minimum(i, last), k))
    in_specs = [a_spec, pl.BlockSpec((tk, N), lambda i, k: (k, 0)), row, vec, row]
    args = [a, w, x, gain, dres]
    if dep is not None:
        in_specs.append(HBM_SPEC)
        args.append(dep)
    return pl.pallas_call(
        body, name=name, grid=(nblk + 1, nk), in_specs=in_specs, out_specs=[row, row, vec],
        out_shape=[_sds((M, N), F32), _sds((M, N), BF16), _sds((1, N), F32)],
        scratch_shapes=[pltpu.VMEM((tm, N), F32), pltpu.VMEM((tm, N), F32)], compiler_params=_cparams(2))(*args)


def _mm_tn(a, b, name, out_dtype=BF16, tt=1024):
    parts = a.shape[0] if a.ndim == 3 else 1
    Tt, Mp = a.shape[-2], a.shape[-1]
    N = b.shape[1]
    tn = _pick(Mp, (1408, 1280, 1024, 512))
    jper = Mp // tn
    nt = Tt // tt

    def body(a_ref, b_ref, o_ref, acc):
        t = pl.program_id(1)

        @pl.when(t == 0)
        def _():
            acc[...] = jnp.zeros_like(acc)

        acc[...] += lax.dot_general(a_ref[...], b_ref[...], (((0,), (0,)), ((), ())),
                                    preferred_element_type=F32)

        @pl.when(t == nt - 1)
        def _():
            o_ref[...] = acc[...].astype(o_ref.dtype)

    if a.ndim == 3:
        a_spec = pl.BlockSpec((None, tt, tn), lambda j, t: (j // jper, t, j % jper))
    else:
        a_spec = pl.BlockSpec((tt, tn), lambda j, t: (t, j))
    return pl.pallas_call(
        body, name=name, grid=(parts * jper, nt),
        in_specs=[a_spec, pl.BlockSpec((tt, N), lambda j, t: (t, 0))],
        out_specs=pl.BlockSpec((tn, N), lambda j, t: (j, 0)),
        out_shape=_sds((parts * Mp, N), out_dtype), scratch_shapes=[pltpu.VMEM((tn, N), F32)],
        compiler_params=_cparams(2))(a, b)


def _ffn_in(h, wt_in, name, tm=2048, tn=256):
    nj = DFF // tn

    def body(h_ref, wg_ref, wu_ref, p_ref, act_ref):
        nt = (((1,), (1,)), ((), ()))
        g = lax.dot_general(h_ref[...], wg_ref[...], nt, preferred_element_type=F32).astype(BF16)
        u = lax.dot_general(h_ref[...], wu_ref[...], nt, preferred_element_type=F32).astype(BF16)
        p_ref[0] = g
        p_ref[1] = u
        act_ref[...] = g * _sigmoid_bf16(g) * u

    return pl.pallas_call(
        body, name=name, grid=(T // tm, nj),
        in_specs=[pl.BlockSpec((tm, D), lambda i, j: (i, 0)), pl.BlockSpec((tn, D), lambda i, j: (j, 0)),
                  pl.BlockSpec((tn, D), lambda i, j: (j + nj, 0))],
        out_specs=[pl.BlockSpec((2, tm, tn), lambda i, j: (0, i, j)), pl.BlockSpec((tm, tn), lambda i, j: (i, j))],
        out_shape=[_sds((2, T, DFF), BF16), _sds((T, DFF), BF16)], compiler_params=_cparams(2))(h, wt_in, wt_in)


def _ffn_dact(dyb, w_down, p3, name, tm=2048, tn=256, dep=None):
    def body(dy_ref, w_ref, p_ref, *rest):
        o_ref = rest[-1]
        da = lax.dot_general(dy_ref[...], w_ref[...], (((1,), (1,)), ((), ())),
                             preferred_element_type=F32).astype(BF16)
        g = p_ref[0]
        u = p_ref[1]
        sg = _sigmoid_bf16(g)
        gs = g * sg
        o_ref[0] = (da * u) * (sg + gs * (1.0 - sg))
        o_ref[1] = da * gs

    pspec = pl.BlockSpec((2, tm, tn), lambda i, j: (0, i, j))
    in_specs = [pl.BlockSpec((tm, D), lambda i, j: (i, 0)), pl.BlockSpec((tn, D), lambda i, j: (j, 0)), pspec]
    args = [dyb, w_down, p3]
    if dep is not None:
        in_specs.append(HBM_SPEC)
        args.append(dep)
    return pl.pallas_call(
        body, name=name, grid=(T // tm, DFF // tn), in_specs=in_specs, out_specs=pspec,
        out_shape=_sds((2, T, DFF), BF16), compiler_params=_cparams(2))(*args)


def _rms_fwd(x, g, name, tm=512):
    def body(x_ref, g_ref, h_ref):
        xf = x_ref[...]
        r = lax.rsqrt(jnp.mean(xf * xf, axis=-1, keepdims=True) + EPS)
        h_ref[...] = (xf * r * g_ref[...]).astype(BF16)

    return pl.pallas_call(
        body, name=name, grid=(T // tm,),
        in_specs=[pl.BlockSpec((tm, D), lambda i: (i, 0)), pl.BlockSpec((1, D), lambda i: (0, 0))],
        out_specs=pl.BlockSpec((tm, D), lambda i: (i, 0)),
        out_shape=_sds((T, D), BF16), compiler_params=_cparams(1))(x, g)


def _tril_mask():
    r = lax.broadcasted_iota(jnp.int32, (128, 128), 0)
    c = lax.broadcasted_iota(jnp.int32, (128, 128), 1)
    return r >= c


def _mix_a_fwd(pab, sgu_g, sgu_b, sgu_w, sgu_bias3, tm=512):
    def body(zu_ref, zv_ref, g_ref, b_ref, w_ref, bias_ref, o_ref):
        u = _gelu(zu_ref[...].astype(F32))
        v = _gelu(zv_ref[...].astype(F32))
        mu = jnp.mean(v, axis=-1, keepdims=True)
        vc = v - mu
        rstd = lax.rsqrt(jnp.mean(vc * vc, axis=-1, keepdims=True) + EPS)
        vn = (vc * rstd * g_ref[...] + b_ref[...]).astype(BF16)
        tri = _tril_mask()
        for gi in range(4):
            wg = jnp.where(tri, w_ref[gi], 0.0).astype(BF16)
            bg = bias_ref[gi]
            for c in range(tm // 128):
                rs, cs = slice(c * 128, (c + 1) * 128), slice(gi * 128, (gi + 1) * 128)
                mixed = jnp.dot(wg, vn[rs, cs], preferred_element_type=F32) + bg
                o_ref[rs, cs] = (u[rs, cs] * mixed).astype(BF16)

    half = pl.BlockSpec((tm, 512), lambda i: (i, 0))
    return pl.pallas_call(
        body, name="mix_a_fwd", grid=(T // tm,),
        in_specs=[half, pl.BlockSpec((tm, 512), lambda i: (i, 1)),
                  pl.BlockSpec((1, 512), lambda i: (0, 0)), pl.BlockSpec((1, 512), lambda i: (0, 0)),
                  pl.BlockSpec((4, 128, 128), lambda i: (0, 0, 0)), pl.BlockSpec((4, 128, 1), lambda i: (0, 0, 0))],
        out_specs=half, out_shape=_sds((T, D), BF16), compiler_params=_cparams(1),
    )(pab, pab, sgu_g, sgu_b, sgu_w, sgu_bias3)


def _rope_tables():
    pos = np.arange(T, dtype=np.float32)
    inv_freq = np.float32(ROPE_THETA) ** (-np.arange(ROPE_HALF, dtype=np.float32) * np.float32(2.0 / (2 * ROPE_HALF)))
    ang = (pos[:, None] * inv_freq[None, :]).astype(np.float32)
    cos, sin = np.cos(ang), np.sin(ang)
    z8 = np.zeros((T, ROPE_HALF), np.float32)
    rest = np.zeros((T, HEAD - 2 * ROPE_HALF), np.float32)
    c64 = np.concatenate([cos, cos, rest + 1.0], axis=1)
    s1 = np.concatenate([z8, sin, rest], axis=1)
    s2 = np.concatenate([-sin, z8, rest], axis=1)
    return tuple(jnp.asarray(np.tile(t, (1, 2)).astype(np.float32)) for t in (c64, s1, s2))


def _lo_mask(shape):
    return lax.broadcasted_iota(jnp.int32, shape, 1) < HEAD


def _seg_mean(x, lo):
    s_all = jnp.sum(x, axis=-1, keepdims=True)
    s_lo = jnp.sum(jnp.where(lo, x, 0.0), axis=-1, keepdims=True)
    return jnp.where(lo, s_lo, s_all - s_lo) * (1.0 / HEAD)


def _rope(n, c, s1, s2):
    return n * c + pltpu.roll(n, ROPE_HALF, 1) * s1 + pltpu.roll(n, PAIR - ROPE_HALF, 1) * s2


def _rope_t(dy, c, s1, s2):
    return dy * c - pltpu.roll(dy, PAIR - ROPE_HALF, 1) * s2 - pltpu.roll(dy, ROPE_HALF, 1) * s1


def _prep_fwd(pab, qg, kg, tabs, tm=512):
    def body(p_ref, qg_ref, kg_ref, c_ref, s1_ref, s2_ref, *outs):
        lo = _lo_mask((tm, PAIR))
        c, s1, s2 = c_ref[...], s1_ref[...], s2_ref[...]
        for g in range(3):
            qn_ref, kn_ref, v_ref = outs[3 * g:3 * g + 3]
            for p in range(NPAIR):
                for which, gains, dst in ((0, qg_ref, qn_ref), (1, kg_ref, kn_ref)):
                    col = (2 + 3 * which + g) * 512 + p * PAIR
                    xr = p_ref[:, col:col + PAIR].astype(F32)
                    rinv = lax.rsqrt(_seg_mean(xr * xr, lo) + EPS)
                    dst[p] = _rope(xr * rinv * gains[g:g + 1, :], c, s1, s2)
                col = (8 + g) * 512 + p * PAIR
                v_ref[p] = p_ref[:, col:col + PAIR].astype(F32)

    pm = pl.BlockSpec((NPAIR, tm, PAIR), lambda i: (0, i, 0))
    tab = pl.BlockSpec((tm, PAIR), lambda i: (i, 0))
    gain = pl.BlockSpec((3, PAIR), lambda i: (0, 0))
    return pl.pallas_call(
        body, name="prep_fwd", grid=(T // tm,),
        in_specs=[pl.BlockSpec((tm, AB_IN), lambda i: (i, 0)), gain, gain, tab, tab, tab],
        out_specs=[pm] * 9, out_shape=[_sds((NPAIR, T, PAIR), F32)] * 9,
        compiler_params=_cparams(1))(pab, qg, kg, *tabs)


def _res_index(it, rate):
    window = NBACK * rate
    b = it // rate
    rho = it % rate
    start = b * window + rho
    startp = jnp.maximum(start - window, rho)
    kmin = jnp.where(b > 0, 0, NBACK)
    return start, startp, kmin


def _rows(start, rate):
    if rate == 1:
        return pl.ds(pl.multiple_of(start, NBACK), NBACK)
    return pl.ds(start, NBACK, stride=rate)


def _band():
    qi = lax.broadcasted_iota(jnp.int32, (NBACK, 2 * NBACK), 0)
    kj = lax.broadcasted_iota(jnp.int32, (NBACK, 2 * NBACK), 1)
    dist = qi + NBACK - kj
    return (dist >= 0) & (dist <= NBACK), kj


def _attn_fwd(qn, kn, v, rate, name, dep=None):
    def body(q_ref, k_ref, v_ref, *rest):
        o_ref, l_ref = rest[-2:]
        lo = _lo_mask((NBACK, PAIR))
        band, kj = _band()

        def step(it, carry):
            start, startp, kmin = _res_index(it, rate)
            q = q_ref[_rows(start, rate), :]
            kcat = jnp.concatenate([k_ref[_rows(startp, rate), :], k_ref[_rows(start, rate), :]], axis=0).astype(BF16)
            vcat = jnp.concatenate([v_ref[_rows(startp, rate), :], v_ref[_rows(start, rate), :]], axis=0).astype(BF16)
            vcat1 = jnp.concatenate([vcat, jnp.ones((2 * NBACK, PAIR), BF16)], axis=1)
            ok = band & (kj >= kmin)
            q2 = jnp.concatenate([jnp.where(lo, q, 0.0), jnp.where(lo, 0.0, q)], axis=0).astype(BF16)
            s = lax.dot_general(q2, kcat, (((1,), (1,)), ((), ())), preferred_element_type=F32) * (HEAD ** -0.5)
            s = jnp.where(jnp.concatenate([ok, ok], axis=0), s, NEG_INF)
            m = jnp.max(s, axis=-1, keepdims=True)
            ol = jnp.dot(jnp.exp(s - m).astype(BF16), vcat1, preferred_element_type=F32)
            o2 = ol[:, 0:PAIR] / ol[:, PAIR:]
            ls = m + jnp.log(ol[:, PAIR:])
            o_ref[_rows(start, rate), :] = jnp.where(lo, o2[0:NBACK], o2[NBACK:])
            l_ref[_rows(start, rate), :] = jnp.where(lo, ls[0:NBACK], ls[NBACK:])
            return carry

        lax.fori_loop(0, T // NBACK, step, 0, unroll=4)

    pm = pl.BlockSpec((None, T, PAIR), lambda p: (p, 0, 0))
    in_specs, args = [pm, pm, pm], [qn, kn, v]
    if dep is not None:
        in_specs.append(HBM_SPEC)
        args.append(dep)
    return pl.pallas_call(
        body, name=name, grid=(NPAIR,), in_specs=in_specs, out_specs=[pm, pm],
        out_shape=[_sds((NPAIR, T, PAIR), F32)] * 2, compiler_params=_cparams(1))(*args)


def _merge_fwd(cat_ab, outs, lses, tm=512):
    def body(cat_in, o0, o1, o2, l0, l1, l2, cat_ref, lse_ref):
        del cat_in
        for p in range(NPAIR):
            a0, a1, a2 = l0[p], l1[p], l2[p]
            m = jnp.maximum(jnp.maximum(a0, a1), a2)
            w0, w1, w2 = jnp.exp(a0 - m), jnp.exp(a1 - m), jnp.exp(a2 - m)
            s = w0 + w1 + w2
            b = (w0 * o0[p] + w1 * o1[p] + w2 * o2[p]) / s
            cat_ref[:, p * PAIR:(p + 1) * PAIR] = b.astype(BF16)
            lse_ref[p] = m + jnp.log(s)

    pm = pl.BlockSpec((NPAIR, tm, PAIR), lambda i: (0, i, 0))
    return pl.pallas_call(
        body, name="merge_fwd", grid=(T // tm,),
        in_specs=[pl.BlockSpec(memory_space=pl.ANY)] + [pm] * 6,
        out_specs=[pl.BlockSpec((tm, 512), lambda i: (i, 1)), pm],
        out_shape=[_sds((T, D), BF16), _sds((NPAIR, T, PAIR), F32)],
        input_output_aliases={0: 0}, compiler_params=_cparams(1))(cat_ab, *outs, *lses)


def _b_pre_bwd(dcat, cat, tm=512):
    def body(db_ref, b_ref, dbp_ref, e_ref):
        lo = _lo_mask((tm, PAIR))
        for p in range(NPAIR):
            db = db_ref[:, p * PAIR:(p + 1) * PAIR].astype(F32)
            b = b_ref[:, p * PAIR:(p + 1) * PAIR].astype(F32)
            dbp_ref[p] = db
            e_ref[p] = _seg_mean(db * b, lo) * float(HEAD)

    pm = pl.BlockSpec((NPAIR, tm, PAIR), lambda i: (0, i, 0))
    right = pl.BlockSpec((tm, 512), lambda i: (i, 1))
    return pl.pallas_call(
        body, name="b_pre_bwd", grid=(T // tm,), in_specs=[right, right], out_specs=[pm, pm],
        out_shape=[_sds((NPAIR, T, PAIR), F32)] * 2, compiler_params=_cparams(1))(dcat, cat)


def _attn_bwd(qn, kn, v, dbp, e, lse, rate, name):
    def body(q_ref, k_ref, v_ref, db_ref, e_ref, lse_ref, dq_ref, dk_ref, dv_ref):
        lo = _lo_mask((NBACK, PAIR))
        band, kj = _band()
        scale = HEAD ** -0.5
        nt = (((1,), (1,)), ((), ()))
        tn = (((0,), (0,)), ((), ()))
        window = NBACK * rate
        nblk = T // window

        def one(it, carry):
            dk_carry, dv_carry = carry
            rho = it // nblk
            b = it % nblk
            start = b * window + rho
            rq = _rows(start, rate)
            rp = _rows(jnp.maximum(start - window, rho), rate)
            kmin = jnp.where(b > 0, 0, NBACK)
            q = q_ref[rq, :]
            db = db_ref[rq, :]
            ev = e_ref[rq, :]
            ls = lse_ref[rq, :]
            kcat = jnp.concatenate([k_ref[rp, :], k_ref[rq, :]], axis=0).astype(BF16)
            vcat = jnp.concatenate([v_ref[rp, :], v_ref[rq, :]], axis=0).astype(BF16)
            ok = band & (kj >= kmin)
            ok2 = jnp.concatenate([ok, ok], axis=0)
            q2 = jnp.concatenate([jnp.where(lo, q, 0.0), jnp.where(lo, 0.0, q)], axis=0).astype(BF16)
            db2 = jnp.concatenate([jnp.where(lo, db, 0.0), jnp.where(lo, 0.0, db)], axis=0).astype(BF16)
            ls2 = jnp.concatenate([ls[:, 0:1], ls[:, HEAD:HEAD + 1]], axis=0)
            ev2 = jnp.concatenate([ev[:, 0:1], ev[:, HEAD:HEAD + 1]], axis=0)
            s = lax.dot_general(q2, kcat, nt, preferred_element_type=F32) * scale
            s = jnp.where(ok2, s, NEG_INF)
            pt = jnp.exp(s - ls2)
            dp = lax.dot_general(db2, vcat, nt, preferred_element_type=F32)
            ds = (pt * (dp - ev2)).astype(BF16)
            dq2 = jnp.dot(ds, kcat, preferred_element_type=F32) * scale
            dkc = lax.dot_general(ds, q2, tn, preferred_element_type=F32) * scale
            dvc = lax.dot_general(pt.astype(BF16), db2, tn, preferred_element_type=F32)
            dq_ref[rq, :] = jnp.where(lo, dq2[0:NBACK], dq2[NBACK:])
            dk_ref[rp, :] = dk_carry + dkc[0:NBACK]
            dk_ref[rq, :] = dkc[NBACK:]
            dv_ref[rp, :] = dv_carry + dvc[0:NBACK]
            dv_ref[rq, :] = dvc[NBACK:]
            return dkc[NBACK:], dvc[NBACK:]

        def step(i, carry):
            for u in range(ATTN_BWD_UNROLL):
                carry = one(i * ATTN_BWD_UNROLL + u, carry)
            return carry

        zero = jnp.zeros((NBACK, PAIR), F32)
        lax.fori_loop(0, T // NBACK // ATTN_BWD_UNROLL, step, (zero, zero))

    pm = pl.BlockSpec((None, T, PAIR), lambda p: (p, 0, 0))
    return pl.pallas_call(
        body, name=name, grid=(NPAIR,), in_specs=[pm] * 6, out_specs=[pm] * 3,
        out_shape=[_sds((NPAIR, T, PAIR), F32)] * 3, compiler_params=_cparams(1, 56))(qn, kn, v, dbp, e, lse)


def _ab_in_bwd(pab, dcat, sgu_g, sgu_b, sgu_w, sgu_bias3, qg, kg, tabs, dqkv, tm=256):
    def body(p_ref, dcat_ref, g_ref, b_ref, w_ref, bias_ref, qg_ref, kg_ref, c_ref, s1_ref, s2_ref, *rest):
        dq_refs = rest[0:9]
        o_ref, dwm_ref, dbias_ref, dsg_ref, dsb_ref, dgain_ref = rest[9:]
        i = pl.program_id(0)

        @pl.when(i == 0)
        def _():
            dwm_ref[...] = jnp.zeros_like(dwm_ref)
            dbias_ref[...] = jnp.zeros_like(dbias_ref)
            dsg_ref[...] = jnp.zeros_like(dsg_ref)
            dsb_ref[...] = jnp.zeros_like(dsb_ref)
            dgain_ref[...] = jnp.zeros_like(dgain_ref)

        zu = p_ref[:, 0:512].astype(F32)
        zv = p_ref[:, 512:1024].astype(F32)
        u = _gelu(zu)
        v = _gelu(zv)
        mu = jnp.mean(v, axis=-1, keepdims=True)
        vc = v - mu
        rstd = lax.rsqrt(jnp.mean(vc * vc, axis=-1, keepdims=True) + EPS)
        xhat = vc * rstd
        vn = (xhat * g_ref[...] + b_ref[...]).astype(BF16)
        da = dcat_ref[...].astype(F32)
        tri = _tril_mask()
        du_parts = [[None] * 4 for _ in range(tm // 128)]
        dvn_parts = [[None] * 4 for _ in range(tm // 128)]
        for gi in range(4):
            wg = jnp.where(tri, w_ref[gi], 0.0).astype(BF16)
            bg = bias_ref[gi]
            for c in range(tm // 128):
                rs, cs = slice(c * 128, (c + 1) * 128), slice(gi * 128, (gi + 1) * 128)
                vblk = vn[rs, cs]
                mixed = jnp.dot(wg, vblk, preferred_element_type=F32) + bg
                dab = da[rs, cs]
                du_parts[c][gi] = dab * mixed
                dmixed = dab * u[rs, cs]
                dmb = dmixed.astype(BF16)
                dvn_parts[c][gi] = lax.dot_general(wg, dmb, (((0,), (0,)), ((), ())), preferred_element_type=F32)
                dwm = lax.dot_general(dmb, vblk, (((1,), (1,)), ((), ())), preferred_element_type=F32)
                dwm_ref[gi] += jnp.where(tri, dwm, 0.0)
                dbias_ref[gi] += dmixed
        du = jnp.concatenate([jnp.concatenate(r, axis=1) for r in du_parts], axis=0)
        dvn = jnp.concatenate([jnp.concatenate(r, axis=1) for r in dvn_parts], axis=0)
        dsg_ref[...] += jnp.sum(dvn * xhat, axis=0, keepdims=True)
        dsb_ref[...] += jnp.sum(dvn, axis=0, keepdims=True)
        dxh = dvn * g_ref[...]
        dv = rstd * (dxh - jnp.mean(dxh, axis=-1, keepdims=True)
                     - xhat * jnp.mean(dxh * xhat, axis=-1, keepdims=True))
        o_ref[:, 0:512] = (du * _gelu_grad(zu)).astype(BF16)
        o_ref[:, 512:1024] = (dv * _gelu_grad(zv)).astype(BF16)

        lo = _lo_mask((tm, PAIR))
        c, s1, s2 = c_ref[...], s1_ref[...], s2_ref[...]
        for g in range(3):
            dq_ref, dk_ref, dv_ref = dq_refs[3 * g:3 * g + 3]
            for p in range(NPAIR):
                for which, gains, src in ((0, qg_ref, dq_ref), (1, kg_ref, dk_ref)):
                    col = (2 + 3 * which + g) * 512 + p * PAIR
                    xr = p_ref[:, col:col + PAIR].astype(F32)
                    rinv = lax.rsqrt(_seg_mean(xr * xr, lo) + EPS)
                    xh = xr * rinv
                    dn = _rope_t(src[p], c, s1, s2)
                    row = 2 * g + which
                    dgain_ref[row:row + 1, :] += jnp.sum(dn * xh, axis=0, keepdims=True)
                    dxh2 = dn * gains[g:g + 1, :]
                    dx = rinv * (dxh2 - xh * _seg_mean(dxh2 * xh, lo))
                    o_ref[:, col:col + PAIR] = dx.astype(BF16)
                col = (8 + g) * 512 + p * PAIR
                o_ref[:, col:col + PAIR] = dv_ref[p].astype(BF16)

    pm = pl.BlockSpec((NPAIR, tm, PAIR), lambda i: (0, i, 0))
    tab = pl.BlockSpec((tm, PAIR), lambda i: (i, 0))
    gain = pl.BlockSpec((3, PAIR), lambda i: (0, 0))
    vec = pl.BlockSpec((1, 512), lambda i: (0, 0))
    full = pl.BlockSpec((tm, AB_IN), lambda i: (i, 0))
    w4 = pl.BlockSpec((4, 128, 128), lambda i: (0, 0, 0))
    return pl.pallas_call(
        body, name="ab_in_bwd", grid=(T // tm,),
        in_specs=[full, pl.BlockSpec((tm, 512), lambda i: (i, 0)), vec, vec, w4,
                  pl.BlockSpec((4, 128, 1), lambda i: (0, 0, 0)), gain, gain, tab, tab, tab] + [pm] * 9,
        out_specs=[full, w4, w4, vec, vec, pl.BlockSpec((8, PAIR), lambda i: (0, 0))],
        out_shape=[_sds((T, AB_IN), BF16), _sds((4, 128, 128), F32), _sds((4, 128, 128), F32),
                   _sds((1, 512), F32), _sds((1, 512), F32), _sds((8, PAIR), F32)],
        compiler_params=_cparams(1))(pab, dcat, sgu_g, sgu_b, sgu_w, sgu_bias3, qg, kg, *tabs, *dqkv)


def _ln_stats(x):
    mu = jnp.mean(x, axis=-1, keepdims=True)
    xc = x - mu
    rstd = lax.rsqrt(jnp.mean(xc * xc, axis=-1, keepdims=True) + EPS)
    return xc * rstd, rstd


CONV_RC = 64


def _shifted_copies(src, dst, tm):
    dst[0] = src[...]
    for b in range(1, 8):
        dst[b, 0:tm + HALO - 8, :] = src[pl.ds(b, tm + HALO - 8), :]


def _cd_fwd(pcd, cw, cb, lg, lb, dw, tm=512):
    per = tm // HALO

    def body(p_ref, h_ref, cw_ref, cb_ref, lg_ref, lb_ref, dw_ref, cat_ref, c0_ref, c1_ref, dd_ref, y_ref,
             buf, buf2, sb):
        i = pl.program_id(0)
        live = jnp.where(i > 0, 1.0, 0.0)
        a = p_ref[:, 0:512].astype(F32)
        gt = p_ref[:, 512:1024].astype(F32)
        gb = p_ref[:, 1024:1536].astype(F32)
        gc = p_ref[:, 1536:2048].astype(F32)
        hv = p_ref[:, 2048:2560].astype(F32)
        c0 = a * _sigmoid(gt)
        dd = gc * hv
        buf[0:HALO, :] = h_ref[:, 0:512].astype(F32) * _sigmoid(h_ref[:, 512:1024].astype(F32)) * live
        buf[HALO:, :] = c0
        buf2[0:HALO, :] = h_ref[:, 1536:2048].astype(F32) * h_ref[:, 2048:2560].astype(F32) * live
        buf2[HALO:, :] = dd
        c0_ref[...] = c0.astype(BF16)
        dd_ref[...] = dd.astype(BF16)
        _shifted_copies(buf, sb, tm)

        def conv_rows(r, carry):
            base = pl.multiple_of(r * CONV_RC, CONV_RC)
            for c in range(4):
                lanes = slice(c * 128, (c + 1) * 128)
                acc = jnp.broadcast_to(cb_ref[:, lanes], (CONV_RC, 128))
                for j in range(CONV_C_TAPS):
                    a8, b8 = divmod(HALO - (CONV_C_TAPS - 1) + j, 8)
                    acc = acc + cw_ref[j:j + 1, lanes] * sb[b8, pl.ds(base + 8 * a8, CONV_RC), lanes]
                c1_ref[pl.ds(base, CONV_RC), lanes] = acc
            return carry

        lax.fori_loop(0, tm // CONV_RC, conv_rows, 0)
        xhat, _ = _ln_stats(c1_ref[...])
        c2 = xhat * lg_ref[...] + lb_ref[...]
        y = jnp.zeros((tm, 512), F32)
        for j in range(CONV_D_TAPS):
            y = y + dw_ref[j:j + 1, :] * buf2[pl.ds(HALO - (CONV_D_TAPS - 1) + j, tm), :]
        cat_ref[:, 0:512] = (c2 * _sigmoid(c2)).astype(BF16)
        cat_ref[:, 512:1024] = (gb * y).astype(BF16)
        y_ref[...] = y.astype(BF16)

    half = pl.BlockSpec((tm, 512), lambda i: (i, 0))
    vec = pl.BlockSpec((1, 512), lambda i: (0, 0))
    return pl.pallas_call(
        body, name="cd_fwd", grid=(T // tm,),
        in_specs=[pl.BlockSpec((tm, CD_IN), lambda i: (i, 0)),
                  pl.BlockSpec((HALO, CD_IN), lambda i: (jnp.maximum(i * per - 1, 0), 0)),
                  pl.BlockSpec((32, 512), lambda i: (0, 0)), vec, vec, vec, pl.BlockSpec((8, 512), lambda i: (0, 0))],
        out_specs=[pl.BlockSpec((tm, D), lambda i: (i, 0)), half, half, half, half],
        out_shape=[_sds((T, D), BF16), _sds((T, 512), BF16), _sds((T, 512), F32), _sds((T, 512), BF16),
                   _sds((T, 512), BF16)],
        scratch_shapes=[pltpu.VMEM((HALO + tm, 512), F32), pltpu.VMEM((HALO + tm, 512), F32),
                        pltpu.VMEM((8, HALO + tm, 512), F32)],
        compiler_params=_cparams(1))(pcd, pcd, cw, cb, lg, lb, dw)


def _cd_bwd_pw(dcat, c1, pcd, y, lg, lb, tm=512):
    def body(dcat_ref, c1_ref, gb_ref, y_ref, lg_ref, lb_ref, dc1_ref, dy3_ref, dgb_ref, dlg_ref, dlb_ref, dcb_ref):
        i = pl.program_id(0)

        @pl.when(i == 0)
        def _():
            dlg_ref[...] = jnp.zeros_like(dlg_ref)
            dlb_ref[...] = jnp.zeros_like(dlb_ref)
            dcb_ref[...] = jnp.zeros_like(dcb_ref)

        dc = dcat_ref[:, 0:512].astype(F32)
        ddo = dcat_ref[:, 512:1024].astype(F32)
        xhat, rstd = _ln_stats(c1_ref[...])
        c2 = xhat * lg_ref[...] + lb_ref[...]
        sg = _sigmoid(c2)
        dc2 = dc * sg * (1.0 + c2 * (1.0 - sg))
        dlg_ref[...] += jnp.sum(dc2 * xhat, axis=0, keepdims=True)
        dlb_ref[...] += jnp.sum(dc2, axis=0, keepdims=True)
        dxh = dc2 * lg_ref[...]
        dc1 = rstd * (dxh - jnp.mean(dxh, axis=-1, keepdims=True)
                      - xhat * jnp.mean(dxh * xhat, axis=-1, keepdims=True))
        dcb_ref[...] += jnp.sum(dc1, axis=0, keepdims=True)
        dc1_ref[...] = dc1
        dgb_ref[...] = (ddo * y_ref[...].astype(F32)).astype(BF16)
        dy3_ref[...] = ddo * gb_ref[...].astype(F32)

    half = pl.BlockSpec((tm, 512), lambda i: (i, 0))
    vec = pl.BlockSpec((1, 512), lambda i: (0, 0))
    return pl.pallas_call(
        body, name="cd_bwd_pw", grid=(T // tm,),
        in_specs=[pl.BlockSpec((tm, D), lambda i: (i, 0)), half, pl.BlockSpec((tm, 512), lambda i: (i, 2)), half,
                  vec, vec],
        out_specs=[half, half, half, vec, vec, vec],
        out_shape=[_sds((T, 512), F32), _sds((T, 512), F32), _sds((T, 512), BF16),
                   _sds((1, 512), F32), _sds((1, 512), F32), _sds((1, 512), F32)],
        compiler_params=_cparams(1))(dcat, c1, pcd, y, lg, lb)


def _cd_bwd_conv(pcd, dc1, dy3, c0, dd, dgb, cw, dw, tm=256):
    per = tm // HALO
    nblk = T // tm
    last32 = T // HALO - 1

    def body(p_ref, dc1_ref, dc1n_ref, dy3_ref, dy3n_ref, c0_ref, c0p_ref, dd_ref, ddp_ref, dgb_ref, cw_ref, dw_ref,
             o_ref, dcw_ref, ddw_ref, dbuf, cbuf, d3buf, ddbuf, sd, sc, dc0_buf):
        i = pl.program_id(0)
        has_prev = jnp.where(i > 0, 1.0, 0.0)
        has_next = jnp.where(i < nblk - 1, 1.0, 0.0)

        @pl.when(i == 0)
        def _():
            dcw_ref[...] = jnp.zeros_like(dcw_ref)
            ddw_ref[...] = jnp.zeros_like(ddw_ref)

        dc1 = dc1_ref[...]
        dy3 = dy3_ref[...]
        dbuf[0:tm, :] = dc1
        dbuf[tm:, :] = dc1n_ref[...] * has_next
        d3buf[0:tm, :] = dy3
        d3buf[tm:, :] = dy3n_ref[...] * has_next
        cbuf[0:HALO, :] = c0p_ref[...].astype(F32) * has_prev
        cbuf[HALO:, :] = c0_ref[...].astype(F32)
        ddbuf[0:HALO, :] = ddp_ref[...].astype(F32) * has_prev
        ddbuf[HALO:, :] = dd_ref[...].astype(F32)

        _shifted_copies(dbuf, sd, tm)
        _shifted_copies(cbuf, sc, tm)
        n_tiles = tm // CONV_RC

        def dc0_rows(r, carry):
            base = pl.multiple_of(r * CONV_RC, CONV_RC)
            for c in range(4):
                lanes = slice(c * 128, (c + 1) * 128)
                acc = jnp.zeros((CONV_RC, 128), F32)
                for j in range(CONV_C_TAPS):
                    a8, b8 = divmod(CONV_C_TAPS - 1 - j, 8)
                    acc = acc + cw_ref[j:j + 1, lanes] * sd[b8, pl.ds(base + 8 * a8, CONV_RC), lanes]
                dc0_buf[pl.ds(base, CONV_RC), lanes] = acc
            return carry

        lax.fori_loop(0, n_tiles, dc0_rows, 0)

        for c in range(4):
            lanes = slice(c * 128, (c + 1) * 128)
            for j0 in range(0, CONV_C_TAPS, 8):
                taps = list(range(j0, min(j0 + 8, CONV_C_TAPS)))

                def dw_rows(r, accs, lanes=lanes, taps=taps):
                    base = pl.multiple_of(r * CONV_RC, CONV_RC)
                    d = dbuf[pl.ds(base, CONV_RC), lanes]
                    out = []
                    for acc, j in zip(accs, taps):
                        a8, b8 = divmod(HALO - (CONV_C_TAPS - 1) + j, 8)
                        prod = d * sc[b8, pl.ds(base + 8 * a8, CONV_RC), lanes]
                        out.append(acc + jnp.sum(prod.reshape(CONV_RC // 8, 8, 128), axis=0))
                    return tuple(out)

                accs = lax.fori_loop(0, n_tiles, dw_rows, tuple(jnp.zeros((8, 128), F32) for _ in taps))
                for acc, j in zip(accs, taps):
                    dcw_ref[j:j + 1, lanes] += jnp.sum(acc, axis=0, keepdims=True)

        dc0 = dc0_buf[...]
        ddd = jnp.zeros((tm, 512), F32)
        for j in range(CONV_D_TAPS):
            ddd = ddd + dw_ref[j:j + 1, :] * d3buf[pl.ds(CONV_D_TAPS - 1 - j, tm), :]
            ddw_ref[j:j + 1, :] += jnp.sum(dy3 * ddbuf[pl.ds(HALO - (CONV_D_TAPS - 1) + j, tm), :], axis=0, keepdims=True)

        a = p_ref[:, 0:512].astype(F32)
        gt = p_ref[:, 512:1024].astype(F32)
        gc = p_ref[:, 1536:2048].astype(F32)
        hv = p_ref[:, 2048:2560].astype(F32)
        sg = _sigmoid(gt)
        o_ref[:, 0:512] = (dc0 * sg).astype(BF16)
        o_ref[:, 512:1024] = (dc0 * a * sg * (1.0 - sg)).astype(BF16)
        o_ref[:, 1024:1536] = dgb_ref[...]
        o_ref[:, 1536:2048] = (ddd * hv).astype(BF16)
        o_ref[:, 2048:2560] = (ddd * gc).astype(BF16)

    half = pl.BlockSpec((tm, 512), lambda i: (i, 0))
    nxt = pl.BlockSpec((HALO, 512), lambda i: (jnp.minimum((i + 1) * per, last32), 0))
    prv = pl.BlockSpec((HALO, 512), lambda i: (jnp.maximum(i * per - 1, 0), 0))
    full = pl.BlockSpec((tm, CD_IN), lambda i: (i, 0))
    return pl.pallas_call(
        body, name="cd_bwd_conv", grid=(nblk,),
        in_specs=[full, half, nxt, half, nxt, half, prv, half, prv, half,
                  pl.BlockSpec((32, 512), lambda i: (0, 0)), pl.BlockSpec((8, 512), lambda i: (0, 0))],
        out_specs=[full, pl.BlockSpec((32, 512), lambda i: (0, 0)), pl.BlockSpec((8, 512), lambda i: (0, 0))],
        out_shape=[_sds((T, CD_IN), BF16), _sds((32, 512), F32), _sds((8, 512), F32)],
        scratch_shapes=[pltpu.VMEM((tm + HALO, 512), F32), pltpu.VMEM((HALO + tm, 512), F32),
                        pltpu.VMEM((tm + HALO, 512), F32), pltpu.VMEM((HALO + tm, 512), F32),
                        pltpu.VMEM((8, tm + HALO, 512), F32), pltpu.VMEM((8, HALO + tm, 512), F32),
                        pltpu.VMEM((tm, 512), F32)],
        compiler_params=_cparams(1))(pcd, dc1, dc1, dy3, dy3, c0, c0, dd, dd, dgb, cw, dw)


def _local_step(x, tgt, W, fetch=None, on_grad=None):
    W = dict(W)
    if fetch is None:
        fetch = lambda stage, after: {}
    if on_grad is None:
        on_grad = lambda key, arr: None
    tabs = _rope_tables()
    qg = jnp.tile(W["q_norm_g"], (1, 2))
    kg = jnp.tile(W["k_norm_g"], (1, 2))
    bias3 = W["sgu_bias"].reshape(4, 128, 1)
    G = {}

    h0 = _rms_fwd(x, W["ab_norm_g"], "rms_fwd_ab")
    pab = _mm_nt(h0, W["wt_ab_in"], "mm_ab_in", dep=W.get("dep0"))
    cat_ab = _mix_a_fwd(pab, W["sgu_norm_g"], W["sgu_norm_b"], W["sgu_w"], bias3)
    qkv = _prep_fwd(pab, qg, kg, tabs)
    outs, lses = [], []
    for g, rate in enumerate(DIL_RATES):
        o, l = _attn_fwd(qkv[3 * g], qkv[3 * g + 1], qkv[3 * g + 2], rate, f"attn_fwd_{g}", dep=W.get(f"dep_attn{g}"))
        outs.append(o)
        lses.append(l)
        W.update(fetch(f"attn{g}", o))
    cat_ab, lse = _merge_fwd(cat_ab, outs, lses)
    W.update(fetch("ab_out", lse))
    x1, h1 = _mm_nn(cat_ab, W["w_ab_out"], "mm_ab_out", mode="rms", resid=x, gain=W["ffn_norm_g"][0:1])
    pf0, act0 = _ffn_in(h1, W["wt_ffn_in0"], "ffn_in0")
    W.update(fetch("ffn_down0", act0))
    x2, h2 = _mm_nn(act0, W["w_ffn_down0"], "mm_ffn_down0", mode="rms", resid=x1, gain=W["cd_norm_g"],
                    dep=W.get("dep_down0"))
    W.update(fetch("cd_in", h2))
    pcd = _mm_nt(h2, W["wt_cd_in"], "mm_cd_in")
    cat_cd, c0, c1, dd, yv = _cd_fwd(pcd, W["conv_c_w32"], W["conv_c_b"], W["c_ln_g"], W["c_ln_b"], W["conv_d_w8"])
    x3, h3 = _mm_nn(cat_cd, W["w_cd_out"], "mm_cd_out", mode="rms", resid=x2, gain=W["ffn_norm_g"][1:2])
    pf1, act1 = _ffn_in(h3, W["wt_ffn_in1"], "ffn_in1")
    dy, dyb, loss_cols = _mm_nn(act1, W["w_ffn_down1"], "mm_ffn_down1", mode="loss", resid=x3, tgt=tgt)

    def ffn_bwd(xin, h, pf, act, dres, dresb, layer):
        G[f"w_ffn_down{layer}"] = _mm_tn(act, dresb, f"mm_g_ffn_down{layer}")
        dep = on_grad(f"w_ffn_down{layer}", G[f"w_ffn_down{layer}"])
        dpf = _ffn_dact(dresb, W[f"w_ffn_down{layer}"], pf, f"ffn_dact{layer}", dep=dep)
        G[f"wt_ffn_in{layer}"] = _mm_tn(dpf, h, f"mm_g_ffn_in{layer}")
        dep = on_grad(f"wt_ffn_in{layer}", G[f"wt_ffn_in{layer}"])
        dx, dxb, G[f"ffn_norm_g{layer}"] = _mm_dh_rms_bwd(
            dpf, W[f"wt_ffn_in{layer}"], xin, W["ffn_norm_g"][layer:layer + 1], dres, f"mm_d_h_ffn{layer}", dep=dep)
        return dx, dxb

    dx3, dx3b = ffn_bwd(x3, h3, pf1, act1, dy, dyb, 1)

    G["w_cd_out"] = _mm_tn(cat_cd, dx3b, "mm_g_cd_out")
    dep = on_grad("w_cd_out", G["w_cd_out"])
    dcat_cd = _mm_nt(dx3b, W["w_cd_out"], "mm_d_cat_cd", dep=dep)
    dc1, dy3, dgb, G["c_ln_g"], G["c_ln_b"], G["conv_c_b"] = _cd_bwd_pw(dcat_cd, c1, pcd, yv, W["c_ln_g"], W["c_ln_b"])
    dpcd, G["conv_c_w32"], G["conv_d_w8"] = _cd_bwd_conv(pcd, dc1, dy3, c0, dd, dgb, W["conv_c_w32"], W["conv_d_w8"])
    G["wt_cd_in"] = _mm_tn(dpcd, h2, "mm_g_cd_in")
    dep = on_grad("wt_cd_in", G["wt_cd_in"])
    dx2, dx2b, G["cd_norm_g"] = _mm_dh_rms_bwd(dpcd, W["wt_cd_in"], x2, W["cd_norm_g"], dx3, "mm_d_h_cd", dep=dep)

    dx1, dx1b = ffn_bwd(x1, h1, pf0, act0, dx2, dx2b, 0)

    G["w_ab_out"] = _mm_tn(cat_ab, dx1b, "mm_g_ab_out")
    dep = on_grad("w_ab_out", G["w_ab_out"])
    dcat_ab = _mm_nt(dx1b, W["w_ab_out"], "mm_d_cat_ab", dep=dep)
    dbp, e = _b_pre_bwd(dcat_ab, cat_ab)
    dqkv = []
    for g, rate in enumerate(DIL_RATES):
        dqkv += _attn_bwd(qkv[3 * g], qkv[3 * g + 1], qkv[3 * g + 2], dbp, e, lse, rate, f"attn_bwd_{g}")
    dpab, G["sgu_w"], dbias_part, G["sgu_norm_g"], G["sgu_norm_b"], dgain = _ab_in_bwd(
        pab, dcat_ab, W["sgu_norm_g"], W["sgu_norm_b"], W["sgu_w"], bias3, qg, kg, tabs, dqkv)
    G["sgu_bias"] = jnp.sum(dbias_part, axis=-1)
    dgain = dgain[0:6, 0:HEAD] + dgain[0:6, HEAD:PAIR]
    G["q_norm_g"] = dgain[0::2]
    G["k_norm_g"] = dgain[1::2]
    G["wt_ab_in"] = _mm_tn(dpab, h0, "mm_g_ab_in")
    dep = on_grad("wt_ab_in", G["wt_ab_in"])
    grad_x, _, G["ab_norm_g"] = _mm_dh_rms_bwd(dpab, W["wt_ab_in"], x, W["ab_norm_g"], dx1, "mm_d_h_ab", dep=dep)
    return loss_cols, grad_x, G


def _my_place():
    return lax.axis_index("x"), lax.axis_index("y"), lax.axis_index("c")


def _dev_index(px, py, pc):
    return 4 * px + 2 * py + pc


def _flip(place, k):
    x, y, c = place
    return (1 - x if k & 4 else x, 1 - y if k & 2 else y, 1 - c if k & 1 else c)


def _landing(shape, dtype, own):
    buf = lax.empty(shape, dtype)
    for lead, part in own:
        buf = lax.dynamic_update_slice(buf, part.reshape((1,) * len(lead) + part.shape),
                                       tuple(lead) + (0,) * part.ndim)
    return buf


def _gather_first(ab_in_t, small, me_index):
    lands = [_landing((NDEV,) + ab_in_t.shape, BF16, [((me_index,), ab_in_t)]),
             _landing((NDEV,) + small.shape, F32, [((me_index,), small)])]
    n_items = 2

    def body(ab_in_r, small_r, l_ab_in, l_small, o_ab_in, o_small, send_sems, recv_sems):
        del l_ab_in, l_small
        x, y, c = _my_place()
        me = (x, y, c)
        sib = (x, y, 1 - c)
        chips = [(1 - x, y), (x, 1 - y), (1 - x, 1 - y)]
        items = [(ab_in_r, lambda d: o_ab_in.at[d]), (small_r, lambda d: o_small.at[d])]

        def rcopy(it, k, src, dst, to):
            return pltpu.make_async_remote_copy(src_ref=src, dst_ref=dst, send_sem=send_sems.at[it, k],
                                                recv_sem=recv_sems.at[it, k], device_id=to, device_id_type=MESH)

        started = []
        for it, (src, dst) in enumerate(items):
            mine = dst(_dev_index(*me))
            first = [rcopy(it, 0, src, mine, sib)]
            first += [rcopy(it, 1 + j, src, mine, (*chip, c)) for j, chip in enumerate(chips)]
            for cp in first:
                cp.start()
            started += first
        for it, (src, dst) in enumerate(items):
            for j, chip in enumerate(chips):
                blk = dst(_dev_index(*chip, c))
                rcopy(it, 1 + j, blk, blk, me).wait_recv()
                fwd = rcopy(it, 4 + j, blk, blk, sib)
                fwd.start()
                started.append(fwd)
        for it, (src, dst) in enumerate(items):
            blk = dst(_dev_index(x, y, 1 - c))
            rcopy(it, 0, blk, blk, me).wait_recv()
            for j, chip in enumerate(chips):
                blk = dst(_dev_index(*chip, 1 - c))
                rcopy(it, 4 + j, blk, blk, me).wait_recv()
        for cp in started:
            cp.wait_send()

    return pl.pallas_call(
        body, name="gather_first", in_specs=[HBM_SPEC] * 4, out_specs=[HBM_SPEC] * 2,
        out_shape=[_sds(a.shape, a.dtype) for a in lands], input_output_aliases={2: 0, 3: 1},
        scratch_shapes=[pltpu.SemaphoreType.DMA((n_items, 7)), pltpu.SemaphoreType.DMA((n_items, 7))],
    )(ab_in_t, small, *lands)


HBM_ONLY = pl.BlockSpec(memory_space=pltpu.HBM)
SEM_SPEC = pl.BlockSpec(memory_space=pltpu.SEMAPHORE)
IN_FLIGHT = pltpu.CompilerParams(has_side_effects=pltpu.SideEffectType.DATAFLOW_SIDE_EFFECTING)


def _in_hbm(a):
    return pltpu.with_memory_space_constraint(a, pltpu.HBM)


def _exchange_start(name, srcs, lands, items, dep=None):
    ns, nl, ni = len(srcs), len(lands), len(items)

    def body(*refs):
        S, L = refs[0:ns], refs[ns:ns + nl]
        first_out = ns + nl + (0 if dep is None else 1)
        send_sems, recv_sems, token = refs[first_out], refs[first_out + 1], refs[-1]
        me = _my_place()
        mi = _dev_index(*me)
        for i, (src, dst) in enumerate(items):
            for k in range(1, NDEV):
                peer = _flip(me, k)
                pltpu.make_async_remote_copy(
                    src_ref=src(S, _dev_index(*peer)), dst_ref=dst(L, mi), send_sem=send_sems.at[7 * i + k - 1],
                    recv_sem=recv_sems.at[7 * i + k - 1], device_id=peer, device_id_type=MESH).start()
        token[...] = jnp.zeros_like(token)

    thru = [pltpu.HBM(a.shape, a.dtype) for a in list(srcs) + list(lands)]
    args = [_in_hbm(a) for a in srcs] + [_in_hbm(a) for a in lands]
    in_specs = [HBM_ONLY] * (ns + nl)
    if dep is not None:
        args.append(dep)
        in_specs.append(HBM_SPEC)
    outs = pl.pallas_call(
        body, name=name, in_specs=in_specs,
        out_shape=(pltpu.SemaphoreType.DMA((7 * ni,)), pltpu.SemaphoreType.DMA((7 * ni,)), *thru, _sds((8, 128), F32)),
        out_specs=(SEM_SPEC, SEM_SPEC, *[HBM_ONLY] * (ns + nl), pl.BlockSpec(memory_space=pltpu.VMEM)),
        input_output_aliases={j: 2 + j for j in range(ns + nl)}, compiler_params=IN_FLIGHT)(*args)
    return dict(send=outs[0], recv=outs[1], srcs=list(outs[2:2 + ns]), lands=list(outs[2 + ns:2 + ns + nl]),
                token=outs[-1], items=items)


def _exchange_wait(name, states, after):
    after = list(after) if isinstance(after, (list, tuple)) else [after]
    counts = [(len(st["srcs"]), len(st["lands"]), len(st["items"])) for st in states]
    n_arrays = sum(c[0] + c[1] for c in counts)

    def body(*refs):
        me = _my_place()
        mi = _dev_index(*me)
        pos = 0
        sem_pos = n_arrays
        for st, (ns, nl, ni) in zip(states, counts):
            S, L = refs[pos:pos + ns], refs[pos + ns:pos + ns + nl]
            send_sems, recv_sems = refs[sem_pos], refs[sem_pos + 1]
            pos += ns + nl
            sem_pos += 2
            for i, (src, dst) in enumerate(st["items"]):
                for k in range(1, NDEV):
                    cp = pltpu.make_async_remote_copy(
                        src_ref=src(S, mi), dst_ref=dst(L, mi), send_sem=send_sems.at[7 * i + k - 1],
                        recv_sem=recv_sems.at[7 * i + k - 1], device_id=me, device_id_type=MESH)
                    cp.wait_send()
                    cp.wait_recv()

    arrays, sems = [], []
    for st in states:
        arrays += st["srcs"] + st["lands"]
        sems += [st["send"], st["recv"]]
    outs = pl.pallas_call(
        body, name=name, in_specs=[HBM_ONLY] * n_arrays + [SEM_SPEC] * len(sems) + [HBM_SPEC] * len(after),
        out_shape=tuple(pltpu.HBM(a.shape, a.dtype) for a in arrays), out_specs=tuple([HBM_ONLY] * n_arrays),
        input_output_aliases={j: j for j in range(n_arrays)}, compiler_params=IN_FLIGHT)(*arrays, *sems, *after)
    lands, pos = [], 0
    for ns, nl, _ in counts:
        lands.append(list(outs[pos + ns:pos + ns + nl]))
        pos += ns + nl
    return lands


def _place_and_neighbours():
    x, y, c = _my_place()
    return (x, y, c), (x, y, 1 - c), [(1 - x, y), (x, 1 - y), (1 - x, 1 - y)]


def _gather_start(name, srcs, lands, items, dep=None):
    ns, nl, ni = len(srcs), len(lands), len(items)

    def body(*refs):
        S, L = refs[0:ns], refs[ns:ns + nl]
        first_out = ns + nl + (0 if dep is None else 1)
        send_sems, recv_sems, token = refs[first_out], refs[first_out + 1], refs[-1]
        me, sib, chips = _place_and_neighbours()
        mi = _dev_index(*me)
        for i, (src, dst) in enumerate(items):
            for k, to in enumerate([sib] + [(*chip, me[2]) for chip in chips]):
                pltpu.make_async_remote_copy(
                    src_ref=src(S), dst_ref=dst(L, mi), send_sem=send_sems.at[4 * i + k],
                    recv_sem=recv_sems.at[4 * i + k], device_id=to, device_id_type=MESH).start()
        token[...] = jnp.zeros_like(token)

    thru = [pltpu.HBM(a.shape, a.dtype) for a in list(srcs) + list(lands)]
    args = [_in_hbm(a) for a in srcs] + [_in_hbm(a) for a in lands]
    in_specs = [HBM_ONLY] * (ns + nl)
    if dep is not None:
        args.append(dep)
        in_specs.append(HBM_SPEC)
    outs = pl.pallas_call(
        body, name=name, in_specs=in_specs,
        out_shape=(pltpu.SemaphoreType.DMA((4 * ni,)), pltpu.SemaphoreType.DMA((4 * ni,)), *thru, _sds((8, 128), F32)),
        out_specs=(SEM_SPEC, SEM_SPEC, *[HBM_ONLY] * (ns + nl), pl.BlockSpec(memory_space=pltpu.VMEM)),
        input_output_aliases={j: 2 + j for j in range(ns + nl)}, compiler_params=IN_FLIGHT)(*args)
    return dict(send=outs[0], recv=outs[1], srcs=list(outs[2:2 + ns]), lands=list(outs[2 + ns:2 + ns + nl]),
                token=outs[-1], items=items)


def _gather_forward(name, st, after):
    nl, ni = len(st["lands"]), len(st["items"])

    def body(*refs):
        L, recv_sems = refs[0:nl], refs[nl]
        fwd_send, fwd_recv, token = refs[2 * nl + 2], refs[2 * nl + 3], refs[-1]
        me, sib, chips = _place_and_neighbours()
        for i, (_, dst) in enumerate(st["items"]):
            for j, chip in enumerate(chips):
                blk = dst(L, _dev_index(*chip, me[2]))
                pltpu.make_async_remote_copy(
                    src_ref=blk, dst_ref=blk, send_sem=fwd_send.at[3 * i + j], recv_sem=recv_sems.at[4 * i + 1 + j],
                    device_id=me, device_id_type=MESH).wait_recv()
                pltpu.make_async_remote_copy(
                    src_ref=blk, dst_ref=blk, send_sem=fwd_send.at[3 * i + j], recv_sem=fwd_recv.at[3 * i + j],
                    device_id=sib, device_id_type=MESH).start()
        token[...] = jnp.zeros_like(token)

    outs = pl.pallas_call(
        body, name=name, in_specs=[HBM_ONLY] * nl + [SEM_SPEC, HBM_SPEC],
        out_shape=(*[pltpu.HBM(a.shape, a.dtype) for a in st["lands"]], pltpu.SemaphoreType.DMA((3 * ni,)),
                   pltpu.SemaphoreType.DMA((3 * ni,)), _sds((8, 128), F32)),
        out_specs=(*[HBM_ONLY] * nl, SEM_SPEC, SEM_SPEC, pl.BlockSpec(memory_space=pltpu.VMEM)),
        input_output_aliases={j: j for j in range(nl)}, compiler_params=IN_FLIGHT)(*st["lands"], st["recv"], after)
    return dict(st, lands=list(outs[0:nl]), fwd_send=outs[nl], fwd_recv=outs[nl + 1], token=outs[-1])


def _gather_wait(name, st, after):
    ns, nl, ni = len(st["srcs"]), len(st["lands"]), len(st["items"])

    def body(*refs):
        S, L = refs[0:ns], refs[ns:ns + nl]
        send_sems, recv_sems, fwd_send, fwd_recv = refs[ns + nl:ns + nl + 4]
        me, sib, chips = _place_and_neighbours()
        mi = _dev_index(*me)
        for i, (src, dst) in enumerate(st["items"]):
            mine = dst(L, mi)
            for k in range(4):
                pltpu.make_async_remote_copy(
                    src_ref=src(S), dst_ref=mine, send_sem=send_sems.at[4 * i + k], recv_sem=recv_sems.at[4 * i + k],
                    device_id=me, device_id_type=MESH).wait_send()
            pltpu.make_async_remote_copy(
                src_ref=src(S), dst_ref=mine, send_sem=send_sems.at[4 * i], recv_sem=recv_sems.at[4 * i],
                device_id=me, device_id_type=MESH).wait_recv()
            for j in range(3):
                cp = pltpu.make_async_remote_copy(
                    src_ref=mine, dst_ref=mine, send_sem=fwd_send.at[3 * i + j], recv_sem=fwd_recv.at[3 * i + j],
                    device_id=me, device_id_type=MESH)
                cp.wait_send()
                cp.wait_recv()

    arrays = st["srcs"] + st["lands"]
    outs = pl.pallas_call(
        body, name=name, in_specs=[HBM_ONLY] * (ns + nl) + [SEM_SPEC] * 4 + [HBM_SPEC],
        out_shape=tuple(pltpu.HBM(a.shape, a.dtype) for a in arrays), out_specs=tuple([HBM_ONLY] * (ns + nl)),
        input_output_aliases={j: j for j in range(ns + nl)},
        compiler_params=IN_FLIGHT)(*arrays, st["send"], st["recv"], st["fwd_send"], st["fwd_recv"], after)
    return list(outs[ns:ns + nl])


def _sum_slots(land):
    def body(l_ref, o_ref):
        acc = l_ref[0]
        for d in range(1, NDEV):
            acc = acc + l_ref[d]
        o_ref[...] = acc

    vm = pl.BlockSpec(memory_space=pltpu.VMEM)
    return pl.pallas_call(body, name="sum_small", out_shape=_sds(land.shape[1:], F32), in_specs=[vm], out_specs=vm)(land)


def _adam_math(w, g, m, v):
    m2 = ADAM_B1 * m + (1.0 - ADAM_B1) * g
    v2 = ADAM_B2 * v + (1.0 - ADAM_B2) * (g * g)
    delta = -ADAM_LR * ((m2 * ADAM_C1) / (jnp.sqrt(v2 * ADAM_C2) + ADAM_EPS) + ADAM_WD * w)
    return delta, m2, v2


def _adam_layer(land, sel, w, m, v, layer, name, prev=None, tc=512):
    R = land.shape[2]

    def body(l_ref, w_ref, m_ref, v_ref, *rest):
        g_out, d_out, m_out, v_out = rest[-4:]
        g = l_ref[0].astype(F32)
        for d in range(1, NDEV):
            g = g + l_ref[d].astype(F32)
        delta, m2, v2 = _adam_math(w_ref[...], g, m_ref[...], v_ref[...])
        g_out[...] = g
        d_out[...] = delta
        m_out[...] = m2
        v_out[...] = v2

    wspec = pl.BlockSpec((None, R, tc), lambda i: (layer, 0, i))
    in_specs = [pl.BlockSpec((None, NDEV, R, tc), lambda i: (sel, 0, 0, i)), wspec, wspec, wspec]
    args = [land, w, m, v]
    aliases = {}
    if prev is not None:
        in_specs += [HBM_SPEC] * 4
        args += list(prev)
        aliases = {4 + j: j for j in range(4)}
    return pl.pallas_call(
        body, name=name, grid=(D // tc,), in_specs=in_specs, out_specs=[wspec] * 4,
        out_shape=[_sds(w.shape, F32)] * 4, input_output_aliases=aliases, compiler_params=_cparams(1))(*args)


def _adam_stacked(lands, sel, w, m, v, name):
    res = None
    for layer, land in enumerate(lands):
        res = _adam_layer(land, sel, w, m, v, layer, f"{name}{layer}", prev=res)
    return res


def _adam_small(ws, gs, ms, vs):
    n = len(ws)

    def body(*refs):
        w_r, g_r, m_r, v_r = refs[0:n], refs[n:2 * n], refs[2 * n:3 * n], refs[3 * n:4 * n]
        d_o, m_o, v_o = refs[4 * n:5 * n], refs[5 * n:6 * n], refs[6 * n:7 * n]
        for i in range(n):
            delta, m2, v2 = _adam_math(w_r[i][...], g_r[i][...], m_r[i][...], v_r[i][...])
            d_o[i][...] = delta
            m_o[i][...] = m2
            v_o[i][...] = v2

    vm = pl.BlockSpec(memory_space=pltpu.VMEM)
    shapes = [_sds(w.shape, F32) for w in ws]
    outs = pl.pallas_call(body, name="adam_small", in_specs=[vm] * (4 * n), out_specs=[vm] * (3 * n),
                          out_shape=shapes * 3)(*ws, *gs, *ms, *vs)
    return outs[0:n], outs[n:2 * n], outs[2 * n:3 * n]


WEIGHT_NAMES = ("ab_norm_g", "ab_w_in", "sgu_norm_g", "sgu_norm_b", "sgu_w", "sgu_bias", "q_norm_g", "k_norm_g",
                "ab_w_out", "cd_norm_g", "cd_w_in", "conv_c_w", "conv_c_b", "c_ln_g", "c_ln_b", "conv_d_w",
                "cd_w_out", "ffn_norm_g", "ffn_w_gate", "ffn_w_up", "ffn_w_down")
SMALL_2D = (("ab_norm_g", (1, 1024)), ("sgu_norm_g", (1, 512)), ("sgu_norm_b", (1, 512)), ("sgu_w", (512, 128)),
            ("sgu_bias", (4, 128)), ("q_norm_g", (3, 64)), ("k_norm_g", (3, 64)), ("cd_norm_g", (1, 128)),
            ("conv_c_w", (31, 64)), ("conv_c_b", (1, 64)), ("c_ln_g", (1, 64)), ("c_ln_b", (1, 64)),
            ("conv_d_w", (3, 64)), ("ffn_norm_g", (2, 1024)))
SHARD_C = 64


def _pack_rows(parts, rows):
    flat = jnp.concatenate([p.reshape(-1) for p in parts])
    return jnp.pad(flat, (0, rows * 128 - flat.shape[0])).reshape(rows, 128)


def kernel(x, ab_norm_g, ab_w_in, sgu_norm_g, sgu_norm_b, sgu_w, sgu_bias, q_norm_g, k_norm_g, ab_w_out, cd_norm_g, cd_w_in, conv_c_w, conv_c_b, c_ln_g, c_ln_b, conv_d_w, cd_w_out, ffn_norm_g, ffn_w_gate, ffn_w_up, ffn_w_down, loss_target, m_ab_norm_g, m_ab_w_in, m_sgu_norm_g, m_sgu_norm_b, m_sgu_w, m_sgu_bias, m_q_norm_g, m_k_norm_g, m_ab_w_out, m_cd_norm_g, m_cd_w_in, m_conv_c_w, m_conv_c_b, m_c_ln_g, m_c_ln_b, m_conv_d_w, m_cd_w_out, m_ffn_norm_g, m_ffn_w_gate, m_ffn_w_up, m_ffn_w_down, v_ab_norm_g, v_ab_w_in, v_sgu_norm_g, v_sgu_norm_b, v_sgu_w, v_sgu_bias, v_q_norm_g, v_k_norm_g, v_ab_w_out, v_cd_norm_g, v_cd_w_in, v_conv_c_w, v_conv_c_b, v_c_ln_g, v_c_ln_b, v_conv_d_w, v_cd_w_out, v_ffn_norm_g, v_ffn_w_gate, v_ffn_w_up, v_ffn_w_down):
    w = dict(zip(WEIGHT_NAMES, (ab_norm_g, ab_w_in, sgu_norm_g, sgu_norm_b, sgu_w, sgu_bias, q_norm_g, k_norm_g, ab_w_out, cd_norm_g, cd_w_in, conv_c_w, conv_c_b, c_ln_g, c_ln_b, conv_d_w, cd_w_out, ffn_norm_g, ffn_w_gate, ffn_w_up, ffn_w_down)))
    m = dict(zip(WEIGHT_NAMES, (m_ab_norm_g, m_ab_w_in, m_sgu_norm_g, m_sgu_norm_b, m_sgu_w, m_sgu_bias, m_q_norm_g, m_k_norm_g, m_ab_w_out, m_cd_norm_g, m_cd_w_in, m_conv_c_w, m_conv_c_b, m_c_ln_g, m_c_ln_b, m_conv_d_w, m_cd_w_out, m_ffn_norm_g, m_ffn_w_gate, m_ffn_w_up, m_ffn_w_down)))
    v = dict(zip(WEIGHT_NAMES, (v_ab_norm_g, v_ab_w_in, v_sgu_norm_g, v_sgu_norm_b, v_sgu_w, v_sgu_bias, v_q_norm_g, v_k_norm_g, v_ab_w_out, v_cd_norm_g, v_cd_w_in, v_conv_c_w, v_conv_c_b, v_c_ln_g, v_c_ln_b, v_conv_d_w, v_cd_w_out, v_ffn_norm_g, v_ffn_w_gate, v_ffn_w_up, v_ffn_w_down)))
    me = _dev_index(*_my_place())

    small_local = _pack_rows([w["cd_norm_g"], w["conv_c_w"], w["conv_c_b"], w["c_ln_g"], w["c_ln_b"], w["conv_d_w"]], 24)
    o_ab_in, o_small = _gather_first(w["ab_w_in"][0].T.astype(BF16), small_local, me)
    r_ff = DFF // NDEV
    one = lambda a: (lambda S, j: S[a])
    slot = lambda b: (lambda L, s: L[b].at[s])
    slot2 = lambda b, part: (lambda L, s: L[b].at[part, s])
    shard = lambda a: (lambda S: S[a])

    def layer_shards(layer):
        return (w["ffn_w_gate"][layer].T.astype(BF16), w["ffn_w_up"][layer].T.astype(BF16),
                w["ffn_w_down"][layer].astype(BF16))

    def gathered(own):
        return _landing((NDEV,) + own.shape, BF16, [((me,), own)])

    def gathered2(a, b):
        return _landing((2, NDEV) + a.shape, BF16, [((0, me), a), ((1, me), b)])

    ab_out_s = w["ab_w_out"][0].astype(BF16)
    gate0, up0, down0 = layer_shards(0)
    gathers = {1: _gather_start(
        "gather1_start", [ab_out_s, gate0, up0, down0], [gathered(ab_out_s), gathered2(gate0, up0), gathered(down0)],
        [(shard(0), slot(0)), (shard(1), slot2(1, 0)), (shard(2), slot2(1, 1)), (shard(3), slot(2))], dep=o_small)}

    def fetch(stage, after):
        if stage == "attn0":
            cd_in_s, cd_out_s = w["cd_w_in"][0].T.astype(BF16), w["cd_w_out"][0].astype(BF16)
            gate1, up1, down1 = layer_shards(1)
            gathers[2] = _gather_start(
                "gather2_start", [cd_in_s, cd_out_s, gate1, up1, down1],
                [gathered(cd_in_s), gathered(cd_out_s), gathered2(gate1, up1), gathered(down1)],
                [(shard(0), slot(0)), (shard(1), slot(1)), (shard(2), slot2(2, 0)), (shard(3), slot2(2, 1)),
                 (shard(4), slot(3))], dep=after)
            return {"dep_attn1": gathers[2]["token"]}
        if stage == "attn1":
            gathers[1] = _gather_forward("gather1_forward", gathers[1], after)
            return {"dep_attn2": gathers[1]["token"]}
        if stage == "ab_out":
            l_out, l_ffn, l_down = _gather_wait("gather1_wait", gathers[1], after)
            return {"w_ab_out": l_out.reshape(D, D), "wt_ffn_in0": l_ffn.reshape(2 * DFF, D),
                    "w_ffn_down0": l_down.reshape(DFF, D)}
        if stage == "ffn_down0":
            gathers[2] = _gather_forward("gather2_forward", gathers[2], after)
            return {"dep_down0": gathers[2]["token"]}
        if stage == "cd_in":
            l_in, l_out, l_ffn, l_down = _gather_wait("gather2_wait", gathers[2], after)
            return {"wt_cd_in": l_in.reshape(CD_IN, D), "w_cd_out": l_out.reshape(D, D),
                    "wt_ffn_in1": l_ffn.reshape(2 * DFF, D), "w_ffn_down1": l_down.reshape(DFF, D)}
        return {}

    scatters = {}
    rides_with = {"w_ffn_down1": "wt_ffn_in1", "w_cd_out": "wt_cd_in", "w_ffn_down0": "wt_ffn_in0",
                  "w_ab_out": "wt_ab_in"}
    held = {}

    def on_grad(key, arr):
        if key in rides_with:
            held[rides_with[key]] = (key, arr)
            return None
        group = ([held.pop(key)] if key in held else []) + [(key, arr)]
        srcs, lands, items = [], [], []
        for n, (k, a) in enumerate(group):
            if k.startswith("wt_ffn_in"):
                src = a.reshape(2, NDEV, r_ff, D)
                own = lax.dynamic_slice_in_dim(src, me, 1, axis=1)
                lands.append(lax.dynamic_update_slice(lax.empty(src.shape, BF16), own, (0, me, 0, 0)))
                items += [((lambda S, j, n=n: S[n].at[0, j]), slot2(n, 0)), ((lambda S, j, n=n: S[n].at[1, j]), slot2(n, 1))]
            else:
                rows = a.shape[0] // NDEV
                src = a.reshape(NDEV, rows, D)
                own = lax.dynamic_index_in_dim(src, me, 0, keepdims=False)
                lands.append(_landing((1, NDEV, rows, D), BF16, [((0, me), own)]))
                items.append(((lambda S, j, n=n: S[n].at[j]), slot2(n, 0)))
            srcs.append(src)
        st = _exchange_start(f"scatter_{key}_start", srcs, lands, items)
        scatters[key] = (st, [k for k, _ in group])
        return st["token"]

    flat = o_small.reshape(NDEV, 24 * 128)

    def chan(lo, taps):
        return flat[:, lo:lo + taps * SHARD_C].reshape(NDEV, taps, SHARD_C).transpose(1, 0, 2).reshape(taps, 512)

    W = {
        "wt_ab_in": o_ab_in.reshape(AB_IN, D), "dep0": gathers[1]["token"],
        "ab_norm_g": w["ab_norm_g"], "sgu_norm_g": w["sgu_norm_g"], "sgu_norm_b": w["sgu_norm_b"],
        "sgu_w": w["sgu_w"][0], "sgu_bias": w["sgu_bias"][0], "q_norm_g": w["q_norm_g"][0],
        "k_norm_g": w["k_norm_g"][0], "ffn_norm_g": w["ffn_norm_g"],
        "cd_norm_g": flat[:, 0:128].reshape(1, D),
        "conv_c_w32": jnp.pad(chan(128, CONV_C_TAPS), ((0, 1), (0, 0))),
        "conv_c_b": chan(2112, 1), "c_ln_g": chan(2176, 1), "c_ln_b": chan(2240, 1),
        "conv_d_w8": jnp.pad(chan(2304, CONV_D_TAPS), ((0, 8 - CONV_D_TAPS), (0, 0))),
    }

    loss_cols, grad_x, G = _local_step(x[0], loss_target[0], W, fetch, on_grad)
    loss = lax.psum(jnp.sum(loss_cols), ("x", "y", "c"))

    small_parts = [G["ab_norm_g"], G["sgu_norm_g"], G["sgu_norm_b"], G["sgu_w"], G["sgu_bias"], G["q_norm_g"],
                   G["k_norm_g"], G["cd_norm_g"], G["conv_c_w32"][:CONV_C_TAPS], G["conv_c_b"], G["c_ln_g"],
                   G["c_ln_b"], G["conv_d_w8"][:CONV_D_TAPS], G["ffn_norm_g0"], G["ffn_norm_g1"]]
    sizes = [p.size for p in small_parts]
    small_rows = 712
    packed = _pack_rows(small_parts, small_rows)
    small = _exchange_start("small_start", [packed], [_landing((NDEV, small_rows, 128), F32, [((me,), packed)])],
                            [(one(0), slot(0))])
    landed = {}

    def wait_scatters(name, group_keys, after):
        res = _exchange_wait(name, [scatters[gk][0] for gk in group_keys], after)
        for gk, lands in zip(group_keys, res):
            landed.update(zip(scatters[gk][1], lands))

    wait_scatters("scatter_wait_early", ["wt_ffn_in1", "wt_cd_in", "wt_ffn_in0"], small["token"])

    grads, deltas, new_m, new_v = {}, {}, {}, {}
    done = []

    def put(name, res):
        grads[name], deltas[name], new_m[name], new_v[name] = res

    def adam(name, lands, sel, transposed):
        flip = (lambda a: jnp.swapaxes(a, 1, 2)) if transposed else (lambda a: a)
        res = _adam_stacked(lands, sel, flip(w[name]), flip(m[name]), flip(v[name]), f"adam_{name}")
        done.append(res[1])
        put(name, [flip(r) for r in res])

    ffn_in_lands = [landed["wt_ffn_in0"], landed["wt_ffn_in1"]]
    adam("cd_w_in", [landed["wt_cd_in"]], 0, True)
    adam("ffn_w_gate", ffn_in_lands, 0, True)
    adam("ffn_w_up", ffn_in_lands, 1, True)
    adam("cd_w_out", [landed["w_cd_out"]], 0, False)
    adam("ffn_w_down", [landed["w_ffn_down0"], landed["w_ffn_down1"]], 0, False)

    small_land = _exchange_wait("small_wait", [small], list(done))[0][0]
    red = _sum_slots(small_land).reshape(-1)
    offs = [0]
    for s in sizes:
        offs.append(offs[-1] + s)
    seg = [red[offs[i]:offs[i + 1]] for i in range(len(sizes))]

    def own_channels(full, taps):
        return lax.dynamic_slice_in_dim(full.reshape(taps, 512), me * SHARD_C, SHARD_C, axis=1)

    g_small = {
        "ab_norm_g": seg[0].reshape(1, 1024), "sgu_norm_g": seg[1].reshape(1, 512), "sgu_norm_b": seg[2].reshape(1, 512),
        "sgu_w": seg[3].reshape(512, 128), "sgu_bias": seg[4].reshape(4, 128), "q_norm_g": seg[5].reshape(3, 64),
        "k_norm_g": seg[6].reshape(3, 64),
        "cd_norm_g": lax.dynamic_slice_in_dim(seg[7].reshape(1, D), me * (D // NDEV), D // NDEV, axis=1),
        "conv_c_w": own_channels(seg[8], CONV_C_TAPS), "conv_c_b": own_channels(seg[9], 1),
        "c_ln_g": own_channels(seg[10], 1), "c_ln_b": own_channels(seg[11], 1),
        "conv_d_w": own_channels(seg[12], CONV_D_TAPS),
        "ffn_norm_g": jnp.concatenate([seg[13].reshape(1, D), seg[14].reshape(1, D)], axis=0),
    }

    names2d = [n for n, _ in SMALL_2D]
    d_s, m_s, v_s = _adam_small([w[n].reshape(s) for n, s in SMALL_2D], [g_small[n] for n in names2d],
                                [m[n].reshape(s) for n, s in SMALL_2D], [v[n].reshape(s) for n, s in SMALL_2D])
    for i, n in enumerate(names2d):
        shape = w[n].shape
        grads[n], deltas[n] = g_small[n].reshape(shape), d_s[i].reshape(shape)
        new_m[n], new_v[n] = m_s[i].reshape(shape), v_s[i].reshape(shape)

    wait_scatters("scatter_wait_last", ["wt_ab_in"], d_s[0])
    adam("ab_w_out", [landed["w_ab_out"]], 0, False)
    adam("ab_w_in", [landed["wt_ab_in"]], 0, True)

    return (loss, grad_x[None], *[grads[n] for n in WEIGHT_NAMES], *[deltas[n] for n in WEIGHT_NAMES],
            *[new_m[n] for n in WEIGHT_NAMES], *[new_v[n] for n in WEIGHT_NAMES])
```

```python
import functools

import jax
import jax.numpy as jnp
import numpy as np
from jax import lax
from jax.experimental import pallas as pl
from jax.experimental.pallas import tpu as pltpu

F32 = jnp.float32
BF16 = jnp.bfloat16

T = 4096
D = 1024
NDEV = 8
EPS = 1e-6
NEG_INF = -1e30
DFF = 2816
AB_IN = 5632
CD_IN = 2560
HEAD = 64
PAIR = 128
NPAIR = 4
NBACK = 128
DIL_RATES = (1, 4, 16)
ROPE_HALF = 8
ROPE_THETA = 500000.0
CONV_C_TAPS = 31
CONV_D_TAPS = 3
HALO = 32
ATTN_BWD_UNROLL = 4

ADAM_LR = 0.001
ADAM_B1 = 0.9
ADAM_B2 = 0.999
ADAM_EPS = 1e-08
ADAM_WD = 0.01
ADAM_STEP = 10
ADAM_C1 = 1.0 / (1.0 - ADAM_B1 ** ADAM_STEP)
ADAM_C2 = 1.0 / (1.0 - ADAM_B2 ** ADAM_STEP)

VMEM_LIMIT_MB = 48
MESH = pl.DeviceIdType.MESH
HBM_SPEC = pl.BlockSpec(memory_space=pl.ANY)


def _cparams(ngrid, vmem_mb=VMEM_LIMIT_MB):
    return pltpu.CompilerParams(dimension_semantics=("arbitrary",) * ngrid,
                                vmem_limit_bytes=vmem_mb * 1024 * 1024)


def _pick(n, options):
    for o in options:
        if n % o == 0:
            return o
    raise ValueError(f"no tile for {n} in {options}")


def _sds(shape, dtype):
    return jax.ShapeDtypeStruct(shape, dtype)


def _sigmoid(x):
    return 1.0 / (1.0 + jnp.exp(-x))


def _sigmoid_bf16(x):
    return 0.5 * jnp.tanh(0.5 * x) + 0.5


def _gelu(z):
    return 0.5 * z * (1.0 + lax.erf(z * 0.7071067811865476))


def _gelu_grad(z):
    return 0.5 * (1.0 + lax.erf(z * 0.7071067811865476)) + z * jnp.exp(-0.5 * z * z) * 0.3989422804014327


def _mm_nt(a, wt, name, out_dtype=BF16, tm=2048, dep=None):
    M, K = a.shape
    N = wt.shape[0]
    tn = _pick(N, (512, 256))

    def body(a_ref, w_ref, *rest):
        o_ref = rest[-1]
        o_ref[...] = lax.dot_general(a_ref[...], w_ref[...], (((1,), (1,)), ((), ())),
                                     preferred_element_type=F32).astype(o_ref.dtype)

    in_specs = [pl.BlockSpec((tm, K), lambda i, j: (i, 0)), pl.BlockSpec((tn, K), lambda i, j: (j, 0))]
    args = [a, wt]
    if dep is not None:
        in_specs.append(HBM_SPEC)
        args.append(dep)
    return pl.pallas_call(
        body, name=name, grid=(M // tm, N // tn), in_specs=in_specs,
        out_specs=pl.BlockSpec((tm, tn), lambda i, j: (i, j)),
        out_shape=_sds((M, N), out_dtype), compiler_params=_cparams(2))(*args)


EPI_ROWS = 256


def _mm_nn(a, w, name, mode, resid, gain=None, tgt=None, dep=None, tm=512):
    M, K = a.shape
    N = w.shape[1]
    side = gain if mode == "rms" else tgt

    def body(a_ref, w_ref, resid_ref, side_ref, *rest):
        outs, acc = rest[-3 if mode == "rms" else -4:-1], rest[-1]
        i = pl.program_id(0)
        acc[...] = jnp.dot(a_ref[...], w_ref[...], preferred_element_type=F32)

        if mode == "loss":
            @pl.when(i == 0)
            def _():
                outs[2][...] = jnp.zeros_like(outs[2])

        for r0 in range(0, tm, EPI_ROWS):
            rows = slice(r0, r0 + EPI_ROWS)
            v = acc[rows, :] + resid_ref[rows, :]
            if mode == "rms":
                outs[0][rows, :] = v
                r = lax.rsqrt(jnp.mean(v * v, axis=-1, keepdims=True) + EPS)
                outs[1][rows, :] = (v * r * side_ref[...]).astype(BF16)
            else:
                d = v - side_ref[rows, :]
                outs[2][...] += jnp.sum(d * d, axis=0, keepdims=True) * (0.5 / N)
                dy = d * (1.0 / N)
                outs[0][rows, :] = dy
                outs[1][rows, :] = dy.astype(BF16)

    row = pl.BlockSpec((tm, N), lambda i: (i, 0))
    vec = pl.BlockSpec((1, N), lambda i: (0, 0))
    in_specs = [pl.BlockSpec((tm, K), lambda i: (i, 0)),
                pl.BlockSpec((K, N), lambda i: (0, 0), pipeline_mode=pl.Buffered(1)), row,
                vec if mode == "rms" else row]
    args = [a, w, resid, side]
    if dep is not None:
        in_specs.append(HBM_SPEC)
        args.append(dep)
    if mode == "rms":
        out_specs, out_shape = [row, row], [_sds((M, N), F32), _sds((M, N), BF16)]
    else:
        out_specs, out_shape = [row, row, vec], [_sds((M, N), F32), _sds((M, N), BF16), _sds((1, N), F32)]
    return pl.pallas_call(
        body, name=name, grid=(M // tm,), in_specs=in_specs, out_specs=out_specs, out_shape=out_shape,
        scratch_shapes=[pltpu.VMEM((tm, N), F32)], compiler_params=_cparams(1))(*args)


def _mm_dh_rms_bwd(a, w, x, gain, dres, name, dep=None, tm=512):
    parts = a.shape[0] if a.ndim == 3 else 1
    M, Kp = a.shape[-2], a.shape[-1]
    N = w.shape[1]
    nblk = M // tm
    assert nblk % 2 == 0

    def body(a_ref, w_ref, x_ref, g_ref, dres_ref, *rest):
        dx_ref, dxb_ref, dg_ref, acc0, acc1 = rest[-5:]
        i = pl.program_id(0)

        def matmul(acc):
            if parts == 1:
                acc[...] = jnp.dot(a_ref[...], w_ref[...], preferred_element_type=F32)
            else:
                d = jnp.dot(a_ref[0], w_ref[0:Kp, :], preferred_element_type=F32)
                for p in range(1, parts):
                    d = d + jnp.dot(a_ref[p], w_ref[p * Kp:(p + 1) * Kp, :], preferred_element_type=F32)
                acc[...] = d

        def finish(acc):
            for r0 in range(0, tm, EPI_ROWS // 2):
                rows = slice(r0, r0 + EPI_ROWS // 2)
                v = acc[rows, :]
                xf = x_ref[rows, :]
                r = lax.rsqrt(jnp.mean(xf * xf, axis=-1, keepdims=True) + EPS)
                xhat = xf * r
                dg_ref[...] += jnp.sum(v * xhat, axis=0, keepdims=True)
                dxh = v * g_ref[...]
                tot = dres_ref[rows, :] + r * (dxh - xhat * jnp.mean(dxh * xhat, axis=-1, keepdims=True))
                dx_ref[rows, :] = tot
                dxb_ref[rows, :] = tot.astype(BF16)

        @pl.when(i == 0)
        def _():
            dg_ref[...] = jnp.zeros_like(dg_ref)
            matmul(acc0)

        @pl.when((i > 0) & (i < nblk) & (i % 2 == 1))
        def _():
            matmul(acc1)
            finish(acc0)

        @pl.when((i > 0) & (i < nblk) & (i % 2 == 0))
        def _():
            matmul(acc0)
            finish(acc1)

        @pl.when(i == nblk)
        def _():
            finish(acc1)

    last = nblk - 1
    row = pl.BlockSpec((tm, N), lambda i: (jnp.maximum(i - 1, 0), 0))
    vec = pl.BlockSpec((1, N), lambda i: (0, 0))
    if a.ndim == 3:
        a_spec = pl.BlockSpec((parts, tm, Kp), lambda i: (0, jnp.minimum(i, last), 0))
    else:
        a_spec = pl.BlockSpec((tm, Kp), lambda i: (jnp.minimum(i, last), 0))
    w_spec = pl.BlockSpec((parts * Kp, N), lambda i: (0, 0), pipeline_mode=pl.Buffered(1))
    in_specs = [a_spec, w_spec, row, vec, row]
    args = [a, w, x, gain, dres]
    if dep is not None:
        in_specs.append(HBM_SPEC)
        args.append(dep)
    return pl.pallas_call(
        body, name=name, grid=(nblk + 1,), in_specs=in_specs, out_specs=[row, row, vec],
        out_shape=[_sds((M, N), F32), _sds((M, N), BF16), _sds((1, N), F32)],
        scratch_shapes=[pltpu.VMEM((tm, N), F32), pltpu.VMEM((tm, N), F32)], compiler_params=_cparams(1, 56))(*args)


def _mm_tn(a, b, name, out_dtype=BF16, tt=1024):
    parts = a.shape[0] if a.ndim == 3 else 1
    Tt, Mp = a.shape[-2], a.shape[-1]
    N = b.shape[1]
    tn = _pick(Mp, (1408, 1280, 1024, 512))
    jper = Mp // tn
    nt = Tt // tt

    def body(a_ref, b_ref, o_ref, acc):
        t = pl.program_id(1)

        @pl.when(t == 0)
        def _():
            acc[...] = jnp.zeros_like(acc)

        acc[...] += lax.dot_general(a_ref[...], b_ref[...], (((0,), (0,)), ((), ())),
                                    preferred_element_type=F32)

        @pl.when(t == nt - 1)
        def _():
            o_ref[...] = acc[...].astype(o_ref.dtype)

    if a.ndim == 3:
        a_spec = pl.BlockSpec((None, tt, tn), lambda j, t: (j // jper, t, j % jper))
    else:
        a_spec = pl.BlockSpec((tt, tn), lambda j, t: (t, j))
    return pl.pallas_call(
        body, name=name, grid=(parts * jper, nt),
        in_specs=[a_spec, pl.BlockSpec((tt, N), lambda j, t: (t, 0))],
        out_specs=pl.BlockSpec((tn, N), lambda j, t: (j, 0)),
        out_shape=_sds((parts * Mp, N), out_dtype), scratch_shapes=[pltpu.VMEM((tn, N), F32)],
        compiler_params=_cparams(2))(a, b)


def _ffn_in(h, wt_in, name, tm=2048, tn=256):
    nj = DFF // tn

    def body(h_ref, wg_ref, wu_ref, p_ref, act_ref):
        nt = (((1,), (1,)), ((), ()))
        g = lax.dot_general(h_ref[...], wg_ref[...], nt, preferred_element_type=F32).astype(BF16)
        u = lax.dot_general(h_ref[...], wu_ref[...], nt, preferred_element_type=F32).astype(BF16)
        p_ref[0] = g
        p_ref[1] = u
        act_ref[...] = g * _sigmoid_bf16(g) * u

    return pl.pallas_call(
        body, name=name, grid=(T // tm, nj),
        in_specs=[pl.BlockSpec((tm, D), lambda i, j: (i, 0)), pl.BlockSpec((tn, D), lambda i, j: (j, 0)),
                  pl.BlockSpec((tn, D), lambda i, j: (j + nj, 0))],
        out_specs=[pl.BlockSpec((2, tm, tn), lambda i, j: (0, i, j)), pl.BlockSpec((tm, tn), lambda i, j: (i, j))],
        out_shape=[_sds((2, T, DFF), BF16), _sds((T, DFF), BF16)], compiler_params=_cparams(2))(h, wt_in, wt_in)


def _ffn_dact(dyb, w_down, p3, name, tm=2048, tn=256, dep=None):
    def body(dy_ref, w_ref, p_ref, *rest):
        o_ref = rest[-1]
        da = lax.dot_general(dy_ref[...], w_ref[...], (((1,), (1,)), ((), ())),
                             preferred_element_type=F32).astype(BF16)
        g = p_ref[0]
        u = p_ref[1]
        sg = _sigmoid_bf16(g)
        gs = g * sg
        o_ref[0] = (da * u) * (sg + gs * (1.0 - sg))
        o_ref[1] = da * gs

    pspec = pl.BlockSpec((2, tm, tn), lambda i, j: (0, i, j))
    in_specs = [pl.BlockSpec((tm, D), lambda i, j: (i, 0)), pl.BlockSpec((tn, D), lambda i, j: (j, 0)), pspec]
    args = [dyb, w_down, p3]
    if dep is not None:
        in_specs.append(HBM_SPEC)
        args.append(dep)
    return pl.pallas_call(
        body, name=name, grid=(T // tm, DFF // tn), in_specs=in_specs, out_specs=pspec,
        out_shape=_sds((2, T, DFF), BF16), compiler_params=_cparams(2))(*args)


def _rms_fwd(x, g, name, tm=512):
    def body(x_ref, g_ref, h_ref):
        xf = x_ref[...]
        r = lax.rsqrt(jnp.mean(xf * xf, axis=-1, keepdims=True) + EPS)
        h_ref[...] = (xf * r * g_ref[...]).astype(BF16)

    return pl.pallas_call(
        body, name=name, grid=(T // tm,),
        in_specs=[pl.BlockSpec((tm, D), lambda i: (i, 0)), pl.BlockSpec((1, D), lambda i: (0, 0))],
        out_specs=pl.BlockSpec((tm, D), lambda i: (i, 0)),
        out_shape=_sds((T, D), BF16), compiler_params=_cparams(1))(x, g)


def _tril_mask():
    r = lax.broadcasted_iota(jnp.int32, (128, 128), 0)
    c = lax.broadcasted_iota(jnp.int32, (128, 128), 1)
    return r >= c


def _mix_a_fwd(pab, sgu_g, sgu_b, sgu_w, sgu_bias3, tm=512):
    def body(zu_ref, zv_ref, g_ref, b_ref, w_ref, bias_ref, o_ref):
        u = _gelu(zu_ref[...].astype(F32))
        v = _gelu(zv_ref[...].astype(F32))
        mu = jnp.mean(v, axis=-1, keepdims=True)
        vc = v - mu
        rstd = lax.rsqrt(jnp.mean(vc * vc, axis=-1, keepdims=True) + EPS)
        vn = (vc * rstd * g_ref[...] + b_ref[...]).astype(BF16)
        tri = _tril_mask()
        for gi in range(4):
            wg = jnp.where(tri, w_ref[gi], 0.0).astype(BF16)
            bg = bias_ref[gi]
            for c in range(tm // 128):
                rs, cs = slice(c * 128, (c + 1) * 128), slice(gi * 128, (gi + 1) * 128)
                mixed = jnp.dot(wg, vn[rs, cs], preferred_element_type=F32) + bg
                o_ref[rs, cs] = (u[rs, cs] * mixed).astype(BF16)

    half = pl.BlockSpec((tm, 512), lambda i: (i, 0))
    return pl.pallas_call(
        body, name="mix_a_fwd", grid=(T // tm,),
        in_specs=[half, pl.BlockSpec((tm, 512), lambda i: (i, 1)),
                  pl.BlockSpec((1, 512), lambda i: (0, 0)), pl.BlockSpec((1, 512), lambda i: (0, 0)),
                  pl.BlockSpec((4, 128, 128), lambda i: (0, 0, 0)), pl.BlockSpec((4, 128, 1), lambda i: (0, 0, 0))],
        out_specs=half, out_shape=_sds((T, D), BF16), compiler_params=_cparams(1),
    )(pab, pab, sgu_g, sgu_b, sgu_w, sgu_bias3)


def _rope_tables():
    pos = np.arange(T, dtype=np.float32)
    inv_freq = np.float32(ROPE_THETA) ** (-np.arange(ROPE_HALF, dtype=np.float32) * np.float32(2.0 / (2 * ROPE_HALF)))
    ang = (pos[:, None] * inv_freq[None, :]).astype(np.float32)
    cos, sin = np.cos(ang), np.sin(ang)
    z8 = np.zeros((T, ROPE_HALF), np.float32)
    rest = np.zeros((T, HEAD - 2 * ROPE_HALF), np.float32)
    c64 = np.concatenate([cos, cos, rest + 1.0], axis=1)
    s1 = np.concatenate([z8, sin, rest], axis=1)
    s2 = np.concatenate([-sin, z8, rest], axis=1)
    return tuple(jnp.asarray(np.tile(t, (1, 2)).astype(np.float32)) for t in (c64, s1, s2))


def _lo_mask(shape):
    return lax.broadcasted_iota(jnp.int32, shape, 1) < HEAD


def _seg_mean(x, lo):
    s_all = jnp.sum(x, axis=-1, keepdims=True)
    s_lo = jnp.sum(jnp.where(lo, x, 0.0), axis=-1, keepdims=True)
    return jnp.where(lo, s_lo, s_all - s_lo) * (1.0 / HEAD)


def _rope(n, c, s1, s2):
    return n * c + pltpu.roll(n, ROPE_HALF, 1) * s1 + pltpu.roll(n, PAIR - ROPE_HALF, 1) * s2


def _rope_t(dy, c, s1, s2):
    return dy * c - pltpu.roll(dy, PAIR - ROPE_HALF, 1) * s2 - pltpu.roll(dy, ROPE_HALF, 1) * s1


def _prep_fwd(pab, qg, kg, tabs, tm=512):
    def body(p_ref, qg_ref, kg_ref, c_ref, s1_ref, s2_ref, *outs):
        lo = _lo_mask((tm, PAIR))
        c, s1, s2 = c_ref[...], s1_ref[...], s2_ref[...]
        for g in range(3):
            qn_ref, kn_ref, v_ref = outs[3 * g:3 * g + 3]
            for p in range(NPAIR):
                for which, gains, dst in ((0, qg_ref, qn_ref), (1, kg_ref, kn_ref)):
                    col = (2 + 3 * which + g) * 512 + p * PAIR
                    xr = p_ref[:, col:col + PAIR].astype(F32)
                    rinv = lax.rsqrt(_seg_mean(xr * xr, lo) + EPS)
                    dst[p] = _rope(xr * rinv * gains[g:g + 1, :], c, s1, s2)
                col = (8 + g) * 512 + p * PAIR
                v_ref[p] = p_ref[:, col:col + PAIR].astype(F32)

    pm = pl.BlockSpec((NPAIR, tm, PAIR), lambda i: (0, i, 0))
    tab = pl.BlockSpec((tm, PAIR), lambda i: (i, 0))
    gain = pl.BlockSpec((3, PAIR), lambda i: (0, 0))
    return pl.pallas_call(
        body, name="prep_fwd", grid=(T // tm,),
        in_specs=[pl.BlockSpec((tm, AB_IN), lambda i: (i, 0)), gain, gain, tab, tab, tab],
        out_specs=[pm] * 9, out_shape=[_sds((NPAIR, T, PAIR), F32)] * 9,
        compiler_params=_cparams(1))(pab, qg, kg, *tabs)


def _res_index(it, rate):
    window = NBACK * rate
    b = it // rate
    rho = it % rate
    start = b * window + rho
    startp = jnp.maximum(start - window, rho)
    kmin = jnp.where(b > 0, 0, NBACK)
    return start, startp, kmin


def _rows(start, rate):
    if rate == 1:
        return pl.ds(pl.multiple_of(start, NBACK), NBACK)
    return pl.ds(start, NBACK, stride=rate)


def _band():
    qi = lax.broadcasted_iota(jnp.int32, (NBACK, 2 * NBACK), 0)
    kj = lax.broadcasted_iota(jnp.int32, (NBACK, 2 * NBACK), 1)
    dist = qi + NBACK - kj
    return (dist >= 0) & (dist <= NBACK), kj


def _attn_fwd(qn, kn, v, rate, name, dep=None):
    def body(q_ref, k_ref, v_ref, *rest):
        o_ref, l_ref = rest[-2:]
        lo = _lo_mask((NBACK, PAIR))
        band, kj = _band()

        def step(it, carry):
            start, startp, kmin = _res_index(it, rate)
            q = q_ref[_rows(start, rate), :]
            kcat = jnp.concatenate([k_ref[_rows(startp, rate), :], k_ref[_rows(start, rate), :]], axis=0).astype(BF16)
            vcat = jnp.concatenate([v_ref[_rows(startp, rate), :], v_ref[_rows(start, rate), :]], axis=0).astype(BF16)
            vcat1 = jnp.concatenate([vcat, jnp.ones((2 * NBACK, PAIR), BF16)], axis=1)
            ok = band & (kj >= kmin)
            q2 = jnp.concatenate([jnp.where(lo, q, 0.0), jnp.where(lo, 0.0, q)], axis=0).astype(BF16)
            s = lax.dot_general(q2, kcat, (((1,), (1,)), ((), ())), preferred_element_type=F32) * (HEAD ** -0.5)
            s = jnp.where(jnp.concatenate([ok, ok], axis=0), s, NEG_INF)
            m = jnp.max(s, axis=-1, keepdims=True)
            ol = jnp.dot(jnp.exp(s - m).astype(BF16), vcat1, preferred_element_type=F32)
            o2 = ol[:, 0:PAIR] / ol[:, PAIR:]
            ls = m + jnp.log(ol[:, PAIR:])
            o_ref[_rows(start, rate), :] = jnp.where(lo, o2[0:NBACK], o2[NBACK:])
            l_ref[_rows(start, rate), :] = jnp.where(lo, ls[0:NBACK], ls[NBACK:])
            return carry

        lax.fori_loop(0, T // NBACK, step, 0, unroll=4)

    pm = pl.BlockSpec((None, T, PAIR), lambda p: (p, 0, 0))
    in_specs, args = [pm, pm, pm], [qn, kn, v]
    if dep is not None:
        in_specs.append(HBM_SPEC)
        args.append(dep)
    return pl.pallas_call(
        body, name=name, grid=(NPAIR,), in_specs=in_specs, out_specs=[pm, pm],
        out_shape=[_sds((NPAIR, T, PAIR), F32)] * 2, compiler_params=_cparams(1))(*args)


def _merge_fwd(cat_ab, outs, lses, tm=512):
    def body(cat_in, o0, o1, o2, l0, l1, l2, cat_ref, lse_ref):
        del cat_in
        for p in range(NPAIR):
            a0, a1, a2 = l0[p], l1[p], l2[p]
            m = jnp.maximum(jnp.maximum(a0, a1), a2)
            w0, w1, w2 = jnp.exp(a0 - m), jnp.exp(a1 - m), jnp.exp(a2 - m)
            s = w0 + w1 + w2
            b = (w0 * o0[p] + w1 * o1[p] + w2 * o2[p]) / s
            cat_ref[:, p * PAIR:(p + 1) * PAIR] = b.astype(BF16)
            lse_ref[p] = m + jnp.log(s)

    pm = pl.BlockSpec((NPAIR, tm, PAIR), lambda i: (0, i, 0))
    return pl.pallas_call(
        body, name="merge_fwd", grid=(T // tm,),
        in_specs=[pl.BlockSpec(memory_space=pl.ANY)] + [pm] * 6,
        out_specs=[pl.BlockSpec((tm, 512), lambda i: (i, 1)), pm],
        out_shape=[_sds((T, D), BF16), _sds((NPAIR, T, PAIR), F32)],
        input_output_aliases={0: 0}, compiler_params=_cparams(1))(cat_ab, *outs, *lses)


def _b_pre_bwd(dcat, cat, tm=512):
    def body(db_ref, b_ref, dbp_ref, e_ref):
        lo = _lo_mask((tm, PAIR))
        for p in range(NPAIR):
            db = db_ref[:, p * PAIR:(p + 1) * PAIR].astype(F32)
            b = b_ref[:, p * PAIR:(p + 1) * PAIR].astype(F32)
            dbp_ref[p] = db
            e_ref[p] = _seg_mean(db * b, lo) * float(HEAD)

    pm = pl.BlockSpec((NPAIR, tm, PAIR), lambda i: (0, i, 0))
    right = pl.BlockSpec((tm, 512), lambda i: (i, 1))
    return pl.pallas_call(
        body, name="b_pre_bwd", grid=(T // tm,), in_specs=[right, right], out_specs=[pm, pm],
        out_shape=[_sds((NPAIR, T, PAIR), F32)] * 2, compiler_params=_cparams(1))(dcat, cat)


def _attn_bwd(qn, kn, v, dbp, e, lse, rate, name):
    def body(q_ref, k_ref, v_ref, db_ref, e_ref, lse_ref, dq_ref, dk_ref, dv_ref):
        lo = _lo_mask((NBACK, PAIR))
        band, kj = _band()
        scale = HEAD ** -0.5
        nt = (((1,), (1,)), ((), ()))
        tn = (((0,), (0,)), ((), ()))
        window = NBACK * rate
        nblk = T // window

        def one(it, carry):
            dk_carry, dv_carry = carry
            rho = it // nblk
            b = it % nblk
            start = b * window + rho
            rq = _rows(start, rate)
            rp = _rows(jnp.maximum(start - window, rho), rate)
            kmin = jnp.where(b > 0, 0, NBACK)
            q = q_ref[rq, :]
            db = db_ref[rq, :]
            ev = e_ref[rq, :]
            ls = lse_ref[rq, :]
            kcat = jnp.concatenate([k_ref[rp, :], k_ref[rq, :]], axis=0).astype(BF16)
            vcat = jnp.concatenate([v_ref[rp, :], v_ref[rq, :]], axis=0).astype(BF16)
            ok = band & (kj >= kmin)
            ok2 = jnp.concatenate([ok, ok], axis=0)
            q2 = jnp.concatenate([jnp.where(lo, q, 0.0), jnp.where(lo, 0.0, q)], axis=0).astype(BF16)
            db2 = jnp.concatenate([jnp.where(lo, db, 0.0), jnp.where(lo, 0.0, db)], axis=0).astype(BF16)
            ls2 = jnp.concatenate([ls[:, 0:1], ls[:, HEAD:HEAD + 1]], axis=0)
            ev2 = jnp.concatenate([ev[:, 0:1], ev[:, HEAD:HEAD + 1]], axis=0)
            s = lax.dot_general(q2, kcat, nt, preferred_element_type=F32) * scale
            s = jnp.where(ok2, s, NEG_INF)
            pt = jnp.exp(s - ls2)
            dp = lax.dot_general(db2, vcat, nt, preferred_element_type=F32)
            ds = (pt * (dp - ev2)).astype(BF16)
            dq2 = jnp.dot(ds, kcat, preferred_element_type=F32) * scale
            dkc = lax.dot_general(ds, q2, tn, preferred_element_type=F32) * scale
            dvc = lax.dot_general(pt.astype(BF16), db2, tn, preferred_element_type=F32)
            dq_ref[rq, :] = jnp.where(lo, dq2[0:NBACK], dq2[NBACK:])
            dk_ref[rp, :] = dk_carry + dkc[0:NBACK]
            dk_ref[rq, :] = dkc[NBACK:]
            dv_ref[rp, :] = dv_carry + dvc[0:NBACK]
            dv_ref[rq, :] = dvc[NBACK:]
            return dkc[NBACK:], dvc[NBACK:]

        def step(i, carry):
            for u in range(ATTN_BWD_UNROLL):
                carry = one(i * ATTN_BWD_UNROLL + u, carry)
            return carry

        zero = jnp.zeros((NBACK, PAIR), F32)
        lax.fori_loop(0, T // NBACK // ATTN_BWD_UNROLL, step, (zero, zero))

    pm = pl.BlockSpec((None, T, PAIR), lambda p: (p, 0, 0))
    return pl.pallas_call(
        body, name=name, grid=(NPAIR,), in_specs=[pm] * 6, out_specs=[pm] * 3,
        out_shape=[_sds((NPAIR, T, PAIR), F32)] * 3, compiler_params=_cparams(1, 56))(qn, kn, v, dbp, e, lse)


def _ab_in_bwd(pab, dcat, sgu_g, sgu_b, sgu_w, sgu_bias3, qg, kg, tabs, dqkv, tm=256):
    def body(p_ref, dcat_ref, g_ref, b_ref, w_ref, bias_ref, qg_ref, kg_ref, c_ref, s1_ref, s2_ref, *rest):
        dq_refs = rest[0:9]
        o_ref, dwm_ref, dbias_ref, dsg_ref, dsb_ref, dgain_ref = rest[9:]
        i = pl.program_id(0)

        @pl.when(i == 0)
        def _():
            dwm_ref[...] = jnp.zeros_like(dwm_ref)
            dbias_ref[...] = jnp.zeros_like(dbias_ref)
            dsg_ref[...] = jnp.zeros_like(dsg_ref)
            dsb_ref[...] = jnp.zeros_like(dsb_ref)
            dgain_ref[...] = jnp.zeros_like(dgain_ref)

        zu = p_ref[:, 0:512].astype(F32)
        zv = p_ref[:, 512:1024].astype(F32)
        u = _gelu(zu)
        v = _gelu(zv)
        mu = jnp.mean(v, axis=-1, keepdims=True)
        vc = v - mu
        rstd = lax.rsqrt(jnp.mean(vc * vc, axis=-1, keepdims=True) + EPS)
        xhat = vc * rstd
        vn = (xhat * g_ref[...] + b_ref[...]).astype(BF16)
        da = dcat_ref[...].astype(F32)
        tri = _tril_mask()
        du_parts = [[None] * 4 for _ in range(tm // 128)]
        dvn_parts = [[None] * 4 for _ in range(tm // 128)]
        for gi in range(4):
            wg = jnp.where(tri, w_ref[gi], 0.0).astype(BF16)
            bg = bias_ref[gi]
            for c in range(tm // 128):
                rs, cs = slice(c * 128, (c + 1) * 128), slice(gi * 128, (gi + 1) * 128)
                vblk = vn[rs, cs]
                mixed = jnp.dot(wg, vblk, preferred_element_type=F32) + bg
                dab = da[rs, cs]
                du_parts[c][gi] = dab * mixed
                dmixed = dab * u[rs, cs]
                dmb = dmixed.astype(BF16)
                dvn_parts[c][gi] = lax.dot_general(wg, dmb, (((0,), (0,)), ((), ())), preferred_element_type=F32)
                dwm = lax.dot_general(dmb, vblk, (((1,), (1,)), ((), ())), preferred_element_type=F32)
                dwm_ref[gi] += jnp.where(tri, dwm, 0.0)
                dbias_ref[gi] += dmixed
        du = jnp.concatenate([jnp.concatenate(r, axis=1) for r in du_parts], axis=0)
        dvn = jnp.concatenate([jnp.concatenate(r, axis=1) for r in dvn_parts], axis=0)
        dsg_ref[...] += jnp.sum(dvn * xhat, axis=0, keepdims=True)
        dsb_ref[...] += jnp.sum(dvn, axis=0, keepdims=True)
        dxh = dvn * g_ref[...]
        dv = rstd * (dxh - jnp.mean(dxh, axis=-1, keepdims=True)
                     - xhat * jnp.mean(dxh * xhat, axis=-1, keepdims=True))
        o_ref[:, 0:512] = (du * _gelu_grad(zu)).astype(BF16)
        o_ref[:, 512:1024] = (dv * _gelu_grad(zv)).astype(BF16)

        lo = _lo_mask((tm, PAIR))
        c, s1, s2 = c_ref[...], s1_ref[...], s2_ref[...]
        for g in range(3):
            dq_ref, dk_ref, dv_ref = dq_refs[3 * g:3 * g + 3]
            for p in range(NPAIR):
                for which, gains, src in ((0, qg_ref, dq_ref), (1, kg_ref, dk_ref)):
                    col = (2 + 3 * which + g) * 512 + p * PAIR
                    xr = p_ref[:, col:col + PAIR].astype(F32)
                    rinv = lax.rsqrt(_seg_mean(xr * xr, lo) + EPS)
                    xh = xr * rinv
                    dn = _rope_t(src[p], c, s1, s2)
                    row = 2 * g + which
                    dgain_ref[row:row + 1, :] += jnp.sum(dn * xh, axis=0, keepdims=True)
                    dxh2 = dn * gains[g:g + 1, :]
                    dx = rinv * (dxh2 - xh * _seg_mean(dxh2 * xh, lo))
                    o_ref[:, col:col + PAIR] = dx.astype(BF16)
                col = (8 + g) * 512 + p * PAIR
                o_ref[:, col:col + PAIR] = dv_ref[p].astype(BF16)

    pm = pl.BlockSpec((NPAIR, tm, PAIR), lambda i: (0, i, 0))
    tab = pl.BlockSpec((tm, PAIR), lambda i: (i, 0))
    gain = pl.BlockSpec((3, PAIR), lambda i: (0, 0))
    vec = pl.BlockSpec((1, 512), lambda i: (0, 0))
    full = pl.BlockSpec((tm, AB_IN), lambda i: (i, 0))
    w4 = pl.BlockSpec((4, 128, 128), lambda i: (0, 0, 0))
    return pl.pallas_call(
        body, name="ab_in_bwd", grid=(T // tm,),
        in_specs=[full, pl.BlockSpec((tm, 512), lambda i: (i, 0)), vec, vec, w4,
                  pl.BlockSpec((4, 128, 1), lambda i: (0, 0, 0)), gain, gain, tab, tab, tab] + [pm] * 9,
        out_specs=[full, w4, w4, vec, vec, pl.BlockSpec((8, PAIR), lambda i: (0, 0))],
        out_shape=[_sds((T, AB_IN), BF16), _sds((4, 128, 128), F32), _sds((4, 128, 128), F32),
                   _sds((1, 512), F32), _sds((1, 512), F32), _sds((8, PAIR), F32)],
        compiler_params=_cparams(1))(pab, dcat, sgu_g, sgu_b, sgu_w, sgu_bias3, qg, kg, *tabs, *dqkv)


def _ln_stats(x):
    mu = jnp.mean(x, axis=-1, keepdims=True)
    xc = x - mu
    rstd = lax.rsqrt(jnp.mean(xc * xc, axis=-1, keepdims=True) + EPS)
    return xc * rstd, rstd


CONV_RC = 64


def _shifted_copies(src, dst, tm):
    dst[0] = src[...]
    for b in range(1, 8):
        dst[b, 0:tm + HALO - 8, :] = src[pl.ds(b, tm + HALO - 8), :]


def _cd_fwd(pcd, cw, cb, lg, lb, dw, tm=512):
    per = tm // HALO

    def body(p_ref, h_ref, cw_ref, cb_ref, lg_ref, lb_ref, dw_ref, cat_ref, c0_ref, c1_ref, dd_ref, y_ref,
             buf, buf2, sb):
        i = pl.program_id(0)
        live = jnp.where(i > 0, 1.0, 0.0)
        a = p_ref[:, 0:512].astype(F32)
        gt = p_ref[:, 512:1024].astype(F32)
        gb = p_ref[:, 1024:1536].astype(F32)
        gc = p_ref[:, 1536:2048].astype(F32)
        hv = p_ref[:, 2048:2560].astype(F32)
        c0 = a * _sigmoid(gt)
        dd = gc * hv
        buf[0:HALO, :] = h_ref[:, 0:512].astype(F32) * _sigmoid(h_ref[:, 512:1024].astype(F32)) * live
        buf[HALO:, :] = c0
        buf2[0:HALO, :] = h_ref[:, 1536:2048].astype(F32) * h_ref[:, 2048:2560].astype(F32) * live
        buf2[HALO:, :] = dd
        c0_ref[...] = c0.astype(BF16)
        dd_ref[...] = dd.astype(BF16)
        _shifted_copies(buf, sb, tm)

        def conv_rows(r, carry):
            base = pl.multiple_of(r * CONV_RC, CONV_RC)
            for c in range(4):
                lanes = slice(c * 128, (c + 1) * 128)
                acc = jnp.broadcast_to(cb_ref[:, lanes], (CONV_RC, 128))
                for j in range(CONV_C_TAPS):
                    a8, b8 = divmod(HALO - (CONV_C_TAPS - 1) + j, 8)
                    acc = acc + cw_ref[j:j + 1, lanes] * sb[b8, pl.ds(base + 8 * a8, CONV_RC), lanes]
                c1_ref[pl.ds(base, CONV_RC), lanes] = acc
            return carry

        lax.fori_loop(0, tm // CONV_RC, conv_rows, 0)
        xhat, _ = _ln_stats(c1_ref[...])
        c2 = xhat * lg_ref[...] + lb_ref[...]
        y = jnp.zeros((tm, 512), F32)
        for j in range(CONV_D_TAPS):
            y = y + dw_ref[j:j + 1, :] * buf2[pl.ds(HALO - (CONV_D_TAPS - 1) + j, tm), :]
        cat_ref[:, 0:512] = (c2 * _sigmoid(c2)).astype(BF16)
        cat_ref[:, 512:1024] = (gb * y).astype(BF16)
        y_ref[...] = y.astype(BF16)

    half = pl.BlockSpec((tm, 512), lambda i: (i, 0))
    vec = pl.BlockSpec((1, 512), lambda i: (0, 0))
    return pl.pallas_call(
        body, name="cd_fwd", grid=(T // tm,),
        in_specs=[pl.BlockSpec((tm, CD_IN), lambda i: (i, 0)),
                  pl.BlockSpec((HALO, CD_IN), lambda i: (jnp.maximum(i * per - 1, 0), 0)),
                  pl.BlockSpec((32, 512), lambda i: (0, 0)), vec, vec, vec, pl.BlockSpec((8, 512), lambda i: (0, 0))],
        out_specs=[pl.BlockSpec((tm, D), lambda i: (i, 0)), half, half, half, half],
        out_shape=[_sds((T, D), BF16), _sds((T, 512), BF16), _sds((T, 512), F32), _sds((T, 512), BF16),
                   _sds((T, 512), BF16)],
        scratch_shapes=[pltpu.VMEM((HALO + tm, 512), F32), pltpu.VMEM((HALO + tm, 512), F32),
                        pltpu.VMEM((8, HALO + tm, 512), F32)],
        compiler_params=_cparams(1))(pcd, pcd, cw, cb, lg, lb, dw)


def _cd_bwd_pw(dcat, c1, pcd, y, lg, lb, tm=512):
    def body(dcat_ref, c1_ref, gb_ref, y_ref, lg_ref, lb_ref, dc1_ref, dy3_ref, dgb_ref, dlg_ref, dlb_ref, dcb_ref):
        i = pl.program_id(0)

        @pl.when(i == 0)
        def _():
            dlg_ref[...] = jnp.zeros_like(dlg_ref)
            dlb_ref[...] = jnp.zeros_like(dlb_ref)
            dcb_ref[...] = jnp.zeros_like(dcb_ref)

        dc = dcat_ref[:, 0:512].astype(F32)
        ddo = dcat_ref[:, 512:1024].astype(F32)
        xhat, rstd = _ln_stats(c1_ref[...])
        c2 = xhat * lg_ref[...] + lb_ref[...]
        sg = _sigmoid(c2)
        dc2 = dc * sg * (1.0 + c2 * (1.0 - sg))
        dlg_ref[...] += jnp.sum(dc2 * xhat, axis=0, keepdims=True)
        dlb_ref[...] += jnp.sum(dc2, axis=0, keepdims=True)
        dxh = dc2 * lg_ref[...]
        dc1 = rstd * (dxh - jnp.mean(dxh, axis=-1, keepdims=True)
                      - xhat * jnp.mean(dxh * xhat, axis=-1, keepdims=True))
        dcb_ref[...] += jnp.sum(dc1, axis=0, keepdims=True)
        dc1_ref[...] = dc1
        dgb_ref[...] = (ddo * y_ref[...].astype(F32)).astype(BF16)
        dy3_ref[...] = ddo * gb_ref[...].astype(F32)

    half = pl.BlockSpec((tm, 512), lambda i: (i, 0))
    vec = pl.BlockSpec((1, 512), lambda i: (0, 0))
    return pl.pallas_call(
        body, name="cd_bwd_pw", grid=(T // tm,),
        in_specs=[pl.BlockSpec((tm, D), lambda i: (i, 0)), half, pl.BlockSpec((tm, 512), lambda i: (i, 2)), half,
                  vec, vec],
        out_specs=[half, half, half, vec, vec, vec],
        out_shape=[_sds((T, 512), F32), _sds((T, 512), F32), _sds((T, 512), BF16),
                   _sds((1, 512), F32), _sds((1, 512), F32), _sds((1, 512), F32)],
        compiler_params=_cparams(1))(dcat, c1, pcd, y, lg, lb)


def _cd_bwd_conv(pcd, dc1, dy3, c0, dd, dgb, cw, dw, tm=256):
    per = tm // HALO
    nblk = T // tm
    last32 = T // HALO - 1

    def body(p_ref, dc1_ref, dc1n_ref, dy3_ref, dy3n_ref, c0_ref, c0p_ref, dd_ref, ddp_ref, dgb_ref, cw_ref, dw_ref,
             o_ref, dcw_ref, ddw_ref, dbuf, cbuf, d3buf, ddbuf, sd, sc, dc0_buf):
        i = pl.program_id(0)
        has_prev = jnp.where(i > 0, 1.0, 0.0)
        has_next = jnp.where(i < nblk - 1, 1.0, 0.0)

        @pl.when(i == 0)
        def _():
            dcw_ref[...] = jnp.zeros_like(dcw_ref)
            ddw_ref[...] = jnp.zeros_like(ddw_ref)

        dc1 = dc1_ref[...]
        dy3 = dy3_ref[...]
        dbuf[0:tm, :] = dc1
        dbuf[tm:, :] = dc1n_ref[...] * has_next
        d3buf[0:tm, :] = dy3
        d3buf[tm:, :] = dy3n_ref[...] * has_next
        cbuf[0:HALO, :] = c0p_ref[...].astype(F32) * has_prev
        cbuf[HALO:, :] = c0_ref[...].astype(F32)
        ddbuf[0:HALO, :] = ddp_ref[...].astype(F32) * has_prev
        ddbuf[HALO:, :] = dd_ref[...].astype(F32)

        _shifted_copies(dbuf, sd, tm)
        _shifted_copies(cbuf, sc, tm)
        n_tiles = tm // CONV_RC

        def dc0_rows(r, carry):
            base = pl.multiple_of(r * CONV_RC, CONV_RC)
            for c in range(4):
                lanes = slice(c * 128, (c + 1) * 128)
                acc = jnp.zeros((CONV_RC, 128), F32)
                for j in range(CONV_C_TAPS):
                    a8, b8 = divmod(CONV_C_TAPS - 1 - j, 8)
                    acc = acc + cw_ref[j:j + 1, lanes] * sd[b8, pl.ds(base + 8 * a8, CONV_RC), lanes]
                dc0_buf[pl.ds(base, CONV_RC), lanes] = acc
            return carry

        lax.fori_loop(0, n_tiles, dc0_rows, 0)

        for c in range(4):
            lanes = slice(c * 128, (c + 1) * 128)
            for j0 in range(0, CONV_C_TAPS, 8):
                taps = list(range(j0, min(j0 + 8, CONV_C_TAPS)))

                def dw_rows(r, accs, lanes=lanes, taps=taps):
                    base = pl.multiple_of(r * CONV_RC, CONV_RC)
                    d = dbuf[pl.ds(base, CONV_RC), lanes]
                    out = []
                    for acc, j in zip(accs, taps):
                        a8, b8 = divmod(HALO - (CONV_C_TAPS - 1) + j, 8)
                        prod = d * sc[b8, pl.ds(base + 8 * a8, CONV_RC), lanes]
                        out.append(acc + jnp.sum(prod.reshape(CONV_RC // 8, 8, 128), axis=0))
                    return tuple(out)

                accs = lax.fori_loop(0, n_tiles, dw_rows, tuple(jnp.zeros((8, 128), F32) for _ in taps))
                for acc, j in zip(accs, taps):
                    dcw_ref[j:j + 1, lanes] += jnp.sum(acc, axis=0, keepdims=True)

        dc0 = dc0_buf[...]
        ddd = jnp.zeros((tm, 512), F32)
        for j in range(CONV_D_TAPS):
            ddd = ddd + dw_ref[j:j + 1, :] * d3buf[pl.ds(CONV_D_TAPS - 1 - j, tm), :]
            ddw_ref[j:j + 1, :] += jnp.sum(dy3 * ddbuf[pl.ds(HALO - (CONV_D_TAPS - 1) + j, tm), :], axis=0, keepdims=True)

        a = p_ref[:, 0:512].astype(F32)
        gt = p_ref[:, 512:1024].astype(F32)
        gc = p_ref[:, 1536:2048].astype(F32)
        hv = p_ref[:, 2048:2560].astype(F32)
        sg = _sigmoid(gt)
        o_ref[:, 0:512] = (dc0 * sg).astype(BF16)
        o_ref[:, 512:1024] = (dc0 * a * sg * (1.0 - sg)).astype(BF16)
        o_ref[:, 1024:1536] = dgb_ref[...]
        o_ref[:, 1536:2048] = (ddd * hv).astype(BF16)
        o_ref[:, 2048:2560] = (ddd * gc).astype(BF16)

    half = pl.BlockSpec((tm, 512), lambda i: (i, 0))
    nxt = pl.BlockSpec((HALO, 512), lambda i: (jnp.minimum((i + 1) * per, last32), 0))
    prv = pl.BlockSpec((HALO, 512), lambda i: (jnp.maximum(i * per - 1, 0), 0))
    full = pl.BlockSpec((tm, CD_IN), lambda i: (i, 0))
    return pl.pallas_call(
        body, name="cd_bwd_conv", grid=(nblk,),
        in_specs=[full, half, nxt, half, nxt, half, prv, half, prv, half,
                  pl.BlockSpec((32, 512), lambda i: (0, 0)), pl.BlockSpec((8, 512), lambda i: (0, 0))],
        out_specs=[full, pl.BlockSpec((32, 512), lambda i: (0, 0)), pl.BlockSpec((8, 512), lambda i: (0, 0))],
        out_shape=[_sds((T, CD_IN), BF16), _sds((32, 512), F32), _sds((8, 512), F32)],
        scratch_shapes=[pltpu.VMEM((tm + HALO, 512), F32), pltpu.VMEM((HALO + tm, 512), F32),
                        pltpu.VMEM((tm + HALO, 512), F32), pltpu.VMEM((HALO + tm, 512), F32),
                        pltpu.VMEM((8, tm + HALO, 512), F32), pltpu.VMEM((8, HALO + tm, 512), F32),
                        pltpu.VMEM((tm, 512), F32)],
        compiler_params=_cparams(1))(pcd, dc1, dc1, dy3, dy3, c0, c0, dd, dd, dgb, cw, dw)


def _local_step(x, tgt, W, fetch=None, on_grad=None):
    W = dict(W)
    if fetch is None:
        fetch = lambda stage, after: {}
    if on_grad is None:
        on_grad = lambda key, arr: None
    tabs = _rope_tables()
    qg = jnp.tile(W["q_norm_g"], (1, 2))
    kg = jnp.tile(W["k_norm_g"], (1, 2))
    bias3 = W["sgu_bias"].reshape(4, 128, 1)
    G = {}

    h0 = _rms_fwd(x, W["ab_norm_g"], "rms_fwd_ab")
    pab = _mm_nt(h0, W["wt_ab_in"], "mm_ab_in", dep=W.get("dep0"))
    cat_ab = _mix_a_fwd(pab, W["sgu_norm_g"], W["sgu_norm_b"], W["sgu_w"], bias3)
    qkv = _prep_fwd(pab, qg, kg, tabs)
    outs, lses = [], []
    for g, rate in enumerate(DIL_RATES):
        o, l = _attn_fwd(qkv[3 * g], qkv[3 * g + 1], qkv[3 * g + 2], rate, f"attn_fwd_{g}", dep=W.get(f"dep_attn{g}"))
        outs.append(o)
        lses.append(l)
        W.update(fetch(f"attn{g}", o))
    cat_ab, lse = _merge_fwd(cat_ab, outs, lses)
    W.update(fetch("ab_out", lse))
    x1, h1 = _mm_nn(cat_ab, W["w_ab_out"], "mm_ab_out", mode="rms", resid=x, gain=W["ffn_norm_g"][0:1])
    pf0, act0 = _ffn_in(h1, W["wt_ffn_in0"], "ffn_in0")
    W.update(fetch("ffn_down0", act0))
    x2, h2 = _mm_nn(act0, W["w_ffn_down0"], "mm_ffn_down0", mode="rms", resid=x1, gain=W["cd_norm_g"],
                    dep=W.get("dep_down0"))
    W.update(fetch("cd_in", h2))
    pcd = _mm_nt(h2, W["wt_cd_in"], "mm_cd_in")
    cat_cd, c0, c1, dd, yv = _cd_fwd(pcd, W["conv_c_w32"], W["conv_c_b"], W["c_ln_g"], W["c_ln_b"], W["conv_d_w8"])
    x3, h3 = _mm_nn(cat_cd, W["w_cd_out"], "mm_cd_out", mode="rms", resid=x2, gain=W["ffn_norm_g"][1:2])
    pf1, act1 = _ffn_in(h3, W["wt_ffn_in1"], "ffn_in1")
    dy, dyb, loss_cols = _mm_nn(act1, W["w_ffn_down1"], "mm_ffn_down1", mode="loss", resid=x3, tgt=tgt)

    def ffn_bwd(xin, h, pf, act, dres, dresb, layer):
        G[f"w_ffn_down{layer}"] = _mm_tn(act, dresb, f"mm_g_ffn_down{layer}")
        dep = on_grad(f"w_ffn_down{layer}", G[f"w_ffn_down{layer}"])
        dpf = _ffn_dact(dresb, W[f"w_ffn_down{layer}"], pf, f"ffn_dact{layer}", dep=dep)
        G[f"wt_ffn_in{layer}"] = _mm_tn(dpf, h, f"mm_g_ffn_in{layer}")
        dep = on_grad(f"wt_ffn_in{layer}", G[f"wt_ffn_in{layer}"])
        dx, dxb, G[f"ffn_norm_g{layer}"] = _mm_dh_rms_bwd(
            dpf, W[f"wt_ffn_in{layer}"], xin, W["ffn_norm_g"][layer:layer + 1], dres, f"mm_d_h_ffn{layer}", dep=dep)
        return dx, dxb

    dx3, dx3b = ffn_bwd(x3, h3, pf1, act1, dy, dyb, 1)

    G["w_cd_out"] = _mm_tn(cat_cd, dx3b, "mm_g_cd_out")
    dep = on_grad("w_cd_out", G["w_cd_out"])
    dcat_cd = _mm_nt(dx3b, W["w_cd_out"], "mm_d_cat_cd", dep=dep)
    dc1, dy3, dgb, G["c_ln_g"], G["c_ln_b"], G["conv_c_b"] = _cd_bwd_pw(dcat_cd, c1, pcd, yv, W["c_ln_g"], W["c_ln_b"])
    dpcd, G["conv_c_w32"], G["conv_d_w8"] = _cd_bwd_conv(pcd, dc1, dy3, c0, dd, dgb, W["conv_c_w32"], W["conv_d_w8"])
    G["wt_cd_in"] = _mm_tn(dpcd, h2, "mm_g_cd_in")
    dep = on_grad("wt_cd_in", G["wt_cd_in"])
    dx2, dx2b, G["cd_norm_g"] = _mm_dh_rms_bwd(dpcd, W["wt_cd_in"], x2, W["cd_norm_g"], dx3, "mm_d_h_cd", dep=dep)

    dx1, dx1b = ffn_bwd(x1, h1, pf0, act0, dx2, dx2b, 0)

    G["w_ab_out"] = _mm_tn(cat_ab, dx1b, "mm_g_ab_out")
    dep = on_grad("w_ab_out", G["w_ab_out"])
    dcat_ab = _mm_nt(dx1b, W["w_ab_out"], "mm_d_cat_ab", dep=dep)
    dbp, e = _b_pre_bwd(dcat_ab, cat_ab)
    dqkv = []
    for g, rate in enumerate(DIL_RATES):
        dqkv += _attn_bwd(qkv[3 * g], qkv[3 * g + 1], qkv[3 * g + 2], dbp, e, lse, rate, f"attn_bwd_{g}")
    dpab, G["sgu_w"], dbias_part, G["sgu_norm_g"], G["sgu_norm_b"], dgain = _ab_in_bwd(
        pab, dcat_ab, W["sgu_norm_g"], W["sgu_norm_b"], W["sgu_w"], bias3, qg, kg, tabs, dqkv)
    G["sgu_bias"] = jnp.sum(dbias_part, axis=-1)
    dgain = dgain[0:6, 0:HEAD] + dgain[0:6, HEAD:PAIR]
    G["q_norm_g"] = dgain[0::2]
    G["k_norm_g"] = dgain[1::2]
    G["wt_ab_in"] = _mm_tn(dpab, h0, "mm_g_ab_in")
    dep = on_grad("wt_ab_in", G["wt_ab_in"])
    grad_x, _, G["ab_norm_g"] = _mm_dh_rms_bwd(dpab, W["wt_ab_in"], x, W["ab_norm_g"], dx1, "mm_d_h_ab", dep=dep)
    return loss_cols, grad_x, G


def _my_place():
    return lax.axis_index("x"), lax.axis_index("y"), lax.axis_index("c")


def _dev_index(px, py, pc):
    return 4 * px + 2 * py + pc


def _flip(place, k):
    x, y, c = place
    return (1 - x if k & 4 else x, 1 - y if k & 2 else y, 1 - c if k & 1 else c)


def _landing(shape, dtype, own):
    buf = lax.empty(shape, dtype)
    for lead, part in own:
        buf = lax.dynamic_update_slice(buf, part.reshape((1,) * len(lead) + part.shape),
                                       tuple(lead) + (0,) * part.ndim)
    return buf


def _gather_first(ab_in_t, small, me_index):
    lands = [_landing((NDEV,) + ab_in_t.shape, BF16, [((me_index,), ab_in_t)]),
             _landing((NDEV,) + small.shape, F32, [((me_index,), small)])]
    n_items = 2

    def body(ab_in_r, small_r, l_ab_in, l_small, o_ab_in, o_small, send_sems, recv_sems):
        del l_ab_in, l_small
        x, y, c = _my_place()
        me = (x, y, c)
        sib = (x, y, 1 - c)
        chips = [(1 - x, y), (x, 1 - y), (1 - x, 1 - y)]
        items = [(ab_in_r, lambda d: o_ab_in.at[d]), (small_r, lambda d: o_small.at[d])]

        def rcopy(it, k, src, dst, to):
            return pltpu.make_async_remote_copy(src_ref=src, dst_ref=dst, send_sem=send_sems.at[it, k],
                                                recv_sem=recv_sems.at[it, k], device_id=to, device_id_type=MESH)

        started = []
        for it, (src, dst) in enumerate(items):
            mine = dst(_dev_index(*me))
            first = [rcopy(it, 0, src, mine, sib)]
            first += [rcopy(it, 1 + j, src, mine, (*chip, c)) for j, chip in enumerate(chips)]
            for cp in first:
                cp.start()
            started += first
        for it, (src, dst) in enumerate(items):
            for j, chip in enumerate(chips):
                blk = dst(_dev_index(*chip, c))
                rcopy(it, 1 + j, blk, blk, me).wait_recv()
                fwd = rcopy(it, 4 + j, blk, blk, sib)
                fwd.start()
                started.append(fwd)
        for it, (src, dst) in enumerate(items):
            blk = dst(_dev_index(x, y, 1 - c))
            rcopy(it, 0, blk, blk, me).wait_recv()
            for j, chip in enumerate(chips):
                blk = dst(_dev_index(*chip, 1 - c))
                rcopy(it, 4 + j, blk, blk, me).wait_recv()
        for cp in started:
            cp.wait_send()

    return pl.pallas_call(
        body, name="gather_first", in_specs=[HBM_SPEC] * 4, out_specs=[HBM_SPEC] * 2,
        out_shape=[_sds(a.shape, a.dtype) for a in lands], input_output_aliases={2: 0, 3: 1},
        scratch_shapes=[pltpu.SemaphoreType.DMA((n_items, 7)), pltpu.SemaphoreType.DMA((n_items, 7))],
    )(ab_in_t, small, *lands)


HBM_ONLY = pl.BlockSpec(memory_space=pltpu.HBM)
SEM_SPEC = pl.BlockSpec(memory_space=pltpu.SEMAPHORE)
IN_FLIGHT = pltpu.CompilerParams(has_side_effects=pltpu.SideEffectType.DATAFLOW_SIDE_EFFECTING)


def _in_hbm(a):
    return pltpu.with_memory_space_constraint(a, pltpu.HBM)


def _exchange_start(name, srcs, lands, items, dep=None):
    ns, nl, ni = len(srcs), len(lands), len(items)

    def body(*refs):
        S, L = refs[0:ns], refs[ns:ns + nl]
        first_out = ns + nl + (0 if dep is None else 1)
        send_sems, recv_sems, token = refs[first_out], refs[first_out + 1], refs[-1]
        me = _my_place()
        mi = _dev_index(*me)
        for i, (src, dst) in enumerate(items):
            for k in range(1, NDEV):
                peer = _flip(me, k)
                pltpu.make_async_remote_copy(
                    src_ref=src(S, _dev_index(*peer)), dst_ref=dst(L, mi), send_sem=send_sems.at[7 * i + k - 1],
                    recv_sem=recv_sems.at[7 * i + k - 1], device_id=peer, device_id_type=MESH).start()
        token[...] = jnp.zeros_like(token)

    thru = [pltpu.HBM(a.shape, a.dtype) for a in list(srcs) + list(lands)]
    args = [_in_hbm(a) for a in srcs] + [_in_hbm(a) for a in lands]
    in_specs = [HBM_ONLY] * (ns + nl)
    if dep is not None:
        args.append(dep)
        in_specs.append(HBM_SPEC)
    outs = pl.pallas_call(
        body, name=name, in_specs=in_specs,
        out_shape=(pltpu.SemaphoreType.DMA((7 * ni,)), pltpu.SemaphoreType.DMA((7 * ni,)), *thru, _sds((8, 128), F32)),
        out_specs=(SEM_SPEC, SEM_SPEC, *[HBM_ONLY] * (ns + nl), pl.BlockSpec(memory_space=pltpu.VMEM)),
        input_output_aliases={j: 2 + j for j in range(ns + nl)}, compiler_params=IN_FLIGHT)(*args)
    return dict(send=outs[0], recv=outs[1], srcs=list(outs[2:2 + ns]), lands=list(outs[2 + ns:2 + ns + nl]),
                token=outs[-1], items=items)


def _exchange_wait(name, states, after):
    after = list(after) if isinstance(after, (list, tuple)) else [after]
    counts = [(len(st["srcs"]), len(st["lands"]), len(st["items"])) for st in states]
    n_arrays = sum(c[0] + c[1] for c in counts)

    def body(*refs):
        me = _my_place()
        mi = _dev_index(*me)
        pos = 0
        sem_pos = n_arrays
        for st, (ns, nl, ni) in zip(states, counts):
            S, L = refs[pos:pos + ns], refs[pos + ns:pos + ns + nl]
            send_sems, recv_sems = refs[sem_pos], refs[sem_pos + 1]
            pos += ns + nl
            sem_pos += 2
            for i, (src, dst) in enumerate(st["items"]):
                for k in range(1, NDEV):
                    cp = pltpu.make_async_remote_copy(
                        src_ref=src(S, mi), dst_ref=dst(L, mi), send_sem=send_sems.at[7 * i + k - 1],
                        recv_sem=recv_sems.at[7 * i + k - 1], device_id=me, device_id_type=MESH)
                    cp.wait_send()
                    cp.wait_recv()

    arrays, sems = [], []
    for st in states:
        arrays += st["srcs"] + st["lands"]
        sems += [st["send"], st["recv"]]
    outs = pl.pallas_call(
        body, name=name, in_specs=[HBM_ONLY] * n_arrays + [SEM_SPEC] * len(sems) + [HBM_SPEC] * len(after),
        out_shape=tuple(pltpu.HBM(a.shape, a.dtype) for a in arrays), out_specs=tuple([HBM_ONLY] * n_arrays),
        input_output_aliases={j: j for j in range(n_arrays)}, compiler_params=IN_FLIGHT)(*arrays, *sems, *after)
    lands, pos = [], 0
    for ns, nl, _ in counts:
        lands.append(list(outs[pos + ns:pos + ns + nl]))
        pos += ns + nl
    return lands


def _place_and_neighbours():
    x, y, c = _my_place()
    return (x, y, c), (x, y, 1 - c), [(1 - x, y), (x, 1 - y), (1 - x, 1 - y)]


def _gather_start(name, srcs, lands, items, dep=None):
    ns, nl, ni = len(srcs), len(lands), len(items)

    def body(*refs):
        S, L = refs[0:ns], refs[ns:ns + nl]
        first_out = ns + nl + (0 if dep is None else 1)
        send_sems, recv_sems, token = refs[first_out], refs[first_out + 1], refs[-1]
        me, sib, chips = _place_and_neighbours()
        mi = _dev_index(*me)
        for i, (src, dst) in enumerate(items):
            for k, to in enumerate([sib] + [(*chip, me[2]) for chip in chips]):
                pltpu.make_async_remote_copy(
                    src_ref=src(S), dst_ref=dst(L, mi), send_sem=send_sems.at[4 * i + k],
                    recv_sem=recv_sems.at[4 * i + k], device_id=to, device_id_type=MESH).start()
        token[...] = jnp.zeros_like(token)

    thru = [pltpu.HBM(a.shape, a.dtype) for a in list(srcs) + list(lands)]
    args = [_in_hbm(a) for a in srcs] + [_in_hbm(a) for a in lands]
    in_specs = [HBM_ONLY] * (ns + nl)
    if dep is not None:
        args.append(dep)
        in_specs.append(HBM_SPEC)
    outs = pl.pallas_call(
        body, name=name, in_specs=in_specs,
        out_shape=(pltpu.SemaphoreType.DMA((4 * ni,)), pltpu.SemaphoreType.DMA((4 * ni,)), *thru, _sds((8, 128), F32)),
        out_specs=(SEM_SPEC, SEM_SPEC, *[HBM_ONLY] * (ns + nl), pl.BlockSpec(memory_space=pltpu.VMEM)),
        input_output_aliases={j: 2 + j for j in range(ns + nl)}, compiler_params=IN_FLIGHT)(*args)
    return dict(send=outs[0], recv=outs[1], srcs=list(outs[2:2 + ns]), lands=list(outs[2 + ns:2 + ns + nl]),
                token=outs[-1], items=items)


def _gather_forward(name, st, after):
    nl, ni = len(st["lands"]), len(st["items"])

    def body(*refs):
        L, recv_sems = refs[0:nl], refs[nl]
        fwd_send, fwd_recv, token = refs[2 * nl + 2], refs[2 * nl + 3], refs[-1]
        me, sib, chips = _place_and_neighbours()
        for i, (_, dst) in enumerate(st["items"]):
            for j, chip in enumerate(chips):
                blk = dst(L, _dev_index(*chip, me[2]))
                pltpu.make_async_remote_copy(
                    src_ref=blk, dst_ref=blk, send_sem=fwd_send.at[3 * i + j], recv_sem=recv_sems.at[4 * i + 1 + j],
                    device_id=me, device_id_type=MESH).wait_recv()
                pltpu.make_async_remote_copy(
                    src_ref=blk, dst_ref=blk, send_sem=fwd_send.at[3 * i + j], recv_sem=fwd_recv.at[3 * i + j],
                    device_id=sib, device_id_type=MESH).start()
        token[...] = jnp.zeros_like(token)

    outs = pl.pallas_call(
        body, name=name, in_specs=[HBM_ONLY] * nl + [SEM_SPEC, HBM_SPEC],
        out_shape=(*[pltpu.HBM(a.shape, a.dtype) for a in st["lands"]], pltpu.SemaphoreType.DMA((3 * ni,)),
                   pltpu.SemaphoreType.DMA((3 * ni,)), _sds((8, 128), F32)),
        out_specs=(*[HBM_ONLY] * nl, SEM_SPEC, SEM_SPEC, pl.BlockSpec(memory_space=pltpu.VMEM)),
        input_output_aliases={j: j for j in range(nl)}, compiler_params=IN_FLIGHT)(*st["lands"], st["recv"], after)
    return dict(st, lands=list(outs[0:nl]), fwd_send=outs[nl], fwd_recv=outs[nl + 1], token=outs[-1])


def _gather_wait(name, st, after):
    ns, nl, ni = len(st["srcs"]), len(st["lands"]), len(st["items"])

    def body(*refs):
        S, L = refs[0:ns], refs[ns:ns + nl]
        send_sems, recv_sems, fwd_send, fwd_recv = refs[ns + nl:ns + nl + 4]
        me, sib, chips = _place_and_neighbours()
        mi = _dev_index(*me)
        for i, (src, dst) in enumerate(st["items"]):
            mine = dst(L, mi)
            for k in range(4):
                pltpu.make_async_remote_copy(
                    src_ref=src(S), dst_ref=mine, send_sem=send_sems.at[4 * i + k], recv_sem=recv_sems.at[4 * i + k],
                    device_id=me, device_id_type=MESH).wait_send()
            pltpu.make_async_remote_copy(
                src_ref=src(S), dst_ref=mine, send_sem=send_sems.at[4 * i], recv_sem=recv_sems.at[4 * i],
                device_id=me, device_id_type=MESH).wait_recv()
            for j in range(3):
                cp = pltpu.make_async_remote_copy(
                    src_ref=mine, dst_ref=mine, send_sem=fwd_send.at[3 * i + j], recv_sem=fwd_recv.at[3 * i + j],
                    device_id=me, device_id_type=MESH)
                cp.wait_send()
                cp.wait_recv()

    arrays = st["srcs"] + st["lands"]
    outs = pl.pallas_call(
        body, name=name, in_specs=[HBM_ONLY] * (ns + nl) + [SEM_SPEC] * 4 + [HBM_SPEC],
        out_shape=tuple(pltpu.HBM(a.shape, a.dtype) for a in arrays), out_specs=tuple([HBM_ONLY] * (ns + nl)),
        input_output_aliases={j: j for j in range(ns + nl)},
        compiler_params=IN_FLIGHT)(*arrays, st["send"], st["recv"], st["fwd_send"], st["fwd_recv"], after)
    return list(outs[ns:ns + nl])


def _sum_slots(land):
    def body(l_ref, o_ref):
        acc = l_ref[0]
        for d in range(1, NDEV):
            acc = acc + l_ref[d]
        o_ref[...] = acc

    vm = pl.BlockSpec(memory_space=pltpu.VMEM)
    return pl.pallas_call(body, name="sum_small", out_shape=_sds(land.shape[1:], F32), in_specs=[vm], out_specs=vm)(land)


def _adam_math(w, g, m, v):
    m2 = ADAM_B1 * m + (1.0 - ADAM_B1) * g
    v2 = ADAM_B2 * v + (1.0 - ADAM_B2) * (g * g)
    delta = -ADAM_LR * ((m2 * ADAM_C1) / (jnp.sqrt(v2 * ADAM_C2) + ADAM_EPS) + ADAM_WD * w)
    return delta, m2, v2


def _adam_layer(land, sel, w, m, v, layer, name, prev=None, tc=512):
    R = land.shape[2]

    def body(l_ref, w_ref, m_ref, v_ref, *rest):
        g_out, d_out, m_out, v_out = rest[-4:]
        g = l_ref[0].astype(F32)
        for d in range(1, NDEV):
            g = g + l_ref[d].astype(F32)
        delta, m2, v2 = _adam_math(w_ref[...], g, m_ref[...], v_ref[...])
        g_out[...] = g
        d_out[...] = delta
        m_out[...] = m2
        v_out[...] = v2

    wspec = pl.BlockSpec((None, R, tc), lambda i: (layer, 0, i))
    in_specs = [pl.BlockSpec((None, NDEV, R, tc), lambda i: (sel, 0, 0, i)), wspec, wspec, wspec]
    args = [land, w, m, v]
    aliases = {}
    if prev is not None:
        in_specs += [HBM_SPEC] * 4
        args += list(prev)
        aliases = {4 + j: j for j in range(4)}
    return pl.pallas_call(
        body, name=name, grid=(D // tc,), in_specs=in_specs, out_specs=[wspec] * 4,
        out_shape=[_sds(w.shape, F32)] * 4, input_output_aliases=aliases, compiler_params=_cparams(1))(*args)


def _adam_stacked(lands, sel, w, m, v, name):
    res = None
    for layer, land in enumerate(lands):
        res = _adam_layer(land, sel, w, m, v, layer, f"{name}{layer}", prev=res)
    return res


def _adam_small(ws, gs, ms, vs):
    n = len(ws)

    def body(*refs):
        w_r, g_r, m_r, v_r = refs[0:n], refs[n:2 * n], refs[2 * n:3 * n], refs[3 * n:4 * n]
        d_o, m_o, v_o = refs[4 * n:5 * n], refs[5 * n:6 * n], refs[6 * n:7 * n]
        for i in range(n):
            delta, m2, v2 = _adam_math(w_r[i][...], g_r[i][...], m_r[i][...], v_r[i][...])
            d_o[i][...] = delta
            m_o[i][...] = m2
            v_o[i][...] = v2

    vm = pl.BlockSpec(memory_space=pltpu.VMEM)
    shapes = [_sds(w.shape, F32) for w in ws]
    outs = pl.pallas_call(body, name="adam_small", in_specs=[vm] * (4 * n), out_specs=[vm] * (3 * n),
                          out_shape=shapes * 3)(*ws, *gs, *ms, *vs)
    return outs[0:n], outs[n:2 * n], outs[2 * n:3 * n]


WEIGHT_NAMES = ("ab_norm_g", "ab_w_in", "sgu_norm_g", "sgu_norm_b", "sgu_w", "sgu_bias", "q_norm_g", "k_norm_g",
                "ab_w_out", "cd_norm_g", "cd_w_in", "conv_c_w", "conv_c_b", "c_ln_g", "c_ln_b", "conv_d_w",
                "cd_w_out", "ffn_norm_g", "ffn_w_gate", "ffn_w_up", "ffn_w_down")
SMALL_2D = (("ab_norm_g", (1, 1024)), ("sgu_norm_g", (1, 512)), ("sgu_norm_b", (1, 512)), ("sgu_w", (512, 128)),
            ("sgu_bias", (4, 128)), ("q_norm_g", (3, 64)), ("k_norm_g", (3, 64)), ("cd_norm_g", (1, 128)),
            ("conv_c_w", (31, 64)), ("conv_c_b", (1, 64)), ("c_ln_g", (1, 64)), ("c_ln_b", (1, 64)),
            ("conv_d_w", (3, 64)), ("ffn_norm_g", (2, 1024)))
SHARD_C = 64


def _pack_rows(parts, rows):
    flat = jnp.concatenate([p.reshape(-1) for p in parts])
    return jnp.pad(flat, (0, rows * 128 - flat.shape[0])).reshape(rows, 128)


def kernel(x, ab_norm_g, ab_w_in, sgu_norm_g, sgu_norm_b, sgu_w, sgu_bias, q_norm_g, k_norm_g, ab_w_out, cd_norm_g, cd_w_in, conv_c_w, conv_c_b, c_ln_g, c_ln_b, conv_d_w, cd_w_out, ffn_norm_g, ffn_w_gate, ffn_w_up, ffn_w_down, loss_target, m_ab_norm_g, m_ab_w_in, m_sgu_norm_g, m_sgu_norm_b, m_sgu_w, m_sgu_bias, m_q_norm_g, m_k_norm_g, m_ab_w_out, m_cd_norm_g, m_cd_w_in, m_conv_c_w, m_conv_c_b, m_c_ln_g, m_c_ln_b, m_conv_d_w, m_cd_w_out, m_ffn_norm_g, m_ffn_w_gate, m_ffn_w_up, m_ffn_w_down, v_ab_norm_g, v_ab_w_in, v_sgu_norm_g, v_sgu_norm_b, v_sgu_w, v_sgu_bias, v_q_norm_g, v_k_norm_g, v_ab_w_out, v_cd_norm_g, v_cd_w_in, v_conv_c_w, v_conv_c_b, v_c_ln_g, v_c_ln_b, v_conv_d_w, v_cd_w_out, v_ffn_norm_g, v_ffn_w_gate, v_ffn_w_up, v_ffn_w_down):
    w = dict(zip(WEIGHT_NAMES, (ab_norm_g, ab_w_in, sgu_norm_g, sgu_norm_b, sgu_w, sgu_bias, q_norm_g, k_norm_g, ab_w_out, cd_norm_g, cd_w_in, conv_c_w, conv_c_b, c_ln_g, c_ln_b, conv_d_w, cd_w_out, ffn_norm_g, ffn_w_gate, ffn_w_up, ffn_w_down)))
    m = dict(zip(WEIGHT_NAMES, (m_ab_norm_g, m_ab_w_in, m_sgu_norm_g, m_sgu_norm_b, m_sgu_w, m_sgu_bias, m_q_norm_g, m_k_norm_g, m_ab_w_out, m_cd_norm_g, m_cd_w_in, m_conv_c_w, m_conv_c_b, m_c_ln_g, m_c_ln_b, m_conv_d_w, m_cd_w_out, m_ffn_norm_g, m_ffn_w_gate, m_ffn_w_up, m_ffn_w_down)))
    v = dict(zip(WEIGHT_NAMES, (v_ab_norm_g, v_ab_w_in, v_sgu_norm_g, v_sgu_norm_b, v_sgu_w, v_sgu_bias, v_q_norm_g, v_k_norm_g, v_ab_w_out, v_cd_norm_g, v_cd_w_in, v_conv_c_w, v_conv_c_b, v_c_ln_g, v_c_ln_b, v_conv_d_w, v_cd_w_out, v_ffn_norm_g, v_ffn_w_gate, v_ffn_w_up, v_ffn_w_down)))
    me = _dev_index(*_my_place())

    small_local = _pack_rows([w["cd_norm_g"], w["conv_c_w"], w["conv_c_b"], w["c_ln_g"], w["c_ln_b"], w["conv_d_w"]], 24)
    o_ab_in, o_small = _gather_first(w["ab_w_in"][0].T.astype(BF16), small_local, me)
    r_ff = DFF // NDEV
    one = lambda a: (lambda S, j: S[a])
    slot = lambda b: (lambda L, s: L[b].at[s])
    slot2 = lambda b, part: (lambda L, s: L[b].at[part, s])
    shard = lambda a: (lambda S: S[a])

    def layer_shards(layer):
        return (w["ffn_w_gate"][layer].T.astype(BF16), w["ffn_w_up"][layer].T.astype(BF16),
                w["ffn_w_down"][layer].astype(BF16))

    def gathered(own):
        return _landing((NDEV,) + own.shape, BF16, [((me,), own)])

    def gathered2(a, b):
        return _landing((2, NDEV) + a.shape, BF16, [((0, me), a), ((1, me), b)])

    ab_out_s = w["ab_w_out"][0].astype(BF16)
    gate0, up0, down0 = layer_shards(0)
    gathers = {1: _gather_start(
        "gather1_start", [ab_out_s, gate0, up0, down0], [gathered(ab_out_s), gathered2(gate0, up0), gathered(down0)],
        [(shard(0), slot(0)), (shard(1), slot2(1, 0)), (shard(2), slot2(1, 1)), (shard(3), slot(2))], dep=o_small)}

    def fetch(stage, after):
        if stage == "attn0":
            cd_in_s, cd_out_s = w["cd_w_in"][0].T.astype(BF16), w["cd_w_out"][0].astype(BF16)
            gate1, up1, down1 = layer_shards(1)
            gathers[2] = _gather_start(
                "gather2_start", [cd_in_s, cd_out_s, gate1, up1, down1],
                [gathered(cd_in_s), gathered(cd_out_s), gathered2(gate1, up1), gathered(down1)],
                [(shard(0), slot(0)), (shard(1), slot(1)), (shard(2), slot2(2, 0)), (shard(3), slot2(2, 1)),
                 (shard(4), slot(3))], dep=after)
            return {"dep_attn1": gathers[2]["token"]}
        if stage == "attn1":
            gathers[1] = _gather_forward("gather1_forward", gathers[1], after)
            return {"dep_attn2": gathers[1]["token"]}
        if stage == "ab_out":
            l_out, l_ffn, l_down = _gather_wait("gather1_wait", gathers[1], after)
            return {"w_ab_out": l_out.reshape(D, D), "wt_ffn_in0": l_ffn.reshape(2 * DFF, D),
                    "w_ffn_down0": l_down.reshape(DFF, D)}
        if stage == "ffn_down0":
            gathers[2] = _gather_forward("gather2_forward", gathers[2], after)
            return {"dep_down0": gathers[2]["token"]}
        if stage == "cd_in":
            l_in, l_out, l_ffn, l_down = _gather_wait("gather2_wait", gathers[2], after)
            return {"wt_cd_in": l_in.reshape(CD_IN, D), "w_cd_out": l_out.reshape(D, D),
                    "wt_ffn_in1": l_ffn.reshape(2 * DFF, D), "w_ffn_down1": l_down.reshape(DFF, D)}
        return {}

    scatters = {}
    rides_with = {"w_ffn_down1": "wt_ffn_in1", "w_cd_out": "wt_cd_in", "w_ffn_down0": "wt_ffn_in0",
                  "w_ab_out": "wt_ab_in"}
    held = {}

    def on_grad(key, arr):
        if key in rides_with:
            held[rides_with[key]] = (key, arr)
            return None
        group = ([held.pop(key)] if key in held else []) + [(key, arr)]
        srcs, lands, items = [], [], []
        for n, (k, a) in enumerate(group):
            if k.startswith("wt_ffn_in"):
                src = a.reshape(2, NDEV, r_ff, D)
                own = lax.dynamic_slice_in_dim(src, me, 1, axis=1)
                lands.append(lax.dynamic_update_slice(lax.empty(src.shape, BF16), own, (0, me, 0, 0)))
                items += [((lambda S, j, n=n: S[n].at[0, j]), slot2(n, 0)), ((lambda S, j, n=n: S[n].at[1, j]), slot2(n, 1))]
            else:
                rows = a.shape[0] // NDEV
                src = a.reshape(NDEV, rows, D)
                own = lax.dynamic_index_in_dim(src, me, 0, keepdims=False)
                lands.append(_landing((1, NDEV, rows, D), BF16, [((0, me), own)]))
                items.append(((lambda S, j, n=n: S[n].at[j]), slot2(n, 0)))
            srcs.append(src)
        st = _exchange_start(f"scatter_{key}_start", srcs, lands, items)
        scatters[key] = (st, [k for k, _ in group])
        return st["token"]

    flat = o_small.reshape(NDEV, 24 * 128)

    def chan(lo, taps):
        return flat[:, lo:lo + taps * SHARD_C].reshape(NDEV, taps, SHARD_C).transpose(1, 0, 2).reshape(taps, 512)

    W = {
        "wt_ab_in": o_ab_in.reshape(AB_IN, D), "dep0": gathers[1]["token"],
        "ab_norm_g": w["ab_norm_g"], "sgu_norm_g": w["sgu_norm_g"], "sgu_norm_b": w["sgu_norm_b"],
        "sgu_w": w["sgu_w"][0], "sgu_bias": w["sgu_bias"][0], "q_norm_g": w["q_norm_g"][0],
        "k_norm_g": w["k_norm_g"][0], "ffn_norm_g": w["ffn_norm_g"],
        "cd_norm_g": flat[:, 0:128].reshape(1, D),
        "conv_c_w32": jnp.pad(chan(128, CONV_C_TAPS), ((0, 1), (0, 0))),
        "conv_c_b": chan(2112, 1), "c_ln_g": chan(2176, 1), "c_ln_b": chan(2240, 1),
        "conv_d_w8": jnp.pad(chan(2304, CONV_D_TAPS), ((0, 8 - CONV_D_TAPS), (0, 0))),
    }

    loss_cols, grad_x, G = _local_step(x[0], loss_target[0], W, fetch, on_grad)
    loss = lax.psum(jnp.sum(loss_cols), ("x", "y", "c"))

    small_parts = [G["ab_norm_g"], G["sgu_norm_g"], G["sgu_norm_b"], G["sgu_w"], G["sgu_bias"], G["q_norm_g"],
                   G["k_norm_g"], G["cd_norm_g"], G["conv_c_w32"][:CONV_C_TAPS], G["conv_c_b"], G["c_ln_g"],
                   G["c_ln_b"], G["conv_d_w8"][:CONV_D_TAPS], G["ffn_norm_g0"], G["ffn_norm_g1"]]
    sizes = [p.size for p in small_parts]
    small_rows = 712
    packed = _pack_rows(small_parts, small_rows)
    small = _exchange_start("small_start", [packed], [_landing((NDEV, small_rows, 128), F32, [((me,), packed)])],
                            [(one(0), slot(0))])
    landed = {}

    def wait_scatters(name, group_keys, after):
        res = _exchange_wait(name, [scatters[gk][0] for gk in group_keys], after)
        for gk, lands in zip(group_keys, res):
            landed.update(zip(scatters[gk][1], lands))

    wait_scatters("scatter_wait_early", ["wt_ffn_in1", "wt_cd_in", "wt_ffn_in0"], small["token"])

    grads, deltas, new_m, new_v = {}, {}, {}, {}
    done = []

    def put(name, res):
        grads[name], deltas[name], new_m[name], new_v[name] = res

    def adam(name, lands, sel, transposed):
        flip = (lambda a: jnp.swapaxes(a, 1, 2)) if transposed else (lambda a: a)
        res = _adam_stacked(lands, sel, flip(w[name]), flip(m[name]), flip(v[name]), f"adam_{name}")
        done.append(res[1])
        put(name, [flip(r) for r in res])

    ffn_in_lands = [landed["wt_ffn_in0"], landed["wt_ffn_in1"]]
    adam("cd_w_in", [landed["wt_cd_in"]], 0, True)
    adam("ffn_w_gate", ffn_in_lands, 0, True)
    adam("ffn_w_up", ffn_in_lands, 1, True)
    adam("cd_w_out", [landed["w_cd_out"]], 0, False)
    adam("ffn_w_down", [landed["w_ffn_down0"], landed["w_ffn_down1"]], 0, False)

    small_land = _exchange_wait("small_wait", [small], list(done))[0][0]
    red = _sum_slots(small_land).reshape(-1)
    offs = [0]
    for s in sizes:
        offs.append(offs[-1] + s)
    seg = [red[offs[i]:offs[i + 1]] for i in range(len(sizes))]

    def own_channels(full, taps):
        return lax.dynamic_slice_in_dim(full.reshape(taps, 512), me * SHARD_C, SHARD_C, axis=1)

    g_small = {
        "ab_norm_g": seg[0].reshape(1, 1024), "sgu_norm_g": seg[1].reshape(1, 512), "sgu_norm_b": seg[2].reshape(1, 512),
        "sgu_w": seg[3].reshape(512, 128), "sgu_bias": seg[4].reshape(4, 128), "q_norm_g": seg[5].reshape(3, 64),
        "k_norm_g": seg[6].reshape(3, 64),
        "cd_norm_g": lax.dynamic_slice_in_dim(seg[7].reshape(1, D), me * (D // NDEV), D // NDEV, axis=1),
        "conv_c_w": own_channels(seg[8], CONV_C_TAPS), "conv_c_b": own_channels(seg[9], 1),
        "c_ln_g": own_channels(seg[10], 1), "c_ln_b": own_channels(seg[11], 1),
        "conv_d_w": own_channels(seg[12], CONV_D_TAPS),
        "ffn_norm_g": jnp.concatenate([seg[13].reshape(1, D), seg[14].reshape(1, D)], axis=0),
    }

    names2d = [n for n, _ in SMALL_2D]
    d_s, m_s, v_s = _adam_small([w[n].reshape(s) for n, s in SMALL_2D], [g_small[n] for n in names2d],
                                [m[n].reshape(s) for n, s in SMALL_2D], [v[n].reshape(s) for n, s in SMALL_2D])
    for i, n in enumerate(names2d):
        shape = w[n].shape
        grads[n], deltas[n] = g_small[n].reshape(shape), d_s[i].reshape(shape)
        new_m[n], new_v[n] = m_s[i].reshape(shape), v_s[i].reshape(shape)

    wait_scatters("scatter_wait_last", ["wt_ab_in"], d_s[0])
    adam("ab_w_out", [landed["w_ab_out"]], 0, False)
    adam("ab_w_in", [landed["wt_ab_in"]], 0, True)

    return (loss, grad_x[None], *[grads[n] for n in WEIGHT_NAMES], *[deltas[n] for n in WEIGHT_NAMES],
            *[new_m[n] for n in WEIGHT_NAMES], *[new_v[n] for n in WEIGHT_NAMES])
```

```python
import functools

import jax
import jax.numpy as jnp
import numpy as np
from jax import lax
from jax.experimental import pallas as pl
from jax.experimental.pallas import tpu as pltpu

F32 = jnp.float32
BF16 = jnp.bfloat16

T = 4096
D = 1024
NDEV = 8
EPS = 1e-6
NEG_INF = -1e30
DFF = 2816
AB_IN = 5632
CD_IN = 2560
HEAD = 64
PAIR = 128
NPAIR = 4
NBACK = 128
DIL_RATES = (1, 4, 16)
ROPE_HALF = 8
ROPE_THETA = 500000.0
CONV_C_TAPS = 31
CONV_D_TAPS = 3
HALO = 32
ATTN_BWD_UNROLL = 4

ADAM_LR = 0.001
ADAM_B1 = 0.9
ADAM_B2 = 0.999
ADAM_EPS = 1e-08
ADAM_WD = 0.01
ADAM_STEP = 10
ADAM_C1 = 1.0 / (1.0 - ADAM_B1 ** ADAM_STEP)
ADAM_C2 = 1.0 / (1.0 - ADAM_B2 ** ADAM_STEP)

VMEM_LIMIT_MB = 48
MESH = pl.DeviceIdType.MESH
HBM_SPEC = pl.BlockSpec(memory_space=pl.ANY)


def _cparams(ngrid, vmem_mb=VMEM_LIMIT_MB):
    return pltpu.CompilerParams(dimension_semantics=("arbitrary",) * ngrid,
                                vmem_limit_bytes=vmem_mb * 1024 * 1024)


def _pick(n, options):
    for o in options:
        if n % o == 0:
            return o
    raise ValueError(f"no tile for {n} in {options}")


def _sds(shape, dtype):
    return jax.ShapeDtypeStruct(shape, dtype)


def _sigmoid(x):
    return 1.0 / (1.0 + jnp.exp(-x))


def _sigmoid_bf16(x):
    return 0.5 * jnp.tanh(0.5 * x) + 0.5


def _gelu(z):
    return 0.5 * z * (1.0 + lax.erf(z * 0.7071067811865476))


def _gelu_grad(z):
    return 0.5 * (1.0 + lax.erf(z * 0.7071067811865476)) + z * jnp.exp(-0.5 * z * z) * 0.3989422804014327


def _mm_nt(a, wt, name, out_dtype=BF16, tm=2048, dep=None):
    M, K = a.shape
    N = wt.shape[0]
    tn = _pick(N, (512, 256))

    def body(a_ref, w_ref, *rest):
        o_ref = rest[-1]
        o_ref[...] = lax.dot_general(a_ref[...], w_ref[...], (((1,), (1,)), ((), ())),
                                     preferred_element_type=F32).astype(o_ref.dtype)

    in_specs = [pl.BlockSpec((tm, K), lambda i, j: (i, 0)), pl.BlockSpec((tn, K), lambda i, j: (j, 0))]
    args = [a, wt]
    if dep is not None:
        in_specs.append(HBM_SPEC)
        args.append(dep)
    return pl.pallas_call(
        body, name=name, grid=(M // tm, N // tn), in_specs=in_specs,
        out_specs=pl.BlockSpec((tm, tn), lambda i, j: (i, j)),
        out_shape=_sds((M, N), out_dtype), compiler_params=_cparams(2))(*args)


EPI_ROWS = 256


def _mm_nn(a, w, name, mode, resid, gain=None, tgt=None, dep=None, tm=512):
    M, K = a.shape
    N = w.shape[1]
    side = gain if mode == "rms" else tgt

    def body(a_ref, w_ref, resid_ref, side_ref, *rest):
        outs, acc = rest[-3 if mode == "rms" else -4:-1], rest[-1]
        i = pl.program_id(0)
        acc[...] = jnp.dot(a_ref[...], w_ref[...], preferred_element_type=F32)

        if mode == "loss":
            @pl.when(i == 0)
            def _():
                outs[2][...] = jnp.zeros_like(outs[2])

        for r0 in range(0, tm, EPI_ROWS):
            rows = slice(r0, r0 + EPI_ROWS)
            v = acc[rows, :] + resid_ref[rows, :]
            if mode == "rms":
                outs[0][rows, :] = v
                r = lax.rsqrt(jnp.mean(v * v, axis=-1, keepdims=True) + EPS)
                outs[1][rows, :] = (v * r * side_ref[...]).astype(BF16)
            else:
                d = v - side_ref[rows, :]
                outs[2][...] += jnp.sum(d * d, axis=0, keepdims=True) * (0.5 / N)
                dy = d * (1.0 / N)
                outs[0][rows, :] = dy
                outs[1][rows, :] = dy.astype(BF16)

    row = pl.BlockSpec((tm, N), lambda i: (i, 0))
    vec = pl.BlockSpec((1, N), lambda i: (0, 0))
    in_specs = [pl.BlockSpec((tm, K), lambda i: (i, 0)),
                pl.BlockSpec((K, N), lambda i: (0, 0), pipeline_mode=pl.Buffered(1)), row,
                vec if mode == "rms" else row]
    args = [a, w, resid, side]
    if dep is not None:
        in_specs.append(HBM_SPEC)
        args.append(dep)
    if mode == "rms":
        out_specs, out_shape = [row, row], [_sds((M, N), F32), _sds((M, N), BF16)]
    else:
        out_specs, out_shape = [row, row, vec], [_sds((M, N), F32), _sds((M, N), BF16), _sds((1, N), F32)]
    return pl.pallas_call(
        body, name=name, grid=(M // tm,), in_specs=in_specs, out_specs=out_specs, out_shape=out_shape,
        scratch_shapes=[pltpu.VMEM((tm, N), F32)], compiler_params=_cparams(1))(*args)


def _mm_dh_rms_bwd(a, w, x, gain, dres, name, dep=None, tm=512):
    parts = a.shape[0] if a.ndim == 3 else 1
    M, Kp = a.shape[-2], a.shape[-1]
    N = w.shape[1]
    nblk = M // tm
    assert nblk % 2 == 0

    def body(a_ref, w_ref, x_ref, g_ref, dres_ref, *rest):
        dx_ref, dxb_ref, dg_ref, acc0, acc1 = rest[-5:]
        i = pl.program_id(0)

        def matmul(acc):
            if parts == 1:
                acc[...] = jnp.dot(a_ref[...], w_ref[...], preferred_element_type=F32)
            else:
                d = jnp.dot(a_ref[0], w_ref[0:Kp, :], preferred_element_type=F32)
                for p in range(1, parts):
                    d = d + jnp.dot(a_ref[p], w_ref[p * Kp:(p + 1) * Kp, :], preferred_element_type=F32)
                acc[...] = d

        def finish(acc):
            for r0 in range(0, tm, EPI_ROWS // 2):
                rows = slice(r0, r0 + EPI_ROWS // 2)
                v = acc[rows, :]
                xf = x_ref[rows, :]
                r = lax.rsqrt(jnp.mean(xf * xf, axis=-1, keepdims=True) + EPS)
                xhat = xf * r
                dg_ref[...] += jnp.sum(v * xhat, axis=0, keepdims=True)
                dxh = v * g_ref[...]
                tot = dres_ref[rows, :] + r * (dxh - xhat * jnp.mean(dxh * xhat, axis=-1, keepdims=True))
                dx_ref[rows, :] = tot
                dxb_ref[rows, :] = tot.astype(BF16)

        @pl.when(i == 0)
        def _():
            dg_ref[...] = jnp.zeros_like(dg_ref)
            matmul(acc0)

        @pl.when((i > 0) & (i < nblk) & (i % 2 == 1))
        def _():
            matmul(acc1)
            finish(acc0)

        @pl.when((i > 0) & (i < nblk) & (i % 2 == 0))
        def _():
            matmul(acc0)
            finish(acc1)

        @pl.when(i == nblk)
        def _():
            finish(acc1)

    last = nblk - 1
    row = pl.BlockSpec((tm, N), lambda i: (jnp.maximum(i - 1, 0), 0))
    vec = pl.BlockSpec((1, N), lambda i: (0, 0))
    if a.ndim == 3:
        a_spec = pl.BlockSpec((parts, tm, Kp), lambda i: (0, jnp.minimum(i, last), 0))
    else:
        a_spec = pl.BlockSpec((tm, Kp), lambda i: (jnp.minimum(i, last), 0))
    w_spec = pl.BlockSpec((parts * Kp, N), lambda i: (0, 0), pipeline_mode=pl.Buffered(1))
    in_specs = [a_spec, w_spec, row, vec, row]
    args = [a, w, x, gain, dres]
    if dep is not None:
        in_specs.append(HBM_SPEC)
        args.append(dep)
    return pl.pallas_call(
        body, name=name, grid=(nblk + 1,), in_specs=in_specs, out_specs=[row, row, vec],
        out_shape=[_sds((M, N), F32), _sds((M, N), BF16), _sds((1, N), F32)],
        scratch_shapes=[pltpu.VMEM((tm, N), F32), pltpu.VMEM((tm, N), F32)], compiler_params=_cparams(1, 56))(*args)


def _mm_tn(a, b, name, out_dtype=BF16, tt=1024):
    parts = a.shape[0] if a.ndim == 3 else 1
    Tt, Mp = a.shape[-2], a.shape[-1]
    N = b.shape[1]
    tn = _pick(Mp, (1408, 1280, 1024, 512))
    jper = Mp // tn
    nt = Tt // tt

    def body(a_ref, b_ref, o_ref, acc):
        t = pl.program_id(1)

        @pl.when(t == 0)
        def _():
            acc[...] = jnp.zeros_like(acc)

        acc[...] += lax.dot_general(a_ref[...], b_ref[...], (((0,), (0,)), ((), ())),
                                    preferred_element_type=F32)

        @pl.when(t == nt - 1)
        def _():
            o_ref[...] = acc[...].astype(o_ref.dtype)

    if a.ndim == 3:
        a_spec = pl.BlockSpec((None, tt, tn), lambda j, t: (j // jper, t, j % jper))
    else:
        a_spec = pl.BlockSpec((tt, tn), lambda j, t: (t, j))
    return pl.pallas_call(
        body, name=name, grid=(parts * jper, nt),
        in_specs=[a_spec, pl.BlockSpec((tt, N), lambda j, t: (t, 0))],
        out_specs=pl.BlockSpec((tn, N), lambda j, t: (j, 0)),
        out_shape=_sds((parts * Mp, N), out_dtype), scratch_shapes=[pltpu.VMEM((tn, N), F32)],
        compiler_params=_cparams(2))(a, b)


def _ffn_in(h, wt_in, name, tm=2048, tn=256):
    nj = DFF // tn

    def body(h_ref, wg_ref, wu_ref, p_ref, act_ref):
        nt = (((1,), (1,)), ((), ()))
        g = lax.dot_general(h_ref[...], wg_ref[...], nt, preferred_element_type=F32).astype(BF16)
        u = lax.dot_general(h_ref[...], wu_ref[...], nt, preferred_element_type=F32).astype(BF16)
        p_ref[0] = g
        p_ref[1] = u
        act_ref[...] = g * _sigmoid_bf16(g) * u

    return pl.pallas_call(
        body, name=name, grid=(T // tm, nj),
        in_specs=[pl.BlockSpec((tm, D), lambda i, j: (i, 0)), pl.BlockSpec((tn, D), lambda i, j: (j, 0)),
                  pl.BlockSpec((tn, D), lambda i, j: (j + nj, 0))],
        out_specs=[pl.BlockSpec((2, tm, tn), lambda i, j: (0, i, j)), pl.BlockSpec((tm, tn), lambda i, j: (i, j))],
        out_shape=[_sds((2, T, DFF), BF16), _sds((T, DFF), BF16)], compiler_params=_cparams(2))(h, wt_in, wt_in)


def _ffn_dact(dyb, w_down, p3, name, tm=2048, tn=256, dep=None):
    def body(dy_ref, w_ref, p_ref, *rest):
        o_ref = rest[-1]
        da = lax.dot_general(dy_ref[...], w_ref[...], (((1,), (1,)), ((), ())),
                             preferred_element_type=F32).astype(BF16)
        g = p_ref[0]
        u = p_ref[1]
        sg = _sigmoid_bf16(g)
        gs = g * sg
        o_ref[0] = (da * u) * (sg + gs * (1.0 - sg))
        o_ref[1] = da * gs

    pspec = pl.BlockSpec((2, tm, tn), lambda i, j: (0, i, j))
    in_specs = [pl.BlockSpec((tm, D), lambda i, j: (i, 0)), pl.BlockSpec((tn, D), lambda i, j: (j, 0)), pspec]
    args = [dyb, w_down, p3]
    if dep is not None:
        in_specs.append(HBM_SPEC)
        args.append(dep)
    return pl.pallas_call(
        body, name=name, grid=(T // tm, DFF // tn), in_specs=in_specs, out_specs=pspec,
        out_shape=_sds((2, T, DFF), BF16), compiler_params=_cparams(2))(*args)


def _rms_fwd(x, g, name, tm=512):
    def body(x_ref, g_ref, h_ref):
        xf = x_ref[...]
        r = lax.rsqrt(jnp.mean(xf * xf, axis=-1, keepdims=True) + EPS)
        h_ref[...] = (xf * r * g_ref[...]).astype(BF16)

    return pl.pallas_call(
        body, name=name, grid=(T // tm,),
        in_specs=[pl.BlockSpec((tm, D), lambda i: (i, 0)), pl.BlockSpec((1, D), lambda i: (0, 0))],
        out_specs=pl.BlockSpec((tm, D), lambda i: (i, 0)),
        out_shape=_sds((T, D), BF16), compiler_params=_cparams(1))(x, g)


def _tril_mask():
    r = lax.broadcasted_iota(jnp.int32, (128, 128), 0)
    c = lax.broadcasted_iota(jnp.int32, (128, 128), 1)
    return r >= c


def _mix_a_fwd(pab, sgu_g, sgu_b, sgu_w, sgu_bias3, tm=512):
    def body(zu_ref, zv_ref, g_ref, b_ref, w_ref, bias_ref, o_ref):
        u = _gelu(zu_ref[...].astype(F32))
        v = _gelu(zv_ref[...].astype(F32))
        mu = jnp.mean(v, axis=-1, keepdims=True)
        vc = v - mu
        rstd = lax.rsqrt(jnp.mean(vc * vc, axis=-1, keepdims=True) + EPS)
        vn = (vc * rstd * g_ref[...] + b_ref[...]).astype(BF16)
        tri = _tril_mask()
        for gi in range(4):
            wg = jnp.where(tri, w_ref[gi], 0.0).astype(BF16)
            bg = bias_ref[gi]
            for c in range(tm // 128):
                rs, cs = slice(c * 128, (c + 1) * 128), slice(gi * 128, (gi + 1) * 128)
                mixed = jnp.dot(wg, vn[rs, cs], preferred_element_type=F32) + bg
                o_ref[rs, cs] = (u[rs, cs] * mixed).astype(BF16)

    half = pl.BlockSpec((tm, 512), lambda i: (i, 0))
    return pl.pallas_call(
        body, name="mix_a_fwd", grid=(T // tm,),
        in_specs=[half, pl.BlockSpec((tm, 512), lambda i: (i, 1)),
                  pl.BlockSpec((1, 512), lambda i: (0, 0)), pl.BlockSpec((1, 512), lambda i: (0, 0)),
                  pl.BlockSpec((4, 128, 128), lambda i: (0, 0, 0)), pl.BlockSpec((4, 128, 1), lambda i: (0, 0, 0))],
        out_specs=half, out_shape=_sds((T, D), BF16), compiler_params=_cparams(1),
    )(pab, pab, sgu_g, sgu_b, sgu_w, sgu_bias3)


def _rope_tables():
    pos = np.arange(T, dtype=np.float32)
    inv_freq = np.float32(ROPE_THETA) ** (-np.arange(ROPE_HALF, dtype=np.float32) * np.float32(2.0 / (2 * ROPE_HALF)))
    ang = (pos[:, None] * inv_freq[None, :]).astype(np.float32)
    cos, sin = np.cos(ang), np.sin(ang)
    z8 = np.zeros((T, ROPE_HALF), np.float32)
    rest = np.zeros((T, HEAD - 2 * ROPE_HALF), np.float32)
    c64 = np.concatenate([cos, cos, rest + 1.0], axis=1)
    s1 = np.concatenate([z8, sin, rest], axis=1)
    s2 = np.concatenate([-sin, z8, rest], axis=1)
    return tuple(jnp.asarray(np.tile(t, (1, 2)).astype(np.float32)) for t in (c64, s1, s2))


def _lo_mask(shape):
    return lax.broadcasted_iota(jnp.int32, shape, 1) < HEAD


def _seg_mean(x, lo):
    s_all = jnp.sum(x, axis=-1, keepdims=True)
    s_lo = jnp.sum(jnp.where(lo, x, 0.0), axis=-1, keepdims=True)
    return jnp.where(lo, s_lo, s_all - s_lo) * (1.0 / HEAD)


def _rope(n, c, s1, s2):
    return n * c + pltpu.roll(n, ROPE_HALF, 1) * s1 + pltpu.roll(n, PAIR - ROPE_HALF, 1) * s2


def _rope_t(dy, c, s1, s2):
    return dy * c - pltpu.roll(dy, PAIR - ROPE_HALF, 1) * s2 - pltpu.roll(dy, ROPE_HALF, 1) * s1


def _prep_fwd(pab, qg, kg, tabs, tm=512):
    def body(p_ref, qg_ref, kg_ref, c_ref, s1_ref, s2_ref, *outs):
        lo = _lo_mask((tm, PAIR))
        c, s1, s2 = c_ref[...], s1_ref[...], s2_ref[...]
        for g in range(3):
            qn_ref, kn_ref, v_ref = outs[3 * g:3 * g + 3]
            for p in range(NPAIR):
                for which, gains, dst in ((0, qg_ref, qn_ref), (1, kg_ref, kn_ref)):
                    col = (2 + 3 * which + g) * 512 + p * PAIR
                    xr = p_ref[:, col:col + PAIR].astype(F32)
                    rinv = lax.rsqrt(_seg_mean(xr * xr, lo) + EPS)
                    outs[9 + 2 * g + which][p] = rinv
                    dst[p] = _rope(xr * rinv * gains[g:g + 1, :], c, s1, s2)
                col = (8 + g) * 512 + p * PAIR
                v_ref[p] = p_ref[:, col:col + PAIR].astype(F32)

    pm = pl.BlockSpec((NPAIR, tm, PAIR), lambda i: (0, i, 0))
    tab = pl.BlockSpec((tm, PAIR), lambda i: (i, 0))
    gain = pl.BlockSpec((3, PAIR), lambda i: (0, 0))
    res = pl.pallas_call(
        body, name="prep_fwd", grid=(T // tm,),
        in_specs=[pl.BlockSpec((tm, AB_IN), lambda i: (i, 0)), gain, gain, tab, tab, tab],
        out_specs=[pm] * 15, out_shape=[_sds((NPAIR, T, PAIR), F32)] * 15,
        compiler_params=_cparams(1))(pab, qg, kg, *tabs)
    return res[0:9], res[9:15]


def _res_index(it, rate):
    window = NBACK * rate
    b = it // rate
    rho = it % rate
    start = b * window + rho
    startp = jnp.maximum(start - window, rho)
    kmin = jnp.where(b > 0, 0, NBACK)
    return start, startp, kmin


def _rows(start, rate):
    if rate == 1:
        return pl.ds(pl.multiple_of(start, NBACK), NBACK)
    return pl.ds(start, NBACK, stride=rate)


def _band():
    qi = lax.broadcasted_iota(jnp.int32, (NBACK, 2 * NBACK), 0)
    kj = lax.broadcasted_iota(jnp.int32, (NBACK, 2 * NBACK), 1)
    dist = qi + NBACK - kj
    return (dist >= 0) & (dist <= NBACK), kj


def _band_bias():
    band, kj = _band()
    both = jnp.concatenate([band, band], axis=0)
    own = jnp.concatenate([band & (kj >= NBACK)] * 2, axis=0)
    return jnp.where(both, 0.0, NEG_INF), jnp.where(own, 0.0, NEG_INF)


def _attn_fwd(qn, kn, v, rate, name, dep=None):
    def body(q_ref, k_ref, v_ref, *rest):
        o_ref, l_ref = rest[-2:]
        lo = _lo_mask((NBACK, PAIR))
        bias_all, bias_first = _band_bias()

        def step(it, carry):
            start, startp, kmin = _res_index(it, rate)
            q = q_ref[_rows(start, rate), :] * (HEAD ** -0.5)
            kcat = jnp.concatenate([k_ref[_rows(startp, rate), :], k_ref[_rows(start, rate), :]], axis=0).astype(BF16)
            vcat = jnp.concatenate([v_ref[_rows(startp, rate), :], v_ref[_rows(start, rate), :]], axis=0).astype(BF16)
            vcat1 = jnp.concatenate([vcat, jnp.ones((2 * NBACK, PAIR), BF16)], axis=1)
            q2 = jnp.concatenate([jnp.where(lo, q, 0.0), jnp.where(lo, 0.0, q)], axis=0).astype(BF16)
            s = lax.dot_general(q2, kcat, (((1,), (1,)), ((), ())), preferred_element_type=F32)
            s = s + jnp.where(kmin == 0, bias_all, bias_first)
            m = jnp.max(s, axis=-1, keepdims=True)
            ol = jnp.dot(jnp.exp(s - m).astype(BF16), vcat1, preferred_element_type=F32)
            o2 = ol[:, 0:PAIR] / ol[:, PAIR:]
            ls = m + jnp.log(ol[:, PAIR:])
            o_ref[_rows(start, rate), :] = jnp.where(lo, o2[0:NBACK], o2[NBACK:])
            l_ref[_rows(start, rate), :] = jnp.where(lo, ls[0:NBACK], ls[NBACK:])
            return carry

        lax.fori_loop(0, T // NBACK, step, 0, unroll=4)

    pm = pl.BlockSpec((None, T, PAIR), lambda p: (p, 0, 0))
    in_specs, args = [pm, pm, pm], [qn, kn, v]
    if dep is not None:
        in_specs.append(HBM_SPEC)
        args.append(dep)
    return pl.pallas_call(
        body, name=name, grid=(NPAIR,), in_specs=in_specs, out_specs=[pm, pm],
        out_shape=[_sds((NPAIR, T, PAIR), F32)] * 2, compiler_params=_cparams(1))(*args)


def _merge_fwd(cat_ab, outs, lses, tm=512):
    def body(cat_in, o0, o1, o2, l0, l1, l2, cat_ref, lse_ref):
        del cat_in
        for p in range(NPAIR):
            a0, a1, a2 = l0[p], l1[p], l2[p]
            m = jnp.maximum(jnp.maximum(a0, a1), a2)
            w0, w1, w2 = jnp.exp(a0 - m), jnp.exp(a1 - m), jnp.exp(a2 - m)
            s = w0 + w1 + w2
            b = (w0 * o0[p] + w1 * o1[p] + w2 * o2[p]) / s
            cat_ref[:, p * PAIR:(p + 1) * PAIR] = b.astype(BF16)
            lse_ref[p] = m + jnp.log(s)

    pm = pl.BlockSpec((NPAIR, tm, PAIR), lambda i: (0, i, 0))
    return pl.pallas_call(
        body, name="merge_fwd", grid=(T // tm,),
        in_specs=[pl.BlockSpec(memory_space=pl.ANY)] + [pm] * 6,
        out_specs=[pl.BlockSpec((tm, 512), lambda i: (i, 1)), pm],
        out_shape=[_sds((T, D), BF16), _sds((NPAIR, T, PAIR), F32)],
        input_output_aliases={0: 0}, compiler_params=_cparams(1))(cat_ab, *outs, *lses)


def _b_pre_bwd(dcat, cat, tm=512):
    def body(db_ref, b_ref, dbp_ref, e_ref):
        lo = _lo_mask((tm, PAIR))
        for p in range(NPAIR):
            db = db_ref[:, p * PAIR:(p + 1) * PAIR].astype(F32)
            b = b_ref[:, p * PAIR:(p + 1) * PAIR].astype(F32)
            dbp_ref[p] = db
            e_ref[p] = _seg_mean(db * b, lo) * float(HEAD)

    pm = pl.BlockSpec((NPAIR, tm, PAIR), lambda i: (0, i, 0))
    right = pl.BlockSpec((tm, 512), lambda i: (i, 1))
    return pl.pallas_call(
        body, name="b_pre_bwd", grid=(T // tm,), in_specs=[right, right], out_specs=[pm, pm],
        out_shape=[_sds((NPAIR, T, PAIR), F32)] * 2, compiler_params=_cparams(1))(dcat, cat)


def _attn_bwd(qn, kn, v, dbp, e, lse, rate, name):
    def body(q_ref, k_ref, v_ref, db_ref, e_ref, lse_ref, dq_ref, dk_ref, dv_ref):
        lo = _lo_mask((NBACK, PAIR))
        bias_all, bias_first = _band_bias()
        scale = HEAD ** -0.5
        nt = (((1,), (1,)), ((), ()))
        tn = (((0,), (0,)), ((), ()))
        window = NBACK * rate
        nblk = T // window

        def one(it, carry):
            dk_carry, dv_carry = carry
            rho = it // nblk
            b = it % nblk
            start = b * window + rho
            rq = _rows(start, rate)
            rp = _rows(jnp.maximum(start - window, rho), rate)
            q = q_ref[rq, :] * scale
            db = db_ref[rq, :]
            ev = e_ref[rq, :]
            ls = lse_ref[rq, :]
            kcat = jnp.concatenate([k_ref[rp, :], k_ref[rq, :]], axis=0).astype(BF16)
            vcat = jnp.concatenate([v_ref[rp, :], v_ref[rq, :]], axis=0).astype(BF16)
            q2 = jnp.concatenate([jnp.where(lo, q, 0.0), jnp.where(lo, 0.0, q)], axis=0).astype(BF16)
            db2 = jnp.concatenate([jnp.where(lo, db, 0.0), jnp.where(lo, 0.0, db)], axis=0).astype(BF16)
            ls2 = jnp.concatenate([ls[:, 0:1], ls[:, HEAD:HEAD + 1]], axis=0)
            ev2 = jnp.concatenate([ev[:, 0:1], ev[:, HEAD:HEAD + 1]], axis=0)
            s = lax.dot_general(q2, kcat, nt, preferred_element_type=F32)
            pt = jnp.exp(s + jnp.where(b > 0, bias_all, bias_first) - ls2)
            dp = lax.dot_general(db2, vcat, nt, preferred_element_type=F32)
            ds = (pt * (dp - ev2)).astype(BF16)
            dq2 = jnp.dot(ds, kcat, preferred_element_type=F32) * scale
            dkc = lax.dot_general(ds, q2, tn, preferred_element_type=F32)
            dvc = lax.dot_general(pt.astype(BF16), db2, tn, preferred_element_type=F32)
            dq_ref[rq, :] = jnp.where(lo, dq2[0:NBACK], dq2[NBACK:])
            dk_ref[rp, :] = dk_carry + dkc[0:NBACK]
            dk_ref[rq, :] = dkc[NBACK:]
            dv_ref[rp, :] = dv_carry + dvc[0:NBACK]
            dv_ref[rq, :] = dvc[NBACK:]
            return dkc[NBACK:], dvc[NBACK:]

        def step(i, carry):
            for u in range(ATTN_BWD_UNROLL):
                carry = one(i * ATTN_BWD_UNROLL + u, carry)
            return carry

        zero = jnp.zeros((NBACK, PAIR), F32)
        lax.fori_loop(0, T // NBACK // ATTN_BWD_UNROLL, step, (zero, zero))

    pm = pl.BlockSpec((None, T, PAIR), lambda p: (p, 0, 0))
    return pl.pallas_call(
        body, name=name, grid=(NPAIR,), in_specs=[pm] * 6, out_specs=[pm] * 3,
        out_shape=[_sds((NPAIR, T, PAIR), F32)] * 3, compiler_params=_cparams(1, 56))(qn, kn, v, dbp, e, lse)


def _ab_in_bwd(pab, dcat, sgu_g, sgu_b, sgu_w, sgu_bias3, qg, kg, tabs, dqkv, rinvs, tm=256):
    def body(p_ref, dcat_ref, g_ref, b_ref, w_ref, bias_ref, qg_ref, kg_ref, c_ref, s1_ref, s2_ref, *rest):
        dq_refs, rinv_refs = rest[0:9], rest[9:15]
        o_ref, dwm_ref, dbias_ref, dsg_ref, dsb_ref, dgain_ref = rest[15:]
        i = pl.program_id(0)

        @pl.when(i == 0)
        def _():
            dwm_ref[...] = jnp.zeros_like(dwm_ref)
            dbias_ref[...] = jnp.zeros_like(dbias_ref)
            dsg_ref[...] = jnp.zeros_like(dsg_ref)
            dsb_ref[...] = jnp.zeros_like(dsb_ref)
            dgain_ref[...] = jnp.zeros_like(dgain_ref)

        zu = p_ref[:, 0:512].astype(F32)
        zv = p_ref[:, 512:1024].astype(F32)
        u = _gelu(zu)
        v = _gelu(zv)
        mu = jnp.mean(v, axis=-1, keepdims=True)
        vc = v - mu
        rstd = lax.rsqrt(jnp.mean(vc * vc, axis=-1, keepdims=True) + EPS)
        xhat = vc * rstd
        vn = (xhat * g_ref[...] + b_ref[...]).astype(BF16)
        da = dcat_ref[...].astype(F32)
        tri = _tril_mask()
        du_parts = [[None] * 4 for _ in range(tm // 128)]
        dvn_parts = [[None] * 4 for _ in range(tm // 128)]
        for gi in range(4):
            wg = jnp.where(tri, w_ref[gi], 0.0).astype(BF16)
            bg = bias_ref[gi]
            for c in range(tm // 128):
                rs, cs = slice(c * 128, (c + 1) * 128), slice(gi * 128, (gi + 1) * 128)
                vblk = vn[rs, cs]
                mixed = jnp.dot(wg, vblk, preferred_element_type=F32) + bg
                dab = da[rs, cs]
                du_parts[c][gi] = dab * mixed
                dmixed = dab * u[rs, cs]
                dmb = dmixed.astype(BF16)
                dvn_parts[c][gi] = lax.dot_general(wg, dmb, (((0,), (0,)), ((), ())), preferred_element_type=F32)
                dwm = lax.dot_general(dmb, vblk, (((1,), (1,)), ((), ())), preferred_element_type=F32)
                dwm_ref[gi] += jnp.where(tri, dwm, 0.0)
                dbias_ref[gi] += dmixed
        du = jnp.concatenate([jnp.concatenate(r, axis=1) for r in du_parts], axis=0)
        dvn = jnp.concatenate([jnp.concatenate(r, axis=1) for r in dvn_parts], axis=0)
        dsg_ref[...] += jnp.sum(dvn * xhat, axis=0, keepdims=True)
        dsb_ref[...] += jnp.sum(dvn, axis=0, keepdims=True)
        dxh = dvn * g_ref[...]
        dv = rstd * (dxh - jnp.mean(dxh, axis=-1, keepdims=True)
                     - xhat * jnp.mean(dxh * xhat, axis=-1, keepdims=True))
        o_ref[:, 0:512] = (du * _gelu_grad(zu)).astype(BF16)
        o_ref[:, 512:1024] = (dv * _gelu_grad(zv)).astype(BF16)

        lo = _lo_mask((tm, PAIR))
        c, s1, s2 = c_ref[...], s1_ref[...], s2_ref[...]
        for g in range(3):
            dq_ref, dk_ref, dv_ref = dq_refs[3 * g:3 * g + 3]
            for p in range(NPAIR):
                for which, gains, src in ((0, qg_ref, dq_ref), (1, kg_ref, dk_ref)):
                    col = (2 + 3 * which + g) * 512 + p * PAIR
                    xr = p_ref[:, col:col + PAIR].astype(F32)
                    rinv = rinv_refs[2 * g + which][p]
                    xh = xr * rinv
                    dn = _rope_t(src[p], c, s1, s2)
                    row = 2 * g + which
                    dgain_ref[row:row + 1, :] += jnp.sum(dn * xh, axis=0, keepdims=True)
                    dxh2 = dn * gains[g:g + 1, :]
                    dx = rinv * (dxh2 - xh * _seg_mean(dxh2 * xh, lo))
                    o_ref[:, col:col + PAIR] = dx.astype(BF16)
                col = (8 + g) * 512 + p * PAIR
                o_ref[:, col:col + PAIR] = dv_ref[p].astype(BF16)

    pm = pl.BlockSpec((NPAIR, tm, PAIR), lambda i: (0, i, 0))
    tab = pl.BlockSpec((tm, PAIR), lambda i: (i, 0))
    gain = pl.BlockSpec((3, PAIR), lambda i: (0, 0))
    vec = pl.BlockSpec((1, 512), lambda i: (0, 0))
    full = pl.BlockSpec((tm, AB_IN), lambda i: (i, 0))
    w4 = pl.BlockSpec((4, 128, 128), lambda i: (0, 0, 0))
    return pl.pallas_call(
        body, name="ab_in_bwd", grid=(T // tm,),
        in_specs=[full, pl.BlockSpec((tm, 512), lambda i: (i, 0)), vec, vec, w4,
                  pl.BlockSpec((4, 128, 1), lambda i: (0, 0, 0)), gain, gain, tab, tab, tab] + [pm] * 15,
        out_specs=[full, w4, w4, vec, vec, pl.BlockSpec((8, PAIR), lambda i: (0, 0))],
        out_shape=[_sds((T, AB_IN), BF16), _sds((4, 128, 128), F32), _sds((4, 128, 128), F32),
                   _sds((1, 512), F32), _sds((1, 512), F32), _sds((8, PAIR), F32)],
        compiler_params=_cparams(1))(pab, dcat, sgu_g, sgu_b, sgu_w, sgu_bias3, qg, kg, *tabs, *dqkv, *rinvs)


def _ln_stats(x):
    mu = jnp.mean(x, axis=-1, keepdims=True)
    xc = x - mu
    rstd = lax.rsqrt(jnp.mean(xc * xc, axis=-1, keepdims=True) + EPS)
    return xc * rstd, rstd


CONV_RC = 64


def _shifted_copies(src, dst, tm):
    dst[0] = src[...]
    for b in range(1, 8):
        dst[b, 0:tm + HALO - 8, :] = src[pl.ds(b, tm + HALO - 8), :]


def _cd_fwd(pcd, cw, cb, lg, lb, dw, tm=512):
    per = tm // HALO

    def body(p_ref, h_ref, cw_ref, cb_ref, lg_ref, lb_ref, dw_ref, cat_ref, c0_ref, c1_ref, dd_ref, y_ref,
             buf, buf2, sb):
        i = pl.program_id(0)
        live = jnp.where(i > 0, 1.0, 0.0)
        a = p_ref[:, 0:512].astype(F32)
        gt = p_ref[:, 512:1024].astype(F32)
        gb = p_ref[:, 1024:1536].astype(F32)
        gc = p_ref[:, 1536:2048].astype(F32)
        hv = p_ref[:, 2048:2560].astype(F32)
        c0 = a * _sigmoid(gt)
        dd = gc * hv
        buf[0:HALO, :] = h_ref[:, 0:512].astype(F32) * _sigmoid(h_ref[:, 512:1024].astype(F32)) * live
        buf[HALO:, :] = c0
        buf2[0:HALO, :] = h_ref[:, 1536:2048].astype(F32) * h_ref[:, 2048:2560].astype(F32) * live
        buf2[HALO:, :] = dd
        c0_ref[...] = c0.astype(BF16)
        dd_ref[...] = dd.astype(BF16)
        _shifted_copies(buf, sb, tm)

        def conv_rows(r, carry):
            base = pl.multiple_of(r * CONV_RC, CONV_RC)
            for c in range(4):
                lanes = slice(c * 128, (c + 1) * 128)
                acc = jnp.broadcast_to(cb_ref[:, lanes], (CONV_RC, 128))
                for j in range(CONV_C_TAPS):
                    a8, b8 = divmod(HALO - (CONV_C_TAPS - 1) + j, 8)
                    acc = acc + cw_ref[j:j + 1, lanes] * sb[b8, pl.ds(base + 8 * a8, CONV_RC), lanes]
                c1_ref[pl.ds(base, CONV_RC), lanes] = acc
            return carry

        lax.fori_loop(0, tm // CONV_RC, conv_rows, 0)
        xhat, _ = _ln_stats(c1_ref[...])
        c2 = xhat * lg_ref[...] + lb_ref[...]
        y = jnp.zeros((tm, 512), F32)
        for j in range(CONV_D_TAPS):
            y = y + dw_ref[j:j + 1, :] * buf2[pl.ds(HALO - (CONV_D_TAPS - 1) + j, tm), :]
        cat_ref[:, 0:512] = (c2 * _sigmoid(c2)).astype(BF16)
        cat_ref[:, 512:1024] = (gb * y).astype(BF16)
        y_ref[...] = y.astype(BF16)

    half = pl.BlockSpec((tm, 512), lambda i: (i, 0))
    vec = pl.BlockSpec((1, 512), lambda i: (0, 0))
    return pl.pallas_call(
        body, name="cd_fwd", grid=(T // tm,),
        in_specs=[pl.BlockSpec((tm, CD_IN), lambda i: (i, 0)),
                  pl.BlockSpec((HALO, CD_IN), lambda i: (jnp.maximum(i * per - 1, 0), 0)),
                  pl.BlockSpec((32, 512), lambda i: (0, 0)), vec, vec, vec, pl.BlockSpec((8, 512), lambda i: (0, 0))],
        out_specs=[pl.BlockSpec((tm, D), lambda i: (i, 0)), half, half, half, half],
        out_shape=[_sds((T, D), BF16), _sds((T, 512), BF16), _sds((T, 512), F32), _sds((T, 512), BF16),
                   _sds((T, 512), BF16)],
        scratch_shapes=[pltpu.VMEM((HALO + tm, 512), F32), pltpu.VMEM((HALO + tm, 512), F32),
                        pltpu.VMEM((8, HALO + tm, 512), F32)],
        compiler_params=_cparams(1))(pcd, pcd, cw, cb, lg, lb, dw)


def _cd_bwd_pw(dcat, c1, pcd, y, lg, lb, tm=512):
    def body(dcat_ref, c1_ref, gb_ref, y_ref, lg_ref, lb_ref, dc1_ref, dy3_ref, dgb_ref, dlg_ref, dlb_ref, dcb_ref):
        i = pl.program_id(0)

        @pl.when(i == 0)
        def _():
            dlg_ref[...] = jnp.zeros_like(dlg_ref)
            dlb_ref[...] = jnp.zeros_like(dlb_ref)
            dcb_ref[...] = jnp.zeros_like(dcb_ref)

        dc = dcat_ref[:, 0:512].astype(F32)
        ddo = dcat_ref[:, 512:1024].astype(F32)
        xhat, rstd = _ln_stats(c1_ref[...])
        c2 = xhat * lg_ref[...] + lb_ref[...]
        sg = _sigmoid(c2)
        dc2 = dc * sg * (1.0 + c2 * (1.0 - sg))
        dlg_ref[...] += jnp.sum(dc2 * xhat, axis=0, keepdims=True)
        dlb_ref[...] += jnp.sum(dc2, axis=0, keepdims=True)
        dxh = dc2 * lg_ref[...]
        dc1 = rstd * (dxh - jnp.mean(dxh, axis=-1, keepdims=True)
                      - xhat * jnp.mean(dxh * xhat, axis=-1, keepdims=True))
        dcb_ref[...] += jnp.sum(dc1, axis=0, keepdims=True)
        dc1_ref[...] = dc1
        dgb_ref[...] = (ddo * y_ref[...].astype(F32)).astype(BF16)
        dy3_ref[...] = ddo * gb_ref[...].astype(F32)

    half = pl.BlockSpec((tm, 512), lambda i: (i, 0))
    vec = pl.BlockSpec((1, 512), lambda i: (0, 0))
    return pl.pallas_call(
        body, name="cd_bwd_pw", grid=(T // tm,),
        in_specs=[pl.BlockSpec((tm, D), lambda i: (i, 0)), half, pl.BlockSpec((tm, 512), lambda i: (i, 2)), half,
                  vec, vec],
        out_specs=[half, half, half, vec, vec, vec],
        out_shape=[_sds((T, 512), F32), _sds((T, 512), F32), _sds((T, 512), BF16),
                   _sds((1, 512), F32), _sds((1, 512), F32), _sds((1, 512), F32)],
        compiler_params=_cparams(1))(dcat, c1, pcd, y, lg, lb)


def _cd_bwd_conv(pcd, dc1, dy3, c0, dd, dgb, cw, dw, tm=256):
    per = tm // HALO
    nblk = T // tm
    last32 = T // HALO - 1

    def body(p_ref, dc1_ref, dc1n_ref, dy3_ref, dy3n_ref, c0_ref, c0p_ref, dd_ref, ddp_ref, dgb_ref, cw_ref, dw_ref,
             o_ref, dcw_ref, ddw_ref, dbuf, cbuf, d3buf, ddbuf, sd, sc, dc0_buf):
        i = pl.program_id(0)
        has_prev = jnp.where(i > 0, 1.0, 0.0)
        has_next = jnp.where(i < nblk - 1, 1.0, 0.0)

        @pl.when(i == 0)
        def _():
            dcw_ref[...] = jnp.zeros_like(dcw_ref)
            ddw_ref[...] = jnp.zeros_like(ddw_ref)

        dc1 = dc1_ref[...]
        dy3 = dy3_ref[...]
        dbuf[0:tm, :] = dc1
        dbuf[tm:, :] = dc1n_ref[...] * has_next
        d3buf[0:tm, :] = dy3
        d3buf[tm:, :] = dy3n_ref[...] * has_next
        cbuf[0:HALO, :] = c0p_ref[...].astype(F32) * has_prev
        cbuf[HALO:, :] = c0_ref[...].astype(F32)
        ddbuf[0:HALO, :] = ddp_ref[...].astype(F32) * has_prev
        ddbuf[HALO:, :] = dd_ref[...].astype(F32)

        _shifted_copies(dbuf, sd, tm)
        _shifted_copies(cbuf, sc, tm)
        n_tiles = tm // CONV_RC

        def dc0_rows(r, carry):
            base = pl.multiple_of(r * CONV_RC, CONV_RC)
            for c in range(4):
                lanes = slice(c * 128, (c + 1) * 128)
                acc = jnp.zeros((CONV_RC, 128), F32)
                for j in range(CONV_C_TAPS):
                    a8, b8 = divmod(CONV_C_TAPS - 1 - j, 8)
                    acc = acc + cw_ref[j:j + 1, lanes] * sd[b8, pl.ds(base + 8 * a8, CONV_RC), lanes]
                dc0_buf[pl.ds(base, CONV_RC), lanes] = acc
            return carry

        lax.fori_loop(0, n_tiles, dc0_rows, 0)

        for c in range(4):
            lanes = slice(c * 128, (c + 1) * 128)
            for j0 in range(0, CONV_C_TAPS, 8):
                taps = list(range(j0, min(j0 + 8, CONV_C_TAPS)))

                def dw_rows(r, accs, lanes=lanes, taps=taps):
                    base = pl.multiple_of(r * CONV_RC, CONV_RC)
                    d = dbuf[pl.ds(base, CONV_RC), lanes]
                    out = []
                    for acc, j in zip(accs, taps):
                        a8, b8 = divmod(HALO - (CONV_C_TAPS - 1) + j, 8)
                        prod = d * sc[b8, pl.ds(base + 8 * a8, CONV_RC), lanes]
                        out.append(acc + jnp.sum(prod.reshape(CONV_RC // 8, 8, 128), axis=0))
                    return tuple(out)

                accs = lax.fori_loop(0, n_tiles, dw_rows, tuple(jnp.zeros((8, 128), F32) for _ in taps))
                for acc, j in zip(accs, taps):
                    dcw_ref[j:j + 1, lanes] += jnp.sum(acc, axis=0, keepdims=True)

        dc0 = dc0_buf[...]
        ddd = jnp.zeros((tm, 512), F32)
        for j in range(CONV_D_TAPS):
            ddd = ddd + dw_ref[j:j + 1, :] * d3buf[pl.ds(CONV_D_TAPS - 1 - j, tm), :]
            ddw_ref[j:j + 1, :] += jnp.sum(dy3 * ddbuf[pl.ds(HALO - (CONV_D_TAPS - 1) + j, tm), :], axis=0, keepdims=True)

        a = p_ref[:, 0:512].astype(F32)
        gt = p_ref[:, 512:1024].astype(F32)
        gc = p_ref[:, 1536:2048].astype(F32)
        hv = p_ref[:, 2048:2560].astype(F32)
        sg = _sigmoid(gt)
        o_ref[:, 0:512] = (dc0 * sg).astype(BF16)
        o_ref[:, 512:1024] = (dc0 * a * sg * (1.0 - sg)).astype(BF16)
        o_ref[:, 1024:1536] = dgb_ref[...]
        o_ref[:, 1536:2048] = (ddd * hv).astype(BF16)
        o_ref[:, 2048:2560] = (ddd * gc).astype(BF16)

    half = pl.BlockSpec((tm, 512), lambda i: (i, 0))
    nxt = pl.BlockSpec((HALO, 512), lambda i: (jnp.minimum((i + 1) * per, last32), 0))
    prv = pl.BlockSpec((HALO, 512), lambda i: (jnp.maximum(i * per - 1, 0), 0))
    full = pl.BlockSpec((tm, CD_IN), lambda i: (i, 0))
    return pl.pallas_call(
        body, name="cd_bwd_conv", grid=(nblk,),
        in_specs=[full, half, nxt, half, nxt, half, prv, half, prv, half,
                  pl.BlockSpec((32, 512), lambda i: (0, 0)), pl.BlockSpec((8, 512), lambda i: (0, 0))],
        out_specs=[full, pl.BlockSpec((32, 512), lambda i: (0, 0)), pl.BlockSpec((8, 512), lambda i: (0, 0))],
        out_shape=[_sds((T, CD_IN), BF16), _sds((32, 512), F32), _sds((8, 512), F32)],
        scratch_shapes=[pltpu.VMEM((tm + HALO, 512), F32), pltpu.VMEM((HALO + tm, 512), F32),
                        pltpu.VMEM((tm + HALO, 512), F32), pltpu.VMEM((HALO + tm, 512), F32),
                        pltpu.VMEM((8, tm + HALO, 512), F32), pltpu.VMEM((8, HALO + tm, 512), F32),
                        pltpu.VMEM((tm, 512), F32)],
        compiler_params=_cparams(1))(pcd, dc1, dc1, dy3, dy3, c0, c0, dd, dd, dgb, cw, dw)


def _local_step(x, tgt, W, fetch=None, on_grad=None):
    W = dict(W)
    if fetch is None:
        fetch = lambda stage, after: {}
    if on_grad is None:
        on_grad = lambda key, arr: None
    tabs = _rope_tables()
    qg = jnp.tile(W["q_norm_g"], (1, 2))
    kg = jnp.tile(W["k_norm_g"], (1, 2))
    bias3 = W["sgu_bias"].reshape(4, 128, 1)
    G = {}

    h0 = _rms_fwd(x, W["ab_norm_g"], "rms_fwd_ab")
    pab = _mm_nt(h0, W["wt_ab_in"], "mm_ab_in", dep=W.get("dep0"))
    cat_ab = _mix_a_fwd(pab, W["sgu_norm_g"], W["sgu_norm_b"], W["sgu_w"], bias3)
    qkv, rinvs = _prep_fwd(pab, qg, kg, tabs)
    outs, lses = [], []
    for g, rate in enumerate(DIL_RATES):
        o, l = _attn_fwd(qkv[3 * g], qkv[3 * g + 1], qkv[3 * g + 2], rate, f"attn_fwd_{g}", dep=W.get(f"dep_attn{g}"))
        outs.append(o)
        lses.append(l)
        W.update(fetch(f"attn{g}", o))
    cat_ab, lse = _merge_fwd(cat_ab, outs, lses)
    W.update(fetch("ab_out", lse))
    x1, h1 = _mm_nn(cat_ab, W["w_ab_out"], "mm_ab_out", mode="rms", resid=x, gain=W["ffn_norm_g"][0:1])
    pf0, act0 = _ffn_in(h1, W["wt_ffn_in0"], "ffn_in0")
    W.update(fetch("ffn_down0", act0))
    x2, h2 = _mm_nn(act0, W["w_ffn_down0"], "mm_ffn_down0", mode="rms", resid=x1, gain=W["cd_norm_g"],
                    dep=W.get("dep_down0"))
    W.update(fetch("cd_in", h2))
    pcd = _mm_nt(h2, W["wt_cd_in"], "mm_cd_in")
    cat_cd, c0, c1, dd, yv = _cd_fwd(pcd, W["conv_c_w32"], W["conv_c_b"], W["c_ln_g"], W["c_ln_b"], W["conv_d_w8"])
    x3, h3 = _mm_nn(cat_cd, W["w_cd_out"], "mm_cd_out", mode="rms", resid=x2, gain=W["ffn_norm_g"][1:2])
    pf1, act1 = _ffn_in(h3, W["wt_ffn_in1"], "ffn_in1")
    dy, dyb, loss_cols = _mm_nn(act1, W["w_ffn_down1"], "mm_ffn_down1", mode="loss", resid=x3, tgt=tgt)

    def ffn_bwd(xin, h, pf, act, dres, dresb, layer):
        G[f"w_ffn_down{layer}"] = _mm_tn(act, dresb, f"mm_g_ffn_down{layer}")
        dep = on_grad(f"w_ffn_down{layer}", G[f"w_ffn_down{layer}"])
        dpf = _ffn_dact(dresb, W[f"w_ffn_down{layer}"], pf, f"ffn_dact{layer}", dep=dep)
        G[f"wt_ffn_in{layer}"] = _mm_tn(dpf, h, f"mm_g_ffn_in{layer}")
        dep = on_grad(f"wt_ffn_in{layer}", G[f"wt_ffn_in{layer}"])
        dx, dxb, G[f"ffn_norm_g{layer}"] = _mm_dh_rms_bwd(
            dpf, W[f"wt_ffn_in{layer}"], xin, W["ffn_norm_g"][layer:layer + 1], dres, f"mm_d_h_ffn{layer}", dep=dep)
        return dx, dxb

    dx3, dx3b = ffn_bwd(x3, h3, pf1, act1, dy, dyb, 1)

    G["w_cd_out"] = _mm_tn(cat_cd, dx3b, "mm_g_cd_out")
    dep = on_grad("w_cd_out", G["w_cd_out"])
    dcat_cd = _mm_nt(dx3b, W["w_cd_out"], "mm_d_cat_cd", dep=dep)
    dc1, dy3, dgb, G["c_ln_g"], G["c_ln_b"], G["conv_c_b"] = _cd_bwd_pw(dcat_cd, c1, pcd, yv, W["c_ln_g"], W["c_ln_b"])
    dpcd, G["conv_c_w32"], G["conv_d_w8"] = _cd_bwd_conv(pcd, dc1, dy3, c0, dd, dgb, W["conv_c_w32"], W["conv_d_w8"])
    G["wt_cd_in"] = _mm_tn(dpcd, h2, "mm_g_cd_in")
    dep = on_grad("wt_cd_in", G["wt_cd_in"])
    dx2, dx2b, G["cd_norm_g"] = _mm_dh_rms_bwd(dpcd, W["wt_cd_in"], x2, W["cd_norm_g"], dx3, "mm_d_h_cd", dep=dep)

    dx1, dx1b = ffn_bwd(x1, h1, pf0, act0, dx2, dx2b, 0)

    G["w_ab_out"] = _mm_tn(cat_ab, dx1b, "mm_g_ab_out")
    dep = on_grad("w_ab_out", G["w_ab_out"])
    dcat_ab = _mm_nt(dx1b, W["w_ab_out"], "mm_d_cat_ab", dep=dep)
    dbp, e = _b_pre_bwd(dcat_ab, cat_ab)
    dqkv = []
    for g, rate in enumerate(DIL_RATES):
        dqkv += _attn_bwd(qkv[3 * g], qkv[3 * g + 1], qkv[3 * g + 2], dbp, e, lse, rate, f"attn_bwd_{g}")
    dpab, G["sgu_w"], dbias_part, G["sgu_norm_g"], G["sgu_norm_b"], dgain = _ab_in_bwd(
        pab, dcat_ab, W["sgu_norm_g"], W["sgu_norm_b"], W["sgu_w"], bias3, qg, kg, tabs, dqkv, rinvs)
    G["sgu_bias"] = jnp.sum(dbias_part, axis=-1)
    dgain = dgain[0:6, 0:HEAD] + dgain[0:6, HEAD:PAIR]
    G["q_norm_g"] = dgain[0::2]
    G["k_norm_g"] = dgain[1::2]
    G["wt_ab_in"] = _mm_tn(dpab, h0, "mm_g_ab_in")
    dep = on_grad("wt_ab_in", G["wt_ab_in"])
    grad_x, _, G["ab_norm_g"] = _mm_dh_rms_bwd(dpab, W["wt_ab_in"], x, W["ab_norm_g"], dx1, "mm_d_h_ab", dep=dep)
    return loss_cols, grad_x, G


def _my_place():
    return lax.axis_index("x"), lax.axis_index("y"), lax.axis_index("c")


def _dev_index(px, py, pc):
    return 4 * px + 2 * py + pc


def _flip(place, k):
    x, y, c = place
    return (1 - x if k & 4 else x, 1 - y if k & 2 else y, 1 - c if k & 1 else c)


def _landing(shape, dtype, own):
    buf = lax.empty(shape, dtype)
    for lead, part in own:
        buf = lax.dynamic_update_slice(buf, part.reshape((1,) * len(lead) + part.shape),
                                       tuple(lead) + (0,) * part.ndim)
    return buf


def _gather_first(ab_in_t, small, me_index):
    lands = [_landing((NDEV,) + ab_in_t.shape, BF16, [((me_index,), ab_in_t)]),
             _landing((NDEV,) + small.shape, F32, [((me_index,), small)])]
    n_items = 2

    def body(ab_in_r, small_r, l_ab_in, l_small, o_ab_in, o_small, send_sems, recv_sems):
        del l_ab_in, l_small
        x, y, c = _my_place()
        me = (x, y, c)
        sib = (x, y, 1 - c)
        chips = [(1 - x, y), (x, 1 - y), (1 - x, 1 - y)]
        items = [(ab_in_r, lambda d: o_ab_in.at[d]), (small_r, lambda d: o_small.at[d])]

        def rcopy(it, k, src, dst, to):
            return pltpu.make_async_remote_copy(src_ref=src, dst_ref=dst, send_sem=send_sems.at[it, k],
                                                recv_sem=recv_sems.at[it, k], device_id=to, device_id_type=MESH)

        started = []
        for it, (src, dst) in enumerate(items):
            mine = dst(_dev_index(*me))
            first = [rcopy(it, 0, src, mine, sib)]
            first += [rcopy(it, 1 + j, src, mine, (*chip, c)) for j, chip in enumerate(chips)]
            for cp in first:
                cp.start()
            started += first
        for it, (src, dst) in enumerate(items):
            for j, chip in enumerate(chips):
                blk = dst(_dev_index(*chip, c))
                rcopy(it, 1 + j, blk, blk, me).wait_recv()
                fwd = rcopy(it, 4 + j, blk, blk, sib)
                fwd.start()
                started.append(fwd)
        for it, (src, dst) in enumerate(items):
            blk = dst(_dev_index(x, y, 1 - c))
            rcopy(it, 0, blk, blk, me).wait_recv()
            for j, chip in enumerate(chips):
                blk = dst(_dev_index(*chip, 1 - c))
                rcopy(it, 4 + j, blk, blk, me).wait_recv()
        for cp in started:
            cp.wait_send()

    return pl.pallas_call(
        body, name="gather_first", in_specs=[HBM_SPEC] * 4, out_specs=[HBM_SPEC] * 2,
        out_shape=[_sds(a.shape, a.dtype) for a in lands], input_output_aliases={2: 0, 3: 1},
        scratch_shapes=[pltpu.SemaphoreType.DMA((n_items, 7)), pltpu.SemaphoreType.DMA((n_items, 7))],
    )(ab_in_t, small, *lands)


HBM_ONLY = pl.BlockSpec(memory_space=pltpu.HBM)
SEM_SPEC = pl.BlockSpec(memory_space=pltpu.SEMAPHORE)
IN_FLIGHT = pltpu.CompilerParams(has_side_effects=pltpu.SideEffectType.DATAFLOW_SIDE_EFFECTING)


def _in_hbm(a):
    return pltpu.with_memory_space_constraint(a, pltpu.HBM)


def _exchange_start(name, srcs, lands, items, dep=None):
    ns, nl, ni = len(srcs), len(lands), len(items)

    def body(*refs):
        S, L = refs[0:ns], refs[ns:ns + nl]
        first_out = ns + nl + (0 if dep is None else 1)
        send_sems, recv_sems, token = refs[first_out], refs[first_out + 1], refs[-1]
        me = _my_place()
        mi = _dev_index(*me)
        for i, (src, dst) in enumerate(items):
            for k in range(1, NDEV):
                peer = _flip(me, k)
                pltpu.make_async_remote_copy(
                    src_ref=src(S, _dev_index(*peer)), dst_ref=dst(L, mi), send_sem=send_sems.at[7 * i + k - 1],
                    recv_sem=recv_sems.at[7 * i + k - 1], device_id=peer, device_id_type=MESH).start()
        token[...] = jnp.zeros_like(token)

    thru = [pltpu.HBM(a.shape, a.dtype) for a in list(srcs) + list(lands)]
    args = [_in_hbm(a) for a in srcs] + [_in_hbm(a) for a in lands]
    in_specs = [HBM_ONLY] * (ns + nl)
    if dep is not None:
        args.append(dep)
        in_specs.append(HBM_SPEC)
    outs = pl.pallas_call(
        body, name=name, in_specs=in_specs,
        out_shape=(pltpu.SemaphoreType.DMA((7 * ni,)), pltpu.SemaphoreType.DMA((7 * ni,)), *thru, _sds((8, 128), F32)),
        out_specs=(SEM_SPEC, SEM_SPEC, *[HBM_ONLY] * (ns + nl), pl.BlockSpec(memory_space=pltpu.VMEM)),
        input_output_aliases={j: 2 + j for j in range(ns + nl)}, compiler_params=IN_FLIGHT)(*args)
    return dict(send=outs[0], recv=outs[1], srcs=list(outs[2:2 + ns]), lands=list(outs[2 + ns:2 + ns + nl]),
                token=outs[-1], items=items)


def _exchange_wait(name, states, after):
    after = list(after) if isinstance(after, (list, tuple)) else [after]
    counts = [(len(st["srcs"]), len(st["lands"]), len(st["items"])) for st in states]
    n_arrays = sum(c[0] + c[1] for c in counts)

    def body(*refs):
        me = _my_place()
        mi = _dev_index(*me)
        pos = 0
        sem_pos = n_arrays
        for st, (ns, nl, ni) in zip(states, counts):
            S, L = refs[pos:pos + ns], refs[pos + ns:pos + ns + nl]
            send_sems, recv_sems = refs[sem_pos], refs[sem_pos + 1]
            pos += ns + nl
            sem_pos += 2
            for i, (src, dst) in enumerate(st["items"]):
                for k in range(1, NDEV):
                    cp = pltpu.make_async_remote_copy(
                        src_ref=src(S, mi), dst_ref=dst(L, mi), send_sem=send_sems.at[7 * i + k - 1],
                        recv_sem=recv_sems.at[7 * i + k - 1], device_id=me, device_id_type=MESH)
                    cp.wait_send()
                    cp.wait_recv()

    arrays, sems = [], []
    for st in states:
        arrays += st["srcs"] + st["lands"]
        sems += [st["send"], st["recv"]]
    outs = pl.pallas_call(
        body, name=name, in_specs=[HBM_ONLY] * n_arrays + [SEM_SPEC] * len(sems) + [HBM_SPEC] * len(after),
        out_shape=tuple(pltpu.HBM(a.shape, a.dtype) for a in arrays), out_specs=tuple([HBM_ONLY] * n_arrays),
        input_output_aliases={j: j for j in range(n_arrays)}, compiler_params=IN_FLIGHT)(*arrays, *sems, *after)
    lands, pos = [], 0
    for ns, nl, _ in counts:
        lands.append(list(outs[pos + ns:pos + ns + nl]))
        pos += ns + nl
    return lands


def _place_and_neighbours():
    x, y, c = _my_place()
    return (x, y, c), (x, y, 1 - c), [(1 - x, y), (x, 1 - y), (1 - x, 1 - y)]


def _gather_start(name, srcs, lands, items, dep=None):
    ns, nl, ni = len(srcs), len(lands), len(items)

    def body(*refs):
        S, L = refs[0:ns], refs[ns:ns + nl]
        first_out = ns + nl + (0 if dep is None else 1)
        send_sems, recv_sems, token = refs[first_out], refs[first_out + 1], refs[-1]
        me, sib, chips = _place_and_neighbours()
        mi = _dev_index(*me)
        for i, (src, dst) in enumerate(items):
            for k, to in enumerate([sib] + [(*chip, me[2]) for chip in chips]):
                pltpu.make_async_remote_copy(
                    src_ref=src(S), dst_ref=dst(L, mi), send_sem=send_sems.at[4 * i + k],
                    recv_sem=recv_sems.at[4 * i + k], device_id=to, device_id_type=MESH).start()
        token[...] = jnp.zeros_like(token)

    thru = [pltpu.HBM(a.shape, a.dtype) for a in list(srcs) + list(lands)]
    args = [_in_hbm(a) for a in srcs] + [_in_hbm(a) for a in lands]
    in_specs = [HBM_ONLY] * (ns + nl)
    if dep is not None:
        args.append(dep)
        in_specs.append(HBM_SPEC)
    outs = pl.pallas_call(
        body, name=name, in_specs=in_specs,
        out_shape=(pltpu.SemaphoreType.DMA((4 * ni,)), pltpu.SemaphoreType.DMA((4 * ni,)), *thru, _sds((8, 128), F32)),
        out_specs=(SEM_SPEC, SEM_SPEC, *[HBM_ONLY] * (ns + nl), pl.BlockSpec(memory_space=pltpu.VMEM)),
        input_output_aliases={j: 2 + j for j in range(ns + nl)}, compiler_params=IN_FLIGHT)(*args)
    return dict(send=outs[0], recv=outs[1], srcs=list(outs[2:2 + ns]), lands=list(outs[2 + ns:2 + ns + nl]),
                token=outs[-1], items=items)


def _gather_forward(name, st, after):
    nl, ni = len(st["lands"]), len(st["items"])

    def body(*refs):
        L, recv_sems = refs[0:nl], refs[nl]
        fwd_send, fwd_recv, token = refs[2 * nl + 2], refs[2 * nl + 3], refs[-1]
        me, sib, chips = _place_and_neighbours()
        for i, (_, dst) in enumerate(st["items"]):
            for j, chip in enumerate(chips):
                blk = dst(L, _dev_index(*chip, me[2]))
                pltpu.make_async_remote_copy(
                    src_ref=blk, dst_ref=blk, send_sem=fwd_send.at[3 * i + j], recv_sem=recv_sems.at[4 * i + 1 + j],
                    device_id=me, device_id_type=MESH).wait_recv()
                pltpu.make_async_remote_copy(
                    src_ref=blk, dst_ref=blk, send_sem=fwd_send.at[3 * i + j], recv_sem=fwd_recv.at[3 * i + j],
                    device_id=sib, device_id_type=MESH).start()
        token[...] = jnp.zeros_like(token)

    outs = pl.pallas_call(
        body, name=name, in_specs=[HBM_ONLY] * nl + [SEM_SPEC, HBM_SPEC],
        out_shape=(*[pltpu.HBM(a.shape, a.dtype) for a in st["lands"]], pltpu.SemaphoreType.DMA((3 * ni,)),
                   pltpu.SemaphoreType.DMA((3 * ni,)), _sds((8, 128), F32)),
        out_specs=(*[HBM_ONLY] * nl, SEM_SPEC, SEM_SPEC, pl.BlockSpec(memory_space=pltpu.VMEM)),
        input_output_aliases={j: j for j in range(nl)}, compiler_params=IN_FLIGHT)(*st["lands"], st["recv"], after)
    return dict(st, lands=list(outs[0:nl]), fwd_send=outs[nl], fwd_recv=outs[nl + 1], token=outs[-1])


def _gather_wait(name, st, after):
    ns, nl, ni = len(st["srcs"]), len(st["lands"]), len(st["items"])

    def body(*refs):
        S, L = refs[0:ns], refs[ns:ns + nl]
        send_sems, recv_sems, fwd_send, fwd_recv = refs[ns + nl:ns + nl + 4]
        me, sib, chips = _place_and_neighbours()
        mi = _dev_index(*me)
        for i, (src, dst) in enumerate(st["items"]):
            mine = dst(L, mi)
            for k in range(4):
                pltpu.make_async_remote_copy(
                    src_ref=src(S), dst_ref=mine, send_sem=send_sems.at[4 * i + k], recv_sem=recv_sems.at[4 * i + k],
                    device_id=me, device_id_type=MESH).wait_send()
            pltpu.make_async_remote_copy(
                src_ref=src(S), dst_ref=mine, send_sem=send_sems.at[4 * i], recv_sem=recv_sems.at[4 * i],
                device_id=me, device_id_type=MESH).wait_recv()
            for j in range(3):
                cp = pltpu.make_async_remote_copy(
                    src_ref=mine, dst_ref=mine, send_sem=fwd_send.at[3 * i + j], recv_sem=fwd_recv.at[3 * i + j],
                    device_id=me, device_id_type=MESH)
                cp.wait_send()
                cp.wait_recv()

    arrays = st["srcs"] + st["lands"]
    outs = pl.pallas_call(
        body, name=name, in_specs=[HBM_ONLY] * (ns + nl) + [SEM_SPEC] * 4 + [HBM_SPEC],
        out_shape=tuple(pltpu.HBM(a.shape, a.dtype) for a in arrays), out_specs=tuple([HBM_ONLY] * (ns + nl)),
        input_output_aliases={j: j for j in range(ns + nl)},
        compiler_params=IN_FLIGHT)(*arrays, st["send"], st["recv"], st["fwd_send"], st["fwd_recv"], after)
    return list(outs[ns:ns + nl])


def _sum_slots(land):
    def body(l_ref, o_ref):
        acc = l_ref[0]
        for d in range(1, NDEV):
            acc = acc + l_ref[d]
        o_ref[...] = acc

    vm = pl.BlockSpec(memory_space=pltpu.VMEM)
    return pl.pallas_call(body, name="sum_small", out_shape=_sds(land.shape[1:], F32), in_specs=[vm], out_specs=vm)(land)


def _adam_math(w, g, m, v):
    m2 = ADAM_B1 * m + (1.0 - ADAM_B1) * g
    v2 = ADAM_B2 * v + (1.0 - ADAM_B2) * (g * g)
    delta = -ADAM_LR * ((m2 * ADAM_C1) / (jnp.sqrt(v2 * ADAM_C2) + ADAM_EPS) + ADAM_WD * w)
    return delta, m2, v2


def _adam_layer(land, sel, w, m, v, layer, name, prev=None, tc=512):
    R = land.shape[2]

    def body(l_ref, w_ref, m_ref, v_ref, *rest):
        g_out, d_out, m_out, v_out = rest[-4:]
        g = l_ref[0].astype(F32)
        for d in range(1, NDEV):
            g = g + l_ref[d].astype(F32)
        delta, m2, v2 = _adam_math(w_ref[...], g, m_ref[...], v_ref[...])
        g_out[...] = g
        d_out[...] = delta
        m_out[...] = m2
        v_out[...] = v2

    wspec = pl.BlockSpec((None, R, tc), lambda i: (layer, 0, i))
    in_specs = [pl.BlockSpec((None, NDEV, R, tc), lambda i: (sel, 0, 0, i)), wspec, wspec, wspec]
    args = [land, w, m, v]
    aliases = {}
    if prev is not None:
        in_specs += [HBM_SPEC] * 4
        args += list(prev)
        aliases = {4 + j: j for j in range(4)}
    return pl.pallas_call(
        body, name=name, grid=(D // tc,), in_specs=in_specs, out_specs=[wspec] * 4,
        out_shape=[_sds(w.shape, F32)] * 4, input_output_aliases=aliases, compiler_params=_cparams(1))(*args)


def _adam_stacked(lands, sel, w, m, v, name):
    res = None
    for layer, land in enumerate(lands):
        res = _adam_layer(land, sel, w, m, v, layer, f"{name}{layer}", prev=res)
    return res


def _adam_small(ws, gs, ms, vs):
    n = len(ws)

    def body(*refs):
        w_r, g_r, m_r, v_r = refs[0:n], refs[n:2 * n], refs[2 * n:3 * n], refs[3 * n:4 * n]
        d_o, m_o, v_o = refs[4 * n:5 * n], refs[5 * n:6 * n], refs[6 * n:7 * n]
        for i in range(n):
            delta, m2, v2 = _adam_math(w_r[i][...], g_r[i][...], m_r[i][...], v_r[i][...])
            d_o[i][...] = delta
            m_o[i][...] = m2
            v_o[i][...] = v2

    vm = pl.BlockSpec(memory_space=pltpu.VMEM)
    shapes = [_sds(w.shape, F32) for w in ws]
    outs = pl.pallas_call(body, name="adam_small", in_specs=[vm] * (4 * n), out_specs=[vm] * (3 * n),
                          out_shape=shapes * 3)(*ws, *gs, *ms, *vs)
    return outs[0:n], outs[n:2 * n], outs[2 * n:3 * n]


WEIGHT_NAMES = ("ab_norm_g", "ab_w_in", "sgu_norm_g", "sgu_norm_b", "sgu_w", "sgu_bias", "q_norm_g", "k_norm_g",
                "ab_w_out", "cd_norm_g", "cd_w_in", "conv_c_w", "conv_c_b", "c_ln_g", "c_ln_b", "conv_d_w",
                "cd_w_out", "ffn_norm_g", "ffn_w_gate", "ffn_w_up", "ffn_w_down")
SMALL_2D = (("ab_norm_g", (1, 1024)), ("sgu_norm_g", (1, 512)), ("sgu_norm_b", (1, 512)), ("sgu_w", (512, 128)),
            ("sgu_bias", (4, 128)), ("q_norm_g", (3, 64)), ("k_norm_g", (3, 64)), ("cd_norm_g", (1, 128)),
            ("conv_c_w", (31, 64)), ("conv_c_b", (1, 64)), ("c_ln_g", (1, 64)), ("c_ln_b", (1, 64)),
            ("conv_d_w", (3, 64)), ("ffn_norm_g", (2, 1024)))
SHARD_C = 64


def _pack_rows(parts, rows):
    flat = jnp.concatenate([p.reshape(-1) for p in parts])
    return jnp.pad(flat, (0, rows * 128 - flat.shape[0])).reshape(rows, 128)


def kernel(x, ab_norm_g, ab_w_in, sgu_norm_g, sgu_norm_b, sgu_w, sgu_bias, q_norm_g, k_norm_g, ab_w_out, cd_norm_g, cd_w_in, conv_c_w, conv_c_b, c_ln_g, c_ln_b, conv_d_w, cd_w_out, ffn_norm_g, ffn_w_gate, ffn_w_up, ffn_w_down, loss_target, m_ab_norm_g, m_ab_w_in, m_sgu_norm_g, m_sgu_norm_b, m_sgu_w, m_sgu_bias, m_q_norm_g, m_k_norm_g, m_ab_w_out, m_cd_norm_g, m_cd_w_in, m_conv_c_w, m_conv_c_b, m_c_ln_g, m_c_ln_b, m_conv_d_w, m_cd_w_out, m_ffn_norm_g, m_ffn_w_gate, m_ffn_w_up, m_ffn_w_down, v_ab_norm_g, v_ab_w_in, v_sgu_norm_g, v_sgu_norm_b, v_sgu_w, v_sgu_bias, v_q_norm_g, v_k_norm_g, v_ab_w_out, v_cd_norm_g, v_cd_w_in, v_conv_c_w, v_conv_c_b, v_c_ln_g, v_c_ln_b, v_conv_d_w, v_cd_w_out, v_ffn_norm_g, v_ffn_w_gate, v_ffn_w_up, v_ffn_w_down):
    w = dict(zip(WEIGHT_NAMES, (ab_norm_g, ab_w_in, sgu_norm_g, sgu_norm_b, sgu_w, sgu_bias, q_norm_g, k_norm_g, ab_w_out, cd_norm_g, cd_w_in, conv_c_w, conv_c_b, c_ln_g, c_ln_b, conv_d_w, cd_w_out, ffn_norm_g, ffn_w_gate, ffn_w_up, ffn_w_down)))
    m = dict(zip(WEIGHT_NAMES, (m_ab_norm_g, m_ab_w_in, m_sgu_norm_g, m_sgu_norm_b, m_sgu_w, m_sgu_bias, m_q_norm_g, m_k_norm_g, m_ab_w_out, m_cd_norm_g, m_cd_w_in, m_conv_c_w, m_conv_c_b, m_c_ln_g, m_c_ln_b, m_conv_d_w, m_cd_w_out, m_ffn_norm_g, m_ffn_w_gate, m_ffn_w_up, m_ffn_w_down)))
    v = dict(zip(WEIGHT_NAMES, (v_ab_norm_g, v_ab_w_in, v_sgu_norm_g, v_sgu_norm_b, v_sgu_w, v_sgu_bias, v_q_norm_g, v_k_norm_g, v_ab_w_out, v_cd_norm_g, v_cd_w_in, v_conv_c_w, v_conv_c_b, v_c_ln_g, v_c_ln_b, v_conv_d_w, v_cd_w_out, v_ffn_norm_g, v_ffn_w_gate, v_ffn_w_up, v_ffn_w_down)))
    me = _dev_index(*_my_place())

    small_local = _pack_rows([w["cd_norm_g"], w["conv_c_w"], w["conv_c_b"], w["c_ln_g"], w["c_ln_b"], w["conv_d_w"]], 24)
    o_ab_in, o_small = _gather_first(w["ab_w_in"][0].T.astype(BF16), small_local, me)
    r_ff = DFF // NDEV
    one = lambda a: (lambda S, j: S[a])
    slot = lambda b: (lambda L, s: L[b].at[s])
    slot2 = lambda b, part: (lambda L, s: L[b].at[part, s])
    shard = lambda a: (lambda S: S[a])

    def layer_shards(layer):
        return (w["ffn_w_gate"][layer].T.astype(BF16), w["ffn_w_up"][layer].T.astype(BF16),
                w["ffn_w_down"][layer].astype(BF16))

    def gathered(own):
        return _landing((NDEV,) + own.shape, BF16, [((me,), own)])

    def gathered2(a, b):
        return _landing((2, NDEV) + a.shape, BF16, [((0, me), a), ((1, me), b)])

    ab_out_s = w["ab_w_out"][0].astype(BF16)
    gate0, up0, down0 = layer_shards(0)
    gathers = {1: _gather_start(
        "gather1_start", [ab_out_s, gate0, up0, down0], [gathered(ab_out_s), gathered2(gate0, up0), gathered(down0)],
        [(shard(0), slot(0)), (shard(1), slot2(1, 0)), (shard(2), slot2(1, 1)), (shard(3), slot(2))], dep=o_small)}

    def fetch(stage, after):
        if stage == "attn0":
            cd_in_s, cd_out_s = w["cd_w_in"][0].T.astype(BF16), w["cd_w_out"][0].astype(BF16)
            gate1, up1, down1 = layer_shards(1)
            gathers[2] = _gather_start(
                "gather2_start", [cd_in_s, cd_out_s, gate1, up1, down1],
                [gathered(cd_in_s), gathered(cd_out_s), gathered2(gate1, up1), gathered(down1)],
                [(shard(0), slot(0)), (shard(1), slot(1)), (shard(2), slot2(2, 0)), (shard(3), slot2(2, 1)),
                 (shard(4), slot(3))], dep=after)
            return {"dep_attn1": gathers[2]["token"]}
        if stage == "attn1":
            gathers[1] = _gather_forward("gather1_forward", gathers[1], after)
            return {"dep_attn2": gathers[1]["token"]}
        if stage == "ab_out":
            l_out, l_ffn, l_down = _gather_wait("gather1_wait", gathers[1], after)
            return {"w_ab_out": l_out.reshape(D, D), "wt_ffn_in0": l_ffn.reshape(2 * DFF, D),
                    "w_ffn_down0": l_down.reshape(DFF, D)}
        if stage == "ffn_down0":
            gathers[2] = _gather_forward("gather2_forward", gathers[2], after)
            return {"dep_down0": gathers[2]["token"]}
        if stage == "cd_in":
            l_in, l_out, l_ffn, l_down = _gather_wait("gather2_wait", gathers[2], after)
            return {"wt_cd_in": l_in.reshape(CD_IN, D), "w_cd_out": l_out.reshape(D, D),
                    "wt_ffn_in1": l_ffn.reshape(2 * DFF, D), "w_ffn_down1": l_down.reshape(DFF, D)}
        return {}

    scatters = {}
    rides_with = {"w_ffn_down1": "wt_ffn_in1", "w_cd_out": "wt_cd_in", "w_ffn_down0": "wt_ffn_in0",
                  "w_ab_out": "wt_ab_in"}
    held = {}

    def on_grad(key, arr):
        if key in rides_with:
            held[rides_with[key]] = (key, arr)
            return None
        group = ([held.pop(key)] if key in held else []) + [(key, arr)]
        srcs, lands, items = [], [], []
        for n, (k, a) in enumerate(group):
            if k.startswith("wt_ffn_in"):
                src = a.reshape(2, NDEV, r_ff, D)
                own = lax.dynamic_slice_in_dim(src, me, 1, axis=1)
                lands.append(lax.dynamic_update_slice(lax.empty(src.shape, BF16), own, (0, me, 0, 0)))
                items += [((lambda S, j, n=n: S[n].at[0, j]), slot2(n, 0)), ((lambda S, j, n=n: S[n].at[1, j]), slot2(n, 1))]
            else:
                rows = a.shape[0] // NDEV
                src = a.reshape(NDEV, rows, D)
                own = lax.dynamic_index_in_dim(src, me, 0, keepdims=False)
                lands.append(_landing((1, NDEV, rows, D), BF16, [((0, me), own)]))
                items.append(((lambda S, j, n=n: S[n].at[j]), slot2(n, 0)))
            srcs.append(src)
        st = _exchange_start(f"scatter_{key}_start", srcs, lands, items)
        scatters[key] = (st, [k for k, _ in group])
        return st["token"]

    flat = o_small.reshape(NDEV, 24 * 128)

    def chan(lo, taps):
        return flat[:, lo:lo + taps * SHARD_C].reshape(NDEV, taps, SHARD_C).transpose(1, 0, 2).reshape(taps, 512)

    W = {
        "wt_ab_in": o_ab_in.reshape(AB_IN, D), "dep0": gathers[1]["token"],
        "ab_norm_g": w["ab_norm_g"], "sgu_norm_g": w["sgu_norm_g"], "sgu_norm_b": w["sgu_norm_b"],
        "sgu_w": w["sgu_w"][0], "sgu_bias": w["sgu_bias"][0], "q_norm_g": w["q_norm_g"][0],
        "k_norm_g": w["k_norm_g"][0], "ffn_norm_g": w["ffn_norm_g"],
        "cd_norm_g": flat[:, 0:128].reshape(1, D),
        "conv_c_w32": jnp.pad(chan(128, CONV_C_TAPS), ((0, 1), (0, 0))),
        "conv_c_b": chan(2112, 1), "c_ln_g": chan(2176, 1), "c_ln_b": chan(2240, 1),
        "conv_d_w8": jnp.pad(chan(2304, CONV_D_TAPS), ((0, 8 - CONV_D_TAPS), (0, 0))),
    }

    loss_cols, grad_x, G = _local_step(x[0], loss_target[0], W, fetch, on_grad)
    loss = lax.psum(jnp.sum(loss_cols), ("x", "y", "c"))

    small_parts = [G["ab_norm_g"], G["sgu_norm_g"], G["sgu_norm_b"], G["sgu_w"], G["sgu_bias"], G["q_norm_g"],
                   G["k_norm_g"], G["cd_norm_g"], G["conv_c_w32"][:CONV_C_TAPS], G["conv_c_b"], G["c_ln_g"],
                   G["c_ln_b"], G["conv_d_w8"][:CONV_D_TAPS], G["ffn_norm_g0"], G["ffn_norm_g1"]]
    sizes = [p.size for p in small_parts]
    small_rows = 712
    packed = _pack_rows(small_parts, small_rows)
    small = _exchange_start("small_start", [packed], [_landing((NDEV, small_rows, 128), F32, [((me,), packed)])],
                            [(one(0), slot(0))])
    landed = {}

    def wait_scatters(name, group_keys, after):
        res = _exchange_wait(name, [scatters[gk][0] for gk in group_keys], after)
        for gk, lands in zip(group_keys, res):
            landed.update(zip(scatters[gk][1], lands))

    wait_scatters("scatter_wait_early", ["wt_ffn_in1", "wt_cd_in", "wt_ffn_in0"], small["token"])

    grads, deltas, new_m, new_v = {}, {}, {}, {}
    done = []

    def put(name, res):
        grads[name], deltas[name], new_m[name], new_v[name] = res

    def adam(name, lands, sel, transposed):
        flip = (lambda a: jnp.swapaxes(a, 1, 2)) if transposed else (lambda a: a)
        res = _adam_stacked(lands, sel, flip(w[name]), flip(m[name]), flip(v[name]), f"adam_{name}")
        done.append(res[1])
        put(name, [flip(r) for r in res])

    ffn_in_lands = [landed["wt_ffn_in0"], landed["wt_ffn_in1"]]
    adam("cd_w_in", [landed["wt_cd_in"]], 0, True)
    adam("ffn_w_gate", ffn_in_lands, 0, True)
    adam("ffn_w_up", ffn_in_lands, 1, True)
    adam("cd_w_out", [landed["w_cd_out"]], 0, False)
    adam("ffn_w_down", [landed["w_ffn_down0"], landed["w_ffn_down1"]], 0, False)

    small_land = _exchange_wait("small_wait", [small], list(done))[0][0]
    red = _sum_slots(small_land).reshape(-1)
    offs = [0]
    for s in sizes:
        offs.append(offs[-1] + s)
    seg = [red[offs[i]:offs[i + 1]] for i in range(len(sizes))]

    def own_channels(full, taps):
        return lax.dynamic_slice_in_dim(full.reshape(taps, 512), me * SHARD_C, SHARD_C, axis=1)

    g_small = {
        "ab_norm_g": seg[0].reshape(1, 1024), "sgu_norm_g": seg[1].reshape(1, 512), "sgu_norm_b": seg[2].reshape(1, 512),
        "sgu_w": seg[3].reshape(512, 128), "sgu_bias": seg[4].reshape(4, 128), "q_norm_g": seg[5].reshape(3, 64),
        "k_norm_g": seg[6].reshape(3, 64),
        "cd_norm_g": lax.dynamic_slice_in_dim(seg[7].reshape(1, D), me * (D // NDEV), D // NDEV, axis=1),
        "conv_c_w": own_channels(seg[8], CONV_C_TAPS), "conv_c_b": own_channels(seg[9], 1),
        "c_ln_g": own_channels(seg[10], 1), "c_ln_b": own_channels(seg[11], 1),
        "conv_d_w": own_channels(seg[12], CONV_D_TAPS),
        "ffn_norm_g": jnp.concatenate([seg[13].reshape(1, D), seg[14].reshape(1, D)], axis=0),
    }

    names2d = [n for n, _ in SMALL_2D]
    d_s, m_s, v_s = _adam_small([w[n].reshape(s) for n, s in SMALL_2D], [g_small[n] for n in names2d],
                                [m[n].reshape(s) for n, s in SMALL_2D], [v[n].reshape(s) for n, s in SMALL_2D])
    for i, n in enumerate(names2d):
        shape = w[n].shape
        grads[n], deltas[n] = g_small[n].reshape(shape), d_s[i].reshape(shape)
        new_m[n], new_v[n] = m_s[i].reshape(shape), v_s[i].reshape(shape)

    wait_scatters("scatter_wait_last", ["wt_ab_in"], d_s[0])
    adam("ab_w_out", [landed["w_ab_out"]], 0, False)
    adam("ab_w_in", [landed["wt_ab_in"]], 0, True)

    return (loss, grad_x[None], *[grads[n] for n in WEIGHT_NAMES], *[deltas[n] for n in WEIGHT_NAMES],
            *[new_m[n] for n in WEIGHT_NAMES], *[new_v[n] for n in WEIGHT_NAMES])
```

```python
import functools

import jax
import jax.numpy as jnp
import numpy as np
from jax import lax
from jax.experimental import pallas as pl
from jax.experimental.pallas import tpu as pltpu

F32 = jnp.float32
BF16 = jnp.bfloat16

T = 4096
D = 1024
NDEV = 8
EPS = 1e-6
NEG_INF = -1e30
DFF = 2816
AB_IN = 5632
CD_IN = 2560
HEAD = 64
PAIR = 128
NPAIR = 4
NBACK = 128
DIL_RATES = (1, 4, 16)
ROPE_HALF = 8
ROPE_THETA = 500000.0
CONV_C_TAPS = 31
CONV_D_TAPS = 3
HALO = 32
ATTN_BWD_UNROLL = 4

ADAM_LR = 0.001
ADAM_B1 = 0.9
ADAM_B2 = 0.999
ADAM_EPS = 1e-08
ADAM_WD = 0.01
ADAM_STEP = 10
ADAM_C1 = 1.0 / (1.0 - ADAM_B1 ** ADAM_STEP)
ADAM_C2 = 1.0 / (1.0 - ADAM_B2 ** ADAM_STEP)

VMEM_LIMIT_MB = 48
MESH = pl.DeviceIdType.MESH
HBM_SPEC = pl.BlockSpec(memory_space=pl.ANY)


def _cparams(ngrid, vmem_mb=VMEM_LIMIT_MB):
    return pltpu.CompilerParams(dimension_semantics=("arbitrary",) * ngrid,
                                vmem_limit_bytes=vmem_mb * 1024 * 1024)


def _pick(n, options):
    for o in options:
        if n % o == 0:
            return o
    raise ValueError(f"no tile for {n} in {options}")


def _sds(shape, dtype):
    return jax.ShapeDtypeStruct(shape, dtype)


def _sigmoid(x):
    return 1.0 / (1.0 + jnp.exp(-x))


def _sigmoid_bf16(x):
    return 0.5 * jnp.tanh(0.5 * x) + 0.5


def _gelu(z):
    return 0.5 * z * (1.0 + lax.erf(z * 0.7071067811865476))


def _gelu_grad(z):
    return 0.5 * (1.0 + lax.erf(z * 0.7071067811865476)) + z * jnp.exp(-0.5 * z * z) * 0.3989422804014327


def _mm_nt(a, wt, name, out_dtype=BF16, tm=2048, dep=None):
    M, K = a.shape
    N = wt.shape[0]
    tn = _pick(N, (512, 256))

    def body(a_ref, w_ref, *rest):
        o_ref = rest[-1]
        o_ref[...] = lax.dot_general(a_ref[...], w_ref[...], (((1,), (1,)), ((), ())),
                                     preferred_element_type=F32).astype(o_ref.dtype)

    in_specs = [pl.BlockSpec((tm, K), lambda i, j: (i, 0)), pl.BlockSpec((tn, K), lambda i, j: (j, 0))]
    args = [a, wt]
    if dep is not None:
        in_specs.append(HBM_SPEC)
        args.append(dep)
    return pl.pallas_call(
        body, name=name, grid=(M // tm, N // tn), in_specs=in_specs,
        out_specs=pl.BlockSpec((tm, tn), lambda i, j: (i, j)),
        out_shape=_sds((M, N), out_dtype), compiler_params=_cparams(2))(*args)


EPI_ROWS = 256


def _mm_nn(a, w, name, mode, resid, gain=None, tgt=None, dep=None, tm=512):
    M, K = a.shape
    N = w.shape[1]
    side = gain if mode == "rms" else tgt

    def body(a_ref, w_ref, resid_ref, side_ref, *rest):
        outs, acc = rest[-3 if mode == "rms" else -4:-1], rest[-1]
        i = pl.program_id(0)
        acc[...] = jnp.dot(a_ref[...], w_ref[...], preferred_element_type=F32)

        if mode == "loss":
            @pl.when(i == 0)
            def _():
                outs[2][...] = jnp.zeros_like(outs[2])

        for r0 in range(0, tm, EPI_ROWS):
            rows = slice(r0, r0 + EPI_ROWS)
            v = acc[rows, :] + resid_ref[rows, :]
            if mode == "rms":
                outs[0][rows, :] = v
                r = lax.rsqrt(jnp.mean(v * v, axis=-1, keepdims=True) + EPS)
                outs[1][rows, :] = (v * r * side_ref[...]).astype(BF16)
            else:
                d = v - side_ref[rows, :]
                outs[2][...] += jnp.sum(d * d, axis=0, keepdims=True) * (0.5 / N)
                dy = d * (1.0 / N)
                outs[0][rows, :] = dy
                outs[1][rows, :] = dy.astype(BF16)

    row = pl.BlockSpec((tm, N), lambda i: (i, 0))
    vec = pl.BlockSpec((1, N), lambda i: (0, 0))
    in_specs = [pl.BlockSpec((tm, K), lambda i: (i, 0)),
                pl.BlockSpec((K, N), lambda i: (0, 0), pipeline_mode=pl.Buffered(1)), row,
                vec if mode == "rms" else row]
    args = [a, w, resid, side]
    if dep is not None:
        in_specs.append(HBM_SPEC)
        args.append(dep)
    if mode == "rms":
        out_specs, out_shape = [row, row], [_sds((M, N), F32), _sds((M, N), BF16)]
    else:
        out_specs, out_shape = [row, row, vec], [_sds((M, N), F32), _sds((M, N), BF16), _sds((1, N), F32)]
    return pl.pallas_call(
        body, name=name, grid=(M // tm,), in_specs=in_specs, out_specs=out_specs, out_shape=out_shape,
        scratch_shapes=[pltpu.VMEM((tm, N), F32)], compiler_params=_cparams(1))(*args)


def _mm_dh_rms_bwd(a, w, x, gain, dres, name, dep=None, tm=512):
    parts = a.shape[0] if a.ndim == 3 else 1
    M, Kp = a.shape[-2], a.shape[-1]
    N = w.shape[1]
    nblk = M // tm
    assert nblk % 2 == 0

    def body(a_ref, w_ref, x_ref, g_ref, dres_ref, *rest):
        dx_ref, dxb_ref, dg_ref, acc0, acc1 = rest[-5:]
        i = pl.program_id(0)

        def matmul(acc):
            if parts == 1:
                acc[...] = jnp.dot(a_ref[...], w_ref[...], preferred_element_type=F32)
            else:
                d = jnp.dot(a_ref[0], w_ref[0:Kp, :], preferred_element_type=F32)
                for p in range(1, parts):
                    d = d + jnp.dot(a_ref[p], w_ref[p * Kp:(p + 1) * Kp, :], preferred_element_type=F32)
                acc[...] = d

        def finish(acc):
            for r0 in range(0, tm, EPI_ROWS // 2):
                rows = slice(r0, r0 + EPI_ROWS // 2)
                v = acc[rows, :]
                xf = x_ref[rows, :]
                r = lax.rsqrt(jnp.mean(xf * xf, axis=-1, keepdims=True) + EPS)
                xhat = xf * r
                dg_ref[...] += jnp.sum(v * xhat, axis=0, keepdims=True)
                dxh = v * g_ref[...]
                tot = dres_ref[rows, :] + r * (dxh - xhat * jnp.mean(dxh * xhat, axis=-1, keepdims=True))
                dx_ref[rows, :] = tot
                dxb_ref[rows, :] = tot.astype(BF16)

        @pl.when(i == 0)
        def _():
            dg_ref[...] = jnp.zeros_like(dg_ref)
            matmul(acc0)

        @pl.when((i > 0) & (i < nblk) & (i % 2 == 1))
        def _():
            matmul(acc1)
            finish(acc0)

        @pl.when((i > 0) & (i < nblk) & (i % 2 == 0))
        def _():
            matmul(acc0)
            finish(acc1)

        @pl.when(i == nblk)
        def _():
            finish(acc1)

    last = nblk - 1
    row = pl.BlockSpec((tm, N), lambda i: (jnp.maximum(i - 1, 0), 0))
    vec = pl.BlockSpec((1, N), lambda i: (0, 0))
    if a.ndim == 3:
        a_spec = pl.BlockSpec((parts, tm, Kp), lambda i: (0, jnp.minimum(i, last), 0))
    else:
        a_spec = pl.BlockSpec((tm, Kp), lambda i: (jnp.minimum(i, last), 0))
    w_spec = pl.BlockSpec((parts * Kp, N), lambda i: (0, 0), pipeline_mode=pl.Buffered(1))
    in_specs = [a_spec, w_spec, row, vec, row]
    args = [a, w, x, gain, dres]
    if dep is not None:
        in_specs.append(HBM_SPEC)
        args.append(dep)
    return pl.pallas_call(
        body, name=name, grid=(nblk + 1,), in_specs=in_specs, out_specs=[row, row, vec],
        out_shape=[_sds((M, N), F32), _sds((M, N), BF16), _sds((1, N), F32)],
        scratch_shapes=[pltpu.VMEM((tm, N), F32), pltpu.VMEM((tm, N), F32)], compiler_params=_cparams(1, 56))(*args)


def _mm_tn(a, b, name, out_dtype=BF16, tt=1024):
    parts = a.shape[0] if a.ndim == 3 else 1
    Tt, Mp = a.shape[-2], a.shape[-1]
    N = b.shape[1]
    tn = _pick(Mp, (1408, 1280, 1024, 512))
    jper = Mp // tn
    nt = Tt // tt

    def body(a_ref, b_ref, o_ref, acc):
        t = pl.program_id(1)

        @pl.when(t == 0)
        def _():
            acc[...] = jnp.zeros_like(acc)

        acc[...] += lax.dot_general(a_ref[...], b_ref[...], (((0,), (0,)), ((), ())),
                                    preferred_element_type=F32)

        @pl.when(t == nt - 1)
        def _():
            o_ref[...] = acc[...].astype(o_ref.dtype)

    if a.ndim == 3:
        a_spec = pl.BlockSpec((None, tt, tn), lambda j, t: (j // jper, t, j % jper))
    else:
        a_spec = pl.BlockSpec((tt, tn), lambda j, t: (t, j))
    return pl.pallas_call(
        body, name=name, grid=(parts * jper, nt),
        in_specs=[a_spec, pl.BlockSpec((tt, N), lambda j, t: (t, 0))],
        out_specs=pl.BlockSpec((tn, N), lambda j, t: (j, 0)),
        out_shape=_sds((parts * Mp, N), out_dtype), scratch_shapes=[pltpu.VMEM((tn, N), F32)],
        compiler_params=_cparams(2))(a, b)


def _ffn_in(h, wt_in, name, tm=2048, tn=256):
    nj = DFF // tn

    def body(h_ref, wg_ref, wu_ref, p_ref, act_ref):
        nt = (((1,), (1,)), ((), ()))
        g = lax.dot_general(h_ref[...], wg_ref[...], nt, preferred_element_type=F32).astype(BF16)
        u = lax.dot_general(h_ref[...], wu_ref[...], nt, preferred_element_type=F32).astype(BF16)
        p_ref[0] = g
        p_ref[1] = u
        act_ref[...] = g * _sigmoid_bf16(g) * u

    return pl.pallas_call(
        body, name=name, grid=(T // tm, nj),
        in_specs=[pl.BlockSpec((tm, D), lambda i, j: (i, 0)), pl.BlockSpec((tn, D), lambda i, j: (j, 0)),
                  pl.BlockSpec((tn, D), lambda i, j: (j + nj, 0))],
        out_specs=[pl.BlockSpec((2, tm, tn), lambda i, j: (0, i, j)), pl.BlockSpec((tm, tn), lambda i, j: (i, j))],
        out_shape=[_sds((2, T, DFF), BF16), _sds((T, DFF), BF16)], compiler_params=_cparams(2))(h, wt_in, wt_in)


def _ffn_dact(dyb, w_down, p3, name, tm=2048, tn=256, dep=None):
    def body(dy_ref, w_ref, p_ref, *rest):
        o_ref = rest[-1]
        da = lax.dot_general(dy_ref[...], w_ref[...], (((1,), (1,)), ((), ())),
                             preferred_element_type=F32).astype(BF16)
        g = p_ref[0]
        u = p_ref[1]
        sg = _sigmoid_bf16(g)
        gs = g * sg
        o_ref[0] = (da * u) * (sg + gs * (1.0 - sg))
        o_ref[1] = da * gs

    pspec = pl.BlockSpec((2, tm, tn), lambda i, j: (0, i, j))
    in_specs = [pl.BlockSpec((tm, D), lambda i, j: (i, 0)), pl.BlockSpec((tn, D), lambda i, j: (j, 0)), pspec]
    args = [dyb, w_down, p3]
    if dep is not None:
        in_specs.append(HBM_SPEC)
        args.append(dep)
    return pl.pallas_call(
        body, name=name, grid=(T // tm, DFF // tn), in_specs=in_specs, out_specs=pspec,
        out_shape=_sds((2, T, DFF), BF16), compiler_params=_cparams(2))(*args)


def _rms_fwd(x, g, name, tm=512):
    def body(x_ref, g_ref, h_ref):
        xf = x_ref[...]
        r = lax.rsqrt(jnp.mean(xf * xf, axis=-1, keepdims=True) + EPS)
        h_ref[...] = (xf * r * g_ref[...]).astype(BF16)

    return pl.pallas_call(
        body, name=name, grid=(T // tm,),
        in_specs=[pl.BlockSpec((tm, D), lambda i: (i, 0)), pl.BlockSpec((1, D), lambda i: (0, 0))],
        out_specs=pl.BlockSpec((tm, D), lambda i: (i, 0)),
        out_shape=_sds((T, D), BF16), compiler_params=_cparams(1))(x, g)


def _tril_mask():
    r = lax.broadcasted_iota(jnp.int32, (128, 128), 0)
    c = lax.broadcasted_iota(jnp.int32, (128, 128), 1)
    return r >= c


def _mix_a_fwd(pab, sgu_g, sgu_b, sgu_w, sgu_bias3, tm=512):
    def body(zu_ref, zv_ref, g_ref, b_ref, w_ref, bias_ref, o_ref):
        u = _gelu(zu_ref[...].astype(F32))
        v = _gelu(zv_ref[...].astype(F32))
        mu = jnp.mean(v, axis=-1, keepdims=True)
        vc = v - mu
        rstd = lax.rsqrt(jnp.mean(vc * vc, axis=-1, keepdims=True) + EPS)
        vn = (vc * rstd * g_ref[...] + b_ref[...]).astype(BF16)
        tri = _tril_mask()
        for gi in range(4):
            wg = jnp.where(tri, w_ref[gi], 0.0).astype(BF16)
            bg = bias_ref[gi]
            for c in range(tm // 128):
                rs, cs = slice(c * 128, (c + 1) * 128), slice(gi * 128, (gi + 1) * 128)
                mixed = jnp.dot(wg, vn[rs, cs], preferred_element_type=F32) + bg
                o_ref[rs, cs] = (u[rs, cs] * mixed).astype(BF16)

    half = pl.BlockSpec((tm, 512), lambda i: (i, 0))
    return pl.pallas_call(
        body, name="mix_a_fwd", grid=(T // tm,),
        in_specs=[half, pl.BlockSpec((tm, 512), lambda i: (i, 1)),
                  pl.BlockSpec((1, 512), lambda i: (0, 0)), pl.BlockSpec((1, 512), lambda i: (0, 0)),
                  pl.BlockSpec((4, 128, 128), lambda i: (0, 0, 0)), pl.BlockSpec((4, 128, 1), lambda i: (0, 0, 0))],
        out_specs=half, out_shape=_sds((T, D), BF16), compiler_params=_cparams(1),
    )(pab, pab, sgu_g, sgu_b, sgu_w, sgu_bias3)


def _rope_tables():
    pos = np.arange(T, dtype=np.float32)
    inv_freq = np.float32(ROPE_THETA) ** (-np.arange(ROPE_HALF, dtype=np.float32) * np.float32(2.0 / (2 * ROPE_HALF)))
    ang = (pos[:, None] * inv_freq[None, :]).astype(np.float32)
    cos, sin = np.cos(ang), np.sin(ang)
    z8 = np.zeros((T, ROPE_HALF), np.float32)
    rest = np.zeros((T, HEAD - 2 * ROPE_HALF), np.float32)
    c64 = np.concatenate([cos, cos, rest + 1.0], axis=1)
    s1 = np.concatenate([z8, sin, rest], axis=1)
    s2 = np.concatenate([-sin, z8, rest], axis=1)
    return tuple(jnp.asarray(np.tile(t, (1, 2)).astype(np.float32)) for t in (c64, s1, s2))


def _lo_mask(shape):
    return lax.broadcasted_iota(jnp.int32, shape, 1) < HEAD


def _seg_mean(x, lo):
    s_all = jnp.sum(x, axis=-1, keepdims=True)
    s_lo = jnp.sum(jnp.where(lo, x, 0.0), axis=-1, keepdims=True)
    return jnp.where(lo, s_lo, s_all - s_lo) * (1.0 / HEAD)


def _rope(n, c, s1, s2):
    return n * c + pltpu.roll(n, ROPE_HALF, 1) * s1 + pltpu.roll(n, PAIR - ROPE_HALF, 1) * s2


def _rope_t(dy, c, s1, s2):
    return dy * c - pltpu.roll(dy, PAIR - ROPE_HALF, 1) * s2 - pltpu.roll(dy, ROPE_HALF, 1) * s1


def _prep_fwd(pab, qg, kg, tabs, tm=512):
    def body(p_ref, qg_ref, kg_ref, c_ref, s1_ref, s2_ref, *outs):
        lo = _lo_mask((tm, PAIR))
        c, s1, s2 = c_ref[...], s1_ref[...], s2_ref[...]
        for g in range(3):
            qn_ref, kn_ref, v_ref = outs[3 * g:3 * g + 3]
            for p in range(NPAIR):
                for which, gains, dst in ((0, qg_ref, qn_ref), (1, kg_ref, kn_ref)):
                    col = (2 + 3 * which + g) * 512 + p * PAIR
                    xr = p_ref[:, col:col + PAIR].astype(F32)
                    rinv = lax.rsqrt(_seg_mean(xr * xr, lo) + EPS)
                    outs[9 + 2 * g + which][p] = rinv
                    dst[p] = _rope(xr * rinv * gains[g:g + 1, :], c, s1, s2)
                col = (8 + g) * 512 + p * PAIR
                v_ref[p] = p_ref[:, col:col + PAIR].astype(F32)

    pm = pl.BlockSpec((NPAIR, tm, PAIR), lambda i: (0, i, 0))
    tab = pl.BlockSpec((tm, PAIR), lambda i: (i, 0))
    gain = pl.BlockSpec((3, PAIR), lambda i: (0, 0))
    res = pl.pallas_call(
        body, name="prep_fwd", grid=(T // tm,),
        in_specs=[pl.BlockSpec((tm, AB_IN), lambda i: (i, 0)), gain, gain, tab, tab, tab],
        out_specs=[pm] * 15, out_shape=[_sds((NPAIR, T, PAIR), F32)] * 15,
        compiler_params=_cparams(1))(pab, qg, kg, *tabs)
    return res[0:9], res[9:15]


def _res_index(it, rate):
    window = NBACK * rate
    b = it // rate
    rho = it % rate
    start = b * window + rho
    startp = jnp.maximum(start - window, rho)
    kmin = jnp.where(b > 0, 0, NBACK)
    return start, startp, kmin


def _rows(start, rate):
    if rate == 1:
        return pl.ds(pl.multiple_of(start, NBACK), NBACK)
    return pl.ds(start, NBACK, stride=rate)


def _band_bias():
    qs = lax.broadcasted_iota(jnp.int32, (2 * NBACK, 2 * NBACK), 0)
    kj = lax.broadcasted_iota(jnp.int32, (2 * NBACK, 2 * NBACK), 1)
    dist = (qs & (NBACK - 1)) + NBACK - kj
    both = (dist >= 0) & (dist <= NBACK)
    return jnp.where(both, 0.0, NEG_INF), jnp.where(both & (kj >= NBACK), 0.0, NEG_INF)


def _attn_fwd(qn, kn, v, rate, name, dep=None):
    def body(q_ref, k_ref, v_ref, *rest):
        o_ref, l_ref = rest[-2:]
        lo = _lo_mask((NBACK, PAIR))
        bias_all, bias_first = _band_bias()

        def step(it, carry):
            start, startp, kmin = _res_index(it, rate)
            q = q_ref[_rows(start, rate), :] * (HEAD ** -0.5)
            kcat = jnp.concatenate([k_ref[_rows(startp, rate), :], k_ref[_rows(start, rate), :]], axis=0).astype(BF16)
            vcat = jnp.concatenate([v_ref[_rows(startp, rate), :], v_ref[_rows(start, rate), :]], axis=0).astype(BF16)
            vcat1 = jnp.concatenate([vcat, jnp.ones((2 * NBACK, PAIR), BF16)], axis=1)
            q2 = jnp.concatenate([jnp.where(lo, q, 0.0), jnp.where(lo, 0.0, q)], axis=0).astype(BF16)
            s = lax.dot_general(q2, kcat, (((1,), (1,)), ((), ())), preferred_element_type=F32)
            s = s + jnp.where(kmin == 0, bias_all, bias_first)
            m = jnp.max(s, axis=-1, keepdims=True)
            ol = jnp.dot(jnp.exp(s - m).astype(BF16), vcat1, preferred_element_type=F32)
            o2 = ol[:, 0:PAIR] / ol[:, PAIR:]
            ls = m + jnp.log(ol[:, PAIR:])
            o_ref[_rows(start, rate), :] = jnp.where(lo, o2[0:NBACK], o2[NBACK:])
            l_ref[_rows(start, rate), :] = jnp.where(lo, ls[0:NBACK], ls[NBACK:])
            return carry

        lax.fori_loop(0, T // NBACK, step, 0, unroll=4)

    pm = pl.BlockSpec((None, T, PAIR), lambda p: (p, 0, 0))
    in_specs, args = [pm, pm, pm], [qn, kn, v]
    if dep is not None:
        in_specs.append(HBM_SPEC)
        args.append(dep)
    return pl.pallas_call(
        body, name=name, grid=(NPAIR,), in_specs=in_specs, out_specs=[pm, pm],
        out_shape=[_sds((NPAIR, T, PAIR), F32)] * 2, compiler_params=_cparams(1))(*args)


def _merge_fwd(cat_ab, outs, lses, tm=512):
    def body(cat_in, o0, o1, o2, l0, l1, l2, cat_ref, lse_ref):
        del cat_in
        for p in range(NPAIR):
            a0, a1, a2 = l0[p], l1[p], l2[p]
            m = jnp.maximum(jnp.maximum(a0, a1), a2)
            w0, w1, w2 = jnp.exp(a0 - m), jnp.exp(a1 - m), jnp.exp(a2 - m)
            s = w0 + w1 + w2
            b = (w0 * o0[p] + w1 * o1[p] + w2 * o2[p]) / s
            cat_ref[:, p * PAIR:(p + 1) * PAIR] = b.astype(BF16)
            lse_ref[p] = m + jnp.log(s)

    pm = pl.BlockSpec((NPAIR, tm, PAIR), lambda i: (0, i, 0))
    return pl.pallas_call(
        body, name="merge_fwd", grid=(T // tm,),
        in_specs=[pl.BlockSpec(memory_space=pl.ANY)] + [pm] * 6,
        out_specs=[pl.BlockSpec((tm, 512), lambda i: (i, 1)), pm],
        out_shape=[_sds((T, D), BF16), _sds((NPAIR, T, PAIR), F32)],
        input_output_aliases={0: 0}, compiler_params=_cparams(1))(cat_ab, *outs, *lses)


def _b_pre_bwd(dcat, cat, tm=512):
    def body(db_ref, b_ref, dbp_ref, e_ref):
        lo = _lo_mask((tm, PAIR))
        for p in range(NPAIR):
            db = db_ref[:, p * PAIR:(p + 1) * PAIR].astype(F32)
            b = b_ref[:, p * PAIR:(p + 1) * PAIR].astype(F32)
            dbp_ref[p] = db
            e_ref[p] = _seg_mean(db * b, lo) * float(HEAD)

    pm = pl.BlockSpec((NPAIR, tm, PAIR), lambda i: (0, i, 0))
    right = pl.BlockSpec((tm, 512), lambda i: (i, 1))
    return pl.pallas_call(
        body, name="b_pre_bwd", grid=(T // tm,), in_specs=[right, right], out_specs=[pm, pm],
        out_shape=[_sds((NPAIR, T, PAIR), F32)] * 2, compiler_params=_cparams(1))(dcat, cat)


def _attn_bwd(qn, kn, v, dbp, e, lse, rate, name):
    def body(q_ref, k_ref, v_ref, db_ref, e_ref, lse_ref, dq_ref, dk_ref, dv_ref):
        lo = _lo_mask((NBACK, PAIR))
        bias_all, bias_first = _band_bias()
        scale = HEAD ** -0.5
        nt = (((1,), (1,)), ((), ()))
        tn = (((0,), (0,)), ((), ()))
        window = NBACK * rate
        nblk = T // window

        def one(it, carry):
            dk_carry, dv_carry = carry
            rho = it // nblk
            b = it % nblk
            start = b * window + rho
            rq = _rows(start, rate)
            rp = _rows(jnp.maximum(start - window, rho), rate)
            q = q_ref[rq, :] * scale
            db = db_ref[rq, :]
            ev = e_ref[rq, :]
            ls = lse_ref[rq, :]
            kcat = jnp.concatenate([k_ref[rp, :], k_ref[rq, :]], axis=0).astype(BF16)
            vcat = jnp.concatenate([v_ref[rp, :], v_ref[rq, :]], axis=0).astype(BF16)
            q2 = jnp.concatenate([jnp.where(lo, q, 0.0), jnp.where(lo, 0.0, q)], axis=0).astype(BF16)
            db2 = jnp.concatenate([jnp.where(lo, db, 0.0), jnp.where(lo, 0.0, db)], axis=0).astype(BF16)
            ls2 = jnp.concatenate([ls[:, 0:1], ls[:, HEAD:HEAD + 1]], axis=0)
            ev2 = jnp.concatenate([ev[:, 0:1], ev[:, HEAD:HEAD + 1]], axis=0)
            s = lax.dot_general(q2, kcat, nt, preferred_element_type=F32)
            pt = jnp.exp(s + jnp.where(b > 0, bias_all, bias_first) - ls2)
            dp = lax.dot_general(db2, vcat, nt, preferred_element_type=F32)
            ds = (pt * (dp - ev2)).astype(BF16)
            dq2 = jnp.dot(ds, kcat, preferred_element_type=F32) * scale
            dkc = lax.dot_general(ds, q2, tn, preferred_element_type=F32)
            dvc = lax.dot_general(pt.astype(BF16), db2, tn, preferred_element_type=F32)
            dq_ref[rq, :] = jnp.where(lo, dq2[0:NBACK], dq2[NBACK:])
            dk_ref[rp, :] = dk_carry + dkc[0:NBACK]
            dk_ref[rq, :] = dkc[NBACK:]
            dv_ref[rp, :] = dv_carry + dvc[0:NBACK]
            dv_ref[rq, :] = dvc[NBACK:]
            return dkc[NBACK:], dvc[NBACK:]

        def step(i, carry):
            for u in range(ATTN_BWD_UNROLL):
                carry = one(i * ATTN_BWD_UNROLL + u, carry)
            return carry

        zero = jnp.zeros((NBACK, PAIR), F32)
        lax.fori_loop(0, T // NBACK // ATTN_BWD_UNROLL, step, (zero, zero))

    pm = pl.BlockSpec((None, T, PAIR), lambda p: (p, 0, 0))
    return pl.pallas_call(
        body, name=name, grid=(NPAIR,), in_specs=[pm] * 6, out_specs=[pm] * 3,
        out_shape=[_sds((NPAIR, T, PAIR), F32)] * 3, compiler_params=_cparams(1, 56))(qn, kn, v, dbp, e, lse)


def _ab_in_bwd(pab, dcat, sgu_g, sgu_b, sgu_w, sgu_bias3, qg, kg, tabs, dqkv, rinvs, tm=256):
    def body(p_ref, dcat_ref, g_ref, b_ref, w_ref, bias_ref, qg_ref, kg_ref, c_ref, s1_ref, s2_ref, *rest):
        dq_refs, rinv_refs = rest[0:9], rest[9:15]
        o_ref, dwm_ref, dbias_ref, dsg_ref, dsb_ref, dgain_ref = rest[15:]
        i = pl.program_id(0)

        @pl.when(i == 0)
        def _():
            dwm_ref[...] = jnp.zeros_like(dwm_ref)
            dbias_ref[...] = jnp.zeros_like(dbias_ref)
            dsg_ref[...] = jnp.zeros_like(dsg_ref)
            dsb_ref[...] = jnp.zeros_like(dsb_ref)
            dgain_ref[...] = jnp.zeros_like(dgain_ref)

        zu = p_ref[:, 0:512].astype(F32)
        zv = p_ref[:, 512:1024].astype(F32)
        u = _gelu(zu)
        v = _gelu(zv)
        mu = jnp.mean(v, axis=-1, keepdims=True)
        vc = v - mu
        rstd = lax.rsqrt(jnp.mean(vc * vc, axis=-1, keepdims=True) + EPS)
        xhat = vc * rstd
        vn = (xhat * g_ref[...] + b_ref[...]).astype(BF16)
        da = dcat_ref[...].astype(F32)
        tri = _tril_mask()
        du_parts = [[None] * 4 for _ in range(tm // 128)]
        dvn_parts = [[None] * 4 for _ in range(tm // 128)]
        for gi in range(4):
            wg = jnp.where(tri, w_ref[gi], 0.0).astype(BF16)
            bg = bias_ref[gi]
            for c in range(tm // 128):
                rs, cs = slice(c * 128, (c + 1) * 128), slice(gi * 128, (gi + 1) * 128)
                vblk = vn[rs, cs]
                mixed = jnp.dot(wg, vblk, preferred_element_type=F32) + bg
                dab = da[rs, cs]
                du_parts[c][gi] = dab * mixed
                dmixed = dab * u[rs, cs]
                dmb = dmixed.astype(BF16)
                dvn_parts[c][gi] = lax.dot_general(wg, dmb, (((0,), (0,)), ((), ())), preferred_element_type=F32)
                dwm = lax.dot_general(dmb, vblk, (((1,), (1,)), ((), ())), preferred_element_type=F32)
                dwm_ref[gi] += jnp.where(tri, dwm, 0.0)
                dbias_ref[gi] += dmixed
        du = jnp.concatenate([jnp.concatenate(r, axis=1) for r in du_parts], axis=0)
        dvn = jnp.concatenate([jnp.concatenate(r, axis=1) for r in dvn_parts], axis=0)
        dsg_ref[...] += jnp.sum(dvn * xhat, axis=0, keepdims=True)
        dsb_ref[...] += jnp.sum(dvn, axis=0, keepdims=True)
        dxh = dvn * g_ref[...]
        dv = rstd * (dxh - jnp.mean(dxh, axis=-1, keepdims=True)
                     - xhat * jnp.mean(dxh * xhat, axis=-1, keepdims=True))
        o_ref[:, 0:512] = (du * _gelu_grad(zu)).astype(BF16)
        o_ref[:, 512:1024] = (dv * _gelu_grad(zv)).astype(BF16)

        lo = _lo_mask((tm, PAIR))
        c, s1, s2 = c_ref[...], s1_ref[...], s2_ref[...]
        for g in range(3):
            dq_ref, dk_ref, dv_ref = dq_refs[3 * g:3 * g + 3]
            for p in range(NPAIR):
                for which, gains, src in ((0, qg_ref, dq_ref), (1, kg_ref, dk_ref)):
                    col = (2 + 3 * which + g) * 512 + p * PAIR
                    xr = p_ref[:, col:col + PAIR].astype(F32)
                    rinv = rinv_refs[2 * g + which][p]
                    xh = xr * rinv
                    dn = _rope_t(src[p], c, s1, s2)
                    row = 2 * g + which
                    dgain_ref[row:row + 1, :] += jnp.sum(dn * xh, axis=0, keepdims=True)
                    dxh2 = dn * gains[g:g + 1, :]
                    dx = rinv * (dxh2 - xh * _seg_mean(dxh2 * xh, lo))
                    o_ref[:, col:col + PAIR] = dx.astype(BF16)
                col = (8 + g) * 512 + p * PAIR
                o_ref[:, col:col + PAIR] = dv_ref[p].astype(BF16)

    pm = pl.BlockSpec((NPAIR, tm, PAIR), lambda i: (0, i, 0))
    tab = pl.BlockSpec((tm, PAIR), lambda i: (i, 0))
    gain = pl.BlockSpec((3, PAIR), lambda i: (0, 0))
    vec = pl.BlockSpec((1, 512), lambda i: (0, 0))
    full = pl.BlockSpec((tm, AB_IN), lambda i: (i, 0))
    w4 = pl.BlockSpec((4, 128, 128), lambda i: (0, 0, 0))
    return pl.pallas_call(
        body, name="ab_in_bwd", grid=(T // tm,),
        in_specs=[full, pl.BlockSpec((tm, 512), lambda i: (i, 0)), vec, vec, w4,
                  pl.BlockSpec((4, 128, 1), lambda i: (0, 0, 0)), gain, gain, tab, tab, tab] + [pm] * 15,
        out_specs=[full, w4, w4, vec, vec, pl.BlockSpec((8, PAIR), lambda i: (0, 0))],
        out_shape=[_sds((T, AB_IN), BF16), _sds((4, 128, 128), F32), _sds((4, 128, 128), F32),
                   _sds((1, 512), F32), _sds((1, 512), F32), _sds((8, PAIR), F32)],
        compiler_params=_cparams(1))(pab, dcat, sgu_g, sgu_b, sgu_w, sgu_bias3, qg, kg, *tabs, *dqkv, *rinvs)


def _ln_stats(x):
    mu = jnp.mean(x, axis=-1, keepdims=True)
    xc = x - mu
    rstd = lax.rsqrt(jnp.mean(xc * xc, axis=-1, keepdims=True) + EPS)
    return xc * rstd, rstd


CONV_RC = 64


def _shifted_copies(src, dst, tm):
    dst[0] = src[...]
    for b in range(1, 8):
        dst[b, 0:tm + HALO - 8, :] = src[pl.ds(b, tm + HALO - 8), :]


def _offsets_by_phase(first):
    groups = {}
    for o in range(first, first + CONV_C_TAPS):
        groups.setdefault(o % 8, []).append(o)
    return sorted(groups.items())


def _window(shifted, b8, base, offsets, lanes):
    rows = 8 * (max(offsets) // 8) + CONV_RC
    return shifted[b8, pl.ds(base, rows), lanes].reshape(rows // 8, 8, 128)


def _cd_fwd(pcd, cw, cb, lg, lb, dw, tm=512):
    per = tm // HALO

    def body(p_ref, h_ref, cw_ref, cb_ref, lg_ref, lb_ref, dw_ref, cat_ref, c0_ref, c1_ref, dd_ref, y_ref,
             buf, buf2, sb):
        i = pl.program_id(0)
        live = jnp.where(i > 0, 1.0, 0.0)
        a = p_ref[:, 0:512].astype(F32)
        gt = p_ref[:, 512:1024].astype(F32)
        gb = p_ref[:, 1024:1536].astype(F32)
        gc = p_ref[:, 1536:2048].astype(F32)
        hv = p_ref[:, 2048:2560].astype(F32)
        c0 = a * _sigmoid(gt)
        dd = gc * hv
        buf[0:HALO, :] = h_ref[:, 0:512].astype(F32) * _sigmoid(h_ref[:, 512:1024].astype(F32)) * live
        buf[HALO:, :] = c0
        buf2[0:HALO, :] = h_ref[:, 1536:2048].astype(F32) * h_ref[:, 2048:2560].astype(F32) * live
        buf2[HALO:, :] = dd
        c0_ref[...] = c0.astype(BF16)
        dd_ref[...] = dd.astype(BF16)
        _shifted_copies(buf, sb, tm)

        def conv_rows(r, carry):
            base = pl.multiple_of(r * CONV_RC, CONV_RC)
            for c in range(4):
                lanes = slice(c * 128, (c + 1) * 128)
                acc = jnp.broadcast_to(cb_ref[:, lanes], (CONV_RC // 8, 8, 128))
                for b8, offsets in _offsets_by_phase(HALO - (CONV_C_TAPS - 1)):
                    win = _window(sb, b8, base, offsets, lanes)
                    for o in offsets:
                        j = o - (HALO - (CONV_C_TAPS - 1))
                        acc = acc + cw_ref[8 * j:8 * j + 8, lanes] * win[o // 8:o // 8 + CONV_RC // 8]
                c1_ref[pl.ds(base, CONV_RC), lanes] = acc.reshape(CONV_RC, 128)
            return carry

        lax.fori_loop(0, tm // CONV_RC, conv_rows, 0)
        xhat, _ = _ln_stats(c1_ref[...])
        c2 = xhat * lg_ref[...] + lb_ref[...]
        y = jnp.zeros((tm, 512), F32)
        for j in range(CONV_D_TAPS):
            y = y + dw_ref[j:j + 1, :] * buf2[pl.ds(HALO - (CONV_D_TAPS - 1) + j, tm), :]
        cat_ref[:, 0:512] = (c2 * _sigmoid(c2)).astype(BF16)
        cat_ref[:, 512:1024] = (gb * y).astype(BF16)
        y_ref[...] = y.astype(BF16)

    half = pl.BlockSpec((tm, 512), lambda i: (i, 0))
    vec = pl.BlockSpec((1, 512), lambda i: (0, 0))
    return pl.pallas_call(
        body, name="cd_fwd", grid=(T // tm,),
        in_specs=[pl.BlockSpec((tm, CD_IN), lambda i: (i, 0)),
                  pl.BlockSpec((HALO, CD_IN), lambda i: (jnp.maximum(i * per - 1, 0), 0)),
                  pl.BlockSpec((8 * 32, 512), lambda i: (0, 0)), vec, vec, vec, pl.BlockSpec((8, 512), lambda i: (0, 0))],
        out_specs=[pl.BlockSpec((tm, D), lambda i: (i, 0)), half, half, half, half],
        out_shape=[_sds((T, D), BF16), _sds((T, 512), BF16), _sds((T, 512), F32), _sds((T, 512), BF16),
                   _sds((T, 512), BF16)],
        scratch_shapes=[pltpu.VMEM((HALO + tm, 512), F32), pltpu.VMEM((HALO + tm, 512), F32),
                        pltpu.VMEM((8, HALO + tm, 512), F32)],
        compiler_params=_cparams(1))(pcd, pcd, cw, cb, lg, lb, dw)


def _cd_bwd_pw(dcat, c1, pcd, y, lg, lb, tm=512):
    def body(dcat_ref, c1_ref, gb_ref, y_ref, lg_ref, lb_ref, dc1_ref, dy3_ref, dgb_ref, dlg_ref, dlb_ref, dcb_ref):
        i = pl.program_id(0)

        @pl.when(i == 0)
        def _():
            dlg_ref[...] = jnp.zeros_like(dlg_ref)
            dlb_ref[...] = jnp.zeros_like(dlb_ref)
            dcb_ref[...] = jnp.zeros_like(dcb_ref)

        dc = dcat_ref[:, 0:512].astype(F32)
        ddo = dcat_ref[:, 512:1024].astype(F32)
        xhat, rstd = _ln_stats(c1_ref[...])
        c2 = xhat * lg_ref[...] + lb_ref[...]
        sg = _sigmoid(c2)
        dc2 = dc * sg * (1.0 + c2 * (1.0 - sg))
        dlg_ref[...] += jnp.sum(dc2 * xhat, axis=0, keepdims=True)
        dlb_ref[...] += jnp.sum(dc2, axis=0, keepdims=True)
        dxh = dc2 * lg_ref[...]
        dc1 = rstd * (dxh - jnp.mean(dxh, axis=-1, keepdims=True)
                      - xhat * jnp.mean(dxh * xhat, axis=-1, keepdims=True))
        dcb_ref[...] += jnp.sum(dc1, axis=0, keepdims=True)
        dc1_ref[...] = dc1
        dgb_ref[...] = (ddo * y_ref[...].astype(F32)).astype(BF16)
        dy3_ref[...] = ddo * gb_ref[...].astype(F32)

    half = pl.BlockSpec((tm, 512), lambda i: (i, 0))
    vec = pl.BlockSpec((1, 512), lambda i: (0, 0))
    return pl.pallas_call(
        body, name="cd_bwd_pw", grid=(T // tm,),
        in_specs=[pl.BlockSpec((tm, D), lambda i: (i, 0)), half, pl.BlockSpec((tm, 512), lambda i: (i, 2)), half,
                  vec, vec],
        out_specs=[half, half, half, vec, vec, vec],
        out_shape=[_sds((T, 512), F32), _sds((T, 512), F32), _sds((T, 512), BF16),
                   _sds((1, 512), F32), _sds((1, 512), F32), _sds((1, 512), F32)],
        compiler_params=_cparams(1))(dcat, c1, pcd, y, lg, lb)


def _cd_bwd_conv(pcd, dc1, dy3, c0, dd, dgb, cw8, dw, tm=256):
    per = tm // HALO
    nblk = T // tm
    last32 = T // HALO - 1

    def body(p_ref, dc1_ref, dc1n_ref, dy3_ref, dy3n_ref, c0_ref, dd_ref, dgb_ref, cw_ref, dw_ref,
             o_ref, dcw_ref, ddw_ref, dbuf, d3buf, sd, dc0_buf):
        i = pl.program_id(0)
        has_next = jnp.where(i < nblk - 1, 1.0, 0.0)

        @pl.when(i == 0)
        def _():
            dcw_ref[...] = jnp.zeros_like(dcw_ref)
            ddw_ref[...] = jnp.zeros_like(ddw_ref)

        dbuf[0:tm, :] = dc1_ref[...]
        dbuf[tm:, :] = dc1n_ref[...] * has_next
        d3buf[0:tm, :] = dy3_ref[...]
        d3buf[tm:, :] = dy3n_ref[...] * has_next
        _shifted_copies(dbuf, sd, tm)
        n_tiles = tm // CONV_RC

        phases = _offsets_by_phase(0)

        def dc0_rows(r, carry):
            base = pl.multiple_of(r * CONV_RC, CONV_RC)
            for c in range(4):
                lanes = slice(c * 128, (c + 1) * 128)
                acc = jnp.zeros((CONV_RC // 8, 8, 128), F32)
                for b8, offsets in phases:
                    win = _window(sd, b8, base, offsets, lanes)
                    for o in offsets:
                        j = CONV_C_TAPS - 1 - o
                        acc = acc + cw_ref[8 * j:8 * j + 8, lanes] * win[o // 8:o // 8 + CONV_RC // 8]
                dc0_buf[pl.ds(base, CONV_RC), lanes] = acc.reshape(CONV_RC, 128)
            return carry

        lax.fori_loop(0, n_tiles, dc0_rows, 0)

        for c in range(4):
            lanes = slice(c * 128, (c + 1) * 128)
            for b8, offsets in phases:
                def dw_rows(r, accs, lanes=lanes, b8=b8, offsets=offsets):
                    base = pl.multiple_of(r * CONV_RC, CONV_RC)
                    xin = c0_ref[pl.ds(base, CONV_RC), lanes].astype(F32).reshape(CONV_RC // 8, 8, 128)
                    win = _window(sd, b8, base, offsets, lanes)
                    return tuple(acc + jnp.sum(xin * win[o // 8:o // 8 + CONV_RC // 8], axis=0)
                                 for acc, o in zip(accs, offsets))

                accs = lax.fori_loop(0, n_tiles, dw_rows, tuple(jnp.zeros((8, 128), F32) for _ in offsets))
                for acc, o in zip(accs, offsets):
                    j = CONV_C_TAPS - 1 - o
                    dcw_ref[j:j + 1, lanes] += jnp.sum(acc, axis=0, keepdims=True)

        dc0 = dc0_buf[...]
        ddin = dd_ref[...].astype(F32)
        ddd = jnp.zeros((tm, 512), F32)
        for j in range(CONV_D_TAPS):
            dy_shift = d3buf[pl.ds(CONV_D_TAPS - 1 - j, tm), :]
            ddd = ddd + dw_ref[j:j + 1, :] * dy_shift
            ddw_ref[j:j + 1, :] += jnp.sum(ddin * dy_shift, axis=0, keepdims=True)

        a = p_ref[:, 0:512].astype(F32)
        gt = p_ref[:, 512:1024].astype(F32)
        gc = p_ref[:, 1536:2048].astype(F32)
        hv = p_ref[:, 2048:2560].astype(F32)
        sg = _sigmoid(gt)
        o_ref[:, 0:512] = (dc0 * sg).astype(BF16)
        o_ref[:, 512:1024] = (dc0 * a * sg * (1.0 - sg)).astype(BF16)
        o_ref[:, 1024:1536] = dgb_ref[...]
        o_ref[:, 1536:2048] = (ddd * hv).astype(BF16)
        o_ref[:, 2048:2560] = (ddd * gc).astype(BF16)

    half = pl.BlockSpec((tm, 512), lambda i: (i, 0))
    nxt = pl.BlockSpec((HALO, 512), lambda i: (jnp.minimum((i + 1) * per, last32), 0))
    full = pl.BlockSpec((tm, CD_IN), lambda i: (i, 0))
    return pl.pallas_call(
        body, name="cd_bwd_conv", grid=(nblk,),
        in_specs=[full, half, nxt, half, nxt, half, half, half,
                  pl.BlockSpec((8 * 32, 512), lambda i: (0, 0)), pl.BlockSpec((8, 512), lambda i: (0, 0))],
        out_specs=[full, pl.BlockSpec((32, 512), lambda i: (0, 0)), pl.BlockSpec((8, 512), lambda i: (0, 0))],
        out_shape=[_sds((T, CD_IN), BF16), _sds((32, 512), F32), _sds((8, 512), F32)],
        scratch_shapes=[pltpu.VMEM((tm + HALO, 512), F32), pltpu.VMEM((tm + HALO, 512), F32),
                        pltpu.VMEM((8, tm + HALO, 512), F32), pltpu.VMEM((tm, 512), F32)],
        compiler_params=_cparams(1))(pcd, dc1, dc1, dy3, dy3, c0, dd, dgb, cw8, dw)


def _local_step(x, tgt, W, fetch=None, on_grad=None):
    W = dict(W)
    if fetch is None:
        fetch = lambda stage, after: {}
    if on_grad is None:
        on_grad = lambda key, arr: None
    tabs = _rope_tables()
    qg = jnp.tile(W["q_norm_g"], (1, 2))
    kg = jnp.tile(W["k_norm_g"], (1, 2))
    bias3 = W["sgu_bias"].reshape(4, 128, 1)
    cw8 = jnp.repeat(W["conv_c_w32"], 8, axis=0)
    G = {}

    h0 = _rms_fwd(x, W["ab_norm_g"], "rms_fwd_ab")
    pab = _mm_nt(h0, W["wt_ab_in"], "mm_ab_in", dep=W.get("dep0"))
    cat_ab = _mix_a_fwd(pab, W["sgu_norm_g"], W["sgu_norm_b"], W["sgu_w"], bias3)
    qkv, rinvs = _prep_fwd(pab, qg, kg, tabs)
    outs, lses = [], []
    for g, rate in enumerate(DIL_RATES):
        o, l = _attn_fwd(qkv[3 * g], qkv[3 * g + 1], qkv[3 * g + 2], rate, f"attn_fwd_{g}", dep=W.get(f"dep_attn{g}"))
        outs.append(o)
        lses.append(l)
        W.update(fetch(f"attn{g}", o))
    cat_ab, lse = _merge_fwd(cat_ab, outs, lses)
    W.update(fetch("ab_out", lse))
    x1, h1 = _mm_nn(cat_ab, W["w_ab_out"], "mm_ab_out", mode="rms", resid=x, gain=W["ffn_norm_g"][0:1])
    pf0, act0 = _ffn_in(h1, W["wt_ffn_in0"], "ffn_in0")
    W.update(fetch("ffn_down0", act0))
    x2, h2 = _mm_nn(act0, W["w_ffn_down0"], "mm_ffn_down0", mode="rms", resid=x1, gain=W["cd_norm_g"],
                    dep=W.get("dep_down0"))
    W.update(fetch("cd_in", h2))
    pcd = _mm_nt(h2, W["wt_cd_in"], "mm_cd_in")
    cat_cd, c0, c1, dd, yv = _cd_fwd(pcd, cw8, W["conv_c_b"], W["c_ln_g"], W["c_ln_b"], W["conv_d_w8"])
    x3, h3 = _mm_nn(cat_cd, W["w_cd_out"], "mm_cd_out", mode="rms", resid=x2, gain=W["ffn_norm_g"][1:2])
    pf1, act1 = _ffn_in(h3, W["wt_ffn_in1"], "ffn_in1")
    dy, dyb, loss_cols = _mm_nn(act1, W["w_ffn_down1"], "mm_ffn_down1", mode="loss", resid=x3, tgt=tgt)

    def ffn_bwd(xin, h, pf, act, dres, dresb, layer):
        G[f"w_ffn_down{layer}"] = _mm_tn(act, dresb, f"mm_g_ffn_down{layer}")
        dep = on_grad(f"w_ffn_down{layer}", G[f"w_ffn_down{layer}"])
        dpf = _ffn_dact(dresb, W[f"w_ffn_down{layer}"], pf, f"ffn_dact{layer}", dep=dep)
        G[f"wt_ffn_in{layer}"] = _mm_tn(dpf, h, f"mm_g_ffn_in{layer}")
        dep = on_grad(f"wt_ffn_in{layer}", G[f"wt_ffn_in{layer}"])
        dx, dxb, G[f"ffn_norm_g{layer}"] = _mm_dh_rms_bwd(
            dpf, W[f"wt_ffn_in{layer}"], xin, W["ffn_norm_g"][layer:layer + 1], dres, f"mm_d_h_ffn{layer}", dep=dep)
        return dx, dxb

    dx3, dx3b = ffn_bwd(x3, h3, pf1, act1, dy, dyb, 1)

    G["w_cd_out"] = _mm_tn(cat_cd, dx3b, "mm_g_cd_out")
    dep = on_grad("w_cd_out", G["w_cd_out"])
    dcat_cd = _mm_nt(dx3b, W["w_cd_out"], "mm_d_cat_cd", dep=dep)
    dc1, dy3, dgb, G["c_ln_g"], G["c_ln_b"], G["conv_c_b"] = _cd_bwd_pw(dcat_cd, c1, pcd, yv, W["c_ln_g"], W["c_ln_b"])
    dpcd, G["conv_c_w32"], G["conv_d_w8"] = _cd_bwd_conv(pcd, dc1, dy3, c0, dd, dgb, cw8, W["conv_d_w8"])
    G["wt_cd_in"] = _mm_tn(dpcd, h2, "mm_g_cd_in")
    dep = on_grad("wt_cd_in", G["wt_cd_in"])
    dx2, dx2b, G["cd_norm_g"] = _mm_dh_rms_bwd(dpcd, W["wt_cd_in"], x2, W["cd_norm_g"], dx3, "mm_d_h_cd", dep=dep)

    dx1, dx1b = ffn_bwd(x1, h1, pf0, act0, dx2, dx2b, 0)

    G["w_ab_out"] = _mm_tn(cat_ab, dx1b, "mm_g_ab_out")
    dep = on_grad("w_ab_out", G["w_ab_out"])
    dcat_ab = _mm_nt(dx1b, W["w_ab_out"], "mm_d_cat_ab", dep=dep)
    dbp, e = _b_pre_bwd(dcat_ab, cat_ab)
    dqkv = []
    for g, rate in enumerate(DIL_RATES):
        dqkv += _attn_bwd(qkv[3 * g], qkv[3 * g + 1], qkv[3 * g + 2], dbp, e, lse, rate, f"attn_bwd_{g}")
    dpab, G["sgu_w"], dbias_part, G["sgu_norm_g"], G["sgu_norm_b"], dgain = _ab_in_bwd(
        pab, dcat_ab, W["sgu_norm_g"], W["sgu_norm_b"], W["sgu_w"], bias3, qg, kg, tabs, dqkv, rinvs)
    G["sgu_bias"] = jnp.sum(dbias_part, axis=-1)
    dgain = dgain[0:6, 0:HEAD] + dgain[0:6, HEAD:PAIR]
    G["q_norm_g"] = dgain[0::2]
    G["k_norm_g"] = dgain[1::2]
    G["wt_ab_in"] = _mm_tn(dpab, h0, "mm_g_ab_in")
    dep = on_grad("wt_ab_in", G["wt_ab_in"])
    grad_x, _, G["ab_norm_g"] = _mm_dh_rms_bwd(dpab, W["wt_ab_in"], x, W["ab_norm_g"], dx1, "mm_d_h_ab", dep=dep)
    return loss_cols, grad_x, G


def _my_place():
    return lax.axis_index("x"), lax.axis_index("y"), lax.axis_index("c")


def _dev_index(px, py, pc):
    return 4 * px + 2 * py + pc


def _flip(place, k):
    x, y, c = place
    return (1 - x if k & 4 else x, 1 - y if k & 2 else y, 1 - c if k & 1 else c)


def _landing(shape, dtype, own):
    buf = lax.empty(shape, dtype)
    for lead, part in own:
        buf = lax.dynamic_update_slice(buf, part.reshape((1,) * len(lead) + part.shape),
                                       tuple(lead) + (0,) * part.ndim)
    return buf


def _gather_first(ab_in_t, small, me_index):
    lands = [_landing((NDEV,) + ab_in_t.shape, BF16, [((me_index,), ab_in_t)]),
             _landing((NDEV,) + small.shape, F32, [((me_index,), small)])]
    n_items = 2

    def body(ab_in_r, small_r, l_ab_in, l_small, o_ab_in, o_small, send_sems, recv_sems):
        del l_ab_in, l_small
        x, y, c = _my_place()
        me = (x, y, c)
        sib = (x, y, 1 - c)
        chips = [(1 - x, y), (x, 1 - y), (1 - x, 1 - y)]
        items = [(ab_in_r, lambda d: o_ab_in.at[d]), (small_r, lambda d: o_small.at[d])]

        def rcopy(it, k, src, dst, to):
            return pltpu.make_async_remote_copy(src_ref=src, dst_ref=dst, send_sem=send_sems.at[it, k],
                                                recv_sem=recv_sems.at[it, k], device_id=to, device_id_type=MESH)

        started = []
        for it, (src, dst) in enumerate(items):
            mine = dst(_dev_index(*me))
            first = [rcopy(it, 0, src, mine, sib)]
            first += [rcopy(it, 1 + j, src, mine, (*chip, c)) for j, chip in enumerate(chips)]
            for cp in first:
                cp.start()
            started += first
        for it, (src, dst) in enumerate(items):
            for j, chip in enumerate(chips):
                blk = dst(_dev_index(*chip, c))
                rcopy(it, 1 + j, blk, blk, me).wait_recv()
                fwd = rcopy(it, 4 + j, blk, blk, sib)
                fwd.start()
                started.append(fwd)
        for it, (src, dst) in enumerate(items):
            blk = dst(_dev_index(x, y, 1 - c))
            rcopy(it, 0, blk, blk, me).wait_recv()
            for j, chip in enumerate(chips):
                blk = dst(_dev_index(*chip, 1 - c))
                rcopy(it, 4 + j, blk, blk, me).wait_recv()
        for cp in started:
            cp.wait_send()

    return pl.pallas_call(
        body, name="gather_first", in_specs=[HBM_SPEC] * 4, out_specs=[HBM_SPEC] * 2,
        out_shape=[_sds(a.shape, a.dtype) for a in lands], input_output_aliases={2: 0, 3: 1},
        scratch_shapes=[pltpu.SemaphoreType.DMA((n_items, 7)), pltpu.SemaphoreType.DMA((n_items, 7))],
    )(ab_in_t, small, *lands)


HBM_ONLY = pl.BlockSpec(memory_space=pltpu.HBM)
SEM_SPEC = pl.BlockSpec(memory_space=pltpu.SEMAPHORE)
IN_FLIGHT = pltpu.CompilerParams(has_side_effects=pltpu.SideEffectType.DATAFLOW_SIDE_EFFECTING)


def _in_hbm(a):
    return pltpu.with_memory_space_constraint(a, pltpu.HBM)


def _exchange_start(name, srcs, lands, items, dep=None):
    ns, nl, ni = len(srcs), len(lands), len(items)

    def body(*refs):
        S, L = refs[0:ns], refs[ns:ns + nl]
        first_out = ns + nl + (0 if dep is None else 1)
        send_sems, recv_sems, token = refs[first_out], refs[first_out + 1], refs[-1]
        me = _my_place()
        mi = _dev_index(*me)
        for i, (src, dst) in enumerate(items):
            for k in range(1, NDEV):
                peer = _flip(me, k)
                pltpu.make_async_remote_copy(
                    src_ref=src(S, _dev_index(*peer)), dst_ref=dst(L, mi), send_sem=send_sems.at[7 * i + k - 1],
                    recv_sem=recv_sems.at[7 * i + k - 1], device_id=peer, device_id_type=MESH).start()
        token[...] = jnp.zeros_like(token)

    thru = [pltpu.HBM(a.shape, a.dtype) for a in list(srcs) + list(lands)]
    args = [_in_hbm(a) for a in srcs] + [_in_hbm(a) for a in lands]
    in_specs = [HBM_ONLY] * (ns + nl)
    if dep is not None:
        args.append(dep)
        in_specs.append(HBM_SPEC)
    outs = pl.pallas_call(
        body, name=name, in_specs=in_specs,
        out_shape=(pltpu.SemaphoreType.DMA((7 * ni,)), pltpu.SemaphoreType.DMA((7 * ni,)), *thru, _sds((8, 128), F32)),
        out_specs=(SEM_SPEC, SEM_SPEC, *[HBM_ONLY] * (ns + nl), pl.BlockSpec(memory_space=pltpu.VMEM)),
        input_output_aliases={j: 2 + j for j in range(ns + nl)}, compiler_params=IN_FLIGHT)(*args)
    return dict(send=outs[0], recv=outs[1], srcs=list(outs[2:2 + ns]), lands=list(outs[2 + ns:2 + ns + nl]),
                token=outs[-1], items=items)


def _exchange_wait(name, states, after):
    after = list(after) if isinstance(after, (list, tuple)) else [after]
    counts = [(len(st["srcs"]), len(st["lands"]), len(st["items"])) for st in states]
    n_arrays = sum(c[0] + c[1] for c in counts)

    def body(*refs):
        me = _my_place()
        mi = _dev_index(*me)
        pos = 0
        sem_pos = n_arrays
        for st, (ns, nl, ni) in zip(states, counts):
            S, L = refs[pos:pos + ns], refs[pos + ns:pos + ns + nl]
            send_sems, recv_sems = refs[sem_pos], refs[sem_pos + 1]
            pos += ns + nl
            sem_pos += 2
            for i, (src, dst) in enumerate(st["items"]):
                for k in range(1, NDEV):
                    cp = pltpu.make_async_remote_copy(
                        src_ref=src(S, mi), dst_ref=dst(L, mi), send_sem=send_sems.at[7 * i + k - 1],
                        recv_sem=recv_sems.at[7 * i + k - 1], device_id=me, device_id_type=MESH)
                    cp.wait_send()
                    cp.wait_recv()

    arrays, sems = [], []
    for st in states:
        arrays += st["srcs"] + st["lands"]
        sems += [st["send"], st["recv"]]
    outs = pl.pallas_call(
        body, name=name, in_specs=[HBM_ONLY] * n_arrays + [SEM_SPEC] * len(sems) + [HBM_SPEC] * len(after),
        out_shape=tuple(pltpu.HBM(a.shape, a.dtype) for a in arrays), out_specs=tuple([HBM_ONLY] * n_arrays),
        input_output_aliases={j: j for j in range(n_arrays)}, compiler_params=IN_FLIGHT)(*arrays, *sems, *after)
    lands, pos = [], 0
    for ns, nl, _ in counts:
        lands.append(list(outs[pos + ns:pos + ns + nl]))
        pos += ns + nl
    return lands


def _place_and_neighbours():
    x, y, c = _my_place()
    return (x, y, c), (x, y, 1 - c), [(1 - x, y), (x, 1 - y), (1 - x, 1 - y)]


def _gather_start(name, srcs, lands, items, dep=None):
    ns, nl, ni = len(srcs), len(lands), len(items)

    def body(*refs):
        S, L = refs[0:ns], refs[ns:ns + nl]
        first_out = ns + nl + (0 if dep is None else 1)
        send_sems, recv_sems, token = refs[first_out], refs[first_out + 1], refs[-1]
        me, sib, chips = _place_and_neighbours()
        mi = _dev_index(*me)
        for i, (src, dst) in enumerate(items):
            for k, to in enumerate([sib] + [(*chip, me[2]) for chip in chips]):
                pltpu.make_async_remote_copy(
                    src_ref=src(S), dst_ref=dst(L, mi), send_sem=send_sems.at[4 * i + k],
                    recv_sem=recv_sems.at[4 * i + k], device_id=to, device_id_type=MESH).start()
        token[...] = jnp.zeros_like(token)

    thru = [pltpu.HBM(a.shape, a.dtype) for a in list(srcs) + list(lands)]
    args = [_in_hbm(a) for a in srcs] + [_in_hbm(a) for a in lands]
    in_specs = [HBM_ONLY] * (ns + nl)
    if dep is not None:
        args.append(dep)
        in_specs.append(HBM_SPEC)
    outs = pl.pallas_call(
        body, name=name, in_specs=in_specs,
        out_shape=(pltpu.SemaphoreType.DMA((4 * ni,)), pltpu.SemaphoreType.DMA((4 * ni,)), *thru, _sds((8, 128), F32)),
        out_specs=(SEM_SPEC, SEM_SPEC, *[HBM_ONLY] * (ns + nl), pl.BlockSpec(memory_space=pltpu.VMEM)),
        input_output_aliases={j: 2 + j for j in range(ns + nl)}, compiler_params=IN_FLIGHT)(*args)
    return dict(send=outs[0], recv=outs[1], srcs=list(outs[2:2 + ns]), lands=list(outs[2 + ns:2 + ns + nl]),
                token=outs[-1], items=items)


def _gather_forward(name, st, after):
    nl, ni = len(st["lands"]), len(st["items"])

    def body(*refs):
        L, recv_sems = refs[0:nl], refs[nl]
        fwd_send, fwd_recv, token = refs[2 * nl + 2], refs[2 * nl + 3], refs[-1]
        me, sib, chips = _place_and_neighbours()
        for i, (_, dst) in enumerate(st["items"]):
            for j, chip in enumerate(chips):
                blk = dst(L, _dev_index(*chip, me[2]))
                pltpu.make_async_remote_copy(
                    src_ref=blk, dst_ref=blk, send_sem=fwd_send.at[3 * i + j], recv_sem=recv_sems.at[4 * i + 1 + j],
                    device_id=me, device_id_type=MESH).wait_recv()
                pltpu.make_async_remote_copy(
                    src_ref=blk, dst_ref=blk, send_sem=fwd_send.at[3 * i + j], recv_sem=fwd_recv.at[3 * i + j],
                    device_id=sib, device_id_type=MESH).start()
        token[...] = jnp.zeros_like(token)

    outs = pl.pallas_call(
        body, name=name, in_specs=[HBM_ONLY] * nl + [SEM_SPEC, HBM_SPEC],
        out_shape=(*[pltpu.HBM(a.shape, a.dtype) for a in st["lands"]], pltpu.SemaphoreType.DMA((3 * ni,)),
                   pltpu.SemaphoreType.DMA((3 * ni,)), _sds((8, 128), F32)),
        out_specs=(*[HBM_ONLY] * nl, SEM_SPEC, SEM_SPEC, pl.BlockSpec(memory_space=pltpu.VMEM)),
        input_output_aliases={j: j for j in range(nl)}, compiler_params=IN_FLIGHT)(*st["lands"], st["recv"], after)
    return dict(st, lands=list(outs[0:nl]), fwd_send=outs[nl], fwd_recv=outs[nl + 1], token=outs[-1])


def _gather_wait(name, st, after):
    ns, nl, ni = len(st["srcs"]), len(st["lands"]), len(st["items"])

    def body(*refs):
        S, L = refs[0:ns], refs[ns:ns + nl]
        send_sems, recv_sems, fwd_send, fwd_recv = refs[ns + nl:ns + nl + 4]
        me, sib, chips = _place_and_neighbours()
        mi = _dev_index(*me)
        for i, (src, dst) in enumerate(st["items"]):
            mine = dst(L, mi)
            for k in range(4):
                pltpu.make_async_remote_copy(
                    src_ref=src(S), dst_ref=mine, send_sem=send_sems.at[4 * i + k], recv_sem=recv_sems.at[4 * i + k],
                    device_id=me, device_id_type=MESH).wait_send()
            pltpu.make_async_remote_copy(
                src_ref=src(S), dst_ref=mine, send_sem=send_sems.at[4 * i], recv_sem=recv_sems.at[4 * i],
                device_id=me, device_id_type=MESH).wait_recv()
            for j in range(3):
                cp = pltpu.make_async_remote_copy(
                    src_ref=mine, dst_ref=mine, send_sem=fwd_send.at[3 * i + j], recv_sem=fwd_recv.at[3 * i + j],
                    device_id=me, device_id_type=MESH)
                cp.wait_send()
                cp.wait_recv()

    arrays = st["srcs"] + st["lands"]
    outs = pl.pallas_call(
        body, name=name, in_specs=[HBM_ONLY] * (ns + nl) + [SEM_SPEC] * 4 + [HBM_SPEC],
        out_shape=tuple(pltpu.HBM(a.shape, a.dtype) for a in arrays), out_specs=tuple([HBM_ONLY] * (ns + nl)),
        input_output_aliases={j: j for j in range(ns + nl)},
        compiler_params=IN_FLIGHT)(*arrays, st["send"], st["recv"], st["fwd_send"], st["fwd_recv"], after)
    return list(outs[ns:ns + nl])


def _sum_slots(land):
    def body(l_ref, o_ref):
        acc = l_ref[0]
        for d in range(1, NDEV):
            acc = acc + l_ref[d]
        o_ref[...] = acc

    vm = pl.BlockSpec(memory_space=pltpu.VMEM)
    return pl.pallas_call(body, name="sum_small", out_shape=_sds(land.shape[1:], F32), in_specs=[vm], out_specs=vm)(land)


def _adam_math(w, g, m, v):
    m2 = ADAM_B1 * m + (1.0 - ADAM_B1) * g
    v2 = ADAM_B2 * v + (1.0 - ADAM_B2) * (g * g)
    delta = -ADAM_LR * ((m2 * ADAM_C1) / (jnp.sqrt(v2 * ADAM_C2) + ADAM_EPS) + ADAM_WD * w)
    return delta, m2, v2


def _adam_layer(land, sel, w, m, v, layer, name, prev=None, tc=512):
    R = land.shape[2]

    def body(l_ref, w_ref, m_ref, v_ref, *rest):
        g_out, d_out, m_out, v_out = rest[-4:]
        g = l_ref[0].astype(F32)
        for d in range(1, NDEV):
            g = g + l_ref[d].astype(F32)
        delta, m2, v2 = _adam_math(w_ref[...], g, m_ref[...], v_ref[...])
        g_out[...] = g
        d_out[...] = delta
        m_out[...] = m2
        v_out[...] = v2

    wspec = pl.BlockSpec((None, R, tc), lambda i: (layer, 0, i))
    in_specs = [pl.BlockSpec((None, NDEV, R, tc), lambda i: (sel, 0, 0, i)), wspec, wspec, wspec]
    args = [land, w, m, v]
    aliases = {}
    if prev is not None:
        in_specs += [HBM_SPEC] * 4
        args += list(prev)
        aliases = {4 + j: j for j in range(4)}
    return pl.pallas_call(
        body, name=name, grid=(D // tc,), in_specs=in_specs, out_specs=[wspec] * 4,
        out_shape=[_sds(w.shape, F32)] * 4, input_output_aliases=aliases, compiler_params=_cparams(1))(*args)


def _adam_stacked(lands, sel, w, m, v, name):
    res = None
    for layer, land in enumerate(lands):
        res = _adam_layer(land, sel, w, m, v, layer, f"{name}{layer}", prev=res)
    return res


def _adam_small(ws, gs, ms, vs):
    n = len(ws)

    def body(*refs):
        w_r, g_r, m_r, v_r = refs[0:n], refs[n:2 * n], refs[2 * n:3 * n], refs[3 * n:4 * n]
        d_o, m_o, v_o = refs[4 * n:5 * n], refs[5 * n:6 * n], refs[6 * n:7 * n]
        for i in range(n):
            delta, m2, v2 = _adam_math(w_r[i][...], g_r[i][...], m_r[i][...], v_r[i][...])
            d_o[i][...] = delta
            m_o[i][...] = m2
            v_o[i][...] = v2

    vm = pl.BlockSpec(memory_space=pltpu.VMEM)
    shapes = [_sds(w.shape, F32) for w in ws]
    outs = pl.pallas_call(body, name="adam_small", in_specs=[vm] * (4 * n), out_specs=[vm] * (3 * n),
                          out_shape=shapes * 3)(*ws, *gs, *ms, *vs)
    return outs[0:n], outs[n:2 * n], outs[2 * n:3 * n]


WEIGHT_NAMES = ("ab_norm_g", "ab_w_in", "sgu_norm_g", "sgu_norm_b", "sgu_w", "sgu_bias", "q_norm_g", "k_norm_g",
                "ab_w_out", "cd_norm_g", "cd_w_in", "conv_c_w", "conv_c_b", "c_ln_g", "c_ln_b", "conv_d_w",
                "cd_w_out", "ffn_norm_g", "ffn_w_gate", "ffn_w_up", "ffn_w_down")
SMALL_2D = (("ab_norm_g", (1, 1024)), ("sgu_norm_g", (1, 512)), ("sgu_norm_b", (1, 512)), ("sgu_w", (512, 128)),
            ("sgu_bias", (4, 128)), ("q_norm_g", (3, 64)), ("k_norm_g", (3, 64)), ("cd_norm_g", (1, 128)),
            ("conv_c_w", (31, 64)), ("conv_c_b", (1, 64)), ("c_ln_g", (1, 64)), ("c_ln_b", (1, 64)),
            ("conv_d_w", (3, 64)), ("ffn_norm_g", (2, 1024)))
SHARD_C = 64


def _pack_rows(parts, rows):
    flat = jnp.concatenate([p.reshape(-1) for p in parts])
    return jnp.pad(flat, (0, rows * 128 - flat.shape[0])).reshape(rows, 128)


def kernel(x, ab_norm_g, ab_w_in, sgu_norm_g, sgu_norm_b, sgu_w, sgu_bias, q_norm_g, k_norm_g, ab_w_out, cd_norm_g, cd_w_in, conv_c_w, conv_c_b, c_ln_g, c_ln_b, conv_d_w, cd_w_out, ffn_norm_g, ffn_w_gate, ffn_w_up, ffn_w_down, loss_target, m_ab_norm_g, m_ab_w_in, m_sgu_norm_g, m_sgu_norm_b, m_sgu_w, m_sgu_bias, m_q_norm_g, m_k_norm_g, m_ab_w_out, m_cd_norm_g, m_cd_w_in, m_conv_c_w, m_conv_c_b, m_c_ln_g, m_c_ln_b, m_conv_d_w, m_cd_w_out, m_ffn_norm_g, m_ffn_w_gate, m_ffn_w_up, m_ffn_w_down, v_ab_norm_g, v_ab_w_in, v_sgu_norm_g, v_sgu_norm_b, v_sgu_w, v_sgu_bias, v_q_norm_g, v_k_norm_g, v_ab_w_out, v_cd_norm_g, v_cd_w_in, v_conv_c_w, v_conv_c_b, v_c_ln_g, v_c_ln_b, v_conv_d_w, v_cd_w_out, v_ffn_norm_g, v_ffn_w_gate, v_ffn_w_up, v_ffn_w_down):
    w = dict(zip(WEIGHT_NAMES, (ab_norm_g, ab_w_in, sgu_norm_g, sgu_norm_b, sgu_w, sgu_bias, q_norm_g, k_norm_g, ab_w_out, cd_norm_g, cd_w_in, conv_c_w, conv_c_b, c_ln_g, c_ln_b, conv_d_w, cd_w_out, ffn_norm_g, ffn_w_gate, ffn_w_up, ffn_w_down)))
    m = dict(zip(WEIGHT_NAMES, (m_ab_norm_g, m_ab_w_in, m_sgu_norm_g, m_sgu_norm_b, m_sgu_w, m_sgu_bias, m_q_norm_g, m_k_norm_g, m_ab_w_out, m_cd_norm_g, m_cd_w_in, m_conv_c_w, m_conv_c_b, m_c_ln_g, m_c_ln_b, m_conv_d_w, m_cd_w_out, m_ffn_norm_g, m_ffn_w_gate, m_ffn_w_up, m_ffn_w_down)))
    v = dict(zip(WEIGHT_NAMES, (v_ab_norm_g, v_ab_w_in, v_sgu_norm_g, v_sgu_norm_b, v_sgu_w, v_sgu_bias, v_q_norm_g, v_k_norm_g, v_ab_w_out, v_cd_norm_g, v_cd_w_in, v_conv_c_w, v_conv_c_b, v_c_ln_g, v_c_ln_b, v_conv_d_w, v_cd_w_out, v_ffn_norm_g, v_ffn_w_gate, v_ffn_w_up, v_ffn_w_down)))
    me = _dev_index(*_my_place())

    small_local = _pack_rows([w["cd_norm_g"], w["conv_c_w"], w["conv_c_b"], w["c_ln_g"], w["c_ln_b"], w["conv_d_w"]], 24)
    o_ab_in, o_small = _gather_first(w["ab_w_in"][0].T.astype(BF16), small_local, me)
    r_ff = DFF // NDEV
    one = lambda a: (lambda S, j: S[a])
    slot = lambda b: (lambda L, s: L[b].at[s])
    slot2 = lambda b, part: (lambda L, s: L[b].at[part, s])
    shard = lambda a: (lambda S: S[a])

    def layer_shards(layer):
        return (w["ffn_w_gate"][layer].T.astype(BF16), w["ffn_w_up"][layer].T.astype(BF16),
                w["ffn_w_down"][layer].astype(BF16))

    def gathered(own):
        return _landing((NDEV,) + own.shape, BF16, [((me,), own)])

    def gathered2(a, b):
        return _landing((2, NDEV) + a.shape, BF16, [((0, me), a), ((1, me), b)])

    ab_out_s = w["ab_w_out"][0].astype(BF16)
    gate0, up0, down0 = layer_shards(0)
    gathers = {1: _gather_start(
        "gather1_start", [ab_out_s, gate0, up0, down0], [gathered(ab_out_s), gathered2(gate0, up0), gathered(down0)],
        [(shard(0), slot(0)), (shard(1), slot2(1, 0)), (shard(2), slot2(1, 1)), (shard(3), slot(2))], dep=o_small)}

    def fetch(stage, after):
        if stage == "attn0":
            cd_in_s, cd_out_s = w["cd_w_in"][0].T.astype(BF16), w["cd_w_out"][0].astype(BF16)
            gate1, up1, down1 = layer_shards(1)
            gathers[2] = _gather_start(
                "gather2_start", [cd_in_s, cd_out_s, gate1, up1, down1],
                [gathered(cd_in_s), gathered(cd_out_s), gathered2(gate1, up1), gathered(down1)],
                [(shard(0), slot(0)), (shard(1), slot(1)), (shard(2), slot2(2, 0)), (shard(3), slot2(2, 1)),
                 (shard(4), slot(3))], dep=after)
            return {"dep_attn1": gathers[2]["token"]}
        if stage == "attn1":
            gathers[1] = _gather_forward("gather1_forward", gathers[1], after)
            return {"dep_attn2": gathers[1]["token"]}
        if stage == "ab_out":
            l_out, l_ffn, l_down = _gather_wait("gather1_wait", gathers[1], after)
            return {"w_ab_out": l_out.reshape(D, D), "wt_ffn_in0": l_ffn.reshape(2 * DFF, D),
                    "w_ffn_down0": l_down.reshape(DFF, D)}
        if stage == "ffn_down0":
            gathers[2] = _gather_forward("gather2_forward", gathers[2], after)
            return {"dep_down0": gathers[2]["token"]}
        if stage == "cd_in":
            l_in, l_out, l_ffn, l_down = _gather_wait("gather2_wait", gathers[2], after)
            return {"wt_cd_in": l_in.reshape(CD_IN, D), "w_cd_out": l_out.reshape(D, D),
                    "wt_ffn_in1": l_ffn.reshape(2 * DFF, D), "w_ffn_down1": l_down.reshape(DFF, D)}
        return {}

    scatters = {}
    rides_with = {"w_ffn_down1": "wt_ffn_in1", "w_cd_out": "wt_cd_in", "w_ffn_down0": "wt_ffn_in0",
                  "w_ab_out": "wt_ab_in"}
    held = {}

    def on_grad(key, arr):
        if key in rides_with:
            held[rides_with[key]] = (key, arr)
            return None
        group = ([held.pop(key)] if key in held else []) + [(key, arr)]
        srcs, lands, items = [], [], []
        for n, (k, a) in enumerate(group):
            if k.startswith("wt_ffn_in"):
                src = a.reshape(2, NDEV, r_ff, D)
                own = lax.dynamic_slice_in_dim(src, me, 1, axis=1)
                lands.append(lax.dynamic_update_slice(lax.empty(src.shape, BF16), own, (0, me, 0, 0)))
                items += [((lambda S, j, n=n: S[n].at[0, j]), slot2(n, 0)), ((lambda S, j, n=n: S[n].at[1, j]), slot2(n, 1))]
            else:
                rows = a.shape[0] // NDEV
                src = a.reshape(NDEV, rows, D)
                own = lax.dynamic_index_in_dim(src, me, 0, keepdims=False)
                lands.append(_landing((1, NDEV, rows, D), BF16, [((0, me), own)]))
                items.append(((lambda S, j, n=n: S[n].at[j]), slot2(n, 0)))
            srcs.append(src)
        st = _exchange_start(f"scatter_{key}_start", srcs, lands, items)
        scatters[key] = (st, [k for k, _ in group])
        return st["token"]

    flat = o_small.reshape(NDEV, 24 * 128)

    def chan(lo, taps):
        return flat[:, lo:lo + taps * SHARD_C].reshape(NDEV, taps, SHARD_C).transpose(1, 0, 2).reshape(taps, 512)

    W = {
        "wt_ab_in": o_ab_in.reshape(AB_IN, D), "dep0": gathers[1]["token"],
        "ab_norm_g": w["ab_norm_g"], "sgu_norm_g": w["sgu_norm_g"], "sgu_norm_b": w["sgu_norm_b"],
        "sgu_w": w["sgu_w"][0], "sgu_bias": w["sgu_bias"][0], "q_norm_g": w["q_norm_g"][0],
        "k_norm_g": w["k_norm_g"][0], "ffn_norm_g": w["ffn_norm_g"],
        "cd_norm_g": flat[:, 0:128].reshape(1, D),
        "conv_c_w32": jnp.pad(chan(128, CONV_C_TAPS), ((0, 1), (0, 0))),
        "conv_c_b": chan(2112, 1), "c_ln_g": chan(2176, 1), "c_ln_b": chan(2240, 1),
        "conv_d_w8": jnp.pad(chan(2304, CONV_D_TAPS), ((0, 8 - CONV_D_TAPS), (0, 0))),
    }

    loss_cols, grad_x, G = _local_step(x[0], loss_target[0], W, fetch, on_grad)
    loss = lax.psum(jnp.sum(loss_cols), ("x", "y", "c"))

    small_parts = [G["ab_norm_g"], G["sgu_norm_g"], G["sgu_norm_b"], G["sgu_w"], G["sgu_bias"], G["q_norm_g"],
                   G["k_norm_g"], G["cd_norm_g"], G["conv_c_w32"][:CONV_C_TAPS], G["conv_c_b"], G["c_ln_g"],
                   G["c_ln_b"], G["conv_d_w8"][:CONV_D_TAPS], G["ffn_norm_g0"], G["ffn_norm_g1"]]
    sizes = [p.size for p in small_parts]
    small_rows = 712
    packed = _pack_rows(small_parts, small_rows)
    small = _exchange_start("small_start", [packed], [_landing((NDEV, small_rows, 128), F32, [((me,), packed)])],
                            [(one(0), slot(0))])
    landed = {}

    def wait_scatters(name, group_keys, after):
        res = _exchange_wait(name, [scatters[gk][0] for gk in group_keys], after)
        for gk, lands in zip(group_keys, res):
            landed.update(zip(scatters[gk][1], lands))

    wait_scatters("scatter_wait_early", ["wt_ffn_in1", "wt_cd_in", "wt_ffn_in0"], small["token"])

    grads, deltas, new_m, new_v = {}, {}, {}, {}
    done = []

    def put(name, res):
        grads[name], deltas[name], new_m[name], new_v[name] = res

    def adam(name, lands, sel, transposed):
        flip = (lambda a: jnp.swapaxes(a, 1, 2)) if transposed else (lambda a: a)
        res = _adam_stacked(lands, sel, flip(w[name]), flip(m[name]), flip(v[name]), f"adam_{name}")
        done.append(res[1])
        put(name, [flip(r) for r in res])

    ffn_in_lands = [landed["wt_ffn_in0"], landed["wt_ffn_in1"]]
    adam("cd_w_in", [landed["wt_cd_in"]], 0, True)
    adam("ffn_w_gate", ffn_in_lands, 0, True)
    adam("ffn_w_up", ffn_in_lands, 1, True)
    adam("cd_w_out", [landed["w_cd_out"]], 0, False)
    adam("ffn_w_down", [landed["w_ffn_down0"], landed["w_ffn_down1"]], 0, False)

    small_land = _exchange_wait("small_wait", [small], list(done))[0][0]
    red = _sum_slots(small_land).reshape(-1)
    offs = [0]
    for s in sizes:
        offs.append(offs[-1] + s)
    seg = [red[offs[i]:offs[i + 1]] for i in range(len(sizes))]

    def own_channels(full, taps):
        return lax.dynamic_slice_in_dim(full.reshape(taps, 512), me * SHARD_C, SHARD_C, axis=1)

    g_small = {
        "ab_norm_g": seg[0].reshape(1, 1024), "sgu_norm_g": seg[1].reshape(1, 512), "sgu_norm_b": seg[2].reshape(1, 512),
        "sgu_w": seg[3].reshape(512, 128), "sgu_bias": seg[4].reshape(4, 128), "q_norm_g": seg[5].reshape(3, 64),
        "k_norm_g": seg[6].reshape(3, 64),
        "cd_norm_g": lax.dynamic_slice_in_dim(seg[7].reshape(1, D), me * (D // NDEV), D // NDEV, axis=1),
        "conv_c_w": own_channels(seg[8], CONV_C_TAPS), "conv_c_b": own_channels(seg[9], 1),
        "c_ln_g": own_channels(seg[10], 1), "c_ln_b": own_channels(seg[11], 1),
        "conv_d_w": own_channels(seg[12], CONV_D_TAPS),
        "ffn_norm_g": jnp.concatenate([seg[13].reshape(1, D), seg[14].reshape(1, D)], axis=0),
    }

    names2d = [n for n, _ in SMALL_2D]
    d_s, m_s, v_s = _adam_small([w[n].reshape(s) for n, s in SMALL_2D], [g_small[n] for n in names2d],
                                [m[n].reshape(s) for n, s in SMALL_2D], [v[n].reshape(s) for n, s in SMALL_2D])
    for i, n in enumerate(names2d):
        shape = w[n].shape
        grads[n], deltas[n] = g_small[n].reshape(shape), d_s[i].reshape(shape)
        new_m[n], new_v[n] = m_s[i].reshape(shape), v_s[i].reshape(shape)

    wait_scatters("scatter_wait_last", ["wt_ab_in"], d_s[0])
    adam("ab_w_out", [landed["w_ab_out"]], 0, False)
    adam("ab_w_in", [landed["wt_ab_in"]], 0, True)

    return (loss, grad_x[None], *[grads[n] for n in WEIGHT_NAMES], *[deltas[n] for n in WEIGHT_NAMES],
            *[new_m[n] for n in WEIGHT_NAMES], *[new_v[n] for n in WEIGHT_NAMES])
```

```python
import functools

import jax
import jax.numpy as jnp
import numpy as np
from jax import lax
from jax.experimental import pallas as pl
from jax.experimental.pallas import tpu as pltpu

F32 = jnp.float32
BF16 = jnp.bfloat16

T = 4096
D = 1024
NDEV = 8
EPS = 1e-6
NEG_INF = -1e30
DFF = 2816
AB_IN = 5632
CD_IN = 2560
HEAD = 64
PAIR = 128
NPAIR = 4
NBACK = 128
DIL_RATES = (1, 4, 16)
ROPE_HALF = 8
ROPE_THETA = 500000.0
CONV_C_TAPS = 31
CONV_D_TAPS = 3
HALO = 32
ATTN_BWD_UNROLL = 4

ADAM_LR = 0.001
ADAM_B1 = 0.9
ADAM_B2 = 0.999
ADAM_EPS = 1e-08
ADAM_WD = 0.01
ADAM_STEP = 10
ADAM_C1 = 1.0 / (1.0 - ADAM_B1 ** ADAM_STEP)
ADAM_C2 = 1.0 / (1.0 - ADAM_B2 ** ADAM_STEP)

VMEM_LIMIT_MB = 48
MESH = pl.DeviceIdType.MESH
HBM_SPEC = pl.BlockSpec(memory_space=pl.ANY)


def _cparams(ngrid, vmem_mb=VMEM_LIMIT_MB):
    return pltpu.CompilerParams(dimension_semantics=("arbitrary",) * ngrid,
                                vmem_limit_bytes=vmem_mb * 1024 * 1024)


def _pick(n, options):
    for o in options:
        if n % o == 0:
            return o
    raise ValueError(f"no tile for {n} in {options}")


def _sds(shape, dtype):
    return jax.ShapeDtypeStruct(shape, dtype)


def _sigmoid(x):
    return 1.0 / (1.0 + jnp.exp(-x))


def _sigmoid_bf16(x):
    return 0.5 * jnp.tanh(0.5 * x) + 0.5


def _gelu(z):
    return 0.5 * z * (1.0 + lax.erf(z * 0.7071067811865476))


def _gelu_grad(z):
    return 0.5 * (1.0 + lax.erf(z * 0.7071067811865476)) + z * jnp.exp(-0.5 * z * z) * 0.3989422804014327


def _mm_nt(a, wt, name, out_dtype=BF16, tm=2048, dep=None):
    M, K = a.shape
    N = wt.shape[0]
    tn = _pick(N, (512, 256))

    def body(a_ref, w_ref, *rest):
        o_ref = rest[-1]
        o_ref[...] = lax.dot_general(a_ref[...], w_ref[...], (((1,), (1,)), ((), ())),
                                     preferred_element_type=F32).astype(o_ref.dtype)

    in_specs = [pl.BlockSpec((tm, K), lambda i, j: (i, 0)), pl.BlockSpec((tn, K), lambda i, j: (j, 0))]
    args = [a, wt]
    if dep is not None:
        in_specs.append(HBM_SPEC)
        args.append(dep)
    return pl.pallas_call(
        body, name=name, grid=(M // tm, N // tn), in_specs=in_specs,
        out_specs=pl.BlockSpec((tm, tn), lambda i, j: (i, j)),
        out_shape=_sds((M, N), out_dtype), compiler_params=_cparams(2))(*args)


EPI_ROWS = 256


def _mm_nn(a, w, name, mode, resid, gain=None, tgt=None, dep=None, tm=512):
    M, K = a.shape
    N = w.shape[1]
    side = gain if mode == "rms" else tgt

    def body(a_ref, w_ref, resid_ref, side_ref, *rest):
        outs, acc = rest[-3 if mode == "rms" else -4:-1], rest[-1]
        i = pl.program_id(0)
        acc[...] = jnp.dot(a_ref[...], w_ref[...], preferred_element_type=F32)

        if mode == "loss":
            @pl.when(i == 0)
            def _():
                outs[2][...] = jnp.zeros_like(outs[2])

        for r0 in range(0, tm, EPI_ROWS):
            rows = slice(r0, r0 + EPI_ROWS)
            v = acc[rows, :] + resid_ref[rows, :]
            if mode == "rms":
                outs[0][rows, :] = v
                r = lax.rsqrt(jnp.mean(v * v, axis=-1, keepdims=True) + EPS)
                outs[1][rows, :] = (v * r * side_ref[...]).astype(BF16)
            else:
                d = v - side_ref[rows, :]
                outs[2][...] += jnp.sum(d * d, axis=0, keepdims=True) * (0.5 / N)
                dy = d * (1.0 / N)
                outs[0][rows, :] = dy
                outs[1][rows, :] = dy.astype(BF16)

    row = pl.BlockSpec((tm, N), lambda i: (i, 0))
    vec = pl.BlockSpec((1, N), lambda i: (0, 0))
    in_specs = [pl.BlockSpec((tm, K), lambda i: (i, 0)),
                pl.BlockSpec((K, N), lambda i: (0, 0), pipeline_mode=pl.Buffered(1)), row,
                vec if mode == "rms" else row]
    args = [a, w, resid, side]
    if dep is not None:
        in_specs.append(HBM_SPEC)
        args.append(dep)
    if mode == "rms":
        out_specs, out_shape = [row, row], [_sds((M, N), F32), _sds((M, N), BF16)]
    else:
        out_specs, out_shape = [row, row, vec], [_sds((M, N), F32), _sds((M, N), BF16), _sds((1, N), F32)]
    return pl.pallas_call(
        body, name=name, grid=(M // tm,), in_specs=in_specs, out_specs=out_specs, out_shape=out_shape,
        scratch_shapes=[pltpu.VMEM((tm, N), F32)], compiler_params=_cparams(1))(*args)


def _mm_dh_rms_bwd(a, w, x, gain, dres, name, dep=None, tm=512):
    parts = a.shape[0] if a.ndim == 3 else 1
    M, Kp = a.shape[-2], a.shape[-1]
    N = w.shape[1]
    nblk = M // tm
    assert nblk % 2 == 0

    def body(a_ref, w_ref, x_ref, g_ref, dres_ref, *rest):
        dx_ref, dxb_ref, dg_ref, acc0, acc1 = rest[-5:]
        i = pl.program_id(0)

        def matmul(acc):
            if parts == 1:
                acc[...] = jnp.dot(a_ref[...], w_ref[...], preferred_element_type=F32)
            else:
                d = jnp.dot(a_ref[0], w_ref[0:Kp, :], preferred_element_type=F32)
                for p in range(1, parts):
                    d = d + jnp.dot(a_ref[p], w_ref[p * Kp:(p + 1) * Kp, :], preferred_element_type=F32)
                acc[...] = d

        def finish(acc):
            for r0 in range(0, tm, EPI_ROWS // 2):
                rows = slice(r0, r0 + EPI_ROWS // 2)
                v = acc[rows, :]
                xf = x_ref[rows, :]
                r = lax.rsqrt(jnp.mean(xf * xf, axis=-1, keepdims=True) + EPS)
                xhat = xf * r
                dg_ref[...] += jnp.sum(v * xhat, axis=0, keepdims=True)
                dxh = v * g_ref[...]
                tot = dres_ref[rows, :] + r * (dxh - xhat * jnp.mean(dxh * xhat, axis=-1, keepdims=True))
                dx_ref[rows, :] = tot
                dxb_ref[rows, :] = tot.astype(BF16)

        @pl.when(i == 0)
        def _():
            dg_ref[...] = jnp.zeros_like(dg_ref)
            matmul(acc0)

        @pl.when((i > 0) & (i < nblk) & (i % 2 == 1))
        def _():
            matmul(acc1)
            finish(acc0)

        @pl.when((i > 0) & (i < nblk) & (i % 2 == 0))
        def _():
            matmul(acc0)
            finish(acc1)

        @pl.when(i == nblk)
        def _():
            finish(acc1)

    last = nblk - 1
    row = pl.BlockSpec((tm, N), lambda i: (jnp.maximum(i - 1, 0), 0))
    vec = pl.BlockSpec((1, N), lambda i: (0, 0))
    if a.ndim == 3:
        a_spec = pl.BlockSpec((parts, tm, Kp), lambda i: (0, jnp.minimum(i, last), 0))
    else:
        a_spec = pl.BlockSpec((tm, Kp), lambda i: (jnp.minimum(i, last), 0))
    w_spec = pl.BlockSpec((parts * Kp, N), lambda i: (0, 0), pipeline_mode=pl.Buffered(1))
    in_specs = [a_spec, w_spec, row, vec, row]
    args = [a, w, x, gain, dres]
    if dep is not None:
        in_specs.append(HBM_SPEC)
        args.append(dep)
    return pl.pallas_call(
        body, name=name, grid=(nblk + 1,), in_specs=in_specs, out_specs=[row, row, vec],
        out_shape=[_sds((M, N), F32), _sds((M, N), BF16), _sds((1, N), F32)],
        scratch_shapes=[pltpu.VMEM((tm, N), F32), pltpu.VMEM((tm, N), F32)], compiler_params=_cparams(1, 56))(*args)


def _mm_tn(a, b, name, out_dtype=BF16, tt=1024):
    parts = a.shape[0] if a.ndim == 3 else 1
    Tt, Mp = a.shape[-2], a.shape[-1]
    N = b.shape[1]
    tn = _pick(Mp, (1408, 1280, 1024, 512))
    jper = Mp // tn
    nt = Tt // tt

    def body(a_ref, b_ref, o_ref, acc):
        t = pl.program_id(1)

        @pl.when(t == 0)
        def _():
            acc[...] = jnp.zeros_like(acc)

        acc[...] += lax.dot_general(a_ref[...], b_ref[...], (((0,), (0,)), ((), ())),
                                    preferred_element_type=F32)

        @pl.when(t == nt - 1)
        def _():
            o_ref[...] = acc[...].astype(o_ref.dtype)

    if a.ndim == 3:
        a_spec = pl.BlockSpec((None, tt, tn), lambda j, t: (j // jper, t, j % jper))
    else:
        a_spec = pl.BlockSpec((tt, tn), lambda j, t: (t, j))
    return pl.pallas_call(
        body, name=name, grid=(parts * jper, nt),
        in_specs=[a_spec, pl.BlockSpec((tt, N), lambda j, t: (t, 0))],
        out_specs=pl.BlockSpec((tn, N), lambda j, t: (j, 0)),
        out_shape=_sds((parts * Mp, N), out_dtype), scratch_shapes=[pltpu.VMEM((tn, N), F32)],
        compiler_params=_cparams(2))(a, b)


def _ffn_in(h, wt_in, name, tm=2048, tn=256):
    nj = DFF // tn

    def body(h_ref, wg_ref, wu_ref, p_ref, act_ref):
        nt = (((1,), (1,)), ((), ()))
        g = lax.dot_general(h_ref[...], wg_ref[...], nt, preferred_element_type=F32).astype(BF16)
        u = lax.dot_general(h_ref[...], wu_ref[...], nt, preferred_element_type=F32).astype(BF16)
        p_ref[0] = g
        p_ref[1] = u
        act_ref[...] = g * _sigmoid_bf16(g) * u

    return pl.pallas_call(
        body, name=name, grid=(T // tm, nj),
        in_specs=[pl.BlockSpec((tm, D), lambda i, j: (i, 0)), pl.BlockSpec((tn, D), lambda i, j: (j, 0)),
                  pl.BlockSpec((tn, D), lambda i, j: (j + nj, 0))],
        out_specs=[pl.BlockSpec((2, tm, tn), lambda i, j: (0, i, j)), pl.BlockSpec((tm, tn), lambda i, j: (i, j))],
        out_shape=[_sds((2, T, DFF), BF16), _sds((T, DFF), BF16)], compiler_params=_cparams(2))(h, wt_in, wt_in)


def _ffn_dact(dyb, w_down, p3, name, tm=2048, tn=256, dep=None):
    def body(dy_ref, w_ref, p_ref, *rest):
        o_ref = rest[-1]
        da = lax.dot_general(dy_ref[...], w_ref[...], (((1,), (1,)), ((), ())),
                             preferred_element_type=F32).astype(BF16)
        g = p_ref[0]
        u = p_ref[1]
        sg = _sigmoid_bf16(g)
        gs = g * sg
        o_ref[0] = (da * u) * (sg + gs * (1.0 - sg))
        o_ref[1] = da * gs

    pspec = pl.BlockSpec((2, tm, tn), lambda i, j: (0, i, j))
    in_specs = [pl.BlockSpec((tm, D), lambda i, j: (i, 0)), pl.BlockSpec((tn, D), lambda i, j: (j, 0)), pspec]
    args = [dyb, w_down, p3]
    if dep is not None:
        in_specs.append(HBM_SPEC)
        args.append(dep)
    return pl.pallas_call(
        body, name=name, grid=(T // tm, DFF // tn), in_specs=in_specs, out_specs=pspec,
        out_shape=_sds((2, T, DFF), BF16), compiler_params=_cparams(2))(*args)


def _rms_fwd(x, g, name, tm=512):
    def body(x_ref, g_ref, h_ref):
        xf = x_ref[...]
        r = lax.rsqrt(jnp.mean(xf * xf, axis=-1, keepdims=True) + EPS)
        h_ref[...] = (xf * r * g_ref[...]).astype(BF16)

    return pl.pallas_call(
        body, name=name, grid=(T // tm,),
        in_specs=[pl.BlockSpec((tm, D), lambda i: (i, 0)), pl.BlockSpec((1, D), lambda i: (0, 0))],
        out_specs=pl.BlockSpec((tm, D), lambda i: (i, 0)),
        out_shape=_sds((T, D), BF16), compiler_params=_cparams(1))(x, g)


def _tril_mask():
    r = lax.broadcasted_iota(jnp.int32, (128, 128), 0)
    c = lax.broadcasted_iota(jnp.int32, (128, 128), 1)
    return r >= c


def _mix_a_fwd(pab, sgu_g, sgu_b, sgu_w, sgu_bias3, tm=512):
    def body(zu_ref, zv_ref, g_ref, b_ref, w_ref, bias_ref, o_ref):
        u = _gelu(zu_ref[...].astype(F32))
        v = _gelu(zv_ref[...].astype(F32))
        mu = jnp.mean(v, axis=-1, keepdims=True)
        vc = v - mu
        rstd = lax.rsqrt(jnp.mean(vc * vc, axis=-1, keepdims=True) + EPS)
        vn = (vc * rstd * g_ref[...] + b_ref[...]).astype(BF16)
        tri = _tril_mask()
        for gi in range(4):
            wg = jnp.where(tri, w_ref[gi], 0.0).astype(BF16)
            bg = bias_ref[gi]
            for c in range(tm // 128):
                rs, cs = slice(c * 128, (c + 1) * 128), slice(gi * 128, (gi + 1) * 128)
                mixed = jnp.dot(wg, vn[rs, cs], preferred_element_type=F32) + bg
                o_ref[rs, cs] = (u[rs, cs] * mixed).astype(BF16)

    half = pl.BlockSpec((tm, 512), lambda i: (i, 0))
    return pl.pallas_call(
        body, name="mix_a_fwd", grid=(T // tm,),
        in_specs=[half, pl.BlockSpec((tm, 512), lambda i: (i, 1)),
                  pl.BlockSpec((1, 512), lambda i: (0, 0)), pl.BlockSpec((1, 512), lambda i: (0, 0)),
                  pl.BlockSpec((4, 128, 128), lambda i: (0, 0, 0)), pl.BlockSpec((4, 128, 1), lambda i: (0, 0, 0))],
        out_specs=half, out_shape=_sds((T, D), BF16), compiler_params=_cparams(1),
    )(pab, pab, sgu_g, sgu_b, sgu_w, sgu_bias3)


def _rope_tables():
    pos = np.arange(T, dtype=np.float32)
    inv_freq = np.float32(ROPE_THETA) ** (-np.arange(ROPE_HALF, dtype=np.float32) * np.float32(2.0 / (2 * ROPE_HALF)))
    ang = (pos[:, None] * inv_freq[None, :]).astype(np.float32)
    cos, sin = np.cos(ang), np.sin(ang)
    z8 = np.zeros((T, ROPE_HALF), np.float32)
    rest = np.zeros((T, HEAD - 2 * ROPE_HALF), np.float32)
    c64 = np.concatenate([cos, cos, rest + 1.0], axis=1)
    s1 = np.concatenate([z8, sin, rest], axis=1)
    s2 = np.concatenate([-sin, z8, rest], axis=1)
    return tuple(jnp.asarray(np.tile(t, (1, 2)).astype(np.float32)) for t in (c64, s1, s2))


def _lo_mask(shape):
    return lax.broadcasted_iota(jnp.int32, shape, 1) < HEAD


def _seg_mean(x, lo):
    s_all = jnp.sum(x, axis=-1, keepdims=True)
    s_lo = jnp.sum(jnp.where(lo, x, 0.0), axis=-1, keepdims=True)
    return jnp.where(lo, s_lo, s_all - s_lo) * (1.0 / HEAD)


def _rope(n, c, s1, s2):
    return n * c + pltpu.roll(n, ROPE_HALF, 1) * s1 + pltpu.roll(n, PAIR - ROPE_HALF, 1) * s2


def _rope_t(dy, c, s1, s2):
    return dy * c - pltpu.roll(dy, PAIR - ROPE_HALF, 1) * s2 - pltpu.roll(dy, ROPE_HALF, 1) * s1


def _prep_fwd(pab, qg, kg, tabs, tm=512):
    def body(p_ref, qg_ref, kg_ref, c_ref, s1_ref, s2_ref, *outs):
        lo = _lo_mask((tm, PAIR))
        c, s1, s2 = c_ref[...], s1_ref[...], s2_ref[...]
        for g in range(3):
            qn_ref, kn_ref, v_ref = outs[3 * g:3 * g + 3]
            for p in range(NPAIR):
                for which, gains, dst in ((0, qg_ref, qn_ref), (1, kg_ref, kn_ref)):
                    col = (2 + 3 * which + g) * 512 + p * PAIR
                    xr = p_ref[:, col:col + PAIR].astype(F32)
                    rinv = lax.rsqrt(_seg_mean(xr * xr, lo) + EPS)
                    outs[9 + 2 * g + which][p] = rinv
                    dst[p] = _rope(xr * rinv * gains[g:g + 1, :], c, s1, s2)
                col = (8 + g) * 512 + p * PAIR
                v_ref[p] = p_ref[:, col:col + PAIR].astype(F32)

    pm = pl.BlockSpec((NPAIR, tm, PAIR), lambda i: (0, i, 0))
    tab = pl.BlockSpec((tm, PAIR), lambda i: (i, 0))
    gain = pl.BlockSpec((3, PAIR), lambda i: (0, 0))
    res = pl.pallas_call(
        body, name="prep_fwd", grid=(T // tm,),
        in_specs=[pl.BlockSpec((tm, AB_IN), lambda i: (i, 0)), gain, gain, tab, tab, tab],
        out_specs=[pm] * 15, out_shape=[_sds((NPAIR, T, PAIR), F32)] * 15,
        compiler_params=_cparams(1))(pab, qg, kg, *tabs)
    return res[0:9], res[9:15]


def _res_index(it, rate):
    window = NBACK * rate
    b = it // rate
    rho = it % rate
    start = b * window + rho
    startp = jnp.maximum(start - window, rho)
    kmin = jnp.where(b > 0, 0, NBACK)
    return start, startp, kmin


def _rows(start, rate):
    if rate == 1:
        return pl.ds(pl.multiple_of(start, NBACK), NBACK)
    return pl.ds(start, NBACK, stride=rate)


def _band_bias():
    qs = lax.broadcasted_iota(jnp.int32, (2 * NBACK, 2 * NBACK), 0)
    kj = lax.broadcasted_iota(jnp.int32, (2 * NBACK, 2 * NBACK), 1)
    dist = (qs & (NBACK - 1)) + NBACK - kj
    both = (dist >= 0) & (dist <= NBACK)
    return jnp.where(both, 0.0, NEG_INF), jnp.where(both & (kj >= NBACK), 0.0, NEG_INF)


def _attn_fwd(qn, kn, v, rate, name, dep=None):
    def body(q_ref, k_ref, v_ref, *rest):
        o_ref, l_ref = rest[-2:]
        lo = _lo_mask((NBACK, PAIR))
        bias_all, bias_first = _band_bias()

        def step(it, carry):
            start, startp, kmin = _res_index(it, rate)
            q = q_ref[_rows(start, rate), :] * (HEAD ** -0.5)
            kcat = jnp.concatenate([k_ref[_rows(startp, rate), :], k_ref[_rows(start, rate), :]], axis=0).astype(BF16)
            vcat = jnp.concatenate([v_ref[_rows(startp, rate), :], v_ref[_rows(start, rate), :]], axis=0).astype(BF16)
            vcat1 = jnp.concatenate([vcat, jnp.ones((2 * NBACK, PAIR), BF16)], axis=1)
            q2 = jnp.concatenate([jnp.where(lo, q, 0.0), jnp.where(lo, 0.0, q)], axis=0).astype(BF16)
            s = lax.dot_general(q2, kcat, (((1,), (1,)), ((), ())), preferred_element_type=F32)
            s = s + jnp.where(kmin == 0, bias_all, bias_first)
            m = jnp.max(s, axis=-1, keepdims=True)
            ol = jnp.dot(jnp.exp(s - m).astype(BF16), vcat1, preferred_element_type=F32)
            o2 = ol[:, 0:PAIR] / ol[:, PAIR:]
            ls = m + jnp.log(ol[:, PAIR:])
            o_ref[_rows(start, rate), :] = jnp.where(lo, o2[0:NBACK], o2[NBACK:])
            l_ref[_rows(start, rate), :] = jnp.where(lo, ls[0:NBACK], ls[NBACK:])
            return carry

        lax.fori_loop(0, T // NBACK, step, 0, unroll=4)

    pm = pl.BlockSpec((None, T, PAIR), lambda p: (p, 0, 0))
    in_specs, args = [pm, pm, pm], [qn, kn, v]
    if dep is not None:
        in_specs.append(HBM_SPEC)
        args.append(dep)
    return pl.pallas_call(
        body, name=name, grid=(NPAIR,), in_specs=in_specs, out_specs=[pm, pm],
        out_shape=[_sds((NPAIR, T, PAIR), F32)] * 2, compiler_params=_cparams(1))(*args)


def _merge_fwd(cat_ab, outs, lses, tm=512):
    def body(cat_in, o0, o1, o2, l0, l1, l2, cat_ref, lse_ref):
        del cat_in
        for p in range(NPAIR):
            a0, a1, a2 = l0[p], l1[p], l2[p]
            m = jnp.maximum(jnp.maximum(a0, a1), a2)
            w0, w1, w2 = jnp.exp(a0 - m), jnp.exp(a1 - m), jnp.exp(a2 - m)
            s = w0 + w1 + w2
            b = (w0 * o0[p] + w1 * o1[p] + w2 * o2[p]) / s
            cat_ref[:, p * PAIR:(p + 1) * PAIR] = b.astype(BF16)
            lse_ref[p] = m + jnp.log(s)

    pm = pl.BlockSpec((NPAIR, tm, PAIR), lambda i: (0, i, 0))
    return pl.pallas_call(
        body, name="merge_fwd", grid=(T // tm,),
        in_specs=[pl.BlockSpec(memory_space=pl.ANY)] + [pm] * 6,
        out_specs=[pl.BlockSpec((tm, 512), lambda i: (i, 1)), pm],
        out_shape=[_sds((T, D), BF16), _sds((NPAIR, T, PAIR), F32)],
        input_output_aliases={0: 0}, compiler_params=_cparams(1))(cat_ab, *outs, *lses)


def _b_pre_bwd(dcat, cat, tm=512):
    def body(db_ref, b_ref, dbp_ref, e_ref):
        lo = _lo_mask((tm, PAIR))
        for p in range(NPAIR):
            db = db_ref[:, p * PAIR:(p + 1) * PAIR].astype(F32)
            b = b_ref[:, p * PAIR:(p + 1) * PAIR].astype(F32)
            dbp_ref[p] = db
            e_ref[p] = _seg_mean(db * b, lo) * float(HEAD)

    pm = pl.BlockSpec((NPAIR, tm, PAIR), lambda i: (0, i, 0))
    right = pl.BlockSpec((tm, 512), lambda i: (i, 1))
    return pl.pallas_call(
        body, name="b_pre_bwd", grid=(T // tm,), in_specs=[right, right], out_specs=[pm, pm],
        out_shape=[_sds((NPAIR, T, PAIR), F32)] * 2, compiler_params=_cparams(1))(dcat, cat)


def _attn_bwd(qn, kn, v, dbp, e, lse, rate, name):
    def body(q_ref, k_ref, v_ref, db_ref, e_ref, lse_ref, dq_ref, dk_ref, dv_ref):
        lo = _lo_mask((NBACK, PAIR))
        bias_all, bias_first = _band_bias()
        scale = HEAD ** -0.5
        nt = (((1,), (1,)), ((), ()))
        tn = (((0,), (0,)), ((), ()))
        window = NBACK * rate
        nblk = T // window

        def one(it, carry):
            dk_carry, dv_carry = carry
            rho = it // nblk
            b = it % nblk
            start = b * window + rho
            rq = _rows(start, rate)
            rp = _rows(jnp.maximum(start - window, rho), rate)
            q = q_ref[rq, :] * scale
            db = db_ref[rq, :]
            ev = e_ref[rq, :]
            ls = lse_ref[rq, :]
            kcat = jnp.concatenate([k_ref[rp, :], k_ref[rq, :]], axis=0).astype(BF16)
            vcat = jnp.concatenate([v_ref[rp, :], v_ref[rq, :]], axis=0).astype(BF16)
            q2 = jnp.concatenate([jnp.where(lo, q, 0.0), jnp.where(lo, 0.0, q)], axis=0).astype(BF16)
            db2 = jnp.concatenate([jnp.where(lo, db, 0.0), jnp.where(lo, 0.0, db)], axis=0).astype(BF16)
            ls2 = jnp.concatenate([ls[:, 0:1], ls[:, HEAD:HEAD + 1]], axis=0)
            ev2 = jnp.concatenate([ev[:, 0:1], ev[:, HEAD:HEAD + 1]], axis=0)
            s = lax.dot_general(q2, kcat, nt, preferred_element_type=F32)
            pt = jnp.exp(s + jnp.where(b > 0, bias_all, bias_first) - ls2)
            dp = lax.dot_general(db2, vcat, nt, preferred_element_type=F32)
            ds = (pt * (dp - ev2)).astype(BF16)
            dq2 = jnp.dot(ds, kcat, preferred_element_type=F32) * scale
            dkc = lax.dot_general(ds, q2, tn, preferred_element_type=F32)
            dvc = lax.dot_general(pt.astype(BF16), db2, tn, preferred_element_type=F32)
            dq_ref[rq, :] = jnp.where(lo, dq2[0:NBACK], dq2[NBACK:])
            dk_ref[rp, :] = dk_carry + dkc[0:NBACK]
            dk_ref[rq, :] = dkc[NBACK:]
            dv_ref[rp, :] = dv_carry + dvc[0:NBACK]
            dv_ref[rq, :] = dvc[NBACK:]
            return dkc[NBACK:], dvc[NBACK:]

        def step(i, carry):
            for u in range(ATTN_BWD_UNROLL):
                carry = one(i * ATTN_BWD_UNROLL + u, carry)
            return carry

        zero = jnp.zeros((NBACK, PAIR), F32)
        lax.fori_loop(0, T // NBACK // ATTN_BWD_UNROLL, step, (zero, zero))

    pm = pl.BlockSpec((None, T, PAIR), lambda p: (p, 0, 0))
    return pl.pallas_call(
        body, name=name, grid=(NPAIR,), in_specs=[pm] * 6, out_specs=[pm] * 3,
        out_shape=[_sds((NPAIR, T, PAIR), F32)] * 3, compiler_params=_cparams(1, 56))(qn, kn, v, dbp, e, lse)


def _ab_in_bwd(pab, dcat, sgu_g, sgu_b, sgu_w, sgu_bias3, qg, kg, tabs, dqkv, rinvs, tm=256):
    def body(p_ref, dcat_ref, g_ref, b_ref, w_ref, bias_ref, qg_ref, kg_ref, c_ref, s1_ref, s2_ref, *rest):
        dq_refs, rinv_refs = rest[0:9], rest[9:15]
        o_ref, dwm_ref, dbias_ref, dsg_ref, dsb_ref, dgain_ref = rest[15:]
        i = pl.program_id(0)

        @pl.when(i == 0)
        def _():
            dwm_ref[...] = jnp.zeros_like(dwm_ref)
            dbias_ref[...] = jnp.zeros_like(dbias_ref)
            dsg_ref[...] = jnp.zeros_like(dsg_ref)
            dsb_ref[...] = jnp.zeros_like(dsb_ref)
            dgain_ref[...] = jnp.zeros_like(dgain_ref)

        zu = p_ref[:, 0:512].astype(F32)
        zv = p_ref[:, 512:1024].astype(F32)
        u = _gelu(zu)
        v = _gelu(zv)
        mu = jnp.mean(v, axis=-1, keepdims=True)
        vc = v - mu
        rstd = lax.rsqrt(jnp.mean(vc * vc, axis=-1, keepdims=True) + EPS)
        xhat = vc * rstd
        vn = (xhat * g_ref[...] + b_ref[...]).astype(BF16)
        da = dcat_ref[...].astype(F32)
        tri = _tril_mask()
        du_parts = [[None] * 4 for _ in range(tm // 128)]
        dvn_parts = [[None] * 4 for _ in range(tm // 128)]
        for gi in range(4):
            wg = jnp.where(tri, w_ref[gi], 0.0).astype(BF16)
            bg = bias_ref[gi]
            for c in range(tm // 128):
                rs, cs = slice(c * 128, (c + 1) * 128), slice(gi * 128, (gi + 1) * 128)
                vblk = vn[rs, cs]
                mixed = jnp.dot(wg, vblk, preferred_element_type=F32) + bg
                dab = da[rs, cs]
                du_parts[c][gi] = dab * mixed
                dmixed = dab * u[rs, cs]
                dmb = dmixed.astype(BF16)
                dvn_parts[c][gi] = lax.dot_general(wg, dmb, (((0,), (0,)), ((), ())), preferred_element_type=F32)
                dwm = lax.dot_general(dmb, vblk, (((1,), (1,)), ((), ())), preferred_element_type=F32)
                dwm_ref[gi] += jnp.where(tri, dwm, 0.0)
                dbias_ref[gi] += dmixed
        du = jnp.concatenate([jnp.concatenate(r, axis=1) for r in du_parts], axis=0)
        dvn = jnp.concatenate([jnp.concatenate(r, axis=1) for r in dvn_parts], axis=0)
        dsg_ref[...] += jnp.sum(dvn * xhat, axis=0, keepdims=True)
        dsb_ref[...] += jnp.sum(dvn, axis=0, keepdims=True)
        dxh = dvn * g_ref[...]
        dv = rstd * (dxh - jnp.mean(dxh, axis=-1, keepdims=True)
                     - xhat * jnp.mean(dxh * xhat, axis=-1, keepdims=True))
        o_ref[:, 0:512] = (du * _gelu_grad(zu)).astype(BF16)
        o_ref[:, 512:1024] = (dv * _gelu_grad(zv)).astype(BF16)

        lo = _lo_mask((tm, PAIR))
        c, s1, s2 = c_ref[...], s1_ref[...], s2_ref[...]
        for g in range(3):
            dq_ref, dk_ref, dv_ref = dq_refs[3 * g:3 * g + 3]
            for p in range(NPAIR):
                for which, gains, src in ((0, qg_ref, dq_ref), (1, kg_ref, dk_ref)):
                    col = (2 + 3 * which + g) * 512 + p * PAIR
                    xr = p_ref[:, col:col + PAIR].astype(F32)
                    rinv = rinv_refs[2 * g + which][p]
                    xh = xr * rinv
                    dn = _rope_t(src[p], c, s1, s2)
                    row = 2 * g + which
                    dgain_ref[row:row + 1, :] += jnp.sum(dn * xh, axis=0, keepdims=True)
                    dxh2 = dn * gains[g:g + 1, :]
                    dx = rinv * (dxh2 - xh * _seg_mean(dxh2 * xh, lo))
                    o_ref[:, col:col + PAIR] = dx.astype(BF16)
                col = (8 + g) * 512 + p * PAIR
                o_ref[:, col:col + PAIR] = dv_ref[p].astype(BF16)

    pm = pl.BlockSpec((NPAIR, tm, PAIR), lambda i: (0, i, 0))
    tab = pl.BlockSpec((tm, PAIR), lambda i: (i, 0))
    gain = pl.BlockSpec((3, PAIR), lambda i: (0, 0))
    vec = pl.BlockSpec((1, 512), lambda i: (0, 0))
    full = pl.BlockSpec((tm, AB_IN), lambda i: (i, 0))
    w4 = pl.BlockSpec((4, 128, 128), lambda i: (0, 0, 0))
    return pl.pallas_call(
        body, name="ab_in_bwd", grid=(T // tm,),
        in_specs=[full, pl.BlockSpec((tm, 512), lambda i: (i, 0)), vec, vec, w4,
                  pl.BlockSpec((4, 128, 1), lambda i: (0, 0, 0)), gain, gain, tab, tab, tab] + [pm] * 15,
        out_specs=[full, w4, w4, vec, vec, pl.BlockSpec((8, PAIR), lambda i: (0, 0))],
        out_shape=[_sds((T, AB_IN), BF16), _sds((4, 128, 128), F32), _sds((4, 128, 128), F32),
                   _sds((1, 512), F32), _sds((1, 512), F32), _sds((8, PAIR), F32)],
        compiler_params=_cparams(1))(pab, dcat, sgu_g, sgu_b, sgu_w, sgu_bias3, qg, kg, *tabs, *dqkv, *rinvs)


def _ln_stats(x):
    mu = jnp.mean(x, axis=-1, keepdims=True)
    xc = x - mu
    rstd = lax.rsqrt(jnp.mean(xc * xc, axis=-1, keepdims=True) + EPS)
    return xc * rstd, rstd


CONV_RC = 64


def _shifted_copies(src, dst, tm):
    dst[0] = src[...]
    for b in range(1, 8):
        dst[b, 0:tm + HALO - 8, :] = src[pl.ds(b, tm + HALO - 8), :]


def _offsets_by_phase(first):
    groups = {}
    for o in range(first, first + CONV_C_TAPS):
        groups.setdefault(o % 8, []).append(o)
    return sorted(groups.items())


def _window(shifted, b8, base, offsets, lanes):
    rows = 8 * (max(offsets) // 8) + CONV_RC
    return shifted[b8, pl.ds(base, rows), lanes].reshape(rows // 8, 8, 128)


def _cd_fwd(pcd, cw, cb, lg, lb, dw, tm=512):
    per = tm // HALO

    def body(p_ref, h_ref, cw_ref, cb_ref, lg_ref, lb_ref, dw_ref, cat_ref, c0_ref, c1_ref, dd_ref, y_ref,
             buf, buf2, sb):
        i = pl.program_id(0)
        live = jnp.where(i > 0, 1.0, 0.0)
        a = p_ref[:, 0:512].astype(F32)
        gt = p_ref[:, 512:1024].astype(F32)
        gb = p_ref[:, 1024:1536].astype(F32)
        gc = p_ref[:, 1536:2048].astype(F32)
        hv = p_ref[:, 2048:2560].astype(F32)
        c0 = a * _sigmoid(gt)
        dd = gc * hv
        buf[0:HALO, :] = h_ref[:, 0:512].astype(F32) * _sigmoid(h_ref[:, 512:1024].astype(F32)) * live
        buf[HALO:, :] = c0
        buf2[0:HALO, :] = h_ref[:, 1536:2048].astype(F32) * h_ref[:, 2048:2560].astype(F32) * live
        buf2[HALO:, :] = dd
        c0_ref[...] = c0.astype(BF16)
        dd_ref[...] = dd.astype(BF16)
        _shifted_copies(buf, sb, tm)

        def conv_rows(r, carry):
            base = pl.multiple_of(r * CONV_RC, CONV_RC)
            for c in range(4):
                lanes = slice(c * 128, (c + 1) * 128)
                acc = jnp.broadcast_to(cb_ref[:, lanes], (CONV_RC // 8, 8, 128))
                for b8, offsets in _offsets_by_phase(HALO - (CONV_C_TAPS - 1)):
                    win = _window(sb, b8, base, offsets, lanes)
                    for o in offsets:
                        j = o - (HALO - (CONV_C_TAPS - 1))
                        acc = acc + cw_ref[8 * j:8 * j + 8, lanes] * win[o // 8:o // 8 + CONV_RC // 8]
                c1_ref[pl.ds(base, CONV_RC), lanes] = acc.reshape(CONV_RC, 128)
            return carry

        lax.fori_loop(0, tm // CONV_RC, conv_rows, 0)
        xhat, _ = _ln_stats(c1_ref[...])
        c2 = xhat * lg_ref[...] + lb_ref[...]
        y = jnp.zeros((tm, 512), F32)
        for j in range(CONV_D_TAPS):
            y = y + dw_ref[j:j + 1, :] * buf2[pl.ds(HALO - (CONV_D_TAPS - 1) + j, tm), :]
        cat_ref[:, 0:512] = (c2 * _sigmoid(c2)).astype(BF16)
        cat_ref[:, 512:1024] = (gb * y).astype(BF16)
        y_ref[...] = y.astype(BF16)

    half = pl.BlockSpec((tm, 512), lambda i: (i, 0))
    vec = pl.BlockSpec((1, 512), lambda i: (0, 0))
    return pl.pallas_call(
        body, name="cd_fwd", grid=(T // tm,),
        in_specs=[pl.BlockSpec((tm, CD_IN), lambda i: (i, 0)),
                  pl.BlockSpec((HALO, CD_IN), lambda i: (jnp.maximum(i * per - 1, 0), 0)),
                  pl.BlockSpec((8 * 32, 512), lambda i: (0, 0)), vec, vec, vec, pl.BlockSpec((8, 512), lambda i: (0, 0))],
        out_specs=[pl.BlockSpec((tm, D), lambda i: (i, 0)), half, half, half, half],
        out_shape=[_sds((T, D), BF16), _sds((T, 512), BF16), _sds((T, 512), F32), _sds((T, 512), BF16),
                   _sds((T, 512), BF16)],
        scratch_shapes=[pltpu.VMEM((HALO + tm, 512), F32), pltpu.VMEM((HALO + tm, 512), F32),
                        pltpu.VMEM((8, HALO + tm, 512), F32)],
        compiler_params=_cparams(1))(pcd, pcd, cw, cb, lg, lb, dw)


def _cd_bwd_pw(dcat, c1, pcd, y, lg, lb, tm=512):
    def body(dcat_ref, c1_ref, gb_ref, y_ref, lg_ref, lb_ref, dc1_ref, dy3_ref, dgb_ref, dlg_ref, dlb_ref, dcb_ref):
        i = pl.program_id(0)

        @pl.when(i == 0)
        def _():
            dlg_ref[...] = jnp.zeros_like(dlg_ref)
            dlb_ref[...] = jnp.zeros_like(dlb_ref)
            dcb_ref[...] = jnp.zeros_like(dcb_ref)

        dc = dcat_ref[:, 0:512].astype(F32)
        ddo = dcat_ref[:, 512:1024].astype(F32)
        xhat, rstd = _ln_stats(c1_ref[...])
        c2 = xhat * lg_ref[...] + lb_ref[...]
        sg = _sigmoid(c2)
        dc2 = dc * sg * (1.0 + c2 * (1.0 - sg))
        dlg_ref[...] += jnp.sum(dc2 * xhat, axis=0, keepdims=True)
        dlb_ref[...] += jnp.sum(dc2, axis=0, keepdims=True)
        dxh = dc2 * lg_ref[...]
        dc1 = rstd * (dxh - jnp.mean(dxh, axis=-1, keepdims=True)
                      - xhat * jnp.mean(dxh * xhat, axis=-1, keepdims=True))
        dcb_ref[...] += jnp.sum(dc1, axis=0, keepdims=True)
        dc1_ref[...] = dc1
        dgb_ref[...] = (ddo * y_ref[...].astype(F32)).astype(BF16)
        dy3_ref[...] = ddo * gb_ref[...].astype(F32)

    half = pl.BlockSpec((tm, 512), lambda i: (i, 0))
    vec = pl.BlockSpec((1, 512), lambda i: (0, 0))
    return pl.pallas_call(
        body, name="cd_bwd_pw", grid=(T // tm,),
        in_specs=[pl.BlockSpec((tm, D), lambda i: (i, 0)), half, pl.BlockSpec((tm, 512), lambda i: (i, 2)), half,
                  vec, vec],
        out_specs=[half, half, half, vec, vec, vec],
        out_shape=[_sds((T, 512), F32), _sds((T, 512), F32), _sds((T, 512), BF16),
                   _sds((1, 512), F32), _sds((1, 512), F32), _sds((1, 512), F32)],
        compiler_params=_cparams(1))(dcat, c1, pcd, y, lg, lb)


def _cd_bwd_conv(pcd, dc1, dy3, c0, dd, dgb, cw8, dw, tm=256):
    per = tm // HALO
    nblk = T // tm
    last32 = T // HALO - 1

    def body(p_ref, dc1_ref, dc1n_ref, dy3_ref, dy3n_ref, c0_ref, dd_ref, dgb_ref, cw_ref, dw_ref,
             o_ref, dcw_ref, ddw_ref, dbuf, d3buf, sd, dc0_buf):
        i = pl.program_id(0)
        has_next = jnp.where(i < nblk - 1, 1.0, 0.0)

        @pl.when(i == 0)
        def _():
            dcw_ref[...] = jnp.zeros_like(dcw_ref)
            ddw_ref[...] = jnp.zeros_like(ddw_ref)

        dbuf[0:tm, :] = dc1_ref[...]
        dbuf[tm:, :] = dc1n_ref[...] * has_next
        d3buf[0:tm, :] = dy3_ref[...]
        d3buf[tm:, :] = dy3n_ref[...] * has_next
        _shifted_copies(dbuf, sd, tm)
        n_tiles = tm // CONV_RC

        phases = _offsets_by_phase(0)

        def dc0_rows(r, carry):
            base = pl.multiple_of(r * CONV_RC, CONV_RC)
            for c in range(4):
                lanes = slice(c * 128, (c + 1) * 128)
                acc = jnp.zeros((CONV_RC // 8, 8, 128), F32)
                for b8, offsets in phases:
                    win = _window(sd, b8, base, offsets, lanes)
                    for o in offsets:
                        j = CONV_C_TAPS - 1 - o
                        acc = acc + cw_ref[8 * j:8 * j + 8, lanes] * win[o // 8:o // 8 + CONV_RC // 8]
                dc0_buf[pl.ds(base, CONV_RC), lanes] = acc.reshape(CONV_RC, 128)
            return carry

        lax.fori_loop(0, n_tiles, dc0_rows, 0)

        for c in range(4):
            lanes = slice(c * 128, (c + 1) * 128)
            for b8, offsets in phases:
                def dw_rows(r, accs, lanes=lanes, b8=b8, offsets=offsets):
                    base = pl.multiple_of(r * CONV_RC, CONV_RC)
                    xin = c0_ref[pl.ds(base, CONV_RC), lanes].astype(F32).reshape(CONV_RC // 8, 8, 128)
                    win = _window(sd, b8, base, offsets, lanes)
                    return tuple(acc + jnp.sum(xin * win[o // 8:o // 8 + CONV_RC // 8], axis=0)
                                 for acc, o in zip(accs, offsets))

                accs = lax.fori_loop(0, n_tiles, dw_rows, tuple(jnp.zeros((8, 128), F32) for _ in offsets))
                for acc, o in zip(accs, offsets):
                    j = CONV_C_TAPS - 1 - o
                    dcw_ref[j:j + 1, lanes] += jnp.sum(acc, axis=0, keepdims=True)

        dc0 = dc0_buf[...]
        ddin = dd_ref[...].astype(F32)
        ddd = jnp.zeros((tm, 512), F32)
        for j in range(CONV_D_TAPS):
            dy_shift = d3buf[pl.ds(CONV_D_TAPS - 1 - j, tm), :]
            ddd = ddd + dw_ref[j:j + 1, :] * dy_shift
            ddw_ref[j:j + 1, :] += jnp.sum(ddin * dy_shift, axis=0, keepdims=True)

        a = p_ref[:, 0:512].astype(F32)
        gt = p_ref[:, 512:1024].astype(F32)
        gc = p_ref[:, 1536:2048].astype(F32)
        hv = p_ref[:, 2048:2560].astype(F32)
        sg = _sigmoid(gt)
        o_ref[:, 0:512] = (dc0 * sg).astype(BF16)
        o_ref[:, 512:1024] = (dc0 * a * sg * (1.0 - sg)).astype(BF16)
        o_ref[:, 1024:1536] = dgb_ref[...]
        o_ref[:, 1536:2048] = (ddd * hv).astype(BF16)
        o_ref[:, 2048:2560] = (ddd * gc).astype(BF16)

    half = pl.BlockSpec((tm, 512), lambda i: (i, 0))
    nxt = pl.BlockSpec((HALO, 512), lambda i: (jnp.minimum((i + 1) * per, last32), 0))
    full = pl.BlockSpec((tm, CD_IN), lambda i: (i, 0))
    return pl.pallas_call(
        body, name="cd_bwd_conv", grid=(nblk,),
        in_specs=[full, half, nxt, half, nxt, half, half, half,
                  pl.BlockSpec((8 * 32, 512), lambda i: (0, 0)), pl.BlockSpec((8, 512), lambda i: (0, 0))],
        out_specs=[full, pl.BlockSpec((32, 512), lambda i: (0, 0)), pl.BlockSpec((8, 512), lambda i: (0, 0))],
        out_shape=[_sds((T, CD_IN), BF16), _sds((32, 512), F32), _sds((8, 512), F32)],
        scratch_shapes=[pltpu.VMEM((tm + HALO, 512), F32), pltpu.VMEM((tm + HALO, 512), F32),
                        pltpu.VMEM((8, tm + HALO, 512), F32), pltpu.VMEM((tm, 512), F32)],
        compiler_params=_cparams(1))(pcd, dc1, dc1, dy3, dy3, c0, dd, dgb, cw8, dw)


def _local_step(x, tgt, W, fetch=None, on_grad=None):
    W = dict(W)
    if fetch is None:
        fetch = lambda stage, after: {}
    if on_grad is None:
        on_grad = lambda key, arr: None
    tabs = _rope_tables()
    qg = jnp.tile(W["q_norm_g"], (1, 2))
    kg = jnp.tile(W["k_norm_g"], (1, 2))
    bias3 = W["sgu_bias"].reshape(4, 128, 1)
    cw8 = jnp.repeat(W["conv_c_w32"], 8, axis=0)
    G = {}

    h0 = _rms_fwd(x, W["ab_norm_g"], "rms_fwd_ab")
    pab = _mm_nt(h0, W["wt_ab_in"], "mm_ab_in", dep=W.get("dep0"))
    cat_ab = _mix_a_fwd(pab, W["sgu_norm_g"], W["sgu_norm_b"], W["sgu_w"], bias3)
    qkv, rinvs = _prep_fwd(pab, qg, kg, tabs)
    outs, lses = [], []
    for g, rate in enumerate(DIL_RATES):
        o, l = _attn_fwd(qkv[3 * g], qkv[3 * g + 1], qkv[3 * g + 2], rate, f"attn_fwd_{g}", dep=W.get(f"dep_attn{g}"))
        outs.append(o)
        lses.append(l)
        W.update(fetch(f"attn{g}", o))
    cat_ab, lse = _merge_fwd(cat_ab, outs, lses)
    W.update(fetch("ab_out", lse))
    x1, h1 = _mm_nn(cat_ab, W["w_ab_out"], "mm_ab_out", mode="rms", resid=x, gain=W["ffn_norm_g"][0:1])
    pf0, act0 = _ffn_in(h1, W["wt_ffn_in0"], "ffn_in0")
    W.update(fetch("ffn_down0", act0))
    x2, h2 = _mm_nn(act0, W["w_ffn_down0"], "mm_ffn_down0", mode="rms", resid=x1, gain=W["cd_norm_g"],
                    dep=W.get("dep_down0"))
    W.update(fetch("cd_in", h2))
    pcd = _mm_nt(h2, W["wt_cd_in"], "mm_cd_in")
    cat_cd, c0, c1, dd, yv = _cd_fwd(pcd, cw8, W["conv_c_b"], W["c_ln_g"], W["c_ln_b"], W["conv_d_w8"])
    x3, h3 = _mm_nn(cat_cd, W["w_cd_out"], "mm_cd_out", mode="rms", resid=x2, gain=W["ffn_norm_g"][1:2])
    pf1, act1 = _ffn_in(h3, W["wt_ffn_in1"], "ffn_in1")
    dy, dyb, loss_cols = _mm_nn(act1, W["w_ffn_down1"], "mm_ffn_down1", mode="loss", resid=x3, tgt=tgt)

    def ffn_bwd(xin, h, pf, act, dres, dresb, layer):
        G[f"w_ffn_down{layer}"] = _mm_tn(act, dresb, f"mm_g_ffn_down{layer}")
        dep = on_grad(f"w_ffn_down{layer}", G[f"w_ffn_down{layer}"])
        dpf = _ffn_dact(dresb, W[f"w_ffn_down{layer}"], pf, f"ffn_dact{layer}", dep=dep)
        G[f"wt_ffn_in{layer}"] = _mm_tn(dpf, h, f"mm_g_ffn_in{layer}")
        dep = on_grad(f"wt_ffn_in{layer}", G[f"wt_ffn_in{layer}"])
        dx, dxb, G[f"ffn_norm_g{layer}"] = _mm_dh_rms_bwd(
            dpf, W[f"wt_ffn_in{layer}"], xin, W["ffn_norm_g"][layer:layer + 1], dres, f"mm_d_h_ffn{layer}", dep=dep)
        return dx, dxb

    dx3, dx3b = ffn_bwd(x3, h3, pf1, act1, dy, dyb, 1)

    G["w_cd_out"] = _mm_tn(cat_cd, dx3b, "mm_g_cd_out")
    dep = on_grad("w_cd_out", G["w_cd_out"])
    dcat_cd = _mm_nt(dx3b, W["w_cd_out"], "mm_d_cat_cd", dep=dep)
    dc1, dy3, dgb, G["c_ln_g"], G["c_ln_b"], G["conv_c_b"] = _cd_bwd_pw(dcat_cd, c1, pcd, yv, W["c_ln_g"], W["c_ln_b"])
    dpcd, G["conv_c_w32"], G["conv_d_w8"] = _cd_bwd_conv(pcd, dc1, dy3, c0, dd, dgb, cw8, W["conv_d_w8"])
    G["wt_cd_in"] = _mm_tn(dpcd, h2, "mm_g_cd_in")
    dep = on_grad("wt_cd_in", G["wt_cd_in"])
    dx2, dx2b, G["cd_norm_g"] = _mm_dh_rms_bwd(dpcd, W["wt_cd_in"], x2, W["cd_norm_g"], dx3, "mm_d_h_cd", dep=dep)

    dx1, dx1b = ffn_bwd(x1, h1, pf0, act0, dx2, dx2b, 0)

    G["w_ab_out"] = _mm_tn(cat_ab, dx1b, "mm_g_ab_out")
    dep = on_grad("w_ab_out", G["w_ab_out"])
    dcat_ab = _mm_nt(dx1b, W["w_ab_out"], "mm_d_cat_ab", dep=dep)
    dbp, e = _b_pre_bwd(dcat_ab, cat_ab)
    dqkv = []
    for g, rate in enumerate(DIL_RATES):
        dqkv += _attn_bwd(qkv[3 * g], qkv[3 * g + 1], qkv[3 * g + 2], dbp, e, lse, rate, f"attn_bwd_{g}")
    dpab, G["sgu_w"], dbias_part, G["sgu_norm_g"], G["sgu_norm_b"], dgain = _ab_in_bwd(
        pab, dcat_ab, W["sgu_norm_g"], W["sgu_norm_b"], W["sgu_w"], bias3, qg, kg, tabs, dqkv, rinvs)
    G["sgu_bias"] = jnp.sum(dbias_part, axis=-1)
    dgain = dgain[0:6, 0:HEAD] + dgain[0:6, HEAD:PAIR]
    G["q_norm_g"] = dgain[0::2]
    G["k_norm_g"] = dgain[1::2]
    G["wt_ab_in"] = _mm_tn(dpab, h0, "mm_g_ab_in")
    dep = on_grad("wt_ab_in", G["wt_ab_in"])
    grad_x, _, G["ab_norm_g"] = _mm_dh_rms_bwd(dpab, W["wt_ab_in"], x, W["ab_norm_g"], dx1, "mm_d_h_ab", dep=dep)
    return loss_cols, grad_x, G


def _my_place():
    return lax.axis_index("x"), lax.axis_index("y"), lax.axis_index("c")


def _dev_index(px, py, pc):
    return 4 * px + 2 * py + pc


def _flip(place, k):
    x, y, c = place
    return (1 - x if k & 4 else x, 1 - y if k & 2 else y, 1 - c if k & 1 else c)


def _landing(shape, dtype, own):
    buf = lax.empty(shape, dtype)
    for lead, part in own:
        buf = lax.dynamic_update_slice(buf, part.reshape((1,) * len(lead) + part.shape),
                                       tuple(lead) + (0,) * part.ndim)
    return buf


def _gather_first(ab_in_t, small, me_index):
    lands = [_landing((NDEV,) + ab_in_t.shape, BF16, [((me_index,), ab_in_t)]),
             _landing((NDEV,) + small.shape, F32, [((me_index,), small)])]
    n_items = 2

    def body(ab_in_r, small_r, l_ab_in, l_small, o_ab_in, o_small, send_sems, recv_sems):
        del l_ab_in, l_small
        x, y, c = _my_place()
        me = (x, y, c)
        sib = (x, y, 1 - c)
        chips = [(1 - x, y), (x, 1 - y), (1 - x, 1 - y)]
        items = [(ab_in_r, lambda d: o_ab_in.at[d]), (small_r, lambda d: o_small.at[d])]

        def rcopy(it, k, src, dst, to):
            return pltpu.make_async_remote_copy(src_ref=src, dst_ref=dst, send_sem=send_sems.at[it, k],
                                                recv_sem=recv_sems.at[it, k], device_id=to, device_id_type=MESH)

        started = []
        for it, (src, dst) in enumerate(items):
            mine = dst(_dev_index(*me))
            first = [rcopy(it, 0, src, mine, sib)]
            first += [rcopy(it, 1 + j, src, mine, (*chip, c)) for j, chip in enumerate(chips)]
            for cp in first:
                cp.start()
            started += first
        for it, (src, dst) in enumerate(items):
            for j, chip in enumerate(chips):
                blk = dst(_dev_index(*chip, c))
                rcopy(it, 1 + j, blk, blk, me).wait_recv()
                fwd = rcopy(it, 4 + j, blk, blk, sib)
                fwd.start()
                started.append(fwd)
        for it, (src, dst) in enumerate(items):
            blk = dst(_dev_index(x, y, 1 - c))
            rcopy(it, 0, blk, blk, me).wait_recv()
            for j, chip in enumerate(chips):
                blk = dst(_dev_index(*chip, 1 - c))
                rcopy(it, 4 + j, blk, blk, me).wait_recv()
        for cp in started:
            cp.wait_send()

    return pl.pallas_call(
        body, name="gather_first", in_specs=[HBM_SPEC] * 4, out_specs=[HBM_SPEC] * 2,
        out_shape=[_sds(a.shape, a.dtype) for a in lands], input_output_aliases={2: 0, 3: 1},
        scratch_shapes=[pltpu.SemaphoreType.DMA((n_items, 7)), pltpu.SemaphoreType.DMA((n_items, 7))],
    )(ab_in_t, small, *lands)


HBM_ONLY = pl.BlockSpec(memory_space=pltpu.HBM)
SEM_SPEC = pl.BlockSpec(memory_space=pltpu.SEMAPHORE)
IN_FLIGHT = pltpu.CompilerParams(has_side_effects=pltpu.SideEffectType.DATAFLOW_SIDE_EFFECTING)


def _in_hbm(a):
    return pltpu.with_memory_space_constraint(a, pltpu.HBM)


def _exchange_start(name, srcs, lands, items, dep=None):
    ns, nl, ni = len(srcs), len(lands), len(items)

    def body(*refs):
        S, L = refs[0:ns], refs[ns:ns + nl]
        first_out = ns + nl + (0 if dep is None else 1)
        send_sems, recv_sems, token = refs[first_out], refs[first_out + 1], refs[-1]
        me = _my_place()
        mi = _dev_index(*me)
        for i, (src, dst) in enumerate(items):
            for k in range(1, NDEV):
                peer = _flip(me, k)
                pltpu.make_async_remote_copy(
                    src_ref=src(S, _dev_index(*peer)), dst_ref=dst(L, mi), send_sem=send_sems.at[7 * i + k - 1],
                    recv_sem=recv_sems.at[7 * i + k - 1], device_id=peer, device_id_type=MESH).start()
        token[...] = jnp.zeros_like(token)

    thru = [pltpu.HBM(a.shape, a.dtype) for a in list(srcs) + list(lands)]
    args = [_in_hbm(a) for a in srcs] + [_in_hbm(a) for a in lands]
    in_specs = [HBM_ONLY] * (ns + nl)
    if dep is not None:
        args.append(dep)
        in_specs.append(HBM_SPEC)
    outs = pl.pallas_call(
        body, name=name, in_specs=in_specs,
        out_shape=(pltpu.SemaphoreType.DMA((7 * ni,)), pltpu.SemaphoreType.DMA((7 * ni,)), *thru, _sds((8, 128), F32)),
        out_specs=(SEM_SPEC, SEM_SPEC, *[HBM_ONLY] * (ns + nl), pl.BlockSpec(memory_space=pltpu.VMEM)),
        input_output_aliases={j: 2 + j for j in range(ns + nl)}, compiler_params=IN_FLIGHT)(*args)
    return dict(send=outs[0], recv=outs[1], srcs=list(outs[2:2 + ns]), lands=list(outs[2 + ns:2 + ns + nl]),
                token=outs[-1], items=items)


def _exchange_wait(name, states, after):
    after = list(after) if isinstance(after, (list, tuple)) else [after]
    counts = [(len(st["srcs"]), len(st["lands"]), len(st["items"])) for st in states]
    n_arrays = sum(c[0] + c[1] for c in counts)

    def body(*refs):
        me = _my_place()
        mi = _dev_index(*me)
        pos = 0
        sem_pos = n_arrays
        for st, (ns, nl, ni) in zip(states, counts):
            S, L = refs[pos:pos + ns], refs[pos + ns:pos + ns + nl]
            send_sems, recv_sems = refs[sem_pos], refs[sem_pos + 1]
            pos += ns + nl
            sem_pos += 2
            for i, (src, dst) in enumerate(st["items"]):
                for k in range(1, NDEV):
                    cp = pltpu.make_async_remote_copy(
                        src_ref=src(S, mi), dst_ref=dst(L, mi), send_sem=send_sems.at[7 * i + k - 1],
                        recv_sem=recv_sems.at[7 * i + k - 1], device_id=me, device_id_type=MESH)
                    cp.wait_send()
                    cp.wait_recv()

    arrays, sems = [], []
    for st in states:
        arrays += st["srcs"] + st["lands"]
        sems += [st["send"], st["recv"]]
    outs = pl.pallas_call(
        body, name=name, in_specs=[HBM_ONLY] * n_arrays + [SEM_SPEC] * len(sems) + [HBM_SPEC] * len(after),
        out_shape=tuple(pltpu.HBM(a.shape, a.dtype) for a in arrays), out_specs=tuple([HBM_ONLY] * n_arrays),
        input_output_aliases={j: j for j in range(n_arrays)}, compiler_params=IN_FLIGHT)(*arrays, *sems, *after)
    lands, pos = [], 0
    for ns, nl, _ in counts:
        lands.append(list(outs[pos + ns:pos + ns + nl]))
        pos += ns + nl
    return lands


def _place_and_neighbours():
    x, y, c = _my_place()
    return (x, y, c), (x, y, 1 - c), [(1 - x, y), (x, 1 - y), (1 - x, 1 - y)]


def _gather_start(name, srcs, lands, items, dep=None):
    ns, nl, ni = len(srcs), len(lands), len(items)

    def body(*refs):
        S, L = refs[0:ns], refs[ns:ns + nl]
        first_out = ns + nl + (0 if dep is None else 1)
        send_sems, recv_sems, token = refs[first_out], refs[first_out + 1], refs[-1]
        me, sib, chips = _place_and_neighbours()
        mi = _dev_index(*me)
        for i, (src, dst) in enumerate(items):
            for k, to in enumerate([sib] + [(*chip, me[2]) for chip in chips]):
                pltpu.make_async_remote_copy(
                    src_ref=src(S), dst_ref=dst(L, mi), send_sem=send_sems.at[4 * i + k],
                    recv_sem=recv_sems.at[4 * i + k], device_id=to, device_id_type=MESH).start()
        token[...] = jnp.zeros_like(token)

    thru = [pltpu.HBM(a.shape, a.dtype) for a in list(srcs) + list(lands)]
    args = [_in_hbm(a) for a in srcs] + [_in_hbm(a) for a in lands]
    in_specs = [HBM_ONLY] * (ns + nl)
    if dep is not None:
        args.append(dep)
        in_specs.append(HBM_SPEC)
    outs = pl.pallas_call(
        body, name=name, in_specs=in_specs,
        out_shape=(pltpu.SemaphoreType.DMA((4 * ni,)), pltpu.SemaphoreType.DMA((4 * ni,)), *thru, _sds((8, 128), F32)),
        out_specs=(SEM_SPEC, SEM_SPEC, *[HBM_ONLY] * (ns + nl), pl.BlockSpec(memory_space=pltpu.VMEM)),
        input_output_aliases={j: 2 + j for j in range(ns + nl)}, compiler_params=IN_FLIGHT)(*args)
    return dict(send=outs[0], recv=outs[1], srcs=list(outs[2:2 + ns]), lands=list(outs[2 + ns:2 + ns + nl]),
                token=outs[-1], items=items)


def _gather_forward(name, st, after):
    nl, ni = len(st["lands"]), len(st["items"])

    def body(*refs):
        L, recv_sems = refs[0:nl], refs[nl]
        fwd_send, fwd_recv, token = refs[2 * nl + 2], refs[2 * nl + 3], refs[-1]
        me, sib, chips = _place_and_neighbours()
        for i, (_, dst) in enumerate(st["items"]):
            for j, chip in enumerate(chips):
                blk = dst(L, _dev_index(*chip, me[2]))
                pltpu.make_async_remote_copy(
                    src_ref=blk, dst_ref=blk, send_sem=fwd_send.at[3 * i + j], recv_sem=recv_sems.at[4 * i + 1 + j],
                    device_id=me, device_id_type=MESH).wait_recv()
                pltpu.make_async_remote_copy(
                    src_ref=blk, dst_ref=blk, send_sem=fwd_send.at[3 * i + j], recv_sem=fwd_recv.at[3 * i + j],
                    device_id=sib, device_id_type=MESH).start()
        token[...] = jnp.zeros_like(token)

    outs = pl.pallas_call(
        body, name=name, in_specs=[HBM_ONLY] * nl + [SEM_SPEC, HBM_SPEC],
        out_shape=(*[pltpu.HBM(a.shape, a.dtype) for a in st["lands"]], pltpu.SemaphoreType.DMA((3 * ni,)),
                   pltpu.SemaphoreType.DMA((3 * ni,)), _sds((8, 128), F32)),
        out_specs=(*[HBM_ONLY] * nl, SEM_SPEC, SEM_SPEC, pl.BlockSpec(memory_space=pltpu.VMEM)),
        input_output_aliases={j: j for j in range(nl)}, compiler_params=IN_FLIGHT)(*st["lands"], st["recv"], after)
    return dict(st, lands=list(outs[0:nl]), fwd_send=outs[nl], fwd_recv=outs[nl + 1], token=outs[-1])


def _gather_wait(name, st, after):
    ns, nl, ni = len(st["srcs"]), len(st["lands"]), len(st["items"])

    def body(*refs):
        S, L = refs[0:ns], refs[ns:ns + nl]
        send_sems, recv_sems, fwd_send, fwd_recv = refs[ns + nl:ns + nl + 4]
        me, sib, chips = _place_and_neighbours()
        mi = _dev_index(*me)
        for i, (src, dst) in enumerate(st["items"]):
            mine = dst(L, mi)
            for k in range(4):
                pltpu.make_async_remote_copy(
                    src_ref=src(S), dst_ref=mine, send_sem=send_sems.at[4 * i + k], recv_sem=recv_sems.at[4 * i + k],
                    device_id=me, device_id_type=MESH).wait_send()
            pltpu.make_async_remote_copy(
                src_ref=src(S), dst_ref=mine, send_sem=send_sems.at[4 * i], recv_sem=recv_sems.at[4 * i],
                device_id=me, device_id_type=MESH).wait_recv()
            for j in range(3):
                cp = pltpu.make_async_remote_copy(
                    src_ref=mine, dst_ref=mine, send_sem=fwd_send.at[3 * i + j], recv_sem=fwd_recv.at[3 * i + j],
                    device_id=me, device_id_type=MESH)
                cp.wait_send()
                cp.wait_recv()

    arrays = st["srcs"] + st["lands"]
    outs = pl.pallas_call(
        body, name=name, in_specs=[HBM_ONLY] * (ns + nl) + [SEM_SPEC] * 4 + [HBM_SPEC],
        out_shape=tuple(pltpu.HBM(a.shape, a.dtype) for a in arrays), out_specs=tuple([HBM_ONLY] * (ns + nl)),
        input_output_aliases={j: j for j in range(ns + nl)},
        compiler_params=IN_FLIGHT)(*arrays, st["send"], st["recv"], st["fwd_send"], st["fwd_recv"], after)
    return list(outs[ns:ns + nl])


def _sum_slots(land):
    def body(l_ref, o_ref):
        acc = l_ref[0]
        for d in range(1, NDEV):
            acc = acc + l_ref[d]
        o_ref[...] = acc

    vm = pl.BlockSpec(memory_space=pltpu.VMEM)
    return pl.pallas_call(body, name="sum_small", out_shape=_sds(land.shape[1:], F32), in_specs=[vm], out_specs=vm)(land)


def _adam_math(w, g, m, v):
    m2 = ADAM_B1 * m + (1.0 - ADAM_B1) * g
    v2 = ADAM_B2 * v + (1.0 - ADAM_B2) * (g * g)
    delta = -ADAM_LR * ((m2 * ADAM_C1) / (jnp.sqrt(v2 * ADAM_C2) + ADAM_EPS) + ADAM_WD * w)
    return delta, m2, v2


def _adam_layer(land, sel, w, m, v, layer, name, prev=None, tc=512):
    R = land.shape[2]

    def body(l_ref, w_ref, m_ref, v_ref, *rest):
        g_out, d_out, m_out, v_out = rest[-4:]
        g = l_ref[0].astype(F32)
        for d in range(1, NDEV):
            g = g + l_ref[d].astype(F32)
        delta, m2, v2 = _adam_math(w_ref[...], g, m_ref[...], v_ref[...])
        g_out[...] = g
        d_out[...] = delta
        m_out[...] = m2
        v_out[...] = v2

    wspec = pl.BlockSpec((None, R, tc), lambda i: (layer, 0, i))
    in_specs = [pl.BlockSpec((None, NDEV, R, tc), lambda i: (sel, 0, 0, i)), wspec, wspec, wspec]
    args = [land, w, m, v]
    aliases = {}
    if prev is not None:
        in_specs += [HBM_SPEC] * 4
        args += list(prev)
        aliases = {4 + j: j for j in range(4)}
    return pl.pallas_call(
        body, name=name, grid=(D // tc,), in_specs=in_specs, out_specs=[wspec] * 4,
        out_shape=[_sds(w.shape, F32)] * 4, input_output_aliases=aliases, compiler_params=_cparams(1))(*args)


def _adam_stacked(lands, sel, w, m, v, name):
    res = None
    for layer, land in enumerate(lands):
        res = _adam_layer(land, sel, w, m, v, layer, f"{name}{layer}", prev=res)
    return res


def _adam_small(ws, gs, ms, vs):
    n = len(ws)

    def body(*refs):
        w_r, g_r, m_r, v_r = refs[0:n], refs[n:2 * n], refs[2 * n:3 * n], refs[3 * n:4 * n]
        d_o, m_o, v_o = refs[4 * n:5 * n], refs[5 * n:6 * n], refs[6 * n:7 * n]
        for i in range(n):
            delta, m2, v2 = _adam_math(w_r[i][...], g_r[i][...], m_r[i][...], v_r[i][...])
            d_o[i][...] = delta
            m_o[i][...] = m2
            v_o[i][...] = v2

    vm = pl.BlockSpec(memory_space=pltpu.VMEM)
    shapes = [_sds(w.shape, F32) for w in ws]
    outs = pl.pallas_call(body, name="adam_small", in_specs=[vm] * (4 * n), out_specs=[vm] * (3 * n),
                          out_shape=shapes * 3)(*ws, *gs, *ms, *vs)
    return outs[0:n], outs[n:2 * n], outs[2 * n:3 * n]


WEIGHT_NAMES = ("ab_norm_g", "ab_w_in", "sgu_norm_g", "sgu_norm_b", "sgu_w", "sgu_bias", "q_norm_g", "k_norm_g",
                "ab_w_out", "cd_norm_g", "cd_w_in", "conv_c_w", "conv_c_b", "c_ln_g", "c_ln_b", "conv_d_w",
                "cd_w_out", "ffn_norm_g", "ffn_w_gate", "ffn_w_up", "ffn_w_down")
SMALL_2D = (("ab_norm_g", (1, 1024)), ("sgu_norm_g", (1, 512)), ("sgu_norm_b", (1, 512)), ("sgu_w", (512, 128)),
            ("sgu_bias", (4, 128)), ("q_norm_g", (3, 64)), ("k_norm_g", (3, 64)), ("cd_norm_g", (1, 128)),
            ("conv_c_w", (31, 64)), ("conv_c_b", (1, 64)), ("c_ln_g", (1, 64)), ("c_ln_b", (1, 64)),
            ("conv_d_w", (3, 64)), ("ffn_norm_g", (2, 1024)))
SHARD_C = 64


def _pack_rows(parts, rows):
    flat = jnp.concatenate([p.reshape(-1) for p in parts])
    return jnp.pad(flat, (0, rows * 128 - flat.shape[0])).reshape(rows, 128)


def kernel(x, ab_norm_g, ab_w_in, sgu_norm_g, sgu_norm_b, sgu_w, sgu_bias, q_norm_g, k_norm_g, ab_w_out, cd_norm_g, cd_w_in, conv_c_w, conv_c_b, c_ln_g, c_ln_b, conv_d_w, cd_w_out, ffn_norm_g, ffn_w_gate, ffn_w_up, ffn_w_down, loss_target, m_ab_norm_g, m_ab_w_in, m_sgu_norm_g, m_sgu_norm_b, m_sgu_w, m_sgu_bias, m_q_norm_g, m_k_norm_g, m_ab_w_out, m_cd_norm_g, m_cd_w_in, m_conv_c_w, m_conv_c_b, m_c_ln_g, m_c_ln_b, m_conv_d_w, m_cd_w_out, m_ffn_norm_g, m_ffn_w_gate, m_ffn_w_up, m_ffn_w_down, v_ab_norm_g, v_ab_w_in, v_sgu_norm_g, v_sgu_norm_b, v_sgu_w, v_sgu_bias, v_q_norm_g, v_k_norm_g, v_ab_w_out, v_cd_norm_g, v_cd_w_in, v_conv_c_w, v_conv_c_b, v_c_ln_g, v_c_ln_b, v_conv_d_w, v_cd_w_out, v_ffn_norm_g, v_ffn_w_gate, v_ffn_w_up, v_ffn_w_down):
    w = dict(zip(WEIGHT_NAMES, (ab_norm_g, ab_w_in, sgu_norm_g, sgu_norm_b, sgu_w, sgu_bias, q_norm_g, k_norm_g, ab_w_out, cd_norm_g, cd_w_in, conv_c_w, conv_c_b, c_ln_g, c_ln_b, conv_d_w, cd_w_out, ffn_norm_g, ffn_w_gate, ffn_w_up, ffn_w_down)))
    m = dict(zip(WEIGHT_NAMES, (m_ab_norm_g, m_ab_w_in, m_sgu_norm_g, m_sgu_norm_b, m_sgu_w, m_sgu_bias, m_q_norm_g, m_k_norm_g, m_ab_w_out, m_cd_norm_g, m_cd_w_in, m_conv_c_w, m_conv_c_b, m_c_ln_g, m_c_ln_b, m_conv_d_w, m_cd_w_out, m_ffn_norm_g, m_ffn_w_gate, m_ffn_w_up, m_ffn_w_down)))
    v = dict(zip(WEIGHT_NAMES, (v_ab_norm_g, v_ab_w_in, v_sgu_norm_g, v_sgu_norm_b, v_sgu_w, v_sgu_bias, v_q_norm_g, v_k_norm_g, v_ab_w_out, v_cd_norm_g, v_cd_w_in, v_conv_c_w, v_conv_c_b, v_c_ln_g, v_c_ln_b, v_conv_d_w, v_cd_w_out, v_ffn_norm_g, v_ffn_w_gate, v_ffn_w_up, v_ffn_w_down)))
    me = _dev_index(*_my_place())

    small_local = _pack_rows([w["cd_norm_g"], w["conv_c_w"], w["conv_c_b"], w["c_ln_g"], w["c_ln_b"], w["conv_d_w"]], 24)
    o_ab_in, o_small = _gather_first(w["ab_w_in"][0].T.astype(BF16), small_local, me)
    r_ff = DFF // NDEV
    one = lambda a: (lambda S, j: S[a])
    slot = lambda b: (lambda L, s: L[b].at[s])
    slot2 = lambda b, part: (lambda L, s: L[b].at[part, s])
    shard = lambda a: (lambda S: S[a])

    def layer_shards(layer):
        return (w["ffn_w_gate"][layer].T.astype(BF16), w["ffn_w_up"][layer].T.astype(BF16),
                w["ffn_w_down"][layer].astype(BF16))

    def gathered(own):
        return _landing((NDEV,) + own.shape, BF16, [((me,), own)])

    def gathered2(a, b):
        return _landing((2, NDEV) + a.shape, BF16, [((0, me), a), ((1, me), b)])

    ab_out_s = w["ab_w_out"][0].astype(BF16)
    gate0, up0, down0 = layer_shards(0)
    gathers = {1: _gather_start(
        "gather1_start", [ab_out_s, gate0, up0, down0], [gathered(ab_out_s), gathered2(gate0, up0), gathered(down0)],
        [(shard(0), slot(0)), (shard(1), slot2(1, 0)), (shard(2), slot2(1, 1)), (shard(3), slot(2))], dep=o_small)}

    def fetch(stage, after):
        if stage == "attn0":
            cd_in_s, cd_out_s = w["cd_w_in"][0].T.astype(BF16), w["cd_w_out"][0].astype(BF16)
            gate1, up1, down1 = layer_shards(1)
            gathers[2] = _gather_start(
                "gather2_start", [cd_in_s, cd_out_s, gate1, up1, down1],
                [gathered(cd_in_s), gathered(cd_out_s), gathered2(gate1, up1), gathered(down1)],
                [(shard(0), slot(0)), (shard(1), slot(1)), (shard(2), slot2(2, 0)), (shard(3), slot2(2, 1)),
                 (shard(4), slot(3))], dep=after)
            return {"dep_attn1": gathers[2]["token"]}
        if stage == "attn1":
            gathers[1] = _gather_forward("gather1_forward", gathers[1], after)
            return {"dep_attn2": gathers[1]["token"]}
        if stage == "ab_out":
            l_out, l_ffn, l_down = _gather_wait("gather1_wait", gathers[1], after)
            return {"w_ab_out": l_out.reshape(D, D), "wt_ffn_in0": l_ffn.reshape(2 * DFF, D),
                    "w_ffn_down0": l_down.reshape(DFF, D)}
        if stage == "ffn_down0":
            gathers[2] = _gather_forward("gather2_forward", gathers[2], after)
            return {"dep_down0": gathers[2]["token"]}
        if stage == "cd_in":
            l_in, l_out, l_ffn, l_down = _gather_wait("gather2_wait", gathers[2], after)
            return {"wt_cd_in": l_in.reshape(CD_IN, D), "w_cd_out": l_out.reshape(D, D),
                    "wt_ffn_in1": l_ffn.reshape(2 * DFF, D), "w_ffn_down1": l_down.reshape(DFF, D)}
        return {}

    scatters = {}
    rides_with = {"w_ffn_down1": "wt_ffn_in1", "w_cd_out": "wt_cd_in", "w_ffn_down0": "wt_ffn_in0"}
    held = {}

    def on_grad(key, arr):
        if key in rides_with:
            held[rides_with[key]] = (key, arr)
            return None
        group = ([held.pop(key)] if key in held else []) + [(key, arr)]
        srcs, lands, items = [], [], []
        for n, (k, a) in enumerate(group):
            if k.startswith("wt_ffn_in"):
                src = a.reshape(2, NDEV, r_ff, D)
                own = lax.dynamic_slice_in_dim(src, me, 1, axis=1)
                lands.append(lax.dynamic_update_slice(lax.empty(src.shape, BF16), own, (0, me, 0, 0)))
                items += [((lambda S, j, n=n: S[n].at[0, j]), slot2(n, 0)), ((lambda S, j, n=n: S[n].at[1, j]), slot2(n, 1))]
            else:
                rows = a.shape[0] // NDEV
                src = a.reshape(NDEV, rows, D)
                own = lax.dynamic_index_in_dim(src, me, 0, keepdims=False)
                lands.append(_landing((1, NDEV, rows, D), BF16, [((0, me), own)]))
                items.append(((lambda S, j, n=n: S[n].at[j]), slot2(n, 0)))
            srcs.append(src)
        st = _exchange_start(f"scatter_{key}_start", srcs, lands, items)
        scatters[key] = (st, [k for k, _ in group])
        return st["token"]

    flat = o_small.reshape(NDEV, 24 * 128)

    def chan(lo, taps):
        return flat[:, lo:lo + taps * SHARD_C].reshape(NDEV, taps, SHARD_C).transpose(1, 0, 2).reshape(taps, 512)

    W = {
        "wt_ab_in": o_ab_in.reshape(AB_IN, D), "dep0": gathers[1]["token"],
        "ab_norm_g": w["ab_norm_g"], "sgu_norm_g": w["sgu_norm_g"], "sgu_norm_b": w["sgu_norm_b"],
        "sgu_w": w["sgu_w"][0], "sgu_bias": w["sgu_bias"][0], "q_norm_g": w["q_norm_g"][0],
        "k_norm_g": w["k_norm_g"][0], "ffn_norm_g": w["ffn_norm_g"],
        "cd_norm_g": flat[:, 0:128].reshape(1, D),
        "conv_c_w32": jnp.pad(chan(128, CONV_C_TAPS), ((0, 1), (0, 0))),
        "conv_c_b": chan(2112, 1), "c_ln_g": chan(2176, 1), "c_ln_b": chan(2240, 1),
        "conv_d_w8": jnp.pad(chan(2304, CONV_D_TAPS), ((0, 8 - CONV_D_TAPS), (0, 0))),
    }

    loss_cols, grad_x, G = _local_step(x[0], loss_target[0], W, fetch, on_grad)

    small_parts = [G["ab_norm_g"], G["sgu_norm_g"], G["sgu_norm_b"], G["sgu_w"], G["sgu_bias"], G["q_norm_g"],
                   G["k_norm_g"], G["cd_norm_g"], G["conv_c_w32"][:CONV_C_TAPS], G["conv_c_b"], G["c_ln_g"],
                   G["c_ln_b"], G["conv_d_w8"][:CONV_D_TAPS], G["ffn_norm_g0"], G["ffn_norm_g1"], loss_cols]
    sizes = [p.size for p in small_parts]
    small_rows = 720
    packed = _pack_rows(small_parts, small_rows)
    small = _exchange_start("small_start", [packed], [_landing((NDEV, small_rows, 128), F32, [((me,), packed)])],
                            [(one(0), slot(0))])
    landed = {}

    def wait_scatters(name, group_keys, after):
        res = _exchange_wait(name, [scatters[gk][0] for gk in group_keys], after)
        for gk, lands in zip(group_keys, res):
            landed.update(zip(scatters[gk][1], lands))

    wait_scatters("scatter_wait_early", ["wt_ffn_in1", "wt_cd_in", "wt_ffn_in0", "w_ab_out"], small["token"])

    grads, deltas, new_m, new_v = {}, {}, {}, {}
    done = []

    def put(name, res):
        grads[name], deltas[name], new_m[name], new_v[name] = res

    def adam(name, lands, sel, transposed):
        flip = (lambda a: jnp.swapaxes(a, 1, 2)) if transposed else (lambda a: a)
        res = _adam_stacked(lands, sel, flip(w[name]), flip(m[name]), flip(v[name]), f"adam_{name}")
        done.append(res[1])
        put(name, [flip(r) for r in res])

    ffn_in_lands = [landed["wt_ffn_in0"], landed["wt_ffn_in1"]]
    adam("cd_w_in", [landed["wt_cd_in"]], 0, True)
    adam("ffn_w_gate", ffn_in_lands, 0, True)
    adam("ffn_w_up", ffn_in_lands, 1, True)
    adam("cd_w_out", [landed["w_cd_out"]], 0, False)
    adam("ab_w_out", [landed["w_ab_out"]], 0, False)
    adam("ffn_w_down", [landed["w_ffn_down0"], landed["w_ffn_down1"]], 0, False)

    small_land = _exchange_wait("small_wait", [small], list(done))[0][0]
    red = _sum_slots(small_land).reshape(-1)
    offs = [0]
    for s in sizes:
        offs.append(offs[-1] + s)
    seg = [red[offs[i]:offs[i + 1]] for i in range(len(sizes))]
    loss = jnp.sum(seg[15])

    def own_channels(full, taps):
        return lax.dynamic_slice_in_dim(full.reshape(taps, 512), me * SHARD_C, SHARD_C, axis=1)

    g_small = {
        "ab_norm_g": seg[0].reshape(1, 1024), "sgu_norm_g": seg[1].reshape(1, 512), "sgu_norm_b": seg[2].reshape(1, 512),
        "sgu_w": seg[3].reshape(512, 128), "sgu_bias": seg[4].reshape(4, 128), "q_norm_g": seg[5].reshape(3, 64),
        "k_norm_g": seg[6].reshape(3, 64),
        "cd_norm_g": lax.dynamic_slice_in_dim(seg[7].reshape(1, D), me * (D // NDEV), D // NDEV, axis=1),
        "conv_c_w": own_channels(seg[8], CONV_C_TAPS), "conv_c_b": own_channels(seg[9], 1),
        "c_ln_g": own_channels(seg[10], 1), "c_ln_b": own_channels(seg[11], 1),
        "conv_d_w": own_channels(seg[12], CONV_D_TAPS),
        "ffn_norm_g": jnp.concatenate([seg[13].reshape(1, D), seg[14].reshape(1, D)], axis=0),
    }

    names2d = [n for n, _ in SMALL_2D]
    d_s, m_s, v_s = _adam_small([w[n].reshape(s) for n, s in SMALL_2D], [g_small[n] for n in names2d],
                                [m[n].reshape(s) for n, s in SMALL_2D], [v[n].reshape(s) for n, s in SMALL_2D])
    for i, n in enumerate(names2d):
        shape = w[n].shape
        grads[n], deltas[n] = g_small[n].reshape(shape), d_s[i].reshape(shape)
        new_m[n], new_v[n] = m_s[i].reshape(shape), v_s[i].reshape(shape)

    wait_scatters("scatter_wait_last", ["wt_ab_in"], d_s[0])
    adam("ab_w_in", [landed["wt_ab_in"]], 0, True)

    return (loss, grad_x[None], *[grads[n] for n in WEIGHT_NAMES], *[deltas[n] for n in WEIGHT_NAMES],
            *[new_m[n] for n in WEIGHT_NAMES], *[new_v[n] for n in WEIGHT_NAMES])
```

```python
import functools

import jax
import jax.numpy as jnp
import numpy as np
from jax import lax
from jax.experimental import pallas as pl
from jax.experimental.pallas import tpu as pltpu

F32 = jnp.float32
BF16 = jnp.bfloat16

T = 4096
D = 1024
NDEV = 8
EPS = 1e-6
NEG_INF = -1e30
DFF = 2816
AB_IN = 5632
CD_IN = 2560
HEAD = 64
PAIR = 128
NPAIR = 4
NBACK = 128
DIL_RATES = (1, 4, 16)
ROPE_HALF = 8
ROPE_THETA = 500000.0
CONV_C_TAPS = 31
CONV_D_TAPS = 3
HALO = 32
ATTN_BWD_UNROLL = 4

ADAM_LR = 0.001
ADAM_B1 = 0.9
ADAM_B2 = 0.999
ADAM_EPS = 1e-08
ADAM_WD = 0.01
ADAM_STEP = 10
ADAM_C1 = 1.0 / (1.0 - ADAM_B1 ** ADAM_STEP)
ADAM_C2 = 1.0 / (1.0 - ADAM_B2 ** ADAM_STEP)

VMEM_LIMIT_MB = 48
MESH = pl.DeviceIdType.MESH
HBM_SPEC = pl.BlockSpec(memory_space=pl.ANY)


def _cparams(ngrid, vmem_mb=VMEM_LIMIT_MB):
    return pltpu.CompilerParams(dimension_semantics=("arbitrary",) * ngrid,
                                vmem_limit_bytes=vmem_mb * 1024 * 1024)


def _pick(n, options):
    for o in options:
        if n % o == 0:
            return o
    raise ValueError(f"no tile for {n} in {options}")


def _sds(shape, dtype):
    return jax.ShapeDtypeStruct(shape, dtype)


def _sigmoid(x):
    return 1.0 / (1.0 + jnp.exp(-x))


def _sigmoid_bf16(x):
    return 0.5 * jnp.tanh(0.5 * x) + 0.5


def _gelu(z):
    return 0.5 * z * (1.0 + lax.erf(z * 0.7071067811865476))


def _gelu_grad(z):
    return 0.5 * (1.0 + lax.erf(z * 0.7071067811865476)) + z * jnp.exp(-0.5 * z * z) * 0.3989422804014327


def _mm_nt(a, wt, name, out_dtype=BF16, tm=2048, dep=None):
    M, K = a.shape
    N = wt.shape[0]
    tn = _pick(N, (512, 256))

    def body(a_ref, w_ref, *rest):
        o_ref = rest[-1]
        o_ref[...] = lax.dot_general(a_ref[...], w_ref[...], (((1,), (1,)), ((), ())),
                                     preferred_element_type=F32).astype(o_ref.dtype)

    in_specs = [pl.BlockSpec((tm, K), lambda i, j: (i, 0)), pl.BlockSpec((tn, K), lambda i, j: (j, 0))]
    args = [a, wt]
    if dep is not None:
        in_specs.append(HBM_SPEC)
        args.append(dep)
    return pl.pallas_call(
        body, name=name, grid=(M // tm, N // tn), in_specs=in_specs,
        out_specs=pl.BlockSpec((tm, tn), lambda i, j: (i, j)),
        out_shape=_sds((M, N), out_dtype), compiler_params=_cparams(2))(*args)


EPI_ROWS = 256


def _mm_nn(a, w, name, mode, resid, gain=None, tgt=None, dep=None, tm=512):
    M, K = a.shape
    N = w.shape[1]
    side = gain if mode == "rms" else tgt

    def body(a_ref, w_ref, resid_ref, side_ref, *rest):
        outs, acc = rest[-3 if mode == "rms" else -4:-1], rest[-1]
        i = pl.program_id(0)
        acc[...] = jnp.dot(a_ref[...], w_ref[...], preferred_element_type=F32)

        if mode == "loss":
            @pl.when(i == 0)
            def _():
                outs[2][...] = jnp.zeros_like(outs[2])

        for r0 in range(0, tm, EPI_ROWS):
            rows = slice(r0, r0 + EPI_ROWS)
            v = acc[rows, :] + resid_ref[rows, :]
            if mode == "rms":
                outs[0][rows, :] = v
                r = lax.rsqrt(jnp.mean(v * v, axis=-1, keepdims=True) + EPS)
                outs[1][rows, :] = (v * r * side_ref[...]).astype(BF16)
            else:
                d = v - side_ref[rows, :]
                outs[2][...] += jnp.sum(d * d, axis=0, keepdims=True) * (0.5 / N)
                dy = d * (1.0 / N)
                outs[0][rows, :] = dy
                outs[1][rows, :] = dy.astype(BF16)

    row = pl.BlockSpec((tm, N), lambda i: (i, 0))
    vec = pl.BlockSpec((1, N), lambda i: (0, 0))
    in_specs = [pl.BlockSpec((tm, K), lambda i: (i, 0)),
                pl.BlockSpec((K, N), lambda i: (0, 0), pipeline_mode=pl.Buffered(1)), row,
                vec if mode == "rms" else row]
    args = [a, w, resid, side]
    if dep is not None:
        in_specs.append(HBM_SPEC)
        args.append(dep)
    if mode == "rms":
        out_specs, out_shape = [row, row], [_sds((M, N), F32), _sds((M, N), BF16)]
    else:
        out_specs, out_shape = [row, row, vec], [_sds((M, N), F32), _sds((M, N), BF16), _sds((1, N), F32)]
    return pl.pallas_call(
        body, name=name, grid=(M // tm,), in_specs=in_specs, out_specs=out_specs, out_shape=out_shape,
        scratch_shapes=[pltpu.VMEM((tm, N), F32)], compiler_params=_cparams(1))(*args)


def _mm_dh_rms_bwd(a, w, x, gain, dres, name, dep=None, tm=512):
    parts = a.shape[0] if a.ndim == 3 else 1
    M, Kp = a.shape[-2], a.shape[-1]
    N = w.shape[1]
    nblk = M // tm
    assert nblk % 2 == 0

    def body(a_ref, w_ref, x_ref, g_ref, dres_ref, *rest):
        dx_ref, dxb_ref, dg_ref, acc0, acc1 = rest[-5:]
        i = pl.program_id(0)

        def matmul(acc):
            if parts == 1:
                acc[...] = jnp.dot(a_ref[...], w_ref[...], preferred_element_type=F32)
            else:
                d = jnp.dot(a_ref[0], w_ref[0:Kp, :], preferred_element_type=F32)
                for p in range(1, parts):
                    d = d + jnp.dot(a_ref[p], w_ref[p * Kp:(p + 1) * Kp, :], preferred_element_type=F32)
                acc[...] = d

        def finish(acc):
            for r0 in range(0, tm, EPI_ROWS // 2):
                rows = slice(r0, r0 + EPI_ROWS // 2)
                v = acc[rows, :]
                xf = x_ref[rows, :]
                r = lax.rsqrt(jnp.mean(xf * xf, axis=-1, keepdims=True) + EPS)
                xhat = xf * r
                dg_ref[...] += jnp.sum(v * xhat, axis=0, keepdims=True)
                dxh = v * g_ref[...]
                tot = dres_ref[rows, :] + r * (dxh - xhat * jnp.mean(dxh * xhat, axis=-1, keepdims=True))
                dx_ref[rows, :] = tot
                dxb_ref[rows, :] = tot.astype(BF16)

        @pl.when(i == 0)
        def _():
            dg_ref[...] = jnp.zeros_like(dg_ref)
            matmul(acc0)

        @pl.when((i > 0) & (i < nblk) & (i % 2 == 1))
        def _():
            matmul(acc1)
            finish(acc0)

        @pl.when((i > 0) & (i < nblk) & (i % 2 == 0))
        def _():
            matmul(acc0)
            finish(acc1)

        @pl.when(i == nblk)
        def _():
            finish(acc1)

    last = nblk - 1
    row = pl.BlockSpec((tm, N), lambda i: (jnp.maximum(i - 1, 0), 0))
    vec = pl.BlockSpec((1, N), lambda i: (0, 0))
    if a.ndim == 3:
        a_spec = pl.BlockSpec((parts, tm, Kp), lambda i: (0, jnp.minimum(i, last), 0))
    else:
        a_spec = pl.BlockSpec((tm, Kp), lambda i: (jnp.minimum(i, last), 0))
    w_spec = pl.BlockSpec((parts * Kp, N), lambda i: (0, 0), pipeline_mode=pl.Buffered(1))
    in_specs = [a_spec, w_spec, row, vec, row]
    args = [a, w, x, gain, dres]
    if dep is not None:
        in_specs.append(HBM_SPEC)
        args.append(dep)
    return pl.pallas_call(
        body, name=name, grid=(nblk + 1,), in_specs=in_specs, out_specs=[row, row, vec],
        out_shape=[_sds((M, N), F32), _sds((M, N), BF16), _sds((1, N), F32)],
        scratch_shapes=[pltpu.VMEM((tm, N), F32), pltpu.VMEM((tm, N), F32)], compiler_params=_cparams(1, 56))(*args)


def _mm_tn(a, b, name, out_dtype=BF16, tt=1024):
    parts = a.shape[0] if a.ndim == 3 else 1
    Tt, Mp = a.shape[-2], a.shape[-1]
    N = b.shape[1]
    tn = _pick(Mp, (1408, 1280, 1024, 512))
    jper = Mp // tn
    nt = Tt // tt

    def body(a_ref, b_ref, o_ref, acc):
        t = pl.program_id(1)

        @pl.when(t == 0)
        def _():
            acc[...] = jnp.zeros_like(acc)

        acc[...] += lax.dot_general(a_ref[...], b_ref[...], (((0,), (0,)), ((), ())),
                                    preferred_element_type=F32)

        @pl.when(t == nt - 1)
        def _():
            o_ref[...] = acc[...].astype(o_ref.dtype)

    if a.ndim == 3:
        a_spec = pl.BlockSpec((None, tt, tn), lambda j, t: (j // jper, t, j % jper))
    else:
        a_spec = pl.BlockSpec((tt, tn), lambda j, t: (t, j))
    return pl.pallas_call(
        body, name=name, grid=(parts * jper, nt),
        in_specs=[a_spec, pl.BlockSpec((tt, N), lambda j, t: (t, 0))],
        out_specs=pl.BlockSpec((tn, N), lambda j, t: (j, 0)),
        out_shape=_sds((parts * Mp, N), out_dtype), scratch_shapes=[pltpu.VMEM((tn, N), F32)],
        compiler_params=_cparams(2))(a, b)


def _ffn_in(h, wt_in, name, tm=2048, tn=256):
    nj = DFF // tn

    def body(h_ref, wg_ref, wu_ref, p_ref, act_ref):
        nt = (((1,), (1,)), ((), ()))
        g = lax.dot_general(h_ref[...], wg_ref[...], nt, preferred_element_type=F32).astype(BF16)
        u = lax.dot_general(h_ref[...], wu_ref[...], nt, preferred_element_type=F32).astype(BF16)
        p_ref[0] = g
        p_ref[1] = u
        act_ref[...] = g * _sigmoid_bf16(g) * u

    return pl.pallas_call(
        body, name=name, grid=(T // tm, nj),
        in_specs=[pl.BlockSpec((tm, D), lambda i, j: (i, 0)), pl.BlockSpec((tn, D), lambda i, j: (j, 0)),
                  pl.BlockSpec((tn, D), lambda i, j: (j + nj, 0))],
        out_specs=[pl.BlockSpec((2, tm, tn), lambda i, j: (0, i, j)), pl.BlockSpec((tm, tn), lambda i, j: (i, j))],
        out_shape=[_sds((2, T, DFF), BF16), _sds((T, DFF), BF16)], compiler_params=_cparams(2))(h, wt_in, wt_in)


def _ffn_dact(dyb, w_down, p3, name, tm=2048, tn=256, dep=None):
    def body(dy_ref, w_ref, p_ref, *rest):
        o_ref = rest[-1]
        da = lax.dot_general(dy_ref[...], w_ref[...], (((1,), (1,)), ((), ())),
                             preferred_element_type=F32).astype(BF16)
        g = p_ref[0]
        u = p_ref[1]
        sg = _sigmoid_bf16(g)
        gs = g * sg
        o_ref[0] = (da * u) * (sg + gs * (1.0 - sg))
        o_ref[1] = da * gs

    pspec = pl.BlockSpec((2, tm, tn), lambda i, j: (0, i, j))
    in_specs = [pl.BlockSpec((tm, D), lambda i, j: (i, 0)), pl.BlockSpec((tn, D), lambda i, j: (j, 0)), pspec]
    args = [dyb, w_down, p3]
    if dep is not None:
        in_specs.append(HBM_SPEC)
        args.append(dep)
    return pl.pallas_call(
        body, name=name, grid=(T // tm, DFF // tn), in_specs=in_specs, out_specs=pspec,
        out_shape=_sds((2, T, DFF), BF16), compiler_params=_cparams(2))(*args)


def _rms_fwd(x, g, name, tm=512):
    def body(x_ref, g_ref, h_ref):
        xf = x_ref[...]
        r = lax.rsqrt(jnp.mean(xf * xf, axis=-1, keepdims=True) + EPS)
        h_ref[...] = (xf * r * g_ref[...]).astype(BF16)

    return pl.pallas_call(
        body, name=name, grid=(T // tm,),
        in_specs=[pl.BlockSpec((tm, D), lambda i: (i, 0)), pl.BlockSpec((1, D), lambda i: (0, 0))],
        out_specs=pl.BlockSpec((tm, D), lambda i: (i, 0)),
        out_shape=_sds((T, D), BF16), compiler_params=_cparams(1))(x, g)


def _tril_mask():
    r = lax.broadcasted_iota(jnp.int32, (128, 128), 0)
    c = lax.broadcasted_iota(jnp.int32, (128, 128), 1)
    return r >= c


def _mix_a_fwd(pab, sgu_g, sgu_b, sgu_w, sgu_bias3, tm=512):
    def body(zu_ref, zv_ref, g_ref, b_ref, w_ref, bias_ref, o_ref):
        u = _gelu(zu_ref[...].astype(F32))
        v = _gelu(zv_ref[...].astype(F32))
        mu = jnp.mean(v, axis=-1, keepdims=True)
        vc = v - mu
        rstd = lax.rsqrt(jnp.mean(vc * vc, axis=-1, keepdims=True) + EPS)
        vn = (vc * rstd * g_ref[...] + b_ref[...]).astype(BF16)
        tri = _tril_mask()
        for gi in range(4):
            wg = jnp.where(tri, w_ref[gi], 0.0).astype(BF16)
            bg = bias_ref[gi]
            for c in range(tm // 128):
                rs, cs = slice(c * 128, (c + 1) * 128), slice(gi * 128, (gi + 1) * 128)
                mixed = jnp.dot(wg, vn[rs, cs], preferred_element_type=F32) + bg
                o_ref[rs, cs] = (u[rs, cs] * mixed).astype(BF16)

    half = pl.BlockSpec((tm, 512), lambda i: (i, 0))
    return pl.pallas_call(
        body, name="mix_a_fwd", grid=(T // tm,),
        in_specs=[half, pl.BlockSpec((tm, 512), lambda i: (i, 1)),
                  pl.BlockSpec((1, 512), lambda i: (0, 0)), pl.BlockSpec((1, 512), lambda i: (0, 0)),
                  pl.BlockSpec((4, 128, 128), lambda i: (0, 0, 0)), pl.BlockSpec((4, 128, 1), lambda i: (0, 0, 0))],
        out_specs=half, out_shape=_sds((T, D), BF16), compiler_params=_cparams(1),
    )(pab, pab, sgu_g, sgu_b, sgu_w, sgu_bias3)


def _rope_tables():
    pos = np.arange(T, dtype=np.float32)
    inv_freq = np.float32(ROPE_THETA) ** (-np.arange(ROPE_HALF, dtype=np.float32) * np.float32(2.0 / (2 * ROPE_HALF)))
    ang = (pos[:, None] * inv_freq[None, :]).astype(np.float32)
    cos, sin = np.cos(ang), np.sin(ang)
    z8 = np.zeros((T, ROPE_HALF), np.float32)
    rest = np.zeros((T, HEAD - 2 * ROPE_HALF), np.float32)
    c64 = np.concatenate([cos, cos, rest + 1.0], axis=1)
    s1 = np.concatenate([z8, sin, rest], axis=1)
    s2 = np.concatenate([-sin, z8, rest], axis=1)
    return tuple(jnp.asarray(np.tile(t, (1, 2)).astype(np.float32)) for t in (c64, s1, s2))


def _lo_mask(shape):
    return lax.broadcasted_iota(jnp.int32, shape, 1) < HEAD


def _seg_mean(x, lo):
    s_all = jnp.sum(x, axis=-1, keepdims=True)
    s_lo = jnp.sum(jnp.where(lo, x, 0.0), axis=-1, keepdims=True)
    return jnp.where(lo, s_lo, s_all - s_lo) * (1.0 / HEAD)


def _rope(n, c, s1, s2):
    return n * c + pltpu.roll(n, ROPE_HALF, 1) * s1 + pltpu.roll(n, PAIR - ROPE_HALF, 1) * s2


def _rope_t(dy, c, s1, s2):
    return dy * c - pltpu.roll(dy, PAIR - ROPE_HALF, 1) * s2 - pltpu.roll(dy, ROPE_HALF, 1) * s1


def _prep_fwd(pab, qg, kg, tabs, tm=512):
    def body(p_ref, qg_ref, kg_ref, c_ref, s1_ref, s2_ref, *outs):
        lo = _lo_mask((tm, PAIR))
        c, s1, s2 = c_ref[...], s1_ref[...], s2_ref[...]
        for g in range(3):
            qn_ref, kn_ref, v_ref = outs[3 * g:3 * g + 3]
            for p in range(NPAIR):
                for which, gains, dst in ((0, qg_ref, qn_ref), (1, kg_ref, kn_ref)):
                    col = (2 + 3 * which + g) * 512 + p * PAIR
                    xr = p_ref[:, col:col + PAIR].astype(F32)
                    rinv = lax.rsqrt(_seg_mean(xr * xr, lo) + EPS)
                    outs[9 + 2 * g + which][p] = rinv
                    dst[p] = _rope(xr * rinv * gains[g:g + 1, :], c, s1, s2)
                col = (8 + g) * 512 + p * PAIR
                v_ref[p] = p_ref[:, col:col + PAIR].astype(F32)

    pm = pl.BlockSpec((NPAIR, tm, PAIR), lambda i: (0, i, 0))
    tab = pl.BlockSpec((tm, PAIR), lambda i: (i, 0))
    gain = pl.BlockSpec((3, PAIR), lambda i: (0, 0))
    res = pl.pallas_call(
        body, name="prep_fwd", grid=(T // tm,),
        in_specs=[pl.BlockSpec((tm, AB_IN), lambda i: (i, 0)), gain, gain, tab, tab, tab],
        out_specs=[pm] * 15, out_shape=[_sds((NPAIR, T, PAIR), F32)] * 15,
        compiler_params=_cparams(1))(pab, qg, kg, *tabs)
    return res[0:9], res[9:15]


def _res_index(it, rate):
    window = NBACK * rate
    b = it // rate
    rho = it % rate
    start = b * window + rho
    startp = jnp.maximum(start - window, rho)
    kmin = jnp.where(b > 0, 0, NBACK)
    return start, startp, kmin


def _rows(start, rate):
    if rate == 1:
        return pl.ds(pl.multiple_of(start, NBACK), NBACK)
    return pl.ds(start, NBACK, stride=rate)


def _band_bias():
    qs = lax.broadcasted_iota(jnp.int32, (2 * NBACK, 2 * NBACK), 0)
    kj = lax.broadcasted_iota(jnp.int32, (2 * NBACK, 2 * NBACK), 1)
    dist = (qs & (NBACK - 1)) + NBACK - kj
    both = (dist >= 0) & (dist <= NBACK)
    return jnp.where(both, 0.0, NEG_INF), jnp.where(both & (kj >= NBACK), 0.0, NEG_INF)


def _attn_fwd(qn, kn, v, rate, name, dep=None):
    def body(q_ref, k_ref, v_ref, *rest):
        o_ref, l_ref = rest[-2:]
        lo = _lo_mask((NBACK, PAIR))
        bias_all, bias_first = _band_bias()

        def step(it, carry):
            start, startp, kmin = _res_index(it, rate)
            q = q_ref[_rows(start, rate), :] * (HEAD ** -0.5)
            kcat = jnp.concatenate([k_ref[_rows(startp, rate), :], k_ref[_rows(start, rate), :]], axis=0).astype(BF16)
            vcat = jnp.concatenate([v_ref[_rows(startp, rate), :], v_ref[_rows(start, rate), :]], axis=0).astype(BF16)
            vcat1 = jnp.concatenate([vcat, jnp.ones((2 * NBACK, PAIR), BF16)], axis=1)
            q2 = jnp.concatenate([jnp.where(lo, q, 0.0), jnp.where(lo, 0.0, q)], axis=0).astype(BF16)
            s = lax.dot_general(q2, kcat, (((1,), (1,)), ((), ())), preferred_element_type=F32)
            s = s + jnp.where(kmin == 0, bias_all, bias_first)
            m = jnp.max(s, axis=-1, keepdims=True)
            ol = jnp.dot(jnp.exp(s - m).astype(BF16), vcat1, preferred_element_type=F32)
            o2 = ol[:, 0:PAIR] / ol[:, PAIR:]
            ls = m + jnp.log(ol[:, PAIR:])
            o_ref[_rows(start, rate), :] = jnp.where(lo, o2[0:NBACK], o2[NBACK:])
            l_ref[_rows(start, rate), :] = jnp.where(lo, ls[0:NBACK], ls[NBACK:])
            return carry

        lax.fori_loop(0, T // NBACK, step, 0, unroll=4)

    pm = pl.BlockSpec((None, T, PAIR), lambda p: (p, 0, 0))
    in_specs, args = [pm, pm, pm], [qn, kn, v]
    if dep is not None:
        in_specs.append(HBM_SPEC)
        args.append(dep)
    return pl.pallas_call(
        body, name=name, grid=(NPAIR,), in_specs=in_specs, out_specs=[pm, pm],
        out_shape=[_sds((NPAIR, T, PAIR), F32)] * 2, compiler_params=_cparams(1))(*args)


def _merge_fwd(cat_ab, outs, lses, tm=512):
    def body(cat_in, o0, o1, o2, l0, l1, l2, cat_ref, lse_ref):
        del cat_in
        for p in range(NPAIR):
            a0, a1, a2 = l0[p], l1[p], l2[p]
            m = jnp.maximum(jnp.maximum(a0, a1), a2)
            w0, w1, w2 = jnp.exp(a0 - m), jnp.exp(a1 - m), jnp.exp(a2 - m)
            s = w0 + w1 + w2
            b = (w0 * o0[p] + w1 * o1[p] + w2 * o2[p]) / s
            cat_ref[:, p * PAIR:(p + 1) * PAIR] = b.astype(BF16)
            lse_ref[p] = m + jnp.log(s)

    pm = pl.BlockSpec((NPAIR, tm, PAIR), lambda i: (0, i, 0))
    return pl.pallas_call(
        body, name="merge_fwd", grid=(T // tm,),
        in_specs=[pl.BlockSpec(memory_space=pl.ANY)] + [pm] * 6,
        out_specs=[pl.BlockSpec((tm, 512), lambda i: (i, 1)), pm],
        out_shape=[_sds((T, D), BF16), _sds((NPAIR, T, PAIR), F32)],
        input_output_aliases={0: 0}, compiler_params=_cparams(1))(cat_ab, *outs, *lses)


def _b_pre_bwd(dcat, cat, tm=512):
    def body(db_ref, b_ref, dbp_ref, e_ref):
        lo = _lo_mask((tm, PAIR))
        for p in range(NPAIR):
            db = db_ref[:, p * PAIR:(p + 1) * PAIR].astype(F32)
            b = b_ref[:, p * PAIR:(p + 1) * PAIR].astype(F32)
            dbp_ref[p] = db
            e_ref[p] = _seg_mean(db * b, lo) * float(HEAD)

    pm = pl.BlockSpec((NPAIR, tm, PAIR), lambda i: (0, i, 0))
    right = pl.BlockSpec((tm, 512), lambda i: (i, 1))
    return pl.pallas_call(
        body, name="b_pre_bwd", grid=(T // tm,), in_specs=[right, right], out_specs=[pm, pm],
        out_shape=[_sds((NPAIR, T, PAIR), F32)] * 2, compiler_params=_cparams(1))(dcat, cat)


def _attn_bwd(qn, kn, v, dbp, e, lse, rate, name):
    def body(q_ref, k_ref, v_ref, db_ref, e_ref, lse_ref, dq_ref, dk_ref, dv_ref):
        lo = _lo_mask((NBACK, PAIR))
        bias_all, bias_first = _band_bias()
        scale = HEAD ** -0.5
        nt = (((1,), (1,)), ((), ()))
        tn = (((0,), (0,)), ((), ()))
        window = NBACK * rate
        nblk = T // window

        def one(it, carry):
            dk_carry, dv_carry = carry
            rho = it // nblk
            b = it % nblk
            start = b * window + rho
            rq = _rows(start, rate)
            rp = _rows(jnp.maximum(start - window, rho), rate)
            q = q_ref[rq, :] * scale
            db = db_ref[rq, :]
            ev = e_ref[rq, :]
            ls = lse_ref[rq, :]
            kcat = jnp.concatenate([k_ref[rp, :], k_ref[rq, :]], axis=0).astype(BF16)
            vcat = jnp.concatenate([v_ref[rp, :], v_ref[rq, :]], axis=0).astype(BF16)
            q2 = jnp.concatenate([jnp.where(lo, q, 0.0), jnp.where(lo, 0.0, q)], axis=0).astype(BF16)
            db2 = jnp.concatenate([jnp.where(lo, db, 0.0), jnp.where(lo, 0.0, db)], axis=0).astype(BF16)
            ls2 = jnp.concatenate([ls[:, 0:1], ls[:, HEAD:HEAD + 1]], axis=0)
            ev2 = jnp.concatenate([ev[:, 0:1], ev[:, HEAD:HEAD + 1]], axis=0)
            s = lax.dot_general(q2, kcat, nt, preferred_element_type=F32)
            pt = jnp.exp(s + jnp.where(b > 0, bias_all, bias_first) - ls2)
            dp = lax.dot_general(db2, vcat, nt, preferred_element_type=F32)
            ds = (pt * (dp - ev2)).astype(BF16)
            dq2 = jnp.dot(ds, kcat, preferred_element_type=F32) * scale
            dkc = lax.dot_general(ds, q2, tn, preferred_element_type=F32)
            dvc = lax.dot_general(pt.astype(BF16), db2, tn, preferred_element_type=F32)
            dq_ref[rq, :] = jnp.where(lo, dq2[0:NBACK], dq2[NBACK:])
            dk_ref[rp, :] = dk_carry + dkc[0:NBACK]
            dk_ref[rq, :] = dkc[NBACK:]
            dv_ref[rp, :] = dv_carry + dvc[0:NBACK]
            dv_ref[rq, :] = dvc[NBACK:]
            return dkc[NBACK:], dvc[NBACK:]

        def step(i, carry):
            for u in range(ATTN_BWD_UNROLL):
                carry = one(i * ATTN_BWD_UNROLL + u, carry)
            return carry

        zero = jnp.zeros((NBACK, PAIR), F32)
        lax.fori_loop(0, T // NBACK // ATTN_BWD_UNROLL, step, (zero, zero))

    pm = pl.BlockSpec((None, T, PAIR), lambda p: (p, 0, 0))
    return pl.pallas_call(
        body, name=name, grid=(NPAIR,), in_specs=[pm] * 6, out_specs=[pm] * 3,
        out_shape=[_sds((NPAIR, T, PAIR), F32)] * 3, compiler_params=_cparams(1, 56))(qn, kn, v, dbp, e, lse)


def _ab_in_bwd(pab, dcat, sgu_g, sgu_b, sgu_w, sgu_bias3, qg, kg, tabs, dqkv, rinvs, tm=256):
    def body(p_ref, dcat_ref, g_ref, b_ref, w_ref, bias_ref, qg_ref, kg_ref, c_ref, s1_ref, s2_ref, *rest):
        dq_refs, rinv_refs = rest[0:9], rest[9:15]
        o_ref, dwm_ref, dbias_ref, dsg_ref, dsb_ref, dgain_ref = rest[15:]
        i = pl.program_id(0)

        @pl.when(i == 0)
        def _():
            dwm_ref[...] = jnp.zeros_like(dwm_ref)
            dbias_ref[...] = jnp.zeros_like(dbias_ref)
            dsg_ref[...] = jnp.zeros_like(dsg_ref)
            dsb_ref[...] = jnp.zeros_like(dsb_ref)
            dgain_ref[...] = jnp.zeros_like(dgain_ref)

        zu = p_ref[:, 0:512].astype(F32)
        zv = p_ref[:, 512:1024].astype(F32)
        u = _gelu(zu)
        v = _gelu(zv)
        mu = jnp.mean(v, axis=-1, keepdims=True)
        vc = v - mu
        rstd = lax.rsqrt(jnp.mean(vc * vc, axis=-1, keepdims=True) + EPS)
        xhat = vc * rstd
        vn = (xhat * g_ref[...] + b_ref[...]).astype(BF16)
        da = dcat_ref[...].astype(F32)
        tri = _tril_mask()
        du_parts = [[None] * 4 for _ in range(tm // 128)]
        dvn_parts = [[None] * 4 for _ in range(tm // 128)]
        for gi in range(4):
            wg = jnp.where(tri, w_ref[gi], 0.0).astype(BF16)
            bg = bias_ref[gi]
            for c in range(tm // 128):
                rs, cs = slice(c * 128, (c + 1) * 128), slice(gi * 128, (gi + 1) * 128)
                vblk = vn[rs, cs]
                mixed = jnp.dot(wg, vblk, preferred_element_type=F32) + bg
                dab = da[rs, cs]
                du_parts[c][gi] = dab * mixed
                dmixed = dab * u[rs, cs]
                dmb = dmixed.astype(BF16)
                dvn_parts[c][gi] = lax.dot_general(wg, dmb, (((0,), (0,)), ((), ())), preferred_element_type=F32)
                dwm = lax.dot_general(dmb, vblk, (((1,), (1,)), ((), ())), preferred_element_type=F32)
                dwm_ref[gi] += jnp.where(tri, dwm, 0.0)
                dbias_ref[gi] += dmixed
        du = jnp.concatenate([jnp.concatenate(r, axis=1) for r in du_parts], axis=0)
        dvn = jnp.concatenate([jnp.concatenate(r, axis=1) for r in dvn_parts], axis=0)
        dsg_ref[...] += jnp.sum(dvn * xhat, axis=0, keepdims=True)
        dsb_ref[...] += jnp.sum(dvn, axis=0, keepdims=True)
        dxh = dvn * g_ref[...]
        dv = rstd * (dxh - jnp.mean(dxh, axis=-1, keepdims=True)
                     - xhat * jnp.mean(dxh * xhat, axis=-1, keepdims=True))
        o_ref[:, 0:512] = (du * _gelu_grad(zu)).astype(BF16)
        o_ref[:, 512:1024] = (dv * _gelu_grad(zv)).astype(BF16)

        lo = _lo_mask((tm, PAIR))
        c, s1, s2 = c_ref[...], s1_ref[...], s2_ref[...]
        for g in range(3):
            dq_ref, dk_ref, dv_ref = dq_refs[3 * g:3 * g + 3]
            for p in range(NPAIR):
                for which, gains, src in ((0, qg_ref, dq_ref), (1, kg_ref, dk_ref)):
                    col = (2 + 3 * which + g) * 512 + p * PAIR
                    xr = p_ref[:, col:col + PAIR].astype(F32)
                    rinv = rinv_refs[2 * g + which][p]
                    xh = xr * rinv
                    dn = _rope_t(src[p], c, s1, s2)
                    row = 2 * g + which
                    dgain_ref[row:row + 1, :] += jnp.sum(dn * xh, axis=0, keepdims=True)
                    dxh2 = dn * gains[g:g + 1, :]
                    dx = rinv * (dxh2 - xh * _seg_mean(dxh2 * xh, lo))
                    o_ref[:, col:col + PAIR] = dx.astype(BF16)
                col = (8 + g) * 512 + p * PAIR
                o_ref[:, col:col + PAIR] = dv_ref[p].astype(BF16)

    pm = pl.BlockSpec((NPAIR, tm, PAIR), lambda i: (0, i, 0))
    tab = pl.BlockSpec((tm, PAIR), lambda i: (i, 0))
    gain = pl.BlockSpec((3, PAIR), lambda i: (0, 0))
    vec = pl.BlockSpec((1, 512), lambda i: (0, 0))
    full = pl.BlockSpec((tm, AB_IN), lambda i: (i, 0))
    w4 = pl.BlockSpec((4, 128, 128), lambda i: (0, 0, 0))
    return pl.pallas_call(
        body, name="ab_in_bwd", grid=(T // tm,),
        in_specs=[full, pl.BlockSpec((tm, 512), lambda i: (i, 0)), vec, vec, w4,
                  pl.BlockSpec((4, 128, 1), lambda i: (0, 0, 0)), gain, gain, tab, tab, tab] + [pm] * 15,
        out_specs=[full, w4, w4, vec, vec, pl.BlockSpec((8, PAIR), lambda i: (0, 0))],
        out_shape=[_sds((T, AB_IN), BF16), _sds((4, 128, 128), F32), _sds((4, 128, 128), F32),
                   _sds((1, 512), F32), _sds((1, 512), F32), _sds((8, PAIR), F32)],
        compiler_params=_cparams(1))(pab, dcat, sgu_g, sgu_b, sgu_w, sgu_bias3, qg, kg, *tabs, *dqkv, *rinvs)


def _ln_stats(x):
    mu = jnp.mean(x, axis=-1, keepdims=True)
    xc = x - mu
    rstd = lax.rsqrt(jnp.mean(xc * xc, axis=-1, keepdims=True) + EPS)
    return xc * rstd, rstd


CONV_RC = 64


def _shifted_copies(src, dst, tm):
    dst[0] = src[...]
    for b in range(1, 8):
        dst[b, 0:tm + HALO - 8, :] = src[pl.ds(b, tm + HALO - 8), :]


def _offsets_by_phase(first):
    groups = {}
    for o in range(first, first + CONV_C_TAPS):
        groups.setdefault(o % 8, []).append(o)
    return sorted(groups.items())


def _window(shifted, b8, base, offsets, lanes):
    rows = 8 * (max(offsets) // 8) + CONV_RC
    return shifted[b8, pl.ds(base, rows), lanes].reshape(rows // 8, 8, 128)


def _cd_fwd(pcd, cw, cb, lg, lb, dw, tm=512):
    per = tm // HALO

    def body(p_ref, h_ref, cw_ref, cb_ref, lg_ref, lb_ref, dw_ref, cat_ref, c0_ref, c1_ref, dd_ref, y_ref,
             buf, buf2, sb):
        i = pl.program_id(0)
        live = jnp.where(i > 0, 1.0, 0.0)
        a = p_ref[:, 0:512].astype(F32)
        gt = p_ref[:, 512:1024].astype(F32)
        gb = p_ref[:, 1024:1536].astype(F32)
        gc = p_ref[:, 1536:2048].astype(F32)
        hv = p_ref[:, 2048:2560].astype(F32)
        c0 = a * _sigmoid(gt)
        dd = gc * hv
        buf[0:HALO, :] = h_ref[:, 0:512].astype(F32) * _sigmoid(h_ref[:, 512:1024].astype(F32)) * live
        buf[HALO:, :] = c0
        buf2[0:HALO, :] = h_ref[:, 1536:2048].astype(F32) * h_ref[:, 2048:2560].astype(F32) * live
        buf2[HALO:, :] = dd
        c0_ref[...] = c0.astype(BF16)
        dd_ref[...] = dd.astype(BF16)
        _shifted_copies(buf, sb, tm)

        def conv_rows(r, carry):
            base = pl.multiple_of(r * CONV_RC, CONV_RC)
            for c in range(4):
                lanes = slice(c * 128, (c + 1) * 128)
                acc = jnp.broadcast_to(cb_ref[:, lanes], (CONV_RC // 8, 8, 128))
                for b8, offsets in _offsets_by_phase(HALO - (CONV_C_TAPS - 1)):
                    win = _window(sb, b8, base, offsets, lanes)
                    for o in offsets:
                        j = o - (HALO - (CONV_C_TAPS - 1))
                        acc = acc + cw_ref[8 * j:8 * j + 8, lanes] * win[o // 8:o // 8 + CONV_RC // 8]
                c1_ref[pl.ds(base, CONV_RC), lanes] = acc.reshape(CONV_RC, 128)
            return carry

        lax.fori_loop(0, tm // CONV_RC, conv_rows, 0)
        xhat, _ = _ln_stats(c1_ref[...])
        c2 = xhat * lg_ref[...] + lb_ref[...]
        y = jnp.zeros((tm, 512), F32)
        for j in range(CONV_D_TAPS):
            y = y + dw_ref[j:j + 1, :] * buf2[pl.ds(HALO - (CONV_D_TAPS - 1) + j, tm), :]
        cat_ref[:, 0:512] = (c2 * _sigmoid(c2)).astype(BF16)
        cat_ref[:, 512:1024] = (gb * y).astype(BF16)
        y_ref[...] = y.astype(BF16)

    half = pl.BlockSpec((tm, 512), lambda i: (i, 0))
    vec = pl.BlockSpec((1, 512), lambda i: (0, 0))
    return pl.pallas_call(
        body, name="cd_fwd", grid=(T // tm,),
        in_specs=[pl.BlockSpec((tm, CD_IN), lambda i: (i, 0)),
                  pl.BlockSpec((HALO, CD_IN), lambda i: (jnp.maximum(i * per - 1, 0), 0)),
                  pl.BlockSpec((8 * 32, 512), lambda i: (0, 0)), vec, vec, vec, pl.BlockSpec((8, 512), lambda i: (0, 0))],
        out_specs=[pl.BlockSpec((tm, D), lambda i: (i, 0)), half, half, half, half],
        out_shape=[_sds((T, D), BF16), _sds((T, 512), BF16), _sds((T, 512), F32), _sds((T, 512), BF16),
                   _sds((T, 512), BF16)],
        scratch_shapes=[pltpu.VMEM((HALO + tm, 512), F32), pltpu.VMEM((HALO + tm, 512), F32),
                        pltpu.VMEM((8, HALO + tm, 512), F32)],
        compiler_params=_cparams(1))(pcd, pcd, cw, cb, lg, lb, dw)


def _cd_bwd_pw(dcat, c1, pcd, y, lg, lb, tm=512):
    def body(dcat_ref, c1_ref, gb_ref, y_ref, lg_ref, lb_ref, dc1_ref, dy3_ref, dgb_ref, dlg_ref, dlb_ref, dcb_ref):
        i = pl.program_id(0)

        @pl.when(i == 0)
        def _():
            dlg_ref[...] = jnp.zeros_like(dlg_ref)
            dlb_ref[...] = jnp.zeros_like(dlb_ref)
            dcb_ref[...] = jnp.zeros_like(dcb_ref)

        dc = dcat_ref[:, 0:512].astype(F32)
        ddo = dcat_ref[:, 512:1024].astype(F32)
        xhat, rstd = _ln_stats(c1_ref[...])
        c2 = xhat * lg_ref[...] + lb_ref[...]
        sg = _sigmoid(c2)
        dc2 = dc * sg * (1.0 + c2 * (1.0 - sg))
        dlg_ref[...] += jnp.sum(dc2 * xhat, axis=0, keepdims=True)
        dlb_ref[...] += jnp.sum(dc2, axis=0, keepdims=True)
        dxh = dc2 * lg_ref[...]
        dc1 = rstd * (dxh - jnp.mean(dxh, axis=-1, keepdims=True)
                      - xhat * jnp.mean(dxh * xhat, axis=-1, keepdims=True))
        dcb_ref[...] += jnp.sum(dc1, axis=0, keepdims=True)
        dc1_ref[...] = dc1
        dgb_ref[...] = (ddo * y_ref[...].astype(F32)).astype(BF16)
        dy3_ref[...] = ddo * gb_ref[...].astype(F32)

    half = pl.BlockSpec((tm, 512), lambda i: (i, 0))
    vec = pl.BlockSpec((1, 512), lambda i: (0, 0))
    return pl.pallas_call(
        body, name="cd_bwd_pw", grid=(T // tm,),
        in_specs=[pl.BlockSpec((tm, D), lambda i: (i, 0)), half, pl.BlockSpec((tm, 512), lambda i: (i, 2)), half,
                  vec, vec],
        out_specs=[half, half, half, vec, vec, vec],
        out_shape=[_sds((T, 512), F32), _sds((T, 512), F32), _sds((T, 512), BF16),
                   _sds((1, 512), F32), _sds((1, 512), F32), _sds((1, 512), F32)],
        compiler_params=_cparams(1))(dcat, c1, pcd, y, lg, lb)


def _cd_bwd_conv(pcd, dc1, dy3, c0, dd, dgb, cw8, dw, tm=256):
    per = tm // HALO
    nblk = T // tm
    last32 = T // HALO - 1

    def body(p_ref, dc1_ref, dc1n_ref, dy3_ref, dy3n_ref, c0_ref, dd_ref, dgb_ref, cw_ref, dw_ref,
             o_ref, dcw_ref, ddw_ref, dbuf, d3buf, sd, dc0_buf):
        i = pl.program_id(0)
        has_next = jnp.where(i < nblk - 1, 1.0, 0.0)

        @pl.when(i == 0)
        def _():
            dcw_ref[...] = jnp.zeros_like(dcw_ref)
            ddw_ref[...] = jnp.zeros_like(ddw_ref)

        dbuf[0:tm, :] = dc1_ref[...]
        dbuf[tm:, :] = dc1n_ref[...] * has_next
        d3buf[0:tm, :] = dy3_ref[...]
        d3buf[tm:, :] = dy3n_ref[...] * has_next
        _shifted_copies(dbuf, sd, tm)
        n_tiles = tm // CONV_RC

        phases = _offsets_by_phase(0)

        def dc0_rows(r, carry):
            base = pl.multiple_of(r * CONV_RC, CONV_RC)
            for c in range(4):
                lanes = slice(c * 128, (c + 1) * 128)
                acc = jnp.zeros((CONV_RC // 8, 8, 128), F32)
                for b8, offsets in phases:
                    win = _window(sd, b8, base, offsets, lanes)
                    for o in offsets:
                        j = CONV_C_TAPS - 1 - o
                        acc = acc + cw_ref[8 * j:8 * j + 8, lanes] * win[o // 8:o // 8 + CONV_RC // 8]
                dc0_buf[pl.ds(base, CONV_RC), lanes] = acc.reshape(CONV_RC, 128)
            return carry

        lax.fori_loop(0, n_tiles, dc0_rows, 0)

        for c in range(4):
            lanes = slice(c * 128, (c + 1) * 128)
            for b8, offsets in phases:
                def dw_rows(r, accs, lanes=lanes, b8=b8, offsets=offsets):
                    base = pl.multiple_of(r * CONV_RC, CONV_RC)
                    xin = c0_ref[pl.ds(base, CONV_RC), lanes].astype(F32).reshape(CONV_RC // 8, 8, 128)
                    win = _window(sd, b8, base, offsets, lanes)
                    return tuple(acc + jnp.sum(xin * win[o // 8:o // 8 + CONV_RC // 8], axis=0)
                                 for acc, o in zip(accs, offsets))

                accs = lax.fori_loop(0, n_tiles, dw_rows, tuple(jnp.zeros((8, 128), F32) for _ in offsets))
                for acc, o in zip(accs, offsets):
                    j = CONV_C_TAPS - 1 - o
                    dcw_ref[j:j + 1, lanes] += jnp.sum(acc, axis=0, keepdims=True)

        dc0 = dc0_buf[...]
        ddin = dd_ref[...].astype(F32)
        ddd = jnp.zeros((tm, 512), F32)
        for j in range(CONV_D_TAPS):
            dy_shift = d3buf[pl.ds(CONV_D_TAPS - 1 - j, tm), :]
            ddd = ddd + dw_ref[j:j + 1, :] * dy_shift
            ddw_ref[j:j + 1, :] += jnp.sum(ddin * dy_shift, axis=0, keepdims=True)

        a = p_ref[:, 0:512].astype(F32)
        gt = p_ref[:, 512:1024].astype(F32)
        gc = p_ref[:, 1536:2048].astype(F32)
        hv = p_ref[:, 2048:2560].astype(F32)
        sg = _sigmoid(gt)
        o_ref[:, 0:512] = (dc0 * sg).astype(BF16)
        o_ref[:, 512:1024] = (dc0 * a * sg * (1.0 - sg)).astype(BF16)
        o_ref[:, 1024:1536] = dgb_ref[...]
        o_ref[:, 1536:2048] = (ddd * hv).astype(BF16)
        o_ref[:, 2048:2560] = (ddd * gc).astype(BF16)

    half = pl.BlockSpec((tm, 512), lambda i: (i, 0))
    nxt = pl.BlockSpec((HALO, 512), lambda i: (jnp.minimum((i + 1) * per, last32), 0))
    full = pl.BlockSpec((tm, CD_IN), lambda i: (i, 0))
    return pl.pallas_call(
        body, name="cd_bwd_conv", grid=(nblk,),
        in_specs=[full, half, nxt, half, nxt, half, half, half,
                  pl.BlockSpec((8 * 32, 512), lambda i: (0, 0)), pl.BlockSpec((8, 512), lambda i: (0, 0))],
        out_specs=[full, pl.BlockSpec((32, 512), lambda i: (0, 0)), pl.BlockSpec((8, 512), lambda i: (0, 0))],
        out_shape=[_sds((T, CD_IN), BF16), _sds((32, 512), F32), _sds((8, 512), F32)],
        scratch_shapes=[pltpu.VMEM((tm + HALO, 512), F32), pltpu.VMEM((tm + HALO, 512), F32),
                        pltpu.VMEM((8, tm + HALO, 512), F32), pltpu.VMEM((tm, 512), F32)],
        compiler_params=_cparams(1))(pcd, dc1, dc1, dy3, dy3, c0, dd, dgb, cw8, dw)


def _local_step(x, tgt, W, fetch=None, on_grad=None):
    W = dict(W)
    if fetch is None:
        fetch = lambda stage, after: {}
    if on_grad is None:
        on_grad = lambda key, arr: None
    tabs = _rope_tables()
    qg = jnp.tile(W["q_norm_g"], (1, 2))
    kg = jnp.tile(W["k_norm_g"], (1, 2))
    bias3 = W["sgu_bias"].reshape(4, 128, 1)
    G = {}

    h0 = _rms_fwd(x, W["ab_norm_g"], "rms_fwd_ab")
    W.update(fetch("ab_in", h0))
    pab = _mm_nt(h0, W["wt_ab_in"], "mm_ab_in", dep=W.get("dep0"))
    cat_ab = _mix_a_fwd(pab, W["sgu_norm_g"], W["sgu_norm_b"], W["sgu_w"], bias3)
    qkv, rinvs = _prep_fwd(pab, qg, kg, tabs)
    outs, lses = [], []
    for g, rate in enumerate(DIL_RATES):
        o, l = _attn_fwd(qkv[3 * g], qkv[3 * g + 1], qkv[3 * g + 2], rate, f"attn_fwd_{g}", dep=W.get(f"dep_attn{g}"))
        outs.append(o)
        lses.append(l)
        W.update(fetch(f"attn{g}", o))
    cat_ab, lse = _merge_fwd(cat_ab, outs, lses)
    W.update(fetch("ab_out", lse))
    x1, h1 = _mm_nn(cat_ab, W["w_ab_out"], "mm_ab_out", mode="rms", resid=x, gain=W["ffn_norm_g"][0:1])
    pf0, act0 = _ffn_in(h1, W["wt_ffn_in0"], "ffn_in0")
    W.update(fetch("ffn_down0", act0))
    x2, h2 = _mm_nn(act0, W["w_ffn_down0"], "mm_ffn_down0", mode="rms", resid=x1, gain=W["cd_norm_g"],
                    dep=W.get("dep_down0"))
    W.update(fetch("cd_in", h2))
    pcd = _mm_nt(h2, W["wt_cd_in"], "mm_cd_in")
    cw8 = jnp.repeat(W["conv_c_w32"], 8, axis=0)
    cat_cd, c0, c1, dd, yv = _cd_fwd(pcd, cw8, W["conv_c_b"], W["c_ln_g"], W["c_ln_b"], W["conv_d_w8"])
    x3, h3 = _mm_nn(cat_cd, W["w_cd_out"], "mm_cd_out", mode="rms", resid=x2, gain=W["ffn_norm_g"][1:2])
    pf1, act1 = _ffn_in(h3, W["wt_ffn_in1"], "ffn_in1")
    dy, dyb, loss_cols = _mm_nn(act1, W["w_ffn_down1"], "mm_ffn_down1", mode="loss", resid=x3, tgt=tgt)

    def ffn_bwd(xin, h, pf, act, dres, dresb, layer):
        G[f"w_ffn_down{layer}"] = _mm_tn(act, dresb, f"mm_g_ffn_down{layer}")
        dep = on_grad(f"w_ffn_down{layer}", G[f"w_ffn_down{layer}"])
        dpf = _ffn_dact(dresb, W[f"w_ffn_down{layer}"], pf, f"ffn_dact{layer}", dep=dep)
        G[f"wt_ffn_in{layer}"] = _mm_tn(dpf, h, f"mm_g_ffn_in{layer}")
        dep = on_grad(f"wt_ffn_in{layer}", G[f"wt_ffn_in{layer}"])
        dx, dxb, G[f"ffn_norm_g{layer}"] = _mm_dh_rms_bwd(
            dpf, W[f"wt_ffn_in{layer}"], xin, W["ffn_norm_g"][layer:layer + 1], dres, f"mm_d_h_ffn{layer}", dep=dep)
        return dx, dxb

    dx3, dx3b = ffn_bwd(x3, h3, pf1, act1, dy, dyb, 1)

    G["w_cd_out"] = _mm_tn(cat_cd, dx3b, "mm_g_cd_out")
    dep = on_grad("w_cd_out", G["w_cd_out"])
    dcat_cd = _mm_nt(dx3b, W["w_cd_out"], "mm_d_cat_cd", dep=dep)
    dc1, dy3, dgb, G["c_ln_g"], G["c_ln_b"], G["conv_c_b"] = _cd_bwd_pw(dcat_cd, c1, pcd, yv, W["c_ln_g"], W["c_ln_b"])
    dpcd, G["conv_c_w32"], G["conv_d_w8"] = _cd_bwd_conv(pcd, dc1, dy3, c0, dd, dgb, cw8, W["conv_d_w8"])
    G["wt_cd_in"] = _mm_tn(dpcd, h2, "mm_g_cd_in")
    dep = on_grad("wt_cd_in", G["wt_cd_in"])
    dx2, dx2b, G["cd_norm_g"] = _mm_dh_rms_bwd(dpcd, W["wt_cd_in"], x2, W["cd_norm_g"], dx3, "mm_d_h_cd", dep=dep)

    dx1, dx1b = ffn_bwd(x1, h1, pf0, act0, dx2, dx2b, 0)

    G["w_ab_out"] = _mm_tn(cat_ab, dx1b, "mm_g_ab_out")
    dep = on_grad("w_ab_out", G["w_ab_out"])
    dcat_ab = _mm_nt(dx1b, W["w_ab_out"], "mm_d_cat_ab", dep=dep)
    dbp, e = _b_pre_bwd(dcat_ab, cat_ab)
    dqkv = []
    for g, rate in enumerate(DIL_RATES):
        dqkv += _attn_bwd(qkv[3 * g], qkv[3 * g + 1], qkv[3 * g + 2], dbp, e, lse, rate, f"attn_bwd_{g}")
    dpab, G["sgu_w"], dbias_part, G["sgu_norm_g"], G["sgu_norm_b"], dgain = _ab_in_bwd(
        pab, dcat_ab, W["sgu_norm_g"], W["sgu_norm_b"], W["sgu_w"], bias3, qg, kg, tabs, dqkv, rinvs)
    G["sgu_bias"] = jnp.sum(dbias_part, axis=-1)
    dgain = dgain[0:6, 0:HEAD] + dgain[0:6, HEAD:PAIR]
    G["q_norm_g"] = dgain[0::2]
    G["k_norm_g"] = dgain[1::2]
    G["wt_ab_in"] = _mm_tn(dpab, h0, "mm_g_ab_in")
    dep = on_grad("wt_ab_in", G["wt_ab_in"])
    grad_x, _, G["ab_norm_g"] = _mm_dh_rms_bwd(dpab, W["wt_ab_in"], x, W["ab_norm_g"], dx1, "mm_d_h_ab", dep=dep)
    return loss_cols, grad_x, G


def _my_place():
    return lax.axis_index("x"), lax.axis_index("y"), lax.axis_index("c")


def _dev_index(px, py, pc):
    return 4 * px + 2 * py + pc


def _flip(place, k):
    x, y, c = place
    return (1 - x if k & 4 else x, 1 - y if k & 2 else y, 1 - c if k & 1 else c)


def _landing(shape, dtype, own):
    buf = lax.empty(shape, dtype)
    for lead, part in own:
        buf = lax.dynamic_update_slice(buf, part.reshape((1,) * len(lead) + part.shape),
                                       tuple(lead) + (0,) * part.ndim)
    return buf


HBM_ONLY = pl.BlockSpec(memory_space=pltpu.HBM)
SEM_SPEC = pl.BlockSpec(memory_space=pltpu.SEMAPHORE)
IN_FLIGHT = pltpu.CompilerParams(has_side_effects=pltpu.SideEffectType.DATAFLOW_SIDE_EFFECTING)


def _in_hbm(a):
    return pltpu.with_memory_space_constraint(a, pltpu.HBM)


def _exchange_start(name, srcs, lands, items, dep=None):
    ns, nl, ni = len(srcs), len(lands), len(items)

    def body(*refs):
        S, L = refs[0:ns], refs[ns:ns + nl]
        first_out = ns + nl + (0 if dep is None else 1)
        send_sems, recv_sems, token = refs[first_out], refs[first_out + 1], refs[-1]
        me = _my_place()
        mi = _dev_index(*me)
        for i, (src, dst) in enumerate(items):
            for k in range(1, NDEV):
                peer = _flip(me, k)
                pltpu.make_async_remote_copy(
                    src_ref=src(S, _dev_index(*peer)), dst_ref=dst(L, mi), send_sem=send_sems.at[7 * i + k - 1],
                    recv_sem=recv_sems.at[7 * i + k - 1], device_id=peer, device_id_type=MESH).start()
        token[...] = jnp.zeros_like(token)

    thru = [pltpu.HBM(a.shape, a.dtype) for a in list(srcs) + list(lands)]
    args = [_in_hbm(a) for a in srcs] + [_in_hbm(a) for a in lands]
    in_specs = [HBM_ONLY] * (ns + nl)
    if dep is not None:
        args.append(dep)
        in_specs.append(HBM_SPEC)
    outs = pl.pallas_call(
        body, name=name, in_specs=in_specs,
        out_shape=(pltpu.SemaphoreType.DMA((7 * ni,)), pltpu.SemaphoreType.DMA((7 * ni,)), *thru, _sds((8, 128), F32)),
        out_specs=(SEM_SPEC, SEM_SPEC, *[HBM_ONLY] * (ns + nl), pl.BlockSpec(memory_space=pltpu.VMEM)),
        input_output_aliases={j: 2 + j for j in range(ns + nl)}, compiler_params=IN_FLIGHT)(*args)
    return dict(send=outs[0], recv=outs[1], srcs=list(outs[2:2 + ns]), lands=list(outs[2 + ns:2 + ns + nl]),
                token=outs[-1], items=items)


def _exchange_wait(name, states, after):
    after = list(after) if isinstance(after, (list, tuple)) else [after]
    counts = [(len(st["srcs"]), len(st["lands"]), len(st["items"])) for st in states]
    n_arrays = sum(c[0] + c[1] for c in counts)

    def body(*refs):
        me = _my_place()
        mi = _dev_index(*me)
        pos = 0
        sem_pos = n_arrays
        for st, (ns, nl, ni) in zip(states, counts):
            S, L = refs[pos:pos + ns], refs[pos + ns:pos + ns + nl]
            send_sems, recv_sems = refs[sem_pos], refs[sem_pos + 1]
            pos += ns + nl
            sem_pos += 2
            for i, (src, dst) in enumerate(st["items"]):
                for k in range(1, NDEV):
                    cp = pltpu.make_async_remote_copy(
                        src_ref=src(S, mi), dst_ref=dst(L, mi), send_sem=send_sems.at[7 * i + k - 1],
                        recv_sem=recv_sems.at[7 * i + k - 1], device_id=me, device_id_type=MESH)
                    cp.wait_send()
                    cp.wait_recv()

    arrays, sems = [], []
    for st in states:
        arrays += st["srcs"] + st["lands"]
        sems += [st["send"], st["recv"]]
    outs = pl.pallas_call(
        body, name=name, in_specs=[HBM_ONLY] * n_arrays + [SEM_SPEC] * len(sems) + [HBM_SPEC] * len(after),
        out_shape=tuple(pltpu.HBM(a.shape, a.dtype) for a in arrays), out_specs=tuple([HBM_ONLY] * n_arrays),
        input_output_aliases={j: j for j in range(n_arrays)}, compiler_params=IN_FLIGHT)(*arrays, *sems, *after)
    lands, pos = [], 0
    for ns, nl, _ in counts:
        lands.append(list(outs[pos + ns:pos + ns + nl]))
        pos += ns + nl
    return lands


def _place_and_neighbours():
    x, y, c = _my_place()
    return (x, y, c), (x, y, 1 - c), [(1 - x, y), (x, 1 - y), (1 - x, 1 - y)]


def _gather_start(name, srcs, lands, items, dep=None):
    ns, nl, ni = len(srcs), len(lands), len(items)

    def body(*refs):
        S, L = refs[0:ns], refs[ns:ns + nl]
        first_out = ns + nl + (0 if dep is None else 1)
        send_sems, recv_sems, token = refs[first_out], refs[first_out + 1], refs[-1]
        me, sib, chips = _place_and_neighbours()
        mi = _dev_index(*me)
        for i, (src, dst) in enumerate(items):
            for k, to in enumerate([sib] + [(*chip, me[2]) for chip in chips]):
                pltpu.make_async_remote_copy(
                    src_ref=src(S), dst_ref=dst(L, mi), send_sem=send_sems.at[4 * i + k],
                    recv_sem=recv_sems.at[4 * i + k], device_id=to, device_id_type=MESH).start()
        token[...] = jnp.zeros_like(token)

    thru = [pltpu.HBM(a.shape, a.dtype) for a in list(srcs) + list(lands)]
    args = [_in_hbm(a) for a in srcs] + [_in_hbm(a) for a in lands]
    in_specs = [HBM_ONLY] * (ns + nl)
    if dep is not None:
        args.append(dep)
        in_specs.append(HBM_SPEC)
    outs = pl.pallas_call(
        body, name=name, in_specs=in_specs,
        out_shape=(pltpu.SemaphoreType.DMA((4 * ni,)), pltpu.SemaphoreType.DMA((4 * ni,)), *thru, _sds((8, 128), F32)),
        out_specs=(SEM_SPEC, SEM_SPEC, *[HBM_ONLY] * (ns + nl), pl.BlockSpec(memory_space=pltpu.VMEM)),
        input_output_aliases={j: 2 + j for j in range(ns + nl)}, compiler_params=IN_FLIGHT)(*args)
    return dict(send=outs[0], recv=outs[1], srcs=list(outs[2:2 + ns]), lands=list(outs[2 + ns:2 + ns + nl]),
                token=outs[-1], items=items)


def _gather_forward(name, st, after):
    nl, ni = len(st["lands"]), len(st["items"])

    def body(*refs):
        L, recv_sems = refs[0:nl], refs[nl]
        fwd_send, fwd_recv, token = refs[2 * nl + 2], refs[2 * nl + 3], refs[-1]
        me, sib, chips = _place_and_neighbours()
        for i, (_, dst) in enumerate(st["items"]):
            for j, chip in enumerate(chips):
                blk = dst(L, _dev_index(*chip, me[2]))
                pltpu.make_async_remote_copy(
                    src_ref=blk, dst_ref=blk, send_sem=fwd_send.at[3 * i + j], recv_sem=recv_sems.at[4 * i + 1 + j],
                    device_id=me, device_id_type=MESH).wait_recv()
                pltpu.make_async_remote_copy(
                    src_ref=blk, dst_ref=blk, send_sem=fwd_send.at[3 * i + j], recv_sem=fwd_recv.at[3 * i + j],
                    device_id=sib, device_id_type=MESH).start()
        token[...] = jnp.zeros_like(token)

    outs = pl.pallas_call(
        body, name=name, in_specs=[HBM_ONLY] * nl + [SEM_SPEC, HBM_SPEC],
        out_shape=(*[pltpu.HBM(a.shape, a.dtype) for a in st["lands"]], pltpu.SemaphoreType.DMA((3 * ni,)),
                   pltpu.SemaphoreType.DMA((3 * ni,)), _sds((8, 128), F32)),
        out_specs=(*[HBM_ONLY] * nl, SEM_SPEC, SEM_SPEC, pl.BlockSpec(memory_space=pltpu.VMEM)),
        input_output_aliases={j: j for j in range(nl)}, compiler_params=IN_FLIGHT)(*st["lands"], st["recv"], after)
    return dict(st, lands=list(outs[0:nl]), fwd_send=outs[nl], fwd_recv=outs[nl + 1], token=outs[-1])


def _gather_wait(name, st, after):
    ns, nl, ni = len(st["srcs"]), len(st["lands"]), len(st["items"])

    def body(*refs):
        S, L = refs[0:ns], refs[ns:ns + nl]
        send_sems, recv_sems, fwd_send, fwd_recv = refs[ns + nl:ns + nl + 4]
        me, sib, chips = _place_and_neighbours()
        mi = _dev_index(*me)
        for i, (src, dst) in enumerate(st["items"]):
            mine = dst(L, mi)
            for k in range(4):
                pltpu.make_async_remote_copy(
                    src_ref=src(S), dst_ref=mine, send_sem=send_sems.at[4 * i + k], recv_sem=recv_sems.at[4 * i + k],
                    device_id=me, device_id_type=MESH).wait_send()
            pltpu.make_async_remote_copy(
                src_ref=src(S), dst_ref=mine, send_sem=send_sems.at[4 * i], recv_sem=recv_sems.at[4 * i],
                device_id=me, device_id_type=MESH).wait_recv()
            for j in range(3):
                cp = pltpu.make_async_remote_copy(
                    src_ref=mine, dst_ref=mine, send_sem=fwd_send.at[3 * i + j], recv_sem=fwd_recv.at[3 * i + j],
                    device_id=me, device_id_type=MESH)
                cp.wait_send()
                cp.wait_recv()

    arrays = st["srcs"] + st["lands"]
    outs = pl.pallas_call(
        body, name=name, in_specs=[HBM_ONLY] * (ns + nl) + [SEM_SPEC] * 4 + [HBM_SPEC],
        out_shape=tuple(pltpu.HBM(a.shape, a.dtype) for a in arrays), out_specs=tuple([HBM_ONLY] * (ns + nl)),
        input_output_aliases={j: j for j in range(ns + nl)},
        compiler_params=IN_FLIGHT)(*arrays, st["send"], st["recv"], st["fwd_send"], st["fwd_recv"], after)
    return list(outs[ns:ns + nl])


def _sum_slots(land):
    def body(l_ref, o_ref):
        acc = l_ref[0]
        for d in range(1, NDEV):
            acc = acc + l_ref[d]
        o_ref[...] = acc

    vm = pl.BlockSpec(memory_space=pltpu.VMEM)
    return pl.pallas_call(body, name="sum_small", out_shape=_sds(land.shape[1:], F32), in_specs=[vm], out_specs=vm)(land)


def _adam_math(w, g, m, v):
    m2 = ADAM_B1 * m + (1.0 - ADAM_B1) * g
    v2 = ADAM_B2 * v + (1.0 - ADAM_B2) * (g * g)
    delta = -ADAM_LR * ((m2 * ADAM_C1) / (jnp.sqrt(v2 * ADAM_C2) + ADAM_EPS) + ADAM_WD * w)
    return delta, m2, v2


def _adam_layer(land, sel, w, m, v, layer, name, prev=None, tc=512):
    R = land.shape[2]

    def body(l_ref, w_ref, m_ref, v_ref, *rest):
        g_out, d_out, m_out, v_out = rest[-4:]
        g = l_ref[0].astype(F32)
        for d in range(1, NDEV):
            g = g + l_ref[d].astype(F32)
        delta, m2, v2 = _adam_math(w_ref[...], g, m_ref[...], v_ref[...])
        g_out[...] = g
        d_out[...] = delta
        m_out[...] = m2
        v_out[...] = v2

    wspec = pl.BlockSpec((None, R, tc), lambda i: (layer, 0, i))
    in_specs = [pl.BlockSpec((None, NDEV, R, tc), lambda i: (sel, 0, 0, i)), wspec, wspec, wspec]
    args = [land, w, m, v]
    aliases = {}
    if prev is not None:
        in_specs += [HBM_SPEC] * 4
        args += list(prev)
        aliases = {4 + j: j for j in range(4)}
    return pl.pallas_call(
        body, name=name, grid=(D // tc,), in_specs=in_specs, out_specs=[wspec] * 4,
        out_shape=[_sds(w.shape, F32)] * 4, input_output_aliases=aliases, compiler_params=_cparams(1))(*args)


def _adam_stacked(lands, sel, w, m, v, name):
    res = None
    for layer, land in enumerate(lands):
        res = _adam_layer(land, sel, w, m, v, layer, f"{name}{layer}", prev=res)
    return res


def _adam_small(ws, gs, ms, vs):
    n = len(ws)

    def body(*refs):
        w_r, g_r, m_r, v_r = refs[0:n], refs[n:2 * n], refs[2 * n:3 * n], refs[3 * n:4 * n]
        d_o, m_o, v_o = refs[4 * n:5 * n], refs[5 * n:6 * n], refs[6 * n:7 * n]
        for i in range(n):
            delta, m2, v2 = _adam_math(w_r[i][...], g_r[i][...], m_r[i][...], v_r[i][...])
            d_o[i][...] = delta
            m_o[i][...] = m2
            v_o[i][...] = v2

    vm = pl.BlockSpec(memory_space=pltpu.VMEM)
    shapes = [_sds(w.shape, F32) for w in ws]
    outs = pl.pallas_call(body, name="adam_small", in_specs=[vm] * (4 * n), out_specs=[vm] * (3 * n),
                          out_shape=shapes * 3)(*ws, *gs, *ms, *vs)
    return outs[0:n], outs[n:2 * n], outs[2 * n:3 * n]


WEIGHT_NAMES = ("ab_norm_g", "ab_w_in", "sgu_norm_g", "sgu_norm_b", "sgu_w", "sgu_bias", "q_norm_g", "k_norm_g",
                "ab_w_out", "cd_norm_g", "cd_w_in", "conv_c_w", "conv_c_b", "c_ln_g", "c_ln_b", "conv_d_w",
                "cd_w_out", "ffn_norm_g", "ffn_w_gate", "ffn_w_up", "ffn_w_down")
SMALL_2D = (("ab_norm_g", (1, 1024)), ("sgu_norm_g", (1, 512)), ("sgu_norm_b", (1, 512)), ("sgu_w", (512, 128)),
            ("sgu_bias", (4, 128)), ("q_norm_g", (3, 64)), ("k_norm_g", (3, 64)), ("cd_norm_g", (1, 128)),
            ("conv_c_w", (31, 64)), ("conv_c_b", (1, 64)), ("c_ln_g", (1, 64)), ("c_ln_b", (1, 64)),
            ("conv_d_w", (3, 64)), ("ffn_norm_g", (2, 1024)))
SHARD_C = 64


def _pack_rows(parts, rows):
    flat = jnp.concatenate([p.reshape(-1) for p in parts])
    return jnp.pad(flat, (0, rows * 128 - flat.shape[0])).reshape(rows, 128)


def kernel(x, ab_norm_g, ab_w_in, sgu_norm_g, sgu_norm_b, sgu_w, sgu_bias, q_norm_g, k_norm_g, ab_w_out, cd_norm_g, cd_w_in, conv_c_w, conv_c_b, c_ln_g, c_ln_b, conv_d_w, cd_w_out, ffn_norm_g, ffn_w_gate, ffn_w_up, ffn_w_down, loss_target, m_ab_norm_g, m_ab_w_in, m_sgu_norm_g, m_sgu_norm_b, m_sgu_w, m_sgu_bias, m_q_norm_g, m_k_norm_g, m_ab_w_out, m_cd_norm_g, m_cd_w_in, m_conv_c_w, m_conv_c_b, m_c_ln_g, m_c_ln_b, m_conv_d_w, m_cd_w_out, m_ffn_norm_g, m_ffn_w_gate, m_ffn_w_up, m_ffn_w_down, v_ab_norm_g, v_ab_w_in, v_sgu_norm_g, v_sgu_norm_b, v_sgu_w, v_sgu_bias, v_q_norm_g, v_k_norm_g, v_ab_w_out, v_cd_norm_g, v_cd_w_in, v_conv_c_w, v_conv_c_b, v_c_ln_g, v_c_ln_b, v_conv_d_w, v_cd_w_out, v_ffn_norm_g, v_ffn_w_gate, v_ffn_w_up, v_ffn_w_down):
    w = dict(zip(WEIGHT_NAMES, (ab_norm_g, ab_w_in, sgu_norm_g, sgu_norm_b, sgu_w, sgu_bias, q_norm_g, k_norm_g, ab_w_out, cd_norm_g, cd_w_in, conv_c_w, conv_c_b, c_ln_g, c_ln_b, conv_d_w, cd_w_out, ffn_norm_g, ffn_w_gate, ffn_w_up, ffn_w_down)))
    m = dict(zip(WEIGHT_NAMES, (m_ab_norm_g, m_ab_w_in, m_sgu_norm_g, m_sgu_norm_b, m_sgu_w, m_sgu_bias, m_q_norm_g, m_k_norm_g, m_ab_w_out, m_cd_norm_g, m_cd_w_in, m_conv_c_w, m_conv_c_b, m_c_ln_g, m_c_ln_b, m_conv_d_w, m_cd_w_out, m_ffn_norm_g, m_ffn_w_gate, m_ffn_w_up, m_ffn_w_down)))
    v = dict(zip(WEIGHT_NAMES, (v_ab_norm_g, v_ab_w_in, v_sgu_norm_g, v_sgu_norm_b, v_sgu_w, v_sgu_bias, v_q_norm_g, v_k_norm_g, v_ab_w_out, v_cd_norm_g, v_cd_w_in, v_conv_c_w, v_conv_c_b, v_c_ln_g, v_c_ln_b, v_conv_d_w, v_cd_w_out, v_ffn_norm_g, v_ffn_w_gate, v_ffn_w_up, v_ffn_w_down)))
    me = _dev_index(*_my_place())

    small_local = _pack_rows([w["cd_norm_g"], w["conv_c_w"], w["conv_c_b"], w["c_ln_g"], w["c_ln_b"], w["conv_d_w"]], 24)
    r_ff = DFF // NDEV
    one = lambda a: (lambda S, j: S[a])
    slot = lambda b: (lambda L, s: L[b].at[s])
    slot2 = lambda b, part: (lambda L, s: L[b].at[part, s])
    shard = lambda a: (lambda S: S[a])

    def layer_shards(layer):
        return (w["ffn_w_gate"][layer].T.astype(BF16), w["ffn_w_up"][layer].T.astype(BF16),
                w["ffn_w_down"][layer].astype(BF16))

    def gathered(own):
        return _landing((NDEV,) + own.shape, BF16, [((me,), own)])

    def gathered2(a, b):
        return _landing((2, NDEV) + a.shape, BF16, [((0, me), a), ((1, me), b)])

    ab_in_s = w["ab_w_in"][0].T.astype(BF16)
    gathers = {0: _gather_start(
        "gather0_start", [ab_in_s, small_local],
        [gathered(ab_in_s), _landing((NDEV,) + small_local.shape, F32, [((me,), small_local)])],
        [(shard(0), slot(0)), (shard(1), slot(1))])}

    def chan(flat, lo, taps):
        return flat[:, lo:lo + taps * SHARD_C].reshape(NDEV, taps, SHARD_C).transpose(1, 0, 2).reshape(taps, 512)

    def fetch(stage, after):
        if stage == "ab_in":
            gathers[0] = _gather_forward("gather0_forward", gathers[0], after)
            l_ab_in, l_small = _gather_wait("gather0_wait", gathers[0], gathers[0]["token"])
            ab_out_s = w["ab_w_out"][0].astype(BF16)
            gate0, up0, down0 = layer_shards(0)
            gathers[1] = _gather_start(
                "gather1_start", [ab_out_s, gate0, up0, down0],
                [gathered(ab_out_s), gathered2(gate0, up0), gathered(down0)],
                [(shard(0), slot(0)), (shard(1), slot2(1, 0)), (shard(2), slot2(1, 1)), (shard(3), slot(2))],
                dep=l_small)
            flat = l_small.reshape(NDEV, 24 * 128)
            return {
                "wt_ab_in": l_ab_in.reshape(AB_IN, D), "dep0": gathers[1]["token"],
                "cd_norm_g": flat[:, 0:128].reshape(1, D),
                "conv_c_w32": jnp.pad(chan(flat, 128, CONV_C_TAPS), ((0, 1), (0, 0))),
                "conv_c_b": chan(flat, 2112, 1), "c_ln_g": chan(flat, 2176, 1), "c_ln_b": chan(flat, 2240, 1),
                "conv_d_w8": jnp.pad(chan(flat, 2304, CONV_D_TAPS), ((0, 8 - CONV_D_TAPS), (0, 0))),
            }
        if stage == "attn0":
            cd_in_s, cd_out_s = w["cd_w_in"][0].T.astype(BF16), w["cd_w_out"][0].astype(BF16)
            gate1, up1, down1 = layer_shards(1)
            gathers[2] = _gather_start(
                "gather2_start", [cd_in_s, cd_out_s, gate1, up1, down1],
                [gathered(cd_in_s), gathered(cd_out_s), gathered2(gate1, up1), gathered(down1)],
                [(shard(0), slot(0)), (shard(1), slot(1)), (shard(2), slot2(2, 0)), (shard(3), slot2(2, 1)),
                 (shard(4), slot(3))], dep=after)
            return {"dep_attn1": gathers[2]["token"]}
        if stage == "attn1":
            gathers[1] = _gather_forward("gather1_forward", gathers[1], after)
            return {"dep_attn2": gathers[1]["token"]}
        if stage == "ab_out":
            l_out, l_ffn, l_down = _gather_wait("gather1_wait", gathers[1], after)
            return {"w_ab_out": l_out.reshape(D, D), "wt_ffn_in0": l_ffn.reshape(2 * DFF, D),
                    "w_ffn_down0": l_down.reshape(DFF, D)}
        if stage == "ffn_down0":
            gathers[2] = _gather_forward("gather2_forward", gathers[2], after)
            return {"dep_down0": gathers[2]["token"]}
        if stage == "cd_in":
            l_in, l_out, l_ffn, l_down = _gather_wait("gather2_wait", gathers[2], after)
            return {"wt_cd_in": l_in.reshape(CD_IN, D), "w_cd_out": l_out.reshape(D, D),
                    "wt_ffn_in1": l_ffn.reshape(2 * DFF, D), "w_ffn_down1": l_down.reshape(DFF, D)}
        return {}

    scatters = {}
    rides_with = {"w_ffn_down1": "wt_ffn_in1", "w_cd_out": "wt_cd_in", "w_ffn_down0": "wt_ffn_in0"}
    held = {}

    def on_grad(key, arr):
        if key in rides_with:
            held[rides_with[key]] = (key, arr)
            return None
        group = ([held.pop(key)] if key in held else []) + [(key, arr)]
        srcs, lands, items = [], [], []
        for n, (k, a) in enumerate(group):
            if k.startswith("wt_ffn_in"):
                src = a.reshape(2, NDEV, r_ff, D)
                own = lax.dynamic_slice_in_dim(src, me, 1, axis=1)
                lands.append(lax.dynamic_update_slice(lax.empty(src.shape, BF16), own, (0, me, 0, 0)))
                items += [((lambda S, j, n=n: S[n].at[0, j]), slot2(n, 0)), ((lambda S, j, n=n: S[n].at[1, j]), slot2(n, 1))]
            else:
                rows = a.shape[0] // NDEV
                src = a.reshape(NDEV, rows, D)
                own = lax.dynamic_index_in_dim(src, me, 0, keepdims=False)
                lands.append(_landing((1, NDEV, rows, D), BF16, [((0, me), own)]))
                items.append(((lambda S, j, n=n: S[n].at[j]), slot2(n, 0)))
            srcs.append(src)
        st = _exchange_start(f"scatter_{key}_start", srcs, lands, items)
        scatters[key] = (st, [k for k, _ in group])
        return st["token"]

    W = {
        "ab_norm_g": w["ab_norm_g"], "sgu_norm_g": w["sgu_norm_g"], "sgu_norm_b": w["sgu_norm_b"],
        "sgu_w": w["sgu_w"][0], "sgu_bias": w["sgu_bias"][0], "q_norm_g": w["q_norm_g"][0],
        "k_norm_g": w["k_norm_g"][0], "ffn_norm_g": w["ffn_norm_g"],
    }

    loss_cols, grad_x, G = _local_step(x[0], loss_target[0], W, fetch, on_grad)

    small_parts = [G["ab_norm_g"], G["sgu_norm_g"], G["sgu_norm_b"], G["sgu_w"], G["sgu_bias"], G["q_norm_g"],
                   G["k_norm_g"], G["cd_norm_g"], G["conv_c_w32"][:CONV_C_TAPS], G["conv_c_b"], G["c_ln_g"],
                   G["c_ln_b"], G["conv_d_w8"][:CONV_D_TAPS], G["ffn_norm_g0"], G["ffn_norm_g1"], loss_cols]
    sizes = [p.size for p in small_parts]
    small_rows = 720
    packed = _pack_rows(small_parts, small_rows)
    small = _exchange_start("small_start", [packed], [_landing((NDEV, small_rows, 128), F32, [((me,), packed)])],
                            [(one(0), slot(0))])
    landed = {}

    def wait_scatters(name, group_keys, after):
        res = _exchange_wait(name, [scatters[gk][0] for gk in group_keys], after)
        for gk, lands in zip(group_keys, res):
            landed.update(zip(scatters[gk][1], lands))

    wait_scatters("scatter_wait_early", ["wt_ffn_in1", "wt_cd_in", "wt_ffn_in0", "w_ab_out"], small["token"])

    grads, deltas, new_m, new_v = {}, {}, {}, {}
    done = []

    def put(name, res):
        grads[name], deltas[name], new_m[name], new_v[name] = res

    def adam(name, lands, sel, transposed):
        flip = (lambda a: jnp.swapaxes(a, 1, 2)) if transposed else (lambda a: a)
        res = _adam_stacked(lands, sel, flip(w[name]), flip(m[name]), flip(v[name]), f"adam_{name}")
        done.append(res[1])
        put(name, [flip(r) for r in res])

    ffn_in_lands = [landed["wt_ffn_in0"], landed["wt_ffn_in1"]]
    adam("cd_w_in", [landed["wt_cd_in"]], 0, True)
    adam("ffn_w_gate", ffn_in_lands, 0, True)
    adam("ffn_w_up", ffn_in_lands, 1, True)
    adam("cd_w_out", [landed["w_cd_out"]], 0, False)
    adam("ab_w_out", [landed["w_ab_out"]], 0, False)
    adam("ffn_w_down", [landed["w_ffn_down0"], landed["w_ffn_down1"]], 0, False)

    small_land = _exchange_wait("small_wait", [small], list(done))[0][0]
    red = _sum_slots(small_land).reshape(-1)
    offs = [0]
    for s in sizes:
        offs.append(offs[-1] + s)
    seg = [red[offs[i]:offs[i + 1]] for i in range(len(sizes))]
    loss = jnp.sum(seg[15])

    def own_channels(full, taps):
        return lax.dynamic_slice_in_dim(full.reshape(taps, 512), me * SHARD_C, SHARD_C, axis=1)

    g_small = {
        "ab_norm_g": seg[0].reshape(1, 1024), "sgu_norm_g": seg[1].reshape(1, 512), "sgu_norm_b": seg[2].reshape(1, 512),
        "sgu_w": seg[3].reshape(512, 128), "sgu_bias": seg[4].reshape(4, 128), "q_norm_g": seg[5].reshape(3, 64),
        "k_norm_g": seg[6].reshape(3, 64),
        "cd_norm_g": lax.dynamic_slice_in_dim(seg[7].reshape(1, D), me * (D // NDEV), D // NDEV, axis=1),
        "conv_c_w": own_channels(seg[8], CONV_C_TAPS), "conv_c_b": own_channels(seg[9], 1),
        "c_ln_g": own_channels(seg[10], 1), "c_ln_b": own_channels(seg[11], 1),
        "conv_d_w": own_channels(seg[12], CONV_D_TAPS),
        "ffn_norm_g": jnp.concatenate([seg[13].reshape(1, D), seg[14].reshape(1, D)], axis=0),
    }

    names2d = [n for n, _ in SMALL_2D]
    d_s, m_s, v_s = _adam_small([w[n].reshape(s) for n, s in SMALL_2D], [g_small[n] for n in names2d],
                                [m[n].reshape(s) for n, s in SMALL_2D], [v[n].reshape(s) for n, s in SMALL_2D])
    for i, n in enumerate(names2d):
        shape = w[n].shape
        grads[n], deltas[n] = g_small[n].reshape(shape), d_s[i].reshape(shape)
        new_m[n], new_v[n] = m_s[i].reshape(shape), v_s[i].reshape(shape)

    wait_scatters("scatter_wait_last", ["wt_ab_in"], d_s[0])
    adam("ab_w_in", [landed["wt_ab_in"]], 0, True)

    return (loss, grad_x[None], *[grads[n] for n in WEIGHT_NAMES], *[deltas[n] for n in WEIGHT_NAMES],
            *[new_m[n] for n in WEIGHT_NAMES], *[new_v[n] for n in WEIGHT_NAMES])
```

```python
import functools

import jax
import jax.numpy as jnp
import numpy as np
from jax import lax
from jax.experimental import pallas as pl
from jax.experimental.pallas import tpu as pltpu

F32 = jnp.float32
BF16 = jnp.bfloat16

T = 4096
D = 1024
NDEV = 8
EPS = 1e-6
NEG_INF = -1e30
DFF = 2816
AB_IN = 5632
CD_IN = 2560
HEAD = 64
PAIR = 128
NPAIR = 4
NBACK = 128
DIL_RATES = (1, 4, 16)
ROPE_HALF = 8
ROPE_THETA = 500000.0
CONV_C_TAPS = 31
CONV_D_TAPS = 3
HALO = 32
ATTN_BWD_UNROLL = 4

ADAM_LR = 0.001
ADAM_B1 = 0.9
ADAM_B2 = 0.999
ADAM_EPS = 1e-08
ADAM_WD = 0.01
ADAM_STEP = 10
ADAM_C1 = 1.0 / (1.0 - ADAM_B1 ** ADAM_STEP)
ADAM_C2 = 1.0 / (1.0 - ADAM_B2 ** ADAM_STEP)

VMEM_LIMIT_MB = 48
MESH = pl.DeviceIdType.MESH
HBM_SPEC = pl.BlockSpec(memory_space=pl.ANY)


def _cparams(ngrid, vmem_mb=VMEM_LIMIT_MB):
    return pltpu.CompilerParams(dimension_semantics=("arbitrary",) * ngrid,
                                vmem_limit_bytes=vmem_mb * 1024 * 1024)


def _pick(n, options):
    for o in options:
        if n % o == 0:
            return o
    raise ValueError(f"no tile for {n} in {options}")


def _sds(shape, dtype):
    return jax.ShapeDtypeStruct(shape, dtype)


def _sigmoid(x):
    return 1.0 / (1.0 + jnp.exp(-x))


def _sigmoid_bf16(x):
    return 0.5 * jnp.tanh(0.5 * x) + 0.5


def _gelu(z):
    return 0.5 * z * (1.0 + lax.erf(z * 0.7071067811865476))


def _gelu_grad(z):
    return 0.5 * (1.0 + lax.erf(z * 0.7071067811865476)) + z * jnp.exp(-0.5 * z * z) * 0.3989422804014327


def _mm_nt(a, wt, name, out_dtype=BF16, tm=2048, dep=None):
    M, K = a.shape
    N = wt.shape[0]
    tn = _pick(N, (512, 256))

    def body(a_ref, w_ref, *rest):
        o_ref = rest[-1]
        o_ref[...] = lax.dot_general(a_ref[...], w_ref[...], (((1,), (1,)), ((), ())),
                                     preferred_element_type=F32).astype(o_ref.dtype)

    in_specs = [pl.BlockSpec((tm, K), lambda i, j: (i, 0)), pl.BlockSpec((tn, K), lambda i, j: (j, 0))]
    args = [a, wt]
    if dep is not None:
        in_specs.append(HBM_SPEC)
        args.append(dep)
    return pl.pallas_call(
        body, name=name, grid=(M // tm, N // tn), in_specs=in_specs,
        out_specs=pl.BlockSpec((tm, tn), lambda i, j: (i, j)),
        out_shape=_sds((M, N), out_dtype), compiler_params=_cparams(2))(*args)


EPI_ROWS = 256


def _mm_nn(a, w, name, mode, resid, gain=None, tgt=None, dep=None, tm=512):
    M, K = a.shape
    N = w.shape[1]
    side = gain if mode == "rms" else tgt

    def body(a_ref, w_ref, resid_ref, side_ref, *rest):
        outs, acc = rest[-3 if mode == "rms" else -4:-1], rest[-1]
        i = pl.program_id(0)
        acc[...] = jnp.dot(a_ref[...], w_ref[...], preferred_element_type=F32)

        if mode == "loss":
            @pl.when(i == 0)
            def _():
                outs[2][...] = jnp.zeros_like(outs[2])

        for r0 in range(0, tm, EPI_ROWS):
            rows = slice(r0, r0 + EPI_ROWS)
            v = acc[rows, :] + resid_ref[rows, :]
            if mode == "rms":
                outs[0][rows, :] = v
                r = lax.rsqrt(jnp.mean(v * v, axis=-1, keepdims=True) + EPS)
                outs[1][rows, :] = (v * r * side_ref[...]).astype(BF16)
            else:
                d = v - side_ref[rows, :]
                outs[2][...] += jnp.sum(d * d, axis=0, keepdims=True) * (0.5 / N)
                dy = d * (1.0 / N)
                outs[0][rows, :] = dy
                outs[1][rows, :] = dy.astype(BF16)

    row = pl.BlockSpec((tm, N), lambda i: (i, 0))
    vec = pl.BlockSpec((1, N), lambda i: (0, 0))
    in_specs = [pl.BlockSpec((tm, K), lambda i: (i, 0)),
                pl.BlockSpec((K, N), lambda i: (0, 0), pipeline_mode=pl.Buffered(1)), row,
                vec if mode == "rms" else row]
    args = [a, w, resid, side]
    if dep is not None:
        in_specs.append(HBM_SPEC)
        args.append(dep)
    if mode == "rms":
        out_specs, out_shape = [row, row], [_sds((M, N), F32), _sds((M, N), BF16)]
    else:
        out_specs, out_shape = [row, row, vec], [_sds((M, N), F32), _sds((M, N), BF16), _sds((1, N), F32)]
    return pl.pallas_call(
        body, name=name, grid=(M // tm,), in_specs=in_specs, out_specs=out_specs, out_shape=out_shape,
        scratch_shapes=[pltpu.VMEM((tm, N), F32)], compiler_params=_cparams(1))(*args)


def _mm_dh_rms_bwd(a, w, x, gain, dres, name, dep=None, tm=512):
    parts = a.shape[0] if a.ndim == 3 else 1
    M, Kp = a.shape[-2], a.shape[-1]
    N = w.shape[1]
    nblk = M // tm
    assert nblk % 2 == 0

    def body(a_ref, w_ref, x_ref, g_ref, dres_ref, *rest):
        dx_ref, dxb_ref, dg_ref, acc0, acc1 = rest[-5:]
        i = pl.program_id(0)

        def matmul(acc):
            if parts == 1:
                acc[...] = jnp.dot(a_ref[...], w_ref[...], preferred_element_type=F32)
            else:
                d = jnp.dot(a_ref[0], w_ref[0:Kp, :], preferred_element_type=F32)
                for p in range(1, parts):
                    d = d + jnp.dot(a_ref[p], w_ref[p * Kp:(p + 1) * Kp, :], preferred_element_type=F32)
                acc[...] = d

        def finish(acc):
            for r0 in range(0, tm, EPI_ROWS // 2):
                rows = slice(r0, r0 + EPI_ROWS // 2)
                v = acc[rows, :]
                xf = x_ref[rows, :]
                r = lax.rsqrt(jnp.mean(xf * xf, axis=-1, keepdims=True) + EPS)
                xhat = xf * r
                dg_ref[...] += jnp.sum(v * xhat, axis=0, keepdims=True)
                dxh = v * g_ref[...]
                tot = dres_ref[rows, :] + r * (dxh - xhat * jnp.mean(dxh * xhat, axis=-1, keepdims=True))
                dx_ref[rows, :] = tot
                dxb_ref[rows, :] = tot.astype(BF16)

        @pl.when(i == 0)
        def _():
            dg_ref[...] = jnp.zeros_like(dg_ref)
            matmul(acc0)

        @pl.when((i > 0) & (i < nblk) & (i % 2 == 1))
        def _():
            matmul(acc1)
            finish(acc0)

        @pl.when((i > 0) & (i < nblk) & (i % 2 == 0))
        def _():
            matmul(acc0)
            finish(acc1)

        @pl.when(i == nblk)
        def _():
            finish(acc1)

    last = nblk - 1
    row = pl.BlockSpec((tm, N), lambda i: (jnp.maximum(i - 1, 0), 0))
    vec = pl.BlockSpec((1, N), lambda i: (0, 0))
    if a.ndim == 3:
        a_spec = pl.BlockSpec((parts, tm, Kp), lambda i: (0, jnp.minimum(i, last), 0))
    else:
        a_spec = pl.BlockSpec((tm, Kp), lambda i: (jnp.minimum(i, last), 0))
    w_spec = pl.BlockSpec((parts * Kp, N), lambda i: (0, 0), pipeline_mode=pl.Buffered(1))
    in_specs = [a_spec, w_spec, row, vec, row]
    args = [a, w, x, gain, dres]
    if dep is not None:
        in_specs.append(HBM_SPEC)
        args.append(dep)
    return pl.pallas_call(
        body, name=name, grid=(nblk + 1,), in_specs=in_specs, out_specs=[row, row, vec],
        out_shape=[_sds((M, N), F32), _sds((M, N), BF16), _sds((1, N), F32)],
        scratch_shapes=[pltpu.VMEM((tm, N), F32), pltpu.VMEM((tm, N), F32)], compiler_params=_cparams(1, 56))(*args)


def _mm_tn(a, b, name, out_dtype=BF16, tt=1024):
    parts = a.shape[0] if a.ndim == 3 else 1
    Tt, Mp = a.shape[-2], a.shape[-1]
    N = b.shape[1]
    tn = _pick(Mp, (1408, 1280, 1024, 512))
    jper = Mp // tn
    nt = Tt // tt

    def body(a_ref, b_ref, o_ref, acc):
        t = pl.program_id(1)

        @pl.when(t == 0)
        def _():
            acc[...] = jnp.zeros_like(acc)

        acc[...] += lax.dot_general(a_ref[...], b_ref[...], (((0,), (0,)), ((), ())),
                                    preferred_element_type=F32)

        @pl.when(t == nt - 1)
        def _():
            o_ref[...] = acc[...].astype(o_ref.dtype)

    if a.ndim == 3:
        a_spec = pl.BlockSpec((None, tt, tn), lambda j, t: (j // jper, t, j % jper))
    else:
        a_spec = pl.BlockSpec((tt, tn), lambda j, t: (t, j))
    return pl.pallas_call(
        body, name=name, grid=(parts * jper, nt),
        in_specs=[a_spec, pl.BlockSpec((tt, N), lambda j, t: (t, 0))],
        out_specs=pl.BlockSpec((tn, N), lambda j, t: (j, 0)),
        out_shape=_sds((parts * Mp, N), out_dtype), scratch_shapes=[pltpu.VMEM((tn, N), F32)],
        compiler_params=_cparams(2))(a, b)


def _ffn_in(h, wt_in, name, tm=2048, tn=256):
    nj = DFF // tn

    def body(h_ref, wg_ref, wu_ref, p_ref, act_ref):
        nt = (((1,), (1,)), ((), ()))
        g = lax.dot_general(h_ref[...], wg_ref[...], nt, preferred_element_type=F32).astype(BF16)
        u = lax.dot_general(h_ref[...], wu_ref[...], nt, preferred_element_type=F32).astype(BF16)
        p_ref[0] = g
        p_ref[1] = u
        act_ref[...] = g * _sigmoid_bf16(g) * u

    return pl.pallas_call(
        body, name=name, grid=(T // tm, nj),
        in_specs=[pl.BlockSpec((tm, D), lambda i, j: (i, 0)), pl.BlockSpec((tn, D), lambda i, j: (j, 0)),
                  pl.BlockSpec((tn, D), lambda i, j: (j + nj, 0))],
        out_specs=[pl.BlockSpec((2, tm, tn), lambda i, j: (0, i, j)), pl.BlockSpec((tm, tn), lambda i, j: (i, j))],
        out_shape=[_sds((2, T, DFF), BF16), _sds((T, DFF), BF16)], compiler_params=_cparams(2))(h, wt_in, wt_in)


def _ffn_dact(dyb, w_down, p3, name, tm=2048, tn=256, dep=None):
    def body(dy_ref, w_ref, p_ref, *rest):
        o_ref = rest[-1]
        da = lax.dot_general(dy_ref[...], w_ref[...], (((1,), (1,)), ((), ())),
                             preferred_element_type=F32).astype(BF16)
        g = p_ref[0]
        u = p_ref[1]
        sg = _sigmoid_bf16(g)
        gs = g * sg
        o_ref[0] = (da * u) * (sg + gs * (1.0 - sg))
        o_ref[1] = da * gs

    pspec = pl.BlockSpec((2, tm, tn), lambda i, j: (0, i, j))
    in_specs = [pl.BlockSpec((tm, D), lambda i, j: (i, 0)), pl.BlockSpec((tn, D), lambda i, j: (j, 0)), pspec]
    args = [dyb, w_down, p3]
    if dep is not None:
        in_specs.append(HBM_SPEC)
        args.append(dep)
    return pl.pallas_call(
        body, name=name, grid=(T // tm, DFF // tn), in_specs=in_specs, out_specs=pspec,
        out_shape=_sds((2, T, DFF), BF16), compiler_params=_cparams(2))(*args)


def _rms_fwd(x, g, name, tm=512, dep=None):
    def body(x_ref, g_ref, *rest):
        h_ref = rest[-1]
        xf = x_ref[...]
        r = lax.rsqrt(jnp.mean(xf * xf, axis=-1, keepdims=True) + EPS)
        h_ref[...] = (xf * r * g_ref[...]).astype(BF16)

    in_specs = [pl.BlockSpec((tm, D), lambda i: (i, 0)), pl.BlockSpec((1, D), lambda i: (0, 0))]
    args = [x, g]
    if dep is not None:
        in_specs.append(HBM_SPEC)
        args.append(dep)
    return pl.pallas_call(
        body, name=name, grid=(T // tm,), in_specs=in_specs, out_specs=pl.BlockSpec((tm, D), lambda i: (i, 0)),
        out_shape=_sds((T, D), BF16), compiler_params=_cparams(1))(*args)


def _tril_mask():
    r = lax.broadcasted_iota(jnp.int32, (128, 128), 0)
    c = lax.broadcasted_iota(jnp.int32, (128, 128), 1)
    return r >= c


def _mix_a_fwd(pab, sgu_g, sgu_b, sgu_w, sgu_bias3, tm=512):
    def body(zu_ref, zv_ref, g_ref, b_ref, w_ref, bias_ref, o_ref):
        u = _gelu(zu_ref[...].astype(F32))
        v = _gelu(zv_ref[...].astype(F32))
        mu = jnp.mean(v, axis=-1, keepdims=True)
        vc = v - mu
        rstd = lax.rsqrt(jnp.mean(vc * vc, axis=-1, keepdims=True) + EPS)
        vn = (vc * rstd * g_ref[...] + b_ref[...]).astype(BF16)
        tri = _tril_mask()
        for gi in range(4):
            wg = jnp.where(tri, w_ref[gi], 0.0).astype(BF16)
            bg = bias_ref[gi]
            for c in range(tm // 128):
                rs, cs = slice(c * 128, (c + 1) * 128), slice(gi * 128, (gi + 1) * 128)
                mixed = jnp.dot(wg, vn[rs, cs], preferred_element_type=F32) + bg
                o_ref[rs, cs] = (u[rs, cs] * mixed).astype(BF16)

    half = pl.BlockSpec((tm, 512), lambda i: (i, 0))
    return pl.pallas_call(
        body, name="mix_a_fwd", grid=(T // tm,),
        in_specs=[half, pl.BlockSpec((tm, 512), lambda i: (i, 1)),
                  pl.BlockSpec((1, 512), lambda i: (0, 0)), pl.BlockSpec((1, 512), lambda i: (0, 0)),
                  pl.BlockSpec((4, 128, 128), lambda i: (0, 0, 0)), pl.BlockSpec((4, 128, 1), lambda i: (0, 0, 0))],
        out_specs=half, out_shape=_sds((T, D), BF16), compiler_params=_cparams(1),
    )(pab, pab, sgu_g, sgu_b, sgu_w, sgu_bias3)


def _rope_tables():
    pos = np.arange(T, dtype=np.float32)
    inv_freq = np.float32(ROPE_THETA) ** (-np.arange(ROPE_HALF, dtype=np.float32) * np.float32(2.0 / (2 * ROPE_HALF)))
    ang = (pos[:, None] * inv_freq[None, :]).astype(np.float32)
    cos, sin = np.cos(ang), np.sin(ang)
    z8 = np.zeros((T, ROPE_HALF), np.float32)
    rest = np.zeros((T, HEAD - 2 * ROPE_HALF), np.float32)
    c64 = np.concatenate([cos, cos, rest + 1.0], axis=1)
    s1 = np.concatenate([z8, sin, rest], axis=1)
    s2 = np.concatenate([-sin, z8, rest], axis=1)
    return tuple(jnp.asarray(np.tile(t, (1, 2)).astype(np.float32)) for t in (c64, s1, s2))


def _lo_mask(shape):
    return lax.broadcasted_iota(jnp.int32, shape, 1) < HEAD


def _seg_mean(x, lo):
    s_all = jnp.sum(x, axis=-1, keepdims=True)
    s_lo = jnp.sum(jnp.where(lo, x, 0.0), axis=-1, keepdims=True)
    return jnp.where(lo, s_lo, s_all - s_lo) * (1.0 / HEAD)


def _rope(n, c, s1, s2):
    return n * c + pltpu.roll(n, ROPE_HALF, 1) * s1 + pltpu.roll(n, PAIR - ROPE_HALF, 1) * s2


def _rope_t(dy, c, s1, s2):
    return dy * c - pltpu.roll(dy, PAIR - ROPE_HALF, 1) * s2 - pltpu.roll(dy, ROPE_HALF, 1) * s1


def _prep_fwd(pab, qg, kg, tabs, tm=512):
    def body(p_ref, qg_ref, kg_ref, c_ref, s1_ref, s2_ref, *outs):
        lo = _lo_mask((tm, PAIR))
        c, s1, s2 = c_ref[...], s1_ref[...], s2_ref[...]
        for g in range(3):
            qn_ref, kn_ref, v_ref = outs[3 * g:3 * g + 3]
            for p in range(NPAIR):
                for which, gains, dst in ((0, qg_ref, qn_ref), (1, kg_ref, kn_ref)):
                    col = (2 + 3 * which + g) * 512 + p * PAIR
                    xr = p_ref[:, col:col + PAIR].astype(F32)
                    rinv = lax.rsqrt(_seg_mean(xr * xr, lo) + EPS)
                    outs[9 + 2 * g + which][p] = rinv
                    dst[p] = _rope(xr * rinv * gains[g:g + 1, :], c, s1, s2)
                col = (8 + g) * 512 + p * PAIR
                v_ref[p] = p_ref[:, col:col + PAIR].astype(F32)

    pm = pl.BlockSpec((NPAIR, tm, PAIR), lambda i: (0, i, 0))
    tab = pl.BlockSpec((tm, PAIR), lambda i: (i, 0))
    gain = pl.BlockSpec((3, PAIR), lambda i: (0, 0))
    res = pl.pallas_call(
        body, name="prep_fwd", grid=(T // tm,),
        in_specs=[pl.BlockSpec((tm, AB_IN), lambda i: (i, 0)), gain, gain, tab, tab, tab],
        out_specs=[pm] * 15, out_shape=[_sds((NPAIR, T, PAIR), F32)] * 15,
        compiler_params=_cparams(1))(pab, qg, kg, *tabs)
    return res[0:9], res[9:15]


def _res_index(it, rate):
    window = NBACK * rate
    b = it // rate
    rho = it % rate
    start = b * window + rho
    startp = jnp.maximum(start - window, rho)
    kmin = jnp.where(b > 0, 0, NBACK)
    return start, startp, kmin


def _rows(start, rate):
    if rate == 1:
        return pl.ds(pl.multiple_of(start, NBACK), NBACK)
    return pl.ds(start, NBACK, stride=rate)


def _band_bias():
    qs = lax.broadcasted_iota(jnp.int32, (2 * NBACK, 2 * NBACK), 0)
    kj = lax.broadcasted_iota(jnp.int32, (2 * NBACK, 2 * NBACK), 1)
    dist = (qs & (NBACK - 1)) + NBACK - kj
    both = (dist >= 0) & (dist <= NBACK)
    return jnp.where(both, 0.0, NEG_INF), jnp.where(both & (kj >= NBACK), 0.0, NEG_INF)


def _attn_fwd(qn, kn, v, rate, name, dep=None):
    def body(q_ref, k_ref, v_ref, *rest):
        o_ref, l_ref = rest[-2:]
        lo = _lo_mask((NBACK, PAIR))
        bias_all, bias_first = _band_bias()

        def step(it, carry):
            start, startp, kmin = _res_index(it, rate)
            q = q_ref[_rows(start, rate), :] * (HEAD ** -0.5)
            kcat = jnp.concatenate([k_ref[_rows(startp, rate), :], k_ref[_rows(start, rate), :]], axis=0).astype(BF16)
            vcat = jnp.concatenate([v_ref[_rows(startp, rate), :], v_ref[_rows(start, rate), :]], axis=0).astype(BF16)
            vcat1 = jnp.concatenate([vcat, jnp.ones((2 * NBACK, PAIR), BF16)], axis=1)
            q2 = jnp.concatenate([jnp.where(lo, q, 0.0), jnp.where(lo, 0.0, q)], axis=0).astype(BF16)
            s = lax.dot_general(q2, kcat, (((1,), (1,)), ((), ())), preferred_element_type=F32)
            s = s + jnp.where(kmin == 0, bias_all, bias_first)
            m = jnp.max(s, axis=-1, keepdims=True)
            ol = jnp.dot(jnp.exp(s - m).astype(BF16), vcat1, preferred_element_type=F32)
            o2 = ol[:, 0:PAIR] / ol[:, PAIR:]
            ls = m + jnp.log(ol[:, PAIR:])
            o_ref[_rows(start, rate), :] = jnp.where(lo, o2[0:NBACK], o2[NBACK:])
            l_ref[_rows(start, rate), :] = jnp.where(lo, ls[0:NBACK], ls[NBACK:])
            return carry

        lax.fori_loop(0, T // NBACK, step, 0, unroll=4)

    pm = pl.BlockSpec((None, T, PAIR), lambda p: (p, 0, 0))
    in_specs, args = [pm, pm, pm], [qn, kn, v]
    if dep is not None:
        in_specs.append(HBM_SPEC)
        args.append(dep)
    return pl.pallas_call(
        body, name=name, grid=(NPAIR,), in_specs=in_specs, out_specs=[pm, pm],
        out_shape=[_sds((NPAIR, T, PAIR), F32)] * 2, compiler_params=_cparams(1))(*args)


def _merge_fwd(cat_ab, outs, lses, tm=512):
    def body(cat_in, o0, o1, o2, l0, l1, l2, cat_ref, lse_ref):
        del cat_in
        for p in range(NPAIR):
            a0, a1, a2 = l0[p], l1[p], l2[p]
            m = jnp.maximum(jnp.maximum(a0, a1), a2)
            w0, w1, w2 = jnp.exp(a0 - m), jnp.exp(a1 - m), jnp.exp(a2 - m)
            s = w0 + w1 + w2
            b = (w0 * o0[p] + w1 * o1[p] + w2 * o2[p]) / s
            cat_ref[:, p * PAIR:(p + 1) * PAIR] = b.astype(BF16)
            lse_ref[p] = m + jnp.log(s)

    pm = pl.BlockSpec((NPAIR, tm, PAIR), lambda i: (0, i, 0))
    return pl.pallas_call(
        body, name="merge_fwd", grid=(T // tm,),
        in_specs=[pl.BlockSpec(memory_space=pl.ANY)] + [pm] * 6,
        out_specs=[pl.BlockSpec((tm, 512), lambda i: (i, 1)), pm],
        out_shape=[_sds((T, D), BF16), _sds((NPAIR, T, PAIR), F32)],
        input_output_aliases={0: 0}, compiler_params=_cparams(1))(cat_ab, *outs, *lses)


def _b_pre_bwd(dcat, cat, tm=512):
    def body(db_ref, b_ref, dbp_ref, e_ref):
        lo = _lo_mask((tm, PAIR))
        for p in range(NPAIR):
            db = db_ref[:, p * PAIR:(p + 1) * PAIR].astype(F32)
            b = b_ref[:, p * PAIR:(p + 1) * PAIR].astype(F32)
            dbp_ref[p] = db
            e_ref[p] = _seg_mean(db * b, lo) * float(HEAD)

    pm = pl.BlockSpec((NPAIR, tm, PAIR), lambda i: (0, i, 0))
    right = pl.BlockSpec((tm, 512), lambda i: (i, 1))
    return pl.pallas_call(
        body, name="b_pre_bwd", grid=(T // tm,), in_specs=[right, right], out_specs=[pm, pm],
        out_shape=[_sds((NPAIR, T, PAIR), F32)] * 2, compiler_params=_cparams(1))(dcat, cat)


def _attn_bwd(qn, kn, v, dbp, e, lse, rate, name):
    def body(q_ref, k_ref, v_ref, db_ref, e_ref, lse_ref, dq_ref, dk_ref, dv_ref):
        lo = _lo_mask((NBACK, PAIR))
        bias_all, bias_first = _band_bias()
        scale = HEAD ** -0.5
        nt = (((1,), (1,)), ((), ()))
        tn = (((0,), (0,)), ((), ()))
        window = NBACK * rate
        nblk = T // window

        def one(it, carry):
            dk_carry, dv_carry = carry
            rho = it // nblk
            b = it % nblk
            start = b * window + rho
            rq = _rows(start, rate)
            rp = _rows(jnp.maximum(start - window, rho), rate)
            q = q_ref[rq, :] * scale
            db = db_ref[rq, :]
            ev = e_ref[rq, :]
            ls = lse_ref[rq, :]
            kcat = jnp.concatenate([k_ref[rp, :], k_ref[rq, :]], axis=0).astype(BF16)
            vcat = jnp.concatenate([v_ref[rp, :], v_ref[rq, :]], axis=0).astype(BF16)
            q2 = jnp.concatenate([jnp.where(lo, q, 0.0), jnp.where(lo, 0.0, q)], axis=0).astype(BF16)
            db2 = jnp.concatenate([jnp.where(lo, db, 0.0), jnp.where(lo, 0.0, db)], axis=0).astype(BF16)
            ls2 = jnp.concatenate([ls[:, 0:1], ls[:, HEAD:HEAD + 1]], axis=0)
            ev2 = jnp.concatenate([ev[:, 0:1], ev[:, HEAD:HEAD + 1]], axis=0)
            s = lax.dot_general(q2, kcat, nt, preferred_element_type=F32)
            pt = jnp.exp(s + jnp.where(b > 0, bias_all, bias_first) - ls2)
            dp = lax.dot_general(db2, vcat, nt, preferred_element_type=F32)
            ds = (pt * (dp - ev2)).astype(BF16)
            dq2 = jnp.dot(ds, kcat, preferred_element_type=F32) * scale
            dkc = lax.dot_general(ds, q2, tn, preferred_element_type=F32)
            dvc = lax.dot_general(pt.astype(BF16), db2, tn, preferred_element_type=F32)
            dq_ref[rq, :] = jnp.where(lo, dq2[0:NBACK], dq2[NBACK:])
            dk_ref[rp, :] = dk_carry + dkc[0:NBACK]
            dk_ref[rq, :] = dkc[NBACK:]
            dv_ref[rp, :] = dv_carry + dvc[0:NBACK]
            dv_ref[rq, :] = dvc[NBACK:]
            return dkc[NBACK:], dvc[NBACK:]

        def step(i, carry):
            for u in range(ATTN_BWD_UNROLL):
                carry = one(i * ATTN_BWD_UNROLL + u, carry)
            return carry

        zero = jnp.zeros((NBACK, PAIR), F32)
        lax.fori_loop(0, T // NBACK // ATTN_BWD_UNROLL, step, (zero, zero))

    pm = pl.BlockSpec((None, T, PAIR), lambda p: (p, 0, 0))
    return pl.pallas_call(
        body, name=name, grid=(NPAIR,), in_specs=[pm] * 6, out_specs=[pm] * 3,
        out_shape=[_sds((NPAIR, T, PAIR), F32)] * 3, compiler_params=_cparams(1, 56))(qn, kn, v, dbp, e, lse)


def _ab_in_bwd(pab, dcat, sgu_g, sgu_b, sgu_w, sgu_bias3, qg, kg, tabs, dqkv, rinvs, tm=256):
    def body(p_ref, dcat_ref, g_ref, b_ref, w_ref, bias_ref, qg_ref, kg_ref, c_ref, s1_ref, s2_ref, *rest):
        dq_refs, rinv_refs = rest[0:9], rest[9:15]
        o_ref, dwm_ref, dbias_ref, dsg_ref, dsb_ref, dgain_ref = rest[15:]
        i = pl.program_id(0)

        @pl.when(i == 0)
        def _():
            dwm_ref[...] = jnp.zeros_like(dwm_ref)
            dbias_ref[...] = jnp.zeros_like(dbias_ref)
            dsg_ref[...] = jnp.zeros_like(dsg_ref)
            dsb_ref[...] = jnp.zeros_like(dsb_ref)
            dgain_ref[...] = jnp.zeros_like(dgain_ref)

        zu = p_ref[:, 0:512].astype(F32)
        zv = p_ref[:, 512:1024].astype(F32)
        u = _gelu(zu)
        v = _gelu(zv)
        mu = jnp.mean(v, axis=-1, keepdims=True)
        vc = v - mu
        rstd = lax.rsqrt(jnp.mean(vc * vc, axis=-1, keepdims=True) + EPS)
        xhat = vc * rstd
        vn = (xhat * g_ref[...] + b_ref[...]).astype(BF16)
        da = dcat_ref[...].astype(F32)
        tri = _tril_mask()
        du_parts = [[None] * 4 for _ in range(tm // 128)]
        dvn_parts = [[None] * 4 for _ in range(tm // 128)]
        for gi in range(4):
            wg = jnp.where(tri, w_ref[gi], 0.0).astype(BF16)
            bg = bias_ref[gi]
            for c in range(tm // 128):
                rs, cs = slice(c * 128, (c + 1) * 128), slice(gi * 128, (gi + 1) * 128)
                vblk = vn[rs, cs]
                mixed = jnp.dot(wg, vblk, preferred_element_type=F32) + bg
                dab = da[rs, cs]
                du_parts[c][gi] = dab * mixed
                dmixed = dab * u[rs, cs]
                dmb = dmixed.astype(BF16)
                dvn_parts[c][gi] = lax.dot_general(wg, dmb, (((0,), (0,)), ((), ())), preferred_element_type=F32)
                dwm = lax.dot_general(dmb, vblk, (((1,), (1,)), ((), ())), preferred_element_type=F32)
                dwm_ref[gi] += jnp.where(tri, dwm, 0.0)
                dbias_ref[gi] += dmixed
        du = jnp.concatenate([jnp.concatenate(r, axis=1) for r in du_parts], axis=0)
        dvn = jnp.concatenate([jnp.concatenate(r, axis=1) for r in dvn_parts], axis=0)
        dsg_ref[...] += jnp.sum(dvn * xhat, axis=0, keepdims=True)
        dsb_ref[...] += jnp.sum(dvn, axis=0, keepdims=True)
        dxh = dvn * g_ref[...]
        dv = rstd * (dxh - jnp.mean(dxh, axis=-1, keepdims=True)
                     - xhat * jnp.mean(dxh * xhat, axis=-1, keepdims=True))
        o_ref[:, 0:512] = (du * _gelu_grad(zu)).astype(BF16)
        o_ref[:, 512:1024] = (dv * _gelu_grad(zv)).astype(BF16)

        lo = _lo_mask((tm, PAIR))
        c, s1, s2 = c_ref[...], s1_ref[...], s2_ref[...]
        for g in range(3):
            dq_ref, dk_ref, dv_ref = dq_refs[3 * g:3 * g + 3]
            for p in range(NPAIR):
                for which, gains, src in ((0, qg_ref, dq_ref), (1, kg_ref, dk_ref)):
                    col = (2 + 3 * which + g) * 512 + p * PAIR
                    xr = p_ref[:, col:col + PAIR].astype(F32)
                    rinv = rinv_refs[2 * g + which][p]
                    xh = xr * rinv
                    dn = _rope_t(src[p], c, s1, s2)
                    row = 2 * g + which
                    dgain_ref[row:row + 1, :] += jnp.sum(dn * xh, axis=0, keepdims=True)
                    dxh2 = dn * gains[g:g + 1, :]
                    dx = rinv * (dxh2 - xh * _seg_mean(dxh2 * xh, lo))
                    o_ref[:, col:col + PAIR] = dx.astype(BF16)
                col = (8 + g) * 512 + p * PAIR
                o_ref[:, col:col + PAIR] = dv_ref[p].astype(BF16)

    pm = pl.BlockSpec((NPAIR, tm, PAIR), lambda i: (0, i, 0))
    tab = pl.BlockSpec((tm, PAIR), lambda i: (i, 0))
    gain = pl.BlockSpec((3, PAIR), lambda i: (0, 0))
    vec = pl.BlockSpec((1, 512), lambda i: (0, 0))
    full = pl.BlockSpec((tm, AB_IN), lambda i: (i, 0))
    w4 = pl.BlockSpec((4, 128, 128), lambda i: (0, 0, 0))
    return pl.pallas_call(
        body, name="ab_in_bwd", grid=(T // tm,),
        in_specs=[full, pl.BlockSpec((tm, 512), lambda i: (i, 0)), vec, vec, w4,
                  pl.BlockSpec((4, 128, 1), lambda i: (0, 0, 0)), gain, gain, tab, tab, tab] + [pm] * 15,
        out_specs=[full, w4, w4, vec, vec, pl.BlockSpec((8, PAIR), lambda i: (0, 0))],
        out_shape=[_sds((T, AB_IN), BF16), _sds((4, 128, 128), F32), _sds((4, 128, 128), F32),
                   _sds((1, 512), F32), _sds((1, 512), F32), _sds((8, PAIR), F32)],
        compiler_params=_cparams(1))(pab, dcat, sgu_g, sgu_b, sgu_w, sgu_bias3, qg, kg, *tabs, *dqkv, *rinvs)


def _ln_stats(x):
    mu = jnp.mean(x, axis=-1, keepdims=True)
    xc = x - mu
    rstd = lax.rsqrt(jnp.mean(xc * xc, axis=-1, keepdims=True) + EPS)
    return xc * rstd, rstd


CONV_RC = 64


def _shifted_copies(src, dst, tm):
    dst[0] = src[...]
    for b in range(1, 8):
        dst[b, 0:tm + HALO - 8, :] = src[pl.ds(b, tm + HALO - 8), :]


def _offsets_by_phase(first):
    groups = {}
    for o in range(first, first + CONV_C_TAPS):
        groups.setdefault(o % 8, []).append(o)
    return sorted(groups.items())


def _window(shifted, b8, base, offsets, lanes):
    rows = 8 * (max(offsets) // 8) + CONV_RC
    return shifted[b8, pl.ds(base, rows), lanes].reshape(rows // 8, 8, 128)


def _cd_fwd(pcd, cw, cb, lg, lb, dw, tm=512):
    per = tm // HALO

    def body(p_ref, h_ref, cw_ref, cb_ref, lg_ref, lb_ref, dw_ref, cat_ref, c0_ref, c1_ref, dd_ref, y_ref,
             buf, buf2, sb):
        i = pl.program_id(0)
        live = jnp.where(i > 0, 1.0, 0.0)
        a = p_ref[:, 0:512].astype(F32)
        gt = p_ref[:, 512:1024].astype(F32)
        gb = p_ref[:, 1024:1536].astype(F32)
        gc = p_ref[:, 1536:2048].astype(F32)
        hv = p_ref[:, 2048:2560].astype(F32)
        c0 = a * _sigmoid(gt)
        dd = gc * hv
        buf[0:HALO, :] = h_ref[:, 0:512].astype(F32) * _sigmoid(h_ref[:, 512:1024].astype(F32)) * live
        buf[HALO:, :] = c0
        buf2[0:HALO, :] = h_ref[:, 1536:2048].astype(F32) * h_ref[:, 2048:2560].astype(F32) * live
        buf2[HALO:, :] = dd
        c0_ref[...] = c0.astype(BF16)
        dd_ref[...] = dd.astype(BF16)
        _shifted_copies(buf, sb, tm)

        def conv_rows(r, carry):
            base = pl.multiple_of(r * CONV_RC, CONV_RC)
            for c in range(4):
                lanes = slice(c * 128, (c + 1) * 128)
                acc = jnp.broadcast_to(cb_ref[:, lanes], (CONV_RC // 8, 8, 128))
                for b8, offsets in _offsets_by_phase(HALO - (CONV_C_TAPS - 1)):
                    win = _window(sb, b8, base, offsets, lanes)
                    for o in offsets:
                        j = o - (HALO - (CONV_C_TAPS - 1))
                        acc = acc + cw_ref[8 * j:8 * j + 8, lanes] * win[o // 8:o // 8 + CONV_RC // 8]
                c1_ref[pl.ds(base, CONV_RC), lanes] = acc.reshape(CONV_RC, 128)
            return carry

        lax.fori_loop(0, tm // CONV_RC, conv_rows, 0)
        xhat, _ = _ln_stats(c1_ref[...])
        c2 = xhat * lg_ref[...] + lb_ref[...]
        y = jnp.zeros((tm, 512), F32)
        for j in range(CONV_D_TAPS):
            y = y + dw_ref[j:j + 1, :] * buf2[pl.ds(HALO - (CONV_D_TAPS - 1) + j, tm), :]
        cat_ref[:, 0:512] = (c2 * _sigmoid(c2)).astype(BF16)
        cat_ref[:, 512:1024] = (gb * y).astype(BF16)
        y_ref[...] = y.astype(BF16)

    half = pl.BlockSpec((tm, 512), lambda i: (i, 0))
    vec = pl.BlockSpec((1, 512), lambda i: (0, 0))
    return pl.pallas_call(
        body, name="cd_fwd", grid=(T // tm,),
        in_specs=[pl.BlockSpec((tm, CD_IN), lambda i: (i, 0)),
                  pl.BlockSpec((HALO, CD_IN), lambda i: (jnp.maximum(i * per - 1, 0), 0)),
                  pl.BlockSpec((8 * 32, 512), lambda i: (0, 0)), vec, vec, vec, pl.BlockSpec((8, 512), lambda i: (0, 0))],
        out_specs=[pl.BlockSpec((tm, D), lambda i: (i, 0)), half, half, half, half],
        out_shape=[_sds((T, D), BF16), _sds((T, 512), BF16), _sds((T, 512), F32), _sds((T, 512), BF16),
                   _sds((T, 512), BF16)],
        scratch_shapes=[pltpu.VMEM((HALO + tm, 512), F32), pltpu.VMEM((HALO + tm, 512), F32),
                        pltpu.VMEM((8, HALO + tm, 512), F32)],
        compiler_params=_cparams(1))(pcd, pcd, cw, cb, lg, lb, dw)


def _cd_bwd_pw(dcat, c1, pcd, y, lg, lb, tm=512):
    def body(dcat_ref, c1_ref, gb_ref, y_ref, lg_ref, lb_ref, dc1_ref, dy3_ref, dgb_ref, dlg_ref, dlb_ref, dcb_ref):
        i = pl.program_id(0)

        @pl.when(i == 0)
        def _():
            dlg_ref[...] = jnp.zeros_like(dlg_ref)
            dlb_ref[...] = jnp.zeros_like(dlb_ref)
            dcb_ref[...] = jnp.zeros_like(dcb_ref)

        dc = dcat_ref[:, 0:512].astype(F32)
        ddo = dcat_ref[:, 512:1024].astype(F32)
        xhat, rstd = _ln_stats(c1_ref[...])
        c2 = xhat * lg_ref[...] + lb_ref[...]
        sg = _sigmoid(c2)
        dc2 = dc * sg * (1.0 + c2 * (1.0 - sg))
        dlg_ref[...] += jnp.sum(dc2 * xhat, axis=0, keepdims=True)
        dlb_ref[...] += jnp.sum(dc2, axis=0, keepdims=True)
        dxh = dc2 * lg_ref[...]
        dc1 = rstd * (dxh - jnp.mean(dxh, axis=-1, keepdims=True)
                      - xhat * jnp.mean(dxh * xhat, axis=-1, keepdims=True))
        dcb_ref[...] += jnp.sum(dc1, axis=0, keepdims=True)
        dc1_ref[...] = dc1
        dgb_ref[...] = (ddo * y_ref[...].astype(F32)).astype(BF16)
        dy3_ref[...] = ddo * gb_ref[...].astype(F32)

    half = pl.BlockSpec((tm, 512), lambda i: (i, 0))
    vec = pl.BlockSpec((1, 512), lambda i: (0, 0))
    return pl.pallas_call(
        body, name="cd_bwd_pw", grid=(T // tm,),
        in_specs=[pl.BlockSpec((tm, D), lambda i: (i, 0)), half, pl.BlockSpec((tm, 512), lambda i: (i, 2)), half,
                  vec, vec],
        out_specs=[half, half, half, vec, vec, vec],
        out_shape=[_sds((T, 512), F32), _sds((T, 512), F32), _sds((T, 512), BF16),
                   _sds((1, 512), F32), _sds((1, 512), F32), _sds((1, 512), F32)],
        compiler_params=_cparams(1))(dcat, c1, pcd, y, lg, lb)


def _cd_bwd_conv(pcd, dc1, dy3, c0, dd, dgb, cw8, dw, tm=256):
    per = tm // HALO
    nblk = T // tm
    last32 = T // HALO - 1

    def body(p_ref, dc1_ref, dc1n_ref, dy3_ref, dy3n_ref, c0_ref, dd_ref, dgb_ref, cw_ref, dw_ref,
             o_ref, dcw_ref, ddw_ref, dbuf, d3buf, sd, dc0_buf):
        i = pl.program_id(0)
        has_next = jnp.where(i < nblk - 1, 1.0, 0.0)

        @pl.when(i == 0)
        def _():
            dcw_ref[...] = jnp.zeros_like(dcw_ref)
            ddw_ref[...] = jnp.zeros_like(ddw_ref)

        dbuf[0:tm, :] = dc1_ref[...]
        dbuf[tm:, :] = dc1n_ref[...] * has_next
        d3buf[0:tm, :] = dy3_ref[...]
        d3buf[tm:, :] = dy3n_ref[...] * has_next
        _shifted_copies(dbuf, sd, tm)
        n_tiles = tm // CONV_RC

        phases = _offsets_by_phase(0)

        def dc0_rows(r, carry):
            base = pl.multiple_of(r * CONV_RC, CONV_RC)
            for c in range(4):
                lanes = slice(c * 128, (c + 1) * 128)
                acc = jnp.zeros((CONV_RC // 8, 8, 128), F32)
                for b8, offsets in phases:
                    win = _window(sd, b8, base, offsets, lanes)
                    for o in offsets:
                        j = CONV_C_TAPS - 1 - o
                        acc = acc + cw_ref[8 * j:8 * j + 8, lanes] * win[o // 8:o // 8 + CONV_RC // 8]
                dc0_buf[pl.ds(base, CONV_RC), lanes] = acc.reshape(CONV_RC, 128)
            return carry

        lax.fori_loop(0, n_tiles, dc0_rows, 0)

        for c in range(4):
            lanes = slice(c * 128, (c + 1) * 128)
            for b8, offsets in phases:
                def dw_rows(r, accs, lanes=lanes, b8=b8, offsets=offsets):
                    base = pl.multiple_of(r * CONV_RC, CONV_RC)
                    xin = c0_ref[pl.ds(base, CONV_RC), lanes].astype(F32).reshape(CONV_RC // 8, 8, 128)
                    win = _window(sd, b8, base, offsets, lanes)
                    return tuple(acc + jnp.sum(xin * win[o // 8:o // 8 + CONV_RC // 8], axis=0)
                                 for acc, o in zip(accs, offsets))

                accs = lax.fori_loop(0, n_tiles, dw_rows, tuple(jnp.zeros((8, 128), F32) for _ in offsets))
                for acc, o in zip(accs, offsets):
                    j = CONV_C_TAPS - 1 - o
                    dcw_ref[j:j + 1, lanes] += jnp.sum(acc, axis=0, keepdims=True)

        dc0 = dc0_buf[...]
        ddin = dd_ref[...].astype(F32)
        ddd = jnp.zeros((tm, 512), F32)
        for j in range(CONV_D_TAPS):
            dy_shift = d3buf[pl.ds(CONV_D_TAPS - 1 - j, tm), :]
            ddd = ddd + dw_ref[j:j + 1, :] * dy_shift
            ddw_ref[j:j + 1, :] += jnp.sum(ddin * dy_shift, axis=0, keepdims=True)

        a = p_ref[:, 0:512].astype(F32)
        gt = p_ref[:, 512:1024].astype(F32)
        gc = p_ref[:, 1536:2048].astype(F32)
        hv = p_ref[:, 2048:2560].astype(F32)
        sg = _sigmoid(gt)
        o_ref[:, 0:512] = (dc0 * sg).astype(BF16)
        o_ref[:, 512:1024] = (dc0 * a * sg * (1.0 - sg)).astype(BF16)
        o_ref[:, 1024:1536] = dgb_ref[...]
        o_ref[:, 1536:2048] = (ddd * hv).astype(BF16)
        o_ref[:, 2048:2560] = (ddd * gc).astype(BF16)

    half = pl.BlockSpec((tm, 512), lambda i: (i, 0))
    nxt = pl.BlockSpec((HALO, 512), lambda i: (jnp.minimum((i + 1) * per, last32), 0))
    full = pl.BlockSpec((tm, CD_IN), lambda i: (i, 0))
    return pl.pallas_call(
        body, name="cd_bwd_conv", grid=(nblk,),
        in_specs=[full, half, nxt, half, nxt, half, half, half,
                  pl.BlockSpec((8 * 32, 512), lambda i: (0, 0)), pl.BlockSpec((8, 512), lambda i: (0, 0))],
        out_specs=[full, pl.BlockSpec((32, 512), lambda i: (0, 0)), pl.BlockSpec((8, 512), lambda i: (0, 0))],
        out_shape=[_sds((T, CD_IN), BF16), _sds((32, 512), F32), _sds((8, 512), F32)],
        scratch_shapes=[pltpu.VMEM((tm + HALO, 512), F32), pltpu.VMEM((tm + HALO, 512), F32),
                        pltpu.VMEM((8, tm + HALO, 512), F32), pltpu.VMEM((tm, 512), F32)],
        compiler_params=_cparams(1))(pcd, dc1, dc1, dy3, dy3, c0, dd, dgb, cw8, dw)


def _local_step(x, tgt, W, fetch=None, on_grad=None):
    W = dict(W)
    if fetch is None:
        fetch = lambda stage, after: {}
    if on_grad is None:
        on_grad = lambda key, arr: None
    tabs = _rope_tables()
    qg = jnp.tile(W["q_norm_g"], (1, 2))
    kg = jnp.tile(W["k_norm_g"], (1, 2))
    bias3 = W["sgu_bias"].reshape(4, 128, 1)
    G = {}

    h0 = _rms_fwd(x, W["ab_norm_g"], "rms_fwd_ab", dep=W.get("dep_first"))
    W.update(fetch("ab_in", h0))
    pab = _mm_nt(h0, W["wt_ab_in"], "mm_ab_in", dep=W.get("dep0"))
    cat_ab = _mix_a_fwd(pab, W["sgu_norm_g"], W["sgu_norm_b"], W["sgu_w"], bias3)
    qkv, rinvs = _prep_fwd(pab, qg, kg, tabs)
    outs, lses = [], []
    for g, rate in enumerate(DIL_RATES):
        o, l = _attn_fwd(qkv[3 * g], qkv[3 * g + 1], qkv[3 * g + 2], rate, f"attn_fwd_{g}", dep=W.get(f"dep_attn{g}"))
        outs.append(o)
        lses.append(l)
        W.update(fetch(f"attn{g}", o))
    cat_ab, lse = _merge_fwd(cat_ab, outs, lses)
    W.update(fetch("ab_out", lse))
    x1, h1 = _mm_nn(cat_ab, W["w_ab_out"], "mm_ab_out", mode="rms", resid=x, gain=W["ffn_norm_g"][0:1])
    pf0, act0 = _ffn_in(h1, W["wt_ffn_in0"], "ffn_in0")
    W.update(fetch("ffn_down0", act0))
    x2, h2 = _mm_nn(act0, W["w_ffn_down0"], "mm_ffn_down0", mode="rms", resid=x1, gain=W["cd_norm_g"],
                    dep=W.get("dep_down0"))
    W.update(fetch("cd_in", h2))
    pcd = _mm_nt(h2, W["wt_cd_in"], "mm_cd_in")
    cw8 = jnp.repeat(W["conv_c_w32"], 8, axis=0)
    cat_cd, c0, c1, dd, yv = _cd_fwd(pcd, cw8, W["conv_c_b"], W["c_ln_g"], W["c_ln_b"], W["conv_d_w8"])
    x3, h3 = _mm_nn(cat_cd, W["w_cd_out"], "mm_cd_out", mode="rms", resid=x2, gain=W["ffn_norm_g"][1:2])
    pf1, act1 = _ffn_in(h3, W["wt_ffn_in1"], "ffn_in1")
    dy, dyb, loss_cols = _mm_nn(act1, W["w_ffn_down1"], "mm_ffn_down1", mode="loss", resid=x3, tgt=tgt)

    def ffn_bwd(xin, h, pf, act, dres, dresb, layer):
        G[f"w_ffn_down{layer}"] = _mm_tn(act, dresb, f"mm_g_ffn_down{layer}")
        dep = on_grad(f"w_ffn_down{layer}", G[f"w_ffn_down{layer}"])
        dpf = _ffn_dact(dresb, W[f"w_ffn_down{layer}"], pf, f"ffn_dact{layer}", dep=dep)
        G[f"wt_ffn_in{layer}"] = _mm_tn(dpf, h, f"mm_g_ffn_in{layer}")
        dep = on_grad(f"wt_ffn_in{layer}", G[f"wt_ffn_in{layer}"])
        dx, dxb, G[f"ffn_norm_g{layer}"] = _mm_dh_rms_bwd(
            dpf, W[f"wt_ffn_in{layer}"], xin, W["ffn_norm_g"][layer:layer + 1], dres, f"mm_d_h_ffn{layer}", dep=dep)
        return dx, dxb

    dx3, dx3b = ffn_bwd(x3, h3, pf1, act1, dy, dyb, 1)

    G["w_cd_out"] = _mm_tn(cat_cd, dx3b, "mm_g_cd_out")
    dep = on_grad("w_cd_out", G["w_cd_out"])
    dcat_cd = _mm_nt(dx3b, W["w_cd_out"], "mm_d_cat_cd", dep=dep)
    dc1, dy3, dgb, G["c_ln_g"], G["c_ln_b"], G["conv_c_b"] = _cd_bwd_pw(dcat_cd, c1, pcd, yv, W["c_ln_g"], W["c_ln_b"])
    dpcd, G["conv_c_w32"], G["conv_d_w8"] = _cd_bwd_conv(pcd, dc1, dy3, c0, dd, dgb, cw8, W["conv_d_w8"])
    G["wt_cd_in"] = _mm_tn(dpcd, h2, "mm_g_cd_in")
    dep = on_grad("wt_cd_in", G["wt_cd_in"])
    dx2, dx2b, G["cd_norm_g"] = _mm_dh_rms_bwd(dpcd, W["wt_cd_in"], x2, W["cd_norm_g"], dx3, "mm_d_h_cd", dep=dep)

    dx1, dx1b = ffn_bwd(x1, h1, pf0, act0, dx2, dx2b, 0)

    G["w_ab_out"] = _mm_tn(cat_ab, dx1b, "mm_g_ab_out")
    dep = on_grad("w_ab_out", G["w_ab_out"])
    dcat_ab = _mm_nt(dx1b, W["w_ab_out"], "mm_d_cat_ab", dep=dep)
    dbp, e = _b_pre_bwd(dcat_ab, cat_ab)
    dqkv = []
    for g, rate in enumerate(DIL_RATES):
        dqkv += _attn_bwd(qkv[3 * g], qkv[3 * g + 1], qkv[3 * g + 2], dbp, e, lse, rate, f"attn_bwd_{g}")
    dpab, G["sgu_w"], dbias_part, G["sgu_norm_g"], G["sgu_norm_b"], dgain = _ab_in_bwd(
        pab, dcat_ab, W["sgu_norm_g"], W["sgu_norm_b"], W["sgu_w"], bias3, qg, kg, tabs, dqkv, rinvs)
    G["sgu_bias"] = jnp.sum(dbias_part, axis=-1)
    dgain = dgain[0:6, 0:HEAD] + dgain[0:6, HEAD:PAIR]
    G["q_norm_g"] = dgain[0::2]
    G["k_norm_g"] = dgain[1::2]
    G["wt_ab_in"] = _mm_tn(dpab, h0, "mm_g_ab_in")
    dep = on_grad("wt_ab_in", G["wt_ab_in"])
    grad_x, _, G["ab_norm_g"] = _mm_dh_rms_bwd(dpab, W["wt_ab_in"], x, W["ab_norm_g"], dx1, "mm_d_h_ab", dep=dep)
    return loss_cols, grad_x, G


def _my_place():
    return lax.axis_index("x"), lax.axis_index("y"), lax.axis_index("c")


def _dev_index(px, py, pc):
    return 4 * px + 2 * py + pc


def _flip(place, k):
    x, y, c = place
    return (1 - x if k & 4 else x, 1 - y if k & 2 else y, 1 - c if k & 1 else c)


def _landing(shape, dtype, own):
    buf = lax.empty(shape, dtype)
    for lead, part in own:
        buf = lax.dynamic_update_slice(buf, part.reshape((1,) * len(lead) + part.shape),
                                       tuple(lead) + (0,) * part.ndim)
    return buf


HBM_ONLY = pl.BlockSpec(memory_space=pltpu.HBM)
SEM_SPEC = pl.BlockSpec(memory_space=pltpu.SEMAPHORE)
IN_FLIGHT = pltpu.CompilerParams(has_side_effects=pltpu.SideEffectType.DATAFLOW_SIDE_EFFECTING)


def _in_hbm(a):
    return pltpu.with_memory_space_constraint(a, pltpu.HBM)


def _exchange_start(name, srcs, lands, items, dep=None):
    ns, nl, ni = len(srcs), len(lands), len(items)

    def body(*refs):
        S, L = refs[0:ns], refs[ns:ns + nl]
        first_out = ns + nl + (0 if dep is None else 1)
        send_sems, recv_sems, token = refs[first_out], refs[first_out + 1], refs[-1]
        me = _my_place()
        mi = _dev_index(*me)
        for i, (src, dst) in enumerate(items):
            for k in range(1, NDEV):
                peer = _flip(me, k)
                pltpu.make_async_remote_copy(
                    src_ref=src(S, _dev_index(*peer)), dst_ref=dst(L, mi), send_sem=send_sems.at[7 * i + k - 1],
                    recv_sem=recv_sems.at[7 * i + k - 1], device_id=peer, device_id_type=MESH).start()
        token[...] = jnp.zeros_like(token)

    thru = [pltpu.HBM(a.shape, a.dtype) for a in list(srcs) + list(lands)]
    args = [_in_hbm(a) for a in srcs] + [_in_hbm(a) for a in lands]
    in_specs = [HBM_ONLY] * (ns + nl)
    if dep is not None:
        args.append(dep)
        in_specs.append(HBM_SPEC)
    outs = pl.pallas_call(
        body, name=name, in_specs=in_specs,
        out_shape=(pltpu.SemaphoreType.DMA((7 * ni,)), pltpu.SemaphoreType.DMA((7 * ni,)), *thru, _sds((8, 128), F32)),
        out_specs=(SEM_SPEC, SEM_SPEC, *[HBM_ONLY] * (ns + nl), pl.BlockSpec(memory_space=pltpu.VMEM)),
        input_output_aliases={j: 2 + j for j in range(ns + nl)}, compiler_params=IN_FLIGHT)(*args)
    return dict(send=outs[0], recv=outs[1], srcs=list(outs[2:2 + ns]), lands=list(outs[2 + ns:2 + ns + nl]),
                token=outs[-1], items=items)


def _exchange_wait(name, states, after):
    after = list(after) if isinstance(after, (list, tuple)) else [after]
    counts = [(len(st["srcs"]), len(st["lands"]), len(st["items"])) for st in states]
    n_arrays = sum(c[0] + c[1] for c in counts)

    def body(*refs):
        me = _my_place()
        mi = _dev_index(*me)
        pos = 0
        sem_pos = n_arrays
        for st, (ns, nl, ni) in zip(states, counts):
            S, L = refs[pos:pos + ns], refs[pos + ns:pos + ns + nl]
            send_sems, recv_sems = refs[sem_pos], refs[sem_pos + 1]
            pos += ns + nl
            sem_pos += 2
            for i, (src, dst) in enumerate(st["items"]):
                for k in range(1, NDEV):
                    cp = pltpu.make_async_remote_copy(
                        src_ref=src(S, mi), dst_ref=dst(L, mi), send_sem=send_sems.at[7 * i + k - 1],
                        recv_sem=recv_sems.at[7 * i + k - 1], device_id=me, device_id_type=MESH)
                    cp.wait_send()
                    cp.wait_recv()

    arrays, sems = [], []
    for st in states:
        arrays += st["srcs"] + st["lands"]
        sems += [st["send"], st["recv"]]
    outs = pl.pallas_call(
        body, name=name, in_specs=[HBM_ONLY] * n_arrays + [SEM_SPEC] * len(sems) + [HBM_SPEC] * len(after),
        out_shape=tuple(pltpu.HBM(a.shape, a.dtype) for a in arrays), out_specs=tuple([HBM_ONLY] * n_arrays),
        input_output_aliases={j: j for j in range(n_arrays)}, compiler_params=IN_FLIGHT)(*arrays, *sems, *after)
    lands, pos = [], 0
    for ns, nl, _ in counts:
        lands.append(list(outs[pos + ns:pos + ns + nl]))
        pos += ns + nl
    return lands


def _place_and_neighbours():
    x, y, c = _my_place()
    return (x, y, c), (x, y, 1 - c), [(1 - x, y), (x, 1 - y), (1 - x, 1 - y)]


def _gather_start(name, srcs, lands, items, dep=None):
    ns, nl, ni = len(srcs), len(lands), len(items)

    def body(*refs):
        S, L = refs[0:ns], refs[ns:ns + nl]
        first_out = ns + nl + (0 if dep is None else 1)
        send_sems, recv_sems, token = refs[first_out], refs[first_out + 1], refs[-1]
        me, sib, chips = _place_and_neighbours()
        mi = _dev_index(*me)
        for i, (src, dst) in enumerate(items):
            for k, to in enumerate([sib] + [(*chip, me[2]) for chip in chips]):
                pltpu.make_async_remote_copy(
                    src_ref=src(S), dst_ref=dst(L, mi), send_sem=send_sems.at[4 * i + k],
                    recv_sem=recv_sems.at[4 * i + k], device_id=to, device_id_type=MESH).start()
        token[...] = jnp.zeros_like(token)

    thru = [pltpu.HBM(a.shape, a.dtype) for a in list(srcs) + list(lands)]
    args = [_in_hbm(a) for a in srcs] + [_in_hbm(a) for a in lands]
    in_specs = [HBM_ONLY] * (ns + nl)
    if dep is not None:
        args.append(dep)
        in_specs.append(HBM_SPEC)
    outs = pl.pallas_call(
        body, name=name, in_specs=in_specs,
        out_shape=(pltpu.SemaphoreType.DMA((4 * ni,)), pltpu.SemaphoreType.DMA((4 * ni,)), *thru, _sds((8, 128), F32)),
        out_specs=(SEM_SPEC, SEM_SPEC, *[HBM_ONLY] * (ns + nl), pl.BlockSpec(memory_space=pltpu.VMEM)),
        input_output_aliases={j: 2 + j for j in range(ns + nl)}, compiler_params=IN_FLIGHT)(*args)
    return dict(send=outs[0], recv=outs[1], srcs=list(outs[2:2 + ns]), lands=list(outs[2 + ns:2 + ns + nl]),
                token=outs[-1], items=items)


def _gather_forward(name, st, after):
    nl, ni = len(st["lands"]), len(st["items"])

    def body(*refs):
        L, recv_sems = refs[0:nl], refs[nl]
        fwd_send, fwd_recv, token = refs[2 * nl + 2], refs[2 * nl + 3], refs[-1]
        me, sib, chips = _place_and_neighbours()
        for i, (_, dst) in enumerate(st["items"]):
            for j, chip in enumerate(chips):
                blk = dst(L, _dev_index(*chip, me[2]))
                pltpu.make_async_remote_copy(
                    src_ref=blk, dst_ref=blk, send_sem=fwd_send.at[3 * i + j], recv_sem=recv_sems.at[4 * i + 1 + j],
                    device_id=me, device_id_type=MESH).wait_recv()
                pltpu.make_async_remote_copy(
                    src_ref=blk, dst_ref=blk, send_sem=fwd_send.at[3 * i + j], recv_sem=fwd_recv.at[3 * i + j],
                    device_id=sib, device_id_type=MESH).start()
        token[...] = jnp.zeros_like(token)

    outs = pl.pallas_call(
        body, name=name, in_specs=[HBM_ONLY] * nl + [SEM_SPEC, HBM_SPEC],
        out_shape=(*[pltpu.HBM(a.shape, a.dtype) for a in st["lands"]], pltpu.SemaphoreType.DMA((3 * ni,)),
                   pltpu.SemaphoreType.DMA((3 * ni,)), _sds((8, 128), F32)),
        out_specs=(*[HBM_ONLY] * nl, SEM_SPEC, SEM_SPEC, pl.BlockSpec(memory_space=pltpu.VMEM)),
        input_output_aliases={j: j for j in range(nl)}, compiler_params=IN_FLIGHT)(*st["lands"], st["recv"], after)
    return dict(st, lands=list(outs[0:nl]), fwd_send=outs[nl], fwd_recv=outs[nl + 1], token=outs[-1])


def _gather_wait(name, st, after):
    ns, nl, ni = len(st["srcs"]), len(st["lands"]), len(st["items"])

    def body(*refs):
        S, L = refs[0:ns], refs[ns:ns + nl]
        send_sems, recv_sems, fwd_send, fwd_recv = refs[ns + nl:ns + nl + 4]
        me, sib, chips = _place_and_neighbours()
        mi = _dev_index(*me)
        for i, (src, dst) in enumerate(st["items"]):
            mine = dst(L, mi)
            for k in range(4):
                pltpu.make_async_remote_copy(
                    src_ref=src(S), dst_ref=mine, send_sem=send_sems.at[4 * i + k], recv_sem=recv_sems.at[4 * i + k],
                    device_id=me, device_id_type=MESH).wait_send()
            pltpu.make_async_remote_copy(
                src_ref=src(S), dst_ref=mine, send_sem=send_sems.at[4 * i], recv_sem=recv_sems.at[4 * i],
                device_id=me, device_id_type=MESH).wait_recv()
            for j in range(3):
                cp = pltpu.make_async_remote_copy(
                    src_ref=mine, dst_ref=mine, send_sem=fwd_send.at[3 * i + j], recv_sem=fwd_recv.at[3 * i + j],
                    device_id=me, device_id_type=MESH)
                cp.wait_send()
                cp.wait_recv()

    arrays = st["srcs"] + st["lands"]
    outs = pl.pallas_call(
        body, name=name, in_specs=[HBM_ONLY] * (ns + nl) + [SEM_SPEC] * 4 + [HBM_SPEC],
        out_shape=tuple(pltpu.HBM(a.shape, a.dtype) for a in arrays), out_specs=tuple([HBM_ONLY] * (ns + nl)),
        input_output_aliases={j: j for j in range(ns + nl)},
        compiler_params=IN_FLIGHT)(*arrays, st["send"], st["recv"], st["fwd_send"], st["fwd_recv"], after)
    return list(outs[ns:ns + nl])


def _sum_slots(land):
    def body(l_ref, o_ref):
        acc = l_ref[0]
        for d in range(1, NDEV):
            acc = acc + l_ref[d]
        o_ref[...] = acc

    vm = pl.BlockSpec(memory_space=pltpu.VMEM)
    return pl.pallas_call(body, name="sum_small", out_shape=_sds(land.shape[1:], F32), in_specs=[vm], out_specs=vm)(land)


def _adam_math(w, g, m, v):
    m2 = ADAM_B1 * m + (1.0 - ADAM_B1) * g
    v2 = ADAM_B2 * v + (1.0 - ADAM_B2) * (g * g)
    delta = -ADAM_LR * ((m2 * ADAM_C1) / (jnp.sqrt(v2 * ADAM_C2) + ADAM_EPS) + ADAM_WD * w)
    return delta, m2, v2


def _adam_layer(land, sel, w, m, v, layer, name, prev=None, tc=512):
    R = land.shape[2]

    def body(l_ref, w_ref, m_ref, v_ref, *rest):
        g_out, d_out, m_out, v_out = rest[-4:]
        g = l_ref[0].astype(F32)
        for d in range(1, NDEV):
            g = g + l_ref[d].astype(F32)
        delta, m2, v2 = _adam_math(w_ref[...], g, m_ref[...], v_ref[...])
        g_out[...] = g
        d_out[...] = delta
        m_out[...] = m2
        v_out[...] = v2

    wspec = pl.BlockSpec((None, R, tc), lambda i: (layer, 0, i))
    in_specs = [pl.BlockSpec((None, NDEV, R, tc), lambda i: (sel, 0, 0, i)), wspec, wspec, wspec]
    args = [land, w, m, v]
    aliases = {}
    if prev is not None:
        in_specs += [HBM_SPEC] * 4
        args += list(prev)
        aliases = {4 + j: j for j in range(4)}
    return pl.pallas_call(
        body, name=name, grid=(D // tc,), in_specs=in_specs, out_specs=[wspec] * 4,
        out_shape=[_sds(w.shape, F32)] * 4, input_output_aliases=aliases, compiler_params=_cparams(1))(*args)


def _adam_stacked(lands, sel, w, m, v, name):
    res = None
    for layer, land in enumerate(lands):
        res = _adam_layer(land, sel, w, m, v, layer, f"{name}{layer}", prev=res)
    return res


def _adam_small(ws, gs, ms, vs):
    n = len(ws)

    def body(*refs):
        w_r, g_r, m_r, v_r = refs[0:n], refs[n:2 * n], refs[2 * n:3 * n], refs[3 * n:4 * n]
        d_o, m_o, v_o = refs[4 * n:5 * n], refs[5 * n:6 * n], refs[6 * n:7 * n]
        for i in range(n):
            delta, m2, v2 = _adam_math(w_r[i][...], g_r[i][...], m_r[i][...], v_r[i][...])
            d_o[i][...] = delta
            m_o[i][...] = m2
            v_o[i][...] = v2

    vm = pl.BlockSpec(memory_space=pltpu.VMEM)
    shapes = [_sds(w.shape, F32) for w in ws]
    outs = pl.pallas_call(body, name="adam_small", in_specs=[vm] * (4 * n), out_specs=[vm] * (3 * n),
                          out_shape=shapes * 3)(*ws, *gs, *ms, *vs)
    return outs[0:n], outs[n:2 * n], outs[2 * n:3 * n]


WEIGHT_NAMES = ("ab_norm_g", "ab_w_in", "sgu_norm_g", "sgu_norm_b", "sgu_w", "sgu_bias", "q_norm_g", "k_norm_g",
                "ab_w_out", "cd_norm_g", "cd_w_in", "conv_c_w", "conv_c_b", "c_ln_g", "c_ln_b", "conv_d_w",
                "cd_w_out", "ffn_norm_g", "ffn_w_gate", "ffn_w_up", "ffn_w_down")
SMALL_2D = (("ab_norm_g", (1, 1024)), ("sgu_norm_g", (1, 512)), ("sgu_norm_b", (1, 512)), ("sgu_w", (512, 128)),
            ("sgu_bias", (4, 128)), ("q_norm_g", (3, 64)), ("k_norm_g", (3, 64)), ("cd_norm_g", (1, 128)),
            ("conv_c_w", (31, 64)), ("conv_c_b", (1, 64)), ("c_ln_g", (1, 64)), ("c_ln_b", (1, 64)),
            ("conv_d_w", (3, 64)), ("ffn_norm_g", (2, 1024)))
SHARD_C = 64


def _pack_rows(parts, rows):
    flat = jnp.concatenate([p.reshape(-1) for p in parts])
    return jnp.pad(flat, (0, rows * 128 - flat.shape[0])).reshape(rows, 128)


def kernel(x, ab_norm_g, ab_w_in, sgu_norm_g, sgu_norm_b, sgu_w, sgu_bias, q_norm_g, k_norm_g, ab_w_out, cd_norm_g, cd_w_in, conv_c_w, conv_c_b, c_ln_g, c_ln_b, conv_d_w, cd_w_out, ffn_norm_g, ffn_w_gate, ffn_w_up, ffn_w_down, loss_target, m_ab_norm_g, m_ab_w_in, m_sgu_norm_g, m_sgu_norm_b, m_sgu_w, m_sgu_bias, m_q_norm_g, m_k_norm_g, m_ab_w_out, m_cd_norm_g, m_cd_w_in, m_conv_c_w, m_conv_c_b, m_c_ln_g, m_c_ln_b, m_conv_d_w, m_cd_w_out, m_ffn_norm_g, m_ffn_w_gate, m_ffn_w_up, m_ffn_w_down, v_ab_norm_g, v_ab_w_in, v_sgu_norm_g, v_sgu_norm_b, v_sgu_w, v_sgu_bias, v_q_norm_g, v_k_norm_g, v_ab_w_out, v_cd_norm_g, v_cd_w_in, v_conv_c_w, v_conv_c_b, v_c_ln_g, v_c_ln_b, v_conv_d_w, v_cd_w_out, v_ffn_norm_g, v_ffn_w_gate, v_ffn_w_up, v_ffn_w_down):
    w = dict(zip(WEIGHT_NAMES, (ab_norm_g, ab_w_in, sgu_norm_g, sgu_norm_b, sgu_w, sgu_bias, q_norm_g, k_norm_g, ab_w_out, cd_norm_g, cd_w_in, conv_c_w, conv_c_b, c_ln_g, c_ln_b, conv_d_w, cd_w_out, ffn_norm_g, ffn_w_gate, ffn_w_up, ffn_w_down)))
    m = dict(zip(WEIGHT_NAMES, (m_ab_norm_g, m_ab_w_in, m_sgu_norm_g, m_sgu_norm_b, m_sgu_w, m_sgu_bias, m_q_norm_g, m_k_norm_g, m_ab_w_out, m_cd_norm_g, m_cd_w_in, m_conv_c_w, m_conv_c_b, m_c_ln_g, m_c_ln_b, m_conv_d_w, m_cd_w_out, m_ffn_norm_g, m_ffn_w_gate, m_ffn_w_up, m_ffn_w_down)))
    v = dict(zip(WEIGHT_NAMES, (v_ab_norm_g, v_ab_w_in, v_sgu_norm_g, v_sgu_norm_b, v_sgu_w, v_sgu_bias, v_q_norm_g, v_k_norm_g, v_ab_w_out, v_cd_norm_g, v_cd_w_in, v_conv_c_w, v_conv_c_b, v_c_ln_g, v_c_ln_b, v_conv_d_w, v_cd_w_out, v_ffn_norm_g, v_ffn_w_gate, v_ffn_w_up, v_ffn_w_down)))
    me = _dev_index(*_my_place())

    small_local = _pack_rows([w["cd_norm_g"], w["conv_c_w"], w["conv_c_b"], w["c_ln_g"], w["c_ln_b"], w["conv_d_w"]], 24)
    r_ff = DFF // NDEV
    one = lambda a: (lambda S, j: S[a])
    slot = lambda b: (lambda L, s: L[b].at[s])
    slot2 = lambda b, part: (lambda L, s: L[b].at[part, s])
    shard = lambda a: (lambda S: S[a])

    def layer_shards(layer):
        return (w["ffn_w_gate"][layer].T.astype(BF16), w["ffn_w_up"][layer].T.astype(BF16),
                w["ffn_w_down"][layer].astype(BF16))

    def gathered(own):
        return _landing((NDEV,) + own.shape, BF16, [((me,), own)])

    def gathered2(a, b):
        return _landing((2, NDEV) + a.shape, BF16, [((0, me), a), ((1, me), b)])

    ab_in_s = w["ab_w_in"][0].T.astype(BF16)
    gathers = {0: _gather_start(
        "gather0_start", [ab_in_s, small_local],
        [gathered(ab_in_s), _landing((NDEV,) + small_local.shape, F32, [((me,), small_local)])],
        [(shard(0), slot(0)), (shard(1), slot(1))])}

    def chan(flat, lo, taps):
        return flat[:, lo:lo + taps * SHARD_C].reshape(NDEV, taps, SHARD_C).transpose(1, 0, 2).reshape(taps, 512)

    def fetch(stage, after):
        if stage == "ab_in":
            gathers[0] = _gather_forward("gather0_forward", gathers[0], after)
            l_ab_in, l_small = _gather_wait("gather0_wait", gathers[0], gathers[0]["token"])
            ab_out_s = w["ab_w_out"][0].astype(BF16)
            gate0, up0, down0 = layer_shards(0)
            gathers[1] = _gather_start(
                "gather1_start", [ab_out_s, gate0, up0, down0],
                [gathered(ab_out_s), gathered2(gate0, up0), gathered(down0)],
                [(shard(0), slot(0)), (shard(1), slot2(1, 0)), (shard(2), slot2(1, 1)), (shard(3), slot(2))],
                dep=l_small)
            flat = l_small.reshape(NDEV, 24 * 128)
            return {
                "wt_ab_in": l_ab_in.reshape(AB_IN, D), "dep0": gathers[1]["token"],
                "cd_norm_g": flat[:, 0:128].reshape(1, D),
                "conv_c_w32": jnp.pad(chan(flat, 128, CONV_C_TAPS), ((0, 1), (0, 0))),
                "conv_c_b": chan(flat, 2112, 1), "c_ln_g": chan(flat, 2176, 1), "c_ln_b": chan(flat, 2240, 1),
                "conv_d_w8": jnp.pad(chan(flat, 2304, CONV_D_TAPS), ((0, 8 - CONV_D_TAPS), (0, 0))),
            }
        if stage == "attn0":
            cd_in_s, cd_out_s = w["cd_w_in"][0].T.astype(BF16), w["cd_w_out"][0].astype(BF16)
            gate1, up1, down1 = layer_shards(1)
            gathers[2] = _gather_start(
                "gather2_start", [cd_in_s, cd_out_s, gate1, up1, down1],
                [gathered(cd_in_s), gathered(cd_out_s), gathered2(gate1, up1), gathered(down1)],
                [(shard(0), slot(0)), (shard(1), slot(1)), (shard(2), slot2(2, 0)), (shard(3), slot2(2, 1)),
                 (shard(4), slot(3))], dep=after)
            return {"dep_attn1": gathers[2]["token"]}
        if stage == "attn1":
            gathers[1] = _gather_forward("gather1_forward", gathers[1], after)
            return {"dep_attn2": gathers[1]["token"]}
        if stage == "ab_out":
            l_out, l_ffn, l_down = _gather_wait("gather1_wait", gathers[1], after)
            return {"w_ab_out": l_out.reshape(D, D), "wt_ffn_in0": l_ffn.reshape(2 * DFF, D),
                    "w_ffn_down0": l_down.reshape(DFF, D)}
        if stage == "ffn_down0":
            gathers[2] = _gather_forward("gather2_forward", gathers[2], after)
            return {"dep_down0": gathers[2]["token"]}
        if stage == "cd_in":
            l_in, l_out, l_ffn, l_down = _gather_wait("gather2_wait", gathers[2], after)
            return {"wt_cd_in": l_in.reshape(CD_IN, D), "w_cd_out": l_out.reshape(D, D),
                    "wt_ffn_in1": l_ffn.reshape(2 * DFF, D), "w_ffn_down1": l_down.reshape(DFF, D)}
        return {}

    scatters = {}
    rides_with = {"w_ffn_down1": "wt_ffn_in1", "w_cd_out": "wt_cd_in", "w_ffn_down0": "wt_ffn_in0"}
    held = {}

    def on_grad(key, arr):
        if key in rides_with:
            held[rides_with[key]] = (key, arr)
            return None
        group = ([held.pop(key)] if key in held else []) + [(key, arr)]
        srcs, lands, items = [], [], []
        for n, (k, a) in enumerate(group):
            if k.startswith("wt_ffn_in"):
                src = a.reshape(2, NDEV, r_ff, D)
                own = lax.dynamic_slice_in_dim(src, me, 1, axis=1)
                lands.append(lax.dynamic_update_slice(lax.empty(src.shape, BF16), own, (0, me, 0, 0)))
                items += [((lambda S, j, n=n: S[n].at[0, j]), slot2(n, 0)), ((lambda S, j, n=n: S[n].at[1, j]), slot2(n, 1))]
            else:
                rows = a.shape[0] // NDEV
                src = a.reshape(NDEV, rows, D)
                own = lax.dynamic_index_in_dim(src, me, 0, keepdims=False)
                lands.append(_landing((1, NDEV, rows, D), BF16, [((0, me), own)]))
                items.append(((lambda S, j, n=n: S[n].at[j]), slot2(n, 0)))
            srcs.append(src)
        st = _exchange_start(f"scatter_{key}_start", srcs, lands, items)
        scatters[key] = (st, [k for k, _ in group])
        return st["token"]

    W = {
        "dep_first": gathers[0]["token"],
        "ab_norm_g": w["ab_norm_g"], "sgu_norm_g": w["sgu_norm_g"], "sgu_norm_b": w["sgu_norm_b"],
        "sgu_w": w["sgu_w"][0], "sgu_bias": w["sgu_bias"][0], "q_norm_g": w["q_norm_g"][0],
        "k_norm_g": w["k_norm_g"][0], "ffn_norm_g": w["ffn_norm_g"],
    }

    loss_cols, grad_x, G = _local_step(x[0], loss_target[0], W, fetch, on_grad)

    small_parts = [G["ab_norm_g"], G["sgu_norm_g"], G["sgu_norm_b"], G["sgu_w"], G["sgu_bias"], G["q_norm_g"],
                   G["k_norm_g"], G["cd_norm_g"], G["conv_c_w32"][:CONV_C_TAPS], G["conv_c_b"], G["c_ln_g"],
                   G["c_ln_b"], G["conv_d_w8"][:CONV_D_TAPS], G["ffn_norm_g0"], G["ffn_norm_g1"], loss_cols]
    sizes = [p.size for p in small_parts]
    small_rows = 720
    packed = _pack_rows(small_parts, small_rows)
    small = _exchange_start("small_start", [packed], [_landing((NDEV, small_rows, 128), F32, [((me,), packed)])],
                            [(one(0), slot(0))])
    landed = {}

    def wait_scatters(name, group_keys, after):
        res = _exchange_wait(name, [scatters[gk][0] for gk in group_keys], after)
        for gk, lands in zip(group_keys, res):
            landed.update(zip(scatters[gk][1], lands))

    wait_scatters("scatter_wait_early", ["wt_ffn_in1", "wt_cd_in", "wt_ffn_in0", "w_ab_out"], small["token"])

    grads, deltas, new_m, new_v = {}, {}, {}, {}
    done = []

    def put(name, res):
        grads[name], deltas[name], new_m[name], new_v[name] = res

    def adam(name, lands, sel, transposed):
        flip = (lambda a: jnp.swapaxes(a, 1, 2)) if transposed else (lambda a: a)
        res = _adam_stacked(lands, sel, flip(w[name]), flip(m[name]), flip(v[name]), f"adam_{name}")
        done.append(res[1])
        put(name, [flip(r) for r in res])

    ffn_in_lands = [landed["wt_ffn_in0"], landed["wt_ffn_in1"]]
    adam("cd_w_in", [landed["wt_cd_in"]], 0, True)
    adam("ffn_w_gate", ffn_in_lands, 0, True)
    adam("ffn_w_up", ffn_in_lands, 1, True)
    adam("cd_w_out", [landed["w_cd_out"]], 0, False)
    adam("ab_w_out", [landed["w_ab_out"]], 0, False)
    adam("ffn_w_down", [landed["w_ffn_down0"], landed["w_ffn_down1"]], 0, False)

    small_land = _exchange_wait("small_wait", [small], list(done))[0][0]
    red = _sum_slots(small_land).reshape(-1)
    offs = [0]
    for s in sizes:
        offs.append(offs[-1] + s)
    seg = [red[offs[i]:offs[i + 1]] for i in range(len(sizes))]
    loss = jnp.sum(seg[15])

    def own_channels(full, taps):
        return lax.dynamic_slice_in_dim(full.reshape(taps, 512), me * SHARD_C, SHARD_C, axis=1)

    g_small = {
        "ab_norm_g": seg[0].reshape(1, 1024), "sgu_norm_g": seg[1].reshape(1, 512), "sgu_norm_b": seg[2].reshape(1, 512),
        "sgu_w": seg[3].reshape(512, 128), "sgu_bias": seg[4].reshape(4, 128), "q_norm_g": seg[5].reshape(3, 64),
        "k_norm_g": seg[6].reshape(3, 64),
        "cd_norm_g": lax.dynamic_slice_in_dim(seg[7].reshape(1, D), me * (D // NDEV), D // NDEV, axis=1),
        "conv_c_w": own_channels(seg[8], CONV_C_TAPS), "conv_c_b": own_channels(seg[9], 1),
        "c_ln_g": own_channels(seg[10], 1), "c_ln_b": own_channels(seg[11], 1),
        "conv_d_w": own_channels(seg[12], CONV_D_TAPS),
        "ffn_norm_g": jnp.concatenate([seg[13].reshape(1, D), seg[14].reshape(1, D)], axis=0),
    }

    names2d = [n for n, _ in SMALL_2D]
    d_s, m_s, v_s = _adam_small([w[n].reshape(s) for n, s in SMALL_2D], [g_small[n] for n in names2d],
                                [m[n].reshape(s) for n, s in SMALL_2D], [v[n].reshape(s) for n, s in SMALL_2D])
    for i, n in enumerate(names2d):
        shape = w[n].shape
        grads[n], deltas[n] = g_small[n].reshape(shape), d_s[i].reshape(shape)
        new_m[n], new_v[n] = m_s[i].reshape(shape), v_s[i].reshape(shape)

    wait_scatters("scatter_wait_last", ["wt_ab_in"], d_s[0])
    adam("ab_w_in", [landed["wt_ab_in"]], 0, True)

    return (loss, grad_x[None], *[grads[n] for n in WEIGHT_NAMES], *[deltas[n] for n in WEIGHT_NAMES],
            *[new_m[n] for n in WEIGHT_NAMES], *[new_v[n] for n in WEIGHT_NAMES])
```

```python
import functools

import jax
import jax.numpy as jnp
import numpy as np
from jax import lax
from jax.experimental import pallas as pl
from jax.experimental.pallas import tpu as pltpu

F32 = jnp.float32
BF16 = jnp.bfloat16

T = 4096
D = 1024
NDEV = 8
EPS = 1e-6
NEG_INF = -1e30
DFF = 2816
AB_IN = 5632
CD_IN = 2560
HEAD = 64
PAIR = 128
NPAIR = 4
NBACK = 128
DIL_RATES = (1, 4, 16)
ROPE_HALF = 8
ROPE_THETA = 500000.0
CONV_C_TAPS = 31
CONV_D_TAPS = 3
HALO = 32
ATTN_BWD_UNROLL = 4

ADAM_LR = 0.001
ADAM_B1 = 0.9
ADAM_B2 = 0.999
ADAM_EPS = 1e-08
ADAM_WD = 0.01
ADAM_STEP = 10
ADAM_C1 = 1.0 / (1.0 - ADAM_B1 ** ADAM_STEP)
ADAM_C2 = 1.0 / (1.0 - ADAM_B2 ** ADAM_STEP)

VMEM_LIMIT_MB = 48
MESH = pl.DeviceIdType.MESH
HBM_SPEC = pl.BlockSpec(memory_space=pl.ANY)


def _cparams(ngrid, vmem_mb=VMEM_LIMIT_MB):
    return pltpu.CompilerParams(dimension_semantics=("arbitrary",) * ngrid,
                                vmem_limit_bytes=vmem_mb * 1024 * 1024)


def _pick(n, options):
    for o in options:
        if n % o == 0:
            return o
    raise ValueError(f"no tile for {n} in {options}")


def _sds(shape, dtype):
    return jax.ShapeDtypeStruct(shape, dtype)


def _sigmoid(x):
    return 1.0 / (1.0 + jnp.exp(-x))


def _sigmoid_bf16(x):
    return 0.5 * jnp.tanh(0.5 * x) + 0.5


def _gelu(z):
    return 0.5 * z * (1.0 + lax.erf(z * 0.7071067811865476))


def _gelu_grad(z):
    return 0.5 * (1.0 + lax.erf(z * 0.7071067811865476)) + z * jnp.exp(-0.5 * z * z) * 0.3989422804014327


def _mm_nt(a, wt, name, out_dtype=BF16, tm=2048, dep=None):
    M, K = a.shape
    N = wt.shape[0]
    tn = _pick(N, (512, 256))

    def body(a_ref, w_ref, *rest):
        o_ref = rest[-1]
        o_ref[...] = lax.dot_general(a_ref[...], w_ref[...], (((1,), (1,)), ((), ())),
                                     preferred_element_type=F32).astype(o_ref.dtype)

    in_specs = [pl.BlockSpec((tm, K), lambda i, j: (i, 0)), pl.BlockSpec((tn, K), lambda i, j: (j, 0))]
    args = [a, wt]
    if dep is not None:
        in_specs.append(HBM_SPEC)
        args.append(dep)
    return pl.pallas_call(
        body, name=name, grid=(M // tm, N // tn), in_specs=in_specs,
        out_specs=pl.BlockSpec((tm, tn), lambda i, j: (i, j)),
        out_shape=_sds((M, N), out_dtype), compiler_params=_cparams(2))(*args)


EPI_ROWS = 256


def _mm_nn(a, w, name, mode, resid, gain=None, tgt=None, dep=None, tm=512):
    M, K = a.shape
    N = w.shape[1]
    side = gain if mode == "rms" else tgt

    def body(a_ref, w_ref, resid_ref, side_ref, *rest):
        outs, acc = rest[-3 if mode == "rms" else -4:-1], rest[-1]
        i = pl.program_id(0)
        acc[...] = jnp.dot(a_ref[...], w_ref[...], preferred_element_type=F32)

        if mode == "loss":
            @pl.when(i == 0)
            def _():
                outs[2][...] = jnp.zeros_like(outs[2])

        for r0 in range(0, tm, EPI_ROWS):
            rows = slice(r0, r0 + EPI_ROWS)
            v = acc[rows, :] + resid_ref[rows, :]
            if mode == "rms":
                outs[0][rows, :] = v
                r = lax.rsqrt(jnp.mean(v * v, axis=-1, keepdims=True) + EPS)
                outs[1][rows, :] = (v * r * side_ref[...]).astype(BF16)
            else:
                d = v - side_ref[rows, :]
                outs[2][...] += jnp.sum(d * d, axis=0, keepdims=True) * (0.5 / N)
                dy = d * (1.0 / N)
                outs[0][rows, :] = dy
                outs[1][rows, :] = dy.astype(BF16)

    row = pl.BlockSpec((tm, N), lambda i: (i, 0))
    vec = pl.BlockSpec((1, N), lambda i: (0, 0))
    in_specs = [pl.BlockSpec((tm, K), lambda i: (i, 0)),
                pl.BlockSpec((K, N), lambda i: (0, 0), pipeline_mode=pl.Buffered(1)), row,
                vec if mode == "rms" else row]
    args = [a, w, resid, side]
    if dep is not None:
        in_specs.append(HBM_SPEC)
        args.append(dep)
    if mode == "rms":
        out_specs, out_shape = [row, row], [_sds((M, N), F32), _sds((M, N), BF16)]
    else:
        out_specs, out_shape = [row, row, vec], [_sds((M, N), F32), _sds((M, N), BF16), _sds((1, N), F32)]
    return pl.pallas_call(
        body, name=name, grid=(M // tm,), in_specs=in_specs, out_specs=out_specs, out_shape=out_shape,
        scratch_shapes=[pltpu.VMEM((tm, N), F32)], compiler_params=_cparams(1))(*args)


def _mm_dh_rms_bwd(a, w, x, gain, dres, name, dep=None, tm=512):
    parts = a.shape[0] if a.ndim == 3 else 1
    M, Kp = a.shape[-2], a.shape[-1]
    N = w.shape[1]
    nblk = M // tm
    assert nblk % 2 == 0

    def body(a_ref, w_ref, x_ref, g_ref, dres_ref, *rest):
        dx_ref, dxb_ref, dg_ref, acc0, acc1 = rest[-5:]
        i = pl.program_id(0)

        def matmul(acc):
            if parts == 1:
                acc[...] = jnp.dot(a_ref[...], w_ref[...], preferred_element_type=F32)
            else:
                d = jnp.dot(a_ref[0], w_ref[0:Kp, :], preferred_element_type=F32)
                for p in range(1, parts):
                    d = d + jnp.dot(a_ref[p], w_ref[p * Kp:(p + 1) * Kp, :], preferred_element_type=F32)
                acc[...] = d

        def finish(acc):
            for r0 in range(0, tm, EPI_ROWS // 2):
                rows = slice(r0, r0 + EPI_ROWS // 2)
                v = acc[rows, :]
                xf = x_ref[rows, :]
                r = lax.rsqrt(jnp.mean(xf * xf, axis=-1, keepdims=True) + EPS)
                xhat = xf * r
                dg_ref[...] += jnp.sum(v * xhat, axis=0, keepdims=True)
                dxh = v * g_ref[...]
                tot = dres_ref[rows, :] + r * (dxh - xhat * jnp.mean(dxh * xhat, axis=-1, keepdims=True))
                dx_ref[rows, :] = tot
                dxb_ref[rows, :] = tot.astype(BF16)

        @pl.when(i == 0)
        def _():
            dg_ref[...] = jnp.zeros_like(dg_ref)
            matmul(acc0)

        @pl.when((i > 0) & (i < nblk) & (i % 2 == 1))
        def _():
            matmul(acc1)
            finish(acc0)

        @pl.when((i > 0) & (i < nblk) & (i % 2 == 0))
        def _():
            matmul(acc0)
            finish(acc1)

        @pl.when(i == nblk)
        def _():
            finish(acc1)

    last = nblk - 1
    row = pl.BlockSpec((tm, N), lambda i: (jnp.maximum(i - 1, 0), 0))
    vec = pl.BlockSpec((1, N), lambda i: (0, 0))
    if a.ndim == 3:
        a_spec = pl.BlockSpec((parts, tm, Kp), lambda i: (0, jnp.minimum(i, last), 0))
    else:
        a_spec = pl.BlockSpec((tm, Kp), lambda i: (jnp.minimum(i, last), 0))
    w_spec = pl.BlockSpec((parts * Kp, N), lambda i: (0, 0), pipeline_mode=pl.Buffered(1))
    in_specs = [a_spec, w_spec, row, vec, row]
    args = [a, w, x, gain, dres]
    if dep is not None:
        in_specs.append(HBM_SPEC)
        args.append(dep)
    return pl.pallas_call(
        body, name=name, grid=(nblk + 1,), in_specs=in_specs, out_specs=[row, row, vec],
        out_shape=[_sds((M, N), F32), _sds((M, N), BF16), _sds((1, N), F32)],
        scratch_shapes=[pltpu.VMEM((tm, N), F32), pltpu.VMEM((tm, N), F32)], compiler_params=_cparams(1, 56))(*args)


def _mm_tn(a, b, name, out_dtype=BF16, tt=1024):
    parts = a.shape[0] if a.ndim == 3 else 1
    Tt, Mp = a.shape[-2], a.shape[-1]
    N = b.shape[1]
    tn = _pick(Mp, (1408, 1280, 1024, 512))
    jper = Mp // tn
    nt = Tt // tt

    def body(a_ref, b_ref, o_ref, acc):
        t = pl.program_id(1)

        @pl.when(t == 0)
        def _():
            acc[...] = jnp.zeros_like(acc)

        acc[...] += lax.dot_general(a_ref[...], b_ref[...], (((0,), (0,)), ((), ())),
                                    preferred_element_type=F32)

        @pl.when(t == nt - 1)
        def _():
            o_ref[...] = acc[...].astype(o_ref.dtype)

    if a.ndim == 3:
        a_spec = pl.BlockSpec((None, tt, tn), lambda j, t: (j // jper, t, j % jper))
    else:
        a_spec = pl.BlockSpec((tt, tn), lambda j, t: (t, j))
    return pl.pallas_call(
        body, name=name, grid=(parts * jper, nt),
        in_specs=[a_spec, pl.BlockSpec((tt, N), lambda j, t: (t, 0))],
        out_specs=pl.BlockSpec((tn, N), lambda j, t: (j, 0)),
        out_shape=_sds((parts * Mp, N), out_dtype), scratch_shapes=[pltpu.VMEM((tn, N), F32)],
        compiler_params=_cparams(2))(a, b)


def _ffn_in(h, wt_in, name, tm=2048, tn=256):
    nj = DFF // tn

    def body(h_ref, wg_ref, wu_ref, p_ref, act_ref):
        nt = (((1,), (1,)), ((), ()))
        g = lax.dot_general(h_ref[...], wg_ref[...], nt, preferred_element_type=F32).astype(BF16)
        u = lax.dot_general(h_ref[...], wu_ref[...], nt, preferred_element_type=F32).astype(BF16)
        p_ref[0] = g
        p_ref[1] = u
        act_ref[...] = g * _sigmoid_bf16(g) * u

    return pl.pallas_call(
        body, name=name, grid=(T // tm, nj),
        in_specs=[pl.BlockSpec((tm, D), lambda i, j: (i, 0)), pl.BlockSpec((tn, D), lambda i, j: (j, 0)),
                  pl.BlockSpec((tn, D), lambda i, j: (j + nj, 0))],
        out_specs=[pl.BlockSpec((2, tm, tn), lambda i, j: (0, i, j)), pl.BlockSpec((tm, tn), lambda i, j: (i, j))],
        out_shape=[_sds((2, T, DFF), BF16), _sds((T, DFF), BF16)], compiler_params=_cparams(2))(h, wt_in, wt_in)


def _ffn_dact(dyb, w_down, p3, name, tm=2048, tn=256, dep=None):
    def body(dy_ref, w_ref, p_ref, *rest):
        o_ref = rest[-1]
        da = lax.dot_general(dy_ref[...], w_ref[...], (((1,), (1,)), ((), ())),
                             preferred_element_type=F32).astype(BF16)
        g = p_ref[0]
        u = p_ref[1]
        sg = _sigmoid_bf16(g)
        gs = g * sg
        o_ref[0] = (da * u) * (sg + gs * (1.0 - sg))
        o_ref[1] = da * gs

    pspec = pl.BlockSpec((2, tm, tn), lambda i, j: (0, i, j))
    in_specs = [pl.BlockSpec((tm, D), lambda i, j: (i, 0)), pl.BlockSpec((tn, D), lambda i, j: (j, 0)), pspec]
    args = [dyb, w_down, p3]
    if dep is not None:
        in_specs.append(HBM_SPEC)
        args.append(dep)
    return pl.pallas_call(
        body, name=name, grid=(T // tm, DFF // tn), in_specs=in_specs, out_specs=pspec,
        out_shape=_sds((2, T, DFF), BF16), compiler_params=_cparams(2))(*args)


def _rms_fwd(x, g, name, tm=512, dep=None):
    def body(x_ref, g_ref, *rest):
        h_ref = rest[-1]
        xf = x_ref[...]
        r = lax.rsqrt(jnp.mean(xf * xf, axis=-1, keepdims=True) + EPS)
        h_ref[...] = (xf * r * g_ref[...]).astype(BF16)

    in_specs = [pl.BlockSpec((tm, D), lambda i: (i, 0)), pl.BlockSpec((1, D), lambda i: (0, 0))]
    args = [x, g]
    if dep is not None:
        in_specs.append(HBM_SPEC)
        args.append(dep)
    return pl.pallas_call(
        body, name=name, grid=(T // tm,), in_specs=in_specs, out_specs=pl.BlockSpec((tm, D), lambda i: (i, 0)),
        out_shape=_sds((T, D), BF16), compiler_params=_cparams(1))(*args)


def _tril_mask():
    r = lax.broadcasted_iota(jnp.int32, (128, 128), 0)
    c = lax.broadcasted_iota(jnp.int32, (128, 128), 1)
    return r >= c


def _mix_a_fwd(pab, sgu_g, sgu_b, sgu_w, sgu_bias3, tm=512):
    def body(zu_ref, zv_ref, g_ref, b_ref, w_ref, bias_ref, o_ref):
        u = _gelu(zu_ref[...].astype(F32))
        v = _gelu(zv_ref[...].astype(F32))
        mu = jnp.mean(v, axis=-1, keepdims=True)
        vc = v - mu
        rstd = lax.rsqrt(jnp.mean(vc * vc, axis=-1, keepdims=True) + EPS)
        vn = (vc * rstd * g_ref[...] + b_ref[...]).astype(BF16)
        tri = _tril_mask()
        for gi in range(4):
            wg = jnp.where(tri, w_ref[gi], 0.0).astype(BF16)
            bg = bias_ref[gi]
            for c in range(tm // 128):
                rs, cs = slice(c * 128, (c + 1) * 128), slice(gi * 128, (gi + 1) * 128)
                mixed = jnp.dot(wg, vn[rs, cs], preferred_element_type=F32) + bg
                o_ref[rs, cs] = (u[rs, cs] * mixed).astype(BF16)

    half = pl.BlockSpec((tm, 512), lambda i: (i, 0))
    return pl.pallas_call(
        body, name="mix_a_fwd", grid=(T // tm,),
        in_specs=[half, pl.BlockSpec((tm, 512), lambda i: (i, 1)),
                  pl.BlockSpec((1, 512), lambda i: (0, 0)), pl.BlockSpec((1, 512), lambda i: (0, 0)),
                  pl.BlockSpec((4, 128, 128), lambda i: (0, 0, 0)), pl.BlockSpec((4, 128, 1), lambda i: (0, 0, 0))],
        out_specs=half, out_shape=_sds((T, D), BF16), compiler_params=_cparams(1),
    )(pab, pab, sgu_g, sgu_b, sgu_w, sgu_bias3)


def _rope_tables():
    pos = np.arange(T, dtype=np.float32)
    inv_freq = np.float32(ROPE_THETA) ** (-np.arange(ROPE_HALF, dtype=np.float32) * np.float32(2.0 / (2 * ROPE_HALF)))
    ang = (pos[:, None] * inv_freq[None, :]).astype(np.float32)
    cos, sin = np.cos(ang), np.sin(ang)
    z8 = np.zeros((T, ROPE_HALF), np.float32)
    rest = np.zeros((T, HEAD - 2 * ROPE_HALF), np.float32)
    c64 = np.concatenate([cos, cos, rest + 1.0], axis=1)
    s1 = np.concatenate([z8, sin, rest], axis=1)
    s2 = np.concatenate([-sin, z8, rest], axis=1)
    return tuple(jnp.asarray(np.tile(t, (1, 2)).astype(np.float32)) for t in (c64, s1, s2))


def _lo_mask(shape):
    return lax.broadcasted_iota(jnp.int32, shape, 1) < HEAD


def _seg_mean(x, lo):
    s_all = jnp.sum(x, axis=-1, keepdims=True)
    s_lo = jnp.sum(jnp.where(lo, x, 0.0), axis=-1, keepdims=True)
    return jnp.where(lo, s_lo, s_all - s_lo) * (1.0 / HEAD)


def _head_blocks():
    r = lax.broadcasted_iota(jnp.int32, (PAIR, PAIR), 0) < HEAD
    c = lax.broadcasted_iota(jnp.int32, (PAIR, PAIR), 1) < HEAD
    return jnp.where(r == c, 1.0, 0.0).astype(BF16)


def _seg_mean_mxu(x, blocks):
    return jnp.dot(x.astype(BF16), blocks, preferred_element_type=F32) * (1.0 / HEAD)


def _rope(n, c, s1, s2):
    return n * c + pltpu.roll(n, ROPE_HALF, 1) * s1 + pltpu.roll(n, PAIR - ROPE_HALF, 1) * s2


def _rope_t(dy, c, s1, s2):
    return dy * c - pltpu.roll(dy, PAIR - ROPE_HALF, 1) * s2 - pltpu.roll(dy, ROPE_HALF, 1) * s1


def _prep_fwd(pab, qg, kg, tabs, tm=512):
    def body(p_ref, qg_ref, kg_ref, c_ref, s1_ref, s2_ref, *outs):
        blocks = _head_blocks()
        c, s1, s2 = c_ref[...], s1_ref[...], s2_ref[...]
        for g in range(3):
            qn_ref, kn_ref, v_ref = outs[3 * g:3 * g + 3]
            for p in range(NPAIR):
                for which, gains, dst in ((0, qg_ref, qn_ref), (1, kg_ref, kn_ref)):
                    col = (2 + 3 * which + g) * 512 + p * PAIR
                    xr = p_ref[:, col:col + PAIR].astype(F32)
                    rinv = lax.rsqrt(_seg_mean_mxu(xr * xr, blocks) + EPS)
                    outs[9 + 2 * g + which][p] = rinv
                    dst[p] = _rope(xr * rinv * gains[g:g + 1, :], c, s1, s2)
                col = (8 + g) * 512 + p * PAIR
                v_ref[p] = p_ref[:, col:col + PAIR].astype(F32)

    pm = pl.BlockSpec((NPAIR, tm, PAIR), lambda i: (0, i, 0))
    tab = pl.BlockSpec((tm, PAIR), lambda i: (i, 0))
    gain = pl.BlockSpec((3, PAIR), lambda i: (0, 0))
    res = pl.pallas_call(
        body, name="prep_fwd", grid=(T // tm,),
        in_specs=[pl.BlockSpec((tm, AB_IN), lambda i: (i, 0)), gain, gain, tab, tab, tab],
        out_specs=[pm] * 15, out_shape=[_sds((NPAIR, T, PAIR), F32)] * 15,
        compiler_params=_cparams(1))(pab, qg, kg, *tabs)
    return res[0:9], res[9:15]


def _res_index(it, rate):
    window = NBACK * rate
    b = it // rate
    rho = it % rate
    start = b * window + rho
    startp = jnp.maximum(start - window, rho)
    kmin = jnp.where(b > 0, 0, NBACK)
    return start, startp, kmin


def _rows(start, rate):
    if rate == 1:
        return pl.ds(pl.multiple_of(start, NBACK), NBACK)
    return pl.ds(start, NBACK, stride=rate)


def _band_bias():
    qs = lax.broadcasted_iota(jnp.int32, (2 * NBACK, 2 * NBACK), 0)
    kj = lax.broadcasted_iota(jnp.int32, (2 * NBACK, 2 * NBACK), 1)
    dist = (qs & (NBACK - 1)) + NBACK - kj
    both = (dist >= 0) & (dist <= NBACK)
    return jnp.where(both, 0.0, NEG_INF), jnp.where(both & (kj >= NBACK), 0.0, NEG_INF)


def _attn_fwd(qn, kn, v, rate, name, dep=None):
    def body(q_ref, k_ref, v_ref, *rest):
        o_ref, l_ref = rest[-2:]
        lo = _lo_mask((NBACK, PAIR))
        bias_all, bias_first = _band_bias()

        def step(it, carry):
            start, startp, kmin = _res_index(it, rate)
            q = q_ref[_rows(start, rate), :] * (HEAD ** -0.5)
            kcat = jnp.concatenate([k_ref[_rows(startp, rate), :], k_ref[_rows(start, rate), :]], axis=0).astype(BF16)
            vcat = jnp.concatenate([v_ref[_rows(startp, rate), :], v_ref[_rows(start, rate), :]], axis=0).astype(BF16)
            vcat1 = jnp.concatenate([vcat, jnp.ones((2 * NBACK, PAIR), BF16)], axis=1)
            q2 = jnp.concatenate([jnp.where(lo, q, 0.0), jnp.where(lo, 0.0, q)], axis=0).astype(BF16)
            s = lax.dot_general(q2, kcat, (((1,), (1,)), ((), ())), preferred_element_type=F32)
            s = s + jnp.where(kmin == 0, bias_all, bias_first)
            m = jnp.max(s, axis=-1, keepdims=True)
            ol = jnp.dot(jnp.exp(s - m).astype(BF16), vcat1, preferred_element_type=F32)
            o2 = ol[:, 0:PAIR] / ol[:, PAIR:]
            ls = m + jnp.log(ol[:, PAIR:])
            o_ref[_rows(start, rate), :] = jnp.where(lo, o2[0:NBACK], o2[NBACK:])
            l_ref[_rows(start, rate), :] = jnp.where(lo, ls[0:NBACK], ls[NBACK:])
            return carry

        lax.fori_loop(0, T // NBACK, step, 0, unroll=4)

    pm = pl.BlockSpec((None, T, PAIR), lambda p: (p, 0, 0))
    in_specs, args = [pm, pm, pm], [qn, kn, v]
    if dep is not None:
        in_specs.append(HBM_SPEC)
        args.append(dep)
    return pl.pallas_call(
        body, name=name, grid=(NPAIR,), in_specs=in_specs, out_specs=[pm, pm],
        out_shape=[_sds((NPAIR, T, PAIR), F32)] * 2, compiler_params=_cparams(1))(*args)


def _merge_fwd(cat_ab, outs, lses, tm=512):
    def body(cat_in, o0, o1, o2, l0, l1, l2, cat_ref, lse_ref):
        del cat_in
        for p in range(NPAIR):
            a0, a1, a2 = l0[p], l1[p], l2[p]
            m = jnp.maximum(jnp.maximum(a0, a1), a2)
            w0, w1, w2 = jnp.exp(a0 - m), jnp.exp(a1 - m), jnp.exp(a2 - m)
            s = w0 + w1 + w2
            b = (w0 * o0[p] + w1 * o1[p] + w2 * o2[p]) / s
            cat_ref[:, p * PAIR:(p + 1) * PAIR] = b.astype(BF16)
            lse_ref[p] = m + jnp.log(s)

    pm = pl.BlockSpec((NPAIR, tm, PAIR), lambda i: (0, i, 0))
    return pl.pallas_call(
        body, name="merge_fwd", grid=(T // tm,),
        in_specs=[pl.BlockSpec(memory_space=pl.ANY)] + [pm] * 6,
        out_specs=[pl.BlockSpec((tm, 512), lambda i: (i, 1)), pm],
        out_shape=[_sds((T, D), BF16), _sds((NPAIR, T, PAIR), F32)],
        input_output_aliases={0: 0}, compiler_params=_cparams(1))(cat_ab, *outs, *lses)


def _b_pre_bwd(dcat, cat, tm=512):
    def body(db_ref, b_ref, dbp_ref, e_ref):
        lo = _lo_mask((tm, PAIR))
        for p in range(NPAIR):
            db = db_ref[:, p * PAIR:(p + 1) * PAIR].astype(F32)
            b = b_ref[:, p * PAIR:(p + 1) * PAIR].astype(F32)
            dbp_ref[p] = db
            e_ref[p] = _seg_mean(db * b, lo) * float(HEAD)

    pm = pl.BlockSpec((NPAIR, tm, PAIR), lambda i: (0, i, 0))
    right = pl.BlockSpec((tm, 512), lambda i: (i, 1))
    return pl.pallas_call(
        body, name="b_pre_bwd", grid=(T // tm,), in_specs=[right, right], out_specs=[pm, pm],
        out_shape=[_sds((NPAIR, T, PAIR), F32)] * 2, compiler_params=_cparams(1))(dcat, cat)


def _attn_bwd(qn, kn, v, dbp, e, lse, rate, name):
    def body(q_ref, k_ref, v_ref, db_ref, e_ref, lse_ref, dq_ref, dk_ref, dv_ref):
        lo = _lo_mask((NBACK, PAIR))
        bias_all, bias_first = _band_bias()
        scale = HEAD ** -0.5
        nt = (((1,), (1,)), ((), ()))
        tn = (((0,), (0,)), ((), ()))
        window = NBACK * rate
        nblk = T // window

        def one(it, carry):
            dk_carry, dv_carry = carry
            rho = it // nblk
            b = it % nblk
            start = b * window + rho
            rq = _rows(start, rate)
            rp = _rows(jnp.maximum(start - window, rho), rate)
            q = q_ref[rq, :] * scale
            db = db_ref[rq, :]
            ev = e_ref[rq, :]
            ls = lse_ref[rq, :]
            kcat = jnp.concatenate([k_ref[rp, :], k_ref[rq, :]], axis=0).astype(BF16)
            vcat = jnp.concatenate([v_ref[rp, :], v_ref[rq, :]], axis=0).astype(BF16)
            q2 = jnp.concatenate([jnp.where(lo, q, 0.0), jnp.where(lo, 0.0, q)], axis=0).astype(BF16)
            db2 = jnp.concatenate([jnp.where(lo, db, 0.0), jnp.where(lo, 0.0, db)], axis=0).astype(BF16)
            ls2 = jnp.concatenate([ls[:, 0:1], ls[:, HEAD:HEAD + 1]], axis=0)
            ev2 = jnp.concatenate([ev[:, 0:1], ev[:, HEAD:HEAD + 1]], axis=0)
            s = lax.dot_general(q2, kcat, nt, preferred_element_type=F32)
            pt = jnp.exp(s + jnp.where(b > 0, bias_all, bias_first) - ls2)
            dp = lax.dot_general(db2, vcat, nt, preferred_element_type=F32)
            ds = (pt * (dp - ev2)).astype(BF16)
            dq2 = jnp.dot(ds, kcat, preferred_element_type=F32) * scale
            dkc = lax.dot_general(ds, q2, tn, preferred_element_type=F32)
            dvc = lax.dot_general(pt.astype(BF16), db2, tn, preferred_element_type=F32)
            dq_ref[rq, :] = jnp.where(lo, dq2[0:NBACK], dq2[NBACK:])
            dk_ref[rp, :] = dk_carry + dkc[0:NBACK]
            dk_ref[rq, :] = dkc[NBACK:]
            dv_ref[rp, :] = dv_carry + dvc[0:NBACK]
            dv_ref[rq, :] = dvc[NBACK:]
            return dkc[NBACK:], dvc[NBACK:]

        def step(i, carry):
            for u in range(ATTN_BWD_UNROLL):
                carry = one(i * ATTN_BWD_UNROLL + u, carry)
            return carry

        zero = jnp.zeros((NBACK, PAIR), F32)
        lax.fori_loop(0, T // NBACK // ATTN_BWD_UNROLL, step, (zero, zero))

    pm = pl.BlockSpec((None, T, PAIR), lambda p: (p, 0, 0))
    return pl.pallas_call(
        body, name=name, grid=(NPAIR,), in_specs=[pm] * 6, out_specs=[pm] * 3,
        out_shape=[_sds((NPAIR, T, PAIR), F32)] * 3, compiler_params=_cparams(1, 56))(qn, kn, v, dbp, e, lse)


def _ab_in_bwd(pab, dcat, sgu_g, sgu_b, sgu_w, sgu_bias3, qg, kg, tabs, dqkv, rinvs, tm=256):
    def body(p_ref, dcat_ref, g_ref, b_ref, w_ref, bias_ref, qg_ref, kg_ref, c_ref, s1_ref, s2_ref, *rest):
        dq_refs, rinv_refs = rest[0:9], rest[9:15]
        o_ref, dwm_ref, dbias_ref, dsg_ref, dsb_ref, dgain_ref = rest[15:]
        i = pl.program_id(0)

        @pl.when(i == 0)
        def _():
            dwm_ref[...] = jnp.zeros_like(dwm_ref)
            dbias_ref[...] = jnp.zeros_like(dbias_ref)
            dsg_ref[...] = jnp.zeros_like(dsg_ref)
            dsb_ref[...] = jnp.zeros_like(dsb_ref)
            dgain_ref[...] = jnp.zeros_like(dgain_ref)

        zu = p_ref[:, 0:512].astype(F32)
        zv = p_ref[:, 512:1024].astype(F32)
        u = _gelu(zu)
        v = _gelu(zv)
        mu = jnp.mean(v, axis=-1, keepdims=True)
        vc = v - mu
        rstd = lax.rsqrt(jnp.mean(vc * vc, axis=-1, keepdims=True) + EPS)
        xhat = vc * rstd
        vn = (xhat * g_ref[...] + b_ref[...]).astype(BF16)
        da = dcat_ref[...].astype(F32)
        tri = _tril_mask()
        du_parts = [[None] * 4 for _ in range(tm // 128)]
        dvn_parts = [[None] * 4 for _ in range(tm // 128)]
        for gi in range(4):
            wg = jnp.where(tri, w_ref[gi], 0.0).astype(BF16)
            bg = bias_ref[gi]
            for c in range(tm // 128):
                rs, cs = slice(c * 128, (c + 1) * 128), slice(gi * 128, (gi + 1) * 128)
                vblk = vn[rs, cs]
                mixed = jnp.dot(wg, vblk, preferred_element_type=F32) + bg
                dab = da[rs, cs]
                du_parts[c][gi] = dab * mixed
                dmixed = dab * u[rs, cs]
                dmb = dmixed.astype(BF16)
                dvn_parts[c][gi] = lax.dot_general(wg, dmb, (((0,), (0,)), ((), ())), preferred_element_type=F32)
                dwm = lax.dot_general(dmb, vblk, (((1,), (1,)), ((), ())), preferred_element_type=F32)
                dwm_ref[gi] += jnp.where(tri, dwm, 0.0)
                dbias_ref[gi] += dmixed
        du = jnp.concatenate([jnp.concatenate(r, axis=1) for r in du_parts], axis=0)
        dvn = jnp.concatenate([jnp.concatenate(r, axis=1) for r in dvn_parts], axis=0)
        dsg_ref[...] += jnp.sum(dvn * xhat, axis=0, keepdims=True)
        dsb_ref[...] += jnp.sum(dvn, axis=0, keepdims=True)
        dxh = dvn * g_ref[...]
        dv = rstd * (dxh - jnp.mean(dxh, axis=-1, keepdims=True)
                     - xhat * jnp.mean(dxh * xhat, axis=-1, keepdims=True))
        o_ref[:, 0:512] = (du * _gelu_grad(zu)).astype(BF16)
        o_ref[:, 512:1024] = (dv * _gelu_grad(zv)).astype(BF16)

        blocks = _head_blocks()
        c, s1, s2 = c_ref[...], s1_ref[...], s2_ref[...]
        for g in range(3):
            dq_ref, dk_ref, dv_ref = dq_refs[3 * g:3 * g + 3]
            for p in range(NPAIR):
                for which, gains, src in ((0, qg_ref, dq_ref), (1, kg_ref, dk_ref)):
                    col = (2 + 3 * which + g) * 512 + p * PAIR
                    xr = p_ref[:, col:col + PAIR].astype(F32)
                    rinv = rinv_refs[2 * g + which][p]
                    xh = xr * rinv
                    dn = _rope_t(src[p], c, s1, s2)
                    row = 2 * g + which
                    dgain_ref[row:row + 1, :] += jnp.sum(dn * xh, axis=0, keepdims=True)
                    dxh2 = dn * gains[g:g + 1, :]
                    dx = rinv * (dxh2 - xh * _seg_mean_mxu(dxh2 * xh, blocks))
                    o_ref[:, col:col + PAIR] = dx.astype(BF16)
                col = (8 + g) * 512 + p * PAIR
                o_ref[:, col:col + PAIR] = dv_ref[p].astype(BF16)

    pm = pl.BlockSpec((NPAIR, tm, PAIR), lambda i: (0, i, 0))
    tab = pl.BlockSpec((tm, PAIR), lambda i: (i, 0))
    gain = pl.BlockSpec((3, PAIR), lambda i: (0, 0))
    vec = pl.BlockSpec((1, 512), lambda i: (0, 0))
    full = pl.BlockSpec((tm, AB_IN), lambda i: (i, 0))
    w4 = pl.BlockSpec((4, 128, 128), lambda i: (0, 0, 0))
    return pl.pallas_call(
        body, name="ab_in_bwd", grid=(T // tm,),
        in_specs=[full, pl.BlockSpec((tm, 512), lambda i: (i, 0)), vec, vec, w4,
                  pl.BlockSpec((4, 128, 1), lambda i: (0, 0, 0)), gain, gain, tab, tab, tab] + [pm] * 15,
        out_specs=[full, w4, w4, vec, vec, pl.BlockSpec((8, PAIR), lambda i: (0, 0))],
        out_shape=[_sds((T, AB_IN), BF16), _sds((4, 128, 128), F32), _sds((4, 128, 128), F32),
                   _sds((1, 512), F32), _sds((1, 512), F32), _sds((8, PAIR), F32)],
        compiler_params=_cparams(1))(pab, dcat, sgu_g, sgu_b, sgu_w, sgu_bias3, qg, kg, *tabs, *dqkv, *rinvs)


def _ln_stats(x):
    mu = jnp.mean(x, axis=-1, keepdims=True)
    xc = x - mu
    rstd = lax.rsqrt(jnp.mean(xc * xc, axis=-1, keepdims=True) + EPS)
    return xc * rstd, rstd


CONV_RC = 64


def _shifted_copies(src, dst, tm):
    dst[0] = src[...]
    for b in range(1, 8):
        dst[b, 0:tm + HALO - 8, :] = src[pl.ds(b, tm + HALO - 8), :]


def _offsets_by_phase(first):
    groups = {}
    for o in range(first, first + CONV_C_TAPS):
        groups.setdefault(o % 8, []).append(o)
    return sorted(groups.items())


def _window(shifted, b8, base, offsets, lanes):
    rows = 8 * (max(offsets) // 8) + CONV_RC
    return shifted[b8, pl.ds(base, rows), lanes].reshape(rows // 8, 8, 128)


def _cd_fwd(pcd, cw, cb, lg, lb, dw, tm=512):
    per = tm // HALO

    def body(p_ref, h_ref, cw_ref, cb_ref, lg_ref, lb_ref, dw_ref, cat_ref, c0_ref, c1_ref, dd_ref, y_ref,
             buf, buf2, sb):
        i = pl.program_id(0)
        live = jnp.where(i > 0, 1.0, 0.0)
        a = p_ref[:, 0:512].astype(F32)
        gt = p_ref[:, 512:1024].astype(F32)
        gb = p_ref[:, 1024:1536].astype(F32)
        gc = p_ref[:, 1536:2048].astype(F32)
        hv = p_ref[:, 2048:2560].astype(F32)
        c0 = a * _sigmoid(gt)
        dd = gc * hv
        buf[0:HALO, :] = h_ref[:, 0:512].astype(F32) * _sigmoid(h_ref[:, 512:1024].astype(F32)) * live
        buf[HALO:, :] = c0
        buf2[0:HALO, :] = h_ref[:, 1536:2048].astype(F32) * h_ref[:, 2048:2560].astype(F32) * live
        buf2[HALO:, :] = dd
        c0_ref[...] = c0.astype(BF16)
        dd_ref[...] = dd.astype(BF16)
        _shifted_copies(buf, sb, tm)

        def conv_rows(r, carry):
            base = pl.multiple_of(r * CONV_RC, CONV_RC)
            for c in range(4):
                lanes = slice(c * 128, (c + 1) * 128)
                acc = jnp.broadcast_to(cb_ref[:, lanes], (CONV_RC // 8, 8, 128))
                for b8, offsets in _offsets_by_phase(HALO - (CONV_C_TAPS - 1)):
                    win = _window(sb, b8, base, offsets, lanes)
                    for o in offsets:
                        j = o - (HALO - (CONV_C_TAPS - 1))
                        acc = acc + cw_ref[8 * j:8 * j + 8, lanes] * win[o // 8:o // 8 + CONV_RC // 8]
                c1_ref[pl.ds(base, CONV_RC), lanes] = acc.reshape(CONV_RC, 128)
            return carry

        lax.fori_loop(0, tm // CONV_RC, conv_rows, 0)
        xhat, _ = _ln_stats(c1_ref[...])
        c2 = xhat * lg_ref[...] + lb_ref[...]
        y = jnp.zeros((tm, 512), F32)
        for j in range(CONV_D_TAPS):
            y = y + dw_ref[j:j + 1, :] * buf2[pl.ds(HALO - (CONV_D_TAPS - 1) + j, tm), :]
        cat_ref[:, 0:512] = (c2 * _sigmoid(c2)).astype(BF16)
        cat_ref[:, 512:1024] = (gb * y).astype(BF16)
        y_ref[...] = y.astype(BF16)

    half = pl.BlockSpec((tm, 512), lambda i: (i, 0))
    vec = pl.BlockSpec((1, 512), lambda i: (0, 0))
    return pl.pallas_call(
        body, name="cd_fwd", grid=(T // tm,),
        in_specs=[pl.BlockSpec((tm, CD_IN), lambda i: (i, 0)),
                  pl.BlockSpec((HALO, CD_IN), lambda i: (jnp.maximum(i * per - 1, 0), 0)),
                  pl.BlockSpec((8 * 32, 512), lambda i: (0, 0)), vec, vec, vec, pl.BlockSpec((8, 512), lambda i: (0, 0))],
        out_specs=[pl.BlockSpec((tm, D), lambda i: (i, 0)), half, half, half, half],
        out_shape=[_sds((T, D), BF16), _sds((T, 512), BF16), _sds((T, 512), F32), _sds((T, 512), BF16),
                   _sds((T, 512), BF16)],
        scratch_shapes=[pltpu.VMEM((HALO + tm, 512), F32), pltpu.VMEM((HALO + tm, 512), F32),
                        pltpu.VMEM((8, HALO + tm, 512), F32)],
        compiler_params=_cparams(1))(pcd, pcd, cw, cb, lg, lb, dw)


def _cd_bwd_pw(dcat, c1, pcd, y, lg, lb, tm=512):
    def body(dcat_ref, c1_ref, gb_ref, y_ref, lg_ref, lb_ref, dc1_ref, dy3_ref, dgb_ref, dlg_ref, dlb_ref, dcb_ref):
        i = pl.program_id(0)

        @pl.when(i == 0)
        def _():
            dlg_ref[...] = jnp.zeros_like(dlg_ref)
            dlb_ref[...] = jnp.zeros_like(dlb_ref)
            dcb_ref[...] = jnp.zeros_like(dcb_ref)

        dc = dcat_ref[:, 0:512].astype(F32)
        ddo = dcat_ref[:, 512:1024].astype(F32)
        xhat, rstd = _ln_stats(c1_ref[...])
        c2 = xhat * lg_ref[...] + lb_ref[...]
        sg = _sigmoid(c2)
        dc2 = dc * sg * (1.0 + c2 * (1.0 - sg))
        dlg_ref[...] += jnp.sum(dc2 * xhat, axis=0, keepdims=True)
        dlb_ref[...] += jnp.sum(dc2, axis=0, keepdims=True)
        dxh = dc2 * lg_ref[...]
        dc1 = rstd * (dxh - jnp.mean(dxh, axis=-1, keepdims=True)
                      - xhat * jnp.mean(dxh * xhat, axis=-1, keepdims=True))
        dcb_ref[...] += jnp.sum(dc1, axis=0, keepdims=True)
        dc1_ref[...] = dc1
        dgb_ref[...] = (ddo * y_ref[...].astype(F32)).astype(BF16)
        dy3_ref[...] = ddo * gb_ref[...].astype(F32)

    half = pl.BlockSpec((tm, 512), lambda i: (i, 0))
    vec = pl.BlockSpec((1, 512), lambda i: (0, 0))
    return pl.pallas_call(
        body, name="cd_bwd_pw", grid=(T // tm,),
        in_specs=[pl.BlockSpec((tm, D), lambda i: (i, 0)), half, pl.BlockSpec((tm, 512), lambda i: (i, 2)), half,
                  vec, vec],
        out_specs=[half, half, half, vec, vec, vec],
        out_shape=[_sds((T, 512), F32), _sds((T, 512), F32), _sds((T, 512), BF16),
                   _sds((1, 512), F32), _sds((1, 512), F32), _sds((1, 512), F32)],
        compiler_params=_cparams(1))(dcat, c1, pcd, y, lg, lb)


def _cd_bwd_conv(pcd, dc1, dy3, c0, dd, dgb, cw8, dw, tm=256):
    per = tm // HALO
    nblk = T // tm
    last32 = T // HALO - 1

    def body(p_ref, dc1_ref, dc1n_ref, dy3_ref, dy3n_ref, c0_ref, dd_ref, dgb_ref, cw_ref, dw_ref,
             o_ref, dcw_ref, ddw_ref, dbuf, d3buf, sd, dc0_buf):
        i = pl.program_id(0)
        has_next = jnp.where(i < nblk - 1, 1.0, 0.0)

        @pl.when(i == 0)
        def _():
            dcw_ref[...] = jnp.zeros_like(dcw_ref)
            ddw_ref[...] = jnp.zeros_like(ddw_ref)

        dbuf[0:tm, :] = dc1_ref[...]
        dbuf[tm:, :] = dc1n_ref[...] * has_next
        d3buf[0:tm, :] = dy3_ref[...]
        d3buf[tm:, :] = dy3n_ref[...] * has_next
        _shifted_copies(dbuf, sd, tm)
        n_tiles = tm // CONV_RC

        phases = _offsets_by_phase(0)

        def dc0_rows(r, carry):
            base = pl.multiple_of(r * CONV_RC, CONV_RC)
            for c in range(4):
                lanes = slice(c * 128, (c + 1) * 128)
                acc = jnp.zeros((CONV_RC // 8, 8, 128), F32)
                for b8, offsets in phases:
                    win = _window(sd, b8, base, offsets, lanes)
                    for o in offsets:
                        j = CONV_C_TAPS - 1 - o
                        acc = acc + cw_ref[8 * j:8 * j + 8, lanes] * win[o // 8:o // 8 + CONV_RC // 8]
                dc0_buf[pl.ds(base, CONV_RC), lanes] = acc.reshape(CONV_RC, 128)
            return carry

        lax.fori_loop(0, n_tiles, dc0_rows, 0)

        for c in range(4):
            lanes = slice(c * 128, (c + 1) * 128)
            for b8, offsets in phases:
                def dw_rows(r, accs, lanes=lanes, b8=b8, offsets=offsets):
                    base = pl.multiple_of(r * CONV_RC, CONV_RC)
                    xin = c0_ref[pl.ds(base, CONV_RC), lanes].astype(F32).reshape(CONV_RC // 8, 8, 128)
                    win = _window(sd, b8, base, offsets, lanes)
                    return tuple(acc + jnp.sum(xin * win[o // 8:o // 8 + CONV_RC // 8], axis=0)
                                 for acc, o in zip(accs, offsets))

                accs = lax.fori_loop(0, n_tiles, dw_rows, tuple(jnp.zeros((8, 128), F32) for _ in offsets))
                for acc, o in zip(accs, offsets):
                    j = CONV_C_TAPS - 1 - o
                    dcw_ref[j:j + 1, lanes] += jnp.sum(acc, axis=0, keepdims=True)

        dc0 = dc0_buf[...]
        ddin = dd_ref[...].astype(F32)
        ddd = jnp.zeros((tm, 512), F32)
        for j in range(CONV_D_TAPS):
            dy_shift = d3buf[pl.ds(CONV_D_TAPS - 1 - j, tm), :]
            ddd = ddd + dw_ref[j:j + 1, :] * dy_shift
            ddw_ref[j:j + 1, :] += jnp.sum(ddin * dy_shift, axis=0, keepdims=True)

        a = p_ref[:, 0:512].astype(F32)
        gt = p_ref[:, 512:1024].astype(F32)
        gc = p_ref[:, 1536:2048].astype(F32)
        hv = p_ref[:, 2048:2560].astype(F32)
        sg = _sigmoid(gt)
        o_ref[:, 0:512] = (dc0 * sg).astype(BF16)
        o_ref[:, 512:1024] = (dc0 * a * sg * (1.0 - sg)).astype(BF16)
        o_ref[:, 1024:1536] = dgb_ref[...]
        o_ref[:, 1536:2048] = (ddd * hv).astype(BF16)
        o_ref[:, 2048:2560] = (ddd * gc).astype(BF16)

    half = pl.BlockSpec((tm, 512), lambda i: (i, 0))
    nxt = pl.BlockSpec((HALO, 512), lambda i: (jnp.minimum((i + 1) * per, last32), 0))
    full = pl.BlockSpec((tm, CD_IN), lambda i: (i, 0))
    return pl.pallas_call(
        body, name="cd_bwd_conv", grid=(nblk,),
        in_specs=[full, half, nxt, half, nxt, half, half, half,
                  pl.BlockSpec((8 * 32, 512), lambda i: (0, 0)), pl.BlockSpec((8, 512), lambda i: (0, 0))],
        out_specs=[full, pl.BlockSpec((32, 512), lambda i: (0, 0)), pl.BlockSpec((8, 512), lambda i: (0, 0))],
        out_shape=[_sds((T, CD_IN), BF16), _sds((32, 512), F32), _sds((8, 512), F32)],
        scratch_shapes=[pltpu.VMEM((tm + HALO, 512), F32), pltpu.VMEM((tm + HALO, 512), F32),
                        pltpu.VMEM((8, tm + HALO, 512), F32), pltpu.VMEM((tm, 512), F32)],
        compiler_params=_cparams(1))(pcd, dc1, dc1, dy3, dy3, c0, dd, dgb, cw8, dw)


def _local_step(x, tgt, W, fetch=None, on_grad=None):
    W = dict(W)
    if fetch is None:
        fetch = lambda stage, after: {}
    if on_grad is None:
        on_grad = lambda key, arr: None
    tabs = _rope_tables()
    qg = jnp.tile(W["q_norm_g"], (1, 2))
    kg = jnp.tile(W["k_norm_g"], (1, 2))
    bias3 = W["sgu_bias"].reshape(4, 128, 1)
    G = {}

    h0 = _rms_fwd(x, W["ab_norm_g"], "rms_fwd_ab", dep=W.get("dep_first"))
    W.update(fetch("ab_in", h0))
    pab = _mm_nt(h0, W["wt_ab_in"], "mm_ab_in", dep=W.get("dep0"))
    cat_ab = _mix_a_fwd(pab, W["sgu_norm_g"], W["sgu_norm_b"], W["sgu_w"], bias3)
    qkv, rinvs = _prep_fwd(pab, qg, kg, tabs)
    outs, lses = [], []
    for g, rate in enumerate(DIL_RATES):
        o, l = _attn_fwd(qkv[3 * g], qkv[3 * g + 1], qkv[3 * g + 2], rate, f"attn_fwd_{g}", dep=W.get(f"dep_attn{g}"))
        outs.append(o)
        lses.append(l)
        W.update(fetch(f"attn{g}", o))
    cat_ab, lse = _merge_fwd(cat_ab, outs, lses)
    W.update(fetch("ab_out", lse))
    x1, h1 = _mm_nn(cat_ab, W["w_ab_out"], "mm_ab_out", mode="rms", resid=x, gain=W["ffn_norm_g"][0:1])
    pf0, act0 = _ffn_in(h1, W["wt_ffn_in0"], "ffn_in0")
    W.update(fetch("ffn_down0", act0))
    x2, h2 = _mm_nn(act0, W["w_ffn_down0"], "mm_ffn_down0", mode="rms", resid=x1, gain=W["cd_norm_g"],
                    dep=W.get("dep_down0"))
    W.update(fetch("cd_in", h2))
    pcd = _mm_nt(h2, W["wt_cd_in"], "mm_cd_in")
    cw8 = jnp.repeat(W["conv_c_w32"], 8, axis=0)
    cat_cd, c0, c1, dd, yv = _cd_fwd(pcd, cw8, W["conv_c_b"], W["c_ln_g"], W["c_ln_b"], W["conv_d_w8"])
    x3, h3 = _mm_nn(cat_cd, W["w_cd_out"], "mm_cd_out", mode="rms", resid=x2, gain=W["ffn_norm_g"][1:2])
    pf1, act1 = _ffn_in(h3, W["wt_ffn_in1"], "ffn_in1")
    dy, dyb, loss_cols = _mm_nn(act1, W["w_ffn_down1"], "mm_ffn_down1", mode="loss", resid=x3, tgt=tgt)

    def ffn_bwd(xin, h, pf, act, dres, dresb, layer):
        G[f"w_ffn_down{layer}"] = _mm_tn(act, dresb, f"mm_g_ffn_down{layer}")
        dep = on_grad(f"w_ffn_down{layer}", G[f"w_ffn_down{layer}"])
        dpf = _ffn_dact(dresb, W[f"w_ffn_down{layer}"], pf, f"ffn_dact{layer}", dep=dep)
        G[f"wt_ffn_in{layer}"] = _mm_tn(dpf, h, f"mm_g_ffn_in{layer}")
        dep = on_grad(f"wt_ffn_in{layer}", G[f"wt_ffn_in{layer}"])
        dx, dxb, G[f"ffn_norm_g{layer}"] = _mm_dh_rms_bwd(
            dpf, W[f"wt_ffn_in{layer}"], xin, W["ffn_norm_g"][layer:layer + 1], dres, f"mm_d_h_ffn{layer}", dep=dep)
        return dx, dxb

    dx3, dx3b = ffn_bwd(x3, h3, pf1, act1, dy, dyb, 1)

    G["w_cd_out"] = _mm_tn(cat_cd, dx3b, "mm_g_cd_out")
    dep = on_grad("w_cd_out", G["w_cd_out"])
    dcat_cd = _mm_nt(dx3b, W["w_cd_out"], "mm_d_cat_cd", dep=dep)
    dc1, dy3, dgb, G["c_ln_g"], G["c_ln_b"], G["conv_c_b"] = _cd_bwd_pw(dcat_cd, c1, pcd, yv, W["c_ln_g"], W["c_ln_b"])
    dpcd, G["conv_c_w32"], G["conv_d_w8"] = _cd_bwd_conv(pcd, dc1, dy3, c0, dd, dgb, cw8, W["conv_d_w8"])
    G["wt_cd_in"] = _mm_tn(dpcd, h2, "mm_g_cd_in")
    dep = on_grad("wt_cd_in", G["wt_cd_in"])
    dx2, dx2b, G["cd_norm_g"] = _mm_dh_rms_bwd(dpcd, W["wt_cd_in"], x2, W["cd_norm_g"], dx3, "mm_d_h_cd", dep=dep)

    dx1, dx1b = ffn_bwd(x1, h1, pf0, act0, dx2, dx2b, 0)

    G["w_ab_out"] = _mm_tn(cat_ab, dx1b, "mm_g_ab_out")
    dep = on_grad("w_ab_out", G["w_ab_out"])
    dcat_ab = _mm_nt(dx1b, W["w_ab_out"], "mm_d_cat_ab", dep=dep)
    dbp, e = _b_pre_bwd(dcat_ab, cat_ab)
    dqkv = []
    for g, rate in enumerate(DIL_RATES):
        dqkv += _attn_bwd(qkv[3 * g], qkv[3 * g + 1], qkv[3 * g + 2], dbp, e, lse, rate, f"attn_bwd_{g}")
    dpab, G["sgu_w"], dbias_part, G["sgu_norm_g"], G["sgu_norm_b"], dgain = _ab_in_bwd(
        pab, dcat_ab, W["sgu_norm_g"], W["sgu_norm_b"], W["sgu_w"], bias3, qg, kg, tabs, dqkv, rinvs)
    G["sgu_bias"] = jnp.sum(dbias_part, axis=-1)
    dgain = dgain[0:6, 0:HEAD] + dgain[0:6, HEAD:PAIR]
    G["q_norm_g"] = dgain[0::2]
    G["k_norm_g"] = dgain[1::2]
    G["wt_ab_in"] = _mm_tn(dpab, h0, "mm_g_ab_in")
    dep = on_grad("wt_ab_in", G["wt_ab_in"])
    grad_x, _, G["ab_norm_g"] = _mm_dh_rms_bwd(dpab, W["wt_ab_in"], x, W["ab_norm_g"], dx1, "mm_d_h_ab", dep=dep)
    return loss_cols, grad_x, G


def _my_place():
    return lax.axis_index("x"), lax.axis_index("y"), lax.axis_index("c")


def _dev_index(px, py, pc):
    return 4 * px + 2 * py + pc


def _flip(place, k):
    x, y, c = place
    return (1 - x if k & 4 else x, 1 - y if k & 2 else y, 1 - c if k & 1 else c)


def _landing(shape, dtype, own):
    buf = lax.empty(shape, dtype)
    for lead, part in own:
        buf = lax.dynamic_update_slice(buf, part.reshape((1,) * len(lead) + part.shape),
                                       tuple(lead) + (0,) * part.ndim)
    return buf


HBM_ONLY = pl.BlockSpec(memory_space=pltpu.HBM)
SEM_SPEC = pl.BlockSpec(memory_space=pltpu.SEMAPHORE)
IN_FLIGHT = pltpu.CompilerParams(has_side_effects=pltpu.SideEffectType.DATAFLOW_SIDE_EFFECTING)


def _in_hbm(a):
    return pltpu.with_memory_space_constraint(a, pltpu.HBM)


def _exchange_start(name, srcs, lands, items, dep=None):
    ns, nl, ni = len(srcs), len(lands), len(items)

    def body(*refs):
        S, L = refs[0:ns], refs[ns:ns + nl]
        first_out = ns + nl + (0 if dep is None else 1)
        send_sems, recv_sems, token = refs[first_out], refs[first_out + 1], refs[-1]
        me = _my_place()
        mi = _dev_index(*me)
        for i, (src, dst) in enumerate(items):
            for k in range(1, NDEV):
                peer = _flip(me, k)
                pltpu.make_async_remote_copy(
                    src_ref=src(S, _dev_index(*peer)), dst_ref=dst(L, mi), send_sem=send_sems.at[7 * i + k - 1],
                    recv_sem=recv_sems.at[7 * i + k - 1], device_id=peer, device_id_type=MESH).start()
        token[...] = jnp.zeros_like(token)

    thru = [pltpu.HBM(a.shape, a.dtype) for a in list(srcs) + list(lands)]
    args = [_in_hbm(a) for a in srcs] + [_in_hbm(a) for a in lands]
    in_specs = [HBM_ONLY] * (ns + nl)
    if dep is not None:
        args.append(dep)
        in_specs.append(HBM_SPEC)
    outs = pl.pallas_call(
        body, name=name, in_specs=in_specs,
        out_shape=(pltpu.SemaphoreType.DMA((7 * ni,)), pltpu.SemaphoreType.DMA((7 * ni,)), *thru, _sds((8, 128), F32)),
        out_specs=(SEM_SPEC, SEM_SPEC, *[HBM_ONLY] * (ns + nl), pl.BlockSpec(memory_space=pltpu.VMEM)),
        input_output_aliases={j: 2 + j for j in range(ns + nl)}, compiler_params=IN_FLIGHT)(*args)
    return dict(send=outs[0], recv=outs[1], srcs=list(outs[2:2 + ns]), lands=list(outs[2 + ns:2 + ns + nl]),
                token=outs[-1], items=items)


def _exchange_wait(name, states, after):
    after = list(after) if isinstance(after, (list, tuple)) else [after]
    counts = [(len(st["srcs"]), len(st["lands"]), len(st["items"])) for st in states]
    n_arrays = sum(c[0] + c[1] for c in counts)

    def body(*refs):
        me = _my_place()
        mi = _dev_index(*me)
        pos = 0
        sem_pos = n_arrays
        for st, (ns, nl, ni) in zip(states, counts):
            S, L = refs[pos:pos + ns], refs[pos + ns:pos + ns + nl]
            send_sems, recv_sems = refs[sem_pos], refs[sem_pos + 1]
            pos += ns + nl
            sem_pos += 2
            for i, (src, dst) in enumerate(st["items"]):
                for k in range(1, NDEV):
                    cp = pltpu.make_async_remote_copy(
                        src_ref=src(S, mi), dst_ref=dst(L, mi), send_sem=send_sems.at[7 * i + k - 1],
                        recv_sem=recv_sems.at[7 * i + k - 1], device_id=me, device_id_type=MESH)
                    cp.wait_send()
                    cp.wait_recv()

    arrays, sems = [], []
    for st in states:
        arrays += st["srcs"] + st["lands"]
        sems += [st["send"], st["recv"]]
    outs = pl.pallas_call(
        body, name=name, in_specs=[HBM_ONLY] * n_arrays + [SEM_SPEC] * len(sems) + [HBM_SPEC] * len(after),
        out_shape=tuple(pltpu.HBM(a.shape, a.dtype) for a in arrays), out_specs=tuple([HBM_ONLY] * n_arrays),
        input_output_aliases={j: j for j in range(n_arrays)}, compiler_params=IN_FLIGHT)(*arrays, *sems, *after)
    lands, pos = [], 0
    for ns, nl, _ in counts:
        lands.append(list(outs[pos + ns:pos + ns + nl]))
        pos += ns + nl
    return lands


def _place_and_neighbours():
    x, y, c = _my_place()
    return (x, y, c), (x, y, 1 - c), [(1 - x, y), (x, 1 - y), (1 - x, 1 - y)]


def _gather_start(name, srcs, lands, items, dep=None):
    ns, nl, ni = len(srcs), len(lands), len(items)

    def body(*refs):
        S, L = refs[0:ns], refs[ns:ns + nl]
        first_out = ns + nl + (0 if dep is None else 1)
        send_sems, recv_sems, token = refs[first_out], refs[first_out + 1], refs[-1]
        me, sib, chips = _place_and_neighbours()
        mi = _dev_index(*me)
        for i, (src, dst) in enumerate(items):
            for k, to in enumerate([sib] + [(*chip, me[2]) for chip in chips]):
                pltpu.make_async_remote_copy(
                    src_ref=src(S), dst_ref=dst(L, mi), send_sem=send_sems.at[4 * i + k],
                    recv_sem=recv_sems.at[4 * i + k], device_id=to, device_id_type=MESH).start()
        token[...] = jnp.zeros_like(token)

    thru = [pltpu.HBM(a.shape, a.dtype) for a in list(srcs) + list(lands)]
    args = [_in_hbm(a) for a in srcs] + [_in_hbm(a) for a in lands]
    in_specs = [HBM_ONLY] * (ns + nl)
    if dep is not None:
        args.append(dep)
        in_specs.append(HBM_SPEC)
    outs = pl.pallas_call(
        body, name=name, in_specs=in_specs,
        out_shape=(pltpu.SemaphoreType.DMA((4 * ni,)), pltpu.SemaphoreType.DMA((4 * ni,)), *thru, _sds((8, 128), F32)),
        out_specs=(SEM_SPEC, SEM_SPEC, *[HBM_ONLY] * (ns + nl), pl.BlockSpec(memory_space=pltpu.VMEM)),
        input_output_aliases={j: 2 + j for j in range(ns + nl)}, compiler_params=IN_FLIGHT)(*args)
    return dict(send=outs[0], recv=outs[1], srcs=list(outs[2:2 + ns]), lands=list(outs[2 + ns:2 + ns + nl]),
                token=outs[-1], items=items)


def _gather_forward(name, st, after):
    nl, ni = len(st["lands"]), len(st["items"])

    def body(*refs):
        L, recv_sems = refs[0:nl], refs[nl]
        fwd_send, fwd_recv, token = refs[2 * nl + 2], refs[2 * nl + 3], refs[-1]
        me, sib, chips = _place_and_neighbours()
        for i, (_, dst) in enumerate(st["items"]):
            for j, chip in enumerate(chips):
                blk = dst(L, _dev_index(*chip, me[2]))
                pltpu.make_async_remote_copy(
                    src_ref=blk, dst_ref=blk, send_sem=fwd_send.at[3 * i + j], recv_sem=recv_sems.at[4 * i + 1 + j],
                    device_id=me, device_id_type=MESH).wait_recv()
                pltpu.make_async_remote_copy(
                    src_ref=blk, dst_ref=blk, send_sem=fwd_send.at[3 * i + j], recv_sem=fwd_recv.at[3 * i + j],
                    device_id=sib, device_id_type=MESH).start()
        token[...] = jnp.zeros_like(token)

    outs = pl.pallas_call(
        body, name=name, in_specs=[HBM_ONLY] * nl + [SEM_SPEC, HBM_SPEC],
        out_shape=(*[pltpu.HBM(a.shape, a.dtype) for a in st["lands"]], pltpu.SemaphoreType.DMA((3 * ni,)),
                   pltpu.SemaphoreType.DMA((3 * ni,)), _sds((8, 128), F32)),
        out_specs=(*[HBM_ONLY] * nl, SEM_SPEC, SEM_SPEC, pl.BlockSpec(memory_space=pltpu.VMEM)),
        input_output_aliases={j: j for j in range(nl)}, compiler_params=IN_FLIGHT)(*st["lands"], st["recv"], after)
    return dict(st, lands=list(outs[0:nl]), fwd_send=outs[nl], fwd_recv=outs[nl + 1], token=outs[-1])


def _gather_wait(name, st, after):
    ns, nl, ni = len(st["srcs"]), len(st["lands"]), len(st["items"])

    def body(*refs):
        S, L = refs[0:ns], refs[ns:ns + nl]
        send_sems, recv_sems, fwd_send, fwd_recv = refs[ns + nl:ns + nl + 4]
        me, sib, chips = _place_and_neighbours()
        mi = _dev_index(*me)
        for i, (src, dst) in enumerate(st["items"]):
            mine = dst(L, mi)
            for k in range(4):
                pltpu.make_async_remote_copy(
                    src_ref=src(S), dst_ref=mine, send_sem=send_sems.at[4 * i + k], recv_sem=recv_sems.at[4 * i + k],
                    device_id=me, device_id_type=MESH).wait_send()
            pltpu.make_async_remote_copy(
                src_ref=src(S), dst_ref=mine, send_sem=send_sems.at[4 * i], recv_sem=recv_sems.at[4 * i],
                device_id=me, device_id_type=MESH).wait_recv()
            for j in range(3):
                cp = pltpu.make_async_remote_copy(
                    src_ref=mine, dst_ref=mine, send_sem=fwd_send.at[3 * i + j], recv_sem=fwd_recv.at[3 * i + j],
                    device_id=me, device_id_type=MESH)
                cp.wait_send()
                cp.wait_recv()

    arrays = st["srcs"] + st["lands"]
    outs = pl.pallas_call(
        body, name=name, in_specs=[HBM_ONLY] * (ns + nl) + [SEM_SPEC] * 4 + [HBM_SPEC],
        out_shape=tuple(pltpu.HBM(a.shape, a.dtype) for a in arrays), out_specs=tuple([HBM_ONLY] * (ns + nl)),
        input_output_aliases={j: j for j in range(ns + nl)},
        compiler_params=IN_FLIGHT)(*arrays, st["send"], st["recv"], st["fwd_send"], st["fwd_recv"], after)
    return list(outs[ns:ns + nl])


def _sum_slots(land):
    def body(l_ref, o_ref):
        acc = l_ref[0]
        for d in range(1, NDEV):
            acc = acc + l_ref[d]
        o_ref[...] = acc

    vm = pl.BlockSpec(memory_space=pltpu.VMEM)
    return pl.pallas_call(body, name="sum_small", out_shape=_sds(land.shape[1:], F32), in_specs=[vm], out_specs=vm)(land)


def _adam_math(w, g, m, v):
    m2 = ADAM_B1 * m + (1.0 - ADAM_B1) * g
    v2 = ADAM_B2 * v + (1.0 - ADAM_B2) * (g * g)
    delta = -ADAM_LR * ((m2 * ADAM_C1) / (jnp.sqrt(v2 * ADAM_C2) + ADAM_EPS) + ADAM_WD * w)
    return delta, m2, v2


def _adam_layer(land, sel, w, m, v, layer, name, prev=None, tc=512):
    R = land.shape[2]

    def body(l_ref, w_ref, m_ref, v_ref, *rest):
        g_out, d_out, m_out, v_out = rest[-4:]
        g = l_ref[0].astype(F32)
        for d in range(1, NDEV):
            g = g + l_ref[d].astype(F32)
        delta, m2, v2 = _adam_math(w_ref[...], g, m_ref[...], v_ref[...])
        g_out[...] = g
        d_out[...] = delta
        m_out[...] = m2
        v_out[...] = v2

    wspec = pl.BlockSpec((None, R, tc), lambda i: (layer, 0, i))
    in_specs = [pl.BlockSpec((None, NDEV, R, tc), lambda i: (sel, 0, 0, i)), wspec, wspec, wspec]
    args = [land, w, m, v]
    aliases = {}
    if prev is not None:
        in_specs += [HBM_SPEC] * 4
        args += list(prev)
        aliases = {4 + j: j for j in range(4)}
    return pl.pallas_call(
        body, name=name, grid=(D // tc,), in_specs=in_specs, out_specs=[wspec] * 4,
        out_shape=[_sds(w.shape, F32)] * 4, input_output_aliases=aliases, compiler_params=_cparams(1))(*args)


def _adam_stacked(lands, sel, w, m, v, name):
    res = None
    for layer, land in enumerate(lands):
        res = _adam_layer(land, sel, w, m, v, layer, f"{name}{layer}", prev=res)
    return res


def _adam_small(ws, gs, ms, vs):
    n = len(ws)

    def body(*refs):
        w_r, g_r, m_r, v_r = refs[0:n], refs[n:2 * n], refs[2 * n:3 * n], refs[3 * n:4 * n]
        d_o, m_o, v_o = refs[4 * n:5 * n], refs[5 * n:6 * n], refs[6 * n:7 * n]
        for i in range(n):
            delta, m2, v2 = _adam_math(w_r[i][...], g_r[i][...], m_r[i][...], v_r[i][...])
            d_o[i][...] = delta
            m_o[i][...] = m2
            v_o[i][...] = v2

    vm = pl.BlockSpec(memory_space=pltpu.VMEM)
    shapes = [_sds(w.shape, F32) for w in ws]
    outs = pl.pallas_call(body, name="adam_small", in_specs=[vm] * (4 * n), out_specs=[vm] * (3 * n),
                          out_shape=shapes * 3)(*ws, *gs, *ms, *vs)
    return outs[0:n], outs[n:2 * n], outs[2 * n:3 * n]


WEIGHT_NAMES = ("ab_norm_g", "ab_w_in", "sgu_norm_g", "sgu_norm_b", "sgu_w", "sgu_bias", "q_norm_g", "k_norm_g",
                "ab_w_out", "cd_norm_g", "cd_w_in", "conv_c_w", "conv_c_b", "c_ln_g", "c_ln_b", "conv_d_w",
                "cd_w_out", "ffn_norm_g", "ffn_w_gate", "ffn_w_up", "ffn_w_down")
SMALL_2D = (("ab_norm_g", (1, 1024)), ("sgu_norm_g", (1, 512)), ("sgu_norm_b", (1, 512)), ("sgu_w", (512, 128)),
            ("sgu_bias", (4, 128)), ("q_norm_g", (3, 64)), ("k_norm_g", (3, 64)), ("cd_norm_g", (1, 128)),
            ("conv_c_w", (31, 64)), ("conv_c_b", (1, 64)), ("c_ln_g", (1, 64)), ("c_ln_b", (1, 64)),
            ("conv_d_w", (3, 64)), ("ffn_norm_g", (2, 1024)))
SHARD_C = 64


def _pack_rows(parts, rows):
    flat = jnp.concatenate([p.reshape(-1) for p in parts])
    return jnp.pad(flat, (0, rows * 128 - flat.shape[0])).reshape(rows, 128)


def kernel(x, ab_norm_g, ab_w_in, sgu_norm_g, sgu_norm_b, sgu_w, sgu_bias, q_norm_g, k_norm_g, ab_w_out, cd_norm_g, cd_w_in, conv_c_w, conv_c_b, c_ln_g, c_ln_b, conv_d_w, cd_w_out, ffn_norm_g, ffn_w_gate, ffn_w_up, ffn_w_down, loss_target, m_ab_norm_g, m_ab_w_in, m_sgu_norm_g, m_sgu_norm_b, m_sgu_w, m_sgu_bias, m_q_norm_g, m_k_norm_g, m_ab_w_out, m_cd_norm_g, m_cd_w_in, m_conv_c_w, m_conv_c_b, m_c_ln_g, m_c_ln_b, m_conv_d_w, m_cd_w_out, m_ffn_norm_g, m_ffn_w_gate, m_ffn_w_up, m_ffn_w_down, v_ab_norm_g, v_ab_w_in, v_sgu_norm_g, v_sgu_norm_b, v_sgu_w, v_sgu_bias, v_q_norm_g, v_k_norm_g, v_ab_w_out, v_cd_norm_g, v_cd_w_in, v_conv_c_w, v_conv_c_b, v_c_ln_g, v_c_ln_b, v_conv_d_w, v_cd_w_out, v_ffn_norm_g, v_ffn_w_gate, v_ffn_w_up, v_ffn_w_down):
    w = dict(zip(WEIGHT_NAMES, (ab_norm_g, ab_w_in, sgu_norm_g, sgu_norm_b, sgu_w, sgu_bias, q_norm_g, k_norm_g, ab_w_out, cd_norm_g, cd_w_in, conv_c_w, conv_c_b, c_ln_g, c_ln_b, conv_d_w, cd_w_out, ffn_norm_g, ffn_w_gate, ffn_w_up, ffn_w_down)))
    m = dict(zip(WEIGHT_NAMES, (m_ab_norm_g, m_ab_w_in, m_sgu_norm_g, m_sgu_norm_b, m_sgu_w, m_sgu_bias, m_q_norm_g, m_k_norm_g, m_ab_w_out, m_cd_norm_g, m_cd_w_in, m_conv_c_w, m_conv_c_b, m_c_ln_g, m_c_ln_b, m_conv_d_w, m_cd_w_out, m_ffn_norm_g, m_ffn_w_gate, m_ffn_w_up, m_ffn_w_down)))
    v = dict(zip(WEIGHT_NAMES, (v_ab_norm_g, v_ab_w_in, v_sgu_norm_g, v_sgu_norm_b, v_sgu_w, v_sgu_bias, v_q_norm_g, v_k_norm_g, v_ab_w_out, v_cd_norm_g, v_cd_w_in, v_conv_c_w, v_conv_c_b, v_c_ln_g, v_c_ln_b, v_conv_d_w, v_cd_w_out, v_ffn_norm_g, v_ffn_w_gate, v_ffn_w_up, v_ffn_w_down)))
    me = _dev_index(*_my_place())

    small_local = _pack_rows([w["cd_norm_g"], w["conv_c_w"], w["conv_c_b"], w["c_ln_g"], w["c_ln_b"], w["conv_d_w"]], 24)
    r_ff = DFF // NDEV
    one = lambda a: (lambda S, j: S[a])
    slot = lambda b: (lambda L, s: L[b].at[s])
    slot2 = lambda b, part: (lambda L, s: L[b].at[part, s])
    shard = lambda a: (lambda S: S[a])

    def layer_shards(layer):
        return (w["ffn_w_gate"][layer].T.astype(BF16), w["ffn_w_up"][layer].T.astype(BF16),
                w["ffn_w_down"][layer].astype(BF16))

    def gathered(own):
        return _landing((NDEV,) + own.shape, BF16, [((me,), own)])

    def gathered2(a, b):
        return _landing((2, NDEV) + a.shape, BF16, [((0, me), a), ((1, me), b)])

    ab_in_s = w["ab_w_in"][0].T.astype(BF16)
    gathers = {0: _gather_start(
        "gather0_start", [ab_in_s, small_local],
        [gathered(ab_in_s), _landing((NDEV,) + small_local.shape, F32, [((me,), small_local)])],
        [(shard(0), slot(0)), (shard(1), slot(1))])}

    def chan(flat, lo, taps):
        return flat[:, lo:lo + taps * SHARD_C].reshape(NDEV, taps, SHARD_C).transpose(1, 0, 2).reshape(taps, 512)

    def fetch(stage, after):
        if stage == "ab_in":
            gathers[0] = _gather_forward("gather0_forward", gathers[0], after)
            l_ab_in, l_small = _gather_wait("gather0_wait", gathers[0], gathers[0]["token"])
            ab_out_s = w["ab_w_out"][0].astype(BF16)
            gate0, up0, down0 = layer_shards(0)
            gathers[1] = _gather_start(
                "gather1_start", [ab_out_s, gate0, up0, down0],
                [gathered(ab_out_s), gathered2(gate0, up0), gathered(down0)],
                [(shard(0), slot(0)), (shard(1), slot2(1, 0)), (shard(2), slot2(1, 1)), (shard(3), slot(2))],
                dep=l_small)
            flat = l_small.reshape(NDEV, 24 * 128)
            return {
                "wt_ab_in": l_ab_in.reshape(AB_IN, D), "dep0": gathers[1]["token"],
                "cd_norm_g": flat[:, 0:128].reshape(1, D),
                "conv_c_w32": jnp.pad(chan(flat, 128, CONV_C_TAPS), ((0, 1), (0, 0))),
                "conv_c_b": chan(flat, 2112, 1), "c_ln_g": chan(flat, 2176, 1), "c_ln_b": chan(flat, 2240, 1),
                "conv_d_w8": jnp.pad(chan(flat, 2304, CONV_D_TAPS), ((0, 8 - CONV_D_TAPS), (0, 0))),
            }
        if stage == "attn0":
            cd_in_s, cd_out_s = w["cd_w_in"][0].T.astype(BF16), w["cd_w_out"][0].astype(BF16)
            gate1, up1, down1 = layer_shards(1)
            gathers[2] = _gather_start(
                "gather2_start", [cd_in_s, cd_out_s, gate1, up1, down1],
                [gathered(cd_in_s), gathered(cd_out_s), gathered2(gate1, up1), gathered(down1)],
                [(shard(0), slot(0)), (shard(1), slot(1)), (shard(2), slot2(2, 0)), (shard(3), slot2(2, 1)),
                 (shard(4), slot(3))], dep=after)
            return {"dep_attn1": gathers[2]["token"]}
        if stage == "attn1":
            gathers[1] = _gather_forward("gather1_forward", gathers[1], after)
            return {"dep_attn2": gathers[1]["token"]}
        if stage == "ab_out":
            l_out, l_ffn, l_down = _gather_wait("gather1_wait", gathers[1], after)
            return {"w_ab_out": l_out.reshape(D, D), "wt_ffn_in0": l_ffn.reshape(2 * DFF, D),
                    "w_ffn_down0": l_down.reshape(DFF, D)}
        if stage == "ffn_down0":
            gathers[2] = _gather_forward("gather2_forward", gathers[2], after)
            return {"dep_down0": gathers[2]["token"]}
        if stage == "cd_in":
            l_in, l_out, l_ffn, l_down = _gather_wait("gather2_wait", gathers[2], after)
            return {"wt_cd_in": l_in.reshape(CD_IN, D), "w_cd_out": l_out.reshape(D, D),
                    "wt_ffn_in1": l_ffn.reshape(2 * DFF, D), "w_ffn_down1": l_down.reshape(DFF, D)}
        return {}

    scatters = {}
    rides_with = {"w_ffn_down1": "wt_ffn_in1", "w_cd_out": "wt_cd_in", "w_ffn_down0": "wt_ffn_in0"}
    held = {}

    def on_grad(key, arr):
        if key in rides_with:
            held[rides_with[key]] = (key, arr)
            return None
        group = ([held.pop(key)] if key in held else []) + [(key, arr)]
        srcs, lands, items = [], [], []
        for n, (k, a) in enumerate(group):
            if k.startswith("wt_ffn_in"):
                src = a.reshape(2, NDEV, r_ff, D)
                own = lax.dynamic_slice_in_dim(src, me, 1, axis=1)
                lands.append(lax.dynamic_update_slice(lax.empty(src.shape, BF16), own, (0, me, 0, 0)))
                items += [((lambda S, j, n=n: S[n].at[0, j]), slot2(n, 0)), ((lambda S, j, n=n: S[n].at[1, j]), slot2(n, 1))]
            else:
                rows = a.shape[0] // NDEV
                src = a.reshape(NDEV, rows, D)
                own = lax.dynamic_index_in_dim(src, me, 0, keepdims=False)
                lands.append(_landing((1, NDEV, rows, D), BF16, [((0, me), own)]))
                items.append(((lambda S, j, n=n: S[n].at[j]), slot2(n, 0)))
            srcs.append(src)
        st = _exchange_start(f"scatter_{key}_start", srcs, lands, items)
        scatters[key] = (st, [k for k, _ in group])
        return st["token"]

    W = {
        "dep_first": gathers[0]["token"],
        "ab_norm_g": w["ab_norm_g"], "sgu_norm_g": w["sgu_norm_g"], "sgu_norm_b": w["sgu_norm_b"],
        "sgu_w": w["sgu_w"][0], "sgu_bias": w["sgu_bias"][0], "q_norm_g": w["q_norm_g"][0],
        "k_norm_g": w["k_norm_g"][0], "ffn_norm_g": w["ffn_norm_g"],
    }

    loss_cols, grad_x, G = _local_step(x[0], loss_target[0], W, fetch, on_grad)

    small_parts = [G["ab_norm_g"], G["sgu_norm_g"], G["sgu_norm_b"], G["sgu_w"], G["sgu_bias"], G["q_norm_g"],
                   G["k_norm_g"], G["cd_norm_g"], G["conv_c_w32"][:CONV_C_TAPS], G["conv_c_b"], G["c_ln_g"],
                   G["c_ln_b"], G["conv_d_w8"][:CONV_D_TAPS], G["ffn_norm_g0"], G["ffn_norm_g1"], loss_cols]
    sizes = [p.size for p in small_parts]
    small_rows = 720
    packed = _pack_rows(small_parts, small_rows)
    small = _exchange_start("small_start", [packed], [_landing((NDEV, small_rows, 128), F32, [((me,), packed)])],
                            [(one(0), slot(0))])
    landed = {}

    def wait_scatters(name, group_keys, after):
        res = _exchange_wait(name, [scatters[gk][0] for gk in group_keys], after)
        for gk, lands in zip(group_keys, res):
            landed.update(zip(scatters[gk][1], lands))

    wait_scatters("scatter_wait_early", ["wt_ffn_in1", "wt_cd_in", "wt_ffn_in0", "w_ab_out"], small["token"])

    grads, deltas, new_m, new_v = {}, {}, {}, {}
    done = []

    def put(name, res):
        grads[name], deltas[name], new_m[name], new_v[name] = res

    def adam(name, lands, sel, transposed):
        flip = (lambda a: jnp.swapaxes(a, 1, 2)) if transposed else (lambda a: a)
        res = _adam_stacked(lands, sel, flip(w[name]), flip(m[name]), flip(v[name]), f"adam_{name}")
        done.append(res[1])
        put(name, [flip(r) for r in res])

    ffn_in_lands = [landed["wt_ffn_in0"], landed["wt_ffn_in1"]]
    adam("cd_w_in", [landed["wt_cd_in"]], 0, True)
    adam("ffn_w_gate", ffn_in_lands, 0, True)
    adam("ffn_w_up", ffn_in_lands, 1, True)
    adam("cd_w_out", [landed["w_cd_out"]], 0, False)
    adam("ab_w_out", [landed["w_ab_out"]], 0, False)
    adam("ffn_w_down", [landed["w_ffn_down0"], landed["w_ffn_down1"]], 0, False)

    small_land = _exchange_wait("small_wait", [small], list(done))[0][0]
    red = _sum_slots(small_land).reshape(-1)
    offs = [0]
    for s in sizes:
        offs.append(offs[-1] + s)
    seg = [red[offs[i]:offs[i + 1]] for i in range(len(sizes))]
    loss = jnp.sum(seg[15])

    def own_channels(full, taps):
        return lax.dynamic_slice_in_dim(full.reshape(taps, 512), me * SHARD_C, SHARD_C, axis=1)

    g_small = {
        "ab_norm_g": seg[0].reshape(1, 1024), "sgu_norm_g": seg[1].reshape(1, 512), "sgu_norm_b": seg[2].reshape(1, 512),
        "sgu_w": seg[3].reshape(512, 128), "sgu_bias": seg[4].reshape(4, 128), "q_norm_g": seg[5].reshape(3, 64),
        "k_norm_g": seg[6].reshape(3, 64),
        "cd_norm_g": lax.dynamic_slice_in_dim(seg[7].reshape(1, D), me * (D // NDEV), D // NDEV, axis=1),
        "conv_c_w": own_channels(seg[8], CONV_C_TAPS), "conv_c_b": own_channels(seg[9], 1),
        "c_ln_g": own_channels(seg[10], 1), "c_ln_b": own_channels(seg[11], 1),
        "conv_d_w": own_channels(seg[12], CONV_D_TAPS),
        "ffn_norm_g": jnp.concatenate([seg[13].reshape(1, D), seg[14].reshape(1, D)], axis=0),
    }

    names2d = [n for n, _ in SMALL_2D]
    d_s, m_s, v_s = _adam_small([w[n].reshape(s) for n, s in SMALL_2D], [g_small[n] for n in names2d],
                                [m[n].reshape(s) for n, s in SMALL_2D], [v[n].reshape(s) for n, s in SMALL_2D])
    for i, n in enumerate(names2d):
        shape = w[n].shape
        grads[n], deltas[n] = g_small[n].reshape(shape), d_s[i].reshape(shape)
        new_m[n], new_v[n] = m_s[i].reshape(shape), v_s[i].reshape(shape)

    wait_scatters("scatter_wait_last", ["wt_ab_in"], d_s[0])
    adam("ab_w_in", [landed["wt_ab_in"]], 0, True)

    return (loss, grad_x[None], *[grads[n] for n in WEIGHT_NAMES], *[deltas[n] for n in WEIGHT_NAMES],
            *[new_m[n] for n in WEIGHT_NAMES], *[new_v[n] for n in WEIGHT_NAMES])
```

```python
import functools

import jax
import jax.numpy as jnp
import numpy as np
from jax import lax
from jax.experimental import pallas as pl
from jax.experimental.pallas import tpu as pltpu

F32 = jnp.float32
BF16 = jnp.bfloat16

T = 4096
D = 1024
NDEV = 8
EPS = 1e-6
NEG_INF = -1e30
DFF = 2816
AB_IN = 5632
CD_IN = 2560
HEAD = 64
PAIR = 128
NPAIR = 4
NBACK = 128
DIL_RATES = (1, 4, 16)
ROPE_HALF = 8
ROPE_THETA = 500000.0
CONV_C_TAPS = 31
CONV_D_TAPS = 3
HALO = 32
ATTN_BWD_UNROLL = 4

ADAM_LR = 0.001
ADAM_B1 = 0.9
ADAM_B2 = 0.999
ADAM_EPS = 1e-08
ADAM_WD = 0.01
ADAM_STEP = 10
ADAM_C1 = 1.0 / (1.0 - ADAM_B1 ** ADAM_STEP)
ADAM_C2 = 1.0 / (1.0 - ADAM_B2 ** ADAM_STEP)

VMEM_LIMIT_MB = 48
MESH = pl.DeviceIdType.MESH
HBM_SPEC = pl.BlockSpec(memory_space=pl.ANY)


def _cparams(ngrid, vmem_mb=VMEM_LIMIT_MB):
    return pltpu.CompilerParams(dimension_semantics=("arbitrary",) * ngrid,
                                vmem_limit_bytes=vmem_mb * 1024 * 1024)


def _pick(n, options):
    for o in options:
        if n % o == 0:
            return o
    raise ValueError(f"no tile for {n} in {options}")


def _sds(shape, dtype):
    return jax.ShapeDtypeStruct(shape, dtype)


def _sigmoid(x):
    return 1.0 / (1.0 + jnp.exp(-x))


def _sigmoid_bf16(x):
    return 0.5 * jnp.tanh(0.5 * x) + 0.5


def _gelu(z):
    return 0.5 * z * (1.0 + lax.erf(z * 0.7071067811865476))


def _gelu_grad(z):
    return 0.5 * (1.0 + lax.erf(z * 0.7071067811865476)) + z * jnp.exp(-0.5 * z * z) * 0.3989422804014327


def _mm_nt(a, wt, name, out_dtype=BF16, tm=2048, dep=None):
    M, K = a.shape
    N = wt.shape[0]
    tn = _pick(N, (512, 256))

    def body(a_ref, w_ref, *rest):
        o_ref = rest[-1]
        o_ref[...] = lax.dot_general(a_ref[...], w_ref[...], (((1,), (1,)), ((), ())),
                                     preferred_element_type=F32).astype(o_ref.dtype)

    in_specs = [pl.BlockSpec((tm, K), lambda i, j: (i, 0)), pl.BlockSpec((tn, K), lambda i, j: (j, 0))]
    args = [a, wt]
    if dep is not None:
        in_specs.append(HBM_SPEC)
        args.append(dep)
    return pl.pallas_call(
        body, name=name, grid=(M // tm, N // tn), in_specs=in_specs,
        out_specs=pl.BlockSpec((tm, tn), lambda i, j: (i, j)),
        out_shape=_sds((M, N), out_dtype), compiler_params=_cparams(2))(*args)


EPI_ROWS = 256


def _mm_nn(a, w, name, mode, resid, gain=None, tgt=None, dep=None, tm=512):
    M, K = a.shape
    N = w.shape[1]
    side = gain if mode == "rms" else tgt

    def body(a_ref, w_ref, resid_ref, side_ref, *rest):
        outs, acc = rest[-3 if mode == "rms" else -4:-1], rest[-1]
        i = pl.program_id(0)
        acc[...] = jnp.dot(a_ref[...], w_ref[...], preferred_element_type=F32)

        if mode == "loss":
            @pl.when(i == 0)
            def _():
                outs[2][...] = jnp.zeros_like(outs[2])

        for r0 in range(0, tm, EPI_ROWS):
            rows = slice(r0, r0 + EPI_ROWS)
            v = acc[rows, :] + resid_ref[rows, :]
            if mode == "rms":
                outs[0][rows, :] = v
                r = lax.rsqrt(jnp.mean(v * v, axis=-1, keepdims=True) + EPS)
                outs[1][rows, :] = (v * r * side_ref[...]).astype(BF16)
            else:
                d = v - side_ref[rows, :]
                outs[2][...] += jnp.sum(d * d, axis=0, keepdims=True) * (0.5 / N)
                dy = d * (1.0 / N)
                outs[0][rows, :] = dy
                outs[1][rows, :] = dy.astype(BF16)

    row = pl.BlockSpec((tm, N), lambda i: (i, 0))
    vec = pl.BlockSpec((1, N), lambda i: (0, 0))
    in_specs = [pl.BlockSpec((tm, K), lambda i: (i, 0)),
                pl.BlockSpec((K, N), lambda i: (0, 0), pipeline_mode=pl.Buffered(1)), row,
                vec if mode == "rms" else row]
    args = [a, w, resid, side]
    if dep is not None:
        in_specs.append(HBM_SPEC)
        args.append(dep)
    if mode == "rms":
        out_specs, out_shape = [row, row], [_sds((M, N), F32), _sds((M, N), BF16)]
    else:
        out_specs, out_shape = [row, row, vec], [_sds((M, N), F32), _sds((M, N), BF16), _sds((1, N), F32)]
    return pl.pallas_call(
        body, name=name, grid=(M // tm,), in_specs=in_specs, out_specs=out_specs, out_shape=out_shape,
        scratch_shapes=[pltpu.VMEM((tm, N), F32)], compiler_params=_cparams(1))(*args)


def _mm_dh_rms_bwd(a, w, x, gain, dres, name, dep=None, tm=512):
    parts = a.shape[0] if a.ndim == 3 else 1
    M, Kp = a.shape[-2], a.shape[-1]
    N = w.shape[1]
    nblk = M // tm
    assert nblk % 2 == 0

    def body(a_ref, w_ref, x_ref, g_ref, dres_ref, *rest):
        dx_ref, dxb_ref, dg_ref, acc0, acc1 = rest[-5:]
        i = pl.program_id(0)

        def matmul(acc):
            if parts == 1:
                acc[...] = jnp.dot(a_ref[...], w_ref[...], preferred_element_type=F32)
            else:
                d = jnp.dot(a_ref[0], w_ref[0:Kp, :], preferred_element_type=F32)
                for p in range(1, parts):
                    d = d + jnp.dot(a_ref[p], w_ref[p * Kp:(p + 1) * Kp, :], preferred_element_type=F32)
                acc[...] = d

        def finish(acc):
            for r0 in range(0, tm, EPI_ROWS // 2):
                rows = slice(r0, r0 + EPI_ROWS // 2)
                v = acc[rows, :]
                xf = x_ref[rows, :]
                r = lax.rsqrt(jnp.mean(xf * xf, axis=-1, keepdims=True) + EPS)
                xhat = xf * r
                dg_ref[...] += jnp.sum(v * xhat, axis=0, keepdims=True)
                dxh = v * g_ref[...]
                tot = dres_ref[rows, :] + r * (dxh - xhat * jnp.mean(dxh * xhat, axis=-1, keepdims=True))
                dx_ref[rows, :] = tot
                dxb_ref[rows, :] = tot.astype(BF16)

        @pl.when(i == 0)
        def _():
            dg_ref[...] = jnp.zeros_like(dg_ref)
            matmul(acc0)

        @pl.when((i > 0) & (i < nblk) & (i % 2 == 1))
        def _():
            matmul(acc1)
            finish(acc0)

        @pl.when((i > 0) & (i < nblk) & (i % 2 == 0))
        def _():
            matmul(acc0)
            finish(acc1)

        @pl.when(i == nblk)
        def _():
            finish(acc1)

    last = nblk - 1
    row = pl.BlockSpec((tm, N), lambda i: (jnp.maximum(i - 1, 0), 0))
    vec = pl.BlockSpec((1, N), lambda i: (0, 0))
    if a.ndim == 3:
        a_spec = pl.BlockSpec((parts, tm, Kp), lambda i: (0, jnp.minimum(i, last), 0))
    else:
        a_spec = pl.BlockSpec((tm, Kp), lambda i: (jnp.minimum(i, last), 0))
    w_spec = pl.BlockSpec((parts * Kp, N), lambda i: (0, 0), pipeline_mode=pl.Buffered(1))
    in_specs = [a_spec, w_spec, row, vec, row]
    args = [a, w, x, gain, dres]
    if dep is not None:
        in_specs.append(HBM_SPEC)
        args.append(dep)
    return pl.pallas_call(
        body, name=name, grid=(nblk + 1,), in_specs=in_specs, out_specs=[row, row, vec],
        out_shape=[_sds((M, N), F32), _sds((M, N), BF16), _sds((1, N), F32)],
        scratch_shapes=[pltpu.VMEM((tm, N), F32), pltpu.VMEM((tm, N), F32)], compiler_params=_cparams(1, 56))(*args)


def _mm_tn(a, b, name, out_dtype=BF16, tt=1024):
    parts = a.shape[0] if a.ndim == 3 else 1
    Tt, Mp = a.shape[-2], a.shape[-1]
    N = b.shape[1]
    tn = _pick(Mp, (1408, 1280, 1024, 512))
    jper = Mp // tn
    nt = Tt // tt

    def body(a_ref, b_ref, o_ref, acc):
        t = pl.program_id(1)

        @pl.when(t == 0)
        def _():
            acc[...] = jnp.zeros_like(acc)

        acc[...] += lax.dot_general(a_ref[...], b_ref[...], (((0,), (0,)), ((), ())),
                                    preferred_element_type=F32)

        @pl.when(t == nt - 1)
        def _():
            o_ref[...] = acc[...].astype(o_ref.dtype)

    if a.ndim == 3:
        a_spec = pl.BlockSpec((None, tt, tn), lambda j, t: (j // jper, t, j % jper))
    else:
        a_spec = pl.BlockSpec((tt, tn), lambda j, t: (t, j))
    return pl.pallas_call(
        body, name=name, grid=(parts * jper, nt),
        in_specs=[a_spec, pl.BlockSpec((tt, N), lambda j, t: (t, 0))],
        out_specs=pl.BlockSpec((tn, N), lambda j, t: (j, 0)),
        out_shape=_sds((parts * Mp, N), out_dtype), scratch_shapes=[pltpu.VMEM((tn, N), F32)],
        compiler_params=_cparams(2))(a, b)


def _ffn_in(h, wt_in, name, tm=2048, tn=256):
    nj = DFF // tn

    def body(h_ref, wg_ref, wu_ref, p_ref, act_ref):
        nt = (((1,), (1,)), ((), ()))
        g = lax.dot_general(h_ref[...], wg_ref[...], nt, preferred_element_type=F32).astype(BF16)
        u = lax.dot_general(h_ref[...], wu_ref[...], nt, preferred_element_type=F32).astype(BF16)
        p_ref[0] = g
        p_ref[1] = u
        act_ref[...] = g * _sigmoid_bf16(g) * u

    return pl.pallas_call(
        body, name=name, grid=(T // tm, nj),
        in_specs=[pl.BlockSpec((tm, D), lambda i, j: (i, 0)), pl.BlockSpec((tn, D), lambda i, j: (j, 0)),
                  pl.BlockSpec((tn, D), lambda i, j: (j + nj, 0))],
        out_specs=[pl.BlockSpec((2, tm, tn), lambda i, j: (0, i, j)), pl.BlockSpec((tm, tn), lambda i, j: (i, j))],
        out_shape=[_sds((2, T, DFF), BF16), _sds((T, DFF), BF16)], compiler_params=_cparams(2))(h, wt_in, wt_in)


def _ffn_dact(dyb, w_down, p3, name, tm=2048, tn=256, dep=None):
    def body(dy_ref, w_ref, p_ref, *rest):
        o_ref = rest[-1]
        da = lax.dot_general(dy_ref[...], w_ref[...], (((1,), (1,)), ((), ())),
                             preferred_element_type=F32).astype(BF16)
        g = p_ref[0]
        u = p_ref[1]
        sg = _sigmoid_bf16(g)
        gs = g * sg
        o_ref[0] = (da * u) * (sg + gs * (1.0 - sg))
        o_ref[1] = da * gs

    pspec = pl.BlockSpec((2, tm, tn), lambda i, j: (0, i, j))
    in_specs = [pl.BlockSpec((tm, D), lambda i, j: (i, 0)), pl.BlockSpec((tn, D), lambda i, j: (j, 0)), pspec]
    args = [dyb, w_down, p3]
    if dep is not None:
        in_specs.append(HBM_SPEC)
        args.append(dep)
    return pl.pallas_call(
        body, name=name, grid=(T // tm, DFF // tn), in_specs=in_specs, out_specs=pspec,
        out_shape=_sds((2, T, DFF), BF16), compiler_params=_cparams(2))(*args)


def _rms_fwd(x, g, name, tm=512, dep=None):
    def body(x_ref, g_ref, *rest):
        h_ref = rest[-1]
        xf = x_ref[...]
        r = lax.rsqrt(jnp.mean(xf * xf, axis=-1, keepdims=True) + EPS)
        h_ref[...] = (xf * r * g_ref[...]).astype(BF16)

    in_specs = [pl.BlockSpec((tm, D), lambda i: (i, 0)), pl.BlockSpec((1, D), lambda i: (0, 0))]
    args = [x, g]
    if dep is not None:
        in_specs.append(HBM_SPEC)
        args.append(dep)
    return pl.pallas_call(
        body, name=name, grid=(T // tm,), in_specs=in_specs, out_specs=pl.BlockSpec((tm, D), lambda i: (i, 0)),
        out_shape=_sds((T, D), BF16), compiler_params=_cparams(1))(*args)


def _tril_mask():
    r = lax.broadcasted_iota(jnp.int32, (128, 128), 0)
    c = lax.broadcasted_iota(jnp.int32, (128, 128), 1)
    return r >= c


def _mix_a_fwd(pab, sgu_g, sgu_b, sgu_w, sgu_bias3, tm=512):
    def body(zu_ref, zv_ref, g_ref, b_ref, w_ref, bias_ref, o_ref):
        u = _gelu(zu_ref[...].astype(F32))
        v = _gelu(zv_ref[...].astype(F32))
        mu = jnp.mean(v, axis=-1, keepdims=True)
        vc = v - mu
        rstd = lax.rsqrt(jnp.mean(vc * vc, axis=-1, keepdims=True) + EPS)
        vn = (vc * rstd * g_ref[...] + b_ref[...]).astype(BF16)
        tri = _tril_mask()
        for gi in range(4):
            wg = jnp.where(tri, w_ref[gi], 0.0).astype(BF16)
            bg = bias_ref[gi]
            for c in range(tm // 128):
                rs, cs = slice(c * 128, (c + 1) * 128), slice(gi * 128, (gi + 1) * 128)
                mixed = jnp.dot(wg, vn[rs, cs], preferred_element_type=F32) + bg
                o_ref[rs, cs] = (u[rs, cs] * mixed).astype(BF16)

    half = pl.BlockSpec((tm, 512), lambda i: (i, 0))
    return pl.pallas_call(
        body, name="mix_a_fwd", grid=(T // tm,),
        in_specs=[half, pl.BlockSpec((tm, 512), lambda i: (i, 1)),
                  pl.BlockSpec((1, 512), lambda i: (0, 0)), pl.BlockSpec((1, 512), lambda i: (0, 0)),
                  pl.BlockSpec((4, 128, 128), lambda i: (0, 0, 0)), pl.BlockSpec((4, 128, 1), lambda i: (0, 0, 0))],
        out_specs=half, out_shape=_sds((T, D), BF16), compiler_params=_cparams(1),
    )(pab, pab, sgu_g, sgu_b, sgu_w, sgu_bias3)


def _rope_tables():
    pos = np.arange(T, dtype=np.float32)
    inv_freq = np.float32(ROPE_THETA) ** (-np.arange(ROPE_HALF, dtype=np.float32) * np.float32(2.0 / (2 * ROPE_HALF)))
    ang = (pos[:, None] * inv_freq[None, :]).astype(np.float32)
    cos, sin = np.cos(ang), np.sin(ang)
    z8 = np.zeros((T, ROPE_HALF), np.float32)
    rest = np.zeros((T, HEAD - 2 * ROPE_HALF), np.float32)
    c64 = np.concatenate([cos, cos, rest + 1.0], axis=1)
    s1 = np.concatenate([z8, sin, rest], axis=1)
    s2 = np.concatenate([-sin, z8, rest], axis=1)
    return tuple(jnp.asarray(np.tile(t, (1, 2)).astype(np.float32)) for t in (c64, s1, s2))


def _lo_mask(shape):
    return lax.broadcasted_iota(jnp.int32, shape, 1) < HEAD


def _seg_mean(x, lo):
    s_all = jnp.sum(x, axis=-1, keepdims=True)
    s_lo = jnp.sum(jnp.where(lo, x, 0.0), axis=-1, keepdims=True)
    return jnp.where(lo, s_lo, s_all - s_lo) * (1.0 / HEAD)


def _head_blocks():
    r = lax.broadcasted_iota(jnp.int32, (PAIR, PAIR), 0) < HEAD
    c = lax.broadcasted_iota(jnp.int32, (PAIR, PAIR), 1) < HEAD
    return jnp.where(r == c, 1.0, 0.0).astype(BF16)


def _seg_mean_mxu(x, blocks):
    return jnp.dot(x.astype(BF16), blocks, preferred_element_type=F32) * (1.0 / HEAD)


def _rope(n, c, s1, s2):
    return n * c + pltpu.roll(n, ROPE_HALF, 1) * s1 + pltpu.roll(n, PAIR - ROPE_HALF, 1) * s2


def _rope_t(dy, c, s1, s2):
    return dy * c - pltpu.roll(dy, PAIR - ROPE_HALF, 1) * s2 - pltpu.roll(dy, ROPE_HALF, 1) * s1


def _prep_fwd(pab, qg, kg, tabs, tm=512):
    def body(p_ref, qg_ref, kg_ref, c_ref, s1_ref, s2_ref, *outs):
        blocks = _head_blocks()
        c, s1, s2 = c_ref[...], s1_ref[...], s2_ref[...]
        for g in range(3):
            qn_ref, kn_ref, v_ref = outs[3 * g:3 * g + 3]
            for p in range(NPAIR):
                for which, gains, dst in ((0, qg_ref, qn_ref), (1, kg_ref, kn_ref)):
                    col = (2 + 3 * which + g) * 512 + p * PAIR
                    xr = p_ref[:, col:col + PAIR].astype(F32)
                    rinv = lax.rsqrt(_seg_mean_mxu(xr * xr, blocks) + EPS)
                    outs[9 + 2 * g + which][p] = rinv.astype(BF16)
                    dst[p] = _rope(xr * rinv * gains[g:g + 1, :], c, s1, s2)
                col = (8 + g) * 512 + p * PAIR
                v_ref[p] = p_ref[:, col:col + PAIR].astype(F32)

    pm = pl.BlockSpec((NPAIR, tm, PAIR), lambda i: (0, i, 0))
    tab = pl.BlockSpec((tm, PAIR), lambda i: (i, 0))
    gain = pl.BlockSpec((3, PAIR), lambda i: (0, 0))
    res = pl.pallas_call(
        body, name="prep_fwd", grid=(T // tm,),
        in_specs=[pl.BlockSpec((tm, AB_IN), lambda i: (i, 0)), gain, gain, tab, tab, tab],
        out_specs=[pm] * 15, out_shape=[_sds((NPAIR, T, PAIR), F32)] * 9 + [_sds((NPAIR, T, PAIR), BF16)] * 6,
        compiler_params=_cparams(1))(pab, qg, kg, *tabs)
    return res[0:9], res[9:15]


def _res_index(it, rate):
    window = NBACK * rate
    b = it // rate
    rho = it % rate
    start = b * window + rho
    startp = jnp.maximum(start - window, rho)
    kmin = jnp.where(b > 0, 0, NBACK)
    return start, startp, kmin


def _rows(start, rate):
    if rate == 1:
        return pl.ds(pl.multiple_of(start, NBACK), NBACK)
    return pl.ds(start, NBACK, stride=rate)


def _band_bias():
    qs = lax.broadcasted_iota(jnp.int32, (2 * NBACK, 2 * NBACK), 0)
    kj = lax.broadcasted_iota(jnp.int32, (2 * NBACK, 2 * NBACK), 1)
    dist = (qs & (NBACK - 1)) + NBACK - kj
    both = (dist >= 0) & (dist <= NBACK)
    return jnp.where(both, 0.0, NEG_INF), jnp.where(both & (kj >= NBACK), 0.0, NEG_INF)


def _attn_fwd(qn, kn, v, rate, name, dep=None):
    def body(q_ref, k_ref, v_ref, *rest):
        o_ref, l_ref = rest[-2:]
        lo = _lo_mask((NBACK, PAIR))
        bias_all, bias_first = _band_bias()

        def step(it, carry):
            start, startp, kmin = _res_index(it, rate)
            q = q_ref[_rows(start, rate), :] * (HEAD ** -0.5)
            kcat = jnp.concatenate([k_ref[_rows(startp, rate), :], k_ref[_rows(start, rate), :]], axis=0).astype(BF16)
            vcat = jnp.concatenate([v_ref[_rows(startp, rate), :], v_ref[_rows(start, rate), :]], axis=0).astype(BF16)
            vcat1 = jnp.concatenate([vcat, jnp.ones((2 * NBACK, PAIR), BF16)], axis=1)
            q2 = jnp.concatenate([jnp.where(lo, q, 0.0), jnp.where(lo, 0.0, q)], axis=0).astype(BF16)
            s = lax.dot_general(q2, kcat, (((1,), (1,)), ((), ())), preferred_element_type=F32)
            s = s + jnp.where(kmin == 0, bias_all, bias_first)
            m = jnp.max(s, axis=-1, keepdims=True)
            ol = jnp.dot(jnp.exp(s - m).astype(BF16), vcat1, preferred_element_type=F32)
            o2 = ol[:, 0:PAIR] / ol[:, PAIR:]
            ls = m + jnp.log(ol[:, PAIR:])
            o_ref[_rows(start, rate), :] = jnp.where(lo, o2[0:NBACK], o2[NBACK:])
            l_ref[_rows(start, rate), :] = jnp.where(lo, ls[0:NBACK], ls[NBACK:])
            return carry

        lax.fori_loop(0, T // NBACK, step, 0, unroll=4)

    pm = pl.BlockSpec((None, T, PAIR), lambda p: (p, 0, 0))
    in_specs, args = [pm, pm, pm], [qn, kn, v]
    if dep is not None:
        in_specs.append(HBM_SPEC)
        args.append(dep)
    return pl.pallas_call(
        body, name=name, grid=(NPAIR,), in_specs=in_specs, out_specs=[pm, pm],
        out_shape=[_sds((NPAIR, T, PAIR), F32)] * 2, compiler_params=_cparams(1))(*args)


def _merge_fwd(cat_ab, outs, lses, tm=512):
    def body(cat_in, o0, o1, o2, l0, l1, l2, cat_ref, lse_ref):
        del cat_in
        for p in range(NPAIR):
            a0, a1, a2 = l0[p], l1[p], l2[p]
            m = jnp.maximum(jnp.maximum(a0, a1), a2)
            w0, w1, w2 = jnp.exp(a0 - m), jnp.exp(a1 - m), jnp.exp(a2 - m)
            s = w0 + w1 + w2
            b = (w0 * o0[p] + w1 * o1[p] + w2 * o2[p]) / s
            cat_ref[:, p * PAIR:(p + 1) * PAIR] = b.astype(BF16)
            lse_ref[p] = m + jnp.log(s)

    pm = pl.BlockSpec((NPAIR, tm, PAIR), lambda i: (0, i, 0))
    return pl.pallas_call(
        body, name="merge_fwd", grid=(T // tm,),
        in_specs=[pl.BlockSpec(memory_space=pl.ANY)] + [pm] * 6,
        out_specs=[pl.BlockSpec((tm, 512), lambda i: (i, 1)), pm],
        out_shape=[_sds((T, D), BF16), _sds((NPAIR, T, PAIR), F32)],
        input_output_aliases={0: 0}, compiler_params=_cparams(1))(cat_ab, *outs, *lses)


def _b_pre_bwd(dcat, cat, tm=512):
    def body(db_ref, b_ref, dbp_ref, e_ref):
        lo = _lo_mask((tm, PAIR))
        for p in range(NPAIR):
            db = db_ref[:, p * PAIR:(p + 1) * PAIR].astype(F32)
            b = b_ref[:, p * PAIR:(p + 1) * PAIR].astype(F32)
            dbp_ref[p] = db
            e_ref[p] = _seg_mean(db * b, lo) * float(HEAD)

    pm = pl.BlockSpec((NPAIR, tm, PAIR), lambda i: (0, i, 0))
    right = pl.BlockSpec((tm, 512), lambda i: (i, 1))
    return pl.pallas_call(
        body, name="b_pre_bwd", grid=(T // tm,), in_specs=[right, right], out_specs=[pm, pm],
        out_shape=[_sds((NPAIR, T, PAIR), F32)] * 2, compiler_params=_cparams(1))(dcat, cat)


def _attn_bwd(qn, kn, v, dbp, e, lse, rate, name):
    def body(q_ref, k_ref, v_ref, db_ref, e_ref, lse_ref, dq_ref, dk_ref, dv_ref):
        lo = _lo_mask((NBACK, PAIR))
        bias_all, bias_first = _band_bias()
        scale = HEAD ** -0.5
        nt = (((1,), (1,)), ((), ()))
        tn = (((0,), (0,)), ((), ()))
        window = NBACK * rate
        nblk = T // window

        def one(it, carry):
            dk_carry, dv_carry = carry
            rho = it // nblk
            b = it % nblk
            start = b * window + rho
            rq = _rows(start, rate)
            rp = _rows(jnp.maximum(start - window, rho), rate)
            q = q_ref[rq, :] * scale
            db = db_ref[rq, :]
            ev = e_ref[rq, :]
            ls = lse_ref[rq, :]
            kcat = jnp.concatenate([k_ref[rp, :], k_ref[rq, :]], axis=0).astype(BF16)
            vcat = jnp.concatenate([v_ref[rp, :], v_ref[rq, :]], axis=0).astype(BF16)
            q2 = jnp.concatenate([jnp.where(lo, q, 0.0), jnp.where(lo, 0.0, q)], axis=0).astype(BF16)
            db2 = jnp.concatenate([jnp.where(lo, db, 0.0), jnp.where(lo, 0.0, db)], axis=0).astype(BF16)
            ls2 = jnp.concatenate([ls[:, 0:1], ls[:, HEAD:HEAD + 1]], axis=0)
            ev2 = jnp.concatenate([ev[:, 0:1], ev[:, HEAD:HEAD + 1]], axis=0)
            s = lax.dot_general(q2, kcat, nt, preferred_element_type=F32)
            pt = jnp.exp(s + jnp.where(b > 0, bias_all, bias_first) - ls2)
            dp = lax.dot_general(db2, vcat, nt, preferred_element_type=F32)
            ds = (pt * (dp - ev2)).astype(BF16)
            dq2 = jnp.dot(ds, kcat, preferred_element_type=F32) * scale
            dkc = lax.dot_general(ds, q2, tn, preferred_element_type=F32)
            dvc = lax.dot_general(pt.astype(BF16), db2, tn, preferred_element_type=F32)
            dq_ref[rq, :] = jnp.where(lo, dq2[0:NBACK], dq2[NBACK:])
            dk_ref[rp, :] = dk_carry + dkc[0:NBACK]
            dk_ref[rq, :] = dkc[NBACK:]
            dv_ref[rp, :] = dv_carry + dvc[0:NBACK]
            dv_ref[rq, :] = dvc[NBACK:]
            return dkc[NBACK:], dvc[NBACK:]

        def step(i, carry):
            for u in range(ATTN_BWD_UNROLL):
                carry = one(i * ATTN_BWD_UNROLL + u, carry)
            return carry

        zero = jnp.zeros((NBACK, PAIR), F32)
        lax.fori_loop(0, T // NBACK // ATTN_BWD_UNROLL, step, (zero, zero))

    pm = pl.BlockSpec((None, T, PAIR), lambda p: (p, 0, 0))
    return pl.pallas_call(
        body, name=name, grid=(NPAIR,), in_specs=[pm] * 6, out_specs=[pm] * 3,
        out_shape=[_sds((NPAIR, T, PAIR), F32)] * 3, compiler_params=_cparams(1, 56))(qn, kn, v, dbp, e, lse)


def _ab_in_bwd(pab, dcat, sgu_g, sgu_b, sgu_w, sgu_bias3, qg, kg, tabs, dqkv, rinvs, tm=256):
    def body(p_ref, dcat_ref, g_ref, b_ref, w_ref, bias_ref, qg_ref, kg_ref, c_ref, s1_ref, s2_ref, *rest):
        dq_refs, rinv_refs = rest[0:9], rest[9:15]
        o_ref, dwm_ref, dbias_ref, dsg_ref, dsb_ref, dgain_ref = rest[15:]
        i = pl.program_id(0)

        @pl.when(i == 0)
        def _():
            dwm_ref[...] = jnp.zeros_like(dwm_ref)
            dbias_ref[...] = jnp.zeros_like(dbias_ref)
            dsg_ref[...] = jnp.zeros_like(dsg_ref)
            dsb_ref[...] = jnp.zeros_like(dsb_ref)
            dgain_ref[...] = jnp.zeros_like(dgain_ref)

        zu = p_ref[:, 0:512].astype(F32)
        zv = p_ref[:, 512:1024].astype(F32)
        u = _gelu(zu)
        v = _gelu(zv)
        mu = jnp.mean(v, axis=-1, keepdims=True)
        vc = v - mu
        rstd = lax.rsqrt(jnp.mean(vc * vc, axis=-1, keepdims=True) + EPS)
        xhat = vc * rstd
        vn = (xhat * g_ref[...] + b_ref[...]).astype(BF16)
        da = dcat_ref[...].astype(F32)
        tri = _tril_mask()
        du_parts = [[None] * 4 for _ in range(tm // 128)]
        dvn_parts = [[None] * 4 for _ in range(tm // 128)]
        for gi in range(4):
            wg = jnp.where(tri, w_ref[gi], 0.0).astype(BF16)
            bg = bias_ref[gi]
            for c in range(tm // 128):
                rs, cs = slice(c * 128, (c + 1) * 128), slice(gi * 128, (gi + 1) * 128)
                vblk = vn[rs, cs]
                mixed = jnp.dot(wg, vblk, preferred_element_type=F32) + bg
                dab = da[rs, cs]
                du_parts[c][gi] = dab * mixed
                dmixed = dab * u[rs, cs]
                dmb = dmixed.astype(BF16)
                dvn_parts[c][gi] = lax.dot_general(wg, dmb, (((0,), (0,)), ((), ())), preferred_element_type=F32)
                dwm = lax.dot_general(dmb, vblk, (((1,), (1,)), ((), ())), preferred_element_type=F32)
                dwm_ref[gi] += jnp.where(tri, dwm, 0.0)
                dbias_ref[gi] += dmixed
        du = jnp.concatenate([jnp.concatenate(r, axis=1) for r in du_parts], axis=0)
        dvn = jnp.concatenate([jnp.concatenate(r, axis=1) for r in dvn_parts], axis=0)
        dsg_ref[...] += jnp.sum(dvn * xhat, axis=0, keepdims=True)
        dsb_ref[...] += jnp.sum(dvn, axis=0, keepdims=True)
        dxh = dvn * g_ref[...]
        dv = rstd * (dxh - jnp.mean(dxh, axis=-1, keepdims=True)
                     - xhat * jnp.mean(dxh * xhat, axis=-1, keepdims=True))
        o_ref[:, 0:512] = (du * _gelu_grad(zu)).astype(BF16)
        o_ref[:, 512:1024] = (dv * _gelu_grad(zv)).astype(BF16)

        blocks = _head_blocks()
        c, s1, s2 = c_ref[...], s1_ref[...], s2_ref[...]
        for g in range(3):
            dq_ref, dk_ref, dv_ref = dq_refs[3 * g:3 * g + 3]
            for p in range(NPAIR):
                for which, gains, src in ((0, qg_ref, dq_ref), (1, kg_ref, dk_ref)):
                    col = (2 + 3 * which + g) * 512 + p * PAIR
                    xr = p_ref[:, col:col + PAIR].astype(F32)
                    rinv = rinv_refs[2 * g + which][p].astype(F32)
                    xh = xr * rinv
                    dn = _rope_t(src[p], c, s1, s2)
                    row = 2 * g + which
                    dgain_ref[row:row + 1, :] += jnp.sum(dn * xh, axis=0, keepdims=True)
                    dxh2 = dn * gains[g:g + 1, :]
                    dx = rinv * (dxh2 - xh * _seg_mean_mxu(dxh2 * xh, blocks))
                    o_ref[:, col:col + PAIR] = dx.astype(BF16)
                col = (8 + g) * 512 + p * PAIR
                o_ref[:, col:col + PAIR] = dv_ref[p].astype(BF16)

    pm = pl.BlockSpec((NPAIR, tm, PAIR), lambda i: (0, i, 0))
    tab = pl.BlockSpec((tm, PAIR), lambda i: (i, 0))
    gain = pl.BlockSpec((3, PAIR), lambda i: (0, 0))
    vec = pl.BlockSpec((1, 512), lambda i: (0, 0))
    full = pl.BlockSpec((tm, AB_IN), lambda i: (i, 0))
    w4 = pl.BlockSpec((4, 128, 128), lambda i: (0, 0, 0))
    return pl.pallas_call(
        body, name="ab_in_bwd", grid=(T // tm,),
        in_specs=[full, pl.BlockSpec((tm, 512), lambda i: (i, 0)), vec, vec, w4,
                  pl.BlockSpec((4, 128, 1), lambda i: (0, 0, 0)), gain, gain, tab, tab, tab] + [pm] * 15,
        out_specs=[full, w4, w4, vec, vec, pl.BlockSpec((8, PAIR), lambda i: (0, 0))],
        out_shape=[_sds((T, AB_IN), BF16), _sds((4, 128, 128), F32), _sds((4, 128, 128), F32),
                   _sds((1, 512), F32), _sds((1, 512), F32), _sds((8, PAIR), F32)],
        compiler_params=_cparams(1))(pab, dcat, sgu_g, sgu_b, sgu_w, sgu_bias3, qg, kg, *tabs, *dqkv, *rinvs)


def _ln_stats(x):
    mu = jnp.mean(x, axis=-1, keepdims=True)
    xc = x - mu
    rstd = lax.rsqrt(jnp.mean(xc * xc, axis=-1, keepdims=True) + EPS)
    return xc * rstd, rstd


CONV_RC = 64


def _shifted_copies(src, dst, tm):
    dst[0] = src[...]
    for b in range(1, 8):
        dst[b, 0:tm + HALO - 8, :] = src[pl.ds(b, tm + HALO - 8), :]


def _offsets_by_phase(first):
    groups = {}
    for o in range(first, first + CONV_C_TAPS):
        groups.setdefault(o % 8, []).append(o)
    return sorted(groups.items())


def _window(shifted, b8, base, offsets, lanes):
    rows = 8 * (max(offsets) // 8) + CONV_RC
    return shifted[b8, pl.ds(base, rows), lanes].reshape(rows // 8, 8, 128)


def _cd_fwd(pcd, cw, cb, lg, lb, dw, tm=512):
    per = tm // HALO

    def body(p_ref, h_ref, cw_ref, cb_ref, lg_ref, lb_ref, dw_ref, cat_ref, c0_ref, c1_ref, dd_ref, y_ref,
             buf, buf2, sb):
        i = pl.program_id(0)
        live = jnp.where(i > 0, 1.0, 0.0)
        a = p_ref[:, 0:512].astype(F32)
        gt = p_ref[:, 512:1024].astype(F32)
        gb = p_ref[:, 1024:1536].astype(F32)
        gc = p_ref[:, 1536:2048].astype(F32)
        hv = p_ref[:, 2048:2560].astype(F32)
        c0 = a * _sigmoid(gt)
        dd = gc * hv
        buf[0:HALO, :] = h_ref[:, 0:512].astype(F32) * _sigmoid(h_ref[:, 512:1024].astype(F32)) * live
        buf[HALO:, :] = c0
        buf2[0:HALO, :] = h_ref[:, 1536:2048].astype(F32) * h_ref[:, 2048:2560].astype(F32) * live
        buf2[HALO:, :] = dd
        c0_ref[...] = c0.astype(BF16)
        dd_ref[...] = dd.astype(BF16)
        _shifted_copies(buf, sb, tm)

        def conv_rows(r, carry):
            base = pl.multiple_of(r * CONV_RC, CONV_RC)
            for c in range(4):
                lanes = slice(c * 128, (c + 1) * 128)
                acc = jnp.broadcast_to(cb_ref[:, lanes], (CONV_RC // 8, 8, 128))
                for b8, offsets in _offsets_by_phase(HALO - (CONV_C_TAPS - 1)):
                    win = _window(sb, b8, base, offsets, lanes)
                    for o in offsets:
                        j = o - (HALO - (CONV_C_TAPS - 1))
                        acc = acc + cw_ref[8 * j:8 * j + 8, lanes] * win[o // 8:o // 8 + CONV_RC // 8]
                c1_ref[pl.ds(base, CONV_RC), lanes] = acc.reshape(CONV_RC, 128)
            return carry

        lax.fori_loop(0, tm // CONV_RC, conv_rows, 0)
        xhat, _ = _ln_stats(c1_ref[...])
        c2 = xhat * lg_ref[...] + lb_ref[...]
        y = jnp.zeros((tm, 512), F32)
        for j in range(CONV_D_TAPS):
            y = y + dw_ref[j:j + 1, :] * buf2[pl.ds(HALO - (CONV_D_TAPS - 1) + j, tm), :]
        cat_ref[:, 0:512] = (c2 * _sigmoid(c2)).astype(BF16)
        cat_ref[:, 512:1024] = (gb * y).astype(BF16)
        y_ref[...] = y.astype(BF16)

    half = pl.BlockSpec((tm, 512), lambda i: (i, 0))
    vec = pl.BlockSpec((1, 512), lambda i: (0, 0))
    return pl.pallas_call(
        body, name="cd_fwd", grid=(T // tm,),
        in_specs=[pl.BlockSpec((tm, CD_IN), lambda i: (i, 0)),
                  pl.BlockSpec((HALO, CD_IN), lambda i: (jnp.maximum(i * per - 1, 0), 0)),
                  pl.BlockSpec((8 * 32, 512), lambda i: (0, 0)), vec, vec, vec, pl.BlockSpec((8, 512), lambda i: (0, 0))],
        out_specs=[pl.BlockSpec((tm, D), lambda i: (i, 0)), half, half, half, half],
        out_shape=[_sds((T, D), BF16), _sds((T, 512), BF16), _sds((T, 512), F32), _sds((T, 512), BF16),
                   _sds((T, 512), BF16)],
        scratch_shapes=[pltpu.VMEM((HALO + tm, 512), F32), pltpu.VMEM((HALO + tm, 512), F32),
                        pltpu.VMEM((8, HALO + tm, 512), F32)],
        compiler_params=_cparams(1))(pcd, pcd, cw, cb, lg, lb, dw)


def _cd_bwd_pw(dcat, c1, pcd, y, lg, lb, tm=512):
    def body(dcat_ref, c1_ref, gb_ref, y_ref, lg_ref, lb_ref, dc1_ref, dy3_ref, dgb_ref, dlg_ref, dlb_ref, dcb_ref):
        i = pl.program_id(0)

        @pl.when(i == 0)
        def _():
            dlg_ref[...] = jnp.zeros_like(dlg_ref)
            dlb_ref[...] = jnp.zeros_like(dlb_ref)
            dcb_ref[...] = jnp.zeros_like(dcb_ref)

        dc = dcat_ref[:, 0:512].astype(F32)
        ddo = dcat_ref[:, 512:1024].astype(F32)
        xhat, rstd = _ln_stats(c1_ref[...])
        c2 = xhat * lg_ref[...] + lb_ref[...]
        sg = _sigmoid(c2)
        dc2 = dc * sg * (1.0 + c2 * (1.0 - sg))
        dlg_ref[...] += jnp.sum(dc2 * xhat, axis=0, keepdims=True)
        dlb_ref[...] += jnp.sum(dc2, axis=0, keepdims=True)
        dxh = dc2 * lg_ref[...]
        dc1 = rstd * (dxh - jnp.mean(dxh, axis=-1, keepdims=True)
                      - xhat * jnp.mean(dxh * xhat, axis=-1, keepdims=True))
        dcb_ref[...] += jnp.sum(dc1, axis=0, keepdims=True)
        dc1_ref[...] = dc1
        dgb_ref[...] = (ddo * y_ref[...].astype(F32)).astype(BF16)
        dy3_ref[...] = ddo * gb_ref[...].astype(F32)

    half = pl.BlockSpec((tm, 512), lambda i: (i, 0))
    vec = pl.BlockSpec((1, 512), lambda i: (0, 0))
    return pl.pallas_call(
        body, name="cd_bwd_pw", grid=(T // tm,),
        in_specs=[pl.BlockSpec((tm, D), lambda i: (i, 0)), half, pl.BlockSpec((tm, 512), lambda i: (i, 2)), half,
                  vec, vec],
        out_specs=[half, half, half, vec, vec, vec],
        out_shape=[_sds((T, 512), F32), _sds((T, 512), F32), _sds((T, 512), BF16),
                   _sds((1, 512), F32), _sds((1, 512), F32), _sds((1, 512), F32)],
        compiler_params=_cparams(1))(dcat, c1, pcd, y, lg, lb)


def _cd_bwd_conv(pcd, dc1, dy3, c0, dd, dgb, cw8, dw, tm=256):
    per = tm // HALO
    nblk = T // tm
    last32 = T // HALO - 1

    def body(p_ref, dc1_ref, dc1n_ref, dy3_ref, dy3n_ref, c0_ref, dd_ref, dgb_ref, cw_ref, dw_ref,
             o_ref, dcw_ref, ddw_ref, dbuf, d3buf, sd, dc0_buf):
        i = pl.program_id(0)
        has_next = jnp.where(i < nblk - 1, 1.0, 0.0)

        @pl.when(i == 0)
        def _():
            dcw_ref[...] = jnp.zeros_like(dcw_ref)
            ddw_ref[...] = jnp.zeros_like(ddw_ref)

        dbuf[0:tm, :] = dc1_ref[...]
        dbuf[tm:, :] = dc1n_ref[...] * has_next
        d3buf[0:tm, :] = dy3_ref[...]
        d3buf[tm:, :] = dy3n_ref[...] * has_next
        _shifted_copies(dbuf, sd, tm)
        n_tiles = tm // CONV_RC

        phases = _offsets_by_phase(0)

        def dc0_rows(r, carry):
            base = pl.multiple_of(r * CONV_RC, CONV_RC)
            for c in range(4):
                lanes = slice(c * 128, (c + 1) * 128)
                acc = jnp.zeros((CONV_RC // 8, 8, 128), F32)
                for b8, offsets in phases:
                    win = _window(sd, b8, base, offsets, lanes)
                    for o in offsets:
                        j = CONV_C_TAPS - 1 - o
                        acc = acc + cw_ref[8 * j:8 * j + 8, lanes] * win[o // 8:o // 8 + CONV_RC // 8]
                dc0_buf[pl.ds(base, CONV_RC), lanes] = acc.reshape(CONV_RC, 128)
            return carry

        lax.fori_loop(0, n_tiles, dc0_rows, 0)

        for c in range(4):
            lanes = slice(c * 128, (c + 1) * 128)
            for b8, offsets in phases:
                def dw_rows(r, accs, lanes=lanes, b8=b8, offsets=offsets):
                    base = pl.multiple_of(r * CONV_RC, CONV_RC)
                    xin = c0_ref[pl.ds(base, CONV_RC), lanes].astype(F32).reshape(CONV_RC // 8, 8, 128)
                    win = _window(sd, b8, base, offsets, lanes)
                    return tuple(acc + jnp.sum(xin * win[o // 8:o // 8 + CONV_RC // 8], axis=0)
                                 for acc, o in zip(accs, offsets))

                accs = lax.fori_loop(0, n_tiles, dw_rows, tuple(jnp.zeros((8, 128), F32) for _ in offsets))
                for acc, o in zip(accs, offsets):
                    j = CONV_C_TAPS - 1 - o
                    dcw_ref[j:j + 1, lanes] += jnp.sum(acc, axis=0, keepdims=True)

        dc0 = dc0_buf[...]
        ddin = dd_ref[...].astype(F32)
        ddd = jnp.zeros((tm, 512), F32)
        for j in range(CONV_D_TAPS):
            dy_shift = d3buf[pl.ds(CONV_D_TAPS - 1 - j, tm), :]
            ddd = ddd + dw_ref[j:j + 1, :] * dy_shift
            ddw_ref[j:j + 1, :] += jnp.sum(ddin * dy_shift, axis=0, keepdims=True)

        a = p_ref[:, 0:512].astype(F32)
        gt = p_ref[:, 512:1024].astype(F32)
        gc = p_ref[:, 1536:2048].astype(F32)
        hv = p_ref[:, 2048:2560].astype(F32)
        sg = _sigmoid(gt)
        o_ref[:, 0:512] = (dc0 * sg).astype(BF16)
        o_ref[:, 512:1024] = (dc0 * a * sg * (1.0 - sg)).astype(BF16)
        o_ref[:, 1024:1536] = dgb_ref[...]
        o_ref[:, 1536:2048] = (ddd * hv).astype(BF16)
        o_ref[:, 2048:2560] = (ddd * gc).astype(BF16)

    half = pl.BlockSpec((tm, 512), lambda i: (i, 0))
    nxt = pl.BlockSpec((HALO, 512), lambda i: (jnp.minimum((i + 1) * per, last32), 0))
    full = pl.BlockSpec((tm, CD_IN), lambda i: (i, 0))
    return pl.pallas_call(
        body, name="cd_bwd_conv", grid=(nblk,),
        in_specs=[full, half, nxt, half, nxt, half, half, half,
                  pl.BlockSpec((8 * 32, 512), lambda i: (0, 0)), pl.BlockSpec((8, 512), lambda i: (0, 0))],
        out_specs=[full, pl.BlockSpec((32, 512), lambda i: (0, 0)), pl.BlockSpec((8, 512), lambda i: (0, 0))],
        out_shape=[_sds((T, CD_IN), BF16), _sds((32, 512), F32), _sds((8, 512), F32)],
        scratch_shapes=[pltpu.VMEM((tm + HALO, 512), F32), pltpu.VMEM((tm + HALO, 512), F32),
                        pltpu.VMEM((8, tm + HALO, 512), F32), pltpu.VMEM((tm, 512), F32)],
        compiler_params=_cparams(1))(pcd, dc1, dc1, dy3, dy3, c0, dd, dgb, cw8, dw)


def _local_step(x, tgt, W, fetch=None, on_grad=None):
    W = dict(W)
    if fetch is None:
        fetch = lambda stage, after: {}
    if on_grad is None:
        on_grad = lambda key, arr: None
    tabs = _rope_tables()
    qg = jnp.tile(W["q_norm_g"], (1, 2))
    kg = jnp.tile(W["k_norm_g"], (1, 2))
    bias3 = W["sgu_bias"].reshape(4, 128, 1)
    G = {}

    h0 = _rms_fwd(x, W["ab_norm_g"], "rms_fwd_ab", dep=W.get("dep_first"))
    W.update(fetch("ab_in", h0))
    pab = _mm_nt(h0, W["wt_ab_in"], "mm_ab_in", dep=W.get("dep0"))
    cat_ab = _mix_a_fwd(pab, W["sgu_norm_g"], W["sgu_norm_b"], W["sgu_w"], bias3)
    qkv, rinvs = _prep_fwd(pab, qg, kg, tabs)
    outs, lses = [], []
    for g, rate in enumerate(DIL_RATES):
        o, l = _attn_fwd(qkv[3 * g], qkv[3 * g + 1], qkv[3 * g + 2], rate, f"attn_fwd_{g}", dep=W.get(f"dep_attn{g}"))
        outs.append(o)
        lses.append(l)
        W.update(fetch(f"attn{g}", o))
    cat_ab, lse = _merge_fwd(cat_ab, outs, lses)
    W.update(fetch("ab_out", lse))
    x1, h1 = _mm_nn(cat_ab, W["w_ab_out"], "mm_ab_out", mode="rms", resid=x, gain=W["ffn_norm_g"][0:1])
    pf0, act0 = _ffn_in(h1, W["wt_ffn_in0"], "ffn_in0")
    W.update(fetch("ffn_down0", act0))
    x2, h2 = _mm_nn(act0, W["w_ffn_down0"], "mm_ffn_down0", mode="rms", resid=x1, gain=W["cd_norm_g"],
                    dep=W.get("dep_down0"))
    W.update(fetch("cd_in", h2))
    pcd = _mm_nt(h2, W["wt_cd_in"], "mm_cd_in")
    cw8 = jnp.repeat(W["conv_c_w32"], 8, axis=0)
    cat_cd, c0, c1, dd, yv = _cd_fwd(pcd, cw8, W["conv_c_b"], W["c_ln_g"], W["c_ln_b"], W["conv_d_w8"])
    x3, h3 = _mm_nn(cat_cd, W["w_cd_out"], "mm_cd_out", mode="rms", resid=x2, gain=W["ffn_norm_g"][1:2])
    pf1, act1 = _ffn_in(h3, W["wt_ffn_in1"], "ffn_in1")
    dy, dyb, loss_cols = _mm_nn(act1, W["w_ffn_down1"], "mm_ffn_down1", mode="loss", resid=x3, tgt=tgt)

    def ffn_bwd(xin, h, pf, act, dres, dresb, layer):
        G[f"w_ffn_down{layer}"] = _mm_tn(act, dresb, f"mm_g_ffn_down{layer}")
        dep = on_grad(f"w_ffn_down{layer}", G[f"w_ffn_down{layer}"])
        dpf = _ffn_dact(dresb, W[f"w_ffn_down{layer}"], pf, f"ffn_dact{layer}", dep=dep)
        G[f"wt_ffn_in{layer}"] = _mm_tn(dpf, h, f"mm_g_ffn_in{layer}")
        dep = on_grad(f"wt_ffn_in{layer}", G[f"wt_ffn_in{layer}"])
        dx, dxb, G[f"ffn_norm_g{layer}"] = _mm_dh_rms_bwd(
            dpf, W[f"wt_ffn_in{layer}"], xin, W["ffn_norm_g"][layer:layer + 1], dres, f"mm_d_h_ffn{layer}", dep=dep)
        return dx, dxb

    dx3, dx3b = ffn_bwd(x3, h3, pf1, act1, dy, dyb, 1)

    G["w_cd_out"] = _mm_tn(cat_cd, dx3b, "mm_g_cd_out")
    dep = on_grad("w_cd_out", G["w_cd_out"])
    dcat_cd = _mm_nt(dx3b, W["w_cd_out"], "mm_d_cat_cd", dep=dep)
    dc1, dy3, dgb, G["c_ln_g"], G["c_ln_b"], G["conv_c_b"] = _cd_bwd_pw(dcat_cd, c1, pcd, yv, W["c_ln_g"], W["c_ln_b"])
    dpcd, G["conv_c_w32"], G["conv_d_w8"] = _cd_bwd_conv(pcd, dc1, dy3, c0, dd, dgb, cw8, W["conv_d_w8"])
    G["wt_cd_in"] = _mm_tn(dpcd, h2, "mm_g_cd_in")
    dep = on_grad("wt_cd_in", G["wt_cd_in"])
    dx2, dx2b, G["cd_norm_g"] = _mm_dh_rms_bwd(dpcd, W["wt_cd_in"], x2, W["cd_norm_g"], dx3, "mm_d_h_cd", dep=dep)

    dx1, dx1b = ffn_bwd(x1, h1, pf0, act0, dx2, dx2b, 0)

    G["w_ab_out"] = _mm_tn(cat_ab, dx1b, "mm_g_ab_out")
    dep = on_grad("w_ab_out", G["w_ab_out"])
    dcat_ab = _mm_nt(dx1b, W["w_ab_out"], "mm_d_cat_ab", dep=dep)
    dbp, e = _b_pre_bwd(dcat_ab, cat_ab)
    dqkv = []
    for g, rate in enumerate(DIL_RATES):
        dqkv += _attn_bwd(qkv[3 * g], qkv[3 * g + 1], qkv[3 * g + 2], dbp, e, lse, rate, f"attn_bwd_{g}")
    dpab, G["sgu_w"], dbias_part, G["sgu_norm_g"], G["sgu_norm_b"], dgain = _ab_in_bwd(
        pab, dcat_ab, W["sgu_norm_g"], W["sgu_norm_b"], W["sgu_w"], bias3, qg, kg, tabs, dqkv, rinvs)
    G["sgu_bias"] = jnp.sum(dbias_part, axis=-1)
    dgain = dgain[0:6, 0:HEAD] + dgain[0:6, HEAD:PAIR]
    G["q_norm_g"] = dgain[0::2]
    G["k_norm_g"] = dgain[1::2]
    G["wt_ab_in"] = _mm_tn(dpab, h0, "mm_g_ab_in")
    dep = on_grad("wt_ab_in", G["wt_ab_in"])
    grad_x, _, G["ab_norm_g"] = _mm_dh_rms_bwd(dpab, W["wt_ab_in"], x, W["ab_norm_g"], dx1, "mm_d_h_ab", dep=dep)
    return loss_cols, grad_x, G


def _my_place():
    return lax.axis_index("x"), lax.axis_index("y"), lax.axis_index("c")


def _dev_index(px, py, pc):
    return 4 * px + 2 * py + pc


def _flip(place, k):
    x, y, c = place
    return (1 - x if k & 4 else x, 1 - y if k & 2 else y, 1 - c if k & 1 else c)


def _landing(shape, dtype, own):
    buf = lax.empty(shape, dtype)
    for lead, part in own:
        buf = lax.dynamic_update_slice(buf, part.reshape((1,) * len(lead) + part.shape),
                                       tuple(lead) + (0,) * part.ndim)
    return buf


HBM_ONLY = pl.BlockSpec(memory_space=pltpu.HBM)
SEM_SPEC = pl.BlockSpec(memory_space=pltpu.SEMAPHORE)
IN_FLIGHT = pltpu.CompilerParams(has_side_effects=pltpu.SideEffectType.DATAFLOW_SIDE_EFFECTING)


def _in_hbm(a):
    return pltpu.with_memory_space_constraint(a, pltpu.HBM)


def _exchange_start(name, srcs, lands, items, dep=None):
    ns, nl, ni = len(srcs), len(lands), len(items)

    def body(*refs):
        S, L = refs[0:ns], refs[ns:ns + nl]
        first_out = ns + nl + (0 if dep is None else 1)
        send_sems, recv_sems, token = refs[first_out], refs[first_out + 1], refs[-1]
        me = _my_place()
        mi = _dev_index(*me)
        for i, (src, dst) in enumerate(items):
            for k in range(1, NDEV):
                peer = _flip(me, k)
                pltpu.make_async_remote_copy(
                    src_ref=src(S, _dev_index(*peer)), dst_ref=dst(L, mi), send_sem=send_sems.at[7 * i + k - 1],
                    recv_sem=recv_sems.at[7 * i + k - 1], device_id=peer, device_id_type=MESH).start()
        token[...] = jnp.zeros_like(token)

    thru = [pltpu.HBM(a.shape, a.dtype) for a in list(srcs) + list(lands)]
    args = [_in_hbm(a) for a in srcs] + [_in_hbm(a) for a in lands]
    in_specs = [HBM_ONLY] * (ns + nl)
    if dep is not None:
        args.append(dep)
        in_specs.append(HBM_SPEC)
    outs = pl.pallas_call(
        body, name=name, in_specs=in_specs,
        out_shape=(pltpu.SemaphoreType.DMA((7 * ni,)), pltpu.SemaphoreType.DMA((7 * ni,)), *thru, _sds((8, 128), F32)),
        out_specs=(SEM_SPEC, SEM_SPEC, *[HBM_ONLY] * (ns + nl), pl.BlockSpec(memory_space=pltpu.VMEM)),
        input_output_aliases={j: 2 + j for j in range(ns + nl)}, compiler_params=IN_FLIGHT)(*args)
    return dict(send=outs[0], recv=outs[1], srcs=list(outs[2:2 + ns]), lands=list(outs[2 + ns:2 + ns + nl]),
                token=outs[-1], items=items)


def _exchange_wait(name, states, after):
    after = list(after) if isinstance(after, (list, tuple)) else [after]
    counts = [(len(st["srcs"]), len(st["lands"]), len(st["items"])) for st in states]
    n_arrays = sum(c[0] + c[1] for c in counts)

    def body(*refs):
        me = _my_place()
        mi = _dev_index(*me)
        pos = 0
        sem_pos = n_arrays
        for st, (ns, nl, ni) in zip(states, counts):
            S, L = refs[pos:pos + ns], refs[pos + ns:pos + ns + nl]
            send_sems, recv_sems = refs[sem_pos], refs[sem_pos + 1]
            pos += ns + nl
            sem_pos += 2
            for i, (src, dst) in enumerate(st["items"]):
                for k in range(1, NDEV):
                    cp = pltpu.make_async_remote_copy(
                        src_ref=src(S, mi), dst_ref=dst(L, mi), send_sem=send_sems.at[7 * i + k - 1],
                        recv_sem=recv_sems.at[7 * i + k - 1], device_id=me, device_id_type=MESH)
                    cp.wait_send()
                    cp.wait_recv()

    arrays, sems = [], []
    for st in states:
        arrays += st["srcs"] + st["lands"]
        sems += [st["send"], st["recv"]]
    outs = pl.pallas_call(
        body, name=name, in_specs=[HBM_ONLY] * n_arrays + [SEM_SPEC] * len(sems) + [HBM_SPEC] * len(after),
        out_shape=tuple(pltpu.HBM(a.shape, a.dtype) for a in arrays), out_specs=tuple([HBM_ONLY] * n_arrays),
        input_output_aliases={j: j for j in range(n_arrays)}, compiler_params=IN_FLIGHT)(*arrays, *sems, *after)
    lands, pos = [], 0
    for ns, nl, _ in counts:
        lands.append(list(outs[pos + ns:pos + ns + nl]))
        pos += ns + nl
    return lands


def _place_and_neighbours():
    x, y, c = _my_place()
    return (x, y, c), (x, y, 1 - c), [(1 - x, y), (x, 1 - y), (1 - x, 1 - y)]


def _gather_start(name, srcs, lands, items, dep=None):
    ns, nl, ni = len(srcs), len(lands), len(items)

    def body(*refs):
        S, L = refs[0:ns], refs[ns:ns + nl]
        first_out = ns + nl + (0 if dep is None else 1)
        send_sems, recv_sems, token = refs[first_out], refs[first_out + 1], refs[-1]
        me, sib, chips = _place_and_neighbours()
        mi = _dev_index(*me)
        for i, (src, dst) in enumerate(items):
            for k, to in enumerate([sib] + [(*chip, me[2]) for chip in chips]):
                pltpu.make_async_remote_copy(
                    src_ref=src(S), dst_ref=dst(L, mi), send_sem=send_sems.at[4 * i + k],
                    recv_sem=recv_sems.at[4 * i + k], device_id=to, device_id_type=MESH).start()
        token[...] = jnp.zeros_like(token)

    thru = [pltpu.HBM(a.shape, a.dtype) for a in list(srcs) + list(lands)]
    args = [_in_hbm(a) for a in srcs] + [_in_hbm(a) for a in lands]
    in_specs = [HBM_ONLY] * (ns + nl)
    if dep is not None:
        args.append(dep)
        in_specs.append(HBM_SPEC)
    outs = pl.pallas_call(
        body, name=name, in_specs=in_specs,
        out_shape=(pltpu.SemaphoreType.DMA((4 * ni,)), pltpu.SemaphoreType.DMA((4 * ni,)), *thru, _sds((8, 128), F32)),
        out_specs=(SEM_SPEC, SEM_SPEC, *[HBM_ONLY] * (ns + nl), pl.BlockSpec(memory_space=pltpu.VMEM)),
        input_output_aliases={j: 2 + j for j in range(ns + nl)}, compiler_params=IN_FLIGHT)(*args)
    return dict(send=outs[0], recv=outs[1], srcs=list(outs[2:2 + ns]), lands=list(outs[2 + ns:2 + ns + nl]),
                token=outs[-1], items=items)


def _gather_forward(name, st, after):
    nl, ni = len(st["lands"]), len(st["items"])

    def body(*refs):
        L, recv_sems = refs[0:nl], refs[nl]
        fwd_send, fwd_recv, token = refs[2 * nl + 2], refs[2 * nl + 3], refs[-1]
        me, sib, chips = _place_and_neighbours()
        for i, (_, dst) in enumerate(st["items"]):
            for j, chip in enumerate(chips):
                blk = dst(L, _dev_index(*chip, me[2]))
                pltpu.make_async_remote_copy(
                    src_ref=blk, dst_ref=blk, send_sem=fwd_send.at[3 * i + j], recv_sem=recv_sems.at[4 * i + 1 + j],
                    device_id=me, device_id_type=MESH).wait_recv()
                pltpu.make_async_remote_copy(
                    src_ref=blk, dst_ref=blk, send_sem=fwd_send.at[3 * i + j], recv_sem=fwd_recv.at[3 * i + j],
                    device_id=sib, device_id_type=MESH).start()
        token[...] = jnp.zeros_like(token)

    outs = pl.pallas_call(
        body, name=name, in_specs=[HBM_ONLY] * nl + [SEM_SPEC, HBM_SPEC],
        out_shape=(*[pltpu.HBM(a.shape, a.dtype) for a in st["lands"]], pltpu.SemaphoreType.DMA((3 * ni,)),
                   pltpu.SemaphoreType.DMA((3 * ni,)), _sds((8, 128), F32)),
        out_specs=(*[HBM_ONLY] * nl, SEM_SPEC, SEM_SPEC, pl.BlockSpec(memory_space=pltpu.VMEM)),
        input_output_aliases={j: j for j in range(nl)}, compiler_params=IN_FLIGHT)(*st["lands"], st["recv"], after)
    return dict(st, lands=list(outs[0:nl]), fwd_send=outs[nl], fwd_recv=outs[nl + 1], token=outs[-1])


def _gather_wait(name, st, after):
    ns, nl, ni = len(st["srcs"]), len(st["lands"]), len(st["items"])

    def body(*refs):
        S, L = refs[0:ns], refs[ns:ns + nl]
        send_sems, recv_sems, fwd_send, fwd_recv = refs[ns + nl:ns + nl + 4]
        me, sib, chips = _place_and_neighbours()
        mi = _dev_index(*me)
        for i, (src, dst) in enumerate(st["items"]):
            mine = dst(L, mi)
            for k in range(4):
                pltpu.make_async_remote_copy(
                    src_ref=src(S), dst_ref=mine, send_sem=send_sems.at[4 * i + k], recv_sem=recv_sems.at[4 * i + k],
                    device_id=me, device_id_type=MESH).wait_send()
            pltpu.make_async_remote_copy(
                src_ref=src(S), dst_ref=mine, send_sem=send_sems.at[4 * i], recv_sem=recv_sems.at[4 * i],
                device_id=me, device_id_type=MESH).wait_recv()
            for j in range(3):
                cp = pltpu.make_async_remote_copy(
                    src_ref=mine, dst_ref=mine, send_sem=fwd_send.at[3 * i + j], recv_sem=fwd_recv.at[3 * i + j],
                    device_id=me, device_id_type=MESH)
                cp.wait_send()
                cp.wait_recv()

    arrays = st["srcs"] + st["lands"]
    outs = pl.pallas_call(
        body, name=name, in_specs=[HBM_ONLY] * (ns + nl) + [SEM_SPEC] * 4 + [HBM_SPEC],
        out_shape=tuple(pltpu.HBM(a.shape, a.dtype) for a in arrays), out_specs=tuple([HBM_ONLY] * (ns + nl)),
        input_output_aliases={j: j for j in range(ns + nl)},
        compiler_params=IN_FLIGHT)(*arrays, st["send"], st["recv"], st["fwd_send"], st["fwd_recv"], after)
    return list(outs[ns:ns + nl])


def _sum_slots(land):
    def body(l_ref, o_ref):
        acc = l_ref[0]
        for d in range(1, NDEV):
            acc = acc + l_ref[d]
        o_ref[...] = acc

    vm = pl.BlockSpec(memory_space=pltpu.VMEM)
    return pl.pallas_call(body, name="sum_small", out_shape=_sds(land.shape[1:], F32), in_specs=[vm], out_specs=vm)(land)


def _adam_math(w, g, m, v):
    m2 = ADAM_B1 * m + (1.0 - ADAM_B1) * g
    v2 = ADAM_B2 * v + (1.0 - ADAM_B2) * (g * g)
    delta = -ADAM_LR * ((m2 * ADAM_C1) / (jnp.sqrt(v2 * ADAM_C2) + ADAM_EPS) + ADAM_WD * w)
    return delta, m2, v2


def _adam_layer(land, sel, w, m, v, layer, name, prev=None, tc=512):
    R = land.shape[2]

    def body(l_ref, w_ref, m_ref, v_ref, *rest):
        g_out, d_out, m_out, v_out = rest[-4:]
        g = l_ref[0].astype(F32)
        for d in range(1, NDEV):
            g = g + l_ref[d].astype(F32)
        delta, m2, v2 = _adam_math(w_ref[...], g, m_ref[...], v_ref[...])
        g_out[...] = g
        d_out[...] = delta
        m_out[...] = m2
        v_out[...] = v2

    wspec = pl.BlockSpec((None, R, tc), lambda i: (layer, 0, i))
    in_specs = [pl.BlockSpec((None, NDEV, R, tc), lambda i: (sel, 0, 0, i)), wspec, wspec, wspec]
    args = [land, w, m, v]
    aliases = {}
    if prev is not None:
        in_specs += [HBM_SPEC] * 4
        args += list(prev)
        aliases = {4 + j: j for j in range(4)}
    return pl.pallas_call(
        body, name=name, grid=(D // tc,), in_specs=in_specs, out_specs=[wspec] * 4,
        out_shape=[_sds(w.shape, F32)] * 4, input_output_aliases=aliases, compiler_params=_cparams(1))(*args)


def _adam_stacked(lands, sel, w, m, v, name):
    res = None
    for layer, land in enumerate(lands):
        res = _adam_layer(land, sel, w, m, v, layer, f"{name}{layer}", prev=res)
    return res


def _adam_small(ws, gs, ms, vs):
    n = len(ws)

    def body(*refs):
        w_r, g_r, m_r, v_r = refs[0:n], refs[n:2 * n], refs[2 * n:3 * n], refs[3 * n:4 * n]
        d_o, m_o, v_o = refs[4 * n:5 * n], refs[5 * n:6 * n], refs[6 * n:7 * n]
        for i in range(n):
            delta, m2, v2 = _adam_math(w_r[i][...], g_r[i][...], m_r[i][...], v_r[i][...])
            d_o[i][...] = delta
            m_o[i][...] = m2
            v_o[i][...] = v2

    vm = pl.BlockSpec(memory_space=pltpu.VMEM)
    shapes = [_sds(w.shape, F32) for w in ws]
    outs = pl.pallas_call(body, name="adam_small", in_specs=[vm] * (4 * n), out_specs=[vm] * (3 * n),
                          out_shape=shapes * 3)(*ws, *gs, *ms, *vs)
    return outs[0:n], outs[n:2 * n], outs[2 * n:3 * n]


WEIGHT_NAMES = ("ab_norm_g", "ab_w_in", "sgu_norm_g", "sgu_norm_b", "sgu_w", "sgu_bias", "q_norm_g", "k_norm_g",
                "ab_w_out", "cd_norm_g", "cd_w_in", "conv_c_w", "conv_c_b", "c_ln_g", "c_ln_b", "conv_d_w",
                "cd_w_out", "ffn_norm_g", "ffn_w_gate", "ffn_w_up", "ffn_w_down")
SMALL_2D = (("ab_norm_g", (1, 1024)), ("sgu_norm_g", (1, 512)), ("sgu_norm_b", (1, 512)), ("sgu_w", (512, 128)),
            ("sgu_bias", (4, 128)), ("q_norm_g", (3, 64)), ("k_norm_g", (3, 64)), ("cd_norm_g", (1, 128)),
            ("conv_c_w", (31, 64)), ("conv_c_b", (1, 64)), ("c_ln_g", (1, 64)), ("c_ln_b", (1, 64)),
            ("conv_d_w", (3, 64)), ("ffn_norm_g", (2, 1024)))
SHARD_C = 64


def _pack_rows(parts, rows):
    flat = jnp.concatenate([p.reshape(-1) for p in parts])
    return jnp.pad(flat, (0, rows * 128 - flat.shape[0])).reshape(rows, 128)


def kernel(x, ab_norm_g, ab_w_in, sgu_norm_g, sgu_norm_b, sgu_w, sgu_bias, q_norm_g, k_norm_g, ab_w_out, cd_norm_g, cd_w_in, conv_c_w, conv_c_b, c_ln_g, c_ln_b, conv_d_w, cd_w_out, ffn_norm_g, ffn_w_gate, ffn_w_up, ffn_w_down, loss_target, m_ab_norm_g, m_ab_w_in, m_sgu_norm_g, m_sgu_norm_b, m_sgu_w, m_sgu_bias, m_q_norm_g, m_k_norm_g, m_ab_w_out, m_cd_norm_g, m_cd_w_in, m_conv_c_w, m_conv_c_b, m_c_ln_g, m_c_ln_b, m_conv_d_w, m_cd_w_out, m_ffn_norm_g, m_ffn_w_gate, m_ffn_w_up, m_ffn_w_down, v_ab_norm_g, v_ab_w_in, v_sgu_norm_g, v_sgu_norm_b, v_sgu_w, v_sgu_bias, v_q_norm_g, v_k_norm_g, v_ab_w_out, v_cd_norm_g, v_cd_w_in, v_conv_c_w, v_conv_c_b, v_c_ln_g, v_c_ln_b, v_conv_d_w, v_cd_w_out, v_ffn_norm_g, v_ffn_w_gate, v_ffn_w_up, v_ffn_w_down):
    w = dict(zip(WEIGHT_NAMES, (ab_norm_g, ab_w_in, sgu_norm_g, sgu_norm_b, sgu_w, sgu_bias, q_norm_g, k_norm_g, ab_w_out, cd_norm_g, cd_w_in, conv_c_w, conv_c_b, c_ln_g, c_ln_b, conv_d_w, cd_w_out, ffn_norm_g, ffn_w_gate, ffn_w_up, ffn_w_down)))
    m = dict(zip(WEIGHT_NAMES, (m_ab_norm_g, m_ab_w_in, m_sgu_norm_g, m_sgu_norm_b, m_sgu_w, m_sgu_bias, m_q_norm_g, m_k_norm_g, m_ab_w_out, m_cd_norm_g, m_cd_w_in, m_conv_c_w, m_conv_c_b, m_c_ln_g, m_c_ln_b, m_conv_d_w, m_cd_w_out, m_ffn_norm_g, m_ffn_w_gate, m_ffn_w_up, m_ffn_w_down)))
    v = dict(zip(WEIGHT_NAMES, (v_ab_norm_g, v_ab_w_in, v_sgu_norm_g, v_sgu_norm_b, v_sgu_w, v_sgu_bias, v_q_norm_g, v_k_norm_g, v_ab_w_out, v_cd_norm_g, v_cd_w_in, v_conv_c_w, v_conv_c_b, v_c_ln_g, v_c_ln_b, v_conv_d_w, v_cd_w_out, v_ffn_norm_g, v_ffn_w_gate, v_ffn_w_up, v_ffn_w_down)))
    me = _dev_index(*_my_place())

    small_local = _pack_rows([w["cd_norm_g"], w["conv_c_w"], w["conv_c_b"], w["c_ln_g"], w["c_ln_b"], w["conv_d_w"]], 24)
    r_ff = DFF // NDEV
    one = lambda a: (lambda S, j: S[a])
    slot = lambda b: (lambda L, s: L[b].at[s])
    slot2 = lambda b, part: (lambda L, s: L[b].at[part, s])
    shard = lambda a: (lambda S: S[a])

    def later(a):
        return lax.optimization_barrier((a, gathers[0]["token"]))[0]

    def layer_shards(layer):
        return (later(w["ffn_w_gate"][layer]).T.astype(BF16), later(w["ffn_w_up"][layer]).T.astype(BF16),
                later(w["ffn_w_down"][layer]).astype(BF16))

    def gathered(own):
        return _landing((NDEV,) + own.shape, BF16, [((me,), own)])

    def gathered2(a, b):
        return _landing((2, NDEV) + a.shape, BF16, [((0, me), a), ((1, me), b)])

    ab_in_s = w["ab_w_in"][0].T.astype(BF16)
    gathers = {0: _gather_start(
        "gather0_start", [ab_in_s, small_local],
        [gathered(ab_in_s), _landing((NDEV,) + small_local.shape, F32, [((me,), small_local)])],
        [(shard(0), slot(0)), (shard(1), slot(1))])}

    def chan(flat, lo, taps):
        return flat[:, lo:lo + taps * SHARD_C].reshape(NDEV, taps, SHARD_C).transpose(1, 0, 2).reshape(taps, 512)

    def fetch(stage, after):
        if stage == "ab_in":
            gathers[0] = _gather_forward("gather0_forward", gathers[0], after)
            l_ab_in, l_small = _gather_wait("gather0_wait", gathers[0], gathers[0]["token"])
            ab_out_s = later(w["ab_w_out"][0]).astype(BF16)
            gate0, up0, down0 = layer_shards(0)
            gathers[1] = _gather_start(
                "gather1_start", [ab_out_s, gate0, up0, down0],
                [gathered(ab_out_s), gathered2(gate0, up0), gathered(down0)],
                [(shard(0), slot(0)), (shard(1), slot2(1, 0)), (shard(2), slot2(1, 1)), (shard(3), slot(2))],
                dep=l_small)
            flat = l_small.reshape(NDEV, 24 * 128)
            return {
                "wt_ab_in": l_ab_in.reshape(AB_IN, D), "dep0": gathers[1]["token"],
                "cd_norm_g": flat[:, 0:128].reshape(1, D),
                "conv_c_w32": jnp.pad(chan(flat, 128, CONV_C_TAPS), ((0, 1), (0, 0))),
                "conv_c_b": chan(flat, 2112, 1), "c_ln_g": chan(flat, 2176, 1), "c_ln_b": chan(flat, 2240, 1),
                "conv_d_w8": jnp.pad(chan(flat, 2304, CONV_D_TAPS), ((0, 8 - CONV_D_TAPS), (0, 0))),
            }
        if stage == "attn0":
            cd_in_s, cd_out_s = later(w["cd_w_in"][0]).T.astype(BF16), later(w["cd_w_out"][0]).astype(BF16)
            gate1, up1, down1 = layer_shards(1)
            gathers[2] = _gather_start(
                "gather2_start", [cd_in_s, cd_out_s, gate1, up1, down1],
                [gathered(cd_in_s), gathered(cd_out_s), gathered2(gate1, up1), gathered(down1)],
                [(shard(0), slot(0)), (shard(1), slot(1)), (shard(2), slot2(2, 0)), (shard(3), slot2(2, 1)),
                 (shard(4), slot(3))], dep=after)
            return {"dep_attn1": gathers[2]["token"]}
        if stage == "attn1":
            gathers[1] = _gather_forward("gather1_forward", gathers[1], after)
            return {"dep_attn2": gathers[1]["token"]}
        if stage == "ab_out":
            l_out, l_ffn, l_down = _gather_wait("gather1_wait", gathers[1], after)
            return {"w_ab_out": l_out.reshape(D, D), "wt_ffn_in0": l_ffn.reshape(2 * DFF, D),
                    "w_ffn_down0": l_down.reshape(DFF, D)}
        if stage == "ffn_down0":
            gathers[2] = _gather_forward("gather2_forward", gathers[2], after)
            return {"dep_down0": gathers[2]["token"]}
        if stage == "cd_in":
            l_in, l_out, l_ffn, l_down = _gather_wait("gather2_wait", gathers[2], after)
            return {"wt_cd_in": l_in.reshape(CD_IN, D), "w_cd_out": l_out.reshape(D, D),
                    "wt_ffn_in1": l_ffn.reshape(2 * DFF, D), "w_ffn_down1": l_down.reshape(DFF, D)}
        return {}

    scatters = {}
    rides_with = {"w_ffn_down1": "wt_ffn_in1", "w_cd_out": "wt_cd_in", "w_ffn_down0": "wt_ffn_in0"}
    held = {}

    def on_grad(key, arr):
        if key in rides_with:
            held[rides_with[key]] = (key, arr)
            return None
        group = ([held.pop(key)] if key in held else []) + [(key, arr)]
        srcs, lands, items = [], [], []
        for n, (k, a) in enumerate(group):
            if k.startswith("wt_ffn_in"):
                src = a.reshape(2, NDEV, r_ff, D)
                own = lax.dynamic_slice_in_dim(src, me, 1, axis=1)
                lands.append(lax.dynamic_update_slice(lax.empty(src.shape, BF16), own, (0, me, 0, 0)))
                items += [((lambda S, j, n=n: S[n].at[0, j]), slot2(n, 0)), ((lambda S, j, n=n: S[n].at[1, j]), slot2(n, 1))]
            else:
                rows = a.shape[0] // NDEV
                src = a.reshape(NDEV, rows, D)
                own = lax.dynamic_index_in_dim(src, me, 0, keepdims=False)
                lands.append(_landing((1, NDEV, rows, D), BF16, [((0, me), own)]))
                items.append(((lambda S, j, n=n: S[n].at[j]), slot2(n, 0)))
            srcs.append(src)
        st = _exchange_start(f"scatter_{key}_start", srcs, lands, items)
        scatters[key] = (st, [k for k, _ in group])
        return st["token"]

    W = {
        "dep_first": gathers[0]["token"],
        "ab_norm_g": w["ab_norm_g"], "sgu_norm_g": w["sgu_norm_g"], "sgu_norm_b": w["sgu_norm_b"],
        "sgu_w": w["sgu_w"][0], "sgu_bias": w["sgu_bias"][0], "q_norm_g": w["q_norm_g"][0],
        "k_norm_g": w["k_norm_g"][0], "ffn_norm_g": w["ffn_norm_g"],
    }

    loss_cols, grad_x, G = _local_step(x[0], loss_target[0], W, fetch, on_grad)

    small_parts = [G["ab_norm_g"], G["sgu_norm_g"], G["sgu_norm_b"], G["sgu_w"], G["sgu_bias"], G["q_norm_g"],
                   G["k_norm_g"], G["cd_norm_g"], G["conv_c_w32"][:CONV_C_TAPS], G["conv_c_b"], G["c_ln_g"],
                   G["c_ln_b"], G["conv_d_w8"][:CONV_D_TAPS], G["ffn_norm_g0"], G["ffn_norm_g1"], loss_cols]
    sizes = [p.size for p in small_parts]
    small_rows = 720
    packed = _pack_rows(small_parts, small_rows)
    small = _exchange_start("small_start", [packed], [_landing((NDEV, small_rows, 128), F32, [((me,), packed)])],
                            [(one(0), slot(0))])
    landed = {}

    def wait_scatters(name, group_keys, after):
        res = _exchange_wait(name, [scatters[gk][0] for gk in group_keys], after)
        for gk, lands in zip(group_keys, res):
            landed.update(zip(scatters[gk][1], lands))

    wait_scatters("scatter_wait_early", ["wt_ffn_in1", "wt_cd_in", "wt_ffn_in0", "w_ab_out"], small["token"])

    grads, deltas, new_m, new_v = {}, {}, {}, {}
    done = []

    def put(name, res):
        grads[name], deltas[name], new_m[name], new_v[name] = res

    def adam(name, lands, sel, transposed):
        flip = (lambda a: jnp.swapaxes(a, 1, 2)) if transposed else (lambda a: a)
        res = _adam_stacked(lands, sel, flip(w[name]), flip(m[name]), flip(v[name]), f"adam_{name}")
        done.append(res[1])
        put(name, [flip(r) for r in res])

    ffn_in_lands = [landed["wt_ffn_in0"], landed["wt_ffn_in1"]]
    adam("cd_w_in", [landed["wt_cd_in"]], 0, True)
    adam("ffn_w_gate", ffn_in_lands, 0, True)
    adam("ffn_w_up", ffn_in_lands, 1, True)
    adam("cd_w_out", [landed["w_cd_out"]], 0, False)
    adam("ab_w_out", [landed["w_ab_out"]], 0, False)
    adam("ffn_w_down", [landed["w_ffn_down0"], landed["w_ffn_down1"]], 0, False)

    small_land = _exchange_wait("small_wait", [small], list(done))[0][0]
    red = _sum_slots(small_land).reshape(-1)
    offs = [0]
    for s in sizes:
        offs.append(offs[-1] + s)
    seg = [red[offs[i]:offs[i + 1]] for i in range(len(sizes))]
    loss = jnp.sum(seg[15])

    def own_channels(full, taps):
        return lax.dynamic_slice_in_dim(full.reshape(taps, 512), me * SHARD_C, SHARD_C, axis=1)

    g_small = {
        "ab_norm_g": seg[0].reshape(1, 1024), "sgu_norm_g": seg[1].reshape(1, 512), "sgu_norm_b": seg[2].reshape(1, 512),
        "sgu_w": seg[3].reshape(512, 128), "sgu_bias": seg[4].reshape(4, 128), "q_norm_g": seg[5].reshape(3, 64),
        "k_norm_g": seg[6].reshape(3, 64),
        "cd_norm_g": lax.dynamic_slice_in_dim(seg[7].reshape(1, D), me * (D // NDEV), D // NDEV, axis=1),
        "conv_c_w": own_channels(seg[8], CONV_C_TAPS), "conv_c_b": own_channels(seg[9], 1),
        "c_ln_g": own_channels(seg[10], 1), "c_ln_b": own_channels(seg[11], 1),
        "conv_d_w": own_channels(seg[12], CONV_D_TAPS),
        "ffn_norm_g": jnp.concatenate([seg[13].reshape(1, D), seg[14].reshape(1, D)], axis=0),
    }

    names2d = [n for n, _ in SMALL_2D]
    d_s, m_s, v_s = _adam_small([w[n].reshape(s) for n, s in SMALL_2D], [g_small[n] for n in names2d],
                                [m[n].reshape(s) for n, s in SMALL_2D], [v[n].reshape(s) for n, s in SMALL_2D])
    for i, n in enumerate(names2d):
        shape = w[n].shape
        grads[n], deltas[n] = g_small[n].reshape(shape), d_s[i].reshape(shape)
        new_m[n], new_v[n] = m_s[i].reshape(shape), v_s[i].reshape(shape)

    wait_scatters("scatter_wait_last", ["wt_ab_in"], d_s[0])
    adam("ab_w_in", [landed["wt_ab_in"]], 0, True)

    return (loss, grad_x[None], *[grads[n] for n in WEIGHT_NAMES], *[deltas[n] for n in WEIGHT_NAMES],
            *[new_m[n] for n in WEIGHT_NAMES], *[new_v[n] for n in WEIGHT_NAMES])
```

```python
import functools

import jax
import jax.numpy as jnp
import numpy as np
from jax import lax
from jax.experimental import pallas as pl
from jax.experimental.pallas import tpu as pltpu

F32 = jnp.float32
BF16 = jnp.bfloat16

T = 4096
D = 1024
NDEV = 8
EPS = 1e-6
NEG_INF = -1e30
DFF = 2816
AB_IN = 5632
CD_IN = 2560
HEAD = 64
PAIR = 128
NPAIR = 4
NBACK = 128
DIL_RATES = (1, 4, 16)
ROPE_HALF = 8
ROPE_THETA = 500000.0
CONV_C_TAPS = 31
CONV_D_TAPS = 3
HALO = 32
ATTN_BWD_UNROLL = 4

ADAM_LR = 0.001
ADAM_B1 = 0.9
ADAM_B2 = 0.999
ADAM_EPS = 1e-08
ADAM_WD = 0.01
ADAM_STEP = 10
ADAM_C1 = 1.0 / (1.0 - ADAM_B1 ** ADAM_STEP)
ADAM_C2 = 1.0 / (1.0 - ADAM_B2 ** ADAM_STEP)

VMEM_LIMIT_MB = 48
MESH = pl.DeviceIdType.MESH
HBM_SPEC = pl.BlockSpec(memory_space=pl.ANY)


def _cparams(ngrid, vmem_mb=VMEM_LIMIT_MB):
    return pltpu.CompilerParams(dimension_semantics=("arbitrary",) * ngrid,
                                vmem_limit_bytes=vmem_mb * 1024 * 1024)


def _pick(n, options):
    for o in options:
        if n % o == 0:
            return o
    raise ValueError(f"no tile for {n} in {options}")


def _sds(shape, dtype):
    return jax.ShapeDtypeStruct(shape, dtype)


def _sigmoid(x):
    return 1.0 / (1.0 + jnp.exp(-x))


def _sigmoid_bf16(x):
    return 0.5 * jnp.tanh(0.5 * x) + 0.5


def _gelu(z):
    return 0.5 * z * (1.0 + lax.erf(z * 0.7071067811865476))


def _gelu_grad(z):
    return 0.5 * (1.0 + lax.erf(z * 0.7071067811865476)) + z * jnp.exp(-0.5 * z * z) * 0.3989422804014327


def _mm_nt(a, wt, name, out_dtype=BF16, tm=2048, dep=None):
    M, K = a.shape
    N = wt.shape[0]
    tn = _pick(N, (512, 256))

    def body(a_ref, w_ref, *rest):
        o_ref = rest[-1]
        o_ref[...] = lax.dot_general(a_ref[...], w_ref[...], (((1,), (1,)), ((), ())),
                                     preferred_element_type=F32).astype(o_ref.dtype)

    in_specs = [pl.BlockSpec((tm, K), lambda i, j: (i, 0)), pl.BlockSpec((tn, K), lambda i, j: (j, 0))]
    args = [a, wt]
    if dep is not None:
        in_specs.append(HBM_SPEC)
        args.append(dep)
    return pl.pallas_call(
        body, name=name, grid=(M // tm, N // tn), in_specs=in_specs,
        out_specs=pl.BlockSpec((tm, tn), lambda i, j: (i, j)),
        out_shape=_sds((M, N), out_dtype), compiler_params=_cparams(2))(*args)


EPI_ROWS = 256


def _mm_nn(a, w, name, mode, resid, gain=None, tgt=None, dep=None, tm=512):
    M, K = a.shape
    N = w.shape[1]
    side = gain if mode == "rms" else tgt

    def body(a_ref, w_ref, resid_ref, side_ref, *rest):
        outs, acc = rest[-3 if mode == "rms" else -4:-1], rest[-1]
        i = pl.program_id(0)
        acc[...] = jnp.dot(a_ref[...], w_ref[...], preferred_element_type=F32)

        if mode == "loss":
            @pl.when(i == 0)
            def _():
                outs[2][...] = jnp.zeros_like(outs[2])

        for r0 in range(0, tm, EPI_ROWS):
            rows = slice(r0, r0 + EPI_ROWS)
            v = acc[rows, :] + resid_ref[rows, :]
            if mode == "rms":
                outs[0][rows, :] = v
                r = lax.rsqrt(jnp.mean(v * v, axis=-1, keepdims=True) + EPS)
                outs[1][rows, :] = (v * r * side_ref[...]).astype(BF16)
            else:
                d = v - side_ref[rows, :]
                outs[2][...] += jnp.sum(d * d, axis=0, keepdims=True) * (0.5 / N)
                dy = d * (1.0 / N)
                outs[0][rows, :] = dy
                outs[1][rows, :] = dy.astype(BF16)

    row = pl.BlockSpec((tm, N), lambda i: (i, 0))
    vec = pl.BlockSpec((1, N), lambda i: (0, 0))
    in_specs = [pl.BlockSpec((tm, K), lambda i: (i, 0)),
                pl.BlockSpec((K, N), lambda i: (0, 0), pipeline_mode=pl.Buffered(1)), row,
                vec if mode == "rms" else row]
    args = [a, w, resid, side]
    if dep is not None:
        in_specs.append(HBM_SPEC)
        args.append(dep)
    if mode == "rms":
        out_specs, out_shape = [row, row], [_sds((M, N), F32), _sds((M, N), BF16)]
    else:
        out_specs, out_shape = [row, row, vec], [_sds((M, N), F32), _sds((M, N), BF16), _sds((1, N), F32)]
    return pl.pallas_call(
        body, name=name, grid=(M // tm,), in_specs=in_specs, out_specs=out_specs, out_shape=out_shape,
        scratch_shapes=[pltpu.VMEM((tm, N), F32)], compiler_params=_cparams(1))(*args)


def _mm_dh_rms_bwd(a, w, x, gain, dres, name, dep=None, tm=512):
    parts = a.shape[0] if a.ndim == 3 else 1
    M, Kp = a.shape[-2], a.shape[-1]
    N = w.shape[1]
    nblk = M // tm
    assert nblk % 2 == 0

    def body(a_ref, w_ref, x_ref, g_ref, dres_ref, *rest):
        dx_ref, dxb_ref, dg_ref, acc0, acc1 = rest[-5:]
        i = pl.program_id(0)

        def matmul(acc):
            if parts == 1:
                acc[...] = jnp.dot(a_ref[...], w_ref[...], preferred_element_type=F32)
            else:
                d = jnp.dot(a_ref[0], w_ref[0:Kp, :], preferred_element_type=F32)
                for p in range(1, parts):
                    d = d + jnp.dot(a_ref[p], w_ref[p * Kp:(p + 1) * Kp, :], preferred_element_type=F32)
                acc[...] = d

        def finish(acc):
            for r0 in range(0, tm, EPI_ROWS // 2):
                rows = slice(r0, r0 + EPI_ROWS // 2)
                v = acc[rows, :]
                xf = x_ref[rows, :]
                r = lax.rsqrt(jnp.mean(xf * xf, axis=-1, keepdims=True) + EPS)
                xhat = xf * r
                dg_ref[...] += jnp.sum(v * xhat, axis=0, keepdims=True)
                dxh = v * g_ref[...]
                tot = dres_ref[rows, :] + r * (dxh - xhat * jnp.mean(dxh * xhat, axis=-1, keepdims=True))
                dx_ref[rows, :] = tot
                dxb_ref[rows, :] = tot.astype(BF16)

        @pl.when(i == 0)
        def _():
            dg_ref[...] = jnp.zeros_like(dg_ref)
            matmul(acc0)

        @pl.when((i > 0) & (i < nblk) & (i % 2 == 1))
        def _():
            matmul(acc1)
            finish(acc0)

        @pl.when((i > 0) & (i < nblk) & (i % 2 == 0))
        def _():
            matmul(acc0)
            finish(acc1)

        @pl.when(i == nblk)
        def _():
            finish(acc1)

    last = nblk - 1
    row = pl.BlockSpec((tm, N), lambda i: (jnp.maximum(i - 1, 0), 0))
    vec = pl.BlockSpec((1, N), lambda i: (0, 0))
    if a.ndim == 3:
        a_spec = pl.BlockSpec((parts, tm, Kp), lambda i: (0, jnp.minimum(i, last), 0))
    else:
        a_spec = pl.BlockSpec((tm, Kp), lambda i: (jnp.minimum(i, last), 0))
    w_spec = pl.BlockSpec((parts * Kp, N), lambda i: (0, 0), pipeline_mode=pl.Buffered(1))
    in_specs = [a_spec, w_spec, row, vec, row]
    args = [a, w, x, gain, dres]
    if dep is not None:
        in_specs.append(HBM_SPEC)
        args.append(dep)
    return pl.pallas_call(
        body, name=name, grid=(nblk + 1,), in_specs=in_specs, out_specs=[row, row, vec],
        out_shape=[_sds((M, N), F32), _sds((M, N), BF16), _sds((1, N), F32)],
        scratch_shapes=[pltpu.VMEM((tm, N), F32), pltpu.VMEM((tm, N), F32)], compiler_params=_cparams(1, 56))(*args)


def _mm_tn(a, b, name, out_dtype=BF16, tt=2048):
    parts = a.shape[0] if a.ndim == 3 else 1
    Tt, Mp = a.shape[-2], a.shape[-1]
    N = b.shape[1]
    tn = _pick(Mp, (1408, 1280, 1024, 512))
    jper = Mp // tn
    nt = Tt // tt

    def body(a_ref, b_ref, o_ref, acc):
        t = pl.program_id(1)

        @pl.when(t == 0)
        def _():
            acc[...] = jnp.zeros_like(acc)

        acc[...] += lax.dot_general(a_ref[...], b_ref[...], (((0,), (0,)), ((), ())),
                                    preferred_element_type=F32)

        @pl.when(t == nt - 1)
        def _():
            o_ref[...] = acc[...].astype(o_ref.dtype)

    if a.ndim == 3:
        a_spec = pl.BlockSpec((None, tt, tn), lambda j, t: (j // jper, t, j % jper))
    else:
        a_spec = pl.BlockSpec((tt, tn), lambda j, t: (t, j))
    return pl.pallas_call(
        body, name=name, grid=(parts * jper, nt),
        in_specs=[a_spec, pl.BlockSpec((tt, N), lambda j, t: (t, 0))],
        out_specs=pl.BlockSpec((tn, N), lambda j, t: (j, 0)),
        out_shape=_sds((parts * Mp, N), out_dtype), scratch_shapes=[pltpu.VMEM((tn, N), F32)],
        compiler_params=_cparams(2))(a, b)


def _ffn_in(h, wt_in, name, tm=2048, tn=256):
    nj = DFF // tn

    def body(h_ref, wg_ref, wu_ref, p_ref, act_ref):
        nt = (((1,), (1,)), ((), ()))
        g = lax.dot_general(h_ref[...], wg_ref[...], nt, preferred_element_type=F32).astype(BF16)
        u = lax.dot_general(h_ref[...], wu_ref[...], nt, preferred_element_type=F32).astype(BF16)
        p_ref[0] = g
        p_ref[1] = u
        act_ref[...] = g * _sigmoid_bf16(g) * u

    return pl.pallas_call(
        body, name=name, grid=(T // tm, nj),
        in_specs=[pl.BlockSpec((tm, D), lambda i, j: (i, 0)), pl.BlockSpec((tn, D), lambda i, j: (j, 0)),
                  pl.BlockSpec((tn, D), lambda i, j: (j + nj, 0))],
        out_specs=[pl.BlockSpec((2, tm, tn), lambda i, j: (0, i, j)), pl.BlockSpec((tm, tn), lambda i, j: (i, j))],
        out_shape=[_sds((2, T, DFF), BF16), _sds((T, DFF), BF16)], compiler_params=_cparams(2))(h, wt_in, wt_in)


def _ffn_dact(dyb, w_down, p3, name, tm=2048, tn=256, dep=None):
    def body(dy_ref, w_ref, p_ref, *rest):
        o_ref = rest[-1]
        da = lax.dot_general(dy_ref[...], w_ref[...], (((1,), (1,)), ((), ())),
                             preferred_element_type=F32).astype(BF16)
        g = p_ref[0]
        u = p_ref[1]
        sg = _sigmoid_bf16(g)
        gs = g * sg
        o_ref[0] = (da * u) * (sg + gs * (1.0 - sg))
        o_ref[1] = da * gs

    pspec = pl.BlockSpec((2, tm, tn), lambda i, j: (0, i, j))
    in_specs = [pl.BlockSpec((tm, D), lambda i, j: (i, 0)), pl.BlockSpec((tn, D), lambda i, j: (j, 0)), pspec]
    args = [dyb, w_down, p3]
    if dep is not None:
        in_specs.append(HBM_SPEC)
        args.append(dep)
    return pl.pallas_call(
        body, name=name, grid=(T // tm, DFF // tn), in_specs=in_specs, out_specs=pspec,
        out_shape=_sds((2, T, DFF), BF16), compiler_params=_cparams(2))(*args)


def _rms_fwd(x, g, name, tm=512, dep=None):
    def body(x_ref, g_ref, *rest):
        h_ref = rest[-1]
        xf = x_ref[...]
        r = lax.rsqrt(jnp.mean(xf * xf, axis=-1, keepdims=True) + EPS)
        h_ref[...] = (xf * r * g_ref[...]).astype(BF16)

    in_specs = [pl.BlockSpec((tm, D), lambda i: (i, 0)), pl.BlockSpec((1, D), lambda i: (0, 0))]
    args = [x, g]
    if dep is not None:
        in_specs.append(HBM_SPEC)
        args.append(dep)
    return pl.pallas_call(
        body, name=name, grid=(T // tm,), in_specs=in_specs, out_specs=pl.BlockSpec((tm, D), lambda i: (i, 0)),
        out_shape=_sds((T, D), BF16), compiler_params=_cparams(1))(*args)


def _tril_mask():
    r = lax.broadcasted_iota(jnp.int32, (128, 128), 0)
    c = lax.broadcasted_iota(jnp.int32, (128, 128), 1)
    return r >= c


def _mix_a_fwd(pab, sgu_g, sgu_b, sgu_w, sgu_bias3, tm=512):
    def body(zu_ref, zv_ref, g_ref, b_ref, w_ref, bias_ref, o_ref):
        u = _gelu(zu_ref[...].astype(F32))
        v = _gelu(zv_ref[...].astype(F32))
        mu = jnp.mean(v, axis=-1, keepdims=True)
        vc = v - mu
        rstd = lax.rsqrt(jnp.mean(vc * vc, axis=-1, keepdims=True) + EPS)
        vn = (vc * rstd * g_ref[...] + b_ref[...]).astype(BF16)
        tri = _tril_mask()
        for gi in range(4):
            wg = jnp.where(tri, w_ref[gi], 0.0).astype(BF16)
            bg = bias_ref[gi]
            for c in range(tm // 128):
                rs, cs = slice(c * 128, (c + 1) * 128), slice(gi * 128, (gi + 1) * 128)
                mixed = jnp.dot(wg, vn[rs, cs], preferred_element_type=F32) + bg
                o_ref[rs, cs] = (u[rs, cs] * mixed).astype(BF16)

    half = pl.BlockSpec((tm, 512), lambda i: (i, 0))
    return pl.pallas_call(
        body, name="mix_a_fwd", grid=(T // tm,),
        in_specs=[half, pl.BlockSpec((tm, 512), lambda i: (i, 1)),
                  pl.BlockSpec((1, 512), lambda i: (0, 0)), pl.BlockSpec((1, 512), lambda i: (0, 0)),
                  pl.BlockSpec((4, 128, 128), lambda i: (0, 0, 0)), pl.BlockSpec((4, 128, 1), lambda i: (0, 0, 0))],
        out_specs=half, out_shape=_sds((T, D), BF16), compiler_params=_cparams(1),
    )(pab, pab, sgu_g, sgu_b, sgu_w, sgu_bias3)


def _rope_tables():
    pos = np.arange(T, dtype=np.float32)
    inv_freq = np.float32(ROPE_THETA) ** (-np.arange(ROPE_HALF, dtype=np.float32) * np.float32(2.0 / (2 * ROPE_HALF)))
    ang = (pos[:, None] * inv_freq[None, :]).astype(np.float32)
    cos, sin = np.cos(ang), np.sin(ang)
    z8 = np.zeros((T, ROPE_HALF), np.float32)
    rest = np.zeros((T, HEAD - 2 * ROPE_HALF), np.float32)
    c64 = np.concatenate([cos, cos, rest + 1.0], axis=1)
    s1 = np.concatenate([z8, sin, rest], axis=1)
    s2 = np.concatenate([-sin, z8, rest], axis=1)
    return tuple(jnp.asarray(np.tile(t, (1, 2)).astype(np.float32)) for t in (c64, s1, s2))


def _lo_mask(shape):
    return lax.broadcasted_iota(jnp.int32, shape, 1) < HEAD


def _seg_mean(x, lo):
    s_all = jnp.sum(x, axis=-1, keepdims=True)
    s_lo = jnp.sum(jnp.where(lo, x, 0.0), axis=-1, keepdims=True)
    return jnp.where(lo, s_lo, s_all - s_lo) * (1.0 / HEAD)


def _head_blocks():
    r = lax.broadcasted_iota(jnp.int32, (PAIR, PAIR), 0) < HEAD
    c = lax.broadcasted_iota(jnp.int32, (PAIR, PAIR), 1) < HEAD
    return jnp.where(r == c, 1.0, 0.0).astype(BF16)


def _seg_mean_mxu(x, blocks):
    return jnp.dot(x.astype(BF16), blocks, preferred_element_type=F32) * (1.0 / HEAD)


def _rope(n, c, s1, s2):
    return n * c + pltpu.roll(n, ROPE_HALF, 1) * s1 + pltpu.roll(n, PAIR - ROPE_HALF, 1) * s2


def _rope_t(dy, c, s1, s2):
    return dy * c - pltpu.roll(dy, PAIR - ROPE_HALF, 1) * s2 - pltpu.roll(dy, ROPE_HALF, 1) * s1


def _prep_fwd(pab, qg, kg, tabs, tm=512):
    def body(p_ref, qg_ref, kg_ref, c_ref, s1_ref, s2_ref, *outs):
        blocks = _head_blocks()
        c, s1, s2 = c_ref[...], s1_ref[...], s2_ref[...]
        for g in range(3):
            qn_ref, kn_ref, v_ref = outs[3 * g:3 * g + 3]
            for p in range(NPAIR):
                for which, gains, dst in ((0, qg_ref, qn_ref), (1, kg_ref, kn_ref)):
                    col = (2 + 3 * which + g) * 512 + p * PAIR
                    xr = p_ref[:, col:col + PAIR].astype(F32)
                    rinv = lax.rsqrt(_seg_mean_mxu(xr * xr, blocks) + EPS)
                    outs[9 + 2 * g + which][p] = rinv.astype(BF16)
                    dst[p] = _rope(xr * rinv * gains[g:g + 1, :], c, s1, s2)
                col = (8 + g) * 512 + p * PAIR
                v_ref[p] = p_ref[:, col:col + PAIR].astype(F32)

    pm = pl.BlockSpec((NPAIR, tm, PAIR), lambda i: (0, i, 0))
    tab = pl.BlockSpec((tm, PAIR), lambda i: (i, 0))
    gain = pl.BlockSpec((3, PAIR), lambda i: (0, 0))
    res = pl.pallas_call(
        body, name="prep_fwd", grid=(T // tm,),
        in_specs=[pl.BlockSpec((tm, AB_IN), lambda i: (i, 0)), gain, gain, tab, tab, tab],
        out_specs=[pm] * 15, out_shape=[_sds((NPAIR, T, PAIR), F32)] * 9 + [_sds((NPAIR, T, PAIR), BF16)] * 6,
        compiler_params=_cparams(1))(pab, qg, kg, *tabs)
    return res[0:9], res[9:15]


def _res_index(it, rate):
    window = NBACK * rate
    b = it // rate
    rho = it % rate
    start = b * window + rho
    startp = jnp.maximum(start - window, rho)
    kmin = jnp.where(b > 0, 0, NBACK)
    return start, startp, kmin


def _rows(start, rate):
    if rate == 1:
        return pl.ds(pl.multiple_of(start, NBACK), NBACK)
    return pl.ds(start, NBACK, stride=rate)


def _band_bias():
    qs = lax.broadcasted_iota(jnp.int32, (2 * NBACK, 2 * NBACK), 0)
    kj = lax.broadcasted_iota(jnp.int32, (2 * NBACK, 2 * NBACK), 1)
    dist = (qs & (NBACK - 1)) + NBACK - kj
    both = (dist >= 0) & (dist <= NBACK)
    return jnp.where(both, 0.0, NEG_INF), jnp.where(both & (kj >= NBACK), 0.0, NEG_INF)


def _attn_fwd(qn, kn, v, rate, name, dep=None):
    def body(q_ref, k_ref, v_ref, *rest):
        o_ref, l_ref = rest[-2:]
        lo = _lo_mask((NBACK, PAIR))
        bias_all, bias_first = _band_bias()

        def step(it, carry):
            start, startp, kmin = _res_index(it, rate)
            q = q_ref[_rows(start, rate), :] * (HEAD ** -0.5)
            kcat = jnp.concatenate([k_ref[_rows(startp, rate), :], k_ref[_rows(start, rate), :]], axis=0).astype(BF16)
            vcat = jnp.concatenate([v_ref[_rows(startp, rate), :], v_ref[_rows(start, rate), :]], axis=0).astype(BF16)
            vcat1 = jnp.concatenate([vcat, jnp.ones((2 * NBACK, PAIR), BF16)], axis=1)
            q2 = jnp.concatenate([jnp.where(lo, q, 0.0), jnp.where(lo, 0.0, q)], axis=0).astype(BF16)
            s = lax.dot_general(q2, kcat, (((1,), (1,)), ((), ())), preferred_element_type=F32)
            s = s + jnp.where(kmin == 0, bias_all, bias_first)
            m = jnp.max(s, axis=-1, keepdims=True)
            ol = jnp.dot(jnp.exp(s - m).astype(BF16), vcat1, preferred_element_type=F32)
            o2 = ol[:, 0:PAIR] / ol[:, PAIR:]
            ls = m + jnp.log(ol[:, PAIR:])
            o_ref[_rows(start, rate), :] = jnp.where(lo, o2[0:NBACK], o2[NBACK:])
            l_ref[_rows(start, rate), :] = jnp.where(lo, ls[0:NBACK], ls[NBACK:])
            return carry

        lax.fori_loop(0, T // NBACK, step, 0, unroll=4)

    pm = pl.BlockSpec((None, T, PAIR), lambda p: (p, 0, 0))
    in_specs, args = [pm, pm, pm], [qn, kn, v]
    if dep is not None:
        in_specs.append(HBM_SPEC)
        args.append(dep)
    return pl.pallas_call(
        body, name=name, grid=(NPAIR,), in_specs=in_specs, out_specs=[pm, pm],
        out_shape=[_sds((NPAIR, T, PAIR), F32)] * 2, compiler_params=_cparams(1))(*args)


def _merge_fwd(cat_ab, outs, lses, tm=512):
    def body(cat_in, o0, o1, o2, l0, l1, l2, cat_ref, lse_ref):
        del cat_in
        for p in range(NPAIR):
            a0, a1, a2 = l0[p], l1[p], l2[p]
            m = jnp.maximum(jnp.maximum(a0, a1), a2)
            w0, w1, w2 = jnp.exp(a0 - m), jnp.exp(a1 - m), jnp.exp(a2 - m)
            s = w0 + w1 + w2
            b = (w0 * o0[p] + w1 * o1[p] + w2 * o2[p]) / s
            cat_ref[:, p * PAIR:(p + 1) * PAIR] = b.astype(BF16)
            lse_ref[p] = m + jnp.log(s)

    pm = pl.BlockSpec((NPAIR, tm, PAIR), lambda i: (0, i, 0))
    return pl.pallas_call(
        body, name="merge_fwd", grid=(T // tm,),
        in_specs=[pl.BlockSpec(memory_space=pl.ANY)] + [pm] * 6,
        out_specs=[pl.BlockSpec((tm, 512), lambda i: (i, 1)), pm],
        out_shape=[_sds((T, D), BF16), _sds((NPAIR, T, PAIR), F32)],
        input_output_aliases={0: 0}, compiler_params=_cparams(1))(cat_ab, *outs, *lses)


def _b_pre_bwd(dcat, cat, tm=512):
    def body(db_ref, b_ref, dbp_ref, e_ref):
        lo = _lo_mask((tm, PAIR))
        for p in range(NPAIR):
            db = db_ref[:, p * PAIR:(p + 1) * PAIR].astype(F32)
            b = b_ref[:, p * PAIR:(p + 1) * PAIR].astype(F32)
            dbp_ref[p] = db
            e_ref[p] = _seg_mean(db * b, lo) * float(HEAD)

    pm = pl.BlockSpec((NPAIR, tm, PAIR), lambda i: (0, i, 0))
    right = pl.BlockSpec((tm, 512), lambda i: (i, 1))
    return pl.pallas_call(
        body, name="b_pre_bwd", grid=(T // tm,), in_specs=[right, right], out_specs=[pm, pm],
        out_shape=[_sds((NPAIR, T, PAIR), F32)] * 2, compiler_params=_cparams(1))(dcat, cat)


def _attn_bwd(qn, kn, v, dbp, e, lse, rate, name):
    def body(q_ref, k_ref, v_ref, db_ref, e_ref, lse_ref, dq_ref, dk_ref, dv_ref):
        lo = _lo_mask((NBACK, PAIR))
        bias_all, bias_first = _band_bias()
        scale = HEAD ** -0.5
        nt = (((1,), (1,)), ((), ()))
        tn = (((0,), (0,)), ((), ()))
        window = NBACK * rate
        nblk = T // window

        def one(it, carry):
            dk_carry, dv_carry = carry
            rho = it // nblk
            b = it % nblk
            start = b * window + rho
            rq = _rows(start, rate)
            rp = _rows(jnp.maximum(start - window, rho), rate)
            q = q_ref[rq, :] * scale
            db = db_ref[rq, :]
            ev = e_ref[rq, :]
            ls = lse_ref[rq, :]
            kcat = jnp.concatenate([k_ref[rp, :], k_ref[rq, :]], axis=0).astype(BF16)
            vcat = jnp.concatenate([v_ref[rp, :], v_ref[rq, :]], axis=0).astype(BF16)
            q2 = jnp.concatenate([jnp.where(lo, q, 0.0), jnp.where(lo, 0.0, q)], axis=0).astype(BF16)
            db2 = jnp.concatenate([jnp.where(lo, db, 0.0), jnp.where(lo, 0.0, db)], axis=0).astype(BF16)
            ls2 = jnp.concatenate([ls[:, 0:1], ls[:, HEAD:HEAD + 1]], axis=0)
            ev2 = jnp.concatenate([ev[:, 0:1], ev[:, HEAD:HEAD + 1]], axis=0)
            s = lax.dot_general(q2, kcat, nt, preferred_element_type=F32)
            pt = jnp.exp(s + jnp.where(b > 0, bias_all, bias_first) - ls2)
            dp = lax.dot_general(db2, vcat, nt, preferred_element_type=F32)
            ds = (pt * (dp - ev2)).astype(BF16)
            dq2 = jnp.dot(ds, kcat, preferred_element_type=F32) * scale
            dkc = lax.dot_general(ds, q2, tn, preferred_element_type=F32)
            dvc = lax.dot_general(pt.astype(BF16), db2, tn, preferred_element_type=F32)
            dq_ref[rq, :] = jnp.where(lo, dq2[0:NBACK], dq2[NBACK:])
            dk_ref[rp, :] = dk_carry + dkc[0:NBACK]
            dk_ref[rq, :] = dkc[NBACK:]
            dv_ref[rp, :] = dv_carry + dvc[0:NBACK]
            dv_ref[rq, :] = dvc[NBACK:]
            return dkc[NBACK:], dvc[NBACK:]

        def step(i, carry):
            for u in range(ATTN_BWD_UNROLL):
                carry = one(i * ATTN_BWD_UNROLL + u, carry)
            return carry

        zero = jnp.zeros((NBACK, PAIR), F32)
        lax.fori_loop(0, T // NBACK // ATTN_BWD_UNROLL, step, (zero, zero))

    pm = pl.BlockSpec((None, T, PAIR), lambda p: (p, 0, 0))
    return pl.pallas_call(
        body, name=name, grid=(NPAIR,), in_specs=[pm] * 6, out_specs=[pm] * 3,
        out_shape=[_sds((NPAIR, T, PAIR), F32)] * 3, compiler_params=_cparams(1, 56))(qn, kn, v, dbp, e, lse)


def _ab_in_bwd(pab, dcat, sgu_g, sgu_b, sgu_w, sgu_bias3, qg, kg, tabs, dqkv, rinvs, tm=256):
    def body(p_ref, dcat_ref, g_ref, b_ref, w_ref, bias_ref, qg_ref, kg_ref, c_ref, s1_ref, s2_ref, *rest):
        dq_refs, rinv_refs = rest[0:9], rest[9:15]
        o_ref, dwm_ref, dbias_ref, dsg_ref, dsb_ref, dgain_ref = rest[15:]
        i = pl.program_id(0)

        @pl.when(i == 0)
        def _():
            dwm_ref[...] = jnp.zeros_like(dwm_ref)
            dbias_ref[...] = jnp.zeros_like(dbias_ref)
            dsg_ref[...] = jnp.zeros_like(dsg_ref)
            dsb_ref[...] = jnp.zeros_like(dsb_ref)
            dgain_ref[...] = jnp.zeros_like(dgain_ref)

        zu = p_ref[:, 0:512].astype(F32)
        zv = p_ref[:, 512:1024].astype(F32)
        u = _gelu(zu)
        v = _gelu(zv)
        mu = jnp.mean(v, axis=-1, keepdims=True)
        vc = v - mu
        rstd = lax.rsqrt(jnp.mean(vc * vc, axis=-1, keepdims=True) + EPS)
        xhat = vc * rstd
        vn = (xhat * g_ref[...] + b_ref[...]).astype(BF16)
        da = dcat_ref[...].astype(F32)
        tri = _tril_mask()
        du_parts = [[None] * 4 for _ in range(tm // 128)]
        dvn_parts = [[None] * 4 for _ in range(tm // 128)]
        for gi in range(4):
            wg = jnp.where(tri, w_ref[gi], 0.0).astype(BF16)
            bg = bias_ref[gi]
            for c in range(tm // 128):
                rs, cs = slice(c * 128, (c + 1) * 128), slice(gi * 128, (gi + 1) * 128)
                vblk = vn[rs, cs]
                mixed = jnp.dot(wg, vblk, preferred_element_type=F32) + bg
                dab = da[rs, cs]
                du_parts[c][gi] = dab * mixed
                dmixed = dab * u[rs, cs]
                dmb = dmixed.astype(BF16)
                dvn_parts[c][gi] = lax.dot_general(wg, dmb, (((0,), (0,)), ((), ())), preferred_element_type=F32)
                dwm = lax.dot_general(dmb, vblk, (((1,), (1,)), ((), ())), preferred_element_type=F32)
                dwm_ref[gi] += jnp.where(tri, dwm, 0.0)
                dbias_ref[gi] += dmixed
        du = jnp.concatenate([jnp.concatenate(r, axis=1) for r in du_parts], axis=0)
        dvn = jnp.concatenate([jnp.concatenate(r, axis=1) for r in dvn_parts], axis=0)
        dsg_ref[...] += jnp.sum(dvn * xhat, axis=0, keepdims=True)
        dsb_ref[...] += jnp.sum(dvn, axis=0, keepdims=True)
        dxh = dvn * g_ref[...]
        dv = rstd * (dxh - jnp.mean(dxh, axis=-1, keepdims=True)
                     - xhat * jnp.mean(dxh * xhat, axis=-1, keepdims=True))
        o_ref[:, 0:512] = (du * _gelu_grad(zu)).astype(BF16)
        o_ref[:, 512:1024] = (dv * _gelu_grad(zv)).astype(BF16)

        blocks = _head_blocks()
        c, s1, s2 = c_ref[...], s1_ref[...], s2_ref[...]
        for g in range(3):
            dq_ref, dk_ref, dv_ref = dq_refs[3 * g:3 * g + 3]
            for p in range(NPAIR):
                for which, gains, src in ((0, qg_ref, dq_ref), (1, kg_ref, dk_ref)):
                    col = (2 + 3 * which + g) * 512 + p * PAIR
                    xr = p_ref[:, col:col + PAIR].astype(F32)
                    rinv = rinv_refs[2 * g + which][p].astype(F32)
                    xh = xr * rinv
                    dn = _rope_t(src[p], c, s1, s2)
                    row = 2 * g + which
                    dgain_ref[row:row + 1, :] += jnp.sum(dn * xh, axis=0, keepdims=True)
                    dxh2 = dn * gains[g:g + 1, :]
                    dx = rinv * (dxh2 - xh * _seg_mean_mxu(dxh2 * xh, blocks))
                    o_ref[:, col:col + PAIR] = dx.astype(BF16)
                col = (8 + g) * 512 + p * PAIR
                o_ref[:, col:col + PAIR] = dv_ref[p].astype(BF16)

    pm = pl.BlockSpec((NPAIR, tm, PAIR), lambda i: (0, i, 0))
    tab = pl.BlockSpec((tm, PAIR), lambda i: (i, 0))
    gain = pl.BlockSpec((3, PAIR), lambda i: (0, 0))
    vec = pl.BlockSpec((1, 512), lambda i: (0, 0))
    full = pl.BlockSpec((tm, AB_IN), lambda i: (i, 0))
    w4 = pl.BlockSpec((4, 128, 128), lambda i: (0, 0, 0))
    return pl.pallas_call(
        body, name="ab_in_bwd", grid=(T // tm,),
        in_specs=[full, pl.BlockSpec((tm, 512), lambda i: (i, 0)), vec, vec, w4,
                  pl.BlockSpec((4, 128, 1), lambda i: (0, 0, 0)), gain, gain, tab, tab, tab] + [pm] * 15,
        out_specs=[full, w4, w4, vec, vec, pl.BlockSpec((8, PAIR), lambda i: (0, 0))],
        out_shape=[_sds((T, AB_IN), BF16), _sds((4, 128, 128), F32), _sds((4, 128, 128), F32),
                   _sds((1, 512), F32), _sds((1, 512), F32), _sds((8, PAIR), F32)],
        compiler_params=_cparams(1))(pab, dcat, sgu_g, sgu_b, sgu_w, sgu_bias3, qg, kg, *tabs, *dqkv, *rinvs)


def _ln_stats(x):
    mu = jnp.mean(x, axis=-1, keepdims=True)
    xc = x - mu
    rstd = lax.rsqrt(jnp.mean(xc * xc, axis=-1, keepdims=True) + EPS)
    return xc * rstd, rstd


CONV_RC = 64


def _shifted_copies(src, dst, tm):
    dst[0] = src[...]
    for b in range(1, 8):
        dst[b, 0:tm + HALO - 8, :] = src[pl.ds(b, tm + HALO - 8), :]


def _offsets_by_phase(first):
    groups = {}
    for o in range(first, first + CONV_C_TAPS):
        groups.setdefault(o % 8, []).append(o)
    return sorted(groups.items())


def _window(shifted, b8, base, offsets, lanes):
    rows = 8 * (max(offsets) // 8) + CONV_RC
    return shifted[b8, pl.ds(base, rows), lanes].reshape(rows // 8, 8, 128)


def _cd_fwd(pcd, cw, cb, lg, lb, dw, tm=512):
    per = tm // HALO

    def body(p_ref, h_ref, cw_ref, cb_ref, lg_ref, lb_ref, dw_ref, cat_ref, c0_ref, c1_ref, dd_ref, y_ref,
             buf, buf2, sb):
        i = pl.program_id(0)
        live = jnp.where(i > 0, 1.0, 0.0)
        a = p_ref[:, 0:512].astype(F32)
        gt = p_ref[:, 512:1024].astype(F32)
        gb = p_ref[:, 1024:1536].astype(F32)
        gc = p_ref[:, 1536:2048].astype(F32)
        hv = p_ref[:, 2048:2560].astype(F32)
        c0 = a * _sigmoid(gt)
        dd = gc * hv
        buf[0:HALO, :] = h_ref[:, 0:512].astype(F32) * _sigmoid(h_ref[:, 512:1024].astype(F32)) * live
        buf[HALO:, :] = c0
        buf2[0:HALO, :] = h_ref[:, 1536:2048].astype(F32) * h_ref[:, 2048:2560].astype(F32) * live
        buf2[HALO:, :] = dd
        c0_ref[...] = c0.astype(BF16)
        dd_ref[...] = dd.astype(BF16)
        _shifted_copies(buf, sb, tm)

        def conv_rows(r, carry):
            base = pl.multiple_of(r * CONV_RC, CONV_RC)
            for c in range(4):
                lanes = slice(c * 128, (c + 1) * 128)
                acc = jnp.broadcast_to(cb_ref[:, lanes], (CONV_RC // 8, 8, 128))
                for b8, offsets in _offsets_by_phase(HALO - (CONV_C_TAPS - 1)):
                    win = _window(sb, b8, base, offsets, lanes)
                    for o in offsets:
                        j = o - (HALO - (CONV_C_TAPS - 1))
                        acc = acc + cw_ref[8 * j:8 * j + 8, lanes] * win[o // 8:o // 8 + CONV_RC // 8]
                c1_ref[pl.ds(base, CONV_RC), lanes] = acc.reshape(CONV_RC, 128)
            return carry

        lax.fori_loop(0, tm // CONV_RC, conv_rows, 0)
        xhat, _ = _ln_stats(c1_ref[...])
        c2 = xhat * lg_ref[...] + lb_ref[...]
        y = jnp.zeros((tm, 512), F32)
        for j in range(CONV_D_TAPS):
            y = y + dw_ref[j:j + 1, :] * buf2[pl.ds(HALO - (CONV_D_TAPS - 1) + j, tm), :]
        cat_ref[:, 0:512] = (c2 * _sigmoid(c2)).astype(BF16)
        cat_ref[:, 512:1024] = (gb * y).astype(BF16)
        y_ref[...] = y.astype(BF16)

    half = pl.BlockSpec((tm, 512), lambda i: (i, 0))
    vec = pl.BlockSpec((1, 512), lambda i: (0, 0))
    return pl.pallas_call(
        body, name="cd_fwd", grid=(T // tm,),
        in_specs=[pl.BlockSpec((tm, CD_IN), lambda i: (i, 0)),
                  pl.BlockSpec((HALO, CD_IN), lambda i: (jnp.maximum(i * per - 1, 0), 0)),
                  pl.BlockSpec((8 * 32, 512), lambda i: (0, 0)), vec, vec, vec, pl.BlockSpec((8, 512), lambda i: (0, 0))],
        out_specs=[pl.BlockSpec((tm, D), lambda i: (i, 0)), half, half, half, half],
        out_shape=[_sds((T, D), BF16), _sds((T, 512), BF16), _sds((T, 512), F32), _sds((T, 512), BF16),
                   _sds((T, 512), BF16)],
        scratch_shapes=[pltpu.VMEM((HALO + tm, 512), F32), pltpu.VMEM((HALO + tm, 512), F32),
                        pltpu.VMEM((8, HALO + tm, 512), F32)],
        compiler_params=_cparams(1))(pcd, pcd, cw, cb, lg, lb, dw)


def _cd_bwd_pw(dcat, c1, pcd, y, lg, lb, tm=512):
    def body(dcat_ref, c1_ref, gb_ref, y_ref, lg_ref, lb_ref, dc1_ref, dy3_ref, dgb_ref, dlg_ref, dlb_ref, dcb_ref):
        i = pl.program_id(0)

        @pl.when(i == 0)
        def _():
            dlg_ref[...] = jnp.zeros_like(dlg_ref)
            dlb_ref[...] = jnp.zeros_like(dlb_ref)
            dcb_ref[...] = jnp.zeros_like(dcb_ref)

        dc = dcat_ref[:, 0:512].astype(F32)
        ddo = dcat_ref[:, 512:1024].astype(F32)
        xhat, rstd = _ln_stats(c1_ref[...])
        c2 = xhat * lg_ref[...] + lb_ref[...]
        sg = _sigmoid(c2)
        dc2 = dc * sg * (1.0 + c2 * (1.0 - sg))
        dlg_ref[...] += jnp.sum(dc2 * xhat, axis=0, keepdims=True)
        dlb_ref[...] += jnp.sum(dc2, axis=0, keepdims=True)
        dxh = dc2 * lg_ref[...]
        dc1 = rstd * (dxh - jnp.mean(dxh, axis=-1, keepdims=True)
                      - xhat * jnp.mean(dxh * xhat, axis=-1, keepdims=True))
        dcb_ref[...] += jnp.sum(dc1, axis=0, keepdims=True)
        dc1_ref[...] = dc1
        dgb_ref[...] = (ddo * y_ref[...].astype(F32)).astype(BF16)
        dy3_ref[...] = ddo * gb_ref[...].astype(F32)

    half = pl.BlockSpec((tm, 512), lambda i: (i, 0))
    vec = pl.BlockSpec((1, 512), lambda i: (0, 0))
    return pl.pallas_call(
        body, name="cd_bwd_pw", grid=(T // tm,),
        in_specs=[pl.BlockSpec((tm, D), lambda i: (i, 0)), half, pl.BlockSpec((tm, 512), lambda i: (i, 2)), half,
                  vec, vec],
        out_specs=[half, half, half, vec, vec, vec],
        out_shape=[_sds((T, 512), F32), _sds((T, 512), F32), _sds((T, 512), BF16),
                   _sds((1, 512), F32), _sds((1, 512), F32), _sds((1, 512), F32)],
        compiler_params=_cparams(1))(dcat, c1, pcd, y, lg, lb)


def _cd_bwd_conv(pcd, dc1, dy3, c0, dd, dgb, cw8, dw, tm=256):
    per = tm // HALO
    nblk = T // tm
    last32 = T // HALO - 1

    def body(p_ref, dc1_ref, dc1n_ref, dy3_ref, dy3n_ref, c0_ref, dd_ref, dgb_ref, cw_ref, dw_ref,
             o_ref, dcw_ref, ddw_ref, dbuf, d3buf, sd, dc0_buf):
        i = pl.program_id(0)
        has_next = jnp.where(i < nblk - 1, 1.0, 0.0)

        @pl.when(i == 0)
        def _():
            dcw_ref[...] = jnp.zeros_like(dcw_ref)
            ddw_ref[...] = jnp.zeros_like(ddw_ref)

        dbuf[0:tm, :] = dc1_ref[...]
        dbuf[tm:, :] = dc1n_ref[...] * has_next
        d3buf[0:tm, :] = dy3_ref[...]
        d3buf[tm:, :] = dy3n_ref[...] * has_next
        _shifted_copies(dbuf, sd, tm)
        n_tiles = tm // CONV_RC

        phases = _offsets_by_phase(0)

        def dc0_rows(r, carry):
            base = pl.multiple_of(r * CONV_RC, CONV_RC)
            for c in range(4):
                lanes = slice(c * 128, (c + 1) * 128)
                acc = jnp.zeros((CONV_RC // 8, 8, 128), F32)
                for b8, offsets in phases:
                    win = _window(sd, b8, base, offsets, lanes)
                    for o in offsets:
                        j = CONV_C_TAPS - 1 - o
                        acc = acc + cw_ref[8 * j:8 * j + 8, lanes] * win[o // 8:o // 8 + CONV_RC // 8]
                dc0_buf[pl.ds(base, CONV_RC), lanes] = acc.reshape(CONV_RC, 128)
            return carry

        lax.fori_loop(0, n_tiles, dc0_rows, 0)

        for c in range(4):
            lanes = slice(c * 128, (c + 1) * 128)
            for b8, offsets in phases:
                def dw_rows(r, accs, lanes=lanes, b8=b8, offsets=offsets):
                    base = pl.multiple_of(r * CONV_RC, CONV_RC)
                    xin = c0_ref[pl.ds(base, CONV_RC), lanes].astype(F32).reshape(CONV_RC // 8, 8, 128)
                    win = _window(sd, b8, base, offsets, lanes)
                    return tuple(acc + jnp.sum(xin * win[o // 8:o // 8 + CONV_RC // 8], axis=0)
                                 for acc, o in zip(accs, offsets))

                accs = lax.fori_loop(0, n_tiles, dw_rows, tuple(jnp.zeros((8, 128), F32) for _ in offsets))
                for acc, o in zip(accs, offsets):
                    j = CONV_C_TAPS - 1 - o
                    dcw_ref[j:j + 1, lanes] += jnp.sum(acc, axis=0, keepdims=True)

        dc0 = dc0_buf[...]
        ddin = dd_ref[...].astype(F32)
        ddd = jnp.zeros((tm, 512), F32)
        for j in range(CONV_D_TAPS):
            dy_shift = d3buf[pl.ds(CONV_D_TAPS - 1 - j, tm), :]
            ddd = ddd + dw_ref[j:j + 1, :] * dy_shift
            ddw_ref[j:j + 1, :] += jnp.sum(ddin * dy_shift, axis=0, keepdims=True)

        a = p_ref[:, 0:512].astype(F32)
        gt = p_ref[:, 512:1024].astype(F32)
        gc = p_ref[:, 1536:2048].astype(F32)
        hv = p_ref[:, 2048:2560].astype(F32)
        sg = _sigmoid(gt)
        o_ref[:, 0:512] = (dc0 * sg).astype(BF16)
        o_ref[:, 512:1024] = (dc0 * a * sg * (1.0 - sg)).astype(BF16)
        o_ref[:, 1024:1536] = dgb_ref[...]
        o_ref[:, 1536:2048] = (ddd * hv).astype(BF16)
        o_ref[:, 2048:2560] = (ddd * gc).astype(BF16)

    half = pl.BlockSpec((tm, 512), lambda i: (i, 0))
    nxt = pl.BlockSpec((HALO, 512), lambda i: (jnp.minimum((i + 1) * per, last32), 0))
    full = pl.BlockSpec((tm, CD_IN), lambda i: (i, 0))
    return pl.pallas_call(
        body, name="cd_bwd_conv", grid=(nblk,),
        in_specs=[full, half, nxt, half, nxt, half, half, half,
                  pl.BlockSpec((8 * 32, 512), lambda i: (0, 0)), pl.BlockSpec((8, 512), lambda i: (0, 0))],
        out_specs=[full, pl.BlockSpec((32, 512), lambda i: (0, 0)), pl.BlockSpec((8, 512), lambda i: (0, 0))],
        out_shape=[_sds((T, CD_IN), BF16), _sds((32, 512), F32), _sds((8, 512), F32)],
        scratch_shapes=[pltpu.VMEM((tm + HALO, 512), F32), pltpu.VMEM((tm + HALO, 512), F32),
                        pltpu.VMEM((8, tm + HALO, 512), F32), pltpu.VMEM((tm, 512), F32)],
        compiler_params=_cparams(1))(pcd, dc1, dc1, dy3, dy3, c0, dd, dgb, cw8, dw)


def _local_step(x, tgt, W, fetch=None, on_grad=None):
    W = dict(W)
    if fetch is None:
        fetch = lambda stage, after: {}
    if on_grad is None:
        on_grad = lambda key, arr: None
    tabs = _rope_tables()
    qg = jnp.tile(W["q_norm_g"], (1, 2))
    kg = jnp.tile(W["k_norm_g"], (1, 2))
    bias3 = W["sgu_bias"].reshape(4, 128, 1)
    G = {}

    h0 = _rms_fwd(x, W["ab_norm_g"], "rms_fwd_ab", dep=W.get("dep_first"))
    W.update(fetch("ab_in", h0))
    pab = _mm_nt(h0, W["wt_ab_in"], "mm_ab_in", dep=W.get("dep0"))
    cat_ab = _mix_a_fwd(pab, W["sgu_norm_g"], W["sgu_norm_b"], W["sgu_w"], bias3)
    qkv, rinvs = _prep_fwd(pab, qg, kg, tabs)
    outs, lses = [], []
    for g, rate in enumerate(DIL_RATES):
        o, l = _attn_fwd(qkv[3 * g], qkv[3 * g + 1], qkv[3 * g + 2], rate, f"attn_fwd_{g}", dep=W.get(f"dep_attn{g}"))
        outs.append(o)
        lses.append(l)
        W.update(fetch(f"attn{g}", o))
    cat_ab, lse = _merge_fwd(cat_ab, outs, lses)
    W.update(fetch("ab_out", lse))
    x1, h1 = _mm_nn(cat_ab, W["w_ab_out"], "mm_ab_out", mode="rms", resid=x, gain=W["ffn_norm_g"][0:1])
    pf0, act0 = _ffn_in(h1, W["wt_ffn_in0"], "ffn_in0")
    W.update(fetch("ffn_down0", act0))
    x2, h2 = _mm_nn(act0, W["w_ffn_down0"], "mm_ffn_down0", mode="rms", resid=x1, gain=W["cd_norm_g"],
                    dep=W.get("dep_down0"))
    W.update(fetch("cd_in", h2))
    pcd = _mm_nt(h2, W["wt_cd_in"], "mm_cd_in")
    cw8 = jnp.repeat(W["conv_c_w32"], 8, axis=0)
    cat_cd, c0, c1, dd, yv = _cd_fwd(pcd, cw8, W["conv_c_b"], W["c_ln_g"], W["c_ln_b"], W["conv_d_w8"])
    x3, h3 = _mm_nn(cat_cd, W["w_cd_out"], "mm_cd_out", mode="rms", resid=x2, gain=W["ffn_norm_g"][1:2])
    pf1, act1 = _ffn_in(h3, W["wt_ffn_in1"], "ffn_in1")
    dy, dyb, loss_cols = _mm_nn(act1, W["w_ffn_down1"], "mm_ffn_down1", mode="loss", resid=x3, tgt=tgt)

    def ffn_bwd(xin, h, pf, act, dres, dresb, layer):
        G[f"w_ffn_down{layer}"] = _mm_tn(act, dresb, f"mm_g_ffn_down{layer}")
        dep = on_grad(f"w_ffn_down{layer}", G[f"w_ffn_down{layer}"])
        dpf = _ffn_dact(dresb, W[f"w_ffn_down{layer}"], pf, f"ffn_dact{layer}", dep=dep)
        G[f"wt_ffn_in{layer}"] = _mm_tn(dpf, h, f"mm_g_ffn_in{layer}")
        dep = on_grad(f"wt_ffn_in{layer}", G[f"wt_ffn_in{layer}"])
        dx, dxb, G[f"ffn_norm_g{layer}"] = _mm_dh_rms_bwd(
            dpf, W[f"wt_ffn_in{layer}"], xin, W["ffn_norm_g"][layer:layer + 1], dres, f"mm_d_h_ffn{layer}", dep=dep)
        return dx, dxb

    dx3, dx3b = ffn_bwd(x3, h3, pf1, act1, dy, dyb, 1)

    G["w_cd_out"] = _mm_tn(cat_cd, dx3b, "mm_g_cd_out")
    dep = on_grad("w_cd_out", G["w_cd_out"])
    dcat_cd = _mm_nt(dx3b, W["w_cd_out"], "mm_d_cat_cd", dep=dep)
    dc1, dy3, dgb, G["c_ln_g"], G["c_ln_b"], G["conv_c_b"] = _cd_bwd_pw(dcat_cd, c1, pcd, yv, W["c_ln_g"], W["c_ln_b"])
    dpcd, G["conv_c_w32"], G["conv_d_w8"] = _cd_bwd_conv(pcd, dc1, dy3, c0, dd, dgb, cw8, W["conv_d_w8"])
    G["wt_cd_in"] = _mm_tn(dpcd, h2, "mm_g_cd_in")
    dep = on_grad("wt_cd_in", G["wt_cd_in"])
    dx2, dx2b, G["cd_norm_g"] = _mm_dh_rms_bwd(dpcd, W["wt_cd_in"], x2, W["cd_norm_g"], dx3, "mm_d_h_cd", dep=dep)

    dx1, dx1b = ffn_bwd(x1, h1, pf0, act0, dx2, dx2b, 0)

    G["w_ab_out"] = _mm_tn(cat_ab, dx1b, "mm_g_ab_out")
    dep = on_grad("w_ab_out", G["w_ab_out"])
    dcat_ab = _mm_nt(dx1b, W["w_ab_out"], "mm_d_cat_ab", dep=dep)
    dbp, e = _b_pre_bwd(dcat_ab, cat_ab)
    dqkv = []
    for g, rate in enumerate(DIL_RATES):
        dqkv += _attn_bwd(qkv[3 * g], qkv[3 * g + 1], qkv[3 * g + 2], dbp, e, lse, rate, f"attn_bwd_{g}")
    dpab, G["sgu_w"], dbias_part, G["sgu_norm_g"], G["sgu_norm_b"], dgain = _ab_in_bwd(
        pab, dcat_ab, W["sgu_norm_g"], W["sgu_norm_b"], W["sgu_w"], bias3, qg, kg, tabs, dqkv, rinvs)
    G["sgu_bias"] = jnp.sum(dbias_part, axis=-1)
    dgain = dgain[0:6, 0:HEAD] + dgain[0:6, HEAD:PAIR]
    G["q_norm_g"] = dgain[0::2]
    G["k_norm_g"] = dgain[1::2]
    G["wt_ab_in"] = _mm_tn(dpab, h0, "mm_g_ab_in")
    dep = on_grad("wt_ab_in", G["wt_ab_in"])
    grad_x, _, G["ab_norm_g"] = _mm_dh_rms_bwd(dpab, W["wt_ab_in"], x, W["ab_norm_g"], dx1, "mm_d_h_ab", dep=dep)
    return loss_cols, grad_x, G


def _my_place():
    return lax.axis_index("x"), lax.axis_index("y"), lax.axis_index("c")


def _dev_index(px, py, pc):
    return 4 * px + 2 * py + pc


def _flip(place, k):
    x, y, c = place
    return (1 - x if k & 4 else x, 1 - y if k & 2 else y, 1 - c if k & 1 else c)


def _landing(shape, dtype, own):
    buf = lax.empty(shape, dtype)
    for lead, part in own:
        buf = lax.dynamic_update_slice(buf, part.reshape((1,) * len(lead) + part.shape),
                                       tuple(lead) + (0,) * part.ndim)
    return buf


HBM_ONLY = pl.BlockSpec(memory_space=pltpu.HBM)
SEM_SPEC = pl.BlockSpec(memory_space=pltpu.SEMAPHORE)
IN_FLIGHT = pltpu.CompilerParams(has_side_effects=pltpu.SideEffectType.DATAFLOW_SIDE_EFFECTING)


def _in_hbm(a):
    return pltpu.with_memory_space_constraint(a, pltpu.HBM)


def _exchange_start(name, srcs, lands, items, dep=None):
    ns, nl, ni = len(srcs), len(lands), len(items)

    def body(*refs):
        S, L = refs[0:ns], refs[ns:ns + nl]
        first_out = ns + nl + (0 if dep is None else 1)
        send_sems, recv_sems, token = refs[first_out], refs[first_out + 1], refs[-1]
        me = _my_place()
        mi = _dev_index(*me)
        for i, (src, dst) in enumerate(items):
            for k in range(1, NDEV):
                peer = _flip(me, k)
                pltpu.make_async_remote_copy(
                    src_ref=src(S, _dev_index(*peer)), dst_ref=dst(L, mi), send_sem=send_sems.at[7 * i + k - 1],
                    recv_sem=recv_sems.at[7 * i + k - 1], device_id=peer, device_id_type=MESH).start()
        token[...] = jnp.zeros_like(token)

    thru = [pltpu.HBM(a.shape, a.dtype) for a in list(srcs) + list(lands)]
    args = [_in_hbm(a) for a in srcs] + [_in_hbm(a) for a in lands]
    in_specs = [HBM_ONLY] * (ns + nl)
    if dep is not None:
        args.append(dep)
        in_specs.append(HBM_SPEC)
    outs = pl.pallas_call(
        body, name=name, in_specs=in_specs,
        out_shape=(pltpu.SemaphoreType.DMA((7 * ni,)), pltpu.SemaphoreType.DMA((7 * ni,)), *thru, _sds((8, 128), F32)),
        out_specs=(SEM_SPEC, SEM_SPEC, *[HBM_ONLY] * (ns + nl), pl.BlockSpec(memory_space=pltpu.VMEM)),
        input_output_aliases={j: 2 + j for j in range(ns + nl)}, compiler_params=IN_FLIGHT)(*args)
    return dict(send=outs[0], recv=outs[1], srcs=list(outs[2:2 + ns]), lands=list(outs[2 + ns:2 + ns + nl]),
                token=outs[-1], items=items)


def _exchange_wait(name, states, after):
    after = list(after) if isinstance(after, (list, tuple)) else [after]
    counts = [(len(st["srcs"]), len(st["lands"]), len(st["items"])) for st in states]
    n_arrays = sum(c[0] + c[1] for c in counts)

    def body(*refs):
        me = _my_place()
        mi = _dev_index(*me)
        pos = 0
        sem_pos = n_arrays
        for st, (ns, nl, ni) in zip(states, counts):
            S, L = refs[pos:pos + ns], refs[pos + ns:pos + ns + nl]
            send_sems, recv_sems = refs[sem_pos], refs[sem_pos + 1]
            pos += ns + nl
            sem_pos += 2
            for i, (src, dst) in enumerate(st["items"]):
                for k in range(1, NDEV):
                    cp = pltpu.make_async_remote_copy(
                        src_ref=src(S, mi), dst_ref=dst(L, mi), send_sem=send_sems.at[7 * i + k - 1],
                        recv_sem=recv_sems.at[7 * i + k - 1], device_id=me, device_id_type=MESH)
                    cp.wait_send()
                    cp.wait_recv()

    arrays, sems = [], []
    for st in states:
        arrays += st["srcs"] + st["lands"]
        sems += [st["send"], st["recv"]]
    outs = pl.pallas_call(
        body, name=name, in_specs=[HBM_ONLY] * n_arrays + [SEM_SPEC] * len(sems) + [HBM_SPEC] * len(after),
        out_shape=tuple(pltpu.HBM(a.shape, a.dtype) for a in arrays), out_specs=tuple([HBM_ONLY] * n_arrays),
        input_output_aliases={j: j for j in range(n_arrays)}, compiler_params=IN_FLIGHT)(*arrays, *sems, *after)
    lands, pos = [], 0
    for ns, nl, _ in counts:
        lands.append(list(outs[pos + ns:pos + ns + nl]))
        pos += ns + nl
    return lands


def _place_and_neighbours():
    x, y, c = _my_place()
    return (x, y, c), (x, y, 1 - c), [(1 - x, y), (x, 1 - y), (1 - x, 1 - y)]


def _gather_start(name, srcs, lands, items, dep=None):
    ns, nl, ni = len(srcs), len(lands), len(items)

    def body(*refs):
        S, L = refs[0:ns], refs[ns:ns + nl]
        first_out = ns + nl + (0 if dep is None else 1)
        send_sems, recv_sems, token = refs[first_out], refs[first_out + 1], refs[-1]
        me, sib, chips = _place_and_neighbours()
        mi = _dev_index(*me)
        for i, (src, dst) in enumerate(items):
            for k, to in enumerate([sib] + [(*chip, me[2]) for chip in chips]):
                pltpu.make_async_remote_copy(
                    src_ref=src(S), dst_ref=dst(L, mi), send_sem=send_sems.at[4 * i + k],
                    recv_sem=recv_sems.at[4 * i + k], device_id=to, device_id_type=MESH).start()
        token[...] = jnp.zeros_like(token)

    thru = [pltpu.HBM(a.shape, a.dtype) for a in list(srcs) + list(lands)]
    args = [_in_hbm(a) for a in srcs] + [_in_hbm(a) for a in lands]
    in_specs = [HBM_ONLY] * (ns + nl)
    if dep is not None:
        args.append(dep)
        in_specs.append(HBM_SPEC)
    outs = pl.pallas_call(
        body, name=name, in_specs=in_specs,
        out_shape=(pltpu.SemaphoreType.DMA((4 * ni,)), pltpu.SemaphoreType.DMA((4 * ni,)), *thru, _sds((8, 128), F32)),
        out_specs=(SEM_SPEC, SEM_SPEC, *[HBM_ONLY] * (ns + nl), pl.BlockSpec(memory_space=pltpu.VMEM)),
        input_output_aliases={j: 2 + j for j in range(ns + nl)}, compiler_params=IN_FLIGHT)(*args)
    return dict(send=outs[0], recv=outs[1], srcs=list(outs[2:2 + ns]), lands=list(outs[2 + ns:2 + ns + nl]),
                token=outs[-1], items=items)


def _gather_forward(name, st, after):
    nl, ni = len(st["lands"]), len(st["items"])

    def body(*refs):
        L, recv_sems = refs[0:nl], refs[nl]
        fwd_send, fwd_recv, token = refs[2 * nl + 2], refs[2 * nl + 3], refs[-1]
        me, sib, chips = _place_and_neighbours()
        for i, (_, dst) in enumerate(st["items"]):
            for j, chip in enumerate(chips):
                blk = dst(L, _dev_index(*chip, me[2]))
                pltpu.make_async_remote_copy(
                    src_ref=blk, dst_ref=blk, send_sem=fwd_send.at[3 * i + j], recv_sem=recv_sems.at[4 * i + 1 + j],
                    device_id=me, device_id_type=MESH).wait_recv()
                pltpu.make_async_remote_copy(
                    src_ref=blk, dst_ref=blk, send_sem=fwd_send.at[3 * i + j], recv_sem=fwd_recv.at[3 * i + j],
                    device_id=sib, device_id_type=MESH).start()
        token[...] = jnp.zeros_like(token)

    outs = pl.pallas_call(
        body, name=name, in_specs=[HBM_ONLY] * nl + [SEM_SPEC, HBM_SPEC],
        out_shape=(*[pltpu.HBM(a.shape, a.dtype) for a in st["lands"]], pltpu.SemaphoreType.DMA((3 * ni,)),
                   pltpu.SemaphoreType.DMA((3 * ni,)), _sds((8, 128), F32)),
        out_specs=(*[HBM_ONLY] * nl, SEM_SPEC, SEM_SPEC, pl.BlockSpec(memory_space=pltpu.VMEM)),
        input_output_aliases={j: j for j in range(nl)}, compiler_params=IN_FLIGHT)(*st["lands"], st["recv"], after)
    return dict(st, lands=list(outs[0:nl]), fwd_send=outs[nl], fwd_recv=outs[nl + 1], token=outs[-1])


def _gather_wait(name, st, after):
    ns, nl, ni = len(st["srcs"]), len(st["lands"]), len(st["items"])

    def body(*refs):
        S, L = refs[0:ns], refs[ns:ns + nl]
        send_sems, recv_sems, fwd_send, fwd_recv = refs[ns + nl:ns + nl + 4]
        me, sib, chips = _place_and_neighbours()
        mi = _dev_index(*me)
        for i, (src, dst) in enumerate(st["items"]):
            mine = dst(L, mi)
            for k in range(4):
                pltpu.make_async_remote_copy(
                    src_ref=src(S), dst_ref=mine, send_sem=send_sems.at[4 * i + k], recv_sem=recv_sems.at[4 * i + k],
                    device_id=me, device_id_type=MESH).wait_send()
            pltpu.make_async_remote_copy(
                src_ref=src(S), dst_ref=mine, send_sem=send_sems.at[4 * i], recv_sem=recv_sems.at[4 * i],
                device_id=me, device_id_type=MESH).wait_recv()
            for j in range(3):
                cp = pltpu.make_async_remote_copy(
                    src_ref=mine, dst_ref=mine, send_sem=fwd_send.at[3 * i + j], recv_sem=fwd_recv.at[3 * i + j],
                    device_id=me, device_id_type=MESH)
                cp.wait_send()
                cp.wait_recv()

    arrays = st["srcs"] + st["lands"]
    outs = pl.pallas_call(
        body, name=name, in_specs=[HBM_ONLY] * (ns + nl) + [SEM_SPEC] * 4 + [HBM_SPEC],
        out_shape=tuple(pltpu.HBM(a.shape, a.dtype) for a in arrays), out_specs=tuple([HBM_ONLY] * (ns + nl)),
        input_output_aliases={j: j for j in range(ns + nl)},
        compiler_params=IN_FLIGHT)(*arrays, st["send"], st["recv"], st["fwd_send"], st["fwd_recv"], after)
    return list(outs[ns:ns + nl])


def _sum_slots(land):
    def body(l_ref, o_ref):
        acc = l_ref[0]
        for d in range(1, NDEV):
            acc = acc + l_ref[d]
        o_ref[...] = acc

    vm = pl.BlockSpec(memory_space=pltpu.VMEM)
    return pl.pallas_call(body, name="sum_small", out_shape=_sds(land.shape[1:], F32), in_specs=[vm], out_specs=vm)(land)


def _adam_math(w, g, m, v):
    m2 = ADAM_B1 * m + (1.0 - ADAM_B1) * g
    v2 = ADAM_B2 * v + (1.0 - ADAM_B2) * (g * g)
    delta = -ADAM_LR * ((m2 * ADAM_C1) / (jnp.sqrt(v2 * ADAM_C2) + ADAM_EPS) + ADAM_WD * w)
    return delta, m2, v2


def _adam_layer(land, sel, w, m, v, layer, name, prev=None, tc=512):
    R = land.shape[2]

    def body(l_ref, w_ref, m_ref, v_ref, *rest):
        g_out, d_out, m_out, v_out = rest[-4:]
        g = l_ref[0].astype(F32)
        for d in range(1, NDEV):
            g = g + l_ref[d].astype(F32)
        delta, m2, v2 = _adam_math(w_ref[...], g, m_ref[...], v_ref[...])
        g_out[...] = g
        d_out[...] = delta
        m_out[...] = m2
        v_out[...] = v2

    wspec = pl.BlockSpec((None, R, tc), lambda i: (layer, 0, i))
    in_specs = [pl.BlockSpec((None, NDEV, R, tc), lambda i: (sel, 0, 0, i)), wspec, wspec, wspec]
    args = [land, w, m, v]
    aliases = {}
    if prev is not None:
        in_specs += [HBM_SPEC] * 4
        args += list(prev)
        aliases = {4 + j: j for j in range(4)}
    return pl.pallas_call(
        body, name=name, grid=(D // tc,), in_specs=in_specs, out_specs=[wspec] * 4,
        out_shape=[_sds(w.shape, F32)] * 4, input_output_aliases=aliases, compiler_params=_cparams(1))(*args)


def _adam_stacked(lands, sel, w, m, v, name):
    res = None
    for layer, land in enumerate(lands):
        res = _adam_layer(land, sel, w, m, v, layer, f"{name}{layer}", prev=res)
    return res


def _adam_small(ws, gs, ms, vs):
    n = len(ws)

    def body(*refs):
        w_r, g_r, m_r, v_r = refs[0:n], refs[n:2 * n], refs[2 * n:3 * n], refs[3 * n:4 * n]
        d_o, m_o, v_o = refs[4 * n:5 * n], refs[5 * n:6 * n], refs[6 * n:7 * n]
        for i in range(n):
            delta, m2, v2 = _adam_math(w_r[i][...], g_r[i][...], m_r[i][...], v_r[i][...])
            d_o[i][...] = delta
            m_o[i][...] = m2
            v_o[i][...] = v2

    vm = pl.BlockSpec(memory_space=pltpu.VMEM)
    shapes = [_sds(w.shape, F32) for w in ws]
    outs = pl.pallas_call(body, name="adam_small", in_specs=[vm] * (4 * n), out_specs=[vm] * (3 * n),
                          out_shape=shapes * 3)(*ws, *gs, *ms, *vs)
    return outs[0:n], outs[n:2 * n], outs[2 * n:3 * n]


WEIGHT_NAMES = ("ab_norm_g", "ab_w_in", "sgu_norm_g", "sgu_norm_b", "sgu_w", "sgu_bias", "q_norm_g", "k_norm_g",
                "ab_w_out", "cd_norm_g", "cd_w_in", "conv_c_w", "conv_c_b", "c_ln_g", "c_ln_b", "conv_d_w",
                "cd_w_out", "ffn_norm_g", "ffn_w_gate", "ffn_w_up", "ffn_w_down")
SMALL_2D = (("ab_norm_g", (1, 1024)), ("sgu_norm_g", (1, 512)), ("sgu_norm_b", (1, 512)), ("sgu_w", (512, 128)),
            ("sgu_bias", (4, 128)), ("q_norm_g", (3, 64)), ("k_norm_g", (3, 64)), ("cd_norm_g", (1, 128)),
            ("conv_c_w", (31, 64)), ("conv_c_b", (1, 64)), ("c_ln_g", (1, 64)), ("c_ln_b", (1, 64)),
            ("conv_d_w", (3, 64)), ("ffn_norm_g", (2, 1024)))
SHARD_C = 64


def _pack_rows(parts, rows):
    flat = jnp.concatenate([p.reshape(-1) for p in parts])
    return jnp.pad(flat, (0, rows * 128 - flat.shape[0])).reshape(rows, 128)


def kernel(x, ab_norm_g, ab_w_in, sgu_norm_g, sgu_norm_b, sgu_w, sgu_bias, q_norm_g, k_norm_g, ab_w_out, cd_norm_g, cd_w_in, conv_c_w, conv_c_b, c_ln_g, c_ln_b, conv_d_w, cd_w_out, ffn_norm_g, ffn_w_gate, ffn_w_up, ffn_w_down, loss_target, m_ab_norm_g, m_ab_w_in, m_sgu_norm_g, m_sgu_norm_b, m_sgu_w, m_sgu_bias, m_q_norm_g, m_k_norm_g, m_ab_w_out, m_cd_norm_g, m_cd_w_in, m_conv_c_w, m_conv_c_b, m_c_ln_g, m_c_ln_b, m_conv_d_w, m_cd_w_out, m_ffn_norm_g, m_ffn_w_gate, m_ffn_w_up, m_ffn_w_down, v_ab_norm_g, v_ab_w_in, v_sgu_norm_g, v_sgu_norm_b, v_sgu_w, v_sgu_bias, v_q_norm_g, v_k_norm_g, v_ab_w_out, v_cd_norm_g, v_cd_w_in, v_conv_c_w, v_conv_c_b, v_c_ln_g, v_c_ln_b, v_conv_d_w, v_cd_w_out, v_ffn_norm_g, v_ffn_w_gate, v_ffn_w_up, v_ffn_w_down):
    w = dict(zip(WEIGHT_NAMES, (ab_norm_g, ab_w_in, sgu_norm_g, sgu_norm_b, sgu_w, sgu_bias, q_norm_g, k_norm_g, ab_w_out, cd_norm_g, cd_w_in, conv_c_w, conv_c_b, c_ln_g, c_ln_b, conv_d_w, cd_w_out, ffn_norm_g, ffn_w_gate, ffn_w_up, ffn_w_down)))
    m = dict(zip(WEIGHT_NAMES, (m_ab_norm_g, m_ab_w_in, m_sgu_norm_g, m_sgu_norm_b, m_sgu_w, m_sgu_bias, m_q_norm_g, m_k_norm_g, m_ab_w_out, m_cd_norm_g, m_cd_w_in, m_conv_c_w, m_conv_c_b, m_c_ln_g, m_c_ln_b, m_conv_d_w, m_cd_w_out, m_ffn_norm_g, m_ffn_w_gate, m_ffn_w_up, m_ffn_w_down)))
    v = dict(zip(WEIGHT_NAMES, (v_ab_norm_g, v_ab_w_in, v_sgu_norm_g, v_sgu_norm_b, v_sgu_w, v_sgu_bias, v_q_norm_g, v_k_norm_g, v_ab_w_out, v_cd_norm_g, v_cd_w_in, v_conv_c_w, v_conv_c_b, v_c_ln_g, v_c_ln_b, v_conv_d_w, v_cd_w_out, v_ffn_norm_g, v_ffn_w_gate, v_ffn_w_up, v_ffn_w_down)))
    me = _dev_index(*_my_place())

    small_local = _pack_rows([w["cd_norm_g"], w["conv_c_w"], w["conv_c_b"], w["c_ln_g"], w["c_ln_b"], w["conv_d_w"]], 24)
    r_ff = DFF // NDEV
    one = lambda a: (lambda S, j: S[a])
    slot = lambda b: (lambda L, s: L[b].at[s])
    slot2 = lambda b, part: (lambda L, s: L[b].at[part, s])
    shard = lambda a: (lambda S: S[a])

    def later(a):
        return lax.optimization_barrier((a, gathers[0]["token"]))[0]

    def layer_shards(layer):
        return (later(w["ffn_w_gate"][layer]).T.astype(BF16), later(w["ffn_w_up"][layer]).T.astype(BF16),
                later(w["ffn_w_down"][layer]).astype(BF16))

    def gathered(own):
        return _landing((NDEV,) + own.shape, BF16, [((me,), own)])

    def gathered2(a, b):
        return _landing((2, NDEV) + a.shape, BF16, [((0, me), a), ((1, me), b)])

    ab_in_s = w["ab_w_in"][0].T.astype(BF16)
    gathers = {0: _gather_start(
        "gather0_start", [ab_in_s, small_local],
        [gathered(ab_in_s), _landing((NDEV,) + small_local.shape, F32, [((me,), small_local)])],
        [(shard(0), slot(0)), (shard(1), slot(1))])}

    def chan(flat, lo, taps):
        return flat[:, lo:lo + taps * SHARD_C].reshape(NDEV, taps, SHARD_C).transpose(1, 0, 2).reshape(taps, 512)

    def fetch(stage, after):
        if stage == "ab_in":
            gathers[0] = _gather_forward("gather0_forward", gathers[0], after)
            l_ab_in, l_small = _gather_wait("gather0_wait", gathers[0], gathers[0]["token"])
            ab_out_s = later(w["ab_w_out"][0]).astype(BF16)
            gate0, up0, down0 = layer_shards(0)
            gathers[1] = _gather_start(
                "gather1_start", [ab_out_s, gate0, up0, down0],
                [gathered(ab_out_s), gathered2(gate0, up0), gathered(down0)],
                [(shard(0), slot(0)), (shard(1), slot2(1, 0)), (shard(2), slot2(1, 1)), (shard(3), slot(2))],
                dep=l_small)
            flat = l_small.reshape(NDEV, 24 * 128)
            return {
                "wt_ab_in": l_ab_in.reshape(AB_IN, D), "dep0": gathers[1]["token"],
                "cd_norm_g": flat[:, 0:128].reshape(1, D),
                "conv_c_w32": jnp.pad(chan(flat, 128, CONV_C_TAPS), ((0, 1), (0, 0))),
                "conv_c_b": chan(flat, 2112, 1), "c_ln_g": chan(flat, 2176, 1), "c_ln_b": chan(flat, 2240, 1),
                "conv_d_w8": jnp.pad(chan(flat, 2304, CONV_D_TAPS), ((0, 8 - CONV_D_TAPS), (0, 0))),
            }
        if stage == "attn0":
            cd_in_s, cd_out_s = later(w["cd_w_in"][0]).T.astype(BF16), later(w["cd_w_out"][0]).astype(BF16)
            gate1, up1, down1 = layer_shards(1)
            gathers[2] = _gather_start(
                "gather2_start", [cd_in_s, cd_out_s, gate1, up1, down1],
                [gathered(cd_in_s), gathered(cd_out_s), gathered2(gate1, up1), gathered(down1)],
                [(shard(0), slot(0)), (shard(1), slot(1)), (shard(2), slot2(2, 0)), (shard(3), slot2(2, 1)),
                 (shard(4), slot(3))], dep=after)
            return {"dep_attn1": gathers[2]["token"]}
        if stage == "attn1":
            gathers[1] = _gather_forward("gather1_forward", gathers[1], after)
            return {"dep_attn2": gathers[1]["token"]}
        if stage == "ab_out":
            l_out, l_ffn, l_down = _gather_wait("gather1_wait", gathers[1], after)
            return {"w_ab_out": l_out.reshape(D, D), "wt_ffn_in0": l_ffn.reshape(2 * DFF, D),
                    "w_ffn_down0": l_down.reshape(DFF, D)}
        if stage == "ffn_down0":
            gathers[2] = _gather_forward("gather2_forward", gathers[2], after)
            return {"dep_down0": gathers[2]["token"]}
        if stage == "cd_in":
            l_in, l_out, l_ffn, l_down = _gather_wait("gather2_wait", gathers[2], after)
            return {"wt_cd_in": l_in.reshape(CD_IN, D), "w_cd_out": l_out.reshape(D, D),
                    "wt_ffn_in1": l_ffn.reshape(2 * DFF, D), "w_ffn_down1": l_down.reshape(DFF, D)}
        return {}

    scatters = {}
    rides_with = {"w_ffn_down1": "wt_ffn_in1", "w_cd_out": "wt_cd_in", "w_ffn_down0": "wt_ffn_in0"}
    held = {}

    def on_grad(key, arr):
        if key in rides_with:
            held[rides_with[key]] = (key, arr)
            return None
        group = ([held.pop(key)] if key in held else []) + [(key, arr)]
        srcs, lands, items = [], [], []
        for n, (k, a) in enumerate(group):
            if k.startswith("wt_ffn_in"):
                src = a.reshape(2, NDEV, r_ff, D)
                own = lax.dynamic_slice_in_dim(src, me, 1, axis=1)
                lands.append(lax.dynamic_update_slice(lax.empty(src.shape, BF16), own, (0, me, 0, 0)))
                items += [((lambda S, j, n=n: S[n].at[0, j]), slot2(n, 0)), ((lambda S, j, n=n: S[n].at[1, j]), slot2(n, 1))]
            else:
                rows = a.shape[0] // NDEV
                src = a.reshape(NDEV, rows, D)
                own = lax.dynamic_index_in_dim(src, me, 0, keepdims=False)
                lands.append(_landing((1, NDEV, rows, D), BF16, [((0, me), own)]))
                items.append(((lambda S, j, n=n: S[n].at[j]), slot2(n, 0)))
            srcs.append(src)
        st = _exchange_start(f"scatter_{key}_start", srcs, lands, items)
        scatters[key] = (st, [k for k, _ in group])
        return st["token"]

    W = {
        "dep_first": gathers[0]["token"],
        "ab_norm_g": w["ab_norm_g"], "sgu_norm_g": w["sgu_norm_g"], "sgu_norm_b": w["sgu_norm_b"],
        "sgu_w": w["sgu_w"][0], "sgu_bias": w["sgu_bias"][0], "q_norm_g": w["q_norm_g"][0],
        "k_norm_g": w["k_norm_g"][0], "ffn_norm_g": w["ffn_norm_g"],
    }

    loss_cols, grad_x, G = _local_step(x[0], loss_target[0], W, fetch, on_grad)

    small_parts = [G["ab_norm_g"], G["sgu_norm_g"], G["sgu_norm_b"], G["sgu_w"], G["sgu_bias"], G["q_norm_g"],
                   G["k_norm_g"], G["cd_norm_g"], G["conv_c_w32"][:CONV_C_TAPS], G["conv_c_b"], G["c_ln_g"],
                   G["c_ln_b"], G["conv_d_w8"][:CONV_D_TAPS], G["ffn_norm_g0"], G["ffn_norm_g1"], loss_cols]
    sizes = [p.size for p in small_parts]
    small_rows = 720
    packed = _pack_rows(small_parts, small_rows)
    small = _exchange_start("small_start", [packed], [_landing((NDEV, small_rows, 128), F32, [((me,), packed)])],
                            [(one(0), slot(0))])
    landed = {}

    def wait_scatters(name, group_keys, after):
        res = _exchange_wait(name, [scatters[gk][0] for gk in group_keys], after)
        for gk, lands in zip(group_keys, res):
            landed.update(zip(scatters[gk][1], lands))

    wait_scatters("scatter_wait_early", ["wt_ffn_in1", "wt_cd_in", "wt_ffn_in0", "w_ab_out"], small["token"])

    grads, deltas, new_m, new_v = {}, {}, {}, {}
    done = []

    def put(name, res):
        grads[name], deltas[name], new_m[name], new_v[name] = res

    def adam(name, lands, sel, transposed):
        flip = (lambda a: jnp.swapaxes(a, 1, 2)) if transposed else (lambda a: a)
        res = _adam_stacked(lands, sel, flip(w[name]), flip(m[name]), flip(v[name]), f"adam_{name}")
        done.append(res[1])
        put(name, [flip(r) for r in res])

    ffn_in_lands = [landed["wt_ffn_in0"], landed["wt_ffn_in1"]]
    adam("cd_w_in", [landed["wt_cd_in"]], 0, True)
    adam("ffn_w_gate", ffn_in_lands, 0, True)
    adam("ffn_w_up", ffn_in_lands, 1, True)
    adam("cd_w_out", [landed["w_cd_out"]], 0, False)
    adam("ab_w_out", [landed["w_ab_out"]], 0, False)
    adam("ffn_w_down", [landed["w_ffn_down0"], landed["w_ffn_down1"]], 0, False)

    small_land = _exchange_wait("small_wait", [small], list(done))[0][0]
    red = _sum_slots(small_land).reshape(-1)
    offs = [0]
    for s in sizes:
        offs.append(offs[-1] + s)
    seg = [red[offs[i]:offs[i + 1]] for i in range(len(sizes))]
    loss = jnp.sum(seg[15])

    def own_channels(full, taps):
        return lax.dynamic_slice_in_dim(full.reshape(taps, 512), me * SHARD_C, SHARD_C, axis=1)

    g_small = {
        "ab_norm_g": seg[0].reshape(1, 1024), "sgu_norm_g": seg[1].reshape(1, 512), "sgu_norm_b": seg[2].reshape(1, 512),
        "sgu_w": seg[3].reshape(512, 128), "sgu_bias": seg[4].reshape(4, 128), "q_norm_g": seg[5].reshape(3, 64),
        "k_norm_g": seg[6].reshape(3, 64),
        "cd_norm_g": lax.dynamic_slice_in_dim(seg[7].reshape(1, D), me * (D // NDEV), D // NDEV, axis=1),
        "conv_c_w": own_channels(seg[8], CONV_C_TAPS), "conv_c_b": own_channels(seg[9], 1),
        "c_ln_g": own_channels(seg[10], 1), "c_ln_b": own_channels(seg[11], 1),
        "conv_d_w": own_channels(seg[12], CONV_D_TAPS),
        "ffn_norm_g": jnp.concatenate([seg[13].reshape(1, D), seg[14].reshape(1, D)], axis=0),
    }

    names2d = [n for n, _ in SMALL_2D]
    d_s, m_s, v_s = _adam_small([w[n].reshape(s) for n, s in SMALL_2D], [g_small[n] for n in names2d],
                                [m[n].reshape(s) for n, s in SMALL_2D], [v[n].reshape(s) for n, s in SMALL_2D])
    for i, n in enumerate(names2d):
        shape = w[n].shape
        grads[n], deltas[n] = g_small[n].reshape(shape), d_s[i].reshape(shape)
        new_m[n], new_v[n] = m_s[i].reshape(shape), v_s[i].reshape(shape)

    wait_scatters("scatter_wait_last", ["wt_ab_in"], d_s[0])
    adam("ab_w_in", [landed["wt_ab_in"]], 0, True)

    return (loss, grad_x[None], *[grads[n] for n in WEIGHT_NAMES], *[deltas[n] for n in WEIGHT_NAMES],
            *[new_m[n] for n in WEIGHT_NAMES], *[new_v[n] for n in WEIGHT_NAMES])
```

```python
import functools

import jax
import jax.numpy as jnp
import numpy as np
from jax import lax
from jax.experimental import pallas as pl
from jax.experimental.pallas import tpu as pltpu

F32 = jnp.float32
BF16 = jnp.bfloat16

T = 4096
D = 1024
NDEV = 8
EPS = 1e-6
NEG_INF = -1e30
DFF = 2816
AB_IN = 5632
CD_IN = 2560
HEAD = 64
PAIR = 128
NPAIR = 4
NBACK = 128
DIL_RATES = (1, 4, 16)
ROPE_HALF = 8
ROPE_THETA = 500000.0
CONV_C_TAPS = 31
CONV_D_TAPS = 3
HALO = 32
ATTN_BWD_UNROLL = 4

ADAM_LR = 0.001
ADAM_B1 = 0.9
ADAM_B2 = 0.999
ADAM_EPS = 1e-08
ADAM_WD = 0.01
ADAM_STEP = 10
ADAM_C1 = 1.0 / (1.0 - ADAM_B1 ** ADAM_STEP)
ADAM_C2 = 1.0 / (1.0 - ADAM_B2 ** ADAM_STEP)

VMEM_LIMIT_MB = 48
MESH = pl.DeviceIdType.MESH
HBM_SPEC = pl.BlockSpec(memory_space=pl.ANY)


def _cparams(ngrid, vmem_mb=VMEM_LIMIT_MB):
    return pltpu.CompilerParams(dimension_semantics=("arbitrary",) * ngrid,
                                vmem_limit_bytes=vmem_mb * 1024 * 1024)


def _pick(n, options):
    for o in options:
        if n % o == 0:
            return o
    raise ValueError(f"no tile for {n} in {options}")


def _sds(shape, dtype):
    return jax.ShapeDtypeStruct(shape, dtype)


def _sigmoid(x):
    return 1.0 / (1.0 + jnp.exp(-x))


def _sigmoid_bf16(x):
    return 0.5 * jnp.tanh(0.5 * x) + 0.5


def _gelu(z):
    return 0.5 * z * (1.0 + lax.erf(z * 0.7071067811865476))


def _gelu_grad(z):
    return 0.5 * (1.0 + lax.erf(z * 0.7071067811865476)) + z * jnp.exp(-0.5 * z * z) * 0.3989422804014327


def _mm_nt(a, wt, name, out_dtype=BF16, dep=None):
    M, K = a.shape
    N = wt.shape[0]
    tn = _pick(N, (512, 256))

    def body(a_ref, w_ref, *rest):
        o_ref = rest[-1]
        for r0 in range(0, M, 1024):
            o_ref[r0:r0 + 1024, :] = lax.dot_general(
                a_ref[r0:r0 + 1024, :], w_ref[...], (((1,), (1,)), ((), ())),
                preferred_element_type=F32).astype(o_ref.dtype)

    in_specs = [pl.BlockSpec((M, K), lambda j: (0, 0), pipeline_mode=pl.Buffered(1)),
                pl.BlockSpec((tn, K), lambda j: (j, 0))]
    args = [a, wt]
    if dep is not None:
        in_specs.append(HBM_SPEC)
        args.append(dep)
    return pl.pallas_call(
        body, name=name, grid=(N // tn,), in_specs=in_specs, out_specs=pl.BlockSpec((M, tn), lambda j: (0, j)),
        out_shape=_sds((M, N), out_dtype), compiler_params=_cparams(1))(*args)


EPI_ROWS = 256


def _mm_nn(a, w, name, mode, resid, gain=None, tgt=None, dep=None, tm=512):
    M, K = a.shape
    N = w.shape[1]
    side = gain if mode == "rms" else tgt

    def body(a_ref, w_ref, resid_ref, side_ref, *rest):
        outs, acc = rest[-3 if mode == "rms" else -4:-1], rest[-1]
        i = pl.program_id(0)
        acc[...] = jnp.dot(a_ref[...], w_ref[...], preferred_element_type=F32)

        if mode == "loss":
            @pl.when(i == 0)
            def _():
                outs[2][...] = jnp.zeros_like(outs[2])

        for r0 in range(0, tm, EPI_ROWS):
            rows = slice(r0, r0 + EPI_ROWS)
            v = acc[rows, :] + resid_ref[rows, :]
            if mode == "rms":
                outs[0][rows, :] = v
                r = lax.rsqrt(jnp.mean(v * v, axis=-1, keepdims=True) + EPS)
                outs[1][rows, :] = (v * r * side_ref[...]).astype(BF16)
            else:
                d = v - side_ref[rows, :]
                outs[2][...] += jnp.sum(d * d, axis=0, keepdims=True) * (0.5 / N)
                dy = d * (1.0 / N)
                outs[0][rows, :] = dy
                outs[1][rows, :] = dy.astype(BF16)

    row = pl.BlockSpec((tm, N), lambda i: (i, 0))
    vec = pl.BlockSpec((1, N), lambda i: (0, 0))
    in_specs = [pl.BlockSpec((tm, K), lambda i: (i, 0)),
                pl.BlockSpec((K, N), lambda i: (0, 0), pipeline_mode=pl.Buffered(1)), row,
                vec if mode == "rms" else row]
    args = [a, w, resid, side]
    if dep is not None:
        in_specs.append(HBM_SPEC)
        args.append(dep)
    if mode == "rms":
        out_specs, out_shape = [row, row], [_sds((M, N), F32), _sds((M, N), BF16)]
    else:
        out_specs, out_shape = [row, row, vec], [_sds((M, N), F32), _sds((M, N), BF16), _sds((1, N), F32)]
    return pl.pallas_call(
        body, name=name, grid=(M // tm,), in_specs=in_specs, out_specs=out_specs, out_shape=out_shape,
        scratch_shapes=[pltpu.VMEM((tm, N), F32)], compiler_params=_cparams(1))(*args)


def _mm_dh_rms_bwd(a, w, x, gain, dres, name, dep=None, tm=512, bf16_copy=True):
    parts = a.shape[0] if a.ndim == 3 else 1
    M, Kp = a.shape[-2], a.shape[-1]
    N = w.shape[1]
    nblk = M // tm
    assert nblk % 2 == 0

    def body(a_ref, w_ref, x_ref, g_ref, dres_ref, *rest):
        dg_ref, acc0, acc1 = rest[-3:]
        dx_ref = rest[-5] if bf16_copy else rest[-4]
        dxb_ref = rest[-4] if bf16_copy else None
        i = pl.program_id(0)

        def matmul(acc):
            if parts == 1:
                acc[...] = jnp.dot(a_ref[...], w_ref[...], preferred_element_type=F32)
            else:
                d = jnp.dot(a_ref[0], w_ref[0:Kp, :], preferred_element_type=F32)
                for p in range(1, parts):
                    d = d + jnp.dot(a_ref[p], w_ref[p * Kp:(p + 1) * Kp, :], preferred_element_type=F32)
                acc[...] = d

        def finish(acc):
            for r0 in range(0, tm, EPI_ROWS // 2):
                rows = slice(r0, r0 + EPI_ROWS // 2)
                v = acc[rows, :]
                xf = x_ref[rows, :]
                r = lax.rsqrt(jnp.mean(xf * xf, axis=-1, keepdims=True) + EPS)
                xhat = xf * r
                dg_ref[...] += jnp.sum(v * xhat, axis=0, keepdims=True)
                dxh = v * g_ref[...]
                tot = dres_ref[rows, :] + r * (dxh - xhat * jnp.mean(dxh * xhat, axis=-1, keepdims=True))
                dx_ref[rows, :] = tot
                if bf16_copy:
                    dxb_ref[rows, :] = tot.astype(BF16)

        @pl.when(i == 0)
        def _():
            dg_ref[...] = jnp.zeros_like(dg_ref)
            matmul(acc0)

        @pl.when((i > 0) & (i < nblk) & (i % 2 == 1))
        def _():
            matmul(acc1)
            finish(acc0)

        @pl.when((i > 0) & (i < nblk) & (i % 2 == 0))
        def _():
            matmul(acc0)
            finish(acc1)

        @pl.when(i == nblk)
        def _():
            finish(acc1)

    last = nblk - 1
    row = pl.BlockSpec((tm, N), lambda i: (jnp.maximum(i - 1, 0), 0))
    vec = pl.BlockSpec((1, N), lambda i: (0, 0))
    if a.ndim == 3:
        a_spec = pl.BlockSpec((parts, tm, Kp), lambda i: (0, jnp.minimum(i, last), 0))
    else:
        a_spec = pl.BlockSpec((tm, Kp), lambda i: (jnp.minimum(i, last), 0))
    w_spec = pl.BlockSpec((parts * Kp, N), lambda i: (0, 0), pipeline_mode=pl.Buffered(1))
    in_specs = [a_spec, w_spec, row, vec, row]
    args = [a, w, x, gain, dres]
    if dep is not None:
        in_specs.append(HBM_SPEC)
        args.append(dep)
    return pl.pallas_call(
        body, name=name, grid=(nblk + 1,), in_specs=in_specs,
        out_specs=[row, row, vec] if bf16_copy else [row, vec],
        out_shape=([_sds((M, N), F32), _sds((M, N), BF16), _sds((1, N), F32)] if bf16_copy
                   else [_sds((M, N), F32), _sds((1, N), F32)]),
        scratch_shapes=[pltpu.VMEM((tm, N), F32), pltpu.VMEM((tm, N), F32)], compiler_params=_cparams(1, 56))(*args)


def _mm_tn(a, b, name, out_dtype=BF16, tt=2048):
    parts = a.shape[0] if a.ndim == 3 else 1
    Tt, Mp = a.shape[-2], a.shape[-1]
    N = b.shape[1]
    tn = _pick(Mp, (1408, 1280, 1024, 512))
    jper = Mp // tn
    nt = Tt // tt

    def body(a_ref, b_ref, o_ref, acc):
        t = pl.program_id(1)

        @pl.when(t == 0)
        def _():
            acc[...] = jnp.zeros_like(acc)

        rows = pl.ds(pl.multiple_of(t * tt, tt), tt)
        acc[...] += lax.dot_general(a_ref[...], b_ref[rows, :], (((0,), (0,)), ((), ())),
                                    preferred_element_type=F32)

        @pl.when(t == nt - 1)
        def _():
            o_ref[...] = acc[...].astype(o_ref.dtype)

    if a.ndim == 3:
        a_spec = pl.BlockSpec((None, tt, tn), lambda j, t: (j // jper, t, j % jper))
    else:
        a_spec = pl.BlockSpec((tt, tn), lambda j, t: (t, j))
    return pl.pallas_call(
        body, name=name, grid=(parts * jper, nt),
        in_specs=[a_spec, pl.BlockSpec((Tt, N), lambda j, t: (0, 0), pipeline_mode=pl.Buffered(1))],
        out_specs=pl.BlockSpec((tn, N), lambda j, t: (j, 0)),
        out_shape=_sds((parts * Mp, N), out_dtype), scratch_shapes=[pltpu.VMEM((tn, N), F32)],
        compiler_params=_cparams(2))(a, b)


FFN_ROWS = 2048


def _ffn_in(h, wt_in, name, tn=256):
    nj = DFF // tn

    def body(h_ref, wg_ref, wu_ref, p_ref, act_ref):
        nt = (((1,), (1,)), ((), ()))
        for r0 in range(0, T, FFN_ROWS):
            rows = slice(r0, r0 + FFN_ROWS)
            g = lax.dot_general(h_ref[rows, :], wg_ref[...], nt, preferred_element_type=F32).astype(BF16)
            u = lax.dot_general(h_ref[rows, :], wu_ref[...], nt, preferred_element_type=F32).astype(BF16)
            p_ref[0, rows, :] = g
            p_ref[1, rows, :] = u
            act_ref[rows, :] = g * _sigmoid_bf16(g) * u

    return pl.pallas_call(
        body, name=name, grid=(nj,),
        in_specs=[pl.BlockSpec((T, D), lambda j: (0, 0), pipeline_mode=pl.Buffered(1)),
                  pl.BlockSpec((tn, D), lambda j: (j, 0)), pl.BlockSpec((tn, D), lambda j: (j + nj, 0))],
        out_specs=[pl.BlockSpec((2, T, tn), lambda j: (0, 0, j)), pl.BlockSpec((T, tn), lambda j: (0, j))],
        out_shape=[_sds((2, T, DFF), BF16), _sds((T, DFF), BF16)], compiler_params=_cparams(1))(h, wt_in, wt_in)


def _ffn_dact(dyb, w_down, p3, name, tn=256, dep=None):
    def body(dy_ref, w_ref, p_ref, *rest):
        o_ref = rest[-1]
        for r0 in range(0, T, FFN_ROWS):
            rows = slice(r0, r0 + FFN_ROWS)
            da = lax.dot_general(dy_ref[rows, :], w_ref[...], (((1,), (1,)), ((), ())),
                                 preferred_element_type=F32).astype(BF16)
            g = p_ref[0, rows, :]
            u = p_ref[1, rows, :]
            sg = _sigmoid_bf16(g)
            gs = g * sg
            o_ref[0, rows, :] = (da * u) * (sg + gs * (1.0 - sg))
            o_ref[1, rows, :] = da * gs

    pspec = pl.BlockSpec((2, T, tn), lambda j: (0, 0, j))
    in_specs = [pl.BlockSpec((T, D), lambda j: (0, 0), pipeline_mode=pl.Buffered(1)),
                pl.BlockSpec((tn, D), lambda j: (j, 0)), pspec]
    args = [dyb, w_down, p3]
    if dep is not None:
        in_specs.append(HBM_SPEC)
        args.append(dep)
    return pl.pallas_call(
        body, name=name, grid=(DFF // tn,), in_specs=in_specs, out_specs=pspec,
        out_shape=_sds((2, T, DFF), BF16), compiler_params=_cparams(1))(*args)


def _rms_fwd(x, g, name, tm=512, dep=None):
    def body(x_ref, g_ref, *rest):
        h_ref = rest[-1]
        xf = x_ref[...]
        r = lax.rsqrt(jnp.mean(xf * xf, axis=-1, keepdims=True) + EPS)
        h_ref[...] = (xf * r * g_ref[...]).astype(BF16)

    in_specs = [pl.BlockSpec((tm, D), lambda i: (i, 0)), pl.BlockSpec((1, D), lambda i: (0, 0))]
    args = [x, g]
    if dep is not None:
        in_specs.append(HBM_SPEC)
        args.append(dep)
    return pl.pallas_call(
        body, name=name, grid=(T // tm,), in_specs=in_specs, out_specs=pl.BlockSpec((tm, D), lambda i: (i, 0)),
        out_shape=_sds((T, D), BF16), compiler_params=_cparams(1))(*args)


def _tril_mask():
    r = lax.broadcasted_iota(jnp.int32, (128, 128), 0)
    c = lax.broadcasted_iota(jnp.int32, (128, 128), 1)
    return r >= c


def _mix_a_fwd(pab, sgu_g, sgu_b, sgu_w, sgu_bias3, tm=512):
    def body(zu_ref, zv_ref, g_ref, b_ref, w_ref, bias_ref, o_ref):
        u = _gelu(zu_ref[...].astype(F32))
        v = _gelu(zv_ref[...].astype(F32))
        mu = jnp.mean(v, axis=-1, keepdims=True)
        vc = v - mu
        rstd = lax.rsqrt(jnp.mean(vc * vc, axis=-1, keepdims=True) + EPS)
        vn = (vc * rstd * g_ref[...] + b_ref[...]).astype(BF16)
        tri = _tril_mask()
        for gi in range(4):
            wg = jnp.where(tri, w_ref[gi], 0.0).astype(BF16)
            bg = bias_ref[gi]
            for c in range(tm // 128):
                rs, cs = slice(c * 128, (c + 1) * 128), slice(gi * 128, (gi + 1) * 128)
                mixed = jnp.dot(wg, vn[rs, cs], preferred_element_type=F32) + bg
                o_ref[rs, cs] = (u[rs, cs] * mixed).astype(BF16)

    half = pl.BlockSpec((tm, 512), lambda i: (i, 0))
    return pl.pallas_call(
        body, name="mix_a_fwd", grid=(T // tm,),
        in_specs=[half, pl.BlockSpec((tm, 512), lambda i: (i, 1)),
                  pl.BlockSpec((1, 512), lambda i: (0, 0)), pl.BlockSpec((1, 512), lambda i: (0, 0)),
                  pl.BlockSpec((4, 128, 128), lambda i: (0, 0, 0)), pl.BlockSpec((4, 128, 1), lambda i: (0, 0, 0))],
        out_specs=half, out_shape=_sds((T, D), BF16), compiler_params=_cparams(1),
    )(pab, pab, sgu_g, sgu_b, sgu_w, sgu_bias3)


def _rope_tables():
    pos = np.arange(T, dtype=np.float32)
    inv_freq = np.float32(ROPE_THETA) ** (-np.arange(ROPE_HALF, dtype=np.float32) * np.float32(2.0 / (2 * ROPE_HALF)))
    ang = (pos[:, None] * inv_freq[None, :]).astype(np.float32)
    cos, sin = np.cos(ang), np.sin(ang)
    z8 = np.zeros((T, ROPE_HALF), np.float32)
    rest = np.zeros((T, HEAD - 2 * ROPE_HALF), np.float32)
    c64 = np.concatenate([cos, cos, rest + 1.0], axis=1)
    s1 = np.concatenate([z8, sin, rest], axis=1)
    s2 = np.concatenate([-sin, z8, rest], axis=1)
    return tuple(jnp.asarray(np.tile(t, (1, 2)).astype(np.float32)) for t in (c64, s1, s2))


def _lo_mask(shape):
    return lax.broadcasted_iota(jnp.int32, shape, 1) < HEAD


def _seg_mean(x, lo):
    s_all = jnp.sum(x, axis=-1, keepdims=True)
    s_lo = jnp.sum(jnp.where(lo, x, 0.0), axis=-1, keepdims=True)
    return jnp.where(lo, s_lo, s_all - s_lo) * (1.0 / HEAD)


def _head_blocks():
    r = lax.broadcasted_iota(jnp.int32, (PAIR, PAIR), 0) < HEAD
    c = lax.broadcasted_iota(jnp.int32, (PAIR, PAIR), 1) < HEAD
    return jnp.where(r == c, 1.0, 0.0).astype(BF16)


def _seg_mean_mxu(x, blocks):
    return jnp.dot(x.astype(BF16), blocks, preferred_element_type=F32) * (1.0 / HEAD)


def _rope(n, c, s1, s2):
    return n * c + pltpu.roll(n, ROPE_HALF, 1) * s1 + pltpu.roll(n, PAIR - ROPE_HALF, 1) * s2


def _rope_t(dy, c, s1, s2):
    return dy * c - pltpu.roll(dy, PAIR - ROPE_HALF, 1) * s2 - pltpu.roll(dy, ROPE_HALF, 1) * s1


def _prep_fwd(pab, qg, kg, tabs, tm=512):
    def body(p_ref, qg_ref, kg_ref, c_ref, s1_ref, s2_ref, *outs):
        blocks = _head_blocks()
        c, s1, s2 = c_ref[...], s1_ref[...], s2_ref[...]
        for g in range(3):
            qn_ref, kn_ref, v_ref = outs[3 * g:3 * g + 3]
            for p in range(NPAIR):
                for which, gains, dst in ((0, qg_ref, qn_ref), (1, kg_ref, kn_ref)):
                    col = (2 + 3 * which + g) * 512 + p * PAIR
                    xr = p_ref[:, col:col + PAIR].astype(F32)
                    rinv = lax.rsqrt(_seg_mean_mxu(xr * xr, blocks) + EPS)
                    outs[9 + 2 * g + which][p] = rinv.astype(BF16)
                    dst[p] = _rope(xr * rinv * gains[g:g + 1, :], c, s1, s2)
                col = (8 + g) * 512 + p * PAIR
                v_ref[p] = p_ref[:, col:col + PAIR].astype(F32)

    pm = pl.BlockSpec((NPAIR, tm, PAIR), lambda i: (0, i, 0))
    tab = pl.BlockSpec((tm, PAIR), lambda i: (i, 0))
    gain = pl.BlockSpec((3, PAIR), lambda i: (0, 0))
    res = pl.pallas_call(
        body, name="prep_fwd", grid=(T // tm,),
        in_specs=[pl.BlockSpec((tm, AB_IN), lambda i: (i, 0)), gain, gain, tab, tab, tab],
        out_specs=[pm] * 15, out_shape=[_sds((NPAIR, T, PAIR), F32)] * 9 + [_sds((NPAIR, T, PAIR), BF16)] * 6,
        compiler_params=_cparams(1))(pab, qg, kg, *tabs)
    return res[0:9], res[9:15]


def _res_index(it, rate):
    window = NBACK * rate
    b = it // rate
    rho = it % rate
    start = b * window + rho
    startp = jnp.maximum(start - window, rho)
    kmin = jnp.where(b > 0, 0, NBACK)
    return start, startp, kmin


def _rows(start, rate):
    if rate == 1:
        return pl.ds(pl.multiple_of(start, NBACK), NBACK)
    return pl.ds(start, NBACK, stride=rate)


def _band_bias():
    qs = lax.broadcasted_iota(jnp.int32, (2 * NBACK, 2 * NBACK), 0)
    kj = lax.broadcasted_iota(jnp.int32, (2 * NBACK, 2 * NBACK), 1)
    dist = (qs & (NBACK - 1)) + NBACK - kj
    both = (dist >= 0) & (dist <= NBACK)
    return jnp.where(both, 0.0, NEG_INF), jnp.where(both & (kj >= NBACK), 0.0, NEG_INF)


def _attn_fwd(qn, kn, v, rate, name, dep=None):
    def body(q_ref, k_ref, v_ref, *rest):
        o_ref, l_ref = rest[-2:]
        lo = _lo_mask((NBACK, PAIR))
        bias_all, bias_first = _band_bias()

        def step(it, carry):
            start, startp, kmin = _res_index(it, rate)
            q = q_ref[_rows(start, rate), :] * (HEAD ** -0.5)
            kcat = jnp.concatenate([k_ref[_rows(startp, rate), :], k_ref[_rows(start, rate), :]], axis=0).astype(BF16)
            vcat = jnp.concatenate([v_ref[_rows(startp, rate), :], v_ref[_rows(start, rate), :]], axis=0).astype(BF16)
            vcat1 = jnp.concatenate([vcat, jnp.ones((2 * NBACK, PAIR), BF16)], axis=1)
            q2 = jnp.concatenate([jnp.where(lo, q, 0.0), jnp.where(lo, 0.0, q)], axis=0).astype(BF16)
            s = lax.dot_general(q2, kcat, (((1,), (1,)), ((), ())), preferred_element_type=F32)
            s = s + jnp.where(kmin == 0, bias_all, bias_first)
            m = jnp.max(s, axis=-1, keepdims=True)
            ol = jnp.dot(jnp.exp(s - m).astype(BF16), vcat1, preferred_element_type=F32)
            o2 = ol[:, 0:PAIR] / ol[:, PAIR:]
            ls = m + jnp.log(ol[:, PAIR:])
            o_ref[_rows(start, rate), :] = jnp.where(lo, o2[0:NBACK], o2[NBACK:])
            l_ref[_rows(start, rate), :] = jnp.where(lo, ls[0:NBACK], ls[NBACK:])
            return carry

        lax.fori_loop(0, T // NBACK, step, 0, unroll=4)

    pm = pl.BlockSpec((None, T, PAIR), lambda p: (p, 0, 0))
    in_specs, args = [pm, pm, pm], [qn, kn, v]
    if dep is not None:
        in_specs.append(HBM_SPEC)
        args.append(dep)
    return pl.pallas_call(
        body, name=name, grid=(NPAIR,), in_specs=in_specs, out_specs=[pm, pm],
        out_shape=[_sds((NPAIR, T, PAIR), F32)] * 2, compiler_params=_cparams(1))(*args)


def _merge_fwd(cat_ab, outs, lses, tm=512):
    def body(cat_in, o0, o1, o2, l0, l1, l2, cat_ref, lse_ref):
        del cat_in
        for p in range(NPAIR):
            a0, a1, a2 = l0[p], l1[p], l2[p]
            m = jnp.maximum(jnp.maximum(a0, a1), a2)
            w0, w1, w2 = jnp.exp(a0 - m), jnp.exp(a1 - m), jnp.exp(a2 - m)
            s = w0 + w1 + w2
            b = (w0 * o0[p] + w1 * o1[p] + w2 * o2[p]) / s
            cat_ref[:, p * PAIR:(p + 1) * PAIR] = b.astype(BF16)
            lse_ref[p] = m + jnp.log(s)

    pm = pl.BlockSpec((NPAIR, tm, PAIR), lambda i: (0, i, 0))
    return pl.pallas_call(
        body, name="merge_fwd", grid=(T // tm,),
        in_specs=[pl.BlockSpec(memory_space=pl.ANY)] + [pm] * 6,
        out_specs=[pl.BlockSpec((tm, 512), lambda i: (i, 1)), pm],
        out_shape=[_sds((T, D), BF16), _sds((NPAIR, T, PAIR), F32)],
        input_output_aliases={0: 0}, compiler_params=_cparams(1))(cat_ab, *outs, *lses)


def _b_pre_bwd(dcat, cat, tm=512):
    def body(db_ref, b_ref, dbp_ref, e_ref):
        lo = _lo_mask((tm, PAIR))
        for p in range(NPAIR):
            db = db_ref[:, p * PAIR:(p + 1) * PAIR].astype(F32)
            b = b_ref[:, p * PAIR:(p + 1) * PAIR].astype(F32)
            dbp_ref[p] = db
            e_ref[p] = _seg_mean(db * b, lo) * float(HEAD)

    pm = pl.BlockSpec((NPAIR, tm, PAIR), lambda i: (0, i, 0))
    right = pl.BlockSpec((tm, 512), lambda i: (i, 1))
    return pl.pallas_call(
        body, name="b_pre_bwd", grid=(T // tm,), in_specs=[right, right], out_specs=[pm, pm],
        out_shape=[_sds((NPAIR, T, PAIR), F32)] * 2, compiler_params=_cparams(1))(dcat, cat)


def _attn_bwd(qn, kn, v, dbp, e, lse, rate, name):
    def body(q_ref, k_ref, v_ref, db_ref, e_ref, lse_ref, dq_ref, dk_ref, dv_ref):
        lo = _lo_mask((NBACK, PAIR))
        bias_all, bias_first = _band_bias()
        scale = HEAD ** -0.5
        nt = (((1,), (1,)), ((), ()))
        tn = (((0,), (0,)), ((), ()))
        window = NBACK * rate
        nblk = T // window

        def one(it, carry):
            dk_carry, dv_carry = carry
            rho = it // nblk
            b = it % nblk
            start = b * window + rho
            rq = _rows(start, rate)
            rp = _rows(jnp.maximum(start - window, rho), rate)
            q = q_ref[rq, :] * scale
            db = db_ref[rq, :]
            ev = e_ref[rq, :]
            ls = lse_ref[rq, :]
            kcat = jnp.concatenate([k_ref[rp, :], k_ref[rq, :]], axis=0).astype(BF16)
            vcat = jnp.concatenate([v_ref[rp, :], v_ref[rq, :]], axis=0).astype(BF16)
            q2 = jnp.concatenate([jnp.where(lo, q, 0.0), jnp.where(lo, 0.0, q)], axis=0).astype(BF16)
            db2 = jnp.concatenate([jnp.where(lo, db, 0.0), jnp.where(lo, 0.0, db)], axis=0).astype(BF16)
            ls2 = jnp.concatenate([ls[:, 0:1], ls[:, HEAD:HEAD + 1]], axis=0)
            ev2 = jnp.concatenate([ev[:, 0:1], ev[:, HEAD:HEAD + 1]], axis=0)
            s = lax.dot_general(q2, kcat, nt, preferred_element_type=F32)
            pt = jnp.exp(s + jnp.where(b > 0, bias_all, bias_first) - ls2)
            dp = lax.dot_general(db2, vcat, nt, preferred_element_type=F32)
            ds = (pt * (dp - ev2)).astype(BF16)
            dq2 = jnp.dot(ds, kcat, preferred_element_type=F32) * scale
            dkc = lax.dot_general(ds, q2, tn, preferred_element_type=F32)
            dvc = lax.dot_general(pt.astype(BF16), db2, tn, preferred_element_type=F32)
            dq_ref[rq, :] = jnp.where(lo, dq2[0:NBACK], dq2[NBACK:])
            dk_ref[rp, :] = dk_carry + dkc[0:NBACK]
            dk_ref[rq, :] = dkc[NBACK:]
            dv_ref[rp, :] = dv_carry + dvc[0:NBACK]
            dv_ref[rq, :] = dvc[NBACK:]
            return dkc[NBACK:], dvc[NBACK:]

        def step(i, carry):
            for u in range(ATTN_BWD_UNROLL):
                carry = one(i * ATTN_BWD_UNROLL + u, carry)
            return carry

        zero = jnp.zeros((NBACK, PAIR), F32)
        lax.fori_loop(0, T // NBACK // ATTN_BWD_UNROLL, step, (zero, zero))

    pm = pl.BlockSpec((None, T, PAIR), lambda p: (p, 0, 0))
    return pl.pallas_call(
        body, name=name, grid=(NPAIR,), in_specs=[pm] * 6, out_specs=[pm] * 3,
        out_shape=[_sds((NPAIR, T, PAIR), F32)] * 3, compiler_params=_cparams(1, 56))(qn, kn, v, dbp, e, lse)


def _ab_in_bwd(pab, dcat, sgu_g, sgu_b, sgu_w, sgu_bias3, qg, kg, tabs, dqkv, rinvs, tm=256):
    def body(p_ref, dcat_ref, g_ref, b_ref, w_ref, bias_ref, qg_ref, kg_ref, c_ref, s1_ref, s2_ref, *rest):
        dq_refs, rinv_refs = rest[0:9], rest[9:15]
        o_ref, dwm_ref, dbias_ref, dsg_ref, dsb_ref, dgain_ref = rest[15:]
        i = pl.program_id(0)

        @pl.when(i == 0)
        def _():
            dwm_ref[...] = jnp.zeros_like(dwm_ref)
            dbias_ref[...] = jnp.zeros_like(dbias_ref)
            dsg_ref[...] = jnp.zeros_like(dsg_ref)
            dsb_ref[...] = jnp.zeros_like(dsb_ref)
            dgain_ref[...] = jnp.zeros_like(dgain_ref)

        zu = p_ref[:, 0:512].astype(F32)
        zv = p_ref[:, 512:1024].astype(F32)
        u = _gelu(zu)
        v = _gelu(zv)
        mu = jnp.mean(v, axis=-1, keepdims=True)
        vc = v - mu
        rstd = lax.rsqrt(jnp.mean(vc * vc, axis=-1, keepdims=True) + EPS)
        xhat = vc * rstd
        vn = (xhat * g_ref[...] + b_ref[...]).astype(BF16)
        da = dcat_ref[...].astype(F32)
        tri = _tril_mask()
        du_parts = [[None] * 4 for _ in range(tm // 128)]
        dvn_parts = [[None] * 4 for _ in range(tm // 128)]
        for gi in range(4):
            wg = jnp.where(tri, w_ref[gi], 0.0).astype(BF16)
            bg = bias_ref[gi]
            for c in range(tm // 128):
                rs, cs = slice(c * 128, (c + 1) * 128), slice(gi * 128, (gi + 1) * 128)
                vblk = vn[rs, cs]
                mixed = jnp.dot(wg, vblk, preferred_element_type=F32) + bg
                dab = da[rs, cs]
                du_parts[c][gi] = dab * mixed
                dmixed = dab * u[rs, cs]
                dmb = dmixed.astype(BF16)
                dvn_parts[c][gi] = lax.dot_general(wg, dmb, (((0,), (0,)), ((), ())), preferred_element_type=F32)
                dwm = lax.dot_general(dmb, vblk, (((1,), (1,)), ((), ())), preferred_element_type=F32)
                dwm_ref[gi] += jnp.where(tri, dwm, 0.0)
                dbias_ref[gi] += dmixed
        du = jnp.concatenate([jnp.concatenate(r, axis=1) for r in du_parts], axis=0)
        dvn = jnp.concatenate([jnp.concatenate(r, axis=1) for r in dvn_parts], axis=0)
        dsg_ref[...] += jnp.sum(dvn * xhat, axis=0, keepdims=True)
        dsb_ref[...] += jnp.sum(dvn, axis=0, keepdims=True)
        dxh = dvn * g_ref[...]
        dv = rstd * (dxh - jnp.mean(dxh, axis=-1, keepdims=True)
                     - xhat * jnp.mean(dxh * xhat, axis=-1, keepdims=True))
        o_ref[:, 0:512] = (du * _gelu_grad(zu)).astype(BF16)
        o_ref[:, 512:1024] = (dv * _gelu_grad(zv)).astype(BF16)

        blocks = _head_blocks()
        c, s1, s2 = c_ref[...], s1_ref[...], s2_ref[...]
        for g in range(3):
            dq_ref, dk_ref, dv_ref = dq_refs[3 * g:3 * g + 3]
            for p in range(NPAIR):
                for which, gains, src in ((0, qg_ref, dq_ref), (1, kg_ref, dk_ref)):
                    col = (2 + 3 * which + g) * 512 + p * PAIR
                    xr = p_ref[:, col:col + PAIR].astype(F32)
                    rinv = rinv_refs[2 * g + which][p].astype(F32)
                    xh = xr * rinv
                    dn = _rope_t(src[p], c, s1, s2)
                    row = 2 * g + which
                    dgain_ref[row:row + 1, :] += jnp.sum(dn * xh, axis=0, keepdims=True)
                    dxh2 = dn * gains[g:g + 1, :]
                    dx = rinv * (dxh2 - xh * _seg_mean_mxu(dxh2 * xh, blocks))
                    o_ref[:, col:col + PAIR] = dx.astype(BF16)
                col = (8 + g) * 512 + p * PAIR
                o_ref[:, col:col + PAIR] = dv_ref[p].astype(BF16)

    pm = pl.BlockSpec((NPAIR, tm, PAIR), lambda i: (0, i, 0))
    tab = pl.BlockSpec((tm, PAIR), lambda i: (i, 0))
    gain = pl.BlockSpec((3, PAIR), lambda i: (0, 0))
    vec = pl.BlockSpec((1, 512), lambda i: (0, 0))
    full = pl.BlockSpec((tm, AB_IN), lambda i: (i, 0))
    w4 = pl.BlockSpec((4, 128, 128), lambda i: (0, 0, 0))
    return pl.pallas_call(
        body, name="ab_in_bwd", grid=(T // tm,),
        in_specs=[full, pl.BlockSpec((tm, 512), lambda i: (i, 0)), vec, vec, w4,
                  pl.BlockSpec((4, 128, 1), lambda i: (0, 0, 0)), gain, gain, tab, tab, tab] + [pm] * 15,
        out_specs=[full, w4, w4, vec, vec, pl.BlockSpec((8, PAIR), lambda i: (0, 0))],
        out_shape=[_sds((T, AB_IN), BF16), _sds((4, 128, 128), F32), _sds((4, 128, 128), F32),
                   _sds((1, 512), F32), _sds((1, 512), F32), _sds((8, PAIR), F32)],
        compiler_params=_cparams(1))(pab, dcat, sgu_g, sgu_b, sgu_w, sgu_bias3, qg, kg, *tabs, *dqkv, *rinvs)


def _ln_stats(x):
    mu = jnp.mean(x, axis=-1, keepdims=True)
    xc = x - mu
    rstd = lax.rsqrt(jnp.mean(xc * xc, axis=-1, keepdims=True) + EPS)
    return xc * rstd, rstd


CONV_RC = 64


def _shifted_copies(src, dst, tm):
    dst[0] = src[...]
    for b in range(1, 8):
        dst[b, 0:tm + HALO - 8, :] = src[pl.ds(b, tm + HALO - 8), :]


def _offsets_by_phase(first):
    groups = {}
    for o in range(first, first + CONV_C_TAPS):
        groups.setdefault(o % 8, []).append(o)
    return sorted(groups.items())


def _window(shifted, b8, base, offsets, lanes):
    rows = 8 * (max(offsets) // 8) + CONV_RC
    return shifted[b8, pl.ds(base, rows), lanes].reshape(rows // 8, 8, 128)


def _cd_fwd(pcd, cw, cb, lg, lb, dw, tm=512):
    per = tm // HALO

    def body(p_ref, h_ref, cw_ref, cb_ref, lg_ref, lb_ref, dw_ref, cat_ref, c0_ref, c1_ref, dd_ref, y_ref,
             buf, buf2, sb):
        i = pl.program_id(0)
        live = jnp.where(i > 0, 1.0, 0.0)
        a = p_ref[:, 0:512].astype(F32)
        gt = p_ref[:, 512:1024].astype(F32)
        gb = p_ref[:, 1024:1536].astype(F32)
        gc = p_ref[:, 1536:2048].astype(F32)
        hv = p_ref[:, 2048:2560].astype(F32)
        c0 = a * _sigmoid(gt)
        dd = gc * hv
        buf[0:HALO, :] = h_ref[:, 0:512].astype(F32) * _sigmoid(h_ref[:, 512:1024].astype(F32)) * live
        buf[HALO:, :] = c0
        buf2[0:HALO, :] = h_ref[:, 1536:2048].astype(F32) * h_ref[:, 2048:2560].astype(F32) * live
        buf2[HALO:, :] = dd
        c0_ref[...] = c0.astype(BF16)
        dd_ref[...] = dd.astype(BF16)
        _shifted_copies(buf, sb, tm)

        def conv_rows(r, carry):
            base = pl.multiple_of(r * CONV_RC, CONV_RC)
            for c in range(4):
                lanes = slice(c * 128, (c + 1) * 128)
                acc = jnp.broadcast_to(cb_ref[:, lanes], (CONV_RC // 8, 8, 128))
                for b8, offsets in _offsets_by_phase(HALO - (CONV_C_TAPS - 1)):
                    win = _window(sb, b8, base, offsets, lanes)
                    for o in offsets:
                        j = o - (HALO - (CONV_C_TAPS - 1))
                        acc = acc + cw_ref[8 * j:8 * j + 8, lanes] * win[o // 8:o // 8 + CONV_RC // 8]
                c1_ref[pl.ds(base, CONV_RC), lanes] = acc.reshape(CONV_RC, 128)
            return carry

        lax.fori_loop(0, tm // CONV_RC, conv_rows, 0)
        xhat, _ = _ln_stats(c1_ref[...])
        c2 = xhat * lg_ref[...] + lb_ref[...]
        y = jnp.zeros((tm, 512), F32)
        for j in range(CONV_D_TAPS):
            y = y + dw_ref[j:j + 1, :] * buf2[pl.ds(HALO - (CONV_D_TAPS - 1) + j, tm), :]
        cat_ref[:, 0:512] = (c2 * _sigmoid(c2)).astype(BF16)
        cat_ref[:, 512:1024] = (gb * y).astype(BF16)
        y_ref[...] = y.astype(BF16)

    half = pl.BlockSpec((tm, 512), lambda i: (i, 0))
    vec = pl.BlockSpec((1, 512), lambda i: (0, 0))
    return pl.pallas_call(
        body, name="cd_fwd", grid=(T // tm,),
        in_specs=[pl.BlockSpec((tm, CD_IN), lambda i: (i, 0)),
                  pl.BlockSpec((HALO, CD_IN), lambda i: (jnp.maximum(i * per - 1, 0), 0)),
                  pl.BlockSpec((8 * 32, 512), lambda i: (0, 0)), vec, vec, vec, pl.BlockSpec((8, 512), lambda i: (0, 0))],
        out_specs=[pl.BlockSpec((tm, D), lambda i: (i, 0)), half, half, half, half],
        out_shape=[_sds((T, D), BF16), _sds((T, 512), BF16), _sds((T, 512), F32), _sds((T, 512), BF16),
                   _sds((T, 512), BF16)],
        scratch_shapes=[pltpu.VMEM((HALO + tm, 512), F32), pltpu.VMEM((HALO + tm, 512), F32),
                        pltpu.VMEM((8, HALO + tm, 512), F32)],
        compiler_params=_cparams(1))(pcd, pcd, cw, cb, lg, lb, dw)


def _cd_bwd_pw(dcat, c1, pcd, y, lg, lb, tm=512):
    def body(dcat_ref, c1_ref, gb_ref, y_ref, lg_ref, lb_ref, dc1_ref, dy3_ref, dgb_ref, dlg_ref, dlb_ref, dcb_ref):
        i = pl.program_id(0)

        @pl.when(i == 0)
        def _():
            dlg_ref[...] = jnp.zeros_like(dlg_ref)
            dlb_ref[...] = jnp.zeros_like(dlb_ref)
            dcb_ref[...] = jnp.zeros_like(dcb_ref)

        dc = dcat_ref[:, 0:512].astype(F32)
        ddo = dcat_ref[:, 512:1024].astype(F32)
        xhat, rstd = _ln_stats(c1_ref[...])
        c2 = xhat * lg_ref[...] + lb_ref[...]
        sg = _sigmoid(c2)
        dc2 = dc * sg * (1.0 + c2 * (1.0 - sg))
        dlg_ref[...] += jnp.sum(dc2 * xhat, axis=0, keepdims=True)
        dlb_ref[...] += jnp.sum(dc2, axis=0, keepdims=True)
        dxh = dc2 * lg_ref[...]
        dc1 = rstd * (dxh - jnp.mean(dxh, axis=-1, keepdims=True)
                      - xhat * jnp.mean(dxh * xhat, axis=-1, keepdims=True))
        dcb_ref[...] += jnp.sum(dc1, axis=0, keepdims=True)
        dc1_ref[...] = dc1
        dgb_ref[...] = (ddo * y_ref[...].astype(F32)).astype(BF16)
        dy3_ref[...] = ddo * gb_ref[...].astype(F32)

    half = pl.BlockSpec((tm, 512), lambda i: (i, 0))
    vec = pl.BlockSpec((1, 512), lambda i: (0, 0))
    return pl.pallas_call(
        body, name="cd_bwd_pw", grid=(T // tm,),
        in_specs=[pl.BlockSpec((tm, D), lambda i: (i, 0)), half, pl.BlockSpec((tm, 512), lambda i: (i, 2)), half,
                  vec, vec],
        out_specs=[half, half, half, vec, vec, vec],
        out_shape=[_sds((T, 512), F32), _sds((T, 512), F32), _sds((T, 512), BF16),
                   _sds((1, 512), F32), _sds((1, 512), F32), _sds((1, 512), F32)],
        compiler_params=_cparams(1))(dcat, c1, pcd, y, lg, lb)


def _cd_bwd_conv(pcd, dc1, dy3, c0, dd, dgb, cw8, dw, tm=256):
    per = tm // HALO
    nblk = T // tm
    last32 = T // HALO - 1

    def body(p_ref, dc1_ref, dc1n_ref, dy3_ref, dy3n_ref, c0_ref, dd_ref, dgb_ref, cw_ref, dw_ref,
             o_ref, dcw_ref, ddw_ref, dbuf, d3buf, sd, dc0_buf):
        i = pl.program_id(0)
        has_next = jnp.where(i < nblk - 1, 1.0, 0.0)

        @pl.when(i == 0)
        def _():
            dcw_ref[...] = jnp.zeros_like(dcw_ref)
            ddw_ref[...] = jnp.zeros_like(ddw_ref)

        dbuf[0:tm, :] = dc1_ref[...]
        dbuf[tm:, :] = dc1n_ref[...] * has_next
        d3buf[0:tm, :] = dy3_ref[...]
        d3buf[tm:, :] = dy3n_ref[...] * has_next
        _shifted_copies(dbuf, sd, tm)
        n_tiles = tm // CONV_RC

        phases = _offsets_by_phase(0)

        def dc0_rows(r, carry):
            base = pl.multiple_of(r * CONV_RC, CONV_RC)
            for c in range(4):
                lanes = slice(c * 128, (c + 1) * 128)
                acc = jnp.zeros((CONV_RC // 8, 8, 128), F32)
                for b8, offsets in phases:
                    win = _window(sd, b8, base, offsets, lanes)
                    for o in offsets:
                        j = CONV_C_TAPS - 1 - o
                        acc = acc + cw_ref[8 * j:8 * j + 8, lanes] * win[o // 8:o // 8 + CONV_RC // 8]
                dc0_buf[pl.ds(base, CONV_RC), lanes] = acc.reshape(CONV_RC, 128)
            return carry

        lax.fori_loop(0, n_tiles, dc0_rows, 0)

        for c in range(4):
            lanes = slice(c * 128, (c + 1) * 128)
            for b8, offsets in phases:
                def dw_rows(r, accs, lanes=lanes, b8=b8, offsets=offsets):
                    base = pl.multiple_of(r * CONV_RC, CONV_RC)
                    xin = c0_ref[pl.ds(base, CONV_RC), lanes].astype(F32).reshape(CONV_RC // 8, 8, 128)
                    win = _window(sd, b8, base, offsets, lanes)
                    return tuple(acc + jnp.sum(xin * win[o // 8:o // 8 + CONV_RC // 8], axis=0)
                                 for acc, o in zip(accs, offsets))

                accs = lax.fori_loop(0, n_tiles, dw_rows, tuple(jnp.zeros((8, 128), F32) for _ in offsets))
                for acc, o in zip(accs, offsets):
                    j = CONV_C_TAPS - 1 - o
                    dcw_ref[j:j + 1, lanes] += jnp.sum(acc, axis=0, keepdims=True)

        dc0 = dc0_buf[...]
        ddin = dd_ref[...].astype(F32)
        ddd = jnp.zeros((tm, 512), F32)
        for j in range(CONV_D_TAPS):
            dy_shift = d3buf[pl.ds(CONV_D_TAPS - 1 - j, tm), :]
            ddd = ddd + dw_ref[j:j + 1, :] * dy_shift
            ddw_ref[j:j + 1, :] += jnp.sum(ddin * dy_shift, axis=0, keepdims=True)

        a = p_ref[:, 0:512].astype(F32)
        gt = p_ref[:, 512:1024].astype(F32)
        gc = p_ref[:, 1536:2048].astype(F32)
        hv = p_ref[:, 2048:2560].astype(F32)
        sg = _sigmoid(gt)
        o_ref[:, 0:512] = (dc0 * sg).astype(BF16)
        o_ref[:, 512:1024] = (dc0 * a * sg * (1.0 - sg)).astype(BF16)
        o_ref[:, 1024:1536] = dgb_ref[...]
        o_ref[:, 1536:2048] = (ddd * hv).astype(BF16)
        o_ref[:, 2048:2560] = (ddd * gc).astype(BF16)

    half = pl.BlockSpec((tm, 512), lambda i: (i, 0))
    nxt = pl.BlockSpec((HALO, 512), lambda i: (jnp.minimum((i + 1) * per, last32), 0))
    full = pl.BlockSpec((tm, CD_IN), lambda i: (i, 0))
    return pl.pallas_call(
        body, name="cd_bwd_conv", grid=(nblk,),
        in_specs=[full, half, nxt, half, nxt, half, half, half,
                  pl.BlockSpec((8 * 32, 512), lambda i: (0, 0)), pl.BlockSpec((8, 512), lambda i: (0, 0))],
        out_specs=[full, pl.BlockSpec((32, 512), lambda i: (0, 0)), pl.BlockSpec((8, 512), lambda i: (0, 0))],
        out_shape=[_sds((T, CD_IN), BF16), _sds((32, 512), F32), _sds((8, 512), F32)],
        scratch_shapes=[pltpu.VMEM((tm + HALO, 512), F32), pltpu.VMEM((tm + HALO, 512), F32),
                        pltpu.VMEM((8, tm + HALO, 512), F32), pltpu.VMEM((tm, 512), F32)],
        compiler_params=_cparams(1))(pcd, dc1, dc1, dy3, dy3, c0, dd, dgb, cw8, dw)


def _local_step(x, tgt, W, fetch=None, on_grad=None):
    W = dict(W)
    if fetch is None:
        fetch = lambda stage, after: {}
    if on_grad is None:
        on_grad = lambda key, arr: None
    tabs = _rope_tables()
    qg = jnp.tile(W["q_norm_g"], (1, 2))
    kg = jnp.tile(W["k_norm_g"], (1, 2))
    bias3 = W["sgu_bias"].reshape(4, 128, 1)
    G = {}

    h0 = _rms_fwd(x, W["ab_norm_g"], "rms_fwd_ab", dep=W.get("dep_first"))
    W.update(fetch("ab_in", h0))
    pab = _mm_nt(h0, W["wt_ab_in"], "mm_ab_in", dep=W.get("dep0"))
    cat_ab = _mix_a_fwd(pab, W["sgu_norm_g"], W["sgu_norm_b"], W["sgu_w"], bias3)
    qkv, rinvs = _prep_fwd(pab, qg, kg, tabs)
    outs, lses = [], []
    for g, rate in enumerate(DIL_RATES):
        o, l = _attn_fwd(qkv[3 * g], qkv[3 * g + 1], qkv[3 * g + 2], rate, f"attn_fwd_{g}", dep=W.get(f"dep_attn{g}"))
        outs.append(o)
        lses.append(l)
        W.update(fetch(f"attn{g}", o))
    cat_ab, lse = _merge_fwd(cat_ab, outs, lses)
    W.update(fetch("ab_out", lse))
    x1, h1 = _mm_nn(cat_ab, W["w_ab_out"], "mm_ab_out", mode="rms", resid=x, gain=W["ffn_norm_g"][0:1])
    pf0, act0 = _ffn_in(h1, W["wt_ffn_in0"], "ffn_in0")
    W.update(fetch("ffn_down0", act0))
    x2, h2 = _mm_nn(act0, W["w_ffn_down0"], "mm_ffn_down0", mode="rms", resid=x1, gain=W["cd_norm_g"],
                    dep=W.get("dep_down0"))
    W.update(fetch("cd_in", h2))
    pcd = _mm_nt(h2, W["wt_cd_in"], "mm_cd_in")
    cw8 = jnp.repeat(W["conv_c_w32"], 8, axis=0)
    cat_cd, c0, c1, dd, yv = _cd_fwd(pcd, cw8, W["conv_c_b"], W["c_ln_g"], W["c_ln_b"], W["conv_d_w8"])
    x3, h3 = _mm_nn(cat_cd, W["w_cd_out"], "mm_cd_out", mode="rms", resid=x2, gain=W["ffn_norm_g"][1:2])
    pf1, act1 = _ffn_in(h3, W["wt_ffn_in1"], "ffn_in1")
    dy, dyb, loss_cols = _mm_nn(act1, W["w_ffn_down1"], "mm_ffn_down1", mode="loss", resid=x3, tgt=tgt)

    def ffn_bwd(xin, h, pf, act, dres, dresb, layer):
        G[f"w_ffn_down{layer}"] = _mm_tn(act, dresb, f"mm_g_ffn_down{layer}")
        dep = on_grad(f"w_ffn_down{layer}", G[f"w_ffn_down{layer}"])
        dpf = _ffn_dact(dresb, W[f"w_ffn_down{layer}"], pf, f"ffn_dact{layer}", dep=dep)
        G[f"wt_ffn_in{layer}"] = _mm_tn(dpf, h, f"mm_g_ffn_in{layer}")
        dep = on_grad(f"wt_ffn_in{layer}", G[f"wt_ffn_in{layer}"])
        dx, dxb, G[f"ffn_norm_g{layer}"] = _mm_dh_rms_bwd(
            dpf, W[f"wt_ffn_in{layer}"], xin, W["ffn_norm_g"][layer:layer + 1], dres, f"mm_d_h_ffn{layer}", dep=dep)
        return dx, dxb

    dx3, dx3b = ffn_bwd(x3, h3, pf1, act1, dy, dyb, 1)

    G["w_cd_out"] = _mm_tn(cat_cd, dx3b, "mm_g_cd_out")
    dep = on_grad("w_cd_out", G["w_cd_out"])
    dcat_cd = _mm_nt(dx3b, W["w_cd_out"], "mm_d_cat_cd", dep=dep)
    dc1, dy3, dgb, G["c_ln_g"], G["c_ln_b"], G["conv_c_b"] = _cd_bwd_pw(dcat_cd, c1, pcd, yv, W["c_ln_g"], W["c_ln_b"])
    dpcd, G["conv_c_w32"], G["conv_d_w8"] = _cd_bwd_conv(pcd, dc1, dy3, c0, dd, dgb, cw8, W["conv_d_w8"])
    G["wt_cd_in"] = _mm_tn(dpcd, h2, "mm_g_cd_in")
    dep = on_grad("wt_cd_in", G["wt_cd_in"])
    dx2, dx2b, G["cd_norm_g"] = _mm_dh_rms_bwd(dpcd, W["wt_cd_in"], x2, W["cd_norm_g"], dx3, "mm_d_h_cd", dep=dep)

    dx1, dx1b = ffn_bwd(x1, h1, pf0, act0, dx2, dx2b, 0)

    G["w_ab_out"] = _mm_tn(cat_ab, dx1b, "mm_g_ab_out")
    dep = on_grad("w_ab_out", G["w_ab_out"])
    dcat_ab = _mm_nt(dx1b, W["w_ab_out"], "mm_d_cat_ab", dep=dep)
    dbp, e = _b_pre_bwd(dcat_ab, cat_ab)
    dqkv = []
    for g, rate in enumerate(DIL_RATES):
        dqkv += _attn_bwd(qkv[3 * g], qkv[3 * g + 1], qkv[3 * g + 2], dbp, e, lse, rate, f"attn_bwd_{g}")
    dpab, G["sgu_w"], dbias_part, G["sgu_norm_g"], G["sgu_norm_b"], dgain = _ab_in_bwd(
        pab, dcat_ab, W["sgu_norm_g"], W["sgu_norm_b"], W["sgu_w"], bias3, qg, kg, tabs, dqkv, rinvs)
    G["sgu_bias"] = jnp.sum(dbias_part, axis=-1)
    dgain = dgain[0:6, 0:HEAD] + dgain[0:6, HEAD:PAIR]
    G["q_norm_g"] = dgain[0::2]
    G["k_norm_g"] = dgain[1::2]
    G["wt_ab_in"] = _mm_tn(dpab, h0, "mm_g_ab_in")
    dep = on_grad("wt_ab_in", G["wt_ab_in"])
    grad_x, G["ab_norm_g"] = _mm_dh_rms_bwd(dpab, W["wt_ab_in"], x, W["ab_norm_g"], dx1, "mm_d_h_ab", dep=dep,
                                            bf16_copy=False)
    return loss_cols, grad_x, G


def _my_place():
    return lax.axis_index("x"), lax.axis_index("y"), lax.axis_index("c")


def _dev_index(px, py, pc):
    return 4 * px + 2 * py + pc


def _flip(place, k):
    x, y, c = place
    return (1 - x if k & 4 else x, 1 - y if k & 2 else y, 1 - c if k & 1 else c)


def _landing(shape, dtype, own):
    buf = lax.empty(shape, dtype)
    for lead, part in own:
        buf = lax.dynamic_update_slice(buf, part.reshape((1,) * len(lead) + part.shape),
                                       tuple(lead) + (0,) * part.ndim)
    return buf


HBM_ONLY = pl.BlockSpec(memory_space=pltpu.HBM)
SEM_SPEC = pl.BlockSpec(memory_space=pltpu.SEMAPHORE)
IN_FLIGHT = pltpu.CompilerParams(has_side_effects=pltpu.SideEffectType.DATAFLOW_SIDE_EFFECTING)


def _in_hbm(a):
    return pltpu.with_memory_space_constraint(a, pltpu.HBM)


def _exchange_start(name, srcs, lands, items, dep=None):
    ns, nl, ni = len(srcs), len(lands), len(items)

    def body(*refs):
        S, L = refs[0:ns], refs[ns:ns + nl]
        first_out = ns + nl + (0 if dep is None else 1)
        send_sems, recv_sems, token = refs[first_out], refs[first_out + 1], refs[-1]
        me = _my_place()
        mi = _dev_index(*me)
        for i, (src, dst) in enumerate(items):
            for k in range(1, NDEV):
                peer = _flip(me, k)
                pltpu.make_async_remote_copy(
                    src_ref=src(S, _dev_index(*peer)), dst_ref=dst(L, mi), send_sem=send_sems.at[7 * i + k - 1],
                    recv_sem=recv_sems.at[7 * i + k - 1], device_id=peer, device_id_type=MESH).start()
        token[...] = jnp.zeros_like(token)

    thru = [pltpu.HBM(a.shape, a.dtype) for a in list(srcs) + list(lands)]
    args = [_in_hbm(a) for a in srcs] + [_in_hbm(a) for a in lands]
    in_specs = [HBM_ONLY] * (ns + nl)
    if dep is not None:
        args.append(dep)
        in_specs.append(HBM_SPEC)
    outs = pl.pallas_call(
        body, name=name, in_specs=in_specs,
        out_shape=(pltpu.SemaphoreType.DMA((7 * ni,)), pltpu.SemaphoreType.DMA((7 * ni,)), *thru, _sds((8, 128), F32)),
        out_specs=(SEM_SPEC, SEM_SPEC, *[HBM_ONLY] * (ns + nl), pl.BlockSpec(memory_space=pltpu.VMEM)),
        input_output_aliases={j: 2 + j for j in range(ns + nl)}, compiler_params=IN_FLIGHT)(*args)
    return dict(send=outs[0], recv=outs[1], srcs=list(outs[2:2 + ns]), lands=list(outs[2 + ns:2 + ns + nl]),
                token=outs[-1], items=items)


def _exchange_wait(name, states, after):
    after = list(after) if isinstance(after, (list, tuple)) else [after]
    counts = [(len(st["srcs"]), len(st["lands"]), len(st["items"])) for st in states]
    n_arrays = sum(c[0] + c[1] for c in counts)

    def body(*refs):
        me = _my_place()
        mi = _dev_index(*me)
        pos = 0
        sem_pos = n_arrays
        for st, (ns, nl, ni) in zip(states, counts):
            S, L = refs[pos:pos + ns], refs[pos + ns:pos + ns + nl]
            send_sems, recv_sems = refs[sem_pos], refs[sem_pos + 1]
            pos += ns + nl
            sem_pos += 2
            for i, (src, dst) in enumerate(st["items"]):
                for k in range(1, NDEV):
                    cp = pltpu.make_async_remote_copy(
                        src_ref=src(S, mi), dst_ref=dst(L, mi), send_sem=send_sems.at[7 * i + k - 1],
                        recv_sem=recv_sems.at[7 * i + k - 1], device_id=me, device_id_type=MESH)
                    cp.wait_send()
                    cp.wait_recv()

    arrays, sems = [], []
    for st in states:
        arrays += st["srcs"] + st["lands"]
        sems += [st["send"], st["recv"]]
    outs = pl.pallas_call(
        body, name=name, in_specs=[HBM_ONLY] * n_arrays + [SEM_SPEC] * len(sems) + [HBM_SPEC] * len(after),
        out_shape=tuple(pltpu.HBM(a.shape, a.dtype) for a in arrays), out_specs=tuple([HBM_ONLY] * n_arrays),
        input_output_aliases={j: j for j in range(n_arrays)}, compiler_params=IN_FLIGHT)(*arrays, *sems, *after)
    lands, pos = [], 0
    for ns, nl, _ in counts:
        lands.append(list(outs[pos + ns:pos + ns + nl]))
        pos += ns + nl
    return lands


def _place_and_neighbours():
    x, y, c = _my_place()
    return (x, y, c), (x, y, 1 - c), [(1 - x, y), (x, 1 - y), (1 - x, 1 - y)]


def _gather_start(name, srcs, lands, items, dep=None):
    ns, nl, ni = len(srcs), len(lands), len(items)

    def body(*refs):
        S, L = refs[0:ns], refs[ns:ns + nl]
        first_out = ns + nl + (0 if dep is None else 1)
        send_sems, recv_sems, token = refs[first_out], refs[first_out + 1], refs[-1]
        me, sib, chips = _place_and_neighbours()
        mi = _dev_index(*me)
        for i, (src, dst) in enumerate(items):
            for k, to in enumerate([sib] + [(*chip, me[2]) for chip in chips]):
                pltpu.make_async_remote_copy(
                    src_ref=src(S), dst_ref=dst(L, mi), send_sem=send_sems.at[4 * i + k],
                    recv_sem=recv_sems.at[4 * i + k], device_id=to, device_id_type=MESH).start()
        token[...] = jnp.zeros_like(token)

    thru = [pltpu.HBM(a.shape, a.dtype) for a in list(srcs) + list(lands)]
    args = [_in_hbm(a) for a in srcs] + [_in_hbm(a) for a in lands]
    in_specs = [HBM_ONLY] * (ns + nl)
    if dep is not None:
        args.append(dep)
        in_specs.append(HBM_SPEC)
    outs = pl.pallas_call(
        body, name=name, in_specs=in_specs,
        out_shape=(pltpu.SemaphoreType.DMA((4 * ni,)), pltpu.SemaphoreType.DMA((4 * ni,)), *thru, _sds((8, 128), F32)),
        out_specs=(SEM_SPEC, SEM_SPEC, *[HBM_ONLY] * (ns + nl), pl.BlockSpec(memory_space=pltpu.VMEM)),
        input_output_aliases={j: 2 + j for j in range(ns + nl)}, compiler_params=IN_FLIGHT)(*args)
    return dict(send=outs[0], recv=outs[1], srcs=list(outs[2:2 + ns]), lands=list(outs[2 + ns:2 + ns + nl]),
                token=outs[-1], items=items)


def _gather_forward(name, st, after):
    nl, ni = len(st["lands"]), len(st["items"])

    def body(*refs):
        L, recv_sems = refs[0:nl], refs[nl]
        fwd_send, fwd_recv, token = refs[2 * nl + 2], refs[2 * nl + 3], refs[-1]
        me, sib, chips = _place_and_neighbours()
        for i, (_, dst) in enumerate(st["items"]):
            for j, chip in enumerate(chips):
                blk = dst(L, _dev_index(*chip, me[2]))
                pltpu.make_async_remote_copy(
                    src_ref=blk, dst_ref=blk, send_sem=fwd_send.at[3 * i + j], recv_sem=recv_sems.at[4 * i + 1 + j],
                    device_id=me, device_id_type=MESH).wait_recv()
                pltpu.make_async_remote_copy(
                    src_ref=blk, dst_ref=blk, send_sem=fwd_send.at[3 * i + j], recv_sem=fwd_recv.at[3 * i + j],
                    device_id=sib, device_id_type=MESH).start()
        token[...] = jnp.zeros_like(token)

    outs = pl.pallas_call(
        body, name=name, in_specs=[HBM_ONLY] * nl + [SEM_SPEC, HBM_SPEC],
        out_shape=(*[pltpu.HBM(a.shape, a.dtype) for a in st["lands"]], pltpu.SemaphoreType.DMA((3 * ni,)),
                   pltpu.SemaphoreType.DMA((3 * ni,)), _sds((8, 128), F32)),
        out_specs=(*[HBM_ONLY] * nl, SEM_SPEC, SEM_SPEC, pl.BlockSpec(memory_space=pltpu.VMEM)),
        input_output_aliases={j: j for j in range(nl)}, compiler_params=IN_FLIGHT)(*st["lands"], st["recv"], after)
    return dict(st, lands=list(outs[0:nl]), fwd_send=outs[nl], fwd_recv=outs[nl + 1], token=outs[-1])


def _gather_wait(name, st, after):
    ns, nl, ni = len(st["srcs"]), len(st["lands"]), len(st["items"])

    def body(*refs):
        S, L = refs[0:ns], refs[ns:ns + nl]
        send_sems, recv_sems, fwd_send, fwd_recv = refs[ns + nl:ns + nl + 4]
        me, sib, chips = _place_and_neighbours()
        mi = _dev_index(*me)
        for i, (src, dst) in enumerate(st["items"]):
            mine = dst(L, mi)
            for k in range(4):
                pltpu.make_async_remote_copy(
                    src_ref=src(S), dst_ref=mine, send_sem=send_sems.at[4 * i + k], recv_sem=recv_sems.at[4 * i + k],
                    device_id=me, device_id_type=MESH).wait_send()
            pltpu.make_async_remote_copy(
                src_ref=src(S), dst_ref=mine, send_sem=send_sems.at[4 * i], recv_sem=recv_sems.at[4 * i],
                device_id=me, device_id_type=MESH).wait_recv()
            for j in range(3):
                cp = pltpu.make_async_remote_copy(
                    src_ref=mine, dst_ref=mine, send_sem=fwd_send.at[3 * i + j], recv_sem=fwd_recv.at[3 * i + j],
                    device_id=me, device_id_type=MESH)
                cp.wait_send()
                cp.wait_recv()

    arrays = st["srcs"] + st["lands"]
    outs = pl.pallas_call(
        body, name=name, in_specs=[HBM_ONLY] * (ns + nl) + [SEM_SPEC] * 4 + [HBM_SPEC],
        out_shape=tuple(pltpu.HBM(a.shape, a.dtype) for a in arrays), out_specs=tuple([HBM_ONLY] * (ns + nl)),
        input_output_aliases={j: j for j in range(ns + nl)},
        compiler_params=IN_FLIGHT)(*arrays, st["send"], st["recv"], st["fwd_send"], st["fwd_recv"], after)
    return list(outs[ns:ns + nl])


def _sum_slots(land):
    def body(l_ref, o_ref):
        acc = l_ref[0]
        for d in range(1, NDEV):
            acc = acc + l_ref[d]
        o_ref[...] = acc

    vm = pl.BlockSpec(memory_space=pltpu.VMEM)
    return pl.pallas_call(body, name="sum_small", out_shape=_sds(land.shape[1:], F32), in_specs=[vm], out_specs=vm)(land)


def _adam_math(w, g, m, v):
    m2 = ADAM_B1 * m + (1.0 - ADAM_B1) * g
    v2 = ADAM_B2 * v + (1.0 - ADAM_B2) * (g * g)
    delta = -ADAM_LR * ((m2 * ADAM_C1) / (jnp.sqrt(v2 * ADAM_C2) + ADAM_EPS) + ADAM_WD * w)
    return delta, m2, v2


def _adam_layer(land, sel, w, m, v, layer, name, prev=None, tc=512):
    R = land.shape[2]

    def body(l_ref, w_ref, m_ref, v_ref, *rest):
        g_out, d_out, m_out, v_out = rest[-4:]
        g = l_ref[0].astype(F32)
        for d in range(1, NDEV):
            g = g + l_ref[d].astype(F32)
        delta, m2, v2 = _adam_math(w_ref[...], g, m_ref[...], v_ref[...])
        g_out[...] = g
        d_out[...] = delta
        m_out[...] = m2
        v_out[...] = v2

    wspec = pl.BlockSpec((None, R, tc), lambda i: (layer, 0, i))
    in_specs = [pl.BlockSpec((None, NDEV, R, tc), lambda i: (sel, 0, 0, i)), wspec, wspec, wspec]
    args = [land, w, m, v]
    aliases = {}
    if prev is not None:
        in_specs += [HBM_SPEC] * 4
        args += list(prev)
        aliases = {4 + j: j for j in range(4)}
    return pl.pallas_call(
        body, name=name, grid=(D // tc,), in_specs=in_specs, out_specs=[wspec] * 4,
        out_shape=[_sds(w.shape, F32)] * 4, input_output_aliases=aliases, compiler_params=_cparams(1))(*args)


def _adam_stacked(lands, sel, w, m, v, name):
    res = None
    for layer, land in enumerate(lands):
        res = _adam_layer(land, sel, w, m, v, layer, f"{name}{layer}", prev=res)
    return res


def _adam_small(ws, gs, ms, vs):
    n = len(ws)

    def body(*refs):
        w_r, g_r, m_r, v_r = refs[0:n], refs[n:2 * n], refs[2 * n:3 * n], refs[3 * n:4 * n]
        d_o, m_o, v_o = refs[4 * n:5 * n], refs[5 * n:6 * n], refs[6 * n:7 * n]
        for i in range(n):
            delta, m2, v2 = _adam_math(w_r[i][...], g_r[i][...], m_r[i][...], v_r[i][...])
            d_o[i][...] = delta
            m_o[i][...] = m2
            v_o[i][...] = v2

    vm = pl.BlockSpec(memory_space=pltpu.VMEM)
    shapes = [_sds(w.shape, F32) for w in ws]
    outs = pl.pallas_call(body, name="adam_small", in_specs=[vm] * (4 * n), out_specs=[vm] * (3 * n),
                          out_shape=shapes * 3)(*ws, *gs, *ms, *vs)
    return outs[0:n], outs[n:2 * n], outs[2 * n:3 * n]


WEIGHT_NAMES = ("ab_norm_g", "ab_w_in", "sgu_norm_g", "sgu_norm_b", "sgu_w", "sgu_bias", "q_norm_g", "k_norm_g",
                "ab_w_out", "cd_norm_g", "cd_w_in", "conv_c_w", "conv_c_b", "c_ln_g", "c_ln_b", "conv_d_w",
                "cd_w_out", "ffn_norm_g", "ffn_w_gate", "ffn_w_up", "ffn_w_down")
SMALL_2D = (("ab_norm_g", (1, 1024)), ("sgu_norm_g", (1, 512)), ("sgu_norm_b", (1, 512)), ("sgu_w", (512, 128)),
            ("sgu_bias", (4, 128)), ("q_norm_g", (3, 64)), ("k_norm_g", (3, 64)), ("cd_norm_g", (1, 128)),
            ("conv_c_w", (31, 64)), ("conv_c_b", (1, 64)), ("c_ln_g", (1, 64)), ("c_ln_b", (1, 64)),
            ("conv_d_w", (3, 64)), ("ffn_norm_g", (2, 1024)))
SHARD_C = 64


def _pack_rows(parts, rows):
    flat = jnp.concatenate([p.reshape(-1) for p in parts])
    return jnp.pad(flat, (0, rows * 128 - flat.shape[0])).reshape(rows, 128)


def kernel(x, ab_norm_g, ab_w_in, sgu_norm_g, sgu_norm_b, sgu_w, sgu_bias, q_norm_g, k_norm_g, ab_w_out, cd_norm_g, cd_w_in, conv_c_w, conv_c_b, c_ln_g, c_ln_b, conv_d_w, cd_w_out, ffn_norm_g, ffn_w_gate, ffn_w_up, ffn_w_down, loss_target, m_ab_norm_g, m_ab_w_in, m_sgu_norm_g, m_sgu_norm_b, m_sgu_w, m_sgu_bias, m_q_norm_g, m_k_norm_g, m_ab_w_out, m_cd_norm_g, m_cd_w_in, m_conv_c_w, m_conv_c_b, m_c_ln_g, m_c_ln_b, m_conv_d_w, m_cd_w_out, m_ffn_norm_g, m_ffn_w_gate, m_ffn_w_up, m_ffn_w_down, v_ab_norm_g, v_ab_w_in, v_sgu_norm_g, v_sgu_norm_b, v_sgu_w, v_sgu_bias, v_q_norm_g, v_k_norm_g, v_ab_w_out, v_cd_norm_g, v_cd_w_in, v_conv_c_w, v_conv_c_b, v_c_ln_g, v_c_ln_b, v_conv_d_w, v_cd_w_out, v_ffn_norm_g, v_ffn_w_gate, v_ffn_w_up, v_ffn_w_down):
    w = dict(zip(WEIGHT_NAMES, (ab_norm_g, ab_w_in, sgu_norm_g, sgu_norm_b, sgu_w, sgu_bias, q_norm_g, k_norm_g, ab_w_out, cd_norm_g, cd_w_in, conv_c_w, conv_c_b, c_ln_g, c_ln_b, conv_d_w, cd_w_out, ffn_norm_g, ffn_w_gate, ffn_w_up, ffn_w_down)))
    m = dict(zip(WEIGHT_NAMES, (m_ab_norm_g, m_ab_w_in, m_sgu_norm_g, m_sgu_norm_b, m_sgu_w, m_sgu_bias, m_q_norm_g, m_k_norm_g, m_ab_w_out, m_cd_norm_g, m_cd_w_in, m_conv_c_w, m_conv_c_b, m_c_ln_g, m_c_ln_b, m_conv_d_w, m_cd_w_out, m_ffn_norm_g, m_ffn_w_gate, m_ffn_w_up, m_ffn_w_down)))
    v = dict(zip(WEIGHT_NAMES, (v_ab_norm_g, v_ab_w_in, v_sgu_norm_g, v_sgu_norm_b, v_sgu_w, v_sgu_bias, v_q_norm_g, v_k_norm_g, v_ab_w_out, v_cd_norm_g, v_cd_w_in, v_conv_c_w, v_conv_c_b, v_c_ln_g, v_c_ln_b, v_conv_d_w, v_cd_w_out, v_ffn_norm_g, v_ffn_w_gate, v_ffn_w_up, v_ffn_w_down)))
    me = _dev_index(*_my_place())

    small_local = _pack_rows([w["cd_norm_g"], w["conv_c_w"], w["conv_c_b"], w["c_ln_g"], w["c_ln_b"], w["conv_d_w"]], 24)
    r_ff = DFF // NDEV
    one = lambda a: (lambda S, j: S[a])
    slot = lambda b: (lambda L, s: L[b].at[s])
    slot2 = lambda b, part: (lambda L, s: L[b].at[part, s])
    shard = lambda a: (lambda S: S[a])

    def later(a):
        return lax.optimization_barrier((a, gathers[0]["token"]))[0]

    def layer_shards(layer):
        return (later(w["ffn_w_gate"][layer]).T.astype(BF16), later(w["ffn_w_up"][layer]).T.astype(BF16),
                later(w["ffn_w_down"][layer]).astype(BF16))

    def gathered(own):
        return _landing((NDEV,) + own.shape, BF16, [((me,), own)])

    def gathered2(a, b):
        return _landing((2, NDEV) + a.shape, BF16, [((0, me), a), ((1, me), b)])

    ab_in_s = w["ab_w_in"][0].T.astype(BF16)
    gathers = {0: _gather_start(
        "gather0_start", [ab_in_s, small_local],
        [gathered(ab_in_s), _landing((NDEV,) + small_local.shape, F32, [((me,), small_local)])],
        [(shard(0), slot(0)), (shard(1), slot(1))])}

    def chan(flat, lo, taps):
        return flat[:, lo:lo + taps * SHARD_C].reshape(NDEV, taps, SHARD_C).transpose(1, 0, 2).reshape(taps, 512)

    def fetch(stage, after):
        if stage == "ab_in":
            gathers[0] = _gather_forward("gather0_forward", gathers[0], after)
            l_ab_in, l_small = _gather_wait("gather0_wait", gathers[0], gathers[0]["token"])
            ab_out_s = later(w["ab_w_out"][0]).astype(BF16)
            gate0, up0, down0 = layer_shards(0)
            gathers[1] = _gather_start(
                "gather1_start", [ab_out_s, gate0, up0, down0],
                [gathered(ab_out_s), gathered2(gate0, up0), gathered(down0)],
                [(shard(0), slot(0)), (shard(1), slot2(1, 0)), (shard(2), slot2(1, 1)), (shard(3), slot(2))],
                dep=l_small)
            flat = l_small.reshape(NDEV, 24 * 128)
            return {
                "wt_ab_in": l_ab_in.reshape(AB_IN, D), "dep0": gathers[1]["token"],
                "cd_norm_g": flat[:, 0:128].reshape(1, D),
                "conv_c_w32": jnp.pad(chan(flat, 128, CONV_C_TAPS), ((0, 1), (0, 0))),
                "conv_c_b": chan(flat, 2112, 1), "c_ln_g": chan(flat, 2176, 1), "c_ln_b": chan(flat, 2240, 1),
                "conv_d_w8": jnp.pad(chan(flat, 2304, CONV_D_TAPS), ((0, 8 - CONV_D_TAPS), (0, 0))),
            }
        if stage == "attn0":
            cd_in_s, cd_out_s = later(w["cd_w_in"][0]).T.astype(BF16), later(w["cd_w_out"][0]).astype(BF16)
            gate1, up1, down1 = layer_shards(1)
            gathers[2] = _gather_start(
                "gather2_start", [cd_in_s, cd_out_s, gate1, up1, down1],
                [gathered(cd_in_s), gathered(cd_out_s), gathered2(gate1, up1), gathered(down1)],
                [(shard(0), slot(0)), (shard(1), slot(1)), (shard(2), slot2(2, 0)), (shard(3), slot2(2, 1)),
                 (shard(4), slot(3))], dep=after)
            return {"dep_attn1": gathers[2]["token"]}
        if stage == "attn1":
            gathers[1] = _gather_forward("gather1_forward", gathers[1], after)
            return {"dep_attn2": gathers[1]["token"]}
        if stage == "ab_out":
            l_out, l_ffn, l_down = _gather_wait("gather1_wait", gathers[1], after)
            return {"w_ab_out": l_out.reshape(D, D), "wt_ffn_in0": l_ffn.reshape(2 * DFF, D),
                    "w_ffn_down0": l_down.reshape(DFF, D)}
        if stage == "ffn_down0":
            gathers[2] = _gather_forward("gather2_forward", gathers[2], after)
            return {"dep_down0": gathers[2]["token"]}
        if stage == "cd_in":
            l_in, l_out, l_ffn, l_down = _gather_wait("gather2_wait", gathers[2], after)
            return {"wt_cd_in": l_in.reshape(CD_IN, D), "w_cd_out": l_out.reshape(D, D),
                    "wt_ffn_in1": l_ffn.reshape(2 * DFF, D), "w_ffn_down1": l_down.reshape(DFF, D)}
        return {}

    scatters = {}
    rides_with = {"w_ffn_down1": "wt_ffn_in1", "w_cd_out": "wt_cd_in", "w_ffn_down0": "wt_ffn_in0"}
    held = {}

    def on_grad(key, arr):
        if key in rides_with:
            held[rides_with[key]] = (key, arr)
            return None
        group = ([held.pop(key)] if key in held else []) + [(key, arr)]
        srcs, lands, items = [], [], []
        for n, (k, a) in enumerate(group):
            if k.startswith("wt_ffn_in"):
                src = a.reshape(2, NDEV, r_ff, D)
                own = lax.dynamic_slice_in_dim(src, me, 1, axis=1)
                lands.append(lax.dynamic_update_slice(lax.empty(src.shape, BF16), own, (0, me, 0, 0)))
                items += [((lambda S, j, n=n: S[n].at[0, j]), slot2(n, 0)), ((lambda S, j, n=n: S[n].at[1, j]), slot2(n, 1))]
            else:
                rows = a.shape[0] // NDEV
                src = a.reshape(NDEV, rows, D)
                own = lax.dynamic_index_in_dim(src, me, 0, keepdims=False)
                lands.append(_landing((1, NDEV, rows, D), BF16, [((0, me), own)]))
                items.append(((lambda S, j, n=n: S[n].at[j]), slot2(n, 0)))
            srcs.append(src)
        st = _exchange_start(f"scatter_{key}_start", srcs, lands, items)
        scatters[key] = (st, [k for k, _ in group])
        return st["token"]

    W = {
        "dep_first": gathers[0]["token"],
        "ab_norm_g": w["ab_norm_g"], "sgu_norm_g": w["sgu_norm_g"], "sgu_norm_b": w["sgu_norm_b"],
        "sgu_w": w["sgu_w"][0], "sgu_bias": w["sgu_bias"][0], "q_norm_g": w["q_norm_g"][0],
        "k_norm_g": w["k_norm_g"][0], "ffn_norm_g": w["ffn_norm_g"],
    }

    loss_cols, grad_x, G = _local_step(x[0], loss_target[0], W, fetch, on_grad)

    small_parts = [G["ab_norm_g"], G["sgu_norm_g"], G["sgu_norm_b"], G["sgu_w"], G["sgu_bias"], G["q_norm_g"],
                   G["k_norm_g"], G["cd_norm_g"], G["conv_c_w32"][:CONV_C_TAPS], G["conv_c_b"], G["c_ln_g"],
                   G["c_ln_b"], G["conv_d_w8"][:CONV_D_TAPS], G["ffn_norm_g0"], G["ffn_norm_g1"], loss_cols]
    sizes = [p.size for p in small_parts]
    small_rows = 720
    packed = _pack_rows(small_parts, small_rows)
    small = _exchange_start("small_start", [packed], [_landing((NDEV, small_rows, 128), F32, [((me,), packed)])],
                            [(one(0), slot(0))])
    landed = {}

    def wait_scatters(name, group_keys, after):
        res = _exchange_wait(name, [scatters[gk][0] for gk in group_keys], after)
        for gk, lands in zip(group_keys, res):
            landed.update(zip(scatters[gk][1], lands))

    wait_scatters("scatter_wait_early", ["wt_ffn_in1", "wt_cd_in", "wt_ffn_in0", "w_ab_out"], small["token"])

    grads, deltas, new_m, new_v = {}, {}, {}, {}
    done = []

    def put(name, res):
        grads[name], deltas[name], new_m[name], new_v[name] = res

    def adam(name, lands, sel, transposed):
        flip = (lambda a: jnp.swapaxes(a, 1, 2)) if transposed else (lambda a: a)
        res = _adam_stacked(lands, sel, flip(w[name]), flip(m[name]), flip(v[name]), f"adam_{name}")
        done.append(res[1])
        put(name, [flip(r) for r in res])

    ffn_in_lands = [landed["wt_ffn_in0"], landed["wt_ffn_in1"]]
    adam("cd_w_in", [landed["wt_cd_in"]], 0, True)
    adam("ffn_w_gate", ffn_in_lands, 0, True)
    adam("ffn_w_up", ffn_in_lands, 1, True)
    adam("cd_w_out", [landed["w_cd_out"]], 0, False)
    adam("ab_w_out", [landed["w_ab_out"]], 0, False)
    adam("ffn_w_down", [landed["w_ffn_down0"], landed["w_ffn_down1"]], 0, False)

    small_land = _exchange_wait("small_wait", [small], list(done))[0][0]
    red = _sum_slots(small_land).reshape(-1)
    offs = [0]
    for s in sizes:
        offs.append(offs[-1] + s)
    seg = [red[offs[i]:offs[i + 1]] for i in range(len(sizes))]
    loss = jnp.sum(seg[15])

    def own_channels(full, taps):
        return lax.dynamic_slice_in_dim(full.reshape(taps, 512), me * SHARD_C, SHARD_C, axis=1)

    g_small = {
        "ab_norm_g": seg[0].reshape(1, 1024), "sgu_norm_g": seg[1].reshape(1, 512), "sgu_norm_b": seg[2].reshape(1, 512),
        "sgu_w": seg[3].reshape(512, 128), "sgu_bias": seg[4].reshape(4, 128), "q_norm_g": seg[5].reshape(3, 64),
        "k_norm_g": seg[6].reshape(3, 64),
        "cd_norm_g": lax.dynamic_slice_in_dim(seg[7].reshape(1, D), me * (D // NDEV), D // NDEV, axis=1),
        "conv_c_w": own_channels(seg[8], CONV_C_TAPS), "conv_c_b": own_channels(seg[9], 1),
        "c_ln_g": own_channels(seg[10], 1), "c_ln_b": own_channels(seg[11], 1),
        "conv_d_w": own_channels(seg[12], CONV_D_TAPS),
        "ffn_norm_g": jnp.concatenate([seg[13].reshape(1, D), seg[14].reshape(1, D)], axis=0),
    }

    names2d = [n for n, _ in SMALL_2D]
    d_s, m_s, v_s = _adam_small([w[n].reshape(s) for n, s in SMALL_2D], [g_small[n] for n in names2d],
                                [m[n].reshape(s) for n, s in SMALL_2D], [v[n].reshape(s) for n, s in SMALL_2D])
    for i, n in enumerate(names2d):
        shape = w[n].shape
        grads[n], deltas[n] = g_small[n].reshape(shape), d_s[i].reshape(shape)
        new_m[n], new_v[n] = m_s[i].reshape(shape), v_s[i].reshape(shape)

    wait_scatters("scatter_wait_last", ["wt_ab_in"], d_s[0])
    adam("ab_w_in", [landed["wt_ab_in"]], 0, True)

    return (loss, grad_x[None], *[grads[n] for n in WEIGHT_NAMES], *[deltas[n] for n in WEIGHT_NAMES],
            *[new_m[n] for n in WEIGHT_NAMES], *[new_v[n] for n in WEIGHT_NAMES])
```

```python
import jax
import jax.numpy as jnp
import numpy as np
from jax import lax
from jax.experimental import pallas as pl
from jax.experimental.pallas import tpu as pltpu

F32 = jnp.float32
BF16 = jnp.bfloat16

T = 4096
D = 1024
NDEV = 8
EPS = 1e-6
NEG_INF = -1e30
DFF = 2816
AB_IN = 5632
CD_IN = 2560
HEAD = 64
PAIR = 128
NPAIR = 4
NBACK = 128
DIL_RATES = (1, 4, 16)
ROPE_HALF = 8
ROPE_THETA = 500000.0
CONV_C_TAPS = 31
CONV_D_TAPS = 3
HALO = 32
ATTN_BWD_UNROLL = 4

ADAM_LR = 0.001
ADAM_B1 = 0.9
ADAM_B2 = 0.999
ADAM_EPS = 1e-08
ADAM_WD = 0.01
ADAM_STEP = 10
ADAM_C1 = 1.0 / (1.0 - ADAM_B1 ** ADAM_STEP)
ADAM_C2 = 1.0 / (1.0 - ADAM_B2 ** ADAM_STEP)

VMEM_LIMIT_MB = 48
MESH = pl.DeviceIdType.MESH
HBM_SPEC = pl.BlockSpec(memory_space=pl.ANY)


def _cparams(ngrid, vmem_mb=VMEM_LIMIT_MB):
    return pltpu.CompilerParams(dimension_semantics=("arbitrary",) * ngrid,
                                vmem_limit_bytes=vmem_mb * 1024 * 1024)


def _pick(n, options):
    for o in options:
        if n % o == 0:
            return o
    raise ValueError(f"no tile for {n} in {options}")


def _sds(shape, dtype):
    return jax.ShapeDtypeStruct(shape, dtype)


def _sigmoid(x):
    return 1.0 / (1.0 + jnp.exp(-x))


def _sigmoid_bf16(x):
    return 0.5 * jnp.tanh(0.5 * x) + 0.5


def _gelu(z):
    return 0.5 * z * (1.0 + lax.erf(z * 0.7071067811865476))


def _gelu_grad(z):
    return 0.5 * (1.0 + lax.erf(z * 0.7071067811865476)) + z * jnp.exp(-0.5 * z * z) * 0.3989422804014327


def _mm_nt(a, wt, name, out_dtype=BF16, dep=None):
    M, K = a.shape
    N = wt.shape[0]
    tn = _pick(N, (512, 256))

    def body(a_ref, w_ref, *rest):
        o_ref = rest[-1]
        for r0 in range(0, M, 1024):
            o_ref[r0:r0 + 1024, :] = lax.dot_general(
                a_ref[r0:r0 + 1024, :], w_ref[...], (((1,), (1,)), ((), ())),
                preferred_element_type=F32).astype(o_ref.dtype)

    in_specs = [pl.BlockSpec((M, K), lambda j: (0, 0), pipeline_mode=pl.Buffered(1)),
                pl.BlockSpec((tn, K), lambda j: (j, 0))]
    args = [a, wt]
    if dep is not None:
        in_specs.append(HBM_SPEC)
        args.append(dep)
    return pl.pallas_call(
        body, name=name, grid=(N // tn,), in_specs=in_specs, out_specs=pl.BlockSpec((M, tn), lambda j: (0, j)),
        out_shape=_sds((M, N), out_dtype), compiler_params=_cparams(1))(*args)


EPI_ROWS = 256


def _mm_nn(a, w, name, mode, resid, gain=None, tgt=None, dep=None, tm=512):
    M, K = a.shape
    N = w.shape[1]
    side = gain if mode == "rms" else tgt

    def body(a_ref, w_ref, resid_ref, side_ref, *rest):
        outs, acc = rest[-3 if mode == "rms" else -4:-1], rest[-1]
        i = pl.program_id(0)
        acc[...] = jnp.dot(a_ref[...], w_ref[...], preferred_element_type=F32)

        if mode == "loss":
            @pl.when(i == 0)
            def _():
                outs[2][...] = jnp.zeros_like(outs[2])

        for r0 in range(0, tm, EPI_ROWS):
            rows = slice(r0, r0 + EPI_ROWS)
            v = acc[rows, :] + resid_ref[rows, :]
            if mode == "rms":
                outs[0][rows, :] = v
                r = lax.rsqrt(jnp.mean(v * v, axis=-1, keepdims=True) + EPS)
                outs[1][rows, :] = (v * r * side_ref[...]).astype(BF16)
            else:
                d = v - side_ref[rows, :]
                outs[2][...] += jnp.sum(d * d, axis=0, keepdims=True) * (0.5 / N)
                dy = d * (1.0 / N)
                outs[0][rows, :] = dy
                outs[1][rows, :] = dy.astype(BF16)

    row = pl.BlockSpec((tm, N), lambda i: (i, 0))
    vec = pl.BlockSpec((1, N), lambda i: (0, 0))
    in_specs = [pl.BlockSpec((tm, K), lambda i: (i, 0)),
                pl.BlockSpec((K, N), lambda i: (0, 0), pipeline_mode=pl.Buffered(1)), row,
                vec if mode == "rms" else row]
    args = [a, w, resid, side]
    if dep is not None:
        in_specs.append(HBM_SPEC)
        args.append(dep)
    if mode == "rms":
        out_specs, out_shape = [row, row], [_sds((M, N), F32), _sds((M, N), BF16)]
    else:
        out_specs, out_shape = [row, row, vec], [_sds((M, N), F32), _sds((M, N), BF16), _sds((1, N), F32)]
    return pl.pallas_call(
        body, name=name, grid=(M // tm,), in_specs=in_specs, out_specs=out_specs, out_shape=out_shape,
        scratch_shapes=[pltpu.VMEM((tm, N), F32)], compiler_params=_cparams(1))(*args)


def _mm_dh_rms_bwd(a, w, x, gain, dres, name, dep=None, tm=512, bf16_copy=True):
    parts = a.shape[0] if a.ndim == 3 else 1
    M, Kp = a.shape[-2], a.shape[-1]
    N = w.shape[1]
    nblk = M // tm
    assert nblk % 2 == 0

    def body(a_ref, w_ref, x_ref, g_ref, dres_ref, *rest):
        dg_ref, acc0, acc1 = rest[-3:]
        dx_ref = rest[-5] if bf16_copy else rest[-4]
        dxb_ref = rest[-4] if bf16_copy else None
        i = pl.program_id(0)

        def matmul(acc):
            if parts == 1:
                acc[...] = jnp.dot(a_ref[...], w_ref[...], preferred_element_type=F32)
            else:
                d = jnp.dot(a_ref[0], w_ref[0:Kp, :], preferred_element_type=F32)
                for p in range(1, parts):
                    d = d + jnp.dot(a_ref[p], w_ref[p * Kp:(p + 1) * Kp, :], preferred_element_type=F32)
                acc[...] = d

        def finish(acc):
            for r0 in range(0, tm, EPI_ROWS // 2):
                rows = slice(r0, r0 + EPI_ROWS // 2)
                v = acc[rows, :]
                xf = x_ref[rows, :]
                r = lax.rsqrt(jnp.mean(xf * xf, axis=-1, keepdims=True) + EPS)
                xhat = xf * r
                dg_ref[...] += jnp.sum(v * xhat, axis=0, keepdims=True)
                dxh = v * g_ref[...]
                tot = dres_ref[rows, :] + r * (dxh - xhat * jnp.mean(dxh * xhat, axis=-1, keepdims=True))
                dx_ref[rows, :] = tot
                if bf16_copy:
                    dxb_ref[rows, :] = tot.astype(BF16)

        @pl.when(i == 0)
        def _():
            dg_ref[...] = jnp.zeros_like(dg_ref)
            matmul(acc0)

        @pl.when((i > 0) & (i < nblk) & (i % 2 == 1))
        def _():
            matmul(acc1)
            finish(acc0)

        @pl.when((i > 0) & (i < nblk) & (i % 2 == 0))
        def _():
            matmul(acc0)
            finish(acc1)

        @pl.when(i == nblk)
        def _():
            finish(acc1)

    last = nblk - 1
    row = pl.BlockSpec((tm, N), lambda i: (jnp.maximum(i - 1, 0), 0))
    vec = pl.BlockSpec((1, N), lambda i: (0, 0))
    if a.ndim == 3:
        a_spec = pl.BlockSpec((parts, tm, Kp), lambda i: (0, jnp.minimum(i, last), 0))
    else:
        a_spec = pl.BlockSpec((tm, Kp), lambda i: (jnp.minimum(i, last), 0))
    w_spec = pl.BlockSpec((parts * Kp, N), lambda i: (0, 0), pipeline_mode=pl.Buffered(1))
    in_specs = [a_spec, w_spec, row, vec, row]
    args = [a, w, x, gain, dres]
    if dep is not None:
        in_specs.append(HBM_SPEC)
        args.append(dep)
    return pl.pallas_call(
        body, name=name, grid=(nblk + 1,), in_specs=in_specs,
        out_specs=[row, row, vec] if bf16_copy else [row, vec],
        out_shape=([_sds((M, N), F32), _sds((M, N), BF16), _sds((1, N), F32)] if bf16_copy
                   else [_sds((M, N), F32), _sds((1, N), F32)]),
        scratch_shapes=[pltpu.VMEM((tm, N), F32), pltpu.VMEM((tm, N), F32)], compiler_params=_cparams(1, 56))(*args)


def _mm_tn(a, b, name, out_dtype=BF16, tt=2048, dep=None):
    parts = a.shape[0] if a.ndim == 3 else 1
    Tt, Mp = a.shape[-2], a.shape[-1]
    N = b.shape[1]
    tn = _pick(Mp, (1408, 1280, 1024, 512))
    jper = Mp // tn
    nt = Tt // tt

    def body(a_ref, b_ref, *rest):
        o_ref, acc = rest[-2:]
        t = pl.program_id(1)

        @pl.when(t == 0)
        def _():
            acc[...] = jnp.zeros_like(acc)

        rows = pl.ds(pl.multiple_of(t * tt, tt), tt)
        acc[...] += lax.dot_general(a_ref[...], b_ref[rows, :], (((0,), (0,)), ((), ())),
                                    preferred_element_type=F32)

        @pl.when(t == nt - 1)
        def _():
            o_ref[...] = acc[...].astype(o_ref.dtype)

    if a.ndim == 3:
        a_spec = pl.BlockSpec((None, tt, tn), lambda j, t: (j // jper, t, j % jper))
    else:
        a_spec = pl.BlockSpec((tt, tn), lambda j, t: (t, j))
    in_specs = [a_spec, pl.BlockSpec((Tt, N), lambda j, t: (0, 0), pipeline_mode=pl.Buffered(1))]
    args = [a, b]
    if dep is not None:
        in_specs.append(HBM_SPEC)
        args.append(dep)
    return pl.pallas_call(
        body, name=name, grid=(parts * jper, nt), in_specs=in_specs,
        out_specs=pl.BlockSpec((tn, N), lambda j, t: (j, 0)),
        out_shape=_sds((parts * Mp, N), out_dtype), scratch_shapes=[pltpu.VMEM((tn, N), F32)],
        compiler_params=_cparams(2))(*args)


FFN_ROWS = 2048


def _ffn_in(h, wt_in, name, tn=256):
    nj = DFF // tn

    def body(h_ref, wg_ref, wu_ref, p_ref, act_ref):
        nt = (((1,), (1,)), ((), ()))
        for r0 in range(0, T, FFN_ROWS):
            rows = slice(r0, r0 + FFN_ROWS)
            g = lax.dot_general(h_ref[rows, :], wg_ref[...], nt, preferred_element_type=F32).astype(BF16)
            u = lax.dot_general(h_ref[rows, :], wu_ref[...], nt, preferred_element_type=F32).astype(BF16)
            p_ref[0, rows, :] = g
            p_ref[1, rows, :] = u
            act_ref[rows, :] = g * _sigmoid_bf16(g) * u

    return pl.pallas_call(
        body, name=name, grid=(nj,),
        in_specs=[pl.BlockSpec((T, D), lambda j: (0, 0), pipeline_mode=pl.Buffered(1)),
                  pl.BlockSpec((tn, D), lambda j: (j, 0)), pl.BlockSpec((tn, D), lambda j: (j + nj, 0))],
        out_specs=[pl.BlockSpec((2, T, tn), lambda j: (0, 0, j)), pl.BlockSpec((T, tn), lambda j: (0, j))],
        out_shape=[_sds((2, T, DFF), BF16), _sds((T, DFF), BF16)], compiler_params=_cparams(1))(h, wt_in, wt_in)


def _ffn_dact(dyb, w_down, p3, name, tn=256, dep=None):
    def body(dy_ref, w_ref, p_ref, *rest):
        o_ref = rest[-1]
        for r0 in range(0, T, FFN_ROWS):
            rows = slice(r0, r0 + FFN_ROWS)
            da = lax.dot_general(dy_ref[rows, :], w_ref[...], (((1,), (1,)), ((), ())),
                                 preferred_element_type=F32).astype(BF16)
            g = p_ref[0, rows, :]
            u = p_ref[1, rows, :]
            sg = _sigmoid_bf16(g)
            gs = g * sg
            o_ref[0, rows, :] = (da * u) * (sg + gs * (1.0 - sg))
            o_ref[1, rows, :] = da * gs

    pspec = pl.BlockSpec((2, T, tn), lambda j: (0, 0, j))
    in_specs = [pl.BlockSpec((T, D), lambda j: (0, 0), pipeline_mode=pl.Buffered(1)),
                pl.BlockSpec((tn, D), lambda j: (j, 0)), pspec]
    args = [dyb, w_down, p3]
    if dep is not None:
        in_specs.append(HBM_SPEC)
        args.append(dep)
    return pl.pallas_call(
        body, name=name, grid=(DFF // tn,), in_specs=in_specs, out_specs=pspec,
        out_shape=_sds((2, T, DFF), BF16), compiler_params=_cparams(1))(*args)


def _rms_fwd(x, g, name, tm=512, dep=None):
    def body(x_ref, g_ref, *rest):
        h_ref = rest[-1]
        xf = x_ref[...]
        r = lax.rsqrt(jnp.mean(xf * xf, axis=-1, keepdims=True) + EPS)
        h_ref[...] = (xf * r * g_ref[...]).astype(BF16)

    in_specs = [pl.BlockSpec((tm, D), lambda i: (i, 0)), pl.BlockSpec((1, D), lambda i: (0, 0))]
    args = [x, g]
    if dep is not None:
        in_specs.append(HBM_SPEC)
        args.append(dep)
    return pl.pallas_call(
        body, name=name, grid=(T // tm,), in_specs=in_specs, out_specs=pl.BlockSpec((tm, D), lambda i: (i, 0)),
        out_shape=_sds((T, D), BF16), compiler_params=_cparams(1))(*args)


def _tril_mask():
    r = lax.broadcasted_iota(jnp.int32, (128, 128), 0)
    c = lax.broadcasted_iota(jnp.int32, (128, 128), 1)
    return r >= c


def _mix_a_fwd(pab, sgu_g, sgu_b, sgu_w, sgu_bias3, tm=512):
    def body(zu_ref, zv_ref, g_ref, b_ref, w_ref, bias_ref, o_ref):
        u = _gelu(zu_ref[...].astype(F32))
        v = _gelu(zv_ref[...].astype(F32))
        mu = jnp.mean(v, axis=-1, keepdims=True)
        vc = v - mu
        rstd = lax.rsqrt(jnp.mean(vc * vc, axis=-1, keepdims=True) + EPS)
        vn = (vc * rstd * g_ref[...] + b_ref[...]).astype(BF16)
        tri = _tril_mask()
        for gi in range(4):
            wg = jnp.where(tri, w_ref[gi], 0.0).astype(BF16)
            bg = bias_ref[gi]
            for c in range(tm // 128):
                rs, cs = slice(c * 128, (c + 1) * 128), slice(gi * 128, (gi + 1) * 128)
                mixed = jnp.dot(wg, vn[rs, cs], preferred_element_type=F32) + bg
                o_ref[rs, cs] = (u[rs, cs] * mixed).astype(BF16)

    half = pl.BlockSpec((tm, 512), lambda i: (i, 0))
    return pl.pallas_call(
        body, name="mix_a_fwd", grid=(T // tm,),
        in_specs=[half, pl.BlockSpec((tm, 512), lambda i: (i, 1)),
                  pl.BlockSpec((1, 512), lambda i: (0, 0)), pl.BlockSpec((1, 512), lambda i: (0, 0)),
                  pl.BlockSpec((4, 128, 128), lambda i: (0, 0, 0)), pl.BlockSpec((4, 128, 1), lambda i: (0, 0, 0))],
        out_specs=half, out_shape=_sds((T, D), BF16), compiler_params=_cparams(1),
    )(pab, pab, sgu_g, sgu_b, sgu_w, sgu_bias3)


def _rope_tables():
    pos = np.arange(T, dtype=np.float32)
    inv_freq = np.float32(ROPE_THETA) ** (-np.arange(ROPE_HALF, dtype=np.float32) * np.float32(2.0 / (2 * ROPE_HALF)))
    ang = (pos[:, None] * inv_freq[None, :]).astype(np.float32)
    cos, sin = np.cos(ang), np.sin(ang)
    z8 = np.zeros((T, ROPE_HALF), np.float32)
    rest = np.zeros((T, HEAD - 2 * ROPE_HALF), np.float32)
    c64 = np.concatenate([cos, cos, rest + 1.0], axis=1)
    s1 = np.concatenate([z8, sin, rest], axis=1)
    s2 = np.concatenate([-sin, z8, rest], axis=1)
    return tuple(jnp.asarray(np.tile(t, (1, 2)).astype(np.float32)) for t in (c64, s1, s2))


def _lo_mask(shape):
    return lax.broadcasted_iota(jnp.int32, shape, 1) < HEAD


def _seg_mean(x, lo):
    s_all = jnp.sum(x, axis=-1, keepdims=True)
    s_lo = jnp.sum(jnp.where(lo, x, 0.0), axis=-1, keepdims=True)
    return jnp.where(lo, s_lo, s_all - s_lo) * (1.0 / HEAD)


def _head_blocks():
    r = lax.broadcasted_iota(jnp.int32, (PAIR, PAIR), 0) < HEAD
    c = lax.broadcasted_iota(jnp.int32, (PAIR, PAIR), 1) < HEAD
    return jnp.where(r == c, 1.0, 0.0).astype(BF16)


def _seg_mean_mxu(x, blocks):
    return jnp.dot(x.astype(BF16), blocks, preferred_element_type=F32) * (1.0 / HEAD)


def _rope(n, c, s1, s2):
    return n * c + pltpu.roll(n, ROPE_HALF, 1) * s1 + pltpu.roll(n, PAIR - ROPE_HALF, 1) * s2


def _rope_t(dy, c, s1, s2):
    return dy * c - pltpu.roll(dy, PAIR - ROPE_HALF, 1) * s2 - pltpu.roll(dy, ROPE_HALF, 1) * s1


def _prep_fwd(pab, qg, kg, tabs, tm=512):
    def body(p_ref, qg_ref, kg_ref, c_ref, s1_ref, s2_ref, *outs):
        blocks = _head_blocks()
        c, s1, s2 = c_ref[...], s1_ref[...], s2_ref[...]
        for g in range(3):
            qn_ref, kn_ref, v_ref = outs[3 * g:3 * g + 3]
            for p in range(NPAIR):
                for which, gains, dst in ((0, qg_ref, qn_ref), (1, kg_ref, kn_ref)):
                    col = (2 + 3 * which + g) * 512 + p * PAIR
                    xr = p_ref[:, col:col + PAIR].astype(F32)
                    rinv = lax.rsqrt(_seg_mean_mxu(xr * xr, blocks) + EPS)
                    outs[9 + 2 * g + which][p] = rinv.astype(BF16)
                    dst[p] = _rope(xr * rinv * gains[g:g + 1, :], c, s1, s2)
                col = (8 + g) * 512 + p * PAIR
                v_ref[p] = p_ref[:, col:col + PAIR].astype(F32)

    pm = pl.BlockSpec((NPAIR, tm, PAIR), lambda i: (0, i, 0))
    tab = pl.BlockSpec((tm, PAIR), lambda i: (i, 0))
    gain = pl.BlockSpec((3, PAIR), lambda i: (0, 0))
    res = pl.pallas_call(
        body, name="prep_fwd", grid=(T // tm,),
        in_specs=[pl.BlockSpec((tm, AB_IN), lambda i: (i, 0)), gain, gain, tab, tab, tab],
        out_specs=[pm] * 15, out_shape=[_sds((NPAIR, T, PAIR), F32)] * 9 + [_sds((NPAIR, T, PAIR), BF16)] * 6,
        compiler_params=_cparams(1))(pab, qg, kg, *tabs)
    return res[0:9], res[9:15]


def _res_index(it, rate):
    window = NBACK * rate
    b = it // rate
    rho = it % rate
    start = b * window + rho
    startp = jnp.maximum(start - window, rho)
    kmin = jnp.where(b > 0, 0, NBACK)
    return start, startp, kmin


def _rows(start, rate):
    if rate == 1:
        return pl.ds(pl.multiple_of(start, NBACK), NBACK)
    return pl.ds(start, NBACK, stride=rate)


def _band_bias():
    qs = lax.broadcasted_iota(jnp.int32, (2 * NBACK, 2 * NBACK), 0)
    kj = lax.broadcasted_iota(jnp.int32, (2 * NBACK, 2 * NBACK), 1)
    dist = (qs & (NBACK - 1)) + NBACK - kj
    both = (dist >= 0) & (dist <= NBACK)
    return jnp.where(both, 0.0, NEG_INF), jnp.where(both & (kj >= NBACK), 0.0, NEG_INF)


def _attn_fwd(qn, kn, v, rate, name, dep=None):
    def body(q_ref, k_ref, v_ref, *rest):
        o_ref, l_ref = rest[-2:]
        lo = _lo_mask((NBACK, PAIR))
        bias_all, bias_first = _band_bias()

        def step(it, carry):
            start, startp, kmin = _res_index(it, rate)
            q = q_ref[_rows(start, rate), :] * (HEAD ** -0.5)
            kcat = jnp.concatenate([k_ref[_rows(startp, rate), :], k_ref[_rows(start, rate), :]], axis=0).astype(BF16)
            vcat = jnp.concatenate([v_ref[_rows(startp, rate), :], v_ref[_rows(start, rate), :]], axis=0).astype(BF16)
            vcat1 = jnp.concatenate([vcat, jnp.ones((2 * NBACK, PAIR), BF16)], axis=1)
            q2 = jnp.concatenate([jnp.where(lo, q, 0.0), jnp.where(lo, 0.0, q)], axis=0).astype(BF16)
            s = lax.dot_general(q2, kcat, (((1,), (1,)), ((), ())), preferred_element_type=F32)
            s = s + jnp.where(kmin == 0, bias_all, bias_first)
            m = jnp.max(s, axis=-1, keepdims=True)
            ol = jnp.dot(jnp.exp(s - m).astype(BF16), vcat1, preferred_element_type=F32)
            o2 = ol[:, 0:PAIR] / ol[:, PAIR:]
            ls = m + jnp.log(ol[:, PAIR:])
            o_ref[_rows(start, rate), :] = jnp.where(lo, o2[0:NBACK], o2[NBACK:])
            l_ref[_rows(start, rate), :] = jnp.where(lo, ls[0:NBACK], ls[NBACK:])
            return carry

        lax.fori_loop(0, T // NBACK, step, 0, unroll=4)

    pm = pl.BlockSpec((None, T, PAIR), lambda p: (p, 0, 0))
    in_specs, args = [pm, pm, pm], [qn, kn, v]
    if dep is not None:
        in_specs.append(HBM_SPEC)
        args.append(dep)
    return pl.pallas_call(
        body, name=name, grid=(NPAIR,), in_specs=in_specs, out_specs=[pm, pm],
        out_shape=[_sds((NPAIR, T, PAIR), F32)] * 2, compiler_params=_cparams(1))(*args)


def _merge_fwd(cat_ab, outs, lses, tm=512):
    def body(cat_in, o0, o1, o2, l0, l1, l2, cat_ref, lse_ref):
        del cat_in
        for p in range(NPAIR):
            a0, a1, a2 = l0[p], l1[p], l2[p]
            m = jnp.maximum(jnp.maximum(a0, a1), a2)
            w0, w1, w2 = jnp.exp(a0 - m), jnp.exp(a1 - m), jnp.exp(a2 - m)
            s = w0 + w1 + w2
            b = (w0 * o0[p] + w1 * o1[p] + w2 * o2[p]) / s
            cat_ref[:, p * PAIR:(p + 1) * PAIR] = b.astype(BF16)
            lse_ref[p] = m + jnp.log(s)

    pm = pl.BlockSpec((NPAIR, tm, PAIR), lambda i: (0, i, 0))
    return pl.pallas_call(
        body, name="merge_fwd", grid=(T // tm,),
        in_specs=[pl.BlockSpec(memory_space=pl.ANY)] + [pm] * 6,
        out_specs=[pl.BlockSpec((tm, 512), lambda i: (i, 1)), pm],
        out_shape=[_sds((T, D), BF16), _sds((NPAIR, T, PAIR), F32)],
        input_output_aliases={0: 0}, compiler_params=_cparams(1))(cat_ab, *outs, *lses)


def _b_pre_bwd(dcat, cat, tm=512):
    def body(db_ref, b_ref, dbp_ref, e_ref):
        lo = _lo_mask((tm, PAIR))
        for p in range(NPAIR):
            db = db_ref[:, p * PAIR:(p + 1) * PAIR].astype(F32)
            b = b_ref[:, p * PAIR:(p + 1) * PAIR].astype(F32)
            dbp_ref[p] = db
            e_ref[p] = _seg_mean(db * b, lo) * float(HEAD)

    pm = pl.BlockSpec((NPAIR, tm, PAIR), lambda i: (0, i, 0))
    right = pl.BlockSpec((tm, 512), lambda i: (i, 1))
    return pl.pallas_call(
        body, name="b_pre_bwd", grid=(T // tm,), in_specs=[right, right], out_specs=[pm, pm],
        out_shape=[_sds((NPAIR, T, PAIR), F32)] * 2, compiler_params=_cparams(1))(dcat, cat)


def _attn_bwd(qn, kn, v, dbp, e, lse, rate, name):
    def body(q_ref, k_ref, v_ref, db_ref, e_ref, lse_ref, dq_ref, dk_ref, dv_ref):
        lo = _lo_mask((NBACK, PAIR))
        bias_all, bias_first = _band_bias()
        scale = HEAD ** -0.5
        nt = (((1,), (1,)), ((), ()))
        tn = (((0,), (0,)), ((), ()))
        window = NBACK * rate
        nblk = T // window

        def one(it, carry):
            dk_carry, dv_carry = carry
            rho = it // nblk
            b = it % nblk
            start = b * window + rho
            rq = _rows(start, rate)
            rp = _rows(jnp.maximum(start - window, rho), rate)
            q = q_ref[rq, :] * scale
            db = db_ref[rq, :]
            ev = e_ref[rq, :]
            ls = lse_ref[rq, :]
            kcat = jnp.concatenate([k_ref[rp, :], k_ref[rq, :]], axis=0).astype(BF16)
            vcat = jnp.concatenate([v_ref[rp, :], v_ref[rq, :]], axis=0).astype(BF16)
            q2 = jnp.concatenate([jnp.where(lo, q, 0.0), jnp.where(lo, 0.0, q)], axis=0).astype(BF16)
            db2 = jnp.concatenate([jnp.where(lo, db, 0.0), jnp.where(lo, 0.0, db)], axis=0).astype(BF16)
            ls2 = jnp.concatenate([ls[:, 0:1], ls[:, HEAD:HEAD + 1]], axis=0)
            ev2 = jnp.concatenate([ev[:, 0:1], ev[:, HEAD:HEAD + 1]], axis=0)
            s = lax.dot_general(q2, kcat, nt, preferred_element_type=F32)
            pt = jnp.exp(s + jnp.where(b > 0, bias_all, bias_first) - ls2)
            dp = lax.dot_general(db2, vcat, nt, preferred_element_type=F32)
            ds = (pt * (dp - ev2)).astype(BF16)
            dq2 = jnp.dot(ds, kcat, preferred_element_type=F32) * scale
            dkc = lax.dot_general(ds, q2, tn, preferred_element_type=F32)
            dvc = lax.dot_general(pt.astype(BF16), db2, tn, preferred_element_type=F32)
            dq_ref[rq, :] = jnp.where(lo, dq2[0:NBACK], dq2[NBACK:])
            dk_ref[rp, :] = dk_carry + dkc[0:NBACK]
            dk_ref[rq, :] = dkc[NBACK:]
            dv_ref[rp, :] = dv_carry + dvc[0:NBACK]
            dv_ref[rq, :] = dvc[NBACK:]
            return dkc[NBACK:], dvc[NBACK:]

        def step(i, carry):
            for u in range(ATTN_BWD_UNROLL):
                carry = one(i * ATTN_BWD_UNROLL + u, carry)
            return carry

        zero = jnp.zeros((NBACK, PAIR), F32)
        lax.fori_loop(0, T // NBACK // ATTN_BWD_UNROLL, step, (zero, zero))

    pm = pl.BlockSpec((None, T, PAIR), lambda p: (p, 0, 0))
    return pl.pallas_call(
        body, name=name, grid=(NPAIR,), in_specs=[pm] * 6, out_specs=[pm] * 3,
        out_shape=[_sds((NPAIR, T, PAIR), F32)] * 3, compiler_params=_cparams(1, 56))(qn, kn, v, dbp, e, lse)


def _ab_in_bwd(pab, dcat, sgu_g, sgu_b, sgu_w, sgu_bias3, qg, kg, tabs, dqkv, rinvs, tm=256):
    def body(p_ref, dcat_ref, g_ref, b_ref, w_ref, bias_ref, qg_ref, kg_ref, c_ref, s1_ref, s2_ref, *rest):
        dq_refs, rinv_refs = rest[0:9], rest[9:15]
        o_ref, dwm_ref, dbias_ref, dsg_ref, dsb_ref, dgain_ref = rest[15:]
        i = pl.program_id(0)

        @pl.when(i == 0)
        def _():
            dwm_ref[...] = jnp.zeros_like(dwm_ref)
            dbias_ref[...] = jnp.zeros_like(dbias_ref)
            dsg_ref[...] = jnp.zeros_like(dsg_ref)
            dsb_ref[...] = jnp.zeros_like(dsb_ref)
            dgain_ref[...] = jnp.zeros_like(dgain_ref)

        zu = p_ref[:, 0:512].astype(F32)
        zv = p_ref[:, 512:1024].astype(F32)
        u = _gelu(zu)
        v = _gelu(zv)
        mu = jnp.mean(v, axis=-1, keepdims=True)
        vc = v - mu
        rstd = lax.rsqrt(jnp.mean(vc * vc, axis=-1, keepdims=True) + EPS)
        xhat = vc * rstd
        vn = (xhat * g_ref[...] + b_ref[...]).astype(BF16)
        da = dcat_ref[...].astype(F32)
        tri = _tril_mask()
        du_parts = [[None] * 4 for _ in range(tm // 128)]
        dvn_parts = [[None] * 4 for _ in range(tm // 128)]
        for gi in range(4):
            wg = jnp.where(tri, w_ref[gi], 0.0).astype(BF16)
            bg = bias_ref[gi]
            for c in range(tm // 128):
                rs, cs = slice(c * 128, (c + 1) * 128), slice(gi * 128, (gi + 1) * 128)
                vblk = vn[rs, cs]
                mixed = jnp.dot(wg, vblk, preferred_element_type=F32) + bg
                dab = da[rs, cs]
                du_parts[c][gi] = dab * mixed
                dmixed = dab * u[rs, cs]
                dmb = dmixed.astype(BF16)
                dvn_parts[c][gi] = lax.dot_general(wg, dmb, (((0,), (0,)), ((), ())), preferred_element_type=F32)
                dwm = lax.dot_general(dmb, vblk, (((1,), (1,)), ((), ())), preferred_element_type=F32)
                dwm_ref[gi] += jnp.where(tri, dwm, 0.0)
                dbias_ref[gi] += dmixed
        du = jnp.concatenate([jnp.concatenate(r, axis=1) for r in du_parts], axis=0)
        dvn = jnp.concatenate([jnp.concatenate(r, axis=1) for r in dvn_parts], axis=0)
        dsg_ref[...] += jnp.sum(dvn * xhat, axis=0, keepdims=True)
        dsb_ref[...] += jnp.sum(dvn, axis=0, keepdims=True)
        dxh = dvn * g_ref[...]
        dv = rstd * (dxh - jnp.mean(dxh, axis=-1, keepdims=True)
                     - xhat * jnp.mean(dxh * xhat, axis=-1, keepdims=True))
        o_ref[:, 0:512] = (du * _gelu_grad(zu)).astype(BF16)
        o_ref[:, 512:1024] = (dv * _gelu_grad(zv)).astype(BF16)

        blocks = _head_blocks()
        c, s1, s2 = c_ref[...], s1_ref[...], s2_ref[...]
        for g in range(3):
            dq_ref, dk_ref, dv_ref = dq_refs[3 * g:3 * g + 3]
            for p in range(NPAIR):
                for which, gains, src in ((0, qg_ref, dq_ref), (1, kg_ref, dk_ref)):
                    col = (2 + 3 * which + g) * 512 + p * PAIR
                    xr = p_ref[:, col:col + PAIR].astype(F32)
                    rinv = rinv_refs[2 * g + which][p].astype(F32)
                    xh = xr * rinv
                    dn = _rope_t(src[p], c, s1, s2)
                    row = 2 * g + which
                    dgain_ref[row:row + 1, :] += jnp.sum(dn * xh, axis=0, keepdims=True)
                    dxh2 = dn * gains[g:g + 1, :]
                    dx = rinv * (dxh2 - xh * _seg_mean_mxu(dxh2 * xh, blocks))
                    o_ref[:, col:col + PAIR] = dx.astype(BF16)
                col = (8 + g) * 512 + p * PAIR
                o_ref[:, col:col + PAIR] = dv_ref[p].astype(BF16)

    pm = pl.BlockSpec((NPAIR, tm, PAIR), lambda i: (0, i, 0))
    tab = pl.BlockSpec((tm, PAIR), lambda i: (i, 0))
    gain = pl.BlockSpec((3, PAIR), lambda i: (0, 0))
    vec = pl.BlockSpec((1, 512), lambda i: (0, 0))
    full = pl.BlockSpec((tm, AB_IN), lambda i: (i, 0))
    w4 = pl.BlockSpec((4, 128, 128), lambda i: (0, 0, 0))
    return pl.pallas_call(
        body, name="ab_in_bwd", grid=(T // tm,),
        in_specs=[full, pl.BlockSpec((tm, 512), lambda i: (i, 0)), vec, vec, w4,
                  pl.BlockSpec((4, 128, 1), lambda i: (0, 0, 0)), gain, gain, tab, tab, tab] + [pm] * 15,
        out_specs=[full, w4, w4, vec, vec, pl.BlockSpec((8, PAIR), lambda i: (0, 0))],
        out_shape=[_sds((T, AB_IN), BF16), _sds((4, 128, 128), F32), _sds((4, 128, 128), F32),
                   _sds((1, 512), F32), _sds((1, 512), F32), _sds((8, PAIR), F32)],
        compiler_params=_cparams(1))(pab, dcat, sgu_g, sgu_b, sgu_w, sgu_bias3, qg, kg, *tabs, *dqkv, *rinvs)


def _ln_stats(x):
    mu = jnp.mean(x, axis=-1, keepdims=True)
    xc = x - mu
    rstd = lax.rsqrt(jnp.mean(xc * xc, axis=-1, keepdims=True) + EPS)
    return xc * rstd, rstd


CONV_RC = 64


def _shifted_copies(src, dst, tm):
    dst[0] = src[...]
    for b in range(1, 8):
        dst[b, 0:tm + HALO - 8, :] = src[pl.ds(b, tm + HALO - 8), :]


def _offsets_by_phase(first):
    groups = {}
    for o in range(first, first + CONV_C_TAPS):
        groups.setdefault(o % 8, []).append(o)
    return sorted(groups.items())


def _window(shifted, b8, base, offsets, lanes):
    rows = 8 * (max(offsets) // 8) + CONV_RC
    return shifted[b8, pl.ds(base, rows), lanes].reshape(rows // 8, 8, 128)


def _cd_fwd(pcd, cw, cb, lg, lb, dw, tm=512):
    per = tm // HALO

    def body(p_ref, h_ref, cw_ref, cb_ref, lg_ref, lb_ref, dw_ref, cat_ref, c0_ref, c1_ref, dd_ref, y_ref,
             buf, buf2, sb):
        i = pl.program_id(0)
        live = jnp.where(i > 0, 1.0, 0.0)
        a = p_ref[:, 0:512].astype(F32)
        gt = p_ref[:, 512:1024].astype(F32)
        gb = p_ref[:, 1024:1536].astype(F32)
        gc = p_ref[:, 1536:2048].astype(F32)
        hv = p_ref[:, 2048:2560].astype(F32)
        c0 = a * _sigmoid(gt)
        dd = gc * hv
        buf[0:HALO, :] = h_ref[:, 0:512].astype(F32) * _sigmoid(h_ref[:, 512:1024].astype(F32)) * live
        buf[HALO:, :] = c0
        buf2[0:HALO, :] = h_ref[:, 1536:2048].astype(F32) * h_ref[:, 2048:2560].astype(F32) * live
        buf2[HALO:, :] = dd
        c0_ref[...] = c0.astype(BF16)
        dd_ref[...] = dd.astype(BF16)
        _shifted_copies(buf, sb, tm)

        def conv_rows(r, carry):
            base = pl.multiple_of(r * CONV_RC, CONV_RC)
            for c in range(4):
                lanes = slice(c * 128, (c + 1) * 128)
                acc = jnp.broadcast_to(cb_ref[:, lanes], (CONV_RC // 8, 8, 128))
                for b8, offsets in _offsets_by_phase(HALO - (CONV_C_TAPS - 1)):
                    win = _window(sb, b8, base, offsets, lanes)
                    for o in offsets:
                        j = o - (HALO - (CONV_C_TAPS - 1))
                        acc = acc + cw_ref[8 * j:8 * j + 8, lanes] * win[o // 8:o // 8 + CONV_RC // 8]
                c1_ref[pl.ds(base, CONV_RC), lanes] = acc.reshape(CONV_RC, 128)
            return carry

        lax.fori_loop(0, tm // CONV_RC, conv_rows, 0)
        xhat, _ = _ln_stats(c1_ref[...])
        c2 = xhat * lg_ref[...] + lb_ref[...]
        y = jnp.zeros((tm, 512), F32)
        for j in range(CONV_D_TAPS):
            y = y + dw_ref[j:j + 1, :] * buf2[pl.ds(HALO - (CONV_D_TAPS - 1) + j, tm), :]
        cat_ref[:, 0:512] = (c2 * _sigmoid(c2)).astype(BF16)
        cat_ref[:, 512:1024] = (gb * y).astype(BF16)
        y_ref[...] = y.astype(BF16)

    half = pl.BlockSpec((tm, 512), lambda i: (i, 0))
    vec = pl.BlockSpec((1, 512), lambda i: (0, 0))
    return pl.pallas_call(
        body, name="cd_fwd", grid=(T // tm,),
        in_specs=[pl.BlockSpec((tm, CD_IN), lambda i: (i, 0)),
                  pl.BlockSpec((HALO, CD_IN), lambda i: (jnp.maximum(i * per - 1, 0), 0)),
                  pl.BlockSpec((8 * 32, 512), lambda i: (0, 0)), vec, vec, vec, pl.BlockSpec((8, 512), lambda i: (0, 0))],
        out_specs=[pl.BlockSpec((tm, D), lambda i: (i, 0)), half, half, half, half],
        out_shape=[_sds((T, D), BF16), _sds((T, 512), BF16), _sds((T, 512), F32), _sds((T, 512), BF16),
                   _sds((T, 512), BF16)],
        scratch_shapes=[pltpu.VMEM((HALO + tm, 512), F32), pltpu.VMEM((HALO + tm, 512), F32),
                        pltpu.VMEM((8, HALO + tm, 512), F32)],
        compiler_params=_cparams(1))(pcd, pcd, cw, cb, lg, lb, dw)


def _cd_bwd_pw(dcat, c1, pcd, y, lg, lb, tm=512):
    def body(dcat_ref, c1_ref, gb_ref, y_ref, lg_ref, lb_ref, dc1_ref, dy3_ref, dgb_ref, dlg_ref, dlb_ref, dcb_ref):
        i = pl.program_id(0)

        @pl.when(i == 0)
        def _():
            dlg_ref[...] = jnp.zeros_like(dlg_ref)
            dlb_ref[...] = jnp.zeros_like(dlb_ref)
            dcb_ref[...] = jnp.zeros_like(dcb_ref)

        dc = dcat_ref[:, 0:512].astype(F32)
        ddo = dcat_ref[:, 512:1024].astype(F32)
        xhat, rstd = _ln_stats(c1_ref[...])
        c2 = xhat * lg_ref[...] + lb_ref[...]
        sg = _sigmoid(c2)
        dc2 = dc * sg * (1.0 + c2 * (1.0 - sg))
        dlg_ref[...] += jnp.sum(dc2 * xhat, axis=0, keepdims=True)
        dlb_ref[...] += jnp.sum(dc2, axis=0, keepdims=True)
        dxh = dc2 * lg_ref[...]
        dc1 = rstd * (dxh - jnp.mean(dxh, axis=-1, keepdims=True)
                      - xhat * jnp.mean(dxh * xhat, axis=-1, keepdims=True))
        dcb_ref[...] += jnp.sum(dc1, axis=0, keepdims=True)
        dc1_ref[...] = dc1
        dgb_ref[...] = (ddo * y_ref[...].astype(F32)).astype(BF16)
        dy3_ref[...] = ddo * gb_ref[...].astype(F32)

    half = pl.BlockSpec((tm, 512), lambda i: (i, 0))
    vec = pl.BlockSpec((1, 512), lambda i: (0, 0))
    return pl.pallas_call(
        body, name="cd_bwd_pw", grid=(T // tm,),
        in_specs=[pl.BlockSpec((tm, D), lambda i: (i, 0)), half, pl.BlockSpec((tm, 512), lambda i: (i, 2)), half,
                  vec, vec],
        out_specs=[half, half, half, vec, vec, vec],
        out_shape=[_sds((T, 512), F32), _sds((T, 512), F32), _sds((T, 512), BF16),
                   _sds((1, 512), F32), _sds((1, 512), F32), _sds((1, 512), F32)],
        compiler_params=_cparams(1))(dcat, c1, pcd, y, lg, lb)


def _cd_bwd_conv(pcd, dc1, dy3, c0, dd, dgb, cw8, dw, tm=256):
    per = tm // HALO
    nblk = T // tm
    last32 = T // HALO - 1

    def body(p_ref, dc1_ref, dc1n_ref, dy3_ref, dy3n_ref, c0_ref, dd_ref, dgb_ref, cw_ref, dw_ref,
             o_ref, dcw_ref, ddw_ref, dbuf, d3buf, sd, dc0_buf):
        i = pl.program_id(0)
        has_next = jnp.where(i < nblk - 1, 1.0, 0.0)

        @pl.when(i == 0)
        def _():
            dcw_ref[...] = jnp.zeros_like(dcw_ref)
            ddw_ref[...] = jnp.zeros_like(ddw_ref)

        dbuf[0:tm, :] = dc1_ref[...]
        dbuf[tm:, :] = dc1n_ref[...] * has_next
        d3buf[0:tm, :] = dy3_ref[...]
        d3buf[tm:, :] = dy3n_ref[...] * has_next
        _shifted_copies(dbuf, sd, tm)
        n_tiles = tm // CONV_RC

        phases = _offsets_by_phase(0)

        def dc0_rows(r, carry):
            base = pl.multiple_of(r * CONV_RC, CONV_RC)
            for c in range(4):
                lanes = slice(c * 128, (c + 1) * 128)
                acc = jnp.zeros((CONV_RC // 8, 8, 128), F32)
                for b8, offsets in phases:
                    win = _window(sd, b8, base, offsets, lanes)
                    for o in offsets:
                        j = CONV_C_TAPS - 1 - o
                        acc = acc + cw_ref[8 * j:8 * j + 8, lanes] * win[o // 8:o // 8 + CONV_RC // 8]
                dc0_buf[pl.ds(base, CONV_RC), lanes] = acc.reshape(CONV_RC, 128)
            return carry

        lax.fori_loop(0, n_tiles, dc0_rows, 0)

        for c in range(4):
            lanes = slice(c * 128, (c + 1) * 128)
            for b8, offsets in phases:
                def dw_rows(r, accs, lanes=lanes, b8=b8, offsets=offsets):
                    base = pl.multiple_of(r * CONV_RC, CONV_RC)
                    xin = c0_ref[pl.ds(base, CONV_RC), lanes].astype(F32).reshape(CONV_RC // 8, 8, 128)
                    win = _window(sd, b8, base, offsets, lanes)
                    return tuple(acc + jnp.sum(xin * win[o // 8:o // 8 + CONV_RC // 8], axis=0)
                                 for acc, o in zip(accs, offsets))

                accs = lax.fori_loop(0, n_tiles, dw_rows, tuple(jnp.zeros((8, 128), F32) for _ in offsets))
                for acc, o in zip(accs, offsets):
                    j = CONV_C_TAPS - 1 - o
                    dcw_ref[j:j + 1, lanes] += jnp.sum(acc, axis=0, keepdims=True)

        dc0 = dc0_buf[...]
        ddin = dd_ref[...].astype(F32)
        ddd = jnp.zeros((tm, 512), F32)
        for j in range(CONV_D_TAPS):
            dy_shift = d3buf[pl.ds(CONV_D_TAPS - 1 - j, tm), :]
            ddd = ddd + dw_ref[j:j + 1, :] * dy_shift
            ddw_ref[j:j + 1, :] += jnp.sum(ddin * dy_shift, axis=0, keepdims=True)

        a = p_ref[:, 0:512].astype(F32)
        gt = p_ref[:, 512:1024].astype(F32)
        gc = p_ref[:, 1536:2048].astype(F32)
        hv = p_ref[:, 2048:2560].astype(F32)
        sg = _sigmoid(gt)
        o_ref[:, 0:512] = (dc0 * sg).astype(BF16)
        o_ref[:, 512:1024] = (dc0 * a * sg * (1.0 - sg)).astype(BF16)
        o_ref[:, 1024:1536] = dgb_ref[...]
        o_ref[:, 1536:2048] = (ddd * hv).astype(BF16)
        o_ref[:, 2048:2560] = (ddd * gc).astype(BF16)

    half = pl.BlockSpec((tm, 512), lambda i: (i, 0))
    nxt = pl.BlockSpec((HALO, 512), lambda i: (jnp.minimum((i + 1) * per, last32), 0))
    full = pl.BlockSpec((tm, CD_IN), lambda i: (i, 0))
    return pl.pallas_call(
        body, name="cd_bwd_conv", grid=(nblk,),
        in_specs=[full, half, nxt, half, nxt, half, half, half,
                  pl.BlockSpec((8 * 32, 512), lambda i: (0, 0)), pl.BlockSpec((8, 512), lambda i: (0, 0))],
        out_specs=[full, pl.BlockSpec((32, 512), lambda i: (0, 0)), pl.BlockSpec((8, 512), lambda i: (0, 0))],
        out_shape=[_sds((T, CD_IN), BF16), _sds((32, 512), F32), _sds((8, 512), F32)],
        scratch_shapes=[pltpu.VMEM((tm + HALO, 512), F32), pltpu.VMEM((tm + HALO, 512), F32),
                        pltpu.VMEM((8, tm + HALO, 512), F32), pltpu.VMEM((tm, 512), F32)],
        compiler_params=_cparams(1))(pcd, dc1, dc1, dy3, dy3, c0, dd, dgb, cw8, dw)


def _local_step(x, tgt, W, fetch=None, on_grad=None):
    W = dict(W)
    if fetch is None:
        fetch = lambda stage, after: {}
    if on_grad is None:
        on_grad = lambda key, arr: None
    tabs = _rope_tables()
    qg = jnp.tile(W["q_norm_g"], (1, 2))
    kg = jnp.tile(W["k_norm_g"], (1, 2))
    bias3 = W["sgu_bias"].reshape(4, 128, 1)
    G = {}

    h0 = _rms_fwd(x, W["ab_norm_g"], "rms_fwd_ab", dep=W.get("dep_first"))
    W.update(fetch("ab_in", h0))
    pab = _mm_nt(h0, W["wt_ab_in"], "mm_ab_in", dep=W.get("dep0"))
    cat_ab = _mix_a_fwd(pab, W["sgu_norm_g"], W["sgu_norm_b"], W["sgu_w"], bias3)
    qkv, rinvs = _prep_fwd(pab, qg, kg, tabs)
    outs, lses = [], []
    for g, rate in enumerate(DIL_RATES):
        o, l = _attn_fwd(qkv[3 * g], qkv[3 * g + 1], qkv[3 * g + 2], rate, f"attn_fwd_{g}", dep=W.get(f"dep_attn{g}"))
        outs.append(o)
        lses.append(l)
        W.update(fetch(f"attn{g}", o))
    cat_ab, lse = _merge_fwd(cat_ab, outs, lses)
    W.update(fetch("ab_out", lse))
    x1, h1 = _mm_nn(cat_ab, W["w_ab_out"], "mm_ab_out", mode="rms", resid=x, gain=W["ffn_norm_g"][0:1])
    pf0, act0 = _ffn_in(h1, W["wt_ffn_in0"], "ffn_in0")
    W.update(fetch("ffn_down0", act0))
    x2, h2 = _mm_nn(act0, W["w_ffn_down0"], "mm_ffn_down0", mode="rms", resid=x1, gain=W["cd_norm_g"],
                    dep=W.get("dep_down0"))
    W.update(fetch("cd_in", h2))
    pcd = _mm_nt(h2, W["wt_cd_in"], "mm_cd_in")
    cw8 = jnp.repeat(W["conv_c_w32"], 8, axis=0)
    cat_cd, c0, c1, dd, yv = _cd_fwd(pcd, cw8, W["conv_c_b"], W["c_ln_g"], W["c_ln_b"], W["conv_d_w8"])
    x3, h3 = _mm_nn(cat_cd, W["w_cd_out"], "mm_cd_out", mode="rms", resid=x2, gain=W["ffn_norm_g"][1:2])
    pf1, act1 = _ffn_in(h3, W["wt_ffn_in1"], "ffn_in1")
    dy, dyb, loss_cols = _mm_nn(act1, W["w_ffn_down1"], "mm_ffn_down1", mode="loss", resid=x3, tgt=tgt)

    def ffn_bwd(xin, h, pf, act, dres, dresb, layer):
        G[f"w_ffn_down{layer}"] = _mm_tn(act, dresb, f"mm_g_ffn_down{layer}")
        dep = on_grad(f"w_ffn_down{layer}", G[f"w_ffn_down{layer}"])
        dpf = _ffn_dact(dresb, W[f"w_ffn_down{layer}"], pf, f"ffn_dact{layer}", dep=dep)
        G[f"wt_ffn_in{layer}"] = _mm_tn(dpf, h, f"mm_g_ffn_in{layer}")
        dep = on_grad(f"wt_ffn_in{layer}", G[f"wt_ffn_in{layer}"])
        dx, dxb, G[f"ffn_norm_g{layer}"] = _mm_dh_rms_bwd(
            dpf, W[f"wt_ffn_in{layer}"], xin, W["ffn_norm_g"][layer:layer + 1], dres, f"mm_d_h_ffn{layer}", dep=dep)
        return dx, dxb

    dx3, dx3b = ffn_bwd(x3, h3, pf1, act1, dy, dyb, 1)

    G["w_cd_out"] = _mm_tn(cat_cd, dx3b, "mm_g_cd_out")
    dep = on_grad("w_cd_out", G["w_cd_out"])
    dcat_cd = _mm_nt(dx3b, W["w_cd_out"], "mm_d_cat_cd", dep=dep)
    dc1, dy3, dgb, G["c_ln_g"], G["c_ln_b"], G["conv_c_b"] = _cd_bwd_pw(dcat_cd, c1, pcd, yv, W["c_ln_g"], W["c_ln_b"])
    dpcd, G["conv_c_w32"], G["conv_d_w8"] = _cd_bwd_conv(pcd, dc1, dy3, c0, dd, dgb, cw8, W["conv_d_w8"])
    G["wt_cd_in"] = _mm_tn(dpcd, h2, "mm_g_cd_in")
    dep = on_grad("wt_cd_in", G["wt_cd_in"])
    dx2, dx2b, G["cd_norm_g"] = _mm_dh_rms_bwd(dpcd, W["wt_cd_in"], x2, W["cd_norm_g"], dx3, "mm_d_h_cd", dep=dep)

    dx1, dx1b = ffn_bwd(x1, h1, pf0, act0, dx2, dx2b, 0)

    G["w_ab_out"] = _mm_tn(cat_ab, dx1b, "mm_g_ab_out")
    dep = on_grad("w_ab_out", G["w_ab_out"])
    dcat_ab = _mm_nt(dx1b, W["w_ab_out"], "mm_d_cat_ab", dep=dep)
    dbp, e = _b_pre_bwd(dcat_ab, cat_ab)
    dqkv = []
    for g, rate in enumerate(DIL_RATES):
        dqkv += _attn_bwd(qkv[3 * g], qkv[3 * g + 1], qkv[3 * g + 2], dbp, e, lse, rate, f"attn_bwd_{g}")
    dpab, G["sgu_w"], dbias_part, G["sgu_norm_g"], G["sgu_norm_b"], dgain = _ab_in_bwd(
        pab, dcat_ab, W["sgu_norm_g"], W["sgu_norm_b"], W["sgu_w"], bias3, qg, kg, tabs, dqkv, rinvs)
    G["sgu_bias"] = jnp.sum(dbias_part, axis=-1)
    dgain = dgain[0:6, 0:HEAD] + dgain[0:6, HEAD:PAIR]
    G["q_norm_g"] = dgain[0::2]
    G["k_norm_g"] = dgain[1::2]
    G["loss_cols"] = loss_cols
    dep = on_grad("small", G)
    G["wt_ab_in"] = _mm_tn(dpab, h0, "mm_g_ab_in", dep=dep)
    dep = on_grad("wt_ab_in", G["wt_ab_in"])
    grad_x, G["ab_norm_g"] = _mm_dh_rms_bwd(dpab, W["wt_ab_in"], x, W["ab_norm_g"], dx1, "mm_d_h_ab", dep=dep,
                                            bf16_copy=False)
    return loss_cols, grad_x, G


def _my_place():
    return lax.axis_index("x"), lax.axis_index("y"), lax.axis_index("c")


def _dev_index(px, py, pc):
    return 4 * px + 2 * py + pc


def _flip(place, k):
    x, y, c = place
    return (1 - x if k & 4 else x, 1 - y if k & 2 else y, 1 - c if k & 1 else c)


def _landing(shape, dtype, own):
    buf = lax.empty(shape, dtype)
    for lead, part in own:
        buf = lax.dynamic_update_slice(buf, part.reshape((1,) * len(lead) + part.shape),
                                       tuple(lead) + (0,) * part.ndim)
    return buf


HBM_ONLY = pl.BlockSpec(memory_space=pltpu.HBM)
SEM_SPEC = pl.BlockSpec(memory_space=pltpu.SEMAPHORE)
IN_FLIGHT = pltpu.CompilerParams(has_side_effects=pltpu.SideEffectType.DATAFLOW_SIDE_EFFECTING)


def _in_hbm(a):
    return pltpu.with_memory_space_constraint(a, pltpu.HBM)


def _exchange_start(name, srcs, lands, items, dep=None):
    ns, nl, ni = len(srcs), len(lands), len(items)

    def body(*refs):
        S, L = refs[0:ns], refs[ns:ns + nl]
        first_out = ns + nl + (0 if dep is None else 1)
        send_sems, recv_sems, token = refs[first_out], refs[first_out + 1], refs[-1]
        me = _my_place()
        mi = _dev_index(*me)
        for i, (src, dst) in enumerate(items):
            for k in range(1, NDEV):
                peer = _flip(me, k)
                pltpu.make_async_remote_copy(
                    src_ref=src(S, _dev_index(*peer)), dst_ref=dst(L, mi), send_sem=send_sems.at[7 * i + k - 1],
                    recv_sem=recv_sems.at[7 * i + k - 1], device_id=peer, device_id_type=MESH).start()
        token[...] = jnp.zeros_like(token)

    thru = [pltpu.HBM(a.shape, a.dtype) for a in list(srcs) + list(lands)]
    args = [_in_hbm(a) for a in srcs] + [_in_hbm(a) for a in lands]
    in_specs = [HBM_ONLY] * (ns + nl)
    if dep is not None:
        args.append(dep)
        in_specs.append(HBM_SPEC)
    outs = pl.pallas_call(
        body, name=name, in_specs=in_specs,
        out_shape=(pltpu.SemaphoreType.DMA((7 * ni,)), pltpu.SemaphoreType.DMA((7 * ni,)), *thru, _sds((8, 128), F32)),
        out_specs=(SEM_SPEC, SEM_SPEC, *[HBM_ONLY] * (ns + nl), pl.BlockSpec(memory_space=pltpu.VMEM)),
        input_output_aliases={j: 2 + j for j in range(ns + nl)}, compiler_params=IN_FLIGHT)(*args)
    return dict(send=outs[0], recv=outs[1], srcs=list(outs[2:2 + ns]), lands=list(outs[2 + ns:2 + ns + nl]),
                token=outs[-1], items=items)


def _exchange_wait(name, states, after):
    after = list(after) if isinstance(after, (list, tuple)) else [after]
    counts = [(len(st["srcs"]), len(st["lands"]), len(st["items"])) for st in states]
    n_arrays = sum(c[0] + c[1] for c in counts)

    def body(*refs):
        me = _my_place()
        mi = _dev_index(*me)
        pos = 0
        sem_pos = n_arrays
        for st, (ns, nl, ni) in zip(states, counts):
            S, L = refs[pos:pos + ns], refs[pos + ns:pos + ns + nl]
            send_sems, recv_sems = refs[sem_pos], refs[sem_pos + 1]
            pos += ns + nl
            sem_pos += 2
            for i, (src, dst) in enumerate(st["items"]):
                for k in range(1, NDEV):
                    cp = pltpu.make_async_remote_copy(
                        src_ref=src(S, mi), dst_ref=dst(L, mi), send_sem=send_sems.at[7 * i + k - 1],
                        recv_sem=recv_sems.at[7 * i + k - 1], device_id=me, device_id_type=MESH)
                    cp.wait_send()
                    cp.wait_recv()

    arrays, sems = [], []
    for st in states:
        arrays += st["srcs"] + st["lands"]
        sems += [st["send"], st["recv"]]
    outs = pl.pallas_call(
        body, name=name, in_specs=[HBM_ONLY] * n_arrays + [SEM_SPEC] * len(sems) + [HBM_SPEC] * len(after),
        out_shape=tuple(pltpu.HBM(a.shape, a.dtype) for a in arrays), out_specs=tuple([HBM_ONLY] * n_arrays),
        input_output_aliases={j: j for j in range(n_arrays)}, compiler_params=IN_FLIGHT)(*arrays, *sems, *after)
    lands, pos = [], 0
    for ns, nl, _ in counts:
        lands.append(list(outs[pos + ns:pos + ns + nl]))
        pos += ns + nl
    return lands


def _place_and_neighbours():
    x, y, c = _my_place()
    return (x, y, c), (x, y, 1 - c), [(1 - x, y), (x, 1 - y), (1 - x, 1 - y)]


def _gather_start(name, srcs, lands, items, dep=None):
    ns, nl, ni = len(srcs), len(lands), len(items)

    def body(*refs):
        S, L = refs[0:ns], refs[ns:ns + nl]
        first_out = ns + nl + (0 if dep is None else 1)
        send_sems, recv_sems, token = refs[first_out], refs[first_out + 1], refs[-1]
        me, sib, chips = _place_and_neighbours()
        mi = _dev_index(*me)
        for i, (src, dst) in enumerate(items):
            for k, to in enumerate([sib] + [(*chip, me[2]) for chip in chips]):
                pltpu.make_async_remote_copy(
                    src_ref=src(S), dst_ref=dst(L, mi), send_sem=send_sems.at[4 * i + k],
                    recv_sem=recv_sems.at[4 * i + k], device_id=to, device_id_type=MESH).start()
        token[...] = jnp.zeros_like(token)

    thru = [pltpu.HBM(a.shape, a.dtype) for a in list(srcs) + list(lands)]
    args = [_in_hbm(a) for a in srcs] + [_in_hbm(a) for a in lands]
    in_specs = [HBM_ONLY] * (ns + nl)
    if dep is not None:
        args.append(dep)
        in_specs.append(HBM_SPEC)
    outs = pl.pallas_call(
        body, name=name, in_specs=in_specs,
        out_shape=(pltpu.SemaphoreType.DMA((4 * ni,)), pltpu.SemaphoreType.DMA((4 * ni,)), *thru, _sds((8, 128), F32)),
        out_specs=(SEM_SPEC, SEM_SPEC, *[HBM_ONLY] * (ns + nl), pl.BlockSpec(memory_space=pltpu.VMEM)),
        input_output_aliases={j: 2 + j for j in range(ns + nl)}, compiler_params=IN_FLIGHT)(*args)
    return dict(send=outs[0], recv=outs[1], srcs=list(outs[2:2 + ns]), lands=list(outs[2 + ns:2 + ns + nl]),
                token=outs[-1], items=items)


def _gather_forward(name, st, after):
    nl, ni = len(st["lands"]), len(st["items"])

    def body(*refs):
        L, recv_sems = refs[0:nl], refs[nl]
        fwd_send, fwd_recv, token = refs[2 * nl + 2], refs[2 * nl + 3], refs[-1]
        me, sib, chips = _place_and_neighbours()
        for i, (_, dst) in enumerate(st["items"]):
            for j, chip in enumerate(chips):
                blk = dst(L, _dev_index(*chip, me[2]))
                pltpu.make_async_remote_copy(
                    src_ref=blk, dst_ref=blk, send_sem=fwd_send.at[3 * i + j], recv_sem=recv_sems.at[4 * i + 1 + j],
                    device_id=me, device_id_type=MESH).wait_recv()
                pltpu.make_async_remote_copy(
                    src_ref=blk, dst_ref=blk, send_sem=fwd_send.at[3 * i + j], recv_sem=fwd_recv.at[3 * i + j],
                    device_id=sib, device_id_type=MESH).start()
        token[...] = jnp.zeros_like(token)

    outs = pl.pallas_call(
        body, name=name, in_specs=[HBM_ONLY] * nl + [SEM_SPEC, HBM_SPEC],
        out_shape=(*[pltpu.HBM(a.shape, a.dtype) for a in st["lands"]], pltpu.SemaphoreType.DMA((3 * ni,)),
                   pltpu.SemaphoreType.DMA((3 * ni,)), _sds((8, 128), F32)),
        out_specs=(*[HBM_ONLY] * nl, SEM_SPEC, SEM_SPEC, pl.BlockSpec(memory_space=pltpu.VMEM)),
        input_output_aliases={j: j for j in range(nl)}, compiler_params=IN_FLIGHT)(*st["lands"], st["recv"], after)
    return dict(st, lands=list(outs[0:nl]), fwd_send=outs[nl], fwd_recv=outs[nl + 1], token=outs[-1])


def _gather_wait(name, st, after):
    ns, nl, ni = len(st["srcs"]), len(st["lands"]), len(st["items"])

    def body(*refs):
        S, L = refs[0:ns], refs[ns:ns + nl]
        send_sems, recv_sems, fwd_send, fwd_recv = refs[ns + nl:ns + nl + 4]
        me, sib, chips = _place_and_neighbours()
        mi = _dev_index(*me)
        for i, (src, dst) in enumerate(st["items"]):
            mine = dst(L, mi)
            for k in range(4):
                pltpu.make_async_remote_copy(
                    src_ref=src(S), dst_ref=mine, send_sem=send_sems.at[4 * i + k], recv_sem=recv_sems.at[4 * i + k],
                    device_id=me, device_id_type=MESH).wait_send()
            pltpu.make_async_remote_copy(
                src_ref=src(S), dst_ref=mine, send_sem=send_sems.at[4 * i], recv_sem=recv_sems.at[4 * i],
                device_id=me, device_id_type=MESH).wait_recv()
            for j in range(3):
                cp = pltpu.make_async_remote_copy(
                    src_ref=mine, dst_ref=mine, send_sem=fwd_send.at[3 * i + j], recv_sem=fwd_recv.at[3 * i + j],
                    device_id=me, device_id_type=MESH)
                cp.wait_send()
                cp.wait_recv()

    arrays = st["srcs"] + st["lands"]
    outs = pl.pallas_call(
        body, name=name, in_specs=[HBM_ONLY] * (ns + nl) + [SEM_SPEC] * 4 + [HBM_SPEC],
        out_shape=tuple(pltpu.HBM(a.shape, a.dtype) for a in arrays), out_specs=tuple([HBM_ONLY] * (ns + nl)),
        input_output_aliases={j: j for j in range(ns + nl)},
        compiler_params=IN_FLIGHT)(*arrays, st["send"], st["recv"], st["fwd_send"], st["fwd_recv"], after)
    return list(outs[ns:ns + nl])


def _sum_slots(land):
    def body(l_ref, o_ref):
        acc = l_ref[0]
        for d in range(1, NDEV):
            acc = acc + l_ref[d]
        o_ref[...] = acc

    vm = pl.BlockSpec(memory_space=pltpu.VMEM)
    return pl.pallas_call(body, name="sum_small", out_shape=_sds(land.shape[1:], F32), in_specs=[vm], out_specs=vm)(land)


def _adam_math(w, g, m, v):
    m2 = ADAM_B1 * m + (1.0 - ADAM_B1) * g
    v2 = ADAM_B2 * v + (1.0 - ADAM_B2) * (g * g)
    delta = -ADAM_LR * ((m2 * ADAM_C1) / (jnp.sqrt(v2 * ADAM_C2) + ADAM_EPS) + ADAM_WD * w)
    return delta, m2, v2


def _adam_layer(land, sel, w, m, v, layer, name, prev=None, tc=512):
    R = land.shape[2]

    def body(l_ref, w_ref, m_ref, v_ref, *rest):
        g_out, d_out, m_out, v_out = rest[-4:]
        g = l_ref[0].astype(F32)
        for d in range(1, NDEV):
            g = g + l_ref[d].astype(F32)
        delta, m2, v2 = _adam_math(w_ref[...], g, m_ref[...], v_ref[...])
        g_out[...] = g
        d_out[...] = delta
        m_out[...] = m2
        v_out[...] = v2

    wspec = pl.BlockSpec((None, R, tc), lambda i: (layer, 0, i))
    in_specs = [pl.BlockSpec((None, NDEV, R, tc), lambda i: (sel, 0, 0, i)), wspec, wspec, wspec]
    args = [land, w, m, v]
    aliases = {}
    if prev is not None:
        in_specs += [HBM_SPEC] * 4
        args += list(prev)
        aliases = {4 + j: j for j in range(4)}
    return pl.pallas_call(
        body, name=name, grid=(D // tc,), in_specs=in_specs, out_specs=[wspec] * 4,
        out_shape=[_sds(w.shape, F32)] * 4, input_output_aliases=aliases, compiler_params=_cparams(1))(*args)


def _adam_stacked(lands, sel, w, m, v, name):
    res = None
    for layer, land in enumerate(lands):
        res = _adam_layer(land, sel, w, m, v, layer, f"{name}{layer}", prev=res)
    return res


def _adam_small(ws, gs, ms, vs):
    n = len(ws)

    def body(*refs):
        w_r, g_r, m_r, v_r = refs[0:n], refs[n:2 * n], refs[2 * n:3 * n], refs[3 * n:4 * n]
        d_o, m_o, v_o = refs[4 * n:5 * n], refs[5 * n:6 * n], refs[6 * n:7 * n]
        for i in range(n):
            delta, m2, v2 = _adam_math(w_r[i][...], g_r[i][...], m_r[i][...], v_r[i][...])
            d_o[i][...] = delta
            m_o[i][...] = m2
            v_o[i][...] = v2

    vm = pl.BlockSpec(memory_space=pltpu.VMEM)
    shapes = [_sds(w.shape, F32) for w in ws]
    outs = pl.pallas_call(body, name="adam_small", in_specs=[vm] * (4 * n), out_specs=[vm] * (3 * n),
                          out_shape=shapes * 3)(*ws, *gs, *ms, *vs)
    return outs[0:n], outs[n:2 * n], outs[2 * n:3 * n]


def _adam_of_slots(land, w, m, v, name):
    def body(l_ref, w_ref, m_ref, v_ref, g_o, d_o, m_o, v_o):
        g = l_ref[0]
        for d in range(1, NDEV):
            g = g + l_ref[d]
        g_o[...] = g
        d_o[...], m_o[...], v_o[...] = _adam_math(w_ref[...], g, m_ref[...], v_ref[...])

    vm = pl.BlockSpec(memory_space=pltpu.VMEM)
    return pl.pallas_call(body, name=name, in_specs=[vm] * 4, out_specs=[vm] * 4,
                          out_shape=[_sds(w.shape, F32)] * 4)(land, w, m, v)


WEIGHT_NAMES = ("ab_norm_g", "ab_w_in", "sgu_norm_g", "sgu_norm_b", "sgu_w", "sgu_bias", "q_norm_g", "k_norm_g",
                "ab_w_out", "cd_norm_g", "cd_w_in", "conv_c_w", "conv_c_b", "c_ln_g", "c_ln_b", "conv_d_w",
                "cd_w_out", "ffn_norm_g", "ffn_w_gate", "ffn_w_up", "ffn_w_down")
SMALL_2D = (("sgu_norm_g", (1, 512)), ("sgu_norm_b", (1, 512)), ("sgu_w", (512, 128)),
            ("sgu_bias", (4, 128)), ("q_norm_g", (3, 64)), ("k_norm_g", (3, 64)), ("cd_norm_g", (1, 128)),
            ("conv_c_w", (31, 64)), ("conv_c_b", (1, 64)), ("c_ln_g", (1, 64)), ("c_ln_b", (1, 64)),
            ("conv_d_w", (3, 64)), ("ffn_norm_g", (2, 1024)))
SHARD_C = 64


def _pack_rows(parts, rows):
    flat = jnp.concatenate([p.reshape(-1) for p in parts])
    return jnp.pad(flat, (0, rows * 128 - flat.shape[0])).reshape(rows, 128)


def kernel(x, ab_norm_g, ab_w_in, sgu_norm_g, sgu_norm_b, sgu_w, sgu_bias, q_norm_g, k_norm_g, ab_w_out, cd_norm_g, cd_w_in, conv_c_w, conv_c_b, c_ln_g, c_ln_b, conv_d_w, cd_w_out, ffn_norm_g, ffn_w_gate, ffn_w_up, ffn_w_down, loss_target, m_ab_norm_g, m_ab_w_in, m_sgu_norm_g, m_sgu_norm_b, m_sgu_w, m_sgu_bias, m_q_norm_g, m_k_norm_g, m_ab_w_out, m_cd_norm_g, m_cd_w_in, m_conv_c_w, m_conv_c_b, m_c_ln_g, m_c_ln_b, m_conv_d_w, m_cd_w_out, m_ffn_norm_g, m_ffn_w_gate, m_ffn_w_up, m_ffn_w_down, v_ab_norm_g, v_ab_w_in, v_sgu_norm_g, v_sgu_norm_b, v_sgu_w, v_sgu_bias, v_q_norm_g, v_k_norm_g, v_ab_w_out, v_cd_norm_g, v_cd_w_in, v_conv_c_w, v_conv_c_b, v_c_ln_g, v_c_ln_b, v_conv_d_w, v_cd_w_out, v_ffn_norm_g, v_ffn_w_gate, v_ffn_w_up, v_ffn_w_down):
    w = dict(zip(WEIGHT_NAMES, (ab_norm_g, ab_w_in, sgu_norm_g, sgu_norm_b, sgu_w, sgu_bias, q_norm_g, k_norm_g, ab_w_out, cd_norm_g, cd_w_in, conv_c_w, conv_c_b, c_ln_g, c_ln_b, conv_d_w, cd_w_out, ffn_norm_g, ffn_w_gate, ffn_w_up, ffn_w_down)))
    m = dict(zip(WEIGHT_NAMES, (m_ab_norm_g, m_ab_w_in, m_sgu_norm_g, m_sgu_norm_b, m_sgu_w, m_sgu_bias, m_q_norm_g, m_k_norm_g, m_ab_w_out, m_cd_norm_g, m_cd_w_in, m_conv_c_w, m_conv_c_b, m_c_ln_g, m_c_ln_b, m_conv_d_w, m_cd_w_out, m_ffn_norm_g, m_ffn_w_gate, m_ffn_w_up, m_ffn_w_down)))
    v = dict(zip(WEIGHT_NAMES, (v_ab_norm_g, v_ab_w_in, v_sgu_norm_g, v_sgu_norm_b, v_sgu_w, v_sgu_bias, v_q_norm_g, v_k_norm_g, v_ab_w_out, v_cd_norm_g, v_cd_w_in, v_conv_c_w, v_conv_c_b, v_c_ln_g, v_c_ln_b, v_conv_d_w, v_cd_w_out, v_ffn_norm_g, v_ffn_w_gate, v_ffn_w_up, v_ffn_w_down)))
    me = _dev_index(*_my_place())

    small_local = _pack_rows([w["cd_norm_g"], w["conv_c_w"], w["conv_c_b"], w["c_ln_g"], w["c_ln_b"], w["conv_d_w"]], 24)
    r_ff = DFF // NDEV
    one = lambda a: (lambda S, j: S[a])
    slot = lambda b: (lambda L, s: L[b].at[s])
    slot2 = lambda b, part: (lambda L, s: L[b].at[part, s])
    shard = lambda a: (lambda S: S[a])

    def later(a):
        return lax.optimization_barrier((a, gathers[0]["token"]))[0]

    def layer_shards(layer):
        return (later(w["ffn_w_gate"][layer]).T.astype(BF16), later(w["ffn_w_up"][layer]).T.astype(BF16),
                later(w["ffn_w_down"][layer]).astype(BF16))

    def gathered(own):
        return _landing((NDEV,) + own.shape, BF16, [((me,), own)])

    def gathered2(a, b):
        return _landing((2, NDEV) + a.shape, BF16, [((0, me), a), ((1, me), b)])

    ab_in_s = w["ab_w_in"][0].T.astype(BF16)
    gathers = {0: _gather_start(
        "gather0_start", [ab_in_s, small_local],
        [gathered(ab_in_s), _landing((NDEV,) + small_local.shape, F32, [((me,), small_local)])],
        [(shard(0), slot(0)), (shard(1), slot(1))])}

    def chan(flat, lo, taps):
        return flat[:, lo:lo + taps * SHARD_C].reshape(NDEV, taps, SHARD_C).transpose(1, 0, 2).reshape(taps, 512)

    def fetch(stage, after):
        if stage == "ab_in":
            gathers[0] = _gather_forward("gather0_forward", gathers[0], after)
            l_ab_in, l_small = _gather_wait("gather0_wait", gathers[0], gathers[0]["token"])
            ab_out_s = later(w["ab_w_out"][0]).astype(BF16)
            gate0, up0, down0 = layer_shards(0)
            gathers[1] = _gather_start(
                "gather1_start", [ab_out_s, gate0, up0, down0],
                [gathered(ab_out_s), gathered2(gate0, up0), gathered(down0)],
                [(shard(0), slot(0)), (shard(1), slot2(1, 0)), (shard(2), slot2(1, 1)), (shard(3), slot(2))],
                dep=l_small)
            flat = l_small.reshape(NDEV, 24 * 128)
            return {
                "wt_ab_in": l_ab_in.reshape(AB_IN, D), "dep0": gathers[1]["token"],
                "cd_norm_g": flat[:, 0:128].reshape(1, D),
                "conv_c_w32": jnp.pad(chan(flat, 128, CONV_C_TAPS), ((0, 1), (0, 0))),
                "conv_c_b": chan(flat, 2112, 1), "c_ln_g": chan(flat, 2176, 1), "c_ln_b": chan(flat, 2240, 1),
                "conv_d_w8": jnp.pad(chan(flat, 2304, CONV_D_TAPS), ((0, 8 - CONV_D_TAPS), (0, 0))),
            }
        if stage == "attn0":
            cd_in_s, cd_out_s = later(w["cd_w_in"][0]).T.astype(BF16), later(w["cd_w_out"][0]).astype(BF16)
            gate1, up1, down1 = layer_shards(1)
            gathers[2] = _gather_start(
                "gather2_start", [cd_in_s, cd_out_s, gate1, up1, down1],
                [gathered(cd_in_s), gathered(cd_out_s), gathered2(gate1, up1), gathered(down1)],
                [(shard(0), slot(0)), (shard(1), slot(1)), (shard(2), slot2(2, 0)), (shard(3), slot2(2, 1)),
                 (shard(4), slot(3))], dep=after)
            return {"dep_attn1": gathers[2]["token"]}
        if stage == "attn1":
            gathers[1] = _gather_forward("gather1_forward", gathers[1], after)
            return {"dep_attn2": gathers[1]["token"]}
        if stage == "ab_out":
            l_out, l_ffn, l_down = _gather_wait("gather1_wait", gathers[1], after)
            return {"w_ab_out": l_out.reshape(D, D), "wt_ffn_in0": l_ffn.reshape(2 * DFF, D),
                    "w_ffn_down0": l_down.reshape(DFF, D)}
        if stage == "ffn_down0":
            gathers[2] = _gather_forward("gather2_forward", gathers[2], after)
            return {"dep_down0": gathers[2]["token"]}
        if stage == "cd_in":
            l_in, l_out, l_ffn, l_down = _gather_wait("gather2_wait", gathers[2], after)
            return {"wt_cd_in": l_in.reshape(CD_IN, D), "w_cd_out": l_out.reshape(D, D),
                    "wt_ffn_in1": l_ffn.reshape(2 * DFF, D), "w_ffn_down1": l_down.reshape(DFF, D)}
        return {}

    scatters = {}
    rides_with = {"w_ffn_down1": "wt_ffn_in1", "w_cd_out": "wt_cd_in", "w_ffn_down0": "wt_ffn_in0"}
    held = {}
    smalls = {}

    def small_exchange(name, block):
        land = _landing((NDEV,) + block.shape, F32, [((me,), block)])
        return _exchange_start(name, [block], [land], [(one(0), slot(0))])

    def on_grad(key, arr):
        if key == "small":
            parts = [arr["sgu_norm_g"], arr["sgu_norm_b"], arr["sgu_w"], arr["sgu_bias"], arr["q_norm_g"],
                     arr["k_norm_g"], arr["cd_norm_g"], arr["conv_c_w32"][:CONV_C_TAPS], arr["conv_c_b"], arr["c_ln_g"],
                     arr["c_ln_b"], arr["conv_d_w8"][:CONV_D_TAPS], arr["ffn_norm_g0"], arr["ffn_norm_g1"],
                     arr["loss_cols"]]
            smalls["sizes"] = [p.size for p in parts]
            rows = -(-sum(smalls["sizes"]) // 1024) * 8
            smalls["early"] = small_exchange("small_start", _pack_rows(parts, rows))
            return smalls["early"]["token"]
        if key in rides_with:
            held[rides_with[key]] = (key, arr)
            return None
        group = ([held.pop(key)] if key in held else []) + [(key, arr)]
        srcs, lands, items = [], [], []
        for n, (k, a) in enumerate(group):
            if k.startswith("wt_ffn_in"):
                src = a.reshape(2, NDEV, r_ff, D)
                own = lax.dynamic_slice_in_dim(src, me, 1, axis=1)
                lands.append(lax.dynamic_update_slice(lax.empty(src.shape, BF16), own, (0, me, 0, 0)))
                items += [((lambda S, j, n=n: S[n].at[0, j]), slot2(n, 0)), ((lambda S, j, n=n: S[n].at[1, j]), slot2(n, 1))]
            else:
                rows = a.shape[0] // NDEV
                src = a.reshape(NDEV, rows, D)
                own = lax.dynamic_index_in_dim(src, me, 0, keepdims=False)
                lands.append(_landing((1, NDEV, rows, D), BF16, [((0, me), own)]))
                items.append(((lambda S, j, n=n: S[n].at[j]), slot2(n, 0)))
            srcs.append(src)
        st = _exchange_start(f"scatter_{key}_start", srcs, lands, items)
        scatters[key] = (st, [k for k, _ in group])
        return st["token"]

    W = {
        "dep_first": gathers[0]["token"],
        "ab_norm_g": w["ab_norm_g"], "sgu_norm_g": w["sgu_norm_g"], "sgu_norm_b": w["sgu_norm_b"],
        "sgu_w": w["sgu_w"][0], "sgu_bias": w["sgu_bias"][0], "q_norm_g": w["q_norm_g"][0],
        "k_norm_g": w["k_norm_g"][0], "ffn_norm_g": w["ffn_norm_g"],
    }

    loss_cols, grad_x, G = _local_step(x[0], loss_target[0], W, fetch, on_grad)

    late_small = small_exchange("small_late_start", G["ab_norm_g"])
    landed = {}

    def wait_scatters(name, group_keys, others, after):
        res = _exchange_wait(name, [scatters[gk][0] for gk in group_keys] + others, after)
        for gk, lands in zip(group_keys, res):
            landed.update(zip(scatters[gk][1], lands))
        return [lands[0] for lands in res[len(group_keys):]]

    small_land, = wait_scatters("scatter_wait_early", ["wt_ffn_in1", "wt_cd_in", "wt_ffn_in0", "w_ab_out"],
                                [smalls["early"]], late_small["token"])

    grads, deltas, new_m, new_v = {}, {}, {}, {}
    done = []

    def put(name, res):
        grads[name], deltas[name], new_m[name], new_v[name] = res

    def adam(name, lands, sel, transposed):
        flip = (lambda a: jnp.swapaxes(a, 1, 2)) if transposed else (lambda a: a)
        res = _adam_stacked(lands, sel, flip(w[name]), flip(m[name]), flip(v[name]), f"adam_{name}")
        done.append(res[1])
        put(name, [flip(r) for r in res])

    ffn_in_lands = [landed["wt_ffn_in0"], landed["wt_ffn_in1"]]
    adam("cd_w_in", [landed["wt_cd_in"]], 0, True)
    adam("ffn_w_gate", ffn_in_lands, 0, True)
    adam("ffn_w_up", ffn_in_lands, 1, True)
    adam("cd_w_out", [landed["w_cd_out"]], 0, False)
    adam("ab_w_out", [landed["w_ab_out"]], 0, False)
    adam("ffn_w_down", [landed["w_ffn_down0"], landed["w_ffn_down1"]], 0, False)

    red = _sum_slots(small_land).reshape(-1)
    offs = [0]
    for s in smalls["sizes"]:
        offs.append(offs[-1] + s)
    seg = [red[offs[i]:offs[i + 1]] for i in range(len(smalls["sizes"]))]
    loss = jnp.sum(seg[14])

    def own_channels(full, taps):
        return lax.dynamic_slice_in_dim(full.reshape(taps, 512), me * SHARD_C, SHARD_C, axis=1)

    g_small = {
        "sgu_norm_g": seg[0].reshape(1, 512), "sgu_norm_b": seg[1].reshape(1, 512),
        "sgu_w": seg[2].reshape(512, 128), "sgu_bias": seg[3].reshape(4, 128), "q_norm_g": seg[4].reshape(3, 64),
        "k_norm_g": seg[5].reshape(3, 64),
        "cd_norm_g": lax.dynamic_slice_in_dim(seg[6].reshape(1, D), me * (D // NDEV), D // NDEV, axis=1),
        "conv_c_w": own_channels(seg[7], CONV_C_TAPS), "conv_c_b": own_channels(seg[8], 1),
        "c_ln_g": own_channels(seg[9], 1), "c_ln_b": own_channels(seg[10], 1),
        "conv_d_w": own_channels(seg[11], CONV_D_TAPS),
        "ffn_norm_g": jnp.concatenate([seg[12].reshape(1, D), seg[13].reshape(1, D)], axis=0),
    }

    d_s, m_s, v_s = _adam_small([w[n].reshape(s) for n, s in SMALL_2D], [g_small[n] for n, _ in SMALL_2D],
                                [m[n].reshape(s) for n, s in SMALL_2D], [v[n].reshape(s) for n, s in SMALL_2D])
    for i, (n, _) in enumerate(SMALL_2D):
        shape = w[n].shape
        grads[n], deltas[n] = g_small[n].reshape(shape), d_s[i].reshape(shape)
        new_m[n], new_v[n] = m_s[i].reshape(shape), v_s[i].reshape(shape)
    done.append(d_s[0])

    late_land, = wait_scatters("scatter_wait_last", ["wt_ab_in"], [late_small], list(done))
    put("ab_norm_g", _adam_of_slots(late_land, w["ab_norm_g"], m["ab_norm_g"], v["ab_norm_g"], "adam_ab_norm_g"))
    adam("ab_w_in", [landed["wt_ab_in"]], 0, True)

    return (loss, grad_x[None], *[grads[n] for n in WEIGHT_NAMES], *[deltas[n] for n in WEIGHT_NAMES],
            *[new_m[n] for n in WEIGHT_NAMES], *[new_v[n] for n in WEIGHT_NAMES])
```

```python
import jax
import jax.numpy as jnp
import numpy as np
from jax import lax
from jax.experimental import pallas as pl
from jax.experimental.pallas import tpu as pltpu

F32 = jnp.float32
BF16 = jnp.bfloat16

T = 4096
D = 1024
NDEV = 8
EPS = 1e-6
NEG_INF = -1e30
DFF = 2816
AB_IN = 5632
CD_IN = 2560
HEAD = 64
PAIR = 128
NPAIR = 4
NBACK = 128
DIL_RATES = (1, 4, 16)
ROPE_HALF = 8
ROPE_THETA = 500000.0
CONV_C_TAPS = 31
CONV_D_TAPS = 3
HALO = 32
ATTN_BWD_UNROLL = 4

ADAM_LR = 0.001
ADAM_B1 = 0.9
ADAM_B2 = 0.999
ADAM_EPS = 1e-08
ADAM_WD = 0.01
ADAM_STEP = 10
ADAM_C1 = 1.0 / (1.0 - ADAM_B1 ** ADAM_STEP)
ADAM_C2 = 1.0 / (1.0 - ADAM_B2 ** ADAM_STEP)

VMEM_LIMIT_MB = 48
MESH = pl.DeviceIdType.MESH
HBM_SPEC = pl.BlockSpec(memory_space=pl.ANY)


def _cparams(ngrid, vmem_mb=VMEM_LIMIT_MB):
    return pltpu.CompilerParams(dimension_semantics=("arbitrary",) * ngrid,
                                vmem_limit_bytes=vmem_mb * 1024 * 1024)


def _pick(n, options):
    for o in options:
        if n % o == 0:
            return o
    raise ValueError(f"no tile for {n} in {options}")


def _sds(shape, dtype):
    return jax.ShapeDtypeStruct(shape, dtype)


def _sigmoid(x):
    return 1.0 / (1.0 + jnp.exp(-x))


def _sigmoid_bf16(x):
    return 0.5 * jnp.tanh(0.5 * x) + 0.5


def _gelu(z):
    return 0.5 * z * (1.0 + lax.erf(z * 0.7071067811865476))


def _gelu_grad(z):
    return 0.5 * (1.0 + lax.erf(z * 0.7071067811865476)) + z * jnp.exp(-0.5 * z * z) * 0.3989422804014327


def _mm_nt(a, wt, name, out_dtype=BF16, dep=None):
    M, K = a.shape
    N = wt.shape[0]
    tn = _pick(N, (512, 256))

    def body(a_ref, w_ref, *rest):
        o_ref = rest[-1]
        for r0 in range(0, M, 1024):
            o_ref[r0:r0 + 1024, :] = lax.dot_general(
                a_ref[r0:r0 + 1024, :], w_ref[...], (((1,), (1,)), ((), ())),
                preferred_element_type=F32).astype(o_ref.dtype)

    in_specs = [pl.BlockSpec((M, K), lambda j: (0, 0), pipeline_mode=pl.Buffered(1)),
                pl.BlockSpec((tn, K), lambda j: (j, 0))]
    args = [a, wt]
    if dep is not None:
        in_specs.append(HBM_SPEC)
        args.append(dep)
    return pl.pallas_call(
        body, name=name, grid=(N // tn,), in_specs=in_specs, out_specs=pl.BlockSpec((M, tn), lambda j: (0, j)),
        out_shape=_sds((M, N), out_dtype), compiler_params=_cparams(1))(*args)


EPI_ROWS = 256


def _mm_nn(a, w, name, mode, resid, gain=None, tgt=None, dep=None, tm=512):
    M, K = a.shape
    N = w.shape[1]
    side = gain if mode == "rms" else tgt

    def body(a_ref, w_ref, resid_ref, side_ref, *rest):
        outs, acc = rest[-3 if mode == "rms" else -4:-1], rest[-1]
        i = pl.program_id(0)
        acc[...] = jnp.dot(a_ref[...], w_ref[...], preferred_element_type=F32)

        if mode == "loss":
            @pl.when(i == 0)
            def _():
                outs[2][...] = jnp.zeros_like(outs[2])

        for r0 in range(0, tm, EPI_ROWS):
            rows = slice(r0, r0 + EPI_ROWS)
            v = acc[rows, :] + resid_ref[rows, :]
            if mode == "rms":
                outs[0][rows, :] = v
                r = lax.rsqrt(jnp.mean(v * v, axis=-1, keepdims=True) + EPS)
                outs[1][rows, :] = (v * r * side_ref[...]).astype(BF16)
            else:
                d = v - side_ref[rows, :]
                outs[2][...] += jnp.sum(d * d, axis=0, keepdims=True) * (0.5 / N)
                dy = d * (1.0 / N)
                outs[0][rows, :] = dy
                outs[1][rows, :] = dy.astype(BF16)

    row = pl.BlockSpec((tm, N), lambda i: (i, 0))
    vec = pl.BlockSpec((1, N), lambda i: (0, 0))
    in_specs = [pl.BlockSpec((tm, K), lambda i: (i, 0)),
                pl.BlockSpec((K, N), lambda i: (0, 0), pipeline_mode=pl.Buffered(1)), row,
                vec if mode == "rms" else row]
    args = [a, w, resid, side]
    if dep is not None:
        in_specs.append(HBM_SPEC)
        args.append(dep)
    if mode == "rms":
        out_specs, out_shape = [row, row], [_sds((M, N), F32), _sds((M, N), BF16)]
    else:
        out_specs, out_shape = [row, row, vec], [_sds((M, N), F32), _sds((M, N), BF16), _sds((1, N), F32)]
    return pl.pallas_call(
        body, name=name, grid=(M // tm,), in_specs=in_specs, out_specs=out_specs, out_shape=out_shape,
        scratch_shapes=[pltpu.VMEM((tm, N), F32)], compiler_params=_cparams(1))(*args)


def _mm_dh_rms_bwd(a, w, x, gain, dres, name, dep=None, tm=512, bf16_copy=True):
    parts = a.shape[0] if a.ndim == 3 else 1
    M, Kp = a.shape[-2], a.shape[-1]
    N = w.shape[1]
    nblk = M // tm
    assert nblk % 2 == 0

    def body(a_ref, w_ref, x_ref, g_ref, dres_ref, *rest):
        dg_ref, acc0, acc1 = rest[-3:]
        dx_ref = rest[-5] if bf16_copy else rest[-4]
        dxb_ref = rest[-4] if bf16_copy else None
        i = pl.program_id(0)

        def matmul(acc):
            if parts == 1:
                acc[...] = jnp.dot(a_ref[...], w_ref[...], preferred_element_type=F32)
            else:
                d = jnp.dot(a_ref[0], w_ref[0:Kp, :], preferred_element_type=F32)
                for p in range(1, parts):
                    d = d + jnp.dot(a_ref[p], w_ref[p * Kp:(p + 1) * Kp, :], preferred_element_type=F32)
                acc[...] = d

        def finish(acc):
            for r0 in range(0, tm, EPI_ROWS // 2):
                rows = slice(r0, r0 + EPI_ROWS // 2)
                v = acc[rows, :]
                xf = x_ref[rows, :]
                r = lax.rsqrt(jnp.mean(xf * xf, axis=-1, keepdims=True) + EPS)
                xhat = xf * r
                dg_ref[...] += jnp.sum(v * xhat, axis=0, keepdims=True)
                dxh = v * g_ref[...]
                tot = dres_ref[rows, :] + r * (dxh - xhat * jnp.mean(dxh * xhat, axis=-1, keepdims=True))
                dx_ref[rows, :] = tot
                if bf16_copy:
                    dxb_ref[rows, :] = tot.astype(BF16)

        @pl.when(i == 0)
        def _():
            dg_ref[...] = jnp.zeros_like(dg_ref)
            matmul(acc0)

        @pl.when((i > 0) & (i < nblk) & (i % 2 == 1))
        def _():
            matmul(acc1)
            finish(acc0)

        @pl.when((i > 0) & (i < nblk) & (i % 2 == 0))
        def _():
            matmul(acc0)
            finish(acc1)

        @pl.when(i == nblk)
        def _():
            finish(acc1)

    last = nblk - 1
    row = pl.BlockSpec((tm, N), lambda i: (jnp.maximum(i - 1, 0), 0))
    vec = pl.BlockSpec((1, N), lambda i: (0, 0))
    if a.ndim == 3:
        a_spec = pl.BlockSpec((parts, tm, Kp), lambda i: (0, jnp.minimum(i, last), 0))
    else:
        a_spec = pl.BlockSpec((tm, Kp), lambda i: (jnp.minimum(i, last), 0))
    w_spec = pl.BlockSpec((parts * Kp, N), lambda i: (0, 0), pipeline_mode=pl.Buffered(1))
    in_specs = [a_spec, w_spec, row, vec, row]
    args = [a, w, x, gain, dres]
    if dep is not None:
        in_specs.append(HBM_SPEC)
        args.append(dep)
    return pl.pallas_call(
        body, name=name, grid=(nblk + 1,), in_specs=in_specs,
        out_specs=[row, row, vec] if bf16_copy else [row, vec],
        out_shape=([_sds((M, N), F32), _sds((M, N), BF16), _sds((1, N), F32)] if bf16_copy
                   else [_sds((M, N), F32), _sds((1, N), F32)]),
        scratch_shapes=[pltpu.VMEM((tm, N), F32), pltpu.VMEM((tm, N), F32)], compiler_params=_cparams(1, 56))(*args)


def _mm_tn(a, b, name, out_dtype=BF16, tt=2048, dep=None):
    parts = a.shape[0] if a.ndim == 3 else 1
    Tt, Mp = a.shape[-2], a.shape[-1]
    N = b.shape[1]
    tn = _pick(Mp, (1408, 1280, 1024, 512))
    jper = Mp // tn
    nt = Tt // tt

    def body(a_ref, b_ref, *rest):
        o_ref, acc = rest[-2:]
        t = pl.program_id(1)

        @pl.when(t == 0)
        def _():
            acc[...] = jnp.zeros_like(acc)

        rows = pl.ds(pl.multiple_of(t * tt, tt), tt)
        acc[...] += lax.dot_general(a_ref[...], b_ref[rows, :], (((0,), (0,)), ((), ())),
                                    preferred_element_type=F32)

        @pl.when(t == nt - 1)
        def _():
            o_ref[...] = acc[...].astype(o_ref.dtype)

    if a.ndim == 3:
        a_spec = pl.BlockSpec((None, tt, tn), lambda j, t: (j // jper, t, j % jper))
    else:
        a_spec = pl.BlockSpec((tt, tn), lambda j, t: (t, j))
    in_specs = [a_spec, pl.BlockSpec((Tt, N), lambda j, t: (0, 0), pipeline_mode=pl.Buffered(1))]
    args = [a, b]
    if dep is not None:
        in_specs.append(HBM_SPEC)
        args.append(dep)
    return pl.pallas_call(
        body, name=name, grid=(parts * jper, nt), in_specs=in_specs,
        out_specs=pl.BlockSpec((tn, N), lambda j, t: (j, 0)),
        out_shape=_sds((parts * Mp, N), out_dtype), scratch_shapes=[pltpu.VMEM((tn, N), F32)],
        compiler_params=_cparams(2))(*args)


FFN_ROWS = 2048


def _ffn_in(h, wt_in, name, tn=256):
    nj = DFF // tn

    def body(h_ref, wg_ref, wu_ref, p_ref, act_ref):
        nt = (((1,), (1,)), ((), ()))
        for r0 in range(0, T, FFN_ROWS):
            rows = slice(r0, r0 + FFN_ROWS)
            g = lax.dot_general(h_ref[rows, :], wg_ref[...], nt, preferred_element_type=F32).astype(BF16)
            u = lax.dot_general(h_ref[rows, :], wu_ref[...], nt, preferred_element_type=F32).astype(BF16)
            p_ref[0, rows, :] = g
            p_ref[1, rows, :] = u
            act_ref[rows, :] = g * _sigmoid_bf16(g) * u

    return pl.pallas_call(
        body, name=name, grid=(nj,),
        in_specs=[pl.BlockSpec((T, D), lambda j: (0, 0), pipeline_mode=pl.Buffered(1)),
                  pl.BlockSpec((tn, D), lambda j: (j, 0)), pl.BlockSpec((tn, D), lambda j: (j + nj, 0))],
        out_specs=[pl.BlockSpec((2, T, tn), lambda j: (0, 0, j)), pl.BlockSpec((T, tn), lambda j: (0, j))],
        out_shape=[_sds((2, T, DFF), BF16), _sds((T, DFF), BF16)], compiler_params=_cparams(1))(h, wt_in, wt_in)


def _ffn_dact(dyb, w_down, p3, name, tn=256, dep=None):
    def body(dy_ref, w_ref, p_ref, *rest):
        o_ref = rest[-1]
        for r0 in range(0, T, FFN_ROWS):
            rows = slice(r0, r0 + FFN_ROWS)
            da = lax.dot_general(dy_ref[rows, :], w_ref[...], (((1,), (1,)), ((), ())),
                                 preferred_element_type=F32).astype(BF16)
            g = p_ref[0, rows, :]
            u = p_ref[1, rows, :]
            sg = _sigmoid_bf16(g)
            gs = g * sg
            o_ref[0, rows, :] = (da * u) * (sg + gs * (1.0 - sg))
            o_ref[1, rows, :] = da * gs

    pspec = pl.BlockSpec((2, T, tn), lambda j: (0, 0, j))
    in_specs = [pl.BlockSpec((T, D), lambda j: (0, 0), pipeline_mode=pl.Buffered(1)),
                pl.BlockSpec((tn, D), lambda j: (j, 0)), pspec]
    args = [dyb, w_down, p3]
    if dep is not None:
        in_specs.append(HBM_SPEC)
        args.append(dep)
    return pl.pallas_call(
        body, name=name, grid=(DFF // tn,), in_specs=in_specs, out_specs=pspec,
        out_shape=_sds((2, T, DFF), BF16), compiler_params=_cparams(1))(*args)


def _rms_fwd(x, g, name, tm=512, dep=None):
    def body(x_ref, g_ref, *rest):
        h_ref = rest[-1]
        xf = x_ref[...]
        r = lax.rsqrt(jnp.mean(xf * xf, axis=-1, keepdims=True) + EPS)
        h_ref[...] = (xf * r * g_ref[...]).astype(BF16)

    in_specs = [pl.BlockSpec((tm, D), lambda i: (i, 0)), pl.BlockSpec((1, D), lambda i: (0, 0))]
    args = [x, g]
    if dep is not None:
        in_specs.append(HBM_SPEC)
        args.append(dep)
    return pl.pallas_call(
        body, name=name, grid=(T // tm,), in_specs=in_specs, out_specs=pl.BlockSpec((tm, D), lambda i: (i, 0)),
        out_shape=_sds((T, D), BF16), compiler_params=_cparams(1))(*args)


def _tril_mask():
    r = lax.broadcasted_iota(jnp.int32, (128, 128), 0)
    c = lax.broadcasted_iota(jnp.int32, (128, 128), 1)
    return r >= c


def _mix_a_fwd(pab, sgu_g, sgu_b, sgu_w, sgu_bias3, tm=512):
    def body(zu_ref, zv_ref, g_ref, b_ref, w_ref, bias_ref, o_ref):
        u = _gelu(zu_ref[...].astype(F32))
        v = _gelu(zv_ref[...].astype(F32))
        mu = jnp.mean(v, axis=-1, keepdims=True)
        vc = v - mu
        rstd = lax.rsqrt(jnp.mean(vc * vc, axis=-1, keepdims=True) + EPS)
        vn = (vc * rstd * g_ref[...] + b_ref[...]).astype(BF16)
        tri = _tril_mask()
        for gi in range(4):
            wg = jnp.where(tri, w_ref[gi], 0.0).astype(BF16)
            bg = bias_ref[gi]
            for c in range(tm // 128):
                rs, cs = slice(c * 128, (c + 1) * 128), slice(gi * 128, (gi + 1) * 128)
                mixed = jnp.dot(wg, vn[rs, cs], preferred_element_type=F32) + bg
                o_ref[rs, cs] = (u[rs, cs] * mixed).astype(BF16)

    half = pl.BlockSpec((tm, 512), lambda i: (i, 0))
    return pl.pallas_call(
        body, name="mix_a_fwd", grid=(T // tm,),
        in_specs=[half, pl.BlockSpec((tm, 512), lambda i: (i, 1)),
                  pl.BlockSpec((1, 512), lambda i: (0, 0)), pl.BlockSpec((1, 512), lambda i: (0, 0)),
                  pl.BlockSpec((4, 128, 128), lambda i: (0, 0, 0)), pl.BlockSpec((4, 128, 1), lambda i: (0, 0, 0))],
        out_specs=half, out_shape=_sds((T, D), BF16), compiler_params=_cparams(1),
    )(pab, pab, sgu_g, sgu_b, sgu_w, sgu_bias3)


def _rope_tables():
    pos = np.arange(T, dtype=np.float32)
    inv_freq = np.float32(ROPE_THETA) ** (-np.arange(ROPE_HALF, dtype=np.float32) * np.float32(2.0 / (2 * ROPE_HALF)))
    ang = (pos[:, None] * inv_freq[None, :]).astype(np.float32)
    cos, sin = np.cos(ang), np.sin(ang)
    z8 = np.zeros((T, ROPE_HALF), np.float32)
    rest = np.zeros((T, HEAD - 2 * ROPE_HALF), np.float32)
    c64 = np.concatenate([cos, cos, rest + 1.0], axis=1)
    s1 = np.concatenate([z8, sin, rest], axis=1)
    s2 = np.concatenate([-sin, z8, rest], axis=1)
    return tuple(jnp.asarray(np.tile(t, (1, 2)).astype(np.float32)) for t in (c64, s1, s2))


def _lo_mask(shape):
    return lax.broadcasted_iota(jnp.int32, shape, 1) < HEAD


def _seg_mean(x, lo):
    s_all = jnp.sum(x, axis=-1, keepdims=True)
    s_lo = jnp.sum(jnp.where(lo, x, 0.0), axis=-1, keepdims=True)
    return jnp.where(lo, s_lo, s_all - s_lo) * (1.0 / HEAD)


def _head_blocks():
    r = lax.broadcasted_iota(jnp.int32, (PAIR, PAIR), 0) < HEAD
    c = lax.broadcasted_iota(jnp.int32, (PAIR, PAIR), 1) < HEAD
    return jnp.where(r == c, 1.0, 0.0).astype(BF16)


def _seg_mean_mxu(x, blocks):
    return jnp.dot(x.astype(BF16), blocks, preferred_element_type=F32) * (1.0 / HEAD)


def _rope(n, c, s1, s2):
    return n * c + pltpu.roll(n, ROPE_HALF, 1) * s1 + pltpu.roll(n, PAIR - ROPE_HALF, 1) * s2


def _rope_t(dy, c, s1, s2):
    return dy * c - pltpu.roll(dy, PAIR - ROPE_HALF, 1) * s2 - pltpu.roll(dy, ROPE_HALF, 1) * s1


def _prep_fwd(pab, qg, kg, tabs, tm=512):
    def body(p_ref, qg_ref, kg_ref, c_ref, s1_ref, s2_ref, *outs):
        blocks = _head_blocks()
        c, s1, s2 = c_ref[...], s1_ref[...], s2_ref[...]
        for g in range(3):
            qn_ref, kn_ref, v_ref = outs[3 * g:3 * g + 3]
            for p in range(NPAIR):
                for which, gains, dst in ((0, qg_ref, qn_ref), (1, kg_ref, kn_ref)):
                    col = (2 + 3 * which + g) * 512 + p * PAIR
                    xr = p_ref[:, col:col + PAIR].astype(F32)
                    rinv = lax.rsqrt(_seg_mean_mxu(xr * xr, blocks) + EPS)
                    outs[9 + 2 * g + which][p] = rinv.astype(BF16)
                    dst[p] = _rope(xr * rinv * gains[g:g + 1, :], c, s1, s2)
                col = (8 + g) * 512 + p * PAIR
                v_ref[p] = p_ref[:, col:col + PAIR].astype(F32)

    pm = pl.BlockSpec((NPAIR, tm, PAIR), lambda i: (0, i, 0))
    tab = pl.BlockSpec((tm, PAIR), lambda i: (i, 0))
    gain = pl.BlockSpec((3, PAIR), lambda i: (0, 0))
    res = pl.pallas_call(
        body, name="prep_fwd", grid=(T // tm,),
        in_specs=[pl.BlockSpec((tm, AB_IN), lambda i: (i, 0)), gain, gain, tab, tab, tab],
        out_specs=[pm] * 15, out_shape=[_sds((NPAIR, T, PAIR), F32)] * 9 + [_sds((NPAIR, T, PAIR), BF16)] * 6,
        compiler_params=_cparams(1))(pab, qg, kg, *tabs)
    return res[0:9], res[9:15]


def _res_index(it, rate):
    window = NBACK * rate
    b = it // rate
    rho = it % rate
    start = b * window + rho
    startp = jnp.maximum(start - window, rho)
    kmin = jnp.where(b > 0, 0, NBACK)
    return start, startp, kmin


def _rows(start, rate):
    if rate == 1:
        return pl.ds(pl.multiple_of(start, NBACK), NBACK)
    return pl.ds(start, NBACK, stride=rate)


def _band_bias():
    qs = lax.broadcasted_iota(jnp.int32, (2 * NBACK, 2 * NBACK), 0)
    kj = lax.broadcasted_iota(jnp.int32, (2 * NBACK, 2 * NBACK), 1)
    dist = (qs & (NBACK - 1)) + NBACK - kj
    both = (dist >= 0) & (dist <= NBACK)
    return jnp.where(both, 0.0, NEG_INF), jnp.where(both & (kj >= NBACK), 0.0, NEG_INF)


def _attn_fwd(qn, kn, v, rate, name, dep=None):
    def body(q_ref, k_ref, v_ref, *rest):
        o_ref, l_ref = rest[-2:]
        lo = _lo_mask((NBACK, PAIR))
        bias_all, bias_first = _band_bias()

        def step(it, carry):
            start, startp, kmin = _res_index(it, rate)
            q = q_ref[_rows(start, rate), :] * (HEAD ** -0.5)
            kcat = jnp.concatenate([k_ref[_rows(startp, rate), :], k_ref[_rows(start, rate), :]], axis=0).astype(BF16)
            vcat = jnp.concatenate([v_ref[_rows(startp, rate), :], v_ref[_rows(start, rate), :]], axis=0).astype(BF16)
            vcat1 = jnp.concatenate([vcat, jnp.ones((2 * NBACK, PAIR), BF16)], axis=1)
            q2 = jnp.concatenate([jnp.where(lo, q, 0.0), jnp.where(lo, 0.0, q)], axis=0).astype(BF16)
            s = lax.dot_general(q2, kcat, (((1,), (1,)), ((), ())), preferred_element_type=F32)
            s = s + jnp.where(kmin == 0, bias_all, bias_first)
            m = jnp.max(s, axis=-1, keepdims=True)
            ol = jnp.dot(jnp.exp(s - m).astype(BF16), vcat1, preferred_element_type=F32)
            o2 = ol[:, 0:PAIR] / ol[:, PAIR:]
            ls = m + jnp.log(ol[:, PAIR:])
            o_ref[_rows(start, rate), :] = jnp.where(lo, o2[0:NBACK], o2[NBACK:])
            l_ref[_rows(start, rate), :] = jnp.where(lo, ls[0:NBACK], ls[NBACK:])
            return carry

        lax.fori_loop(0, T // NBACK, step, 0, unroll=4)

    pm = pl.BlockSpec((None, T, PAIR), lambda p: (p, 0, 0))
    in_specs, args = [pm, pm, pm], [qn, kn, v]
    if dep is not None:
        in_specs.append(HBM_SPEC)
        args.append(dep)
    return pl.pallas_call(
        body, name=name, grid=(NPAIR,), in_specs=in_specs, out_specs=[pm, pm],
        out_shape=[_sds((NPAIR, T, PAIR), F32)] * 2, compiler_params=_cparams(1))(*args)


def _merge_fwd(cat_ab, outs, lses, tm=512):
    def body(cat_in, o0, o1, o2, l0, l1, l2, cat_ref, lse_ref):
        del cat_in
        for p in range(NPAIR):
            a0, a1, a2 = l0[p], l1[p], l2[p]
            m = jnp.maximum(jnp.maximum(a0, a1), a2)
            w0, w1, w2 = jnp.exp(a0 - m), jnp.exp(a1 - m), jnp.exp(a2 - m)
            s = w0 + w1 + w2
            b = (w0 * o0[p] + w1 * o1[p] + w2 * o2[p]) / s
            cat_ref[:, p * PAIR:(p + 1) * PAIR] = b.astype(BF16)
            lse_ref[p] = m + jnp.log(s)

    pm = pl.BlockSpec((NPAIR, tm, PAIR), lambda i: (0, i, 0))
    return pl.pallas_call(
        body, name="merge_fwd", grid=(T // tm,),
        in_specs=[pl.BlockSpec(memory_space=pl.ANY)] + [pm] * 6,
        out_specs=[pl.BlockSpec((tm, 512), lambda i: (i, 1)), pm],
        out_shape=[_sds((T, D), BF16), _sds((NPAIR, T, PAIR), F32)],
        input_output_aliases={0: 0}, compiler_params=_cparams(1))(cat_ab, *outs, *lses)


def _b_pre_bwd(dcat, cat, tm=512):
    def body(db_ref, b_ref, dbp_ref, e_ref):
        lo = _lo_mask((tm, PAIR))
        for p in range(NPAIR):
            db = db_ref[:, p * PAIR:(p + 1) * PAIR].astype(F32)
            b = b_ref[:, p * PAIR:(p + 1) * PAIR].astype(F32)
            dbp_ref[p] = db
            e_ref[p] = _seg_mean(db * b, lo) * float(HEAD)

    pm = pl.BlockSpec((NPAIR, tm, PAIR), lambda i: (0, i, 0))
    right = pl.BlockSpec((tm, 512), lambda i: (i, 1))
    return pl.pallas_call(
        body, name="b_pre_bwd", grid=(T // tm,), in_specs=[right, right], out_specs=[pm, pm],
        out_shape=[_sds((NPAIR, T, PAIR), F32)] * 2, compiler_params=_cparams(1))(dcat, cat)


def _attn_bwd(qn, kn, v, dbp, e, lse, rate, name):
    def body(q_ref, k_ref, v_ref, db_ref, e_ref, lse_ref, dq_ref, dk_ref, dv_ref):
        lo = _lo_mask((NBACK, PAIR))
        bias_all, bias_first = _band_bias()
        scale = HEAD ** -0.5
        nt = (((1,), (1,)), ((), ()))
        tn = (((0,), (0,)), ((), ()))
        window = NBACK * rate
        nblk = T // window

        def one(it, carry):
            dk_carry, dv_carry = carry
            rho = it // nblk
            b = it % nblk
            start = b * window + rho
            rq = _rows(start, rate)
            rp = _rows(jnp.maximum(start - window, rho), rate)
            q = q_ref[rq, :] * scale
            db = db_ref[rq, :]
            ev = e_ref[rq, :]
            ls = lse_ref[rq, :]
            kcat = jnp.concatenate([k_ref[rp, :], k_ref[rq, :]], axis=0).astype(BF16)
            vcat = jnp.concatenate([v_ref[rp, :], v_ref[rq, :]], axis=0).astype(BF16)
            q2 = jnp.concatenate([jnp.where(lo, q, 0.0), jnp.where(lo, 0.0, q)], axis=0).astype(BF16)
            db2 = jnp.concatenate([jnp.where(lo, db, 0.0), jnp.where(lo, 0.0, db)], axis=0).astype(BF16)
            ls2 = jnp.concatenate([ls[:, 0:1], ls[:, HEAD:HEAD + 1]], axis=0)
            ev2 = jnp.concatenate([ev[:, 0:1], ev[:, HEAD:HEAD + 1]], axis=0)
            s = lax.dot_general(q2, kcat, nt, preferred_element_type=F32)
            pt = jnp.exp(s + jnp.where(b > 0, bias_all, bias_first) - ls2)
            dp = lax.dot_general(db2, vcat, nt, preferred_element_type=F32)
            ds = (pt * (dp - ev2)).astype(BF16)
            dq2 = jnp.dot(ds, kcat, preferred_element_type=F32) * scale
            dkc = lax.dot_general(ds, q2, tn, preferred_element_type=F32)
            dvc = lax.dot_general(pt.astype(BF16), db2, tn, preferred_element_type=F32)
            dq_ref[rq, :] = jnp.where(lo, dq2[0:NBACK], dq2[NBACK:])
            dk_ref[rp, :] = dk_carry + dkc[0:NBACK]
            dk_ref[rq, :] = dkc[NBACK:]
            dv_ref[rp, :] = dv_carry + dvc[0:NBACK]
            dv_ref[rq, :] = dvc[NBACK:]
            return dkc[NBACK:], dvc[NBACK:]

        def step(i, carry):
            for u in range(ATTN_BWD_UNROLL):
                carry = one(i * ATTN_BWD_UNROLL + u, carry)
            return carry

        zero = jnp.zeros((NBACK, PAIR), F32)
        lax.fori_loop(0, T // NBACK // ATTN_BWD_UNROLL, step, (zero, zero))

    pm = pl.BlockSpec((None, T, PAIR), lambda p: (p, 0, 0))
    return pl.pallas_call(
        body, name=name, grid=(NPAIR,), in_specs=[pm] * 6, out_specs=[pm] * 3,
        out_shape=[_sds((NPAIR, T, PAIR), F32)] * 3, compiler_params=_cparams(1, 56))(qn, kn, v, dbp, e, lse)


def _ab_in_bwd(pab, dcat, sgu_g, sgu_b, sgu_w, sgu_bias3, qg, kg, tabs, dqkv, rinvs, tm=256):
    def body(p_ref, dcat_ref, g_ref, b_ref, w_ref, bias_ref, qg_ref, kg_ref, c_ref, s1_ref, s2_ref, *rest):
        dq_refs, rinv_refs = rest[0:9], rest[9:15]
        o_ref, dwm_ref, dbias_ref, dsg_ref, dsb_ref, dgain_ref = rest[15:]
        i = pl.program_id(0)

        @pl.when(i == 0)
        def _():
            dwm_ref[...] = jnp.zeros_like(dwm_ref)
            dbias_ref[...] = jnp.zeros_like(dbias_ref)
            dsg_ref[...] = jnp.zeros_like(dsg_ref)
            dsb_ref[...] = jnp.zeros_like(dsb_ref)
            dgain_ref[...] = jnp.zeros_like(dgain_ref)

        zu = p_ref[:, 0:512].astype(F32)
        zv = p_ref[:, 512:1024].astype(F32)
        u = _gelu(zu)
        v = _gelu(zv)
        mu = jnp.mean(v, axis=-1, keepdims=True)
        vc = v - mu
        rstd = lax.rsqrt(jnp.mean(vc * vc, axis=-1, keepdims=True) + EPS)
        xhat = vc * rstd
        vn = (xhat * g_ref[...] + b_ref[...]).astype(BF16)
        da = dcat_ref[...].astype(F32)
        tri = _tril_mask()
        du_parts = [[None] * 4 for _ in range(tm // 128)]
        dvn_parts = [[None] * 4 for _ in range(tm // 128)]
        for gi in range(4):
            wg = jnp.where(tri, w_ref[gi], 0.0).astype(BF16)
            bg = bias_ref[gi]
            for c in range(tm // 128):
                rs, cs = slice(c * 128, (c + 1) * 128), slice(gi * 128, (gi + 1) * 128)
                vblk = vn[rs, cs]
                mixed = jnp.dot(wg, vblk, preferred_element_type=F32) + bg
                dab = da[rs, cs]
                du_parts[c][gi] = dab * mixed
                dmixed = dab * u[rs, cs]
                dmb = dmixed.astype(BF16)
                dvn_parts[c][gi] = lax.dot_general(wg, dmb, (((0,), (0,)), ((), ())), preferred_element_type=F32)
                dwm = lax.dot_general(dmb, vblk, (((1,), (1,)), ((), ())), preferred_element_type=F32)
                dwm_ref[gi] += jnp.where(tri, dwm, 0.0)
                dbias_ref[gi] += dmixed
        du = jnp.concatenate([jnp.concatenate(r, axis=1) for r in du_parts], axis=0)
        dvn = jnp.concatenate([jnp.concatenate(r, axis=1) for r in dvn_parts], axis=0)
        dsg_ref[...] += jnp.sum(dvn * xhat, axis=0, keepdims=True)
        dsb_ref[...] += jnp.sum(dvn, axis=0, keepdims=True)
        dxh = dvn * g_ref[...]
        dv = rstd * (dxh - jnp.mean(dxh, axis=-1, keepdims=True)
                     - xhat * jnp.mean(dxh * xhat, axis=-1, keepdims=True))
        o_ref[:, 0:512] = (du * _gelu_grad(zu)).astype(BF16)
        o_ref[:, 512:1024] = (dv * _gelu_grad(zv)).astype(BF16)

        blocks = _head_blocks()
        c, s1, s2 = c_ref[...], s1_ref[...], s2_ref[...]
        for g in range(3):
            dq_ref, dk_ref, dv_ref = dq_refs[3 * g:3 * g + 3]
            for p in range(NPAIR):
                for which, gains, src in ((0, qg_ref, dq_ref), (1, kg_ref, dk_ref)):
                    col = (2 + 3 * which + g) * 512 + p * PAIR
                    xr = p_ref[:, col:col + PAIR].astype(F32)
                    rinv = rinv_refs[2 * g + which][p].astype(F32)
                    xh = xr * rinv
                    dn = _rope_t(src[p], c, s1, s2)
                    row = 2 * g + which
                    dgain_ref[row:row + 1, :] += jnp.sum(dn * xh, axis=0, keepdims=True)
                    dxh2 = dn * gains[g:g + 1, :]
                    dx = rinv * (dxh2 - xh * _seg_mean_mxu(dxh2 * xh, blocks))
                    o_ref[:, col:col + PAIR] = dx.astype(BF16)
                col = (8 + g) * 512 + p * PAIR
                o_ref[:, col:col + PAIR] = dv_ref[p].astype(BF16)

    pm = pl.BlockSpec((NPAIR, tm, PAIR), lambda i: (0, i, 0))
    tab = pl.BlockSpec((tm, PAIR), lambda i: (i, 0))
    gain = pl.BlockSpec((3, PAIR), lambda i: (0, 0))
    vec = pl.BlockSpec((1, 512), lambda i: (0, 0))
    full = pl.BlockSpec((tm, AB_IN), lambda i: (i, 0))
    w4 = pl.BlockSpec((4, 128, 128), lambda i: (0, 0, 0))
    return pl.pallas_call(
        body, name="ab_in_bwd", grid=(T // tm,),
        in_specs=[full, pl.BlockSpec((tm, 512), lambda i: (i, 0)), vec, vec, w4,
                  pl.BlockSpec((4, 128, 1), lambda i: (0, 0, 0)), gain, gain, tab, tab, tab] + [pm] * 15,
        out_specs=[full, w4, w4, vec, vec, pl.BlockSpec((8, PAIR), lambda i: (0, 0))],
        out_shape=[_sds((T, AB_IN), BF16), _sds((4, 128, 128), F32), _sds((4, 128, 128), F32),
                   _sds((1, 512), F32), _sds((1, 512), F32), _sds((8, PAIR), F32)],
        compiler_params=_cparams(1))(pab, dcat, sgu_g, sgu_b, sgu_w, sgu_bias3, qg, kg, *tabs, *dqkv, *rinvs)


def _ln_stats(x):
    mu = jnp.mean(x, axis=-1, keepdims=True)
    xc = x - mu
    rstd = lax.rsqrt(jnp.mean(xc * xc, axis=-1, keepdims=True) + EPS)
    return xc * rstd, rstd


CONV_RC = 64


def _shifted_copies(src, dst, tm):
    dst[0] = src[...]
    for b in range(1, 8):
        dst[b, 0:tm + HALO - 8, :] = src[pl.ds(b, tm + HALO - 8), :]


def _offsets_by_phase(first):
    groups = {}
    for o in range(first, first + CONV_C_TAPS):
        groups.setdefault(o % 8, []).append(o)
    return sorted(groups.items())


def _window(shifted, b8, base, offsets, lanes):
    rows = 8 * (max(offsets) // 8) + CONV_RC
    return shifted[b8, pl.ds(base, rows), lanes].reshape(rows // 8, 8, 128)


def _cd_fwd(pcd, cw, cb, lg, lb, dw, tm=512):
    per = tm // HALO

    def body(p_ref, h_ref, cw_ref, cb_ref, lg_ref, lb_ref, dw_ref, cat_ref, c0_ref, c1_ref, dd_ref, y_ref,
             buf, buf2, sb):
        i = pl.program_id(0)
        live = jnp.where(i > 0, 1.0, 0.0)
        a = p_ref[:, 0:512].astype(F32)
        gt = p_ref[:, 512:1024].astype(F32)
        gb = p_ref[:, 1024:1536].astype(F32)
        gc = p_ref[:, 1536:2048].astype(F32)
        hv = p_ref[:, 2048:2560].astype(F32)
        c0 = a * _sigmoid(gt)
        dd = gc * hv
        buf[0:HALO, :] = h_ref[:, 0:512].astype(F32) * _sigmoid(h_ref[:, 512:1024].astype(F32)) * live
        buf[HALO:, :] = c0
        buf2[0:HALO, :] = h_ref[:, 1536:2048].astype(F32) * h_ref[:, 2048:2560].astype(F32) * live
        buf2[HALO:, :] = dd
        c0_ref[...] = c0.astype(BF16)
        dd_ref[...] = dd.astype(BF16)
        _shifted_copies(buf, sb, tm)

        def conv_rows(r, carry):
            base = pl.multiple_of(r * CONV_RC, CONV_RC)
            for c in range(4):
                lanes = slice(c * 128, (c + 1) * 128)
                acc = jnp.broadcast_to(cb_ref[:, lanes], (CONV_RC // 8, 8, 128))
                for b8, offsets in _offsets_by_phase(HALO - (CONV_C_TAPS - 1)):
                    win = _window(sb, b8, base, offsets, lanes)
                    for o in offsets:
                        j = o - (HALO - (CONV_C_TAPS - 1))
                        acc = acc + cw_ref[8 * j:8 * j + 8, lanes] * win[o // 8:o // 8 + CONV_RC // 8]
                c1_ref[pl.ds(base, CONV_RC), lanes] = acc.reshape(CONV_RC, 128)
            return carry

        lax.fori_loop(0, tm // CONV_RC, conv_rows, 0)
        xhat, _ = _ln_stats(c1_ref[...])
        c2 = xhat * lg_ref[...] + lb_ref[...]
        y = jnp.zeros((tm, 512), F32)
        for j in range(CONV_D_TAPS):
            y = y + dw_ref[j:j + 1, :] * buf2[pl.ds(HALO - (CONV_D_TAPS - 1) + j, tm), :]
        cat_ref[:, 0:512] = (c2 * _sigmoid(c2)).astype(BF16)
        cat_ref[:, 512:1024] = (gb * y).astype(BF16)
        y_ref[...] = y.astype(BF16)

    half = pl.BlockSpec((tm, 512), lambda i: (i, 0))
    vec = pl.BlockSpec((1, 512), lambda i: (0, 0))
    return pl.pallas_call(
        body, name="cd_fwd", grid=(T // tm,),
        in_specs=[pl.BlockSpec((tm, CD_IN), lambda i: (i, 0)),
                  pl.BlockSpec((HALO, CD_IN), lambda i: (jnp.maximum(i * per - 1, 0), 0)),
                  pl.BlockSpec((8 * 32, 512), lambda i: (0, 0)), vec, vec, vec, pl.BlockSpec((8, 512), lambda i: (0, 0))],
        out_specs=[pl.BlockSpec((tm, D), lambda i: (i, 0)), half, half, half, half],
        out_shape=[_sds((T, D), BF16), _sds((T, 512), BF16), _sds((T, 512), F32), _sds((T, 512), BF16),
                   _sds((T, 512), BF16)],
        scratch_shapes=[pltpu.VMEM((HALO + tm, 512), F32), pltpu.VMEM((HALO + tm, 512), F32),
                        pltpu.VMEM((8, HALO + tm, 512), F32)],
        compiler_params=_cparams(1))(pcd, pcd, cw, cb, lg, lb, dw)


def _cd_bwd_pw(dcat, c1, pcd, y, lg, lb, tm=512):
    def body(dcat_ref, c1_ref, gb_ref, y_ref, lg_ref, lb_ref, dc1_ref, dy3_ref, dgb_ref, dlg_ref, dlb_ref, dcb_ref):
        i = pl.program_id(0)

        @pl.when(i == 0)
        def _():
            dlg_ref[...] = jnp.zeros_like(dlg_ref)
            dlb_ref[...] = jnp.zeros_like(dlb_ref)
            dcb_ref[...] = jnp.zeros_like(dcb_ref)

        dc = dcat_ref[:, 0:512].astype(F32)
        ddo = dcat_ref[:, 512:1024].astype(F32)
        xhat, rstd = _ln_stats(c1_ref[...])
        c2 = xhat * lg_ref[...] + lb_ref[...]
        sg = _sigmoid(c2)
        dc2 = dc * sg * (1.0 + c2 * (1.0 - sg))
        dlg_ref[...] += jnp.sum(dc2 * xhat, axis=0, keepdims=True)
        dlb_ref[...] += jnp.sum(dc2, axis=0, keepdims=True)
        dxh = dc2 * lg_ref[...]
        dc1 = rstd * (dxh - jnp.mean(dxh, axis=-1, keepdims=True)
                      - xhat * jnp.mean(dxh * xhat, axis=-1, keepdims=True))
        dcb_ref[...] += jnp.sum(dc1, axis=0, keepdims=True)
        dc1_ref[...] = dc1
        dgb_ref[...] = (ddo * y_ref[...].astype(F32)).astype(BF16)
        dy3_ref[...] = ddo * gb_ref[...].astype(F32)

    half = pl.BlockSpec((tm, 512), lambda i: (i, 0))
    vec = pl.BlockSpec((1, 512), lambda i: (0, 0))
    return pl.pallas_call(
        body, name="cd_bwd_pw", grid=(T // tm,),
        in_specs=[pl.BlockSpec((tm, D), lambda i: (i, 0)), half, pl.BlockSpec((tm, 512), lambda i: (i, 2)), half,
                  vec, vec],
        out_specs=[half, half, half, vec, vec, vec],
        out_shape=[_sds((T, 512), F32), _sds((T, 512), F32), _sds((T, 512), BF16),
                   _sds((1, 512), F32), _sds((1, 512), F32), _sds((1, 512), F32)],
        compiler_params=_cparams(1))(dcat, c1, pcd, y, lg, lb)


def _cd_bwd_conv(pcd, dc1, dy3, c0, dd, dgb, cw8, dw, tm=256):
    per = tm // HALO
    nblk = T // tm
    last32 = T // HALO - 1

    def body(p_ref, dc1_ref, dc1n_ref, dy3_ref, dy3n_ref, c0_ref, dd_ref, dgb_ref, cw_ref, dw_ref,
             o_ref, dcw_ref, ddw_ref, dbuf, d3buf, sd, dc0_buf):
        i = pl.program_id(0)
        has_next = jnp.where(i < nblk - 1, 1.0, 0.0)

        @pl.when(i == 0)
        def _():
            dcw_ref[...] = jnp.zeros_like(dcw_ref)
            ddw_ref[...] = jnp.zeros_like(ddw_ref)

        dbuf[0:tm, :] = dc1_ref[...]
        dbuf[tm:, :] = dc1n_ref[...] * has_next
        d3buf[0:tm, :] = dy3_ref[...]
        d3buf[tm:, :] = dy3n_ref[...] * has_next
        _shifted_copies(dbuf, sd, tm)
        n_tiles = tm // CONV_RC

        phases = _offsets_by_phase(0)

        def dc0_rows(r, carry):
            base = pl.multiple_of(r * CONV_RC, CONV_RC)
            for c in range(4):
                lanes = slice(c * 128, (c + 1) * 128)
                acc = jnp.zeros((CONV_RC // 8, 8, 128), F32)
                for b8, offsets in phases:
                    win = _window(sd, b8, base, offsets, lanes)
                    for o in offsets:
                        j = CONV_C_TAPS - 1 - o
                        acc = acc + cw_ref[8 * j:8 * j + 8, lanes] * win[o // 8:o // 8 + CONV_RC // 8]
                dc0_buf[pl.ds(base, CONV_RC), lanes] = acc.reshape(CONV_RC, 128)
            return carry

        lax.fori_loop(0, n_tiles, dc0_rows, 0)

        for c in range(4):
            lanes = slice(c * 128, (c + 1) * 128)
            for b8, offsets in phases:
                def dw_rows(r, accs, lanes=lanes, b8=b8, offsets=offsets):
                    base = pl.multiple_of(r * CONV_RC, CONV_RC)
                    xin = c0_ref[pl.ds(base, CONV_RC), lanes].astype(F32).reshape(CONV_RC // 8, 8, 128)
                    win = _window(sd, b8, base, offsets, lanes)
                    return tuple(acc + jnp.sum(xin * win[o // 8:o // 8 + CONV_RC // 8], axis=0)
                                 for acc, o in zip(accs, offsets))

                accs = lax.fori_loop(0, n_tiles, dw_rows, tuple(jnp.zeros((8, 128), F32) for _ in offsets))
                for acc, o in zip(accs, offsets):
                    j = CONV_C_TAPS - 1 - o
                    dcw_ref[j:j + 1, lanes] += jnp.sum(acc, axis=0, keepdims=True)

        dc0 = dc0_buf[...]
        ddin = dd_ref[...].astype(F32)
        ddd = jnp.zeros((tm, 512), F32)
        for j in range(CONV_D_TAPS):
            dy_shift = d3buf[pl.ds(CONV_D_TAPS - 1 - j, tm), :]
            ddd = ddd + dw_ref[j:j + 1, :] * dy_shift
            ddw_ref[j:j + 1, :] += jnp.sum(ddin * dy_shift, axis=0, keepdims=True)

        a = p_ref[:, 0:512].astype(F32)
        gt = p_ref[:, 512:1024].astype(F32)
        gc = p_ref[:, 1536:2048].astype(F32)
        hv = p_ref[:, 2048:2560].astype(F32)
        sg = _sigmoid(gt)
        o_ref[:, 0:512] = (dc0 * sg).astype(BF16)
        o_ref[:, 512:1024] = (dc0 * a * sg * (1.0 - sg)).astype(BF16)
        o_ref[:, 1024:1536] = dgb_ref[...]
        o_ref[:, 1536:2048] = (ddd * hv).astype(BF16)
        o_ref[:, 2048:2560] = (ddd * gc).astype(BF16)

    half = pl.BlockSpec((tm, 512), lambda i: (i, 0))
    nxt = pl.BlockSpec((HALO, 512), lambda i: (jnp.minimum((i + 1) * per, last32), 0))
    full = pl.BlockSpec((tm, CD_IN), lambda i: (i, 0))
    return pl.pallas_call(
        body, name="cd_bwd_conv", grid=(nblk,),
        in_specs=[full, half, nxt, half, nxt, half, half, half,
                  pl.BlockSpec((8 * 32, 512), lambda i: (0, 0)), pl.BlockSpec((8, 512), lambda i: (0, 0))],
        out_specs=[full, pl.BlockSpec((32, 512), lambda i: (0, 0)), pl.BlockSpec((8, 512), lambda i: (0, 0))],
        out_shape=[_sds((T, CD_IN), BF16), _sds((32, 512), F32), _sds((8, 512), F32)],
        scratch_shapes=[pltpu.VMEM((tm + HALO, 512), F32), pltpu.VMEM((tm + HALO, 512), F32),
                        pltpu.VMEM((8, tm + HALO, 512), F32), pltpu.VMEM((tm, 512), F32)],
        compiler_params=_cparams(1))(pcd, dc1, dc1, dy3, dy3, c0, dd, dgb, cw8, dw)


def _local_step(x, tgt, W, fetch=None, on_grad=None):
    W = dict(W)
    if fetch is None:
        fetch = lambda stage, after: {}
    if on_grad is None:
        on_grad = lambda key, arr: None
    tabs = _rope_tables()
    qg = jnp.tile(W["q_norm_g"], (1, 2))
    kg = jnp.tile(W["k_norm_g"], (1, 2))
    bias3 = W["sgu_bias"].reshape(4, 128, 1)
    G = {}

    h0 = _rms_fwd(x, W["ab_norm_g"], "rms_fwd_ab", dep=W.get("dep_first"))
    W.update(fetch("ab_in", h0))
    pab = _mm_nt(h0, W["wt_ab_in"], "mm_ab_in", dep=W.get("dep0"))
    cat_ab = _mix_a_fwd(pab, W["sgu_norm_g"], W["sgu_norm_b"], W["sgu_w"], bias3)
    qkv, rinvs = _prep_fwd(pab, qg, kg, tabs)
    outs, lses = [], []
    for g, rate in enumerate(DIL_RATES):
        o, l = _attn_fwd(qkv[3 * g], qkv[3 * g + 1], qkv[3 * g + 2], rate, f"attn_fwd_{g}", dep=W.get(f"dep_attn{g}"))
        outs.append(o)
        lses.append(l)
        W.update(fetch(f"attn{g}", o))
    cat_ab, lse = _merge_fwd(cat_ab, outs, lses)
    W.update(fetch("ab_out", lse))
    x1, h1 = _mm_nn(cat_ab, W["w_ab_out"], "mm_ab_out", mode="rms", resid=x, gain=W["ffn_norm_g"][0:1])
    pf0, act0 = _ffn_in(h1, W["wt_ffn_in0"], "ffn_in0")
    W.update(fetch("ffn_down0", act0))
    x2, h2 = _mm_nn(act0, W["w_ffn_down0"], "mm_ffn_down0", mode="rms", resid=x1, gain=W["cd_norm_g"],
                    dep=W.get("dep_down0"))
    W.update(fetch("cd_in", h2))
    pcd = _mm_nt(h2, W["wt_cd_in"], "mm_cd_in")
    cw8 = jnp.repeat(W["conv_c_w32"], 8, axis=0)
    cat_cd, c0, c1, dd, yv = _cd_fwd(pcd, cw8, W["conv_c_b"], W["c_ln_g"], W["c_ln_b"], W["conv_d_w8"])
    x3, h3 = _mm_nn(cat_cd, W["w_cd_out"], "mm_cd_out", mode="rms", resid=x2, gain=W["ffn_norm_g"][1:2])
    pf1, act1 = _ffn_in(h3, W["wt_ffn_in1"], "ffn_in1")
    dy, dyb, loss_cols = _mm_nn(act1, W["w_ffn_down1"], "mm_ffn_down1", mode="loss", resid=x3, tgt=tgt)

    def ffn_bwd(xin, h, pf, act, dres, dresb, layer):
        G[f"w_ffn_down{layer}"] = _mm_tn(act, dresb, f"mm_g_ffn_down{layer}")
        dep = on_grad(f"w_ffn_down{layer}", G[f"w_ffn_down{layer}"])
        dpf = _ffn_dact(dresb, W[f"w_ffn_down{layer}"], pf, f"ffn_dact{layer}", dep=dep)
        G[f"wt_ffn_in{layer}"] = _mm_tn(dpf, h, f"mm_g_ffn_in{layer}")
        dep = on_grad(f"wt_ffn_in{layer}", G[f"wt_ffn_in{layer}"])
        dx, dxb, G[f"ffn_norm_g{layer}"] = _mm_dh_rms_bwd(
            dpf, W[f"wt_ffn_in{layer}"], xin, W["ffn_norm_g"][layer:layer + 1], dres, f"mm_d_h_ffn{layer}", dep=dep)
        return dx, dxb

    dx3, dx3b = ffn_bwd(x3, h3, pf1, act1, dy, dyb, 1)

    G["w_cd_out"] = _mm_tn(cat_cd, dx3b, "mm_g_cd_out")
    dep = on_grad("w_cd_out", G["w_cd_out"])
    dcat_cd = _mm_nt(dx3b, W["w_cd_out"], "mm_d_cat_cd", dep=dep)
    dc1, dy3, dgb, G["c_ln_g"], G["c_ln_b"], G["conv_c_b"] = _cd_bwd_pw(dcat_cd, c1, pcd, yv, W["c_ln_g"], W["c_ln_b"])
    dpcd, G["conv_c_w32"], G["conv_d_w8"] = _cd_bwd_conv(pcd, dc1, dy3, c0, dd, dgb, cw8, W["conv_d_w8"])
    G["wt_cd_in"] = _mm_tn(dpcd, h2, "mm_g_cd_in")
    dep = on_grad("wt_cd_in", G["wt_cd_in"])
    dx2, dx2b, G["cd_norm_g"] = _mm_dh_rms_bwd(dpcd, W["wt_cd_in"], x2, W["cd_norm_g"], dx3, "mm_d_h_cd", dep=dep)

    dx1, dx1b = ffn_bwd(x1, h1, pf0, act0, dx2, dx2b, 0)

    G["w_ab_out"] = _mm_tn(cat_ab, dx1b, "mm_g_ab_out")
    dep = on_grad("w_ab_out", G["w_ab_out"])
    dcat_ab = _mm_nt(dx1b, W["w_ab_out"], "mm_d_cat_ab", dep=dep)
    dbp, e = _b_pre_bwd(dcat_ab, cat_ab)
    dqkv = []
    for g, rate in enumerate(DIL_RATES):
        dqkv += _attn_bwd(qkv[3 * g], qkv[3 * g + 1], qkv[3 * g + 2], dbp, e, lse, rate, f"attn_bwd_{g}")
    dpab, G["sgu_w"], dbias_part, G["sgu_norm_g"], G["sgu_norm_b"], dgain = _ab_in_bwd(
        pab, dcat_ab, W["sgu_norm_g"], W["sgu_norm_b"], W["sgu_w"], bias3, qg, kg, tabs, dqkv, rinvs)
    G["sgu_bias"] = jnp.sum(dbias_part, axis=-1)
    dgain = dgain[0:6, 0:HEAD] + dgain[0:6, HEAD:PAIR]
    G["q_norm_g"] = dgain[0::2]
    G["k_norm_g"] = dgain[1::2]
    G["loss_cols"] = loss_cols
    dep = on_grad("small", G)
    G["wt_ab_in"] = _mm_tn(dpab, h0, "mm_g_ab_in", dep=dep)
    dep = on_grad("wt_ab_in", G["wt_ab_in"])
    grad_x, G["ab_norm_g"] = _mm_dh_rms_bwd(dpab, W["wt_ab_in"], x, W["ab_norm_g"], dx1, "mm_d_h_ab", dep=dep,
                                            bf16_copy=False)
    return loss_cols, grad_x, G


def _my_place():
    return lax.axis_index("x"), lax.axis_index("y"), lax.axis_index("c")


def _dev_index(px, py, pc):
    return 4 * px + 2 * py + pc


def _flip(place, k):
    x, y, c = place
    return (1 - x if k & 4 else x, 1 - y if k & 2 else y, 1 - c if k & 1 else c)


def _landing(shape, dtype, own):
    buf = lax.empty(shape, dtype)
    for lead, part in own:
        buf = lax.dynamic_update_slice(buf, part.reshape((1,) * len(lead) + part.shape),
                                       tuple(lead) + (0,) * part.ndim)
    return buf


HBM_ONLY = pl.BlockSpec(memory_space=pltpu.HBM)
SEM_SPEC = pl.BlockSpec(memory_space=pltpu.SEMAPHORE)
IN_FLIGHT = pltpu.CompilerParams(has_side_effects=pltpu.SideEffectType.DATAFLOW_SIDE_EFFECTING)


def _in_hbm(a):
    return pltpu.with_memory_space_constraint(a, pltpu.HBM)


def _exchange_start(name, srcs, lands, items, dep=None):
    ns, nl, ni = len(srcs), len(lands), len(items)

    def body(*refs):
        S, L = refs[0:ns], refs[ns:ns + nl]
        first_out = ns + nl + (0 if dep is None else 1)
        send_sems, recv_sems, token = refs[first_out], refs[first_out + 1], refs[-1]
        me = _my_place()
        mi = _dev_index(*me)
        for i, (src, dst) in enumerate(items):
            for k in range(1, NDEV):
                peer = _flip(me, k)
                pltpu.make_async_remote_copy(
                    src_ref=src(S, _dev_index(*peer)), dst_ref=dst(L, mi), send_sem=send_sems.at[7 * i + k - 1],
                    recv_sem=recv_sems.at[7 * i + k - 1], device_id=peer, device_id_type=MESH).start()
        token[...] = jnp.zeros_like(token)

    thru = [pltpu.HBM(a.shape, a.dtype) for a in list(srcs) + list(lands)]
    args = [_in_hbm(a) for a in srcs] + [_in_hbm(a) for a in lands]
    in_specs = [HBM_ONLY] * (ns + nl)
    if dep is not None:
        args.append(dep)
        in_specs.append(HBM_SPEC)
    outs = pl.pallas_call(
        body, name=name, in_specs=in_specs,
        out_shape=(pltpu.SemaphoreType.DMA((7 * ni,)), pltpu.SemaphoreType.DMA((7 * ni,)), *thru, _sds((8, 128), F32)),
        out_specs=(SEM_SPEC, SEM_SPEC, *[HBM_ONLY] * (ns + nl), pl.BlockSpec(memory_space=pltpu.VMEM)),
        input_output_aliases={j: 2 + j for j in range(ns + nl)}, compiler_params=IN_FLIGHT)(*args)
    return dict(send=outs[0], recv=outs[1], srcs=list(outs[2:2 + ns]), lands=list(outs[2 + ns:2 + ns + nl]),
                token=outs[-1], items=items)


def _exchange_wait(name, states, after):
    after = list(after) if isinstance(after, (list, tuple)) else [after]
    counts = [(len(st["srcs"]), len(st["lands"]), len(st["items"])) for st in states]
    n_arrays = sum(c[0] + c[1] for c in counts)

    def body(*refs):
        me = _my_place()
        mi = _dev_index(*me)
        pos = 0
        sem_pos = n_arrays
        for st, (ns, nl, ni) in zip(states, counts):
            S, L = refs[pos:pos + ns], refs[pos + ns:pos + ns + nl]
            send_sems, recv_sems = refs[sem_pos], refs[sem_pos + 1]
            pos += ns + nl
            sem_pos += 2
            for i, (src, dst) in enumerate(st["items"]):
                for k in range(1, NDEV):
                    cp = pltpu.make_async_remote_copy(
                        src_ref=src(S, mi), dst_ref=dst(L, mi), send_sem=send_sems.at[7 * i + k - 1],
                        recv_sem=recv_sems.at[7 * i + k - 1], device_id=me, device_id_type=MESH)
                    cp.wait_send()
                    cp.wait_recv()

    arrays, sems = [], []
    for st in states:
        arrays += st["srcs"] + st["lands"]
        sems += [st["send"], st["recv"]]
    outs = pl.pallas_call(
        body, name=name, in_specs=[HBM_ONLY] * n_arrays + [SEM_SPEC] * len(sems) + [HBM_SPEC] * len(after),
        out_shape=tuple(pltpu.HBM(a.shape, a.dtype) for a in arrays), out_specs=tuple([HBM_ONLY] * n_arrays),
        input_output_aliases={j: j for j in range(n_arrays)}, compiler_params=IN_FLIGHT)(*arrays, *sems, *after)
    lands, pos = [], 0
    for ns, nl, _ in counts:
        lands.append(list(outs[pos + ns:pos + ns + nl]))
        pos += ns + nl
    return lands


def _place_and_neighbours():
    x, y, c = _my_place()
    return (x, y, c), (x, y, 1 - c), [(1 - x, y), (x, 1 - y), (1 - x, 1 - y)]


def _gather_start(name, srcs, lands, items, dep=None):
    ns, nl, ni = len(srcs), len(lands), len(items)

    def body(*refs):
        S, L = refs[0:ns], refs[ns:ns + nl]
        first_out = ns + nl + (0 if dep is None else 1)
        send_sems, recv_sems, token = refs[first_out], refs[first_out + 1], refs[-1]
        me, sib, chips = _place_and_neighbours()
        mi = _dev_index(*me)
        for i, (src, dst) in enumerate(items):
            for k, to in enumerate([sib] + [(*chip, me[2]) for chip in chips]):
                pltpu.make_async_remote_copy(
                    src_ref=src(S), dst_ref=dst(L, mi), send_sem=send_sems.at[4 * i + k],
                    recv_sem=recv_sems.at[4 * i + k], device_id=to, device_id_type=MESH).start()
        token[...] = jnp.zeros_like(token)

    thru = [pltpu.HBM(a.shape, a.dtype) for a in list(srcs) + list(lands)]
    args = [_in_hbm(a) for a in srcs] + [_in_hbm(a) for a in lands]
    in_specs = [HBM_ONLY] * (ns + nl)
    if dep is not None:
        args.append(dep)
        in_specs.append(HBM_SPEC)
    outs = pl.pallas_call(
        body, name=name, in_specs=in_specs,
        out_shape=(pltpu.SemaphoreType.DMA((4 * ni,)), pltpu.SemaphoreType.DMA((4 * ni,)), *thru, _sds((8, 128), F32)),
        out_specs=(SEM_SPEC, SEM_SPEC, *[HBM_ONLY] * (ns + nl), pl.BlockSpec(memory_space=pltpu.VMEM)),
        input_output_aliases={j: 2 + j for j in range(ns + nl)}, compiler_params=IN_FLIGHT)(*args)
    return dict(send=outs[0], recv=outs[1], srcs=list(outs[2:2 + ns]), lands=list(outs[2 + ns:2 + ns + nl]),
                token=outs[-1], items=items)


def _gather_forward(name, st, after):
    nl, ni = len(st["lands"]), len(st["items"])

    def body(*refs):
        L, recv_sems = refs[0:nl], refs[nl]
        fwd_send, fwd_recv, token = refs[-3:]
        me, sib, chips = _place_and_neighbours()
        for i, (_, dst) in enumerate(st["items"]):
            for j, chip in enumerate(chips):
                blk = dst(L, _dev_index(*chip, me[2]))
                pltpu.make_async_remote_copy(
                    src_ref=blk, dst_ref=blk, send_sem=fwd_send.at[3 * i + j], recv_sem=recv_sems.at[4 * i + 1 + j],
                    device_id=me, device_id_type=MESH).wait_recv()
                pltpu.make_async_remote_copy(
                    src_ref=blk, dst_ref=blk, send_sem=fwd_send.at[3 * i + j], recv_sem=fwd_recv.at[3 * i + j],
                    device_id=sib, device_id_type=MESH).start()
        token[...] = jnp.zeros_like(token)

    after = list(after) if isinstance(after, (list, tuple)) else [after]
    outs = pl.pallas_call(
        body, name=name, in_specs=[HBM_ONLY] * nl + [SEM_SPEC] + [HBM_SPEC] * len(after),
        out_shape=(*[pltpu.HBM(a.shape, a.dtype) for a in st["lands"]], pltpu.SemaphoreType.DMA((3 * ni,)),
                   pltpu.SemaphoreType.DMA((3 * ni,)), _sds((8, 128), F32)),
        out_specs=(*[HBM_ONLY] * nl, SEM_SPEC, SEM_SPEC, pl.BlockSpec(memory_space=pltpu.VMEM)),
        input_output_aliases={j: j for j in range(nl)}, compiler_params=IN_FLIGHT)(*st["lands"], st["recv"], *after)
    return dict(st, lands=list(outs[0:nl]), fwd_send=outs[nl], fwd_recv=outs[nl + 1], token=outs[-1])


def _gather_wait(name, st, after):
    ns, nl, ni = len(st["srcs"]), len(st["lands"]), len(st["items"])

    def body(*refs):
        S, L = refs[0:ns], refs[ns:ns + nl]
        send_sems, recv_sems, fwd_send, fwd_recv = refs[ns + nl:ns + nl + 4]
        me, sib, chips = _place_and_neighbours()
        mi = _dev_index(*me)
        for i, (src, dst) in enumerate(st["items"]):
            mine = dst(L, mi)
            for k in range(4):
                pltpu.make_async_remote_copy(
                    src_ref=src(S), dst_ref=mine, send_sem=send_sems.at[4 * i + k], recv_sem=recv_sems.at[4 * i + k],
                    device_id=me, device_id_type=MESH).wait_send()
            pltpu.make_async_remote_copy(
                src_ref=src(S), dst_ref=mine, send_sem=send_sems.at[4 * i], recv_sem=recv_sems.at[4 * i],
                device_id=me, device_id_type=MESH).wait_recv()
            for j in range(3):
                cp = pltpu.make_async_remote_copy(
                    src_ref=mine, dst_ref=mine, send_sem=fwd_send.at[3 * i + j], recv_sem=fwd_recv.at[3 * i + j],
                    device_id=me, device_id_type=MESH)
                cp.wait_send()
                cp.wait_recv()

    arrays = st["srcs"] + st["lands"]
    outs = pl.pallas_call(
        body, name=name, in_specs=[HBM_ONLY] * (ns + nl) + [SEM_SPEC] * 4 + [HBM_SPEC],
        out_shape=tuple(pltpu.HBM(a.shape, a.dtype) for a in arrays), out_specs=tuple([HBM_ONLY] * (ns + nl)),
        input_output_aliases={j: j for j in range(ns + nl)},
        compiler_params=IN_FLIGHT)(*arrays, st["send"], st["recv"], st["fwd_send"], st["fwd_recv"], after)
    return list(outs[ns:ns + nl])


def _sum_slots(land):
    def body(l_ref, o_ref):
        acc = l_ref[0]
        for d in range(1, NDEV):
            acc = acc + l_ref[d]
        o_ref[...] = acc

    vm = pl.BlockSpec(memory_space=pltpu.VMEM)
    return pl.pallas_call(body, name="sum_small", out_shape=_sds(land.shape[1:], F32), in_specs=[vm], out_specs=vm)(land)


def _adam_math(w, g, m, v):
    m2 = ADAM_B1 * m + (1.0 - ADAM_B1) * g
    v2 = ADAM_B2 * v + (1.0 - ADAM_B2) * (g * g)
    delta = -ADAM_LR * ((m2 * ADAM_C1) / (jnp.sqrt(v2 * ADAM_C2) + ADAM_EPS) + ADAM_WD * w)
    return delta, m2, v2


def _adam_layer(land, sel, w, m, v, layer, name, prev=None, tc=512):
    R = land.shape[2]

    def body(l_ref, w_ref, m_ref, v_ref, *rest):
        g_out, d_out, m_out, v_out = rest[-4:]
        g = l_ref[0].astype(F32)
        for d in range(1, NDEV):
            g = g + l_ref[d].astype(F32)
        delta, m2, v2 = _adam_math(w_ref[...], g, m_ref[...], v_ref[...])
        g_out[...] = g
        d_out[...] = delta
        m_out[...] = m2
        v_out[...] = v2

    wspec = pl.BlockSpec((None, R, tc), lambda i: (layer, 0, i))
    in_specs = [pl.BlockSpec((None, NDEV, R, tc), lambda i: (sel, 0, 0, i)), wspec, wspec, wspec]
    args = [land, w, m, v]
    aliases = {}
    if prev is not None:
        in_specs += [HBM_SPEC] * 4
        args += list(prev)
        aliases = {4 + j: j for j in range(4)}
    return pl.pallas_call(
        body, name=name, grid=(D // tc,), in_specs=in_specs, out_specs=[wspec] * 4,
        out_shape=[_sds(w.shape, F32)] * 4, input_output_aliases=aliases, compiler_params=_cparams(1))(*args)


def _adam_stacked(lands, sel, w, m, v, name):
    res = None
    for layer, land in enumerate(lands):
        res = _adam_layer(land, sel, w, m, v, layer, f"{name}{layer}", prev=res)
    return res


def _adam_small(ws, gs, ms, vs):
    n = len(ws)

    def body(*refs):
        w_r, g_r, m_r, v_r = refs[0:n], refs[n:2 * n], refs[2 * n:3 * n], refs[3 * n:4 * n]
        d_o, m_o, v_o = refs[4 * n:5 * n], refs[5 * n:6 * n], refs[6 * n:7 * n]
        for i in range(n):
            delta, m2, v2 = _adam_math(w_r[i][...], g_r[i][...], m_r[i][...], v_r[i][...])
            d_o[i][...] = delta
            m_o[i][...] = m2
            v_o[i][...] = v2

    vm = pl.BlockSpec(memory_space=pltpu.VMEM)
    shapes = [_sds(w.shape, F32) for w in ws]
    outs = pl.pallas_call(body, name="adam_small", in_specs=[vm] * (4 * n), out_specs=[vm] * (3 * n),
                          out_shape=shapes * 3)(*ws, *gs, *ms, *vs)
    return outs[0:n], outs[n:2 * n], outs[2 * n:3 * n]


def _adam_of_slots(land, w, m, v, name):
    def body(l_ref, w_ref, m_ref, v_ref, g_o, d_o, m_o, v_o):
        g = l_ref[0]
        for d in range(1, NDEV):
            g = g + l_ref[d]
        g_o[...] = g
        d_o[...], m_o[...], v_o[...] = _adam_math(w_ref[...], g, m_ref[...], v_ref[...])

    vm = pl.BlockSpec(memory_space=pltpu.VMEM)
    return pl.pallas_call(body, name=name, in_specs=[vm] * 4, out_specs=[vm] * 4,
                          out_shape=[_sds(w.shape, F32)] * 4)(land, w, m, v)


WEIGHT_NAMES = ("ab_norm_g", "ab_w_in", "sgu_norm_g", "sgu_norm_b", "sgu_w", "sgu_bias", "q_norm_g", "k_norm_g",
                "ab_w_out", "cd_norm_g", "cd_w_in", "conv_c_w", "conv_c_b", "c_ln_g", "c_ln_b", "conv_d_w",
                "cd_w_out", "ffn_norm_g", "ffn_w_gate", "ffn_w_up", "ffn_w_down")
SMALL_2D = (("sgu_norm_g", (1, 512)), ("sgu_norm_b", (1, 512)), ("sgu_w", (512, 128)),
            ("sgu_bias", (4, 128)), ("q_norm_g", (3, 64)), ("k_norm_g", (3, 64)), ("cd_norm_g", (1, 128)),
            ("conv_c_w", (31, 64)), ("conv_c_b", (1, 64)), ("c_ln_g", (1, 64)), ("c_ln_b", (1, 64)),
            ("conv_d_w", (3, 64)), ("ffn_norm_g", (2, 1024)))
SHARD_C = 64


def _pack_rows(parts, rows):
    flat = jnp.concatenate([p.reshape(-1) for p in parts])
    return jnp.pad(flat, (0, rows * 128 - flat.shape[0])).reshape(rows, 128)


def kernel(x, ab_norm_g, ab_w_in, sgu_norm_g, sgu_norm_b, sgu_w, sgu_bias, q_norm_g, k_norm_g, ab_w_out, cd_norm_g, cd_w_in, conv_c_w, conv_c_b, c_ln_g, c_ln_b, conv_d_w, cd_w_out, ffn_norm_g, ffn_w_gate, ffn_w_up, ffn_w_down, loss_target, m_ab_norm_g, m_ab_w_in, m_sgu_norm_g, m_sgu_norm_b, m_sgu_w, m_sgu_bias, m_q_norm_g, m_k_norm_g, m_ab_w_out, m_cd_norm_g, m_cd_w_in, m_conv_c_w, m_conv_c_b, m_c_ln_g, m_c_ln_b, m_conv_d_w, m_cd_w_out, m_ffn_norm_g, m_ffn_w_gate, m_ffn_w_up, m_ffn_w_down, v_ab_norm_g, v_ab_w_in, v_sgu_norm_g, v_sgu_norm_b, v_sgu_w, v_sgu_bias, v_q_norm_g, v_k_norm_g, v_ab_w_out, v_cd_norm_g, v_cd_w_in, v_conv_c_w, v_conv_c_b, v_c_ln_g, v_c_ln_b, v_conv_d_w, v_cd_w_out, v_ffn_norm_g, v_ffn_w_gate, v_ffn_w_up, v_ffn_w_down):
    w = dict(zip(WEIGHT_NAMES, (ab_norm_g, ab_w_in, sgu_norm_g, sgu_norm_b, sgu_w, sgu_bias, q_norm_g, k_norm_g, ab_w_out, cd_norm_g, cd_w_in, conv_c_w, conv_c_b, c_ln_g, c_ln_b, conv_d_w, cd_w_out, ffn_norm_g, ffn_w_gate, ffn_w_up, ffn_w_down)))
    m = dict(zip(WEIGHT_NAMES, (m_ab_norm_g, m_ab_w_in, m_sgu_norm_g, m_sgu_norm_b, m_sgu_w, m_sgu_bias, m_q_norm_g, m_k_norm_g, m_ab_w_out, m_cd_norm_g, m_cd_w_in, m_conv_c_w, m_conv_c_b, m_c_ln_g, m_c_ln_b, m_conv_d_w, m_cd_w_out, m_ffn_norm_g, m_ffn_w_gate, m_ffn_w_up, m_ffn_w_down)))
    v = dict(zip(WEIGHT_NAMES, (v_ab_norm_g, v_ab_w_in, v_sgu_norm_g, v_sgu_norm_b, v_sgu_w, v_sgu_bias, v_q_norm_g, v_k_norm_g, v_ab_w_out, v_cd_norm_g, v_cd_w_in, v_conv_c_w, v_conv_c_b, v_c_ln_g, v_c_ln_b, v_conv_d_w, v_cd_w_out, v_ffn_norm_g, v_ffn_w_gate, v_ffn_w_up, v_ffn_w_down)))
    me = _dev_index(*_my_place())

    r_ff = DFF // NDEV
    one = lambda a: (lambda S, j: S[a])
    slot = lambda b: (lambda L, s: L[b].at[s])
    slot2 = lambda b, part: (lambda L, s: L[b].at[part, s])
    shard = lambda a: (lambda S: S[a])

    def later(a):
        return lax.optimization_barrier((a, gathers[0]["token"]))[0]

    def layer_shards(layer):
        return (later(w["ffn_w_gate"][layer]).T.astype(BF16), later(w["ffn_w_up"][layer]).T.astype(BF16),
                later(w["ffn_w_down"][layer]).astype(BF16))

    def gathered(own):
        return _landing((NDEV,) + own.shape, BF16, [((me,), own)])

    def gathered2(a, b):
        return _landing((2, NDEV) + a.shape, BF16, [((0, me), a), ((1, me), b)])

    ab_in_s = w["ab_w_in"][0].T.astype(BF16)
    gathers = {0: _gather_start("gather0_start", [ab_in_s], [gathered(ab_in_s)], [(shard(0), slot(0))])}

    def chan(flat, lo, taps):
        return flat[:, lo:lo + taps * SHARD_C].reshape(NDEV, taps, SHARD_C).transpose(1, 0, 2).reshape(taps, 512)

    def fetch(stage, after):
        if stage == "ab_in":
            ab_out_s = later(w["ab_w_out"][0]).astype(BF16)
            gate0, up0, down0 = layer_shards(0)
            small_s = _pack_rows([later(w[n]) for n in ("cd_norm_g", "conv_c_w", "conv_c_b", "c_ln_g", "c_ln_b",
                                                        "conv_d_w")], 24)
            lands1 = [gathered(ab_out_s), gathered2(gate0, up0), gathered(down0),
                      _landing((NDEV,) + small_s.shape, F32, [((me,), small_s)])]
            gathers[0] = _gather_forward("gather0_forward", gathers[0], [after] + lands1)
            l_ab_in, = _gather_wait("gather0_wait", gathers[0], gathers[0]["token"])
            gathers[1] = _gather_start(
                "gather1_start", [ab_out_s, gate0, up0, down0, small_s], lands1,
                [(shard(0), slot(0)), (shard(1), slot2(1, 0)), (shard(2), slot2(1, 1)), (shard(3), slot(2)),
                 (shard(4), slot(3))], dep=l_ab_in)
            return {"wt_ab_in": l_ab_in.reshape(AB_IN, D), "dep0": gathers[1]["token"]}
        if stage == "attn0":
            cd_in_s, cd_out_s = later(w["cd_w_in"][0]).T.astype(BF16), later(w["cd_w_out"][0]).astype(BF16)
            gate1, up1, down1 = layer_shards(1)
            gathers[2] = _gather_start(
                "gather2_start", [cd_in_s, cd_out_s, gate1, up1, down1],
                [gathered(cd_in_s), gathered(cd_out_s), gathered2(gate1, up1), gathered(down1)],
                [(shard(0), slot(0)), (shard(1), slot(1)), (shard(2), slot2(2, 0)), (shard(3), slot2(2, 1)),
                 (shard(4), slot(3))], dep=after)
            return {"dep_attn1": gathers[2]["token"]}
        if stage == "attn1":
            gathers[1] = _gather_forward("gather1_forward", gathers[1], after)
            return {"dep_attn2": gathers[1]["token"]}
        if stage == "ab_out":
            l_out, l_ffn, l_down, l_small = _gather_wait("gather1_wait", gathers[1], after)
            flat = l_small.reshape(NDEV, 24 * 128)
            return {
                "w_ab_out": l_out.reshape(D, D), "wt_ffn_in0": l_ffn.reshape(2 * DFF, D),
                "w_ffn_down0": l_down.reshape(DFF, D), "cd_norm_g": flat[:, 0:128].reshape(1, D),
                "conv_c_w32": jnp.pad(chan(flat, 128, CONV_C_TAPS), ((0, 1), (0, 0))),
                "conv_c_b": chan(flat, 2112, 1), "c_ln_g": chan(flat, 2176, 1), "c_ln_b": chan(flat, 2240, 1),
                "conv_d_w8": jnp.pad(chan(flat, 2304, CONV_D_TAPS), ((0, 8 - CONV_D_TAPS), (0, 0))),
            }
        if stage == "ffn_down0":
            gathers[2] = _gather_forward("gather2_forward", gathers[2], after)
            return {"dep_down0": gathers[2]["token"]}
        if stage == "cd_in":
            l_in, l_out, l_ffn, l_down = _gather_wait("gather2_wait", gathers[2], after)
            return {"wt_cd_in": l_in.reshape(CD_IN, D), "w_cd_out": l_out.reshape(D, D),
                    "wt_ffn_in1": l_ffn.reshape(2 * DFF, D), "w_ffn_down1": l_down.reshape(DFF, D)}
        return {}

    scatters = {}
    rides_with = {"w_ffn_down1": "wt_ffn_in1", "w_cd_out": "wt_cd_in", "w_ffn_down0": "wt_ffn_in0"}
    held = {}
    smalls = {}

    def small_exchange(name, block):
        land = _landing((NDEV,) + block.shape, F32, [((me,), block)])
        return _exchange_start(name, [block], [land], [(one(0), slot(0))])

    def on_grad(key, arr):
        if key == "small":
            parts = [arr["sgu_norm_g"], arr["sgu_norm_b"], arr["sgu_w"], arr["sgu_bias"], arr["q_norm_g"],
                     arr["k_norm_g"], arr["cd_norm_g"], arr["conv_c_w32"][:CONV_C_TAPS], arr["conv_c_b"], arr["c_ln_g"],
                     arr["c_ln_b"], arr["conv_d_w8"][:CONV_D_TAPS], arr["ffn_norm_g0"], arr["ffn_norm_g1"],
                     arr["loss_cols"]]
            smalls["sizes"] = [p.size for p in parts]
            rows = -(-sum(smalls["sizes"]) // 1024) * 8
            smalls["early"] = small_exchange("small_start", _pack_rows(parts, rows))
            return smalls["early"]["token"]
        if key in rides_with:
            held[rides_with[key]] = (key, arr)
            return None
        group = ([held.pop(key)] if key in held else []) + [(key, arr)]
        srcs, lands, items = [], [], []
        for n, (k, a) in enumerate(group):
            if k.startswith("wt_ffn_in"):
                src = a.reshape(2, NDEV, r_ff, D)
                own = lax.dynamic_slice_in_dim(src, me, 1, axis=1)
                lands.append(lax.dynamic_update_slice(lax.empty(src.shape, BF16), own, (0, me, 0, 0)))
                items += [((lambda S, j, n=n: S[n].at[0, j]), slot2(n, 0)), ((lambda S, j, n=n: S[n].at[1, j]), slot2(n, 1))]
            else:
                rows = a.shape[0] // NDEV
                src = a.reshape(NDEV, rows, D)
                own = lax.dynamic_index_in_dim(src, me, 0, keepdims=False)
                lands.append(_landing((1, NDEV, rows, D), BF16, [((0, me), own)]))
                items.append(((lambda S, j, n=n: S[n].at[j]), slot2(n, 0)))
            srcs.append(src)
        st = _exchange_start(f"scatter_{key}_start", srcs, lands, items)
        scatters[key] = (st, [k for k, _ in group])
        return st["token"]

    W = {
        "dep_first": gathers[0]["token"],
        "ab_norm_g": w["ab_norm_g"], "sgu_norm_g": w["sgu_norm_g"], "sgu_norm_b": w["sgu_norm_b"],
        "sgu_w": w["sgu_w"][0], "sgu_bias": w["sgu_bias"][0], "q_norm_g": w["q_norm_g"][0],
        "k_norm_g": w["k_norm_g"][0], "ffn_norm_g": w["ffn_norm_g"],
    }

    loss_cols, grad_x, G = _local_step(x[0], loss_target[0], W, fetch, on_grad)

    late_small = small_exchange("small_late_start", G["ab_norm_g"])
    landed = {}

    def wait_scatters(name, group_keys, others, after):
        res = _exchange_wait(name, [scatters[gk][0] for gk in group_keys] + others, after)
        for gk, lands in zip(group_keys, res):
            landed.update(zip(scatters[gk][1], lands))
        return [lands[0] for lands in res[len(group_keys):]]

    small_land, = wait_scatters("scatter_wait_early", ["wt_ffn_in1", "wt_cd_in", "wt_ffn_in0", "w_ab_out"],
                                [smalls["early"]], late_small["token"])

    grads, deltas, new_m, new_v = {}, {}, {}, {}
    done = []

    def put(name, res):
        grads[name], deltas[name], new_m[name], new_v[name] = res

    def adam(name, lands, sel, transposed):
        flip = (lambda a: jnp.swapaxes(a, 1, 2)) if transposed else (lambda a: a)
        res = _adam_stacked(lands, sel, flip(w[name]), flip(m[name]), flip(v[name]), f"adam_{name}")
        done.append(res[1])
        put(name, [flip(r) for r in res])

    ffn_in_lands = [landed["wt_ffn_in0"], landed["wt_ffn_in1"]]
    adam("cd_w_in", [landed["wt_cd_in"]], 0, True)
    adam("ffn_w_gate", ffn_in_lands, 0, True)
    adam("ffn_w_up", ffn_in_lands, 1, True)
    adam("cd_w_out", [landed["w_cd_out"]], 0, False)
    adam("ab_w_out", [landed["w_ab_out"]], 0, False)
    adam("ffn_w_down", [landed["w_ffn_down0"], landed["w_ffn_down1"]], 0, False)

    red = _sum_slots(small_land).reshape(-1)
    offs = [0]
    for s in smalls["sizes"]:
        offs.append(offs[-1] + s)
    seg = [red[offs[i]:offs[i + 1]] for i in range(len(smalls["sizes"]))]
    loss = jnp.sum(seg[14])

    def own_channels(full, taps):
        return lax.dynamic_slice_in_dim(full.reshape(taps, 512), me * SHARD_C, SHARD_C, axis=1)

    g_small = {
        "sgu_norm_g": seg[0].reshape(1, 512), "sgu_norm_b": seg[1].reshape(1, 512),
        "sgu_w": seg[2].reshape(512, 128), "sgu_bias": seg[3].reshape(4, 128), "q_norm_g": seg[4].reshape(3, 64),
        "k_norm_g": seg[5].reshape(3, 64),
        "cd_norm_g": lax.dynamic_slice_in_dim(seg[6].reshape(1, D), me * (D // NDEV), D // NDEV, axis=1),
        "conv_c_w": own_channels(seg[7], CONV_C_TAPS), "conv_c_b": own_channels(seg[8], 1),
        "c_ln_g": own_channels(seg[9], 1), "c_ln_b": own_channels(seg[10], 1),
        "conv_d_w": own_channels(seg[11], CONV_D_TAPS),
        "ffn_norm_g": jnp.concatenate([seg[12].reshape(1, D), seg[13].reshape(1, D)], axis=0),
    }

    d_s, m_s, v_s = _adam_small([w[n].reshape(s) for n, s in SMALL_2D], [g_small[n] for n, _ in SMALL_2D],
                                [m[n].reshape(s) for n, s in SMALL_2D], [v[n].reshape(s) for n, s in SMALL_2D])
    for i, (n, _) in enumerate(SMALL_2D):
        shape = w[n].shape
        grads[n], deltas[n] = g_small[n].reshape(shape), d_s[i].reshape(shape)
        new_m[n], new_v[n] = m_s[i].reshape(shape), v_s[i].reshape(shape)
    done.append(d_s[0])

    late_land, = wait_scatters("scatter_wait_last", ["wt_ab_in"], [late_small], list(done))
    put("ab_norm_g", _adam_of_slots(late_land, w["ab_norm_g"], m["ab_norm_g"], v["ab_norm_g"], "adam_ab_norm_g"))
    adam("ab_w_in", [landed["wt_ab_in"]], 0, True)

    return (loss, grad_x[None], *[grads[n] for n in WEIGHT_NAMES], *[deltas[n] for n in WEIGHT_NAMES],
            *[new_m[n] for n in WEIGHT_NAMES], *[new_v[n] for n in WEIGHT_NAMES])
```

```python
import jax
import jax.numpy as jnp
import numpy as np
from jax import lax
from jax.experimental import pallas as pl
from jax.experimental.pallas import tpu as pltpu

F32 = jnp.float32
BF16 = jnp.bfloat16

T = 4096
D = 1024
NDEV = 8
EPS = 1e-6
NEG_INF = -1e30
DFF = 2816
AB_IN = 5632
CD_IN = 2560
HEAD = 64
PAIR = 128
NPAIR = 4
NBACK = 128
DIL_RATES = (1, 4, 16)
ROPE_HALF = 8
ROPE_THETA = 500000.0
CONV_C_TAPS = 31
CONV_D_TAPS = 3
HALO = 32
ATTN_BWD_UNROLL = 4

ADAM_LR = 0.001
ADAM_B1 = 0.9
ADAM_B2 = 0.999
ADAM_EPS = 1e-08
ADAM_WD = 0.01
ADAM_STEP = 10
ADAM_C1 = 1.0 / (1.0 - ADAM_B1 ** ADAM_STEP)
ADAM_C2 = 1.0 / (1.0 - ADAM_B2 ** ADAM_STEP)

VMEM_LIMIT_MB = 48
MESH = pl.DeviceIdType.MESH
HBM_SPEC = pl.BlockSpec(memory_space=pl.ANY)


def _cparams(ngrid, vmem_mb=VMEM_LIMIT_MB):
    return pltpu.CompilerParams(dimension_semantics=("arbitrary",) * ngrid,
                                vmem_limit_bytes=vmem_mb * 1024 * 1024)


def _pick(n, options):
    for o in options:
        if n % o == 0:
            return o
    raise ValueError(f"no tile for {n} in {options}")


def _sds(shape, dtype):
    return jax.ShapeDtypeStruct(shape, dtype)


def _sigmoid(x):
    return 1.0 / (1.0 + jnp.exp(-x))


def _sigmoid_bf16(x):
    return 0.5 * jnp.tanh(0.5 * x) + 0.5


def _gelu(z):
    return 0.5 * z * (1.0 + lax.erf(z * 0.7071067811865476))


def _gelu_grad(z):
    return 0.5 * (1.0 + lax.erf(z * 0.7071067811865476)) + z * jnp.exp(-0.5 * z * z) * 0.3989422804014327


def _mm_nt(a, wt, name, out_dtype=BF16, dep=None):
    M, K = a.shape
    N = wt.shape[0]
    tn = _pick(N, (512, 256))

    def body(a_ref, w_ref, *rest):
        o_ref = rest[-1]
        for r0 in range(0, M, 1024):
            o_ref[r0:r0 + 1024, :] = lax.dot_general(
                a_ref[r0:r0 + 1024, :], w_ref[...], (((1,), (1,)), ((), ())),
                preferred_element_type=F32).astype(o_ref.dtype)

    in_specs = [pl.BlockSpec((M, K), lambda j: (0, 0), pipeline_mode=pl.Buffered(1)),
                pl.BlockSpec((tn, K), lambda j: (j, 0))]
    args = [a, wt]
    if dep is not None:
        in_specs.append(HBM_SPEC)
        args.append(dep)
    return pl.pallas_call(
        body, name=name, grid=(N // tn,), in_specs=in_specs, out_specs=pl.BlockSpec((M, tn), lambda j: (0, j)),
        out_shape=_sds((M, N), out_dtype), compiler_params=_cparams(1))(*args)


EPI_ROWS = 256


def _mm_nt_rows(a, wt, name, epilogue, side, side_specs, out_specs, out_shape, sums=(), dep=None, tm=512):
    M, K = a.shape
    N = wt.shape[0]
    ns, no = len(side), len(out_shape)

    def body(a_ref, w_ref, *rest):
        side_refs, outs, acc = rest[0:ns], rest[-1 - no:-1], rest[-1]
        acc[...] = lax.dot_general(a_ref[...], w_ref[...], (((1,), (1,)), ((), ())), preferred_element_type=F32)

        @pl.when(pl.program_id(0) == 0)
        def _():
            for j in sums:
                outs[j][...] = jnp.zeros_like(outs[j])

        for r0 in range(0, tm, EPI_ROWS):
            rows = slice(r0, r0 + EPI_ROWS)
            epilogue(acc[rows, :].astype(BF16).astype(F32), rows, side_refs, outs)

    in_specs = [pl.BlockSpec((tm, K), lambda i: (i, 0)),
                pl.BlockSpec((N, K), lambda i: (0, 0), pipeline_mode=pl.Buffered(1))] + list(side_specs)
    args = [a, wt, *side]
    if dep is not None:
        in_specs.append(HBM_SPEC)
        args.append(dep)
    return pl.pallas_call(
        body, name=name, grid=(M // tm,), in_specs=in_specs, out_specs=list(out_specs), out_shape=list(out_shape),
        scratch_shapes=[pltpu.VMEM((tm, N), F32)], compiler_params=_cparams(1))(*args)


def _mm_nn(a, w, name, mode, resid, gain=None, tgt=None, dep=None, tm=512):
    M, K = a.shape
    N = w.shape[1]
    side = gain if mode == "rms" else tgt

    def body(a_ref, w_ref, resid_ref, side_ref, *rest):
        outs, acc = rest[-3 if mode == "rms" else -4:-1], rest[-1]
        i = pl.program_id(0)
        acc[...] = jnp.dot(a_ref[...], w_ref[...], preferred_element_type=F32)

        if mode == "loss":
            @pl.when(i == 0)
            def _():
                outs[2][...] = jnp.zeros_like(outs[2])

        for r0 in range(0, tm, EPI_ROWS):
            rows = slice(r0, r0 + EPI_ROWS)
            v = acc[rows, :] + resid_ref[rows, :]
            if mode == "rms":
                outs[0][rows, :] = v
                r = lax.rsqrt(jnp.mean(v * v, axis=-1, keepdims=True) + EPS)
                outs[1][rows, :] = (v * r * side_ref[...]).astype(BF16)
            else:
                d = v - side_ref[rows, :]
                outs[2][...] += jnp.sum(d * d, axis=0, keepdims=True) * (0.5 / N)
                dy = d * (1.0 / N)
                outs[0][rows, :] = dy
                outs[1][rows, :] = dy.astype(BF16)

    row = pl.BlockSpec((tm, N), lambda i: (i, 0))
    vec = pl.BlockSpec((1, N), lambda i: (0, 0))
    in_specs = [pl.BlockSpec((tm, K), lambda i: (i, 0)),
                pl.BlockSpec((K, N), lambda i: (0, 0), pipeline_mode=pl.Buffered(1)), row,
                vec if mode == "rms" else row]
    args = [a, w, resid, side]
    if dep is not None:
        in_specs.append(HBM_SPEC)
        args.append(dep)
    if mode == "rms":
        out_specs, out_shape = [row, row], [_sds((M, N), F32), _sds((M, N), BF16)]
    else:
        out_specs, out_shape = [row, row, vec], [_sds((M, N), F32), _sds((M, N), BF16), _sds((1, N), F32)]
    return pl.pallas_call(
        body, name=name, grid=(M // tm,), in_specs=in_specs, out_specs=out_specs, out_shape=out_shape,
        scratch_shapes=[pltpu.VMEM((tm, N), F32)], compiler_params=_cparams(1))(*args)


def _mm_dh_rms_bwd(a, w, x, gain, dres, name, dep=None, tm=512, bf16_copy=True):
    parts = a.shape[0] if a.ndim == 3 else 1
    M, Kp = a.shape[-2], a.shape[-1]
    N = w.shape[1]
    nblk = M // tm
    assert nblk % 2 == 0

    def body(a_ref, w_ref, x_ref, g_ref, dres_ref, *rest):
        dg_ref, acc0, acc1 = rest[-3:]
        dx_ref = rest[-5] if bf16_copy else rest[-4]
        dxb_ref = rest[-4] if bf16_copy else None
        i = pl.program_id(0)

        def matmul(acc):
            if parts == 1:
                acc[...] = jnp.dot(a_ref[...], w_ref[...], preferred_element_type=F32)
            else:
                d = jnp.dot(a_ref[0], w_ref[0:Kp, :], preferred_element_type=F32)
                for p in range(1, parts):
                    d = d + jnp.dot(a_ref[p], w_ref[p * Kp:(p + 1) * Kp, :], preferred_element_type=F32)
                acc[...] = d

        def finish(acc):
            for r0 in range(0, tm, EPI_ROWS // 2):
                rows = slice(r0, r0 + EPI_ROWS // 2)
                v = acc[rows, :]
                xf = x_ref[rows, :]
                r = lax.rsqrt(jnp.mean(xf * xf, axis=-1, keepdims=True) + EPS)
                xhat = xf * r
                dg_ref[...] += jnp.sum(v * xhat, axis=0, keepdims=True)
                dxh = v * g_ref[...]
                tot = dres_ref[rows, :] + r * (dxh - xhat * jnp.mean(dxh * xhat, axis=-1, keepdims=True))
                dx_ref[rows, :] = tot
                if bf16_copy:
                    dxb_ref[rows, :] = tot.astype(BF16)

        @pl.when(i == 0)
        def _():
            dg_ref[...] = jnp.zeros_like(dg_ref)
            matmul(acc0)

        @pl.when((i > 0) & (i < nblk) & (i % 2 == 1))
        def _():
            matmul(acc1)
            finish(acc0)

        @pl.when((i > 0) & (i < nblk) & (i % 2 == 0))
        def _():
            matmul(acc0)
            finish(acc1)

        @pl.when(i == nblk)
        def _():
            finish(acc1)

    last = nblk - 1
    row = pl.BlockSpec((tm, N), lambda i: (jnp.maximum(i - 1, 0), 0))
    vec = pl.BlockSpec((1, N), lambda i: (0, 0))
    if a.ndim == 3:
        a_spec = pl.BlockSpec((parts, tm, Kp), lambda i: (0, jnp.minimum(i, last), 0))
    else:
        a_spec = pl.BlockSpec((tm, Kp), lambda i: (jnp.minimum(i, last), 0))
    w_spec = pl.BlockSpec((parts * Kp, N), lambda i: (0, 0), pipeline_mode=pl.Buffered(1))
    in_specs = [a_spec, w_spec, row, vec, row]
    args = [a, w, x, gain, dres]
    if dep is not None:
        in_specs.append(HBM_SPEC)
        args.append(dep)
    return pl.pallas_call(
        body, name=name, grid=(nblk + 1,), in_specs=in_specs,
        out_specs=[row, row, vec] if bf16_copy else [row, vec],
        out_shape=([_sds((M, N), F32), _sds((M, N), BF16), _sds((1, N), F32)] if bf16_copy
                   else [_sds((M, N), F32), _sds((1, N), F32)]),
        scratch_shapes=[pltpu.VMEM((tm, N), F32), pltpu.VMEM((tm, N), F32)], compiler_params=_cparams(1, 56))(*args)


def _mm_tn(a, b, name, out_dtype=BF16, tt=2048, dep=None):
    parts = a.shape[0] if a.ndim == 3 else 1
    Tt, Mp = a.shape[-2], a.shape[-1]
    N = b.shape[1]
    tn = _pick(Mp, (1408, 1280, 1024, 512))
    jper = Mp // tn
    nt = Tt // tt

    def body(a_ref, b_ref, *rest):
        o_ref, acc = rest[-2:]
        t = pl.program_id(1)

        @pl.when(t == 0)
        def _():
            acc[...] = jnp.zeros_like(acc)

        rows = pl.ds(pl.multiple_of(t * tt, tt), tt)
        acc[...] += lax.dot_general(a_ref[...], b_ref[rows, :], (((0,), (0,)), ((), ())),
                                    preferred_element_type=F32)

        @pl.when(t == nt - 1)
        def _():
            o_ref[...] = acc[...].astype(o_ref.dtype)

    if a.ndim == 3:
        a_spec = pl.BlockSpec((None, tt, tn), lambda j, t: (j // jper, t, j % jper))
    else:
        a_spec = pl.BlockSpec((tt, tn), lambda j, t: (t, j))
    in_specs = [a_spec, pl.BlockSpec((Tt, N), lambda j, t: (0, 0), pipeline_mode=pl.Buffered(1))]
    args = [a, b]
    if dep is not None:
        in_specs.append(HBM_SPEC)
        args.append(dep)
    return pl.pallas_call(
        body, name=name, grid=(parts * jper, nt), in_specs=in_specs,
        out_specs=pl.BlockSpec((tn, N), lambda j, t: (j, 0)),
        out_shape=_sds((parts * Mp, N), out_dtype), scratch_shapes=[pltpu.VMEM((tn, N), F32)],
        compiler_params=_cparams(2))(*args)


FFN_ROWS = 2048


def _ffn_in(h, wt_in, name, tn=256):
    nj = DFF // tn

    def body(h_ref, wg_ref, wu_ref, p_ref, act_ref):
        nt = (((1,), (1,)), ((), ()))
        for r0 in range(0, T, FFN_ROWS):
            rows = slice(r0, r0 + FFN_ROWS)
            g = lax.dot_general(h_ref[rows, :], wg_ref[...], nt, preferred_element_type=F32).astype(BF16)
            u = lax.dot_general(h_ref[rows, :], wu_ref[...], nt, preferred_element_type=F32).astype(BF16)
            p_ref[0, rows, :] = g
            p_ref[1, rows, :] = u
            act_ref[rows, :] = g * _sigmoid_bf16(g) * u

    return pl.pallas_call(
        body, name=name, grid=(nj,),
        in_specs=[pl.BlockSpec((T, D), lambda j: (0, 0), pipeline_mode=pl.Buffered(1)),
                  pl.BlockSpec((tn, D), lambda j: (j, 0)), pl.BlockSpec((tn, D), lambda j: (j + nj, 0))],
        out_specs=[pl.BlockSpec((2, T, tn), lambda j: (0, 0, j)), pl.BlockSpec((T, tn), lambda j: (0, j))],
        out_shape=[_sds((2, T, DFF), BF16), _sds((T, DFF), BF16)], compiler_params=_cparams(1))(h, wt_in, wt_in)


def _ffn_dact(dyb, w_down, p3, name, tn=256, dep=None):
    def body(dy_ref, w_ref, p_ref, *rest):
        o_ref = rest[-1]
        for r0 in range(0, T, FFN_ROWS):
            rows = slice(r0, r0 + FFN_ROWS)
            da = lax.dot_general(dy_ref[rows, :], w_ref[...], (((1,), (1,)), ((), ())),
                                 preferred_element_type=F32).astype(BF16)
            g = p_ref[0, rows, :]
            u = p_ref[1, rows, :]
            sg = _sigmoid_bf16(g)
            gs = g * sg
            o_ref[0, rows, :] = (da * u) * (sg + gs * (1.0 - sg))
            o_ref[1, rows, :] = da * gs

    pspec = pl.BlockSpec((2, T, tn), lambda j: (0, 0, j))
    in_specs = [pl.BlockSpec((T, D), lambda j: (0, 0), pipeline_mode=pl.Buffered(1)),
                pl.BlockSpec((tn, D), lambda j: (j, 0)), pspec]
    args = [dyb, w_down, p3]
    if dep is not None:
        in_specs.append(HBM_SPEC)
        args.append(dep)
    return pl.pallas_call(
        body, name=name, grid=(DFF // tn,), in_specs=in_specs, out_specs=pspec,
        out_shape=_sds((2, T, DFF), BF16), compiler_params=_cparams(1))(*args)


def _rms_fwd(x, g, name, tm=512, dep=None):
    def body(x_ref, g_ref, *rest):
        h_ref = rest[-1]
        xf = x_ref[...]
        r = lax.rsqrt(jnp.mean(xf * xf, axis=-1, keepdims=True) + EPS)
        h_ref[...] = (xf * r * g_ref[...]).astype(BF16)

    in_specs = [pl.BlockSpec((tm, D), lambda i: (i, 0)), pl.BlockSpec((1, D), lambda i: (0, 0))]
    args = [x, g]
    if dep is not None:
        in_specs.append(HBM_SPEC)
        args.append(dep)
    return pl.pallas_call(
        body, name=name, grid=(T // tm,), in_specs=in_specs, out_specs=pl.BlockSpec((tm, D), lambda i: (i, 0)),
        out_shape=_sds((T, D), BF16), compiler_params=_cparams(1))(*args)


def _tril_mask():
    r = lax.broadcasted_iota(jnp.int32, (128, 128), 0)
    c = lax.broadcasted_iota(jnp.int32, (128, 128), 1)
    return r >= c


def _mix_a_fwd(pab, sgu_g, sgu_b, sgu_w, sgu_bias3, tm=512):
    def body(zu_ref, zv_ref, g_ref, b_ref, w_ref, bias_ref, o_ref):
        u = _gelu(zu_ref[...].astype(F32))
        v = _gelu(zv_ref[...].astype(F32))
        mu = jnp.mean(v, axis=-1, keepdims=True)
        vc = v - mu
        rstd = lax.rsqrt(jnp.mean(vc * vc, axis=-1, keepdims=True) + EPS)
        vn = (vc * rstd * g_ref[...] + b_ref[...]).astype(BF16)
        tri = _tril_mask()
        for gi in range(4):
            wg = jnp.where(tri, w_ref[gi], 0.0).astype(BF16)
            bg = bias_ref[gi]
            for c in range(tm // 128):
                rs, cs = slice(c * 128, (c + 1) * 128), slice(gi * 128, (gi + 1) * 128)
                mixed = jnp.dot(wg, vn[rs, cs], preferred_element_type=F32) + bg
                o_ref[rs, cs] = (u[rs, cs] * mixed).astype(BF16)

    half = pl.BlockSpec((tm, 512), lambda i: (i, 0))
    return pl.pallas_call(
        body, name="mix_a_fwd", grid=(T // tm,),
        in_specs=[half, pl.BlockSpec((tm, 512), lambda i: (i, 1)),
                  pl.BlockSpec((1, 512), lambda i: (0, 0)), pl.BlockSpec((1, 512), lambda i: (0, 0)),
                  pl.BlockSpec((4, 128, 128), lambda i: (0, 0, 0)), pl.BlockSpec((4, 128, 1), lambda i: (0, 0, 0))],
        out_specs=half, out_shape=_sds((T, D), BF16), compiler_params=_cparams(1),
    )(pab, pab, sgu_g, sgu_b, sgu_w, sgu_bias3)


def _rope_tables():
    pos = np.arange(T, dtype=np.float32)
    inv_freq = np.float32(ROPE_THETA) ** (-np.arange(ROPE_HALF, dtype=np.float32) * np.float32(2.0 / (2 * ROPE_HALF)))
    ang = (pos[:, None] * inv_freq[None, :]).astype(np.float32)
    cos, sin = np.cos(ang), np.sin(ang)
    z8 = np.zeros((T, ROPE_HALF), np.float32)
    rest = np.zeros((T, HEAD - 2 * ROPE_HALF), np.float32)
    c64 = np.concatenate([cos, cos, rest + 1.0], axis=1)
    s1 = np.concatenate([z8, sin, rest], axis=1)
    s2 = np.concatenate([-sin, z8, rest], axis=1)
    return tuple(jnp.asarray(np.tile(t, (1, 2)).astype(np.float32)) for t in (c64, s1, s2))


def _lo_mask(shape):
    return lax.broadcasted_iota(jnp.int32, shape, 1) < HEAD


def _seg_mean(x, lo):
    s_all = jnp.sum(x, axis=-1, keepdims=True)
    s_lo = jnp.sum(jnp.where(lo, x, 0.0), axis=-1, keepdims=True)
    return jnp.where(lo, s_lo, s_all - s_lo) * (1.0 / HEAD)


def _head_blocks():
    r = lax.broadcasted_iota(jnp.int32, (PAIR, PAIR), 0) < HEAD
    c = lax.broadcasted_iota(jnp.int32, (PAIR, PAIR), 1) < HEAD
    return jnp.where(r == c, 1.0, 0.0).astype(BF16)


def _seg_mean_mxu(x, blocks):
    return jnp.dot(x.astype(BF16), blocks, preferred_element_type=F32) * (1.0 / HEAD)


def _rope(n, c, s1, s2):
    return n * c + pltpu.roll(n, ROPE_HALF, 1) * s1 + pltpu.roll(n, PAIR - ROPE_HALF, 1) * s2


def _rope_t(dy, c, s1, s2):
    return dy * c - pltpu.roll(dy, PAIR - ROPE_HALF, 1) * s2 - pltpu.roll(dy, ROPE_HALF, 1) * s1


def _prep_fwd(pab, qg, kg, tabs, tm=512):
    def body(p_ref, qg_ref, kg_ref, c_ref, s1_ref, s2_ref, *outs):
        blocks = _head_blocks()
        c, s1, s2 = c_ref[...], s1_ref[...], s2_ref[...]
        for g in range(3):
            qn_ref, kn_ref, v_ref = outs[3 * g:3 * g + 3]
            for p in range(NPAIR):
                for which, gains, dst in ((0, qg_ref, qn_ref), (1, kg_ref, kn_ref)):
                    col = (2 + 3 * which + g) * 512 + p * PAIR
                    xr = p_ref[:, col:col + PAIR].astype(F32)
                    rinv = lax.rsqrt(_seg_mean_mxu(xr * xr, blocks) + EPS)
                    outs[9 + 2 * g + which][p] = rinv.astype(BF16)
                    dst[p] = _rope(xr * rinv * gains[g:g + 1, :], c, s1, s2)
                col = (8 + g) * 512 + p * PAIR
                v_ref[p] = p_ref[:, col:col + PAIR].astype(F32)

    pm = pl.BlockSpec((NPAIR, tm, PAIR), lambda i: (0, i, 0))
    tab = pl.BlockSpec((tm, PAIR), lambda i: (i, 0))
    gain = pl.BlockSpec((3, PAIR), lambda i: (0, 0))
    res = pl.pallas_call(
        body, name="prep_fwd", grid=(T // tm,),
        in_specs=[pl.BlockSpec((tm, AB_IN), lambda i: (i, 0)), gain, gain, tab, tab, tab],
        out_specs=[pm] * 15, out_shape=[_sds((NPAIR, T, PAIR), F32)] * 9 + [_sds((NPAIR, T, PAIR), BF16)] * 6,
        compiler_params=_cparams(1))(pab, qg, kg, *tabs)
    return res[0:9], res[9:15]


def _res_index(it, rate):
    window = NBACK * rate
    b = it // rate
    rho = it % rate
    start = b * window + rho
    startp = jnp.maximum(start - window, rho)
    kmin = jnp.where(b > 0, 0, NBACK)
    return start, startp, kmin


def _rows(start, rate):
    if rate == 1:
        return pl.ds(pl.multiple_of(start, NBACK), NBACK)
    return pl.ds(start, NBACK, stride=rate)


def _band_bias():
    qs = lax.broadcasted_iota(jnp.int32, (2 * NBACK, 2 * NBACK), 0)
    kj = lax.broadcasted_iota(jnp.int32, (2 * NBACK, 2 * NBACK), 1)
    dist = (qs & (NBACK - 1)) + NBACK - kj
    both = (dist >= 0) & (dist <= NBACK)
    return jnp.where(both, 0.0, NEG_INF), jnp.where(both & (kj >= NBACK), 0.0, NEG_INF)


def _attn_fwd(qn, kn, v, rate, name, dep=None):
    def body(q_ref, k_ref, v_ref, *rest):
        o_ref, l_ref = rest[-2:]
        lo = _lo_mask((NBACK, PAIR))
        bias_all, bias_first = _band_bias()

        def step(it, carry):
            start, startp, kmin = _res_index(it, rate)
            q = q_ref[_rows(start, rate), :] * (HEAD ** -0.5)
            kcat = jnp.concatenate([k_ref[_rows(startp, rate), :], k_ref[_rows(start, rate), :]], axis=0).astype(BF16)
            vcat = jnp.concatenate([v_ref[_rows(startp, rate), :], v_ref[_rows(start, rate), :]], axis=0).astype(BF16)
            vcat1 = jnp.concatenate([vcat, jnp.ones((2 * NBACK, PAIR), BF16)], axis=1)
            q2 = jnp.concatenate([jnp.where(lo, q, 0.0), jnp.where(lo, 0.0, q)], axis=0).astype(BF16)
            s = lax.dot_general(q2, kcat, (((1,), (1,)), ((), ())), preferred_element_type=F32)
            s = s + jnp.where(kmin == 0, bias_all, bias_first)
            m = jnp.max(s, axis=-1, keepdims=True)
            ol = jnp.dot(jnp.exp(s - m).astype(BF16), vcat1, preferred_element_type=F32)
            o2 = ol[:, 0:PAIR] / ol[:, PAIR:]
            ls = m + jnp.log(ol[:, PAIR:])
            o_ref[_rows(start, rate), :] = jnp.where(lo, o2[0:NBACK], o2[NBACK:])
            l_ref[_rows(start, rate), :] = jnp.where(lo, ls[0:NBACK], ls[NBACK:])
            return carry

        lax.fori_loop(0, T // NBACK, step, 0, unroll=4)

    pm = pl.BlockSpec((None, T, PAIR), lambda p: (p, 0, 0))
    in_specs, args = [pm, pm, pm], [qn, kn, v]
    if dep is not None:
        in_specs.append(HBM_SPEC)
        args.append(dep)
    return pl.pallas_call(
        body, name=name, grid=(NPAIR,), in_specs=in_specs, out_specs=[pm, pm],
        out_shape=[_sds((NPAIR, T, PAIR), F32)] * 2, compiler_params=_cparams(1))(*args)


def _merge_fwd(cat_ab, outs, lses, tm=512):
    def body(cat_in, o0, o1, o2, l0, l1, l2, cat_ref, lse_ref):
        del cat_in
        for p in range(NPAIR):
            a0, a1, a2 = l0[p], l1[p], l2[p]
            m = jnp.maximum(jnp.maximum(a0, a1), a2)
            w0, w1, w2 = jnp.exp(a0 - m), jnp.exp(a1 - m), jnp.exp(a2 - m)
            s = w0 + w1 + w2
            b = (w0 * o0[p] + w1 * o1[p] + w2 * o2[p]) / s
            cat_ref[:, p * PAIR:(p + 1) * PAIR] = b.astype(BF16)
            lse_ref[p] = m + jnp.log(s)

    pm = pl.BlockSpec((NPAIR, tm, PAIR), lambda i: (0, i, 0))
    return pl.pallas_call(
        body, name="merge_fwd", grid=(T // tm,),
        in_specs=[pl.BlockSpec(memory_space=pl.ANY)] + [pm] * 6,
        out_specs=[pl.BlockSpec((tm, 512), lambda i: (i, 1)), pm],
        out_shape=[_sds((T, D), BF16), _sds((NPAIR, T, PAIR), F32)],
        input_output_aliases={0: 0}, compiler_params=_cparams(1))(cat_ab, *outs, *lses)


def _d_cat_ab(dxb, w_ab_out, cat, dep, tm=512):
    def epilogue(d, rows, side, outs):
        (b_ref,), (da_ref, dbp_ref, e_ref) = side, outs
        da_ref[rows, :] = d[:, 0:512].astype(BF16)
        lo = _lo_mask((EPI_ROWS, PAIR))
        for p in range(NPAIR):
            db = d[:, 512 + p * PAIR:512 + (p + 1) * PAIR]
            b = b_ref[rows, p * PAIR:(p + 1) * PAIR].astype(F32)
            dbp_ref[p, rows, :] = db
            e_ref[p, rows, :] = _seg_mean(db * b, lo) * float(HEAD)

    pm = pl.BlockSpec((NPAIR, tm, PAIR), lambda i: (0, i, 0))
    return _mm_nt_rows(
        dxb, w_ab_out, "mm_d_cat_ab", epilogue, [cat], [pl.BlockSpec((tm, 512), lambda i: (i, 1))],
        [pl.BlockSpec((tm, 512), lambda i: (i, 0)), pm, pm],
        [_sds((T, 512), BF16), _sds((NPAIR, T, PAIR), F32), _sds((NPAIR, T, PAIR), F32)], dep=dep, tm=tm)


def _attn_bwd(qn, kn, v, dbp, e, lse, rate, name):
    def body(q_ref, k_ref, v_ref, db_ref, e_ref, lse_ref, dq_ref, dk_ref, dv_ref):
        lo = _lo_mask((NBACK, PAIR))
        bias_all, bias_first = _band_bias()
        scale = HEAD ** -0.5
        nt = (((1,), (1,)), ((), ()))
        tn = (((0,), (0,)), ((), ()))
        window = NBACK * rate
        nblk = T // window

        def one(it, carry):
            dk_carry, dv_carry = carry
            rho = it // nblk
            b = it % nblk
            start = b * window + rho
            rq = _rows(start, rate)
            rp = _rows(jnp.maximum(start - window, rho), rate)
            q = q_ref[rq, :] * scale
            db = db_ref[rq, :]
            ev = e_ref[rq, :]
            ls = lse_ref[rq, :]
            kcat = jnp.concatenate([k_ref[rp, :], k_ref[rq, :]], axis=0).astype(BF16)
            vcat = jnp.concatenate([v_ref[rp, :], v_ref[rq, :]], axis=0).astype(BF16)
            q2 = jnp.concatenate([jnp.where(lo, q, 0.0), jnp.where(lo, 0.0, q)], axis=0).astype(BF16)
            db2 = jnp.concatenate([jnp.where(lo, db, 0.0), jnp.where(lo, 0.0, db)], axis=0).astype(BF16)
            ls2 = jnp.concatenate([ls[:, 0:1], ls[:, HEAD:HEAD + 1]], axis=0)
            ev2 = jnp.concatenate([ev[:, 0:1], ev[:, HEAD:HEAD + 1]], axis=0)
            s = lax.dot_general(q2, kcat, nt, preferred_element_type=F32)
            pt = jnp.exp(s + jnp.where(b > 0, bias_all, bias_first) - ls2)
            dp = lax.dot_general(db2, vcat, nt, preferred_element_type=F32)
            ds = (pt * (dp - ev2)).astype(BF16)
            dq2 = jnp.dot(ds, kcat, preferred_element_type=F32) * scale
            dkc = lax.dot_general(ds, q2, tn, preferred_element_type=F32)
            dvc = lax.dot_general(pt.astype(BF16), db2, tn, preferred_element_type=F32)
            dq_ref[rq, :] = jnp.where(lo, dq2[0:NBACK], dq2[NBACK:])
            dk_ref[rp, :] = dk_carry + dkc[0:NBACK]
            dk_ref[rq, :] = dkc[NBACK:]
            dv_ref[rp, :] = dv_carry + dvc[0:NBACK]
            dv_ref[rq, :] = dvc[NBACK:]
            return dkc[NBACK:], dvc[NBACK:]

        def step(i, carry):
            for u in range(ATTN_BWD_UNROLL):
                carry = one(i * ATTN_BWD_UNROLL + u, carry)
            return carry

        zero = jnp.zeros((NBACK, PAIR), F32)
        lax.fori_loop(0, T // NBACK // ATTN_BWD_UNROLL, step, (zero, zero))

    pm = pl.BlockSpec((None, T, PAIR), lambda p: (p, 0, 0))
    return pl.pallas_call(
        body, name=name, grid=(NPAIR,), in_specs=[pm] * 6, out_specs=[pm] * 3,
        out_shape=[_sds((NPAIR, T, PAIR), F32)] * 3, compiler_params=_cparams(1, 56))(qn, kn, v, dbp, e, lse)


def _ab_in_bwd(pab, dcat, sgu_g, sgu_b, sgu_w, sgu_bias3, qg, kg, tabs, dqkv, rinvs, tm=256):
    def body(p_ref, dcat_ref, g_ref, b_ref, w_ref, bias_ref, qg_ref, kg_ref, c_ref, s1_ref, s2_ref, *rest):
        dq_refs, rinv_refs = rest[0:9], rest[9:15]
        o_ref, dwm_ref, dbias_ref, dsg_ref, dsb_ref, dgain_ref = rest[15:]
        i = pl.program_id(0)

        @pl.when(i == 0)
        def _():
            dwm_ref[...] = jnp.zeros_like(dwm_ref)
            dbias_ref[...] = jnp.zeros_like(dbias_ref)
            dsg_ref[...] = jnp.zeros_like(dsg_ref)
            dsb_ref[...] = jnp.zeros_like(dsb_ref)
            dgain_ref[...] = jnp.zeros_like(dgain_ref)

        zu = p_ref[:, 0:512].astype(F32)
        zv = p_ref[:, 512:1024].astype(F32)
        u = _gelu(zu)
        v = _gelu(zv)
        mu = jnp.mean(v, axis=-1, keepdims=True)
        vc = v - mu
        rstd = lax.rsqrt(jnp.mean(vc * vc, axis=-1, keepdims=True) + EPS)
        xhat = vc * rstd
        vn = (xhat * g_ref[...] + b_ref[...]).astype(BF16)
        da = dcat_ref[...].astype(F32)
        tri = _tril_mask()
        du_parts = [[None] * 4 for _ in range(tm // 128)]
        dvn_parts = [[None] * 4 for _ in range(tm // 128)]
        for gi in range(4):
            wg = jnp.where(tri, w_ref[gi], 0.0).astype(BF16)
            bg = bias_ref[gi]
            for c in range(tm // 128):
                rs, cs = slice(c * 128, (c + 1) * 128), slice(gi * 128, (gi + 1) * 128)
                vblk = vn[rs, cs]
                mixed = jnp.dot(wg, vblk, preferred_element_type=F32) + bg
                dab = da[rs, cs]
                du_parts[c][gi] = dab * mixed
                dmixed = dab * u[rs, cs]
                dmb = dmixed.astype(BF16)
                dvn_parts[c][gi] = lax.dot_general(wg, dmb, (((0,), (0,)), ((), ())), preferred_element_type=F32)
                dwm = lax.dot_general(dmb, vblk, (((1,), (1,)), ((), ())), preferred_element_type=F32)
                dwm_ref[gi] += jnp.where(tri, dwm, 0.0)
                dbias_ref[gi] += dmixed
        du = jnp.concatenate([jnp.concatenate(r, axis=1) for r in du_parts], axis=0)
        dvn = jnp.concatenate([jnp.concatenate(r, axis=1) for r in dvn_parts], axis=0)
        dsg_ref[...] += jnp.sum(dvn * xhat, axis=0, keepdims=True)
        dsb_ref[...] += jnp.sum(dvn, axis=0, keepdims=True)
        dxh = dvn * g_ref[...]
        dv = rstd * (dxh - jnp.mean(dxh, axis=-1, keepdims=True)
                     - xhat * jnp.mean(dxh * xhat, axis=-1, keepdims=True))
        o_ref[:, 0:512] = (du * _gelu_grad(zu)).astype(BF16)
        o_ref[:, 512:1024] = (dv * _gelu_grad(zv)).astype(BF16)

        blocks = _head_blocks()
        c, s1, s2 = c_ref[...], s1_ref[...], s2_ref[...]
        for g in range(3):
            dq_ref, dk_ref, dv_ref = dq_refs[3 * g:3 * g + 3]
            for p in range(NPAIR):
                for which, gains, src in ((0, qg_ref, dq_ref), (1, kg_ref, dk_ref)):
                    col = (2 + 3 * which + g) * 512 + p * PAIR
                    xr = p_ref[:, col:col + PAIR].astype(F32)
                    rinv = rinv_refs[2 * g + which][p].astype(F32)
                    xh = xr * rinv
                    dn = _rope_t(src[p], c, s1, s2)
                    row = 2 * g + which
                    dgain_ref[row:row + 1, :] += jnp.sum(dn * xh, axis=0, keepdims=True)
                    dxh2 = dn * gains[g:g + 1, :]
                    dx = rinv * (dxh2 - xh * _seg_mean_mxu(dxh2 * xh, blocks))
                    o_ref[:, col:col + PAIR] = dx.astype(BF16)
                col = (8 + g) * 512 + p * PAIR
                o_ref[:, col:col + PAIR] = dv_ref[p].astype(BF16)

    pm = pl.BlockSpec((NPAIR, tm, PAIR), lambda i: (0, i, 0))
    tab = pl.BlockSpec((tm, PAIR), lambda i: (i, 0))
    gain = pl.BlockSpec((3, PAIR), lambda i: (0, 0))
    vec = pl.BlockSpec((1, 512), lambda i: (0, 0))
    full = pl.BlockSpec((tm, AB_IN), lambda i: (i, 0))
    w4 = pl.BlockSpec((4, 128, 128), lambda i: (0, 0, 0))
    return pl.pallas_call(
        body, name="ab_in_bwd", grid=(T // tm,),
        in_specs=[full, pl.BlockSpec((tm, 512), lambda i: (i, 0)), vec, vec, w4,
                  pl.BlockSpec((4, 128, 1), lambda i: (0, 0, 0)), gain, gain, tab, tab, tab] + [pm] * 15,
        out_specs=[full, w4, w4, vec, vec, pl.BlockSpec((8, PAIR), lambda i: (0, 0))],
        out_shape=[_sds((T, AB_IN), BF16), _sds((4, 128, 128), F32), _sds((4, 128, 128), F32),
                   _sds((1, 512), F32), _sds((1, 512), F32), _sds((8, PAIR), F32)],
        compiler_params=_cparams(1))(pab, dcat, sgu_g, sgu_b, sgu_w, sgu_bias3, qg, kg, *tabs, *dqkv, *rinvs)


def _ln_stats(x):
    mu = jnp.mean(x, axis=-1, keepdims=True)
    xc = x - mu
    rstd = lax.rsqrt(jnp.mean(xc * xc, axis=-1, keepdims=True) + EPS)
    return xc * rstd, rstd


CONV_RC = 64


def _shifted_copies(src, dst, tm):
    dst[0] = src[...]
    for b in range(1, 8):
        dst[b, 0:tm + HALO - 8, :] = src[pl.ds(b, tm + HALO - 8), :]


def _offsets_by_phase(first):
    groups = {}
    for o in range(first, first + CONV_C_TAPS):
        groups.setdefault(o % 8, []).append(o)
    return sorted(groups.items())


def _window(shifted, b8, base, offsets, lanes):
    rows = 8 * (max(offsets) // 8) + CONV_RC
    return shifted[b8, pl.ds(base, rows), lanes].reshape(rows // 8, 8, 128)


def _cd_fwd(pcd, cw, cb, lg, lb, dw, tm=512):
    per = tm // HALO

    def body(p_ref, h_ref, cw_ref, cb_ref, lg_ref, lb_ref, dw_ref, cat_ref, c0_ref, c1_ref, dd_ref, y_ref,
             buf, buf2, sb):
        i = pl.program_id(0)
        live = jnp.where(i > 0, 1.0, 0.0)
        a = p_ref[:, 0:512].astype(F32)
        gt = p_ref[:, 512:1024].astype(F32)
        gb = p_ref[:, 1024:1536].astype(F32)
        gc = p_ref[:, 1536:2048].astype(F32)
        hv = p_ref[:, 2048:2560].astype(F32)
        c0 = a * _sigmoid(gt)
        dd = gc * hv
        buf[0:HALO, :] = h_ref[:, 0:512].astype(F32) * _sigmoid(h_ref[:, 512:1024].astype(F32)) * live
        buf[HALO:, :] = c0
        buf2[0:HALO, :] = h_ref[:, 1536:2048].astype(F32) * h_ref[:, 2048:2560].astype(F32) * live
        buf2[HALO:, :] = dd
        c0_ref[...] = c0.astype(BF16)
        dd_ref[...] = dd.astype(BF16)
        _shifted_copies(buf, sb, tm)

        def conv_rows(r, carry):
            base = pl.multiple_of(r * CONV_RC, CONV_RC)
            for c in range(4):
                lanes = slice(c * 128, (c + 1) * 128)
                acc = jnp.broadcast_to(cb_ref[:, lanes], (CONV_RC // 8, 8, 128))
                for b8, offsets in _offsets_by_phase(HALO - (CONV_C_TAPS - 1)):
                    win = _window(sb, b8, base, offsets, lanes)
                    for o in offsets:
                        j = o - (HALO - (CONV_C_TAPS - 1))
                        acc = acc + cw_ref[8 * j:8 * j + 8, lanes] * win[o // 8:o // 8 + CONV_RC // 8]
                c1_ref[pl.ds(base, CONV_RC), lanes] = acc.reshape(CONV_RC, 128)
            return carry

        lax.fori_loop(0, tm // CONV_RC, conv_rows, 0)
        xhat, _ = _ln_stats(c1_ref[...])
        c2 = xhat * lg_ref[...] + lb_ref[...]
        y = jnp.zeros((tm, 512), F32)
        for j in range(CONV_D_TAPS):
            y = y + dw_ref[j:j + 1, :] * buf2[pl.ds(HALO - (CONV_D_TAPS - 1) + j, tm), :]
        cat_ref[:, 0:512] = (c2 * _sigmoid(c2)).astype(BF16)
        cat_ref[:, 512:1024] = (gb * y).astype(BF16)
        y_ref[...] = y.astype(BF16)

    half = pl.BlockSpec((tm, 512), lambda i: (i, 0))
    vec = pl.BlockSpec((1, 512), lambda i: (0, 0))
    return pl.pallas_call(
        body, name="cd_fwd", grid=(T // tm,),
        in_specs=[pl.BlockSpec((tm, CD_IN), lambda i: (i, 0)),
                  pl.BlockSpec((HALO, CD_IN), lambda i: (jnp.maximum(i * per - 1, 0), 0)),
                  pl.BlockSpec((8 * 32, 512), lambda i: (0, 0)), vec, vec, vec, pl.BlockSpec((8, 512), lambda i: (0, 0))],
        out_specs=[pl.BlockSpec((tm, D), lambda i: (i, 0)), half, half, half, half],
        out_shape=[_sds((T, D), BF16), _sds((T, 512), BF16), _sds((T, 512), F32), _sds((T, 512), BF16),
                   _sds((T, 512), BF16)],
        scratch_shapes=[pltpu.VMEM((HALO + tm, 512), F32), pltpu.VMEM((HALO + tm, 512), F32),
                        pltpu.VMEM((8, HALO + tm, 512), F32)],
        compiler_params=_cparams(1))(pcd, pcd, cw, cb, lg, lb, dw)


def _d_cat_cd(dxb, w_cd_out, c1, pcd, y, lg, lb, dep, tm=512):
    def epilogue(d, rows, side, outs):
        c1_ref, gb_ref, y_ref, lg_ref, lb_ref = side
        dc1_ref, dy3_ref, dgb_ref, dlg_ref, dlb_ref, dcb_ref = outs
        dc, ddo = d[:, 0:512], d[:, 512:1024]
        xhat, rstd = _ln_stats(c1_ref[rows, :])
        c2 = xhat * lg_ref[...] + lb_ref[...]
        sg = _sigmoid(c2)
        dc2 = dc * sg * (1.0 + c2 * (1.0 - sg))
        dlg_ref[...] += jnp.sum(dc2 * xhat, axis=0, keepdims=True)
        dlb_ref[...] += jnp.sum(dc2, axis=0, keepdims=True)
        dxh = dc2 * lg_ref[...]
        dc1 = rstd * (dxh - jnp.mean(dxh, axis=-1, keepdims=True)
                      - xhat * jnp.mean(dxh * xhat, axis=-1, keepdims=True))
        dcb_ref[...] += jnp.sum(dc1, axis=0, keepdims=True)
        dc1_ref[rows, :] = dc1
        dgb_ref[rows, :] = (ddo * y_ref[rows, :].astype(F32)).astype(BF16)
        dy3_ref[rows, :] = ddo * gb_ref[rows, :].astype(F32)

    half = pl.BlockSpec((tm, 512), lambda i: (i, 0))
    vec = pl.BlockSpec((1, 512), lambda i: (0, 0))
    return _mm_nt_rows(
        dxb, w_cd_out, "mm_d_cat_cd", epilogue, [c1, pcd, y, lg, lb],
        [half, pl.BlockSpec((tm, 512), lambda i: (i, 2)), half, vec, vec], [half, half, half, vec, vec, vec],
        [_sds((T, 512), F32), _sds((T, 512), F32), _sds((T, 512), BF16),
         _sds((1, 512), F32), _sds((1, 512), F32), _sds((1, 512), F32)], sums=(3, 4, 5), dep=dep, tm=tm)


def _cd_bwd_conv(pcd, dc1, dy3, c0, dd, dgb, cw8, dw, tm=256):
    per = tm // HALO
    nblk = T // tm
    last32 = T // HALO - 1

    def body(p_ref, dc1_ref, dc1n_ref, dy3_ref, dy3n_ref, c0_ref, dd_ref, dgb_ref, cw_ref, dw_ref,
             o_ref, dcw_ref, ddw_ref, dbuf, d3buf, sd, dc0_buf):
        i = pl.program_id(0)
        has_next = jnp.where(i < nblk - 1, 1.0, 0.0)

        @pl.when(i == 0)
        def _():
            dcw_ref[...] = jnp.zeros_like(dcw_ref)
            ddw_ref[...] = jnp.zeros_like(ddw_ref)

        dbuf[0:tm, :] = dc1_ref[...]
        dbuf[tm:, :] = dc1n_ref[...] * has_next
        d3buf[0:tm, :] = dy3_ref[...]
        d3buf[tm:, :] = dy3n_ref[...] * has_next
        _shifted_copies(dbuf, sd, tm)
        n_tiles = tm // CONV_RC

        phases = _offsets_by_phase(0)

        def dc0_rows(r, carry):
            base = pl.multiple_of(r * CONV_RC, CONV_RC)
            for c in range(4):
                lanes = slice(c * 128, (c + 1) * 128)
                acc = jnp.zeros((CONV_RC // 8, 8, 128), F32)
                for b8, offsets in phases:
                    win = _window(sd, b8, base, offsets, lanes)
                    for o in offsets:
                        j = CONV_C_TAPS - 1 - o
                        acc = acc + cw_ref[8 * j:8 * j + 8, lanes] * win[o // 8:o // 8 + CONV_RC // 8]
                dc0_buf[pl.ds(base, CONV_RC), lanes] = acc.reshape(CONV_RC, 128)
            return carry

        lax.fori_loop(0, n_tiles, dc0_rows, 0)

        for c in range(4):
            lanes = slice(c * 128, (c + 1) * 128)
            for b8, offsets in phases:
                def dw_rows(r, accs, lanes=lanes, b8=b8, offsets=offsets):
                    base = pl.multiple_of(r * CONV_RC, CONV_RC)
                    xin = c0_ref[pl.ds(base, CONV_RC), lanes].astype(F32).reshape(CONV_RC // 8, 8, 128)
                    win = _window(sd, b8, base, offsets, lanes)
                    return tuple(acc + jnp.sum(xin * win[o // 8:o // 8 + CONV_RC // 8], axis=0)
                                 for acc, o in zip(accs, offsets))

                accs = lax.fori_loop(0, n_tiles, dw_rows, tuple(jnp.zeros((8, 128), F32) for _ in offsets))
                for acc, o in zip(accs, offsets):
                    j = CONV_C_TAPS - 1 - o
                    dcw_ref[j:j + 1, lanes] += jnp.sum(acc, axis=0, keepdims=True)

        dc0 = dc0_buf[...]
        ddin = dd_ref[...].astype(F32)
        ddd = jnp.zeros((tm, 512), F32)
        for j in range(CONV_D_TAPS):
            dy_shift = d3buf[pl.ds(CONV_D_TAPS - 1 - j, tm), :]
            ddd = ddd + dw_ref[j:j + 1, :] * dy_shift
            ddw_ref[j:j + 1, :] += jnp.sum(ddin * dy_shift, axis=0, keepdims=True)

        a = p_ref[:, 0:512].astype(F32)
        gt = p_ref[:, 512:1024].astype(F32)
        gc = p_ref[:, 1536:2048].astype(F32)
        hv = p_ref[:, 2048:2560].astype(F32)
        sg = _sigmoid(gt)
        o_ref[:, 0:512] = (dc0 * sg).astype(BF16)
        o_ref[:, 512:1024] = (dc0 * a * sg * (1.0 - sg)).astype(BF16)
        o_ref[:, 1024:1536] = dgb_ref[...]
        o_ref[:, 1536:2048] = (ddd * hv).astype(BF16)
        o_ref[:, 2048:2560] = (ddd * gc).astype(BF16)

    half = pl.BlockSpec((tm, 512), lambda i: (i, 0))
    nxt = pl.BlockSpec((HALO, 512), lambda i: (jnp.minimum((i + 1) * per, last32), 0))
    full = pl.BlockSpec((tm, CD_IN), lambda i: (i, 0))
    return pl.pallas_call(
        body, name="cd_bwd_conv", grid=(nblk,),
        in_specs=[full, half, nxt, half, nxt, half, half, half,
                  pl.BlockSpec((8 * 32, 512), lambda i: (0, 0)), pl.BlockSpec((8, 512), lambda i: (0, 0))],
        out_specs=[full, pl.BlockSpec((32, 512), lambda i: (0, 0)), pl.BlockSpec((8, 512), lambda i: (0, 0))],
        out_shape=[_sds((T, CD_IN), BF16), _sds((32, 512), F32), _sds((8, 512), F32)],
        scratch_shapes=[pltpu.VMEM((tm + HALO, 512), F32), pltpu.VMEM((tm + HALO, 512), F32),
                        pltpu.VMEM((8, tm + HALO, 512), F32), pltpu.VMEM((tm, 512), F32)],
        compiler_params=_cparams(1))(pcd, dc1, dc1, dy3, dy3, c0, dd, dgb, cw8, dw)


def _local_step(x, tgt, W, fetch=None, on_grad=None):
    W = dict(W)
    if fetch is None:
        fetch = lambda stage, after: {}
    if on_grad is None:
        on_grad = lambda key, arr: None
    tabs = _rope_tables()
    qg = jnp.tile(W["q_norm_g"], (1, 2))
    kg = jnp.tile(W["k_norm_g"], (1, 2))
    bias3 = W["sgu_bias"].reshape(4, 128, 1)
    G = {}

    h0 = _rms_fwd(x, W["ab_norm_g"], "rms_fwd_ab", dep=W.get("dep_first"))
    W.update(fetch("ab_in", h0))
    pab = _mm_nt(h0, W["wt_ab_in"], "mm_ab_in", dep=W.get("dep0"))
    cat_ab = _mix_a_fwd(pab, W["sgu_norm_g"], W["sgu_norm_b"], W["sgu_w"], bias3)
    qkv, rinvs = _prep_fwd(pab, qg, kg, tabs)
    outs, lses = [], []
    for g, rate in enumerate(DIL_RATES):
        o, l = _attn_fwd(qkv[3 * g], qkv[3 * g + 1], qkv[3 * g + 2], rate, f"attn_fwd_{g}", dep=W.get(f"dep_attn{g}"))
        outs.append(o)
        lses.append(l)
        W.update(fetch(f"attn{g}", o))
    cat_ab, lse = _merge_fwd(cat_ab, outs, lses)
    W.update(fetch("ab_out", lse))
    x1, h1 = _mm_nn(cat_ab, W["w_ab_out"], "mm_ab_out", mode="rms", resid=x, gain=W["ffn_norm_g"][0:1])
    pf0, act0 = _ffn_in(h1, W["wt_ffn_in0"], "ffn_in0")
    W.update(fetch("ffn_down0", act0))
    x2, h2 = _mm_nn(act0, W["w_ffn_down0"], "mm_ffn_down0", mode="rms", resid=x1, gain=W["cd_norm_g"],
                    dep=W.get("dep_down0"))
    W.update(fetch("cd_in", h2))
    pcd = _mm_nt(h2, W["wt_cd_in"], "mm_cd_in")
    cw8 = jnp.repeat(W["conv_c_w32"], 8, axis=0)
    cat_cd, c0, c1, dd, yv = _cd_fwd(pcd, cw8, W["conv_c_b"], W["c_ln_g"], W["c_ln_b"], W["conv_d_w8"])
    x3, h3 = _mm_nn(cat_cd, W["w_cd_out"], "mm_cd_out", mode="rms", resid=x2, gain=W["ffn_norm_g"][1:2])
    pf1, act1 = _ffn_in(h3, W["wt_ffn_in1"], "ffn_in1")
    dy, dyb, loss_cols = _mm_nn(act1, W["w_ffn_down1"], "mm_ffn_down1", mode="loss", resid=x3, tgt=tgt)

    def ffn_bwd(xin, h, pf, act, dres, dresb, layer):
        G[f"w_ffn_down{layer}"] = _mm_tn(act, dresb, f"mm_g_ffn_down{layer}")
        dep = on_grad(f"w_ffn_down{layer}", G[f"w_ffn_down{layer}"])
        dpf = _ffn_dact(dresb, W[f"w_ffn_down{layer}"], pf, f"ffn_dact{layer}", dep=dep)
        G[f"wt_ffn_in{layer}"] = _mm_tn(dpf, h, f"mm_g_ffn_in{layer}")
        dep = on_grad(f"wt_ffn_in{layer}", G[f"wt_ffn_in{layer}"])
        dx, dxb, G[f"ffn_norm_g{layer}"] = _mm_dh_rms_bwd(
            dpf, W[f"wt_ffn_in{layer}"], xin, W["ffn_norm_g"][layer:layer + 1], dres, f"mm_d_h_ffn{layer}", dep=dep)
        return dx, dxb

    dx3, dx3b = ffn_bwd(x3, h3, pf1, act1, dy, dyb, 1)

    G["w_cd_out"] = _mm_tn(cat_cd, dx3b, "mm_g_cd_out")
    dep = on_grad("w_cd_out", G["w_cd_out"])
    dc1, dy3, dgb, G["c_ln_g"], G["c_ln_b"], G["conv_c_b"] = _d_cat_cd(
        dx3b, W["w_cd_out"], c1, pcd, yv, W["c_ln_g"], W["c_ln_b"], dep)
    dpcd, G["conv_c_w32"], G["conv_d_w8"] = _cd_bwd_conv(pcd, dc1, dy3, c0, dd, dgb, cw8, W["conv_d_w8"])
    G["wt_cd_in"] = _mm_tn(dpcd, h2, "mm_g_cd_in")
    dep = on_grad("wt_cd_in", G["wt_cd_in"])
    dx2, dx2b, G["cd_norm_g"] = _mm_dh_rms_bwd(dpcd, W["wt_cd_in"], x2, W["cd_norm_g"], dx3, "mm_d_h_cd", dep=dep)

    dx1, dx1b = ffn_bwd(x1, h1, pf0, act0, dx2, dx2b, 0)

    G["w_ab_out"] = _mm_tn(cat_ab, dx1b, "mm_g_ab_out")
    dep = on_grad("w_ab_out", G["w_ab_out"])
    dcat_a, dbp, e = _d_cat_ab(dx1b, W["w_ab_out"], cat_ab, dep)
    dqkv = []
    for g, rate in enumerate(DIL_RATES):
        dqkv += _attn_bwd(qkv[3 * g], qkv[3 * g + 1], qkv[3 * g + 2], dbp, e, lse, rate, f"attn_bwd_{g}")
    dpab, G["sgu_w"], dbias_part, G["sgu_norm_g"], G["sgu_norm_b"], dgain = _ab_in_bwd(
        pab, dcat_a, W["sgu_norm_g"], W["sgu_norm_b"], W["sgu_w"], bias3, qg, kg, tabs, dqkv, rinvs)
    G["sgu_bias"] = jnp.sum(dbias_part, axis=-1)
    dgain = dgain[0:6, 0:HEAD] + dgain[0:6, HEAD:PAIR]
    G["q_norm_g"] = dgain[0::2]
    G["k_norm_g"] = dgain[1::2]
    G["loss_cols"] = loss_cols
    dep = on_grad("small", G)
    G["wt_ab_in"] = _mm_tn(dpab, h0, "mm_g_ab_in", dep=dep)
    dep = on_grad("wt_ab_in", G["wt_ab_in"])
    grad_x, G["ab_norm_g"] = _mm_dh_rms_bwd(dpab, W["wt_ab_in"], x, W["ab_norm_g"], dx1, "mm_d_h_ab", dep=dep,
                                            bf16_copy=False)
    return loss_cols, grad_x, G


def _my_place():
    return lax.axis_index("x"), lax.axis_index("y"), lax.axis_index("c")


def _dev_index(px, py, pc):
    return 4 * px + 2 * py + pc


def _flip(place, k):
    x, y, c = place
    return (1 - x if k & 4 else x, 1 - y if k & 2 else y, 1 - c if k & 1 else c)


def _landing(shape, dtype, own):
    buf = lax.empty(shape, dtype)
    for lead, part in own:
        buf = lax.dynamic_update_slice(buf, part.reshape((1,) * len(lead) + part.shape),
                                       tuple(lead) + (0,) * part.ndim)
    return buf


HBM_ONLY = pl.BlockSpec(memory_space=pltpu.HBM)
SEM_SPEC = pl.BlockSpec(memory_space=pltpu.SEMAPHORE)
IN_FLIGHT = pltpu.CompilerParams(has_side_effects=pltpu.SideEffectType.DATAFLOW_SIDE_EFFECTING)


def _in_hbm(a):
    return pltpu.with_memory_space_constraint(a, pltpu.HBM)


def _exchange_start(name, srcs, lands, items, dep=None):
    ns, nl, ni = len(srcs), len(lands), len(items)

    def body(*refs):
        S, L = refs[0:ns], refs[ns:ns + nl]
        first_out = ns + nl + (0 if dep is None else 1)
        send_sems, recv_sems, token = refs[first_out], refs[first_out + 1], refs[-1]
        me = _my_place()
        mi = _dev_index(*me)
        for i, (src, dst) in enumerate(items):
            for k in range(1, NDEV):
                peer = _flip(me, k)
                pltpu.make_async_remote_copy(
                    src_ref=src(S, _dev_index(*peer)), dst_ref=dst(L, mi), send_sem=send_sems.at[7 * i + k - 1],
                    recv_sem=recv_sems.at[7 * i + k - 1], device_id=peer, device_id_type=MESH).start()
        token[...] = jnp.zeros_like(token)

    thru = [pltpu.HBM(a.shape, a.dtype) for a in list(srcs) + list(lands)]
    args = [_in_hbm(a) for a in srcs] + [_in_hbm(a) for a in lands]
    in_specs = [HBM_ONLY] * (ns + nl)
    if dep is not None:
        args.append(dep)
        in_specs.append(HBM_SPEC)
    outs = pl.pallas_call(
        body, name=name, in_specs=in_specs,
        out_shape=(pltpu.SemaphoreType.DMA((7 * ni,)), pltpu.SemaphoreType.DMA((7 * ni,)), *thru, _sds((8, 128), F32)),
        out_specs=(SEM_SPEC, SEM_SPEC, *[HBM_ONLY] * (ns + nl), pl.BlockSpec(memory_space=pltpu.VMEM)),
        input_output_aliases={j: 2 + j for j in range(ns + nl)}, compiler_params=IN_FLIGHT)(*args)
    return dict(send=outs[0], recv=outs[1], srcs=list(outs[2:2 + ns]), lands=list(outs[2 + ns:2 + ns + nl]),
                token=outs[-1], items=items)


def _exchange_wait(name, states, after):
    after = list(after) if isinstance(after, (list, tuple)) else [after]
    counts = [(len(st["srcs"]), len(st["lands"]), len(st["items"])) for st in states]
    n_arrays = sum(c[0] + c[1] for c in counts)

    def body(*refs):
        me = _my_place()
        mi = _dev_index(*me)
        pos = 0
        sem_pos = n_arrays
        for st, (ns, nl, ni) in zip(states, counts):
            S, L = refs[pos:pos + ns], refs[pos + ns:pos + ns + nl]
            send_sems, recv_sems = refs[sem_pos], refs[sem_pos + 1]
            pos += ns + nl
            sem_pos += 2
            for i, (src, dst) in enumerate(st["items"]):
                for k in range(1, NDEV):
                    cp = pltpu.make_async_remote_copy(
                        src_ref=src(S, mi), dst_ref=dst(L, mi), send_sem=send_sems.at[7 * i + k - 1],
                        recv_sem=recv_sems.at[7 * i + k - 1], device_id=me, device_id_type=MESH)
                    cp.wait_send()
                    cp.wait_recv()

    arrays, sems = [], []
    for st in states:
        arrays += st["srcs"] + st["lands"]
        sems += [st["send"], st["recv"]]
    outs = pl.pallas_call(
        body, name=name, in_specs=[HBM_ONLY] * n_arrays + [SEM_SPEC] * len(sems) + [HBM_SPEC] * len(after),
        out_shape=tuple(pltpu.HBM(a.shape, a.dtype) for a in arrays), out_specs=tuple([HBM_ONLY] * n_arrays),
        input_output_aliases={j: j for j in range(n_arrays)}, compiler_params=IN_FLIGHT)(*arrays, *sems, *after)
    lands, pos = [], 0
    for ns, nl, _ in counts:
        lands.append(list(outs[pos + ns:pos + ns + nl]))
        pos += ns + nl
    return lands


def _place_and_neighbours():
    x, y, c = _my_place()
    return (x, y, c), (x, y, 1 - c), [(1 - x, y), (x, 1 - y), (1 - x, 1 - y)]


def _gather_start(name, srcs, lands, items, dep=None):
    ns, nl, ni = len(srcs), len(lands), len(items)

    def body(*refs):
        S, L = refs[0:ns], refs[ns:ns + nl]
        first_out = ns + nl + (0 if dep is None else 1)
        send_sems, recv_sems, token = refs[first_out], refs[first_out + 1], refs[-1]
        me, sib, chips = _place_and_neighbours()
        mi = _dev_index(*me)
        for i, (src, dst) in enumerate(items):
            for k, to in enumerate([sib] + [(*chip, me[2]) for chip in chips]):
                pltpu.make_async_remote_copy(
                    src_ref=src(S), dst_ref=dst(L, mi), send_sem=send_sems.at[4 * i + k],
                    recv_sem=recv_sems.at[4 * i + k], device_id=to, device_id_type=MESH).start()
        token[...] = jnp.zeros_like(token)

    thru = [pltpu.HBM(a.shape, a.dtype) for a in list(srcs) + list(lands)]
    args = [_in_hbm(a) for a in srcs] + [_in_hbm(a) for a in lands]
    in_specs = [HBM_ONLY] * (ns + nl)
    if dep is not None:
        args.append(dep)
        in_specs.append(HBM_SPEC)
    outs = pl.pallas_call(
        body, name=name, in_specs=in_specs,
        out_shape=(pltpu.SemaphoreType.DMA((4 * ni,)), pltpu.SemaphoreType.DMA((4 * ni,)), *thru, _sds((8, 128), F32)),
        out_specs=(SEM_SPEC, SEM_SPEC, *[HBM_ONLY] * (ns + nl), pl.BlockSpec(memory_space=pltpu.VMEM)),
        input_output_aliases={j: 2 + j for j in range(ns + nl)}, compiler_params=IN_FLIGHT)(*args)
    return dict(send=outs[0], recv=outs[1], srcs=list(outs[2:2 + ns]), lands=list(outs[2 + ns:2 + ns + nl]),
                token=outs[-1], items=items)


def _gather_forward(name, st, after):
    nl, ni = len(st["lands"]), len(st["items"])

    def body(*refs):
        L, recv_sems = refs[0:nl], refs[nl]
        fwd_send, fwd_recv, token = refs[-3:]
        me, sib, chips = _place_and_neighbours()
        for i, (_, dst) in enumerate(st["items"]):
            for j, chip in enumerate(chips):
                blk = dst(L, _dev_index(*chip, me[2]))
                pltpu.make_async_remote_copy(
                    src_ref=blk, dst_ref=blk, send_sem=fwd_send.at[3 * i + j], recv_sem=recv_sems.at[4 * i + 1 + j],
                    device_id=me, device_id_type=MESH).wait_recv()
                pltpu.make_async_remote_copy(
                    src_ref=blk, dst_ref=blk, send_sem=fwd_send.at[3 * i + j], recv_sem=fwd_recv.at[3 * i + j],
                    device_id=sib, device_id_type=MESH).start()
        token[...] = jnp.zeros_like(token)

    after = list(after) if isinstance(after, (list, tuple)) else [after]
    outs = pl.pallas_call(
        body, name=name, in_specs=[HBM_ONLY] * nl + [SEM_SPEC] + [HBM_SPEC] * len(after),
        out_shape=(*[pltpu.HBM(a.shape, a.dtype) for a in st["lands"]], pltpu.SemaphoreType.DMA((3 * ni,)),
                   pltpu.SemaphoreType.DMA((3 * ni,)), _sds((8, 128), F32)),
        out_specs=(*[HBM_ONLY] * nl, SEM_SPEC, SEM_SPEC, pl.BlockSpec(memory_space=pltpu.VMEM)),
        input_output_aliases={j: j for j in range(nl)}, compiler_params=IN_FLIGHT)(*st["lands"], st["recv"], *after)
    return dict(st, lands=list(outs[0:nl]), fwd_send=outs[nl], fwd_recv=outs[nl + 1], token=outs[-1])


def _gather_wait(name, st, after):
    ns, nl, ni = len(st["srcs"]), len(st["lands"]), len(st["items"])

    def body(*refs):
        S, L = refs[0:ns], refs[ns:ns + nl]
        send_sems, recv_sems, fwd_send, fwd_recv = refs[ns + nl:ns + nl + 4]
        me, sib, chips = _place_and_neighbours()
        mi = _dev_index(*me)
        for i, (src, dst) in enumerate(st["items"]):
            mine = dst(L, mi)
            for k in range(4):
                pltpu.make_async_remote_copy(
                    src_ref=src(S), dst_ref=mine, send_sem=send_sems.at[4 * i + k], recv_sem=recv_sems.at[4 * i + k],
                    device_id=me, device_id_type=MESH).wait_send()
            pltpu.make_async_remote_copy(
                src_ref=src(S), dst_ref=mine, send_sem=send_sems.at[4 * i], recv_sem=recv_sems.at[4 * i],
                device_id=me, device_id_type=MESH).wait_recv()
            for j in range(3):
                cp = pltpu.make_async_remote_copy(
                    src_ref=mine, dst_ref=mine, send_sem=fwd_send.at[3 * i + j], recv_sem=fwd_recv.at[3 * i + j],
                    device_id=me, device_id_type=MESH)
                cp.wait_send()
                cp.wait_recv()

    arrays = st["srcs"] + st["lands"]
    outs = pl.pallas_call(
        body, name=name, in_specs=[HBM_ONLY] * (ns + nl) + [SEM_SPEC] * 4 + [HBM_SPEC],
        out_shape=tuple(pltpu.HBM(a.shape, a.dtype) for a in arrays), out_specs=tuple([HBM_ONLY] * (ns + nl)),
        input_output_aliases={j: j for j in range(ns + nl)},
        compiler_params=IN_FLIGHT)(*arrays, st["send"], st["recv"], st["fwd_send"], st["fwd_recv"], after)
    return list(outs[ns:ns + nl])


def _sum_slots(land):
    def body(l_ref, o_ref):
        acc = l_ref[0]
        for d in range(1, NDEV):
            acc = acc + l_ref[d]
        o_ref[...] = acc

    vm = pl.BlockSpec(memory_space=pltpu.VMEM)
    return pl.pallas_call(body, name="sum_small", out_shape=_sds(land.shape[1:], F32), in_specs=[vm], out_specs=vm)(land)


def _adam_math(w, g, m, v):
    m2 = ADAM_B1 * m + (1.0 - ADAM_B1) * g
    v2 = ADAM_B2 * v + (1.0 - ADAM_B2) * (g * g)
    delta = -ADAM_LR * ((m2 * ADAM_C1) / (jnp.sqrt(v2 * ADAM_C2) + ADAM_EPS) + ADAM_WD * w)
    return delta, m2, v2


def _adam_layer(land, sel, w, m, v, layer, name, prev=None, tc=512):
    R = land.shape[2]

    def body(l_ref, w_ref, m_ref, v_ref, *rest):
        g_out, d_out, m_out, v_out = rest[-4:]
        g = l_ref[0].astype(F32)
        for d in range(1, NDEV):
            g = g + l_ref[d].astype(F32)
        delta, m2, v2 = _adam_math(w_ref[...], g, m_ref[...], v_ref[...])
        g_out[...] = g
        d_out[...] = delta
        m_out[...] = m2
        v_out[...] = v2

    wspec = pl.BlockSpec((None, R, tc), lambda i: (layer, 0, i))
    in_specs = [pl.BlockSpec((None, NDEV, R, tc), lambda i: (sel, 0, 0, i)), wspec, wspec, wspec]
    args = [land, w, m, v]
    aliases = {}
    if prev is not None:
        in_specs += [HBM_SPEC] * 4
        args += list(prev)
        aliases = {4 + j: j for j in range(4)}
    return pl.pallas_call(
        body, name=name, grid=(D // tc,), in_specs=in_specs, out_specs=[wspec] * 4,
        out_shape=[_sds(w.shape, F32)] * 4, input_output_aliases=aliases, compiler_params=_cparams(1))(*args)


def _adam_stacked(lands, sel, w, m, v, name):
    res = None
    for layer, land in enumerate(lands):
        res = _adam_layer(land, sel, w, m, v, layer, f"{name}{layer}", prev=res)
    return res


def _adam_small(ws, gs, ms, vs):
    n = len(ws)

    def body(*refs):
        w_r, g_r, m_r, v_r = refs[0:n], refs[n:2 * n], refs[2 * n:3 * n], refs[3 * n:4 * n]
        d_o, m_o, v_o = refs[4 * n:5 * n], refs[5 * n:6 * n], refs[6 * n:7 * n]
        for i in range(n):
            delta, m2, v2 = _adam_math(w_r[i][...], g_r[i][...], m_r[i][...], v_r[i][...])
            d_o[i][...] = delta
            m_o[i][...] = m2
            v_o[i][...] = v2

    vm = pl.BlockSpec(memory_space=pltpu.VMEM)
    shapes = [_sds(w.shape, F32) for w in ws]
    outs = pl.pallas_call(body, name="adam_small", in_specs=[vm] * (4 * n), out_specs=[vm] * (3 * n),
                          out_shape=shapes * 3)(*ws, *gs, *ms, *vs)
    return outs[0:n], outs[n:2 * n], outs[2 * n:3 * n]


def _adam_of_slots(land, w, m, v, name):
    def body(l_ref, w_ref, m_ref, v_ref, g_o, d_o, m_o, v_o):
        g = l_ref[0]
        for d in range(1, NDEV):
            g = g + l_ref[d]
        g_o[...] = g
        d_o[...], m_o[...], v_o[...] = _adam_math(w_ref[...], g, m_ref[...], v_ref[...])

    vm = pl.BlockSpec(memory_space=pltpu.VMEM)
    return pl.pallas_call(body, name=name, in_specs=[vm] * 4, out_specs=[vm] * 4,
                          out_shape=[_sds(w.shape, F32)] * 4)(land, w, m, v)


WEIGHT_NAMES = ("ab_norm_g", "ab_w_in", "sgu_norm_g", "sgu_norm_b", "sgu_w", "sgu_bias", "q_norm_g", "k_norm_g",
                "ab_w_out", "cd_norm_g", "cd_w_in", "conv_c_w", "conv_c_b", "c_ln_g", "c_ln_b", "conv_d_w",
                "cd_w_out", "ffn_norm_g", "ffn_w_gate", "ffn_w_up", "ffn_w_down")
SMALL_2D = (("sgu_norm_g", (1, 512)), ("sgu_norm_b", (1, 512)), ("sgu_w", (512, 128)),
            ("sgu_bias", (4, 128)), ("q_norm_g", (3, 64)), ("k_norm_g", (3, 64)), ("cd_norm_g", (1, 128)),
            ("conv_c_w", (31, 64)), ("conv_c_b", (1, 64)), ("c_ln_g", (1, 64)), ("c_ln_b", (1, 64)),
            ("conv_d_w", (3, 64)), ("ffn_norm_g", (2, 1024)))
SHARD_C = 64


def _pack_rows(parts, rows):
    flat = jnp.concatenate([p.reshape(-1) for p in parts])
    return jnp.pad(flat, (0, rows * 128 - flat.shape[0])).reshape(rows, 128)


def kernel(x, ab_norm_g, ab_w_in, sgu_norm_g, sgu_norm_b, sgu_w, sgu_bias, q_norm_g, k_norm_g, ab_w_out, cd_norm_g, cd_w_in, conv_c_w, conv_c_b, c_ln_g, c_ln_b, conv_d_w, cd_w_out, ffn_norm_g, ffn_w_gate, ffn_w_up, ffn_w_down, loss_target, m_ab_norm_g, m_ab_w_in, m_sgu_norm_g, m_sgu_norm_b, m_sgu_w, m_sgu_bias, m_q_norm_g, m_k_norm_g, m_ab_w_out, m_cd_norm_g, m_cd_w_in, m_conv_c_w, m_conv_c_b, m_c_ln_g, m_c_ln_b, m_conv_d_w, m_cd_w_out, m_ffn_norm_g, m_ffn_w_gate, m_ffn_w_up, m_ffn_w_down, v_ab_norm_g, v_ab_w_in, v_sgu_norm_g, v_sgu_norm_b, v_sgu_w, v_sgu_bias, v_q_norm_g, v_k_norm_g, v_ab_w_out, v_cd_norm_g, v_cd_w_in, v_conv_c_w, v_conv_c_b, v_c_ln_g, v_c_ln_b, v_conv_d_w, v_cd_w_out, v_ffn_norm_g, v_ffn_w_gate, v_ffn_w_up, v_ffn_w_down):
    w = dict(zip(WEIGHT_NAMES, (ab_norm_g, ab_w_in, sgu_norm_g, sgu_norm_b, sgu_w, sgu_bias, q_norm_g, k_norm_g, ab_w_out, cd_norm_g, cd_w_in, conv_c_w, conv_c_b, c_ln_g, c_ln_b, conv_d_w, cd_w_out, ffn_norm_g, ffn_w_gate, ffn_w_up, ffn_w_down)))
    m = dict(zip(WEIGHT_NAMES, (m_ab_norm_g, m_ab_w_in, m_sgu_norm_g, m_sgu_norm_b, m_sgu_w, m_sgu_bias, m_q_norm_g, m_k_norm_g, m_ab_w_out, m_cd_norm_g, m_cd_w_in, m_conv_c_w, m_conv_c_b, m_c_ln_g, m_c_ln_b, m_conv_d_w, m_cd_w_out, m_ffn_norm_g, m_ffn_w_gate, m_ffn_w_up, m_ffn_w_down)))
    v = dict(zip(WEIGHT_NAMES, (v_ab_norm_g, v_ab_w_in, v_sgu_norm_g, v_sgu_norm_b, v_sgu_w, v_sgu_bias, v_q_norm_g, v_k_norm_g, v_ab_w_out, v_cd_norm_g, v_cd_w_in, v_conv_c_w, v_conv_c_b, v_c_ln_g, v_c_ln_b, v_conv_d_w, v_cd_w_out, v_ffn_norm_g, v_ffn_w_gate, v_ffn_w_up, v_ffn_w_down)))
    me = _dev_index(*_my_place())

    r_ff = DFF // NDEV
    one = lambda a: (lambda S, j: S[a])
    slot = lambda b: (lambda L, s: L[b].at[s])
    slot2 = lambda b, part: (lambda L, s: L[b].at[part, s])
    shard = lambda a: (lambda S: S[a])

    def later(a):
        return lax.optimization_barrier((a, gathers[0]["token"]))[0]

    def layer_shards(layer):
        return (later(w["ffn_w_gate"][layer]).T.astype(BF16), later(w["ffn_w_up"][layer]).T.astype(BF16),
                later(w["ffn_w_down"][layer]).astype(BF16))

    def gathered(own):
        return _landing((NDEV,) + own.shape, BF16, [((me,), own)])

    def gathered2(a, b):
        return _landing((2, NDEV) + a.shape, BF16, [((0, me), a), ((1, me), b)])

    ab_in_s = w["ab_w_in"][0].T.astype(BF16)
    gathers = {0: _gather_start("gather0_start", [ab_in_s], [gathered(ab_in_s)], [(shard(0), slot(0))])}

    def chan(flat, lo, taps):
        return flat[:, lo:lo + taps * SHARD_C].reshape(NDEV, taps, SHARD_C).transpose(1, 0, 2).reshape(taps, 512)

    def fetch(stage, after):
        if stage == "ab_in":
            ab_out_s = later(w["ab_w_out"][0]).astype(BF16)
            gate0, up0, down0 = layer_shards(0)
            small_s = _pack_rows([later(w[n]) for n in ("cd_norm_g", "conv_c_w", "conv_c_b", "c_ln_g", "c_ln_b",
                                                        "conv_d_w")], 24)
            lands1 = [gathered(ab_out_s), gathered2(gate0, up0), gathered(down0),
                      _landing((NDEV,) + small_s.shape, F32, [((me,), small_s)])]
            gathers[0] = _gather_forward("gather0_forward", gathers[0], [after] + lands1)
            l_ab_in, = _gather_wait("gather0_wait", gathers[0], gathers[0]["token"])
            gathers[1] = _gather_start(
                "gather1_start", [ab_out_s, gate0, up0, down0, small_s], lands1,
                [(shard(0), slot(0)), (shard(1), slot2(1, 0)), (shard(2), slot2(1, 1)), (shard(3), slot(2)),
                 (shard(4), slot(3))], dep=l_ab_in)
            return {"wt_ab_in": l_ab_in.reshape(AB_IN, D), "dep0": gathers[1]["token"]}
        if stage == "attn0":
            cd_in_s, cd_out_s = later(w["cd_w_in"][0]).T.astype(BF16), later(w["cd_w_out"][0]).astype(BF16)
            gate1, up1, down1 = layer_shards(1)
            gathers[2] = _gather_start(
                "gather2_start", [cd_in_s, cd_out_s, gate1, up1, down1],
                [gathered(cd_in_s), gathered(cd_out_s), gathered2(gate1, up1), gathered(down1)],
                [(shard(0), slot(0)), (shard(1), slot(1)), (shard(2), slot2(2, 0)), (shard(3), slot2(2, 1)),
                 (shard(4), slot(3))], dep=after)
            return {"dep_attn1": gathers[2]["token"]}
        if stage == "attn1":
            gathers[1] = _gather_forward("gather1_forward", gathers[1], after)
            return {"dep_attn2": gathers[1]["token"]}
        if stage == "ab_out":
            l_out, l_ffn, l_down, l_small = _gather_wait("gather1_wait", gathers[1], after)
            flat = l_small.reshape(NDEV, 24 * 128)
            return {
                "w_ab_out": l_out.reshape(D, D), "wt_ffn_in0": l_ffn.reshape(2 * DFF, D),
                "w_ffn_down0": l_down.reshape(DFF, D), "cd_norm_g": flat[:, 0:128].reshape(1, D),
                "conv_c_w32": jnp.pad(chan(flat, 128, CONV_C_TAPS), ((0, 1), (0, 0))),
                "conv_c_b": chan(flat, 2112, 1), "c_ln_g": chan(flat, 2176, 1), "c_ln_b": chan(flat, 2240, 1),
                "conv_d_w8": jnp.pad(chan(flat, 2304, CONV_D_TAPS), ((0, 8 - CONV_D_TAPS), (0, 0))),
            }
        if stage == "ffn_down0":
            gathers[2] = _gather_forward("gather2_forward", gathers[2], after)
            return {"dep_down0": gathers[2]["token"]}
        if stage == "cd_in":
            l_in, l_out, l_ffn, l_down = _gather_wait("gather2_wait", gathers[2], after)
            return {"wt_cd_in": l_in.reshape(CD_IN, D), "w_cd_out": l_out.reshape(D, D),
                    "wt_ffn_in1": l_ffn.reshape(2 * DFF, D), "w_ffn_down1": l_down.reshape(DFF, D)}
        return {}

    scatters = {}
    rides_with = {"w_ffn_down1": "wt_ffn_in1", "w_cd_out": "wt_cd_in", "w_ffn_down0": "wt_ffn_in0"}
    held = {}
    smalls = {}

    def small_exchange(name, block):
        land = _landing((NDEV,) + block.shape, F32, [((me,), block)])
        return _exchange_start(name, [block], [land], [(one(0), slot(0))])

    def on_grad(key, arr):
        if key == "small":
            parts = [arr["sgu_norm_g"], arr["sgu_norm_b"], arr["sgu_w"], arr["sgu_bias"], arr["q_norm_g"],
                     arr["k_norm_g"], arr["cd_norm_g"], arr["conv_c_w32"][:CONV_C_TAPS], arr["conv_c_b"], arr["c_ln_g"],
                     arr["c_ln_b"], arr["conv_d_w8"][:CONV_D_TAPS], arr["ffn_norm_g0"], arr["ffn_norm_g1"],
                     arr["loss_cols"]]
            smalls["sizes"] = [p.size for p in parts]
            rows = -(-sum(smalls["sizes"]) // 1024) * 8
            smalls["early"] = small_exchange("small_start", _pack_rows(parts, rows))
            return smalls["early"]["token"]
        if key in rides_with:
            held[rides_with[key]] = (key, arr)
            return None
        group = ([held.pop(key)] if key in held else []) + [(key, arr)]
        srcs, lands, items = [], [], []
        for n, (k, a) in enumerate(group):
            if k.startswith("wt_ffn_in"):
                src = a.reshape(2, NDEV, r_ff, D)
                own = lax.dynamic_slice_in_dim(src, me, 1, axis=1)
                lands.append(lax.dynamic_update_slice(lax.empty(src.shape, BF16), own, (0, me, 0, 0)))
                items += [((lambda S, j, n=n: S[n].at[0, j]), slot2(n, 0)), ((lambda S, j, n=n: S[n].at[1, j]), slot2(n, 1))]
            else:
                rows = a.shape[0] // NDEV
                src = a.reshape(NDEV, rows, D)
                own = lax.dynamic_index_in_dim(src, me, 0, keepdims=False)
                lands.append(_landing((1, NDEV, rows, D), BF16, [((0, me), own)]))
                items.append(((lambda S, j, n=n: S[n].at[j]), slot2(n, 0)))
            srcs.append(src)
        st = _exchange_start(f"scatter_{key}_start", srcs, lands, items)
        scatters[key] = (st, [k for k, _ in group])
        return st["token"]

    W = {
        "dep_first": gathers[0]["token"],
        "ab_norm_g": w["ab_norm_g"], "sgu_norm_g": w["sgu_norm_g"], "sgu_norm_b": w["sgu_norm_b"],
        "sgu_w": w["sgu_w"][0], "sgu_bias": w["sgu_bias"][0], "q_norm_g": w["q_norm_g"][0],
        "k_norm_g": w["k_norm_g"][0], "ffn_norm_g": w["ffn_norm_g"],
    }

    loss_cols, grad_x, G = _local_step(x[0], loss_target[0], W, fetch, on_grad)

    late_small = small_exchange("small_late_start", G["ab_norm_g"])
    landed = {}

    def wait_scatters(name, group_keys, others, after):
        res = _exchange_wait(name, [scatters[gk][0] for gk in group_keys] + others, after)
        for gk, lands in zip(group_keys, res):
            landed.update(zip(scatters[gk][1], lands))
        return [lands[0] for lands in res[len(group_keys):]]

    small_land, = wait_scatters("scatter_wait_early", ["wt_ffn_in1", "wt_cd_in", "wt_ffn_in0", "w_ab_out"],
                                [smalls["early"]], late_small["token"])

    grads, deltas, new_m, new_v = {}, {}, {}, {}
    done = []

    def put(name, res):
        grads[name], deltas[name], new_m[name], new_v[name] = res

    def adam(name, lands, sel, transposed):
        flip = (lambda a: jnp.swapaxes(a, 1, 2)) if transposed else (lambda a: a)
        res = _adam_stacked(lands, sel, flip(w[name]), flip(m[name]), flip(v[name]), f"adam_{name}")
        done.append(res[1])
        put(name, [flip(r) for r in res])

    ffn_in_lands = [landed["wt_ffn_in0"], landed["wt_ffn_in1"]]
    adam("cd_w_in", [landed["wt_cd_in"]], 0, True)
    adam("ffn_w_gate", ffn_in_lands, 0, True)
    adam("ffn_w_up", ffn_in_lands, 1, True)
    adam("cd_w_out", [landed["w_cd_out"]], 0, False)
    adam("ab_w_out", [landed["w_ab_out"]], 0, False)
    adam("ffn_w_down", [landed["w_ffn_down0"], landed["w_ffn_down1"]], 0, False)

    red = _sum_slots(small_land).reshape(-1)
    offs = [0]
    for s in smalls["sizes"]:
        offs.append(offs[-1] + s)
    seg = [red[offs[i]:offs[i + 1]] for i in range(len(smalls["sizes"]))]
    loss = jnp.sum(seg[14])

    def own_channels(full, taps):
        return lax.dynamic_slice_in_dim(full.reshape(taps, 512), me * SHARD_C, SHARD_C, axis=1)

    g_small = {
        "sgu_norm_g": seg[0].reshape(1, 512), "sgu_norm_b": seg[1].reshape(1, 512),
        "sgu_w": seg[2].reshape(512, 128), "sgu_bias": seg[3].reshape(4, 128), "q_norm_g": seg[4].reshape(3, 64),
        "k_norm_g": seg[5].reshape(3, 64),
        "cd_norm_g": lax.dynamic_slice_in_dim(seg[6].reshape(1, D), me * (D // NDEV), D // NDEV, axis=1),
        "conv_c_w": own_channels(seg[7], CONV_C_TAPS), "conv_c_b": own_channels(seg[8], 1),
        "c_ln_g": own_channels(seg[9], 1), "c_ln_b": own_channels(seg[10], 1),
        "conv_d_w": own_channels(seg[11], CONV_D_TAPS),
        "ffn_norm_g": jnp.concatenate([seg[12].reshape(1, D), seg[13].reshape(1, D)], axis=0),
    }

    d_s, m_s, v_s = _adam_small([w[n].reshape(s) for n, s in SMALL_2D], [g_small[n] for n, _ in SMALL_2D],
                                [m[n].reshape(s) for n, s in SMALL_2D], [v[n].reshape(s) for n, s in SMALL_2D])
    for i, (n, _) in enumerate(SMALL_2D):
        shape = w[n].shape
        grads[n], deltas[n] = g_small[n].reshape(shape), d_s[i].reshape(shape)
        new_m[n], new_v[n] = m_s[i].reshape(shape), v_s[i].reshape(shape)
    done.append(d_s[0])

    late_land, = wait_scatters("scatter_wait_last", ["wt_ab_in"], [late_small], list(done))
    put("ab_norm_g", _adam_of_slots(late_land, w["ab_norm_g"], m["ab_norm_g"], v["ab_norm_g"], "adam_ab_norm_g"))
    adam("ab_w_in", [landed["wt_ab_in"]], 0, True)

    return (loss, grad_x[None], *[grads[n] for n in WEIGHT_NAMES], *[deltas[n] for n in WEIGHT_NAMES],
            *[new_m[n] for n in WEIGHT_NAMES], *[new_v[n] for n in WEIGHT_NAMES])
```

```python
import jax
import jax.numpy as jnp
import numpy as np
from jax import lax
from jax.experimental import pallas as pl
from jax.experimental.pallas import tpu as pltpu

F32 = jnp.float32
BF16 = jnp.bfloat16

T = 4096
D = 1024
NDEV = 8
EPS = 1e-6
NEG_INF = -1e30
DFF = 2816
AB_IN = 5632
CD_IN = 2560
HEAD = 64
PAIR = 128
NPAIR = 4
NBACK = 128
DIL_RATES = (1, 4, 16)
ROPE_HALF = 8
ROPE_THETA = 500000.0
CONV_C_TAPS = 31
CONV_D_TAPS = 3
HALO = 32
ATTN_BWD_UNROLL = 4
MAX_ROW_STRIDE = 4

ADAM_LR = 0.001
ADAM_B1 = 0.9
ADAM_B2 = 0.999
ADAM_EPS = 1e-08
ADAM_WD = 0.01
ADAM_STEP = 10
ADAM_C1 = 1.0 / (1.0 - ADAM_B1 ** ADAM_STEP)
ADAM_C2 = 1.0 / (1.0 - ADAM_B2 ** ADAM_STEP)

VMEM_LIMIT_MB = 48
MESH = pl.DeviceIdType.MESH
HBM_SPEC = pl.BlockSpec(memory_space=pl.ANY)


def _cparams(ngrid, vmem_mb=VMEM_LIMIT_MB):
    return pltpu.CompilerParams(dimension_semantics=("arbitrary",) * ngrid,
                                vmem_limit_bytes=vmem_mb * 1024 * 1024)


def _pick(n, options):
    for o in options:
        if n % o == 0:
            return o
    raise ValueError(f"no tile for {n} in {options}")


def _sds(shape, dtype):
    return jax.ShapeDtypeStruct(shape, dtype)


def _sigmoid(x):
    return 1.0 / (1.0 + jnp.exp(-x))


def _sigmoid_bf16(x):
    return 0.5 * jnp.tanh(0.5 * x) + 0.5


def _gelu(z):
    return 0.5 * z * (1.0 + lax.erf(z * 0.7071067811865476))


def _gelu_grad(z):
    return 0.5 * (1.0 + lax.erf(z * 0.7071067811865476)) + z * jnp.exp(-0.5 * z * z) * 0.3989422804014327


def _mm_nt(a, wt, name, out_dtype=BF16, dep=None):
    M, K = a.shape
    N = wt.shape[0]
    tn = _pick(N, (512, 256))

    def body(a_ref, w_ref, *rest):
        o_ref = rest[-1]
        for r0 in range(0, M, 1024):
            o_ref[r0:r0 + 1024, :] = lax.dot_general(
                a_ref[r0:r0 + 1024, :], w_ref[...], (((1,), (1,)), ((), ())),
                preferred_element_type=F32).astype(o_ref.dtype)

    in_specs = [pl.BlockSpec((M, K), lambda j: (0, 0), pipeline_mode=pl.Buffered(1)),
                pl.BlockSpec((tn, K), lambda j: (j, 0))]
    args = [a, wt]
    if dep is not None:
        in_specs.append(HBM_SPEC)
        args.append(dep)
    return pl.pallas_call(
        body, name=name, grid=(N // tn,), in_specs=in_specs, out_specs=pl.BlockSpec((M, tn), lambda j: (0, j)),
        out_shape=_sds((M, N), out_dtype), compiler_params=_cparams(1))(*args)


EPI_ROWS = 256


def _mm_nt_rows(a, wt, name, epilogue, side, side_specs, out_specs, out_shape, sums=(), dep=None, tm=512):
    M, K = a.shape
    N = wt.shape[0]
    ns, no = len(side), len(out_shape)

    def body(a_ref, w_ref, *rest):
        side_refs, outs, acc = rest[0:ns], rest[-1 - no:-1], rest[-1]
        acc[...] = lax.dot_general(a_ref[...], w_ref[...], (((1,), (1,)), ((), ())), preferred_element_type=F32)

        @pl.when(pl.program_id(0) == 0)
        def _():
            for j in sums:
                outs[j][...] = jnp.zeros_like(outs[j])

        for r0 in range(0, tm, EPI_ROWS):
            rows = slice(r0, r0 + EPI_ROWS)
            epilogue(acc[rows, :].astype(BF16).astype(F32), rows, side_refs, outs)

    in_specs = [pl.BlockSpec((tm, K), lambda i: (i, 0)),
                pl.BlockSpec((N, K), lambda i: (0, 0), pipeline_mode=pl.Buffered(1))] + list(side_specs)
    args = [a, wt, *side]
    if dep is not None:
        in_specs.append(HBM_SPEC)
        args.append(dep)
    return pl.pallas_call(
        body, name=name, grid=(M // tm,), in_specs=in_specs, out_specs=list(out_specs), out_shape=list(out_shape),
        scratch_shapes=[pltpu.VMEM((tm, N), F32)], compiler_params=_cparams(1))(*args)


def _mm_nn(a, w, name, mode, resid, gain=None, tgt=None, dep=None, tm=512):
    M, K = a.shape
    N = w.shape[1]
    side = gain if mode == "rms" else tgt

    def body(a_ref, w_ref, resid_ref, side_ref, *rest):
        outs, acc = rest[-3 if mode == "rms" else -4:-1], rest[-1]
        i = pl.program_id(0)
        acc[...] = jnp.dot(a_ref[...], w_ref[...], preferred_element_type=F32)

        if mode == "loss":
            @pl.when(i == 0)
            def _():
                outs[2][...] = jnp.zeros_like(outs[2])

        for r0 in range(0, tm, EPI_ROWS):
            rows = slice(r0, r0 + EPI_ROWS)
            v = acc[rows, :] + resid_ref[rows, :]
            if mode == "rms":
                outs[0][rows, :] = v
                r = lax.rsqrt(jnp.mean(v * v, axis=-1, keepdims=True) + EPS)
                outs[1][rows, :] = (v * r * side_ref[...]).astype(BF16)
            else:
                d = v - side_ref[rows, :]
                outs[2][...] += jnp.sum(d * d, axis=0, keepdims=True) * (0.5 / N)
                dy = d * (1.0 / N)
                outs[0][rows, :] = dy
                outs[1][rows, :] = dy.astype(BF16)

    row = pl.BlockSpec((tm, N), lambda i: (i, 0))
    vec = pl.BlockSpec((1, N), lambda i: (0, 0))
    in_specs = [pl.BlockSpec((tm, K), lambda i: (i, 0)),
                pl.BlockSpec((K, N), lambda i: (0, 0), pipeline_mode=pl.Buffered(1)), row,
                vec if mode == "rms" else row]
    args = [a, w, resid, side]
    if dep is not None:
        in_specs.append(HBM_SPEC)
        args.append(dep)
    if mode == "rms":
        out_specs, out_shape = [row, row], [_sds((M, N), F32), _sds((M, N), BF16)]
    else:
        out_specs, out_shape = [row, row, vec], [_sds((M, N), F32), _sds((M, N), BF16), _sds((1, N), F32)]
    return pl.pallas_call(
        body, name=name, grid=(M // tm,), in_specs=in_specs, out_specs=out_specs, out_shape=out_shape,
        scratch_shapes=[pltpu.VMEM((tm, N), F32)], compiler_params=_cparams(1))(*args)


def _mm_dh_rms_bwd(a, w, x, gain, dres, name, dep=None, tm=512, bf16_copy=True):
    parts = a.shape[0] if a.ndim == 3 else 1
    M, Kp = a.shape[-2], a.shape[-1]
    N = w.shape[1]
    nblk = M // tm
    assert nblk % 2 == 0

    def body(a_ref, w_ref, x_ref, g_ref, dres_ref, *rest):
        dg_ref, acc0, acc1 = rest[-3:]
        dx_ref = rest[-5] if bf16_copy else rest[-4]
        dxb_ref = rest[-4] if bf16_copy else None
        i = pl.program_id(0)

        def matmul(acc):
            if parts == 1:
                acc[...] = jnp.dot(a_ref[...], w_ref[...], preferred_element_type=F32)
            else:
                d = jnp.dot(a_ref[0], w_ref[0:Kp, :], preferred_element_type=F32)
                for p in range(1, parts):
                    d = d + jnp.dot(a_ref[p], w_ref[p * Kp:(p + 1) * Kp, :], preferred_element_type=F32)
                acc[...] = d

        def finish(acc):
            for r0 in range(0, tm, EPI_ROWS // 2):
                rows = slice(r0, r0 + EPI_ROWS // 2)
                v = acc[rows, :]
                xf = x_ref[rows, :]
                r = lax.rsqrt(jnp.mean(xf * xf, axis=-1, keepdims=True) + EPS)
                xhat = xf * r
                dg_ref[...] += jnp.sum(v * xhat, axis=0, keepdims=True)
                dxh = v * g_ref[...]
                tot = dres_ref[rows, :] + r * (dxh - xhat * jnp.mean(dxh * xhat, axis=-1, keepdims=True))
                dx_ref[rows, :] = tot
                if bf16_copy:
                    dxb_ref[rows, :] = tot.astype(BF16)

        @pl.when(i == 0)
        def _():
            dg_ref[...] = jnp.zeros_like(dg_ref)
            matmul(acc0)

        @pl.when((i > 0) & (i < nblk) & (i % 2 == 1))
        def _():
            matmul(acc1)
            finish(acc0)

        @pl.when((i > 0) & (i < nblk) & (i % 2 == 0))
        def _():
            matmul(acc0)
            finish(acc1)

        @pl.when(i == nblk)
        def _():
            finish(acc1)

    last = nblk - 1
    row = pl.BlockSpec((tm, N), lambda i: (jnp.maximum(i - 1, 0), 0))
    vec = pl.BlockSpec((1, N), lambda i: (0, 0))
    if a.ndim == 3:
        a_spec = pl.BlockSpec((parts, tm, Kp), lambda i: (0, jnp.minimum(i, last), 0))
    else:
        a_spec = pl.BlockSpec((tm, Kp), lambda i: (jnp.minimum(i, last), 0))
    w_spec = pl.BlockSpec((parts * Kp, N), lambda i: (0, 0), pipeline_mode=pl.Buffered(1))
    in_specs = [a_spec, w_spec, row, vec, row]
    args = [a, w, x, gain, dres]
    if dep is not None:
        in_specs.append(HBM_SPEC)
        args.append(dep)
    return pl.pallas_call(
        body, name=name, grid=(nblk + 1,), in_specs=in_specs,
        out_specs=[row, row, vec] if bf16_copy else [row, vec],
        out_shape=([_sds((M, N), F32), _sds((M, N), BF16), _sds((1, N), F32)] if bf16_copy
                   else [_sds((M, N), F32), _sds((1, N), F32)]),
        scratch_shapes=[pltpu.VMEM((tm, N), F32), pltpu.VMEM((tm, N), F32)], compiler_params=_cparams(1, 56))(*args)


def _mm_tn(a, b, name, out_dtype=BF16, tt=2048, dep=None):
    parts = a.shape[0] if a.ndim == 3 else 1
    Tt, Mp = a.shape[-2], a.shape[-1]
    N = b.shape[1]
    tn = _pick(Mp, (1408, 1280, 1024, 512))
    jper = Mp // tn
    nt = Tt // tt

    def body(a_ref, b_ref, *rest):
        o_ref, acc = rest[-2:]
        t = pl.program_id(1)

        @pl.when(t == 0)
        def _():
            acc[...] = jnp.zeros_like(acc)

        rows = pl.ds(pl.multiple_of(t * tt, tt), tt)
        acc[...] += lax.dot_general(a_ref[...], b_ref[rows, :], (((0,), (0,)), ((), ())),
                                    preferred_element_type=F32)

        @pl.when(t == nt - 1)
        def _():
            o_ref[...] = acc[...].astype(o_ref.dtype)

    if a.ndim == 3:
        a_spec = pl.BlockSpec((None, tt, tn), lambda j, t: (j // jper, t, j % jper))
    else:
        a_spec = pl.BlockSpec((tt, tn), lambda j, t: (t, j))
    in_specs = [a_spec, pl.BlockSpec((Tt, N), lambda j, t: (0, 0), pipeline_mode=pl.Buffered(1))]
    args = [a, b]
    if dep is not None:
        in_specs.append(HBM_SPEC)
        args.append(dep)
    return pl.pallas_call(
        body, name=name, grid=(parts * jper, nt), in_specs=in_specs,
        out_specs=pl.BlockSpec((tn, N), lambda j, t: (j, 0)),
        out_shape=_sds((parts * Mp, N), out_dtype), scratch_shapes=[pltpu.VMEM((tn, N), F32)],
        compiler_params=_cparams(2))(*args)


FFN_ROWS = 2048


def _ffn_in(h, wt_in, name, tn=256):
    nj = DFF // tn

    def body(h_ref, wg_ref, wu_ref, p_ref, act_ref):
        nt = (((1,), (1,)), ((), ()))
        for r0 in range(0, T, FFN_ROWS):
            rows = slice(r0, r0 + FFN_ROWS)
            g = lax.dot_general(h_ref[rows, :], wg_ref[...], nt, preferred_element_type=F32).astype(BF16)
            u = lax.dot_general(h_ref[rows, :], wu_ref[...], nt, preferred_element_type=F32).astype(BF16)
            p_ref[0, rows, :] = g
            p_ref[1, rows, :] = u
            act_ref[rows, :] = g * _sigmoid_bf16(g) * u

    return pl.pallas_call(
        body, name=name, grid=(nj,),
        in_specs=[pl.BlockSpec((T, D), lambda j: (0, 0), pipeline_mode=pl.Buffered(1)),
                  pl.BlockSpec((tn, D), lambda j: (j, 0)), pl.BlockSpec((tn, D), lambda j: (j + nj, 0))],
        out_specs=[pl.BlockSpec((2, T, tn), lambda j: (0, 0, j)), pl.BlockSpec((T, tn), lambda j: (0, j))],
        out_shape=[_sds((2, T, DFF), BF16), _sds((T, DFF), BF16)], compiler_params=_cparams(1))(h, wt_in, wt_in)


def _ffn_dact(dyb, w_down, p3, name, tn=256, dep=None):
    def body(dy_ref, w_ref, p_ref, *rest):
        o_ref = rest[-1]
        for r0 in range(0, T, FFN_ROWS):
            rows = slice(r0, r0 + FFN_ROWS)
            da = lax.dot_general(dy_ref[rows, :], w_ref[...], (((1,), (1,)), ((), ())),
                                 preferred_element_type=F32).astype(BF16)
            g = p_ref[0, rows, :]
            u = p_ref[1, rows, :]
            sg = _sigmoid_bf16(g)
            gs = g * sg
            o_ref[0, rows, :] = (da * u) * (sg + gs * (1.0 - sg))
            o_ref[1, rows, :] = da * gs

    pspec = pl.BlockSpec((2, T, tn), lambda j: (0, 0, j))
    in_specs = [pl.BlockSpec((T, D), lambda j: (0, 0), pipeline_mode=pl.Buffered(1)),
                pl.BlockSpec((tn, D), lambda j: (j, 0)), pspec]
    args = [dyb, w_down, p3]
    if dep is not None:
        in_specs.append(HBM_SPEC)
        args.append(dep)
    return pl.pallas_call(
        body, name=name, grid=(DFF // tn,), in_specs=in_specs, out_specs=pspec,
        out_shape=_sds((2, T, DFF), BF16), compiler_params=_cparams(1))(*args)


def _rms_fwd(x, g, name, tm=512, dep=None):
    def body(x_ref, g_ref, *rest):
        h_ref = rest[-1]
        xf = x_ref[...]
        r = lax.rsqrt(jnp.mean(xf * xf, axis=-1, keepdims=True) + EPS)
        h_ref[...] = (xf * r * g_ref[...]).astype(BF16)

    in_specs = [pl.BlockSpec((tm, D), lambda i: (i, 0)), pl.BlockSpec((1, D), lambda i: (0, 0))]
    args = [x, g]
    if dep is not None:
        in_specs.append(HBM_SPEC)
        args.append(dep)
    return pl.pallas_call(
        body, name=name, grid=(T // tm,), in_specs=in_specs, out_specs=pl.BlockSpec((tm, D), lambda i: (i, 0)),
        out_shape=_sds((T, D), BF16), compiler_params=_cparams(1))(*args)


def _tril_mask():
    r = lax.broadcasted_iota(jnp.int32, (128, 128), 0)
    c = lax.broadcasted_iota(jnp.int32, (128, 128), 1)
    return r >= c


def _mix_a_fwd(pab, sgu_g, sgu_b, sgu_w, sgu_bias3, tm=512):
    def body(zu_ref, zv_ref, g_ref, b_ref, w_ref, bias_ref, o_ref):
        u = _gelu(zu_ref[...].astype(F32))
        v = _gelu(zv_ref[...].astype(F32))
        mu = jnp.mean(v, axis=-1, keepdims=True)
        vc = v - mu
        rstd = lax.rsqrt(jnp.mean(vc * vc, axis=-1, keepdims=True) + EPS)
        vn = (vc * rstd * g_ref[...] + b_ref[...]).astype(BF16)
        tri = _tril_mask()
        for gi in range(4):
            wg = jnp.where(tri, w_ref[gi], 0.0).astype(BF16)
            bg = bias_ref[gi]
            for c in range(tm // 128):
                rs, cs = slice(c * 128, (c + 1) * 128), slice(gi * 128, (gi + 1) * 128)
                mixed = jnp.dot(wg, vn[rs, cs], preferred_element_type=F32) + bg
                o_ref[rs, cs] = (u[rs, cs] * mixed).astype(BF16)

    half = pl.BlockSpec((tm, 512), lambda i: (i, 0))
    return pl.pallas_call(
        body, name="mix_a_fwd", grid=(T // tm,),
        in_specs=[half, pl.BlockSpec((tm, 512), lambda i: (i, 1)),
                  pl.BlockSpec((1, 512), lambda i: (0, 0)), pl.BlockSpec((1, 512), lambda i: (0, 0)),
                  pl.BlockSpec((4, 128, 128), lambda i: (0, 0, 0)), pl.BlockSpec((4, 128, 1), lambda i: (0, 0, 0))],
        out_specs=half, out_shape=_sds((T, D), BF16), compiler_params=_cparams(1),
    )(pab, pab, sgu_g, sgu_b, sgu_w, sgu_bias3)


def _rope_tables():
    pos = np.arange(T, dtype=np.float32)
    inv_freq = np.float32(ROPE_THETA) ** (-np.arange(ROPE_HALF, dtype=np.float32) * np.float32(2.0 / (2 * ROPE_HALF)))
    ang = (pos[:, None] * inv_freq[None, :]).astype(np.float32)
    cos, sin = np.cos(ang), np.sin(ang)
    z8 = np.zeros((T, ROPE_HALF), np.float32)
    rest = np.zeros((T, HEAD - 2 * ROPE_HALF), np.float32)
    c64 = np.concatenate([cos, cos, rest + 1.0], axis=1)
    s1 = np.concatenate([z8, sin, rest], axis=1)
    s2 = np.concatenate([-sin, z8, rest], axis=1)
    return tuple(jnp.asarray(np.tile(t, (1, 2)).astype(np.float32)) for t in (c64, s1, s2))


def _lo_mask(shape):
    return lax.broadcasted_iota(jnp.int32, shape, 1) < HEAD


def _seg_mean(x, lo):
    s_all = jnp.sum(x, axis=-1, keepdims=True)
    s_lo = jnp.sum(jnp.where(lo, x, 0.0), axis=-1, keepdims=True)
    return jnp.where(lo, s_lo, s_all - s_lo) * (1.0 / HEAD)


def _head_blocks():
    r = lax.broadcasted_iota(jnp.int32, (PAIR, PAIR), 0) < HEAD
    c = lax.broadcasted_iota(jnp.int32, (PAIR, PAIR), 1) < HEAD
    return jnp.where(r == c, 1.0, 0.0).astype(BF16)


def _seg_mean_mxu(x, blocks):
    return jnp.dot(x.astype(BF16), blocks, preferred_element_type=F32) * (1.0 / HEAD)


def _rope(n, c, s1, s2):
    return n * c + pltpu.roll(n, ROPE_HALF, 1) * s1 + pltpu.roll(n, PAIR - ROPE_HALF, 1) * s2


def _rope_t(dy, c, s1, s2):
    return dy * c - pltpu.roll(dy, PAIR - ROPE_HALF, 1) * s2 - pltpu.roll(dy, ROPE_HALF, 1) * s1


def _prep_fwd(pab, qg, kg, tabs, tm=512):
    def body(p_ref, qg_ref, kg_ref, c_ref, s1_ref, s2_ref, *outs):
        blocks = _head_blocks()
        c, s1, s2 = c_ref[...], s1_ref[...], s2_ref[...]
        for g in range(3):
            qn_ref, kn_ref, v_ref = outs[3 * g:3 * g + 3]
            for p in range(NPAIR):
                for which, gains, dst in ((0, qg_ref, qn_ref), (1, kg_ref, kn_ref)):
                    col = (2 + 3 * which + g) * 512 + p * PAIR
                    xr = p_ref[:, col:col + PAIR].astype(F32)
                    rinv = lax.rsqrt(_seg_mean_mxu(xr * xr, blocks) + EPS)
                    outs[9 + 2 * g + which][p] = rinv.astype(BF16)
                    dst[p] = _rope(xr * rinv * gains[g:g + 1, :], c, s1, s2)
                col = (8 + g) * 512 + p * PAIR
                v_ref[p] = p_ref[:, col:col + PAIR].astype(F32)

    pm = pl.BlockSpec((NPAIR, tm, PAIR), lambda i: (0, i, 0))
    tab = pl.BlockSpec((tm, PAIR), lambda i: (i, 0))
    gain = pl.BlockSpec((3, PAIR), lambda i: (0, 0))
    res = pl.pallas_call(
        body, name="prep_fwd", grid=(T // tm,),
        in_specs=[pl.BlockSpec((tm, AB_IN), lambda i: (i, 0)), gain, gain, tab, tab, tab],
        out_specs=[pm] * 15, out_shape=[_sds((NPAIR, T, PAIR), F32)] * 9 + [_sds((NPAIR, T, PAIR), BF16)] * 6,
        compiler_params=_cparams(1))(pab, qg, kg, *tabs)
    return res[0:9], res[9:15]


def _res_index(it, rate):
    window = NBACK * rate
    b = it // rate
    rho = it % rate
    start = b * window + rho
    startp = jnp.maximum(start - window, rho)
    kmin = jnp.where(b > 0, 0, NBACK)
    return start, startp, kmin


def _rows(start, rate):
    if rate == 1:
        return pl.ds(pl.multiple_of(start, NBACK), NBACK)
    return pl.ds(start, NBACK, stride=rate)


def _band_bias():
    qs = lax.broadcasted_iota(jnp.int32, (2 * NBACK, 2 * NBACK), 0)
    kj = lax.broadcasted_iota(jnp.int32, (2 * NBACK, 2 * NBACK), 1)
    dist = (qs & (NBACK - 1)) + NBACK - kj
    both = (dist >= 0) & (dist <= NBACK)
    return jnp.where(both, 0.0, NEG_INF), jnp.where(both & (kj >= NBACK), 0.0, NEG_INF)


def _attn_fwd_block(q, kcat, vcat, first, lo, biases):
    vcat1 = jnp.concatenate([vcat, jnp.ones((2 * NBACK, PAIR), BF16)], axis=1)
    q2 = jnp.concatenate([jnp.where(lo, q, 0.0), jnp.where(lo, 0.0, q)], axis=0).astype(BF16)
    s = lax.dot_general(q2, kcat, (((1,), (1,)), ((), ())), preferred_element_type=F32)
    s = s + jnp.where(first, biases[1], biases[0])
    m = jnp.max(s, axis=-1, keepdims=True)
    ol = jnp.dot(jnp.exp(s - m).astype(BF16), vcat1, preferred_element_type=F32)
    o2 = ol[:, 0:PAIR] / ol[:, PAIR:]
    ls = m + jnp.log(ol[:, PAIR:])
    return jnp.where(lo, o2[0:NBACK], o2[NBACK:]), jnp.where(lo, ls[0:NBACK], ls[NBACK:])


def _attn_fwd(qn, kn, v, rate, name, dep=None):
    if rate > MAX_ROW_STRIDE:
        return _attn_fwd_gathered(qn, kn, v, rate, name, dep)

    def body(q_ref, k_ref, v_ref, *rest):
        o_ref, l_ref = rest[-2:]
        lo = _lo_mask((NBACK, PAIR))
        biases = _band_bias()

        def step(it, carry):
            start, startp, kmin = _res_index(it, rate)
            q = q_ref[_rows(start, rate), :] * (HEAD ** -0.5)
            kcat = jnp.concatenate([k_ref[_rows(startp, rate), :], k_ref[_rows(start, rate), :]], axis=0).astype(BF16)
            vcat = jnp.concatenate([v_ref[_rows(startp, rate), :], v_ref[_rows(start, rate), :]], axis=0).astype(BF16)
            o, ls = _attn_fwd_block(q, kcat, vcat, kmin != 0, lo, biases)
            o_ref[_rows(start, rate), :] = o
            l_ref[_rows(start, rate), :] = ls
            return carry

        lax.fori_loop(0, T // NBACK, step, 0, unroll=4)

    pm = pl.BlockSpec((None, T, PAIR), lambda p: (p, 0, 0))
    in_specs, args = [pm, pm, pm], [qn, kn, v]
    if dep is not None:
        in_specs.append(HBM_SPEC)
        args.append(dep)
    return pl.pallas_call(
        body, name=name, grid=(NPAIR,), in_specs=in_specs, out_specs=[pm, pm],
        out_shape=[_sds((NPAIR, T, PAIR), F32)] * 2, compiler_params=_cparams(1))(*args)


def _attn_fwd_gathered(qn, kn, v, rate, name, dep):
    n = T // rate
    nblk = n // NBACK

    def body(q_hbm, k_hbm, v_hbm, *rest):
        o_hbm, l_hbm, qb, kb, vb, ob, lb, in_sem, out_sem = rest[-9:]
        p = pl.program_id(0)
        slot = p % 2

        def loads(pair, s):
            return [pltpu.make_async_copy(x.at[pair, :, r, :], buf.at[s, r], in_sem.at[3 * s + a])
                    for a, (x, buf) in enumerate(((q_hbm, qb), (k_hbm, kb), (v_hbm, vb))) for r in range(rate)]

        def stores(pair, s):
            return [pltpu.make_async_copy(buf.at[s, r], x.at[pair, :, r, :], out_sem.at[2 * s + a])
                    for a, (x, buf) in enumerate(((o_hbm, ob), (l_hbm, lb))) for r in range(rate)]

        @pl.when(p == 0)
        def _():
            for c in loads(0, 0):
                c.start()

        @pl.when(p + 1 < NPAIR)
        def _():
            for c in loads(p + 1, 1 - slot):
                c.start()

        for c in loads(p, slot):
            c.wait()

        @pl.when(p >= 2)
        def _():
            for c in stores(p - 2, slot):
                c.wait()

        lo = _lo_mask((NBACK, PAIR))
        biases = _band_bias()

        def step(it, carry):
            b, r = it % nblk, it // nblk
            cur = pl.ds(pl.multiple_of(b * NBACK, NBACK), NBACK)
            prev = pl.ds(pl.multiple_of(jnp.maximum(b - 1, 0) * NBACK, NBACK), NBACK)
            q = qb[slot, r, cur, :] * (HEAD ** -0.5)
            kcat = jnp.concatenate([kb[slot, r, prev, :], kb[slot, r, cur, :]], axis=0).astype(BF16)
            vcat = jnp.concatenate([vb[slot, r, prev, :], vb[slot, r, cur, :]], axis=0).astype(BF16)
            o, ls = _attn_fwd_block(q, kcat, vcat, b == 0, lo, biases)
            ob[slot, r, cur, :] = o
            lb[slot, r, cur, :] = ls
            return carry

        lax.fori_loop(0, T // NBACK, step, 0, unroll=4)

        for c in stores(p, slot):
            c.start()

        @pl.when(p == NPAIR - 1)
        def _():
            for c in stores(p - 1, 1 - slot) + stores(p, slot):
                c.wait()

    by_residue = lambda a: a.reshape(NPAIR, n, rate, PAIR)
    in_specs, args = [HBM_SPEC] * 3, [by_residue(qn), by_residue(kn), by_residue(v)]
    if dep is not None:
        in_specs.append(HBM_SPEC)
        args.append(dep)
    o, l = pl.pallas_call(
        body, name=name, grid=(NPAIR,), in_specs=in_specs, out_specs=[HBM_SPEC] * 2,
        out_shape=[_sds((NPAIR, n, rate, PAIR), F32)] * 2,
        scratch_shapes=[pltpu.VMEM((2, rate, n, PAIR), F32)] * 5
        + [pltpu.SemaphoreType.DMA((6,)), pltpu.SemaphoreType.DMA((4,))],
        compiler_params=_cparams(1))(*args)
    return o.reshape(NPAIR, T, PAIR), l.reshape(NPAIR, T, PAIR)


def _merge_fwd(cat_ab, outs, lses, tm=512):
    def body(cat_in, o0, o1, o2, l0, l1, l2, cat_ref, lse_ref):
        del cat_in
        for p in range(NPAIR):
            a0, a1, a2 = l0[p], l1[p], l2[p]
            m = jnp.maximum(jnp.maximum(a0, a1), a2)
            w0, w1, w2 = jnp.exp(a0 - m), jnp.exp(a1 - m), jnp.exp(a2 - m)
            s = w0 + w1 + w2
            b = (w0 * o0[p] + w1 * o1[p] + w2 * o2[p]) / s
            cat_ref[:, p * PAIR:(p + 1) * PAIR] = b.astype(BF16)
            lse_ref[p] = m + jnp.log(s)

    pm = pl.BlockSpec((NPAIR, tm, PAIR), lambda i: (0, i, 0))
    return pl.pallas_call(
        body, name="merge_fwd", grid=(T // tm,),
        in_specs=[pl.BlockSpec(memory_space=pl.ANY)] + [pm] * 6,
        out_specs=[pl.BlockSpec((tm, 512), lambda i: (i, 1)), pm],
        out_shape=[_sds((T, D), BF16), _sds((NPAIR, T, PAIR), F32)],
        input_output_aliases={0: 0}, compiler_params=_cparams(1))(cat_ab, *outs, *lses)


def _d_cat_ab(dxb, w_ab_out, cat, dep, tm=512):
    def epilogue(d, rows, side, outs):
        (b_ref,), (da_ref, dbp_ref, e_ref) = side, outs
        da_ref[rows, :] = d[:, 0:512].astype(BF16)
        lo = _lo_mask((EPI_ROWS, PAIR))
        for p in range(NPAIR):
            db = d[:, 512 + p * PAIR:512 + (p + 1) * PAIR]
            b = b_ref[rows, p * PAIR:(p + 1) * PAIR].astype(F32)
            dbp_ref[p, rows, :] = db
            e_ref[p, rows, :] = _seg_mean(db * b, lo) * float(HEAD)

    pm = pl.BlockSpec((NPAIR, tm, PAIR), lambda i: (0, i, 0))
    return _mm_nt_rows(
        dxb, w_ab_out, "mm_d_cat_ab", epilogue, [cat], [pl.BlockSpec((tm, 512), lambda i: (i, 1))],
        [pl.BlockSpec((tm, 512), lambda i: (i, 0)), pm, pm],
        [_sds((T, 512), BF16), _sds((NPAIR, T, PAIR), F32), _sds((NPAIR, T, PAIR), F32)], dep=dep, tm=tm)


def _attn_bwd(qn, kn, v, dbp, e, lse, rate, name):
    def body(q_ref, k_ref, v_ref, db_ref, e_ref, lse_ref, dq_ref, dk_ref, dv_ref):
        lo = _lo_mask((NBACK, PAIR))
        bias_all, bias_first = _band_bias()
        scale = HEAD ** -0.5
        nt = (((1,), (1,)), ((), ()))
        tn = (((0,), (0,)), ((), ()))
        window = NBACK * rate
        nblk = T // window

        def one(it, carry):
            dk_carry, dv_carry = carry
            rho = it // nblk
            b = it % nblk
            start = b * window + rho
            rq = _rows(start, rate)
            rp = _rows(jnp.maximum(start - window, rho), rate)
            q = q_ref[rq, :] * scale
            db = db_ref[rq, :]
            ev = e_ref[rq, :]
            ls = lse_ref[rq, :]
            kcat = jnp.concatenate([k_ref[rp, :], k_ref[rq, :]], axis=0).astype(BF16)
            vcat = jnp.concatenate([v_ref[rp, :], v_ref[rq, :]], axis=0).astype(BF16)
            q2 = jnp.concatenate([jnp.where(lo, q, 0.0), jnp.where(lo, 0.0, q)], axis=0).astype(BF16)
            db2 = jnp.concatenate([jnp.where(lo, db, 0.0), jnp.where(lo, 0.0, db)], axis=0).astype(BF16)
            ls2 = jnp.concatenate([ls[:, 0:1], ls[:, HEAD:HEAD + 1]], axis=0)
            ev2 = jnp.concatenate([ev[:, 0:1], ev[:, HEAD:HEAD + 1]], axis=0)
            s = lax.dot_general(q2, kcat, nt, preferred_element_type=F32)
            pt = jnp.exp(s + jnp.where(b > 0, bias_all, bias_first) - ls2)
            dp = lax.dot_general(db2, vcat, nt, preferred_element_type=F32)
            ds = (pt * (dp - ev2)).astype(BF16)
            dq2 = jnp.dot(ds, kcat, preferred_element_type=F32) * scale
            dkc = lax.dot_general(ds, q2, tn, preferred_element_type=F32)
            dvc = lax.dot_general(pt.astype(BF16), db2, tn, preferred_element_type=F32)
            dq_ref[rq, :] = jnp.where(lo, dq2[0:NBACK], dq2[NBACK:])
            dk_ref[rp, :] = dk_carry + dkc[0:NBACK]
            dk_ref[rq, :] = dkc[NBACK:]
            dv_ref[rp, :] = dv_carry + dvc[0:NBACK]
            dv_ref[rq, :] = dvc[NBACK:]
            return dkc[NBACK:], dvc[NBACK:]

        def step(i, carry):
            for u in range(ATTN_BWD_UNROLL):
                carry = one(i * ATTN_BWD_UNROLL + u, carry)
            return carry

        zero = jnp.zeros((NBACK, PAIR), F32)
        lax.fori_loop(0, T // NBACK // ATTN_BWD_UNROLL, step, (zero, zero))

    pm = pl.BlockSpec((None, T, PAIR), lambda p: (p, 0, 0))
    return pl.pallas_call(
        body, name=name, grid=(NPAIR,), in_specs=[pm] * 6, out_specs=[pm] * 3,
        out_shape=[_sds((NPAIR, T, PAIR), F32)] * 3, compiler_params=_cparams(1, 56))(qn, kn, v, dbp, e, lse)


def _ab_in_bwd(pab, dcat, sgu_g, sgu_b, sgu_w, sgu_bias3, qg, kg, tabs, dqkv, rinvs, tm=256):
    def body(p_ref, dcat_ref, g_ref, b_ref, w_ref, bias_ref, qg_ref, kg_ref, c_ref, s1_ref, s2_ref, *rest):
        dq_refs, rinv_refs = rest[0:9], rest[9:15]
        o_ref, dwm_ref, dbias_ref, dsg_ref, dsb_ref, dgain_ref = rest[15:]
        i = pl.program_id(0)

        @pl.when(i == 0)
        def _():
            dwm_ref[...] = jnp.zeros_like(dwm_ref)
            dbias_ref[...] = jnp.zeros_like(dbias_ref)
            dsg_ref[...] = jnp.zeros_like(dsg_ref)
            dsb_ref[...] = jnp.zeros_like(dsb_ref)
            dgain_ref[...] = jnp.zeros_like(dgain_ref)

        zu = p_ref[:, 0:512].astype(F32)
        zv = p_ref[:, 512:1024].astype(F32)
        u = _gelu(zu)
        v = _gelu(zv)
        mu = jnp.mean(v, axis=-1, keepdims=True)
        vc = v - mu
        rstd = lax.rsqrt(jnp.mean(vc * vc, axis=-1, keepdims=True) + EPS)
        xhat = vc * rstd
        vn = (xhat * g_ref[...] + b_ref[...]).astype(BF16)
        da = dcat_ref[...].astype(F32)
        tri = _tril_mask()
        du_parts = [[None] * 4 for _ in range(tm // 128)]
        dvn_parts = [[None] * 4 for _ in range(tm // 128)]
        for gi in range(4):
            wg = jnp.where(tri, w_ref[gi], 0.0).astype(BF16)
            bg = bias_ref[gi]
            for c in range(tm // 128):
                rs, cs = slice(c * 128, (c + 1) * 128), slice(gi * 128, (gi + 1) * 128)
                vblk = vn[rs, cs]
                mixed = jnp.dot(wg, vblk, preferred_element_type=F32) + bg
                dab = da[rs, cs]
                du_parts[c][gi] = dab * mixed
                dmixed = dab * u[rs, cs]
                dmb = dmixed.astype(BF16)
                dvn_parts[c][gi] = lax.dot_general(wg, dmb, (((0,), (0,)), ((), ())), preferred_element_type=F32)
                dwm = lax.dot_general(dmb, vblk, (((1,), (1,)), ((), ())), preferred_element_type=F32)
                dwm_ref[gi] += jnp.where(tri, dwm, 0.0)
                dbias_ref[gi] += dmixed
        du = jnp.concatenate([jnp.concatenate(r, axis=1) for r in du_parts], axis=0)
        dvn = jnp.concatenate([jnp.concatenate(r, axis=1) for r in dvn_parts], axis=0)
        dsg_ref[...] += jnp.sum(dvn * xhat, axis=0, keepdims=True)
        dsb_ref[...] += jnp.sum(dvn, axis=0, keepdims=True)
        dxh = dvn * g_ref[...]
        dv = rstd * (dxh - jnp.mean(dxh, axis=-1, keepdims=True)
                     - xhat * jnp.mean(dxh * xhat, axis=-1, keepdims=True))
        o_ref[:, 0:512] = (du * _gelu_grad(zu)).astype(BF16)
        o_ref[:, 512:1024] = (dv * _gelu_grad(zv)).astype(BF16)

        blocks = _head_blocks()
        c, s1, s2 = c_ref[...], s1_ref[...], s2_ref[...]
        for g in range(3):
            dq_ref, dk_ref, dv_ref = dq_refs[3 * g:3 * g + 3]
            for p in range(NPAIR):
                for which, gains, src in ((0, qg_ref, dq_ref), (1, kg_ref, dk_ref)):
                    col = (2 + 3 * which + g) * 512 + p * PAIR
                    xr = p_ref[:, col:col + PAIR].astype(F32)
                    rinv = rinv_refs[2 * g + which][p].astype(F32)
                    xh = xr * rinv
                    dn = _rope_t(src[p], c, s1, s2)
                    row = 2 * g + which
                    dgain_ref[row:row + 1, :] += jnp.sum(dn * xh, axis=0, keepdims=True)
                    dxh2 = dn * gains[g:g + 1, :]
                    dx = rinv * (dxh2 - xh * _seg_mean_mxu(dxh2 * xh, blocks))
                    o_ref[:, col:col + PAIR] = dx.astype(BF16)
                col = (8 + g) * 512 + p * PAIR
                o_ref[:, col:col + PAIR] = dv_ref[p].astype(BF16)

    pm = pl.BlockSpec((NPAIR, tm, PAIR), lambda i: (0, i, 0))
    tab = pl.BlockSpec((tm, PAIR), lambda i: (i, 0))
    gain = pl.BlockSpec((3, PAIR), lambda i: (0, 0))
    vec = pl.BlockSpec((1, 512), lambda i: (0, 0))
    full = pl.BlockSpec((tm, AB_IN), lambda i: (i, 0))
    w4 = pl.BlockSpec((4, 128, 128), lambda i: (0, 0, 0))
    return pl.pallas_call(
        body, name="ab_in_bwd", grid=(T // tm,),
        in_specs=[full, pl.BlockSpec((tm, 512), lambda i: (i, 0)), vec, vec, w4,
                  pl.BlockSpec((4, 128, 1), lambda i: (0, 0, 0)), gain, gain, tab, tab, tab] + [pm] * 15,
        out_specs=[full, w4, w4, vec, vec, pl.BlockSpec((8, PAIR), lambda i: (0, 0))],
        out_shape=[_sds((T, AB_IN), BF16), _sds((4, 128, 128), F32), _sds((4, 128, 128), F32),
                   _sds((1, 512), F32), _sds((1, 512), F32), _sds((8, PAIR), F32)],
        compiler_params=_cparams(1))(pab, dcat, sgu_g, sgu_b, sgu_w, sgu_bias3, qg, kg, *tabs, *dqkv, *rinvs)


def _ln_stats(x):
    mu = jnp.mean(x, axis=-1, keepdims=True)
    xc = x - mu
    rstd = lax.rsqrt(jnp.mean(xc * xc, axis=-1, keepdims=True) + EPS)
    return xc * rstd, rstd


CONV_RC = 64


def _shifted_copies(src, dst, tm):
    dst[0] = src[...]
    for b in range(1, 8):
        dst[b, 0:tm + HALO - 8, :] = src[pl.ds(b, tm + HALO - 8), :]


def _offsets_by_phase(first):
    groups = {}
    for o in range(first, first + CONV_C_TAPS):
        groups.setdefault(o % 8, []).append(o)
    return sorted(groups.items())


def _window(shifted, b8, base, offsets, lanes):
    rows = 8 * (max(offsets) // 8) + CONV_RC
    return shifted[b8, pl.ds(base, rows), lanes].reshape(rows // 8, 8, 128)


def _cd_fwd(pcd, cw, cb, lg, lb, dw, tm=512):
    per = tm // HALO

    def body(p_ref, h_ref, cw_ref, cb_ref, lg_ref, lb_ref, dw_ref, cat_ref, c0_ref, c1_ref, dd_ref, y_ref,
             buf, buf2, sb):
        i = pl.program_id(0)
        live = jnp.where(i > 0, 1.0, 0.0)
        a = p_ref[:, 0:512].astype(F32)
        gt = p_ref[:, 512:1024].astype(F32)
        gb = p_ref[:, 1024:1536].astype(F32)
        gc = p_ref[:, 1536:2048].astype(F32)
        hv = p_ref[:, 2048:2560].astype(F32)
        c0 = a * _sigmoid(gt)
        dd = gc * hv
        buf[0:HALO, :] = h_ref[:, 0:512].astype(F32) * _sigmoid(h_ref[:, 512:1024].astype(F32)) * live
        buf[HALO:, :] = c0
        buf2[0:HALO, :] = h_ref[:, 1536:2048].astype(F32) * h_ref[:, 2048:2560].astype(F32) * live
        buf2[HALO:, :] = dd
        c0_ref[...] = c0.astype(BF16)
        dd_ref[...] = dd.astype(BF16)
        _shifted_copies(buf, sb, tm)

        def conv_rows(r, carry):
            base = pl.multiple_of(r * CONV_RC, CONV_RC)
            for c in range(4):
                lanes = slice(c * 128, (c + 1) * 128)
                acc = jnp.broadcast_to(cb_ref[:, lanes], (CONV_RC // 8, 8, 128))
                for b8, offsets in _offsets_by_phase(HALO - (CONV_C_TAPS - 1)):
                    win = _window(sb, b8, base, offsets, lanes)
                    for o in offsets:
                        j = o - (HALO - (CONV_C_TAPS - 1))
                        acc = acc + cw_ref[8 * j:8 * j + 8, lanes] * win[o // 8:o // 8 + CONV_RC // 8]
                c1_ref[pl.ds(base, CONV_RC), lanes] = acc.reshape(CONV_RC, 128)
            return carry

        lax.fori_loop(0, tm // CONV_RC, conv_rows, 0)
        xhat, _ = _ln_stats(c1_ref[...])
        c2 = xhat * lg_ref[...] + lb_ref[...]
        y = jnp.zeros((tm, 512), F32)
        for j in range(CONV_D_TAPS):
            y = y + dw_ref[j:j + 1, :] * buf2[pl.ds(HALO - (CONV_D_TAPS - 1) + j, tm), :]
        cat_ref[:, 0:512] = (c2 * _sigmoid(c2)).astype(BF16)
        cat_ref[:, 512:1024] = (gb * y).astype(BF16)
        y_ref[...] = y.astype(BF16)

    half = pl.BlockSpec((tm, 512), lambda i: (i, 0))
    vec = pl.BlockSpec((1, 512), lambda i: (0, 0))
    return pl.pallas_call(
        body, name="cd_fwd", grid=(T // tm,),
        in_specs=[pl.BlockSpec((tm, CD_IN), lambda i: (i, 0)),
                  pl.BlockSpec((HALO, CD_IN), lambda i: (jnp.maximum(i * per - 1, 0), 0)),
                  pl.BlockSpec((8 * 32, 512), lambda i: (0, 0)), vec, vec, vec, pl.BlockSpec((8, 512), lambda i: (0, 0))],
        out_specs=[pl.BlockSpec((tm, D), lambda i: (i, 0)), half, half, half, half],
        out_shape=[_sds((T, D), BF16), _sds((T, 512), BF16), _sds((T, 512), F32), _sds((T, 512), BF16),
                   _sds((T, 512), BF16)],
        scratch_shapes=[pltpu.VMEM((HALO + tm, 512), F32), pltpu.VMEM((HALO + tm, 512), F32),
                        pltpu.VMEM((8, HALO + tm, 512), F32)],
        compiler_params=_cparams(1))(pcd, pcd, cw, cb, lg, lb, dw)


def _d_cat_cd(dxb, w_cd_out, c1, pcd, y, lg, lb, dep, tm=512):
    def epilogue(d, rows, side, outs):
        c1_ref, gb_ref, y_ref, lg_ref, lb_ref = side
        dc1_ref, dy3_ref, dgb_ref, dlg_ref, dlb_ref, dcb_ref = outs
        dc, ddo = d[:, 0:512], d[:, 512:1024]
        xhat, rstd = _ln_stats(c1_ref[rows, :])
        c2 = xhat * lg_ref[...] + lb_ref[...]
        sg = _sigmoid(c2)
        dc2 = dc * sg * (1.0 + c2 * (1.0 - sg))
        dlg_ref[...] += jnp.sum(dc2 * xhat, axis=0, keepdims=True)
        dlb_ref[...] += jnp.sum(dc2, axis=0, keepdims=True)
        dxh = dc2 * lg_ref[...]
        dc1 = rstd * (dxh - jnp.mean(dxh, axis=-1, keepdims=True)
                      - xhat * jnp.mean(dxh * xhat, axis=-1, keepdims=True))
        dcb_ref[...] += jnp.sum(dc1, axis=0, keepdims=True)
        dc1_ref[rows, :] = dc1
        dgb_ref[rows, :] = (ddo * y_ref[rows, :].astype(F32)).astype(BF16)
        dy3_ref[rows, :] = ddo * gb_ref[rows, :].astype(F32)

    half = pl.BlockSpec((tm, 512), lambda i: (i, 0))
    vec = pl.BlockSpec((1, 512), lambda i: (0, 0))
    return _mm_nt_rows(
        dxb, w_cd_out, "mm_d_cat_cd", epilogue, [c1, pcd, y, lg, lb],
        [half, pl.BlockSpec((tm, 512), lambda i: (i, 2)), half, vec, vec], [half, half, half, vec, vec, vec],
        [_sds((T, 512), F32), _sds((T, 512), F32), _sds((T, 512), BF16),
         _sds((1, 512), F32), _sds((1, 512), F32), _sds((1, 512), F32)], sums=(3, 4, 5), dep=dep, tm=tm)


def _cd_bwd_conv(pcd, dc1, dy3, c0, dd, dgb, cw8, dw, tm=256):
    per = tm // HALO
    nblk = T // tm
    last32 = T // HALO - 1

    def body(p_ref, dc1_ref, dc1n_ref, dy3_ref, dy3n_ref, c0_ref, dd_ref, dgb_ref, cw_ref, dw_ref,
             o_ref, dcw_ref, ddw_ref, dbuf, d3buf, sd, dc0_buf):
        i = pl.program_id(0)
        has_next = jnp.where(i < nblk - 1, 1.0, 0.0)

        @pl.when(i == 0)
        def _():
            dcw_ref[...] = jnp.zeros_like(dcw_ref)
            ddw_ref[...] = jnp.zeros_like(ddw_ref)

        dbuf[0:tm, :] = dc1_ref[...]
        dbuf[tm:, :] = dc1n_ref[...] * has_next
        d3buf[0:tm, :] = dy3_ref[...]
        d3buf[tm:, :] = dy3n_ref[...] * has_next
        _shifted_copies(dbuf, sd, tm)
        n_tiles = tm // CONV_RC

        phases = _offsets_by_phase(0)

        def dc0_rows(r, carry):
            base = pl.multiple_of(r * CONV_RC, CONV_RC)
            for c in range(4):
                lanes = slice(c * 128, (c + 1) * 128)
                acc = jnp.zeros((CONV_RC // 8, 8, 128), F32)
                for b8, offsets in phases:
                    win = _window(sd, b8, base, offsets, lanes)
                    for o in offsets:
                        j = CONV_C_TAPS - 1 - o
                        acc = acc + cw_ref[8 * j:8 * j + 8, lanes] * win[o // 8:o // 8 + CONV_RC // 8]
                dc0_buf[pl.ds(base, CONV_RC), lanes] = acc.reshape(CONV_RC, 128)
            return carry

        lax.fori_loop(0, n_tiles, dc0_rows, 0)

        for c in range(4):
            lanes = slice(c * 128, (c + 1) * 128)
            for b8, offsets in phases:
                def dw_rows(r, accs, lanes=lanes, b8=b8, offsets=offsets):
                    base = pl.multiple_of(r * CONV_RC, CONV_RC)
                    xin = c0_ref[pl.ds(base, CONV_RC), lanes].astype(F32).reshape(CONV_RC // 8, 8, 128)
                    win = _window(sd, b8, base, offsets, lanes)
                    return tuple(acc + jnp.sum(xin * win[o // 8:o // 8 + CONV_RC // 8], axis=0)
                                 for acc, o in zip(accs, offsets))

                accs = lax.fori_loop(0, n_tiles, dw_rows, tuple(jnp.zeros((8, 128), F32) for _ in offsets))
                for acc, o in zip(accs, offsets):
                    j = CONV_C_TAPS - 1 - o
                    dcw_ref[j:j + 1, lanes] += jnp.sum(acc, axis=0, keepdims=True)

        dc0 = dc0_buf[...]
        ddin = dd_ref[...].astype(F32)
        ddd = jnp.zeros((tm, 512), F32)
        for j in range(CONV_D_TAPS):
            dy_shift = d3buf[pl.ds(CONV_D_TAPS - 1 - j, tm), :]
            ddd = ddd + dw_ref[j:j + 1, :] * dy_shift
            ddw_ref[j:j + 1, :] += jnp.sum(ddin * dy_shift, axis=0, keepdims=True)

        a = p_ref[:, 0:512].astype(F32)
        gt = p_ref[:, 512:1024].astype(F32)
        gc = p_ref[:, 1536:2048].astype(F32)
        hv = p_ref[:, 2048:2560].astype(F32)
        sg = _sigmoid(gt)
        o_ref[:, 0:512] = (dc0 * sg).astype(BF16)
        o_ref[:, 512:1024] = (dc0 * a * sg * (1.0 - sg)).astype(BF16)
        o_ref[:, 1024:1536] = dgb_ref[...]
        o_ref[:, 1536:2048] = (ddd * hv).astype(BF16)
        o_ref[:, 2048:2560] = (ddd * gc).astype(BF16)

    half = pl.BlockSpec((tm, 512), lambda i: (i, 0))
    nxt = pl.BlockSpec((HALO, 512), lambda i: (jnp.minimum((i + 1) * per, last32), 0))
    full = pl.BlockSpec((tm, CD_IN), lambda i: (i, 0))
    return pl.pallas_call(
        body, name="cd_bwd_conv", grid=(nblk,),
        in_specs=[full, half, nxt, half, nxt, half, half, half,
                  pl.BlockSpec((8 * 32, 512), lambda i: (0, 0)), pl.BlockSpec((8, 512), lambda i: (0, 0))],
        out_specs=[full, pl.BlockSpec((32, 512), lambda i: (0, 0)), pl.BlockSpec((8, 512), lambda i: (0, 0))],
        out_shape=[_sds((T, CD_IN), BF16), _sds((32, 512), F32), _sds((8, 512), F32)],
        scratch_shapes=[pltpu.VMEM((tm + HALO, 512), F32), pltpu.VMEM((tm + HALO, 512), F32),
                        pltpu.VMEM((8, tm + HALO, 512), F32), pltpu.VMEM((tm, 512), F32)],
        compiler_params=_cparams(1))(pcd, dc1, dc1, dy3, dy3, c0, dd, dgb, cw8, dw)


def _local_step(x, tgt, W, fetch=None, on_grad=None):
    W = dict(W)
    if fetch is None:
        fetch = lambda stage, after: {}
    if on_grad is None:
        on_grad = lambda key, arr: None
    tabs = _rope_tables()
    qg = jnp.tile(W["q_norm_g"], (1, 2))
    kg = jnp.tile(W["k_norm_g"], (1, 2))
    bias3 = W["sgu_bias"].reshape(4, 128, 1)
    G = {}

    h0 = _rms_fwd(x, W["ab_norm_g"], "rms_fwd_ab", dep=W.get("dep_first"))
    W.update(fetch("ab_in", h0))
    pab = _mm_nt(h0, W["wt_ab_in"], "mm_ab_in", dep=W.get("dep0"))
    cat_ab = _mix_a_fwd(pab, W["sgu_norm_g"], W["sgu_norm_b"], W["sgu_w"], bias3)
    qkv, rinvs = _prep_fwd(pab, qg, kg, tabs)
    outs, lses = [], []
    for g, rate in enumerate(DIL_RATES):
        o, l = _attn_fwd(qkv[3 * g], qkv[3 * g + 1], qkv[3 * g + 2], rate, f"attn_fwd_{g}", dep=W.get(f"dep_attn{g}"))
        outs.append(o)
        lses.append(l)
        W.update(fetch(f"attn{g}", o))
    cat_ab, lse = _merge_fwd(cat_ab, outs, lses)
    W.update(fetch("ab_out", lse))
    x1, h1 = _mm_nn(cat_ab, W["w_ab_out"], "mm_ab_out", mode="rms", resid=x, gain=W["ffn_norm_g"][0:1])
    pf0, act0 = _ffn_in(h1, W["wt_ffn_in0"], "ffn_in0")
    W.update(fetch("ffn_down0", act0))
    x2, h2 = _mm_nn(act0, W["w_ffn_down0"], "mm_ffn_down0", mode="rms", resid=x1, gain=W["cd_norm_g"],
                    dep=W.get("dep_down0"))
    W.update(fetch("cd_in", h2))
    pcd = _mm_nt(h2, W["wt_cd_in"], "mm_cd_in")
    cw8 = jnp.repeat(W["conv_c_w32"], 8, axis=0)
    cat_cd, c0, c1, dd, yv = _cd_fwd(pcd, cw8, W["conv_c_b"], W["c_ln_g"], W["c_ln_b"], W["conv_d_w8"])
    x3, h3 = _mm_nn(cat_cd, W["w_cd_out"], "mm_cd_out", mode="rms", resid=x2, gain=W["ffn_norm_g"][1:2])
    pf1, act1 = _ffn_in(h3, W["wt_ffn_in1"], "ffn_in1")
    dy, dyb, loss_cols = _mm_nn(act1, W["w_ffn_down1"], "mm_ffn_down1", mode="loss", resid=x3, tgt=tgt)

    def ffn_bwd(xin, h, pf, act, dres, dresb, layer):
        G[f"w_ffn_down{layer}"] = _mm_tn(act, dresb, f"mm_g_ffn_down{layer}")
        dep = on_grad(f"w_ffn_down{layer}", G[f"w_ffn_down{layer}"])
        dpf = _ffn_dact(dresb, W[f"w_ffn_down{layer}"], pf, f"ffn_dact{layer}", dep=dep)
        G[f"wt_ffn_in{layer}"] = _mm_tn(dpf, h, f"mm_g_ffn_in{layer}")
        dep = on_grad(f"wt_ffn_in{layer}", G[f"wt_ffn_in{layer}"])
        dx, dxb, G[f"ffn_norm_g{layer}"] = _mm_dh_rms_bwd(
            dpf, W[f"wt_ffn_in{layer}"], xin, W["ffn_norm_g"][layer:layer + 1], dres, f"mm_d_h_ffn{layer}", dep=dep)
        return dx, dxb

    dx3, dx3b = ffn_bwd(x3, h3, pf1, act1, dy, dyb, 1)

    G["w_cd_out"] = _mm_tn(cat_cd, dx3b, "mm_g_cd_out")
    dep = on_grad("w_cd_out", G["w_cd_out"])
    dc1, dy3, dgb, G["c_ln_g"], G["c_ln_b"], G["conv_c_b"] = _d_cat_cd(
        dx3b, W["w_cd_out"], c1, pcd, yv, W["c_ln_g"], W["c_ln_b"], dep)
    dpcd, G["conv_c_w32"], G["conv_d_w8"] = _cd_bwd_conv(pcd, dc1, dy3, c0, dd, dgb, cw8, W["conv_d_w8"])
    G["wt_cd_in"] = _mm_tn(dpcd, h2, "mm_g_cd_in")
    dep = on_grad("wt_cd_in", G["wt_cd_in"])
    dx2, dx2b, G["cd_norm_g"] = _mm_dh_rms_bwd(dpcd, W["wt_cd_in"], x2, W["cd_norm_g"], dx3, "mm_d_h_cd", dep=dep)

    dx1, dx1b = ffn_bwd(x1, h1, pf0, act0, dx2, dx2b, 0)

    G["w_ab_out"] = _mm_tn(cat_ab, dx1b, "mm_g_ab_out")
    dep = on_grad("w_ab_out", G["w_ab_out"])
    dcat_a, dbp, e = _d_cat_ab(dx1b, W["w_ab_out"], cat_ab, dep)
    dqkv = []
    for g, rate in enumerate(DIL_RATES):
        dqkv += _attn_bwd(qkv[3 * g], qkv[3 * g + 1], qkv[3 * g + 2], dbp, e, lse, rate, f"attn_bwd_{g}")
    dpab, G["sgu_w"], dbias_part, G["sgu_norm_g"], G["sgu_norm_b"], dgain = _ab_in_bwd(
        pab, dcat_a, W["sgu_norm_g"], W["sgu_norm_b"], W["sgu_w"], bias3, qg, kg, tabs, dqkv, rinvs)
    G["sgu_bias"] = jnp.sum(dbias_part, axis=-1)
    dgain = dgain[0:6, 0:HEAD] + dgain[0:6, HEAD:PAIR]
    G["q_norm_g"] = dgain[0::2]
    G["k_norm_g"] = dgain[1::2]
    G["loss_cols"] = loss_cols
    dep = on_grad("small", G)
    G["wt_ab_in"] = _mm_tn(dpab, h0, "mm_g_ab_in", dep=dep)
    dep = on_grad("wt_ab_in", G["wt_ab_in"])
    grad_x, G["ab_norm_g"] = _mm_dh_rms_bwd(dpab, W["wt_ab_in"], x, W["ab_norm_g"], dx1, "mm_d_h_ab", dep=dep,
                                            bf16_copy=False)
    return loss_cols, grad_x, G


def _my_place():
    return lax.axis_index("x"), lax.axis_index("y"), lax.axis_index("c")


def _dev_index(px, py, pc):
    return 4 * px + 2 * py + pc


def _flip(place, k):
    x, y, c = place
    return (1 - x if k & 4 else x, 1 - y if k & 2 else y, 1 - c if k & 1 else c)


def _landing(shape, dtype, own):
    buf = lax.empty(shape, dtype)
    for lead, part in own:
        buf = lax.dynamic_update_slice(buf, part.reshape((1,) * len(lead) + part.shape),
                                       tuple(lead) + (0,) * part.ndim)
    return buf


HBM_ONLY = pl.BlockSpec(memory_space=pltpu.HBM)
SEM_SPEC = pl.BlockSpec(memory_space=pltpu.SEMAPHORE)
IN_FLIGHT = pltpu.CompilerParams(has_side_effects=pltpu.SideEffectType.DATAFLOW_SIDE_EFFECTING)


def _in_hbm(a):
    return pltpu.with_memory_space_constraint(a, pltpu.HBM)


def _exchange_start(name, srcs, lands, items, dep=None):
    ns, nl, ni = len(srcs), len(lands), len(items)

    def body(*refs):
        S, L = refs[0:ns], refs[ns:ns + nl]
        first_out = ns + nl + (0 if dep is None else 1)
        send_sems, recv_sems, token = refs[first_out], refs[first_out + 1], refs[-1]
        me = _my_place()
        mi = _dev_index(*me)
        for i, (src, dst) in enumerate(items):
            for k in range(1, NDEV):
                peer = _flip(me, k)
                pltpu.make_async_remote_copy(
                    src_ref=src(S, _dev_index(*peer)), dst_ref=dst(L, mi), send_sem=send_sems.at[7 * i + k - 1],
                    recv_sem=recv_sems.at[7 * i + k - 1], device_id=peer, device_id_type=MESH).start()
        token[...] = jnp.zeros_like(token)

    thru = [pltpu.HBM(a.shape, a.dtype) for a in list(srcs) + list(lands)]
    args = [_in_hbm(a) for a in srcs] + [_in_hbm(a) for a in lands]
    in_specs = [HBM_ONLY] * (ns + nl)
    if dep is not None:
        args.append(dep)
        in_specs.append(HBM_SPEC)
    outs = pl.pallas_call(
        body, name=name, in_specs=in_specs,
        out_shape=(pltpu.SemaphoreType.DMA((7 * ni,)), pltpu.SemaphoreType.DMA((7 * ni,)), *thru, _sds((8, 128), F32)),
        out_specs=(SEM_SPEC, SEM_SPEC, *[HBM_ONLY] * (ns + nl), pl.BlockSpec(memory_space=pltpu.VMEM)),
        input_output_aliases={j: 2 + j for j in range(ns + nl)}, compiler_params=IN_FLIGHT)(*args)
    return dict(send=outs[0], recv=outs[1], srcs=list(outs[2:2 + ns]), lands=list(outs[2 + ns:2 + ns + nl]),
                token=outs[-1], items=items)


def _exchange_wait(name, states, after):
    after = list(after) if isinstance(after, (list, tuple)) else [after]
    counts = [(len(st["srcs"]), len(st["lands"]), len(st["items"])) for st in states]
    n_arrays = sum(c[0] + c[1] for c in counts)

    def body(*refs):
        me = _my_place()
        mi = _dev_index(*me)
        pos = 0
        sem_pos = n_arrays
        for st, (ns, nl, ni) in zip(states, counts):
            S, L = refs[pos:pos + ns], refs[pos + ns:pos + ns + nl]
            send_sems, recv_sems = refs[sem_pos], refs[sem_pos + 1]
            pos += ns + nl
            sem_pos += 2
            for i, (src, dst) in enumerate(st["items"]):
                for k in range(1, NDEV):
                    cp = pltpu.make_async_remote_copy(
                        src_ref=src(S, mi), dst_ref=dst(L, mi), send_sem=send_sems.at[7 * i + k - 1],
                        recv_sem=recv_sems.at[7 * i + k - 1], device_id=me, device_id_type=MESH)
                    cp.wait_send()
                    cp.wait_recv()

    arrays, sems = [], []
    for st in states:
        arrays += st["srcs"] + st["lands"]
        sems += [st["send"], st["recv"]]
    outs = pl.pallas_call(
        body, name=name, in_specs=[HBM_ONLY] * n_arrays + [SEM_SPEC] * len(sems) + [HBM_SPEC] * len(after),
        out_shape=tuple(pltpu.HBM(a.shape, a.dtype) for a in arrays), out_specs=tuple([HBM_ONLY] * n_arrays),
        input_output_aliases={j: j for j in range(n_arrays)}, compiler_params=IN_FLIGHT)(*arrays, *sems, *after)
    lands, pos = [], 0
    for ns, nl, _ in counts:
        lands.append(list(outs[pos + ns:pos + ns + nl]))
        pos += ns + nl
    return lands


def _place_and_neighbours():
    x, y, c = _my_place()
    return (x, y, c), (x, y, 1 - c), [(1 - x, y), (x, 1 - y), (1 - x, 1 - y)]


def _gather_start(name, srcs, lands, items, dep=None):
    ns, nl, ni = len(srcs), len(lands), len(items)

    def body(*refs):
        S, L = refs[0:ns], refs[ns:ns + nl]
        first_out = ns + nl + (0 if dep is None else 1)
        send_sems, recv_sems, token = refs[first_out], refs[first_out + 1], refs[-1]
        me, sib, chips = _place_and_neighbours()
        mi = _dev_index(*me)
        for i, (src, dst) in enumerate(items):
            for k, to in enumerate([sib] + [(*chip, me[2]) for chip in chips]):
                pltpu.make_async_remote_copy(
                    src_ref=src(S), dst_ref=dst(L, mi), send_sem=send_sems.at[4 * i + k],
                    recv_sem=recv_sems.at[4 * i + k], device_id=to, device_id_type=MESH).start()
        token[...] = jnp.zeros_like(token)

    thru = [pltpu.HBM(a.shape, a.dtype) for a in list(srcs) + list(lands)]
    args = [_in_hbm(a) for a in srcs] + [_in_hbm(a) for a in lands]
    in_specs = [HBM_ONLY] * (ns + nl)
    if dep is not None:
        args.append(dep)
        in_specs.append(HBM_SPEC)
    outs = pl.pallas_call(
        body, name=name, in_specs=in_specs,
        out_shape=(pltpu.SemaphoreType.DMA((4 * ni,)), pltpu.SemaphoreType.DMA((4 * ni,)), *thru, _sds((8, 128), F32)),
        out_specs=(SEM_SPEC, SEM_SPEC, *[HBM_ONLY] * (ns + nl), pl.BlockSpec(memory_space=pltpu.VMEM)),
        input_output_aliases={j: 2 + j for j in range(ns + nl)}, compiler_params=IN_FLIGHT)(*args)
    return dict(send=outs[0], recv=outs[1], srcs=list(outs[2:2 + ns]), lands=list(outs[2 + ns:2 + ns + nl]),
                token=outs[-1], items=items)


def _gather_forward(name, st, after):
    nl, ni = len(st["lands"]), len(st["items"])

    def body(*refs):
        L, recv_sems = refs[0:nl], refs[nl]
        fwd_send, fwd_recv, token = refs[-3:]
        me, sib, chips = _place_and_neighbours()
        for i, (_, dst) in enumerate(st["items"]):
            for j, chip in enumerate(chips):
                blk = dst(L, _dev_index(*chip, me[2]))
                pltpu.make_async_remote_copy(
                    src_ref=blk, dst_ref=blk, send_sem=fwd_send.at[3 * i + j], recv_sem=recv_sems.at[4 * i + 1 + j],
                    device_id=me, device_id_type=MESH).wait_recv()
                pltpu.make_async_remote_copy(
                    src_ref=blk, dst_ref=blk, send_sem=fwd_send.at[3 * i + j], recv_sem=fwd_recv.at[3 * i + j],
                    device_id=sib, device_id_type=MESH).start()
        token[...] = jnp.zeros_like(token)

    after = list(after) if isinstance(after, (list, tuple)) else [after]
    outs = pl.pallas_call(
        body, name=name, in_specs=[HBM_ONLY] * nl + [SEM_SPEC] + [HBM_SPEC] * len(after),
        out_shape=(*[pltpu.HBM(a.shape, a.dtype) for a in st["lands"]], pltpu.SemaphoreType.DMA((3 * ni,)),
                   pltpu.SemaphoreType.DMA((3 * ni,)), _sds((8, 128), F32)),
        out_specs=(*[HBM_ONLY] * nl, SEM_SPEC, SEM_SPEC, pl.BlockSpec(memory_space=pltpu.VMEM)),
        input_output_aliases={j: j for j in range(nl)}, compiler_params=IN_FLIGHT)(*st["lands"], st["recv"], *after)
    return dict(st, lands=list(outs[0:nl]), fwd_send=outs[nl], fwd_recv=outs[nl + 1], token=outs[-1])


def _gather_wait(name, st, after):
    ns, nl, ni = len(st["srcs"]), len(st["lands"]), len(st["items"])

    def body(*refs):
        S, L = refs[0:ns], refs[ns:ns + nl]
        send_sems, recv_sems, fwd_send, fwd_recv = refs[ns + nl:ns + nl + 4]
        me, sib, chips = _place_and_neighbours()
        mi = _dev_index(*me)
        for i, (src, dst) in enumerate(st["items"]):
            mine = dst(L, mi)
            for k in range(4):
                pltpu.make_async_remote_copy(
                    src_ref=src(S), dst_ref=mine, send_sem=send_sems.at[4 * i + k], recv_sem=recv_sems.at[4 * i + k],
                    device_id=me, device_id_type=MESH).wait_send()
            pltpu.make_async_remote_copy(
                src_ref=src(S), dst_ref=mine, send_sem=send_sems.at[4 * i], recv_sem=recv_sems.at[4 * i],
                device_id=me, device_id_type=MESH).wait_recv()
            for j in range(3):
                cp = pltpu.make_async_remote_copy(
                    src_ref=mine, dst_ref=mine, send_sem=fwd_send.at[3 * i + j], recv_sem=fwd_recv.at[3 * i + j],
                    device_id=me, device_id_type=MESH)
                cp.wait_send()
                cp.wait_recv()

    arrays = st["srcs"] + st["lands"]
    outs = pl.pallas_call(
        body, name=name, in_specs=[HBM_ONLY] * (ns + nl) + [SEM_SPEC] * 4 + [HBM_SPEC],
        out_shape=tuple(pltpu.HBM(a.shape, a.dtype) for a in arrays), out_specs=tuple([HBM_ONLY] * (ns + nl)),
        input_output_aliases={j: j for j in range(ns + nl)},
        compiler_params=IN_FLIGHT)(*arrays, st["send"], st["recv"], st["fwd_send"], st["fwd_recv"], after)
    return list(outs[ns:ns + nl])


def _sum_slots(land):
    def body(l_ref, o_ref):
        acc = l_ref[0]
        for d in range(1, NDEV):
            acc = acc + l_ref[d]
        o_ref[...] = acc

    vm = pl.BlockSpec(memory_space=pltpu.VMEM)
    return pl.pallas_call(body, name="sum_small", out_shape=_sds(land.shape[1:], F32), in_specs=[vm], out_specs=vm)(land)


def _adam_math(w, g, m, v):
    m2 = ADAM_B1 * m + (1.0 - ADAM_B1) * g
    v2 = ADAM_B2 * v + (1.0 - ADAM_B2) * (g * g)
    delta = -ADAM_LR * ((m2 * ADAM_C1) / (jnp.sqrt(v2 * ADAM_C2) + ADAM_EPS) + ADAM_WD * w)
    return delta, m2, v2


def _adam_layer(land, sel, w, m, v, layer, name, prev=None, tc=512):
    R = land.shape[2]

    def body(l_ref, w_ref, m_ref, v_ref, *rest):
        g_out, d_out, m_out, v_out = rest[-4:]
        g = l_ref[0].astype(F32)
        for d in range(1, NDEV):
            g = g + l_ref[d].astype(F32)
        delta, m2, v2 = _adam_math(w_ref[...], g, m_ref[...], v_ref[...])
        g_out[...] = g
        d_out[...] = delta
        m_out[...] = m2
        v_out[...] = v2

    wspec = pl.BlockSpec((None, R, tc), lambda i: (layer, 0, i))
    in_specs = [pl.BlockSpec((None, NDEV, R, tc), lambda i: (sel, 0, 0, i)), wspec, wspec, wspec]
    args = [land, w, m, v]
    aliases = {}
    if prev is not None:
        in_specs += [HBM_SPEC] * 4
        args += list(prev)
        aliases = {4 + j: j for j in range(4)}
    return pl.pallas_call(
        body, name=name, grid=(D // tc,), in_specs=in_specs, out_specs=[wspec] * 4,
        out_shape=[_sds(w.shape, F32)] * 4, input_output_aliases=aliases, compiler_params=_cparams(1))(*args)


def _adam_stacked(lands, sel, w, m, v, name):
    res = None
    for layer, land in enumerate(lands):
        res = _adam_layer(land, sel, w, m, v, layer, f"{name}{layer}", prev=res)
    return res


def _adam_small(ws, gs, ms, vs):
    n = len(ws)

    def body(*refs):
        w_r, g_r, m_r, v_r = refs[0:n], refs[n:2 * n], refs[2 * n:3 * n], refs[3 * n:4 * n]
        d_o, m_o, v_o = refs[4 * n:5 * n], refs[5 * n:6 * n], refs[6 * n:7 * n]
        for i in range(n):
            delta, m2, v2 = _adam_math(w_r[i][...], g_r[i][...], m_r[i][...], v_r[i][...])
            d_o[i][...] = delta
            m_o[i][...] = m2
            v_o[i][...] = v2

    vm = pl.BlockSpec(memory_space=pltpu.VMEM)
    shapes = [_sds(w.shape, F32) for w in ws]
    outs = pl.pallas_call(body, name="adam_small", in_specs=[vm] * (4 * n), out_specs=[vm] * (3 * n),
                          out_shape=shapes * 3)(*ws, *gs, *ms, *vs)
    return outs[0:n], outs[n:2 * n], outs[2 * n:3 * n]


def _adam_of_slots(land, w, m, v, name):
    def body(l_ref, w_ref, m_ref, v_ref, g_o, d_o, m_o, v_o):
        g = l_ref[0]
        for d in range(1, NDEV):
            g = g + l_ref[d]
        g_o[...] = g
        d_o[...], m_o[...], v_o[...] = _adam_math(w_ref[...], g, m_ref[...], v_ref[...])

    vm = pl.BlockSpec(memory_space=pltpu.VMEM)
    return pl.pallas_call(body, name=name, in_specs=[vm] * 4, out_specs=[vm] * 4,
                          out_shape=[_sds(w.shape, F32)] * 4)(land, w, m, v)


WEIGHT_NAMES = ("ab_norm_g", "ab_w_in", "sgu_norm_g", "sgu_norm_b", "sgu_w", "sgu_bias", "q_norm_g", "k_norm_g",
                "ab_w_out", "cd_norm_g", "cd_w_in", "conv_c_w", "conv_c_b", "c_ln_g", "c_ln_b", "conv_d_w",
                "cd_w_out", "ffn_norm_g", "ffn_w_gate", "ffn_w_up", "ffn_w_down")
SMALL_2D = (("sgu_norm_g", (1, 512)), ("sgu_norm_b", (1, 512)), ("sgu_w", (512, 128)),
            ("sgu_bias", (4, 128)), ("q_norm_g", (3, 64)), ("k_norm_g", (3, 64)), ("cd_norm_g", (1, 128)),
            ("conv_c_w", (31, 64)), ("conv_c_b", (1, 64)), ("c_ln_g", (1, 64)), ("c_ln_b", (1, 64)),
            ("conv_d_w", (3, 64)), ("ffn_norm_g", (2, 1024)))
SHARD_C = 64


def _pack_rows(parts, rows):
    flat = jnp.concatenate([p.reshape(-1) for p in parts])
    return jnp.pad(flat, (0, rows * 128 - flat.shape[0])).reshape(rows, 128)


def kernel(x, ab_norm_g, ab_w_in, sgu_norm_g, sgu_norm_b, sgu_w, sgu_bias, q_norm_g, k_norm_g, ab_w_out, cd_norm_g, cd_w_in, conv_c_w, conv_c_b, c_ln_g, c_ln_b, conv_d_w, cd_w_out, ffn_norm_g, ffn_w_gate, ffn_w_up, ffn_w_down, loss_target, m_ab_norm_g, m_ab_w_in, m_sgu_norm_g, m_sgu_norm_b, m_sgu_w, m_sgu_bias, m_q_norm_g, m_k_norm_g, m_ab_w_out, m_cd_norm_g, m_cd_w_in, m_conv_c_w, m_conv_c_b, m_c_ln_g, m_c_ln_b, m_conv_d_w, m_cd_w_out, m_ffn_norm_g, m_ffn_w_gate, m_ffn_w_up, m_ffn_w_down, v_ab_norm_g, v_ab_w_in, v_sgu_norm_g, v_sgu_norm_b, v_sgu_w, v_sgu_bias, v_q_norm_g, v_k_norm_g, v_ab_w_out, v_cd_norm_g, v_cd_w_in, v_conv_c_w, v_conv_c_b, v_c_ln_g, v_c_ln_b, v_conv_d_w, v_cd_w_out, v_ffn_norm_g, v_ffn_w_gate, v_ffn_w_up, v_ffn_w_down):
    w = dict(zip(WEIGHT_NAMES, (ab_norm_g, ab_w_in, sgu_norm_g, sgu_norm_b, sgu_w, sgu_bias, q_norm_g, k_norm_g, ab_w_out, cd_norm_g, cd_w_in, conv_c_w, conv_c_b, c_ln_g, c_ln_b, conv_d_w, cd_w_out, ffn_norm_g, ffn_w_gate, ffn_w_up, ffn_w_down)))
    m = dict(zip(WEIGHT_NAMES, (m_ab_norm_g, m_ab_w_in, m_sgu_norm_g, m_sgu_norm_b, m_sgu_w, m_sgu_bias, m_q_norm_g, m_k_norm_g, m_ab_w_out, m_cd_norm_g, m_cd_w_in, m_conv_c_w, m_conv_c_b, m_c_ln_g, m_c_ln_b, m_conv_d_w, m_cd_w_out, m_ffn_norm_g, m_ffn_w_gate, m_ffn_w_up, m_ffn_w_down)))
    v = dict(zip(WEIGHT_NAMES, (v_ab_norm_g, v_ab_w_in, v_sgu_norm_g, v_sgu_norm_b, v_sgu_w, v_sgu_bias, v_q_norm_g, v_k_norm_g, v_ab_w_out, v_cd_norm_g, v_cd_w_in, v_conv_c_w, v_conv_c_b, v_c_ln_g, v_c_ln_b, v_conv_d_w, v_cd_w_out, v_ffn_norm_g, v_ffn_w_gate, v_ffn_w_up, v_ffn_w_down)))
    me = _dev_index(*_my_place())

    r_ff = DFF // NDEV
    one = lambda a: (lambda S, j: S[a])
    slot = lambda b: (lambda L, s: L[b].at[s])
    slot2 = lambda b, part: (lambda L, s: L[b].at[part, s])
    shard = lambda a: (lambda S: S[a])

    def later(a):
        return lax.optimization_barrier((a, gathers[0]["token"]))[0]

    def layer_shards(layer):
        return (later(w["ffn_w_gate"][layer]).T.astype(BF16), later(w["ffn_w_up"][layer]).T.astype(BF16),
                later(w["ffn_w_down"][layer]).astype(BF16))

    def gathered(own):
        return _landing((NDEV,) + own.shape, BF16, [((me,), own)])

    def gathered2(a, b):
        return _landing((2, NDEV) + a.shape, BF16, [((0, me), a), ((1, me), b)])

    ab_in_s = w["ab_w_in"][0].T.astype(BF16)
    gathers = {0: _gather_start("gather0_start", [ab_in_s], [gathered(ab_in_s)], [(shard(0), slot(0))])}

    def chan(flat, lo, taps):
        return flat[:, lo:lo + taps * SHARD_C].reshape(NDEV, taps, SHARD_C).transpose(1, 0, 2).reshape(taps, 512)

    def fetch(stage, after):
        if stage == "ab_in":
            ab_out_s = later(w["ab_w_out"][0]).astype(BF16)
            gate0, up0, down0 = layer_shards(0)
            small_s = _pack_rows([later(w[n]) for n in ("cd_norm_g", "conv_c_w", "conv_c_b", "c_ln_g", "c_ln_b",
                                                        "conv_d_w")], 24)
            lands1 = [gathered(ab_out_s), gathered2(gate0, up0), gathered(down0),
                      _landing((NDEV,) + small_s.shape, F32, [((me,), small_s)])]
            gathers[0] = _gather_forward("gather0_forward", gathers[0], [after] + lands1)
            l_ab_in, = _gather_wait("gather0_wait", gathers[0], gathers[0]["token"])
            gathers[1] = _gather_start(
                "gather1_start", [ab_out_s, gate0, up0, down0, small_s], lands1,
                [(shard(0), slot(0)), (shard(1), slot2(1, 0)), (shard(2), slot2(1, 1)), (shard(3), slot(2)),
                 (shard(4), slot(3))], dep=l_ab_in)
            return {"wt_ab_in": l_ab_in.reshape(AB_IN, D), "dep0": gathers[1]["token"]}
        if stage == "attn0":
            cd_in_s, cd_out_s = later(w["cd_w_in"][0]).T.astype(BF16), later(w["cd_w_out"][0]).astype(BF16)
            gate1, up1, down1 = layer_shards(1)
            gathers[2] = _gather_start(
                "gather2_start", [cd_in_s, cd_out_s, gate1, up1, down1],
                [gathered(cd_in_s), gathered(cd_out_s), gathered2(gate1, up1), gathered(down1)],
                [(shard(0), slot(0)), (shard(1), slot(1)), (shard(2), slot2(2, 0)), (shard(3), slot2(2, 1)),
                 (shard(4), slot(3))], dep=after)
            return {"dep_attn1": gathers[2]["token"]}
        if stage == "attn1":
            gathers[1] = _gather_forward("gather1_forward", gathers[1], after)
            return {"dep_attn2": gathers[1]["token"]}
        if stage == "ab_out":
            l_out, l_ffn, l_down, l_small = _gather_wait("gather1_wait", gathers[1], after)
            flat = l_small.reshape(NDEV, 24 * 128)
            return {
                "w_ab_out": l_out.reshape(D, D), "wt_ffn_in0": l_ffn.reshape(2 * DFF, D),
                "w_ffn_down0": l_down.reshape(DFF, D), "cd_norm_g": flat[:, 0:128].reshape(1, D),
                "conv_c_w32": jnp.pad(chan(flat, 128, CONV_C_TAPS), ((0, 1), (0, 0))),
                "conv_c_b": chan(flat, 2112, 1), "c_ln_g": chan(flat, 2176, 1), "c_ln_b": chan(flat, 2240, 1),
                "conv_d_w8": jnp.pad(chan(flat, 2304, CONV_D_TAPS), ((0, 8 - CONV_D_TAPS), (0, 0))),
            }
        if stage == "ffn_down0":
            gathers[2] = _gather_forward("gather2_forward", gathers[2], after)
            return {"dep_down0": gathers[2]["token"]}
        if stage == "cd_in":
            l_in, l_out, l_ffn, l_down = _gather_wait("gather2_wait", gathers[2], after)
            return {"wt_cd_in": l_in.reshape(CD_IN, D), "w_cd_out": l_out.reshape(D, D),
                    "wt_ffn_in1": l_ffn.reshape(2 * DFF, D), "w_ffn_down1": l_down.reshape(DFF, D)}
        return {}

    scatters = {}
    rides_with = {"w_ffn_down1": "wt_ffn_in1", "w_cd_out": "wt_cd_in", "w_ffn_down0": "wt_ffn_in0"}
    held = {}
    smalls = {}

    def small_exchange(name, block):
        land = _landing((NDEV,) + block.shape, F32, [((me,), block)])
        return _exchange_start(name, [block], [land], [(one(0), slot(0))])

    def on_grad(key, arr):
        if key == "small":
            parts = [arr["sgu_norm_g"], arr["sgu_norm_b"], arr["sgu_w"], arr["sgu_bias"], arr["q_norm_g"],
                     arr["k_norm_g"], arr["cd_norm_g"], arr["conv_c_w32"][:CONV_C_TAPS], arr["conv_c_b"], arr["c_ln_g"],
                     arr["c_ln_b"], arr["conv_d_w8"][:CONV_D_TAPS], arr["ffn_norm_g0"], arr["ffn_norm_g1"],
                     arr["loss_cols"]]
            smalls["sizes"] = [p.size for p in parts]
            rows = -(-sum(smalls["sizes"]) // 1024) * 8
            smalls["early"] = small_exchange("small_start", _pack_rows(parts, rows))
            return smalls["early"]["token"]
        if key in rides_with:
            held[rides_with[key]] = (key, arr)
            return None
        group = ([held.pop(key)] if key in held else []) + [(key, arr)]
        srcs, lands, items = [], [], []
        for n, (k, a) in enumerate(group):
            if k.startswith("wt_ffn_in"):
                src = a.reshape(2, NDEV, r_ff, D)
                own = lax.dynamic_slice_in_dim(src, me, 1, axis=1)
                lands.append(lax.dynamic_update_slice(lax.empty(src.shape, BF16), own, (0, me, 0, 0)))
                items += [((lambda S, j, n=n: S[n].at[0, j]), slot2(n, 0)), ((lambda S, j, n=n: S[n].at[1, j]), slot2(n, 1))]
            else:
                rows = a.shape[0] // NDEV
                src = a.reshape(NDEV, rows, D)
                own = lax.dynamic_index_in_dim(src, me, 0, keepdims=False)
                lands.append(_landing((1, NDEV, rows, D), BF16, [((0, me), own)]))
                items.append(((lambda S, j, n=n: S[n].at[j]), slot2(n, 0)))
            srcs.append(src)
        st = _exchange_start(f"scatter_{key}_start", srcs, lands, items)
        scatters[key] = (st, [k for k, _ in group])
        return st["token"]

    W = {
        "dep_first": gathers[0]["token"],
        "ab_norm_g": w["ab_norm_g"], "sgu_norm_g": w["sgu_norm_g"], "sgu_norm_b": w["sgu_norm_b"],
        "sgu_w": w["sgu_w"][0], "sgu_bias": w["sgu_bias"][0], "q_norm_g": w["q_norm_g"][0],
        "k_norm_g": w["k_norm_g"][0], "ffn_norm_g": w["ffn_norm_g"],
    }

    loss_cols, grad_x, G = _local_step(x[0], loss_target[0], W, fetch, on_grad)

    late_small = small_exchange("small_late_start", G["ab_norm_g"])
    landed = {}

    def wait_scatters(name, group_keys, others, after):
        res = _exchange_wait(name, [scatters[gk][0] for gk in group_keys] + others, after)
        for gk, lands in zip(group_keys, res):
            landed.update(zip(scatters[gk][1], lands))
        return [lands[0] for lands in res[len(group_keys):]]

    small_land, = wait_scatters("scatter_wait_early", ["wt_ffn_in1", "wt_cd_in", "wt_ffn_in0", "w_ab_out"],
                                [smalls["early"]], late_small["token"])

    grads, deltas, new_m, new_v = {}, {}, {}, {}
    done = []

    def put(name, res):
        grads[name], deltas[name], new_m[name], new_v[name] = res

    def adam(name, lands, sel, transposed):
        flip = (lambda a: jnp.swapaxes(a, 1, 2)) if transposed else (lambda a: a)
        res = _adam_stacked(lands, sel, flip(w[name]), flip(m[name]), flip(v[name]), f"adam_{name}")
        done.append(res[1])
        put(name, [flip(r) for r in res])

    ffn_in_lands = [landed["wt_ffn_in0"], landed["wt_ffn_in1"]]
    adam("cd_w_in", [landed["wt_cd_in"]], 0, True)
    adam("ffn_w_gate", ffn_in_lands, 0, True)
    adam("ffn_w_up", ffn_in_lands, 1, True)
    adam("cd_w_out", [landed["w_cd_out"]], 0, False)
    adam("ab_w_out", [landed["w_ab_out"]], 0, False)
    adam("ffn_w_down", [landed["w_ffn_down0"], landed["w_ffn_down1"]], 0, False)

    red = _sum_slots(small_land).reshape(-1)
    offs = [0]
    for s in smalls["sizes"]:
        offs.append(offs[-1] + s)
    seg = [red[offs[i]:offs[i + 1]] for i in range(len(smalls["sizes"]))]
    loss = jnp.sum(seg[14])

    def own_channels(full, taps):
        return lax.dynamic_slice_in_dim(full.reshape(taps, 512), me * SHARD_C, SHARD_C, axis=1)

    g_small = {
        "sgu_norm_g": seg[0].reshape(1, 512), "sgu_norm_b": seg[1].reshape(1, 512),
        "sgu_w": seg[2].reshape(512, 128), "sgu_bias": seg[3].reshape(4, 128), "q_norm_g": seg[4].reshape(3, 64),
        "k_norm_g": seg[5].reshape(3, 64),
        "cd_norm_g": lax.dynamic_slice_in_dim(seg[6].reshape(1, D), me * (D // NDEV), D // NDEV, axis=1),
        "conv_c_w": own_channels(seg[7], CONV_C_TAPS), "conv_c_b": own_channels(seg[8], 1),
        "c_ln_g": own_channels(seg[9], 1), "c_ln_b": own_channels(seg[10], 1),
        "conv_d_w": own_channels(seg[11], CONV_D_TAPS),
        "ffn_norm_g": jnp.concatenate([seg[12].reshape(1, D), seg[13].reshape(1, D)], axis=0),
    }

    d_s, m_s, v_s = _adam_small([w[n].reshape(s) for n, s in SMALL_2D], [g_small[n] for n, _ in SMALL_2D],
                                [m[n].reshape(s) for n, s in SMALL_2D], [v[n].reshape(s) for n, s in SMALL_2D])
    for i, (n, _) in enumerate(SMALL_2D):
        shape = w[n].shape
        grads[n], deltas[n] = g_small[n].reshape(shape), d_s[i].reshape(shape)
        new_m[n], new_v[n] = m_s[i].reshape(shape), v_s[i].reshape(shape)
    done.append(d_s[0])

    late_land, = wait_scatters("scatter_wait_last", ["wt_ab_in"], [late_small], list(done))
    put("ab_norm_g", _adam_of_slots(late_land, w["ab_norm_g"], m["ab_norm_g"], v["ab_norm_g"], "adam_ab_norm_g"))
    adam("ab_w_in", [landed["wt_ab_in"]], 0, True)

    return (loss, grad_x[None], *[grads[n] for n in WEIGHT_NAMES], *[deltas[n] for n in WEIGHT_NAMES],
            *[new_m[n] for n in WEIGHT_NAMES], *[new_v[n] for n in WEIGHT_NAMES])
```

```python
import jax
import jax.numpy as jnp
import numpy as np
from jax import lax
from jax.experimental import pallas as pl
from jax.experimental.pallas import tpu as pltpu

F32 = jnp.float32
BF16 = jnp.bfloat16

T = 4096
D = 1024
NDEV = 8
EPS = 1e-6
NEG_INF = -1e30
DFF = 2816
AB_IN = 5632
CD_IN = 2560
HEAD = 64
PAIR = 128
NPAIR = 4
NBACK = 128
DIL_RATES = (1, 4, 16)
ROPE_HALF = 8
ROPE_THETA = 500000.0
CONV_C_TAPS = 31
CONV_D_TAPS = 3
HALO = 32
ATTN_BWD_UNROLL = 4
MAX_ROW_STRIDE = 4

ADAM_LR = 0.001
ADAM_B1 = 0.9
ADAM_B2 = 0.999
ADAM_EPS = 1e-08
ADAM_WD = 0.01
ADAM_STEP = 10
ADAM_C1 = 1.0 / (1.0 - ADAM_B1 ** ADAM_STEP)
ADAM_C2 = 1.0 / (1.0 - ADAM_B2 ** ADAM_STEP)

VMEM_LIMIT_MB = 48
MESH = pl.DeviceIdType.MESH
HBM_SPEC = pl.BlockSpec(memory_space=pl.ANY)


def _cparams(ngrid, vmem_mb=VMEM_LIMIT_MB):
    return pltpu.CompilerParams(dimension_semantics=("arbitrary",) * ngrid,
                                vmem_limit_bytes=vmem_mb * 1024 * 1024)


def _pick(n, options):
    for o in options:
        if n % o == 0:
            return o
    raise ValueError(f"no tile for {n} in {options}")


def _sds(shape, dtype):
    return jax.ShapeDtypeStruct(shape, dtype)


def _sigmoid(x):
    return 1.0 / (1.0 + jnp.exp(-x))


def _sigmoid_bf16(x):
    return 0.5 * jnp.tanh(0.5 * x) + 0.5


def _gelu(z):
    return 0.5 * z * (1.0 + lax.erf(z * 0.7071067811865476))


def _gelu_grad(z):
    return 0.5 * (1.0 + lax.erf(z * 0.7071067811865476)) + z * jnp.exp(-0.5 * z * z) * 0.3989422804014327


def _mm_nt(a, wt, name, out_dtype=BF16, dep=None):
    M, K = a.shape
    N = wt.shape[0]
    tn = _pick(N, (512, 256))

    def body(a_ref, w_ref, *rest):
        o_ref = rest[-1]
        for r0 in range(0, M, 1024):
            o_ref[r0:r0 + 1024, :] = lax.dot_general(
                a_ref[r0:r0 + 1024, :], w_ref[...], (((1,), (1,)), ((), ())),
                preferred_element_type=F32).astype(o_ref.dtype)

    in_specs = [pl.BlockSpec((M, K), lambda j: (0, 0), pipeline_mode=pl.Buffered(1)),
                pl.BlockSpec((tn, K), lambda j: (j, 0))]
    args = [a, wt]
    if dep is not None:
        in_specs.append(HBM_SPEC)
        args.append(dep)
    return pl.pallas_call(
        body, name=name, grid=(N // tn,), in_specs=in_specs, out_specs=pl.BlockSpec((M, tn), lambda j: (0, j)),
        out_shape=_sds((M, N), out_dtype), compiler_params=_cparams(1))(*args)


EPI_ROWS = 256


def _mm_nt_rows(a, wt, name, epilogue, side, side_specs, out_specs, out_shape, sums=(), dep=None, tm=512):
    M, K = a.shape
    N = wt.shape[0]
    ns, no = len(side), len(out_shape)

    def body(a_ref, w_ref, *rest):
        side_refs, outs, acc = rest[0:ns], rest[-1 - no:-1], rest[-1]
        acc[...] = lax.dot_general(a_ref[...], w_ref[...], (((1,), (1,)), ((), ())), preferred_element_type=F32)

        @pl.when(pl.program_id(0) == 0)
        def _():
            for j in sums:
                outs[j][...] = jnp.zeros_like(outs[j])

        for r0 in range(0, tm, EPI_ROWS):
            rows = slice(r0, r0 + EPI_ROWS)
            epilogue(acc[rows, :].astype(BF16).astype(F32), rows, side_refs, outs)

    in_specs = [pl.BlockSpec((tm, K), lambda i: (i, 0)),
                pl.BlockSpec((N, K), lambda i: (0, 0), pipeline_mode=pl.Buffered(1))] + list(side_specs)
    args = [a, wt, *side]
    if dep is not None:
        in_specs.append(HBM_SPEC)
        args.append(dep)
    return pl.pallas_call(
        body, name=name, grid=(M // tm,), in_specs=in_specs, out_specs=list(out_specs), out_shape=list(out_shape),
        scratch_shapes=[pltpu.VMEM((tm, N), F32)], compiler_params=_cparams(1))(*args)


def _mm_nn(a, w, name, mode, resid, gain=None, tgt=None, dep=None, tm=512):
    M, K = a.shape
    N = w.shape[1]
    side = gain if mode == "rms" else tgt

    def body(a_ref, w_ref, resid_ref, side_ref, *rest):
        outs, acc = rest[-3 if mode == "rms" else -4:-1], rest[-1]
        i = pl.program_id(0)
        acc[...] = jnp.dot(a_ref[...], w_ref[...], preferred_element_type=F32)

        if mode == "loss":
            @pl.when(i == 0)
            def _():
                outs[2][...] = jnp.zeros_like(outs[2])

        for r0 in range(0, tm, EPI_ROWS):
            rows = slice(r0, r0 + EPI_ROWS)
            v = acc[rows, :] + resid_ref[rows, :]
            if mode == "rms":
                outs[0][rows, :] = v
                r = lax.rsqrt(jnp.mean(v * v, axis=-1, keepdims=True) + EPS)
                outs[1][rows, :] = (v * r * side_ref[...]).astype(BF16)
            else:
                d = v - side_ref[rows, :]
                outs[2][...] += jnp.sum(d * d, axis=0, keepdims=True) * (0.5 / N)
                dy = d * (1.0 / N)
                outs[0][rows, :] = dy
                outs[1][rows, :] = dy.astype(BF16)

    row = pl.BlockSpec((tm, N), lambda i: (i, 0))
    vec = pl.BlockSpec((1, N), lambda i: (0, 0))
    in_specs = [pl.BlockSpec((tm, K), lambda i: (i, 0)),
                pl.BlockSpec((K, N), lambda i: (0, 0), pipeline_mode=pl.Buffered(1)), row,
                vec if mode == "rms" else row]
    args = [a, w, resid, side]
    if dep is not None:
        in_specs.append(HBM_SPEC)
        args.append(dep)
    if mode == "rms":
        out_specs, out_shape = [row, row], [_sds((M, N), F32), _sds((M, N), BF16)]
    else:
        out_specs, out_shape = [row, row, vec], [_sds((M, N), F32), _sds((M, N), BF16), _sds((1, N), F32)]
    return pl.pallas_call(
        body, name=name, grid=(M // tm,), in_specs=in_specs, out_specs=out_specs, out_shape=out_shape,
        scratch_shapes=[pltpu.VMEM((tm, N), F32)], compiler_params=_cparams(1))(*args)


def _mm_dh_rms_bwd(a, w, x, gain, dres, name, dep=None, tm=512, bf16_copy=True):
    parts = a.shape[0] if a.ndim == 3 else 1
    M, Kp = a.shape[-2], a.shape[-1]
    N = w.shape[1]
    nblk = M // tm
    assert nblk % 2 == 0

    def body(a_ref, w_ref, x_ref, g_ref, dres_ref, *rest):
        dg_ref, acc0, acc1 = rest[-3:]
        dx_ref = rest[-5] if bf16_copy else rest[-4]
        dxb_ref = rest[-4] if bf16_copy else None
        i = pl.program_id(0)

        def matmul(acc):
            if parts == 1:
                acc[...] = jnp.dot(a_ref[...], w_ref[...], preferred_element_type=F32)
            else:
                d = jnp.dot(a_ref[0], w_ref[0:Kp, :], preferred_element_type=F32)
                for p in range(1, parts):
                    d = d + jnp.dot(a_ref[p], w_ref[p * Kp:(p + 1) * Kp, :], preferred_element_type=F32)
                acc[...] = d

        def finish(acc):
            for r0 in range(0, tm, EPI_ROWS // 2):
                rows = slice(r0, r0 + EPI_ROWS // 2)
                v = acc[rows, :]
                xf = x_ref[rows, :]
                r = lax.rsqrt(jnp.mean(xf * xf, axis=-1, keepdims=True) + EPS)
                xhat = xf * r
                dg_ref[...] += jnp.sum(v * xhat, axis=0, keepdims=True)
                dxh = v * g_ref[...]
                tot = dres_ref[rows, :] + r * (dxh - xhat * jnp.mean(dxh * xhat, axis=-1, keepdims=True))
                dx_ref[rows, :] = tot
                if bf16_copy:
                    dxb_ref[rows, :] = tot.astype(BF16)

        @pl.when(i == 0)
        def _():
            dg_ref[...] = jnp.zeros_like(dg_ref)
            matmul(acc0)

        @pl.when((i > 0) & (i < nblk) & (i % 2 == 1))
        def _():
            matmul(acc1)
            finish(acc0)

        @pl.when((i > 0) & (i < nblk) & (i % 2 == 0))
        def _():
            matmul(acc0)
            finish(acc1)

        @pl.when(i == nblk)
        def _():
            finish(acc1)

    last = nblk - 1
    row = pl.BlockSpec((tm, N), lambda i: (jnp.maximum(i - 1, 0), 0))
    vec = pl.BlockSpec((1, N), lambda i: (0, 0))
    if a.ndim == 3:
        a_spec = pl.BlockSpec((parts, tm, Kp), lambda i: (0, jnp.minimum(i, last), 0))
    else:
        a_spec = pl.BlockSpec((tm, Kp), lambda i: (jnp.minimum(i, last), 0))
    w_spec = pl.BlockSpec((parts * Kp, N), lambda i: (0, 0), pipeline_mode=pl.Buffered(1))
    in_specs = [a_spec, w_spec, row, vec, row]
    args = [a, w, x, gain, dres]
    if dep is not None:
        in_specs.append(HBM_SPEC)
        args.append(dep)
    return pl.pallas_call(
        body, name=name, grid=(nblk + 1,), in_specs=in_specs,
        out_specs=[row, row, vec] if bf16_copy else [row, vec],
        out_shape=([_sds((M, N), F32), _sds((M, N), BF16), _sds((1, N), F32)] if bf16_copy
                   else [_sds((M, N), F32), _sds((1, N), F32)]),
        scratch_shapes=[pltpu.VMEM((tm, N), F32), pltpu.VMEM((tm, N), F32)], compiler_params=_cparams(1, 56))(*args)


def _mm_tn(a, b, name, out_dtype=BF16, tt=2048, dep=None):
    parts = a.shape[0] if a.ndim == 3 else 1
    Tt, Mp = a.shape[-2], a.shape[-1]
    N = b.shape[1]
    tn = _pick(Mp, (1408, 1280, 1024, 512))
    jper = Mp // tn
    nt = Tt // tt

    def body(a_ref, b_ref, *rest):
        o_ref, acc = rest[-2:]
        t = pl.program_id(1)

        @pl.when(t == 0)
        def _():
            acc[...] = jnp.zeros_like(acc)

        rows = pl.ds(pl.multiple_of(t * tt, tt), tt)
        acc[...] += lax.dot_general(a_ref[...], b_ref[rows, :], (((0,), (0,)), ((), ())),
                                    preferred_element_type=F32)

        @pl.when(t == nt - 1)
        def _():
            o_ref[...] = acc[...].astype(o_ref.dtype)

    if a.ndim == 3:
        a_spec = pl.BlockSpec((None, tt, tn), lambda j, t: (j // jper, t, j % jper))
    else:
        a_spec = pl.BlockSpec((tt, tn), lambda j, t: (t, j))
    in_specs = [a_spec, pl.BlockSpec((Tt, N), lambda j, t: (0, 0), pipeline_mode=pl.Buffered(1))]
    args = [a, b]
    if dep is not None:
        in_specs.append(HBM_SPEC)
        args.append(dep)
    return pl.pallas_call(
        body, name=name, grid=(parts * jper, nt), in_specs=in_specs,
        out_specs=pl.BlockSpec((tn, N), lambda j, t: (j, 0)),
        out_shape=_sds((parts * Mp, N), out_dtype), scratch_shapes=[pltpu.VMEM((tn, N), F32)],
        compiler_params=_cparams(2))(*args)


FFN_ROWS = 2048


def _ffn_in(h, wt_in, name, tn=256):
    nj = DFF // tn

    def body(h_ref, wg_ref, wu_ref, p_ref, act_ref):
        nt = (((1,), (1,)), ((), ()))
        for r0 in range(0, T, FFN_ROWS):
            rows = slice(r0, r0 + FFN_ROWS)
            g = lax.dot_general(h_ref[rows, :], wg_ref[...], nt, preferred_element_type=F32).astype(BF16)
            u = lax.dot_general(h_ref[rows, :], wu_ref[...], nt, preferred_element_type=F32).astype(BF16)
            p_ref[0, rows, :] = g
            p_ref[1, rows, :] = u
            act_ref[rows, :] = g * _sigmoid_bf16(g) * u

    return pl.pallas_call(
        body, name=name, grid=(nj,),
        in_specs=[pl.BlockSpec((T, D), lambda j: (0, 0), pipeline_mode=pl.Buffered(1)),
                  pl.BlockSpec((tn, D), lambda j: (j, 0)), pl.BlockSpec((tn, D), lambda j: (j + nj, 0))],
        out_specs=[pl.BlockSpec((2, T, tn), lambda j: (0, 0, j)), pl.BlockSpec((T, tn), lambda j: (0, j))],
        out_shape=[_sds((2, T, DFF), BF16), _sds((T, DFF), BF16)], compiler_params=_cparams(1))(h, wt_in, wt_in)


def _ffn_dact(dyb, w_down, p3, name, tn=256, dep=None):
    def body(dy_ref, w_ref, p_ref, *rest):
        o_ref = rest[-1]
        for r0 in range(0, T, FFN_ROWS):
            rows = slice(r0, r0 + FFN_ROWS)
            da = lax.dot_general(dy_ref[rows, :], w_ref[...], (((1,), (1,)), ((), ())),
                                 preferred_element_type=F32).astype(BF16)
            g = p_ref[0, rows, :]
            u = p_ref[1, rows, :]
            sg = _sigmoid_bf16(g)
            gs = g * sg
            o_ref[0, rows, :] = (da * u) * (sg + gs * (1.0 - sg))
            o_ref[1, rows, :] = da * gs

    pspec = pl.BlockSpec((2, T, tn), lambda j: (0, 0, j))
    in_specs = [pl.BlockSpec((T, D), lambda j: (0, 0), pipeline_mode=pl.Buffered(1)),
                pl.BlockSpec((tn, D), lambda j: (j, 0)), pspec]
    args = [dyb, w_down, p3]
    if dep is not None:
        in_specs.append(HBM_SPEC)
        args.append(dep)
    return pl.pallas_call(
        body, name=name, grid=(DFF // tn,), in_specs=in_specs, out_specs=pspec,
        out_shape=_sds((2, T, DFF), BF16), compiler_params=_cparams(1))(*args)


def _rms_fwd(x, g, name, tm=512, dep=None):
    def body(x_ref, g_ref, *rest):
        h_ref = rest[-1]
        xf = x_ref[...]
        r = lax.rsqrt(jnp.mean(xf * xf, axis=-1, keepdims=True) + EPS)
        h_ref[...] = (xf * r * g_ref[...]).astype(BF16)

    in_specs = [pl.BlockSpec((tm, D), lambda i: (i, 0)), pl.BlockSpec((1, D), lambda i: (0, 0))]
    args = [x, g]
    if dep is not None:
        in_specs.append(HBM_SPEC)
        args.append(dep)
    return pl.pallas_call(
        body, name=name, grid=(T // tm,), in_specs=in_specs, out_specs=pl.BlockSpec((tm, D), lambda i: (i, 0)),
        out_shape=_sds((T, D), BF16), compiler_params=_cparams(1))(*args)


def _tril_mask():
    r = lax.broadcasted_iota(jnp.int32, (128, 128), 0)
    c = lax.broadcasted_iota(jnp.int32, (128, 128), 1)
    return r >= c


def _mix_a_fwd(pab, sgu_g, sgu_b, sgu_w, sgu_bias3, tm=512):
    def body(zu_ref, zv_ref, g_ref, b_ref, w_ref, bias_ref, o_ref):
        u = _gelu(zu_ref[...].astype(F32))
        v = _gelu(zv_ref[...].astype(F32))
        mu = jnp.mean(v, axis=-1, keepdims=True)
        vc = v - mu
        rstd = lax.rsqrt(jnp.mean(vc * vc, axis=-1, keepdims=True) + EPS)
        vn = (vc * rstd * g_ref[...] + b_ref[...]).astype(BF16)
        tri = _tril_mask()
        for gi in range(4):
            wg = jnp.where(tri, w_ref[gi], 0.0).astype(BF16)
            bg = bias_ref[gi]
            for c in range(tm // 128):
                rs, cs = slice(c * 128, (c + 1) * 128), slice(gi * 128, (gi + 1) * 128)
                mixed = jnp.dot(wg, vn[rs, cs], preferred_element_type=F32) + bg
                o_ref[rs, cs] = (u[rs, cs] * mixed).astype(BF16)

    half = pl.BlockSpec((tm, 512), lambda i: (i, 0))
    return pl.pallas_call(
        body, name="mix_a_fwd", grid=(T // tm,),
        in_specs=[half, pl.BlockSpec((tm, 512), lambda i: (i, 1)),
                  pl.BlockSpec((1, 512), lambda i: (0, 0)), pl.BlockSpec((1, 512), lambda i: (0, 0)),
                  pl.BlockSpec((4, 128, 128), lambda i: (0, 0, 0)), pl.BlockSpec((4, 128, 1), lambda i: (0, 0, 0))],
        out_specs=half, out_shape=_sds((T, D), BF16), compiler_params=_cparams(1),
    )(pab, pab, sgu_g, sgu_b, sgu_w, sgu_bias3)


def _rope_tables():
    pos = np.arange(T, dtype=np.float32)
    inv_freq = np.float32(ROPE_THETA) ** (-np.arange(ROPE_HALF, dtype=np.float32) * np.float32(2.0 / (2 * ROPE_HALF)))
    ang = (pos[:, None] * inv_freq[None, :]).astype(np.float32)
    cos, sin = np.cos(ang), np.sin(ang)
    z8 = np.zeros((T, ROPE_HALF), np.float32)
    rest = np.zeros((T, HEAD - 2 * ROPE_HALF), np.float32)
    c64 = np.concatenate([cos, cos, rest + 1.0], axis=1)
    s1 = np.concatenate([z8, sin, rest], axis=1)
    s2 = np.concatenate([-sin, z8, rest], axis=1)
    return tuple(jnp.asarray(np.tile(t, (1, 2)).astype(np.float32)) for t in (c64, s1, s2))


def _lo_mask(shape):
    return lax.broadcasted_iota(jnp.int32, shape, 1) < HEAD


def _seg_mean(x, lo):
    s_all = jnp.sum(x, axis=-1, keepdims=True)
    s_lo = jnp.sum(jnp.where(lo, x, 0.0), axis=-1, keepdims=True)
    return jnp.where(lo, s_lo, s_all - s_lo) * (1.0 / HEAD)


def _head_blocks():
    r = lax.broadcasted_iota(jnp.int32, (PAIR, PAIR), 0) < HEAD
    c = lax.broadcasted_iota(jnp.int32, (PAIR, PAIR), 1) < HEAD
    return jnp.where(r == c, 1.0, 0.0).astype(BF16)


def _seg_mean_mxu(x, blocks):
    return jnp.dot(x.astype(BF16), blocks, preferred_element_type=F32) * (1.0 / HEAD)


def _rope(n, c, s1, s2):
    return n * c + pltpu.roll(n, ROPE_HALF, 1) * s1 + pltpu.roll(n, PAIR - ROPE_HALF, 1) * s2


def _rope_t(dy, c, s1, s2):
    return dy * c - pltpu.roll(dy, PAIR - ROPE_HALF, 1) * s2 - pltpu.roll(dy, ROPE_HALF, 1) * s1


def _prep_fwd(pab, qg, kg, tabs, tm=512):
    def body(p_ref, qg_ref, kg_ref, c_ref, s1_ref, s2_ref, *outs):
        blocks = _head_blocks()
        c, s1, s2 = c_ref[...], s1_ref[...], s2_ref[...]
        for g in range(3):
            qn_ref, kn_ref, v_ref = outs[3 * g:3 * g + 3]
            for p in range(NPAIR):
                for which, gains, dst in ((0, qg_ref, qn_ref), (1, kg_ref, kn_ref)):
                    col = (2 + 3 * which + g) * 512 + p * PAIR
                    xr = p_ref[:, col:col + PAIR].astype(F32)
                    rinv = lax.rsqrt(_seg_mean_mxu(xr * xr, blocks) + EPS)
                    outs[9 + 2 * g + which][p] = rinv.astype(BF16)
                    dst[p] = _rope(xr * rinv * gains[g:g + 1, :], c, s1, s2)
                col = (8 + g) * 512 + p * PAIR
                v_ref[p] = p_ref[:, col:col + PAIR].astype(F32)

    pm = pl.BlockSpec((NPAIR, tm, PAIR), lambda i: (0, i, 0))
    tab = pl.BlockSpec((tm, PAIR), lambda i: (i, 0))
    gain = pl.BlockSpec((3, PAIR), lambda i: (0, 0))
    res = pl.pallas_call(
        body, name="prep_fwd", grid=(T // tm,),
        in_specs=[pl.BlockSpec((tm, AB_IN), lambda i: (i, 0)), gain, gain, tab, tab, tab],
        out_specs=[pm] * 15, out_shape=[_sds((NPAIR, T, PAIR), F32)] * 9 + [_sds((NPAIR, T, PAIR), BF16)] * 6,
        compiler_params=_cparams(1))(pab, qg, kg, *tabs)
    return res[0:9], res[9:15]


def _res_index(it, rate):
    window = NBACK * rate
    b = it // rate
    rho = it % rate
    start = b * window + rho
    startp = jnp.maximum(start - window, rho)
    kmin = jnp.where(b > 0, 0, NBACK)
    return start, startp, kmin


def _rows(start, rate):
    if rate == 1:
        return pl.ds(pl.multiple_of(start, NBACK), NBACK)
    return pl.ds(start, NBACK, stride=rate)


def _band_bias():
    qs = lax.broadcasted_iota(jnp.int32, (2 * NBACK, 2 * NBACK), 0)
    kj = lax.broadcasted_iota(jnp.int32, (2 * NBACK, 2 * NBACK), 1)
    dist = (qs & (NBACK - 1)) + NBACK - kj
    both = (dist >= 0) & (dist <= NBACK)
    return jnp.where(both, 0.0, NEG_INF), jnp.where(both & (kj >= NBACK), 0.0, NEG_INF)


def _attn_fwd_block(q, kcat, vcat, first, lo, biases):
    vcat1 = jnp.concatenate([vcat, jnp.ones((2 * NBACK, PAIR), BF16)], axis=1)
    q2 = jnp.concatenate([jnp.where(lo, q, 0.0), jnp.where(lo, 0.0, q)], axis=0).astype(BF16)
    s = lax.dot_general(q2, kcat, (((1,), (1,)), ((), ())), preferred_element_type=F32)
    s = s + jnp.where(first, biases[1], biases[0])
    m = jnp.max(s, axis=-1, keepdims=True)
    ol = jnp.dot(jnp.exp(s - m).astype(BF16), vcat1, preferred_element_type=F32)
    o2 = ol[:, 0:PAIR] / ol[:, PAIR:]
    ls = m + jnp.log(ol[:, PAIR:])
    return jnp.where(lo, o2[0:NBACK], o2[NBACK:]), jnp.where(lo, ls[0:NBACK], ls[NBACK:])


def _attn_fwd(qn, kn, v, rate, name, dep=None):
    if rate > MAX_ROW_STRIDE:
        return _attn_fwd_gathered(qn, kn, v, rate, name, dep)

    def body(q_ref, k_ref, v_ref, *rest):
        o_ref, l_ref = rest[-2:]
        lo = _lo_mask((NBACK, PAIR))
        biases = _band_bias()

        def step(it, carry):
            start, startp, kmin = _res_index(it, rate)
            q = q_ref[_rows(start, rate), :] * (HEAD ** -0.5)
            kcat = jnp.concatenate([k_ref[_rows(startp, rate), :], k_ref[_rows(start, rate), :]], axis=0).astype(BF16)
            vcat = jnp.concatenate([v_ref[_rows(startp, rate), :], v_ref[_rows(start, rate), :]], axis=0).astype(BF16)
            o, ls = _attn_fwd_block(q, kcat, vcat, kmin != 0, lo, biases)
            o_ref[_rows(start, rate), :] = o
            l_ref[_rows(start, rate), :] = ls
            return carry

        lax.fori_loop(0, T // NBACK, step, 0, unroll=4)

    pm = pl.BlockSpec((None, T, PAIR), lambda p: (p, 0, 0))
    in_specs, args = [pm, pm, pm], [qn, kn, v]
    if dep is not None:
        in_specs.append(HBM_SPEC)
        args.append(dep)
    return pl.pallas_call(
        body, name=name, grid=(NPAIR,), in_specs=in_specs, out_specs=[pm, pm],
        out_shape=[_sds((NPAIR, T, PAIR), F32)] * 2, compiler_params=_cparams(1))(*args)


def _attn_fwd_gathered(qn, kn, v, rate, name, dep):
    n = T // rate
    nblk = n // NBACK

    def body(q_hbm, k_hbm, v_hbm, *rest):
        o_hbm, l_hbm, qb, kb, vb, ob, lb, in_sem, out_sem = rest[-9:]
        p = pl.program_id(0)
        slot = p % 2

        def loads(pair, s):
            return [pltpu.make_async_copy(x.at[pair, :, r, :], buf.at[s, r], in_sem.at[3 * s + a])
                    for a, (x, buf) in enumerate(((q_hbm, qb), (k_hbm, kb), (v_hbm, vb))) for r in range(rate)]

        def stores(pair, s):
            return [pltpu.make_async_copy(buf.at[s, r], x.at[pair, :, r, :], out_sem.at[2 * s + a])
                    for a, (x, buf) in enumerate(((o_hbm, ob), (l_hbm, lb))) for r in range(rate)]

        @pl.when(p == 0)
        def _():
            for c in loads(0, 0):
                c.start()

        @pl.when(p + 1 < NPAIR)
        def _():
            for c in loads(p + 1, 1 - slot):
                c.start()

        for c in loads(p, slot):
            c.wait()

        @pl.when(p >= 2)
        def _():
            for c in stores(p - 2, slot):
                c.wait()

        lo = _lo_mask((NBACK, PAIR))
        biases = _band_bias()

        def step(it, carry):
            b, r = it % nblk, it // nblk
            cur = pl.ds(pl.multiple_of(b * NBACK, NBACK), NBACK)
            prev = pl.ds(pl.multiple_of(jnp.maximum(b - 1, 0) * NBACK, NBACK), NBACK)
            q = qb[slot, r, cur, :] * (HEAD ** -0.5)
            kcat = jnp.concatenate([kb[slot, r, prev, :], kb[slot, r, cur, :]], axis=0).astype(BF16)
            vcat = jnp.concatenate([vb[slot, r, prev, :], vb[slot, r, cur, :]], axis=0).astype(BF16)
            o, ls = _attn_fwd_block(q, kcat, vcat, b == 0, lo, biases)
            ob[slot, r, cur, :] = o
            lb[slot, r, cur, :] = ls
            return carry

        lax.fori_loop(0, T // NBACK, step, 0, unroll=4)

        for c in stores(p, slot):
            c.start()

        @pl.when(p == NPAIR - 1)
        def _():
            for c in stores(p - 1, 1 - slot) + stores(p, slot):
                c.wait()

    by_residue = lambda a: a.reshape(NPAIR, n, rate, PAIR)
    in_specs, args = [HBM_SPEC] * 3, [by_residue(qn), by_residue(kn), by_residue(v)]
    if dep is not None:
        in_specs.append(HBM_SPEC)
        args.append(dep)
    o, l = pl.pallas_call(
        body, name=name, grid=(NPAIR,), in_specs=in_specs, out_specs=[HBM_SPEC] * 2,
        out_shape=[_sds((NPAIR, n, rate, PAIR), F32)] * 2,
        scratch_shapes=[pltpu.VMEM((2, rate, n, PAIR), F32)] * 5
        + [pltpu.SemaphoreType.DMA((6,)), pltpu.SemaphoreType.DMA((4,))],
        compiler_params=_cparams(1))(*args)
    return o.reshape(NPAIR, T, PAIR), l.reshape(NPAIR, T, PAIR)


def _merge_fwd(cat_ab, outs, lses, tm=512):
    def body(cat_in, o0, o1, o2, l0, l1, l2, cat_ref, lse_ref):
        del cat_in
        for p in range(NPAIR):
            a0, a1, a2 = l0[p], l1[p], l2[p]
            m = jnp.maximum(jnp.maximum(a0, a1), a2)
            w0, w1, w2 = jnp.exp(a0 - m), jnp.exp(a1 - m), jnp.exp(a2 - m)
            s = w0 + w1 + w2
            b = (w0 * o0[p] + w1 * o1[p] + w2 * o2[p]) / s
            cat_ref[:, p * PAIR:(p + 1) * PAIR] = b.astype(BF16)
            lse_ref[p] = m + jnp.log(s)

    pm = pl.BlockSpec((NPAIR, tm, PAIR), lambda i: (0, i, 0))
    return pl.pallas_call(
        body, name="merge_fwd", grid=(T // tm,),
        in_specs=[pl.BlockSpec(memory_space=pl.ANY)] + [pm] * 6,
        out_specs=[pl.BlockSpec((tm, 512), lambda i: (i, 1)), pm],
        out_shape=[_sds((T, D), BF16), _sds((NPAIR, T, PAIR), F32)],
        input_output_aliases={0: 0}, compiler_params=_cparams(1))(cat_ab, *outs, *lses)


def _d_cat_ab(dxb, w_ab_out, cat, dep, tm=512):
    def epilogue(d, rows, side, outs):
        (b_ref,), (da_ref, dbp_ref, e_ref) = side, outs
        da_ref[rows, :] = d[:, 0:512].astype(BF16)
        lo = _lo_mask((EPI_ROWS, PAIR))
        for p in range(NPAIR):
            db = d[:, 512 + p * PAIR:512 + (p + 1) * PAIR]
            b = b_ref[rows, p * PAIR:(p + 1) * PAIR].astype(F32)
            dbp_ref[p, rows, :] = db
            e_ref[p, rows, :] = _seg_mean(db * b, lo) * float(HEAD)

    pm = pl.BlockSpec((NPAIR, tm, PAIR), lambda i: (0, i, 0))
    return _mm_nt_rows(
        dxb, w_ab_out, "mm_d_cat_ab", epilogue, [cat], [pl.BlockSpec((tm, 512), lambda i: (i, 1))],
        [pl.BlockSpec((tm, 512), lambda i: (i, 0)), pm, pm],
        [_sds((T, 512), BF16), _sds((NPAIR, T, PAIR), F32), _sds((NPAIR, T, PAIR), F32)], dep=dep, tm=tm)


def _attn_bwd_block(q, db, ev, ls, kcat, vcat, first, lo, biases):
    scale = HEAD ** -0.5
    nt = (((1,), (1,)), ((), ()))
    tn = (((0,), (0,)), ((), ()))
    q = q * scale
    q2 = jnp.concatenate([jnp.where(lo, q, 0.0), jnp.where(lo, 0.0, q)], axis=0).astype(BF16)
    db2 = jnp.concatenate([jnp.where(lo, db, 0.0), jnp.where(lo, 0.0, db)], axis=0).astype(BF16)
    ls2 = jnp.concatenate([ls[:, 0:1], ls[:, HEAD:HEAD + 1]], axis=0)
    ev2 = jnp.concatenate([ev[:, 0:1], ev[:, HEAD:HEAD + 1]], axis=0)
    s = lax.dot_general(q2, kcat, nt, preferred_element_type=F32)
    pt = jnp.exp(s + jnp.where(first, biases[1], biases[0]) - ls2)
    dp = lax.dot_general(db2, vcat, nt, preferred_element_type=F32)
    ds = (pt * (dp - ev2)).astype(BF16)
    dq2 = jnp.dot(ds, kcat, preferred_element_type=F32) * scale
    dkc = lax.dot_general(ds, q2, tn, preferred_element_type=F32)
    dvc = lax.dot_general(pt.astype(BF16), db2, tn, preferred_element_type=F32)
    return jnp.where(lo, dq2[0:NBACK], dq2[NBACK:]), dkc, dvc


def _attn_bwd_loop(read, write, nblk):
    lo = _lo_mask((NBACK, PAIR))
    biases = _band_bias()

    def one(it, carry):
        dk_carry, dv_carry = carry
        rho = it // nblk
        b = it % nblk
        bp = jnp.maximum(b - 1, 0)
        kcat = jnp.concatenate([read(1, rho, bp), read(1, rho, b)], axis=0).astype(BF16)
        vcat = jnp.concatenate([read(2, rho, bp), read(2, rho, b)], axis=0).astype(BF16)
        dq, dkc, dvc = _attn_bwd_block(read(0, rho, b), read(3, rho, b), read(4, rho, b), read(5, rho, b), kcat, vcat,
                                       b == 0, lo, biases)
        write(0, rho, b, dq)
        write(1, rho, bp, dk_carry + dkc[0:NBACK])
        write(1, rho, b, dkc[NBACK:])
        write(2, rho, bp, dv_carry + dvc[0:NBACK])
        write(2, rho, b, dvc[NBACK:])
        return dkc[NBACK:], dvc[NBACK:]

    def step(i, carry):
        for u in range(ATTN_BWD_UNROLL):
            carry = one(i * ATTN_BWD_UNROLL + u, carry)
        return carry

    zero = jnp.zeros((NBACK, PAIR), F32)
    lax.fori_loop(0, T // NBACK // ATTN_BWD_UNROLL, step, (zero, zero))


def _attn_bwd(qn, kn, v, dbp, e, lse, rate, name):
    if rate > MAX_ROW_STRIDE:
        return _attn_bwd_gathered(qn, kn, v, dbp, e, lse, rate, name)
    window = NBACK * rate

    def body(*refs):
        rows = lambda rho, b: _rows(b * window + rho, rate)

        def write(j, rho, b, value):
            refs[6 + j][rows(rho, b), :] = value

        _attn_bwd_loop(lambda j, rho, b: refs[j][rows(rho, b), :], write, T // window)

    pm = pl.BlockSpec((None, T, PAIR), lambda p: (p, 0, 0))
    return pl.pallas_call(
        body, name=name, grid=(NPAIR,), in_specs=[pm] * 6, out_specs=[pm] * 3,
        out_shape=[_sds((NPAIR, T, PAIR), F32)] * 3, compiler_params=_cparams(1, 56))(qn, kn, v, dbp, e, lse)


def _attn_bwd_gathered(qn, kn, v, dbp, e, lse, rate, name):
    n = T // rate

    def body(*refs):
        ins, outs, in_bufs, out_bufs, (in_sem, out_sem) = refs[0:6], refs[6:9], refs[9:15], refs[15:18], refs[18:20]
        p = pl.program_id(0)
        slot = p % 2

        def loads(pair, s):
            return [pltpu.make_async_copy(x.at[pair, :, r, :], buf.at[s, r], in_sem.at[6 * s + a])
                    for a, (x, buf) in enumerate(zip(ins, in_bufs)) for r in range(rate)]

        def stores(pair, s):
            return [pltpu.make_async_copy(buf.at[s, r], x.at[pair, :, r, :], out_sem.at[3 * s + a])
                    for a, (x, buf) in enumerate(zip(outs, out_bufs)) for r in range(rate)]

        @pl.when(p == 0)
        def _():
            for c in loads(0, 0):
                c.start()

        @pl.when(p + 1 < NPAIR)
        def _():
            for c in loads(p + 1, 1 - slot):
                c.start()

        for c in loads(p, slot):
            c.wait()

        @pl.when(p >= 2)
        def _():
            for c in stores(p - 2, slot):
                c.wait()

        rows = lambda b: pl.ds(pl.multiple_of(b * NBACK, NBACK), NBACK)

        def write(j, rho, b, value):
            out_bufs[j][slot, rho, rows(b), :] = value

        _attn_bwd_loop(lambda j, rho, b: in_bufs[j][slot, rho, rows(b), :], write, n // NBACK)

        for c in stores(p, slot):
            c.start()

        @pl.when(p == NPAIR - 1)
        def _():
            for c in stores(p - 1, 1 - slot) + stores(p, slot):
                c.wait()

    by_residue = lambda a: a.reshape(NPAIR, n, rate, PAIR)
    res = pl.pallas_call(
        body, name=name, grid=(NPAIR,), in_specs=[HBM_SPEC] * 6, out_specs=[HBM_SPEC] * 3,
        out_shape=[_sds((NPAIR, n, rate, PAIR), F32)] * 3,
        scratch_shapes=[pltpu.VMEM((2, rate, n, PAIR), F32)] * 9
        + [pltpu.SemaphoreType.DMA((12,)), pltpu.SemaphoreType.DMA((6,))],
        compiler_params=_cparams(1, 56))(*[by_residue(a) for a in (qn, kn, v, dbp, e, lse)])
    return [r.reshape(NPAIR, T, PAIR) for r in res]


def _ab_in_bwd(pab, dcat, sgu_g, sgu_b, sgu_w, sgu_bias3, qg, kg, tabs, dqkv, rinvs, tm=256):
    def body(p_ref, dcat_ref, g_ref, b_ref, w_ref, bias_ref, qg_ref, kg_ref, c_ref, s1_ref, s2_ref, *rest):
        dq_refs, rinv_refs = rest[0:9], rest[9:15]
        o_ref, dwm_ref, dbias_ref, dsg_ref, dsb_ref, dgain_ref = rest[15:]
        i = pl.program_id(0)

        @pl.when(i == 0)
        def _():
            dwm_ref[...] = jnp.zeros_like(dwm_ref)
            dbias_ref[...] = jnp.zeros_like(dbias_ref)
            dsg_ref[...] = jnp.zeros_like(dsg_ref)
            dsb_ref[...] = jnp.zeros_like(dsb_ref)
            dgain_ref[...] = jnp.zeros_like(dgain_ref)

        zu = p_ref[:, 0:512].astype(F32)
        zv = p_ref[:, 512:1024].astype(F32)
        u = _gelu(zu)
        v = _gelu(zv)
        mu = jnp.mean(v, axis=-1, keepdims=True)
        vc = v - mu
        rstd = lax.rsqrt(jnp.mean(vc * vc, axis=-1, keepdims=True) + EPS)
        xhat = vc * rstd
        vn = (xhat * g_ref[...] + b_ref[...]).astype(BF16)
        da = dcat_ref[...].astype(F32)
        tri = _tril_mask()
        du_parts = [[None] * 4 for _ in range(tm // 128)]
        dvn_parts = [[None] * 4 for _ in range(tm // 128)]
        for gi in range(4):
            wg = jnp.where(tri, w_ref[gi], 0.0).astype(BF16)
            bg = bias_ref[gi]
            for c in range(tm // 128):
                rs, cs = slice(c * 128, (c + 1) * 128), slice(gi * 128, (gi + 1) * 128)
                vblk = vn[rs, cs]
                mixed = jnp.dot(wg, vblk, preferred_element_type=F32) + bg
                dab = da[rs, cs]
                du_parts[c][gi] = dab * mixed
                dmixed = dab * u[rs, cs]
                dmb = dmixed.astype(BF16)
                dvn_parts[c][gi] = lax.dot_general(wg, dmb, (((0,), (0,)), ((), ())), preferred_element_type=F32)
                dwm = lax.dot_general(dmb, vblk, (((1,), (1,)), ((), ())), preferred_element_type=F32)
                dwm_ref[gi] += jnp.where(tri, dwm, 0.0)
                dbias_ref[gi] += dmixed
        du = jnp.concatenate([jnp.concatenate(r, axis=1) for r in du_parts], axis=0)
        dvn = jnp.concatenate([jnp.concatenate(r, axis=1) for r in dvn_parts], axis=0)
        dsg_ref[...] += jnp.sum(dvn * xhat, axis=0, keepdims=True)
        dsb_ref[...] += jnp.sum(dvn, axis=0, keepdims=True)
        dxh = dvn * g_ref[...]
        dv = rstd * (dxh - jnp.mean(dxh, axis=-1, keepdims=True)
                     - xhat * jnp.mean(dxh * xhat, axis=-1, keepdims=True))
        o_ref[:, 0:512] = (du * _gelu_grad(zu)).astype(BF16)
        o_ref[:, 512:1024] = (dv * _gelu_grad(zv)).astype(BF16)

        blocks = _head_blocks()
        c, s1, s2 = c_ref[...], s1_ref[...], s2_ref[...]
        for g in range(3):
            dq_ref, dk_ref, dv_ref = dq_refs[3 * g:3 * g + 3]
            for p in range(NPAIR):
                for which, gains, src in ((0, qg_ref, dq_ref), (1, kg_ref, dk_ref)):
                    col = (2 + 3 * which + g) * 512 + p * PAIR
                    xr = p_ref[:, col:col + PAIR].astype(F32)
                    rinv = rinv_refs[2 * g + which][p].astype(F32)
                    xh = xr * rinv
                    dn = _rope_t(src[p], c, s1, s2)
                    row = 2 * g + which
                    dgain_ref[row:row + 1, :] += jnp.sum(dn * xh, axis=0, keepdims=True)
                    dxh2 = dn * gains[g:g + 1, :]
                    dx = rinv * (dxh2 - xh * _seg_mean_mxu(dxh2 * xh, blocks))
                    o_ref[:, col:col + PAIR] = dx.astype(BF16)
                col = (8 + g) * 512 + p * PAIR
                o_ref[:, col:col + PAIR] = dv_ref[p].astype(BF16)

    pm = pl.BlockSpec((NPAIR, tm, PAIR), lambda i: (0, i, 0))
    tab = pl.BlockSpec((tm, PAIR), lambda i: (i, 0))
    gain = pl.BlockSpec((3, PAIR), lambda i: (0, 0))
    vec = pl.BlockSpec((1, 512), lambda i: (0, 0))
    full = pl.BlockSpec((tm, AB_IN), lambda i: (i, 0))
    w4 = pl.BlockSpec((4, 128, 128), lambda i: (0, 0, 0))
    return pl.pallas_call(
        body, name="ab_in_bwd", grid=(T // tm,),
        in_specs=[full, pl.BlockSpec((tm, 512), lambda i: (i, 0)), vec, vec, w4,
                  pl.BlockSpec((4, 128, 1), lambda i: (0, 0, 0)), gain, gain, tab, tab, tab] + [pm] * 15,
        out_specs=[full, w4, w4, vec, vec, pl.BlockSpec((8, PAIR), lambda i: (0, 0))],
        out_shape=[_sds((T, AB_IN), BF16), _sds((4, 128, 128), F32), _sds((4, 128, 128), F32),
                   _sds((1, 512), F32), _sds((1, 512), F32), _sds((8, PAIR), F32)],
        compiler_params=_cparams(1))(pab, dcat, sgu_g, sgu_b, sgu_w, sgu_bias3, qg, kg, *tabs, *dqkv, *rinvs)


def _ln_stats(x):
    mu = jnp.mean(x, axis=-1, keepdims=True)
    xc = x - mu
    rstd = lax.rsqrt(jnp.mean(xc * xc, axis=-1, keepdims=True) + EPS)
    return xc * rstd, rstd


CONV_RC = 64


def _shifted_copies(src, dst, tm):
    dst[0] = src[...]
    for b in range(1, 8):
        dst[b, 0:tm + HALO - 8, :] = src[pl.ds(b, tm + HALO - 8), :]


def _offsets_by_phase(first):
    groups = {}
    for o in range(first, first + CONV_C_TAPS):
        groups.setdefault(o % 8, []).append(o)
    return sorted(groups.items())


def _window(shifted, b8, base, offsets, lanes):
    rows = 8 * (max(offsets) // 8) + CONV_RC
    return shifted[b8, pl.ds(base, rows), lanes].reshape(rows // 8, 8, 128)


def _cd_fwd(pcd, cw, cb, lg, lb, dw, tm=512):
    per = tm // HALO

    def body(p_ref, h_ref, cw_ref, cb_ref, lg_ref, lb_ref, dw_ref, cat_ref, c0_ref, c1_ref, dd_ref, y_ref,
             buf, buf2, sb):
        i = pl.program_id(0)
        live = jnp.where(i > 0, 1.0, 0.0)
        a = p_ref[:, 0:512].astype(F32)
        gt = p_ref[:, 512:1024].astype(F32)
        gb = p_ref[:, 1024:1536].astype(F32)
        gc = p_ref[:, 1536:2048].astype(F32)
        hv = p_ref[:, 2048:2560].astype(F32)
        c0 = a * _sigmoid(gt)
        dd = gc * hv
        buf[0:HALO, :] = h_ref[:, 0:512].astype(F32) * _sigmoid(h_ref[:, 512:1024].astype(F32)) * live
        buf[HALO:, :] = c0
        buf2[0:HALO, :] = h_ref[:, 1536:2048].astype(F32) * h_ref[:, 2048:2560].astype(F32) * live
        buf2[HALO:, :] = dd
        c0_ref[...] = c0.astype(BF16)
        dd_ref[...] = dd.astype(BF16)
        _shifted_copies(buf, sb, tm)

        def conv_rows(r, carry):
            base = pl.multiple_of(r * CONV_RC, CONV_RC)
            for c in range(4):
                lanes = slice(c * 128, (c + 1) * 128)
                acc = jnp.broadcast_to(cb_ref[:, lanes], (CONV_RC // 8, 8, 128))
                for b8, offsets in _offsets_by_phase(HALO - (CONV_C_TAPS - 1)):
                    win = _window(sb, b8, base, offsets, lanes)
                    for o in offsets:
                        j = o - (HALO - (CONV_C_TAPS - 1))
                        acc = acc + cw_ref[8 * j:8 * j + 8, lanes] * win[o // 8:o // 8 + CONV_RC // 8]
                c1_ref[pl.ds(base, CONV_RC), lanes] = acc.reshape(CONV_RC, 128)
            return carry

        lax.fori_loop(0, tm // CONV_RC, conv_rows, 0)
        xhat, _ = _ln_stats(c1_ref[...])
        c2 = xhat * lg_ref[...] + lb_ref[...]
        y = jnp.zeros((tm, 512), F32)
        for j in range(CONV_D_TAPS):
            y = y + dw_ref[j:j + 1, :] * buf2[pl.ds(HALO - (CONV_D_TAPS - 1) + j, tm), :]
        cat_ref[:, 0:512] = (c2 * _sigmoid(c2)).astype(BF16)
        cat_ref[:, 512:1024] = (gb * y).astype(BF16)
        y_ref[...] = y.astype(BF16)

    half = pl.BlockSpec((tm, 512), lambda i: (i, 0))
    vec = pl.BlockSpec((1, 512), lambda i: (0, 0))
    return pl.pallas_call(
        body, name="cd_fwd", grid=(T // tm,),
        in_specs=[pl.BlockSpec((tm, CD_IN), lambda i: (i, 0)),
                  pl.BlockSpec((HALO, CD_IN), lambda i: (jnp.maximum(i * per - 1, 0), 0)),
                  pl.BlockSpec((8 * 32, 512), lambda i: (0, 0)), vec, vec, vec, pl.BlockSpec((8, 512), lambda i: (0, 0))],
        out_specs=[pl.BlockSpec((tm, D), lambda i: (i, 0)), half, half, half, half],
        out_shape=[_sds((T, D), BF16), _sds((T, 512), BF16), _sds((T, 512), F32), _sds((T, 512), BF16),
                   _sds((T, 512), BF16)],
        scratch_shapes=[pltpu.VMEM((HALO + tm, 512), F32), pltpu.VMEM((HALO + tm, 512), F32),
                        pltpu.VMEM((8, HALO + tm, 512), F32)],
        compiler_params=_cparams(1))(pcd, pcd, cw, cb, lg, lb, dw)


def _d_cat_cd(dxb, w_cd_out, c1, pcd, y, lg, lb, dep, tm=512):
    def epilogue(d, rows, side, outs):
        c1_ref, gb_ref, y_ref, lg_ref, lb_ref = side
        dc1_ref, dy3_ref, dgb_ref, dlg_ref, dlb_ref, dcb_ref = outs
        dc, ddo = d[:, 0:512], d[:, 512:1024]
        xhat, rstd = _ln_stats(c1_ref[rows, :])
        c2 = xhat * lg_ref[...] + lb_ref[...]
        sg = _sigmoid(c2)
        dc2 = dc * sg * (1.0 + c2 * (1.0 - sg))
        dlg_ref[...] += jnp.sum(dc2 * xhat, axis=0, keepdims=True)
        dlb_ref[...] += jnp.sum(dc2, axis=0, keepdims=True)
        dxh = dc2 * lg_ref[...]
        dc1 = rstd * (dxh - jnp.mean(dxh, axis=-1, keepdims=True)
                      - xhat * jnp.mean(dxh * xhat, axis=-1, keepdims=True))
        dcb_ref[...] += jnp.sum(dc1, axis=0, keepdims=True)
        dc1_ref[rows, :] = dc1
        dgb_ref[rows, :] = (ddo * y_ref[rows, :].astype(F32)).astype(BF16)
        dy3_ref[rows, :] = ddo * gb_ref[rows, :].astype(F32)

    half = pl.BlockSpec((tm, 512), lambda i: (i, 0))
    vec = pl.BlockSpec((1, 512), lambda i: (0, 0))
    return _mm_nt_rows(
        dxb, w_cd_out, "mm_d_cat_cd", epilogue, [c1, pcd, y, lg, lb],
        [half, pl.BlockSpec((tm, 512), lambda i: (i, 2)), half, vec, vec], [half, half, half, vec, vec, vec],
        [_sds((T, 512), F32), _sds((T, 512), F32), _sds((T, 512), BF16),
         _sds((1, 512), F32), _sds((1, 512), F32), _sds((1, 512), F32)], sums=(3, 4, 5), dep=dep, tm=tm)


def _cd_bwd_conv(pcd, dc1, dy3, c0, dd, dgb, cw8, dw, tm=256):
    per = tm // HALO
    nblk = T // tm
    last32 = T // HALO - 1

    def body(p_ref, dc1_ref, dc1n_ref, dy3_ref, dy3n_ref, c0_ref, dd_ref, dgb_ref, cw_ref, dw_ref,
             o_ref, dcw_ref, ddw_ref, dbuf, d3buf, sd, dc0_buf):
        i = pl.program_id(0)
        has_next = jnp.where(i < nblk - 1, 1.0, 0.0)

        @pl.when(i == 0)
        def _():
            dcw_ref[...] = jnp.zeros_like(dcw_ref)
            ddw_ref[...] = jnp.zeros_like(ddw_ref)

        dbuf[0:tm, :] = dc1_ref[...]
        dbuf[tm:, :] = dc1n_ref[...] * has_next
        d3buf[0:tm, :] = dy3_ref[...]
        d3buf[tm:, :] = dy3n_ref[...] * has_next
        _shifted_copies(dbuf, sd, tm)
        n_tiles = tm // CONV_RC

        phases = _offsets_by_phase(0)

        def dc0_rows(r, carry):
            base = pl.multiple_of(r * CONV_RC, CONV_RC)
            for c in range(4):
                lanes = slice(c * 128, (c + 1) * 128)
                acc = jnp.zeros((CONV_RC // 8, 8, 128), F32)
                for b8, offsets in phases:
                    win = _window(sd, b8, base, offsets, lanes)
                    for o in offsets:
                        j = CONV_C_TAPS - 1 - o
                        acc = acc + cw_ref[8 * j:8 * j + 8, lanes] * win[o // 8:o // 8 + CONV_RC // 8]
                dc0_buf[pl.ds(base, CONV_RC), lanes] = acc.reshape(CONV_RC, 128)
            return carry

        lax.fori_loop(0, n_tiles, dc0_rows, 0)

        for c in range(4):
            lanes = slice(c * 128, (c + 1) * 128)
            for b8, offsets in phases:
                def dw_rows(r, accs, lanes=lanes, b8=b8, offsets=offsets):
                    base = pl.multiple_of(r * CONV_RC, CONV_RC)
                    xin = c0_ref[pl.ds(base, CONV_RC), lanes].astype(F32).reshape(CONV_RC // 8, 8, 128)
                    win = _window(sd, b8, base, offsets, lanes)
                    return tuple(acc + jnp.sum(xin * win[o // 8:o // 8 + CONV_RC // 8], axis=0)
                                 for acc, o in zip(accs, offsets))

                accs = lax.fori_loop(0, n_tiles, dw_rows, tuple(jnp.zeros((8, 128), F32) for _ in offsets))
                for acc, o in zip(accs, offsets):
                    j = CONV_C_TAPS - 1 - o
                    dcw_ref[j:j + 1, lanes] += jnp.sum(acc, axis=0, keepdims=True)

        dc0 = dc0_buf[...]
        ddin = dd_ref[...].astype(F32)
        ddd = jnp.zeros((tm, 512), F32)
        for j in range(CONV_D_TAPS):
            dy_shift = d3buf[pl.ds(CONV_D_TAPS - 1 - j, tm), :]
            ddd = ddd + dw_ref[j:j + 1, :] * dy_shift
            ddw_ref[j:j + 1, :] += jnp.sum(ddin * dy_shift, axis=0, keepdims=True)

        a = p_ref[:, 0:512].astype(F32)
        gt = p_ref[:, 512:1024].astype(F32)
        gc = p_ref[:, 1536:2048].astype(F32)
        hv = p_ref[:, 2048:2560].astype(F32)
        sg = _sigmoid(gt)
        o_ref[:, 0:512] = (dc0 * sg).astype(BF16)
        o_ref[:, 512:1024] = (dc0 * a * sg * (1.0 - sg)).astype(BF16)
        o_ref[:, 1024:1536] = dgb_ref[...]
        o_ref[:, 1536:2048] = (ddd * hv).astype(BF16)
        o_ref[:, 2048:2560] = (ddd * gc).astype(BF16)

    half = pl.BlockSpec((tm, 512), lambda i: (i, 0))
    nxt = pl.BlockSpec((HALO, 512), lambda i: (jnp.minimum((i + 1) * per, last32), 0))
    full = pl.BlockSpec((tm, CD_IN), lambda i: (i, 0))
    return pl.pallas_call(
        body, name="cd_bwd_conv", grid=(nblk,),
        in_specs=[full, half, nxt, half, nxt, half, half, half,
                  pl.BlockSpec((8 * 32, 512), lambda i: (0, 0)), pl.BlockSpec((8, 512), lambda i: (0, 0))],
        out_specs=[full, pl.BlockSpec((32, 512), lambda i: (0, 0)), pl.BlockSpec((8, 512), lambda i: (0, 0))],
        out_shape=[_sds((T, CD_IN), BF16), _sds((32, 512), F32), _sds((8, 512), F32)],
        scratch_shapes=[pltpu.VMEM((tm + HALO, 512), F32), pltpu.VMEM((tm + HALO, 512), F32),
                        pltpu.VMEM((8, tm + HALO, 512), F32), pltpu.VMEM((tm, 512), F32)],
        compiler_params=_cparams(1))(pcd, dc1, dc1, dy3, dy3, c0, dd, dgb, cw8, dw)


def _local_step(x, tgt, W, fetch=None, on_grad=None):
    W = dict(W)
    if fetch is None:
        fetch = lambda stage, after: {}
    if on_grad is None:
        on_grad = lambda key, arr: None
    tabs = _rope_tables()
    qg = jnp.tile(W["q_norm_g"], (1, 2))
    kg = jnp.tile(W["k_norm_g"], (1, 2))
    bias3 = W["sgu_bias"].reshape(4, 128, 1)
    G = {}

    h0 = _rms_fwd(x, W["ab_norm_g"], "rms_fwd_ab", dep=W.get("dep_first"))
    W.update(fetch("ab_in", h0))
    pab = _mm_nt(h0, W["wt_ab_in"], "mm_ab_in", dep=W.get("dep0"))
    cat_ab = _mix_a_fwd(pab, W["sgu_norm_g"], W["sgu_norm_b"], W["sgu_w"], bias3)
    qkv, rinvs = _prep_fwd(pab, qg, kg, tabs)
    outs, lses = [], []
    for g, rate in enumerate(DIL_RATES):
        o, l = _attn_fwd(qkv[3 * g], qkv[3 * g + 1], qkv[3 * g + 2], rate, f"attn_fwd_{g}", dep=W.get(f"dep_attn{g}"))
        outs.append(o)
        lses.append(l)
        W.update(fetch(f"attn{g}", o))
    cat_ab, lse = _merge_fwd(cat_ab, outs, lses)
    W.update(fetch("ab_out", lse))
    x1, h1 = _mm_nn(cat_ab, W["w_ab_out"], "mm_ab_out", mode="rms", resid=x, gain=W["ffn_norm_g"][0:1])
    pf0, act0 = _ffn_in(h1, W["wt_ffn_in0"], "ffn_in0")
    W.update(fetch("ffn_down0", act0))
    x2, h2 = _mm_nn(act0, W["w_ffn_down0"], "mm_ffn_down0", mode="rms", resid=x1, gain=W["cd_norm_g"],
                    dep=W.get("dep_down0"))
    W.update(fetch("cd_in", h2))
    pcd = _mm_nt(h2, W["wt_cd_in"], "mm_cd_in")
    cw8 = jnp.repeat(W["conv_c_w32"], 8, axis=0)
    cat_cd, c0, c1, dd, yv = _cd_fwd(pcd, cw8, W["conv_c_b"], W["c_ln_g"], W["c_ln_b"], W["conv_d_w8"])
    x3, h3 = _mm_nn(cat_cd, W["w_cd_out"], "mm_cd_out", mode="rms", resid=x2, gain=W["ffn_norm_g"][1:2])
    pf1, act1 = _ffn_in(h3, W["wt_ffn_in1"], "ffn_in1")
    dy, dyb, loss_cols = _mm_nn(act1, W["w_ffn_down1"], "mm_ffn_down1", mode="loss", resid=x3, tgt=tgt)

    def ffn_bwd(xin, h, pf, act, dres, dresb, layer):
        G[f"w_ffn_down{layer}"] = _mm_tn(act, dresb, f"mm_g_ffn_down{layer}")
        dep = on_grad(f"w_ffn_down{layer}", G[f"w_ffn_down{layer}"])
        dpf = _ffn_dact(dresb, W[f"w_ffn_down{layer}"], pf, f"ffn_dact{layer}", dep=dep)
        G[f"wt_ffn_in{layer}"] = _mm_tn(dpf, h, f"mm_g_ffn_in{layer}")
        dep = on_grad(f"wt_ffn_in{layer}", G[f"wt_ffn_in{layer}"])
        dx, dxb, G[f"ffn_norm_g{layer}"] = _mm_dh_rms_bwd(
            dpf, W[f"wt_ffn_in{layer}"], xin, W["ffn_norm_g"][layer:layer + 1], dres, f"mm_d_h_ffn{layer}", dep=dep)
        return dx, dxb

    dx3, dx3b = ffn_bwd(x3, h3, pf1, act1, dy, dyb, 1)

    G["w_cd_out"] = _mm_tn(cat_cd, dx3b, "mm_g_cd_out")
    dep = on_grad("w_cd_out", G["w_cd_out"])
    dc1, dy3, dgb, G["c_ln_g"], G["c_ln_b"], G["conv_c_b"] = _d_cat_cd(
        dx3b, W["w_cd_out"], c1, pcd, yv, W["c_ln_g"], W["c_ln_b"], dep)
    dpcd, G["conv_c_w32"], G["conv_d_w8"] = _cd_bwd_conv(pcd, dc1, dy3, c0, dd, dgb, cw8, W["conv_d_w8"])
    G["wt_cd_in"] = _mm_tn(dpcd, h2, "mm_g_cd_in")
    dep = on_grad("wt_cd_in", G["wt_cd_in"])
    dx2, dx2b, G["cd_norm_g"] = _mm_dh_rms_bwd(dpcd, W["wt_cd_in"], x2, W["cd_norm_g"], dx3, "mm_d_h_cd", dep=dep)

    dx1, dx1b = ffn_bwd(x1, h1, pf0, act0, dx2, dx2b, 0)

    G["w_ab_out"] = _mm_tn(cat_ab, dx1b, "mm_g_ab_out")
    dep = on_grad("w_ab_out", G["w_ab_out"])
    dcat_a, dbp, e = _d_cat_ab(dx1b, W["w_ab_out"], cat_ab, dep)
    dqkv = []
    for g, rate in enumerate(DIL_RATES):
        dqkv += _attn_bwd(qkv[3 * g], qkv[3 * g + 1], qkv[3 * g + 2], dbp, e, lse, rate, f"attn_bwd_{g}")
    dpab, G["sgu_w"], dbias_part, G["sgu_norm_g"], G["sgu_norm_b"], dgain = _ab_in_bwd(
        pab, dcat_a, W["sgu_norm_g"], W["sgu_norm_b"], W["sgu_w"], bias3, qg, kg, tabs, dqkv, rinvs)
    G["sgu_bias"] = jnp.sum(dbias_part, axis=-1)
    dgain = dgain[0:6, 0:HEAD] + dgain[0:6, HEAD:PAIR]
    G["q_norm_g"] = dgain[0::2]
    G["k_norm_g"] = dgain[1::2]
    G["loss_cols"] = loss_cols
    dep = on_grad("small", G)
    G["wt_ab_in"] = _mm_tn(dpab, h0, "mm_g_ab_in", dep=dep)
    dep = on_grad("wt_ab_in", G["wt_ab_in"])
    grad_x, G["ab_norm_g"] = _mm_dh_rms_bwd(dpab, W["wt_ab_in"], x, W["ab_norm_g"], dx1, "mm_d_h_ab", dep=dep,
                                            bf16_copy=False)
    return loss_cols, grad_x, G


def _my_place():
    return lax.axis_index("x"), lax.axis_index("y"), lax.axis_index("c")


def _dev_index(px, py, pc):
    return 4 * px + 2 * py + pc


def _flip(place, k):
    x, y, c = place
    return (1 - x if k & 4 else x, 1 - y if k & 2 else y, 1 - c if k & 1 else c)


def _landing(shape, dtype, own):
    buf = lax.empty(shape, dtype)
    for lead, part in own:
        buf = lax.dynamic_update_slice(buf, part.reshape((1,) * len(lead) + part.shape),
                                       tuple(lead) + (0,) * part.ndim)
    return buf


HBM_ONLY = pl.BlockSpec(memory_space=pltpu.HBM)
SEM_SPEC = pl.BlockSpec(memory_space=pltpu.SEMAPHORE)
IN_FLIGHT = pltpu.CompilerParams(has_side_effects=pltpu.SideEffectType.DATAFLOW_SIDE_EFFECTING)


def _in_hbm(a):
    return pltpu.with_memory_space_constraint(a, pltpu.HBM)


def _exchange_start(name, srcs, lands, items, dep=None):
    ns, nl, ni = len(srcs), len(lands), len(items)

    def body(*refs):
        S, L = refs[0:ns], refs[ns:ns + nl]
        first_out = ns + nl + (0 if dep is None else 1)
        send_sems, recv_sems, token = refs[first_out], refs[first_out + 1], refs[-1]
        me = _my_place()
        mi = _dev_index(*me)
        for i, (src, dst) in enumerate(items):
            for k in range(1, NDEV):
                peer = _flip(me, k)
                pltpu.make_async_remote_copy(
                    src_ref=src(S, _dev_index(*peer)), dst_ref=dst(L, mi), send_sem=send_sems.at[7 * i + k - 1],
                    recv_sem=recv_sems.at[7 * i + k - 1], device_id=peer, device_id_type=MESH).start()
        token[...] = jnp.zeros_like(token)

    thru = [pltpu.HBM(a.shape, a.dtype) for a in list(srcs) + list(lands)]
    args = [_in_hbm(a) for a in srcs] + [_in_hbm(a) for a in lands]
    in_specs = [HBM_ONLY] * (ns + nl)
    if dep is not None:
        args.append(dep)
        in_specs.append(HBM_SPEC)
    outs = pl.pallas_call(
        body, name=name, in_specs=in_specs,
        out_shape=(pltpu.SemaphoreType.DMA((7 * ni,)), pltpu.SemaphoreType.DMA((7 * ni,)), *thru, _sds((8, 128), F32)),
        out_specs=(SEM_SPEC, SEM_SPEC, *[HBM_ONLY] * (ns + nl), pl.BlockSpec(memory_space=pltpu.VMEM)),
        input_output_aliases={j: 2 + j for j in range(ns + nl)}, compiler_params=IN_FLIGHT)(*args)
    return dict(send=outs[0], recv=outs[1], srcs=list(outs[2:2 + ns]), lands=list(outs[2 + ns:2 + ns + nl]),
                token=outs[-1], items=items)


def _exchange_wait(name, states, after):
    after = list(after) if isinstance(after, (list, tuple)) else [after]
    counts = [(len(st["srcs"]), len(st["lands"]), len(st["items"])) for st in states]
    n_arrays = sum(c[0] + c[1] for c in counts)

    def body(*refs):
        me = _my_place()
        mi = _dev_index(*me)
        pos = 0
        sem_pos = n_arrays
        for st, (ns, nl, ni) in zip(states, counts):
            S, L = refs[pos:pos + ns], refs[pos + ns:pos + ns + nl]
            send_sems, recv_sems = refs[sem_pos], refs[sem_pos + 1]
            pos += ns + nl
            sem_pos += 2
            for i, (src, dst) in enumerate(st["items"]):
                for k in range(1, NDEV):
                    cp = pltpu.make_async_remote_copy(
                        src_ref=src(S, mi), dst_ref=dst(L, mi), send_sem=send_sems.at[7 * i + k - 1],
                        recv_sem=recv_sems.at[7 * i + k - 1], device_id=me, device_id_type=MESH)
                    cp.wait_send()
                    cp.wait_recv()

    arrays, sems = [], []
    for st in states:
        arrays += st["srcs"] + st["lands"]
        sems += [st["send"], st["recv"]]
    outs = pl.pallas_call(
        body, name=name, in_specs=[HBM_ONLY] * n_arrays + [SEM_SPEC] * len(sems) + [HBM_SPEC] * len(after),
        out_shape=tuple(pltpu.HBM(a.shape, a.dtype) for a in arrays), out_specs=tuple([HBM_ONLY] * n_arrays),
        input_output_aliases={j: j for j in range(n_arrays)}, compiler_params=IN_FLIGHT)(*arrays, *sems, *after)
    lands, pos = [], 0
    for ns, nl, _ in counts:
        lands.append(list(outs[pos + ns:pos + ns + nl]))
        pos += ns + nl
    return lands


def _place_and_neighbours():
    x, y, c = _my_place()
    return (x, y, c), (x, y, 1 - c), [(1 - x, y), (x, 1 - y), (1 - x, 1 - y)]


def _gather_start(name, srcs, lands, items, dep=None):
    ns, nl, ni = len(srcs), len(lands), len(items)

    def body(*refs):
        S, L = refs[0:ns], refs[ns:ns + nl]
        first_out = ns + nl + (0 if dep is None else 1)
        send_sems, recv_sems, token = refs[first_out], refs[first_out + 1], refs[-1]
        me, sib, chips = _place_and_neighbours()
        mi = _dev_index(*me)
        for i, (src, dst) in enumerate(items):
            for k, to in enumerate([sib] + [(*chip, me[2]) for chip in chips]):
                pltpu.make_async_remote_copy(
                    src_ref=src(S), dst_ref=dst(L, mi), send_sem=send_sems.at[4 * i + k],
                    recv_sem=recv_sems.at[4 * i + k], device_id=to, device_id_type=MESH).start()
        token[...] = jnp.zeros_like(token)

    thru = [pltpu.HBM(a.shape, a.dtype) for a in list(srcs) + list(lands)]
    args = [_in_hbm(a) for a in srcs] + [_in_hbm(a) for a in lands]
    in_specs = [HBM_ONLY] * (ns + nl)
    if dep is not None:
        args.append(dep)
        in_specs.append(HBM_SPEC)
    outs = pl.pallas_call(
        body, name=name, in_specs=in_specs,
        out_shape=(pltpu.SemaphoreType.DMA((4 * ni,)), pltpu.SemaphoreType.DMA((4 * ni,)), *thru, _sds((8, 128), F32)),
        out_specs=(SEM_SPEC, SEM_SPEC, *[HBM_ONLY] * (ns + nl), pl.BlockSpec(memory_space=pltpu.VMEM)),
        input_output_aliases={j: 2 + j for j in range(ns + nl)}, compiler_params=IN_FLIGHT)(*args)
    return dict(send=outs[0], recv=outs[1], srcs=list(outs[2:2 + ns]), lands=list(outs[2 + ns:2 + ns + nl]),
                token=outs[-1], items=items)


def _gather_forward(name, st, after):
    nl, ni = len(st["lands"]), len(st["items"])

    def body(*refs):
        L, recv_sems = refs[0:nl], refs[nl]
        fwd_send, fwd_recv, token = refs[-3:]
        me, sib, chips = _place_and_neighbours()
        for i, (_, dst) in enumerate(st["items"]):
            for j, chip in enumerate(chips):
                blk = dst(L, _dev_index(*chip, me[2]))
                pltpu.make_async_remote_copy(
                    src_ref=blk, dst_ref=blk, send_sem=fwd_send.at[3 * i + j], recv_sem=recv_sems.at[4 * i + 1 + j],
                    device_id=me, device_id_type=MESH).wait_recv()
                pltpu.make_async_remote_copy(
                    src_ref=blk, dst_ref=blk, send_sem=fwd_send.at[3 * i + j], recv_sem=fwd_recv.at[3 * i + j],
                    device_id=sib, device_id_type=MESH).start()
        token[...] = jnp.zeros_like(token)

    after = list(after) if isinstance(after, (list, tuple)) else [after]
    outs = pl.pallas_call(
        body, name=name, in_specs=[HBM_ONLY] * nl + [SEM_SPEC] + [HBM_SPEC] * len(after),
        out_shape=(*[pltpu.HBM(a.shape, a.dtype) for a in st["lands"]], pltpu.SemaphoreType.DMA((3 * ni,)),
                   pltpu.SemaphoreType.DMA((3 * ni,)), _sds((8, 128), F32)),
        out_specs=(*[HBM_ONLY] * nl, SEM_SPEC, SEM_SPEC, pl.BlockSpec(memory_space=pltpu.VMEM)),
        input_output_aliases={j: j for j in range(nl)}, compiler_params=IN_FLIGHT)(*st["lands"], st["recv"], *after)
    return dict(st, lands=list(outs[0:nl]), fwd_send=outs[nl], fwd_recv=outs[nl + 1], token=outs[-1])


def _gather_wait(name, st, after):
    ns, nl, ni = len(st["srcs"]), len(st["lands"]), len(st["items"])

    def body(*refs):
        S, L = refs[0:ns], refs[ns:ns + nl]
        send_sems, recv_sems, fwd_send, fwd_recv = refs[ns + nl:ns + nl + 4]
        me, sib, chips = _place_and_neighbours()
        mi = _dev_index(*me)
        for i, (src, dst) in enumerate(st["items"]):
            mine = dst(L, mi)
            for k in range(4):
                pltpu.make_async_remote_copy(
                    src_ref=src(S), dst_ref=mine, send_sem=send_sems.at[4 * i + k], recv_sem=recv_sems.at[4 * i + k],
                    device_id=me, device_id_type=MESH).wait_send()
            pltpu.make_async_remote_copy(
                src_ref=src(S), dst_ref=mine, send_sem=send_sems.at[4 * i], recv_sem=recv_sems.at[4 * i],
                device_id=me, device_id_type=MESH).wait_recv()
            for j in range(3):
                cp = pltpu.make_async_remote_copy(
                    src_ref=mine, dst_ref=mine, send_sem=fwd_send.at[3 * i + j], recv_sem=fwd_recv.at[3 * i + j],
                    device_id=me, device_id_type=MESH)
                cp.wait_send()
                cp.wait_recv()

    arrays = st["srcs"] + st["lands"]
    outs = pl.pallas_call(
        body, name=name, in_specs=[HBM_ONLY] * (ns + nl) + [SEM_SPEC] * 4 + [HBM_SPEC],
        out_shape=tuple(pltpu.HBM(a.shape, a.dtype) for a in arrays), out_specs=tuple([HBM_ONLY] * (ns + nl)),
        input_output_aliases={j: j for j in range(ns + nl)},
        compiler_params=IN_FLIGHT)(*arrays, st["send"], st["recv"], st["fwd_send"], st["fwd_recv"], after)
    return list(outs[ns:ns + nl])


def _sum_slots(land):
    def body(l_ref, o_ref):
        acc = l_ref[0]
        for d in range(1, NDEV):
            acc = acc + l_ref[d]
        o_ref[...] = acc

    vm = pl.BlockSpec(memory_space=pltpu.VMEM)
    return pl.pallas_call(body, name="sum_small", out_shape=_sds(land.shape[1:], F32), in_specs=[vm], out_specs=vm)(land)


def _adam_math(w, g, m, v):
    m2 = ADAM_B1 * m + (1.0 - ADAM_B1) * g
    v2 = ADAM_B2 * v + (1.0 - ADAM_B2) * (g * g)
    delta = -ADAM_LR * ((m2 * ADAM_C1) / (jnp.sqrt(v2 * ADAM_C2) + ADAM_EPS) + ADAM_WD * w)
    return delta, m2, v2


def _adam_layer(land, sel, w, m, v, layer, name, prev=None, tc=512):
    R = land.shape[2]

    def body(l_ref, w_ref, m_ref, v_ref, *rest):
        g_out, d_out, m_out, v_out = rest[-4:]
        g = l_ref[0].astype(F32)
        for d in range(1, NDEV):
            g = g + l_ref[d].astype(F32)
        delta, m2, v2 = _adam_math(w_ref[...], g, m_ref[...], v_ref[...])
        g_out[...] = g
        d_out[...] = delta
        m_out[...] = m2
        v_out[...] = v2

    wspec = pl.BlockSpec((None, R, tc), lambda i: (layer, 0, i))
    in_specs = [pl.BlockSpec((None, NDEV, R, tc), lambda i: (sel, 0, 0, i)), wspec, wspec, wspec]
    args = [land, w, m, v]
    aliases = {}
    if prev is not None:
        in_specs += [HBM_SPEC] * 4
        args += list(prev)
        aliases = {4 + j: j for j in range(4)}
    return pl.pallas_call(
        body, name=name, grid=(D // tc,), in_specs=in_specs, out_specs=[wspec] * 4,
        out_shape=[_sds(w.shape, F32)] * 4, input_output_aliases=aliases, compiler_params=_cparams(1))(*args)


def _adam_stacked(lands, sel, w, m, v, name):
    res = None
    for layer, land in enumerate(lands):
        res = _adam_layer(land, sel, w, m, v, layer, f"{name}{layer}", prev=res)
    return res


def _adam_small(ws, gs, ms, vs):
    n = len(ws)

    def body(*refs):
        w_r, g_r, m_r, v_r = refs[0:n], refs[n:2 * n], refs[2 * n:3 * n], refs[3 * n:4 * n]
        d_o, m_o, v_o = refs[4 * n:5 * n], refs[5 * n:6 * n], refs[6 * n:7 * n]
        for i in range(n):
            delta, m2, v2 = _adam_math(w_r[i][...], g_r[i][...], m_r[i][...], v_r[i][...])
            d_o[i][...] = delta
            m_o[i][...] = m2
            v_o[i][...] = v2

    vm = pl.BlockSpec(memory_space=pltpu.VMEM)
    shapes = [_sds(w.shape, F32) for w in ws]
    outs = pl.pallas_call(body, name="adam_small", in_specs=[vm] * (4 * n), out_specs=[vm] * (3 * n),
                          out_shape=shapes * 3)(*ws, *gs, *ms, *vs)
    return outs[0:n], outs[n:2 * n], outs[2 * n:3 * n]


def _adam_of_slots(land, w, m, v, name):
    def body(l_ref, w_ref, m_ref, v_ref, g_o, d_o, m_o, v_o):
        g = l_ref[0]
        for d in range(1, NDEV):
            g = g + l_ref[d]
        g_o[...] = g
        d_o[...], m_o[...], v_o[...] = _adam_math(w_ref[...], g, m_ref[...], v_ref[...])

    vm = pl.BlockSpec(memory_space=pltpu.VMEM)
    return pl.pallas_call(body, name=name, in_specs=[vm] * 4, out_specs=[vm] * 4,
                          out_shape=[_sds(w.shape, F32)] * 4)(land, w, m, v)


WEIGHT_NAMES = ("ab_norm_g", "ab_w_in", "sgu_norm_g", "sgu_norm_b", "sgu_w", "sgu_bias", "q_norm_g", "k_norm_g",
                "ab_w_out", "cd_norm_g", "cd_w_in", "conv_c_w", "conv_c_b", "c_ln_g", "c_ln_b", "conv_d_w",
                "cd_w_out", "ffn_norm_g", "ffn_w_gate", "ffn_w_up", "ffn_w_down")
SMALL_2D = (("sgu_norm_g", (1, 512)), ("sgu_norm_b", (1, 512)), ("sgu_w", (512, 128)),
            ("sgu_bias", (4, 128)), ("q_norm_g", (3, 64)), ("k_norm_g", (3, 64)), ("cd_norm_g", (1, 128)),
            ("conv_c_w", (31, 64)), ("conv_c_b", (1, 64)), ("c_ln_g", (1, 64)), ("c_ln_b", (1, 64)),
            ("conv_d_w", (3, 64)), ("ffn_norm_g", (2, 1024)))
SHARD_C = 64


def _pack_rows(parts, rows):
    flat = jnp.concatenate([p.reshape(-1) for p in parts])
    return jnp.pad(flat, (0, rows * 128 - flat.shape[0])).reshape(rows, 128)


def kernel(x, ab_norm_g, ab_w_in, sgu_norm_g, sgu_norm_b, sgu_w, sgu_bias, q_norm_g, k_norm_g, ab_w_out, cd_norm_g, cd_w_in, conv_c_w, conv_c_b, c_ln_g, c_ln_b, conv_d_w, cd_w_out, ffn_norm_g, ffn_w_gate, ffn_w_up, ffn_w_down, loss_target, m_ab_norm_g, m_ab_w_in, m_sgu_norm_g, m_sgu_norm_b, m_sgu_w, m_sgu_bias, m_q_norm_g, m_k_norm_g, m_ab_w_out, m_cd_norm_g, m_cd_w_in, m_conv_c_w, m_conv_c_b, m_c_ln_g, m_c_ln_b, m_conv_d_w, m_cd_w_out, m_ffn_norm_g, m_ffn_w_gate, m_ffn_w_up, m_ffn_w_down, v_ab_norm_g, v_ab_w_in, v_sgu_norm_g, v_sgu_norm_b, v_sgu_w, v_sgu_bias, v_q_norm_g, v_k_norm_g, v_ab_w_out, v_cd_norm_g, v_cd_w_in, v_conv_c_w, v_conv_c_b, v_c_ln_g, v_c_ln_b, v_conv_d_w, v_cd_w_out, v_ffn_norm_g, v_ffn_w_gate, v_ffn_w_up, v_ffn_w_down):
    w = dict(zip(WEIGHT_NAMES, (ab_norm_g, ab_w_in, sgu_norm_g, sgu_norm_b, sgu_w, sgu_bias, q_norm_g, k_norm_g, ab_w_out, cd_norm_g, cd_w_in, conv_c_w, conv_c_b, c_ln_g, c_ln_b, conv_d_w, cd_w_out, ffn_norm_g, ffn_w_gate, ffn_w_up, ffn_w_down)))
    m = dict(zip(WEIGHT_NAMES, (m_ab_norm_g, m_ab_w_in, m_sgu_norm_g, m_sgu_norm_b, m_sgu_w, m_sgu_bias, m_q_norm_g, m_k_norm_g, m_ab_w_out, m_cd_norm_g, m_cd_w_in, m_conv_c_w, m_conv_c_b, m_c_ln_g, m_c_ln_b, m_conv_d_w, m_cd_w_out, m_ffn_norm_g, m_ffn_w_gate, m_ffn_w_up, m_ffn_w_down)))
    v = dict(zip(WEIGHT_NAMES, (v_ab_norm_g, v_ab_w_in, v_sgu_norm_g, v_sgu_norm_b, v_sgu_w, v_sgu_bias, v_q_norm_g, v_k_norm_g, v_ab_w_out, v_cd_norm_g, v_cd_w_in, v_conv_c_w, v_conv_c_b, v_c_ln_g, v_c_ln_b, v_conv_d_w, v_cd_w_out, v_ffn_norm_g, v_ffn_w_gate, v_ffn_w_up, v_ffn_w_down)))
    me = _dev_index(*_my_place())

    r_ff = DFF // NDEV
    one = lambda a: (lambda S, j: S[a])
    slot = lambda b: (lambda L, s: L[b].at[s])
    slot2 = lambda b, part: (lambda L, s: L[b].at[part, s])
    shard = lambda a: (lambda S: S[a])

    def later(a):
        return lax.optimization_barrier((a, gathers[0]["token"]))[0]

    def layer_shards(layer):
        return (later(w["ffn_w_gate"][layer]).T.astype(BF16), later(w["ffn_w_up"][layer]).T.astype(BF16),
                later(w["ffn_w_down"][layer]).astype(BF16))

    def gathered(own):
        return _landing((NDEV,) + own.shape, BF16, [((me,), own)])

    def gathered2(a, b):
        return _landing((2, NDEV) + a.shape, BF16, [((0, me), a), ((1, me), b)])

    ab_in_s = w["ab_w_in"][0].T.astype(BF16)
    gathers = {0: _gather_start("gather0_start", [ab_in_s], [gathered(ab_in_s)], [(shard(0), slot(0))])}

    def chan(flat, lo, taps):
        return flat[:, lo:lo + taps * SHARD_C].reshape(NDEV, taps, SHARD_C).transpose(1, 0, 2).reshape(taps, 512)

    def fetch(stage, after):
        if stage == "ab_in":
            ab_out_s = later(w["ab_w_out"][0]).astype(BF16)
            gate0, up0, down0 = layer_shards(0)
            small_s = _pack_rows([later(w[n]) for n in ("cd_norm_g", "conv_c_w", "conv_c_b", "c_ln_g", "c_ln_b",
                                                        "conv_d_w")], 24)
            lands1 = [gathered(ab_out_s), gathered2(gate0, up0), gathered(down0),
                      _landing((NDEV,) + small_s.shape, F32, [((me,), small_s)])]
            gathers[0] = _gather_forward("gather0_forward", gathers[0], [after] + lands1)
            l_ab_in, = _gather_wait("gather0_wait", gathers[0], gathers[0]["token"])
            gathers[1] = _gather_start(
                "gather1_start", [ab_out_s, gate0, up0, down0, small_s], lands1,
                [(shard(0), slot(0)), (shard(1), slot2(1, 0)), (shard(2), slot2(1, 1)), (shard(3), slot(2)),
                 (shard(4), slot(3))], dep=l_ab_in)
            return {"wt_ab_in": l_ab_in.reshape(AB_IN, D), "dep0": gathers[1]["token"]}
        if stage == "attn0":
            cd_in_s, cd_out_s = later(w["cd_w_in"][0]).T.astype(BF16), later(w["cd_w_out"][0]).astype(BF16)
            gate1, up1, down1 = layer_shards(1)
            gathers[2] = _gather_start(
                "gather2_start", [cd_in_s, cd_out_s, gate1, up1, down1],
                [gathered(cd_in_s), gathered(cd_out_s), gathered2(gate1, up1), gathered(down1)],
                [(shard(0), slot(0)), (shard(1), slot(1)), (shard(2), slot2(2, 0)), (shard(3), slot2(2, 1)),
                 (shard(4), slot(3))], dep=after)
            return {"dep_attn1": gathers[2]["token"]}
        if stage == "attn1":
            gathers[1] = _gather_forward("gather1_forward", gathers[1], after)
            return {"dep_attn2": gathers[1]["token"]}
        if stage == "ab_out":
            l_out, l_ffn, l_down, l_small = _gather_wait("gather1_wait", gathers[1], after)
            flat = l_small.reshape(NDEV, 24 * 128)
            return {
                "w_ab_out": l_out.reshape(D, D), "wt_ffn_in0": l_ffn.reshape(2 * DFF, D),
                "w_ffn_down0": l_down.reshape(DFF, D), "cd_norm_g": flat[:, 0:128].reshape(1, D),
                "conv_c_w32": jnp.pad(chan(flat, 128, CONV_C_TAPS), ((0, 1), (0, 0))),
                "conv_c_b": chan(flat, 2112, 1), "c_ln_g": chan(flat, 2176, 1), "c_ln_b": chan(flat, 2240, 1),
                "conv_d_w8": jnp.pad(chan(flat, 2304, CONV_D_TAPS), ((0, 8 - CONV_D_TAPS), (0, 0))),
            }
        if stage == "ffn_down0":
            gathers[2] = _gather_forward("gather2_forward", gathers[2], after)
            return {"dep_down0": gathers[2]["token"]}
        if stage == "cd_in":
            l_in, l_out, l_ffn, l_down = _gather_wait("gather2_wait", gathers[2], after)
            return {"wt_cd_in": l_in.reshape(CD_IN, D), "w_cd_out": l_out.reshape(D, D),
                    "wt_ffn_in1": l_ffn.reshape(2 * DFF, D), "w_ffn_down1": l_down.reshape(DFF, D)}
        return {}

    scatters = {}
    rides_with = {"w_ffn_down1": "wt_ffn_in1", "w_cd_out": "wt_cd_in", "w_ffn_down0": "wt_ffn_in0"}
    held = {}
    smalls = {}

    def small_exchange(name, block):
        land = _landing((NDEV,) + block.shape, F32, [((me,), block)])
        return _exchange_start(name, [block], [land], [(one(0), slot(0))])

    def on_grad(key, arr):
        if key == "small":
            parts = [arr["sgu_norm_g"], arr["sgu_norm_b"], arr["sgu_w"], arr["sgu_bias"], arr["q_norm_g"],
                     arr["k_norm_g"], arr["cd_norm_g"], arr["conv_c_w32"][:CONV_C_TAPS], arr["conv_c_b"], arr["c_ln_g"],
                     arr["c_ln_b"], arr["conv_d_w8"][:CONV_D_TAPS], arr["ffn_norm_g0"], arr["ffn_norm_g1"],
                     arr["loss_cols"]]
            smalls["sizes"] = [p.size for p in parts]
            rows = -(-sum(smalls["sizes"]) // 1024) * 8
            smalls["early"] = small_exchange("small_start", _pack_rows(parts, rows))
            return smalls["early"]["token"]
        if key in rides_with:
            held[rides_with[key]] = (key, arr)
            return None
        group = ([held.pop(key)] if key in held else []) + [(key, arr)]
        srcs, lands, items = [], [], []
        for n, (k, a) in enumerate(group):
            if k.startswith("wt_ffn_in"):
                src = a.reshape(2, NDEV, r_ff, D)
                own = lax.dynamic_slice_in_dim(src, me, 1, axis=1)
                lands.append(lax.dynamic_update_slice(lax.empty(src.shape, BF16), own, (0, me, 0, 0)))
                items += [((lambda S, j, n=n: S[n].at[0, j]), slot2(n, 0)), ((lambda S, j, n=n: S[n].at[1, j]), slot2(n, 1))]
            else:
                rows = a.shape[0] // NDEV
                src = a.reshape(NDEV, rows, D)
                own = lax.dynamic_index_in_dim(src, me, 0, keepdims=False)
                lands.append(_landing((1, NDEV, rows, D), BF16, [((0, me), own)]))
                items.append(((lambda S, j, n=n: S[n].at[j]), slot2(n, 0)))
            srcs.append(src)
        st = _exchange_start(f"scatter_{key}_start", srcs, lands, items)
        scatters[key] = (st, [k for k, _ in group])
        return st["token"]

    W = {
        "dep_first": gathers[0]["token"],
        "ab_norm_g": w["ab_norm_g"], "sgu_norm_g": w["sgu_norm_g"], "sgu_norm_b": w["sgu_norm_b"],
        "sgu_w": w["sgu_w"][0], "sgu_bias": w["sgu_bias"][0], "q_norm_g": w["q_norm_g"][0],
        "k_norm_g": w["k_norm_g"][0], "ffn_norm_g": w["ffn_norm_g"],
    }

    loss_cols, grad_x, G = _local_step(x[0], loss_target[0], W, fetch, on_grad)

    late_small = small_exchange("small_late_start", G["ab_norm_g"])
    landed = {}

    def wait_scatters(name, group_keys, others, after):
        res = _exchange_wait(name, [scatters[gk][0] for gk in group_keys] + others, after)
        for gk, lands in zip(group_keys, res):
            landed.update(zip(scatters[gk][1], lands))
        return [lands[0] for lands in res[len(group_keys):]]

    small_land, = wait_scatters("scatter_wait_early", ["wt_ffn_in1", "wt_cd_in", "wt_ffn_in0", "w_ab_out"],
                                [smalls["early"]], late_small["token"])

    grads, deltas, new_m, new_v = {}, {}, {}, {}
    done = []

    def put(name, res):
        grads[name], deltas[name], new_m[name], new_v[name] = res

    def adam(name, lands, sel, transposed):
        flip = (lambda a: jnp.swapaxes(a, 1, 2)) if transposed else (lambda a: a)
        res = _adam_stacked(lands, sel, flip(w[name]), flip(m[name]), flip(v[name]), f"adam_{name}")
        done.append(res[1])
        put(name, [flip(r) for r in res])

    ffn_in_lands = [landed["wt_ffn_in0"], landed["wt_ffn_in1"]]
    adam("cd_w_in", [landed["wt_cd_in"]], 0, True)
    adam("ffn_w_gate", ffn_in_lands, 0, True)
    adam("ffn_w_up", ffn_in_lands, 1, True)
    adam("cd_w_out", [landed["w_cd_out"]], 0, False)
    adam("ab_w_out", [landed["w_ab_out"]], 0, False)
    adam("ffn_w_down", [landed["w_ffn_down0"], landed["w_ffn_down1"]], 0, False)

    red = _sum_slots(small_land).reshape(-1)
    offs = [0]
    for s in smalls["sizes"]:
        offs.append(offs[-1] + s)
    seg = [red[offs[i]:offs[i + 1]] for i in range(len(smalls["sizes"]))]
    loss = jnp.sum(seg[14])

    def own_channels(full, taps):
        return lax.dynamic_slice_in_dim(full.reshape(taps, 512), me * SHARD_C, SHARD_C, axis=1)

    g_small = {
        "sgu_norm_g": seg[0].reshape(1, 512), "sgu_norm_b": seg[1].reshape(1, 512),
        "sgu_w": seg[2].reshape(512, 128), "sgu_bias": seg[3].reshape(4, 128), "q_norm_g": seg[4].reshape(3, 64),
        "k_norm_g": seg[5].reshape(3, 64),
        "cd_norm_g": lax.dynamic_slice_in_dim(seg[6].reshape(1, D), me * (D // NDEV), D // NDEV, axis=1),
        "conv_c_w": own_channels(seg[7], CONV_C_TAPS), "conv_c_b": own_channels(seg[8], 1),
        "c_ln_g": own_channels(seg[9], 1), "c_ln_b": own_channels(seg[10], 1),
        "conv_d_w": own_channels(seg[11], CONV_D_TAPS),
        "ffn_norm_g": jnp.concatenate([seg[12].reshape(1, D), seg[13].reshape(1, D)], axis=0),
    }

    d_s, m_s, v_s = _adam_small([w[n].reshape(s) for n, s in SMALL_2D], [g_small[n] for n, _ in SMALL_2D],
                                [m[n].reshape(s) for n, s in SMALL_2D], [v[n].reshape(s) for n, s in SMALL_2D])
    for i, (n, _) in enumerate(SMALL_2D):
        shape = w[n].shape
        grads[n], deltas[n] = g_small[n].reshape(shape), d_s[i].reshape(shape)
        new_m[n], new_v[n] = m_s[i].reshape(shape), v_s[i].reshape(shape)
    done.append(d_s[0])

    late_land, = wait_scatters("scatter_wait_last", ["wt_ab_in"], [late_small], list(done))
    put("ab_norm_g", _adam_of_slots(late_land, w["ab_norm_g"], m["ab_norm_g"], v["ab_norm_g"], "adam_ab_norm_g"))
    adam("ab_w_in", [landed["wt_ab_in"]], 0, True)

    return (loss, grad_x[None], *[grads[n] for n in WEIGHT_NAMES], *[deltas[n] for n in WEIGHT_NAMES],
            *[new_m[n] for n in WEIGHT_NAMES], *[new_v[n] for n in WEIGHT_NAMES])
```

```python
import jax
import jax.numpy as jnp
import numpy as np
from jax import lax
from jax.experimental import pallas as pl
from jax.experimental.pallas import tpu as pltpu

F32 = jnp.float32
BF16 = jnp.bfloat16

T = 4096
D = 1024
NDEV = 8
EPS = 1e-6
NEG_INF = -1e30
DFF = 2816
AB_IN = 5632
CD_IN = 2560
HEAD = 64
PAIR = 128
NPAIR = 4
NBACK = 128
DIL_RATES = (1, 4, 16)
ROPE_HALF = 8
ROPE_THETA = 500000.0
CONV_C_TAPS = 31
CONV_D_TAPS = 3
HALO = 32
ATTN_BWD_UNROLL = 4
MAX_ROW_STRIDE = 4

ADAM_LR = 0.001
ADAM_B1 = 0.9
ADAM_B2 = 0.999
ADAM_EPS = 1e-08
ADAM_WD = 0.01
ADAM_STEP = 10
ADAM_C1 = 1.0 / (1.0 - ADAM_B1 ** ADAM_STEP)
ADAM_C2 = 1.0 / (1.0 - ADAM_B2 ** ADAM_STEP)

VMEM_LIMIT_MB = 48
MESH = pl.DeviceIdType.MESH
HBM_SPEC = pl.BlockSpec(memory_space=pl.ANY)


def _cparams(ngrid, vmem_mb=VMEM_LIMIT_MB):
    return pltpu.CompilerParams(dimension_semantics=("arbitrary",) * ngrid,
                                vmem_limit_bytes=vmem_mb * 1024 * 1024)


def _pick(n, options):
    for o in options:
        if n % o == 0:
            return o
    raise ValueError(f"no tile for {n} in {options}")


def _sds(shape, dtype):
    return jax.ShapeDtypeStruct(shape, dtype)


def _sigmoid(x):
    return 1.0 / (1.0 + jnp.exp(-x))


def _sigmoid_bf16(x):
    return 0.5 * jnp.tanh(0.5 * x) + 0.5


def _gelu(z):
    return 0.5 * z * (1.0 + lax.erf(z * 0.7071067811865476))


def _gelu_grad(z):
    return 0.5 * (1.0 + lax.erf(z * 0.7071067811865476)) + z * jnp.exp(-0.5 * z * z) * 0.3989422804014327


def _mm_nt(a, wt, name, out_dtype=BF16, dep=None):
    M, K = a.shape
    N = wt.shape[0]
    tn = _pick(N, (512, 256))

    def body(a_ref, w_ref, *rest):
        o_ref = rest[-1]
        for r0 in range(0, M, 1024):
            o_ref[r0:r0 + 1024, :] = lax.dot_general(
                a_ref[r0:r0 + 1024, :], w_ref[...], (((1,), (1,)), ((), ())),
                preferred_element_type=F32).astype(o_ref.dtype)

    in_specs = [pl.BlockSpec((M, K), lambda j: (0, 0), pipeline_mode=pl.Buffered(1)),
                pl.BlockSpec((tn, K), lambda j: (j, 0))]
    args = [a, wt]
    if dep is not None:
        in_specs.append(HBM_SPEC)
        args.append(dep)
    return pl.pallas_call(
        body, name=name, grid=(N // tn,), in_specs=in_specs, out_specs=pl.BlockSpec((M, tn), lambda j: (0, j)),
        out_shape=_sds((M, N), out_dtype), compiler_params=_cparams(1))(*args)


EPI_ROWS = 256


def _mm_nt_rows(a, wt, name, epilogue, side, side_specs, out_specs, out_shape, sums=(), dep=None, tm=512):
    M, K = a.shape
    N = wt.shape[0]
    ns, no = len(side), len(out_shape)

    def body(a_ref, w_ref, *rest):
        side_refs, outs, acc = rest[0:ns], rest[-1 - no:-1], rest[-1]
        acc[...] = lax.dot_general(a_ref[...], w_ref[...], (((1,), (1,)), ((), ())), preferred_element_type=F32)

        @pl.when(pl.program_id(0) == 0)
        def _():
            for j in sums:
                outs[j][...] = jnp.zeros_like(outs[j])

        for r0 in range(0, tm, EPI_ROWS):
            rows = slice(r0, r0 + EPI_ROWS)
            epilogue(acc[rows, :].astype(BF16).astype(F32), rows, side_refs, outs)

    in_specs = [pl.BlockSpec((tm, K), lambda i: (i, 0)),
                pl.BlockSpec((N, K), lambda i: (0, 0), pipeline_mode=pl.Buffered(1))] + list(side_specs)
    args = [a, wt, *side]
    if dep is not None:
        in_specs.append(HBM_SPEC)
        args.append(dep)
    return pl.pallas_call(
        body, name=name, grid=(M // tm,), in_specs=in_specs, out_specs=list(out_specs), out_shape=list(out_shape),
        scratch_shapes=[pltpu.VMEM((tm, N), F32)], compiler_params=_cparams(1))(*args)


def _mm_nn(a, w, name, mode, resid, gain=None, tgt=None, dep=None, tm=512):
    M, K = a.shape
    N = w.shape[1]
    side = gain if mode == "rms" else tgt

    def body(a_ref, w_ref, resid_ref, side_ref, *rest):
        outs, acc = rest[-3 if mode == "rms" else -4:-1], rest[-1]
        i = pl.program_id(0)
        acc[...] = jnp.dot(a_ref[...], w_ref[...], preferred_element_type=F32)

        if mode == "loss":
            @pl.when(i == 0)
            def _():
                outs[2][...] = jnp.zeros_like(outs[2])

        for r0 in range(0, tm, EPI_ROWS):
            rows = slice(r0, r0 + EPI_ROWS)
            v = acc[rows, :] + resid_ref[rows, :]
            if mode == "rms":
                outs[0][rows, :] = v
                r = lax.rsqrt(jnp.mean(v * v, axis=-1, keepdims=True) + EPS)
                outs[1][rows, :] = (v * r * side_ref[...]).astype(BF16)
            else:
                d = v - side_ref[rows, :]
                outs[2][...] += jnp.sum(d * d, axis=0, keepdims=True) * (0.5 / N)
                dy = d * (1.0 / N)
                outs[0][rows, :] = dy
                outs[1][rows, :] = dy.astype(BF16)

    row = pl.BlockSpec((tm, N), lambda i: (i, 0))
    vec = pl.BlockSpec((1, N), lambda i: (0, 0))
    in_specs = [pl.BlockSpec((tm, K), lambda i: (i, 0)),
                pl.BlockSpec((K, N), lambda i: (0, 0), pipeline_mode=pl.Buffered(1)), row,
                vec if mode == "rms" else row]
    args = [a, w, resid, side]
    if dep is not None:
        in_specs.append(HBM_SPEC)
        args.append(dep)
    if mode == "rms":
        out_specs, out_shape = [row, row], [_sds((M, N), F32), _sds((M, N), BF16)]
    else:
        out_specs, out_shape = [row, row, vec], [_sds((M, N), F32), _sds((M, N), BF16), _sds((1, N), F32)]
    return pl.pallas_call(
        body, name=name, grid=(M // tm,), in_specs=in_specs, out_specs=out_specs, out_shape=out_shape,
        scratch_shapes=[pltpu.VMEM((tm, N), F32)], compiler_params=_cparams(1))(*args)


def _mm_dh_rms_bwd(a, w, x, gain, dres, name, dep=None, tm=512, bf16_copy=True):
    parts = a.shape[0] if a.ndim == 3 else 1
    M, Kp = a.shape[-2], a.shape[-1]
    N = w.shape[1]
    nblk = M // tm
    assert nblk % 2 == 0

    def body(a_ref, w_ref, x_ref, g_ref, dres_ref, *rest):
        dg_ref, acc0, acc1 = rest[-3:]
        dx_ref = rest[-5] if bf16_copy else rest[-4]
        dxb_ref = rest[-4] if bf16_copy else None
        i = pl.program_id(0)

        def matmul(acc):
            if parts == 1:
                acc[...] = jnp.dot(a_ref[...], w_ref[...], preferred_element_type=F32)
            else:
                d = jnp.dot(a_ref[0], w_ref[0:Kp, :], preferred_element_type=F32)
                for p in range(1, parts):
                    d = d + jnp.dot(a_ref[p], w_ref[p * Kp:(p + 1) * Kp, :], preferred_element_type=F32)
                acc[...] = d

        def finish(acc):
            for r0 in range(0, tm, EPI_ROWS // 2):
                rows = slice(r0, r0 + EPI_ROWS // 2)
                v = acc[rows, :]
                xf = x_ref[rows, :]
                r = lax.rsqrt(jnp.mean(xf * xf, axis=-1, keepdims=True) + EPS)
                xhat = xf * r
                dg_ref[...] += jnp.sum(v * xhat, axis=0, keepdims=True)
                dxh = v * g_ref[...]
                tot = dres_ref[rows, :] + r * (dxh - xhat * jnp.mean(dxh * xhat, axis=-1, keepdims=True))
                dx_ref[rows, :] = tot
                if bf16_copy:
                    dxb_ref[rows, :] = tot.astype(BF16)

        @pl.when(i == 0)
        def _():
            dg_ref[...] = jnp.zeros_like(dg_ref)
            matmul(acc0)

        @pl.when((i > 0) & (i < nblk) & (i % 2 == 1))
        def _():
            matmul(acc1)
            finish(acc0)

        @pl.when((i > 0) & (i < nblk) & (i % 2 == 0))
        def _():
            matmul(acc0)
            finish(acc1)

        @pl.when(i == nblk)
        def _():
            finish(acc1)

    last = nblk - 1
    row = pl.BlockSpec((tm, N), lambda i: (jnp.maximum(i - 1, 0), 0))
    vec = pl.BlockSpec((1, N), lambda i: (0, 0))
    if a.ndim == 3:
        a_spec = pl.BlockSpec((parts, tm, Kp), lambda i: (0, jnp.minimum(i, last), 0))
    else:
        a_spec = pl.BlockSpec((tm, Kp), lambda i: (jnp.minimum(i, last), 0))
    w_spec = pl.BlockSpec((parts * Kp, N), lambda i: (0, 0), pipeline_mode=pl.Buffered(1))
    in_specs = [a_spec, w_spec, row, vec, row]
    args = [a, w, x, gain, dres]
    if dep is not None:
        in_specs.append(HBM_SPEC)
        args.append(dep)
    return pl.pallas_call(
        body, name=name, grid=(nblk + 1,), in_specs=in_specs,
        out_specs=[row, row, vec] if bf16_copy else [row, vec],
        out_shape=([_sds((M, N), F32), _sds((M, N), BF16), _sds((1, N), F32)] if bf16_copy
                   else [_sds((M, N), F32), _sds((1, N), F32)]),
        scratch_shapes=[pltpu.VMEM((tm, N), F32), pltpu.VMEM((tm, N), F32)], compiler_params=_cparams(1, 56))(*args)


def _mm_tn(a, b, name, out_dtype=BF16, tt=2048, dep=None):
    parts = a.shape[0] if a.ndim == 3 else 1
    Tt, Mp = a.shape[-2], a.shape[-1]
    N = b.shape[1]
    tn = _pick(Mp, (1408, 1280, 1024, 512))
    jper = Mp // tn
    nt = Tt // tt

    def body(a_ref, b_ref, *rest):
        o_ref, acc = rest[-2:]
        t = pl.program_id(1)

        @pl.when(t == 0)
        def _():
            acc[...] = jnp.zeros_like(acc)

        rows = pl.ds(pl.multiple_of(t * tt, tt), tt)
        acc[...] += lax.dot_general(a_ref[...], b_ref[rows, :], (((0,), (0,)), ((), ())),
                                    preferred_element_type=F32)

        @pl.when(t == nt - 1)
        def _():
            o_ref[...] = acc[...].astype(o_ref.dtype)

    if a.ndim == 3:
        a_spec = pl.BlockSpec((None, tt, tn), lambda j, t: (j // jper, t, j % jper))
    else:
        a_spec = pl.BlockSpec((tt, tn), lambda j, t: (t, j))
    in_specs = [a_spec, pl.BlockSpec((Tt, N), lambda j, t: (0, 0), pipeline_mode=pl.Buffered(1))]
    args = [a, b]
    if dep is not None:
        in_specs.append(HBM_SPEC)
        args.append(dep)
    return pl.pallas_call(
        body, name=name, grid=(parts * jper, nt), in_specs=in_specs,
        out_specs=pl.BlockSpec((tn, N), lambda j, t: (j, 0)),
        out_shape=_sds((parts * Mp, N), out_dtype), scratch_shapes=[pltpu.VMEM((tn, N), F32)],
        compiler_params=_cparams(2))(*args)


FFN_ROWS = 2048


def _ffn_in(h, wt_in, name, tn=256):
    nj = DFF // tn

    def body(h_ref, wg_ref, wu_ref, p_ref, act_ref):
        nt = (((1,), (1,)), ((), ()))
        for r0 in range(0, T, FFN_ROWS):
            rows = slice(r0, r0 + FFN_ROWS)
            g = lax.dot_general(h_ref[rows, :], wg_ref[...], nt, preferred_element_type=F32).astype(BF16)
            u = lax.dot_general(h_ref[rows, :], wu_ref[...], nt, preferred_element_type=F32).astype(BF16)
            p_ref[0, rows, :] = g
            p_ref[1, rows, :] = u
            act_ref[rows, :] = g * _sigmoid_bf16(g) * u

    return pl.pallas_call(
        body, name=name, grid=(nj,),
        in_specs=[pl.BlockSpec((T, D), lambda j: (0, 0), pipeline_mode=pl.Buffered(1)),
                  pl.BlockSpec((tn, D), lambda j: (j, 0)), pl.BlockSpec((tn, D), lambda j: (j + nj, 0))],
        out_specs=[pl.BlockSpec((2, T, tn), lambda j: (0, 0, j)), pl.BlockSpec((T, tn), lambda j: (0, j))],
        out_shape=[_sds((2, T, DFF), BF16), _sds((T, DFF), BF16)], compiler_params=_cparams(1))(h, wt_in, wt_in)


def _ffn_dact(dyb, w_down, p3, name, tn=256, dep=None):
    def body(dy_ref, w_ref, p_ref, *rest):
        o_ref = rest[-1]
        for r0 in range(0, T, FFN_ROWS):
            rows = slice(r0, r0 + FFN_ROWS)
            da = lax.dot_general(dy_ref[rows, :], w_ref[...], (((1,), (1,)), ((), ())),
                                 preferred_element_type=F32).astype(BF16)
            g = p_ref[0, rows, :]
            u = p_ref[1, rows, :]
            sg = _sigmoid_bf16(g)
            gs = g * sg
            o_ref[0, rows, :] = (da * u) * (sg + gs * (1.0 - sg))
            o_ref[1, rows, :] = da * gs

    pspec = pl.BlockSpec((2, T, tn), lambda j: (0, 0, j))
    in_specs = [pl.BlockSpec((T, D), lambda j: (0, 0), pipeline_mode=pl.Buffered(1)),
                pl.BlockSpec((tn, D), lambda j: (j, 0)), pspec]
    args = [dyb, w_down, p3]
    if dep is not None:
        in_specs.append(HBM_SPEC)
        args.append(dep)
    return pl.pallas_call(
        body, name=name, grid=(DFF // tn,), in_specs=in_specs, out_specs=pspec,
        out_shape=_sds((2, T, DFF), BF16), compiler_params=_cparams(1))(*args)


def _rms_fwd(x, g, name, tm=512, dep=None):
    def body(x_ref, g_ref, *rest):
        h_ref = rest[-1]
        xf = x_ref[...]
        r = lax.rsqrt(jnp.mean(xf * xf, axis=-1, keepdims=True) + EPS)
        h_ref[...] = (xf * r * g_ref[...]).astype(BF16)

    in_specs = [pl.BlockSpec((tm, D), lambda i: (i, 0)), pl.BlockSpec((1, D), lambda i: (0, 0))]
    args = [x, g]
    if dep is not None:
        in_specs.append(HBM_SPEC)
        args.append(dep)
    return pl.pallas_call(
        body, name=name, grid=(T // tm,), in_specs=in_specs, out_specs=pl.BlockSpec((tm, D), lambda i: (i, 0)),
        out_shape=_sds((T, D), BF16), compiler_params=_cparams(1))(*args)


def _tril_mask():
    r = lax.broadcasted_iota(jnp.int32, (128, 128), 0)
    c = lax.broadcasted_iota(jnp.int32, (128, 128), 1)
    return r >= c


def _mix_a_fwd(pab, sgu_g, sgu_b, sgu_w, sgu_bias3, tm=512):
    def body(zu_ref, zv_ref, g_ref, b_ref, w_ref, bias_ref, o_ref):
        u = _gelu(zu_ref[...].astype(F32))
        v = _gelu(zv_ref[...].astype(F32))
        mu = jnp.mean(v, axis=-1, keepdims=True)
        vc = v - mu
        rstd = lax.rsqrt(jnp.mean(vc * vc, axis=-1, keepdims=True) + EPS)
        vn = (vc * rstd * g_ref[...] + b_ref[...]).astype(BF16)
        tri = _tril_mask()
        for gi in range(4):
            wg = jnp.where(tri, w_ref[gi], 0.0).astype(BF16)
            bg = bias_ref[gi]
            for c in range(tm // 128):
                rs, cs = slice(c * 128, (c + 1) * 128), slice(gi * 128, (gi + 1) * 128)
                mixed = jnp.dot(wg, vn[rs, cs], preferred_element_type=F32) + bg
                o_ref[rs, cs] = (u[rs, cs] * mixed).astype(BF16)

    half = pl.BlockSpec((tm, 512), lambda i: (i, 0))
    return pl.pallas_call(
        body, name="mix_a_fwd", grid=(T // tm,),
        in_specs=[half, pl.BlockSpec((tm, 512), lambda i: (i, 1)),
                  pl.BlockSpec((1, 512), lambda i: (0, 0)), pl.BlockSpec((1, 512), lambda i: (0, 0)),
                  pl.BlockSpec((4, 128, 128), lambda i: (0, 0, 0)), pl.BlockSpec((4, 128, 1), lambda i: (0, 0, 0))],
        out_specs=half, out_shape=_sds((T, D), BF16), compiler_params=_cparams(1),
    )(pab, pab, sgu_g, sgu_b, sgu_w, sgu_bias3)


def _rope_tables():
    pos = np.arange(T, dtype=np.float32)
    inv_freq = np.float32(ROPE_THETA) ** (-np.arange(ROPE_HALF, dtype=np.float32) * np.float32(2.0 / (2 * ROPE_HALF)))
    ang = (pos[:, None] * inv_freq[None, :]).astype(np.float32)
    cos, sin = np.cos(ang), np.sin(ang)
    z8 = np.zeros((T, ROPE_HALF), np.float32)
    rest = np.zeros((T, HEAD - 2 * ROPE_HALF), np.float32)
    c64 = np.concatenate([cos, cos, rest + 1.0], axis=1)
    s1 = np.concatenate([z8, sin, rest], axis=1)
    s2 = np.concatenate([-sin, z8, rest], axis=1)
    return tuple(jnp.asarray(np.tile(t, (1, 2)).astype(np.float32)) for t in (c64, s1, s2))


def _lo_mask(shape):
    return lax.broadcasted_iota(jnp.int32, shape, 1) < HEAD


def _seg_mean(x, lo):
    s_all = jnp.sum(x, axis=-1, keepdims=True)
    s_lo = jnp.sum(jnp.where(lo, x, 0.0), axis=-1, keepdims=True)
    return jnp.where(lo, s_lo, s_all - s_lo) * (1.0 / HEAD)


def _head_blocks():
    r = lax.broadcasted_iota(jnp.int32, (PAIR, PAIR), 0) < HEAD
    c = lax.broadcasted_iota(jnp.int32, (PAIR, PAIR), 1) < HEAD
    return jnp.where(r == c, 1.0, 0.0).astype(BF16)


def _seg_mean_mxu(x, blocks):
    return jnp.dot(x.astype(BF16), blocks, preferred_element_type=F32) * (1.0 / HEAD)


def _rope(n, c, s1, s2):
    return n * c + pltpu.roll(n, ROPE_HALF, 1) * s1 + pltpu.roll(n, PAIR - ROPE_HALF, 1) * s2


def _rope_t(dy, c, s1, s2):
    return dy * c - pltpu.roll(dy, PAIR - ROPE_HALF, 1) * s2 - pltpu.roll(dy, ROPE_HALF, 1) * s1


def _prep_fwd(pab, qg, kg, tabs, tm=512):
    def body(p_ref, qg_ref, kg_ref, c_ref, s1_ref, s2_ref, *outs):
        blocks = _head_blocks()
        c, s1, s2 = c_ref[...], s1_ref[...], s2_ref[...]
        for g in range(3):
            qn_ref, kn_ref, v_ref = outs[3 * g:3 * g + 3]
            for p in range(NPAIR):
                for which, gains, dst in ((0, qg_ref, qn_ref), (1, kg_ref, kn_ref)):
                    col = (2 + 3 * which + g) * 512 + p * PAIR
                    xr = p_ref[:, col:col + PAIR].astype(F32)
                    rinv = lax.rsqrt(_seg_mean_mxu(xr * xr, blocks) + EPS)
                    outs[9 + 2 * g + which][p] = rinv.astype(BF16)
                    dst[p] = _rope(xr * rinv * gains[g:g + 1, :], c, s1, s2)
                col = (8 + g) * 512 + p * PAIR
                v_ref[p] = p_ref[:, col:col + PAIR].astype(F32)

    pm = pl.BlockSpec((NPAIR, tm, PAIR), lambda i: (0, i, 0))
    tab = pl.BlockSpec((tm, PAIR), lambda i: (i, 0))
    gain = pl.BlockSpec((3, PAIR), lambda i: (0, 0))
    res = pl.pallas_call(
        body, name="prep_fwd", grid=(T // tm,),
        in_specs=[pl.BlockSpec((tm, AB_IN), lambda i: (i, 0)), gain, gain, tab, tab, tab],
        out_specs=[pm] * 15, out_shape=[_sds((NPAIR, T, PAIR), F32)] * 9 + [_sds((NPAIR, T, PAIR), BF16)] * 6,
        compiler_params=_cparams(1))(pab, qg, kg, *tabs)
    return res[0:9], res[9:15]


def _res_index(it, rate):
    window = NBACK * rate
    b = it // rate
    rho = it % rate
    start = b * window + rho
    startp = jnp.maximum(start - window, rho)
    kmin = jnp.where(b > 0, 0, NBACK)
    return start, startp, kmin


def _rows(start, rate):
    if rate == 1:
        return pl.ds(pl.multiple_of(start, NBACK), NBACK)
    return pl.ds(start, NBACK, stride=rate)


def _band_bias():
    qs = lax.broadcasted_iota(jnp.int32, (2 * NBACK, 2 * NBACK), 0)
    kj = lax.broadcasted_iota(jnp.int32, (2 * NBACK, 2 * NBACK), 1)
    dist = (qs & (NBACK - 1)) + NBACK - kj
    both = (dist >= 0) & (dist <= NBACK)
    return jnp.where(both, 0.0, NEG_INF), jnp.where(both & (kj >= NBACK), 0.0, NEG_INF)


def _attn_fwd_block(q, kcat, vcat, first, lo, biases):
    vcat1 = jnp.concatenate([vcat, jnp.ones((2 * NBACK, PAIR), BF16)], axis=1)
    q2 = jnp.concatenate([jnp.where(lo, q, 0.0), jnp.where(lo, 0.0, q)], axis=0).astype(BF16)
    s = lax.dot_general(q2, kcat, (((1,), (1,)), ((), ())), preferred_element_type=F32)
    s = s + jnp.where(first, biases[1], biases[0])
    m = jnp.max(s, axis=-1, keepdims=True)
    ol = jnp.dot(jnp.exp(s - m).astype(BF16), vcat1, preferred_element_type=F32)
    o2 = ol[:, 0:PAIR] / ol[:, PAIR:]
    ls = m + jnp.log(ol[:, PAIR:])
    return jnp.where(lo, o2[0:NBACK], o2[NBACK:]), jnp.where(lo, ls[0:NBACK], ls[NBACK:])


def _attn_fwd(qn, kn, v, rate, name, dep=None):
    if rate > MAX_ROW_STRIDE:
        return _attn_fwd_gathered(qn, kn, v, rate, name, dep)

    def body(q_ref, k_ref, v_ref, *rest):
        o_ref, l_ref = rest[-2:]
        lo = _lo_mask((NBACK, PAIR))
        biases = _band_bias()

        def step(it, carry):
            start, startp, kmin = _res_index(it, rate)
            q = q_ref[_rows(start, rate), :] * (HEAD ** -0.5)
            kcat = jnp.concatenate([k_ref[_rows(startp, rate), :], k_ref[_rows(start, rate), :]], axis=0).astype(BF16)
            vcat = jnp.concatenate([v_ref[_rows(startp, rate), :], v_ref[_rows(start, rate), :]], axis=0).astype(BF16)
            o, ls = _attn_fwd_block(q, kcat, vcat, kmin != 0, lo, biases)
            o_ref[_rows(start, rate), :] = o
            l_ref[_rows(start, rate), :] = ls
            return carry

        lax.fori_loop(0, T // NBACK, step, 0, unroll=4)

    pm = pl.BlockSpec((None, T, PAIR), lambda p: (p, 0, 0))
    in_specs, args = [pm, pm, pm], [qn, kn, v]
    if dep is not None:
        in_specs.append(HBM_SPEC)
        args.append(dep)
    return pl.pallas_call(
        body, name=name, grid=(NPAIR,), in_specs=in_specs, out_specs=[pm, pm],
        out_shape=[_sds((NPAIR, T, PAIR), F32)] * 2, compiler_params=_cparams(1))(*args)


def _attn_fwd_gathered(qn, kn, v, rate, name, dep):
    n = T // rate
    nblk = n // NBACK

    def body(q_hbm, k_hbm, v_hbm, *rest):
        o_hbm, l_hbm, qb, kb, vb, ob, lb, in_sem, out_sem = rest[-9:]
        p = pl.program_id(0)
        slot = p % 2

        def loads(pair, s):
            return [pltpu.make_async_copy(x.at[pair, :, r, :], buf.at[s, r], in_sem.at[3 * s + a])
                    for a, (x, buf) in enumerate(((q_hbm, qb), (k_hbm, kb), (v_hbm, vb))) for r in range(rate)]

        def stores(pair, s):
            return [pltpu.make_async_copy(buf.at[s, r], x.at[pair, :, r, :], out_sem.at[2 * s + a])
                    for a, (x, buf) in enumerate(((o_hbm, ob), (l_hbm, lb))) for r in range(rate)]

        @pl.when(p == 0)
        def _():
            for c in loads(0, 0):
                c.start()

        @pl.when(p + 1 < NPAIR)
        def _():
            for c in loads(p + 1, 1 - slot):
                c.start()

        for c in loads(p, slot):
            c.wait()

        @pl.when(p >= 2)
        def _():
            for c in stores(p - 2, slot):
                c.wait()

        lo = _lo_mask((NBACK, PAIR))
        biases = _band_bias()

        def step(it, carry):
            b, r = it % nblk, it // nblk
            cur = pl.ds(pl.multiple_of(b * NBACK, NBACK), NBACK)
            prev = pl.ds(pl.multiple_of(jnp.maximum(b - 1, 0) * NBACK, NBACK), NBACK)
            q = qb[slot, r, cur, :] * (HEAD ** -0.5)
            kcat = jnp.concatenate([kb[slot, r, prev, :], kb[slot, r, cur, :]], axis=0).astype(BF16)
            vcat = jnp.concatenate([vb[slot, r, prev, :], vb[slot, r, cur, :]], axis=0).astype(BF16)
            o, ls = _attn_fwd_block(q, kcat, vcat, b == 0, lo, biases)
            ob[slot, r, cur, :] = o
            lb[slot, r, cur, :] = ls
            return carry

        lax.fori_loop(0, T // NBACK, step, 0, unroll=4)

        for c in stores(p, slot):
            c.start()

        @pl.when(p == NPAIR - 1)
        def _():
            for c in stores(p - 1, 1 - slot) + stores(p, slot):
                c.wait()

    by_residue = lambda a: a.reshape(NPAIR, n, rate, PAIR)
    in_specs, args = [HBM_SPEC] * 3, [by_residue(qn), by_residue(kn), by_residue(v)]
    if dep is not None:
        in_specs.append(HBM_SPEC)
        args.append(dep)
    o, l = pl.pallas_call(
        body, name=name, grid=(NPAIR,), in_specs=in_specs, out_specs=[HBM_SPEC] * 2,
        out_shape=[_sds((NPAIR, n, rate, PAIR), F32)] * 2,
        scratch_shapes=[pltpu.VMEM((2, rate, n, PAIR), F32)] * 5
        + [pltpu.SemaphoreType.DMA((6,)), pltpu.SemaphoreType.DMA((4,))],
        compiler_params=_cparams(1))(*args)
    return o.reshape(NPAIR, T, PAIR), l.reshape(NPAIR, T, PAIR)


def _merge_fwd(cat_ab, outs, lses, tm=512):
    def body(cat_in, o0, o1, o2, l0, l1, l2, cat_ref, lse_ref):
        del cat_in
        for p in range(NPAIR):
            a0, a1, a2 = l0[p], l1[p], l2[p]
            m = jnp.maximum(jnp.maximum(a0, a1), a2)
            w0, w1, w2 = jnp.exp(a0 - m), jnp.exp(a1 - m), jnp.exp(a2 - m)
            s = w0 + w1 + w2
            b = (w0 * o0[p] + w1 * o1[p] + w2 * o2[p]) / s
            cat_ref[:, p * PAIR:(p + 1) * PAIR] = b.astype(BF16)
            lse_ref[p] = m + jnp.log(s)

    pm = pl.BlockSpec((NPAIR, tm, PAIR), lambda i: (0, i, 0))
    return pl.pallas_call(
        body, name="merge_fwd", grid=(T // tm,),
        in_specs=[pl.BlockSpec(memory_space=pl.ANY)] + [pm] * 6,
        out_specs=[pl.BlockSpec((tm, 512), lambda i: (i, 1)), pm],
        out_shape=[_sds((T, D), BF16), _sds((NPAIR, T, PAIR), F32)],
        input_output_aliases={0: 0}, compiler_params=_cparams(1))(cat_ab, *outs, *lses)


def _d_cat_ab(dxb, w_ab_out, cat, dep, tm=512):
    def epilogue(d, rows, side, outs):
        (b_ref,), (da_ref, dbp_ref, e_ref) = side, outs
        da_ref[rows, :] = d[:, 0:512].astype(BF16)
        lo = _lo_mask((EPI_ROWS, PAIR))
        for p in range(NPAIR):
            db = d[:, 512 + p * PAIR:512 + (p + 1) * PAIR]
            b = b_ref[rows, p * PAIR:(p + 1) * PAIR].astype(F32)
            dbp_ref[p, rows, :] = db
            e_ref[p, rows, :] = _seg_mean(db * b, lo) * float(HEAD)

    pm = pl.BlockSpec((NPAIR, tm, PAIR), lambda i: (0, i, 0))
    return _mm_nt_rows(
        dxb, w_ab_out, "mm_d_cat_ab", epilogue, [cat], [pl.BlockSpec((tm, 512), lambda i: (i, 1))],
        [pl.BlockSpec((tm, 512), lambda i: (i, 0)), pm, pm],
        [_sds((T, 512), BF16), _sds((NPAIR, T, PAIR), F32), _sds((NPAIR, T, PAIR), F32)], dep=dep, tm=tm)


def _attn_bwd_block(q, db, ev, ls, kcat, vcat, first, lo, biases):
    scale = HEAD ** -0.5
    nt = (((1,), (1,)), ((), ()))
    tn = (((0,), (0,)), ((), ()))
    q = q * scale
    q2 = jnp.concatenate([jnp.where(lo, q, 0.0), jnp.where(lo, 0.0, q)], axis=0).astype(BF16)
    db2 = jnp.concatenate([jnp.where(lo, db, 0.0), jnp.where(lo, 0.0, db)], axis=0).astype(BF16)
    ls2 = jnp.concatenate([ls[:, 0:1], ls[:, HEAD:HEAD + 1]], axis=0)
    ev2 = jnp.concatenate([ev[:, 0:1], ev[:, HEAD:HEAD + 1]], axis=0)
    s = lax.dot_general(q2, kcat, nt, preferred_element_type=F32)
    pt = jnp.exp(s + jnp.where(first, biases[1], biases[0]) - ls2)
    dp = lax.dot_general(db2, vcat, nt, preferred_element_type=F32)
    ds = (pt * (dp - ev2)).astype(BF16)
    dq2 = jnp.dot(ds, kcat, preferred_element_type=F32) * scale
    dkc = lax.dot_general(ds, q2, tn, preferred_element_type=F32)
    dvc = lax.dot_general(pt.astype(BF16), db2, tn, preferred_element_type=F32)
    return jnp.where(lo, dq2[0:NBACK], dq2[NBACK:]), dkc, dvc


def _attn_bwd_loop(read, write, nblk):
    lo = _lo_mask((NBACK, PAIR))
    biases = _band_bias()

    def one(it, carry):
        dk_carry, dv_carry = carry
        rho = it // nblk
        b = it % nblk
        bp = jnp.maximum(b - 1, 0)
        kcat = jnp.concatenate([read(1, rho, bp), read(1, rho, b)], axis=0).astype(BF16)
        vcat = jnp.concatenate([read(2, rho, bp), read(2, rho, b)], axis=0).astype(BF16)
        dq, dkc, dvc = _attn_bwd_block(read(0, rho, b), read(3, rho, b), read(4, rho, b), read(5, rho, b), kcat, vcat,
                                       b == 0, lo, biases)
        write(0, rho, b, dq)
        write(1, rho, bp, dk_carry + dkc[0:NBACK])
        write(1, rho, b, dkc[NBACK:])
        write(2, rho, bp, dv_carry + dvc[0:NBACK])
        write(2, rho, b, dvc[NBACK:])
        return dkc[NBACK:], dvc[NBACK:]

    def step(i, carry):
        for u in range(ATTN_BWD_UNROLL):
            carry = one(i * ATTN_BWD_UNROLL + u, carry)
        return carry

    zero = jnp.zeros((NBACK, PAIR), F32)
    lax.fori_loop(0, T // NBACK // ATTN_BWD_UNROLL, step, (zero, zero))


def _attn_bwd(qn, kn, v, dbp, e, lse, rate, name):
    if rate > MAX_ROW_STRIDE:
        return _attn_bwd_gathered(qn, kn, v, dbp, e, lse, rate, name)
    window = NBACK * rate

    def body(*refs):
        rows = lambda rho, b: _rows(b * window + rho, rate)

        def write(j, rho, b, value):
            refs[6 + j][rows(rho, b), :] = value

        _attn_bwd_loop(lambda j, rho, b: refs[j][rows(rho, b), :], write, T // window)

    pm = pl.BlockSpec((None, T, PAIR), lambda p: (p, 0, 0))
    return pl.pallas_call(
        body, name=name, grid=(NPAIR,), in_specs=[pm] * 6, out_specs=[pm] * 3,
        out_shape=[_sds((NPAIR, T, PAIR), F32)] * 3, compiler_params=_cparams(1, 56))(qn, kn, v, dbp, e, lse)


def _attn_bwd_gathered(qn, kn, v, dbp, e, lse, rate, name):
    n = T // rate

    def body(*refs):
        ins, outs, in_bufs, out_bufs, (in_sem, out_sem) = refs[0:6], refs[6:9], refs[9:15], refs[15:18], refs[18:20]
        p = pl.program_id(0)
        slot = p % 2

        def loads(pair, s):
            return [pltpu.make_async_copy(x.at[pair, :, r, :], buf.at[s, r], in_sem.at[6 * s + a])
                    for a, (x, buf) in enumerate(zip(ins, in_bufs)) for r in range(rate)]

        def stores(pair, s):
            return [pltpu.make_async_copy(buf.at[s, r], x.at[pair, :, r, :], out_sem.at[3 * s + a])
                    for a, (x, buf) in enumerate(zip(outs, out_bufs)) for r in range(rate)]

        @pl.when(p == 0)
        def _():
            for c in loads(0, 0):
                c.start()

        @pl.when(p + 1 < NPAIR)
        def _():
            for c in loads(p + 1, 1 - slot):
                c.start()

        for c in loads(p, slot):
            c.wait()

        @pl.when(p >= 2)
        def _():
            for c in stores(p - 2, slot):
                c.wait()

        rows = lambda b: pl.ds(pl.multiple_of(b * NBACK, NBACK), NBACK)

        def write(j, rho, b, value):
            out_bufs[j][slot, rho, rows(b), :] = value

        _attn_bwd_loop(lambda j, rho, b: in_bufs[j][slot, rho, rows(b), :], write, n // NBACK)

        for c in stores(p, slot):
            c.start()

        @pl.when(p == NPAIR - 1)
        def _():
            for c in stores(p - 1, 1 - slot) + stores(p, slot):
                c.wait()

    by_residue = lambda a: a.reshape(NPAIR, n, rate, PAIR)
    res = pl.pallas_call(
        body, name=name, grid=(NPAIR,), in_specs=[HBM_SPEC] * 6, out_specs=[HBM_SPEC] * 3,
        out_shape=[_sds((NPAIR, n, rate, PAIR), F32)] * 3,
        scratch_shapes=[pltpu.VMEM((2, rate, n, PAIR), F32)] * 9
        + [pltpu.SemaphoreType.DMA((12,)), pltpu.SemaphoreType.DMA((6,))],
        compiler_params=_cparams(1, 56))(*[by_residue(a) for a in (qn, kn, v, dbp, e, lse)])
    return [r.reshape(NPAIR, T, PAIR) for r in res]


def _ab_in_bwd(pab, dcat, sgu_g, sgu_b, sgu_w, sgu_bias3, qg, kg, tabs, dqkv, rinvs, tm=256):
    def body(p_ref, dcat_ref, g_ref, b_ref, w_ref, bias_ref, qg_ref, kg_ref, c_ref, s1_ref, s2_ref, *rest):
        dq_refs, rinv_refs = rest[0:9], rest[9:15]
        o_ref, dwm_ref, dbias_ref, dsg_ref, dsb_ref, dgain_ref = rest[15:]
        i = pl.program_id(0)

        @pl.when(i == 0)
        def _():
            dwm_ref[...] = jnp.zeros_like(dwm_ref)
            dbias_ref[...] = jnp.zeros_like(dbias_ref)
            dsg_ref[...] = jnp.zeros_like(dsg_ref)
            dsb_ref[...] = jnp.zeros_like(dsb_ref)
            dgain_ref[...] = jnp.zeros_like(dgain_ref)

        zu = p_ref[:, 0:512].astype(F32)
        zv = p_ref[:, 512:1024].astype(F32)
        u = _gelu(zu)
        v = _gelu(zv)
        mu = jnp.mean(v, axis=-1, keepdims=True)
        vc = v - mu
        rstd = lax.rsqrt(jnp.mean(vc * vc, axis=-1, keepdims=True) + EPS)
        xhat = vc * rstd
        vn = (xhat * g_ref[...] + b_ref[...]).astype(BF16)
        da = dcat_ref[...].astype(F32)
        tri = _tril_mask()
        du_parts = [[None] * 4 for _ in range(tm // 128)]
        dvn_parts = [[None] * 4 for _ in range(tm // 128)]
        for gi in range(4):
            wg = jnp.where(tri, w_ref[gi], 0.0).astype(BF16)
            bg = bias_ref[gi]
            for c in range(tm // 128):
                rs, cs = slice(c * 128, (c + 1) * 128), slice(gi * 128, (gi + 1) * 128)
                vblk = vn[rs, cs]
                mixed = jnp.dot(wg, vblk, preferred_element_type=F32) + bg
                dab = da[rs, cs]
                du_parts[c][gi] = dab * mixed
                dmixed = dab * u[rs, cs]
                dmb = dmixed.astype(BF16)
                dvn_parts[c][gi] = lax.dot_general(wg, dmb, (((0,), (0,)), ((), ())), preferred_element_type=F32)
                dwm = lax.dot_general(dmb, vblk, (((1,), (1,)), ((), ())), preferred_element_type=F32)
                dwm_ref[gi] += jnp.where(tri, dwm, 0.0)
                dbias_ref[gi] += dmixed
        du = jnp.concatenate([jnp.concatenate(r, axis=1) for r in du_parts], axis=0)
        dvn = jnp.concatenate([jnp.concatenate(r, axis=1) for r in dvn_parts], axis=0)
        dsg_ref[...] += jnp.sum(dvn * xhat, axis=0, keepdims=True)
        dsb_ref[...] += jnp.sum(dvn, axis=0, keepdims=True)
        dxh = dvn * g_ref[...]
        dv = rstd * (dxh - jnp.mean(dxh, axis=-1, keepdims=True)
                     - xhat * jnp.mean(dxh * xhat, axis=-1, keepdims=True))
        o_ref[:, 0:512] = (du * _gelu_grad(zu)).astype(BF16)
        o_ref[:, 512:1024] = (dv * _gelu_grad(zv)).astype(BF16)

        blocks = _head_blocks()
        c, s1, s2 = c_ref[...], s1_ref[...], s2_ref[...]
        for g in range(3):
            dq_ref, dk_ref, dv_ref = dq_refs[3 * g:3 * g + 3]
            for p in range(NPAIR):
                for which, gains, src in ((0, qg_ref, dq_ref), (1, kg_ref, dk_ref)):
                    col = (2 + 3 * which + g) * 512 + p * PAIR
                    xr = p_ref[:, col:col + PAIR].astype(F32)
                    rinv = rinv_refs[2 * g + which][p].astype(F32)
                    xh = xr * rinv
                    dn = _rope_t(src[p], c, s1, s2)
                    row = 2 * g + which
                    dgain_ref[row:row + 1, :] += jnp.sum(dn * xh, axis=0, keepdims=True)
                    dxh2 = dn * gains[g:g + 1, :]
                    dx = rinv * (dxh2 - xh * _seg_mean_mxu(dxh2 * xh, blocks))
                    o_ref[:, col:col + PAIR] = dx.astype(BF16)
                col = (8 + g) * 512 + p * PAIR
                o_ref[:, col:col + PAIR] = dv_ref[p].astype(BF16)

    pm = pl.BlockSpec((NPAIR, tm, PAIR), lambda i: (0, i, 0))
    tab = pl.BlockSpec((tm, PAIR), lambda i: (i, 0))
    gain = pl.BlockSpec((3, PAIR), lambda i: (0, 0))
    vec = pl.BlockSpec((1, 512), lambda i: (0, 0))
    full = pl.BlockSpec((tm, AB_IN), lambda i: (i, 0))
    w4 = pl.BlockSpec((4, 128, 128), lambda i: (0, 0, 0))
    return pl.pallas_call(
        body, name="ab_in_bwd", grid=(T // tm,),
        in_specs=[full, pl.BlockSpec((tm, 512), lambda i: (i, 0)), vec, vec, w4,
                  pl.BlockSpec((4, 128, 1), lambda i: (0, 0, 0)), gain, gain, tab, tab, tab] + [pm] * 15,
        out_specs=[full, w4, w4, vec, vec, pl.BlockSpec((8, PAIR), lambda i: (0, 0))],
        out_shape=[_sds((T, AB_IN), BF16), _sds((4, 128, 128), F32), _sds((4, 128, 128), F32),
                   _sds((1, 512), F32), _sds((1, 512), F32), _sds((8, PAIR), F32)],
        compiler_params=_cparams(1))(pab, dcat, sgu_g, sgu_b, sgu_w, sgu_bias3, qg, kg, *tabs, *dqkv, *rinvs)


def _ln_stats(x):
    mu = jnp.mean(x, axis=-1, keepdims=True)
    xc = x - mu
    rstd = lax.rsqrt(jnp.mean(xc * xc, axis=-1, keepdims=True) + EPS)
    return xc * rstd, rstd


CONV_RC = 64


def _shifted_copies(src, dst, tm):
    dst[0] = src[...]
    for b in range(1, 8):
        dst[b, 0:tm + HALO - 8, :] = src[pl.ds(b, tm + HALO - 8), :]


def _offsets_by_phase(first):
    groups = {}
    for o in range(first, first + CONV_C_TAPS):
        groups.setdefault(o % 8, []).append(o)
    return sorted(groups.items())


def _window(shifted, b8, base, offsets, lanes):
    rows = 8 * (max(offsets) // 8) + CONV_RC
    return shifted[b8, pl.ds(base, rows), lanes].reshape(rows // 8, 8, 128)


def _cd_fwd(pcd, cw, cb, lg, lb, dw, tm=512):
    per = tm // HALO

    def body(p_ref, h_ref, cw_ref, cb_ref, lg_ref, lb_ref, dw_ref, cat_ref, c0_ref, c1_ref, dd_ref, y_ref,
             buf, buf2, sb):
        i = pl.program_id(0)
        live = jnp.where(i > 0, 1.0, 0.0)
        a = p_ref[:, 0:512].astype(F32)
        gt = p_ref[:, 512:1024].astype(F32)
        gb = p_ref[:, 1024:1536].astype(F32)
        gc = p_ref[:, 1536:2048].astype(F32)
        hv = p_ref[:, 2048:2560].astype(F32)
        c0 = a * _sigmoid(gt)
        dd = gc * hv
        buf[0:HALO, :] = h_ref[:, 0:512].astype(F32) * _sigmoid(h_ref[:, 512:1024].astype(F32)) * live
        buf[HALO:, :] = c0
        buf2[0:HALO, :] = h_ref[:, 1536:2048].astype(F32) * h_ref[:, 2048:2560].astype(F32) * live
        buf2[HALO:, :] = dd
        c0_ref[...] = c0.astype(BF16)
        dd_ref[...] = dd.astype(BF16)
        _shifted_copies(buf, sb, tm)

        def conv_rows(r, carry):
            base = pl.multiple_of(r * CONV_RC, CONV_RC)
            for c in range(4):
                lanes = slice(c * 128, (c + 1) * 128)
                acc = jnp.broadcast_to(cb_ref[:, lanes], (CONV_RC // 8, 8, 128))
                for b8, offsets in _offsets_by_phase(HALO - (CONV_C_TAPS - 1)):
                    win = _window(sb, b8, base, offsets, lanes)
                    for o in offsets:
                        j = o - (HALO - (CONV_C_TAPS - 1))
                        acc = acc + cw_ref[8 * j:8 * j + 8, lanes] * win[o // 8:o // 8 + CONV_RC // 8]
                c1_ref[pl.ds(base, CONV_RC), lanes] = acc.reshape(CONV_RC, 128)
            return carry

        lax.fori_loop(0, tm // CONV_RC, conv_rows, 0)
        xhat, _ = _ln_stats(c1_ref[...])
        c2 = xhat * lg_ref[...] + lb_ref[...]
        y = jnp.zeros((tm, 512), F32)
        for j in range(CONV_D_TAPS):
            y = y + dw_ref[j:j + 1, :] * buf2[pl.ds(HALO - (CONV_D_TAPS - 1) + j, tm), :]
        cat_ref[:, 0:512] = (c2 * _sigmoid(c2)).astype(BF16)
        cat_ref[:, 512:1024] = (gb * y).astype(BF16)
        y_ref[...] = y.astype(BF16)

    half = pl.BlockSpec((tm, 512), lambda i: (i, 0))
    vec = pl.BlockSpec((1, 512), lambda i: (0, 0))
    return pl.pallas_call(
        body, name="cd_fwd", grid=(T // tm,),
        in_specs=[pl.BlockSpec((tm, CD_IN), lambda i: (i, 0)),
                  pl.BlockSpec((HALO, CD_IN), lambda i: (jnp.maximum(i * per - 1, 0), 0)),
                  pl.BlockSpec((8 * 32, 512), lambda i: (0, 0)), vec, vec, vec, pl.BlockSpec((8, 512), lambda i: (0, 0))],
        out_specs=[pl.BlockSpec((tm, D), lambda i: (i, 0)), half, half, half, half],
        out_shape=[_sds((T, D), BF16), _sds((T, 512), BF16), _sds((T, 512), F32), _sds((T, 512), BF16),
                   _sds((T, 512), BF16)],
        scratch_shapes=[pltpu.VMEM((HALO + tm, 512), F32), pltpu.VMEM((HALO + tm, 512), F32),
                        pltpu.VMEM((8, HALO + tm, 512), F32)],
        compiler_params=_cparams(1))(pcd, pcd, cw, cb, lg, lb, dw)


def _d_cat_cd(dxb, w_cd_out, c1, pcd, y, lg, lb, dep, tm=512):
    def epilogue(d, rows, side, outs):
        c1_ref, gb_ref, y_ref, lg_ref, lb_ref = side
        dc1_ref, dy3_ref, dgb_ref, dlg_ref, dlb_ref, dcb_ref = outs
        dc, ddo = d[:, 0:512], d[:, 512:1024]
        xhat, rstd = _ln_stats(c1_ref[rows, :])
        c2 = xhat * lg_ref[...] + lb_ref[...]
        sg = _sigmoid(c2)
        dc2 = dc * sg * (1.0 + c2 * (1.0 - sg))
        dlg_ref[...] += jnp.sum(dc2 * xhat, axis=0, keepdims=True)
        dlb_ref[...] += jnp.sum(dc2, axis=0, keepdims=True)
        dxh = dc2 * lg_ref[...]
        dc1 = rstd * (dxh - jnp.mean(dxh, axis=-1, keepdims=True)
                      - xhat * jnp.mean(dxh * xhat, axis=-1, keepdims=True))
        dcb_ref[...] += jnp.sum(dc1, axis=0, keepdims=True)
        dc1_ref[rows, :] = dc1
        dgb_ref[rows, :] = (ddo * y_ref[rows, :].astype(F32)).astype(BF16)
        dy3_ref[rows, :] = ddo * gb_ref[rows, :].astype(F32)

    half = pl.BlockSpec((tm, 512), lambda i: (i, 0))
    vec = pl.BlockSpec((1, 512), lambda i: (0, 0))
    return _mm_nt_rows(
        dxb, w_cd_out, "mm_d_cat_cd", epilogue, [c1, pcd, y, lg, lb],
        [half, pl.BlockSpec((tm, 512), lambda i: (i, 2)), half, vec, vec], [half, half, half, vec, vec, vec],
        [_sds((T, 512), F32), _sds((T, 512), F32), _sds((T, 512), BF16),
         _sds((1, 512), F32), _sds((1, 512), F32), _sds((1, 512), F32)], sums=(3, 4, 5), dep=dep, tm=tm)


def _cd_bwd_conv(pcd, dc1, dy3, c0, dd, dgb, cw8, dw, tm=256):
    per = tm // HALO
    nblk = T // tm
    last32 = T // HALO - 1

    def body(p_ref, dc1_ref, dc1n_ref, dy3_ref, dy3n_ref, c0_ref, dd_ref, dgb_ref, cw_ref, dw_ref,
             o_ref, dcw_ref, ddw_ref, dbuf, d3buf, sd, dc0_buf):
        i = pl.program_id(0)
        has_next = jnp.where(i < nblk - 1, 1.0, 0.0)

        @pl.when(i == 0)
        def _():
            dcw_ref[...] = jnp.zeros_like(dcw_ref)
            ddw_ref[...] = jnp.zeros_like(ddw_ref)

        dbuf[0:tm, :] = dc1_ref[...]
        dbuf[tm:, :] = dc1n_ref[...] * has_next
        d3buf[0:tm, :] = dy3_ref[...]
        d3buf[tm:, :] = dy3n_ref[...] * has_next
        _shifted_copies(dbuf, sd, tm)
        n_tiles = tm // CONV_RC

        phases = _offsets_by_phase(0)

        def dc0_rows(r, carry):
            base = pl.multiple_of(r * CONV_RC, CONV_RC)
            for c in range(4):
                lanes = slice(c * 128, (c + 1) * 128)
                acc = jnp.zeros((CONV_RC // 8, 8, 128), F32)
                for b8, offsets in phases:
                    win = _window(sd, b8, base, offsets, lanes)
                    for o in offsets:
                        j = CONV_C_TAPS - 1 - o
                        acc = acc + cw_ref[8 * j:8 * j + 8, lanes] * win[o // 8:o // 8 + CONV_RC // 8]
                dc0_buf[pl.ds(base, CONV_RC), lanes] = acc.reshape(CONV_RC, 128)
            return carry

        lax.fori_loop(0, n_tiles, dc0_rows, 0)

        for c in range(4):
            lanes = slice(c * 128, (c + 1) * 128)
            for b8, offsets in phases:
                def dw_rows(r, accs, lanes=lanes, b8=b8, offsets=offsets):
                    base = pl.multiple_of(r * CONV_RC, CONV_RC)
                    xin = c0_ref[pl.ds(base, CONV_RC), lanes].astype(F32).reshape(CONV_RC // 8, 8, 128)
                    win = _window(sd, b8, base, offsets, lanes)
                    return tuple(acc + jnp.sum(xin * win[o // 8:o // 8 + CONV_RC // 8], axis=0)
                                 for acc, o in zip(accs, offsets))

                accs = lax.fori_loop(0, n_tiles, dw_rows, tuple(jnp.zeros((8, 128), F32) for _ in offsets))
                for acc, o in zip(accs, offsets):
                    j = CONV_C_TAPS - 1 - o
                    dcw_ref[j:j + 1, lanes] += jnp.sum(acc, axis=0, keepdims=True)

        dc0 = dc0_buf[...]
        ddin = dd_ref[...].astype(F32)
        ddd = jnp.zeros((tm, 512), F32)
        for j in range(CONV_D_TAPS):
            dy_shift = d3buf[pl.ds(CONV_D_TAPS - 1 - j, tm), :]
            ddd = ddd + dw_ref[j:j + 1, :] * dy_shift
            ddw_ref[j:j + 1, :] += jnp.sum(ddin * dy_shift, axis=0, keepdims=True)

        a = p_ref[:, 0:512].astype(F32)
        gt = p_ref[:, 512:1024].astype(F32)
        gc = p_ref[:, 1536:2048].astype(F32)
        hv = p_ref[:, 2048:2560].astype(F32)
        sg = _sigmoid(gt)
        o_ref[:, 0:512] = (dc0 * sg).astype(BF16)
        o_ref[:, 512:1024] = (dc0 * a * sg * (1.0 - sg)).astype(BF16)
        o_ref[:, 1024:1536] = dgb_ref[...]
        o_ref[:, 1536:2048] = (ddd * hv).astype(BF16)
        o_ref[:, 2048:2560] = (ddd * gc).astype(BF16)

    half = pl.BlockSpec((tm, 512), lambda i: (i, 0))
    nxt = pl.BlockSpec((HALO, 512), lambda i: (jnp.minimum((i + 1) * per, last32), 0))
    full = pl.BlockSpec((tm, CD_IN), lambda i: (i, 0))
    return pl.pallas_call(
        body, name="cd_bwd_conv", grid=(nblk,),
        in_specs=[full, half, nxt, half, nxt, half, half, half,
                  pl.BlockSpec((8 * 32, 512), lambda i: (0, 0)), pl.BlockSpec((8, 512), lambda i: (0, 0))],
        out_specs=[full, pl.BlockSpec((32, 512), lambda i: (0, 0)), pl.BlockSpec((8, 512), lambda i: (0, 0))],
        out_shape=[_sds((T, CD_IN), BF16), _sds((32, 512), F32), _sds((8, 512), F32)],
        scratch_shapes=[pltpu.VMEM((tm + HALO, 512), F32), pltpu.VMEM((tm + HALO, 512), F32),
                        pltpu.VMEM((8, tm + HALO, 512), F32), pltpu.VMEM((tm, 512), F32)],
        compiler_params=_cparams(1))(pcd, dc1, dc1, dy3, dy3, c0, dd, dgb, cw8, dw)


def _local_step(x, tgt, W, fetch=None, on_grad=None):
    W = dict(W)
    if fetch is None:
        fetch = lambda stage, after: {}
    if on_grad is None:
        on_grad = lambda key, arr: None
    tabs = _rope_tables()
    qg = jnp.tile(W["q_norm_g"], (1, 2))
    kg = jnp.tile(W["k_norm_g"], (1, 2))
    bias3 = W["sgu_bias"].reshape(4, 128, 1)
    G = {}

    h0 = _rms_fwd(x, W["ab_norm_g"], "rms_fwd_ab", dep=W.get("dep_first"))
    W.update(fetch("ab_in", h0))
    pab = _mm_nt(h0, W["wt_ab_in"], "mm_ab_in", dep=W.get("dep0"))
    cat_ab = _mix_a_fwd(pab, W["sgu_norm_g"], W["sgu_norm_b"], W["sgu_w"], bias3)
    qkv, rinvs = _prep_fwd(pab, qg, kg, tabs)
    outs, lses = [], []
    for g, rate in enumerate(DIL_RATES):
        o, l = _attn_fwd(qkv[3 * g], qkv[3 * g + 1], qkv[3 * g + 2], rate, f"attn_fwd_{g}", dep=W.get(f"dep_attn{g}"))
        outs.append(o)
        lses.append(l)
        W.update(fetch(f"attn{g}", o))
    cat_ab, lse = _merge_fwd(cat_ab, outs, lses)
    W.update(fetch("ab_out", lse))
    x1, h1 = _mm_nn(cat_ab, W["w_ab_out"], "mm_ab_out", mode="rms", resid=x, gain=W["ffn_norm_g"][0:1])
    pf0, act0 = _ffn_in(h1, W["wt_ffn_in0"], "ffn_in0")
    W.update(fetch("ffn_down0", act0))
    x2, h2 = _mm_nn(act0, W["w_ffn_down0"], "mm_ffn_down0", mode="rms", resid=x1, gain=W["cd_norm_g"],
                    dep=W.get("dep_down0"))
    W.update(fetch("cd_in", h2))
    pcd = _mm_nt(h2, W["wt_cd_in"], "mm_cd_in")
    cw8 = jnp.repeat(W["conv_c_w32"], 8, axis=0)
    cat_cd, c0, c1, dd, yv = _cd_fwd(pcd, cw8, W["conv_c_b"], W["c_ln_g"], W["c_ln_b"], W["conv_d_w8"])
    x3, h3 = _mm_nn(cat_cd, W["w_cd_out"], "mm_cd_out", mode="rms", resid=x2, gain=W["ffn_norm_g"][1:2])
    pf1, act1 = _ffn_in(h3, W["wt_ffn_in1"], "ffn_in1")
    dy, dyb, loss_cols = _mm_nn(act1, W["w_ffn_down1"], "mm_ffn_down1", mode="loss", resid=x3, tgt=tgt)

    def ffn_bwd(xin, h, pf, act, dres, dresb, layer):
        G[f"w_ffn_down{layer}"] = _mm_tn(act, dresb, f"mm_g_ffn_down{layer}")
        dep = on_grad(f"w_ffn_down{layer}", G[f"w_ffn_down{layer}"])
        dpf = _ffn_dact(dresb, W[f"w_ffn_down{layer}"], pf, f"ffn_dact{layer}", dep=dep)
        G[f"wt_ffn_in{layer}"] = _mm_tn(dpf, h, f"mm_g_ffn_in{layer}")
        dep = on_grad(f"wt_ffn_in{layer}", G[f"wt_ffn_in{layer}"])
        dx, dxb, G[f"ffn_norm_g{layer}"] = _mm_dh_rms_bwd(
            dpf, W[f"wt_ffn_in{layer}"], xin, W["ffn_norm_g"][layer:layer + 1], dres, f"mm_d_h_ffn{layer}", dep=dep)
        return dx, dxb

    dx3, dx3b = ffn_bwd(x3, h3, pf1, act1, dy, dyb, 1)

    G["w_cd_out"] = _mm_tn(cat_cd, dx3b, "mm_g_cd_out")
    dep = on_grad("w_cd_out", G["w_cd_out"])
    dc1, dy3, dgb, G["c_ln_g"], G["c_ln_b"], G["conv_c_b"] = _d_cat_cd(
        dx3b, W["w_cd_out"], c1, pcd, yv, W["c_ln_g"], W["c_ln_b"], dep)
    dpcd, G["conv_c_w32"], G["conv_d_w8"] = _cd_bwd_conv(pcd, dc1, dy3, c0, dd, dgb, cw8, W["conv_d_w8"])
    G["wt_cd_in"] = _mm_tn(dpcd, h2, "mm_g_cd_in")
    dep = on_grad("wt_cd_in", G["wt_cd_in"])
    dx2, dx2b, G["cd_norm_g"] = _mm_dh_rms_bwd(dpcd, W["wt_cd_in"], x2, W["cd_norm_g"], dx3, "mm_d_h_cd", dep=dep)

    dx1, dx1b = ffn_bwd(x1, h1, pf0, act0, dx2, dx2b, 0)

    G["w_ab_out"] = _mm_tn(cat_ab, dx1b, "mm_g_ab_out")
    dep = on_grad("w_ab_out", G["w_ab_out"])
    dcat_a, dbp, e = _d_cat_ab(dx1b, W["w_ab_out"], cat_ab, dep)
    dqkv = []
    for g, rate in enumerate(DIL_RATES):
        dqkv += _attn_bwd(qkv[3 * g], qkv[3 * g + 1], qkv[3 * g + 2], dbp, e, lse, rate, f"attn_bwd_{g}")
    dpab, G["sgu_w"], dbias_part, G["sgu_norm_g"], G["sgu_norm_b"], dgain = _ab_in_bwd(
        pab, dcat_a, W["sgu_norm_g"], W["sgu_norm_b"], W["sgu_w"], bias3, qg, kg, tabs, dqkv, rinvs)
    G["sgu_bias"] = jnp.sum(dbias_part, axis=-1)
    dgain = dgain[0:6, 0:HEAD] + dgain[0:6, HEAD:PAIR]
    G["q_norm_g"] = dgain[0::2]
    G["k_norm_g"] = dgain[1::2]
    G["loss_cols"] = loss_cols
    dep = on_grad("small", G)
    G["wt_ab_in"] = _mm_tn(dpab, h0, "mm_g_ab_in", dep=dep)
    dep = on_grad("wt_ab_in", G["wt_ab_in"])
    grad_x, G["ab_norm_g"] = _mm_dh_rms_bwd(dpab, W["wt_ab_in"], x, W["ab_norm_g"], dx1, "mm_d_h_ab", dep=dep,
                                            bf16_copy=False)
    return loss_cols, grad_x, G


def _my_place():
    return lax.axis_index("x"), lax.axis_index("y"), lax.axis_index("c")


def _dev_index(px, py, pc):
    return 4 * px + 2 * py + pc


def _flip(place, k):
    x, y, c = place
    return (1 - x if k & 4 else x, 1 - y if k & 2 else y, 1 - c if k & 1 else c)


def _landing(shape, dtype, own):
    buf = lax.empty(shape, dtype)
    for lead, part in own:
        buf = lax.dynamic_update_slice(buf, part.reshape((1,) * len(lead) + part.shape),
                                       tuple(lead) + (0,) * part.ndim)
    return buf


HBM_ONLY = pl.BlockSpec(memory_space=pltpu.HBM)
SEM_SPEC = pl.BlockSpec(memory_space=pltpu.SEMAPHORE)
IN_FLIGHT = pltpu.CompilerParams(has_side_effects=pltpu.SideEffectType.DATAFLOW_SIDE_EFFECTING)


def _in_hbm(a):
    return pltpu.with_memory_space_constraint(a, pltpu.HBM)


def _exchange_start(name, srcs, lands, items, dep=None):
    ns, nl, ni = len(srcs), len(lands), len(items)

    def body(*refs):
        S, L = refs[0:ns], refs[ns:ns + nl]
        first_out = ns + nl + (0 if dep is None else 1)
        send_sems, recv_sems, token = refs[first_out], refs[first_out + 1], refs[-1]
        me = _my_place()
        mi = _dev_index(*me)
        for i, (src, dst) in enumerate(items):
            for k in range(1, NDEV):
                peer = _flip(me, k)
                pltpu.make_async_remote_copy(
                    src_ref=src(S, _dev_index(*peer)), dst_ref=dst(L, mi), send_sem=send_sems.at[7 * i + k - 1],
                    recv_sem=recv_sems.at[7 * i + k - 1], device_id=peer, device_id_type=MESH).start()
        token[...] = jnp.zeros_like(token)

    thru = [pltpu.HBM(a.shape, a.dtype) for a in list(srcs) + list(lands)]
    args = [_in_hbm(a) for a in srcs] + [_in_hbm(a) for a in lands]
    in_specs = [HBM_ONLY] * (ns + nl)
    if dep is not None:
        args.append(dep)
        in_specs.append(HBM_SPEC)
    outs = pl.pallas_call(
        body, name=name, in_specs=in_specs,
        out_shape=(pltpu.SemaphoreType.DMA((7 * ni,)), pltpu.SemaphoreType.DMA((7 * ni,)), *thru, _sds((8, 128), F32)),
        out_specs=(SEM_SPEC, SEM_SPEC, *[HBM_ONLY] * (ns + nl), pl.BlockSpec(memory_space=pltpu.VMEM)),
        input_output_aliases={j: 2 + j for j in range(ns + nl)}, compiler_params=IN_FLIGHT)(*args)
    return dict(send=outs[0], recv=outs[1], srcs=list(outs[2:2 + ns]), lands=list(outs[2 + ns:2 + ns + nl]),
                token=outs[-1], items=items)


def _exchange_wait(name, states, after):
    after = list(after) if isinstance(after, (list, tuple)) else [after]
    counts = [(len(st["srcs"]), len(st["lands"]), len(st["items"])) for st in states]
    n_arrays = sum(c[0] + c[1] for c in counts)

    def body(*refs):
        me = _my_place()
        mi = _dev_index(*me)
        pos = 0
        sem_pos = n_arrays
        for st, (ns, nl, ni) in zip(states, counts):
            S, L = refs[pos:pos + ns], refs[pos + ns:pos + ns + nl]
            send_sems, recv_sems = refs[sem_pos], refs[sem_pos + 1]
            pos += ns + nl
            sem_pos += 2
            for i, (src, dst) in enumerate(st["items"]):
                for k in range(1, NDEV):
                    cp = pltpu.make_async_remote_copy(
                        src_ref=src(S, mi), dst_ref=dst(L, mi), send_sem=send_sems.at[7 * i + k - 1],
                        recv_sem=recv_sems.at[7 * i + k - 1], device_id=me, device_id_type=MESH)
                    cp.wait_send()
                    cp.wait_recv()

    arrays, sems = [], []
    for st in states:
        arrays += st["srcs"] + st["lands"]
        sems += [st["send"], st["recv"]]
    outs = pl.pallas_call(
        body, name=name, in_specs=[HBM_ONLY] * n_arrays + [SEM_SPEC] * len(sems) + [HBM_SPEC] * len(after),
        out_shape=tuple(pltpu.HBM(a.shape, a.dtype) for a in arrays), out_specs=tuple([HBM_ONLY] * n_arrays),
        input_output_aliases={j: j for j in range(n_arrays)}, compiler_params=IN_FLIGHT)(*arrays, *sems, *after)
    lands, pos = [], 0
    for ns, nl, _ in counts:
        lands.append(list(outs[pos + ns:pos + ns + nl]))
        pos += ns + nl
    return lands


def _place_and_neighbours():
    x, y, c = _my_place()
    return (x, y, c), (x, y, 1 - c), [(1 - x, y), (x, 1 - y), (1 - x, 1 - y)]


def _gather_start(name, srcs, lands, items, dep=None):
    ns, nl, ni = len(srcs), len(lands), len(items)

    def body(*refs):
        S, L = refs[0:ns], refs[ns:ns + nl]
        first_out = ns + nl + (0 if dep is None else 1)
        send_sems, recv_sems, token = refs[first_out], refs[first_out + 1], refs[-1]
        me, sib, chips = _place_and_neighbours()
        mi = _dev_index(*me)
        for i, (src, dst) in enumerate(items):
            for k, to in enumerate([sib] + [(*chip, me[2]) for chip in chips]):
                pltpu.make_async_remote_copy(
                    src_ref=src(S), dst_ref=dst(L, mi), send_sem=send_sems.at[4 * i + k],
                    recv_sem=recv_sems.at[4 * i + k], device_id=to, device_id_type=MESH).start()
        token[...] = jnp.zeros_like(token)

    thru = [pltpu.HBM(a.shape, a.dtype) for a in list(srcs) + list(lands)]
    args = [_in_hbm(a) for a in srcs] + [_in_hbm(a) for a in lands]
    in_specs = [HBM_ONLY] * (ns + nl)
    if dep is not None:
        args.append(dep)
        in_specs.append(HBM_SPEC)
    outs = pl.pallas_call(
        body, name=name, in_specs=in_specs,
        out_shape=(pltpu.SemaphoreType.DMA((4 * ni,)), pltpu.SemaphoreType.DMA((4 * ni,)), *thru, _sds((8, 128), F32)),
        out_specs=(SEM_SPEC, SEM_SPEC, *[HBM_ONLY] * (ns + nl), pl.BlockSpec(memory_space=pltpu.VMEM)),
        input_output_aliases={j: 2 + j for j in range(ns + nl)}, compiler_params=IN_FLIGHT)(*args)
    return dict(send=outs[0], recv=outs[1], srcs=list(outs[2:2 + ns]), lands=list(outs[2 + ns:2 + ns + nl]),
                token=outs[-1], items=items)


def _gather_forward(name, st, after):
    nl, ni = len(st["lands"]), len(st["items"])

    def body(*refs):
        L, recv_sems = refs[0:nl], refs[nl]
        fwd_send, fwd_recv, token = refs[-3:]
        me, sib, chips = _place_and_neighbours()
        for i, (_, dst) in enumerate(st["items"]):
            for j, chip in enumerate(chips):
                blk = dst(L, _dev_index(*chip, me[2]))
                pltpu.make_async_remote_copy(
                    src_ref=blk, dst_ref=blk, send_sem=fwd_send.at[3 * i + j], recv_sem=recv_sems.at[4 * i + 1 + j],
                    device_id=me, device_id_type=MESH).wait_recv()
                pltpu.make_async_remote_copy(
                    src_ref=blk, dst_ref=blk, send_sem=fwd_send.at[3 * i + j], recv_sem=fwd_recv.at[3 * i + j],
                    device_id=sib, device_id_type=MESH).start()
        token[...] = jnp.zeros_like(token)

    after = list(after) if isinstance(after, (list, tuple)) else [after]
    outs = pl.pallas_call(
        body, name=name, in_specs=[HBM_ONLY] * nl + [SEM_SPEC] + [HBM_SPEC] * len(after),
        out_shape=(*[pltpu.HBM(a.shape, a.dtype) for a in st["lands"]], pltpu.SemaphoreType.DMA((3 * ni,)),
                   pltpu.SemaphoreType.DMA((3 * ni,)), _sds((8, 128), F32)),
        out_specs=(*[HBM_ONLY] * nl, SEM_SPEC, SEM_SPEC, pl.BlockSpec(memory_space=pltpu.VMEM)),
        input_output_aliases={j: j for j in range(nl)}, compiler_params=IN_FLIGHT)(*st["lands"], st["recv"], *after)
    return dict(st, lands=list(outs[0:nl]), fwd_send=outs[nl], fwd_recv=outs[nl + 1], token=outs[-1])


def _gather_wait(name, st, after):
    ns, nl, ni = len(st["srcs"]), len(st["lands"]), len(st["items"])

    def body(*refs):
        S, L = refs[0:ns], refs[ns:ns + nl]
        send_sems, recv_sems, fwd_send, fwd_recv = refs[ns + nl:ns + nl + 4]
        me, sib, chips = _place_and_neighbours()
        mi = _dev_index(*me)
        for i, (src, dst) in enumerate(st["items"]):
            mine = dst(L, mi)
            for k in range(4):
                pltpu.make_async_remote_copy(
                    src_ref=src(S), dst_ref=mine, send_sem=send_sems.at[4 * i + k], recv_sem=recv_sems.at[4 * i + k],
                    device_id=me, device_id_type=MESH).wait_send()
            pltpu.make_async_remote_copy(
                src_ref=src(S), dst_ref=mine, send_sem=send_sems.at[4 * i], recv_sem=recv_sems.at[4 * i],
                device_id=me, device_id_type=MESH).wait_recv()
            for j in range(3):
                cp = pltpu.make_async_remote_copy(
                    src_ref=mine, dst_ref=mine, send_sem=fwd_send.at[3 * i + j], recv_sem=fwd_recv.at[3 * i + j],
                    device_id=me, device_id_type=MESH)
                cp.wait_send()
                cp.wait_recv()

    arrays = st["srcs"] + st["lands"]
    outs = pl.pallas_call(
        body, name=name, in_specs=[HBM_ONLY] * (ns + nl) + [SEM_SPEC] * 4 + [HBM_SPEC],
        out_shape=tuple(pltpu.HBM(a.shape, a.dtype) for a in arrays), out_specs=tuple([HBM_ONLY] * (ns + nl)),
        input_output_aliases={j: j for j in range(ns + nl)},
        compiler_params=IN_FLIGHT)(*arrays, st["send"], st["recv"], st["fwd_send"], st["fwd_recv"], after)
    return list(outs[ns:ns + nl])


def _sum_slots(land):
    def body(l_ref, o_ref):
        acc = l_ref[0]
        for d in range(1, NDEV):
            acc = acc + l_ref[d]
        o_ref[...] = acc

    vm = pl.BlockSpec(memory_space=pltpu.VMEM)
    return pl.pallas_call(body, name="sum_small", out_shape=_sds(land.shape[1:], F32), in_specs=[vm], out_specs=vm)(land)


def _adam_math(w, g, m, v):
    m2 = ADAM_B1 * m + (1.0 - ADAM_B1) * g
    v2 = ADAM_B2 * v + (1.0 - ADAM_B2) * (g * g)
    delta = -ADAM_LR * ((m2 * ADAM_C1) / (jnp.sqrt(v2 * ADAM_C2) + ADAM_EPS) + ADAM_WD * w)
    return delta, m2, v2


def _adam_layer(land, sel, w, m, v, layer, name, prev=None, tc=512):
    R = land.shape[2]

    def body(l_ref, w_ref, m_ref, v_ref, *rest):
        g_out, d_out, m_out, v_out = rest[-4:]
        g = l_ref[0].astype(F32)
        for d in range(1, NDEV):
            g = g + l_ref[d].astype(F32)
        delta, m2, v2 = _adam_math(w_ref[...], g, m_ref[...], v_ref[...])
        g_out[...] = g
        d_out[...] = delta
        m_out[...] = m2
        v_out[...] = v2

    wspec = pl.BlockSpec((None, R, tc), lambda i: (layer, 0, i))
    in_specs = [pl.BlockSpec((None, NDEV, R, tc), lambda i: (sel, 0, 0, i)), wspec, wspec, wspec]
    args = [land, w, m, v]
    aliases = {}
    if prev is not None:
        in_specs += [HBM_SPEC] * 4
        args += list(prev)
        aliases = {4 + j: j for j in range(4)}
    return pl.pallas_call(
        body, name=name, grid=(D // tc,), in_specs=in_specs, out_specs=[wspec] * 4,
        out_shape=[_sds(w.shape, F32)] * 4, input_output_aliases=aliases, compiler_params=_cparams(1))(*args)


def _adam_stacked(lands, sel, w, m, v, name):
    res = None
    for layer, land in enumerate(lands):
        res = _adam_layer(land, sel, w, m, v, layer, f"{name}{layer}", prev=res)
    return res


def _adam_small(ws, gs, ms, vs):
    n = len(ws)

    def body(*refs):
        w_r, g_r, m_r, v_r = refs[0:n], refs[n:2 * n], refs[2 * n:3 * n], refs[3 * n:4 * n]
        d_o, m_o, v_o = refs[4 * n:5 * n], refs[5 * n:6 * n], refs[6 * n:7 * n]
        for i in range(n):
            delta, m2, v2 = _adam_math(w_r[i][...], g_r[i][...], m_r[i][...], v_r[i][...])
            d_o[i][...] = delta
            m_o[i][...] = m2
            v_o[i][...] = v2

    vm = pl.BlockSpec(memory_space=pltpu.VMEM)
    shapes = [_sds(w.shape, F32) for w in ws]
    outs = pl.pallas_call(body, name="adam_small", in_specs=[vm] * (4 * n), out_specs=[vm] * (3 * n),
                          out_shape=shapes * 3)(*ws, *gs, *ms, *vs)
    return outs[0:n], outs[n:2 * n], outs[2 * n:3 * n]


def _adam_of_slots(land, w, m, v, name):
    def body(l_ref, w_ref, m_ref, v_ref, g_o, d_o, m_o, v_o):
        g = l_ref[0]
        for d in range(1, NDEV):
            g = g + l_ref[d]
        g_o[...] = g
        d_o[...], m_o[...], v_o[...] = _adam_math(w_ref[...], g, m_ref[...], v_ref[...])

    vm = pl.BlockSpec(memory_space=pltpu.VMEM)
    return pl.pallas_call(body, name=name, in_specs=[vm] * 4, out_specs=[vm] * 4,
                          out_shape=[_sds(w.shape, F32)] * 4)(land, w, m, v)


WEIGHT_NAMES = ("ab_norm_g", "ab_w_in", "sgu_norm_g", "sgu_norm_b", "sgu_w", "sgu_bias", "q_norm_g", "k_norm_g",
                "ab_w_out", "cd_norm_g", "cd_w_in", "conv_c_w", "conv_c_b", "c_ln_g", "c_ln_b", "conv_d_w",
                "cd_w_out", "ffn_norm_g", "ffn_w_gate", "ffn_w_up", "ffn_w_down")
SMALL_SHAPES = (("sgu_norm_g", (1, 512)), ("sgu_norm_b", (1, 512)), ("sgu_w", (512, 128)),
                ("sgu_bias", (4, 128)), ("q_norm_g", (3, 1, 64)), ("k_norm_g", (3, 1, 64)), ("cd_norm_g", (1, 128)),
                ("conv_c_w", (31, 1, 64)), ("conv_c_b", (1, 64)), ("c_ln_g", (1, 64)), ("c_ln_b", (1, 64)),
                ("conv_d_w", (3, 1, 64)), ("ffn_norm_g", (2, 1024)))
SHARD_C = 64


def _pack_rows(parts, rows):
    flat = jnp.concatenate([p.reshape(-1) for p in parts])
    return jnp.pad(flat, (0, rows * 128 - flat.shape[0])).reshape(rows, 128)


def kernel(x, ab_norm_g, ab_w_in, sgu_norm_g, sgu_norm_b, sgu_w, sgu_bias, q_norm_g, k_norm_g, ab_w_out, cd_norm_g, cd_w_in, conv_c_w, conv_c_b, c_ln_g, c_ln_b, conv_d_w, cd_w_out, ffn_norm_g, ffn_w_gate, ffn_w_up, ffn_w_down, loss_target, m_ab_norm_g, m_ab_w_in, m_sgu_norm_g, m_sgu_norm_b, m_sgu_w, m_sgu_bias, m_q_norm_g, m_k_norm_g, m_ab_w_out, m_cd_norm_g, m_cd_w_in, m_conv_c_w, m_conv_c_b, m_c_ln_g, m_c_ln_b, m_conv_d_w, m_cd_w_out, m_ffn_norm_g, m_ffn_w_gate, m_ffn_w_up, m_ffn_w_down, v_ab_norm_g, v_ab_w_in, v_sgu_norm_g, v_sgu_norm_b, v_sgu_w, v_sgu_bias, v_q_norm_g, v_k_norm_g, v_ab_w_out, v_cd_norm_g, v_cd_w_in, v_conv_c_w, v_conv_c_b, v_c_ln_g, v_c_ln_b, v_conv_d_w, v_cd_w_out, v_ffn_norm_g, v_ffn_w_gate, v_ffn_w_up, v_ffn_w_down):
    w = dict(zip(WEIGHT_NAMES, (ab_norm_g, ab_w_in, sgu_norm_g, sgu_norm_b, sgu_w, sgu_bias, q_norm_g, k_norm_g, ab_w_out, cd_norm_g, cd_w_in, conv_c_w, conv_c_b, c_ln_g, c_ln_b, conv_d_w, cd_w_out, ffn_norm_g, ffn_w_gate, ffn_w_up, ffn_w_down)))
    m = dict(zip(WEIGHT_NAMES, (m_ab_norm_g, m_ab_w_in, m_sgu_norm_g, m_sgu_norm_b, m_sgu_w, m_sgu_bias, m_q_norm_g, m_k_norm_g, m_ab_w_out, m_cd_norm_g, m_cd_w_in, m_conv_c_w, m_conv_c_b, m_c_ln_g, m_c_ln_b, m_conv_d_w, m_cd_w_out, m_ffn_norm_g, m_ffn_w_gate, m_ffn_w_up, m_ffn_w_down)))
    v = dict(zip(WEIGHT_NAMES, (v_ab_norm_g, v_ab_w_in, v_sgu_norm_g, v_sgu_norm_b, v_sgu_w, v_sgu_bias, v_q_norm_g, v_k_norm_g, v_ab_w_out, v_cd_norm_g, v_cd_w_in, v_conv_c_w, v_conv_c_b, v_c_ln_g, v_c_ln_b, v_conv_d_w, v_cd_w_out, v_ffn_norm_g, v_ffn_w_gate, v_ffn_w_up, v_ffn_w_down)))
    me = _dev_index(*_my_place())

    r_ff = DFF // NDEV
    one = lambda a: (lambda S, j: S[a])
    slot = lambda b: (lambda L, s: L[b].at[s])
    slot2 = lambda b, part: (lambda L, s: L[b].at[part, s])
    shard = lambda a: (lambda S: S[a])

    def later(a):
        return lax.optimization_barrier((a, gathers[0]["token"]))[0]

    def layer_shards(layer):
        return (later(w["ffn_w_gate"][layer]).T.astype(BF16), later(w["ffn_w_up"][layer]).T.astype(BF16),
                later(w["ffn_w_down"][layer]).astype(BF16))

    def gathered(own):
        return _landing((NDEV,) + own.shape, BF16, [((me,), own)])

    def gathered2(a, b):
        return _landing((2, NDEV) + a.shape, BF16, [((0, me), a), ((1, me), b)])

    ab_in_s = w["ab_w_in"][0].T.astype(BF16)
    gathers = {0: _gather_start("gather0_start", [ab_in_s], [gathered(ab_in_s)], [(shard(0), slot(0))])}

    def chan(flat, lo, taps):
        return flat[:, lo:lo + taps * SHARD_C].reshape(NDEV, taps, SHARD_C).transpose(1, 0, 2).reshape(taps, 512)

    def fetch(stage, after):
        if stage == "ab_in":
            ab_out_s = later(w["ab_w_out"][0]).astype(BF16)
            gate0, up0, down0 = layer_shards(0)
            small_s = _pack_rows([later(w[n]) for n in ("cd_norm_g", "conv_c_w", "conv_c_b", "c_ln_g", "c_ln_b",
                                                        "conv_d_w")], 24)
            lands1 = [gathered(ab_out_s), gathered2(gate0, up0), gathered(down0),
                      _landing((NDEV,) + small_s.shape, F32, [((me,), small_s)])]
            gathers[0] = _gather_forward("gather0_forward", gathers[0], [after] + lands1)
            l_ab_in, = _gather_wait("gather0_wait", gathers[0], gathers[0]["token"])
            gathers[1] = _gather_start(
                "gather1_start", [ab_out_s, gate0, up0, down0, small_s], lands1,
                [(shard(0), slot(0)), (shard(1), slot2(1, 0)), (shard(2), slot2(1, 1)), (shard(3), slot(2)),
                 (shard(4), slot(3))], dep=l_ab_in)
            return {"wt_ab_in": l_ab_in.reshape(AB_IN, D), "dep0": gathers[1]["token"]}
        if stage == "attn0":
            cd_in_s, cd_out_s = later(w["cd_w_in"][0]).T.astype(BF16), later(w["cd_w_out"][0]).astype(BF16)
            gate1, up1, down1 = layer_shards(1)
            gathers[2] = _gather_start(
                "gather2_start", [cd_in_s, cd_out_s, gate1, up1, down1],
                [gathered(cd_in_s), gathered(cd_out_s), gathered2(gate1, up1), gathered(down1)],
                [(shard(0), slot(0)), (shard(1), slot(1)), (shard(2), slot2(2, 0)), (shard(3), slot2(2, 1)),
                 (shard(4), slot(3))], dep=after)
            return {"dep_attn1": gathers[2]["token"]}
        if stage == "attn1":
            gathers[1] = _gather_forward("gather1_forward", gathers[1], after)
            return {"dep_attn2": gathers[1]["token"]}
        if stage == "ab_out":
            l_out, l_ffn, l_down, l_small = _gather_wait("gather1_wait", gathers[1], after)
            flat = l_small.reshape(NDEV, 24 * 128)
            return {
                "w_ab_out": l_out.reshape(D, D), "wt_ffn_in0": l_ffn.reshape(2 * DFF, D),
                "w_ffn_down0": l_down.reshape(DFF, D), "cd_norm_g": flat[:, 0:128].reshape(1, D),
                "conv_c_w32": jnp.pad(chan(flat, 128, CONV_C_TAPS), ((0, 1), (0, 0))),
                "conv_c_b": chan(flat, 2112, 1), "c_ln_g": chan(flat, 2176, 1), "c_ln_b": chan(flat, 2240, 1),
                "conv_d_w8": jnp.pad(chan(flat, 2304, CONV_D_TAPS), ((0, 8 - CONV_D_TAPS), (0, 0))),
            }
        if stage == "ffn_down0":
            gathers[2] = _gather_forward("gather2_forward", gathers[2], after)
            return {"dep_down0": gathers[2]["token"]}
        if stage == "cd_in":
            l_in, l_out, l_ffn, l_down = _gather_wait("gather2_wait", gathers[2], after)
            return {"wt_cd_in": l_in.reshape(CD_IN, D), "w_cd_out": l_out.reshape(D, D),
                    "wt_ffn_in1": l_ffn.reshape(2 * DFF, D), "w_ffn_down1": l_down.reshape(DFF, D)}
        return {}

    scatters = {}
    rides_with = {"w_ffn_down1": "wt_ffn_in1", "w_cd_out": "wt_cd_in", "w_ffn_down0": "wt_ffn_in0"}
    held = {}
    smalls = {}

    def small_exchange(name, block):
        land = _landing((NDEV,) + block.shape, F32, [((me,), block)])
        return _exchange_start(name, [block], [land], [(one(0), slot(0))])

    def on_grad(key, arr):
        if key == "small":
            parts = [arr["sgu_norm_g"], arr["sgu_norm_b"], arr["sgu_w"], arr["sgu_bias"], arr["q_norm_g"],
                     arr["k_norm_g"], arr["cd_norm_g"], arr["conv_c_w32"][:CONV_C_TAPS], arr["conv_c_b"], arr["c_ln_g"],
                     arr["c_ln_b"], arr["conv_d_w8"][:CONV_D_TAPS], arr["ffn_norm_g0"], arr["ffn_norm_g1"],
                     arr["loss_cols"]]
            smalls["sizes"] = [p.size for p in parts]
            rows = -(-sum(smalls["sizes"]) // 1024) * 8
            smalls["early"] = small_exchange("small_start", _pack_rows(parts, rows))
            return smalls["early"]["token"]
        if key in rides_with:
            held[rides_with[key]] = (key, arr)
            return None
        group = ([held.pop(key)] if key in held else []) + [(key, arr)]
        srcs, lands, items = [], [], []
        for n, (k, a) in enumerate(group):
            if k.startswith("wt_ffn_in"):
                src = a.reshape(2, NDEV, r_ff, D)
                own = lax.dynamic_slice_in_dim(src, me, 1, axis=1)
                lands.append(lax.dynamic_update_slice(lax.empty(src.shape, BF16), own, (0, me, 0, 0)))
                items += [((lambda S, j, n=n: S[n].at[0, j]), slot2(n, 0)), ((lambda S, j, n=n: S[n].at[1, j]), slot2(n, 1))]
            else:
                rows = a.shape[0] // NDEV
                src = a.reshape(NDEV, rows, D)
                own = lax.dynamic_index_in_dim(src, me, 0, keepdims=False)
                lands.append(_landing((1, NDEV, rows, D), BF16, [((0, me), own)]))
                items.append(((lambda S, j, n=n: S[n].at[j]), slot2(n, 0)))
            srcs.append(src)
        st = _exchange_start(f"scatter_{key}_start", srcs, lands, items)
        scatters[key] = (st, [k for k, _ in group])
        return st["token"]

    W = {
        "dep_first": gathers[0]["token"],
        "ab_norm_g": w["ab_norm_g"], "sgu_norm_g": w["sgu_norm_g"], "sgu_norm_b": w["sgu_norm_b"],
        "sgu_w": w["sgu_w"][0], "sgu_bias": w["sgu_bias"][0], "q_norm_g": w["q_norm_g"][0],
        "k_norm_g": w["k_norm_g"][0], "ffn_norm_g": w["ffn_norm_g"],
    }

    loss_cols, grad_x, G = _local_step(x[0], loss_target[0], W, fetch, on_grad)

    late_small = small_exchange("small_late_start", G["ab_norm_g"])
    landed = {}

    def wait_scatters(name, group_keys, others, after):
        res = _exchange_wait(name, [scatters[gk][0] for gk in group_keys] + others, after)
        for gk, lands in zip(group_keys, res):
            landed.update(zip(scatters[gk][1], lands))
        return [lands[0] for lands in res[len(group_keys):]]

    small_land, = wait_scatters("scatter_wait_early", ["wt_ffn_in1", "wt_cd_in", "wt_ffn_in0", "w_ab_out"],
                                [smalls["early"]], late_small["token"])

    grads, deltas, new_m, new_v = {}, {}, {}, {}
    done = []

    def put(name, res):
        grads[name], deltas[name], new_m[name], new_v[name] = res

    def adam(name, lands, sel, transposed):
        flip = (lambda a: jnp.swapaxes(a, 1, 2)) if transposed else (lambda a: a)
        res = _adam_stacked(lands, sel, flip(w[name]), flip(m[name]), flip(v[name]), f"adam_{name}")
        done.append(res[1])
        put(name, [flip(r) for r in res])

    ffn_in_lands = [landed["wt_ffn_in0"], landed["wt_ffn_in1"]]
    adam("cd_w_in", [landed["wt_cd_in"]], 0, True)
    adam("ffn_w_gate", ffn_in_lands, 0, True)
    adam("ffn_w_up", ffn_in_lands, 1, True)
    adam("cd_w_out", [landed["w_cd_out"]], 0, False)
    adam("ab_w_out", [landed["w_ab_out"]], 0, False)
    adam("ffn_w_down", [landed["w_ffn_down0"], landed["w_ffn_down1"]], 0, False)

    red = _sum_slots(small_land).reshape(-1)
    offs = [0]
    for s in smalls["sizes"]:
        offs.append(offs[-1] + s)
    seg = [red[offs[i]:offs[i + 1]] for i in range(len(smalls["sizes"]))]
    loss = jnp.sum(seg[14])

    def own_channels(full, taps):
        return lax.dynamic_slice_in_dim(full.reshape(taps, 512), me * SHARD_C, SHARD_C, axis=1)

    g_small = {
        "sgu_norm_g": seg[0].reshape(1, 512), "sgu_norm_b": seg[1].reshape(1, 512),
        "sgu_w": seg[2].reshape(512, 128), "sgu_bias": seg[3].reshape(4, 128), "q_norm_g": seg[4].reshape(3, 64),
        "k_norm_g": seg[5].reshape(3, 64),
        "cd_norm_g": lax.dynamic_slice_in_dim(seg[6].reshape(1, D), me * (D // NDEV), D // NDEV, axis=1),
        "conv_c_w": own_channels(seg[7], CONV_C_TAPS), "conv_c_b": own_channels(seg[8], 1),
        "c_ln_g": own_channels(seg[9], 1), "c_ln_b": own_channels(seg[10], 1),
        "conv_d_w": own_channels(seg[11], CONV_D_TAPS),
        "ffn_norm_g": jnp.concatenate([seg[12].reshape(1, D), seg[13].reshape(1, D)], axis=0),
    }

    def small_in(s, a):
        return jnp.swapaxes(a, 0, 1) if len(s) == 3 else a.reshape(s)

    def small_out(n, s, a):
        return jnp.swapaxes(a, 0, 1) if len(s) == 3 else a.reshape(w[n].shape)

    g_in = [g_small[n].reshape(s) for n, s in SMALL_SHAPES]
    d_s, m_s, v_s = _adam_small([small_in(s, w[n]) for n, s in SMALL_SHAPES], g_in,
                                [small_in(s, m[n]) for n, s in SMALL_SHAPES],
                                [small_in(s, v[n]) for n, s in SMALL_SHAPES])
    for i, (n, s) in enumerate(SMALL_SHAPES):
        grads[n], deltas[n] = small_out(n, s, g_in[i]), small_out(n, s, d_s[i])
        new_m[n], new_v[n] = small_out(n, s, m_s[i]), small_out(n, s, v_s[i])
    done.append(d_s[0])

    late_land, = wait_scatters("scatter_wait_last", ["wt_ab_in"], [late_small], list(done))
    put("ab_norm_g", _adam_of_slots(late_land, w["ab_norm_g"], m["ab_norm_g"], v["ab_norm_g"], "adam_ab_norm_g"))
    adam("ab_w_in", [landed["wt_ab_in"]], 0, True)

    return (loss, grad_x[None], *[grads[n] for n in WEIGHT_NAMES], *[deltas[n] for n in WEIGHT_NAMES],
            *[new_m[n] for n in WEIGHT_NAMES], *[new_v[n] for n in WEIGHT_NAMES])
```

```python
import jax
import jax.numpy as jnp
import numpy as np
from jax import lax
from jax.experimental import pallas as pl
from jax.experimental.pallas import tpu as pltpu

F32 = jnp.float32
BF16 = jnp.bfloat16

T = 4096
D = 1024
NDEV = 8
EPS = 1e-6
NEG_INF = -1e30
DFF = 2816
AB_IN = 5632
CD_IN = 2560
HEAD = 64
PAIR = 128
NPAIR = 4
NBACK = 128
DIL_RATES = (1, 4, 16)
ROPE_HALF = 8
ROPE_THETA = 500000.0
CONV_C_TAPS = 31
CONV_D_TAPS = 3
HALO = 32
ATTN_BWD_UNROLL = 4
MAX_ROW_STRIDE = 4

ADAM_LR = 0.001
ADAM_B1 = 0.9
ADAM_B2 = 0.999
ADAM_EPS = 1e-08
ADAM_WD = 0.01
ADAM_STEP = 10
ADAM_C1 = 1.0 / (1.0 - ADAM_B1 ** ADAM_STEP)
ADAM_C2 = 1.0 / (1.0 - ADAM_B2 ** ADAM_STEP)

VMEM_LIMIT_MB = 48
MESH = pl.DeviceIdType.MESH
HBM_SPEC = pl.BlockSpec(memory_space=pl.ANY)


def _cparams(ngrid, vmem_mb=VMEM_LIMIT_MB):
    return pltpu.CompilerParams(dimension_semantics=("arbitrary",) * ngrid,
                                vmem_limit_bytes=vmem_mb * 1024 * 1024)


def _pick(n, options):
    for o in options:
        if n % o == 0:
            return o
    raise ValueError(f"no tile for {n} in {options}")


def _sds(shape, dtype):
    return jax.ShapeDtypeStruct(shape, dtype)


def _sigmoid(x):
    return 1.0 / (1.0 + jnp.exp(-x))


def _sigmoid_bf16(x):
    return 0.5 * jnp.tanh(0.5 * x) + 0.5


def _gelu(z):
    return 0.5 * z * (1.0 + lax.erf(z * 0.7071067811865476))


def _gelu_grad(z):
    return 0.5 * (1.0 + lax.erf(z * 0.7071067811865476)) + z * jnp.exp(-0.5 * z * z) * 0.3989422804014327


STREAM_ROWS = 1024


def _stream_in(src, buf, sem, first, chunk=STREAM_ROWS):
    copies = [pltpu.make_async_copy(src.at[pl.ds(r, chunk), :], buf.at[pl.ds(r, chunk), :], sem.at[r // chunk])
              for r in range(0, src.shape[0], chunk)]

    @pl.when(first)
    def _():
        for cp in copies:
            cp.start()

    def wait(lo, hi, when):
        @pl.when(when)
        def _():
            for cp in copies[lo // chunk:hi // chunk]:
                cp.wait()

    return wait


def _stream_scratch(shape, dtype, chunk=STREAM_ROWS):
    return [pltpu.VMEM(shape, dtype), pltpu.SemaphoreType.DMA((shape[0] // chunk,))]


def _mm_nt(a, wt, name, out_dtype=BF16, dep=None):
    M, K = a.shape
    N = wt.shape[0]
    tn = _pick(N, (512, 256))

    def body(a_hbm, w_ref, *rest):
        o_ref, a_buf, sem = rest[-3:]
        first = pl.program_id(0) == 0
        wait = _stream_in(a_hbm, a_buf, sem, first)
        for r0 in range(0, M, 1024):
            wait(r0, r0 + 1024, first)
            o_ref[r0:r0 + 1024, :] = lax.dot_general(
                a_buf[r0:r0 + 1024, :], w_ref[...], (((1,), (1,)), ((), ())),
                preferred_element_type=F32).astype(o_ref.dtype)

    in_specs = [HBM_SPEC, pl.BlockSpec((tn, K), lambda j: (j, 0))]
    args = [a, wt]
    if dep is not None:
        in_specs.append(HBM_SPEC)
        args.append(dep)
    return pl.pallas_call(
        body, name=name, grid=(N // tn,), in_specs=in_specs, out_specs=pl.BlockSpec((M, tn), lambda j: (0, j)),
        out_shape=_sds((M, N), out_dtype), scratch_shapes=_stream_scratch((M, K), a.dtype),
        compiler_params=_cparams(1))(*args)


EPI_ROWS = 256


def _mm_nt_rows(a, wt, name, epilogue, side, side_specs, out_specs, out_shape, sums=(), dep=None, tm=512):
    M, K = a.shape
    N = wt.shape[0]
    ns, no = len(side), len(out_shape)

    def body(a_ref, w_ref, *rest):
        side_refs, outs, acc = rest[0:ns], rest[-1 - no:-1], rest[-1]
        acc[...] = lax.dot_general(a_ref[...], w_ref[...], (((1,), (1,)), ((), ())), preferred_element_type=F32)

        @pl.when(pl.program_id(0) == 0)
        def _():
            for j in sums:
                outs[j][...] = jnp.zeros_like(outs[j])

        for r0 in range(0, tm, EPI_ROWS):
            rows = slice(r0, r0 + EPI_ROWS)
            epilogue(acc[rows, :].astype(BF16).astype(F32), rows, side_refs, outs)

    in_specs = [pl.BlockSpec((tm, K), lambda i: (i, 0)),
                pl.BlockSpec((N, K), lambda i: (0, 0), pipeline_mode=pl.Buffered(1))] + list(side_specs)
    args = [a, wt, *side]
    if dep is not None:
        in_specs.append(HBM_SPEC)
        args.append(dep)
    return pl.pallas_call(
        body, name=name, grid=(M // tm,), in_specs=in_specs, out_specs=list(out_specs), out_shape=list(out_shape),
        scratch_shapes=[pltpu.VMEM((tm, N), F32)], compiler_params=_cparams(1))(*args)


def _mm_nn(a, w, name, mode, resid, gain=None, tgt=None, dep=None, tm=512):
    M, K = a.shape
    N = w.shape[1]
    side = gain if mode == "rms" else tgt

    def body(a_ref, w_ref, resid_ref, side_ref, *rest):
        outs, acc = rest[-3 if mode == "rms" else -4:-1], rest[-1]
        i = pl.program_id(0)
        acc[...] = jnp.dot(a_ref[...], w_ref[...], preferred_element_type=F32)

        if mode == "loss":
            @pl.when(i == 0)
            def _():
                outs[2][...] = jnp.zeros_like(outs[2])

        for r0 in range(0, tm, EPI_ROWS):
            rows = slice(r0, r0 + EPI_ROWS)
            v = acc[rows, :] + resid_ref[rows, :]
            if mode == "rms":
                outs[0][rows, :] = v
                r = lax.rsqrt(jnp.mean(v * v, axis=-1, keepdims=True) + EPS)
                outs[1][rows, :] = (v * r * side_ref[...]).astype(BF16)
            else:
                d = v - side_ref[rows, :]
                outs[2][...] += jnp.sum(d * d, axis=0, keepdims=True) * (0.5 / N)
                dy = d * (1.0 / N)
                outs[0][rows, :] = dy
                outs[1][rows, :] = dy.astype(BF16)

    row = pl.BlockSpec((tm, N), lambda i: (i, 0))
    vec = pl.BlockSpec((1, N), lambda i: (0, 0))
    in_specs = [pl.BlockSpec((tm, K), lambda i: (i, 0)),
                pl.BlockSpec((K, N), lambda i: (0, 0), pipeline_mode=pl.Buffered(1)), row,
                vec if mode == "rms" else row]
    args = [a, w, resid, side]
    if dep is not None:
        in_specs.append(HBM_SPEC)
        args.append(dep)
    if mode == "rms":
        out_specs, out_shape = [row, row], [_sds((M, N), F32), _sds((M, N), BF16)]
    else:
        out_specs, out_shape = [row, row, vec], [_sds((M, N), F32), _sds((M, N), BF16), _sds((1, N), F32)]
    return pl.pallas_call(
        body, name=name, grid=(M // tm,), in_specs=in_specs, out_specs=out_specs, out_shape=out_shape,
        scratch_shapes=[pltpu.VMEM((tm, N), F32)], compiler_params=_cparams(1))(*args)


def _mm_dh_rms_bwd(a, w, x, gain, dres, name, dep=None, tm=512, bf16_copy=True):
    parts = a.shape[0] if a.ndim == 3 else 1
    M, Kp = a.shape[-2], a.shape[-1]
    N = w.shape[1]
    nblk = M // tm
    assert nblk % 2 == 0
    nk = max(parts, 2)
    kc = parts * Kp // nk

    def body(a_ref, w_hbm, x_ref, g_ref, dres_ref, *rest):
        dg_ref, acc0, acc1, w_ref, sem = rest[-5:]
        dx_ref = rest[-7] if bf16_copy else rest[-6]
        dxb_ref = rest[-6] if bf16_copy else None
        i = pl.program_id(0)
        wait = _stream_in(w_hbm, w_ref, sem, i == 0, chunk=kc)

        def matmul(acc):
            d = None
            for c in range(nk):
                wait(c * kc, (c + 1) * kc, i == 0)
                lhs = a_ref[:, c * kc:(c + 1) * kc] if parts == 1 else a_ref[c]
                part = jnp.dot(lhs, w_ref[c * kc:(c + 1) * kc, :], preferred_element_type=F32)
                d = part if d is None else d + part
            acc[...] = d

        def finish(acc):
            for r0 in range(0, tm, EPI_ROWS // 2):
                rows = slice(r0, r0 + EPI_ROWS // 2)
                v = acc[rows, :]
                xf = x_ref[rows, :]
                r = lax.rsqrt(jnp.mean(xf * xf, axis=-1, keepdims=True) + EPS)
                xhat = xf * r
                dg_ref[...] += jnp.sum(v * xhat, axis=0, keepdims=True)
                dxh = v * g_ref[...]
                tot = dres_ref[rows, :] + r * (dxh - xhat * jnp.mean(dxh * xhat, axis=-1, keepdims=True))
                dx_ref[rows, :] = tot
                if bf16_copy:
                    dxb_ref[rows, :] = tot.astype(BF16)

        @pl.when(i == 0)
        def _():
            dg_ref[...] = jnp.zeros_like(dg_ref)
            matmul(acc0)

        @pl.when((i > 0) & (i < nblk) & (i % 2 == 1))
        def _():
            matmul(acc1)
            finish(acc0)

        @pl.when((i > 0) & (i < nblk) & (i % 2 == 0))
        def _():
            matmul(acc0)
            finish(acc1)

        @pl.when(i == nblk)
        def _():
            finish(acc1)

    last = nblk - 1
    row = pl.BlockSpec((tm, N), lambda i: (jnp.maximum(i - 1, 0), 0))
    vec = pl.BlockSpec((1, N), lambda i: (0, 0))
    if a.ndim == 3:
        a_spec = pl.BlockSpec((parts, tm, Kp), lambda i: (0, jnp.minimum(i, last), 0))
    else:
        a_spec = pl.BlockSpec((tm, Kp), lambda i: (jnp.minimum(i, last), 0))
    in_specs = [a_spec, HBM_SPEC, row, vec, row]
    args = [a, w, x, gain, dres]
    if dep is not None:
        in_specs.append(HBM_SPEC)
        args.append(dep)
    return pl.pallas_call(
        body, name=name, grid=(nblk + 1,), in_specs=in_specs,
        out_specs=[row, row, vec] if bf16_copy else [row, vec],
        out_shape=([_sds((M, N), F32), _sds((M, N), BF16), _sds((1, N), F32)] if bf16_copy
                   else [_sds((M, N), F32), _sds((1, N), F32)]),
        scratch_shapes=[pltpu.VMEM((tm, N), F32), pltpu.VMEM((tm, N), F32)]
        + _stream_scratch((parts * Kp, N), w.dtype, chunk=kc), compiler_params=_cparams(1, 56))(*args)


def _mm_tn(a, b, name, out_dtype=BF16, tt=2048, dep=None):
    parts = a.shape[0] if a.ndim == 3 else 1
    Tt, Mp = a.shape[-2], a.shape[-1]
    N = b.shape[1]
    tn = _pick(Mp, (1408, 1280, 1024, 512))
    jper = Mp // tn
    nt = Tt // tt

    def body(a_ref, b_hbm, *rest):
        o_ref, acc, b_buf, sem = rest[-4:]
        j, t = pl.program_id(0), pl.program_id(1)
        wait = _stream_in(b_hbm, b_buf, sem, (j == 0) & (t == 0))
        for step in range(nt):
            wait(step * tt, (step + 1) * tt, (j == 0) & (t == step))

        @pl.when(t == 0)
        def _():
            acc[...] = jnp.zeros_like(acc)

        rows = pl.ds(pl.multiple_of(t * tt, tt), tt)
        acc[...] += lax.dot_general(a_ref[...], b_buf[rows, :], (((0,), (0,)), ((), ())),
                                    preferred_element_type=F32)

        @pl.when(t == nt - 1)
        def _():
            o_ref[...] = acc[...].astype(o_ref.dtype)

    if a.ndim == 3:
        a_spec = pl.BlockSpec((None, tt, tn), lambda j, t: (j // jper, t, j % jper))
    else:
        a_spec = pl.BlockSpec((tt, tn), lambda j, t: (t, j))
    in_specs = [a_spec, HBM_SPEC]
    args = [a, b]
    if dep is not None:
        in_specs.append(HBM_SPEC)
        args.append(dep)
    return pl.pallas_call(
        body, name=name, grid=(parts * jper, nt), in_specs=in_specs,
        out_specs=pl.BlockSpec((tn, N), lambda j, t: (j, 0)),
        out_shape=_sds((parts * Mp, N), out_dtype),
        scratch_shapes=[pltpu.VMEM((tn, N), F32)] + _stream_scratch((Tt, N), b.dtype),
        compiler_params=_cparams(2))(*args)


FFN_ROWS = 2048


def _ffn_in(h, wt_in, name, tn=256):
    nj = DFF // tn

    def body(h_hbm, wg_ref, wu_ref, p_ref, act_ref, h_buf, sem):
        nt = (((1,), (1,)), ((), ()))
        first = pl.program_id(0) == 0
        wait = _stream_in(h_hbm, h_buf, sem, first)
        for r0 in range(0, T, FFN_ROWS):
            rows = slice(r0, r0 + FFN_ROWS)
            wait(r0, r0 + FFN_ROWS, first)
            g = lax.dot_general(h_buf[rows, :], wg_ref[...], nt, preferred_element_type=F32).astype(BF16)
            u = lax.dot_general(h_buf[rows, :], wu_ref[...], nt, preferred_element_type=F32).astype(BF16)
            p_ref[0, rows, :] = g
            p_ref[1, rows, :] = u
            act_ref[rows, :] = g * _sigmoid_bf16(g) * u

    return pl.pallas_call(
        body, name=name, grid=(nj,),
        in_specs=[HBM_SPEC, pl.BlockSpec((tn, D), lambda j: (j, 0)), pl.BlockSpec((tn, D), lambda j: (j + nj, 0))],
        out_specs=[pl.BlockSpec((2, T, tn), lambda j: (0, 0, j)), pl.BlockSpec((T, tn), lambda j: (0, j))],
        out_shape=[_sds((2, T, DFF), BF16), _sds((T, DFF), BF16)], scratch_shapes=_stream_scratch((T, D), BF16),
        compiler_params=_cparams(1))(h, wt_in, wt_in)


def _ffn_dact(dyb, w_down, p3, name, tn=256, dep=None):
    def body(dy_hbm, w_ref, p_ref, *rest):
        o_ref, dy_buf, sem = rest[-3:]
        first = pl.program_id(0) == 0
        wait = _stream_in(dy_hbm, dy_buf, sem, first)
        for r0 in range(0, T, FFN_ROWS):
            rows = slice(r0, r0 + FFN_ROWS)
            wait(r0, r0 + FFN_ROWS, first)
            da = lax.dot_general(dy_buf[rows, :], w_ref[...], (((1,), (1,)), ((), ())),
                                 preferred_element_type=F32).astype(BF16)
            g = p_ref[0, rows, :]
            u = p_ref[1, rows, :]
            sg = _sigmoid_bf16(g)
            gs = g * sg
            o_ref[0, rows, :] = (da * u) * (sg + gs * (1.0 - sg))
            o_ref[1, rows, :] = da * gs

    pspec = pl.BlockSpec((2, T, tn), lambda j: (0, 0, j))
    in_specs = [HBM_SPEC, pl.BlockSpec((tn, D), lambda j: (j, 0)), pspec]
    args = [dyb, w_down, p3]
    if dep is not None:
        in_specs.append(HBM_SPEC)
        args.append(dep)
    return pl.pallas_call(
        body, name=name, grid=(DFF // tn,), in_specs=in_specs, out_specs=pspec,
        out_shape=_sds((2, T, DFF), BF16), scratch_shapes=_stream_scratch((T, D), BF16),
        compiler_params=_cparams(1))(*args)


def _rms_fwd(x, g, name, tm=512, dep=None):
    def body(x_ref, g_ref, *rest):
        h_ref = rest[-1]
        xf = x_ref[...]
        r = lax.rsqrt(jnp.mean(xf * xf, axis=-1, keepdims=True) + EPS)
        h_ref[...] = (xf * r * g_ref[...]).astype(BF16)

    in_specs = [pl.BlockSpec((tm, D), lambda i: (i, 0)), pl.BlockSpec((1, D), lambda i: (0, 0))]
    args = [x, g]
    if dep is not None:
        in_specs.append(HBM_SPEC)
        args.append(dep)
    return pl.pallas_call(
        body, name=name, grid=(T // tm,), in_specs=in_specs, out_specs=pl.BlockSpec((tm, D), lambda i: (i, 0)),
        out_shape=_sds((T, D), BF16), compiler_params=_cparams(1))(*args)


def _tril_mask():
    r = lax.broadcasted_iota(jnp.int32, (128, 128), 0)
    c = lax.broadcasted_iota(jnp.int32, (128, 128), 1)
    return r >= c


def _mix_a_fwd(pab, sgu_g, sgu_b, sgu_w, sgu_bias3, tm=512):
    def body(zu_ref, zv_ref, g_ref, b_ref, w_ref, bias_ref, o_ref):
        u = _gelu(zu_ref[...].astype(F32))
        v = _gelu(zv_ref[...].astype(F32))
        mu = jnp.mean(v, axis=-1, keepdims=True)
        vc = v - mu
        rstd = lax.rsqrt(jnp.mean(vc * vc, axis=-1, keepdims=True) + EPS)
        vn = (vc * rstd * g_ref[...] + b_ref[...]).astype(BF16)
        tri = _tril_mask()
        for gi in range(4):
            wg = jnp.where(tri, w_ref[gi], 0.0).astype(BF16)
            bg = bias_ref[gi]
            for c in range(tm // 128):
                rs, cs = slice(c * 128, (c + 1) * 128), slice(gi * 128, (gi + 1) * 128)
                mixed = jnp.dot(wg, vn[rs, cs], preferred_element_type=F32) + bg
                o_ref[rs, cs] = (u[rs, cs] * mixed).astype(BF16)

    half = pl.BlockSpec((tm, 512), lambda i: (i, 0))
    return pl.pallas_call(
        body, name="mix_a_fwd", grid=(T // tm,),
        in_specs=[half, pl.BlockSpec((tm, 512), lambda i: (i, 1)),
                  pl.BlockSpec((1, 512), lambda i: (0, 0)), pl.BlockSpec((1, 512), lambda i: (0, 0)),
                  pl.BlockSpec((4, 128, 128), lambda i: (0, 0, 0)), pl.BlockSpec((4, 128, 1), lambda i: (0, 0, 0))],
        out_specs=half, out_shape=_sds((T, D), BF16), compiler_params=_cparams(1),
    )(pab, pab, sgu_g, sgu_b, sgu_w, sgu_bias3)


def _rope_tables():
    pos = np.arange(T, dtype=np.float32)
    inv_freq = np.float32(ROPE_THETA) ** (-np.arange(ROPE_HALF, dtype=np.float32) * np.float32(2.0 / (2 * ROPE_HALF)))
    ang = (pos[:, None] * inv_freq[None, :]).astype(np.float32)
    cos, sin = np.cos(ang), np.sin(ang)
    z8 = np.zeros((T, ROPE_HALF), np.float32)
    rest = np.zeros((T, HEAD - 2 * ROPE_HALF), np.float32)
    c64 = np.concatenate([cos, cos, rest + 1.0], axis=1)
    s1 = np.concatenate([z8, sin, rest], axis=1)
    s2 = np.concatenate([-sin, z8, rest], axis=1)
    return tuple(jnp.asarray(np.tile(t, (1, 2)).astype(np.float32)) for t in (c64, s1, s2))


def _lo_mask(shape):
    return lax.broadcasted_iota(jnp.int32, shape, 1) < HEAD


def _seg_mean(x, lo):
    s_all = jnp.sum(x, axis=-1, keepdims=True)
    s_lo = jnp.sum(jnp.where(lo, x, 0.0), axis=-1, keepdims=True)
    return jnp.where(lo, s_lo, s_all - s_lo) * (1.0 / HEAD)


def _head_blocks():
    r = lax.broadcasted_iota(jnp.int32, (PAIR, PAIR), 0) < HEAD
    c = lax.broadcasted_iota(jnp.int32, (PAIR, PAIR), 1) < HEAD
    return jnp.where(r == c, 1.0, 0.0).astype(BF16)


def _seg_mean_mxu(x, blocks):
    return jnp.dot(x.astype(BF16), blocks, preferred_element_type=F32) * (1.0 / HEAD)


def _rope(n, c, s1, s2):
    return n * c + pltpu.roll(n, ROPE_HALF, 1) * s1 + pltpu.roll(n, PAIR - ROPE_HALF, 1) * s2


def _rope_t(dy, c, s1, s2):
    return dy * c - pltpu.roll(dy, PAIR - ROPE_HALF, 1) * s2 - pltpu.roll(dy, ROPE_HALF, 1) * s1


def _prep_fwd(pab, qg, kg, tabs, tm=512):
    def body(p_ref, qg_ref, kg_ref, c_ref, s1_ref, s2_ref, *outs):
        blocks = _head_blocks()
        c, s1, s2 = c_ref[...], s1_ref[...], s2_ref[...]
        for g in range(3):
            qn_ref, kn_ref, v_ref = outs[3 * g:3 * g + 3]
            for p in range(NPAIR):
                for which, gains, dst in ((0, qg_ref, qn_ref), (1, kg_ref, kn_ref)):
                    col = (2 + 3 * which + g) * 512 + p * PAIR
                    xr = p_ref[:, col:col + PAIR].astype(F32)
                    rinv = lax.rsqrt(_seg_mean_mxu(xr * xr, blocks) + EPS)
                    outs[9 + 2 * g + which][p] = rinv.astype(BF16)
                    dst[p] = _rope(xr * rinv * gains[g:g + 1, :], c, s1, s2)
                col = (8 + g) * 512 + p * PAIR
                v_ref[p] = p_ref[:, col:col + PAIR].astype(F32)

    pm = pl.BlockSpec((NPAIR, tm, PAIR), lambda i: (0, i, 0))
    tab = pl.BlockSpec((tm, PAIR), lambda i: (i, 0))
    gain = pl.BlockSpec((3, PAIR), lambda i: (0, 0))
    res = pl.pallas_call(
        body, name="prep_fwd", grid=(T // tm,),
        in_specs=[pl.BlockSpec((tm, AB_IN), lambda i: (i, 0)), gain, gain, tab, tab, tab],
        out_specs=[pm] * 15, out_shape=[_sds((NPAIR, T, PAIR), F32)] * 9 + [_sds((NPAIR, T, PAIR), BF16)] * 6,
        compiler_params=_cparams(1))(pab, qg, kg, *tabs)
    return res[0:9], res[9:15]


def _res_index(it, rate):
    window = NBACK * rate
    b = it // rate
    rho = it % rate
    start = b * window + rho
    startp = jnp.maximum(start - window, rho)
    kmin = jnp.where(b > 0, 0, NBACK)
    return start, startp, kmin


def _rows(start, rate):
    if rate == 1:
        return pl.ds(pl.multiple_of(start, NBACK), NBACK)
    return pl.ds(start, NBACK, stride=rate)


def _band_bias():
    qs = lax.broadcasted_iota(jnp.int32, (2 * NBACK, 2 * NBACK), 0)
    kj = lax.broadcasted_iota(jnp.int32, (2 * NBACK, 2 * NBACK), 1)
    dist = (qs & (NBACK - 1)) + NBACK - kj
    both = (dist >= 0) & (dist <= NBACK)
    return jnp.where(both, 0.0, NEG_INF), jnp.where(both & (kj >= NBACK), 0.0, NEG_INF)


def _attn_fwd_block(q, kcat, vcat, first, lo, biases):
    vcat1 = jnp.concatenate([vcat, jnp.ones((2 * NBACK, PAIR), BF16)], axis=1)
    q2 = jnp.concatenate([jnp.where(lo, q, 0.0), jnp.where(lo, 0.0, q)], axis=0).astype(BF16)
    s = lax.dot_general(q2, kcat, (((1,), (1,)), ((), ())), preferred_element_type=F32)
    s = s + jnp.where(first, biases[1], biases[0])
    m = jnp.max(s, axis=-1, keepdims=True)
    ol = jnp.dot(jnp.exp(s - m).astype(BF16), vcat1, preferred_element_type=F32)
    o2 = ol[:, 0:PAIR] / ol[:, PAIR:]
    ls = m + jnp.log(ol[:, PAIR:])
    return jnp.where(lo, o2[0:NBACK], o2[NBACK:]), jnp.where(lo, ls[0:NBACK], ls[NBACK:])


def _attn_fwd(qn, kn, v, rate, name, dep=None):
    if rate > MAX_ROW_STRIDE:
        return _attn_fwd_gathered(qn, kn, v, rate, name, dep)

    def body(q_ref, k_ref, v_ref, *rest):
        o_ref, l_ref = rest[-2:]
        lo = _lo_mask((NBACK, PAIR))
        biases = _band_bias()

        def step(it, carry):
            start, startp, kmin = _res_index(it, rate)
            q = q_ref[_rows(start, rate), :] * (HEAD ** -0.5)
            kcat = jnp.concatenate([k_ref[_rows(startp, rate), :], k_ref[_rows(start, rate), :]], axis=0).astype(BF16)
            vcat = jnp.concatenate([v_ref[_rows(startp, rate), :], v_ref[_rows(start, rate), :]], axis=0).astype(BF16)
            o, ls = _attn_fwd_block(q, kcat, vcat, kmin != 0, lo, biases)
            o_ref[_rows(start, rate), :] = o
            l_ref[_rows(start, rate), :] = ls
            return carry

        lax.fori_loop(0, T // NBACK, step, 0, unroll=4)

    pm = pl.BlockSpec((None, T, PAIR), lambda p: (p, 0, 0))
    in_specs, args = [pm, pm, pm], [qn, kn, v]
    if dep is not None:
        in_specs.append(HBM_SPEC)
        args.append(dep)
    return pl.pallas_call(
        body, name=name, grid=(NPAIR,), in_specs=in_specs, out_specs=[pm, pm],
        out_shape=[_sds((NPAIR, T, PAIR), F32)] * 2, compiler_params=_cparams(1))(*args)


def _attn_fwd_gathered(qn, kn, v, rate, name, dep):
    n = T // rate
    nblk = n // NBACK

    def body(q_hbm, k_hbm, v_hbm, *rest):
        o_hbm, l_hbm, qb, kb, vb, ob, lb, in_sem, out_sem = rest[-9:]
        p = pl.program_id(0)
        slot = p % 2

        def loads(pair, s):
            return [pltpu.make_async_copy(x.at[pair, :, r, :], buf.at[s, r], in_sem.at[3 * s + a])
                    for a, (x, buf) in enumerate(((q_hbm, qb), (k_hbm, kb), (v_hbm, vb))) for r in range(rate)]

        def stores(pair, s):
            return [pltpu.make_async_copy(buf.at[s, r], x.at[pair, :, r, :], out_sem.at[2 * s + a])
                    for a, (x, buf) in enumerate(((o_hbm, ob), (l_hbm, lb))) for r in range(rate)]

        @pl.when(p == 0)
        def _():
            for c in loads(0, 0):
                c.start()

        @pl.when(p + 1 < NPAIR)
        def _():
            for c in loads(p + 1, 1 - slot):
                c.start()

        for c in loads(p, slot):
            c.wait()

        @pl.when(p >= 2)
        def _():
            for c in stores(p - 2, slot):
                c.wait()

        lo = _lo_mask((NBACK, PAIR))
        biases = _band_bias()

        def step(it, carry):
            b, r = it % nblk, it // nblk
            cur = pl.ds(pl.multiple_of(b * NBACK, NBACK), NBACK)
            prev = pl.ds(pl.multiple_of(jnp.maximum(b - 1, 0) * NBACK, NBACK), NBACK)
            q = qb[slot, r, cur, :] * (HEAD ** -0.5)
            kcat = jnp.concatenate([kb[slot, r, prev, :], kb[slot, r, cur, :]], axis=0).astype(BF16)
            vcat = jnp.concatenate([vb[slot, r, prev, :], vb[slot, r, cur, :]], axis=0).astype(BF16)
            o, ls = _attn_fwd_block(q, kcat, vcat, b == 0, lo, biases)
            ob[slot, r, cur, :] = o
            lb[slot, r, cur, :] = ls
            return carry

        lax.fori_loop(0, T // NBACK, step, 0, unroll=4)

        for c in stores(p, slot):
            c.start()

        @pl.when(p == NPAIR - 1)
        def _():
            for c in stores(p - 1, 1 - slot) + stores(p, slot):
                c.wait()

    by_residue = lambda a: a.reshape(NPAIR, n, rate, PAIR)
    in_specs, args = [HBM_SPEC] * 3, [by_residue(qn), by_residue(kn), by_residue(v)]
    if dep is not None:
        in_specs.append(HBM_SPEC)
        args.append(dep)
    o, l = pl.pallas_call(
        body, name=name, grid=(NPAIR,), in_specs=in_specs, out_specs=[HBM_SPEC] * 2,
        out_shape=[_sds((NPAIR, n, rate, PAIR), F32)] * 2,
        scratch_shapes=[pltpu.VMEM((2, rate, n, PAIR), F32)] * 5
        + [pltpu.SemaphoreType.DMA((6,)), pltpu.SemaphoreType.DMA((4,))],
        compiler_params=_cparams(1))(*args)
    return o.reshape(NPAIR, T, PAIR), l.reshape(NPAIR, T, PAIR)


def _merge_fwd(cat_ab, outs, lses, tm=512):
    def body(cat_in, o0, o1, o2, l0, l1, l2, cat_ref, lse_ref):
        del cat_in
        for p in range(NPAIR):
            a0, a1, a2 = l0[p], l1[p], l2[p]
            m = jnp.maximum(jnp.maximum(a0, a1), a2)
            w0, w1, w2 = jnp.exp(a0 - m), jnp.exp(a1 - m), jnp.exp(a2 - m)
            s = w0 + w1 + w2
            b = (w0 * o0[p] + w1 * o1[p] + w2 * o2[p]) / s
            cat_ref[:, p * PAIR:(p + 1) * PAIR] = b.astype(BF16)
            lse_ref[p] = m + jnp.log(s)

    pm = pl.BlockSpec((NPAIR, tm, PAIR), lambda i: (0, i, 0))
    return pl.pallas_call(
        body, name="merge_fwd", grid=(T // tm,),
        in_specs=[pl.BlockSpec(memory_space=pl.ANY)] + [pm] * 6,
        out_specs=[pl.BlockSpec((tm, 512), lambda i: (i, 1)), pm],
        out_shape=[_sds((T, D), BF16), _sds((NPAIR, T, PAIR), F32)],
        input_output_aliases={0: 0}, compiler_params=_cparams(1))(cat_ab, *outs, *lses)


def _d_cat_ab(dxb, w_ab_out, cat, dep, tm=512):
    def epilogue(d, rows, side, outs):
        (b_ref,), (da_ref, dbp_ref, e_ref) = side, outs
        da_ref[rows, :] = d[:, 0:512].astype(BF16)
        lo = _lo_mask((EPI_ROWS, PAIR))
        for p in range(NPAIR):
            db = d[:, 512 + p * PAIR:512 + (p + 1) * PAIR]
            b = b_ref[rows, p * PAIR:(p + 1) * PAIR].astype(F32)
            dbp_ref[p, rows, :] = db
            e_ref[p, rows, :] = _seg_mean(db * b, lo) * float(HEAD)

    pm = pl.BlockSpec((NPAIR, tm, PAIR), lambda i: (0, i, 0))
    return _mm_nt_rows(
        dxb, w_ab_out, "mm_d_cat_ab", epilogue, [cat], [pl.BlockSpec((tm, 512), lambda i: (i, 1))],
        [pl.BlockSpec((tm, 512), lambda i: (i, 0)), pm, pm],
        [_sds((T, 512), BF16), _sds((NPAIR, T, PAIR), F32), _sds((NPAIR, T, PAIR), F32)], dep=dep, tm=tm)


def _attn_bwd_block(q, db, ev, ls, kcat, vcat, first, lo, biases):
    scale = HEAD ** -0.5
    nt = (((1,), (1,)), ((), ()))
    tn = (((0,), (0,)), ((), ()))
    q = q * scale
    q2 = jnp.concatenate([jnp.where(lo, q, 0.0), jnp.where(lo, 0.0, q)], axis=0).astype(BF16)
    db2 = jnp.concatenate([jnp.where(lo, db, 0.0), jnp.where(lo, 0.0, db)], axis=0).astype(BF16)
    ls2 = jnp.concatenate([ls[:, 0:1], ls[:, HEAD:HEAD + 1]], axis=0)
    ev2 = jnp.concatenate([ev[:, 0:1], ev[:, HEAD:HEAD + 1]], axis=0)
    s = lax.dot_general(q2, kcat, nt, preferred_element_type=F32)
    pt = jnp.exp(s + jnp.where(first, biases[1], biases[0]) - ls2)
    dp = lax.dot_general(db2, vcat, nt, preferred_element_type=F32)
    ds = (pt * (dp - ev2)).astype(BF16)
    dq2 = jnp.dot(ds, kcat, preferred_element_type=F32) * scale
    dkc = lax.dot_general(ds, q2, tn, preferred_element_type=F32)
    dvc = lax.dot_general(pt.astype(BF16), db2, tn, preferred_element_type=F32)
    return jnp.where(lo, dq2[0:NBACK], dq2[NBACK:]), dkc, dvc


def _attn_bwd_loop(read, write, nblk):
    lo = _lo_mask((NBACK, PAIR))
    biases = _band_bias()

    def one(it, carry):
        dk_carry, dv_carry = carry
        rho = it // nblk
        b = it % nblk
        bp = jnp.maximum(b - 1, 0)
        kcat = jnp.concatenate([read(1, rho, bp), read(1, rho, b)], axis=0).astype(BF16)
        vcat = jnp.concatenate([read(2, rho, bp), read(2, rho, b)], axis=0).astype(BF16)
        dq, dkc, dvc = _attn_bwd_block(read(0, rho, b), read(3, rho, b), read(4, rho, b), read(5, rho, b), kcat, vcat,
                                       b == 0, lo, biases)
        write(0, rho, b, dq)
        write(1, rho, bp, dk_carry + dkc[0:NBACK])
        write(1, rho, b, dkc[NBACK:])
        write(2, rho, bp, dv_carry + dvc[0:NBACK])
        write(2, rho, b, dvc[NBACK:])
        return dkc[NBACK:], dvc[NBACK:]

    def step(i, carry):
        for u in range(ATTN_BWD_UNROLL):
            carry = one(i * ATTN_BWD_UNROLL + u, carry)
        return carry

    zero = jnp.zeros((NBACK, PAIR), F32)
    lax.fori_loop(0, T // NBACK // ATTN_BWD_UNROLL, step, (zero, zero))


def _attn_bwd(qn, kn, v, dbp, e, lse, rate, name):
    if rate > MAX_ROW_STRIDE:
        return _attn_bwd_gathered(qn, kn, v, dbp, e, lse, rate, name)
    window = NBACK * rate

    def body(*refs):
        rows = lambda rho, b: _rows(b * window + rho, rate)

        def write(j, rho, b, value):
            refs[6 + j][rows(rho, b), :] = value

        _attn_bwd_loop(lambda j, rho, b: refs[j][rows(rho, b), :], write, T // window)

    pm = pl.BlockSpec((None, T, PAIR), lambda p: (p, 0, 0))
    return pl.pallas_call(
        body, name=name, grid=(NPAIR,), in_specs=[pm] * 6, out_specs=[pm] * 3,
        out_shape=[_sds((NPAIR, T, PAIR), F32)] * 3, compiler_params=_cparams(1, 56))(qn, kn, v, dbp, e, lse)


def _attn_bwd_gathered(qn, kn, v, dbp, e, lse, rate, name):
    n = T // rate

    def body(*refs):
        ins, outs, in_bufs, out_bufs, (in_sem, out_sem) = refs[0:6], refs[6:9], refs[9:15], refs[15:18], refs[18:20]
        p = pl.program_id(0)
        slot = p % 2

        def loads(pair, s):
            return [pltpu.make_async_copy(x.at[pair, :, r, :], buf.at[s, r], in_sem.at[6 * s + a])
                    for a, (x, buf) in enumerate(zip(ins, in_bufs)) for r in range(rate)]

        def stores(pair, s):
            return [pltpu.make_async_copy(buf.at[s, r], x.at[pair, :, r, :], out_sem.at[3 * s + a])
                    for a, (x, buf) in enumerate(zip(outs, out_bufs)) for r in range(rate)]

        @pl.when(p == 0)
        def _():
            for c in loads(0, 0):
                c.start()

        @pl.when(p + 1 < NPAIR)
        def _():
            for c in loads(p + 1, 1 - slot):
                c.start()

        for c in loads(p, slot):
            c.wait()

        @pl.when(p >= 2)
        def _():
            for c in stores(p - 2, slot):
                c.wait()

        rows = lambda b: pl.ds(pl.multiple_of(b * NBACK, NBACK), NBACK)

        def write(j, rho, b, value):
            out_bufs[j][slot, rho, rows(b), :] = value

        _attn_bwd_loop(lambda j, rho, b: in_bufs[j][slot, rho, rows(b), :], write, n // NBACK)

        for c in stores(p, slot):
            c.start()

        @pl.when(p == NPAIR - 1)
        def _():
            for c in stores(p - 1, 1 - slot) + stores(p, slot):
                c.wait()

    by_residue = lambda a: a.reshape(NPAIR, n, rate, PAIR)
    res = pl.pallas_call(
        body, name=name, grid=(NPAIR,), in_specs=[HBM_SPEC] * 6, out_specs=[HBM_SPEC] * 3,
        out_shape=[_sds((NPAIR, n, rate, PAIR), F32)] * 3,
        scratch_shapes=[pltpu.VMEM((2, rate, n, PAIR), F32)] * 9
        + [pltpu.SemaphoreType.DMA((12,)), pltpu.SemaphoreType.DMA((6,))],
        compiler_params=_cparams(1, 56))(*[by_residue(a) for a in (qn, kn, v, dbp, e, lse)])
    return [r.reshape(NPAIR, T, PAIR) for r in res]


def _ab_in_bwd(pab, dcat, sgu_g, sgu_b, sgu_w, sgu_bias3, qg, kg, tabs, dqkv, rinvs, tm=256):
    def body(p_ref, dcat_ref, g_ref, b_ref, w_ref, bias_ref, qg_ref, kg_ref, c_ref, s1_ref, s2_ref, *rest):
        dq_refs, rinv_refs = rest[0:9], rest[9:15]
        o_ref, dwm_ref, dbias_ref, dsg_ref, dsb_ref, dgain_ref = rest[15:]
        i = pl.program_id(0)

        @pl.when(i == 0)
        def _():
            dwm_ref[...] = jnp.zeros_like(dwm_ref)
            dbias_ref[...] = jnp.zeros_like(dbias_ref)
            dsg_ref[...] = jnp.zeros_like(dsg_ref)
            dsb_ref[...] = jnp.zeros_like(dsb_ref)
            dgain_ref[...] = jnp.zeros_like(dgain_ref)

        zu = p_ref[:, 0:512].astype(F32)
        zv = p_ref[:, 512:1024].astype(F32)
        u = _gelu(zu)
        v = _gelu(zv)
        mu = jnp.mean(v, axis=-1, keepdims=True)
        vc = v - mu
        rstd = lax.rsqrt(jnp.mean(vc * vc, axis=-1, keepdims=True) + EPS)
        xhat = vc * rstd
        vn = (xhat * g_ref[...] + b_ref[...]).astype(BF16)
        da = dcat_ref[...].astype(F32)
        tri = _tril_mask()
        du_parts = [[None] * 4 for _ in range(tm // 128)]
        dvn_parts = [[None] * 4 for _ in range(tm // 128)]
        for gi in range(4):
            wg = jnp.where(tri, w_ref[gi], 0.0).astype(BF16)
            bg = bias_ref[gi]
            for c in range(tm // 128):
                rs, cs = slice(c * 128, (c + 1) * 128), slice(gi * 128, (gi + 1) * 128)
                vblk = vn[rs, cs]
                mixed = jnp.dot(wg, vblk, preferred_element_type=F32) + bg
                dab = da[rs, cs]
                du_parts[c][gi] = dab * mixed
                dmixed = dab * u[rs, cs]
                dmb = dmixed.astype(BF16)
                dvn_parts[c][gi] = lax.dot_general(wg, dmb, (((0,), (0,)), ((), ())), preferred_element_type=F32)
                dwm = lax.dot_general(dmb, vblk, (((1,), (1,)), ((), ())), preferred_element_type=F32)
                dwm_ref[gi] += jnp.where(tri, dwm, 0.0)
                dbias_ref[gi] += dmixed
        du = jnp.concatenate([jnp.concatenate(r, axis=1) for r in du_parts], axis=0)
        dvn = jnp.concatenate([jnp.concatenate(r, axis=1) for r in dvn_parts], axis=0)
        dsg_ref[...] += jnp.sum(dvn * xhat, axis=0, keepdims=True)
        dsb_ref[...] += jnp.sum(dvn, axis=0, keepdims=True)
        dxh = dvn * g_ref[...]
        dv = rstd * (dxh - jnp.mean(dxh, axis=-1, keepdims=True)
                     - xhat * jnp.mean(dxh * xhat, axis=-1, keepdims=True))
        o_ref[:, 0:512] = (du * _gelu_grad(zu)).astype(BF16)
        o_ref[:, 512:1024] = (dv * _gelu_grad(zv)).astype(BF16)

        blocks = _head_blocks()
        c, s1, s2 = c_ref[...], s1_ref[...], s2_ref[...]
        for g in range(3):
            dq_ref, dk_ref, dv_ref = dq_refs[3 * g:3 * g + 3]
            for p in range(NPAIR):
                for which, gains, src in ((0, qg_ref, dq_ref), (1, kg_ref, dk_ref)):
                    col = (2 + 3 * which + g) * 512 + p * PAIR
                    xr = p_ref[:, col:col + PAIR].astype(F32)
                    rinv = rinv_refs[2 * g + which][p].astype(F32)
                    xh = xr * rinv
                    dn = _rope_t(src[p], c, s1, s2)
                    row = 2 * g + which
                    dgain_ref[row:row + 1, :] += jnp.sum(dn * xh, axis=0, keepdims=True)
                    dxh2 = dn * gains[g:g + 1, :]
                    dx = rinv * (dxh2 - xh * _seg_mean_mxu(dxh2 * xh, blocks))
                    o_ref[:, col:col + PAIR] = dx.astype(BF16)
                col = (8 + g) * 512 + p * PAIR
                o_ref[:, col:col + PAIR] = dv_ref[p].astype(BF16)

    pm = pl.BlockSpec((NPAIR, tm, PAIR), lambda i: (0, i, 0))
    tab = pl.BlockSpec((tm, PAIR), lambda i: (i, 0))
    gain = pl.BlockSpec((3, PAIR), lambda i: (0, 0))
    vec = pl.BlockSpec((1, 512), lambda i: (0, 0))
    full = pl.BlockSpec((tm, AB_IN), lambda i: (i, 0))
    w4 = pl.BlockSpec((4, 128, 128), lambda i: (0, 0, 0))
    return pl.pallas_call(
        body, name="ab_in_bwd", grid=(T // tm,),
        in_specs=[full, pl.BlockSpec((tm, 512), lambda i: (i, 0)), vec, vec, w4,
                  pl.BlockSpec((4, 128, 1), lambda i: (0, 0, 0)), gain, gain, tab, tab, tab] + [pm] * 15,
        out_specs=[full, w4, w4, vec, vec, pl.BlockSpec((8, PAIR), lambda i: (0, 0))],
        out_shape=[_sds((T, AB_IN), BF16), _sds((4, 128, 128), F32), _sds((4, 128, 128), F32),
                   _sds((1, 512), F32), _sds((1, 512), F32), _sds((8, PAIR), F32)],
        compiler_params=_cparams(1))(pab, dcat, sgu_g, sgu_b, sgu_w, sgu_bias3, qg, kg, *tabs, *dqkv, *rinvs)


def _ln_stats(x):
    mu = jnp.mean(x, axis=-1, keepdims=True)
    xc = x - mu
    rstd = lax.rsqrt(jnp.mean(xc * xc, axis=-1, keepdims=True) + EPS)
    return xc * rstd, rstd


CONV_RC = 64


def _shifted_copies(src, dst, tm):
    dst[0] = src[...]
    for b in range(1, 8):
        dst[b, 0:tm + HALO - 8, :] = src[pl.ds(b, tm + HALO - 8), :]


def _offsets_by_phase(first):
    groups = {}
    for o in range(first, first + CONV_C_TAPS):
        groups.setdefault(o % 8, []).append(o)
    return sorted(groups.items())


def _window(shifted, b8, base, offsets, lanes):
    rows = 8 * (max(offsets) // 8) + CONV_RC
    return shifted[b8, pl.ds(base, rows), lanes].reshape(rows // 8, 8, 128)


def _cd_fwd(pcd, cw, cb, lg, lb, dw, tm=512):
    per = tm // HALO

    def body(p_ref, h_ref, cw_ref, cb_ref, lg_ref, lb_ref, dw_ref, cat_ref, c0_ref, c1_ref, dd_ref, y_ref,
             buf, buf2, sb):
        i = pl.program_id(0)
        live = jnp.where(i > 0, 1.0, 0.0)
        a = p_ref[:, 0:512].astype(F32)
        gt = p_ref[:, 512:1024].astype(F32)
        gb = p_ref[:, 1024:1536].astype(F32)
        gc = p_ref[:, 1536:2048].astype(F32)
        hv = p_ref[:, 2048:2560].astype(F32)
        c0 = a * _sigmoid(gt)
        dd = gc * hv
        buf[0:HALO, :] = h_ref[:, 0:512].astype(F32) * _sigmoid(h_ref[:, 512:1024].astype(F32)) * live
        buf[HALO:, :] = c0
        buf2[0:HALO, :] = h_ref[:, 1536:2048].astype(F32) * h_ref[:, 2048:2560].astype(F32) * live
        buf2[HALO:, :] = dd
        c0_ref[...] = c0.astype(BF16)
        dd_ref[...] = dd.astype(BF16)
        _shifted_copies(buf, sb, tm)

        def conv_rows(r, carry):
            base = pl.multiple_of(r * CONV_RC, CONV_RC)
            for c in range(4):
                lanes = slice(c * 128, (c + 1) * 128)
                acc = jnp.broadcast_to(cb_ref[:, lanes], (CONV_RC // 8, 8, 128))
                for b8, offsets in _offsets_by_phase(HALO - (CONV_C_TAPS - 1)):
                    win = _window(sb, b8, base, offsets, lanes)
                    for o in offsets:
                        j = o - (HALO - (CONV_C_TAPS - 1))
                        acc = acc + cw_ref[8 * j:8 * j + 8, lanes] * win[o // 8:o // 8 + CONV_RC // 8]
                c1_ref[pl.ds(base, CONV_RC), lanes] = acc.reshape(CONV_RC, 128)
            return carry

        lax.fori_loop(0, tm // CONV_RC, conv_rows, 0)
        xhat, _ = _ln_stats(c1_ref[...])
        c2 = xhat * lg_ref[...] + lb_ref[...]
        y = jnp.zeros((tm, 512), F32)
        for j in range(CONV_D_TAPS):
            y = y + dw_ref[j:j + 1, :] * buf2[pl.ds(HALO - (CONV_D_TAPS - 1) + j, tm), :]
        cat_ref[:, 0:512] = (c2 * _sigmoid(c2)).astype(BF16)
        cat_ref[:, 512:1024] = (gb * y).astype(BF16)
        y_ref[...] = y.astype(BF16)

    half = pl.BlockSpec((tm, 512), lambda i: (i, 0))
    vec = pl.BlockSpec((1, 512), lambda i: (0, 0))
    return pl.pallas_call(
        body, name="cd_fwd", grid=(T // tm,),
        in_specs=[pl.BlockSpec((tm, CD_IN), lambda i: (i, 0)),
                  pl.BlockSpec((HALO, CD_IN), lambda i: (jnp.maximum(i * per - 1, 0), 0)),
                  pl.BlockSpec((8 * 32, 512), lambda i: (0, 0)), vec, vec, vec, pl.BlockSpec((8, 512), lambda i: (0, 0))],
        out_specs=[pl.BlockSpec((tm, D), lambda i: (i, 0)), half, half, half, half],
        out_shape=[_sds((T, D), BF16), _sds((T, 512), BF16), _sds((T, 512), F32), _sds((T, 512), BF16),
                   _sds((T, 512), BF16)],
        scratch_shapes=[pltpu.VMEM((HALO + tm, 512), F32), pltpu.VMEM((HALO + tm, 512), F32),
                        pltpu.VMEM((8, HALO + tm, 512), F32)],
        compiler_params=_cparams(1))(pcd, pcd, cw, cb, lg, lb, dw)


def _d_cat_cd(dxb, w_cd_out, c1, pcd, y, lg, lb, dep, tm=512):
    def epilogue(d, rows, side, outs):
        c1_ref, gb_ref, y_ref, lg_ref, lb_ref = side
        dc1_ref, dy3_ref, dgb_ref, dlg_ref, dlb_ref, dcb_ref = outs
        dc, ddo = d[:, 0:512], d[:, 512:1024]
        xhat, rstd = _ln_stats(c1_ref[rows, :])
        c2 = xhat * lg_ref[...] + lb_ref[...]
        sg = _sigmoid(c2)
        dc2 = dc * sg * (1.0 + c2 * (1.0 - sg))
        dlg_ref[...] += jnp.sum(dc2 * xhat, axis=0, keepdims=True)
        dlb_ref[...] += jnp.sum(dc2, axis=0, keepdims=True)
        dxh = dc2 * lg_ref[...]
        dc1 = rstd * (dxh - jnp.mean(dxh, axis=-1, keepdims=True)
                      - xhat * jnp.mean(dxh * xhat, axis=-1, keepdims=True))
        dcb_ref[...] += jnp.sum(dc1, axis=0, keepdims=True)
        dc1_ref[rows, :] = dc1
        dgb_ref[rows, :] = (ddo * y_ref[rows, :].astype(F32)).astype(BF16)
        dy3_ref[rows, :] = ddo * gb_ref[rows, :].astype(F32)

    half = pl.BlockSpec((tm, 512), lambda i: (i, 0))
    vec = pl.BlockSpec((1, 512), lambda i: (0, 0))
    return _mm_nt_rows(
        dxb, w_cd_out, "mm_d_cat_cd", epilogue, [c1, pcd, y, lg, lb],
        [half, pl.BlockSpec((tm, 512), lambda i: (i, 2)), half, vec, vec], [half, half, half, vec, vec, vec],
        [_sds((T, 512), F32), _sds((T, 512), F32), _sds((T, 512), BF16),
         _sds((1, 512), F32), _sds((1, 512), F32), _sds((1, 512), F32)], sums=(3, 4, 5), dep=dep, tm=tm)


def _cd_bwd_conv(pcd, dc1, dy3, c0, dd, dgb, cw8, dw, tm=256):
    per = tm // HALO
    nblk = T // tm
    last32 = T // HALO - 1

    def body(p_ref, dc1_ref, dc1n_ref, dy3_ref, dy3n_ref, c0_ref, dd_ref, dgb_ref, cw_ref, dw_ref,
             o_ref, dcw_ref, ddw_ref, dbuf, d3buf, sd, dc0_buf):
        i = pl.program_id(0)
        has_next = jnp.where(i < nblk - 1, 1.0, 0.0)

        @pl.when(i == 0)
        def _():
            dcw_ref[...] = jnp.zeros_like(dcw_ref)
            ddw_ref[...] = jnp.zeros_like(ddw_ref)

        dbuf[0:tm, :] = dc1_ref[...]
        dbuf[tm:, :] = dc1n_ref[...] * has_next
        d3buf[0:tm, :] = dy3_ref[...]
        d3buf[tm:, :] = dy3n_ref[...] * has_next
        _shifted_copies(dbuf, sd, tm)
        n_tiles = tm // CONV_RC

        phases = _offsets_by_phase(0)

        def dc0_rows(r, carry):
            base = pl.multiple_of(r * CONV_RC, CONV_RC)
            for c in range(4):
                lanes = slice(c * 128, (c + 1) * 128)
                acc = jnp.zeros((CONV_RC // 8, 8, 128), F32)
                for b8, offsets in phases:
                    win = _window(sd, b8, base, offsets, lanes)
                    for o in offsets:
                        j = CONV_C_TAPS - 1 - o
                        acc = acc + cw_ref[8 * j:8 * j + 8, lanes] * win[o // 8:o // 8 + CONV_RC // 8]
                dc0_buf[pl.ds(base, CONV_RC), lanes] = acc.reshape(CONV_RC, 128)
            return carry

        lax.fori_loop(0, n_tiles, dc0_rows, 0)

        for c in range(4):
            lanes = slice(c * 128, (c + 1) * 128)
            for b8, offsets in phases:
                def dw_rows(r, accs, lanes=lanes, b8=b8, offsets=offsets):
                    base = pl.multiple_of(r * CONV_RC, CONV_RC)
                    xin = c0_ref[pl.ds(base, CONV_RC), lanes].astype(F32).reshape(CONV_RC // 8, 8, 128)
                    win = _window(sd, b8, base, offsets, lanes)
                    return tuple(acc + jnp.sum(xin * win[o // 8:o // 8 + CONV_RC // 8], axis=0)
                                 for acc, o in zip(accs, offsets))

                accs = lax.fori_loop(0, n_tiles, dw_rows, tuple(jnp.zeros((8, 128), F32) for _ in offsets))
                for acc, o in zip(accs, offsets):
                    j = CONV_C_TAPS - 1 - o
                    dcw_ref[j:j + 1, lanes] += jnp.sum(acc, axis=0, keepdims=True)

        dc0 = dc0_buf[...]
        ddin = dd_ref[...].astype(F32)
        ddd = jnp.zeros((tm, 512), F32)
        for j in range(CONV_D_TAPS):
            dy_shift = d3buf[pl.ds(CONV_D_TAPS - 1 - j, tm), :]
            ddd = ddd + dw_ref[j:j + 1, :] * dy_shift
            ddw_ref[j:j + 1, :] += jnp.sum(ddin * dy_shift, axis=0, keepdims=True)

        a = p_ref[:, 0:512].astype(F32)
        gt = p_ref[:, 512:1024].astype(F32)
        gc = p_ref[:, 1536:2048].astype(F32)
        hv = p_ref[:, 2048:2560].astype(F32)
        sg = _sigmoid(gt)
        o_ref[:, 0:512] = (dc0 * sg).astype(BF16)
        o_ref[:, 512:1024] = (dc0 * a * sg * (1.0 - sg)).astype(BF16)
        o_ref[:, 1024:1536] = dgb_ref[...]
        o_ref[:, 1536:2048] = (ddd * hv).astype(BF16)
        o_ref[:, 2048:2560] = (ddd * gc).astype(BF16)

    half = pl.BlockSpec((tm, 512), lambda i: (i, 0))
    nxt = pl.BlockSpec((HALO, 512), lambda i: (jnp.minimum((i + 1) * per, last32), 0))
    full = pl.BlockSpec((tm, CD_IN), lambda i: (i, 0))
    return pl.pallas_call(
        body, name="cd_bwd_conv", grid=(nblk,),
        in_specs=[full, half, nxt, half, nxt, half, half, half,
                  pl.BlockSpec((8 * 32, 512), lambda i: (0, 0)), pl.BlockSpec((8, 512), lambda i: (0, 0))],
        out_specs=[full, pl.BlockSpec((32, 512), lambda i: (0, 0)), pl.BlockSpec((8, 512), lambda i: (0, 0))],
        out_shape=[_sds((T, CD_IN), BF16), _sds((32, 512), F32), _sds((8, 512), F32)],
        scratch_shapes=[pltpu.VMEM((tm + HALO, 512), F32), pltpu.VMEM((tm + HALO, 512), F32),
                        pltpu.VMEM((8, tm + HALO, 512), F32), pltpu.VMEM((tm, 512), F32)],
        compiler_params=_cparams(1))(pcd, dc1, dc1, dy3, dy3, c0, dd, dgb, cw8, dw)


def _local_step(x, tgt, W, fetch=None, on_grad=None):
    W = dict(W)
    if fetch is None:
        fetch = lambda stage, after: {}
    if on_grad is None:
        on_grad = lambda key, arr: None
    tabs = _rope_tables()
    qg = jnp.tile(W["q_norm_g"], (1, 2))
    kg = jnp.tile(W["k_norm_g"], (1, 2))
    bias3 = W["sgu_bias"].reshape(4, 128, 1)
    G = {}

    h0 = _rms_fwd(x, W["ab_norm_g"], "rms_fwd_ab", dep=W.get("dep_first"))
    W.update(fetch("ab_in", h0))
    pab = _mm_nt(h0, W["wt_ab_in"], "mm_ab_in", dep=W.get("dep0"))
    cat_ab = _mix_a_fwd(pab, W["sgu_norm_g"], W["sgu_norm_b"], W["sgu_w"], bias3)
    qkv, rinvs = _prep_fwd(pab, qg, kg, tabs)
    outs, lses = [], []
    for g, rate in enumerate(DIL_RATES):
        o, l = _attn_fwd(qkv[3 * g], qkv[3 * g + 1], qkv[3 * g + 2], rate, f"attn_fwd_{g}", dep=W.get(f"dep_attn{g}"))
        outs.append(o)
        lses.append(l)
        W.update(fetch(f"attn{g}", o))
    cat_ab, lse = _merge_fwd(cat_ab, outs, lses)
    W.update(fetch("ab_out", lse))
    x1, h1 = _mm_nn(cat_ab, W["w_ab_out"], "mm_ab_out", mode="rms", resid=x, gain=W["ffn_norm_g"][0:1])
    pf0, act0 = _ffn_in(h1, W["wt_ffn_in0"], "ffn_in0")
    W.update(fetch("ffn_down0", act0))
    x2, h2 = _mm_nn(act0, W["w_ffn_down0"], "mm_ffn_down0", mode="rms", resid=x1, gain=W["cd_norm_g"],
                    dep=W.get("dep_down0"))
    W.update(fetch("cd_in", h2))
    pcd = _mm_nt(h2, W["wt_cd_in"], "mm_cd_in")
    cw8 = jnp.repeat(W["conv_c_w32"], 8, axis=0)
    cat_cd, c0, c1, dd, yv = _cd_fwd(pcd, cw8, W["conv_c_b"], W["c_ln_g"], W["c_ln_b"], W["conv_d_w8"])
    x3, h3 = _mm_nn(cat_cd, W["w_cd_out"], "mm_cd_out", mode="rms", resid=x2, gain=W["ffn_norm_g"][1:2])
    pf1, act1 = _ffn_in(h3, W["wt_ffn_in1"], "ffn_in1")
    dy, dyb, loss_cols = _mm_nn(act1, W["w_ffn_down1"], "mm_ffn_down1", mode="loss", resid=x3, tgt=tgt)

    def ffn_bwd(xin, h, pf, act, dres, dresb, layer):
        G[f"w_ffn_down{layer}"] = _mm_tn(act, dresb, f"mm_g_ffn_down{layer}")
        dep = on_grad(f"w_ffn_down{layer}", G[f"w_ffn_down{layer}"])
        dpf = _ffn_dact(dresb, W[f"w_ffn_down{layer}"], pf, f"ffn_dact{layer}", dep=dep)
        G[f"wt_ffn_in{layer}"] = _mm_tn(dpf, h, f"mm_g_ffn_in{layer}")
        dep = on_grad(f"wt_ffn_in{layer}", G[f"wt_ffn_in{layer}"])
        dx, dxb, G[f"ffn_norm_g{layer}"] = _mm_dh_rms_bwd(
            dpf, W[f"wt_ffn_in{layer}"], xin, W["ffn_norm_g"][layer:layer + 1], dres, f"mm_d_h_ffn{layer}", dep=dep)
        return dx, dxb

    dx3, dx3b = ffn_bwd(x3, h3, pf1, act1, dy, dyb, 1)

    G["w_cd_out"] = _mm_tn(cat_cd, dx3b, "mm_g_cd_out")
    dep = on_grad("w_cd_out", G["w_cd_out"])
    dc1, dy3, dgb, G["c_ln_g"], G["c_ln_b"], G["conv_c_b"] = _d_cat_cd(
        dx3b, W["w_cd_out"], c1, pcd, yv, W["c_ln_g"], W["c_ln_b"], dep)
    dpcd, G["conv_c_w32"], G["conv_d_w8"] = _cd_bwd_conv(pcd, dc1, dy3, c0, dd, dgb, cw8, W["conv_d_w8"])
    G["wt_cd_in"] = _mm_tn(dpcd, h2, "mm_g_cd_in")
    dep = on_grad("wt_cd_in", G["wt_cd_in"])
    dx2, dx2b, G["cd_norm_g"] = _mm_dh_rms_bwd(dpcd, W["wt_cd_in"], x2, W["cd_norm_g"], dx3, "mm_d_h_cd", dep=dep)

    dx1, dx1b = ffn_bwd(x1, h1, pf0, act0, dx2, dx2b, 0)

    G["w_ab_out"] = _mm_tn(cat_ab, dx1b, "mm_g_ab_out")
    dep = on_grad("w_ab_out", G["w_ab_out"])
    dcat_a, dbp, e = _d_cat_ab(dx1b, W["w_ab_out"], cat_ab, dep)
    dqkv = []
    for g, rate in enumerate(DIL_RATES):
        dqkv += _attn_bwd(qkv[3 * g], qkv[3 * g + 1], qkv[3 * g + 2], dbp, e, lse, rate, f"attn_bwd_{g}")
    dpab, G["sgu_w"], dbias_part, G["sgu_norm_g"], G["sgu_norm_b"], dgain = _ab_in_bwd(
        pab, dcat_a, W["sgu_norm_g"], W["sgu_norm_b"], W["sgu_w"], bias3, qg, kg, tabs, dqkv, rinvs)
    G["sgu_bias"] = jnp.sum(dbias_part, axis=-1)
    dgain = dgain[0:6, 0:HEAD] + dgain[0:6, HEAD:PAIR]
    G["q_norm_g"] = dgain[0::2]
    G["k_norm_g"] = dgain[1::2]
    G["loss_cols"] = loss_cols
    dep = on_grad("small", G)
    G["wt_ab_in"] = _mm_tn(dpab, h0, "mm_g_ab_in", dep=dep)
    dep = on_grad("wt_ab_in", G["wt_ab_in"])
    grad_x, G["ab_norm_g"] = _mm_dh_rms_bwd(dpab, W["wt_ab_in"], x, W["ab_norm_g"], dx1, "mm_d_h_ab", dep=dep,
                                            bf16_copy=False)
    return loss_cols, grad_x, G


def _my_place():
    return lax.axis_index("x"), lax.axis_index("y"), lax.axis_index("c")


def _dev_index(px, py, pc):
    return 4 * px + 2 * py + pc


def _flip(place, k):
    x, y, c = place
    return (1 - x if k & 4 else x, 1 - y if k & 2 else y, 1 - c if k & 1 else c)


def _landing(shape, dtype, own):
    buf = lax.empty(shape, dtype)
    for lead, part in own:
        buf = lax.dynamic_update_slice(buf, part.reshape((1,) * len(lead) + part.shape),
                                       tuple(lead) + (0,) * part.ndim)
    return buf


HBM_ONLY = pl.BlockSpec(memory_space=pltpu.HBM)
SEM_SPEC = pl.BlockSpec(memory_space=pltpu.SEMAPHORE)
IN_FLIGHT = pltpu.CompilerParams(has_side_effects=pltpu.SideEffectType.DATAFLOW_SIDE_EFFECTING)


def _in_hbm(a):
    return pltpu.with_memory_space_constraint(a, pltpu.HBM)


def _exchange_start(name, srcs, lands, items, dep=None):
    ns, nl, ni = len(srcs), len(lands), len(items)

    def body(*refs):
        S, L = refs[0:ns], refs[ns:ns + nl]
        first_out = ns + nl + (0 if dep is None else 1)
        send_sems, recv_sems, token = refs[first_out], refs[first_out + 1], refs[-1]
        me = _my_place()
        mi = _dev_index(*me)
        for i, (src, dst) in enumerate(items):
            for k in range(1, NDEV):
                peer = _flip(me, k)
                pltpu.make_async_remote_copy(
                    src_ref=src(S, _dev_index(*peer)), dst_ref=dst(L, mi), send_sem=send_sems.at[7 * i + k - 1],
                    recv_sem=recv_sems.at[7 * i + k - 1], device_id=peer, device_id_type=MESH).start()
        token[...] = jnp.zeros_like(token)

    thru = [pltpu.HBM(a.shape, a.dtype) for a in list(srcs) + list(lands)]
    args = [_in_hbm(a) for a in srcs] + [_in_hbm(a) for a in lands]
    in_specs = [HBM_ONLY] * (ns + nl)
    if dep is not None:
        args.append(dep)
        in_specs.append(HBM_SPEC)
    outs = pl.pallas_call(
        body, name=name, in_specs=in_specs,
        out_shape=(pltpu.SemaphoreType.DMA((7 * ni,)), pltpu.SemaphoreType.DMA((7 * ni,)), *thru, _sds((8, 128), F32)),
        out_specs=(SEM_SPEC, SEM_SPEC, *[HBM_ONLY] * (ns + nl), pl.BlockSpec(memory_space=pltpu.VMEM)),
        input_output_aliases={j: 2 + j for j in range(ns + nl)}, compiler_params=IN_FLIGHT)(*args)
    return dict(send=outs[0], recv=outs[1], srcs=list(outs[2:2 + ns]), lands=list(outs[2 + ns:2 + ns + nl]),
                token=outs[-1], items=items)


def _exchange_wait(name, states, after):
    after = list(after) if isinstance(after, (list, tuple)) else [after]
    counts = [(len(st["srcs"]), len(st["lands"]), len(st["items"])) for st in states]
    n_arrays = sum(c[0] + c[1] for c in counts)

    def body(*refs):
        me = _my_place()
        mi = _dev_index(*me)
        pos = 0
        sem_pos = n_arrays
        for st, (ns, nl, ni) in zip(states, counts):
            S, L = refs[pos:pos + ns], refs[pos + ns:pos + ns + nl]
            send_sems, recv_sems = refs[sem_pos], refs[sem_pos + 1]
            pos += ns + nl
            sem_pos += 2
            for i, (src, dst) in enumerate(st["items"]):
                for k in range(1, NDEV):
                    cp = pltpu.make_async_remote_copy(
                        src_ref=src(S, mi), dst_ref=dst(L, mi), send_sem=send_sems.at[7 * i + k - 1],
                        recv_sem=recv_sems.at[7 * i + k - 1], device_id=me, device_id_type=MESH)
                    cp.wait_send()
                    cp.wait_recv()

    arrays, sems = [], []
    for st in states:
        arrays += st["srcs"] + st["lands"]
        sems += [st["send"], st["recv"]]
    outs = pl.pallas_call(
        body, name=name, in_specs=[HBM_ONLY] * n_arrays + [SEM_SPEC] * len(sems) + [HBM_SPEC] * len(after),
        out_shape=tuple(pltpu.HBM(a.shape, a.dtype) for a in arrays), out_specs=tuple([HBM_ONLY] * n_arrays),
        input_output_aliases={j: j for j in range(n_arrays)}, compiler_params=IN_FLIGHT)(*arrays, *sems, *after)
    lands, pos = [], 0
    for ns, nl, _ in counts:
        lands.append(list(outs[pos + ns:pos + ns + nl]))
        pos += ns + nl
    return lands


def _place_and_neighbours():
    x, y, c = _my_place()
    return (x, y, c), (x, y, 1 - c), [(1 - x, y), (x, 1 - y), (1 - x, 1 - y)]


def _gather_start(name, srcs, lands, items, dep=None):
    ns, nl, ni = len(srcs), len(lands), len(items)

    def body(*refs):
        S, L = refs[0:ns], refs[ns:ns + nl]
        first_out = ns + nl + (0 if dep is None else 1)
        send_sems, recv_sems, token = refs[first_out], refs[first_out + 1], refs[-1]
        me, sib, chips = _place_and_neighbours()
        mi = _dev_index(*me)
        for i, (src, dst) in enumerate(items):
            for k, to in enumerate([sib] + [(*chip, me[2]) for chip in chips]):
                pltpu.make_async_remote_copy(
                    src_ref=src(S), dst_ref=dst(L, mi), send_sem=send_sems.at[4 * i + k],
                    recv_sem=recv_sems.at[4 * i + k], device_id=to, device_id_type=MESH).start()
        token[...] = jnp.zeros_like(token)

    thru = [pltpu.HBM(a.shape, a.dtype) for a in list(srcs) + list(lands)]
    args = [_in_hbm(a) for a in srcs] + [_in_hbm(a) for a in lands]
    in_specs = [HBM_ONLY] * (ns + nl)
    if dep is not None:
        args.append(dep)
        in_specs.append(HBM_SPEC)
    outs = pl.pallas_call(
        body, name=name, in_specs=in_specs,
        out_shape=(pltpu.SemaphoreType.DMA((4 * ni,)), pltpu.SemaphoreType.DMA((4 * ni,)), *thru, _sds((8, 128), F32)),
        out_specs=(SEM_SPEC, SEM_SPEC, *[HBM_ONLY] * (ns + nl), pl.BlockSpec(memory_space=pltpu.VMEM)),
        input_output_aliases={j: 2 + j for j in range(ns + nl)}, compiler_params=IN_FLIGHT)(*args)
    return dict(send=outs[0], recv=outs[1], srcs=list(outs[2:2 + ns]), lands=list(outs[2 + ns:2 + ns + nl]),
                token=outs[-1], items=items)


def _gather_forward(name, st, after):
    nl, ni = len(st["lands"]), len(st["items"])

    def body(*refs):
        L, recv_sems = refs[0:nl], refs[nl]
        fwd_send, fwd_recv, token = refs[-3:]
        me, sib, chips = _place_and_neighbours()
        for i, (_, dst) in enumerate(st["items"]):
            for j, chip in enumerate(chips):
                blk = dst(L, _dev_index(*chip, me[2]))
                pltpu.make_async_remote_copy(
                    src_ref=blk, dst_ref=blk, send_sem=fwd_send.at[3 * i + j], recv_sem=recv_sems.at[4 * i + 1 + j],
                    device_id=me, device_id_type=MESH).wait_recv()
                pltpu.make_async_remote_copy(
                    src_ref=blk, dst_ref=blk, send_sem=fwd_send.at[3 * i + j], recv_sem=fwd_recv.at[3 * i + j],
                    device_id=sib, device_id_type=MESH).start()
        token[...] = jnp.zeros_like(token)

    after = list(after) if isinstance(after, (list, tuple)) else [after]
    outs = pl.pallas_call(
        body, name=name, in_specs=[HBM_ONLY] * nl + [SEM_SPEC] + [HBM_SPEC] * len(after),
        out_shape=(*[pltpu.HBM(a.shape, a.dtype) for a in st["lands"]], pltpu.SemaphoreType.DMA((3 * ni,)),
                   pltpu.SemaphoreType.DMA((3 * ni,)), _sds((8, 128), F32)),
        out_specs=(*[HBM_ONLY] * nl, SEM_SPEC, SEM_SPEC, pl.BlockSpec(memory_space=pltpu.VMEM)),
        input_output_aliases={j: j for j in range(nl)}, compiler_params=IN_FLIGHT)(*st["lands"], st["recv"], *after)
    return dict(st, lands=list(outs[0:nl]), fwd_send=outs[nl], fwd_recv=outs[nl + 1], token=outs[-1])


def _gather_wait(name, st, after):
    ns, nl, ni = len(st["srcs"]), len(st["lands"]), len(st["items"])

    def body(*refs):
        S, L = refs[0:ns], refs[ns:ns + nl]
        send_sems, recv_sems, fwd_send, fwd_recv = refs[ns + nl:ns + nl + 4]
        me, sib, chips = _place_and_neighbours()
        mi = _dev_index(*me)
        for i, (src, dst) in enumerate(st["items"]):
            mine = dst(L, mi)
            for k in range(4):
                pltpu.make_async_remote_copy(
                    src_ref=src(S), dst_ref=mine, send_sem=send_sems.at[4 * i + k], recv_sem=recv_sems.at[4 * i + k],
                    device_id=me, device_id_type=MESH).wait_send()
            pltpu.make_async_remote_copy(
                src_ref=src(S), dst_ref=mine, send_sem=send_sems.at[4 * i], recv_sem=recv_sems.at[4 * i],
                device_id=me, device_id_type=MESH).wait_recv()
            for j in range(3):
                cp = pltpu.make_async_remote_copy(
                    src_ref=mine, dst_ref=mine, send_sem=fwd_send.at[3 * i + j], recv_sem=fwd_recv.at[3 * i + j],
                    device_id=me, device_id_type=MESH)
                cp.wait_send()
                cp.wait_recv()

    arrays = st["srcs"] + st["lands"]
    outs = pl.pallas_call(
        body, name=name, in_specs=[HBM_ONLY] * (ns + nl) + [SEM_SPEC] * 4 + [HBM_SPEC],
        out_shape=tuple(pltpu.HBM(a.shape, a.dtype) for a in arrays), out_specs=tuple([HBM_ONLY] * (ns + nl)),
        input_output_aliases={j: j for j in range(ns + nl)},
        compiler_params=IN_FLIGHT)(*arrays, st["send"], st["recv"], st["fwd_send"], st["fwd_recv"], after)
    return list(outs[ns:ns + nl])


def _sum_slots(land):
    def body(l_ref, o_ref):
        acc = l_ref[0]
        for d in range(1, NDEV):
            acc = acc + l_ref[d]
        o_ref[...] = acc

    vm = pl.BlockSpec(memory_space=pltpu.VMEM)
    return pl.pallas_call(body, name="sum_small", out_shape=_sds(land.shape[1:], F32), in_specs=[vm], out_specs=vm)(land)


def _adam_math(w, g, m, v):
    m2 = ADAM_B1 * m + (1.0 - ADAM_B1) * g
    v2 = ADAM_B2 * v + (1.0 - ADAM_B2) * (g * g)
    delta = -ADAM_LR * ((m2 * ADAM_C1) / (jnp.sqrt(v2 * ADAM_C2) + ADAM_EPS) + ADAM_WD * w)
    return delta, m2, v2


def _adam_layer(land, sel, w, m, v, layer, name, prev=None, tc=512):
    R = land.shape[2]

    def body(l_ref, w_ref, m_ref, v_ref, *rest):
        g_out, d_out, m_out, v_out = rest[-4:]
        g = l_ref[0].astype(F32)
        for d in range(1, NDEV):
            g = g + l_ref[d].astype(F32)
        delta, m2, v2 = _adam_math(w_ref[...], g, m_ref[...], v_ref[...])
        g_out[...] = g
        d_out[...] = delta
        m_out[...] = m2
        v_out[...] = v2

    wspec = pl.BlockSpec((None, R, tc), lambda i: (layer, 0, i))
    in_specs = [pl.BlockSpec((None, NDEV, R, tc), lambda i: (sel, 0, 0, i)), wspec, wspec, wspec]
    args = [land, w, m, v]
    aliases = {}
    if prev is not None:
        in_specs += [HBM_SPEC] * 4
        args += list(prev)
        aliases = {4 + j: j for j in range(4)}
    return pl.pallas_call(
        body, name=name, grid=(D // tc,), in_specs=in_specs, out_specs=[wspec] * 4,
        out_shape=[_sds(w.shape, F32)] * 4, input_output_aliases=aliases, compiler_params=_cparams(1))(*args)


def _adam_stacked(lands, sel, w, m, v, name):
    res = None
    for layer, land in enumerate(lands):
        res = _adam_layer(land, sel, w, m, v, layer, f"{name}{layer}", prev=res)
    return res


def _adam_small(ws, gs, ms, vs):
    n = len(ws)

    def body(*refs):
        w_r, g_r, m_r, v_r = refs[0:n], refs[n:2 * n], refs[2 * n:3 * n], refs[3 * n:4 * n]
        d_o, m_o, v_o = refs[4 * n:5 * n], refs[5 * n:6 * n], refs[6 * n:7 * n]
        for i in range(n):
            delta, m2, v2 = _adam_math(w_r[i][...], g_r[i][...], m_r[i][...], v_r[i][...])
            d_o[i][...] = delta
            m_o[i][...] = m2
            v_o[i][...] = v2

    vm = pl.BlockSpec(memory_space=pltpu.VMEM)
    shapes = [_sds(w.shape, F32) for w in ws]
    outs = pl.pallas_call(body, name="adam_small", in_specs=[vm] * (4 * n), out_specs=[vm] * (3 * n),
                          out_shape=shapes * 3)(*ws, *gs, *ms, *vs)
    return outs[0:n], outs[n:2 * n], outs[2 * n:3 * n]


def _adam_of_slots(land, w, m, v, name):
    def body(l_ref, w_ref, m_ref, v_ref, g_o, d_o, m_o, v_o):
        g = l_ref[0]
        for d in range(1, NDEV):
            g = g + l_ref[d]
        g_o[...] = g
        d_o[...], m_o[...], v_o[...] = _adam_math(w_ref[...], g, m_ref[...], v_ref[...])

    vm = pl.BlockSpec(memory_space=pltpu.VMEM)
    return pl.pallas_call(body, name=name, in_specs=[vm] * 4, out_specs=[vm] * 4,
                          out_shape=[_sds(w.shape, F32)] * 4)(land, w, m, v)


WEIGHT_NAMES = ("ab_norm_g", "ab_w_in", "sgu_norm_g", "sgu_norm_b", "sgu_w", "sgu_bias", "q_norm_g", "k_norm_g",
                "ab_w_out", "cd_norm_g", "cd_w_in", "conv_c_w", "conv_c_b", "c_ln_g", "c_ln_b", "conv_d_w",
                "cd_w_out", "ffn_norm_g", "ffn_w_gate", "ffn_w_up", "ffn_w_down")
SMALL_SHAPES = (("sgu_norm_g", (1, 512)), ("sgu_norm_b", (1, 512)), ("sgu_w", (512, 128)),
                ("sgu_bias", (4, 128)), ("q_norm_g", (3, 1, 64)), ("k_norm_g", (3, 1, 64)), ("cd_norm_g", (1, 128)),
                ("conv_c_w", (31, 1, 64)), ("conv_c_b", (1, 64)), ("c_ln_g", (1, 64)), ("c_ln_b", (1, 64)),
                ("conv_d_w", (3, 1, 64)), ("ffn_norm_g", (2, 1024)))
SHARD_C = 64


def _pack_rows(parts, rows):
    flat = jnp.concatenate([p.reshape(-1) for p in parts])
    return jnp.pad(flat, (0, rows * 128 - flat.shape[0])).reshape(rows, 128)


def kernel(x, ab_norm_g, ab_w_in, sgu_norm_g, sgu_norm_b, sgu_w, sgu_bias, q_norm_g, k_norm_g, ab_w_out, cd_norm_g, cd_w_in, conv_c_w, conv_c_b, c_ln_g, c_ln_b, conv_d_w, cd_w_out, ffn_norm_g, ffn_w_gate, ffn_w_up, ffn_w_down, loss_target, m_ab_norm_g, m_ab_w_in, m_sgu_norm_g, m_sgu_norm_b, m_sgu_w, m_sgu_bias, m_q_norm_g, m_k_norm_g, m_ab_w_out, m_cd_norm_g, m_cd_w_in, m_conv_c_w, m_conv_c_b, m_c_ln_g, m_c_ln_b, m_conv_d_w, m_cd_w_out, m_ffn_norm_g, m_ffn_w_gate, m_ffn_w_up, m_ffn_w_down, v_ab_norm_g, v_ab_w_in, v_sgu_norm_g, v_sgu_norm_b, v_sgu_w, v_sgu_bias, v_q_norm_g, v_k_norm_g, v_ab_w_out, v_cd_norm_g, v_cd_w_in, v_conv_c_w, v_conv_c_b, v_c_ln_g, v_c_ln_b, v_conv_d_w, v_cd_w_out, v_ffn_norm_g, v_ffn_w_gate, v_ffn_w_up, v_ffn_w_down):
    w = dict(zip(WEIGHT_NAMES, (ab_norm_g, ab_w_in, sgu_norm_g, sgu_norm_b, sgu_w, sgu_bias, q_norm_g, k_norm_g, ab_w_out, cd_norm_g, cd_w_in, conv_c_w, conv_c_b, c_ln_g, c_ln_b, conv_d_w, cd_w_out, ffn_norm_g, ffn_w_gate, ffn_w_up, ffn_w_down)))
    m = dict(zip(WEIGHT_NAMES, (m_ab_norm_g, m_ab_w_in, m_sgu_norm_g, m_sgu_norm_b, m_sgu_w, m_sgu_bias, m_q_norm_g, m_k_norm_g, m_ab_w_out, m_cd_norm_g, m_cd_w_in, m_conv_c_w, m_conv_c_b, m_c_ln_g, m_c_ln_b, m_conv_d_w, m_cd_w_out, m_ffn_norm_g, m_ffn_w_gate, m_ffn_w_up, m_ffn_w_down)))
    v = dict(zip(WEIGHT_NAMES, (v_ab_norm_g, v_ab_w_in, v_sgu_norm_g, v_sgu_norm_b, v_sgu_w, v_sgu_bias, v_q_norm_g, v_k_norm_g, v_ab_w_out, v_cd_norm_g, v_cd_w_in, v_conv_c_w, v_conv_c_b, v_c_ln_g, v_c_ln_b, v_conv_d_w, v_cd_w_out, v_ffn_norm_g, v_ffn_w_gate, v_ffn_w_up, v_ffn_w_down)))
    me = _dev_index(*_my_place())

    r_ff = DFF // NDEV
    one = lambda a: (lambda S, j: S[a])
    slot = lambda b: (lambda L, s: L[b].at[s])
    slot2 = lambda b, part: (lambda L, s: L[b].at[part, s])
    shard = lambda a: (lambda S: S[a])

    def later(a):
        return lax.optimization_barrier((a, gathers[0]["token"]))[0]

    def layer_shards(layer):
        return (later(w["ffn_w_gate"][layer]).T.astype(BF16), later(w["ffn_w_up"][layer]).T.astype(BF16),
                later(w["ffn_w_down"][layer]).astype(BF16))

    def gathered(own):
        return _landing((NDEV,) + own.shape, BF16, [((me,), own)])

    def gathered2(a, b):
        return _landing((2, NDEV) + a.shape, BF16, [((0, me), a), ((1, me), b)])

    ab_in_s = w["ab_w_in"][0].T.astype(BF16)
    gathers = {0: _gather_start("gather0_start", [ab_in_s], [gathered(ab_in_s)], [(shard(0), slot(0))])}

    def chan(flat, lo, taps):
        return flat[:, lo:lo + taps * SHARD_C].reshape(NDEV, taps, SHARD_C).transpose(1, 0, 2).reshape(taps, 512)

    def fetch(stage, after):
        if stage == "ab_in":
            ab_out_s = later(w["ab_w_out"][0]).astype(BF16)
            gate0, up0, down0 = layer_shards(0)
            small_s = _pack_rows([later(w[n]) for n in ("cd_norm_g", "conv_c_w", "conv_c_b", "c_ln_g", "c_ln_b",
                                                        "conv_d_w")], 24)
            lands1 = [gathered(ab_out_s), gathered2(gate0, up0), gathered(down0),
                      _landing((NDEV,) + small_s.shape, F32, [((me,), small_s)])]
            gathers[0] = _gather_forward("gather0_forward", gathers[0], [after] + lands1)
            l_ab_in, = _gather_wait("gather0_wait", gathers[0], gathers[0]["token"])
            gathers[1] = _gather_start(
                "gather1_start", [ab_out_s, gate0, up0, down0, small_s], lands1,
                [(shard(0), slot(0)), (shard(1), slot2(1, 0)), (shard(2), slot2(1, 1)), (shard(3), slot(2)),
                 (shard(4), slot(3))], dep=l_ab_in)
            return {"wt_ab_in": l_ab_in.reshape(AB_IN, D), "dep0": gathers[1]["token"]}
        if stage == "attn0":
            cd_in_s, cd_out_s = later(w["cd_w_in"][0]).T.astype(BF16), later(w["cd_w_out"][0]).astype(BF16)
            gate1, up1, down1 = layer_shards(1)
            gathers[2] = _gather_start(
                "gather2_start", [cd_in_s, cd_out_s, gate1, up1, down1],
                [gathered(cd_in_s), gathered(cd_out_s), gathered2(gate1, up1), gathered(down1)],
                [(shard(0), slot(0)), (shard(1), slot(1)), (shard(2), slot2(2, 0)), (shard(3), slot2(2, 1)),
                 (shard(4), slot(3))], dep=after)
            return {"dep_attn1": gathers[2]["token"]}
        if stage == "attn1":
            gathers[1] = _gather_forward("gather1_forward", gathers[1], after)
            return {"dep_attn2": gathers[1]["token"]}
        if stage == "ab_out":
            l_out, l_ffn, l_down, l_small = _gather_wait("gather1_wait", gathers[1], after)
            flat = l_small.reshape(NDEV, 24 * 128)
            return {
                "w_ab_out": l_out.reshape(D, D), "wt_ffn_in0": l_ffn.reshape(2 * DFF, D),
                "w_ffn_down0": l_down.reshape(DFF, D), "cd_norm_g": flat[:, 0:128].reshape(1, D),
                "conv_c_w32": jnp.pad(chan(flat, 128, CONV_C_TAPS), ((0, 1), (0, 0))),
                "conv_c_b": chan(flat, 2112, 1), "c_ln_g": chan(flat, 2176, 1), "c_ln_b": chan(flat, 2240, 1),
                "conv_d_w8": jnp.pad(chan(flat, 2304, CONV_D_TAPS), ((0, 8 - CONV_D_TAPS), (0, 0))),
            }
        if stage == "ffn_down0":
            gathers[2] = _gather_forward("gather2_forward", gathers[2], after)
            return {"dep_down0": gathers[2]["token"]}
        if stage == "cd_in":
            l_in, l_out, l_ffn, l_down = _gather_wait("gather2_wait", gathers[2], after)
            return {"wt_cd_in": l_in.reshape(CD_IN, D), "w_cd_out": l_out.reshape(D, D),
                    "wt_ffn_in1": l_ffn.reshape(2 * DFF, D), "w_ffn_down1": l_down.reshape(DFF, D)}
        return {}

    scatters = {}
    rides_with = {"w_ffn_down1": "wt_ffn_in1", "w_cd_out": "wt_cd_in", "w_ffn_down0": "wt_ffn_in0"}
    held = {}
    smalls = {}

    def small_exchange(name, block):
        land = _landing((NDEV,) + block.shape, F32, [((me,), block)])
        return _exchange_start(name, [block], [land], [(one(0), slot(0))])

    def on_grad(key, arr):
        if key == "small":
            parts = [arr["sgu_norm_g"], arr["sgu_norm_b"], arr["sgu_w"], arr["sgu_bias"], arr["q_norm_g"],
                     arr["k_norm_g"], arr["cd_norm_g"], arr["conv_c_w32"][:CONV_C_TAPS], arr["conv_c_b"], arr["c_ln_g"],
                     arr["c_ln_b"], arr["conv_d_w8"][:CONV_D_TAPS], arr["ffn_norm_g0"], arr["ffn_norm_g1"],
                     arr["loss_cols"]]
            smalls["sizes"] = [p.size for p in parts]
            rows = -(-sum(smalls["sizes"]) // 1024) * 8
            smalls["early"] = small_exchange("small_start", _pack_rows(parts, rows))
            return smalls["early"]["token"]
        if key in rides_with:
            held[rides_with[key]] = (key, arr)
            return None
        group = ([held.pop(key)] if key in held else []) + [(key, arr)]
        srcs, lands, items = [], [], []
        for n, (k, a) in enumerate(group):
            if k.startswith("wt_ffn_in"):
                src = a.reshape(2, NDEV, r_ff, D)
                own = lax.dynamic_slice_in_dim(src, me, 1, axis=1)
                lands.append(lax.dynamic_update_slice(lax.empty(src.shape, BF16), own, (0, me, 0, 0)))
                items += [((lambda S, j, n=n: S[n].at[0, j]), slot2(n, 0)), ((lambda S, j, n=n: S[n].at[1, j]), slot2(n, 1))]
            else:
                rows = a.shape[0] // NDEV
                src = a.reshape(NDEV, rows, D)
                own = lax.dynamic_index_in_dim(src, me, 0, keepdims=False)
                lands.append(_landing((1, NDEV, rows, D), BF16, [((0, me), own)]))
                items.append(((lambda S, j, n=n: S[n].at[j]), slot2(n, 0)))
            srcs.append(src)
        st = _exchange_start(f"scatter_{key}_start", srcs, lands, items)
        scatters[key] = (st, [k for k, _ in group])
        return st["token"]

    W = {
        "dep_first": gathers[0]["token"],
        "ab_norm_g": w["ab_norm_g"], "sgu_norm_g": w["sgu_norm_g"], "sgu_norm_b": w["sgu_norm_b"],
        "sgu_w": w["sgu_w"][0], "sgu_bias": w["sgu_bias"][0], "q_norm_g": w["q_norm_g"][0],
        "k_norm_g": w["k_norm_g"][0], "ffn_norm_g": w["ffn_norm_g"],
    }

    loss_cols, grad_x, G = _local_step(x[0], loss_target[0], W, fetch, on_grad)

    late_small = small_exchange("small_late_start", G["ab_norm_g"])
    landed = {}

    def wait_scatters(name, group_keys, others, after):
        res = _exchange_wait(name, [scatters[gk][0] for gk in group_keys] + others, after)
        for gk, lands in zip(group_keys, res):
            landed.update(zip(scatters[gk][1], lands))
        return [lands[0] for lands in res[len(group_keys):]]

    small_land, = wait_scatters("scatter_wait_early", ["wt_ffn_in1", "wt_cd_in", "wt_ffn_in0", "w_ab_out"],
                                [smalls["early"]], late_small["token"])

    grads, deltas, new_m, new_v = {}, {}, {}, {}
    done = []

    def put(name, res):
        grads[name], deltas[name], new_m[name], new_v[name] = res

    def adam(name, lands, sel, transposed):
        flip = (lambda a: jnp.swapaxes(a, 1, 2)) if transposed else (lambda a: a)
        res = _adam_stacked(lands, sel, flip(w[name]), flip(m[name]), flip(v[name]), f"adam_{name}")
        done.append(res[1])
        put(name, [flip(r) for r in res])

    ffn_in_lands = [landed["wt_ffn_in0"], landed["wt_ffn_in1"]]
    adam("cd_w_in", [landed["wt_cd_in"]], 0, True)
    adam("ffn_w_gate", ffn_in_lands, 0, True)
    adam("ffn_w_up", ffn_in_lands, 1, True)
    adam("cd_w_out", [landed["w_cd_out"]], 0, False)
    adam("ab_w_out", [landed["w_ab_out"]], 0, False)
    adam("ffn_w_down", [landed["w_ffn_down0"], landed["w_ffn_down1"]], 0, False)

    red = _sum_slots(small_land).reshape(-1)
    offs = [0]
    for s in smalls["sizes"]:
        offs.append(offs[-1] + s)
    seg = [red[offs[i]:offs[i + 1]] for i in range(len(smalls["sizes"]))]
    loss = jnp.sum(seg[14])

    def own_channels(full, taps):
        return lax.dynamic_slice_in_dim(full.reshape(taps, 512), me * SHARD_C, SHARD_C, axis=1)

    g_small = {
        "sgu_norm_g": seg[0].reshape(1, 512), "sgu_norm_b": seg[1].reshape(1, 512),
        "sgu_w": seg[2].reshape(512, 128), "sgu_bias": seg[3].reshape(4, 128), "q_norm_g": seg[4].reshape(3, 64),
        "k_norm_g": seg[5].reshape(3, 64),
        "cd_norm_g": lax.dynamic_slice_in_dim(seg[6].reshape(1, D), me * (D // NDEV), D // NDEV, axis=1),
        "conv_c_w": own_channels(seg[7], CONV_C_TAPS), "conv_c_b": own_channels(seg[8], 1),
        "c_ln_g": own_channels(seg[9], 1), "c_ln_b": own_channels(seg[10], 1),
        "conv_d_w": own_channels(seg[11], CONV_D_TAPS),
        "ffn_norm_g": jnp.concatenate([seg[12].reshape(1, D), seg[13].reshape(1, D)], axis=0),
    }

    def small_in(s, a):
        return jnp.swapaxes(a, 0, 1) if len(s) == 3 else a.reshape(s)

    def small_out(n, s, a):
        return jnp.swapaxes(a, 0, 1) if len(s) == 3 else a.reshape(w[n].shape)

    g_in = [g_small[n].reshape(s) for n, s in SMALL_SHAPES]
    d_s, m_s, v_s = _adam_small([small_in(s, w[n]) for n, s in SMALL_SHAPES], g_in,
                                [small_in(s, m[n]) for n, s in SMALL_SHAPES],
                                [small_in(s, v[n]) for n, s in SMALL_SHAPES])
    for i, (n, s) in enumerate(SMALL_SHAPES):
        grads[n], deltas[n] = small_out(n, s, g_in[i]), small_out(n, s, d_s[i])
        new_m[n], new_v[n] = small_out(n, s, m_s[i]), small_out(n, s, v_s[i])
    done.append(d_s[0])

    late_land, = wait_scatters("scatter_wait_last", ["wt_ab_in"], [late_small], list(done))
    put("ab_norm_g", _adam_of_slots(late_land, w["ab_norm_g"], m["ab_norm_g"], v["ab_norm_g"], "adam_ab_norm_g"))
    adam("ab_w_in", [landed["wt_ab_in"]], 0, True)

    return (loss, grad_x[None], *[grads[n] for n in WEIGHT_NAMES], *[deltas[n] for n in WEIGHT_NAMES],
            *[new_m[n] for n in WEIGHT_NAMES], *[new_v[n] for n in WEIGHT_NAMES])
```

```python
import jax
import jax.numpy as jnp
import numpy as np
from jax import lax
from jax.experimental import pallas as pl
from jax.experimental.pallas import tpu as pltpu

F32 = jnp.float32
BF16 = jnp.bfloat16

T = 4096
D = 1024
NDEV = 8
EPS = 1e-6
NEG_INF = -1e30
DFF = 2816
AB_IN = 5632
CD_IN = 2560
HEAD = 64
PAIR = 128
NPAIR = 4
NBACK = 128
DIL_RATES = (1, 4, 16)
ROPE_HALF = 8
ROPE_THETA = 500000.0
CONV_C_TAPS = 31
CONV_D_TAPS = 3
HALO = 32
ATTN_BWD_UNROLL = 4
MAX_ROW_STRIDE = 4

ADAM_LR = 0.001
ADAM_B1 = 0.9
ADAM_B2 = 0.999
ADAM_EPS = 1e-08
ADAM_WD = 0.01
ADAM_STEP = 10
ADAM_C1 = 1.0 / (1.0 - ADAM_B1 ** ADAM_STEP)
ADAM_C2 = 1.0 / (1.0 - ADAM_B2 ** ADAM_STEP)

VMEM_LIMIT_MB = 48
MESH = pl.DeviceIdType.MESH
HBM_SPEC = pl.BlockSpec(memory_space=pl.ANY)


def _cparams(ngrid, vmem_mb=VMEM_LIMIT_MB):
    return pltpu.CompilerParams(dimension_semantics=("arbitrary",) * ngrid,
                                vmem_limit_bytes=vmem_mb * 1024 * 1024)


def _pick(n, options):
    for o in options:
        if n % o == 0:
            return o
    raise ValueError(f"no tile for {n} in {options}")


def _sds(shape, dtype):
    return jax.ShapeDtypeStruct(shape, dtype)


def _sigmoid(x):
    return 1.0 / (1.0 + jnp.exp(-x))


def _sigmoid_bf16(x):
    return 0.5 * jnp.tanh(0.5 * x) + 0.5


def _gelu(z):
    return 0.5 * z * (1.0 + lax.erf(z * 0.7071067811865476))


def _gelu_grad(z):
    return 0.5 * (1.0 + lax.erf(z * 0.7071067811865476)) + z * jnp.exp(-0.5 * z * z) * 0.3989422804014327


def _mm_nt(a, wt, name, out_dtype=BF16, dep=None):
    M, K = a.shape
    N = wt.shape[0]
    tn = _pick(N, (512, 256))

    def body(a_ref, w_ref, *rest):
        o_ref = rest[-1]
        for r0 in range(0, M, 1024):
            o_ref[r0:r0 + 1024, :] = lax.dot_general(
                a_ref[r0:r0 + 1024, :], w_ref[...], (((1,), (1,)), ((), ())),
                preferred_element_type=F32).astype(o_ref.dtype)

    in_specs = [pl.BlockSpec((M, K), lambda j: (0, 0), pipeline_mode=pl.Buffered(1)),
                pl.BlockSpec((tn, K), lambda j: (j, 0))]
    args = [a, wt]
    if dep is not None:
        in_specs.append(HBM_SPEC)
        args.append(dep)
    return pl.pallas_call(
        body, name=name, grid=(N // tn,), in_specs=in_specs, out_specs=pl.BlockSpec((M, tn), lambda j: (0, j)),
        out_shape=_sds((M, N), out_dtype), compiler_params=_cparams(1))(*args)


EPI_ROWS = 256


def _mm_nt_rows(a, wt, name, epilogue, side, side_specs, out_specs, out_shape, sums=(), dep=None, tm=512):
    M, K = a.shape
    N = wt.shape[0]
    ns, no = len(side), len(out_shape)

    def body(a_ref, w_ref, *rest):
        side_refs, outs, acc = rest[0:ns], rest[-1 - no:-1], rest[-1]
        acc[...] = lax.dot_general(a_ref[...], w_ref[...], (((1,), (1,)), ((), ())), preferred_element_type=F32)

        @pl.when(pl.program_id(0) == 0)
        def _():
            for j in sums:
                outs[j][...] = jnp.zeros_like(outs[j])

        for r0 in range(0, tm, EPI_ROWS):
            rows = slice(r0, r0 + EPI_ROWS)
            epilogue(acc[rows, :].astype(BF16).astype(F32), rows, side_refs, outs)

    in_specs = [pl.BlockSpec((tm, K), lambda i: (i, 0)),
                pl.BlockSpec((N, K), lambda i: (0, 0), pipeline_mode=pl.Buffered(1))] + list(side_specs)
    args = [a, wt, *side]
    if dep is not None:
        in_specs.append(HBM_SPEC)
        args.append(dep)
    return pl.pallas_call(
        body, name=name, grid=(M // tm,), in_specs=in_specs, out_specs=list(out_specs), out_shape=list(out_shape),
        scratch_shapes=[pltpu.VMEM((tm, N), F32)], compiler_params=_cparams(1))(*args)


def _mm_nn(a, w, name, mode, resid, gain=None, tgt=None, dep=None, tm=512):
    M, K = a.shape
    N = w.shape[1]
    side = gain if mode == "rms" else tgt

    def body(a_ref, w_ref, resid_ref, side_ref, *rest):
        outs, acc = rest[-3 if mode == "rms" else -4:-1], rest[-1]
        i = pl.program_id(0)
        acc[...] = jnp.dot(a_ref[...], w_ref[...], preferred_element_type=F32)

        if mode == "loss":
            @pl.when(i == 0)
            def _():
                outs[2][...] = jnp.zeros_like(outs[2])

        for r0 in range(0, tm, EPI_ROWS):
            rows = slice(r0, r0 + EPI_ROWS)
            v = acc[rows, :] + resid_ref[rows, :]
            if mode == "rms":
                outs[0][rows, :] = v
                r = lax.rsqrt(jnp.mean(v * v, axis=-1, keepdims=True) + EPS)
                outs[1][rows, :] = (v * r * side_ref[...]).astype(BF16)
            else:
                d = v - side_ref[rows, :]
                outs[2][...] += jnp.sum(d * d, axis=0, keepdims=True) * (0.5 / N)
                dy = d * (1.0 / N)
                outs[0][rows, :] = dy
                outs[1][rows, :] = dy.astype(BF16)

    row = pl.BlockSpec((tm, N), lambda i: (i, 0))
    vec = pl.BlockSpec((1, N), lambda i: (0, 0))
    in_specs = [pl.BlockSpec((tm, K), lambda i: (i, 0)),
                pl.BlockSpec((K, N), lambda i: (0, 0), pipeline_mode=pl.Buffered(1)), row,
                vec if mode == "rms" else row]
    args = [a, w, resid, side]
    if dep is not None:
        in_specs.append(HBM_SPEC)
        args.append(dep)
    if mode == "rms":
        out_specs, out_shape = [row, row], [_sds((M, N), F32), _sds((M, N), BF16)]
    else:
        out_specs, out_shape = [row, row, vec], [_sds((M, N), F32), _sds((M, N), BF16), _sds((1, N), F32)]
    return pl.pallas_call(
        body, name=name, grid=(M // tm,), in_specs=in_specs, out_specs=out_specs, out_shape=out_shape,
        scratch_shapes=[pltpu.VMEM((tm, N), F32)], compiler_params=_cparams(1))(*args)


def _mm_dh_rms_bwd(a, w, x, gain, dres, name, dep=None, tm=512, bf16_copy=True):
    parts = a.shape[0] if a.ndim == 3 else 1
    M, Kp = a.shape[-2], a.shape[-1]
    N = w.shape[1]
    nblk = M // tm
    assert nblk % 2 == 0

    def body(a_ref, w_ref, x_ref, g_ref, dres_ref, *rest):
        dg_ref, acc0, acc1 = rest[-3:]
        dx_ref = rest[-5] if bf16_copy else rest[-4]
        dxb_ref = rest[-4] if bf16_copy else None
        i = pl.program_id(0)

        def matmul(acc):
            if parts == 1:
                acc[...] = jnp.dot(a_ref[...], w_ref[...], preferred_element_type=F32)
            else:
                d = jnp.dot(a_ref[0], w_ref[0:Kp, :], preferred_element_type=F32)
                for p in range(1, parts):
                    d = d + jnp.dot(a_ref[p], w_ref[p * Kp:(p + 1) * Kp, :], preferred_element_type=F32)
                acc[...] = d

        def finish(acc):
            for r0 in range(0, tm, EPI_ROWS // 2):
                rows = slice(r0, r0 + EPI_ROWS // 2)
                v = acc[rows, :]
                xf = x_ref[rows, :]
                r = lax.rsqrt(jnp.mean(xf * xf, axis=-1, keepdims=True) + EPS)
                xhat = xf * r
                dg_ref[...] += jnp.sum(v * xhat, axis=0, keepdims=True)
                dxh = v * g_ref[...]
                tot = dres_ref[rows, :] + r * (dxh - xhat * jnp.mean(dxh * xhat, axis=-1, keepdims=True))
                dx_ref[rows, :] = tot
                if bf16_copy:
                    dxb_ref[rows, :] = tot.astype(BF16)

        @pl.when(i == 0)
        def _():
            dg_ref[...] = jnp.zeros_like(dg_ref)
            matmul(acc0)

        @pl.when((i > 0) & (i < nblk) & (i % 2 == 1))
        def _():
            matmul(acc1)
            finish(acc0)

        @pl.when((i > 0) & (i < nblk) & (i % 2 == 0))
        def _():
            matmul(acc0)
            finish(acc1)

        @pl.when(i == nblk)
        def _():
            finish(acc1)

    last = nblk - 1
    row = pl.BlockSpec((tm, N), lambda i: (jnp.maximum(i - 1, 0), 0))
    vec = pl.BlockSpec((1, N), lambda i: (0, 0))
    if a.ndim == 3:
        a_spec = pl.BlockSpec((parts, tm, Kp), lambda i: (0, jnp.minimum(i, last), 0))
    else:
        a_spec = pl.BlockSpec((tm, Kp), lambda i: (jnp.minimum(i, last), 0))
    w_spec = pl.BlockSpec((parts * Kp, N), lambda i: (0, 0), pipeline_mode=pl.Buffered(1))
    in_specs = [a_spec, w_spec, row, vec, row]
    args = [a, w, x, gain, dres]
    if dep is not None:
        in_specs.append(HBM_SPEC)
        args.append(dep)
    return pl.pallas_call(
        body, name=name, grid=(nblk + 1,), in_specs=in_specs,
        out_specs=[row, row, vec] if bf16_copy else [row, vec],
        out_shape=([_sds((M, N), F32), _sds((M, N), BF16), _sds((1, N), F32)] if bf16_copy
                   else [_sds((M, N), F32), _sds((1, N), F32)]),
        scratch_shapes=[pltpu.VMEM((tm, N), F32), pltpu.VMEM((tm, N), F32)], compiler_params=_cparams(1, 56))(*args)


def _mm_tn(a, b, name, out_dtype=BF16, tt=2048, dep=None):
    parts = a.shape[0] if a.ndim == 3 else 1
    Tt, Mp = a.shape[-2], a.shape[-1]
    N = b.shape[1]
    tn = _pick(Mp, (1408, 1280, 1024, 512))
    jper = Mp // tn
    nt = Tt // tt

    def body(a_ref, b_ref, *rest):
        o_ref, acc = rest[-2:]
        t = pl.program_id(1)

        @pl.when(t == 0)
        def _():
            acc[...] = jnp.zeros_like(acc)

        rows = pl.ds(pl.multiple_of(t * tt, tt), tt)
        acc[...] += lax.dot_general(a_ref[...], b_ref[rows, :], (((0,), (0,)), ((), ())),
                                    preferred_element_type=F32)

        @pl.when(t == nt - 1)
        def _():
            o_ref[...] = acc[...].astype(o_ref.dtype)

    if a.ndim == 3:
        a_spec = pl.BlockSpec((None, tt, tn), lambda j, t: (j // jper, t, j % jper))
    else:
        a_spec = pl.BlockSpec((tt, tn), lambda j, t: (t, j))
    in_specs = [a_spec, pl.BlockSpec((Tt, N), lambda j, t: (0, 0), pipeline_mode=pl.Buffered(1))]
    args = [a, b]
    if dep is not None:
        in_specs.append(HBM_SPEC)
        args.append(dep)
    return pl.pallas_call(
        body, name=name, grid=(parts * jper, nt), in_specs=in_specs,
        out_specs=pl.BlockSpec((tn, N), lambda j, t: (j, 0)),
        out_shape=_sds((parts * Mp, N), out_dtype), scratch_shapes=[pltpu.VMEM((tn, N), F32)],
        compiler_params=_cparams(2))(*args)


FFN_ROWS = 2048


def _ffn_in(h, wt_in, name, tn=256):
    nj = DFF // tn

    def body(h_ref, wg_ref, wu_ref, p_ref, act_ref):
        nt = (((1,), (1,)), ((), ()))
        for r0 in range(0, T, FFN_ROWS):
            rows = slice(r0, r0 + FFN_ROWS)
            g = lax.dot_general(h_ref[rows, :], wg_ref[...], nt, preferred_element_type=F32).astype(BF16)
            u = lax.dot_general(h_ref[rows, :], wu_ref[...], nt, preferred_element_type=F32).astype(BF16)
            p_ref[0, rows, :] = g
            p_ref[1, rows, :] = u
            act_ref[rows, :] = g * _sigmoid_bf16(g) * u

    return pl.pallas_call(
        body, name=name, grid=(nj,),
        in_specs=[pl.BlockSpec((T, D), lambda j: (0, 0), pipeline_mode=pl.Buffered(1)),
                  pl.BlockSpec((tn, D), lambda j: (j, 0)), pl.BlockSpec((tn, D), lambda j: (j + nj, 0))],
        out_specs=[pl.BlockSpec((2, T, tn), lambda j: (0, 0, j)), pl.BlockSpec((T, tn), lambda j: (0, j))],
        out_shape=[_sds((2, T, DFF), BF16), _sds((T, DFF), BF16)], compiler_params=_cparams(1))(h, wt_in, wt_in)


def _ffn_dact(dyb, w_down, p3, name, tn=256, dep=None):
    def body(dy_ref, w_ref, p_ref, *rest):
        o_ref = rest[-1]
        for r0 in range(0, T, FFN_ROWS):
            rows = slice(r0, r0 + FFN_ROWS)
            da = lax.dot_general(dy_ref[rows, :], w_ref[...], (((1,), (1,)), ((), ())),
                                 preferred_element_type=F32).astype(BF16)
            g = p_ref[0, rows, :]
            u = p_ref[1, rows, :]
            sg = _sigmoid_bf16(g)
            gs = g * sg
            o_ref[0, rows, :] = (da * u) * (sg + gs * (1.0 - sg))
            o_ref[1, rows, :] = da * gs

    pspec = pl.BlockSpec((2, T, tn), lambda j: (0, 0, j))
    in_specs = [pl.BlockSpec((T, D), lambda j: (0, 0), pipeline_mode=pl.Buffered(1)),
                pl.BlockSpec((tn, D), lambda j: (j, 0)), pspec]
    args = [dyb, w_down, p3]
    if dep is not None:
        in_specs.append(HBM_SPEC)
        args.append(dep)
    return pl.pallas_call(
        body, name=name, grid=(DFF // tn,), in_specs=in_specs, out_specs=pspec,
        out_shape=_sds((2, T, DFF), BF16), compiler_params=_cparams(1))(*args)


def _rms_fwd(x, g, name, tm=512, dep=None):
    def body(x_ref, g_ref, *rest):
        h_ref = rest[-1]
        xf = x_ref[...]
        r = lax.rsqrt(jnp.mean(xf * xf, axis=-1, keepdims=True) + EPS)
        h_ref[...] = (xf * r * g_ref[...]).astype(BF16)

    in_specs = [pl.BlockSpec((tm, D), lambda i: (i, 0)), pl.BlockSpec((1, D), lambda i: (0, 0))]
    args = [x, g]
    if dep is not None:
        in_specs.append(HBM_SPEC)
        args.append(dep)
    return pl.pallas_call(
        body, name=name, grid=(T // tm,), in_specs=in_specs, out_specs=pl.BlockSpec((tm, D), lambda i: (i, 0)),
        out_shape=_sds((T, D), BF16), compiler_params=_cparams(1))(*args)


def _tril_mask():
    r = lax.broadcasted_iota(jnp.int32, (128, 128), 0)
    c = lax.broadcasted_iota(jnp.int32, (128, 128), 1)
    return r >= c


def _mix_a_fwd(pab, sgu_g, sgu_b, sgu_w, sgu_bias3, tm=512):
    def body(zu_ref, zv_ref, g_ref, b_ref, w_ref, bias_ref, o_ref):
        u = _gelu(zu_ref[...].astype(F32))
        v = _gelu(zv_ref[...].astype(F32))
        mu = jnp.mean(v, axis=-1, keepdims=True)
        vc = v - mu
        rstd = lax.rsqrt(jnp.mean(vc * vc, axis=-1, keepdims=True) + EPS)
        vn = (vc * rstd * g_ref[...] + b_ref[...]).astype(BF16)
        tri = _tril_mask()
        for gi in range(4):
            wg = jnp.where(tri, w_ref[gi], 0.0).astype(BF16)
            bg = bias_ref[gi]
            for c in range(tm // 128):
                rs, cs = slice(c * 128, (c + 1) * 128), slice(gi * 128, (gi + 1) * 128)
                mixed = jnp.dot(wg, vn[rs, cs], preferred_element_type=F32) + bg
                o_ref[rs, cs] = (u[rs, cs] * mixed).astype(BF16)

    half = pl.BlockSpec((tm, 512), lambda i: (i, 0))
    return pl.pallas_call(
        body, name="mix_a_fwd", grid=(T // tm,),
        in_specs=[half, pl.BlockSpec((tm, 512), lambda i: (i, 1)),
                  pl.BlockSpec((1, 512), lambda i: (0, 0)), pl.BlockSpec((1, 512), lambda i: (0, 0)),
                  pl.BlockSpec((4, 128, 128), lambda i: (0, 0, 0)), pl.BlockSpec((4, 128, 1), lambda i: (0, 0, 0))],
        out_specs=half, out_shape=_sds((T, D), BF16), compiler_params=_cparams(1),
    )(pab, pab, sgu_g, sgu_b, sgu_w, sgu_bias3)


def _rope_tables():
    pos = np.arange(T, dtype=np.float32)
    inv_freq = np.float32(ROPE_THETA) ** (-np.arange(ROPE_HALF, dtype=np.float32) * np.float32(2.0 / (2 * ROPE_HALF)))
    ang = (pos[:, None] * inv_freq[None, :]).astype(np.float32)
    cos, sin = np.cos(ang), np.sin(ang)
    z8 = np.zeros((T, ROPE_HALF), np.float32)
    rest = np.zeros((T, HEAD - 2 * ROPE_HALF), np.float32)
    c64 = np.concatenate([cos, cos, rest + 1.0], axis=1)
    s1 = np.concatenate([z8, sin, rest], axis=1)
    s2 = np.concatenate([-sin, z8, rest], axis=1)
    return tuple(jnp.asarray(np.tile(t, (1, 2)).astype(np.float32)) for t in (c64, s1, s2))


def _lo_mask(shape):
    return lax.broadcasted_iota(jnp.int32, shape, 1) < HEAD


def _seg_mean(x, lo):
    s_all = jnp.sum(x, axis=-1, keepdims=True)
    s_lo = jnp.sum(jnp.where(lo, x, 0.0), axis=-1, keepdims=True)
    return jnp.where(lo, s_lo, s_all - s_lo) * (1.0 / HEAD)


def _head_blocks():
    r = lax.broadcasted_iota(jnp.int32, (PAIR, PAIR), 0) < HEAD
    c = lax.broadcasted_iota(jnp.int32, (PAIR, PAIR), 1) < HEAD
    return jnp.where(r == c, 1.0, 0.0).astype(BF16)


def _seg_mean_mxu(x, blocks):
    return jnp.dot(x.astype(BF16), blocks, preferred_element_type=F32) * (1.0 / HEAD)


def _rope(n, c, s1, s2):
    return n * c + pltpu.roll(n, ROPE_HALF, 1) * s1 + pltpu.roll(n, PAIR - ROPE_HALF, 1) * s2


def _rope_t(dy, c, s1, s2):
    return dy * c - pltpu.roll(dy, PAIR - ROPE_HALF, 1) * s2 - pltpu.roll(dy, ROPE_HALF, 1) * s1


def _prep_fwd(pab, qg, kg, tabs, tm=512):
    def body(p_ref, qg_ref, kg_ref, c_ref, s1_ref, s2_ref, *outs):
        blocks = _head_blocks()
        c, s1, s2 = c_ref[...], s1_ref[...], s2_ref[...]
        for g in range(3):
            qn_ref, kn_ref, v_ref = outs[3 * g:3 * g + 3]
            for p in range(NPAIR):
                for which, gains, dst in ((0, qg_ref, qn_ref), (1, kg_ref, kn_ref)):
                    col = (2 + 3 * which + g) * 512 + p * PAIR
                    xr = p_ref[:, col:col + PAIR].astype(F32)
                    rinv = lax.rsqrt(_seg_mean_mxu(xr * xr, blocks) + EPS)
                    outs[9 + 2 * g + which][p] = rinv.astype(BF16)
                    dst[p] = _rope(xr * rinv * gains[g:g + 1, :], c, s1, s2)
                col = (8 + g) * 512 + p * PAIR
                v_ref[p] = p_ref[:, col:col + PAIR].astype(F32)

    pm = pl.BlockSpec((NPAIR, tm, PAIR), lambda i: (0, i, 0))
    tab = pl.BlockSpec((tm, PAIR), lambda i: (i, 0))
    gain = pl.BlockSpec((3, PAIR), lambda i: (0, 0))
    res = pl.pallas_call(
        body, name="prep_fwd", grid=(T // tm,),
        in_specs=[pl.BlockSpec((tm, AB_IN), lambda i: (i, 0)), gain, gain, tab, tab, tab],
        out_specs=[pm] * 15, out_shape=[_sds((NPAIR, T, PAIR), F32)] * 9 + [_sds((NPAIR, T, PAIR), BF16)] * 6,
        compiler_params=_cparams(1))(pab, qg, kg, *tabs)
    return res[0:9], res[9:15]


def _res_index(it, rate):
    window = NBACK * rate
    b = it // rate
    rho = it % rate
    start = b * window + rho
    startp = jnp.maximum(start - window, rho)
    kmin = jnp.where(b > 0, 0, NBACK)
    return start, startp, kmin


def _rows(start, rate):
    if rate == 1:
        return pl.ds(pl.multiple_of(start, NBACK), NBACK)
    return pl.ds(start, NBACK, stride=rate)


def _band_bias():
    qs = lax.broadcasted_iota(jnp.int32, (2 * NBACK, 2 * NBACK), 0)
    kj = lax.broadcasted_iota(jnp.int32, (2 * NBACK, 2 * NBACK), 1)
    dist = (qs & (NBACK - 1)) + NBACK - kj
    both = (dist >= 0) & (dist <= NBACK)
    return jnp.where(both, 0.0, NEG_INF), jnp.where(both & (kj >= NBACK), 0.0, NEG_INF)


def _attn_fwd_block(q, kcat, vcat, first, lo, biases):
    vcat1 = jnp.concatenate([vcat, jnp.ones((2 * NBACK, PAIR), BF16)], axis=1)
    q2 = jnp.concatenate([jnp.where(lo, q, 0.0), jnp.where(lo, 0.0, q)], axis=0).astype(BF16)
    s = lax.dot_general(q2, kcat, (((1,), (1,)), ((), ())), preferred_element_type=F32)
    s = s + jnp.where(first, biases[1], biases[0])
    m = jnp.max(s, axis=-1, keepdims=True)
    ol = jnp.dot(jnp.exp(s - m).astype(BF16), vcat1, preferred_element_type=F32)
    o2 = ol[:, 0:PAIR] / ol[:, PAIR:]
    ls = m + jnp.log(ol[:, PAIR:])
    return jnp.where(lo, o2[0:NBACK], o2[NBACK:]), jnp.where(lo, ls[0:NBACK], ls[NBACK:])


def _attn_fwd(qn, kn, v, rate, name, dep=None):
    if rate > MAX_ROW_STRIDE:
        return _attn_fwd_gathered(qn, kn, v, rate, name, dep)

    def body(q_ref, k_ref, v_ref, *rest):
        o_ref, l_ref = rest[-2:]
        lo = _lo_mask((NBACK, PAIR))
        biases = _band_bias()

        def step(it, carry):
            start, startp, kmin = _res_index(it, rate)
            q = q_ref[_rows(start, rate), :] * (HEAD ** -0.5)
            kcat = jnp.concatenate([k_ref[_rows(startp, rate), :], k_ref[_rows(start, rate), :]], axis=0).astype(BF16)
            vcat = jnp.concatenate([v_ref[_rows(startp, rate), :], v_ref[_rows(start, rate), :]], axis=0).astype(BF16)
            o, ls = _attn_fwd_block(q, kcat, vcat, kmin != 0, lo, biases)
            o_ref[_rows(start, rate), :] = o
            l_ref[_rows(start, rate), :] = ls
            return carry

        lax.fori_loop(0, T // NBACK, step, 0, unroll=4)

    pm = pl.BlockSpec((None, T, PAIR), lambda p: (p, 0, 0))
    in_specs, args = [pm, pm, pm], [qn, kn, v]
    if dep is not None:
        in_specs.append(HBM_SPEC)
        args.append(dep)
    return pl.pallas_call(
        body, name=name, grid=(NPAIR,), in_specs=in_specs, out_specs=[pm, pm],
        out_shape=[_sds((NPAIR, T, PAIR), F32)] * 2, compiler_params=_cparams(1))(*args)


def _attn_fwd_gathered(qn, kn, v, rate, name, dep):
    n = T // rate
    nblk = n // NBACK

    def body(q_hbm, k_hbm, v_hbm, *rest):
        o_hbm, l_hbm, qb, kb, vb, ob, lb, in_sem, out_sem = rest[-9:]
        p = pl.program_id(0)
        slot = p % 2

        def loads(pair, s):
            return [pltpu.make_async_copy(x.at[pair, :, r, :], buf.at[s, r], in_sem.at[3 * s + a])
                    for a, (x, buf) in enumerate(((q_hbm, qb), (k_hbm, kb), (v_hbm, vb))) for r in range(rate)]

        def stores(pair, s):
            return [pltpu.make_async_copy(buf.at[s, r], x.at[pair, :, r, :], out_sem.at[2 * s + a])
                    for a, (x, buf) in enumerate(((o_hbm, ob), (l_hbm, lb))) for r in range(rate)]

        @pl.when(p == 0)
        def _():
            for c in loads(0, 0):
                c.start()

        @pl.when(p + 1 < NPAIR)
        def _():
            for c in loads(p + 1, 1 - slot):
                c.start()

        for c in loads(p, slot):
            c.wait()

        @pl.when(p >= 2)
        def _():
            for c in stores(p - 2, slot):
                c.wait()

        lo = _lo_mask((NBACK, PAIR))
        biases = _band_bias()

        def step(it, carry):
            b, r = it % nblk, it // nblk
            cur = pl.ds(pl.multiple_of(b * NBACK, NBACK), NBACK)
            prev = pl.ds(pl.multiple_of(jnp.maximum(b - 1, 0) * NBACK, NBACK), NBACK)
            q = qb[slot, r, cur, :] * (HEAD ** -0.5)
            kcat = jnp.concatenate([kb[slot, r, prev, :], kb[slot, r, cur, :]], axis=0).astype(BF16)
            vcat = jnp.concatenate([vb[slot, r, prev, :], vb[slot, r, cur, :]], axis=0).astype(BF16)
            o, ls = _attn_fwd_block(q, kcat, vcat, b == 0, lo, biases)
            ob[slot, r, cur, :] = o
            lb[slot, r, cur, :] = ls
            return carry

        lax.fori_loop(0, T // NBACK, step, 0, unroll=4)

        for c in stores(p, slot):
            c.start()

        @pl.when(p == NPAIR - 1)
        def _():
            for c in stores(p - 1, 1 - slot) + stores(p, slot):
                c.wait()

    by_residue = lambda a: a.reshape(NPAIR, n, rate, PAIR)
    in_specs, args = [HBM_SPEC] * 3, [by_residue(qn), by_residue(kn), by_residue(v)]
    if dep is not None:
        in_specs.append(HBM_SPEC)
        args.append(dep)
    o, l = pl.pallas_call(
        body, name=name, grid=(NPAIR,), in_specs=in_specs, out_specs=[HBM_SPEC] * 2,
        out_shape=[_sds((NPAIR, n, rate, PAIR), F32)] * 2,
        scratch_shapes=[pltpu.VMEM((2, rate, n, PAIR), F32)] * 5
        + [pltpu.SemaphoreType.DMA((6,)), pltpu.SemaphoreType.DMA((4,))],
        compiler_params=_cparams(1))(*args)
    return o.reshape(NPAIR, T, PAIR), l.reshape(NPAIR, T, PAIR)


def _merge_fwd(cat_ab, outs, lses, tm=512):
    def body(cat_in, o0, o1, o2, l0, l1, l2, cat_ref, lse_ref):
        del cat_in
        for p in range(NPAIR):
            a0, a1, a2 = l0[p], l1[p], l2[p]
            m = jnp.maximum(jnp.maximum(a0, a1), a2)
            w0, w1, w2 = jnp.exp(a0 - m), jnp.exp(a1 - m), jnp.exp(a2 - m)
            s = w0 + w1 + w2
            b = (w0 * o0[p] + w1 * o1[p] + w2 * o2[p]) / s
            cat_ref[:, p * PAIR:(p + 1) * PAIR] = b.astype(BF16)
            lse_ref[p] = m + jnp.log(s)

    pm = pl.BlockSpec((NPAIR, tm, PAIR), lambda i: (0, i, 0))
    return pl.pallas_call(
        body, name="merge_fwd", grid=(T // tm,),
        in_specs=[pl.BlockSpec(memory_space=pl.ANY)] + [pm] * 6,
        out_specs=[pl.BlockSpec((tm, 512), lambda i: (i, 1)), pm],
        out_shape=[_sds((T, D), BF16), _sds((NPAIR, T, PAIR), F32)],
        input_output_aliases={0: 0}, compiler_params=_cparams(1))(cat_ab, *outs, *lses)


def _d_cat_ab(dxb, w_ab_out, cat, dep, tm=512):
    def epilogue(d, rows, side, outs):
        (b_ref,), (da_ref, dbp_ref, e_ref) = side, outs
        da_ref[rows, :] = d[:, 0:512].astype(BF16)
        lo = _lo_mask((EPI_ROWS, PAIR))
        for p in range(NPAIR):
            db = d[:, 512 + p * PAIR:512 + (p + 1) * PAIR]
            b = b_ref[rows, p * PAIR:(p + 1) * PAIR].astype(F32)
            dbp_ref[p, rows, :] = db
            e_ref[p, rows, :] = _seg_mean(db * b, lo) * float(HEAD)

    pm = pl.BlockSpec((NPAIR, tm, PAIR), lambda i: (0, i, 0))
    return _mm_nt_rows(
        dxb, w_ab_out, "mm_d_cat_ab", epilogue, [cat], [pl.BlockSpec((tm, 512), lambda i: (i, 1))],
        [pl.BlockSpec((tm, 512), lambda i: (i, 0)), pm, pm],
        [_sds((T, 512), BF16), _sds((NPAIR, T, PAIR), F32), _sds((NPAIR, T, PAIR), F32)], dep=dep, tm=tm)


def _attn_bwd_block(q, db, ev, ls, kcat, vcat, first, lo, biases):
    scale = HEAD ** -0.5
    nt = (((1,), (1,)), ((), ()))
    tn = (((0,), (0,)), ((), ()))
    q = q * scale
    q2 = jnp.concatenate([jnp.where(lo, q, 0.0), jnp.where(lo, 0.0, q)], axis=0).astype(BF16)
    db2 = jnp.concatenate([jnp.where(lo, db, 0.0), jnp.where(lo, 0.0, db)], axis=0).astype(BF16)
    ls2 = jnp.concatenate([ls[:, 0:1], ls[:, HEAD:HEAD + 1]], axis=0)
    ev2 = jnp.concatenate([ev[:, 0:1], ev[:, HEAD:HEAD + 1]], axis=0)
    s = lax.dot_general(q2, kcat, nt, preferred_element_type=F32)
    pt = jnp.exp(s + jnp.where(first, biases[1], biases[0]) - ls2)
    dp = lax.dot_general(db2, vcat, nt, preferred_element_type=F32)
    ds = (pt * (dp - ev2)).astype(BF16)
    dq2 = jnp.dot(ds, kcat, preferred_element_type=F32) * scale
    dkc = lax.dot_general(ds, q2, tn, preferred_element_type=F32)
    dvc = lax.dot_general(pt.astype(BF16), db2, tn, preferred_element_type=F32)
    return jnp.where(lo, dq2[0:NBACK], dq2[NBACK:]), dkc, dvc


def _attn_bwd_loop(read, write, nblk):
    lo = _lo_mask((NBACK, PAIR))
    biases = _band_bias()

    def one(it, carry):
        dk_carry, dv_carry = carry
        rho = it // nblk
        b = it % nblk
        bp = jnp.maximum(b - 1, 0)
        kcat = jnp.concatenate([read(1, rho, bp), read(1, rho, b)], axis=0).astype(BF16)
        vcat = jnp.concatenate([read(2, rho, bp), read(2, rho, b)], axis=0).astype(BF16)
        dq, dkc, dvc = _attn_bwd_block(read(0, rho, b), read(3, rho, b), read(4, rho, b), read(5, rho, b), kcat, vcat,
                                       b == 0, lo, biases)
        write(0, rho, b, dq)
        write(1, rho, bp, dk_carry + dkc[0:NBACK])
        write(1, rho, b, dkc[NBACK:])
        write(2, rho, bp, dv_carry + dvc[0:NBACK])
        write(2, rho, b, dvc[NBACK:])
        return dkc[NBACK:], dvc[NBACK:]

    def step(i, carry):
        for u in range(ATTN_BWD_UNROLL):
            carry = one(i * ATTN_BWD_UNROLL + u, carry)
        return carry

    zero = jnp.zeros((NBACK, PAIR), F32)
    lax.fori_loop(0, T // NBACK // ATTN_BWD_UNROLL, step, (zero, zero))


def _attn_bwd(qn, kn, v, dbp, e, lse, rate, name):
    if rate > MAX_ROW_STRIDE:
        return _attn_bwd_gathered(qn, kn, v, dbp, e, lse, rate, name)
    window = NBACK * rate

    def body(*refs):
        rows = lambda rho, b: _rows(b * window + rho, rate)

        def write(j, rho, b, value):
            refs[6 + j][rows(rho, b), :] = value

        _attn_bwd_loop(lambda j, rho, b: refs[j][rows(rho, b), :], write, T // window)

    pm = pl.BlockSpec((None, T, PAIR), lambda p: (p, 0, 0))
    return pl.pallas_call(
        body, name=name, grid=(NPAIR,), in_specs=[pm] * 6, out_specs=[pm] * 3,
        out_shape=[_sds((NPAIR, T, PAIR), F32)] * 3, compiler_params=_cparams(1, 56))(qn, kn, v, dbp, e, lse)


def _attn_bwd_gathered(qn, kn, v, dbp, e, lse, rate, name):
    n = T // rate

    def body(*refs):
        ins, outs, in_bufs, out_bufs, (in_sem, out_sem) = refs[0:6], refs[6:9], refs[9:15], refs[15:18], refs[18:20]
        p = pl.program_id(0)
        slot = p % 2

        def loads(pair, s):
            return [pltpu.make_async_copy(x.at[pair, :, r, :], buf.at[s, r], in_sem.at[6 * s + a])
                    for a, (x, buf) in enumerate(zip(ins, in_bufs)) for r in range(rate)]

        def stores(pair, s):
            return [pltpu.make_async_copy(buf.at[s, r], x.at[pair, :, r, :], out_sem.at[3 * s + a])
                    for a, (x, buf) in enumerate(zip(outs, out_bufs)) for r in range(rate)]

        @pl.when(p == 0)
        def _():
            for c in loads(0, 0):
                c.start()

        @pl.when(p + 1 < NPAIR)
        def _():
            for c in loads(p + 1, 1 - slot):
                c.start()

        for c in loads(p, slot):
            c.wait()

        @pl.when(p >= 2)
        def _():
            for c in stores(p - 2, slot):
                c.wait()

        rows = lambda b: pl.ds(pl.multiple_of(b * NBACK, NBACK), NBACK)

        def write(j, rho, b, value):
            out_bufs[j][slot, rho, rows(b), :] = value

        _attn_bwd_loop(lambda j, rho, b: in_bufs[j][slot, rho, rows(b), :], write, n // NBACK)

        for c in stores(p, slot):
            c.start()

        @pl.when(p == NPAIR - 1)
        def _():
            for c in stores(p - 1, 1 - slot) + stores(p, slot):
                c.wait()

    by_residue = lambda a: a.reshape(NPAIR, n, rate, PAIR)
    res = pl.pallas_call(
        body, name=name, grid=(NPAIR,), in_specs=[HBM_SPEC] * 6, out_specs=[HBM_SPEC] * 3,
        out_shape=[_sds((NPAIR, n, rate, PAIR), F32)] * 3,
        scratch_shapes=[pltpu.VMEM((2, rate, n, PAIR), F32)] * 9
        + [pltpu.SemaphoreType.DMA((12,)), pltpu.SemaphoreType.DMA((6,))],
        compiler_params=_cparams(1, 56))(*[by_residue(a) for a in (qn, kn, v, dbp, e, lse)])
    return [r.reshape(NPAIR, T, PAIR) for r in res]


def _ab_in_bwd(pab, dcat, sgu_g, sgu_b, sgu_w, sgu_bias3, qg, kg, tabs, dqkv, rinvs, tm=256):
    def body(p_ref, dcat_ref, g_ref, b_ref, w_ref, bias_ref, qg_ref, kg_ref, c_ref, s1_ref, s2_ref, *rest):
        dq_refs, rinv_refs = rest[0:9], rest[9:15]
        o_ref, dwm_ref, dbias_ref, dsg_ref, dsb_ref, dgain_ref = rest[15:]
        i = pl.program_id(0)

        @pl.when(i == 0)
        def _():
            dwm_ref[...] = jnp.zeros_like(dwm_ref)
            dbias_ref[...] = jnp.zeros_like(dbias_ref)
            dsg_ref[...] = jnp.zeros_like(dsg_ref)
            dsb_ref[...] = jnp.zeros_like(dsb_ref)
            dgain_ref[...] = jnp.zeros_like(dgain_ref)

        zu = p_ref[:, 0:512].astype(F32)
        zv = p_ref[:, 512:1024].astype(F32)
        u = _gelu(zu)
        v = _gelu(zv)
        mu = jnp.mean(v, axis=-1, keepdims=True)
        vc = v - mu
        rstd = lax.rsqrt(jnp.mean(vc * vc, axis=-1, keepdims=True) + EPS)
        xhat = vc * rstd
        vn = (xhat * g_ref[...] + b_ref[...]).astype(BF16)
        da = dcat_ref[...].astype(F32)
        tri = _tril_mask()
        du_parts = [[None] * 4 for _ in range(tm // 128)]
        dvn_parts = [[None] * 4 for _ in range(tm // 128)]
        for gi in range(4):
            wg = jnp.where(tri, w_ref[gi], 0.0).astype(BF16)
            bg = bias_ref[gi]
            for c in range(tm // 128):
                rs, cs = slice(c * 128, (c + 1) * 128), slice(gi * 128, (gi + 1) * 128)
                vblk = vn[rs, cs]
                mixed = jnp.dot(wg, vblk, preferred_element_type=F32) + bg
                dab = da[rs, cs]
                du_parts[c][gi] = dab * mixed
                dmixed = dab * u[rs, cs]
                dmb = dmixed.astype(BF16)
                dvn_parts[c][gi] = lax.dot_general(wg, dmb, (((0,), (0,)), ((), ())), preferred_element_type=F32)
                dwm = lax.dot_general(dmb, vblk, (((1,), (1,)), ((), ())), preferred_element_type=F32)
                dwm_ref[gi] += jnp.where(tri, dwm, 0.0)
                dbias_ref[gi] += dmixed
        du = jnp.concatenate([jnp.concatenate(r, axis=1) for r in du_parts], axis=0)
        dvn = jnp.concatenate([jnp.concatenate(r, axis=1) for r in dvn_parts], axis=0)
        dsg_ref[...] += jnp.sum(dvn * xhat, axis=0, keepdims=True)
        dsb_ref[...] += jnp.sum(dvn, axis=0, keepdims=True)
        dxh = dvn * g_ref[...]
        dv = rstd * (dxh - jnp.mean(dxh, axis=-1, keepdims=True)
                     - xhat * jnp.mean(dxh * xhat, axis=-1, keepdims=True))
        o_ref[:, 0:512] = (du * _gelu_grad(zu)).astype(BF16)
        o_ref[:, 512:1024] = (dv * _gelu_grad(zv)).astype(BF16)

        blocks = _head_blocks()
        c, s1, s2 = c_ref[...], s1_ref[...], s2_ref[...]
        for g in range(3):
            dq_ref, dk_ref, dv_ref = dq_refs[3 * g:3 * g + 3]
            for p in range(NPAIR):
                for which, gains, src in ((0, qg_ref, dq_ref), (1, kg_ref, dk_ref)):
                    col = (2 + 3 * which + g) * 512 + p * PAIR
                    xr = p_ref[:, col:col + PAIR].astype(F32)
                    rinv = rinv_refs[2 * g + which][p].astype(F32)
                    xh = xr * rinv
                    dn = _rope_t(src[p], c, s1, s2)
                    row = 2 * g + which
                    dgain_ref[row:row + 1, :] += jnp.sum(dn * xh, axis=0, keepdims=True)
                    dxh2 = dn * gains[g:g + 1, :]
                    dx = rinv * (dxh2 - xh * _seg_mean_mxu(dxh2 * xh, blocks))
                    o_ref[:, col:col + PAIR] = dx.astype(BF16)
                col = (8 + g) * 512 + p * PAIR
                o_ref[:, col:col + PAIR] = dv_ref[p].astype(BF16)

    pm = pl.BlockSpec((NPAIR, tm, PAIR), lambda i: (0, i, 0))
    tab = pl.BlockSpec((tm, PAIR), lambda i: (i, 0))
    gain = pl.BlockSpec((3, PAIR), lambda i: (0, 0))
    vec = pl.BlockSpec((1, 512), lambda i: (0, 0))
    full = pl.BlockSpec((tm, AB_IN), lambda i: (i, 0))
    w4 = pl.BlockSpec((4, 128, 128), lambda i: (0, 0, 0))
    return pl.pallas_call(
        body, name="ab_in_bwd", grid=(T // tm,),
        in_specs=[full, pl.BlockSpec((tm, 512), lambda i: (i, 0)), vec, vec, w4,
                  pl.BlockSpec((4, 128, 1), lambda i: (0, 0, 0)), gain, gain, tab, tab, tab] + [pm] * 15,
        out_specs=[full, w4, w4, vec, vec, pl.BlockSpec((8, PAIR), lambda i: (0, 0))],
        out_shape=[_sds((T, AB_IN), BF16), _sds((4, 128, 128), F32), _sds((4, 128, 128), F32),
                   _sds((1, 512), F32), _sds((1, 512), F32), _sds((8, PAIR), F32)],
        compiler_params=_cparams(1))(pab, dcat, sgu_g, sgu_b, sgu_w, sgu_bias3, qg, kg, *tabs, *dqkv, *rinvs)


def _ln_stats(x):
    mu = jnp.mean(x, axis=-1, keepdims=True)
    xc = x - mu
    rstd = lax.rsqrt(jnp.mean(xc * xc, axis=-1, keepdims=True) + EPS)
    return xc * rstd, rstd


CONV_RC = 64


def _shifted_copies(src, dst, tm):
    dst[0] = src[...]
    for b in range(1, 8):
        dst[b, 0:tm + HALO - 8, :] = src[pl.ds(b, tm + HALO - 8), :]


def _offsets_by_phase(first):
    groups = {}
    for o in range(first, first + CONV_C_TAPS):
        groups.setdefault(o % 8, []).append(o)
    return sorted(groups.items())


def _window(shifted, b8, base, offsets, lanes):
    rows = 8 * (max(offsets) // 8) + CONV_RC
    return shifted[b8, pl.ds(base, rows), lanes].reshape(rows // 8, 8, 128)


def _cd_fwd(pcd, cw, cb, lg, lb, dw, tm=512):
    per = tm // HALO

    def body(p_ref, h_ref, cw_ref, cb_ref, lg_ref, lb_ref, dw_ref, cat_ref, c0_ref, c1_ref, dd_ref, y_ref,
             buf, buf2, sb):
        i = pl.program_id(0)
        live = jnp.where(i > 0, 1.0, 0.0)
        a = p_ref[:, 0:512].astype(F32)
        gt = p_ref[:, 512:1024].astype(F32)
        gb = p_ref[:, 1024:1536].astype(F32)
        gc = p_ref[:, 1536:2048].astype(F32)
        hv = p_ref[:, 2048:2560].astype(F32)
        c0 = a * _sigmoid(gt)
        dd = gc * hv
        buf[0:HALO, :] = h_ref[:, 0:512].astype(F32) * _sigmoid(h_ref[:, 512:1024].astype(F32)) * live
        buf[HALO:, :] = c0
        buf2[0:HALO, :] = h_ref[:, 1536:2048].astype(F32) * h_ref[:, 2048:2560].astype(F32) * live
        buf2[HALO:, :] = dd
        c0_ref[...] = c0.astype(BF16)
        dd_ref[...] = dd.astype(BF16)
        _shifted_copies(buf, sb, tm)

        def conv_rows(r, carry):
            base = pl.multiple_of(r * CONV_RC, CONV_RC)
            for c in range(4):
                lanes = slice(c * 128, (c + 1) * 128)
                acc = jnp.broadcast_to(cb_ref[:, lanes], (CONV_RC // 8, 8, 128))
                for b8, offsets in _offsets_by_phase(HALO - (CONV_C_TAPS - 1)):
                    win = _window(sb, b8, base, offsets, lanes)
                    for o in offsets:
                        j = o - (HALO - (CONV_C_TAPS - 1))
                        acc = acc + cw_ref[8 * j:8 * j + 8, lanes] * win[o // 8:o // 8 + CONV_RC // 8]
                c1_ref[pl.ds(base, CONV_RC), lanes] = acc.reshape(CONV_RC, 128)
            return carry

        lax.fori_loop(0, tm // CONV_RC, conv_rows, 0)
        xhat, _ = _ln_stats(c1_ref[...])
        c2 = xhat * lg_ref[...] + lb_ref[...]
        y = jnp.zeros((tm, 512), F32)
        for j in range(CONV_D_TAPS):
            y = y + dw_ref[j:j + 1, :] * buf2[pl.ds(HALO - (CONV_D_TAPS - 1) + j, tm), :]
        cat_ref[:, 0:512] = (c2 * _sigmoid(c2)).astype(BF16)
        cat_ref[:, 512:1024] = (gb * y).astype(BF16)
        y_ref[...] = y.astype(BF16)

    half = pl.BlockSpec((tm, 512), lambda i: (i, 0))
    vec = pl.BlockSpec((1, 512), lambda i: (0, 0))
    return pl.pallas_call(
        body, name="cd_fwd", grid=(T // tm,),
        in_specs=[pl.BlockSpec((tm, CD_IN), lambda i: (i, 0)),
                  pl.BlockSpec((HALO, CD_IN), lambda i: (jnp.maximum(i * per - 1, 0), 0)),
                  pl.BlockSpec((8 * 32, 512), lambda i: (0, 0)), vec, vec, vec, pl.BlockSpec((8, 512), lambda i: (0, 0))],
        out_specs=[pl.BlockSpec((tm, D), lambda i: (i, 0)), half, half, half, half],
        out_shape=[_sds((T, D), BF16), _sds((T, 512), BF16), _sds((T, 512), F32), _sds((T, 512), BF16),
                   _sds((T, 512), BF16)],
        scratch_shapes=[pltpu.VMEM((HALO + tm, 512), F32), pltpu.VMEM((HALO + tm, 512), F32),
                        pltpu.VMEM((8, HALO + tm, 512), F32)],
        compiler_params=_cparams(1))(pcd, pcd, cw, cb, lg, lb, dw)


def _d_cat_cd(dxb, w_cd_out, c1, pcd, y, lg, lb, dep, tm=512):
    def epilogue(d, rows, side, outs):
        c1_ref, gb_ref, y_ref, lg_ref, lb_ref = side
        dc1_ref, dy3_ref, dgb_ref, dlg_ref, dlb_ref, dcb_ref = outs
        dc, ddo = d[:, 0:512], d[:, 512:1024]
        xhat, rstd = _ln_stats(c1_ref[rows, :])
        c2 = xhat * lg_ref[...] + lb_ref[...]
        sg = _sigmoid(c2)
        dc2 = dc * sg * (1.0 + c2 * (1.0 - sg))
        dlg_ref[...] += jnp.sum(dc2 * xhat, axis=0, keepdims=True)
        dlb_ref[...] += jnp.sum(dc2, axis=0, keepdims=True)
        dxh = dc2 * lg_ref[...]
        dc1 = rstd * (dxh - jnp.mean(dxh, axis=-1, keepdims=True)
                      - xhat * jnp.mean(dxh * xhat, axis=-1, keepdims=True))
        dcb_ref[...] += jnp.sum(dc1, axis=0, keepdims=True)
        dc1_ref[rows, :] = dc1
        dgb_ref[rows, :] = (ddo * y_ref[rows, :].astype(F32)).astype(BF16)
        dy3_ref[rows, :] = ddo * gb_ref[rows, :].astype(F32)

    half = pl.BlockSpec((tm, 512), lambda i: (i, 0))
    vec = pl.BlockSpec((1, 512), lambda i: (0, 0))
    return _mm_nt_rows(
        dxb, w_cd_out, "mm_d_cat_cd", epilogue, [c1, pcd, y, lg, lb],
        [half, pl.BlockSpec((tm, 512), lambda i: (i, 2)), half, vec, vec], [half, half, half, vec, vec, vec],
        [_sds((T, 512), F32), _sds((T, 512), F32), _sds((T, 512), BF16),
         _sds((1, 512), F32), _sds((1, 512), F32), _sds((1, 512), F32)], sums=(3, 4, 5), dep=dep, tm=tm)


def _cd_bwd_conv(pcd, dc1, dy3, c0, dd, dgb, cw8, dw, tm=256):
    per = tm // HALO
    nblk = T // tm
    last32 = T // HALO - 1

    def body(p_ref, dc1_ref, dc1n_ref, dy3_ref, dy3n_ref, c0_ref, dd_ref, dgb_ref, cw_ref, dw_ref,
             o_ref, dcw_ref, ddw_ref, dbuf, d3buf, sd, dc0_buf):
        i = pl.program_id(0)
        has_next = jnp.where(i < nblk - 1, 1.0, 0.0)

        @pl.when(i == 0)
        def _():
            dcw_ref[...] = jnp.zeros_like(dcw_ref)
            ddw_ref[...] = jnp.zeros_like(ddw_ref)

        dbuf[0:tm, :] = dc1_ref[...]
        dbuf[tm:, :] = dc1n_ref[...] * has_next
        d3buf[0:tm, :] = dy3_ref[...]
        d3buf[tm:, :] = dy3n_ref[...] * has_next
        _shifted_copies(dbuf, sd, tm)
        n_tiles = tm // CONV_RC

        phases = _offsets_by_phase(0)

        def dc0_rows(r, carry):
            base = pl.multiple_of(r * CONV_RC, CONV_RC)
            for c in range(4):
                lanes = slice(c * 128, (c + 1) * 128)
                acc = jnp.zeros((CONV_RC // 8, 8, 128), F32)
                for b8, offsets in phases:
                    win = _window(sd, b8, base, offsets, lanes)
                    for o in offsets:
                        j = CONV_C_TAPS - 1 - o
                        acc = acc + cw_ref[8 * j:8 * j + 8, lanes] * win[o // 8:o // 8 + CONV_RC // 8]
                dc0_buf[pl.ds(base, CONV_RC), lanes] = acc.reshape(CONV_RC, 128)
            return carry

        lax.fori_loop(0, n_tiles, dc0_rows, 0)

        for c in range(4):
            lanes = slice(c * 128, (c + 1) * 128)
            for b8, offsets in phases:
                def dw_rows(r, accs, lanes=lanes, b8=b8, offsets=offsets):
                    base = pl.multiple_of(r * CONV_RC, CONV_RC)
                    xin = c0_ref[pl.ds(base, CONV_RC), lanes].astype(F32).reshape(CONV_RC // 8, 8, 128)
                    win = _window(sd, b8, base, offsets, lanes)
                    return tuple(acc + jnp.sum(xin * win[o // 8:o // 8 + CONV_RC // 8], axis=0)
                                 for acc, o in zip(accs, offsets))

                accs = lax.fori_loop(0, n_tiles, dw_rows, tuple(jnp.zeros((8, 128), F32) for _ in offsets))
                for acc, o in zip(accs, offsets):
                    j = CONV_C_TAPS - 1 - o
                    dcw_ref[j:j + 1, lanes] += jnp.sum(acc, axis=0, keepdims=True)

        dc0 = dc0_buf[...]
        ddin = dd_ref[...].astype(F32)
        ddd = jnp.zeros((tm, 512), F32)
        for j in range(CONV_D_TAPS):
            dy_shift = d3buf[pl.ds(CONV_D_TAPS - 1 - j, tm), :]
            ddd = ddd + dw_ref[j:j + 1, :] * dy_shift
            ddw_ref[j:j + 1, :] += jnp.sum(ddin * dy_shift, axis=0, keepdims=True)

        a = p_ref[:, 0:512].astype(F32)
        gt = p_ref[:, 512:1024].astype(F32)
        gc = p_ref[:, 1536:2048].astype(F32)
        hv = p_ref[:, 2048:2560].astype(F32)
        sg = _sigmoid(gt)
        o_ref[:, 0:512] = (dc0 * sg).astype(BF16)
        o_ref[:, 512:1024] = (dc0 * a * sg * (1.0 - sg)).astype(BF16)
        o_ref[:, 1024:1536] = dgb_ref[...]
        o_ref[:, 1536:2048] = (ddd * hv).astype(BF16)
        o_ref[:, 2048:2560] = (ddd * gc).astype(BF16)

    half = pl.BlockSpec((tm, 512), lambda i: (i, 0))
    nxt = pl.BlockSpec((HALO, 512), lambda i: (jnp.minimum((i + 1) * per, last32), 0))
    full = pl.BlockSpec((tm, CD_IN), lambda i: (i, 0))
    return pl.pallas_call(
        body, name="cd_bwd_conv", grid=(nblk,),
        in_specs=[full, half, nxt, half, nxt, half, half, half,
                  pl.BlockSpec((8 * 32, 512), lambda i: (0, 0)), pl.BlockSpec((8, 512), lambda i: (0, 0))],
        out_specs=[full, pl.BlockSpec((32, 512), lambda i: (0, 0)), pl.BlockSpec((8, 512), lambda i: (0, 0))],
        out_shape=[_sds((T, CD_IN), BF16), _sds((32, 512), F32), _sds((8, 512), F32)],
        scratch_shapes=[pltpu.VMEM((tm + HALO, 512), F32), pltpu.VMEM((tm + HALO, 512), F32),
                        pltpu.VMEM((8, tm + HALO, 512), F32), pltpu.VMEM((tm, 512), F32)],
        compiler_params=_cparams(1))(pcd, dc1, dc1, dy3, dy3, c0, dd, dgb, cw8, dw)


def _local_step(x, tgt, W, fetch=None, on_grad=None):
    W = dict(W)
    if fetch is None:
        fetch = lambda stage, after: {}
    if on_grad is None:
        on_grad = lambda key, arr: None
    tabs = _rope_tables()
    qg = jnp.tile(W["q_norm_g"], (1, 2))
    kg = jnp.tile(W["k_norm_g"], (1, 2))
    bias3 = W["sgu_bias"].reshape(4, 128, 1)
    G = {}

    h0 = _rms_fwd(x, W["ab_norm_g"], "rms_fwd_ab", dep=W.get("dep_first"))
    W.update(fetch("ab_in", h0))
    pab = _mm_nt(h0, W["wt_ab_in"], "mm_ab_in", dep=W.get("dep0"))
    cat_ab = _mix_a_fwd(pab, W["sgu_norm_g"], W["sgu_norm_b"], W["sgu_w"], bias3)
    qkv, rinvs = _prep_fwd(pab, qg, kg, tabs)
    outs, lses = [], []
    for g, rate in enumerate(DIL_RATES):
        o, l = _attn_fwd(qkv[3 * g], qkv[3 * g + 1], qkv[3 * g + 2], rate, f"attn_fwd_{g}", dep=W.get(f"dep_attn{g}"))
        outs.append(o)
        lses.append(l)
        W.update(fetch(f"attn{g}", o))
    cat_ab, lse = _merge_fwd(cat_ab, outs, lses)
    W.update(fetch("ab_out", lse))
    x1, h1 = _mm_nn(cat_ab, W["w_ab_out"], "mm_ab_out", mode="rms", resid=x, gain=W["ffn_norm_g"][0:1])
    pf0, act0 = _ffn_in(h1, W["wt_ffn_in0"], "ffn_in0")
    W.update(fetch("ffn_down0", act0))
    x2, h2 = _mm_nn(act0, W["w_ffn_down0"], "mm_ffn_down0", mode="rms", resid=x1, gain=W["cd_norm_g"],
                    dep=W.get("dep_down0"))
    W.update(fetch("cd_in", h2))
    pcd = _mm_nt(h2, W["wt_cd_in"], "mm_cd_in")
    cw8 = jnp.repeat(W["conv_c_w32"], 8, axis=0)
    cat_cd, c0, c1, dd, yv = _cd_fwd(pcd, cw8, W["conv_c_b"], W["c_ln_g"], W["c_ln_b"], W["conv_d_w8"])
    x3, h3 = _mm_nn(cat_cd, W["w_cd_out"], "mm_cd_out", mode="rms", resid=x2, gain=W["ffn_norm_g"][1:2])
    pf1, act1 = _ffn_in(h3, W["wt_ffn_in1"], "ffn_in1")
    dy, dyb, loss_cols = _mm_nn(act1, W["w_ffn_down1"], "mm_ffn_down1", mode="loss", resid=x3, tgt=tgt)

    def ffn_bwd(xin, h, pf, act, dres, dresb, layer):
        G[f"w_ffn_down{layer}"] = _mm_tn(act, dresb, f"mm_g_ffn_down{layer}")
        dep = on_grad(f"w_ffn_down{layer}", G[f"w_ffn_down{layer}"])
        dpf = _ffn_dact(dresb, W[f"w_ffn_down{layer}"], pf, f"ffn_dact{layer}", dep=dep)
        G[f"wt_ffn_in{layer}"] = _mm_tn(dpf, h, f"mm_g_ffn_in{layer}")
        dep = on_grad(f"wt_ffn_in{layer}", G[f"wt_ffn_in{layer}"])
        dx, dxb, G[f"ffn_norm_g{layer}"] = _mm_dh_rms_bwd(
            dpf, W[f"wt_ffn_in{layer}"], xin, W["ffn_norm_g"][layer:layer + 1], dres, f"mm_d_h_ffn{layer}", dep=dep)
        return dx, dxb

    dx3, dx3b = ffn_bwd(x3, h3, pf1, act1, dy, dyb, 1)

    G["w_cd_out"] = _mm_tn(cat_cd, dx3b, "mm_g_cd_out")
    dep = on_grad("w_cd_out", G["w_cd_out"])
    dc1, dy3, dgb, G["c_ln_g"], G["c_ln_b"], G["conv_c_b"] = _d_cat_cd(
        dx3b, W["w_cd_out"], c1, pcd, yv, W["c_ln_g"], W["c_ln_b"], dep)
    dpcd, G["conv_c_w32"], G["conv_d_w8"] = _cd_bwd_conv(pcd, dc1, dy3, c0, dd, dgb, cw8, W["conv_d_w8"])
    G["wt_cd_in"] = _mm_tn(dpcd, h2, "mm_g_cd_in")
    dep = on_grad("wt_cd_in", G["wt_cd_in"])
    dx2, dx2b, G["cd_norm_g"] = _mm_dh_rms_bwd(dpcd, W["wt_cd_in"], x2, W["cd_norm_g"], dx3, "mm_d_h_cd", dep=dep)

    dx1, dx1b = ffn_bwd(x1, h1, pf0, act0, dx2, dx2b, 0)

    G["w_ab_out"] = _mm_tn(cat_ab, dx1b, "mm_g_ab_out")
    dep = on_grad("w_ab_out", G["w_ab_out"])
    dcat_a, dbp, e = _d_cat_ab(dx1b, W["w_ab_out"], cat_ab, dep)
    dqkv = []
    for g, rate in enumerate(DIL_RATES):
        dqkv += _attn_bwd(qkv[3 * g], qkv[3 * g + 1], qkv[3 * g + 2], dbp, e, lse, rate, f"attn_bwd_{g}")
    dpab, G["sgu_w"], dbias_part, G["sgu_norm_g"], G["sgu_norm_b"], dgain = _ab_in_bwd(
        pab, dcat_a, W["sgu_norm_g"], W["sgu_norm_b"], W["sgu_w"], bias3, qg, kg, tabs, dqkv, rinvs)
    G["sgu_bias"] = jnp.sum(dbias_part, axis=-1)
    dgain = dgain[0:6, 0:HEAD] + dgain[0:6, HEAD:PAIR]
    G["q_norm_g"] = dgain[0::2]
    G["k_norm_g"] = dgain[1::2]
    G["loss_cols"] = loss_cols
    dep = on_grad("small", G)
    G["wt_ab_in"] = _mm_tn(dpab, h0, "mm_g_ab_in", dep=dep)
    dep = on_grad("wt_ab_in", G["wt_ab_in"])
    grad_x, G["ab_norm_g"] = _mm_dh_rms_bwd(dpab, W["wt_ab_in"], x, W["ab_norm_g"], dx1, "mm_d_h_ab", dep=dep,
                                            bf16_copy=False)
    return loss_cols, grad_x, G


def _my_place():
    return lax.axis_index("x"), lax.axis_index("y"), lax.axis_index("c")


def _dev_index(px, py, pc):
    return 4 * px + 2 * py + pc


def _flip(place, k):
    x, y, c = place
    return (1 - x if k & 4 else x, 1 - y if k & 2 else y, 1 - c if k & 1 else c)


def _landing(shape, dtype, own):
    buf = lax.empty(shape, dtype)
    for lead, part in own:
        buf = lax.dynamic_update_slice(buf, part.reshape((1,) * len(lead) + part.shape),
                                       tuple(lead) + (0,) * part.ndim)
    return buf


HBM_ONLY = pl.BlockSpec(memory_space=pltpu.HBM)
SEM_SPEC = pl.BlockSpec(memory_space=pltpu.SEMAPHORE)
IN_FLIGHT = pltpu.CompilerParams(has_side_effects=pltpu.SideEffectType.DATAFLOW_SIDE_EFFECTING)


def _in_hbm(a):
    return pltpu.with_memory_space_constraint(a, pltpu.HBM)


def _exchange_start(name, srcs, lands, items, dep=None):
    ns, nl, ni = len(srcs), len(lands), len(items)

    def body(*refs):
        S, L = refs[0:ns], refs[ns:ns + nl]
        first_out = ns + nl + (0 if dep is None else 1)
        send_sems, recv_sems, token = refs[first_out], refs[first_out + 1], refs[-1]
        me = _my_place()
        mi = _dev_index(*me)
        for i, (src, dst) in enumerate(items):
            for k in range(1, NDEV):
                peer = _flip(me, k)
                pltpu.make_async_remote_copy(
                    src_ref=src(S, _dev_index(*peer)), dst_ref=dst(L, mi), send_sem=send_sems.at[7 * i + k - 1],
                    recv_sem=recv_sems.at[7 * i + k - 1], device_id=peer, device_id_type=MESH).start()
        token[...] = jnp.zeros_like(token)

    thru = [pltpu.HBM(a.shape, a.dtype) for a in list(srcs) + list(lands)]
    args = [_in_hbm(a) for a in srcs] + [_in_hbm(a) for a in lands]
    in_specs = [HBM_ONLY] * (ns + nl)
    if dep is not None:
        args.append(dep)
        in_specs.append(HBM_SPEC)
    outs = pl.pallas_call(
        body, name=name, in_specs=in_specs,
        out_shape=(pltpu.SemaphoreType.DMA((7 * ni,)), pltpu.SemaphoreType.DMA((7 * ni,)), *thru, _sds((8, 128), F32)),
        out_specs=(SEM_SPEC, SEM_SPEC, *[HBM_ONLY] * (ns + nl), pl.BlockSpec(memory_space=pltpu.VMEM)),
        input_output_aliases={j: 2 + j for j in range(ns + nl)}, compiler_params=IN_FLIGHT)(*args)
    return dict(send=outs[0], recv=outs[1], srcs=list(outs[2:2 + ns]), lands=list(outs[2 + ns:2 + ns + nl]),
                token=outs[-1], items=items)


def _exchange_wait(name, states, after):
    after = list(after) if isinstance(after, (list, tuple)) else [after]
    counts = [(len(st["srcs"]), len(st["lands"]), len(st["items"])) for st in states]
    n_arrays = sum(c[0] + c[1] for c in counts)

    def body(*refs):
        me = _my_place()
        mi = _dev_index(*me)
        pos = 0
        sem_pos = n_arrays
        for st, (ns, nl, ni) in zip(states, counts):
            S, L = refs[pos:pos + ns], refs[pos + ns:pos + ns + nl]
            send_sems, recv_sems = refs[sem_pos], refs[sem_pos + 1]
            pos += ns + nl
            sem_pos += 2
            for i, (src, dst) in enumerate(st["items"]):
                for k in range(1, NDEV):
                    cp = pltpu.make_async_remote_copy(
                        src_ref=src(S, mi), dst_ref=dst(L, mi), send_sem=send_sems.at[7 * i + k - 1],
                        recv_sem=recv_sems.at[7 * i + k - 1], device_id=me, device_id_type=MESH)
                    cp.wait_send()
                    cp.wait_recv()

    arrays, sems = [], []
    for st in states:
        arrays += st["srcs"] + st["lands"]
        sems += [st["send"], st["recv"]]
    outs = pl.pallas_call(
        body, name=name, in_specs=[HBM_ONLY] * n_arrays + [SEM_SPEC] * len(sems) + [HBM_SPEC] * len(after),
        out_shape=tuple(pltpu.HBM(a.shape, a.dtype) for a in arrays), out_specs=tuple([HBM_ONLY] * n_arrays),
        input_output_aliases={j: j for j in range(n_arrays)}, compiler_params=IN_FLIGHT)(*arrays, *sems, *after)
    lands, pos = [], 0
    for ns, nl, _ in counts:
        lands.append(list(outs[pos + ns:pos + ns + nl]))
        pos += ns + nl
    return lands


def _place_and_neighbours():
    x, y, c = _my_place()
    return (x, y, c), (x, y, 1 - c), [(1 - x, y), (x, 1 - y), (1 - x, 1 - y)]


def _gather_start(name, srcs, lands, items, dep=None):
    ns, nl, ni = len(srcs), len(lands), len(items)

    def body(*refs):
        S, L = refs[0:ns], refs[ns:ns + nl]
        first_out = ns + nl + (0 if dep is None else 1)
        send_sems, recv_sems, token = refs[first_out], refs[first_out + 1], refs[-1]
        me, sib, chips = _place_and_neighbours()
        mi = _dev_index(*me)
        for i, (src, dst) in enumerate(items):
            for k, to in enumerate([sib] + [(*chip, me[2]) for chip in chips]):
                pltpu.make_async_remote_copy(
                    src_ref=src(S), dst_ref=dst(L, mi), send_sem=send_sems.at[4 * i + k],
                    recv_sem=recv_sems.at[4 * i + k], device_id=to, device_id_type=MESH).start()
        token[...] = jnp.zeros_like(token)

    thru = [pltpu.HBM(a.shape, a.dtype) for a in list(srcs) + list(lands)]
    args = [_in_hbm(a) for a in srcs] + [_in_hbm(a) for a in lands]
    in_specs = [HBM_ONLY] * (ns + nl)
    if dep is not None:
        args.append(dep)
        in_specs.append(HBM_SPEC)
    outs = pl.pallas_call(
        body, name=name, in_specs=in_specs,
        out_shape=(pltpu.SemaphoreType.DMA((4 * ni,)), pltpu.SemaphoreType.DMA((4 * ni,)), *thru, _sds((8, 128), F32)),
        out_specs=(SEM_SPEC, SEM_SPEC, *[HBM_ONLY] * (ns + nl), pl.BlockSpec(memory_space=pltpu.VMEM)),
        input_output_aliases={j: 2 + j for j in range(ns + nl)}, compiler_params=IN_FLIGHT)(*args)
    return dict(send=outs[0], recv=outs[1], srcs=list(outs[2:2 + ns]), lands=list(outs[2 + ns:2 + ns + nl]),
                token=outs[-1], items=items)


def _gather_forward(name, st, after):
    nl, ni = len(st["lands"]), len(st["items"])

    def body(*refs):
        L, recv_sems = refs[0:nl], refs[nl]
        fwd_send, fwd_recv, token = refs[-3:]
        me, sib, chips = _place_and_neighbours()
        for i, (_, dst) in enumerate(st["items"]):
            for j, chip in enumerate(chips):
                blk = dst(L, _dev_index(*chip, me[2]))
                pltpu.make_async_remote_copy(
                    src_ref=blk, dst_ref=blk, send_sem=fwd_send.at[3 * i + j], recv_sem=recv_sems.at[4 * i + 1 + j],
                    device_id=me, device_id_type=MESH).wait_recv()
                pltpu.make_async_remote_copy(
                    src_ref=blk, dst_ref=blk, send_sem=fwd_send.at[3 * i + j], recv_sem=fwd_recv.at[3 * i + j],
                    device_id=sib, device_id_type=MESH).start()
        token[...] = jnp.zeros_like(token)

    after = list(after) if isinstance(after, (list, tuple)) else [after]
    outs = pl.pallas_call(
        body, name=name, in_specs=[HBM_ONLY] * nl + [SEM_SPEC] + [HBM_SPEC] * len(after),
        out_shape=(*[pltpu.HBM(a.shape, a.dtype) for a in st["lands"]], pltpu.SemaphoreType.DMA((3 * ni,)),
                   pltpu.SemaphoreType.DMA((3 * ni,)), _sds((8, 128), F32)),
        out_specs=(*[HBM_ONLY] * nl, SEM_SPEC, SEM_SPEC, pl.BlockSpec(memory_space=pltpu.VMEM)),
        input_output_aliases={j: j for j in range(nl)}, compiler_params=IN_FLIGHT)(*st["lands"], st["recv"], *after)
    return dict(st, lands=list(outs[0:nl]), fwd_send=outs[nl], fwd_recv=outs[nl + 1], token=outs[-1])


def _gather_wait(name, st, after):
    ns, nl, ni = len(st["srcs"]), len(st["lands"]), len(st["items"])

    def body(*refs):
        S, L = refs[0:ns], refs[ns:ns + nl]
        send_sems, recv_sems, fwd_send, fwd_recv = refs[ns + nl:ns + nl + 4]
        me, sib, chips = _place_and_neighbours()
        mi = _dev_index(*me)
        for i, (src, dst) in enumerate(st["items"]):
            mine = dst(L, mi)
            for k in range(4):
                pltpu.make_async_remote_copy(
                    src_ref=src(S), dst_ref=mine, send_sem=send_sems.at[4 * i + k], recv_sem=recv_sems.at[4 * i + k],
                    device_id=me, device_id_type=MESH).wait_send()
            pltpu.make_async_remote_copy(
                src_ref=src(S), dst_ref=mine, send_sem=send_sems.at[4 * i], recv_sem=recv_sems.at[4 * i],
                device_id=me, device_id_type=MESH).wait_recv()
            for j in range(3):
                cp = pltpu.make_async_remote_copy(
                    src_ref=mine, dst_ref=mine, send_sem=fwd_send.at[3 * i + j], recv_sem=fwd_recv.at[3 * i + j],
                    device_id=me, device_id_type=MESH)
                cp.wait_send()
                cp.wait_recv()

    arrays = st["srcs"] + st["lands"]
    outs = pl.pallas_call(
        body, name=name, in_specs=[HBM_ONLY] * (ns + nl) + [SEM_SPEC] * 4 + [HBM_SPEC],
        out_shape=tuple(pltpu.HBM(a.shape, a.dtype) for a in arrays), out_specs=tuple([HBM_ONLY] * (ns + nl)),
        input_output_aliases={j: j for j in range(ns + nl)},
        compiler_params=IN_FLIGHT)(*arrays, st["send"], st["recv"], st["fwd_send"], st["fwd_recv"], after)
    return list(outs[ns:ns + nl])


def _sum_slots(land):
    def body(l_ref, o_ref):
        acc = l_ref[0]
        for d in range(1, NDEV):
            acc = acc + l_ref[d]
        o_ref[...] = acc

    vm = pl.BlockSpec(memory_space=pltpu.VMEM)
    return pl.pallas_call(body, name="sum_small", out_shape=_sds(land.shape[1:], F32), in_specs=[vm], out_specs=vm)(land)


def _adam_math(w, g, m, v):
    m2 = ADAM_B1 * m + (1.0 - ADAM_B1) * g
    v2 = ADAM_B2 * v + (1.0 - ADAM_B2) * (g * g)
    delta = -ADAM_LR * ((m2 * ADAM_C1) / (jnp.sqrt(v2 * ADAM_C2) + ADAM_EPS) + ADAM_WD * w)
    return delta, m2, v2


def _adam_layer(land, sel, w, m, v, layer, name, prev=None, tc=512):
    R = land.shape[2]

    def body(l_ref, w_ref, m_ref, v_ref, *rest):
        g_out, d_out, m_out, v_out = rest[-4:]
        g = l_ref[0].astype(F32)
        for d in range(1, NDEV):
            g = g + l_ref[d].astype(F32)
        delta, m2, v2 = _adam_math(w_ref[...], g, m_ref[...], v_ref[...])
        g_out[...] = g
        d_out[...] = delta
        m_out[...] = m2
        v_out[...] = v2

    wspec = pl.BlockSpec((None, R, tc), lambda i: (layer, 0, i))
    in_specs = [pl.BlockSpec((None, NDEV, R, tc), lambda i: (sel, 0, 0, i)), wspec, wspec, wspec]
    args = [land, w, m, v]
    aliases = {}
    if prev is not None:
        in_specs += [HBM_SPEC] * 4
        args += list(prev)
        aliases = {4 + j: j for j in range(4)}
    return pl.pallas_call(
        body, name=name, grid=(D // tc,), in_specs=in_specs, out_specs=[wspec] * 4,
        out_shape=[_sds(w.shape, F32)] * 4, input_output_aliases=aliases, compiler_params=_cparams(1))(*args)


def _adam_stacked(lands, sel, w, m, v, name):
    res = None
    for layer, land in enumerate(lands):
        res = _adam_layer(land, sel, w, m, v, layer, f"{name}{layer}", prev=res)
    return res


def _adam_small(ws, gs, ms, vs):
    n = len(ws)

    def body(*refs):
        w_r, g_r, m_r, v_r = refs[0:n], refs[n:2 * n], refs[2 * n:3 * n], refs[3 * n:4 * n]
        d_o, m_o, v_o = refs[4 * n:5 * n], refs[5 * n:6 * n], refs[6 * n:7 * n]
        for i in range(n):
            delta, m2, v2 = _adam_math(w_r[i][...], g_r[i][...], m_r[i][...], v_r[i][...])
            d_o[i][...] = delta
            m_o[i][...] = m2
            v_o[i][...] = v2

    vm = pl.BlockSpec(memory_space=pltpu.VMEM)
    shapes = [_sds(w.shape, F32) for w in ws]
    outs = pl.pallas_call(body, name="adam_small", in_specs=[vm] * (4 * n), out_specs=[vm] * (3 * n),
                          out_shape=shapes * 3)(*ws, *gs, *ms, *vs)
    return outs[0:n], outs[n:2 * n], outs[2 * n:3 * n]


def _adam_of_slots(land, w, m, v, name):
    def body(l_ref, w_ref, m_ref, v_ref, g_o, d_o, m_o, v_o):
        g = l_ref[0]
        for d in range(1, NDEV):
            g = g + l_ref[d]
        g_o[...] = g
        d_o[...], m_o[...], v_o[...] = _adam_math(w_ref[...], g, m_ref[...], v_ref[...])

    vm = pl.BlockSpec(memory_space=pltpu.VMEM)
    return pl.pallas_call(body, name=name, in_specs=[vm] * 4, out_specs=[vm] * 4,
                          out_shape=[_sds(w.shape, F32)] * 4)(land, w, m, v)


WEIGHT_NAMES = ("ab_norm_g", "ab_w_in", "sgu_norm_g", "sgu_norm_b", "sgu_w", "sgu_bias", "q_norm_g", "k_norm_g",
                "ab_w_out", "cd_norm_g", "cd_w_in", "conv_c_w", "conv_c_b", "c_ln_g", "c_ln_b", "conv_d_w",
                "cd_w_out", "ffn_norm_g", "ffn_w_gate", "ffn_w_up", "ffn_w_down")
SMALL_SHAPES = (("sgu_norm_g", (1, 512)), ("sgu_norm_b", (1, 512)), ("sgu_w", (512, 128)),
                ("sgu_bias", (4, 128)), ("q_norm_g", (3, 1, 64)), ("k_norm_g", (3, 1, 64)), ("cd_norm_g", (1, 128)),
                ("conv_c_w", (31, 1, 64)), ("conv_c_b", (1, 64)), ("c_ln_g", (1, 64)), ("c_ln_b", (1, 64)),
                ("conv_d_w", (3, 1, 64)), ("ffn_norm_g", (2, 1024)))
SHARD_C = 64


def _pack_rows(parts, rows):
    flat = jnp.concatenate([p.reshape(-1) for p in parts])
    return jnp.pad(flat, (0, rows * 128 - flat.shape[0])).reshape(rows, 128)


def kernel(x, ab_norm_g, ab_w_in, sgu_norm_g, sgu_norm_b, sgu_w, sgu_bias, q_norm_g, k_norm_g, ab_w_out, cd_norm_g, cd_w_in, conv_c_w, conv_c_b, c_ln_g, c_ln_b, conv_d_w, cd_w_out, ffn_norm_g, ffn_w_gate, ffn_w_up, ffn_w_down, loss_target, m_ab_norm_g, m_ab_w_in, m_sgu_norm_g, m_sgu_norm_b, m_sgu_w, m_sgu_bias, m_q_norm_g, m_k_norm_g, m_ab_w_out, m_cd_norm_g, m_cd_w_in, m_conv_c_w, m_conv_c_b, m_c_ln_g, m_c_ln_b, m_conv_d_w, m_cd_w_out, m_ffn_norm_g, m_ffn_w_gate, m_ffn_w_up, m_ffn_w_down, v_ab_norm_g, v_ab_w_in, v_sgu_norm_g, v_sgu_norm_b, v_sgu_w, v_sgu_bias, v_q_norm_g, v_k_norm_g, v_ab_w_out, v_cd_norm_g, v_cd_w_in, v_conv_c_w, v_conv_c_b, v_c_ln_g, v_c_ln_b, v_conv_d_w, v_cd_w_out, v_ffn_norm_g, v_ffn_w_gate, v_ffn_w_up, v_ffn_w_down):
    w = dict(zip(WEIGHT_NAMES, (ab_norm_g, ab_w_in, sgu_norm_g, sgu_norm_b, sgu_w, sgu_bias, q_norm_g, k_norm_g, ab_w_out, cd_norm_g, cd_w_in, conv_c_w, conv_c_b, c_ln_g, c_ln_b, conv_d_w, cd_w_out, ffn_norm_g, ffn_w_gate, ffn_w_up, ffn_w_down)))
    m = dict(zip(WEIGHT_NAMES, (m_ab_norm_g, m_ab_w_in, m_sgu_norm_g, m_sgu_norm_b, m_sgu_w, m_sgu_bias, m_q_norm_g, m_k_norm_g, m_ab_w_out, m_cd_norm_g, m_cd_w_in, m_conv_c_w, m_conv_c_b, m_c_ln_g, m_c_ln_b, m_conv_d_w, m_cd_w_out, m_ffn_norm_g, m_ffn_w_gate, m_ffn_w_up, m_ffn_w_down)))
    v = dict(zip(WEIGHT_NAMES, (v_ab_norm_g, v_ab_w_in, v_sgu_norm_g, v_sgu_norm_b, v_sgu_w, v_sgu_bias, v_q_norm_g, v_k_norm_g, v_ab_w_out, v_cd_norm_g, v_cd_w_in, v_conv_c_w, v_conv_c_b, v_c_ln_g, v_c_ln_b, v_conv_d_w, v_cd_w_out, v_ffn_norm_g, v_ffn_w_gate, v_ffn_w_up, v_ffn_w_down)))
    me = _dev_index(*_my_place())

    r_ff = DFF // NDEV
    one = lambda a: (lambda S, j: S[a])
    slot = lambda b: (lambda L, s: L[b].at[s])
    slot2 = lambda b, part: (lambda L, s: L[b].at[part, s])
    shard = lambda a: (lambda S: S[a])

    def later(a):
        return lax.optimization_barrier((a, gathers[0]["token"]))[0]

    def layer_shards(layer):
        return (later(w["ffn_w_gate"][layer]).T.astype(BF16), later(w["ffn_w_up"][layer]).T.astype(BF16),
                later(w["ffn_w_down"][layer]).astype(BF16))

    def gathered(own):
        return _landing((NDEV,) + own.shape, BF16, [((me,), own)])

    def gathered2(a, b):
        return _landing((2, NDEV) + a.shape, BF16, [((0, me), a), ((1, me), b)])

    ab_in_s = w["ab_w_in"][0].T.astype(BF16)
    gathers = {0: _gather_start("gather0_start", [ab_in_s], [gathered(ab_in_s)], [(shard(0), slot(0))])}
    prepared = {}

    def chan(flat, lo, taps):
        return flat[:, lo:lo + taps * SHARD_C].reshape(NDEV, taps, SHARD_C).transpose(1, 0, 2).reshape(taps, 512)

    def fetch(stage, after):
        if stage == "ab_in":
            ab_out_s = later(w["ab_w_out"][0]).astype(BF16)
            gate0, up0, down0 = layer_shards(0)
            small_s = _pack_rows([later(w[n]) for n in ("cd_norm_g", "conv_c_w", "conv_c_b", "c_ln_g", "c_ln_b",
                                                        "conv_d_w")], 24)
            lands1 = [gathered(ab_out_s), gathered2(gate0, up0), gathered(down0),
                      _landing((NDEV,) + small_s.shape, F32, [((me,), small_s)])]
            cd_in_s, cd_out_s = later(w["cd_w_in"][0]).T.astype(BF16), later(w["cd_w_out"][0]).astype(BF16)
            gate1, up1, down1 = layer_shards(1)
            prepared[2] = ([cd_in_s, cd_out_s, gate1, up1, down1],
                           [gathered(cd_in_s), gathered(cd_out_s), gathered2(gate1, up1), gathered(down1)])
            gathers[0] = _gather_forward("gather0_forward", gathers[0], [after] + lands1 + prepared[2][1])
            l_ab_in, = _gather_wait("gather0_wait", gathers[0], gathers[0]["token"])
            gathers[1] = _gather_start(
                "gather1_start", [ab_out_s, gate0, up0, down0, small_s], lands1,
                [(shard(0), slot(0)), (shard(1), slot2(1, 0)), (shard(2), slot2(1, 1)), (shard(3), slot(2)),
                 (shard(4), slot(3))], dep=l_ab_in)
            return {"wt_ab_in": l_ab_in.reshape(AB_IN, D), "dep0": gathers[1]["token"]}
        if stage == "attn0":
            gathers[2] = _gather_start(
                "gather2_start", *prepared[2],
                [(shard(0), slot(0)), (shard(1), slot(1)), (shard(2), slot2(2, 0)), (shard(3), slot2(2, 1)),
                 (shard(4), slot(3))], dep=after)
            return {"dep_attn1": gathers[2]["token"]}
        if stage == "attn1":
            gathers[1] = _gather_forward("gather1_forward", gathers[1], after)
            return {"dep_attn2": gathers[1]["token"]}
        if stage == "ab_out":
            l_out, l_ffn, l_down, l_small = _gather_wait("gather1_wait", gathers[1], after)
            flat = l_small.reshape(NDEV, 24 * 128)
            return {
                "w_ab_out": l_out.reshape(D, D), "wt_ffn_in0": l_ffn.reshape(2 * DFF, D),
                "w_ffn_down0": l_down.reshape(DFF, D), "cd_norm_g": flat[:, 0:128].reshape(1, D),
                "conv_c_w32": jnp.pad(chan(flat, 128, CONV_C_TAPS), ((0, 1), (0, 0))),
                "conv_c_b": chan(flat, 2112, 1), "c_ln_g": chan(flat, 2176, 1), "c_ln_b": chan(flat, 2240, 1),
                "conv_d_w8": jnp.pad(chan(flat, 2304, CONV_D_TAPS), ((0, 8 - CONV_D_TAPS), (0, 0))),
            }
        if stage == "ffn_down0":
            gathers[2] = _gather_forward("gather2_forward", gathers[2], after)
            return {"dep_down0": gathers[2]["token"]}
        if stage == "cd_in":
            l_in, l_out, l_ffn, l_down = _gather_wait("gather2_wait", gathers[2], after)
            return {"wt_cd_in": l_in.reshape(CD_IN, D), "w_cd_out": l_out.reshape(D, D),
                    "wt_ffn_in1": l_ffn.reshape(2 * DFF, D), "w_ffn_down1": l_down.reshape(DFF, D)}
        return {}

    scatters = {}
    rides_with = {"w_ffn_down1": "wt_ffn_in1", "w_cd_out": "wt_cd_in", "w_ffn_down0": "wt_ffn_in0"}
    held = {}
    smalls = {}

    def small_exchange(name, block):
        land = _landing((NDEV,) + block.shape, F32, [((me,), block)])
        return _exchange_start(name, [block], [land], [(one(0), slot(0))])

    def on_grad(key, arr):
        if key == "small":
            parts = [arr["sgu_norm_g"], arr["sgu_norm_b"], arr["sgu_w"], arr["sgu_bias"], arr["q_norm_g"],
                     arr["k_norm_g"], arr["cd_norm_g"], arr["conv_c_w32"][:CONV_C_TAPS], arr["conv_c_b"], arr["c_ln_g"],
                     arr["c_ln_b"], arr["conv_d_w8"][:CONV_D_TAPS], arr["ffn_norm_g0"], arr["ffn_norm_g1"],
                     arr["loss_cols"]]
            smalls["sizes"] = [p.size for p in parts]
            rows = -(-sum(smalls["sizes"]) // 1024) * 8
            smalls["early"] = small_exchange("small_start", _pack_rows(parts, rows))
            return smalls["early"]["token"]
        if key in rides_with:
            held[rides_with[key]] = (key, arr)
            return None
        group = ([held.pop(key)] if key in held else []) + [(key, arr)]
        srcs, lands, items = [], [], []
        for n, (k, a) in enumerate(group):
            if k.startswith("wt_ffn_in"):
                src = a.reshape(2, NDEV, r_ff, D)
                own = lax.dynamic_slice_in_dim(src, me, 1, axis=1)
                lands.append(lax.dynamic_update_slice(lax.empty(src.shape, BF16), own, (0, me, 0, 0)))
                items += [((lambda S, j, n=n: S[n].at[0, j]), slot2(n, 0)), ((lambda S, j, n=n: S[n].at[1, j]), slot2(n, 1))]
            else:
                rows = a.shape[0] // NDEV
                src = a.reshape(NDEV, rows, D)
                own = lax.dynamic_index_in_dim(src, me, 0, keepdims=False)
                lands.append(_landing((1, NDEV, rows, D), BF16, [((0, me), own)]))
                items.append(((lambda S, j, n=n: S[n].at[j]), slot2(n, 0)))
            srcs.append(src)
        st = _exchange_start(f"scatter_{key}_start", srcs, lands, items)
        scatters[key] = (st, [k for k, _ in group])
        return st["token"]

    W = {
        "dep_first": gathers[0]["token"],
        "ab_norm_g": w["ab_norm_g"], "sgu_norm_g": w["sgu_norm_g"], "sgu_norm_b": w["sgu_norm_b"],
        "sgu_w": w["sgu_w"][0], "sgu_bias": w["sgu_bias"][0], "q_norm_g": w["q_norm_g"][0],
        "k_norm_g": w["k_norm_g"][0], "ffn_norm_g": w["ffn_norm_g"],
    }

    loss_cols, grad_x, G = _local_step(x[0], loss_target[0], W, fetch, on_grad)

    late_small = small_exchange("small_late_start", G["ab_norm_g"])
    landed = {}

    def wait_scatters(name, group_keys, others, after):
        res = _exchange_wait(name, [scatters[gk][0] for gk in group_keys] + others, after)
        for gk, lands in zip(group_keys, res):
            landed.update(zip(scatters[gk][1], lands))
        return [lands[0] for lands in res[len(group_keys):]]

    small_land, = wait_scatters("scatter_wait_early", ["wt_ffn_in1", "wt_cd_in", "wt_ffn_in0", "w_ab_out"],
                                [smalls["early"]], late_small["token"])

    grads, deltas, new_m, new_v = {}, {}, {}, {}
    done = []

    def put(name, res):
        grads[name], deltas[name], new_m[name], new_v[name] = res

    def adam(name, lands, sel, transposed):
        flip = (lambda a: jnp.swapaxes(a, 1, 2)) if transposed else (lambda a: a)
        res = _adam_stacked(lands, sel, flip(w[name]), flip(m[name]), flip(v[name]), f"adam_{name}")
        done.append(res[1])
        put(name, [flip(r) for r in res])

    ffn_in_lands = [landed["wt_ffn_in0"], landed["wt_ffn_in1"]]
    adam("cd_w_in", [landed["wt_cd_in"]], 0, True)
    adam("ffn_w_gate", ffn_in_lands, 0, True)
    adam("ffn_w_up", ffn_in_lands, 1, True)
    adam("cd_w_out", [landed["w_cd_out"]], 0, False)
    adam("ab_w_out", [landed["w_ab_out"]], 0, False)
    adam("ffn_w_down", [landed["w_ffn_down0"], landed["w_ffn_down1"]], 0, False)

    red = _sum_slots(small_land).reshape(-1)
    offs = [0]
    for s in smalls["sizes"]:
        offs.append(offs[-1] + s)
    seg = [red[offs[i]:offs[i + 1]] for i in range(len(smalls["sizes"]))]
    loss = jnp.sum(seg[14])

    def own_channels(full, taps):
        return lax.dynamic_slice_in_dim(full.reshape(taps, 512), me * SHARD_C, SHARD_C, axis=1)

    g_small = {
        "sgu_norm_g": seg[0].reshape(1, 512), "sgu_norm_b": seg[1].reshape(1, 512),
        "sgu_w": seg[2].reshape(512, 128), "sgu_bias": seg[3].reshape(4, 128), "q_norm_g": seg[4].reshape(3, 64),
        "k_norm_g": seg[5].reshape(3, 64),
        "cd_norm_g": lax.dynamic_slice_in_dim(seg[6].reshape(1, D), me * (D // NDEV), D // NDEV, axis=1),
        "conv_c_w": own_channels(seg[7], CONV_C_TAPS), "conv_c_b": own_channels(seg[8], 1),
        "c_ln_g": own_channels(seg[9], 1), "c_ln_b": own_channels(seg[10], 1),
        "conv_d_w": own_channels(seg[11], CONV_D_TAPS),
        "ffn_norm_g": jnp.concatenate([seg[12].reshape(1, D), seg[13].reshape(1, D)], axis=0),
    }

    def small_in(s, a):
        return jnp.swapaxes(a, 0, 1) if len(s) == 3 else a.reshape(s)

    def small_out(n, s, a):
        return jnp.swapaxes(a, 0, 1) if len(s) == 3 else a.reshape(w[n].shape)

    g_in = [g_small[n].reshape(s) for n, s in SMALL_SHAPES]
    d_s, m_s, v_s = _adam_small([small_in(s, w[n]) for n, s in SMALL_SHAPES], g_in,
                                [small_in(s, m[n]) for n, s in SMALL_SHAPES],
                                [small_in(s, v[n]) for n, s in SMALL_SHAPES])
    for i, (n, s) in enumerate(SMALL_SHAPES):
        grads[n], deltas[n] = small_out(n, s, g_in[i]), small_out(n, s, d_s[i])
        new_m[n], new_v[n] = small_out(n, s, m_s[i]), small_out(n, s, v_s[i])
    done.append(d_s[0])

    late_land, = wait_scatters("scatter_wait_last", ["wt_ab_in"], [late_small], list(done))
    put("ab_norm_g", _adam_of_slots(late_land, w["ab_norm_g"], m["ab_norm_g"], v["ab_norm_g"], "adam_ab_norm_g"))
    adam("ab_w_in", [landed["wt_ab_in"]], 0, True)

    return (loss, grad_x[None], *[grads[n] for n in WEIGHT_NAMES], *[deltas[n] for n in WEIGHT_NAMES],
            *[new_m[n] for n in WEIGHT_NAMES], *[new_v[n] for n in WEIGHT_NAMES])
```

```python
import jax
import jax.numpy as jnp
import numpy as np
from jax import lax
from jax.experimental import pallas as pl
from jax.experimental.pallas import tpu as pltpu

F32 = jnp.float32
BF16 = jnp.bfloat16

T = 4096
D = 1024
NDEV = 8
EPS = 1e-6
NEG_INF = -1e30
DFF = 2816
AB_IN = 5632
CD_IN = 2560
HEAD = 64
PAIR = 128
NPAIR = 4
NBACK = 128
DIL_RATES = (1, 4, 16)
ROPE_HALF = 8
ROPE_THETA = 500000.0
CONV_C_TAPS = 31
CONV_D_TAPS = 3
HALO = 32
ATTN_BWD_UNROLL = 4
MAX_ROW_STRIDE = 4

ADAM_LR = 0.001
ADAM_B1 = 0.9
ADAM_B2 = 0.999
ADAM_EPS = 1e-08
ADAM_WD = 0.01
ADAM_STEP = 10
ADAM_C1 = 1.0 / (1.0 - ADAM_B1 ** ADAM_STEP)
ADAM_C2 = 1.0 / (1.0 - ADAM_B2 ** ADAM_STEP)

VMEM_LIMIT_MB = 48
MESH = pl.DeviceIdType.MESH
HBM_SPEC = pl.BlockSpec(memory_space=pl.ANY)


def _cparams(ngrid, vmem_mb=VMEM_LIMIT_MB):
    return pltpu.CompilerParams(dimension_semantics=("arbitrary",) * ngrid,
                                vmem_limit_bytes=vmem_mb * 1024 * 1024)


def _pick(n, options):
    for o in options:
        if n % o == 0:
            return o
    raise ValueError(f"no tile for {n} in {options}")


def _sds(shape, dtype):
    return jax.ShapeDtypeStruct(shape, dtype)


def _sigmoid(x):
    return 1.0 / (1.0 + jnp.exp(-x))


def _sigmoid_bf16(x):
    return 0.5 * jnp.tanh(0.5 * x) + 0.5


def _gelu(z):
    return 0.5 * z * (1.0 + lax.erf(z * 0.7071067811865476))


def _gelu_grad(z):
    return 0.5 * (1.0 + lax.erf(z * 0.7071067811865476)) + z * jnp.exp(-0.5 * z * z) * 0.3989422804014327


def _mm_nt(a, wt, name, out_dtype=BF16, dep=None):
    M, K = a.shape
    N = wt.shape[0]
    tn = _pick(N, (512, 256))

    def body(a_ref, w_ref, *rest):
        o_ref = rest[-1]
        for r0 in range(0, M, 1024):
            o_ref[r0:r0 + 1024, :] = lax.dot_general(
                a_ref[r0:r0 + 1024, :], w_ref[...], (((1,), (1,)), ((), ())),
                preferred_element_type=F32).astype(o_ref.dtype)

    in_specs = [pl.BlockSpec((M, K), lambda j: (0, 0), pipeline_mode=pl.Buffered(1)),
                pl.BlockSpec((tn, K), lambda j: (j, 0))]
    args = [a, wt]
    if dep is not None:
        in_specs.append(HBM_SPEC)
        args.append(dep)
    return pl.pallas_call(
        body, name=name, grid=(N // tn,), in_specs=in_specs, out_specs=pl.BlockSpec((M, tn), lambda j: (0, j)),
        out_shape=_sds((M, N), out_dtype), compiler_params=_cparams(1))(*args)


EPI_ROWS = 256


def _mm_nt_rows(a, wt, name, epilogue, side, side_specs, out_specs, out_shape, sums=(), dep=None, tm=512):
    M, K = a.shape
    N = wt.shape[0]
    ns, no = len(side), len(out_shape)

    def body(a_ref, w_ref, *rest):
        side_refs, outs, acc = rest[0:ns], rest[-1 - no:-1], rest[-1]
        acc[...] = lax.dot_general(a_ref[...], w_ref[...], (((1,), (1,)), ((), ())), preferred_element_type=F32)

        @pl.when(pl.program_id(0) == 0)
        def _():
            for j in sums:
                outs[j][...] = jnp.zeros_like(outs[j])

        for r0 in range(0, tm, EPI_ROWS):
            rows = slice(r0, r0 + EPI_ROWS)
            epilogue(acc[rows, :].astype(BF16).astype(F32), rows, side_refs, outs)

    in_specs = [pl.BlockSpec((tm, K), lambda i: (i, 0)),
                pl.BlockSpec((N, K), lambda i: (0, 0), pipeline_mode=pl.Buffered(1))] + list(side_specs)
    args = [a, wt, *side]
    if dep is not None:
        in_specs.append(HBM_SPEC)
        args.append(dep)
    return pl.pallas_call(
        body, name=name, grid=(M // tm,), in_specs=in_specs, out_specs=list(out_specs), out_shape=list(out_shape),
        scratch_shapes=[pltpu.VMEM((tm, N), F32)], compiler_params=_cparams(1))(*args)


def _mm_nn(a, w, name, mode, resid, gain=None, tgt=None, dep=None, tm=512):
    M, K = a.shape
    N = w.shape[1]
    side = gain if mode == "rms" else tgt

    def body(a_ref, w_ref, resid_ref, side_ref, *rest):
        outs, acc = rest[-3 if mode == "rms" else -4:-1], rest[-1]
        i = pl.program_id(0)
        acc[...] = jnp.dot(a_ref[...], w_ref[...], preferred_element_type=F32)

        if mode == "loss":
            @pl.when(i == 0)
            def _():
                outs[2][...] = jnp.zeros_like(outs[2])

        for r0 in range(0, tm, EPI_ROWS):
            rows = slice(r0, r0 + EPI_ROWS)
            v = acc[rows, :] + resid_ref[rows, :]
            if mode == "rms":
                outs[0][rows, :] = v
                r = lax.rsqrt(jnp.mean(v * v, axis=-1, keepdims=True) + EPS)
                outs[1][rows, :] = (v * r * side_ref[...]).astype(BF16)
            else:
                d = v - side_ref[rows, :]
                outs[2][...] += jnp.sum(d * d, axis=0, keepdims=True) * (0.5 / N)
                dy = d * (1.0 / N)
                outs[0][rows, :] = dy
                outs[1][rows, :] = dy.astype(BF16)

    row = pl.BlockSpec((tm, N), lambda i: (i, 0))
    vec = pl.BlockSpec((1, N), lambda i: (0, 0))
    in_specs = [pl.BlockSpec((tm, K), lambda i: (i, 0)),
                pl.BlockSpec((K, N), lambda i: (0, 0), pipeline_mode=pl.Buffered(1)), row,
                vec if mode == "rms" else row]
    args = [a, w, resid, side]
    if dep is not None:
        in_specs.append(HBM_SPEC)
        args.append(dep)
    if mode == "rms":
        out_specs, out_shape = [row, row], [_sds((M, N), F32), _sds((M, N), BF16)]
    else:
        out_specs, out_shape = [row, row, vec], [_sds((M, N), F32), _sds((M, N), BF16), _sds((1, N), F32)]
    return pl.pallas_call(
        body, name=name, grid=(M // tm,), in_specs=in_specs, out_specs=out_specs, out_shape=out_shape,
        scratch_shapes=[pltpu.VMEM((tm, N), F32)], compiler_params=_cparams(1))(*args)


def _mm_dh_rms_bwd(a, w, x, gain, dres, name, dep=None, tm=512, bf16_copy=True):
    parts = a.shape[0] if a.ndim == 3 else 1
    M, Kp = a.shape[-2], a.shape[-1]
    N = w.shape[1]
    nblk = M // tm
    assert nblk % 2 == 0

    def body(a_ref, w_ref, x_ref, g_ref, dres_ref, *rest):
        dg_ref, acc0, acc1 = rest[-3:]
        dx_ref = rest[-5] if bf16_copy else rest[-4]
        dxb_ref = rest[-4] if bf16_copy else None
        i = pl.program_id(0)

        def matmul(acc):
            if parts == 1:
                acc[...] = jnp.dot(a_ref[...], w_ref[...], preferred_element_type=F32)
            else:
                d = jnp.dot(a_ref[0], w_ref[0:Kp, :], preferred_element_type=F32)
                for p in range(1, parts):
                    d = d + jnp.dot(a_ref[p], w_ref[p * Kp:(p + 1) * Kp, :], preferred_element_type=F32)
                acc[...] = d

        def finish(acc):
            for r0 in range(0, tm, EPI_ROWS // 2):
                rows = slice(r0, r0 + EPI_ROWS // 2)
                v = acc[rows, :]
                xf = x_ref[rows, :]
                r = lax.rsqrt(jnp.mean(xf * xf, axis=-1, keepdims=True) + EPS)
                xhat = xf * r
                dg_ref[...] += jnp.sum(v * xhat, axis=0, keepdims=True)
                dxh = v * g_ref[...]
                tot = dres_ref[rows, :] + r * (dxh - xhat * jnp.mean(dxh * xhat, axis=-1, keepdims=True))
                dx_ref[rows, :] = tot
                if bf16_copy:
                    dxb_ref[rows, :] = tot.astype(BF16)

        @pl.when(i == 0)
        def _():
            dg_ref[...] = jnp.zeros_like(dg_ref)
            matmul(acc0)

        @pl.when((i > 0) & (i < nblk) & (i % 2 == 1))
        def _():
            matmul(acc1)
            finish(acc0)

        @pl.when((i > 0) & (i < nblk) & (i % 2 == 0))
        def _():
            matmul(acc0)
            finish(acc1)

        @pl.when(i == nblk)
        def _():
            finish(acc1)

    last = nblk - 1
    row = pl.BlockSpec((tm, N), lambda i: (jnp.maximum(i - 1, 0), 0))
    vec = pl.BlockSpec((1, N), lambda i: (0, 0))
    if a.ndim == 3:
        a_spec = pl.BlockSpec((parts, tm, Kp), lambda i: (0, jnp.minimum(i, last), 0))
    else:
        a_spec = pl.BlockSpec((tm, Kp), lambda i: (jnp.minimum(i, last), 0))
    w_spec = pl.BlockSpec((parts * Kp, N), lambda i: (0, 0), pipeline_mode=pl.Buffered(1))
    in_specs = [a_spec, w_spec, row, vec, row]
    args = [a, w, x, gain, dres]
    if dep is not None:
        in_specs.append(HBM_SPEC)
        args.append(dep)
    return pl.pallas_call(
        body, name=name, grid=(nblk + 1,), in_specs=in_specs,
        out_specs=[row, row, vec] if bf16_copy else [row, vec],
        out_shape=([_sds((M, N), F32), _sds((M, N), BF16), _sds((1, N), F32)] if bf16_copy
                   else [_sds((M, N), F32), _sds((1, N), F32)]),
        scratch_shapes=[pltpu.VMEM((tm, N), F32), pltpu.VMEM((tm, N), F32)], compiler_params=_cparams(1, 56))(*args)


def _mm_tn(a, b, name, out_dtype=BF16, tt=2048, dep=None):
    parts = a.shape[0] if a.ndim == 3 else 1
    Tt, Mp = a.shape[-2], a.shape[-1]
    N = b.shape[1]
    tn = _pick(Mp, (1408, 1280, 1024, 512))
    jper = Mp // tn
    nt = Tt // tt

    def body(a_ref, b_ref, *rest):
        o_ref, acc = rest[-2:]
        t = pl.program_id(1)

        @pl.when(t == 0)
        def _():
            acc[...] = jnp.zeros_like(acc)

        rows = pl.ds(pl.multiple_of(t * tt, tt), tt)
        acc[...] += lax.dot_general(a_ref[...], b_ref[rows, :], (((0,), (0,)), ((), ())),
                                    preferred_element_type=F32)

        @pl.when(t == nt - 1)
        def _():
            o_ref[...] = acc[...].astype(o_ref.dtype)

    if a.ndim == 3:
        a_spec = pl.BlockSpec((None, tt, tn), lambda j, t: (j // jper, t, j % jper))
    else:
        a_spec = pl.BlockSpec((tt, tn), lambda j, t: (t, j))
    in_specs = [a_spec, pl.BlockSpec((Tt, N), lambda j, t: (0, 0), pipeline_mode=pl.Buffered(1))]
    args = [a, b]
    if dep is not None:
        in_specs.append(HBM_SPEC)
        args.append(dep)
    return pl.pallas_call(
        body, name=name, grid=(parts * jper, nt), in_specs=in_specs,
        out_specs=pl.BlockSpec((tn, N), lambda j, t: (j, 0)),
        out_shape=_sds((parts * Mp, N), out_dtype), scratch_shapes=[pltpu.VMEM((tn, N), F32)],
        compiler_params=_cparams(2))(*args)


FFN_ROWS = 256


def _ffn_in(h, wt_in, name, tn=256):
    nj = DFF // tn

    def body(h_ref, wg_ref, wu_ref, p_ref, act_ref):
        nt = (((1,), (1,)), ((), ()))
        for r0 in range(0, T, FFN_ROWS):
            rows = slice(r0, r0 + FFN_ROWS)
            g = lax.dot_general(h_ref[rows, :], wg_ref[...], nt, preferred_element_type=F32).astype(BF16)
            u = lax.dot_general(h_ref[rows, :], wu_ref[...], nt, preferred_element_type=F32).astype(BF16)
            p_ref[0, rows, :] = g
            p_ref[1, rows, :] = u
            act_ref[rows, :] = g * _sigmoid_bf16(g) * u

    return pl.pallas_call(
        body, name=name, grid=(nj,),
        in_specs=[pl.BlockSpec((T, D), lambda j: (0, 0), pipeline_mode=pl.Buffered(1)),
                  pl.BlockSpec((tn, D), lambda j: (j, 0)), pl.BlockSpec((tn, D), lambda j: (j + nj, 0))],
        out_specs=[pl.BlockSpec((2, T, tn), lambda j: (0, 0, j)), pl.BlockSpec((T, tn), lambda j: (0, j))],
        out_shape=[_sds((2, T, DFF), BF16), _sds((T, DFF), BF16)], compiler_params=_cparams(1))(h, wt_in, wt_in)


def _ffn_dact(dyb, w_down, p3, name, tn=256, dep=None):
    def body(dy_ref, w_ref, p_ref, *rest):
        o_ref = rest[-1]
        for r0 in range(0, T, FFN_ROWS):
            rows = slice(r0, r0 + FFN_ROWS)
            da = lax.dot_general(dy_ref[rows, :], w_ref[...], (((1,), (1,)), ((), ())),
                                 preferred_element_type=F32).astype(BF16)
            g = p_ref[0, rows, :]
            u = p_ref[1, rows, :]
            sg = _sigmoid_bf16(g)
            gs = g * sg
            o_ref[0, rows, :] = (da * u) * (sg + gs * (1.0 - sg))
            o_ref[1, rows, :] = da * gs

    pspec = pl.BlockSpec((2, T, tn), lambda j: (0, 0, j))
    in_specs = [pl.BlockSpec((T, D), lambda j: (0, 0), pipeline_mode=pl.Buffered(1)),
                pl.BlockSpec((tn, D), lambda j: (j, 0)), pspec]
    args = [dyb, w_down, p3]
    if dep is not None:
        in_specs.append(HBM_SPEC)
        args.append(dep)
    return pl.pallas_call(
        body, name=name, grid=(DFF // tn,), in_specs=in_specs, out_specs=pspec,
        out_shape=_sds((2, T, DFF), BF16), compiler_params=_cparams(1))(*args)


def _rms_fwd(x, g, name, tm=512, dep=None):
    def body(x_ref, g_ref, *rest):
        h_ref = rest[-1]
        xf = x_ref[...]
        r = lax.rsqrt(jnp.mean(xf * xf, axis=-1, keepdims=True) + EPS)
        h_ref[...] = (xf * r * g_ref[...]).astype(BF16)

    in_specs = [pl.BlockSpec((tm, D), lambda i: (i, 0)), pl.BlockSpec((1, D), lambda i: (0, 0))]
    args = [x, g]
    if dep is not None:
        in_specs.append(HBM_SPEC)
        args.append(dep)
    return pl.pallas_call(
        body, name=name, grid=(T // tm,), in_specs=in_specs, out_specs=pl.BlockSpec((tm, D), lambda i: (i, 0)),
        out_shape=_sds((T, D), BF16), compiler_params=_cparams(1))(*args)


def _tril_mask():
    r = lax.broadcasted_iota(jnp.int32, (128, 128), 0)
    c = lax.broadcasted_iota(jnp.int32, (128, 128), 1)
    return r >= c


def _mix_a_fwd(pab, sgu_g, sgu_b, sgu_w, sgu_bias3, tm=512):
    def body(zu_ref, zv_ref, g_ref, b_ref, w_ref, bias_ref, o_ref):
        u = _gelu(zu_ref[...].astype(F32))
        v = _gelu(zv_ref[...].astype(F32))
        mu = jnp.mean(v, axis=-1, keepdims=True)
        vc = v - mu
        rstd = lax.rsqrt(jnp.mean(vc * vc, axis=-1, keepdims=True) + EPS)
        vn = (vc * rstd * g_ref[...] + b_ref[...]).astype(BF16)
        tri = _tril_mask()
        for gi in range(4):
            wg = jnp.where(tri, w_ref[gi], 0.0).astype(BF16)
            bg = bias_ref[gi]
            for c in range(tm // 128):
                rs, cs = slice(c * 128, (c + 1) * 128), slice(gi * 128, (gi + 1) * 128)
                mixed = jnp.dot(wg, vn[rs, cs], preferred_element_type=F32) + bg
                o_ref[rs, cs] = (u[rs, cs] * mixed).astype(BF16)

    half = pl.BlockSpec((tm, 512), lambda i: (i, 0))
    return pl.pallas_call(
        body, name="mix_a_fwd", grid=(T // tm,),
        in_specs=[half, pl.BlockSpec((tm, 512), lambda i: (i, 1)),
                  pl.BlockSpec((1, 512), lambda i: (0, 0)), pl.BlockSpec((1, 512), lambda i: (0, 0)),
                  pl.BlockSpec((4, 128, 128), lambda i: (0, 0, 0)), pl.BlockSpec((4, 128, 1), lambda i: (0, 0, 0))],
        out_specs=half, out_shape=_sds((T, D), BF16), compiler_params=_cparams(1),
    )(pab, pab, sgu_g, sgu_b, sgu_w, sgu_bias3)


def _rope_tables():
    pos = np.arange(T, dtype=np.float32)
    inv_freq = np.float32(ROPE_THETA) ** (-np.arange(ROPE_HALF, dtype=np.float32) * np.float32(2.0 / (2 * ROPE_HALF)))
    ang = (pos[:, None] * inv_freq[None, :]).astype(np.float32)
    cos, sin = np.cos(ang), np.sin(ang)
    z8 = np.zeros((T, ROPE_HALF), np.float32)
    rest = np.zeros((T, HEAD - 2 * ROPE_HALF), np.float32)
    c64 = np.concatenate([cos, cos, rest + 1.0], axis=1)
    s1 = np.concatenate([z8, sin, rest], axis=1)
    s2 = np.concatenate([-sin, z8, rest], axis=1)
    return tuple(jnp.asarray(np.tile(t, (1, 2)).astype(np.float32)) for t in (c64, s1, s2))


def _lo_mask(shape):
    return lax.broadcasted_iota(jnp.int32, shape, 1) < HEAD


def _seg_mean(x, lo):
    s_all = jnp.sum(x, axis=-1, keepdims=True)
    s_lo = jnp.sum(jnp.where(lo, x, 0.0), axis=-1, keepdims=True)
    return jnp.where(lo, s_lo, s_all - s_lo) * (1.0 / HEAD)


def _head_blocks():
    r = lax.broadcasted_iota(jnp.int32, (PAIR, PAIR), 0) < HEAD
    c = lax.broadcasted_iota(jnp.int32, (PAIR, PAIR), 1) < HEAD
    return jnp.where(r == c, 1.0, 0.0).astype(BF16)


def _seg_mean_mxu(x, blocks):
    return jnp.dot(x.astype(BF16), blocks, preferred_element_type=F32) * (1.0 / HEAD)


def _rope(n, c, s1, s2):
    return n * c + pltpu.roll(n, ROPE_HALF, 1) * s1 + pltpu.roll(n, PAIR - ROPE_HALF, 1) * s2


def _rope_t(dy, c, s1, s2):
    return dy * c - pltpu.roll(dy, PAIR - ROPE_HALF, 1) * s2 - pltpu.roll(dy, ROPE_HALF, 1) * s1


def _prep_fwd(pab, qg, kg, tabs, tm=512):
    def body(p_ref, qg_ref, kg_ref, c_ref, s1_ref, s2_ref, *outs):
        blocks = _head_blocks()
        c, s1, s2 = c_ref[...], s1_ref[...], s2_ref[...]
        for g in range(3):
            qn_ref, kn_ref, v_ref = outs[3 * g:3 * g + 3]
            for p in range(NPAIR):
                for which, gains, dst in ((0, qg_ref, qn_ref), (1, kg_ref, kn_ref)):
                    col = (2 + 3 * which + g) * 512 + p * PAIR
                    xr = p_ref[:, col:col + PAIR].astype(F32)
                    rinv = lax.rsqrt(_seg_mean_mxu(xr * xr, blocks) + EPS)
                    outs[9 + 2 * g + which][p] = rinv.astype(BF16)
                    dst[p] = _rope(xr * rinv * gains[g:g + 1, :], c, s1, s2)
                col = (8 + g) * 512 + p * PAIR
                v_ref[p] = p_ref[:, col:col + PAIR].astype(F32)

    pm = pl.BlockSpec((NPAIR, tm, PAIR), lambda i: (0, i, 0))
    tab = pl.BlockSpec((tm, PAIR), lambda i: (i, 0))
    gain = pl.BlockSpec((3, PAIR), lambda i: (0, 0))
    res = pl.pallas_call(
        body, name="prep_fwd", grid=(T // tm,),
        in_specs=[pl.BlockSpec((tm, AB_IN), lambda i: (i, 0)), gain, gain, tab, tab, tab],
        out_specs=[pm] * 15, out_shape=[_sds((NPAIR, T, PAIR), F32)] * 9 + [_sds((NPAIR, T, PAIR), BF16)] * 6,
        compiler_params=_cparams(1))(pab, qg, kg, *tabs)
    return res[0:9], res[9:15]


def _res_index(it, rate):
    window = NBACK * rate
    b = it // rate
    rho = it % rate
    start = b * window + rho
    startp = jnp.maximum(start - window, rho)
    kmin = jnp.where(b > 0, 0, NBACK)
    return start, startp, kmin


def _rows(start, rate):
    if rate == 1:
        return pl.ds(pl.multiple_of(start, NBACK), NBACK)
    return pl.ds(start, NBACK, stride=rate)


def _band_bias():
    qs = lax.broadcasted_iota(jnp.int32, (2 * NBACK, 2 * NBACK), 0)
    kj = lax.broadcasted_iota(jnp.int32, (2 * NBACK, 2 * NBACK), 1)
    dist = (qs & (NBACK - 1)) + NBACK - kj
    both = (dist >= 0) & (dist <= NBACK)
    return jnp.where(both, 0.0, NEG_INF), jnp.where(both & (kj >= NBACK), 0.0, NEG_INF)


def _attn_fwd_block(q, kcat, vcat, first, lo, biases):
    vcat1 = jnp.concatenate([vcat, jnp.ones((2 * NBACK, PAIR), BF16)], axis=1)
    q2 = jnp.concatenate([jnp.where(lo, q, 0.0), jnp.where(lo, 0.0, q)], axis=0).astype(BF16)
    s = lax.dot_general(q2, kcat, (((1,), (1,)), ((), ())), preferred_element_type=F32)
    s = s + jnp.where(first, biases[1], biases[0])
    m = jnp.max(s, axis=-1, keepdims=True)
    ol = jnp.dot(jnp.exp(s - m).astype(BF16), vcat1, preferred_element_type=F32)
    o2 = ol[:, 0:PAIR] / ol[:, PAIR:]
    ls = m + jnp.log(ol[:, PAIR:])
    return jnp.where(lo, o2[0:NBACK], o2[NBACK:]), jnp.where(lo, ls[0:NBACK], ls[NBACK:])


def _attn_fwd(qn, kn, v, rate, name, dep=None):
    if rate > MAX_ROW_STRIDE:
        return _attn_fwd_gathered(qn, kn, v, rate, name, dep)

    def body(q_ref, k_ref, v_ref, *rest):
        o_ref, l_ref = rest[-2:]
        lo = _lo_mask((NBACK, PAIR))
        biases = _band_bias()

        def step(it, carry):
            start, startp, kmin = _res_index(it, rate)
            q = q_ref[_rows(start, rate), :] * (HEAD ** -0.5)
            kcat = jnp.concatenate([k_ref[_rows(startp, rate), :], k_ref[_rows(start, rate), :]], axis=0).astype(BF16)
            vcat = jnp.concatenate([v_ref[_rows(startp, rate), :], v_ref[_rows(start, rate), :]], axis=0).astype(BF16)
            o, ls = _attn_fwd_block(q, kcat, vcat, kmin != 0, lo, biases)
            o_ref[_rows(start, rate), :] = o
            l_ref[_rows(start, rate), :] = ls
            return carry

        lax.fori_loop(0, T // NBACK, step, 0, unroll=4)

    pm = pl.BlockSpec((None, T, PAIR), lambda p: (p, 0, 0))
    in_specs, args = [pm, pm, pm], [qn, kn, v]
    if dep is not None:
        in_specs.append(HBM_SPEC)
        args.append(dep)
    return pl.pallas_call(
        body, name=name, grid=(NPAIR,), in_specs=in_specs, out_specs=[pm, pm],
        out_shape=[_sds((NPAIR, T, PAIR), F32)] * 2, compiler_params=_cparams(1))(*args)


def _attn_fwd_gathered(qn, kn, v, rate, name, dep):
    n = T // rate
    nblk = n // NBACK

    def body(q_hbm, k_hbm, v_hbm, *rest):
        o_hbm, l_hbm, qb, kb, vb, ob, lb, in_sem, out_sem = rest[-9:]
        p = pl.program_id(0)
        slot = p % 2

        def loads(pair, s):
            return [pltpu.make_async_copy(x.at[pair, :, r, :], buf.at[s, r], in_sem.at[3 * s + a])
                    for a, (x, buf) in enumerate(((q_hbm, qb), (k_hbm, kb), (v_hbm, vb))) for r in range(rate)]

        def stores(pair, s):
            return [pltpu.make_async_copy(buf.at[s, r], x.at[pair, :, r, :], out_sem.at[2 * s + a])
                    for a, (x, buf) in enumerate(((o_hbm, ob), (l_hbm, lb))) for r in range(rate)]

        @pl.when(p == 0)
        def _():
            for c in loads(0, 0):
                c.start()

        @pl.when(p + 1 < NPAIR)
        def _():
            for c in loads(p + 1, 1 - slot):
                c.start()

        for c in loads(p, slot):
            c.wait()

        @pl.when(p >= 2)
        def _():
            for c in stores(p - 2, slot):
                c.wait()

        lo = _lo_mask((NBACK, PAIR))
        biases = _band_bias()

        def step(it, carry):
            b, r = it % nblk, it // nblk
            cur = pl.ds(pl.multiple_of(b * NBACK, NBACK), NBACK)
            prev = pl.ds(pl.multiple_of(jnp.maximum(b - 1, 0) * NBACK, NBACK), NBACK)
            q = qb[slot, r, cur, :] * (HEAD ** -0.5)
            kcat = jnp.concatenate([kb[slot, r, prev, :], kb[slot, r, cur, :]], axis=0).astype(BF16)
            vcat = jnp.concatenate([vb[slot, r, prev, :], vb[slot, r, cur, :]], axis=0).astype(BF16)
            o, ls = _attn_fwd_block(q, kcat, vcat, b == 0, lo, biases)
            ob[slot, r, cur, :] = o
            lb[slot, r, cur, :] = ls
            return carry

        lax.fori_loop(0, T // NBACK, step, 0, unroll=4)

        for c in stores(p, slot):
            c.start()

        @pl.when(p == NPAIR - 1)
        def _():
            for c in stores(p - 1, 1 - slot) + stores(p, slot):
                c.wait()

    by_residue = lambda a: a.reshape(NPAIR, n, rate, PAIR)
    in_specs, args = [HBM_SPEC] * 3, [by_residue(qn), by_residue(kn), by_residue(v)]
    if dep is not None:
        in_specs.append(HBM_SPEC)
        args.append(dep)
    o, l = pl.pallas_call(
        body, name=name, grid=(NPAIR,), in_specs=in_specs, out_specs=[HBM_SPEC] * 2,
        out_shape=[_sds((NPAIR, n, rate, PAIR), F32)] * 2,
        scratch_shapes=[pltpu.VMEM((2, rate, n, PAIR), F32)] * 5
        + [pltpu.SemaphoreType.DMA((6,)), pltpu.SemaphoreType.DMA((4,))],
        compiler_params=_cparams(1))(*args)
    return o.reshape(NPAIR, T, PAIR), l.reshape(NPAIR, T, PAIR)


def _merge_fwd(cat_ab, outs, lses, tm=512):
    def body(cat_in, o0, o1, o2, l0, l1, l2, cat_ref, lse_ref):
        del cat_in
        for p in range(NPAIR):
            a0, a1, a2 = l0[p], l1[p], l2[p]
            m = jnp.maximum(jnp.maximum(a0, a1), a2)
            w0, w1, w2 = jnp.exp(a0 - m), jnp.exp(a1 - m), jnp.exp(a2 - m)
            s = w0 + w1 + w2
            b = (w0 * o0[p] + w1 * o1[p] + w2 * o2[p]) / s
            cat_ref[:, p * PAIR:(p + 1) * PAIR] = b.astype(BF16)
            lse_ref[p] = m + jnp.log(s)

    pm = pl.BlockSpec((NPAIR, tm, PAIR), lambda i: (0, i, 0))
    return pl.pallas_call(
        body, name="merge_fwd", grid=(T // tm,),
        in_specs=[pl.BlockSpec(memory_space=pl.ANY)] + [pm] * 6,
        out_specs=[pl.BlockSpec((tm, 512), lambda i: (i, 1)), pm],
        out_shape=[_sds((T, D), BF16), _sds((NPAIR, T, PAIR), F32)],
        input_output_aliases={0: 0}, compiler_params=_cparams(1))(cat_ab, *outs, *lses)


def _d_cat_ab(dxb, w_ab_out, cat, dep, tm=512):
    def epilogue(d, rows, side, outs):
        (b_ref,), (da_ref, dbp_ref, e_ref) = side, outs
        da_ref[rows, :] = d[:, 0:512].astype(BF16)
        lo = _lo_mask((EPI_ROWS, PAIR))
        for p in range(NPAIR):
            db = d[:, 512 + p * PAIR:512 + (p + 1) * PAIR]
            b = b_ref[rows, p * PAIR:(p + 1) * PAIR].astype(F32)
            dbp_ref[p, rows, :] = db
            e_ref[p, rows, :] = _seg_mean(db * b, lo) * float(HEAD)

    pm = pl.BlockSpec((NPAIR, tm, PAIR), lambda i: (0, i, 0))
    return _mm_nt_rows(
        dxb, w_ab_out, "mm_d_cat_ab", epilogue, [cat], [pl.BlockSpec((tm, 512), lambda i: (i, 1))],
        [pl.BlockSpec((tm, 512), lambda i: (i, 0)), pm, pm],
        [_sds((T, 512), BF16), _sds((NPAIR, T, PAIR), F32), _sds((NPAIR, T, PAIR), F32)], dep=dep, tm=tm)


def _attn_bwd_block(q, db, ev, ls, kcat, vcat, first, lo, biases):
    scale = HEAD ** -0.5
    nt = (((1,), (1,)), ((), ()))
    tn = (((0,), (0,)), ((), ()))
    q = q * scale
    q2 = jnp.concatenate([jnp.where(lo, q, 0.0), jnp.where(lo, 0.0, q)], axis=0).astype(BF16)
    db2 = jnp.concatenate([jnp.where(lo, db, 0.0), jnp.where(lo, 0.0, db)], axis=0).astype(BF16)
    ls2 = jnp.concatenate([ls[:, 0:1], ls[:, HEAD:HEAD + 1]], axis=0)
    ev2 = jnp.concatenate([ev[:, 0:1], ev[:, HEAD:HEAD + 1]], axis=0)
    s = lax.dot_general(q2, kcat, nt, preferred_element_type=F32)
    pt = jnp.exp(s + jnp.where(first, biases[1], biases[0]) - ls2)
    dp = lax.dot_general(db2, vcat, nt, preferred_element_type=F32)
    ds = (pt * (dp - ev2)).astype(BF16)
    dq2 = jnp.dot(ds, kcat, preferred_element_type=F32) * scale
    dkc = lax.dot_general(ds, q2, tn, preferred_element_type=F32)
    dvc = lax.dot_general(pt.astype(BF16), db2, tn, preferred_element_type=F32)
    return jnp.where(lo, dq2[0:NBACK], dq2[NBACK:]), dkc, dvc


def _attn_bwd_loop(read, write, nblk):
    lo = _lo_mask((NBACK, PAIR))
    biases = _band_bias()

    def one(it, carry):
        dk_carry, dv_carry = carry
        rho = it // nblk
        b = it % nblk
        bp = jnp.maximum(b - 1, 0)
        kcat = jnp.concatenate([read(1, rho, bp), read(1, rho, b)], axis=0).astype(BF16)
        vcat = jnp.concatenate([read(2, rho, bp), read(2, rho, b)], axis=0).astype(BF16)
        dq, dkc, dvc = _attn_bwd_block(read(0, rho, b), read(3, rho, b), read(4, rho, b), read(5, rho, b), kcat, vcat,
                                       b == 0, lo, biases)
        write(0, rho, b, dq)
        write(1, rho, bp, dk_carry + dkc[0:NBACK])
        write(1, rho, b, dkc[NBACK:])
        write(2, rho, bp, dv_carry + dvc[0:NBACK])
        write(2, rho, b, dvc[NBACK:])
        return dkc[NBACK:], dvc[NBACK:]

    def step(i, carry):
        for u in range(ATTN_BWD_UNROLL):
            carry = one(i * ATTN_BWD_UNROLL + u, carry)
        return carry

    zero = jnp.zeros((NBACK, PAIR), F32)
    lax.fori_loop(0, T // NBACK // ATTN_BWD_UNROLL, step, (zero, zero))


def _attn_bwd(qn, kn, v, dbp, e, lse, rate, name):
    if rate > MAX_ROW_STRIDE:
        return _attn_bwd_gathered(qn, kn, v, dbp, e, lse, rate, name)
    window = NBACK * rate

    def body(*refs):
        rows = lambda rho, b: _rows(b * window + rho, rate)

        def write(j, rho, b, value):
            refs[6 + j][rows(rho, b), :] = value

        _attn_bwd_loop(lambda j, rho, b: refs[j][rows(rho, b), :], write, T // window)

    pm = pl.BlockSpec((None, T, PAIR), lambda p: (p, 0, 0))
    return pl.pallas_call(
        body, name=name, grid=(NPAIR,), in_specs=[pm] * 6, out_specs=[pm] * 3,
        out_shape=[_sds((NPAIR, T, PAIR), F32)] * 3, compiler_params=_cparams(1, 56))(qn, kn, v, dbp, e, lse)


def _attn_bwd_gathered(qn, kn, v, dbp, e, lse, rate, name):
    n = T // rate

    def body(*refs):
        ins, outs, in_bufs, out_bufs, (in_sem, out_sem) = refs[0:6], refs[6:9], refs[9:15], refs[15:18], refs[18:20]
        p = pl.program_id(0)
        slot = p % 2

        def loads(pair, s):
            return [pltpu.make_async_copy(x.at[pair, :, r, :], buf.at[s, r], in_sem.at[6 * s + a])
                    for a, (x, buf) in enumerate(zip(ins, in_bufs)) for r in range(rate)]

        def stores(pair, s):
            return [pltpu.make_async_copy(buf.at[s, r], x.at[pair, :, r, :], out_sem.at[3 * s + a])
                    for a, (x, buf) in enumerate(zip(outs, out_bufs)) for r in range(rate)]

        @pl.when(p == 0)
        def _():
            for c in loads(0, 0):
                c.start()

        @pl.when(p + 1 < NPAIR)
        def _():
            for c in loads(p + 1, 1 - slot):
                c.start()

        for c in loads(p, slot):
            c.wait()

        @pl.when(p >= 2)
        def _():
            for c in stores(p - 2, slot):
                c.wait()

        rows = lambda b: pl.ds(pl.multiple_of(b * NBACK, NBACK), NBACK)

        def write(j, rho, b, value):
            out_bufs[j][slot, rho, rows(b), :] = value

        _attn_bwd_loop(lambda j, rho, b: in_bufs[j][slot, rho, rows(b), :], write, n // NBACK)

        for c in stores(p, slot):
            c.start()

        @pl.when(p == NPAIR - 1)
        def _():
            for c in stores(p - 1, 1 - slot) + stores(p, slot):
                c.wait()

    by_residue = lambda a: a.reshape(NPAIR, n, rate, PAIR)
    res = pl.pallas_call(
        body, name=name, grid=(NPAIR,), in_specs=[HBM_SPEC] * 6, out_specs=[HBM_SPEC] * 3,
        out_shape=[_sds((NPAIR, n, rate, PAIR), F32)] * 3,
        scratch_shapes=[pltpu.VMEM((2, rate, n, PAIR), F32)] * 9
        + [pltpu.SemaphoreType.DMA((12,)), pltpu.SemaphoreType.DMA((6,))],
        compiler_params=_cparams(1, 56))(*[by_residue(a) for a in (qn, kn, v, dbp, e, lse)])
    return [r.reshape(NPAIR, T, PAIR) for r in res]


def _ab_in_bwd(pab, dcat, sgu_g, sgu_b, sgu_w, sgu_bias3, qg, kg, tabs, dqkv, rinvs, tm=256):
    def body(p_ref, dcat_ref, g_ref, b_ref, w_ref, bias_ref, qg_ref, kg_ref, c_ref, s1_ref, s2_ref, *rest):
        dq_refs, rinv_refs = rest[0:9], rest[9:15]
        o_ref, dwm_ref, dbias_ref, dsg_ref, dsb_ref, dgain_ref = rest[15:]
        i = pl.program_id(0)

        @pl.when(i == 0)
        def _():
            dwm_ref[...] = jnp.zeros_like(dwm_ref)
            dbias_ref[...] = jnp.zeros_like(dbias_ref)
            dsg_ref[...] = jnp.zeros_like(dsg_ref)
            dsb_ref[...] = jnp.zeros_like(dsb_ref)
            dgain_ref[...] = jnp.zeros_like(dgain_ref)

        zu = p_ref[:, 0:512].astype(F32)
        zv = p_ref[:, 512:1024].astype(F32)
        u = _gelu(zu)
        v = _gelu(zv)
        mu = jnp.mean(v, axis=-1, keepdims=True)
        vc = v - mu
        rstd = lax.rsqrt(jnp.mean(vc * vc, axis=-1, keepdims=True) + EPS)
        xhat = vc * rstd
        vn = (xhat * g_ref[...] + b_ref[...]).astype(BF16)
        da = dcat_ref[...].astype(F32)
        tri = _tril_mask()
        du_parts = [[None] * 4 for _ in range(tm // 128)]
        dvn_parts = [[None] * 4 for _ in range(tm // 128)]
        for gi in range(4):
            wg = jnp.where(tri, w_ref[gi], 0.0).astype(BF16)
            bg = bias_ref[gi]
            for c in range(tm // 128):
                rs, cs = slice(c * 128, (c + 1) * 128), slice(gi * 128, (gi + 1) * 128)
                vblk = vn[rs, cs]
                mixed = jnp.dot(wg, vblk, preferred_element_type=F32) + bg
                dab = da[rs, cs]
                du_parts[c][gi] = dab * mixed
                dmixed = dab * u[rs, cs]
                dmb = dmixed.astype(BF16)
                dvn_parts[c][gi] = lax.dot_general(wg, dmb, (((0,), (0,)), ((), ())), preferred_element_type=F32)
                dwm = lax.dot_general(dmb, vblk, (((1,), (1,)), ((), ())), preferred_element_type=F32)
                dwm_ref[gi] += jnp.where(tri, dwm, 0.0)
                dbias_ref[gi] += dmixed
        du = jnp.concatenate([jnp.concatenate(r, axis=1) for r in du_parts], axis=0)
        dvn = jnp.concatenate([jnp.concatenate(r, axis=1) for r in dvn_parts], axis=0)
        dsg_ref[...] += jnp.sum(dvn * xhat, axis=0, keepdims=True)
        dsb_ref[...] += jnp.sum(dvn, axis=0, keepdims=True)
        dxh = dvn * g_ref[...]
        dv = rstd * (dxh - jnp.mean(dxh, axis=-1, keepdims=True)
                     - xhat * jnp.mean(dxh * xhat, axis=-1, keepdims=True))
        o_ref[:, 0:512] = (du * _gelu_grad(zu)).astype(BF16)
        o_ref[:, 512:1024] = (dv * _gelu_grad(zv)).astype(BF16)

        blocks = _head_blocks()
        c, s1, s2 = c_ref[...], s1_ref[...], s2_ref[...]
        for g in range(3):
            dq_ref, dk_ref, dv_ref = dq_refs[3 * g:3 * g + 3]
            for p in range(NPAIR):
                for which, gains, src in ((0, qg_ref, dq_ref), (1, kg_ref, dk_ref)):
                    col = (2 + 3 * which + g) * 512 + p * PAIR
                    xr = p_ref[:, col:col + PAIR].astype(F32)
                    rinv = rinv_refs[2 * g + which][p].astype(F32)
                    xh = xr * rinv
                    dn = _rope_t(src[p], c, s1, s2)
                    row = 2 * g + which
                    dgain_ref[row:row + 1, :] += jnp.sum(dn * xh, axis=0, keepdims=True)
                    dxh2 = dn * gains[g:g + 1, :]
                    dx = rinv * (dxh2 - xh * _seg_mean_mxu(dxh2 * xh, blocks))
                    o_ref[:, col:col + PAIR] = dx.astype(BF16)
                col = (8 + g) * 512 + p * PAIR
                o_ref[:, col:col + PAIR] = dv_ref[p].astype(BF16)

    pm = pl.BlockSpec((NPAIR, tm, PAIR), lambda i: (0, i, 0))
    tab = pl.BlockSpec((tm, PAIR), lambda i: (i, 0))
    gain = pl.BlockSpec((3, PAIR), lambda i: (0, 0))
    vec = pl.BlockSpec((1, 512), lambda i: (0, 0))
    full = pl.BlockSpec((tm, AB_IN), lambda i: (i, 0))
    w4 = pl.BlockSpec((4, 128, 128), lambda i: (0, 0, 0))
    return pl.pallas_call(
        body, name="ab_in_bwd", grid=(T // tm,),
        in_specs=[full, pl.BlockSpec((tm, 512), lambda i: (i, 0)), vec, vec, w4,
                  pl.BlockSpec((4, 128, 1), lambda i: (0, 0, 0)), gain, gain, tab, tab, tab] + [pm] * 15,
        out_specs=[full, w4, w4, vec, vec, pl.BlockSpec((8, PAIR), lambda i: (0, 0))],
        out_shape=[_sds((T, AB_IN), BF16), _sds((4, 128, 128), F32), _sds((4, 128, 128), F32),
                   _sds((1, 512), F32), _sds((1, 512), F32), _sds((8, PAIR), F32)],
        compiler_params=_cparams(1))(pab, dcat, sgu_g, sgu_b, sgu_w, sgu_bias3, qg, kg, *tabs, *dqkv, *rinvs)


def _ln_stats(x):
    mu = jnp.mean(x, axis=-1, keepdims=True)
    xc = x - mu
    rstd = lax.rsqrt(jnp.mean(xc * xc, axis=-1, keepdims=True) + EPS)
    return xc * rstd, rstd


CONV_RC = 64


def _shifted_copies(src, dst, tm):
    dst[0] = src[...]
    for b in range(1, 8):
        dst[b, 0:tm + HALO - 8, :] = src[pl.ds(b, tm + HALO - 8), :]


def _offsets_by_phase(first):
    groups = {}
    for o in range(first, first + CONV_C_TAPS):
        groups.setdefault(o % 8, []).append(o)
    return sorted(groups.items())


def _window(shifted, b8, base, offsets, lanes):
    rows = 8 * (max(offsets) // 8) + CONV_RC
    return shifted[b8, pl.ds(base, rows), lanes].reshape(rows // 8, 8, 128)


def _cd_fwd(pcd, cw, cb, lg, lb, dw, tm=512):
    per = tm // HALO

    def body(p_ref, h_ref, cw_ref, cb_ref, lg_ref, lb_ref, dw_ref, cat_ref, c0_ref, c1_ref, dd_ref, y_ref,
             buf, buf2, sb):
        i = pl.program_id(0)
        live = jnp.where(i > 0, 1.0, 0.0)
        a = p_ref[:, 0:512].astype(F32)
        gt = p_ref[:, 512:1024].astype(F32)
        gb = p_ref[:, 1024:1536].astype(F32)
        gc = p_ref[:, 1536:2048].astype(F32)
        hv = p_ref[:, 2048:2560].astype(F32)
        c0 = a * _sigmoid(gt)
        dd = gc * hv
        buf[0:HALO, :] = h_ref[:, 0:512].astype(F32) * _sigmoid(h_ref[:, 512:1024].astype(F32)) * live
        buf[HALO:, :] = c0
        buf2[0:HALO, :] = h_ref[:, 1536:2048].astype(F32) * h_ref[:, 2048:2560].astype(F32) * live
        buf2[HALO:, :] = dd
        c0_ref[...] = c0.astype(BF16)
        dd_ref[...] = dd.astype(BF16)
        _shifted_copies(buf, sb, tm)

        def conv_rows(r, carry):
            base = pl.multiple_of(r * CONV_RC, CONV_RC)
            for c in range(4):
                lanes = slice(c * 128, (c + 1) * 128)
                acc = jnp.broadcast_to(cb_ref[:, lanes], (CONV_RC // 8, 8, 128))
                for b8, offsets in _offsets_by_phase(HALO - (CONV_C_TAPS - 1)):
                    win = _window(sb, b8, base, offsets, lanes)
                    for o in offsets:
                        j = o - (HALO - (CONV_C_TAPS - 1))
                        acc = acc + cw_ref[8 * j:8 * j + 8, lanes] * win[o // 8:o // 8 + CONV_RC // 8]
                c1_ref[pl.ds(base, CONV_RC), lanes] = acc.reshape(CONV_RC, 128)
            return carry

        lax.fori_loop(0, tm // CONV_RC, conv_rows, 0)
        xhat, _ = _ln_stats(c1_ref[...])
        c2 = xhat * lg_ref[...] + lb_ref[...]
        y = jnp.zeros((tm, 512), F32)
        for j in range(CONV_D_TAPS):
            y = y + dw_ref[j:j + 1, :] * buf2[pl.ds(HALO - (CONV_D_TAPS - 1) + j, tm), :]
        cat_ref[:, 0:512] = (c2 * _sigmoid(c2)).astype(BF16)
        cat_ref[:, 512:1024] = (gb * y).astype(BF16)
        y_ref[...] = y.astype(BF16)

    half = pl.BlockSpec((tm, 512), lambda i: (i, 0))
    vec = pl.BlockSpec((1, 512), lambda i: (0, 0))
    return pl.pallas_call(
        body, name="cd_fwd", grid=(T // tm,),
        in_specs=[pl.BlockSpec((tm, CD_IN), lambda i: (i, 0)),
                  pl.BlockSpec((HALO, CD_IN), lambda i: (jnp.maximum(i * per - 1, 0), 0)),
                  pl.BlockSpec((8 * 32, 512), lambda i: (0, 0)), vec, vec, vec, pl.BlockSpec((8, 512), lambda i: (0, 0))],
        out_specs=[pl.BlockSpec((tm, D), lambda i: (i, 0)), half, half, half, half],
        out_shape=[_sds((T, D), BF16), _sds((T, 512), BF16), _sds((T, 512), F32), _sds((T, 512), BF16),
                   _sds((T, 512), BF16)],
        scratch_shapes=[pltpu.VMEM((HALO + tm, 512), F32), pltpu.VMEM((HALO + tm, 512), F32),
                        pltpu.VMEM((8, HALO + tm, 512), F32)],
        compiler_params=_cparams(1))(pcd, pcd, cw, cb, lg, lb, dw)


def _d_cat_cd(dxb, w_cd_out, c1, pcd, y, lg, lb, dep, tm=512):
    def epilogue(d, rows, side, outs):
        c1_ref, gb_ref, y_ref, lg_ref, lb_ref = side
        dc1_ref, dy3_ref, dgb_ref, dlg_ref, dlb_ref, dcb_ref = outs
        dc, ddo = d[:, 0:512], d[:, 512:1024]
        xhat, rstd = _ln_stats(c1_ref[rows, :])
        c2 = xhat * lg_ref[...] + lb_ref[...]
        sg = _sigmoid(c2)
        dc2 = dc * sg * (1.0 + c2 * (1.0 - sg))
        dlg_ref[...] += jnp.sum(dc2 * xhat, axis=0, keepdims=True)
        dlb_ref[...] += jnp.sum(dc2, axis=0, keepdims=True)
        dxh = dc2 * lg_ref[...]
        dc1 = rstd * (dxh - jnp.mean(dxh, axis=-1, keepdims=True)
                      - xhat * jnp.mean(dxh * xhat, axis=-1, keepdims=True))
        dcb_ref[...] += jnp.sum(dc1, axis=0, keepdims=True)
        dc1_ref[rows, :] = dc1
        dgb_ref[rows, :] = (ddo * y_ref[rows, :].astype(F32)).astype(BF16)
        dy3_ref[rows, :] = ddo * gb_ref[rows, :].astype(F32)

    half = pl.BlockSpec((tm, 512), lambda i: (i, 0))
    vec = pl.BlockSpec((1, 512), lambda i: (0, 0))
    return _mm_nt_rows(
        dxb, w_cd_out, "mm_d_cat_cd", epilogue, [c1, pcd, y, lg, lb],
        [half, pl.BlockSpec((tm, 512), lambda i: (i, 2)), half, vec, vec], [half, half, half, vec, vec, vec],
        [_sds((T, 512), F32), _sds((T, 512), F32), _sds((T, 512), BF16),
         _sds((1, 512), F32), _sds((1, 512), F32), _sds((1, 512), F32)], sums=(3, 4, 5), dep=dep, tm=tm)


def _cd_bwd_conv(pcd, dc1, dy3, c0, dd, dgb, cw8, dw, tm=256):
    per = tm // HALO
    nblk = T // tm
    last32 = T // HALO - 1

    def body(p_ref, dc1_ref, dc1n_ref, dy3_ref, dy3n_ref, c0_ref, dd_ref, dgb_ref, cw_ref, dw_ref,
             o_ref, dcw_ref, ddw_ref, dbuf, d3buf, sd, dc0_buf):
        i = pl.program_id(0)
        has_next = jnp.where(i < nblk - 1, 1.0, 0.0)

        @pl.when(i == 0)
        def _():
            dcw_ref[...] = jnp.zeros_like(dcw_ref)
            ddw_ref[...] = jnp.zeros_like(ddw_ref)

        dbuf[0:tm, :] = dc1_ref[...]
        dbuf[tm:, :] = dc1n_ref[...] * has_next
        d3buf[0:tm, :] = dy3_ref[...]
        d3buf[tm:, :] = dy3n_ref[...] * has_next
        _shifted_copies(dbuf, sd, tm)
        n_tiles = tm // CONV_RC

        phases = _offsets_by_phase(0)

        def dc0_rows(r, carry):
            base = pl.multiple_of(r * CONV_RC, CONV_RC)
            for c in range(4):
                lanes = slice(c * 128, (c + 1) * 128)
                acc = jnp.zeros((CONV_RC // 8, 8, 128), F32)
                for b8, offsets in phases:
                    win = _window(sd, b8, base, offsets, lanes)
                    for o in offsets:
                        j = CONV_C_TAPS - 1 - o
                        acc = acc + cw_ref[8 * j:8 * j + 8, lanes] * win[o // 8:o // 8 + CONV_RC // 8]
                dc0_buf[pl.ds(base, CONV_RC), lanes] = acc.reshape(CONV_RC, 128)
            return carry

        lax.fori_loop(0, n_tiles, dc0_rows, 0)

        for c in range(4):
            lanes = slice(c * 128, (c + 1) * 128)
            for b8, offsets in phases:
                def dw_rows(r, accs, lanes=lanes, b8=b8, offsets=offsets):
                    base = pl.multiple_of(r * CONV_RC, CONV_RC)
                    xin = c0_ref[pl.ds(base, CONV_RC), lanes].astype(F32).reshape(CONV_RC // 8, 8, 128)
                    win = _window(sd, b8, base, offsets, lanes)
                    return tuple(acc + jnp.sum(xin * win[o // 8:o // 8 + CONV_RC // 8], axis=0)
                                 for acc, o in zip(accs, offsets))

                accs = lax.fori_loop(0, n_tiles, dw_rows, tuple(jnp.zeros((8, 128), F32) for _ in offsets))
                for acc, o in zip(accs, offsets):
                    j = CONV_C_TAPS - 1 - o
                    dcw_ref[j:j + 1, lanes] += jnp.sum(acc, axis=0, keepdims=True)

        dc0 = dc0_buf[...]
        ddin = dd_ref[...].astype(F32)
        ddd = jnp.zeros((tm, 512), F32)
        for j in range(CONV_D_TAPS):
            dy_shift = d3buf[pl.ds(CONV_D_TAPS - 1 - j, tm), :]
            ddd = ddd + dw_ref[j:j + 1, :] * dy_shift
            ddw_ref[j:j + 1, :] += jnp.sum(ddin * dy_shift, axis=0, keepdims=True)

        a = p_ref[:, 0:512].astype(F32)
        gt = p_ref[:, 512:1024].astype(F32)
        gc = p_ref[:, 1536:2048].astype(F32)
        hv = p_ref[:, 2048:2560].astype(F32)
        sg = _sigmoid(gt)
        o_ref[:, 0:512] = (dc0 * sg).astype(BF16)
        o_ref[:, 512:1024] = (dc0 * a * sg * (1.0 - sg)).astype(BF16)
        o_ref[:, 1024:1536] = dgb_ref[...]
        o_ref[:, 1536:2048] = (ddd * hv).astype(BF16)
        o_ref[:, 2048:2560] = (ddd * gc).astype(BF16)

    half = pl.BlockSpec((tm, 512), lambda i: (i, 0))
    nxt = pl.BlockSpec((HALO, 512), lambda i: (jnp.minimum((i + 1) * per, last32), 0))
    full = pl.BlockSpec((tm, CD_IN), lambda i: (i, 0))
    return pl.pallas_call(
        body, name="cd_bwd_conv", grid=(nblk,),
        in_specs=[full, half, nxt, half, nxt, half, half, half,
                  pl.BlockSpec((8 * 32, 512), lambda i: (0, 0)), pl.BlockSpec((8, 512), lambda i: (0, 0))],
        out_specs=[full, pl.BlockSpec((32, 512), lambda i: (0, 0)), pl.BlockSpec((8, 512), lambda i: (0, 0))],
        out_shape=[_sds((T, CD_IN), BF16), _sds((32, 512), F32), _sds((8, 512), F32)],
        scratch_shapes=[pltpu.VMEM((tm + HALO, 512), F32), pltpu.VMEM((tm + HALO, 512), F32),
                        pltpu.VMEM((8, tm + HALO, 512), F32), pltpu.VMEM((tm, 512), F32)],
        compiler_params=_cparams(1))(pcd, dc1, dc1, dy3, dy3, c0, dd, dgb, cw8, dw)


def _local_step(x, tgt, W, fetch=None, on_grad=None):
    W = dict(W)
    if fetch is None:
        fetch = lambda stage, after: {}
    if on_grad is None:
        on_grad = lambda key, arr: None
    tabs = _rope_tables()
    qg = jnp.tile(W["q_norm_g"], (1, 2))
    kg = jnp.tile(W["k_norm_g"], (1, 2))
    bias3 = W["sgu_bias"].reshape(4, 128, 1)
    G = {}

    h0 = _rms_fwd(x, W["ab_norm_g"], "rms_fwd_ab", dep=W.get("dep_first"))
    W.update(fetch("ab_in", h0))
    pab = _mm_nt(h0, W["wt_ab_in"], "mm_ab_in", dep=W.get("dep0"))
    cat_ab = _mix_a_fwd(pab, W["sgu_norm_g"], W["sgu_norm_b"], W["sgu_w"], bias3)
    qkv, rinvs = _prep_fwd(pab, qg, kg, tabs)
    outs, lses = [], []
    for g, rate in enumerate(DIL_RATES):
        o, l = _attn_fwd(qkv[3 * g], qkv[3 * g + 1], qkv[3 * g + 2], rate, f"attn_fwd_{g}", dep=W.get(f"dep_attn{g}"))
        outs.append(o)
        lses.append(l)
        W.update(fetch(f"attn{g}", o))
    cat_ab, lse = _merge_fwd(cat_ab, outs, lses)
    W.update(fetch("ab_out", lse))
    x1, h1 = _mm_nn(cat_ab, W["w_ab_out"], "mm_ab_out", mode="rms", resid=x, gain=W["ffn_norm_g"][0:1])
    pf0, act0 = _ffn_in(h1, W["wt_ffn_in0"], "ffn_in0")
    W.update(fetch("ffn_down0", act0))
    x2, h2 = _mm_nn(act0, W["w_ffn_down0"], "mm_ffn_down0", mode="rms", resid=x1, gain=W["cd_norm_g"],
                    dep=W.get("dep_down0"))
    W.update(fetch("cd_in", h2))
    pcd = _mm_nt(h2, W["wt_cd_in"], "mm_cd_in")
    cw8 = jnp.repeat(W["conv_c_w32"], 8, axis=0)
    cat_cd, c0, c1, dd, yv = _cd_fwd(pcd, cw8, W["conv_c_b"], W["c_ln_g"], W["c_ln_b"], W["conv_d_w8"])
    x3, h3 = _mm_nn(cat_cd, W["w_cd_out"], "mm_cd_out", mode="rms", resid=x2, gain=W["ffn_norm_g"][1:2])
    pf1, act1 = _ffn_in(h3, W["wt_ffn_in1"], "ffn_in1")
    dy, dyb, loss_cols = _mm_nn(act1, W["w_ffn_down1"], "mm_ffn_down1", mode="loss", resid=x3, tgt=tgt)

    def ffn_bwd(xin, h, pf, act, dres, dresb, layer):
        G[f"w_ffn_down{layer}"] = _mm_tn(act, dresb, f"mm_g_ffn_down{layer}")
        dep = on_grad(f"w_ffn_down{layer}", G[f"w_ffn_down{layer}"])
        dpf = _ffn_dact(dresb, W[f"w_ffn_down{layer}"], pf, f"ffn_dact{layer}", dep=dep)
        G[f"wt_ffn_in{layer}"] = _mm_tn(dpf, h, f"mm_g_ffn_in{layer}")
        dep = on_grad(f"wt_ffn_in{layer}", G[f"wt_ffn_in{layer}"])
        dx, dxb, G[f"ffn_norm_g{layer}"] = _mm_dh_rms_bwd(
            dpf, W[f"wt_ffn_in{layer}"], xin, W["ffn_norm_g"][layer:layer + 1], dres, f"mm_d_h_ffn{layer}", dep=dep)
        return dx, dxb

    dx3, dx3b = ffn_bwd(x3, h3, pf1, act1, dy, dyb, 1)

    G["w_cd_out"] = _mm_tn(cat_cd, dx3b, "mm_g_cd_out")
    dep = on_grad("w_cd_out", G["w_cd_out"])
    dc1, dy3, dgb, G["c_ln_g"], G["c_ln_b"], G["conv_c_b"] = _d_cat_cd(
        dx3b, W["w_cd_out"], c1, pcd, yv, W["c_ln_g"], W["c_ln_b"], dep)
    dpcd, G["conv_c_w32"], G["conv_d_w8"] = _cd_bwd_conv(pcd, dc1, dy3, c0, dd, dgb, cw8, W["conv_d_w8"])
    G["wt_cd_in"] = _mm_tn(dpcd, h2, "mm_g_cd_in")
    dep = on_grad("wt_cd_in", G["wt_cd_in"])
    dx2, dx2b, G["cd_norm_g"] = _mm_dh_rms_bwd(dpcd, W["wt_cd_in"], x2, W["cd_norm_g"], dx3, "mm_d_h_cd", dep=dep)

    dx1, dx1b = ffn_bwd(x1, h1, pf0, act0, dx2, dx2b, 0)

    G["w_ab_out"] = _mm_tn(cat_ab, dx1b, "mm_g_ab_out")
    dep = on_grad("w_ab_out", G["w_ab_out"])
    dcat_a, dbp, e = _d_cat_ab(dx1b, W["w_ab_out"], cat_ab, dep)
    dqkv = []
    for g, rate in enumerate(DIL_RATES):
        dqkv += _attn_bwd(qkv[3 * g], qkv[3 * g + 1], qkv[3 * g + 2], dbp, e, lse, rate, f"attn_bwd_{g}")
    dpab, G["sgu_w"], dbias_part, G["sgu_norm_g"], G["sgu_norm_b"], dgain = _ab_in_bwd(
        pab, dcat_a, W["sgu_norm_g"], W["sgu_norm_b"], W["sgu_w"], bias3, qg, kg, tabs, dqkv, rinvs)
    G["sgu_bias"] = jnp.sum(dbias_part, axis=-1)
    dgain = dgain[0:6, 0:HEAD] + dgain[0:6, HEAD:PAIR]
    G["q_norm_g"] = dgain[0::2]
    G["k_norm_g"] = dgain[1::2]
    G["loss_cols"] = loss_cols
    dep = on_grad("small", G)
    G["wt_ab_in"] = _mm_tn(dpab, h0, "mm_g_ab_in", dep=dep)
    dep = on_grad("wt_ab_in", G["wt_ab_in"])
    grad_x, G["ab_norm_g"] = _mm_dh_rms_bwd(dpab, W["wt_ab_in"], x, W["ab_norm_g"], dx1, "mm_d_h_ab", dep=dep,
                                            bf16_copy=False)
    return loss_cols, grad_x, G


def _my_place():
    return lax.axis_index("x"), lax.axis_index("y"), lax.axis_index("c")


def _dev_index(px, py, pc):
    return 4 * px + 2 * py + pc


def _flip(place, k):
    x, y, c = place
    return (1 - x if k & 4 else x, 1 - y if k & 2 else y, 1 - c if k & 1 else c)


def _landing(shape, dtype, own):
    buf = lax.empty(shape, dtype)
    for lead, part in own:
        buf = lax.dynamic_update_slice(buf, part.reshape((1,) * len(lead) + part.shape),
                                       tuple(lead) + (0,) * part.ndim)
    return buf


HBM_ONLY = pl.BlockSpec(memory_space=pltpu.HBM)
SEM_SPEC = pl.BlockSpec(memory_space=pltpu.SEMAPHORE)
IN_FLIGHT = pltpu.CompilerParams(has_side_effects=pltpu.SideEffectType.DATAFLOW_SIDE_EFFECTING)


def _in_hbm(a):
    return pltpu.with_memory_space_constraint(a, pltpu.HBM)


def _exchange_start(name, srcs, lands, items, dep=None):
    ns, nl, ni = len(srcs), len(lands), len(items)

    def body(*refs):
        S, L = refs[0:ns], refs[ns:ns + nl]
        first_out = ns + nl + (0 if dep is None else 1)
        send_sems, recv_sems, token = refs[first_out], refs[first_out + 1], refs[-1]
        me = _my_place()
        mi = _dev_index(*me)
        for i, (src, dst) in enumerate(items):
            for k in range(1, NDEV):
                peer = _flip(me, k)
                pltpu.make_async_remote_copy(
                    src_ref=src(S, _dev_index(*peer)), dst_ref=dst(L, mi), send_sem=send_sems.at[7 * i + k - 1],
                    recv_sem=recv_sems.at[7 * i + k - 1], device_id=peer, device_id_type=MESH).start()
        token[...] = jnp.zeros_like(token)

    thru = [pltpu.HBM(a.shape, a.dtype) for a in list(srcs) + list(lands)]
    args = [_in_hbm(a) for a in srcs] + [_in_hbm(a) for a in lands]
    in_specs = [HBM_ONLY] * (ns + nl)
    if dep is not None:
        args.append(dep)
        in_specs.append(HBM_SPEC)
    outs = pl.pallas_call(
        body, name=name, in_specs=in_specs,
        out_shape=(pltpu.SemaphoreType.DMA((7 * ni,)), pltpu.SemaphoreType.DMA((7 * ni,)), *thru, _sds((8, 128), F32)),
        out_specs=(SEM_SPEC, SEM_SPEC, *[HBM_ONLY] * (ns + nl), pl.BlockSpec(memory_space=pltpu.VMEM)),
        input_output_aliases={j: 2 + j for j in range(ns + nl)}, compiler_params=IN_FLIGHT)(*args)
    return dict(send=outs[0], recv=outs[1], srcs=list(outs[2:2 + ns]), lands=list(outs[2 + ns:2 + ns + nl]),
                token=outs[-1], items=items)


def _exchange_wait(name, states, after):
    after = list(after) if isinstance(after, (list, tuple)) else [after]
    counts = [(len(st["srcs"]), len(st["lands"]), len(st["items"])) for st in states]
    n_arrays = sum(c[0] + c[1] for c in counts)

    def body(*refs):
        me = _my_place()
        mi = _dev_index(*me)
        pos = 0
        sem_pos = n_arrays
        for st, (ns, nl, ni) in zip(states, counts):
            S, L = refs[pos:pos + ns], refs[pos + ns:pos + ns + nl]
            send_sems, recv_sems = refs[sem_pos], refs[sem_pos + 1]
            pos += ns + nl
            sem_pos += 2
            for i, (src, dst) in enumerate(st["items"]):
                for k in range(1, NDEV):
                    cp = pltpu.make_async_remote_copy(
                        src_ref=src(S, mi), dst_ref=dst(L, mi), send_sem=send_sems.at[7 * i + k - 1],
                        recv_sem=recv_sems.at[7 * i + k - 1], device_id=me, device_id_type=MESH)
                    cp.wait_send()
                    cp.wait_recv()

    arrays, sems = [], []
    for st in states:
        arrays += st["srcs"] + st["lands"]
        sems += [st["send"], st["recv"]]
    outs = pl.pallas_call(
        body, name=name, in_specs=[HBM_ONLY] * n_arrays + [SEM_SPEC] * len(sems) + [HBM_SPEC] * len(after),
        out_shape=tuple(pltpu.HBM(a.shape, a.dtype) for a in arrays), out_specs=tuple([HBM_ONLY] * n_arrays),
        input_output_aliases={j: j for j in range(n_arrays)}, compiler_params=IN_FLIGHT)(*arrays, *sems, *after)
    lands, pos = [], 0
    for ns, nl, _ in counts:
        lands.append(list(outs[pos + ns:pos + ns + nl]))
        pos += ns + nl
    return lands


def _place_and_neighbours():
    x, y, c = _my_place()
    return (x, y, c), (x, y, 1 - c), [(1 - x, y), (x, 1 - y), (1 - x, 1 - y)]


def _gather_start(name, srcs, lands, items, dep=None):
    ns, nl, ni = len(srcs), len(lands), len(items)

    def body(*refs):
        S, L = refs[0:ns], refs[ns:ns + nl]
        first_out = ns + nl + (0 if dep is None else 1)
        send_sems, recv_sems, token = refs[first_out], refs[first_out + 1], refs[-1]
        me, sib, chips = _place_and_neighbours()
        mi = _dev_index(*me)
        for i, (src, dst) in enumerate(items):
            for k, to in enumerate([sib] + [(*chip, me[2]) for chip in chips]):
                pltpu.make_async_remote_copy(
                    src_ref=src(S), dst_ref=dst(L, mi), send_sem=send_sems.at[4 * i + k],
                    recv_sem=recv_sems.at[4 * i + k], device_id=to, device_id_type=MESH).start()
        token[...] = jnp.zeros_like(token)

    thru = [pltpu.HBM(a.shape, a.dtype) for a in list(srcs) + list(lands)]
    args = [_in_hbm(a) for a in srcs] + [_in_hbm(a) for a in lands]
    in_specs = [HBM_ONLY] * (ns + nl)
    if dep is not None:
        args.append(dep)
        in_specs.append(HBM_SPEC)
    outs = pl.pallas_call(
        body, name=name, in_specs=in_specs,
        out_shape=(pltpu.SemaphoreType.DMA((4 * ni,)), pltpu.SemaphoreType.DMA((4 * ni,)), *thru, _sds((8, 128), F32)),
        out_specs=(SEM_SPEC, SEM_SPEC, *[HBM_ONLY] * (ns + nl), pl.BlockSpec(memory_space=pltpu.VMEM)),
        input_output_aliases={j: 2 + j for j in range(ns + nl)}, compiler_params=IN_FLIGHT)(*args)
    return dict(send=outs[0], recv=outs[1], srcs=list(outs[2:2 + ns]), lands=list(outs[2 + ns:2 + ns + nl]),
                token=outs[-1], items=items)


def _gather_forward(name, st, after):
    nl, ni = len(st["lands"]), len(st["items"])

    def body(*refs):
        L, recv_sems = refs[0:nl], refs[nl]
        fwd_send, fwd_recv, token = refs[-3:]
        me, sib, chips = _place_and_neighbours()
        for i, (_, dst) in enumerate(st["items"]):
            for j, chip in enumerate(chips):
                blk = dst(L, _dev_index(*chip, me[2]))
                pltpu.make_async_remote_copy(
                    src_ref=blk, dst_ref=blk, send_sem=fwd_send.at[3 * i + j], recv_sem=recv_sems.at[4 * i + 1 + j],
                    device_id=me, device_id_type=MESH).wait_recv()
                pltpu.make_async_remote_copy(
                    src_ref=blk, dst_ref=blk, send_sem=fwd_send.at[3 * i + j], recv_sem=fwd_recv.at[3 * i + j],
                    device_id=sib, device_id_type=MESH).start()
        token[...] = jnp.zeros_like(token)

    after = list(after) if isinstance(after, (list, tuple)) else [after]
    outs = pl.pallas_call(
        body, name=name, in_specs=[HBM_ONLY] * nl + [SEM_SPEC] + [HBM_SPEC] * len(after),
        out_shape=(*[pltpu.HBM(a.shape, a.dtype) for a in st["lands"]], pltpu.SemaphoreType.DMA((3 * ni,)),
                   pltpu.SemaphoreType.DMA((3 * ni,)), _sds((8, 128), F32)),
        out_specs=(*[HBM_ONLY] * nl, SEM_SPEC, SEM_SPEC, pl.BlockSpec(memory_space=pltpu.VMEM)),
        input_output_aliases={j: j for j in range(nl)}, compiler_params=IN_FLIGHT)(*st["lands"], st["recv"], *after)
    return dict(st, lands=list(outs[0:nl]), fwd_send=outs[nl], fwd_recv=outs[nl + 1], token=outs[-1])


def _gather_wait(name, st, after):
    ns, nl, ni = len(st["srcs"]), len(st["lands"]), len(st["items"])

    def body(*refs):
        S, L = refs[0:ns], refs[ns:ns + nl]
        send_sems, recv_sems, fwd_send, fwd_recv = refs[ns + nl:ns + nl + 4]
        me, sib, chips = _place_and_neighbours()
        mi = _dev_index(*me)
        for i, (src, dst) in enumerate(st["items"]):
            mine = dst(L, mi)
            for k in range(4):
                pltpu.make_async_remote_copy(
                    src_ref=src(S), dst_ref=mine, send_sem=send_sems.at[4 * i + k], recv_sem=recv_sems.at[4 * i + k],
                    device_id=me, device_id_type=MESH).wait_send()
            pltpu.make_async_remote_copy(
                src_ref=src(S), dst_ref=mine, send_sem=send_sems.at[4 * i], recv_sem=recv_sems.at[4 * i],
                device_id=me, device_id_type=MESH).wait_recv()
            for j in range(3):
                cp = pltpu.make_async_remote_copy(
                    src_ref=mine, dst_ref=mine, send_sem=fwd_send.at[3 * i + j], recv_sem=fwd_recv.at[3 * i + j],
                    device_id=me, device_id_type=MESH)
                cp.wait_send()
                cp.wait_recv()

    arrays = st["srcs"] + st["lands"]
    outs = pl.pallas_call(
        body, name=name, in_specs=[HBM_ONLY] * (ns + nl) + [SEM_SPEC] * 4 + [HBM_SPEC],
        out_shape=tuple(pltpu.HBM(a.shape, a.dtype) for a in arrays), out_specs=tuple([HBM_ONLY] * (ns + nl)),
        input_output_aliases={j: j for j in range(ns + nl)},
        compiler_params=IN_FLIGHT)(*arrays, st["send"], st["recv"], st["fwd_send"], st["fwd_recv"], after)
    return list(outs[ns:ns + nl])


def _sum_slots(land):
    def body(l_ref, o_ref):
        acc = l_ref[0]
        for d in range(1, NDEV):
            acc = acc + l_ref[d]
        o_ref[...] = acc

    vm = pl.BlockSpec(memory_space=pltpu.VMEM)
    return pl.pallas_call(body, name="sum_small", out_shape=_sds(land.shape[1:], F32), in_specs=[vm], out_specs=vm)(land)


def _adam_math(w, g, m, v):
    m2 = ADAM_B1 * m + (1.0 - ADAM_B1) * g
    v2 = ADAM_B2 * v + (1.0 - ADAM_B2) * (g * g)
    delta = -ADAM_LR * ((m2 * ADAM_C1) / (jnp.sqrt(v2 * ADAM_C2) + ADAM_EPS) + ADAM_WD * w)
    return delta, m2, v2


def _adam_layer(land, sel, w, m, v, layer, name, prev=None, tc=512):
    R = land.shape[2]

    def body(l_ref, w_ref, m_ref, v_ref, *rest):
        g_out, d_out, m_out, v_out = rest[-4:]
        g = l_ref[0].astype(F32)
        for d in range(1, NDEV):
            g = g + l_ref[d].astype(F32)
        delta, m2, v2 = _adam_math(w_ref[...], g, m_ref[...], v_ref[...])
        g_out[...] = g
        d_out[...] = delta
        m_out[...] = m2
        v_out[...] = v2

    wspec = pl.BlockSpec((None, R, tc), lambda i: (layer, 0, i))
    in_specs = [pl.BlockSpec((None, NDEV, R, tc), lambda i: (sel, 0, 0, i)), wspec, wspec, wspec]
    args = [land, w, m, v]
    aliases = {}
    if prev is not None:
        in_specs += [HBM_SPEC] * 4
        args += list(prev)
        aliases = {4 + j: j for j in range(4)}
    return pl.pallas_call(
        body, name=name, grid=(D // tc,), in_specs=in_specs, out_specs=[wspec] * 4,
        out_shape=[_sds(w.shape, F32)] * 4, input_output_aliases=aliases, compiler_params=_cparams(1))(*args)


def _adam_stacked(lands, sel, w, m, v, name):
    res = None
    for layer, land in enumerate(lands):
        res = _adam_layer(land, sel, w, m, v, layer, f"{name}{layer}", prev=res)
    return res


def _adam_small(ws, gs, ms, vs):
    n = len(ws)

    def body(*refs):
        w_r, g_r, m_r, v_r = refs[0:n], refs[n:2 * n], refs[2 * n:3 * n], refs[3 * n:4 * n]
        d_o, m_o, v_o = refs[4 * n:5 * n], refs[5 * n:6 * n], refs[6 * n:7 * n]
        for i in range(n):
            delta, m2, v2 = _adam_math(w_r[i][...], g_r[i][...], m_r[i][...], v_r[i][...])
            d_o[i][...] = delta
            m_o[i][...] = m2
            v_o[i][...] = v2

    vm = pl.BlockSpec(memory_space=pltpu.VMEM)
    shapes = [_sds(w.shape, F32) for w in ws]
    outs = pl.pallas_call(body, name="adam_small", in_specs=[vm] * (4 * n), out_specs=[vm] * (3 * n),
                          out_shape=shapes * 3)(*ws, *gs, *ms, *vs)
    return outs[0:n], outs[n:2 * n], outs[2 * n:3 * n]


def _adam_of_slots(land, w, m, v, name):
    def body(l_ref, w_ref, m_ref, v_ref, g_o, d_o, m_o, v_o):
        g = l_ref[0]
        for d in range(1, NDEV):
            g = g + l_ref[d]
        g_o[...] = g
        d_o[...], m_o[...], v_o[...] = _adam_math(w_ref[...], g, m_ref[...], v_ref[...])

    vm = pl.BlockSpec(memory_space=pltpu.VMEM)
    return pl.pallas_call(body, name=name, in_specs=[vm] * 4, out_specs=[vm] * 4,
                          out_shape=[_sds(w.shape, F32)] * 4)(land, w, m, v)


WEIGHT_NAMES = ("ab_norm_g", "ab_w_in", "sgu_norm_g", "sgu_norm_b", "sgu_w", "sgu_bias", "q_norm_g", "k_norm_g",
                "ab_w_out", "cd_norm_g", "cd_w_in", "conv_c_w", "conv_c_b", "c_ln_g", "c_ln_b", "conv_d_w",
                "cd_w_out", "ffn_norm_g", "ffn_w_gate", "ffn_w_up", "ffn_w_down")
SMALL_SHAPES = (("sgu_norm_g", (1, 512)), ("sgu_norm_b", (1, 512)), ("sgu_w", (512, 128)),
                ("sgu_bias", (4, 128)), ("q_norm_g", (3, 1, 64)), ("k_norm_g", (3, 1, 64)), ("cd_norm_g", (1, 128)),
                ("conv_c_w", (31, 1, 64)), ("conv_c_b", (1, 64)), ("c_ln_g", (1, 64)), ("c_ln_b", (1, 64)),
                ("conv_d_w", (3, 1, 64)), ("ffn_norm_g", (2, 1024)))
SHARD_C = 64


def _pack_rows(parts, rows):
    flat = jnp.concatenate([p.reshape(-1) for p in parts])
    return jnp.pad(flat, (0, rows * 128 - flat.shape[0])).reshape(rows, 128)


def kernel(x, ab_norm_g, ab_w_in, sgu_norm_g, sgu_norm_b, sgu_w, sgu_bias, q_norm_g, k_norm_g, ab_w_out, cd_norm_g, cd_w_in, conv_c_w, conv_c_b, c_ln_g, c_ln_b, conv_d_w, cd_w_out, ffn_norm_g, ffn_w_gate, ffn_w_up, ffn_w_down, loss_target, m_ab_norm_g, m_ab_w_in, m_sgu_norm_g, m_sgu_norm_b, m_sgu_w, m_sgu_bias, m_q_norm_g, m_k_norm_g, m_ab_w_out, m_cd_norm_g, m_cd_w_in, m_conv_c_w, m_conv_c_b, m_c_ln_g, m_c_ln_b, m_conv_d_w, m_cd_w_out, m_ffn_norm_g, m_ffn_w_gate, m_ffn_w_up, m_ffn_w_down, v_ab_norm_g, v_ab_w_in, v_sgu_norm_g, v_sgu_norm_b, v_sgu_w, v_sgu_bias, v_q_norm_g, v_k_norm_g, v_ab_w_out, v_cd_norm_g, v_cd_w_in, v_conv_c_w, v_conv_c_b, v_c_ln_g, v_c_ln_b, v_conv_d_w, v_cd_w_out, v_ffn_norm_g, v_ffn_w_gate, v_ffn_w_up, v_ffn_w_down):
    w = dict(zip(WEIGHT_NAMES, (ab_norm_g, ab_w_in, sgu_norm_g, sgu_norm_b, sgu_w, sgu_bias, q_norm_g, k_norm_g, ab_w_out, cd_norm_g, cd_w_in, conv_c_w, conv_c_b, c_ln_g, c_ln_b, conv_d_w, cd_w_out, ffn_norm_g, ffn_w_gate, ffn_w_up, ffn_w_down)))
    m = dict(zip(WEIGHT_NAMES, (m_ab_norm_g, m_ab_w_in, m_sgu_norm_g, m_sgu_norm_b, m_sgu_w, m_sgu_bias, m_q_norm_g, m_k_norm_g, m_ab_w_out, m_cd_norm_g, m_cd_w_in, m_conv_c_w, m_conv_c_b, m_c_ln_g, m_c_ln_b, m_conv_d_w, m_cd_w_out, m_ffn_norm_g, m_ffn_w_gate, m_ffn_w_up, m_ffn_w_down)))
    v = dict(zip(WEIGHT_NAMES, (v_ab_norm_g, v_ab_w_in, v_sgu_norm_g, v_sgu_norm_b, v_sgu_w, v_sgu_bias, v_q_norm_g, v_k_norm_g, v_ab_w_out, v_cd_norm_g, v_cd_w_in, v_conv_c_w, v_conv_c_b, v_c_ln_g, v_c_ln_b, v_conv_d_w, v_cd_w_out, v_ffn_norm_g, v_ffn_w_gate, v_ffn_w_up, v_ffn_w_down)))
    me = _dev_index(*_my_place())

    r_ff = DFF // NDEV
    one = lambda a: (lambda S, j: S[a])
    slot = lambda b: (lambda L, s: L[b].at[s])
    slot2 = lambda b, part: (lambda L, s: L[b].at[part, s])
    shard = lambda a: (lambda S: S[a])

    def later(a):
        return lax.optimization_barrier((a, gathers[0]["token"]))[0]

    def layer_shards(layer):
        return (later(w["ffn_w_gate"][layer]).T.astype(BF16), later(w["ffn_w_up"][layer]).T.astype(BF16),
                later(w["ffn_w_down"][layer]).astype(BF16))

    def gathered(own):
        return _landing((NDEV,) + own.shape, BF16, [((me,), own)])

    def gathered2(a, b):
        return _landing((2, NDEV) + a.shape, BF16, [((0, me), a), ((1, me), b)])

    ab_in_s = w["ab_w_in"][0].T.astype(BF16)
    gathers = {0: _gather_start("gather0_start", [ab_in_s], [gathered(ab_in_s)], [(shard(0), slot(0))])}

    def chan(flat, lo, taps):
        return flat[:, lo:lo + taps * SHARD_C].reshape(NDEV, taps, SHARD_C).transpose(1, 0, 2).reshape(taps, 512)

    def fetch(stage, after):
        if stage == "ab_in":
            ab_out_s = later(w["ab_w_out"][0]).astype(BF16)
            gate0, up0, down0 = layer_shards(0)
            small_s = _pack_rows([later(w[n]) for n in ("cd_norm_g", "conv_c_w", "conv_c_b", "c_ln_g", "c_ln_b",
                                                        "conv_d_w")], 24)
            lands1 = [gathered(ab_out_s), gathered2(gate0, up0), gathered(down0),
                      _landing((NDEV,) + small_s.shape, F32, [((me,), small_s)])]
            gathers[0] = _gather_forward("gather0_forward", gathers[0], [after] + lands1)
            l_ab_in, = _gather_wait("gather0_wait", gathers[0], gathers[0]["token"])
            gathers[1] = _gather_start(
                "gather1_start", [ab_out_s, gate0, up0, down0, small_s], lands1,
                [(shard(0), slot(0)), (shard(1), slot2(1, 0)), (shard(2), slot2(1, 1)), (shard(3), slot(2)),
                 (shard(4), slot(3))], dep=l_ab_in)
            return {"wt_ab_in": l_ab_in.reshape(AB_IN, D), "dep0": gathers[1]["token"]}
        if stage == "attn0":
            cd_in_s, cd_out_s = later(w["cd_w_in"][0]).T.astype(BF16), later(w["cd_w_out"][0]).astype(BF16)
            gate1, up1, down1 = layer_shards(1)
            gathers[2] = _gather_start(
                "gather2_start", [cd_in_s, cd_out_s, gate1, up1, down1],
                [gathered(cd_in_s), gathered(cd_out_s), gathered2(gate1, up1), gathered(down1)],
                [(shard(0), slot(0)), (shard(1), slot(1)), (shard(2), slot2(2, 0)), (shard(3), slot2(2, 1)),
                 (shard(4), slot(3))], dep=after)
            return {"dep_attn1": gathers[2]["token"]}
        if stage == "attn1":
            gathers[1] = _gather_forward("gather1_forward", gathers[1], after)
            return {"dep_attn2": gathers[1]["token"]}
        if stage == "ab_out":
            l_out, l_ffn, l_down, l_small = _gather_wait("gather1_wait", gathers[1], after)
            flat = l_small.reshape(NDEV, 24 * 128)
            return {
                "w_ab_out": l_out.reshape(D, D), "wt_ffn_in0": l_ffn.reshape(2 * DFF, D),
                "w_ffn_down0": l_down.reshape(DFF, D), "cd_norm_g": flat[:, 0:128].reshape(1, D),
                "conv_c_w32": jnp.pad(chan(flat, 128, CONV_C_TAPS), ((0, 1), (0, 0))),
                "conv_c_b": chan(flat, 2112, 1), "c_ln_g": chan(flat, 2176, 1), "c_ln_b": chan(flat, 2240, 1),
                "conv_d_w8": jnp.pad(chan(flat, 2304, CONV_D_TAPS), ((0, 8 - CONV_D_TAPS), (0, 0))),
            }
        if stage == "ffn_down0":
            gathers[2] = _gather_forward("gather2_forward", gathers[2], after)
            return {"dep_down0": gathers[2]["token"]}
        if stage == "cd_in":
            l_in, l_out, l_ffn, l_down = _gather_wait("gather2_wait", gathers[2], after)
            return {"wt_cd_in": l_in.reshape(CD_IN, D), "w_cd_out": l_out.reshape(D, D),
                    "wt_ffn_in1": l_ffn.reshape(2 * DFF, D), "w_ffn_down1": l_down.reshape(DFF, D)}
        return {}

    scatters = {}
    rides_with = {"w_ffn_down1": "wt_ffn_in1", "w_cd_out": "wt_cd_in", "w_ffn_down0": "wt_ffn_in0"}
    held = {}
    smalls = {}

    def small_exchange(name, block):
        land = _landing((NDEV,) + block.shape, F32, [((me,), block)])
        return _exchange_start(name, [block], [land], [(one(0), slot(0))])

    def on_grad(key, arr):
        if key == "small":
            parts = [arr["sgu_norm_g"], arr["sgu_norm_b"], arr["sgu_w"], arr["sgu_bias"], arr["q_norm_g"],
                     arr["k_norm_g"], arr["cd_norm_g"], arr["conv_c_w32"][:CONV_C_TAPS], arr["conv_c_b"], arr["c_ln_g"],
                     arr["c_ln_b"], arr["conv_d_w8"][:CONV_D_TAPS], arr["ffn_norm_g0"], arr["ffn_norm_g1"],
                     arr["loss_cols"]]
            smalls["sizes"] = [p.size for p in parts]
            rows = -(-sum(smalls["sizes"]) // 1024) * 8
            smalls["early"] = small_exchange("small_start", _pack_rows(parts, rows))
            return smalls["early"]["token"]
        if key in rides_with:
            held[rides_with[key]] = (key, arr)
            return None
        group = ([held.pop(key)] if key in held else []) + [(key, arr)]
        srcs, lands, items = [], [], []
        for n, (k, a) in enumerate(group):
            if k.startswith("wt_ffn_in"):
                src = a.reshape(2, NDEV, r_ff, D)
                own = lax.dynamic_slice_in_dim(src, me, 1, axis=1)
                lands.append(lax.dynamic_update_slice(lax.empty(src.shape, BF16), own, (0, me, 0, 0)))
                items += [((lambda S, j, n=n: S[n].at[0, j]), slot2(n, 0)), ((lambda S, j, n=n: S[n].at[1, j]), slot2(n, 1))]
            else:
                rows = a.shape[0] // NDEV
                src = a.reshape(NDEV, rows, D)
                own = lax.dynamic_index_in_dim(src, me, 0, keepdims=False)
                lands.append(_landing((1, NDEV, rows, D), BF16, [((0, me), own)]))
                items.append(((lambda S, j, n=n: S[n].at[j]), slot2(n, 0)))
            srcs.append(src)
        st = _exchange_start(f"scatter_{key}_start", srcs, lands, items)
        scatters[key] = (st, [k for k, _ in group])
        return st["token"]

    W = {
        "dep_first": gathers[0]["token"],
        "ab_norm_g": w["ab_norm_g"], "sgu_norm_g": w["sgu_norm_g"], "sgu_norm_b": w["sgu_norm_b"],
        "sgu_w": w["sgu_w"][0], "sgu_bias": w["sgu_bias"][0], "q_norm_g": w["q_norm_g"][0],
        "k_norm_g": w["k_norm_g"][0], "ffn_norm_g": w["ffn_norm_g"],
    }

    loss_cols, grad_x, G = _local_step(x[0], loss_target[0], W, fetch, on_grad)

    late_small = small_exchange("small_late_start", G["ab_norm_g"])
    landed = {}

    def wait_scatters(name, group_keys, others, after):
        res = _exchange_wait(name, [scatters[gk][0] for gk in group_keys] + others, after)
        for gk, lands in zip(group_keys, res):
            landed.update(zip(scatters[gk][1], lands))
        return [lands[0] for lands in res[len(group_keys):]]

    small_land, = wait_scatters("scatter_wait_early", ["wt_ffn_in1", "wt_cd_in", "wt_ffn_in0", "w_ab_out"],
                                [smalls["early"]], late_small["token"])

    grads, deltas, new_m, new_v = {}, {}, {}, {}
    done = []

    def put(name, res):
        grads[name], deltas[name], new_m[name], new_v[name] = res

    def adam(name, lands, sel, transposed):
        flip = (lambda a: jnp.swapaxes(a, 1, 2)) if transposed else (lambda a: a)
        res = _adam_stacked(lands, sel, flip(w[name]), flip(m[name]), flip(v[name]), f"adam_{name}")
        done.append(res[1])
        put(name, [flip(r) for r in res])

    ffn_in_lands = [landed["wt_ffn_in0"], landed["wt_ffn_in1"]]
    adam("cd_w_in", [landed["wt_cd_in"]], 0, True)
    adam("ffn_w_gate", ffn_in_lands, 0, True)
    adam("ffn_w_up", ffn_in_lands, 1, True)
    adam("cd_w_out", [landed["w_cd_out"]], 0, False)
    adam("ab_w_out", [landed["w_ab_out"]], 0, False)
    adam("ffn_w_down", [landed["w_ffn_down0"], landed["w_ffn_down1"]], 0, False)

    red = _sum_slots(small_land).reshape(-1)
    offs = [0]
    for s in smalls["sizes"]:
        offs.append(offs[-1] + s)
    seg = [red[offs[i]:offs[i + 1]] for i in range(len(smalls["sizes"]))]
    loss = jnp.sum(seg[14])

    def own_channels(full, taps):
        return lax.dynamic_slice_in_dim(full.reshape(taps, 512), me * SHARD_C, SHARD_C, axis=1)

    g_small = {
        "sgu_norm_g": seg[0].reshape(1, 512), "sgu_norm_b": seg[1].reshape(1, 512),
        "sgu_w": seg[2].reshape(512, 128), "sgu_bias": seg[3].reshape(4, 128), "q_norm_g": seg[4].reshape(3, 64),
        "k_norm_g": seg[5].reshape(3, 64),
        "cd_norm_g": lax.dynamic_slice_in_dim(seg[6].reshape(1, D), me * (D // NDEV), D // NDEV, axis=1),
        "conv_c_w": own_channels(seg[7], CONV_C_TAPS), "conv_c_b": own_channels(seg[8], 1),
        "c_ln_g": own_channels(seg[9], 1), "c_ln_b": own_channels(seg[10], 1),
        "conv_d_w": own_channels(seg[11], CONV_D_TAPS),
        "ffn_norm_g": jnp.concatenate([seg[12].reshape(1, D), seg[13].reshape(1, D)], axis=0),
    }

    def small_in(s, a):
        return jnp.swapaxes(a, 0, 1) if len(s) == 3 else a.reshape(s)

    def small_out(n, s, a):
        return jnp.swapaxes(a, 0, 1) if len(s) == 3 else a.reshape(w[n].shape)

    g_in = [g_small[n].reshape(s) for n, s in SMALL_SHAPES]
    d_s, m_s, v_s = _adam_small([small_in(s, w[n]) for n, s in SMALL_SHAPES], g_in,
                                [small_in(s, m[n]) for n, s in SMALL_SHAPES],
                                [small_in(s, v[n]) for n, s in SMALL_SHAPES])
    for i, (n, s) in enumerate(SMALL_SHAPES):
        grads[n], deltas[n] = small_out(n, s, g_in[i]), small_out(n, s, d_s[i])
        new_m[n], new_v[n] = small_out(n, s, m_s[i]), small_out(n, s, v_s[i])
    done.append(d_s[0])

    late_land, = wait_scatters("scatter_wait_last", ["wt_ab_in"], [late_small], list(done))
    put("ab_norm_g", _adam_of_slots(late_land, w["ab_norm_g"], m["ab_norm_g"], v["ab_norm_g"], "adam_ab_norm_g"))
    adam("ab_w_in", [landed["wt_ab_in"]], 0, True)

    return (loss, grad_x[None], *[grads[n] for n in WEIGHT_NAMES], *[deltas[n] for n in WEIGHT_NAMES],
            *[new_m[n] for n in WEIGHT_NAMES], *[new_v[n] for n in WEIGHT_NAMES])
```

```python
import jax
import jax.numpy as jnp
import numpy as np
from jax import lax
from jax.experimental import pallas as pl
from jax.experimental.pallas import tpu as pltpu

F32 = jnp.float32
BF16 = jnp.bfloat16

T = 4096
D = 1024
NDEV = 8
EPS = 1e-6
NEG_INF = -1e30
DFF = 2816
AB_IN = 5632
CD_IN = 2560
HEAD = 64
PAIR = 128
NPAIR = 4
NBACK = 128
DIL_RATES = (1, 4, 16)
ROPE_HALF = 8
ROPE_THETA = 500000.0
CONV_C_TAPS = 31
CONV_D_TAPS = 3
HALO = 32
ATTN_BWD_UNROLL = 4
MAX_ROW_STRIDE = 4

ADAM_LR = 0.001
ADAM_B1 = 0.9
ADAM_B2 = 0.999
ADAM_EPS = 1e-08
ADAM_WD = 0.01
ADAM_STEP = 10
ADAM_C1 = 1.0 / (1.0 - ADAM_B1 ** ADAM_STEP)
ADAM_C2 = 1.0 / (1.0 - ADAM_B2 ** ADAM_STEP)

VMEM_LIMIT_MB = 48
MESH = pl.DeviceIdType.MESH
HBM_SPEC = pl.BlockSpec(memory_space=pl.ANY)


def _cparams(ngrid, vmem_mb=VMEM_LIMIT_MB):
    return pltpu.CompilerParams(dimension_semantics=("arbitrary",) * ngrid,
                                vmem_limit_bytes=vmem_mb * 1024 * 1024)


def _pick(n, options):
    for o in options:
        if n % o == 0:
            return o
    raise ValueError(f"no tile for {n} in {options}")


def _sds(shape, dtype):
    return jax.ShapeDtypeStruct(shape, dtype)


def _sigmoid(x):
    return 1.0 / (1.0 + jnp.exp(-x))


def _sigmoid_bf16(x):
    return 0.5 * jnp.tanh(0.5 * x) + 0.5


def _gelu(z):
    return 0.5 * z * (1.0 + lax.erf(z * 0.7071067811865476))


def _gelu_grad(z):
    return 0.5 * (1.0 + lax.erf(z * 0.7071067811865476)) + z * jnp.exp(-0.5 * z * z) * 0.3989422804014327


def _mm_nt(a, wt, name, out_dtype=BF16, dep=None):
    M, K = a.shape
    N = wt.shape[0]
    tn = _pick(N, (512, 256))

    def body(a_ref, w_ref, *rest):
        o_ref = rest[-1]
        for r0 in range(0, M, 1024):
            o_ref[r0:r0 + 1024, :] = lax.dot_general(
                a_ref[r0:r0 + 1024, :], w_ref[...], (((1,), (1,)), ((), ())),
                preferred_element_type=F32).astype(o_ref.dtype)

    in_specs = [pl.BlockSpec((M, K), lambda j: (0, 0), pipeline_mode=pl.Buffered(1)),
                pl.BlockSpec((tn, K), lambda j: (j, 0))]
    args = [a, wt]
    if dep is not None:
        in_specs.append(HBM_SPEC)
        args.append(dep)
    return pl.pallas_call(
        body, name=name, grid=(N // tn,), in_specs=in_specs, out_specs=pl.BlockSpec((M, tn), lambda j: (0, j)),
        out_shape=_sds((M, N), out_dtype), compiler_params=_cparams(1))(*args)


EPI_ROWS = 256


def _mm_nt_rows(a, wt, name, epilogue, side, side_specs, out_specs, out_shape, sums=(), dep=None, tm=512):
    M, K = a.shape
    N = wt.shape[0]
    ns, no = len(side), len(out_shape)

    def body(a_ref, w_ref, *rest):
        side_refs, outs, acc = rest[0:ns], rest[-1 - no:-1], rest[-1]
        acc[...] = lax.dot_general(a_ref[...], w_ref[...], (((1,), (1,)), ((), ())), preferred_element_type=F32)

        @pl.when(pl.program_id(0) == 0)
        def _():
            for j in sums:
                outs[j][...] = jnp.zeros_like(outs[j])

        for r0 in range(0, tm, EPI_ROWS):
            rows = slice(r0, r0 + EPI_ROWS)
            epilogue(acc[rows, :].astype(BF16).astype(F32), rows, side_refs, outs)

    in_specs = [pl.BlockSpec((tm, K), lambda i: (i, 0)),
                pl.BlockSpec((N, K), lambda i: (0, 0), pipeline_mode=pl.Buffered(1))] + list(side_specs)
    args = [a, wt, *side]
    if dep is not None:
        in_specs.append(HBM_SPEC)
        args.append(dep)
    return pl.pallas_call(
        body, name=name, grid=(M // tm,), in_specs=in_specs, out_specs=list(out_specs), out_shape=list(out_shape),
        scratch_shapes=[pltpu.VMEM((tm, N), F32)], compiler_params=_cparams(1))(*args)


def _mm_nn(a, w, name, mode, resid, gain=None, tgt=None, dep=None, tm=512):
    M, K = a.shape
    N = w.shape[1]
    side = gain if mode == "rms" else tgt

    def body(a_ref, w_ref, resid_ref, side_ref, *rest):
        outs, acc = rest[-3 if mode == "rms" else -4:-1], rest[-1]
        i = pl.program_id(0)
        acc[...] = jnp.dot(a_ref[...], w_ref[...], preferred_element_type=F32)

        if mode == "loss":
            @pl.when(i == 0)
            def _():
                outs[2][...] = jnp.zeros_like(outs[2])

        for r0 in range(0, tm, EPI_ROWS):
            rows = slice(r0, r0 + EPI_ROWS)
            v = acc[rows, :] + resid_ref[rows, :]
            if mode == "rms":
                outs[0][rows, :] = v
                r = lax.rsqrt(jnp.mean(v * v, axis=-1, keepdims=True) + EPS)
                outs[1][rows, :] = (v * r * side_ref[...]).astype(BF16)
            else:
                d = v - side_ref[rows, :]
                outs[2][...] += jnp.sum(d * d, axis=0, keepdims=True) * (0.5 / N)
                dy = d * (1.0 / N)
                outs[0][rows, :] = dy
                outs[1][rows, :] = dy.astype(BF16)

    row = pl.BlockSpec((tm, N), lambda i: (i, 0))
    vec = pl.BlockSpec((1, N), lambda i: (0, 0))
    in_specs = [pl.BlockSpec((tm, K), lambda i: (i, 0)),
                pl.BlockSpec((K, N), lambda i: (0, 0), pipeline_mode=pl.Buffered(1)), row,
                vec if mode == "rms" else row]
    args = [a, w, resid, side]
    if dep is not None:
        in_specs.append(HBM_SPEC)
        args.append(dep)
    if mode == "rms":
        out_specs, out_shape = [row, row], [_sds((M, N), F32), _sds((M, N), BF16)]
    else:
        out_specs, out_shape = [row, row, vec], [_sds((M, N), F32), _sds((M, N), BF16), _sds((1, N), F32)]
    return pl.pallas_call(
        body, name=name, grid=(M // tm,), in_specs=in_specs, out_specs=out_specs, out_shape=out_shape,
        scratch_shapes=[pltpu.VMEM((tm, N), F32)], compiler_params=_cparams(1))(*args)


def _mm_dh_rms_bwd(a, w, x, gain, dres, name, dep=None, tm=512, bf16_copy=True):
    parts = a.shape[0] if a.ndim == 3 else 1
    M, Kp = a.shape[-2], a.shape[-1]
    N = w.shape[1]
    nblk = M // tm
    assert nblk % 2 == 0

    def body(a_ref, w_ref, x_ref, g_ref, dres_ref, *rest):
        dg_ref, acc0, acc1 = rest[-3:]
        dx_ref = rest[-5] if bf16_copy else rest[-4]
        dxb_ref = rest[-4] if bf16_copy else None
        i = pl.program_id(0)

        def matmul(acc):
            if parts == 1:
                acc[...] = jnp.dot(a_ref[...], w_ref[...], preferred_element_type=F32)
            else:
                d = jnp.dot(a_ref[0], w_ref[0:Kp, :], preferred_element_type=F32)
                for p in range(1, parts):
                    d = d + jnp.dot(a_ref[p], w_ref[p * Kp:(p + 1) * Kp, :], preferred_element_type=F32)
                acc[...] = d

        def finish(acc):
            for r0 in range(0, tm, EPI_ROWS // 2):
                rows = slice(r0, r0 + EPI_ROWS // 2)
                v = acc[rows, :]
                xf = x_ref[rows, :]
                r = lax.rsqrt(jnp.mean(xf * xf, axis=-1, keepdims=True) + EPS)
                xhat = xf * r
                dg_ref[...] += jnp.sum(v * xhat, axis=0, keepdims=True)
                dxh = v * g_ref[...]
                tot = dres_ref[rows, :] + r * (dxh - xhat * jnp.mean(dxh * xhat, axis=-1, keepdims=True))
                dx_ref[rows, :] = tot
                if bf16_copy:
                    dxb_ref[rows, :] = tot.astype(BF16)

        @pl.when(i == 0)
        def _():
            dg_ref[...] = jnp.zeros_like(dg_ref)
            matmul(acc0)

        @pl.when((i > 0) & (i < nblk) & (i % 2 == 1))
        def _():
            matmul(acc1)
            finish(acc0)

        @pl.when((i > 0) & (i < nblk) & (i % 2 == 0))
        def _():
            matmul(acc0)
            finish(acc1)

        @pl.when(i == nblk)
        def _():
            finish(acc1)

    last = nblk - 1
    row = pl.BlockSpec((tm, N), lambda i: (jnp.maximum(i - 1, 0), 0))
    vec = pl.BlockSpec((1, N), lambda i: (0, 0))
    if a.ndim == 3:
        a_spec = pl.BlockSpec((parts, tm, Kp), lambda i: (0, jnp.minimum(i, last), 0))
    else:
        a_spec = pl.BlockSpec((tm, Kp), lambda i: (jnp.minimum(i, last), 0))
    w_spec = pl.BlockSpec((parts * Kp, N), lambda i: (0, 0), pipeline_mode=pl.Buffered(1))
    in_specs = [a_spec, w_spec, row, vec, row]
    args = [a, w, x, gain, dres]
    if dep is not None:
        in_specs.append(HBM_SPEC)
        args.append(dep)
    return pl.pallas_call(
        body, name=name, grid=(nblk + 1,), in_specs=in_specs,
        out_specs=[row, row, vec] if bf16_copy else [row, vec],
        out_shape=([_sds((M, N), F32), _sds((M, N), BF16), _sds((1, N), F32)] if bf16_copy
                   else [_sds((M, N), F32), _sds((1, N), F32)]),
        scratch_shapes=[pltpu.VMEM((tm, N), F32), pltpu.VMEM((tm, N), F32)], compiler_params=_cparams(1, 56))(*args)


def _mm_tn(a, b, name, out_dtype=BF16, tt=2048, dep=None):
    parts = a.shape[0] if a.ndim == 3 else 1
    Tt, Mp = a.shape[-2], a.shape[-1]
    N = b.shape[1]
    tn = _pick(Mp, (1408, 1280, 1024, 512))
    jper = Mp // tn
    nt = Tt // tt

    def body(a_ref, b_ref, *rest):
        o_ref, acc = rest[-2:]
        t = pl.program_id(1)

        @pl.when(t == 0)
        def _():
            acc[...] = jnp.zeros_like(acc)

        rows = pl.ds(pl.multiple_of(t * tt, tt), tt)
        acc[...] += lax.dot_general(a_ref[...], b_ref[rows, :], (((0,), (0,)), ((), ())),
                                    preferred_element_type=F32)

        @pl.when(t == nt - 1)
        def _():
            o_ref[...] = acc[...].astype(o_ref.dtype)

    if a.ndim == 3:
        a_spec = pl.BlockSpec((None, tt, tn), lambda j, t: (j // jper, t, j % jper))
    else:
        a_spec = pl.BlockSpec((tt, tn), lambda j, t: (t, j))
    in_specs = [a_spec, pl.BlockSpec((Tt, N), lambda j, t: (0, 0), pipeline_mode=pl.Buffered(1))]
    args = [a, b]
    if dep is not None:
        in_specs.append(HBM_SPEC)
        args.append(dep)
    return pl.pallas_call(
        body, name=name, grid=(parts * jper, nt), in_specs=in_specs,
        out_specs=pl.BlockSpec((tn, N), lambda j, t: (j, 0)),
        out_shape=_sds((parts * Mp, N), out_dtype), scratch_shapes=[pltpu.VMEM((tn, N), F32)],
        compiler_params=_cparams(2))(*args)


FFN_ROWS = 256
PREFETCH_SLOTS = 3


def _ffn_in(h, wt_in, name, tn=256):
    nj = DFF // tn

    def body(h_ref, wg_ref, wu_ref, p_ref, act_ref):
        nt = (((1,), (1,)), ((), ()))
        for r0 in range(0, T, FFN_ROWS):
            rows = slice(r0, r0 + FFN_ROWS)
            g = lax.dot_general(h_ref[rows, :], wg_ref[...], nt, preferred_element_type=F32).astype(BF16)
            u = lax.dot_general(h_ref[rows, :], wu_ref[...], nt, preferred_element_type=F32).astype(BF16)
            p_ref[0, rows, :] = g
            p_ref[1, rows, :] = u
            act_ref[rows, :] = g * _sigmoid_bf16(g) * u

    return pl.pallas_call(
        body, name=name, grid=(nj,),
        in_specs=[pl.BlockSpec((T, D), lambda j: (0, 0), pipeline_mode=pl.Buffered(1)),
                  pl.BlockSpec((tn, D), lambda j: (j, 0)), pl.BlockSpec((tn, D), lambda j: (j + nj, 0))],
        out_specs=[pl.BlockSpec((2, T, tn), lambda j: (0, 0, j)), pl.BlockSpec((T, tn), lambda j: (0, j))],
        out_shape=[_sds((2, T, DFF), BF16), _sds((T, DFF), BF16)], compiler_params=_cparams(1))(h, wt_in, wt_in)


def _ffn_dact(dyb, w_down, p3, name, tn=256, dep=None):
    nj = DFF // tn

    def body(dy_ref, w_ref, p_hbm, *rest):
        o_ref, p_buf, sem = rest[-3:]
        j = pl.program_id(0)

        def fetch(step):
            slot = step % PREFETCH_SLOTS
            cols = pl.ds(pl.multiple_of(step * tn, tn), tn)
            return pltpu.make_async_copy(p_hbm.at[:, :, cols], p_buf.at[slot], sem.at[slot])

        @pl.when(j == 0)
        def _():
            for s in range(PREFETCH_SLOTS - 1):
                fetch(s).start()

        @pl.when(j + PREFETCH_SLOTS - 1 < nj)
        def _():
            fetch(j + PREFETCH_SLOTS - 1).start()

        fetch(j).wait()
        p_ref = p_buf.at[j % PREFETCH_SLOTS]
        for r0 in range(0, T, FFN_ROWS):
            rows = slice(r0, r0 + FFN_ROWS)
            da = lax.dot_general(dy_ref[rows, :], w_ref[...], (((1,), (1,)), ((), ())),
                                 preferred_element_type=F32).astype(BF16)
            g = p_ref[0, rows, :]
            u = p_ref[1, rows, :]
            sg = _sigmoid_bf16(g)
            gs = g * sg
            o_ref[0, rows, :] = (da * u) * (sg + gs * (1.0 - sg))
            o_ref[1, rows, :] = da * gs

    in_specs = [pl.BlockSpec((T, D), lambda j: (0, 0), pipeline_mode=pl.Buffered(1)),
                pl.BlockSpec((tn, D), lambda j: (j, 0)), HBM_SPEC]
    args = [dyb, w_down, p3]
    if dep is not None:
        in_specs.append(HBM_SPEC)
        args.append(dep)
    return pl.pallas_call(
        body, name=name, grid=(nj,), in_specs=in_specs, out_specs=pl.BlockSpec((2, T, tn), lambda j: (0, 0, j)),
        out_shape=_sds((2, T, DFF), BF16),
        scratch_shapes=[pltpu.VMEM((PREFETCH_SLOTS, 2, T, tn), BF16), pltpu.SemaphoreType.DMA((PREFETCH_SLOTS,))],
        compiler_params=_cparams(1))(*args)


def _rms_fwd(x, g, name, tm=512, dep=None):
    def body(x_ref, g_ref, *rest):
        h_ref = rest[-1]
        xf = x_ref[...]
        r = lax.rsqrt(jnp.mean(xf * xf, axis=-1, keepdims=True) + EPS)
        h_ref[...] = (xf * r * g_ref[...]).astype(BF16)

    in_specs = [pl.BlockSpec((tm, D), lambda i: (i, 0)), pl.BlockSpec((1, D), lambda i: (0, 0))]
    args = [x, g]
    if dep is not None:
        in_specs.append(HBM_SPEC)
        args.append(dep)
    return pl.pallas_call(
        body, name=name, grid=(T // tm,), in_specs=in_specs, out_specs=pl.BlockSpec((tm, D), lambda i: (i, 0)),
        out_shape=_sds((T, D), BF16), compiler_params=_cparams(1))(*args)


def _tril_mask():
    r = lax.broadcasted_iota(jnp.int32, (128, 128), 0)
    c = lax.broadcasted_iota(jnp.int32, (128, 128), 1)
    return r >= c


def _mix_a_fwd(pab, sgu_g, sgu_b, sgu_w, sgu_bias3, tm=512):
    def body(zu_ref, zv_ref, g_ref, b_ref, w_ref, bias_ref, o_ref):
        u = _gelu(zu_ref[...].astype(F32))
        v = _gelu(zv_ref[...].astype(F32))
        mu = jnp.mean(v, axis=-1, keepdims=True)
        vc = v - mu
        rstd = lax.rsqrt(jnp.mean(vc * vc, axis=-1, keepdims=True) + EPS)
        vn = (vc * rstd * g_ref[...] + b_ref[...]).astype(BF16)
        tri = _tril_mask()
        for gi in range(4):
            wg = jnp.where(tri, w_ref[gi], 0.0).astype(BF16)
            bg = bias_ref[gi]
            for c in range(tm // 128):
                rs, cs = slice(c * 128, (c + 1) * 128), slice(gi * 128, (gi + 1) * 128)
                mixed = jnp.dot(wg, vn[rs, cs], preferred_element_type=F32) + bg
                o_ref[rs, cs] = (u[rs, cs] * mixed).astype(BF16)

    half = pl.BlockSpec((tm, 512), lambda i: (i, 0))
    return pl.pallas_call(
        body, name="mix_a_fwd", grid=(T // tm,),
        in_specs=[half, pl.BlockSpec((tm, 512), lambda i: (i, 1)),
                  pl.BlockSpec((1, 512), lambda i: (0, 0)), pl.BlockSpec((1, 512), lambda i: (0, 0)),
                  pl.BlockSpec((4, 128, 128), lambda i: (0, 0, 0)), pl.BlockSpec((4, 128, 1), lambda i: (0, 0, 0))],
        out_specs=half, out_shape=_sds((T, D), BF16), compiler_params=_cparams(1),
    )(pab, pab, sgu_g, sgu_b, sgu_w, sgu_bias3)


def _rope_tables():
    pos = np.arange(T, dtype=np.float32)
    inv_freq = np.float32(ROPE_THETA) ** (-np.arange(ROPE_HALF, dtype=np.float32) * np.float32(2.0 / (2 * ROPE_HALF)))
    ang = (pos[:, None] * inv_freq[None, :]).astype(np.float32)
    cos, sin = np.cos(ang), np.sin(ang)
    z8 = np.zeros((T, ROPE_HALF), np.float32)
    rest = np.zeros((T, HEAD - 2 * ROPE_HALF), np.float32)
    c64 = np.concatenate([cos, cos, rest + 1.0], axis=1)
    s1 = np.concatenate([z8, sin, rest], axis=1)
    s2 = np.concatenate([-sin, z8, rest], axis=1)
    return tuple(jnp.asarray(np.tile(t, (1, 2)).astype(np.float32)) for t in (c64, s1, s2))


def _lo_mask(shape):
    return lax.broadcasted_iota(jnp.int32, shape, 1) < HEAD


def _seg_mean(x, lo):
    s_all = jnp.sum(x, axis=-1, keepdims=True)
    s_lo = jnp.sum(jnp.where(lo, x, 0.0), axis=-1, keepdims=True)
    return jnp.where(lo, s_lo, s_all - s_lo) * (1.0 / HEAD)


def _head_blocks():
    r = lax.broadcasted_iota(jnp.int32, (PAIR, PAIR), 0) < HEAD
    c = lax.broadcasted_iota(jnp.int32, (PAIR, PAIR), 1) < HEAD
    return jnp.where(r == c, 1.0, 0.0).astype(BF16)


def _seg_mean_mxu(x, blocks):
    return jnp.dot(x.astype(BF16), blocks, preferred_element_type=F32) * (1.0 / HEAD)


def _rope(n, c, s1, s2):
    return n * c + pltpu.roll(n, ROPE_HALF, 1) * s1 + pltpu.roll(n, PAIR - ROPE_HALF, 1) * s2


def _rope_t(dy, c, s1, s2):
    return dy * c - pltpu.roll(dy, PAIR - ROPE_HALF, 1) * s2 - pltpu.roll(dy, ROPE_HALF, 1) * s1


def _prep_fwd(pab, qg, kg, tabs, tm=512):
    def body(p_ref, qg_ref, kg_ref, c_ref, s1_ref, s2_ref, *outs):
        blocks = _head_blocks()
        c, s1, s2 = c_ref[...], s1_ref[...], s2_ref[...]
        for g in range(3):
            qn_ref, kn_ref, v_ref = outs[3 * g:3 * g + 3]
            for p in range(NPAIR):
                for which, gains, dst in ((0, qg_ref, qn_ref), (1, kg_ref, kn_ref)):
                    col = (2 + 3 * which + g) * 512 + p * PAIR
                    xr = p_ref[:, col:col + PAIR].astype(F32)
                    rinv = lax.rsqrt(_seg_mean_mxu(xr * xr, blocks) + EPS)
                    outs[9 + 2 * g + which][p] = rinv.astype(BF16)
                    dst[p] = _rope(xr * rinv * gains[g:g + 1, :], c, s1, s2)
                col = (8 + g) * 512 + p * PAIR
                v_ref[p] = p_ref[:, col:col + PAIR].astype(F32)

    pm = pl.BlockSpec((NPAIR, tm, PAIR), lambda i: (0, i, 0))
    tab = pl.BlockSpec((tm, PAIR), lambda i: (i, 0))
    gain = pl.BlockSpec((3, PAIR), lambda i: (0, 0))
    res = pl.pallas_call(
        body, name="prep_fwd", grid=(T // tm,),
        in_specs=[pl.BlockSpec((tm, AB_IN), lambda i: (i, 0)), gain, gain, tab, tab, tab],
        out_specs=[pm] * 15, out_shape=[_sds((NPAIR, T, PAIR), F32)] * 9 + [_sds((NPAIR, T, PAIR), BF16)] * 6,
        compiler_params=_cparams(1))(pab, qg, kg, *tabs)
    return res[0:9], res[9:15]


def _res_index(it, rate):
    window = NBACK * rate
    b = it // rate
    rho = it % rate
    start = b * window + rho
    startp = jnp.maximum(start - window, rho)
    kmin = jnp.where(b > 0, 0, NBACK)
    return start, startp, kmin


def _rows(start, rate):
    if rate == 1:
        return pl.ds(pl.multiple_of(start, NBACK), NBACK)
    return pl.ds(start, NBACK, stride=rate)


def _band_bias():
    qs = lax.broadcasted_iota(jnp.int32, (2 * NBACK, 2 * NBACK), 0)
    kj = lax.broadcasted_iota(jnp.int32, (2 * NBACK, 2 * NBACK), 1)
    dist = (qs & (NBACK - 1)) + NBACK - kj
    both = (dist >= 0) & (dist <= NBACK)
    return jnp.where(both, 0.0, NEG_INF), jnp.where(both & (kj >= NBACK), 0.0, NEG_INF)


def _attn_fwd_block(q, kcat, vcat, first, lo, biases):
    vcat1 = jnp.concatenate([vcat, jnp.ones((2 * NBACK, PAIR), BF16)], axis=1)
    q2 = jnp.concatenate([jnp.where(lo, q, 0.0), jnp.where(lo, 0.0, q)], axis=0).astype(BF16)
    s = lax.dot_general(q2, kcat, (((1,), (1,)), ((), ())), preferred_element_type=F32)
    s = s + jnp.where(first, biases[1], biases[0])
    m = jnp.max(s, axis=-1, keepdims=True)
    ol = jnp.dot(jnp.exp(s - m).astype(BF16), vcat1, preferred_element_type=F32)
    o2 = ol[:, 0:PAIR] / ol[:, PAIR:]
    ls = m + jnp.log(ol[:, PAIR:])
    return jnp.where(lo, o2[0:NBACK], o2[NBACK:]), jnp.where(lo, ls[0:NBACK], ls[NBACK:])


def _attn_fwd(qn, kn, v, rate, name, dep=None):
    if rate > MAX_ROW_STRIDE:
        return _attn_fwd_gathered(qn, kn, v, rate, name, dep)

    def body(q_ref, k_ref, v_ref, *rest):
        o_ref, l_ref = rest[-2:]
        lo = _lo_mask((NBACK, PAIR))
        biases = _band_bias()

        def step(it, carry):
            start, startp, kmin = _res_index(it, rate)
            q = q_ref[_rows(start, rate), :] * (HEAD ** -0.5)
            kcat = jnp.concatenate([k_ref[_rows(startp, rate), :], k_ref[_rows(start, rate), :]], axis=0).astype(BF16)
            vcat = jnp.concatenate([v_ref[_rows(startp, rate), :], v_ref[_rows(start, rate), :]], axis=0).astype(BF16)
            o, ls = _attn_fwd_block(q, kcat, vcat, kmin != 0, lo, biases)
            o_ref[_rows(start, rate), :] = o
            l_ref[_rows(start, rate), :] = ls
            return carry

        lax.fori_loop(0, T // NBACK, step, 0, unroll=4)

    pm = pl.BlockSpec((None, T, PAIR), lambda p: (p, 0, 0))
    in_specs, args = [pm, pm, pm], [qn, kn, v]
    if dep is not None:
        in_specs.append(HBM_SPEC)
        args.append(dep)
    return pl.pallas_call(
        body, name=name, grid=(NPAIR,), in_specs=in_specs, out_specs=[pm, pm],
        out_shape=[_sds((NPAIR, T, PAIR), F32)] * 2, compiler_params=_cparams(1))(*args)


def _attn_fwd_gathered(qn, kn, v, rate, name, dep):
    n = T // rate
    nblk = n // NBACK

    def body(q_hbm, k_hbm, v_hbm, *rest):
        o_hbm, l_hbm, qb, kb, vb, ob, lb, in_sem, out_sem = rest[-9:]
        p = pl.program_id(0)
        slot = p % 2

        def loads(pair, s):
            return [pltpu.make_async_copy(x.at[pair, :, r, :], buf.at[s, r], in_sem.at[3 * s + a])
                    for a, (x, buf) in enumerate(((q_hbm, qb), (k_hbm, kb), (v_hbm, vb))) for r in range(rate)]

        def stores(pair, s):
            return [pltpu.make_async_copy(buf.at[s, r], x.at[pair, :, r, :], out_sem.at[2 * s + a])
                    for a, (x, buf) in enumerate(((o_hbm, ob), (l_hbm, lb))) for r in range(rate)]

        @pl.when(p == 0)
        def _():
            for c in loads(0, 0):
                c.start()

        @pl.when(p + 1 < NPAIR)
        def _():
            for c in loads(p + 1, 1 - slot):
                c.start()

        for c in loads(p, slot):
            c.wait()

        @pl.when(p >= 2)
        def _():
            for c in stores(p - 2, slot):
                c.wait()

        lo = _lo_mask((NBACK, PAIR))
        biases = _band_bias()

        def step(it, carry):
            b, r = it % nblk, it // nblk
            cur = pl.ds(pl.multiple_of(b * NBACK, NBACK), NBACK)
            prev = pl.ds(pl.multiple_of(jnp.maximum(b - 1, 0) * NBACK, NBACK), NBACK)
            q = qb[slot, r, cur, :] * (HEAD ** -0.5)
            kcat = jnp.concatenate([kb[slot, r, prev, :], kb[slot, r, cur, :]], axis=0).astype(BF16)
            vcat = jnp.concatenate([vb[slot, r, prev, :], vb[slot, r, cur, :]], axis=0).astype(BF16)
            o, ls = _attn_fwd_block(q, kcat, vcat, b == 0, lo, biases)
            ob[slot, r, cur, :] = o
            lb[slot, r, cur, :] = ls
            return carry

        lax.fori_loop(0, T // NBACK, step, 0, unroll=4)

        for c in stores(p, slot):
            c.start()

        @pl.when(p == NPAIR - 1)
        def _():
            for c in stores(p - 1, 1 - slot) + stores(p, slot):
                c.wait()

    by_residue = lambda a: a.reshape(NPAIR, n, rate, PAIR)
    in_specs, args = [HBM_SPEC] * 3, [by_residue(qn), by_residue(kn), by_residue(v)]
    if dep is not None:
        in_specs.append(HBM_SPEC)
        args.append(dep)
    o, l = pl.pallas_call(
        body, name=name, grid=(NPAIR,), in_specs=in_specs, out_specs=[HBM_SPEC] * 2,
        out_shape=[_sds((NPAIR, n, rate, PAIR), F32)] * 2,
        scratch_shapes=[pltpu.VMEM((2, rate, n, PAIR), F32)] * 5
        + [pltpu.SemaphoreType.DMA((6,)), pltpu.SemaphoreType.DMA((4,))],
        compiler_params=_cparams(1))(*args)
    return o.reshape(NPAIR, T, PAIR), l.reshape(NPAIR, T, PAIR)


def _merge_fwd(cat_ab, outs, lses, tm=512):
    def body(cat_in, o0, o1, o2, l0, l1, l2, cat_ref, lse_ref):
        del cat_in
        for p in range(NPAIR):
            a0, a1, a2 = l0[p], l1[p], l2[p]
            m = jnp.maximum(jnp.maximum(a0, a1), a2)
            w0, w1, w2 = jnp.exp(a0 - m), jnp.exp(a1 - m), jnp.exp(a2 - m)
            s = w0 + w1 + w2
            b = (w0 * o0[p] + w1 * o1[p] + w2 * o2[p]) / s
            cat_ref[:, p * PAIR:(p + 1) * PAIR] = b.astype(BF16)
            lse_ref[p] = m + jnp.log(s)

    pm = pl.BlockSpec((NPAIR, tm, PAIR), lambda i: (0, i, 0))
    return pl.pallas_call(
        body, name="merge_fwd", grid=(T // tm,),
        in_specs=[pl.BlockSpec(memory_space=pl.ANY)] + [pm] * 6,
        out_specs=[pl.BlockSpec((tm, 512), lambda i: (i, 1)), pm],
        out_shape=[_sds((T, D), BF16), _sds((NPAIR, T, PAIR), F32)],
        input_output_aliases={0: 0}, compiler_params=_cparams(1))(cat_ab, *outs, *lses)


def _d_cat_ab(dxb, w_ab_out, cat, dep, tm=512):
    def epilogue(d, rows, side, outs):
        (b_ref,), (da_ref, dbp_ref, e_ref) = side, outs
        da_ref[rows, :] = d[:, 0:512].astype(BF16)
        lo = _lo_mask((EPI_ROWS, PAIR))
        for p in range(NPAIR):
            db = d[:, 512 + p * PAIR:512 + (p + 1) * PAIR]
            b = b_ref[rows, p * PAIR:(p + 1) * PAIR].astype(F32)
            dbp_ref[p, rows, :] = db
            e_ref[p, rows, :] = _seg_mean(db * b, lo) * float(HEAD)

    pm = pl.BlockSpec((NPAIR, tm, PAIR), lambda i: (0, i, 0))
    return _mm_nt_rows(
        dxb, w_ab_out, "mm_d_cat_ab", epilogue, [cat], [pl.BlockSpec((tm, 512), lambda i: (i, 1))],
        [pl.BlockSpec((tm, 512), lambda i: (i, 0)), pm, pm],
        [_sds((T, 512), BF16), _sds((NPAIR, T, PAIR), F32), _sds((NPAIR, T, PAIR), F32)], dep=dep, tm=tm)


def _attn_bwd_block(q, db, ev, ls, kcat, vcat, first, lo, biases):
    scale = HEAD ** -0.5
    nt = (((1,), (1,)), ((), ()))
    tn = (((0,), (0,)), ((), ()))
    q = q * scale
    q2 = jnp.concatenate([jnp.where(lo, q, 0.0), jnp.where(lo, 0.0, q)], axis=0).astype(BF16)
    db2 = jnp.concatenate([jnp.where(lo, db, 0.0), jnp.where(lo, 0.0, db)], axis=0).astype(BF16)
    ls2 = jnp.concatenate([ls[:, 0:1], ls[:, HEAD:HEAD + 1]], axis=0)
    ev2 = jnp.concatenate([ev[:, 0:1], ev[:, HEAD:HEAD + 1]], axis=0)
    s = lax.dot_general(q2, kcat, nt, preferred_element_type=F32)
    pt = jnp.exp(s + jnp.where(first, biases[1], biases[0]) - ls2)
    dp = lax.dot_general(db2, vcat, nt, preferred_element_type=F32)
    ds = (pt * (dp - ev2)).astype(BF16)
    dq2 = jnp.dot(ds, kcat, preferred_element_type=F32) * scale
    dkc = lax.dot_general(ds, q2, tn, preferred_element_type=F32)
    dvc = lax.dot_general(pt.astype(BF16), db2, tn, preferred_element_type=F32)
    return jnp.where(lo, dq2[0:NBACK], dq2[NBACK:]), dkc, dvc


def _attn_bwd_loop(read, write, nblk):
    lo = _lo_mask((NBACK, PAIR))
    biases = _band_bias()

    def one(it, carry):
        dk_carry, dv_carry = carry
        rho = it // nblk
        b = it % nblk
        bp = jnp.maximum(b - 1, 0)
        kcat = jnp.concatenate([read(1, rho, bp), read(1, rho, b)], axis=0).astype(BF16)
        vcat = jnp.concatenate([read(2, rho, bp), read(2, rho, b)], axis=0).astype(BF16)
        dq, dkc, dvc = _attn_bwd_block(read(0, rho, b), read(3, rho, b), read(4, rho, b), read(5, rho, b), kcat, vcat,
                                       b == 0, lo, biases)
        write(0, rho, b, dq)
        write(1, rho, bp, dk_carry + dkc[0:NBACK])
        write(1, rho, b, dkc[NBACK:])
        write(2, rho, bp, dv_carry + dvc[0:NBACK])
        write(2, rho, b, dvc[NBACK:])
        return dkc[NBACK:], dvc[NBACK:]

    def step(i, carry):
        for u in range(ATTN_BWD_UNROLL):
            carry = one(i * ATTN_BWD_UNROLL + u, carry)
        return carry

    zero = jnp.zeros((NBACK, PAIR), F32)
    lax.fori_loop(0, T // NBACK // ATTN_BWD_UNROLL, step, (zero, zero))


def _attn_bwd(qn, kn, v, dbp, e, lse, rate, name):
    if rate > MAX_ROW_STRIDE:
        return _attn_bwd_gathered(qn, kn, v, dbp, e, lse, rate, name)
    window = NBACK * rate

    def body(*refs):
        rows = lambda rho, b: _rows(b * window + rho, rate)

        def write(j, rho, b, value):
            refs[6 + j][rows(rho, b), :] = value

        _attn_bwd_loop(lambda j, rho, b: refs[j][rows(rho, b), :], write, T // window)

    pm = pl.BlockSpec((None, T, PAIR), lambda p: (p, 0, 0))
    return pl.pallas_call(
        body, name=name, grid=(NPAIR,), in_specs=[pm] * 6, out_specs=[pm] * 3,
        out_shape=[_sds((NPAIR, T, PAIR), F32)] * 3, compiler_params=_cparams(1, 56))(qn, kn, v, dbp, e, lse)


def _attn_bwd_gathered(qn, kn, v, dbp, e, lse, rate, name):
    n = T // rate

    def body(*refs):
        ins, outs, in_bufs, out_bufs, (in_sem, out_sem) = refs[0:6], refs[6:9], refs[9:15], refs[15:18], refs[18:20]
        p = pl.program_id(0)
        slot = p % 2

        def loads(pair, s):
            return [pltpu.make_async_copy(x.at[pair, :, r, :], buf.at[s, r], in_sem.at[6 * s + a])
                    for a, (x, buf) in enumerate(zip(ins, in_bufs)) for r in range(rate)]

        def stores(pair, s):
            return [pltpu.make_async_copy(buf.at[s, r], x.at[pair, :, r, :], out_sem.at[3 * s + a])
                    for a, (x, buf) in enumerate(zip(outs, out_bufs)) for r in range(rate)]

        @pl.when(p == 0)
        def _():
            for c in loads(0, 0):
                c.start()

        @pl.when(p + 1 < NPAIR)
        def _():
            for c in loads(p + 1, 1 - slot):
                c.start()

        for c in loads(p, slot):
            c.wait()

        @pl.when(p >= 2)
        def _():
            for c in stores(p - 2, slot):
                c.wait()

        rows = lambda b: pl.ds(pl.multiple_of(b * NBACK, NBACK), NBACK)

        def write(j, rho, b, value):
            out_bufs[j][slot, rho, rows(b), :] = value

        _attn_bwd_loop(lambda j, rho, b: in_bufs[j][slot, rho, rows(b), :], write, n // NBACK)

        for c in stores(p, slot):
            c.start()

        @pl.when(p == NPAIR - 1)
        def _():
            for c in stores(p - 1, 1 - slot) + stores(p, slot):
                c.wait()

    by_residue = lambda a: a.reshape(NPAIR, n, rate, PAIR)
    res = pl.pallas_call(
        body, name=name, grid=(NPAIR,), in_specs=[HBM_SPEC] * 6, out_specs=[HBM_SPEC] * 3,
        out_shape=[_sds((NPAIR, n, rate, PAIR), F32)] * 3,
        scratch_shapes=[pltpu.VMEM((2, rate, n, PAIR), F32)] * 9
        + [pltpu.SemaphoreType.DMA((12,)), pltpu.SemaphoreType.DMA((6,))],
        compiler_params=_cparams(1, 56))(*[by_residue(a) for a in (qn, kn, v, dbp, e, lse)])
    return [r.reshape(NPAIR, T, PAIR) for r in res]


def _ab_in_bwd(pab, dcat, sgu_g, sgu_b, sgu_w, sgu_bias3, qg, kg, tabs, dqkv, rinvs, tm=256):
    def body(p_ref, dcat_ref, g_ref, b_ref, w_ref, bias_ref, qg_ref, kg_ref, c_ref, s1_ref, s2_ref, *rest):
        dq_refs, rinv_refs = rest[0:9], rest[9:15]
        o_ref, dwm_ref, dbias_ref, dsg_ref, dsb_ref, dgain_ref = rest[15:]
        i = pl.program_id(0)

        @pl.when(i == 0)
        def _():
            dwm_ref[...] = jnp.zeros_like(dwm_ref)
            dbias_ref[...] = jnp.zeros_like(dbias_ref)
            dsg_ref[...] = jnp.zeros_like(dsg_ref)
            dsb_ref[...] = jnp.zeros_like(dsb_ref)
            dgain_ref[...] = jnp.zeros_like(dgain_ref)

        zu = p_ref[:, 0:512].astype(F32)
        zv = p_ref[:, 512:1024].astype(F32)
        u = _gelu(zu)
        v = _gelu(zv)
        mu = jnp.mean(v, axis=-1, keepdims=True)
        vc = v - mu
        rstd = lax.rsqrt(jnp.mean(vc * vc, axis=-1, keepdims=True) + EPS)
        xhat = vc * rstd
        vn = (xhat * g_ref[...] + b_ref[...]).astype(BF16)
        da = dcat_ref[...].astype(F32)
        tri = _tril_mask()
        du_parts = [[None] * 4 for _ in range(tm // 128)]
        dvn_parts = [[None] * 4 for _ in range(tm // 128)]
        for gi in range(4):
            wg = jnp.where(tri, w_ref[gi], 0.0).astype(BF16)
            bg = bias_ref[gi]
            for c in range(tm // 128):
                rs, cs = slice(c * 128, (c + 1) * 128), slice(gi * 128, (gi + 1) * 128)
                vblk = vn[rs, cs]
                mixed = jnp.dot(wg, vblk, preferred_element_type=F32) + bg
                dab = da[rs, cs]
                du_parts[c][gi] = dab * mixed
                dmixed = dab * u[rs, cs]
                dmb = dmixed.astype(BF16)
                dvn_parts[c][gi] = lax.dot_general(wg, dmb, (((0,), (0,)), ((), ())), preferred_element_type=F32)
                dwm = lax.dot_general(dmb, vblk, (((1,), (1,)), ((), ())), preferred_element_type=F32)
                dwm_ref[gi] += jnp.where(tri, dwm, 0.0)
                dbias_ref[gi] += dmixed
        du = jnp.concatenate([jnp.concatenate(r, axis=1) for r in du_parts], axis=0)
        dvn = jnp.concatenate([jnp.concatenate(r, axis=1) for r in dvn_parts], axis=0)
        dsg_ref[...] += jnp.sum(dvn * xhat, axis=0, keepdims=True)
        dsb_ref[...] += jnp.sum(dvn, axis=0, keepdims=True)
        dxh = dvn * g_ref[...]
        dv = rstd * (dxh - jnp.mean(dxh, axis=-1, keepdims=True)
                     - xhat * jnp.mean(dxh * xhat, axis=-1, keepdims=True))
        o_ref[:, 0:512] = (du * _gelu_grad(zu)).astype(BF16)
        o_ref[:, 512:1024] = (dv * _gelu_grad(zv)).astype(BF16)

        blocks = _head_blocks()
        c, s1, s2 = c_ref[...], s1_ref[...], s2_ref[...]
        for g in range(3):
            dq_ref, dk_ref, dv_ref = dq_refs[3 * g:3 * g + 3]
            for p in range(NPAIR):
                for which, gains, src in ((0, qg_ref, dq_ref), (1, kg_ref, dk_ref)):
                    col = (2 + 3 * which + g) * 512 + p * PAIR
                    xr = p_ref[:, col:col + PAIR].astype(F32)
                    rinv = rinv_refs[2 * g + which][p].astype(F32)
                    xh = xr * rinv
                    dn = _rope_t(src[p], c, s1, s2)
                    row = 2 * g + which
                    dgain_ref[row:row + 1, :] += jnp.sum(dn * xh, axis=0, keepdims=True)
                    dxh2 = dn * gains[g:g + 1, :]
                    dx = rinv * (dxh2 - xh * _seg_mean_mxu(dxh2 * xh, blocks))
                    o_ref[:, col:col + PAIR] = dx.astype(BF16)
                col = (8 + g) * 512 + p * PAIR
                o_ref[:, col:col + PAIR] = dv_ref[p].astype(BF16)

    pm = pl.BlockSpec((NPAIR, tm, PAIR), lambda i: (0, i, 0))
    tab = pl.BlockSpec((tm, PAIR), lambda i: (i, 0))
    gain = pl.BlockSpec((3, PAIR), lambda i: (0, 0))
    vec = pl.BlockSpec((1, 512), lambda i: (0, 0))
    full = pl.BlockSpec((tm, AB_IN), lambda i: (i, 0))
    w4 = pl.BlockSpec((4, 128, 128), lambda i: (0, 0, 0))
    return pl.pallas_call(
        body, name="ab_in_bwd", grid=(T // tm,),
        in_specs=[full, pl.BlockSpec((tm, 512), lambda i: (i, 0)), vec, vec, w4,
                  pl.BlockSpec((4, 128, 1), lambda i: (0, 0, 0)), gain, gain, tab, tab, tab] + [pm] * 15,
        out_specs=[full, w4, w4, vec, vec, pl.BlockSpec((8, PAIR), lambda i: (0, 0))],
        out_shape=[_sds((T, AB_IN), BF16), _sds((4, 128, 128), F32), _sds((4, 128, 128), F32),
                   _sds((1, 512), F32), _sds((1, 512), F32), _sds((8, PAIR), F32)],
        compiler_params=_cparams(1))(pab, dcat, sgu_g, sgu_b, sgu_w, sgu_bias3, qg, kg, *tabs, *dqkv, *rinvs)


def _ln_stats(x):
    mu = jnp.mean(x, axis=-1, keepdims=True)
    xc = x - mu
    rstd = lax.rsqrt(jnp.mean(xc * xc, axis=-1, keepdims=True) + EPS)
    return xc * rstd, rstd


CONV_RC = 64


def _shifted_copies(src, dst, tm):
    dst[0] = src[...]
    for b in range(1, 8):
        dst[b, 0:tm + HALO - 8, :] = src[pl.ds(b, tm + HALO - 8), :]


def _offsets_by_phase(first):
    groups = {}
    for o in range(first, first + CONV_C_TAPS):
        groups.setdefault(o % 8, []).append(o)
    return sorted(groups.items())


def _window(shifted, b8, base, offsets, lanes):
    rows = 8 * (max(offsets) // 8) + CONV_RC
    return shifted[b8, pl.ds(base, rows), lanes].reshape(rows // 8, 8, 128)


def _cd_fwd(pcd, cw, cb, lg, lb, dw, tm=512):
    per = tm // HALO

    def body(p_ref, h_ref, cw_ref, cb_ref, lg_ref, lb_ref, dw_ref, cat_ref, c0_ref, c1_ref, dd_ref, y_ref,
             buf, buf2, sb):
        i = pl.program_id(0)
        live = jnp.where(i > 0, 1.0, 0.0)
        a = p_ref[:, 0:512].astype(F32)
        gt = p_ref[:, 512:1024].astype(F32)
        gb = p_ref[:, 1024:1536].astype(F32)
        gc = p_ref[:, 1536:2048].astype(F32)
        hv = p_ref[:, 2048:2560].astype(F32)
        c0 = a * _sigmoid(gt)
        dd = gc * hv
        buf[0:HALO, :] = h_ref[:, 0:512].astype(F32) * _sigmoid(h_ref[:, 512:1024].astype(F32)) * live
        buf[HALO:, :] = c0
        buf2[0:HALO, :] = h_ref[:, 1536:2048].astype(F32) * h_ref[:, 2048:2560].astype(F32) * live
        buf2[HALO:, :] = dd
        c0_ref[...] = c0.astype(BF16)
        dd_ref[...] = dd.astype(BF16)
        _shifted_copies(buf, sb, tm)

        def conv_rows(r, carry):
            base = pl.multiple_of(r * CONV_RC, CONV_RC)
            for c in range(4):
                lanes = slice(c * 128, (c + 1) * 128)
                acc = jnp.broadcast_to(cb_ref[:, lanes], (CONV_RC // 8, 8, 128))
                for b8, offsets in _offsets_by_phase(HALO - (CONV_C_TAPS - 1)):
                    win = _window(sb, b8, base, offsets, lanes)
                    for o in offsets:
                        j = o - (HALO - (CONV_C_TAPS - 1))
                        acc = acc + cw_ref[8 * j:8 * j + 8, lanes] * win[o // 8:o // 8 + CONV_RC // 8]
                c1_ref[pl.ds(base, CONV_RC), lanes] = acc.reshape(CONV_RC, 128)
            return carry

        lax.fori_loop(0, tm // CONV_RC, conv_rows, 0)
        xhat, _ = _ln_stats(c1_ref[...])
        c2 = xhat * lg_ref[...] + lb_ref[...]
        y = jnp.zeros((tm, 512), F32)
        for j in range(CONV_D_TAPS):
            y = y + dw_ref[j:j + 1, :] * buf2[pl.ds(HALO - (CONV_D_TAPS - 1) + j, tm), :]
        cat_ref[:, 0:512] = (c2 * _sigmoid(c2)).astype(BF16)
        cat_ref[:, 512:1024] = (gb * y).astype(BF16)
        y_ref[...] = y.astype(BF16)

    half = pl.BlockSpec((tm, 512), lambda i: (i, 0))
    vec = pl.BlockSpec((1, 512), lambda i: (0, 0))
    return pl.pallas_call(
        body, name="cd_fwd", grid=(T // tm,),
        in_specs=[pl.BlockSpec((tm, CD_IN), lambda i: (i, 0)),
                  pl.BlockSpec((HALO, CD_IN), lambda i: (jnp.maximum(i * per - 1, 0), 0)),
                  pl.BlockSpec((8 * 32, 512), lambda i: (0, 0)), vec, vec, vec, pl.BlockSpec((8, 512), lambda i: (0, 0))],
        out_specs=[pl.BlockSpec((tm, D), lambda i: (i, 0)), half, half, half, half],
        out_shape=[_sds((T, D), BF16), _sds((T, 512), BF16), _sds((T, 512), F32), _sds((T, 512), BF16),
                   _sds((T, 512), BF16)],
        scratch_shapes=[pltpu.VMEM((HALO + tm, 512), F32), pltpu.VMEM((HALO + tm, 512), F32),
                        pltpu.VMEM((8, HALO + tm, 512), F32)],
        compiler_params=_cparams(1))(pcd, pcd, cw, cb, lg, lb, dw)


def _d_cat_cd(dxb, w_cd_out, c1, pcd, y, lg, lb, dep, tm=512):
    def epilogue(d, rows, side, outs):
        c1_ref, gb_ref, y_ref, lg_ref, lb_ref = side
        dc1_ref, dy3_ref, dgb_ref, dlg_ref, dlb_ref, dcb_ref = outs
        dc, ddo = d[:, 0:512], d[:, 512:1024]
        xhat, rstd = _ln_stats(c1_ref[rows, :])
        c2 = xhat * lg_ref[...] + lb_ref[...]
        sg = _sigmoid(c2)
        dc2 = dc * sg * (1.0 + c2 * (1.0 - sg))
        dlg_ref[...] += jnp.sum(dc2 * xhat, axis=0, keepdims=True)
        dlb_ref[...] += jnp.sum(dc2, axis=0, keepdims=True)
        dxh = dc2 * lg_ref[...]
        dc1 = rstd * (dxh - jnp.mean(dxh, axis=-1, keepdims=True)
                      - xhat * jnp.mean(dxh * xhat, axis=-1, keepdims=True))
        dcb_ref[...] += jnp.sum(dc1, axis=0, keepdims=True)
        dc1_ref[rows, :] = dc1
        dgb_ref[rows, :] = (ddo * y_ref[rows, :].astype(F32)).astype(BF16)
        dy3_ref[rows, :] = ddo * gb_ref[rows, :].astype(F32)

    half = pl.BlockSpec((tm, 512), lambda i: (i, 0))
    vec = pl.BlockSpec((1, 512), lambda i: (0, 0))
    return _mm_nt_rows(
        dxb, w_cd_out, "mm_d_cat_cd", epilogue, [c1, pcd, y, lg, lb],
        [half, pl.BlockSpec((tm, 512), lambda i: (i, 2)), half, vec, vec], [half, half, half, vec, vec, vec],
        [_sds((T, 512), F32), _sds((T, 512), F32), _sds((T, 512), BF16),
         _sds((1, 512), F32), _sds((1, 512), F32), _sds((1, 512), F32)], sums=(3, 4, 5), dep=dep, tm=tm)


def _cd_bwd_conv(pcd, dc1, dy3, c0, dd, dgb, cw8, dw, tm=256):
    per = tm // HALO
    nblk = T // tm
    last32 = T // HALO - 1

    def body(p_ref, dc1_ref, dc1n_ref, dy3_ref, dy3n_ref, c0_ref, dd_ref, dgb_ref, cw_ref, dw_ref,
             o_ref, dcw_ref, ddw_ref, dbuf, d3buf, sd, dc0_buf):
        i = pl.program_id(0)
        has_next = jnp.where(i < nblk - 1, 1.0, 0.0)

        @pl.when(i == 0)
        def _():
            dcw_ref[...] = jnp.zeros_like(dcw_ref)
            ddw_ref[...] = jnp.zeros_like(ddw_ref)

        dbuf[0:tm, :] = dc1_ref[...]
        dbuf[tm:, :] = dc1n_ref[...] * has_next
        d3buf[0:tm, :] = dy3_ref[...]
        d3buf[tm:, :] = dy3n_ref[...] * has_next
        _shifted_copies(dbuf, sd, tm)
        n_tiles = tm // CONV_RC

        phases = _offsets_by_phase(0)

        def dc0_rows(r, carry):
            base = pl.multiple_of(r * CONV_RC, CONV_RC)
            for c in range(4):
                lanes = slice(c * 128, (c + 1) * 128)
                acc = jnp.zeros((CONV_RC // 8, 8, 128), F32)
                for b8, offsets in phases:
                    win = _window(sd, b8, base, offsets, lanes)
                    for o in offsets:
                        j = CONV_C_TAPS - 1 - o
                        acc = acc + cw_ref[8 * j:8 * j + 8, lanes] * win[o // 8:o // 8 + CONV_RC // 8]
                dc0_buf[pl.ds(base, CONV_RC), lanes] = acc.reshape(CONV_RC, 128)
            return carry

        lax.fori_loop(0, n_tiles, dc0_rows, 0)

        for c in range(4):
            lanes = slice(c * 128, (c + 1) * 128)
            for b8, offsets in phases:
                def dw_rows(r, accs, lanes=lanes, b8=b8, offsets=offsets):
                    base = pl.multiple_of(r * CONV_RC, CONV_RC)
                    xin = c0_ref[pl.ds(base, CONV_RC), lanes].astype(F32).reshape(CONV_RC // 8, 8, 128)
                    win = _window(sd, b8, base, offsets, lanes)
                    return tuple(acc + jnp.sum(xin * win[o // 8:o // 8 + CONV_RC // 8], axis=0)
                                 for acc, o in zip(accs, offsets))

                accs = lax.fori_loop(0, n_tiles, dw_rows, tuple(jnp.zeros((8, 128), F32) for _ in offsets))
                for acc, o in zip(accs, offsets):
                    j = CONV_C_TAPS - 1 - o
                    dcw_ref[j:j + 1, lanes] += jnp.sum(acc, axis=0, keepdims=True)

        dc0 = dc0_buf[...]
        ddin = dd_ref[...].astype(F32)
        ddd = jnp.zeros((tm, 512), F32)
        for j in range(CONV_D_TAPS):
            dy_shift = d3buf[pl.ds(CONV_D_TAPS - 1 - j, tm), :]
            ddd = ddd + dw_ref[j:j + 1, :] * dy_shift
            ddw_ref[j:j + 1, :] += jnp.sum(ddin * dy_shift, axis=0, keepdims=True)

        a = p_ref[:, 0:512].astype(F32)
        gt = p_ref[:, 512:1024].astype(F32)
        gc = p_ref[:, 1536:2048].astype(F32)
        hv = p_ref[:, 2048:2560].astype(F32)
        sg = _sigmoid(gt)
        o_ref[:, 0:512] = (dc0 * sg).astype(BF16)
        o_ref[:, 512:1024] = (dc0 * a * sg * (1.0 - sg)).astype(BF16)
        o_ref[:, 1024:1536] = dgb_ref[...]
        o_ref[:, 1536:2048] = (ddd * hv).astype(BF16)
        o_ref[:, 2048:2560] = (ddd * gc).astype(BF16)

    half = pl.BlockSpec((tm, 512), lambda i: (i, 0))
    nxt = pl.BlockSpec((HALO, 512), lambda i: (jnp.minimum((i + 1) * per, last32), 0))
    full = pl.BlockSpec((tm, CD_IN), lambda i: (i, 0))
    return pl.pallas_call(
        body, name="cd_bwd_conv", grid=(nblk,),
        in_specs=[full, half, nxt, half, nxt, half, half, half,
                  pl.BlockSpec((8 * 32, 512), lambda i: (0, 0)), pl.BlockSpec((8, 512), lambda i: (0, 0))],
        out_specs=[full, pl.BlockSpec((32, 512), lambda i: (0, 0)), pl.BlockSpec((8, 512), lambda i: (0, 0))],
        out_shape=[_sds((T, CD_IN), BF16), _sds((32, 512), F32), _sds((8, 512), F32)],
        scratch_shapes=[pltpu.VMEM((tm + HALO, 512), F32), pltpu.VMEM((tm + HALO, 512), F32),
                        pltpu.VMEM((8, tm + HALO, 512), F32), pltpu.VMEM((tm, 512), F32)],
        compiler_params=_cparams(1))(pcd, dc1, dc1, dy3, dy3, c0, dd, dgb, cw8, dw)


def _local_step(x, tgt, W, fetch=None, on_grad=None):
    W = dict(W)
    if fetch is None:
        fetch = lambda stage, after: {}
    if on_grad is None:
        on_grad = lambda key, arr: None
    tabs = _rope_tables()
    qg = jnp.tile(W["q_norm_g"], (1, 2))
    kg = jnp.tile(W["k_norm_g"], (1, 2))
    bias3 = W["sgu_bias"].reshape(4, 128, 1)
    G = {}

    h0 = _rms_fwd(x, W["ab_norm_g"], "rms_fwd_ab", dep=W.get("dep_first"))
    W.update(fetch("ab_in", h0))
    pab = _mm_nt(h0, W["wt_ab_in"], "mm_ab_in", dep=W.get("dep0"))
    cat_ab = _mix_a_fwd(pab, W["sgu_norm_g"], W["sgu_norm_b"], W["sgu_w"], bias3)
    qkv, rinvs = _prep_fwd(pab, qg, kg, tabs)
    outs, lses = [], []
    for g, rate in enumerate(DIL_RATES):
        o, l = _attn_fwd(qkv[3 * g], qkv[3 * g + 1], qkv[3 * g + 2], rate, f"attn_fwd_{g}", dep=W.get(f"dep_attn{g}"))
        outs.append(o)
        lses.append(l)
        W.update(fetch(f"attn{g}", o))
    cat_ab, lse = _merge_fwd(cat_ab, outs, lses)
    W.update(fetch("ab_out", lse))
    x1, h1 = _mm_nn(cat_ab, W["w_ab_out"], "mm_ab_out", mode="rms", resid=x, gain=W["ffn_norm_g"][0:1])
    pf0, act0 = _ffn_in(h1, W["wt_ffn_in0"], "ffn_in0")
    W.update(fetch("ffn_down0", act0))
    x2, h2 = _mm_nn(act0, W["w_ffn_down0"], "mm_ffn_down0", mode="rms", resid=x1, gain=W["cd_norm_g"],
                    dep=W.get("dep_down0"))
    W.update(fetch("cd_in", h2))
    pcd = _mm_nt(h2, W["wt_cd_in"], "mm_cd_in")
    cw8 = jnp.repeat(W["conv_c_w32"], 8, axis=0)
    cat_cd, c0, c1, dd, yv = _cd_fwd(pcd, cw8, W["conv_c_b"], W["c_ln_g"], W["c_ln_b"], W["conv_d_w8"])
    x3, h3 = _mm_nn(cat_cd, W["w_cd_out"], "mm_cd_out", mode="rms", resid=x2, gain=W["ffn_norm_g"][1:2])
    pf1, act1 = _ffn_in(h3, W["wt_ffn_in1"], "ffn_in1")
    dy, dyb, loss_cols = _mm_nn(act1, W["w_ffn_down1"], "mm_ffn_down1", mode="loss", resid=x3, tgt=tgt)

    def ffn_bwd(xin, h, pf, act, dres, dresb, layer):
        G[f"w_ffn_down{layer}"] = _mm_tn(act, dresb, f"mm_g_ffn_down{layer}")
        dep = on_grad(f"w_ffn_down{layer}", G[f"w_ffn_down{layer}"])
        dpf = _ffn_dact(dresb, W[f"w_ffn_down{layer}"], pf, f"ffn_dact{layer}", dep=dep)
        G[f"wt_ffn_in{layer}"] = _mm_tn(dpf, h, f"mm_g_ffn_in{layer}")
        dep = on_grad(f"wt_ffn_in{layer}", G[f"wt_ffn_in{layer}"])
        dx, dxb, G[f"ffn_norm_g{layer}"] = _mm_dh_rms_bwd(
            dpf, W[f"wt_ffn_in{layer}"], xin, W["ffn_norm_g"][layer:layer + 1], dres, f"mm_d_h_ffn{layer}", dep=dep)
        return dx, dxb

    dx3, dx3b = ffn_bwd(x3, h3, pf1, act1, dy, dyb, 1)

    G["w_cd_out"] = _mm_tn(cat_cd, dx3b, "mm_g_cd_out")
    dep = on_grad("w_cd_out", G["w_cd_out"])
    dc1, dy3, dgb, G["c_ln_g"], G["c_ln_b"], G["conv_c_b"] = _d_cat_cd(
        dx3b, W["w_cd_out"], c1, pcd, yv, W["c_ln_g"], W["c_ln_b"], dep)
    dpcd, G["conv_c_w32"], G["conv_d_w8"] = _cd_bwd_conv(pcd, dc1, dy3, c0, dd, dgb, cw8, W["conv_d_w8"])
    G["wt_cd_in"] = _mm_tn(dpcd, h2, "mm_g_cd_in")
    dep = on_grad("wt_cd_in", G["wt_cd_in"])
    dx2, dx2b, G["cd_norm_g"] = _mm_dh_rms_bwd(dpcd, W["wt_cd_in"], x2, W["cd_norm_g"], dx3, "mm_d_h_cd", dep=dep)

    dx1, dx1b = ffn_bwd(x1, h1, pf0, act0, dx2, dx2b, 0)

    G["w_ab_out"] = _mm_tn(cat_ab, dx1b, "mm_g_ab_out")
    dep = on_grad("w_ab_out", G["w_ab_out"])
    dcat_a, dbp, e = _d_cat_ab(dx1b, W["w_ab_out"], cat_ab, dep)
    dqkv = []
    for g, rate in enumerate(DIL_RATES):
        dqkv += _attn_bwd(qkv[3 * g], qkv[3 * g + 1], qkv[3 * g + 2], dbp, e, lse, rate, f"attn_bwd_{g}")
    dpab, G["sgu_w"], dbias_part, G["sgu_norm_g"], G["sgu_norm_b"], dgain = _ab_in_bwd(
        pab, dcat_a, W["sgu_norm_g"], W["sgu_norm_b"], W["sgu_w"], bias3, qg, kg, tabs, dqkv, rinvs)
    G["sgu_bias"] = jnp.sum(dbias_part, axis=-1)
    dgain = dgain[0:6, 0:HEAD] + dgain[0:6, HEAD:PAIR]
    G["q_norm_g"] = dgain[0::2]
    G["k_norm_g"] = dgain[1::2]
    G["loss_cols"] = loss_cols
    dep = on_grad("small", G)
    G["wt_ab_in"] = _mm_tn(dpab, h0, "mm_g_ab_in", dep=dep)
    dep = on_grad("wt_ab_in", G["wt_ab_in"])
    grad_x, G["ab_norm_g"] = _mm_dh_rms_bwd(dpab, W["wt_ab_in"], x, W["ab_norm_g"], dx1, "mm_d_h_ab", dep=dep,
                                            bf16_copy=False)
    return loss_cols, grad_x, G


def _my_place():
    return lax.axis_index("x"), lax.axis_index("y"), lax.axis_index("c")


def _dev_index(px, py, pc):
    return 4 * px + 2 * py + pc


def _flip(place, k):
    x, y, c = place
    return (1 - x if k & 4 else x, 1 - y if k & 2 else y, 1 - c if k & 1 else c)


def _landing(shape, dtype, own):
    buf = lax.empty(shape, dtype)
    for lead, part in own:
        buf = lax.dynamic_update_slice(buf, part.reshape((1,) * len(lead) + part.shape),
                                       tuple(lead) + (0,) * part.ndim)
    return buf


HBM_ONLY = pl.BlockSpec(memory_space=pltpu.HBM)
SEM_SPEC = pl.BlockSpec(memory_space=pltpu.SEMAPHORE)
IN_FLIGHT = pltpu.CompilerParams(has_side_effects=pltpu.SideEffectType.DATAFLOW_SIDE_EFFECTING)


def _in_hbm(a):
    return pltpu.with_memory_space_constraint(a, pltpu.HBM)


def _exchange_start(name, srcs, lands, items, dep=None):
    ns, nl, ni = len(srcs), len(lands), len(items)

    def body(*refs):
        S, L = refs[0:ns], refs[ns:ns + nl]
        first_out = ns + nl + (0 if dep is None else 1)
        send_sems, recv_sems, token = refs[first_out], refs[first_out + 1], refs[-1]
        me = _my_place()
        mi = _dev_index(*me)
        for i, (src, dst) in enumerate(items):
            for k in range(1, NDEV):
                peer = _flip(me, k)
                pltpu.make_async_remote_copy(
                    src_ref=src(S, _dev_index(*peer)), dst_ref=dst(L, mi), send_sem=send_sems.at[7 * i + k - 1],
                    recv_sem=recv_sems.at[7 * i + k - 1], device_id=peer, device_id_type=MESH).start()
        token[...] = jnp.zeros_like(token)

    thru = [pltpu.HBM(a.shape, a.dtype) for a in list(srcs) + list(lands)]
    args = [_in_hbm(a) for a in srcs] + [_in_hbm(a) for a in lands]
    in_specs = [HBM_ONLY] * (ns + nl)
    if dep is not None:
        args.append(dep)
        in_specs.append(HBM_SPEC)
    outs = pl.pallas_call(
        body, name=name, in_specs=in_specs,
        out_shape=(pltpu.SemaphoreType.DMA((7 * ni,)), pltpu.SemaphoreType.DMA((7 * ni,)), *thru, _sds((8, 128), F32)),
        out_specs=(SEM_SPEC, SEM_SPEC, *[HBM_ONLY] * (ns + nl), pl.BlockSpec(memory_space=pltpu.VMEM)),
        input_output_aliases={j: 2 + j for j in range(ns + nl)}, compiler_params=IN_FLIGHT)(*args)
    return dict(send=outs[0], recv=outs[1], srcs=list(outs[2:2 + ns]), lands=list(outs[2 + ns:2 + ns + nl]),
                token=outs[-1], items=items)


def _exchange_wait(name, states, after):
    after = list(after) if isinstance(after, (list, tuple)) else [after]
    counts = [(len(st["srcs"]), len(st["lands"]), len(st["items"])) for st in states]
    n_arrays = sum(c[0] + c[1] for c in counts)

    def body(*refs):
        me = _my_place()
        mi = _dev_index(*me)
        pos = 0
        sem_pos = n_arrays
        for st, (ns, nl, ni) in zip(states, counts):
            S, L = refs[pos:pos + ns], refs[pos + ns:pos + ns + nl]
            send_sems, recv_sems = refs[sem_pos], refs[sem_pos + 1]
            pos += ns + nl
            sem_pos += 2
            for i, (src, dst) in enumerate(st["items"]):
                for k in range(1, NDEV):
                    cp = pltpu.make_async_remote_copy(
                        src_ref=src(S, mi), dst_ref=dst(L, mi), send_sem=send_sems.at[7 * i + k - 1],
                        recv_sem=recv_sems.at[7 * i + k - 1], device_id=me, device_id_type=MESH)
                    cp.wait_send()
                    cp.wait_recv()

    arrays, sems = [], []
    for st in states:
        arrays += st["srcs"] + st["lands"]
        sems += [st["send"], st["recv"]]
    outs = pl.pallas_call(
        body, name=name, in_specs=[HBM_ONLY] * n_arrays + [SEM_SPEC] * len(sems) + [HBM_SPEC] * len(after),
        out_shape=tuple(pltpu.HBM(a.shape, a.dtype) for a in arrays), out_specs=tuple([HBM_ONLY] * n_arrays),
        input_output_aliases={j: j for j in range(n_arrays)}, compiler_params=IN_FLIGHT)(*arrays, *sems, *after)
    lands, pos = [], 0
    for ns, nl, _ in counts:
        lands.append(list(outs[pos + ns:pos + ns + nl]))
        pos += ns + nl
    return lands


def _place_and_neighbours():
    x, y, c = _my_place()
    return (x, y, c), (x, y, 1 - c), [(1 - x, y), (x, 1 - y), (1 - x, 1 - y)]


def _gather_start(name, srcs, lands, items, dep=None):
    ns, nl, ni = len(srcs), len(lands), len(items)

    def body(*refs):
        S, L = refs[0:ns], refs[ns:ns + nl]
        first_out = ns + nl + (0 if dep is None else 1)
        send_sems, recv_sems, token = refs[first_out], refs[first_out + 1], refs[-1]
        me, sib, chips = _place_and_neighbours()
        mi = _dev_index(*me)
        for i, (src, dst) in enumerate(items):
            for k, to in enumerate([sib] + [(*chip, me[2]) for chip in chips]):
                pltpu.make_async_remote_copy(
                    src_ref=src(S), dst_ref=dst(L, mi), send_sem=send_sems.at[4 * i + k],
                    recv_sem=recv_sems.at[4 * i + k], device_id=to, device_id_type=MESH).start()
        token[...] = jnp.zeros_like(token)

    thru = [pltpu.HBM(a.shape, a.dtype) for a in list(srcs) + list(lands)]
    args = [_in_hbm(a) for a in srcs] + [_in_hbm(a) for a in lands]
    in_specs = [HBM_ONLY] * (ns + nl)
    if dep is not None:
        args.append(dep)
        in_specs.append(HBM_SPEC)
    outs = pl.pallas_call(
        body, name=name, in_specs=in_specs,
        out_shape=(pltpu.SemaphoreType.DMA((4 * ni,)), pltpu.SemaphoreType.DMA((4 * ni,)), *thru, _sds((8, 128), F32)),
        out_specs=(SEM_SPEC, SEM_SPEC, *[HBM_ONLY] * (ns + nl), pl.BlockSpec(memory_space=pltpu.VMEM)),
        input_output_aliases={j: 2 + j for j in range(ns + nl)}, compiler_params=IN_FLIGHT)(*args)
    return dict(send=outs[0], recv=outs[1], srcs=list(outs[2:2 + ns]), lands=list(outs[2 + ns:2 + ns + nl]),
                token=outs[-1], items=items)


def _gather_forward(name, st, after):
    nl, ni = len(st["lands"]), len(st["items"])

    def body(*refs):
        L, recv_sems = refs[0:nl], refs[nl]
        fwd_send, fwd_recv, token = refs[-3:]
        me, sib, chips = _place_and_neighbours()
        for i, (_, dst) in enumerate(st["items"]):
            for j, chip in enumerate(chips):
                blk = dst(L, _dev_index(*chip, me[2]))
                pltpu.make_async_remote_copy(
                    src_ref=blk, dst_ref=blk, send_sem=fwd_send.at[3 * i + j], recv_sem=recv_sems.at[4 * i + 1 + j],
                    device_id=me, device_id_type=MESH).wait_recv()
                pltpu.make_async_remote_copy(
                    src_ref=blk, dst_ref=blk, send_sem=fwd_send.at[3 * i + j], recv_sem=fwd_recv.at[3 * i + j],
                    device_id=sib, device_id_type=MESH).start()
        token[...] = jnp.zeros_like(token)

    after = list(after) if isinstance(after, (list, tuple)) else [after]
    outs = pl.pallas_call(
        body, name=name, in_specs=[HBM_ONLY] * nl + [SEM_SPEC] + [HBM_SPEC] * len(after),
        out_shape=(*[pltpu.HBM(a.shape, a.dtype) for a in st["lands"]], pltpu.SemaphoreType.DMA((3 * ni,)),
                   pltpu.SemaphoreType.DMA((3 * ni,)), _sds((8, 128), F32)),
        out_specs=(*[HBM_ONLY] * nl, SEM_SPEC, SEM_SPEC, pl.BlockSpec(memory_space=pltpu.VMEM)),
        input_output_aliases={j: j for j in range(nl)}, compiler_params=IN_FLIGHT)(*st["lands"], st["recv"], *after)
    return dict(st, lands=list(outs[0:nl]), fwd_send=outs[nl], fwd_recv=outs[nl + 1], token=outs[-1])


def _gather_wait(name, st, after):
    ns, nl, ni = len(st["srcs"]), len(st["lands"]), len(st["items"])

    def body(*refs):
        S, L = refs[0:ns], refs[ns:ns + nl]
        send_sems, recv_sems, fwd_send, fwd_recv = refs[ns + nl:ns + nl + 4]
        me, sib, chips = _place_and_neighbours()
        mi = _dev_index(*me)
        for i, (src, dst) in enumerate(st["items"]):
            mine = dst(L, mi)
            for k in range(4):
                pltpu.make_async_remote_copy(
                    src_ref=src(S), dst_ref=mine, send_sem=send_sems.at[4 * i + k], recv_sem=recv_sems.at[4 * i + k],
                    device_id=me, device_id_type=MESH).wait_send()
            pltpu.make_async_remote_copy(
                src_ref=src(S), dst_ref=mine, send_sem=send_sems.at[4 * i], recv_sem=recv_sems.at[4 * i],
                device_id=me, device_id_type=MESH).wait_recv()
            for j in range(3):
                cp = pltpu.make_async_remote_copy(
                    src_ref=mine, dst_ref=mine, send_sem=fwd_send.at[3 * i + j], recv_sem=fwd_recv.at[3 * i + j],
                    device_id=me, device_id_type=MESH)
                cp.wait_send()
                cp.wait_recv()

    arrays = st["srcs"] + st["lands"]
    outs = pl.pallas_call(
        body, name=name, in_specs=[HBM_ONLY] * (ns + nl) + [SEM_SPEC] * 4 + [HBM_SPEC],
        out_shape=tuple(pltpu.HBM(a.shape, a.dtype) for a in arrays), out_specs=tuple([HBM_ONLY] * (ns + nl)),
        input_output_aliases={j: j for j in range(ns + nl)},
        compiler_params=IN_FLIGHT)(*arrays, st["send"], st["recv"], st["fwd_send"], st["fwd_recv"], after)
    return list(outs[ns:ns + nl])


def _sum_slots(land):
    def body(l_ref, o_ref):
        acc = l_ref[0]
        for d in range(1, NDEV):
            acc = acc + l_ref[d]
        o_ref[...] = acc

    vm = pl.BlockSpec(memory_space=pltpu.VMEM)
    return pl.pallas_call(body, name="sum_small", out_shape=_sds(land.shape[1:], F32), in_specs=[vm], out_specs=vm)(land)


def _adam_math(w, g, m, v):
    m2 = ADAM_B1 * m + (1.0 - ADAM_B1) * g
    v2 = ADAM_B2 * v + (1.0 - ADAM_B2) * (g * g)
    delta = -ADAM_LR * ((m2 * ADAM_C1) / (jnp.sqrt(v2 * ADAM_C2) + ADAM_EPS) + ADAM_WD * w)
    return delta, m2, v2


def _adam_layer(land, sel, w, m, v, layer, name, prev=None, tc=512):
    R = land.shape[2]

    def body(l_ref, w_ref, m_ref, v_ref, *rest):
        g_out, d_out, m_out, v_out = rest[-4:]
        g = l_ref[0].astype(F32)
        for d in range(1, NDEV):
            g = g + l_ref[d].astype(F32)
        delta, m2, v2 = _adam_math(w_ref[...], g, m_ref[...], v_ref[...])
        g_out[...] = g
        d_out[...] = delta
        m_out[...] = m2
        v_out[...] = v2

    wspec = pl.BlockSpec((None, R, tc), lambda i: (layer, 0, i))
    in_specs = [pl.BlockSpec((None, NDEV, R, tc), lambda i: (sel, 0, 0, i)), wspec, wspec, wspec]
    args = [land, w, m, v]
    aliases = {}
    if prev is not None:
        in_specs += [HBM_SPEC] * 4
        args += list(prev)
        aliases = {4 + j: j for j in range(4)}
    return pl.pallas_call(
        body, name=name, grid=(D // tc,), in_specs=in_specs, out_specs=[wspec] * 4,
        out_shape=[_sds(w.shape, F32)] * 4, input_output_aliases=aliases, compiler_params=_cparams(1))(*args)


def _adam_stacked(lands, sel, w, m, v, name):
    res = None
    for layer, land in enumerate(lands):
        res = _adam_layer(land, sel, w, m, v, layer, f"{name}{layer}", prev=res)
    return res


def _adam_small(ws, gs, ms, vs):
    n = len(ws)

    def body(*refs):
        w_r, g_r, m_r, v_r = refs[0:n], refs[n:2 * n], refs[2 * n:3 * n], refs[3 * n:4 * n]
        d_o, m_o, v_o = refs[4 * n:5 * n], refs[5 * n:6 * n], refs[6 * n:7 * n]
        for i in range(n):
            delta, m2, v2 = _adam_math(w_r[i][...], g_r[i][...], m_r[i][...], v_r[i][...])
            d_o[i][...] = delta
            m_o[i][...] = m2
            v_o[i][...] = v2

    vm = pl.BlockSpec(memory_space=pltpu.VMEM)
    shapes = [_sds(w.shape, F32) for w in ws]
    outs = pl.pallas_call(body, name="adam_small", in_specs=[vm] * (4 * n), out_specs=[vm] * (3 * n),
                          out_shape=shapes * 3)(*ws, *gs, *ms, *vs)
    return outs[0:n], outs[n:2 * n], outs[2 * n:3 * n]


def _adam_of_slots(land, w, m, v, name):
    def body(l_ref, w_ref, m_ref, v_ref, g_o, d_o, m_o, v_o):
        g = l_ref[0]
        for d in range(1, NDEV):
            g = g + l_ref[d]
        g_o[...] = g
        d_o[...], m_o[...], v_o[...] = _adam_math(w_ref[...], g, m_ref[...], v_ref[...])

    vm = pl.BlockSpec(memory_space=pltpu.VMEM)
    return pl.pallas_call(body, name=name, in_specs=[vm] * 4, out_specs=[vm] * 4,
                          out_shape=[_sds(w.shape, F32)] * 4)(land, w, m, v)


WEIGHT_NAMES = ("ab_norm_g", "ab_w_in", "sgu_norm_g", "sgu_norm_b", "sgu_w", "sgu_bias", "q_norm_g", "k_norm_g",
                "ab_w_out", "cd_norm_g", "cd_w_in", "conv_c_w", "conv_c_b", "c_ln_g", "c_ln_b", "conv_d_w",
                "cd_w_out", "ffn_norm_g", "ffn_w_gate", "ffn_w_up", "ffn_w_down")
SMALL_SHAPES = (("sgu_norm_g", (1, 512)), ("sgu_norm_b", (1, 512)), ("sgu_w", (512, 128)),
                ("sgu_bias", (4, 128)), ("q_norm_g", (3, 1, 64)), ("k_norm_g", (3, 1, 64)), ("cd_norm_g", (1, 128)),
                ("conv_c_w", (31, 1, 64)), ("conv_c_b", (1, 64)), ("c_ln_g", (1, 64)), ("c_ln_b", (1, 64)),
                ("conv_d_w", (3, 1, 64)), ("ffn_norm_g", (2, 1024)))
SHARD_C = 64


def _pack_rows(parts, rows):
    flat = jnp.concatenate([p.reshape(-1) for p in parts])
    return jnp.pad(flat, (0, rows * 128 - flat.shape[0])).reshape(rows, 128)


def kernel(x, ab_norm_g, ab_w_in, sgu_norm_g, sgu_norm_b, sgu_w, sgu_bias, q_norm_g, k_norm_g, ab_w_out, cd_norm_g, cd_w_in, conv_c_w, conv_c_b, c_ln_g, c_ln_b, conv_d_w, cd_w_out, ffn_norm_g, ffn_w_gate, ffn_w_up, ffn_w_down, loss_target, m_ab_norm_g, m_ab_w_in, m_sgu_norm_g, m_sgu_norm_b, m_sgu_w, m_sgu_bias, m_q_norm_g, m_k_norm_g, m_ab_w_out, m_cd_norm_g, m_cd_w_in, m_conv_c_w, m_conv_c_b, m_c_ln_g, m_c_ln_b, m_conv_d_w, m_cd_w_out, m_ffn_norm_g, m_ffn_w_gate, m_ffn_w_up, m_ffn_w_down, v_ab_norm_g, v_ab_w_in, v_sgu_norm_g, v_sgu_norm_b, v_sgu_w, v_sgu_bias, v_q_norm_g, v_k_norm_g, v_ab_w_out, v_cd_norm_g, v_cd_w_in, v_conv_c_w, v_conv_c_b, v_c_ln_g, v_c_ln_b, v_conv_d_w, v_cd_w_out, v_ffn_norm_g, v_ffn_w_gate, v_ffn_w_up, v_ffn_w_down):
    w = dict(zip(WEIGHT_NAMES, (ab_norm_g, ab_w_in, sgu_norm_g, sgu_norm_b, sgu_w, sgu_bias, q_norm_g, k_norm_g, ab_w_out, cd_norm_g, cd_w_in, conv_c_w, conv_c_b, c_ln_g, c_ln_b, conv_d_w, cd_w_out, ffn_norm_g, ffn_w_gate, ffn_w_up, ffn_w_down)))
    m = dict(zip(WEIGHT_NAMES, (m_ab_norm_g, m_ab_w_in, m_sgu_norm_g, m_sgu_norm_b, m_sgu_w, m_sgu_bias, m_q_norm_g, m_k_norm_g, m_ab_w_out, m_cd_norm_g, m_cd_w_in, m_conv_c_w, m_conv_c_b, m_c_ln_g, m_c_ln_b, m_conv_d_w, m_cd_w_out, m_ffn_norm_g, m_ffn_w_gate, m_ffn_w_up, m_ffn_w_down)))
    v = dict(zip(WEIGHT_NAMES, (v_ab_norm_g, v_ab_w_in, v_sgu_norm_g, v_sgu_norm_b, v_sgu_w, v_sgu_bias, v_q_norm_g, v_k_norm_g, v_ab_w_out, v_cd_norm_g, v_cd_w_in, v_conv_c_w, v_conv_c_b, v_c_ln_g, v_c_ln_b, v_conv_d_w, v_cd_w_out, v_ffn_norm_g, v_ffn_w_gate, v_ffn_w_up, v_ffn_w_down)))
    me = _dev_index(*_my_place())

    r_ff = DFF // NDEV
    one = lambda a: (lambda S, j: S[a])
    slot = lambda b: (lambda L, s: L[b].at[s])
    slot2 = lambda b, part: (lambda L, s: L[b].at[part, s])
    shard = lambda a: (lambda S: S[a])

    def later(a):
        return lax.optimization_barrier((a, gathers[0]["token"]))[0]

    def layer_shards(layer):
        return (later(w["ffn_w_gate"][layer]).T.astype(BF16), later(w["ffn_w_up"][layer]).T.astype(BF16),
                later(w["ffn_w_down"][layer]).astype(BF16))

    def gathered(own):
        return _landing((NDEV,) + own.shape, BF16, [((me,), own)])

    def gathered2(a, b):
        return _landing((2, NDEV) + a.shape, BF16, [((0, me), a), ((1, me), b)])

    ab_in_s = w["ab_w_in"][0].T.astype(BF16)
    gathers = {0: _gather_start("gather0_start", [ab_in_s], [gathered(ab_in_s)], [(shard(0), slot(0))])}

    def chan(flat, lo, taps):
        return flat[:, lo:lo + taps * SHARD_C].reshape(NDEV, taps, SHARD_C).transpose(1, 0, 2).reshape(taps, 512)

    def fetch(stage, after):
        if stage == "ab_in":
            ab_out_s = later(w["ab_w_out"][0]).astype(BF16)
            gate0, up0, down0 = layer_shards(0)
            small_s = _pack_rows([later(w[n]) for n in ("cd_norm_g", "conv_c_w", "conv_c_b", "c_ln_g", "c_ln_b",
                                                        "conv_d_w")], 24)
            lands1 = [gathered(ab_out_s), gathered2(gate0, up0), gathered(down0),
                      _landing((NDEV,) + small_s.shape, F32, [((me,), small_s)])]
            gathers[0] = _gather_forward("gather0_forward", gathers[0], [after] + lands1)
            l_ab_in, = _gather_wait("gather0_wait", gathers[0], gathers[0]["token"])
            gathers[1] = _gather_start(
                "gather1_start", [ab_out_s, gate0, up0, down0, small_s], lands1,
                [(shard(0), slot(0)), (shard(1), slot2(1, 0)), (shard(2), slot2(1, 1)), (shard(3), slot(2)),
                 (shard(4), slot(3))], dep=l_ab_in)
            return {"wt_ab_in": l_ab_in.reshape(AB_IN, D), "dep0": gathers[1]["token"]}
        if stage == "attn0":
            cd_in_s, cd_out_s = later(w["cd_w_in"][0]).T.astype(BF16), later(w["cd_w_out"][0]).astype(BF16)
            gate1, up1, down1 = layer_shards(1)
            gathers[2] = _gather_start(
                "gather2_start", [cd_in_s, cd_out_s, gate1, up1, down1],
                [gathered(cd_in_s), gathered(cd_out_s), gathered2(gate1, up1), gathered(down1)],
                [(shard(0), slot(0)), (shard(1), slot(1)), (shard(2), slot2(2, 0)), (shard(3), slot2(2, 1)),
                 (shard(4), slot(3))], dep=after)
            return {"dep_attn1": gathers[2]["token"]}
        if stage == "attn1":
            gathers[1] = _gather_forward("gather1_forward", gathers[1], after)
            return {"dep_attn2": gathers[1]["token"]}
        if stage == "ab_out":
            l_out, l_ffn, l_down, l_small = _gather_wait("gather1_wait", gathers[1], after)
            flat = l_small.reshape(NDEV, 24 * 128)
            return {
                "w_ab_out": l_out.reshape(D, D), "wt_ffn_in0": l_ffn.reshape(2 * DFF, D),
                "w_ffn_down0": l_down.reshape(DFF, D), "cd_norm_g": flat[:, 0:128].reshape(1, D),
                "conv_c_w32": jnp.pad(chan(flat, 128, CONV_C_TAPS), ((0, 1), (0, 0))),
                "conv_c_b": chan(flat, 2112, 1), "c_ln_g": chan(flat, 2176, 1), "c_ln_b": chan(flat, 2240, 1),
                "conv_d_w8": jnp.pad(chan(flat, 2304, CONV_D_TAPS), ((0, 8 - CONV_D_TAPS), (0, 0))),
            }
        if stage == "ffn_down0":
            gathers[2] = _gather_forward("gather2_forward", gathers[2], after)
            return {"dep_down0": gathers[2]["token"]}
        if stage == "cd_in":
            l_in, l_out, l_ffn, l_down = _gather_wait("gather2_wait", gathers[2], after)
            return {"wt_cd_in": l_in.reshape(CD_IN, D), "w_cd_out": l_out.reshape(D, D),
                    "wt_ffn_in1": l_ffn.reshape(2 * DFF, D), "w_ffn_down1": l_down.reshape(DFF, D)}
        return {}

    scatters = {}
    rides_with = {"w_ffn_down1": "wt_ffn_in1", "w_cd_out": "wt_cd_in", "w_ffn_down0": "wt_ffn_in0"}
    held = {}
    smalls = {}

    def small_exchange(name, block):
        land = _landing((NDEV,) + block.shape, F32, [((me,), block)])
        return _exchange_start(name, [block], [land], [(one(0), slot(0))])

    def on_grad(key, arr):
        if key == "small":
            parts = [arr["sgu_norm_g"], arr["sgu_norm_b"], arr["sgu_w"], arr["sgu_bias"], arr["q_norm_g"],
                     arr["k_norm_g"], arr["cd_norm_g"], arr["conv_c_w32"][:CONV_C_TAPS], arr["conv_c_b"], arr["c_ln_g"],
                     arr["c_ln_b"], arr["conv_d_w8"][:CONV_D_TAPS], arr["ffn_norm_g0"], arr["ffn_norm_g1"],
                     arr["loss_cols"]]
            smalls["sizes"] = [p.size for p in parts]
            rows = -(-sum(smalls["sizes"]) // 1024) * 8
            smalls["early"] = small_exchange("small_start", _pack_rows(parts, rows))
            return smalls["early"]["token"]
        if key in rides_with:
            held[rides_with[key]] = (key, arr)
            return None
        group = ([held.pop(key)] if key in held else []) + [(key, arr)]
        srcs, lands, items = [], [], []
        for n, (k, a) in enumerate(group):
            if k.startswith("wt_ffn_in"):
                src = a.reshape(2, NDEV, r_ff, D)
                own = lax.dynamic_slice_in_dim(src, me, 1, axis=1)
                lands.append(lax.dynamic_update_slice(lax.empty(src.shape, BF16), own, (0, me, 0, 0)))
                items += [((lambda S, j, n=n: S[n].at[0, j]), slot2(n, 0)), ((lambda S, j, n=n: S[n].at[1, j]), slot2(n, 1))]
            else:
                rows = a.shape[0] // NDEV
                src = a.reshape(NDEV, rows, D)
                own = lax.dynamic_index_in_dim(src, me, 0, keepdims=False)
                lands.append(_landing((1, NDEV, rows, D), BF16, [((0, me), own)]))
                items.append(((lambda S, j, n=n: S[n].at[j]), slot2(n, 0)))
            srcs.append(src)
        st = _exchange_start(f"scatter_{key}_start", srcs, lands, items)
        scatters[key] = (st, [k for k, _ in group])
        return st["token"]

    W = {
        "dep_first": gathers[0]["token"],
        "ab_norm_g": w["ab_norm_g"], "sgu_norm_g": w["sgu_norm_g"], "sgu_norm_b": w["sgu_norm_b"],
        "sgu_w": w["sgu_w"][0], "sgu_bias": w["sgu_bias"][0], "q_norm_g": w["q_norm_g"][0],
        "k_norm_g": w["k_norm_g"][0], "ffn_norm_g": w["ffn_norm_g"],
    }

    loss_cols, grad_x, G = _local_step(x[0], loss_target[0], W, fetch, on_grad)

    late_small = small_exchange("small_late_start", G["ab_norm_g"])
    landed = {}

    def wait_scatters(name, group_keys, others, after):
        res = _exchange_wait(name, [scatters[gk][0] for gk in group_keys] + others, after)
        for gk, lands in zip(group_keys, res):
            landed.update(zip(scatters[gk][1], lands))
        return [lands[0] for lands in res[len(group_keys):]]

    small_land, = wait_scatters("scatter_wait_early", ["wt_ffn_in1", "wt_cd_in", "wt_ffn_in0", "w_ab_out"],
                                [smalls["early"]], late_small["token"])

    grads, deltas, new_m, new_v = {}, {}, {}, {}
    done = []

    def put(name, res):
        grads[name], deltas[name], new_m[name], new_v[name] = res

    def adam(name, lands, sel, transposed):
        flip = (lambda a: jnp.swapaxes(a, 1, 2)) if transposed else (lambda a: a)
        res = _adam_stacked(lands, sel, flip(w[name]), flip(m[name]), flip(v[name]), f"adam_{name}")
        done.append(res[1])
        put(name, [flip(r) for r in res])

    ffn_in_lands = [landed["wt_ffn_in0"], landed["wt_ffn_in1"]]
    adam("cd_w_in", [landed["wt_cd_in"]], 0, True)
    adam("ffn_w_gate", ffn_in_lands, 0, True)
    adam("ffn_w_up", ffn_in_lands, 1, True)
    adam("cd_w_out", [landed["w_cd_out"]], 0, False)
    adam("ab_w_out", [landed["w_ab_out"]], 0, False)
    adam("ffn_w_down", [landed["w_ffn_down0"], landed["w_ffn_down1"]], 0, False)

    red = _sum_slots(small_land).reshape(-1)
    offs = [0]
    for s in smalls["sizes"]:
        offs.append(offs[-1] + s)
    seg = [red[offs[i]:offs[i + 1]] for i in range(len(smalls["sizes"]))]
    loss = jnp.sum(seg[14])

    def own_channels(full, taps):
        return lax.dynamic_slice_in_dim(full.reshape(taps, 512), me * SHARD_C, SHARD_C, axis=1)

    g_small = {
        "sgu_norm_g": seg[0].reshape(1, 512), "sgu_norm_b": seg[1].reshape(1, 512),
        "sgu_w": seg[2].reshape(512, 128), "sgu_bias": seg[3].reshape(4, 128), "q_norm_g": seg[4].reshape(3, 64),
        "k_norm_g": seg[5].reshape(3, 64),
        "cd_norm_g": lax.dynamic_slice_in_dim(seg[6].reshape(1, D), me * (D // NDEV), D // NDEV, axis=1),
        "conv_c_w": own_channels(seg[7], CONV_C_TAPS), "conv_c_b": own_channels(seg[8], 1),
        "c_ln_g": own_channels(seg[9], 1), "c_ln_b": own_channels(seg[10], 1),
        "conv_d_w": own_channels(seg[11], CONV_D_TAPS),
        "ffn_norm_g": jnp.concatenate([seg[12].reshape(1, D), seg[13].reshape(1, D)], axis=0),
    }

    def small_in(s, a):
        return jnp.swapaxes(a, 0, 1) if len(s) == 3 else a.reshape(s)

    def small_out(n, s, a):
        return jnp.swapaxes(a, 0, 1) if len(s) == 3 else a.reshape(w[n].shape)

    g_in = [g_small[n].reshape(s) for n, s in SMALL_SHAPES]
    d_s, m_s, v_s = _adam_small([small_in(s, w[n]) for n, s in SMALL_SHAPES], g_in,
                                [small_in(s, m[n]) for n, s in SMALL_SHAPES],
                                [small_in(s, v[n]) for n, s in SMALL_SHAPES])
    for i, (n, s) in enumerate(SMALL_SHAPES):
        grads[n], deltas[n] = small_out(n, s, g_in[i]), small_out(n, s, d_s[i])
        new_m[n], new_v[n] = small_out(n, s, m_s[i]), small_out(n, s, v_s[i])
    done.append(d_s[0])

    late_land, = wait_scatters("scatter_wait_last", ["wt_ab_in"], [late_small], list(done))
    put("ab_norm_g", _adam_of_slots(late_land, w["ab_norm_g"], m["ab_norm_g"], v["ab_norm_g"], "adam_ab_norm_g"))
    adam("ab_w_in", [landed["wt_ab_in"]], 0, True)

    return (loss, grad_x[None], *[grads[n] for n in WEIGHT_NAMES], *[deltas[n] for n in WEIGHT_NAMES],
            *[new_m[n] for n in WEIGHT_NAMES], *[new_v[n] for n in WEIGHT_NAMES])
```

```python
import jax
import jax.numpy as jnp
import numpy as np
from jax import lax
from jax.experimental import pallas as pl
from jax.experimental.pallas import tpu as pltpu

F32 = jnp.float32
BF16 = jnp.bfloat16

T = 4096
D = 1024
NDEV = 8
EPS = 1e-6
NEG_INF = -1e30
DFF = 2816
AB_IN = 5632
CD_IN = 2560
HEAD = 64
PAIR = 128
NPAIR = 4
NBACK = 128
DIL_RATES = (1, 4, 16)
ROPE_HALF = 8
ROPE_THETA = 500000.0
CONV_C_TAPS = 31
CONV_D_TAPS = 3
HALO = 32
ATTN_BWD_UNROLL = 4
MAX_ROW_STRIDE = 4

ADAM_LR = 0.001
ADAM_B1 = 0.9
ADAM_B2 = 0.999
ADAM_EPS = 1e-08
ADAM_WD = 0.01
ADAM_STEP = 10
ADAM_C1 = 1.0 / (1.0 - ADAM_B1 ** ADAM_STEP)
ADAM_C2 = 1.0 / (1.0 - ADAM_B2 ** ADAM_STEP)

VMEM_LIMIT_MB = 48
MESH = pl.DeviceIdType.MESH
HBM_SPEC = pl.BlockSpec(memory_space=pl.ANY)


def _cparams(ngrid, vmem_mb=VMEM_LIMIT_MB):
    return pltpu.CompilerParams(dimension_semantics=("arbitrary",) * ngrid,
                                vmem_limit_bytes=vmem_mb * 1024 * 1024)


def _pick(n, options):
    for o in options:
        if n % o == 0:
            return o
    raise ValueError(f"no tile for {n} in {options}")


def _sds(shape, dtype):
    return jax.ShapeDtypeStruct(shape, dtype)


def _sigmoid(x):
    return 1.0 / (1.0 + jnp.exp(-x))


def _sigmoid_bf16(x):
    return 0.5 * jnp.tanh(0.5 * x) + 0.5


def _gelu(z):
    return 0.5 * z * (1.0 + lax.erf(z * 0.7071067811865476))


def _gelu_grad(z):
    return 0.5 * (1.0 + lax.erf(z * 0.7071067811865476)) + z * jnp.exp(-0.5 * z * z) * 0.3989422804014327


def _mm_nt(a, wt, name, out_dtype=BF16, dep=None):
    M, K = a.shape
    N = wt.shape[0]
    tn = _pick(N, (512, 256))

    def body(a_ref, w_ref, *rest):
        o_ref = rest[-1]
        for r0 in range(0, M, 1024):
            o_ref[r0:r0 + 1024, :] = lax.dot_general(
                a_ref[r0:r0 + 1024, :], w_ref[...], (((1,), (1,)), ((), ())),
                preferred_element_type=F32).astype(o_ref.dtype)

    in_specs = [pl.BlockSpec((M, K), lambda j: (0, 0), pipeline_mode=pl.Buffered(1)),
                pl.BlockSpec((tn, K), lambda j: (j, 0))]
    args = [a, wt]
    if dep is not None:
        in_specs.append(HBM_SPEC)
        args.append(dep)
    return pl.pallas_call(
        body, name=name, grid=(N // tn,), in_specs=in_specs, out_specs=pl.BlockSpec((M, tn), lambda j: (0, j)),
        out_shape=_sds((M, N), out_dtype), compiler_params=_cparams(1))(*args)


EPI_ROWS = 256


def _mm_nt_rows(a, wt, name, epilogue, side, side_specs, out_specs, out_shape, sums=(), dep=None, tm=512):
    M, K = a.shape
    N = wt.shape[0]
    ns, no = len(side), len(out_shape)

    def body(a_ref, w_ref, *rest):
        side_refs, outs, acc = rest[0:ns], rest[-1 - no:-1], rest[-1]
        acc[...] = lax.dot_general(a_ref[...], w_ref[...], (((1,), (1,)), ((), ())), preferred_element_type=F32)

        @pl.when(pl.program_id(0) == 0)
        def _():
            for j in sums:
                outs[j][...] = jnp.zeros_like(outs[j])

        for r0 in range(0, tm, EPI_ROWS):
            rows = slice(r0, r0 + EPI_ROWS)
            epilogue(acc[rows, :].astype(BF16).astype(F32), rows, side_refs, outs)

    in_specs = [pl.BlockSpec((tm, K), lambda i: (i, 0)),
                pl.BlockSpec((N, K), lambda i: (0, 0), pipeline_mode=pl.Buffered(1))] + list(side_specs)
    args = [a, wt, *side]
    if dep is not None:
        in_specs.append(HBM_SPEC)
        args.append(dep)
    return pl.pallas_call(
        body, name=name, grid=(M // tm,), in_specs=in_specs, out_specs=list(out_specs), out_shape=list(out_shape),
        scratch_shapes=[pltpu.VMEM((tm, N), F32)], compiler_params=_cparams(1))(*args)


def _mm_nn(a, w, name, mode, resid, gain=None, tgt=None, dep=None, tm=512):
    M, K = a.shape
    N = w.shape[1]
    side = gain if mode == "rms" else tgt

    def body(a_ref, w_ref, resid_ref, side_ref, *rest):
        outs, acc = rest[-3 if mode == "rms" else -4:-1], rest[-1]
        i = pl.program_id(0)
        acc[...] = jnp.dot(a_ref[...], w_ref[...], preferred_element_type=F32)

        if mode == "loss":
            @pl.when(i == 0)
            def _():
                outs[2][...] = jnp.zeros_like(outs[2])

        for r0 in range(0, tm, EPI_ROWS):
            rows = slice(r0, r0 + EPI_ROWS)
            v = acc[rows, :] + resid_ref[rows, :]
            if mode == "rms":
                outs[0][rows, :] = v
                r = lax.rsqrt(jnp.mean(v * v, axis=-1, keepdims=True) + EPS)
                outs[1][rows, :] = (v * r * side_ref[...]).astype(BF16)
            else:
                d = v - side_ref[rows, :]
                outs[2][...] += jnp.sum(d * d, axis=0, keepdims=True) * (0.5 / N)
                dy = d * (1.0 / N)
                outs[0][rows, :] = dy
                outs[1][rows, :] = dy.astype(BF16)

    row = pl.BlockSpec((tm, N), lambda i: (i, 0))
    vec = pl.BlockSpec((1, N), lambda i: (0, 0))
    in_specs = [pl.BlockSpec((tm, K), lambda i: (i, 0)),
                pl.BlockSpec((K, N), lambda i: (0, 0), pipeline_mode=pl.Buffered(1)), row,
                vec if mode == "rms" else row]
    args = [a, w, resid, side]
    if dep is not None:
        in_specs.append(HBM_SPEC)
        args.append(dep)
    if mode == "rms":
        out_specs, out_shape = [row, row], [_sds((M, N), F32), _sds((M, N), BF16)]
    else:
        out_specs, out_shape = [row, row, vec], [_sds((M, N), F32), _sds((M, N), BF16), _sds((1, N), F32)]
    return pl.pallas_call(
        body, name=name, grid=(M // tm,), in_specs=in_specs, out_specs=out_specs, out_shape=out_shape,
        scratch_shapes=[pltpu.VMEM((tm, N), F32)], compiler_params=_cparams(1))(*args)


def _mm_dh_rms_bwd(a, w, x, gain, dres, name, dep=None, tm=512, bf16_copy=True):
    parts = a.shape[0] if a.ndim == 3 else 1
    M, Kp = a.shape[-2], a.shape[-1]
    N = w.shape[1]
    nblk = M // tm
    assert nblk % 2 == 0

    def body(a_ref, w_ref, x_ref, g_ref, dres_ref, *rest):
        dg_ref, acc0, acc1 = rest[-3:]
        dx_ref = rest[-5] if bf16_copy else rest[-4]
        dxb_ref = rest[-4] if bf16_copy else None
        i = pl.program_id(0)

        def matmul(acc):
            if parts == 1:
                acc[...] = jnp.dot(a_ref[...], w_ref[...], preferred_element_type=F32)
            else:
                d = jnp.dot(a_ref[0], w_ref[0:Kp, :], preferred_element_type=F32)
                for p in range(1, parts):
                    d = d + jnp.dot(a_ref[p], w_ref[p * Kp:(p + 1) * Kp, :], preferred_element_type=F32)
                acc[...] = d

        def finish(acc):
            for r0 in range(0, tm, EPI_ROWS // 2):
                rows = slice(r0, r0 + EPI_ROWS // 2)
                v = acc[rows, :]
                xf = x_ref[rows, :]
                r = lax.rsqrt(jnp.mean(xf * xf, axis=-1, keepdims=True) + EPS)
                xhat = xf * r
                dg_ref[...] += jnp.sum(v * xhat, axis=0, keepdims=True)
                dxh = v * g_ref[...]
                tot = dres_ref[rows, :] + r * (dxh - xhat * jnp.mean(dxh * xhat, axis=-1, keepdims=True))
                dx_ref[rows, :] = tot
                if bf16_copy:
                    dxb_ref[rows, :] = tot.astype(BF16)

        @pl.when(i == 0)
        def _():
            dg_ref[...] = jnp.zeros_like(dg_ref)
            matmul(acc0)

        @pl.when((i > 0) & (i < nblk) & (i % 2 == 1))
        def _():
            matmul(acc1)
            finish(acc0)

        @pl.when((i > 0) & (i < nblk) & (i % 2 == 0))
        def _():
            matmul(acc0)
            finish(acc1)

        @pl.when(i == nblk)
        def _():
            finish(acc1)

    last = nblk - 1
    row = pl.BlockSpec((tm, N), lambda i: (jnp.maximum(i - 1, 0), 0))
    vec = pl.BlockSpec((1, N), lambda i: (0, 0))
    if a.ndim == 3:
        a_spec = pl.BlockSpec((parts, tm, Kp), lambda i: (0, jnp.minimum(i, last), 0))
    else:
        a_spec = pl.BlockSpec((tm, Kp), lambda i: (jnp.minimum(i, last), 0))
    w_spec = pl.BlockSpec((parts * Kp, N), lambda i: (0, 0), pipeline_mode=pl.Buffered(1))
    in_specs = [a_spec, w_spec, row, vec, row]
    args = [a, w, x, gain, dres]
    if dep is not None:
        in_specs.append(HBM_SPEC)
        args.append(dep)
    return pl.pallas_call(
        body, name=name, grid=(nblk + 1,), in_specs=in_specs,
        out_specs=[row, row, vec] if bf16_copy else [row, vec],
        out_shape=([_sds((M, N), F32), _sds((M, N), BF16), _sds((1, N), F32)] if bf16_copy
                   else [_sds((M, N), F32), _sds((1, N), F32)]),
        scratch_shapes=[pltpu.VMEM((tm, N), F32), pltpu.VMEM((tm, N), F32)], compiler_params=_cparams(1, 56))(*args)


def _mm_tn(a, b, name, out_dtype=BF16, tt=2048, dep=None):
    parts = a.shape[0] if a.ndim == 3 else 1
    Tt, Mp = a.shape[-2], a.shape[-1]
    N = b.shape[1]
    tn = _pick(Mp, (1408, 1280, 1024, 512))
    jper = Mp // tn
    nt = Tt // tt

    def body(a_ref, b_ref, *rest):
        o_ref, acc = rest[-2:]
        t = pl.program_id(1)

        @pl.when(t == 0)
        def _():
            acc[...] = jnp.zeros_like(acc)

        rows = pl.ds(pl.multiple_of(t * tt, tt), tt)
        acc[...] += lax.dot_general(a_ref[...], b_ref[rows, :], (((0,), (0,)), ((), ())),
                                    preferred_element_type=F32)

        @pl.when(t == nt - 1)
        def _():
            o_ref[...] = acc[...].astype(o_ref.dtype)

    if a.ndim == 3:
        a_spec = pl.BlockSpec((None, tt, tn), lambda j, t: (j // jper, t, j % jper))
    else:
        a_spec = pl.BlockSpec((tt, tn), lambda j, t: (t, j))
    in_specs = [a_spec, pl.BlockSpec((Tt, N), lambda j, t: (0, 0), pipeline_mode=pl.Buffered(1))]
    args = [a, b]
    if dep is not None:
        in_specs.append(HBM_SPEC)
        args.append(dep)
    return pl.pallas_call(
        body, name=name, grid=(parts * jper, nt), in_specs=in_specs,
        out_specs=pl.BlockSpec((tn, N), lambda j, t: (j, 0)),
        out_shape=_sds((parts * Mp, N), out_dtype), scratch_shapes=[pltpu.VMEM((tn, N), F32)],
        compiler_params=_cparams(2))(*args)


FFN_ROWS = 256
PREFETCH_SLOTS = 3


def _ring_tile(window, buf, sem, j, nj):
    def fetch(step):
        slot = step % PREFETCH_SLOTS
        return pltpu.make_async_copy(window(step), buf.at[slot], sem.at[slot])

    @pl.when(j == 0)
    def _():
        for s in range(PREFETCH_SLOTS - 1):
            fetch(s).start()

    @pl.when(j + PREFETCH_SLOTS - 1 < nj)
    def _():
        fetch(j + PREFETCH_SLOTS - 1).start()

    fetch(j).wait()
    return buf.at[j % PREFETCH_SLOTS]


def _ring_scratch(tile_shape, dtype):
    return [pltpu.VMEM((PREFETCH_SLOTS,) + tuple(tile_shape), dtype), pltpu.SemaphoreType.DMA((PREFETCH_SLOTS,))]


def _ffn_in(h, wt_in, name, tn=256):
    nj = DFF // tn

    def body(h_ref, wg_ref, wu_ref, p_ref, act_ref):
        nt = (((1,), (1,)), ((), ()))
        for r0 in range(0, T, FFN_ROWS):
            rows = slice(r0, r0 + FFN_ROWS)
            g = lax.dot_general(h_ref[rows, :], wg_ref[...], nt, preferred_element_type=F32).astype(BF16)
            u = lax.dot_general(h_ref[rows, :], wu_ref[...], nt, preferred_element_type=F32).astype(BF16)
            p_ref[0, rows, :] = g
            p_ref[1, rows, :] = u
            act_ref[rows, :] = g * _sigmoid_bf16(g) * u

    return pl.pallas_call(
        body, name=name, grid=(nj,),
        in_specs=[pl.BlockSpec((T, D), lambda j: (0, 0), pipeline_mode=pl.Buffered(1)),
                  pl.BlockSpec((tn, D), lambda j: (j, 0)), pl.BlockSpec((tn, D), lambda j: (j + nj, 0))],
        out_specs=[pl.BlockSpec((2, T, tn), lambda j: (0, 0, j)), pl.BlockSpec((T, tn), lambda j: (0, j))],
        out_shape=[_sds((2, T, DFF), BF16), _sds((T, DFF), BF16)], compiler_params=_cparams(1))(h, wt_in, wt_in)


def _ffn_dact(dyb, w_down, p3, name, tn=256, dep=None):
    nj = DFF // tn

    def body(dy_ref, w_ref, p_hbm, *rest):
        o_ref, p_buf, sem = rest[-3:]
        p_ref = _ring_tile(lambda step: p_hbm.at[:, :, pl.ds(pl.multiple_of(step * tn, tn), tn)], p_buf, sem,
                           pl.program_id(0), nj)
        for r0 in range(0, T, FFN_ROWS):
            rows = slice(r0, r0 + FFN_ROWS)
            da = lax.dot_general(dy_ref[rows, :], w_ref[...], (((1,), (1,)), ((), ())),
                                 preferred_element_type=F32).astype(BF16)
            g = p_ref[0, rows, :]
            u = p_ref[1, rows, :]
            sg = _sigmoid_bf16(g)
            gs = g * sg
            o_ref[0, rows, :] = (da * u) * (sg + gs * (1.0 - sg))
            o_ref[1, rows, :] = da * gs

    in_specs = [pl.BlockSpec((T, D), lambda j: (0, 0), pipeline_mode=pl.Buffered(1)),
                pl.BlockSpec((tn, D), lambda j: (j, 0)), HBM_SPEC]
    args = [dyb, w_down, p3]
    if dep is not None:
        in_specs.append(HBM_SPEC)
        args.append(dep)
    return pl.pallas_call(
        body, name=name, grid=(nj,), in_specs=in_specs, out_specs=pl.BlockSpec((2, T, tn), lambda j: (0, 0, j)),
        out_shape=_sds((2, T, DFF), BF16), scratch_shapes=_ring_scratch((2, T, tn), BF16),
        compiler_params=_cparams(1))(*args)


def _rms_fwd(x, g, name, tm=512, dep=None):
    def body(x_ref, g_ref, *rest):
        h_ref = rest[-1]
        xf = x_ref[...]
        r = lax.rsqrt(jnp.mean(xf * xf, axis=-1, keepdims=True) + EPS)
        h_ref[...] = (xf * r * g_ref[...]).astype(BF16)

    in_specs = [pl.BlockSpec((tm, D), lambda i: (i, 0)), pl.BlockSpec((1, D), lambda i: (0, 0))]
    args = [x, g]
    if dep is not None:
        in_specs.append(HBM_SPEC)
        args.append(dep)
    return pl.pallas_call(
        body, name=name, grid=(T // tm,), in_specs=in_specs, out_specs=pl.BlockSpec((tm, D), lambda i: (i, 0)),
        out_shape=_sds((T, D), BF16), compiler_params=_cparams(1))(*args)


def _tril_mask():
    r = lax.broadcasted_iota(jnp.int32, (128, 128), 0)
    c = lax.broadcasted_iota(jnp.int32, (128, 128), 1)
    return r >= c


def _mix_a_fwd(pab, sgu_g, sgu_b, sgu_w, sgu_bias3, tm=512):
    def body(zu_ref, zv_ref, g_ref, b_ref, w_ref, bias_ref, o_ref):
        u = _gelu(zu_ref[...].astype(F32))
        v = _gelu(zv_ref[...].astype(F32))
        mu = jnp.mean(v, axis=-1, keepdims=True)
        vc = v - mu
        rstd = lax.rsqrt(jnp.mean(vc * vc, axis=-1, keepdims=True) + EPS)
        vn = (vc * rstd * g_ref[...] + b_ref[...]).astype(BF16)
        tri = _tril_mask()
        for gi in range(4):
            wg = jnp.where(tri, w_ref[gi], 0.0).astype(BF16)
            bg = bias_ref[gi]
            for c in range(tm // 128):
                rs, cs = slice(c * 128, (c + 1) * 128), slice(gi * 128, (gi + 1) * 128)
                mixed = jnp.dot(wg, vn[rs, cs], preferred_element_type=F32) + bg
                o_ref[rs, cs] = (u[rs, cs] * mixed).astype(BF16)

    half = pl.BlockSpec((tm, 512), lambda i: (i, 0))
    return pl.pallas_call(
        body, name="mix_a_fwd", grid=(T // tm,),
        in_specs=[half, pl.BlockSpec((tm, 512), lambda i: (i, 1)),
                  pl.BlockSpec((1, 512), lambda i: (0, 0)), pl.BlockSpec((1, 512), lambda i: (0, 0)),
                  pl.BlockSpec((4, 128, 128), lambda i: (0, 0, 0)), pl.BlockSpec((4, 128, 1), lambda i: (0, 0, 0))],
        out_specs=half, out_shape=_sds((T, D), BF16), compiler_params=_cparams(1),
    )(pab, pab, sgu_g, sgu_b, sgu_w, sgu_bias3)


def _rope_tables():
    pos = np.arange(T, dtype=np.float32)
    inv_freq = np.float32(ROPE_THETA) ** (-np.arange(ROPE_HALF, dtype=np.float32) * np.float32(2.0 / (2 * ROPE_HALF)))
    ang = (pos[:, None] * inv_freq[None, :]).astype(np.float32)
    cos, sin = np.cos(ang), np.sin(ang)
    z8 = np.zeros((T, ROPE_HALF), np.float32)
    rest = np.zeros((T, HEAD - 2 * ROPE_HALF), np.float32)
    c64 = np.concatenate([cos, cos, rest + 1.0], axis=1)
    s1 = np.concatenate([z8, sin, rest], axis=1)
    s2 = np.concatenate([-sin, z8, rest], axis=1)
    return tuple(jnp.asarray(np.tile(t, (1, 2)).astype(np.float32)) for t in (c64, s1, s2))


def _lo_mask(shape):
    return lax.broadcasted_iota(jnp.int32, shape, 1) < HEAD


def _seg_mean(x, lo):
    s_all = jnp.sum(x, axis=-1, keepdims=True)
    s_lo = jnp.sum(jnp.where(lo, x, 0.0), axis=-1, keepdims=True)
    return jnp.where(lo, s_lo, s_all - s_lo) * (1.0 / HEAD)


def _head_blocks():
    r = lax.broadcasted_iota(jnp.int32, (PAIR, PAIR), 0) < HEAD
    c = lax.broadcasted_iota(jnp.int32, (PAIR, PAIR), 1) < HEAD
    return jnp.where(r == c, 1.0, 0.0).astype(BF16)


def _seg_mean_mxu(x, blocks):
    return jnp.dot(x.astype(BF16), blocks, preferred_element_type=F32) * (1.0 / HEAD)


def _rope(n, c, s1, s2):
    return n * c + pltpu.roll(n, ROPE_HALF, 1) * s1 + pltpu.roll(n, PAIR - ROPE_HALF, 1) * s2


def _rope_t(dy, c, s1, s2):
    return dy * c - pltpu.roll(dy, PAIR - ROPE_HALF, 1) * s2 - pltpu.roll(dy, ROPE_HALF, 1) * s1


def _prep_fwd(pab, qg, kg, tabs, tm=512):
    def body(p_hbm, qg_ref, kg_ref, c_ref, s1_ref, s2_ref, *rest):
        outs, (p_buf, sem) = rest[:-2], rest[-2:]
        p_ref = _ring_tile(lambda step: p_hbm.at[pl.ds(pl.multiple_of(step * tm, tm), tm), :], p_buf, sem,
                           pl.program_id(0), T // tm)
        blocks = _head_blocks()
        c, s1, s2 = c_ref[...], s1_ref[...], s2_ref[...]
        for g in range(3):
            qn_ref, kn_ref, v_ref = outs[3 * g:3 * g + 3]
            for p in range(NPAIR):
                for which, gains, dst in ((0, qg_ref, qn_ref), (1, kg_ref, kn_ref)):
                    col = (2 + 3 * which + g) * 512 + p * PAIR
                    xr = p_ref[:, col:col + PAIR].astype(F32)
                    rinv = lax.rsqrt(_seg_mean_mxu(xr * xr, blocks) + EPS)
                    outs[9 + 2 * g + which][p] = rinv.astype(BF16)
                    dst[p] = _rope(xr * rinv * gains[g:g + 1, :], c, s1, s2)
                col = (8 + g) * 512 + p * PAIR
                v_ref[p] = p_ref[:, col:col + PAIR].astype(F32)

    pm = pl.BlockSpec((NPAIR, tm, PAIR), lambda i: (0, i, 0))
    tab = pl.BlockSpec((tm, PAIR), lambda i: (i, 0))
    gain = pl.BlockSpec((3, PAIR), lambda i: (0, 0))
    res = pl.pallas_call(
        body, name="prep_fwd", grid=(T // tm,),
        in_specs=[HBM_SPEC, gain, gain, tab, tab, tab],
        out_specs=[pm] * 15, out_shape=[_sds((NPAIR, T, PAIR), F32)] * 9 + [_sds((NPAIR, T, PAIR), BF16)] * 6,
        scratch_shapes=_ring_scratch((tm, AB_IN), BF16), compiler_params=_cparams(1, 56))(pab, qg, kg, *tabs)
    return res[0:9], res[9:15]


def _res_index(it, rate):
    window = NBACK * rate
    b = it // rate
    rho = it % rate
    start = b * window + rho
    startp = jnp.maximum(start - window, rho)
    kmin = jnp.where(b > 0, 0, NBACK)
    return start, startp, kmin


def _rows(start, rate):
    if rate == 1:
        return pl.ds(pl.multiple_of(start, NBACK), NBACK)
    return pl.ds(start, NBACK, stride=rate)


def _band_bias():
    qs = lax.broadcasted_iota(jnp.int32, (2 * NBACK, 2 * NBACK), 0)
    kj = lax.broadcasted_iota(jnp.int32, (2 * NBACK, 2 * NBACK), 1)
    dist = (qs & (NBACK - 1)) + NBACK - kj
    both = (dist >= 0) & (dist <= NBACK)
    return jnp.where(both, 0.0, NEG_INF), jnp.where(both & (kj >= NBACK), 0.0, NEG_INF)


def _attn_fwd_block(q, kcat, vcat, first, lo, biases):
    vcat1 = jnp.concatenate([vcat, jnp.ones((2 * NBACK, PAIR), BF16)], axis=1)
    q2 = jnp.concatenate([jnp.where(lo, q, 0.0), jnp.where(lo, 0.0, q)], axis=0).astype(BF16)
    s = lax.dot_general(q2, kcat, (((1,), (1,)), ((), ())), preferred_element_type=F32)
    s = s + jnp.where(first, biases[1], biases[0])
    m = jnp.max(s, axis=-1, keepdims=True)
    ol = jnp.dot(jnp.exp(s - m).astype(BF16), vcat1, preferred_element_type=F32)
    o2 = ol[:, 0:PAIR] / ol[:, PAIR:]
    ls = m + jnp.log(ol[:, PAIR:])
    return jnp.where(lo, o2[0:NBACK], o2[NBACK:]), jnp.where(lo, ls[0:NBACK], ls[NBACK:])


def _attn_fwd(qn, kn, v, rate, name, dep=None):
    if rate > MAX_ROW_STRIDE:
        return _attn_fwd_gathered(qn, kn, v, rate, name, dep)

    def body(q_ref, k_ref, v_ref, *rest):
        o_ref, l_ref = rest[-2:]
        lo = _lo_mask((NBACK, PAIR))
        biases = _band_bias()

        def step(it, carry):
            start, startp, kmin = _res_index(it, rate)
            q = q_ref[_rows(start, rate), :] * (HEAD ** -0.5)
            kcat = jnp.concatenate([k_ref[_rows(startp, rate), :], k_ref[_rows(start, rate), :]], axis=0).astype(BF16)
            vcat = jnp.concatenate([v_ref[_rows(startp, rate), :], v_ref[_rows(start, rate), :]], axis=0).astype(BF16)
            o, ls = _attn_fwd_block(q, kcat, vcat, kmin != 0, lo, biases)
            o_ref[_rows(start, rate), :] = o
            l_ref[_rows(start, rate), :] = ls
            return carry

        lax.fori_loop(0, T // NBACK, step, 0, unroll=4)

    pm = pl.BlockSpec((None, T, PAIR), lambda p: (p, 0, 0))
    in_specs, args = [pm, pm, pm], [qn, kn, v]
    if dep is not None:
        in_specs.append(HBM_SPEC)
        args.append(dep)
    return pl.pallas_call(
        body, name=name, grid=(NPAIR,), in_specs=in_specs, out_specs=[pm, pm],
        out_shape=[_sds((NPAIR, T, PAIR), F32)] * 2, compiler_params=_cparams(1))(*args)


def _attn_fwd_gathered(qn, kn, v, rate, name, dep):
    n = T // rate
    nblk = n // NBACK

    def body(q_hbm, k_hbm, v_hbm, *rest):
        o_hbm, l_hbm, qb, kb, vb, ob, lb, in_sem, out_sem = rest[-9:]
        p = pl.program_id(0)
        slot = p % 2

        def loads(pair, s):
            return [pltpu.make_async_copy(x.at[pair, :, r, :], buf.at[s, r], in_sem.at[3 * s + a])
                    for a, (x, buf) in enumerate(((q_hbm, qb), (k_hbm, kb), (v_hbm, vb))) for r in range(rate)]

        def stores(pair, s):
            return [pltpu.make_async_copy(buf.at[s, r], x.at[pair, :, r, :], out_sem.at[2 * s + a])
                    for a, (x, buf) in enumerate(((o_hbm, ob), (l_hbm, lb))) for r in range(rate)]

        @pl.when(p == 0)
        def _():
            for c in loads(0, 0):
                c.start()

        @pl.when(p + 1 < NPAIR)
        def _():
            for c in loads(p + 1, 1 - slot):
                c.start()

        for c in loads(p, slot):
            c.wait()

        @pl.when(p >= 2)
        def _():
            for c in stores(p - 2, slot):
                c.wait()

        lo = _lo_mask((NBACK, PAIR))
        biases = _band_bias()

        def step(it, carry):
            b, r = it % nblk, it // nblk
            cur = pl.ds(pl.multiple_of(b * NBACK, NBACK), NBACK)
            prev = pl.ds(pl.multiple_of(jnp.maximum(b - 1, 0) * NBACK, NBACK), NBACK)
            q = qb[slot, r, cur, :] * (HEAD ** -0.5)
            kcat = jnp.concatenate([kb[slot, r, prev, :], kb[slot, r, cur, :]], axis=0).astype(BF16)
            vcat = jnp.concatenate([vb[slot, r, prev, :], vb[slot, r, cur, :]], axis=0).astype(BF16)
            o, ls = _attn_fwd_block(q, kcat, vcat, b == 0, lo, biases)
            ob[slot, r, cur, :] = o
            lb[slot, r, cur, :] = ls
            return carry

        lax.fori_loop(0, T // NBACK, step, 0, unroll=4)

        for c in stores(p, slot):
            c.start()

        @pl.when(p == NPAIR - 1)
        def _():
            for c in stores(p - 1, 1 - slot) + stores(p, slot):
                c.wait()

    by_residue = lambda a: a.reshape(NPAIR, n, rate, PAIR)
    in_specs, args = [HBM_SPEC] * 3, [by_residue(qn), by_residue(kn), by_residue(v)]
    if dep is not None:
        in_specs.append(HBM_SPEC)
        args.append(dep)
    o, l = pl.pallas_call(
        body, name=name, grid=(NPAIR,), in_specs=in_specs, out_specs=[HBM_SPEC] * 2,
        out_shape=[_sds((NPAIR, n, rate, PAIR), F32)] * 2,
        scratch_shapes=[pltpu.VMEM((2, rate, n, PAIR), F32)] * 5
        + [pltpu.SemaphoreType.DMA((6,)), pltpu.SemaphoreType.DMA((4,))],
        compiler_params=_cparams(1))(*args)
    return o.reshape(NPAIR, T, PAIR), l.reshape(NPAIR, T, PAIR)


def _merge_fwd(cat_ab, outs, lses, tm=512):
    def body(cat_in, o0, o1, o2, l0, l1, l2, cat_ref, lse_ref):
        del cat_in
        for p in range(NPAIR):
            a0, a1, a2 = l0[p], l1[p], l2[p]
            m = jnp.maximum(jnp.maximum(a0, a1), a2)
            w0, w1, w2 = jnp.exp(a0 - m), jnp.exp(a1 - m), jnp.exp(a2 - m)
            s = w0 + w1 + w2
            b = (w0 * o0[p] + w1 * o1[p] + w2 * o2[p]) / s
            cat_ref[:, p * PAIR:(p + 1) * PAIR] = b.astype(BF16)
            lse_ref[p] = m + jnp.log(s)

    pm = pl.BlockSpec((NPAIR, tm, PAIR), lambda i: (0, i, 0))
    return pl.pallas_call(
        body, name="merge_fwd", grid=(T // tm,),
        in_specs=[pl.BlockSpec(memory_space=pl.ANY)] + [pm] * 6,
        out_specs=[pl.BlockSpec((tm, 512), lambda i: (i, 1)), pm],
        out_shape=[_sds((T, D), BF16), _sds((NPAIR, T, PAIR), F32)],
        input_output_aliases={0: 0}, compiler_params=_cparams(1))(cat_ab, *outs, *lses)


def _d_cat_ab(dxb, w_ab_out, cat, dep, tm=512):
    def epilogue(d, rows, side, outs):
        (b_ref,), (da_ref, dbp_ref, e_ref) = side, outs
        da_ref[rows, :] = d[:, 0:512].astype(BF16)
        lo = _lo_mask((EPI_ROWS, PAIR))
        for p in range(NPAIR):
            db = d[:, 512 + p * PAIR:512 + (p + 1) * PAIR]
            b = b_ref[rows, p * PAIR:(p + 1) * PAIR].astype(F32)
            dbp_ref[p, rows, :] = db
            e_ref[p, rows, :] = _seg_mean(db * b, lo) * float(HEAD)

    pm = pl.BlockSpec((NPAIR, tm, PAIR), lambda i: (0, i, 0))
    return _mm_nt_rows(
        dxb, w_ab_out, "mm_d_cat_ab", epilogue, [cat], [pl.BlockSpec((tm, 512), lambda i: (i, 1))],
        [pl.BlockSpec((tm, 512), lambda i: (i, 0)), pm, pm],
        [_sds((T, 512), BF16), _sds((NPAIR, T, PAIR), F32), _sds((NPAIR, T, PAIR), F32)], dep=dep, tm=tm)


def _attn_bwd_block(q, db, ev, ls, kcat, vcat, first, lo, biases):
    scale = HEAD ** -0.5
    nt = (((1,), (1,)), ((), ()))
    tn = (((0,), (0,)), ((), ()))
    q = q * scale
    q2 = jnp.concatenate([jnp.where(lo, q, 0.0), jnp.where(lo, 0.0, q)], axis=0).astype(BF16)
    db2 = jnp.concatenate([jnp.where(lo, db, 0.0), jnp.where(lo, 0.0, db)], axis=0).astype(BF16)
    ls2 = jnp.concatenate([ls[:, 0:1], ls[:, HEAD:HEAD + 1]], axis=0)
    ev2 = jnp.concatenate([ev[:, 0:1], ev[:, HEAD:HEAD + 1]], axis=0)
    s = lax.dot_general(q2, kcat, nt, preferred_element_type=F32)
    pt = jnp.exp(s + jnp.where(first, biases[1], biases[0]) - ls2)
    dp = lax.dot_general(db2, vcat, nt, preferred_element_type=F32)
    ds = (pt * (dp - ev2)).astype(BF16)
    dq2 = jnp.dot(ds, kcat, preferred_element_type=F32) * scale
    dkc = lax.dot_general(ds, q2, tn, preferred_element_type=F32)
    dvc = lax.dot_general(pt.astype(BF16), db2, tn, preferred_element_type=F32)
    return jnp.where(lo, dq2[0:NBACK], dq2[NBACK:]), dkc, dvc


def _attn_bwd_loop(read, write, nblk):
    lo = _lo_mask((NBACK, PAIR))
    biases = _band_bias()

    def one(it, carry):
        dk_carry, dv_carry = carry
        rho = it // nblk
        b = it % nblk
        bp = jnp.maximum(b - 1, 0)
        kcat = jnp.concatenate([read(1, rho, bp), read(1, rho, b)], axis=0).astype(BF16)
        vcat = jnp.concatenate([read(2, rho, bp), read(2, rho, b)], axis=0).astype(BF16)
        dq, dkc, dvc = _attn_bwd_block(read(0, rho, b), read(3, rho, b), read(4, rho, b), read(5, rho, b), kcat, vcat,
                                       b == 0, lo, biases)
        write(0, rho, b, dq)
        write(1, rho, bp, dk_carry + dkc[0:NBACK])
        write(1, rho, b, dkc[NBACK:])
        write(2, rho, bp, dv_carry + dvc[0:NBACK])
        write(2, rho, b, dvc[NBACK:])
        return dkc[NBACK:], dvc[NBACK:]

    def step(i, carry):
        for u in range(ATTN_BWD_UNROLL):
            carry = one(i * ATTN_BWD_UNROLL + u, carry)
        return carry

    zero = jnp.zeros((NBACK, PAIR), F32)
    lax.fori_loop(0, T // NBACK // ATTN_BWD_UNROLL, step, (zero, zero))


def _attn_bwd(qn, kn, v, dbp, e, lse, rate, name):
    if rate > MAX_ROW_STRIDE:
        return _attn_bwd_gathered(qn, kn, v, dbp, e, lse, rate, name)
    window = NBACK * rate

    def body(*refs):
        rows = lambda rho, b: _rows(b * window + rho, rate)

        def write(j, rho, b, value):
            refs[6 + j][rows(rho, b), :] = value

        _attn_bwd_loop(lambda j, rho, b: refs[j][rows(rho, b), :], write, T // window)

    pm = pl.BlockSpec((None, T, PAIR), lambda p: (p, 0, 0))
    return pl.pallas_call(
        body, name=name, grid=(NPAIR,), in_specs=[pm] * 6, out_specs=[pm] * 3,
        out_shape=[_sds((NPAIR, T, PAIR), F32)] * 3, compiler_params=_cparams(1, 56))(qn, kn, v, dbp, e, lse)


def _attn_bwd_gathered(qn, kn, v, dbp, e, lse, rate, name):
    n = T // rate

    def body(*refs):
        ins, outs, in_bufs, out_bufs, (in_sem, out_sem) = refs[0:6], refs[6:9], refs[9:15], refs[15:18], refs[18:20]
        p = pl.program_id(0)
        slot = p % 2

        def loads(pair, s):
            return [pltpu.make_async_copy(x.at[pair, :, r, :], buf.at[s, r], in_sem.at[6 * s + a])
                    for a, (x, buf) in enumerate(zip(ins, in_bufs)) for r in range(rate)]

        def stores(pair, s):
            return [pltpu.make_async_copy(buf.at[s, r], x.at[pair, :, r, :], out_sem.at[3 * s + a])
                    for a, (x, buf) in enumerate(zip(outs, out_bufs)) for r in range(rate)]

        @pl.when(p == 0)
        def _():
            for c in loads(0, 0):
                c.start()

        @pl.when(p + 1 < NPAIR)
        def _():
            for c in loads(p + 1, 1 - slot):
                c.start()

        for c in loads(p, slot):
            c.wait()

        @pl.when(p >= 2)
        def _():
            for c in stores(p - 2, slot):
                c.wait()

        rows = lambda b: pl.ds(pl.multiple_of(b * NBACK, NBACK), NBACK)

        def write(j, rho, b, value):
            out_bufs[j][slot, rho, rows(b), :] = value

        _attn_bwd_loop(lambda j, rho, b: in_bufs[j][slot, rho, rows(b), :], write, n // NBACK)

        for c in stores(p, slot):
            c.start()

        @pl.when(p == NPAIR - 1)
        def _():
            for c in stores(p - 1, 1 - slot) + stores(p, slot):
                c.wait()

    by_residue = lambda a: a.reshape(NPAIR, n, rate, PAIR)
    res = pl.pallas_call(
        body, name=name, grid=(NPAIR,), in_specs=[HBM_SPEC] * 6, out_specs=[HBM_SPEC] * 3,
        out_shape=[_sds((NPAIR, n, rate, PAIR), F32)] * 3,
        scratch_shapes=[pltpu.VMEM((2, rate, n, PAIR), F32)] * 9
        + [pltpu.SemaphoreType.DMA((12,)), pltpu.SemaphoreType.DMA((6,))],
        compiler_params=_cparams(1, 56))(*[by_residue(a) for a in (qn, kn, v, dbp, e, lse)])
    return [r.reshape(NPAIR, T, PAIR) for r in res]


def _ab_in_bwd(pab, dcat, sgu_g, sgu_b, sgu_w, sgu_bias3, qg, kg, tabs, dqkv, rinvs, tm=256):
    def body(p_hbm, dcat_ref, g_ref, b_ref, w_ref, bias_ref, qg_ref, kg_ref, c_ref, s1_ref, s2_ref, *rest):
        dq_refs, rinv_refs = rest[0:9], rest[9:15]
        o_ref, dwm_ref, dbias_ref, dsg_ref, dsb_ref, dgain_ref = rest[15:21]
        p_buf, sem = rest[21:]
        i = pl.program_id(0)
        p_ref = _ring_tile(lambda step: p_hbm.at[pl.ds(pl.multiple_of(step * tm, tm), tm), :], p_buf, sem, i, T // tm)

        @pl.when(i == 0)
        def _():
            dwm_ref[...] = jnp.zeros_like(dwm_ref)
            dbias_ref[...] = jnp.zeros_like(dbias_ref)
            dsg_ref[...] = jnp.zeros_like(dsg_ref)
            dsb_ref[...] = jnp.zeros_like(dsb_ref)
            dgain_ref[...] = jnp.zeros_like(dgain_ref)

        zu = p_ref[:, 0:512].astype(F32)
        zv = p_ref[:, 512:1024].astype(F32)
        u = _gelu(zu)
        v = _gelu(zv)
        mu = jnp.mean(v, axis=-1, keepdims=True)
        vc = v - mu
        rstd = lax.rsqrt(jnp.mean(vc * vc, axis=-1, keepdims=True) + EPS)
        xhat = vc * rstd
        vn = (xhat * g_ref[...] + b_ref[...]).astype(BF16)
        da = dcat_ref[...].astype(F32)
        tri = _tril_mask()
        du_parts = [[None] * 4 for _ in range(tm // 128)]
        dvn_parts = [[None] * 4 for _ in range(tm // 128)]
        for gi in range(4):
            wg = jnp.where(tri, w_ref[gi], 0.0).astype(BF16)
            bg = bias_ref[gi]
            for c in range(tm // 128):
                rs, cs = slice(c * 128, (c + 1) * 128), slice(gi * 128, (gi + 1) * 128)
                vblk = vn[rs, cs]
                mixed = jnp.dot(wg, vblk, preferred_element_type=F32) + bg
                dab = da[rs, cs]
                du_parts[c][gi] = dab * mixed
                dmixed = dab * u[rs, cs]
                dmb = dmixed.astype(BF16)
                dvn_parts[c][gi] = lax.dot_general(wg, dmb, (((0,), (0,)), ((), ())), preferred_element_type=F32)
                dwm = lax.dot_general(dmb, vblk, (((1,), (1,)), ((), ())), preferred_element_type=F32)
                dwm_ref[gi] += jnp.where(tri, dwm, 0.0)
                dbias_ref[gi] += dmixed
        du = jnp.concatenate([jnp.concatenate(r, axis=1) for r in du_parts], axis=0)
        dvn = jnp.concatenate([jnp.concatenate(r, axis=1) for r in dvn_parts], axis=0)
        dsg_ref[...] += jnp.sum(dvn * xhat, axis=0, keepdims=True)
        dsb_ref[...] += jnp.sum(dvn, axis=0, keepdims=True)
        dxh = dvn * g_ref[...]
        dv = rstd * (dxh - jnp.mean(dxh, axis=-1, keepdims=True)
                     - xhat * jnp.mean(dxh * xhat, axis=-1, keepdims=True))
        o_ref[:, 0:512] = (du * _gelu_grad(zu)).astype(BF16)
        o_ref[:, 512:1024] = (dv * _gelu_grad(zv)).astype(BF16)

        blocks = _head_blocks()
        c, s1, s2 = c_ref[...], s1_ref[...], s2_ref[...]
        for g in range(3):
            dq_ref, dk_ref, dv_ref = dq_refs[3 * g:3 * g + 3]
            for p in range(NPAIR):
                for which, gains, src in ((0, qg_ref, dq_ref), (1, kg_ref, dk_ref)):
                    col = (2 + 3 * which + g) * 512 + p * PAIR
                    xr = p_ref[:, col:col + PAIR].astype(F32)
                    rinv = rinv_refs[2 * g + which][p].astype(F32)
                    xh = xr * rinv
                    dn = _rope_t(src[p], c, s1, s2)
                    row = 2 * g + which
                    dgain_ref[row:row + 1, :] += jnp.sum(dn * xh, axis=0, keepdims=True)
                    dxh2 = dn * gains[g:g + 1, :]
                    dx = rinv * (dxh2 - xh * _seg_mean_mxu(dxh2 * xh, blocks))
                    o_ref[:, col:col + PAIR] = dx.astype(BF16)
                col = (8 + g) * 512 + p * PAIR
                o_ref[:, col:col + PAIR] = dv_ref[p].astype(BF16)

    pm = pl.BlockSpec((NPAIR, tm, PAIR), lambda i: (0, i, 0))
    tab = pl.BlockSpec((tm, PAIR), lambda i: (i, 0))
    gain = pl.BlockSpec((3, PAIR), lambda i: (0, 0))
    vec = pl.BlockSpec((1, 512), lambda i: (0, 0))
    full = pl.BlockSpec((tm, AB_IN), lambda i: (i, 0))
    w4 = pl.BlockSpec((4, 128, 128), lambda i: (0, 0, 0))
    return pl.pallas_call(
        body, name="ab_in_bwd", grid=(T // tm,),
        in_specs=[HBM_SPEC, pl.BlockSpec((tm, 512), lambda i: (i, 0)), vec, vec, w4,
                  pl.BlockSpec((4, 128, 1), lambda i: (0, 0, 0)), gain, gain, tab, tab, tab] + [pm] * 15,
        out_specs=[full, w4, w4, vec, vec, pl.BlockSpec((8, PAIR), lambda i: (0, 0))],
        out_shape=[_sds((T, AB_IN), BF16), _sds((4, 128, 128), F32), _sds((4, 128, 128), F32),
                   _sds((1, 512), F32), _sds((1, 512), F32), _sds((8, PAIR), F32)],
        scratch_shapes=_ring_scratch((tm, AB_IN), BF16), compiler_params=_cparams(1))(pab, dcat, sgu_g, sgu_b, sgu_w, sgu_bias3, qg, kg, *tabs, *dqkv, *rinvs)


def _ln_stats(x):
    mu = jnp.mean(x, axis=-1, keepdims=True)
    xc = x - mu
    rstd = lax.rsqrt(jnp.mean(xc * xc, axis=-1, keepdims=True) + EPS)
    return xc * rstd, rstd


CONV_RC = 64


def _shifted_copies(src, dst, tm):
    dst[0] = src[...]
    for b in range(1, 8):
        dst[b, 0:tm + HALO - 8, :] = src[pl.ds(b, tm + HALO - 8), :]


def _offsets_by_phase(first):
    groups = {}
    for o in range(first, first + CONV_C_TAPS):
        groups.setdefault(o % 8, []).append(o)
    return sorted(groups.items())


def _window(shifted, b8, base, offsets, lanes):
    rows = 8 * (max(offsets) // 8) + CONV_RC
    return shifted[b8, pl.ds(base, rows), lanes].reshape(rows // 8, 8, 128)


def _cd_fwd(pcd, cw, cb, lg, lb, dw, tm=512):
    per = tm // HALO

    def body(p_ref, h_ref, cw_ref, cb_ref, lg_ref, lb_ref, dw_ref, cat_ref, c0_ref, c1_ref, dd_ref, y_ref,
             buf, buf2, sb):
        i = pl.program_id(0)
        live = jnp.where(i > 0, 1.0, 0.0)
        a = p_ref[:, 0:512].astype(F32)
        gt = p_ref[:, 512:1024].astype(F32)
        gb = p_ref[:, 1024:1536].astype(F32)
        gc = p_ref[:, 1536:2048].astype(F32)
        hv = p_ref[:, 2048:2560].astype(F32)
        c0 = a * _sigmoid(gt)
        dd = gc * hv
        buf[0:HALO, :] = h_ref[:, 0:512].astype(F32) * _sigmoid(h_ref[:, 512:1024].astype(F32)) * live
        buf[HALO:, :] = c0
        buf2[0:HALO, :] = h_ref[:, 1536:2048].astype(F32) * h_ref[:, 2048:2560].astype(F32) * live
        buf2[HALO:, :] = dd
        c0_ref[...] = c0.astype(BF16)
        dd_ref[...] = dd.astype(BF16)
        _shifted_copies(buf, sb, tm)

        def conv_rows(r, carry):
            base = pl.multiple_of(r * CONV_RC, CONV_RC)
            for c in range(4):
                lanes = slice(c * 128, (c + 1) * 128)
                acc = jnp.broadcast_to(cb_ref[:, lanes], (CONV_RC // 8, 8, 128))
                for b8, offsets in _offsets_by_phase(HALO - (CONV_C_TAPS - 1)):
                    win = _window(sb, b8, base, offsets, lanes)
                    for o in offsets:
                        j = o - (HALO - (CONV_C_TAPS - 1))
                        acc = acc + cw_ref[8 * j:8 * j + 8, lanes] * win[o // 8:o // 8 + CONV_RC // 8]
                c1_ref[pl.ds(base, CONV_RC), lanes] = acc.reshape(CONV_RC, 128)
            return carry

        lax.fori_loop(0, tm // CONV_RC, conv_rows, 0)
        xhat, _ = _ln_stats(c1_ref[...])
        c2 = xhat * lg_ref[...] + lb_ref[...]
        y = jnp.zeros((tm, 512), F32)
        for j in range(CONV_D_TAPS):
            y = y + dw_ref[j:j + 1, :] * buf2[pl.ds(HALO - (CONV_D_TAPS - 1) + j, tm), :]
        cat_ref[:, 0:512] = (c2 * _sigmoid(c2)).astype(BF16)
        cat_ref[:, 512:1024] = (gb * y).astype(BF16)
        y_ref[...] = y.astype(BF16)

    half = pl.BlockSpec((tm, 512), lambda i: (i, 0))
    vec = pl.BlockSpec((1, 512), lambda i: (0, 0))
    return pl.pallas_call(
        body, name="cd_fwd", grid=(T // tm,),
        in_specs=[pl.BlockSpec((tm, CD_IN), lambda i: (i, 0)),
                  pl.BlockSpec((HALO, CD_IN), lambda i: (jnp.maximum(i * per - 1, 0), 0)),
                  pl.BlockSpec((8 * 32, 512), lambda i: (0, 0)), vec, vec, vec, pl.BlockSpec((8, 512), lambda i: (0, 0))],
        out_specs=[pl.BlockSpec((tm, D), lambda i: (i, 0)), half, half, half, half],
        out_shape=[_sds((T, D), BF16), _sds((T, 512), BF16), _sds((T, 512), F32), _sds((T, 512), BF16),
                   _sds((T, 512), BF16)],
        scratch_shapes=[pltpu.VMEM((HALO + tm, 512), F32), pltpu.VMEM((HALO + tm, 512), F32),
                        pltpu.VMEM((8, HALO + tm, 512), F32)],
        compiler_params=_cparams(1))(pcd, pcd, cw, cb, lg, lb, dw)


def _d_cat_cd(dxb, w_cd_out, c1, pcd, y, lg, lb, dep, tm=512):
    def epilogue(d, rows, side, outs):
        c1_ref, gb_ref, y_ref, lg_ref, lb_ref = side
        dc1_ref, dy3_ref, dgb_ref, dlg_ref, dlb_ref, dcb_ref = outs
        dc, ddo = d[:, 0:512], d[:, 512:1024]
        xhat, rstd = _ln_stats(c1_ref[rows, :])
        c2 = xhat * lg_ref[...] + lb_ref[...]
        sg = _sigmoid(c2)
        dc2 = dc * sg * (1.0 + c2 * (1.0 - sg))
        dlg_ref[...] += jnp.sum(dc2 * xhat, axis=0, keepdims=True)
        dlb_ref[...] += jnp.sum(dc2, axis=0, keepdims=True)
        dxh = dc2 * lg_ref[...]
        dc1 = rstd * (dxh - jnp.mean(dxh, axis=-1, keepdims=True)
                      - xhat * jnp.mean(dxh * xhat, axis=-1, keepdims=True))
        dcb_ref[...] += jnp.sum(dc1, axis=0, keepdims=True)
        dc1_ref[rows, :] = dc1
        dgb_ref[rows, :] = (ddo * y_ref[rows, :].astype(F32)).astype(BF16)
        dy3_ref[rows, :] = ddo * gb_ref[rows, :].astype(F32)

    half = pl.BlockSpec((tm, 512), lambda i: (i, 0))
    vec = pl.BlockSpec((1, 512), lambda i: (0, 0))
    return _mm_nt_rows(
        dxb, w_cd_out, "mm_d_cat_cd", epilogue, [c1, pcd, y, lg, lb],
        [half, pl.BlockSpec((tm, 512), lambda i: (i, 2)), half, vec, vec], [half, half, half, vec, vec, vec],
        [_sds((T, 512), F32), _sds((T, 512), F32), _sds((T, 512), BF16),
         _sds((1, 512), F32), _sds((1, 512), F32), _sds((1, 512), F32)], sums=(3, 4, 5), dep=dep, tm=tm)


def _cd_bwd_conv(pcd, dc1, dy3, c0, dd, dgb, cw8, dw, tm=256):
    per = tm // HALO
    nblk = T // tm
    last32 = T // HALO - 1

    def body(p_ref, dc1_ref, dc1n_ref, dy3_ref, dy3n_ref, c0_ref, dd_ref, dgb_ref, cw_ref, dw_ref,
             o_ref, dcw_ref, ddw_ref, dbuf, d3buf, sd, dc0_buf):
        i = pl.program_id(0)
        has_next = jnp.where(i < nblk - 1, 1.0, 0.0)

        @pl.when(i == 0)
        def _():
            dcw_ref[...] = jnp.zeros_like(dcw_ref)
            ddw_ref[...] = jnp.zeros_like(ddw_ref)

        dbuf[0:tm, :] = dc1_ref[...]
        dbuf[tm:, :] = dc1n_ref[...] * has_next
        d3buf[0:tm, :] = dy3_ref[...]
        d3buf[tm:, :] = dy3n_ref[...] * has_next
        _shifted_copies(dbuf, sd, tm)
        n_tiles = tm // CONV_RC

        phases = _offsets_by_phase(0)

        def dc0_rows(r, carry):
            base = pl.multiple_of(r * CONV_RC, CONV_RC)
            for c in range(4):
                lanes = slice(c * 128, (c + 1) * 128)
                acc = jnp.zeros((CONV_RC // 8, 8, 128), F32)
                for b8, offsets in phases:
                    win = _window(sd, b8, base, offsets, lanes)
                    for o in offsets:
                        j = CONV_C_TAPS - 1 - o
                        acc = acc + cw_ref[8 * j:8 * j + 8, lanes] * win[o // 8:o // 8 + CONV_RC // 8]
                dc0_buf[pl.ds(base, CONV_RC), lanes] = acc.reshape(CONV_RC, 128)
            return carry

        lax.fori_loop(0, n_tiles, dc0_rows, 0)

        for c in range(4):
            lanes = slice(c * 128, (c + 1) * 128)
            for b8, offsets in phases:
                def dw_rows(r, accs, lanes=lanes, b8=b8, offsets=offsets):
                    base = pl.multiple_of(r * CONV_RC, CONV_RC)
                    xin = c0_ref[pl.ds(base, CONV_RC), lanes].astype(F32).reshape(CONV_RC // 8, 8, 128)
                    win = _window(sd, b8, base, offsets, lanes)
                    return tuple(acc + jnp.sum(xin * win[o // 8:o // 8 + CONV_RC // 8], axis=0)
                                 for acc, o in zip(accs, offsets))

                accs = lax.fori_loop(0, n_tiles, dw_rows, tuple(jnp.zeros((8, 128), F32) for _ in offsets))
                for acc, o in zip(accs, offsets):
                    j = CONV_C_TAPS - 1 - o
                    dcw_ref[j:j + 1, lanes] += jnp.sum(acc, axis=0, keepdims=True)

        dc0 = dc0_buf[...]
        ddin = dd_ref[...].astype(F32)
        ddd = jnp.zeros((tm, 512), F32)
        for j in range(CONV_D_TAPS):
            dy_shift = d3buf[pl.ds(CONV_D_TAPS - 1 - j, tm), :]
            ddd = ddd + dw_ref[j:j + 1, :] * dy_shift
            ddw_ref[j:j + 1, :] += jnp.sum(ddin * dy_shift, axis=0, keepdims=True)

        a = p_ref[:, 0:512].astype(F32)
        gt = p_ref[:, 512:1024].astype(F32)
        gc = p_ref[:, 1536:2048].astype(F32)
        hv = p_ref[:, 2048:2560].astype(F32)
        sg = _sigmoid(gt)
        o_ref[:, 0:512] = (dc0 * sg).astype(BF16)
        o_ref[:, 512:1024] = (dc0 * a * sg * (1.0 - sg)).astype(BF16)
        o_ref[:, 1024:1536] = dgb_ref[...]
        o_ref[:, 1536:2048] = (ddd * hv).astype(BF16)
        o_ref[:, 2048:2560] = (ddd * gc).astype(BF16)

    half = pl.BlockSpec((tm, 512), lambda i: (i, 0))
    nxt = pl.BlockSpec((HALO, 512), lambda i: (jnp.minimum((i + 1) * per, last32), 0))
    full = pl.BlockSpec((tm, CD_IN), lambda i: (i, 0))
    return pl.pallas_call(
        body, name="cd_bwd_conv", grid=(nblk,),
        in_specs=[full, half, nxt, half, nxt, half, half, half,
                  pl.BlockSpec((8 * 32, 512), lambda i: (0, 0)), pl.BlockSpec((8, 512), lambda i: (0, 0))],
        out_specs=[full, pl.BlockSpec((32, 512), lambda i: (0, 0)), pl.BlockSpec((8, 512), lambda i: (0, 0))],
        out_shape=[_sds((T, CD_IN), BF16), _sds((32, 512), F32), _sds((8, 512), F32)],
        scratch_shapes=[pltpu.VMEM((tm + HALO, 512), F32), pltpu.VMEM((tm + HALO, 512), F32),
                        pltpu.VMEM((8, tm + HALO, 512), F32), pltpu.VMEM((tm, 512), F32)],
        compiler_params=_cparams(1))(pcd, dc1, dc1, dy3, dy3, c0, dd, dgb, cw8, dw)


def _local_step(x, tgt, W, fetch=None, on_grad=None):
    W = dict(W)
    if fetch is None:
        fetch = lambda stage, after: {}
    if on_grad is None:
        on_grad = lambda key, arr: None
    tabs = _rope_tables()
    qg = jnp.tile(W["q_norm_g"], (1, 2))
    kg = jnp.tile(W["k_norm_g"], (1, 2))
    bias3 = W["sgu_bias"].reshape(4, 128, 1)
    G = {}

    h0 = _rms_fwd(x, W["ab_norm_g"], "rms_fwd_ab", dep=W.get("dep_first"))
    W.update(fetch("ab_in", h0))
    pab = _mm_nt(h0, W["wt_ab_in"], "mm_ab_in", dep=W.get("dep0"))
    cat_ab = _mix_a_fwd(pab, W["sgu_norm_g"], W["sgu_norm_b"], W["sgu_w"], bias3)
    qkv, rinvs = _prep_fwd(pab, qg, kg, tabs)
    outs, lses = [], []
    for g, rate in enumerate(DIL_RATES):
        o, l = _attn_fwd(qkv[3 * g], qkv[3 * g + 1], qkv[3 * g + 2], rate, f"attn_fwd_{g}", dep=W.get(f"dep_attn{g}"))
        outs.append(o)
        lses.append(l)
        W.update(fetch(f"attn{g}", o))
    cat_ab, lse = _merge_fwd(cat_ab, outs, lses)
    W.update(fetch("ab_out", lse))
    x1, h1 = _mm_nn(cat_ab, W["w_ab_out"], "mm_ab_out", mode="rms", resid=x, gain=W["ffn_norm_g"][0:1])
    pf0, act0 = _ffn_in(h1, W["wt_ffn_in0"], "ffn_in0")
    W.update(fetch("ffn_down0", act0))
    x2, h2 = _mm_nn(act0, W["w_ffn_down0"], "mm_ffn_down0", mode="rms", resid=x1, gain=W["cd_norm_g"],
                    dep=W.get("dep_down0"))
    W.update(fetch("cd_in", h2))
    pcd = _mm_nt(h2, W["wt_cd_in"], "mm_cd_in")
    cw8 = jnp.repeat(W["conv_c_w32"], 8, axis=0)
    cat_cd, c0, c1, dd, yv = _cd_fwd(pcd, cw8, W["conv_c_b"], W["c_ln_g"], W["c_ln_b"], W["conv_d_w8"])
    x3, h3 = _mm_nn(cat_cd, W["w_cd_out"], "mm_cd_out", mode="rms", resid=x2, gain=W["ffn_norm_g"][1:2])
    pf1, act1 = _ffn_in(h3, W["wt_ffn_in1"], "ffn_in1")
    dy, dyb, loss_cols = _mm_nn(act1, W["w_ffn_down1"], "mm_ffn_down1", mode="loss", resid=x3, tgt=tgt)

    def ffn_bwd(xin, h, pf, act, dres, dresb, layer):
        G[f"w_ffn_down{layer}"] = _mm_tn(act, dresb, f"mm_g_ffn_down{layer}")
        dep = on_grad(f"w_ffn_down{layer}", G[f"w_ffn_down{layer}"])
        dpf = _ffn_dact(dresb, W[f"w_ffn_down{layer}"], pf, f"ffn_dact{layer}", dep=dep)
        G[f"wt_ffn_in{layer}"] = _mm_tn(dpf, h, f"mm_g_ffn_in{layer}")
        dep = on_grad(f"wt_ffn_in{layer}", G[f"wt_ffn_in{layer}"])
        dx, dxb, G[f"ffn_norm_g{layer}"] = _mm_dh_rms_bwd(
            dpf, W[f"wt_ffn_in{layer}"], xin, W["ffn_norm_g"][layer:layer + 1], dres, f"mm_d_h_ffn{layer}", dep=dep)
        return dx, dxb

    dx3, dx3b = ffn_bwd(x3, h3, pf1, act1, dy, dyb, 1)

    G["w_cd_out"] = _mm_tn(cat_cd, dx3b, "mm_g_cd_out")
    dep = on_grad("w_cd_out", G["w_cd_out"])
    dc1, dy3, dgb, G["c_ln_g"], G["c_ln_b"], G["conv_c_b"] = _d_cat_cd(
        dx3b, W["w_cd_out"], c1, pcd, yv, W["c_ln_g"], W["c_ln_b"], dep)
    dpcd, G["conv_c_w32"], G["conv_d_w8"] = _cd_bwd_conv(pcd, dc1, dy3, c0, dd, dgb, cw8, W["conv_d_w8"])
    G["wt_cd_in"] = _mm_tn(dpcd, h2, "mm_g_cd_in")
    dep = on_grad("wt_cd_in", G["wt_cd_in"])
    dx2, dx2b, G["cd_norm_g"] = _mm_dh_rms_bwd(dpcd, W["wt_cd_in"], x2, W["cd_norm_g"], dx3, "mm_d_h_cd", dep=dep)

    dx1, dx1b = ffn_bwd(x1, h1, pf0, act0, dx2, dx2b, 0)

    G["w_ab_out"] = _mm_tn(cat_ab, dx1b, "mm_g_ab_out")
    dep = on_grad("w_ab_out", G["w_ab_out"])
    dcat_a, dbp, e = _d_cat_ab(dx1b, W["w_ab_out"], cat_ab, dep)
    dqkv = []
    for g, rate in enumerate(DIL_RATES):
        dqkv += _attn_bwd(qkv[3 * g], qkv[3 * g + 1], qkv[3 * g + 2], dbp, e, lse, rate, f"attn_bwd_{g}")
    dpab, G["sgu_w"], dbias_part, G["sgu_norm_g"], G["sgu_norm_b"], dgain = _ab_in_bwd(
        pab, dcat_a, W["sgu_norm_g"], W["sgu_norm_b"], W["sgu_w"], bias3, qg, kg, tabs, dqkv, rinvs)
    G["sgu_bias"] = jnp.sum(dbias_part, axis=-1)
    dgain = dgain[0:6, 0:HEAD] + dgain[0:6, HEAD:PAIR]
    G["q_norm_g"] = dgain[0::2]
    G["k_norm_g"] = dgain[1::2]
    G["loss_cols"] = loss_cols
    dep = on_grad("small", G)
    G["wt_ab_in"] = _mm_tn(dpab, h0, "mm_g_ab_in", dep=dep)
    dep = on_grad("wt_ab_in", G["wt_ab_in"])
    grad_x, G["ab_norm_g"] = _mm_dh_rms_bwd(dpab, W["wt_ab_in"], x, W["ab_norm_g"], dx1, "mm_d_h_ab", dep=dep,
                                            bf16_copy=False)
    return loss_cols, grad_x, G


def _my_place():
    return lax.axis_index("x"), lax.axis_index("y"), lax.axis_index("c")


def _dev_index(px, py, pc):
    return 4 * px + 2 * py + pc


def _flip(place, k):
    x, y, c = place
    return (1 - x if k & 4 else x, 1 - y if k & 2 else y, 1 - c if k & 1 else c)


def _landing(shape, dtype, own):
    buf = lax.empty(shape, dtype)
    for lead, part in own:
        buf = lax.dynamic_update_slice(buf, part.reshape((1,) * len(lead) + part.shape),
                                       tuple(lead) + (0,) * part.ndim)
    return buf


HBM_ONLY = pl.BlockSpec(memory_space=pltpu.HBM)
SEM_SPEC = pl.BlockSpec(memory_space=pltpu.SEMAPHORE)
IN_FLIGHT = pltpu.CompilerParams(has_side_effects=pltpu.SideEffectType.DATAFLOW_SIDE_EFFECTING)


def _in_hbm(a):
    return pltpu.with_memory_space_constraint(a, pltpu.HBM)


def _exchange_start(name, srcs, lands, items, dep=None):
    ns, nl, ni = len(srcs), len(lands), len(items)

    def body(*refs):
        S, L = refs[0:ns], refs[ns:ns + nl]
        first_out = ns + nl + (0 if dep is None else 1)
        send_sems, recv_sems, token = refs[first_out], refs[first_out + 1], refs[-1]
        me = _my_place()
        mi = _dev_index(*me)
        for i, (src, dst) in enumerate(items):
            for k in range(1, NDEV):
                peer = _flip(me, k)
                pltpu.make_async_remote_copy(
                    src_ref=src(S, _dev_index(*peer)), dst_ref=dst(L, mi), send_sem=send_sems.at[7 * i + k - 1],
                    recv_sem=recv_sems.at[7 * i + k - 1], device_id=peer, device_id_type=MESH).start()
        token[...] = jnp.zeros_like(token)

    thru = [pltpu.HBM(a.shape, a.dtype) for a in list(srcs) + list(lands)]
    args = [_in_hbm(a) for a in srcs] + [_in_hbm(a) for a in lands]
    in_specs = [HBM_ONLY] * (ns + nl)
    if dep is not None:
        args.append(dep)
        in_specs.append(HBM_SPEC)
    outs = pl.pallas_call(
        body, name=name, in_specs=in_specs,
        out_shape=(pltpu.SemaphoreType.DMA((7 * ni,)), pltpu.SemaphoreType.DMA((7 * ni,)), *thru, _sds((8, 128), F32)),
        out_specs=(SEM_SPEC, SEM_SPEC, *[HBM_ONLY] * (ns + nl), pl.BlockSpec(memory_space=pltpu.VMEM)),
        input_output_aliases={j: 2 + j for j in range(ns + nl)}, compiler_params=IN_FLIGHT)(*args)
    return dict(send=outs[0], recv=outs[1], srcs=list(outs[2:2 + ns]), lands=list(outs[2 + ns:2 + ns + nl]),
                token=outs[-1], items=items)


def _exchange_wait(name, states, after):
    after = list(after) if isinstance(after, (list, tuple)) else [after]
    counts = [(len(st["srcs"]), len(st["lands"]), len(st["items"])) for st in states]
    n_arrays = sum(c[0] + c[1] for c in counts)

    def body(*refs):
        me = _my_place()
        mi = _dev_index(*me)
        pos = 0
        sem_pos = n_arrays
        for st, (ns, nl, ni) in zip(states, counts):
            S, L = refs[pos:pos + ns], refs[pos + ns:pos + ns + nl]
            send_sems, recv_sems = refs[sem_pos], refs[sem_pos + 1]
            pos += ns + nl
            sem_pos += 2
            for i, (src, dst) in enumerate(st["items"]):
                for k in range(1, NDEV):
                    cp = pltpu.make_async_remote_copy(
                        src_ref=src(S, mi), dst_ref=dst(L, mi), send_sem=send_sems.at[7 * i + k - 1],
                        recv_sem=recv_sems.at[7 * i + k - 1], device_id=me, device_id_type=MESH)
                    cp.wait_send()
                    cp.wait_recv()

    arrays, sems = [], []
    for st in states:
        arrays += st["srcs"] + st["lands"]
        sems += [st["send"], st["recv"]]
    outs = pl.pallas_call(
        body, name=name, in_specs=[HBM_ONLY] * n_arrays + [SEM_SPEC] * len(sems) + [HBM_SPEC] * len(after),
        out_shape=tuple(pltpu.HBM(a.shape, a.dtype) for a in arrays), out_specs=tuple([HBM_ONLY] * n_arrays),
        input_output_aliases={j: j for j in range(n_arrays)}, compiler_params=IN_FLIGHT)(*arrays, *sems, *after)
    lands, pos = [], 0
    for ns, nl, _ in counts:
        lands.append(list(outs[pos + ns:pos + ns + nl]))
        pos += ns + nl
    return lands


def _place_and_neighbours():
    x, y, c = _my_place()
    return (x, y, c), (x, y, 1 - c), [(1 - x, y), (x, 1 - y), (1 - x, 1 - y)]


def _gather_start(name, srcs, lands, items, dep=None):
    ns, nl, ni = len(srcs), len(lands), len(items)

    def body(*refs):
        S, L = refs[0:ns], refs[ns:ns + nl]
        first_out = ns + nl + (0 if dep is None else 1)
        send_sems, recv_sems, token = refs[first_out], refs[first_out + 1], refs[-1]
        me, sib, chips = _place_and_neighbours()
        mi = _dev_index(*me)
        for i, (src, dst) in enumerate(items):
            for k, to in enumerate([sib] + [(*chip, me[2]) for chip in chips]):
                pltpu.make_async_remote_copy(
                    src_ref=src(S), dst_ref=dst(L, mi), send_sem=send_sems.at[4 * i + k],
                    recv_sem=recv_sems.at[4 * i + k], device_id=to, device_id_type=MESH).start()
        token[...] = jnp.zeros_like(token)

    thru = [pltpu.HBM(a.shape, a.dtype) for a in list(srcs) + list(lands)]
    args = [_in_hbm(a) for a in srcs] + [_in_hbm(a) for a in lands]
    in_specs = [HBM_ONLY] * (ns + nl)
    if dep is not None:
        args.append(dep)
        in_specs.append(HBM_SPEC)
    outs = pl.pallas_call(
        body, name=name, in_specs=in_specs,
        out_shape=(pltpu.SemaphoreType.DMA((4 * ni,)), pltpu.SemaphoreType.DMA((4 * ni,)), *thru, _sds((8, 128), F32)),
        out_specs=(SEM_SPEC, SEM_SPEC, *[HBM_ONLY] * (ns + nl), pl.BlockSpec(memory_space=pltpu.VMEM)),
        input_output_aliases={j: 2 + j for j in range(ns + nl)}, compiler_params=IN_FLIGHT)(*args)
    return dict(send=outs[0], recv=outs[1], srcs=list(outs[2:2 + ns]), lands=list(outs[2 + ns:2 + ns + nl]),
                token=outs[-1], items=items)


def _gather_forward(name, st, after):
    nl, ni = len(st["lands"]), len(st["items"])

    def body(*refs):
        L, recv_sems = refs[0:nl], refs[nl]
        fwd_send, fwd_recv, token = refs[-3:]
        me, sib, chips = _place_and_neighbours()
        for i, (_, dst) in enumerate(st["items"]):
            for j, chip in enumerate(chips):
                blk = dst(L, _dev_index(*chip, me[2]))
                pltpu.make_async_remote_copy(
                    src_ref=blk, dst_ref=blk, send_sem=fwd_send.at[3 * i + j], recv_sem=recv_sems.at[4 * i + 1 + j],
                    device_id=me, device_id_type=MESH).wait_recv()
                pltpu.make_async_remote_copy(
                    src_ref=blk, dst_ref=blk, send_sem=fwd_send.at[3 * i + j], recv_sem=fwd_recv.at[3 * i + j],
                    device_id=sib, device_id_type=MESH).start()
        token[...] = jnp.zeros_like(token)

    after = list(after) if isinstance(after, (list, tuple)) else [after]
    outs = pl.pallas_call(
        body, name=name, in_specs=[HBM_ONLY] * nl + [SEM_SPEC] + [HBM_SPEC] * len(after),
        out_shape=(*[pltpu.HBM(a.shape, a.dtype) for a in st["lands"]], pltpu.SemaphoreType.DMA((3 * ni,)),
                   pltpu.SemaphoreType.DMA((3 * ni,)), _sds((8, 128), F32)),
        out_specs=(*[HBM_ONLY] * nl, SEM_SPEC, SEM_SPEC, pl.BlockSpec(memory_space=pltpu.VMEM)),
        input_output_aliases={j: j for j in range(nl)}, compiler_params=IN_FLIGHT)(*st["lands"], st["recv"], *after)
    return dict(st, lands=list(outs[0:nl]), fwd_send=outs[nl], fwd_recv=outs[nl + 1], token=outs[-1])


def _gather_wait(name, st, after):
    ns, nl, ni = len(st["srcs"]), len(st["lands"]), len(st["items"])

    def body(*refs):
        S, L = refs[0:ns], refs[ns:ns + nl]
        send_sems, recv_sems, fwd_send, fwd_recv = refs[ns + nl:ns + nl + 4]
        me, sib, chips = _place_and_neighbours()
        mi = _dev_index(*me)
        for i, (src, dst) in enumerate(st["items"]):
            mine = dst(L, mi)
            for k in range(4):
                pltpu.make_async_remote_copy(
                    src_ref=src(S), dst_ref=mine, send_sem=send_sems.at[4 * i + k], recv_sem=recv_sems.at[4 * i + k],
                    device_id=me, device_id_type=MESH).wait_send()
            pltpu.make_async_remote_copy(
                src_ref=src(S), dst_ref=mine, send_sem=send_sems.at[4 * i], recv_sem=recv_sems.at[4 * i],
                device_id=me, device_id_type=MESH).wait_recv()
            for j in range(3):
                cp = pltpu.make_async_remote_copy(
                    src_ref=mine, dst_ref=mine, send_sem=fwd_send.at[3 * i + j], recv_sem=fwd_recv.at[3 * i + j],
                    device_id=me, device_id_type=MESH)
                cp.wait_send()
                cp.wait_recv()

    arrays = st["srcs"] + st["lands"]
    outs = pl.pallas_call(
        body, name=name, in_specs=[HBM_ONLY] * (ns + nl) + [SEM_SPEC] * 4 + [HBM_SPEC],
        out_shape=tuple(pltpu.HBM(a.shape, a.dtype) for a in arrays), out_specs=tuple([HBM_ONLY] * (ns + nl)),
        input_output_aliases={j: j for j in range(ns + nl)},
        compiler_params=IN_FLIGHT)(*arrays, st["send"], st["recv"], st["fwd_send"], st["fwd_recv"], after)
    return list(outs[ns:ns + nl])


def _sum_slots(land):
    def body(l_ref, o_ref):
        acc = l_ref[0]
        for d in range(1, NDEV):
            acc = acc + l_ref[d]
        o_ref[...] = acc

    vm = pl.BlockSpec(memory_space=pltpu.VMEM)
    return pl.pallas_call(body, name="sum_small", out_shape=_sds(land.shape[1:], F32), in_specs=[vm], out_specs=vm)(land)


def _adam_math(w, g, m, v):
    m2 = ADAM_B1 * m + (1.0 - ADAM_B1) * g
    v2 = ADAM_B2 * v + (1.0 - ADAM_B2) * (g * g)
    delta = -ADAM_LR * ((m2 * ADAM_C1) / (jnp.sqrt(v2 * ADAM_C2) + ADAM_EPS) + ADAM_WD * w)
    return delta, m2, v2


def _adam_layer(land, sel, w, m, v, layer, name, prev=None, tc=512):
    R = land.shape[2]

    def body(l_ref, w_ref, m_ref, v_ref, *rest):
        g_out, d_out, m_out, v_out = rest[-4:]
        g = l_ref[0].astype(F32)
        for d in range(1, NDEV):
            g = g + l_ref[d].astype(F32)
        delta, m2, v2 = _adam_math(w_ref[...], g, m_ref[...], v_ref[...])
        g_out[...] = g
        d_out[...] = delta
        m_out[...] = m2
        v_out[...] = v2

    wspec = pl.BlockSpec((None, R, tc), lambda i: (layer, 0, i))
    in_specs = [pl.BlockSpec((None, NDEV, R, tc), lambda i: (sel, 0, 0, i)), wspec, wspec, wspec]
    args = [land, w, m, v]
    aliases = {}
    if prev is not None:
        in_specs += [HBM_SPEC] * 4
        args += list(prev)
        aliases = {4 + j: j for j in range(4)}
    return pl.pallas_call(
        body, name=name, grid=(D // tc,), in_specs=in_specs, out_specs=[wspec] * 4,
        out_shape=[_sds(w.shape, F32)] * 4, input_output_aliases=aliases, compiler_params=_cparams(1))(*args)


def _adam_stacked(lands, sel, w, m, v, name):
    res = None
    for layer, land in enumerate(lands):
        res = _adam_layer(land, sel, w, m, v, layer, f"{name}{layer}", prev=res)
    return res


def _adam_small(ws, gs, ms, vs):
    n = len(ws)

    def body(*refs):
        w_r, g_r, m_r, v_r = refs[0:n], refs[n:2 * n], refs[2 * n:3 * n], refs[3 * n:4 * n]
        d_o, m_o, v_o = refs[4 * n:5 * n], refs[5 * n:6 * n], refs[6 * n:7 * n]
        for i in range(n):
            delta, m2, v2 = _adam_math(w_r[i][...], g_r[i][...], m_r[i][...], v_r[i][...])
            d_o[i][...] = delta
            m_o[i][...] = m2
            v_o[i][...] = v2

    vm = pl.BlockSpec(memory_space=pltpu.VMEM)
    shapes = [_sds(w.shape, F32) for w in ws]
    outs = pl.pallas_call(body, name="adam_small", in_specs=[vm] * (4 * n), out_specs=[vm] * (3 * n),
                          out_shape=shapes * 3)(*ws, *gs, *ms, *vs)
    return outs[0:n], outs[n:2 * n], outs[2 * n:3 * n]


def _adam_of_slots(land, w, m, v, name):
    def body(l_ref, w_ref, m_ref, v_ref, g_o, d_o, m_o, v_o):
        g = l_ref[0]
        for d in range(1, NDEV):
            g = g + l_ref[d]
        g_o[...] = g
        d_o[...], m_o[...], v_o[...] = _adam_math(w_ref[...], g, m_ref[...], v_ref[...])

    vm = pl.BlockSpec(memory_space=pltpu.VMEM)
    return pl.pallas_call(body, name=name, in_specs=[vm] * 4, out_specs=[vm] * 4,
                          out_shape=[_sds(w.shape, F32)] * 4)(land, w, m, v)


WEIGHT_NAMES = ("ab_norm_g", "ab_w_in", "sgu_norm_g", "sgu_norm_b", "sgu_w", "sgu_bias", "q_norm_g", "k_norm_g",
                "ab_w_out", "cd_norm_g", "cd_w_in", "conv_c_w", "conv_c_b", "c_ln_g", "c_ln_b", "conv_d_w",
                "cd_w_out", "ffn_norm_g", "ffn_w_gate", "ffn_w_up", "ffn_w_down")
SMALL_SHAPES = (("sgu_norm_g", (1, 512)), ("sgu_norm_b", (1, 512)), ("sgu_w", (512, 128)),
                ("sgu_bias", (4, 128)), ("q_norm_g", (3, 1, 64)), ("k_norm_g", (3, 1, 64)), ("cd_norm_g", (1, 128)),
                ("conv_c_w", (31, 1, 64)), ("conv_c_b", (1, 64)), ("c_ln_g", (1, 64)), ("c_ln_b", (1, 64)),
                ("conv_d_w", (3, 1, 64)), ("ffn_norm_g", (2, 1024)))
SHARD_C = 64


def _pack_rows(parts, rows):
    flat = jnp.concatenate([p.reshape(-1) for p in parts])
    return jnp.pad(flat, (0, rows * 128 - flat.shape[0])).reshape(rows, 128)


def kernel(x, ab_norm_g, ab_w_in, sgu_norm_g, sgu_norm_b, sgu_w, sgu_bias, q_norm_g, k_norm_g, ab_w_out, cd_norm_g, cd_w_in, conv_c_w, conv_c_b, c_ln_g, c_ln_b, conv_d_w, cd_w_out, ffn_norm_g, ffn_w_gate, ffn_w_up, ffn_w_down, loss_target, m_ab_norm_g, m_ab_w_in, m_sgu_norm_g, m_sgu_norm_b, m_sgu_w, m_sgu_bias, m_q_norm_g, m_k_norm_g, m_ab_w_out, m_cd_norm_g, m_cd_w_in, m_conv_c_w, m_conv_c_b, m_c_ln_g, m_c_ln_b, m_conv_d_w, m_cd_w_out, m_ffn_norm_g, m_ffn_w_gate, m_ffn_w_up, m_ffn_w_down, v_ab_norm_g, v_ab_w_in, v_sgu_norm_g, v_sgu_norm_b, v_sgu_w, v_sgu_bias, v_q_norm_g, v_k_norm_g, v_ab_w_out, v_cd_norm_g, v_cd_w_in, v_conv_c_w, v_conv_c_b, v_c_ln_g, v_c_ln_b, v_conv_d_w, v_cd_w_out, v_ffn_norm_g, v_ffn_w_gate, v_ffn_w_up, v_ffn_w_down):
    w = dict(zip(WEIGHT_NAMES, (ab_norm_g, ab_w_in, sgu_norm_g, sgu_norm_b, sgu_w, sgu_bias, q_norm_g, k_norm_g, ab_w_out, cd_norm_g, cd_w_in, conv_c_w, conv_c_b, c_ln_g, c_ln_b, conv_d_w, cd_w_out, ffn_norm_g, ffn_w_gate, ffn_w_up, ffn_w_down)))
    m = dict(zip(WEIGHT_NAMES, (m_ab_norm_g, m_ab_w_in, m_sgu_norm_g, m_sgu_norm_b, m_sgu_w, m_sgu_bias, m_q_norm_g, m_k_norm_g, m_ab_w_out, m_cd_norm_g, m_cd_w_in, m_conv_c_w, m_conv_c_b, m_c_ln_g, m_c_ln_b, m_conv_d_w, m_cd_w_out, m_ffn_norm_g, m_ffn_w_gate, m_ffn_w_up, m_ffn_w_down)))
    v = dict(zip(WEIGHT_NAMES, (v_ab_norm_g, v_ab_w_in, v_sgu_norm_g, v_sgu_norm_b, v_sgu_w, v_sgu_bias, v_q_norm_g, v_k_norm_g, v_ab_w_out, v_cd_norm_g, v_cd_w_in, v_conv_c_w, v_conv_c_b, v_c_ln_g, v_c_ln_b, v_conv_d_w, v_cd_w_out, v_ffn_norm_g, v_ffn_w_gate, v_ffn_w_up, v_ffn_w_down)))
    me = _dev_index(*_my_place())

    r_ff = DFF // NDEV
    one = lambda a: (lambda S, j: S[a])
    slot = lambda b: (lambda L, s: L[b].at[s])
    slot2 = lambda b, part: (lambda L, s: L[b].at[part, s])
    shard = lambda a: (lambda S: S[a])

    def later(a):
        return lax.optimization_barrier((a, gathers[0]["token"]))[0]

    def layer_shards(layer):
        return (later(w["ffn_w_gate"][layer]).T.astype(BF16), later(w["ffn_w_up"][layer]).T.astype(BF16),
                later(w["ffn_w_down"][layer]).astype(BF16))

    def gathered(own):
        return _landing((NDEV,) + own.shape, BF16, [((me,), own)])

    def gathered2(a, b):
        return _landing((2, NDEV) + a.shape, BF16, [((0, me), a), ((1, me), b)])

    ab_in_s = w["ab_w_in"][0].T.astype(BF16)
    gathers = {0: _gather_start("gather0_start", [ab_in_s], [gathered(ab_in_s)], [(shard(0), slot(0))])}

    def chan(flat, lo, taps):
        return flat[:, lo:lo + taps * SHARD_C].reshape(NDEV, taps, SHARD_C).transpose(1, 0, 2).reshape(taps, 512)

    def fetch(stage, after):
        if stage == "ab_in":
            ab_out_s = later(w["ab_w_out"][0]).astype(BF16)
            gate0, up0, down0 = layer_shards(0)
            small_s = _pack_rows([later(w[n]) for n in ("cd_norm_g", "conv_c_w", "conv_c_b", "c_ln_g", "c_ln_b",
                                                        "conv_d_w")], 24)
            lands1 = [gathered(ab_out_s), gathered2(gate0, up0), gathered(down0),
                      _landing((NDEV,) + small_s.shape, F32, [((me,), small_s)])]
            gathers[0] = _gather_forward("gather0_forward", gathers[0], [after] + lands1)
            l_ab_in, = _gather_wait("gather0_wait", gathers[0], gathers[0]["token"])
            gathers[1] = _gather_start(
                "gather1_start", [ab_out_s, gate0, up0, down0, small_s], lands1,
                [(shard(0), slot(0)), (shard(1), slot2(1, 0)), (shard(2), slot2(1, 1)), (shard(3), slot(2)),
                 (shard(4), slot(3))], dep=l_ab_in)
            return {"wt_ab_in": l_ab_in.reshape(AB_IN, D), "dep0": gathers[1]["token"]}
        if stage == "attn0":
            cd_in_s, cd_out_s = later(w["cd_w_in"][0]).T.astype(BF16), later(w["cd_w_out"][0]).astype(BF16)
            gate1, up1, down1 = layer_shards(1)
            gathers[2] = _gather_start(
                "gather2_start", [cd_in_s, cd_out_s, gate1, up1, down1],
                [gathered(cd_in_s), gathered(cd_out_s), gathered2(gate1, up1), gathered(down1)],
                [(shard(0), slot(0)), (shard(1), slot(1)), (shard(2), slot2(2, 0)), (shard(3), slot2(2, 1)),
                 (shard(4), slot(3))], dep=after)
            return {"dep_attn1": gathers[2]["token"]}
        if stage == "attn1":
            gathers[1] = _gather_forward("gather1_forward", gathers[1], after)
            return {"dep_attn2": gathers[1]["token"]}
        if stage == "ab_out":
            l_out, l_ffn, l_down, l_small = _gather_wait("gather1_wait", gathers[1], after)
            flat = l_small.reshape(NDEV, 24 * 128)
            return {
                "w_ab_out": l_out.reshape(D, D), "wt_ffn_in0": l_ffn.reshape(2 * DFF, D),
                "w_ffn_down0": l_down.reshape(DFF, D), "cd_norm_g": flat[:, 0:128].reshape(1, D),
                "conv_c_w32": jnp.pad(chan(flat, 128, CONV_C_TAPS), ((0, 1), (0, 0))),
                "conv_c_b": chan(flat, 2112, 1), "c_ln_g": chan(flat, 2176, 1), "c_ln_b": chan(flat, 2240, 1),
                "conv_d_w8": jnp.pad(chan(flat, 2304, CONV_D_TAPS), ((0, 8 - CONV_D_TAPS), (0, 0))),
            }
        if stage == "ffn_down0":
            gathers[2] = _gather_forward("gather2_forward", gathers[2], after)
            return {"dep_down0": gathers[2]["token"]}
        if stage == "cd_in":
            l_in, l_out, l_ffn, l_down = _gather_wait("gather2_wait", gathers[2], after)
            return {"wt_cd_in": l_in.reshape(CD_IN, D), "w_cd_out": l_out.reshape(D, D),
                    "wt_ffn_in1": l_ffn.reshape(2 * DFF, D), "w_ffn_down1": l_down.reshape(DFF, D)}
        return {}

    scatters = {}
    rides_with = {"w_ffn_down1": "wt_ffn_in1", "w_cd_out": "wt_cd_in", "w_ffn_down0": "wt_ffn_in0"}
    held = {}
    smalls = {}

    def small_exchange(name, block):
        land = _landing((NDEV,) + block.shape, F32, [((me,), block)])
        return _exchange_start(name, [block], [land], [(one(0), slot(0))])

    def on_grad(key, arr):
        if key == "small":
            parts = [arr["sgu_norm_g"], arr["sgu_norm_b"], arr["sgu_w"], arr["sgu_bias"], arr["q_norm_g"],
                     arr["k_norm_g"], arr["cd_norm_g"], arr["conv_c_w32"][:CONV_C_TAPS], arr["conv_c_b"], arr["c_ln_g"],
                     arr["c_ln_b"], arr["conv_d_w8"][:CONV_D_TAPS], arr["ffn_norm_g0"], arr["ffn_norm_g1"],
                     arr["loss_cols"]]
            smalls["sizes"] = [p.size for p in parts]
            rows = -(-sum(smalls["sizes"]) // 1024) * 8
            smalls["early"] = small_exchange("small_start", _pack_rows(parts, rows))
            return smalls["early"]["token"]
        if key in rides_with:
            held[rides_with[key]] = (key, arr)
            return None
        group = ([held.pop(key)] if key in held else []) + [(key, arr)]
        srcs, lands, items = [], [], []
        for n, (k, a) in enumerate(group):
            if k.startswith("wt_ffn_in"):
                src = a.reshape(2, NDEV, r_ff, D)
                own = lax.dynamic_slice_in_dim(src, me, 1, axis=1)
                lands.append(lax.dynamic_update_slice(lax.empty(src.shape, BF16), own, (0, me, 0, 0)))
                items += [((lambda S, j, n=n: S[n].at[0, j]), slot2(n, 0)), ((lambda S, j, n=n: S[n].at[1, j]), slot2(n, 1))]
            else:
                rows = a.shape[0] // NDEV
                src = a.reshape(NDEV, rows, D)
                own = lax.dynamic_index_in_dim(src, me, 0, keepdims=False)
                lands.append(_landing((1, NDEV, rows, D), BF16, [((0, me), own)]))
                items.append(((lambda S, j, n=n: S[n].at[j]), slot2(n, 0)))
            srcs.append(src)
        st = _exchange_start(f"scatter_{key}_start", srcs, lands, items)
        scatters[key] = (st, [k for k, _ in group])
        return st["token"]

    W = {
        "dep_first": gathers[0]["token"],
        "ab_norm_g": w["ab_norm_g"], "sgu_norm_g": w["sgu_norm_g"], "sgu_norm_b": w["sgu_norm_b"],
        "sgu_w": w["sgu_w"][0], "sgu_bias": w["sgu_bias"][0], "q_norm_g": w["q_norm_g"][0],
        "k_norm_g": w["k_norm_g"][0], "ffn_norm_g": w["ffn_norm_g"],
    }

    loss_cols, grad_x, G = _local_step(x[0], loss_target[0], W, fetch, on_grad)

    late_small = small_exchange("small_late_start", G["ab_norm_g"])
    landed = {}

    def wait_scatters(name, group_keys, others, after):
        res = _exchange_wait(name, [scatters[gk][0] for gk in group_keys] + others, after)
        for gk, lands in zip(group_keys, res):
            landed.update(zip(scatters[gk][1], lands))
        return [lands[0] for lands in res[len(group_keys):]]

    small_land, = wait_scatters("scatter_wait_early", ["wt_ffn_in1", "wt_cd_in", "wt_ffn_in0", "w_ab_out"],
                                [smalls["early"]], late_small["token"])

    grads, deltas, new_m, new_v = {}, {}, {}, {}
    done = []

    def put(name, res):
        grads[name], deltas[name], new_m[name], new_v[name] = res

    def adam(name, lands, sel, transposed):
        flip = (lambda a: jnp.swapaxes(a, 1, 2)) if transposed else (lambda a: a)
        res = _adam_stacked(lands, sel, flip(w[name]), flip(m[name]), flip(v[name]), f"adam_{name}")
        done.append(res[1])
        put(name, [flip(r) for r in res])

    ffn_in_lands = [landed["wt_ffn_in0"], landed["wt_ffn_in1"]]
    adam("cd_w_in", [landed["wt_cd_in"]], 0, True)
    adam("ffn_w_gate", ffn_in_lands, 0, True)
    adam("ffn_w_up", ffn_in_lands, 1, True)
    adam("cd_w_out", [landed["w_cd_out"]], 0, False)
    adam("ab_w_out", [landed["w_ab_out"]], 0, False)
    adam("ffn_w_down", [landed["w_ffn_down0"], landed["w_ffn_down1"]], 0, False)

    red = _sum_slots(small_land).reshape(-1)
    offs = [0]
    for s in smalls["sizes"]:
        offs.append(offs[-1] + s)
    seg = [red[offs[i]:offs[i + 1]] for i in range(len(smalls["sizes"]))]
    loss = jnp.sum(seg[14])

    def own_channels(full, taps):
        return lax.dynamic_slice_in_dim(full.reshape(taps, 512), me * SHARD_C, SHARD_C, axis=1)

    g_small = {
        "sgu_norm_g": seg[0].reshape(1, 512), "sgu_norm_b": seg[1].reshape(1, 512),
        "sgu_w": seg[2].reshape(512, 128), "sgu_bias": seg[3].reshape(4, 128), "q_norm_g": seg[4].reshape(3, 64),
        "k_norm_g": seg[5].reshape(3, 64),
        "cd_norm_g": lax.dynamic_slice_in_dim(seg[6].reshape(1, D), me * (D // NDEV), D // NDEV, axis=1),
        "conv_c_w": own_channels(seg[7], CONV_C_TAPS), "conv_c_b": own_channels(seg[8], 1),
        "c_ln_g": own_channels(seg[9], 1), "c_ln_b": own_channels(seg[10], 1),
        "conv_d_w": own_channels(seg[11], CONV_D_TAPS),
        "ffn_norm_g": jnp.concatenate([seg[12].reshape(1, D), seg[13].reshape(1, D)], axis=0),
    }

    def small_in(s, a):
        return jnp.swapaxes(a, 0, 1) if len(s) == 3 else a.reshape(s)

    def small_out(n, s, a):
        return jnp.swapaxes(a, 0, 1) if len(s) == 3 else a.reshape(w[n].shape)

    g_in = [g_small[n].reshape(s) for n, s in SMALL_SHAPES]
    d_s, m_s, v_s = _adam_small([small_in(s, w[n]) for n, s in SMALL_SHAPES], g_in,
                                [small_in(s, m[n]) for n, s in SMALL_SHAPES],
                                [small_in(s, v[n]) for n, s in SMALL_SHAPES])
    for i, (n, s) in enumerate(SMALL_SHAPES):
        grads[n], deltas[n] = small_out(n, s, g_in[i]), small_out(n, s, d_s[i])
        new_m[n], new_v[n] = small_out(n, s, m_s[i]), small_out(n, s, v_s[i])
    done.append(d_s[0])

    late_land, = wait_scatters("scatter_wait_last", ["wt_ab_in"], [late_small], list(done))
    put("ab_norm_g", _adam_of_slots(late_land, w["ab_norm_g"], m["ab_norm_g"], v["ab_norm_g"], "adam_ab_norm_g"))
    adam("ab_w_in", [landed["wt_ab_in"]], 0, True)

    return (loss, grad_x[None], *[grads[n] for n in WEIGHT_NAMES], *[deltas[n] for n in WEIGHT_NAMES],
            *[new_m[n] for n in WEIGHT_NAMES], *[new_v[n] for n in WEIGHT_NAMES])
```

```python
import jax
import jax.numpy as jnp
import numpy as np
from jax import lax
from jax.experimental import pallas as pl
from jax.experimental.pallas import tpu as pltpu

F32 = jnp.float32
BF16 = jnp.bfloat16

T = 4096
D = 1024
NDEV = 8
EPS = 1e-6
NEG_INF = -1e30
DFF = 2816
AB_IN = 5632
CD_IN = 2560
HEAD = 64
PAIR = 128
NPAIR = 4
NBACK = 128
DIL_RATES = (1, 4, 16)
ROPE_HALF = 8
ROPE_THETA = 500000.0
CONV_C_TAPS = 31
CONV_D_TAPS = 3
HALO = 32
ATTN_BWD_UNROLL = 4
MAX_ROW_STRIDE = 4

ADAM_LR = 0.001
ADAM_B1 = 0.9
ADAM_B2 = 0.999
ADAM_EPS = 1e-08
ADAM_WD = 0.01
ADAM_STEP = 10
ADAM_C1 = 1.0 / (1.0 - ADAM_B1 ** ADAM_STEP)
ADAM_C2 = 1.0 / (1.0 - ADAM_B2 ** ADAM_STEP)

VMEM_LIMIT_MB = 48
MESH = pl.DeviceIdType.MESH
HBM_SPEC = pl.BlockSpec(memory_space=pl.ANY)


def _cparams(ngrid, vmem_mb=VMEM_LIMIT_MB):
    return pltpu.CompilerParams(dimension_semantics=("arbitrary",) * ngrid,
                                vmem_limit_bytes=vmem_mb * 1024 * 1024)


def _pick(n, options):
    for o in options:
        if n % o == 0:
            return o
    raise ValueError(f"no tile for {n} in {options}")


def _sds(shape, dtype):
    return jax.ShapeDtypeStruct(shape, dtype)


def _sigmoid(x):
    return 1.0 / (1.0 + jnp.exp(-x))


def _sigmoid_bf16(x):
    return 0.5 * jnp.tanh(0.5 * x) + 0.5


def _gelu(z):
    return 0.5 * z * (1.0 + lax.erf(z * 0.7071067811865476))


def _gelu_grad(z):
    return 0.5 * (1.0 + lax.erf(z * 0.7071067811865476)) + z * jnp.exp(-0.5 * z * z) * 0.3989422804014327


def _mm_nt(a, wt, name, out_dtype=BF16, dep=None):
    M, K = a.shape
    N = wt.shape[0]
    tn = _pick(N, (512, 256))

    def body(a_ref, w_ref, *rest):
        o_ref = rest[-1]
        for r0 in range(0, M, 1024):
            o_ref[r0:r0 + 1024, :] = lax.dot_general(
                a_ref[r0:r0 + 1024, :], w_ref[...], (((1,), (1,)), ((), ())),
                preferred_element_type=F32).astype(o_ref.dtype)

    in_specs = [pl.BlockSpec((M, K), lambda j: (0, 0), pipeline_mode=pl.Buffered(1)),
                pl.BlockSpec((tn, K), lambda j: (j, 0))]
    args = [a, wt]
    if dep is not None:
        in_specs.append(HBM_SPEC)
        args.append(dep)
    return pl.pallas_call(
        body, name=name, grid=(N // tn,), in_specs=in_specs, out_specs=pl.BlockSpec((M, tn), lambda j: (0, j)),
        out_shape=_sds((M, N), out_dtype), compiler_params=_cparams(1))(*args)


EPI_ROWS = 256


def _mm_nt_rows(a, wt, name, epilogue, side, side_specs, out_specs, out_shape, sums=(), dep=None, tm=512):
    M, K = a.shape
    N = wt.shape[0]
    ns, no = len(side), len(out_shape)

    def body(a_ref, w_ref, *rest):
        side_refs, outs, acc = rest[0:ns], rest[-1 - no:-1], rest[-1]
        acc[...] = lax.dot_general(a_ref[...], w_ref[...], (((1,), (1,)), ((), ())), preferred_element_type=F32)

        @pl.when(pl.program_id(0) == 0)
        def _():
            for j in sums:
                outs[j][...] = jnp.zeros_like(outs[j])

        for r0 in range(0, tm, EPI_ROWS):
            rows = slice(r0, r0 + EPI_ROWS)
            epilogue(acc[rows, :].astype(BF16).astype(F32), rows, side_refs, outs)

    in_specs = [pl.BlockSpec((tm, K), lambda i: (i, 0)),
                pl.BlockSpec((N, K), lambda i: (0, 0), pipeline_mode=pl.Buffered(1))] + list(side_specs)
    args = [a, wt, *side]
    if dep is not None:
        in_specs.append(HBM_SPEC)
        args.append(dep)
    return pl.pallas_call(
        body, name=name, grid=(M // tm,), in_specs=in_specs, out_specs=list(out_specs), out_shape=list(out_shape),
        scratch_shapes=[pltpu.VMEM((tm, N), F32)], compiler_params=_cparams(1))(*args)


def _mm_nn(a, w, name, mode, resid, gain=None, tgt=None, dep=None, tm=512):
    M, K = a.shape
    N = w.shape[1]
    side = gain if mode == "rms" else tgt

    def body(a_ref, w_ref, resid_ref, side_ref, *rest):
        outs, acc = rest[-3 if mode == "rms" else -4:-1], rest[-1]
        i = pl.program_id(0)
        acc[...] = jnp.dot(a_ref[...], w_ref[...], preferred_element_type=F32)

        if mode == "loss":
            @pl.when(i == 0)
            def _():
                outs[2][...] = jnp.zeros_like(outs[2])

        for r0 in range(0, tm, EPI_ROWS):
            rows = slice(r0, r0 + EPI_ROWS)
            v = acc[rows, :] + resid_ref[rows, :]
            if mode == "rms":
                outs[0][rows, :] = v
                r = lax.rsqrt(jnp.mean(v * v, axis=-1, keepdims=True) + EPS)
                outs[1][rows, :] = (v * r * side_ref[...]).astype(BF16)
            else:
                d = v - side_ref[rows, :]
                outs[2][...] += jnp.sum(d * d, axis=0, keepdims=True) * (0.5 / N)
                dy = d * (1.0 / N)
                outs[0][rows, :] = dy
                outs[1][rows, :] = dy.astype(BF16)

    row = pl.BlockSpec((tm, N), lambda i: (i, 0))
    vec = pl.BlockSpec((1, N), lambda i: (0, 0))
    in_specs = [pl.BlockSpec((tm, K), lambda i: (i, 0)),
                pl.BlockSpec((K, N), lambda i: (0, 0), pipeline_mode=pl.Buffered(1)), row,
                vec if mode == "rms" else row]
    args = [a, w, resid, side]
    if dep is not None:
        in_specs.append(HBM_SPEC)
        args.append(dep)
    if mode == "rms":
        out_specs, out_shape = [row, row], [_sds((M, N), F32), _sds((M, N), BF16)]
    else:
        out_specs, out_shape = [row, row, vec], [_sds((M, N), F32), _sds((M, N), BF16), _sds((1, N), F32)]
    return pl.pallas_call(
        body, name=name, grid=(M // tm,), in_specs=in_specs, out_specs=out_specs, out_shape=out_shape,
        scratch_shapes=[pltpu.VMEM((tm, N), F32)], compiler_params=_cparams(1))(*args)


def _mm_dh_rms_bwd(a, w, x, gain, dres, name, dep=None, tm=512, bf16_copy=True):
    parts = a.shape[0] if a.ndim == 3 else 1
    M, Kp = a.shape[-2], a.shape[-1]
    N = w.shape[1]
    nblk = M // tm
    assert nblk % 2 == 0

    def body(a_ref, w_ref, x_ref, g_ref, dres_ref, *rest):
        dg_ref, acc0, acc1 = rest[-3:]
        dx_ref = rest[-5] if bf16_copy else rest[-4]
        dxb_ref = rest[-4] if bf16_copy else None
        i = pl.program_id(0)

        def matmul(acc):
            if parts == 1:
                acc[...] = jnp.dot(a_ref[...], w_ref[...], preferred_element_type=F32)
            else:
                d = jnp.dot(a_ref[0], w_ref[0:Kp, :], preferred_element_type=F32)
                for p in range(1, parts):
                    d = d + jnp.dot(a_ref[p], w_ref[p * Kp:(p + 1) * Kp, :], preferred_element_type=F32)
                acc[...] = d

        def finish(acc):
            for r0 in range(0, tm, EPI_ROWS // 2):
                rows = slice(r0, r0 + EPI_ROWS // 2)
                v = acc[rows, :]
                xf = x_ref[rows, :]
                r = lax.rsqrt(jnp.mean(xf * xf, axis=-1, keepdims=True) + EPS)
                xhat = xf * r
                dg_ref[...] += jnp.sum(v * xhat, axis=0, keepdims=True)
                dxh = v * g_ref[...]
                tot = dres_ref[rows, :] + r * (dxh - xhat * jnp.mean(dxh * xhat, axis=-1, keepdims=True))
                dx_ref[rows, :] = tot
                if bf16_copy:
                    dxb_ref[rows, :] = tot.astype(BF16)

        @pl.when(i == 0)
        def _():
            dg_ref[...] = jnp.zeros_like(dg_ref)
            matmul(acc0)

        @pl.when((i > 0) & (i < nblk) & (i % 2 == 1))
        def _():
            matmul(acc1)
            finish(acc0)

        @pl.when((i > 0) & (i < nblk) & (i % 2 == 0))
        def _():
            matmul(acc0)
            finish(acc1)

        @pl.when(i == nblk)
        def _():
            finish(acc1)

    last = nblk - 1
    row = pl.BlockSpec((tm, N), lambda i: (jnp.maximum(i - 1, 0), 0))
    vec = pl.BlockSpec((1, N), lambda i: (0, 0))
    if a.ndim == 3:
        a_spec = pl.BlockSpec((parts, tm, Kp), lambda i: (0, jnp.minimum(i, last), 0))
    else:
        a_spec = pl.BlockSpec((tm, Kp), lambda i: (jnp.minimum(i, last), 0))
    w_spec = pl.BlockSpec((parts * Kp, N), lambda i: (0, 0), pipeline_mode=pl.Buffered(1))
    in_specs = [a_spec, w_spec, row, vec, row]
    args = [a, w, x, gain, dres]
    if dep is not None:
        in_specs.append(HBM_SPEC)
        args.append(dep)
    return pl.pallas_call(
        body, name=name, grid=(nblk + 1,), in_specs=in_specs,
        out_specs=[row, row, vec] if bf16_copy else [row, vec],
        out_shape=([_sds((M, N), F32), _sds((M, N), BF16), _sds((1, N), F32)] if bf16_copy
                   else [_sds((M, N), F32), _sds((1, N), F32)]),
        scratch_shapes=[pltpu.VMEM((tm, N), F32), pltpu.VMEM((tm, N), F32)], compiler_params=_cparams(1, 56))(*args)


def _mm_tn(a, b, name, out_dtype=BF16, tt=2048, dep=None):
    parts = a.shape[0] if a.ndim == 3 else 1
    Tt, Mp = a.shape[-2], a.shape[-1]
    N = b.shape[1]
    tn = _pick(Mp, (1408, 1280, 1024, 512))
    jper = Mp // tn
    nt = Tt // tt

    def body(a_ref, b_ref, *rest):
        o_ref, acc = rest[-2:]
        t = pl.program_id(1)

        @pl.when(t == 0)
        def _():
            acc[...] = jnp.zeros_like(acc)

        rows = pl.ds(pl.multiple_of(t * tt, tt), tt)
        acc[...] += lax.dot_general(a_ref[...], b_ref[rows, :], (((0,), (0,)), ((), ())),
                                    preferred_element_type=F32)

        @pl.when(t == nt - 1)
        def _():
            o_ref[...] = acc[...].astype(o_ref.dtype)

    if a.ndim == 3:
        a_spec = pl.BlockSpec((None, tt, tn), lambda j, t: (j // jper, t, j % jper))
    else:
        a_spec = pl.BlockSpec((tt, tn), lambda j, t: (t, j))
    in_specs = [a_spec, pl.BlockSpec((Tt, N), lambda j, t: (0, 0), pipeline_mode=pl.Buffered(1))]
    args = [a, b]
    if dep is not None:
        in_specs.append(HBM_SPEC)
        args.append(dep)
    return pl.pallas_call(
        body, name=name, grid=(parts * jper, nt), in_specs=in_specs,
        out_specs=pl.BlockSpec((tn, N), lambda j, t: (j, 0)),
        out_shape=_sds((parts * Mp, N), out_dtype), scratch_shapes=[pltpu.VMEM((tn, N), F32)],
        compiler_params=_cparams(2))(*args)


FFN_ROWS = 256
PREFETCH_SLOTS = 3


def _ring_tile(window, buf, sem, j, nj):
    def fetch(step):
        slot = step % PREFETCH_SLOTS
        return pltpu.make_async_copy(window(step), buf.at[slot], sem.at[slot])

    @pl.when(j == 0)
    def _():
        for s in range(PREFETCH_SLOTS - 1):
            fetch(s).start()

    @pl.when(j + PREFETCH_SLOTS - 1 < nj)
    def _():
        fetch(j + PREFETCH_SLOTS - 1).start()

    fetch(j).wait()
    return buf.at[j % PREFETCH_SLOTS]


def _ring_scratch(tile_shape, dtype):
    return [pltpu.VMEM((PREFETCH_SLOTS,) + tuple(tile_shape), dtype), pltpu.SemaphoreType.DMA((PREFETCH_SLOTS,))]


def _ffn_in(h, wt_in, name, tn=256):
    nj = DFF // tn

    def body(h_ref, wg_ref, wu_ref, p_ref, act_ref):
        nt = (((1,), (1,)), ((), ()))
        for r0 in range(0, T, FFN_ROWS):
            rows = slice(r0, r0 + FFN_ROWS)
            g = lax.dot_general(h_ref[rows, :], wg_ref[...], nt, preferred_element_type=F32).astype(BF16)
            u = lax.dot_general(h_ref[rows, :], wu_ref[...], nt, preferred_element_type=F32).astype(BF16)
            p_ref[0, rows, :] = g
            p_ref[1, rows, :] = u
            act_ref[rows, :] = g * _sigmoid_bf16(g) * u

    return pl.pallas_call(
        body, name=name, grid=(nj,),
        in_specs=[pl.BlockSpec((T, D), lambda j: (0, 0), pipeline_mode=pl.Buffered(1)),
                  pl.BlockSpec((tn, D), lambda j: (j, 0)), pl.BlockSpec((tn, D), lambda j: (j + nj, 0))],
        out_specs=[pl.BlockSpec((2, T, tn), lambda j: (0, 0, j)), pl.BlockSpec((T, tn), lambda j: (0, j))],
        out_shape=[_sds((2, T, DFF), BF16), _sds((T, DFF), BF16)], compiler_params=_cparams(1))(h, wt_in, wt_in)


def _ffn_dact(dyb, w_down, p3, name, tn=256, dep=None):
    nj = DFF // tn

    def body(dy_ref, w_ref, p_hbm, *rest):
        o_ref, p_buf, sem = rest[-3:]
        p_ref = _ring_tile(lambda step: p_hbm.at[:, :, pl.ds(pl.multiple_of(step * tn, tn), tn)], p_buf, sem,
                           pl.program_id(0), nj)
        for r0 in range(0, T, FFN_ROWS):
            rows = slice(r0, r0 + FFN_ROWS)
            da = lax.dot_general(dy_ref[rows, :], w_ref[...], (((1,), (1,)), ((), ())),
                                 preferred_element_type=F32).astype(BF16)
            g = p_ref[0, rows, :]
            u = p_ref[1, rows, :]
            sg = _sigmoid_bf16(g)
            gs = g * sg
            o_ref[0, rows, :] = (da * u) * (sg + gs * (1.0 - sg))
            o_ref[1, rows, :] = da * gs

    in_specs = [pl.BlockSpec((T, D), lambda j: (0, 0), pipeline_mode=pl.Buffered(1)),
                pl.BlockSpec((tn, D), lambda j: (j, 0)), HBM_SPEC]
    args = [dyb, w_down, p3]
    if dep is not None:
        in_specs.append(HBM_SPEC)
        args.append(dep)
    return pl.pallas_call(
        body, name=name, grid=(nj,), in_specs=in_specs, out_specs=pl.BlockSpec((2, T, tn), lambda j: (0, 0, j)),
        out_shape=_sds((2, T, DFF), BF16), scratch_shapes=_ring_scratch((2, T, tn), BF16),
        compiler_params=_cparams(1))(*args)


def _rms_fwd(x, g, name, tm=512, dep=None):
    def body(x_ref, g_ref, *rest):
        h_ref = rest[-1]
        xf = x_ref[...]
        r = lax.rsqrt(jnp.mean(xf * xf, axis=-1, keepdims=True) + EPS)
        h_ref[...] = (xf * r * g_ref[...]).astype(BF16)

    in_specs = [pl.BlockSpec((tm, D), lambda i: (i, 0)), pl.BlockSpec((1, D), lambda i: (0, 0))]
    args = [x, g]
    if dep is not None:
        in_specs.append(HBM_SPEC)
        args.append(dep)
    return pl.pallas_call(
        body, name=name, grid=(T // tm,), in_specs=in_specs, out_specs=pl.BlockSpec((tm, D), lambda i: (i, 0)),
        out_shape=_sds((T, D), BF16), compiler_params=_cparams(1))(*args)


def _tril_mask():
    r = lax.broadcasted_iota(jnp.int32, (128, 128), 0)
    c = lax.broadcasted_iota(jnp.int32, (128, 128), 1)
    return r >= c


def _mix_a_fwd(pab, sgu_g, sgu_b, sgu_w, sgu_bias3, tm=512):
    def body(zu_ref, zv_ref, g_ref, b_ref, w_ref, bias_ref, o_ref):
        u = _gelu(zu_ref[...].astype(F32))
        v = _gelu(zv_ref[...].astype(F32))
        mu = jnp.mean(v, axis=-1, keepdims=True)
        vc = v - mu
        rstd = lax.rsqrt(jnp.mean(vc * vc, axis=-1, keepdims=True) + EPS)
        vn = (vc * rstd * g_ref[...] + b_ref[...]).astype(BF16)
        tri = _tril_mask()
        for gi in range(4):
            wg = jnp.where(tri, w_ref[gi], 0.0).astype(BF16)
            bg = bias_ref[gi]
            for c in range(tm // 128):
                rs, cs = slice(c * 128, (c + 1) * 128), slice(gi * 128, (gi + 1) * 128)
                mixed = jnp.dot(wg, vn[rs, cs], preferred_element_type=F32) + bg
                o_ref[rs, cs] = (u[rs, cs] * mixed).astype(BF16)

    half = pl.BlockSpec((tm, 512), lambda i: (i, 0))
    return pl.pallas_call(
        body, name="mix_a_fwd", grid=(T // tm,),
        in_specs=[half, pl.BlockSpec((tm, 512), lambda i: (i, 1)),
                  pl.BlockSpec((1, 512), lambda i: (0, 0)), pl.BlockSpec((1, 512), lambda i: (0, 0)),
                  pl.BlockSpec((4, 128, 128), lambda i: (0, 0, 0)), pl.BlockSpec((4, 128, 1), lambda i: (0, 0, 0))],
        out_specs=half, out_shape=_sds((T, D), BF16), compiler_params=_cparams(1),
    )(pab, pab, sgu_g, sgu_b, sgu_w, sgu_bias3)


def _rope_tables():
    pos = np.arange(T, dtype=np.float32)
    inv_freq = np.float32(ROPE_THETA) ** (-np.arange(ROPE_HALF, dtype=np.float32) * np.float32(2.0 / (2 * ROPE_HALF)))
    ang = (pos[:, None] * inv_freq[None, :]).astype(np.float32)
    cos, sin = np.cos(ang), np.sin(ang)
    z8 = np.zeros((T, ROPE_HALF), np.float32)
    rest = np.zeros((T, HEAD - 2 * ROPE_HALF), np.float32)
    c64 = np.concatenate([cos, cos, rest + 1.0], axis=1)
    s1 = np.concatenate([z8, sin, rest], axis=1)
    s2 = np.concatenate([-sin, z8, rest], axis=1)
    return tuple(jnp.asarray(np.tile(t, (1, 2)).astype(np.float32)) for t in (c64, s1, s2))


def _lo_mask(shape):
    return lax.broadcasted_iota(jnp.int32, shape, 1) < HEAD


def _seg_mean(x, lo):
    s_all = jnp.sum(x, axis=-1, keepdims=True)
    s_lo = jnp.sum(jnp.where(lo, x, 0.0), axis=-1, keepdims=True)
    return jnp.where(lo, s_lo, s_all - s_lo) * (1.0 / HEAD)


def _head_blocks():
    r = lax.broadcasted_iota(jnp.int32, (PAIR, PAIR), 0) < HEAD
    c = lax.broadcasted_iota(jnp.int32, (PAIR, PAIR), 1) < HEAD
    return jnp.where(r == c, 1.0, 0.0).astype(BF16)


def _seg_mean_mxu(x, blocks):
    return jnp.dot(x.astype(BF16), blocks, preferred_element_type=F32) * (1.0 / HEAD)


def _rope(n, c, s1, s2):
    return n * c + pltpu.roll(n, ROPE_HALF, 1) * s1 + pltpu.roll(n, PAIR - ROPE_HALF, 1) * s2


def _rope_t(dy, c, s1, s2):
    return dy * c - pltpu.roll(dy, PAIR - ROPE_HALF, 1) * s2 - pltpu.roll(dy, ROPE_HALF, 1) * s1


def _prep_fwd(pab, qg, kg, tabs, tm=512):
    def body(p_ref, qg_ref, kg_ref, c_ref, s1_ref, s2_ref, *outs):
        blocks = _head_blocks()
        c, s1, s2 = c_ref[...], s1_ref[...], s2_ref[...]
        for g in range(3):
            qn_ref, kn_ref, v_ref = outs[3 * g:3 * g + 3]
            for p in range(NPAIR):
                for which, gains, dst in ((0, qg_ref, qn_ref), (1, kg_ref, kn_ref)):
                    col = (2 + 3 * which + g) * 512 + p * PAIR
                    xr = p_ref[:, col:col + PAIR].astype(F32)
                    rinv = lax.rsqrt(_seg_mean_mxu(xr * xr, blocks) + EPS)
                    outs[9 + 2 * g + which][p] = rinv.astype(BF16)
                    dst[p] = _rope(xr * rinv * gains[g:g + 1, :], c, s1, s2)
                col = (8 + g) * 512 + p * PAIR
                v_ref[p] = p_ref[:, col:col + PAIR].astype(F32)

    pm = pl.BlockSpec((NPAIR, tm, PAIR), lambda i: (0, i, 0))
    tab = pl.BlockSpec((tm, PAIR), lambda i: (i, 0))
    gain = pl.BlockSpec((3, PAIR), lambda i: (0, 0))
    res = pl.pallas_call(
        body, name="prep_fwd", grid=(T // tm,),
        in_specs=[pl.BlockSpec((tm, AB_IN), lambda i: (i, 0)), gain, gain, tab, tab, tab],
        out_specs=[pm] * 15, out_shape=[_sds((NPAIR, T, PAIR), F32)] * 9 + [_sds((NPAIR, T, PAIR), BF16)] * 6,
        compiler_params=_cparams(1))(pab, qg, kg, *tabs)
    return res[0:9], res[9:15]


def _res_index(it, rate):
    window = NBACK * rate
    b = it // rate
    rho = it % rate
    start = b * window + rho
    startp = jnp.maximum(start - window, rho)
    kmin = jnp.where(b > 0, 0, NBACK)
    return start, startp, kmin


def _rows(start, rate):
    if rate == 1:
        return pl.ds(pl.multiple_of(start, NBACK), NBACK)
    return pl.ds(start, NBACK, stride=rate)


def _band_bias():
    qs = lax.broadcasted_iota(jnp.int32, (2 * NBACK, 2 * NBACK), 0)
    kj = lax.broadcasted_iota(jnp.int32, (2 * NBACK, 2 * NBACK), 1)
    dist = (qs & (NBACK - 1)) + NBACK - kj
    both = (dist >= 0) & (dist <= NBACK)
    return jnp.where(both, 0.0, NEG_INF), jnp.where(both & (kj >= NBACK), 0.0, NEG_INF)


def _attn_fwd_block(q, kcat, vcat, first, lo, biases):
    vcat1 = jnp.concatenate([vcat, jnp.ones((2 * NBACK, PAIR), BF16)], axis=1)
    q2 = jnp.concatenate([jnp.where(lo, q, 0.0), jnp.where(lo, 0.0, q)], axis=0).astype(BF16)
    s = lax.dot_general(q2, kcat, (((1,), (1,)), ((), ())), preferred_element_type=F32)
    s = s + jnp.where(first, biases[1], biases[0])
    m = jnp.max(s, axis=-1, keepdims=True)
    ol = jnp.dot(jnp.exp(s - m).astype(BF16), vcat1, preferred_element_type=F32)
    o2 = ol[:, 0:PAIR] / ol[:, PAIR:]
    ls = m + jnp.log(ol[:, PAIR:])
    return jnp.where(lo, o2[0:NBACK], o2[NBACK:]), jnp.where(lo, ls[0:NBACK], ls[NBACK:])


def _attn_fwd(qn, kn, v, rate, name, dep=None):
    if rate > MAX_ROW_STRIDE:
        return _attn_fwd_gathered(qn, kn, v, rate, name, dep)

    def body(q_ref, k_ref, v_ref, *rest):
        o_ref, l_ref = rest[-2:]
        lo = _lo_mask((NBACK, PAIR))
        biases = _band_bias()

        def step(it, carry):
            start, startp, kmin = _res_index(it, rate)
            q = q_ref[_rows(start, rate), :] * (HEAD ** -0.5)
            kcat = jnp.concatenate([k_ref[_rows(startp, rate), :], k_ref[_rows(start, rate), :]], axis=0).astype(BF16)
            vcat = jnp.concatenate([v_ref[_rows(startp, rate), :], v_ref[_rows(start, rate), :]], axis=0).astype(BF16)
            o, ls = _attn_fwd_block(q, kcat, vcat, kmin != 0, lo, biases)
            o_ref[_rows(start, rate), :] = o
            l_ref[_rows(start, rate), :] = ls
            return carry

        lax.fori_loop(0, T // NBACK, step, 0, unroll=4)

    pm = pl.BlockSpec((None, T, PAIR), lambda p: (p, 0, 0))
    in_specs, args = [pm, pm, pm], [qn, kn, v]
    if dep is not None:
        in_specs.append(HBM_SPEC)
        args.append(dep)
    return pl.pallas_call(
        body, name=name, grid=(NPAIR,), in_specs=in_specs, out_specs=[pm, pm],
        out_shape=[_sds((NPAIR, T, PAIR), F32)] * 2, compiler_params=_cparams(1))(*args)


def _attn_fwd_gathered(qn, kn, v, rate, name, dep):
    n = T // rate
    nblk = n // NBACK

    def body(q_hbm, k_hbm, v_hbm, *rest):
        o_hbm, l_hbm, qb, kb, vb, ob, lb, in_sem, out_sem = rest[-9:]
        p = pl.program_id(0)
        slot = p % 2

        def loads(pair, s):
            return [pltpu.make_async_copy(x.at[pair, :, r, :], buf.at[s, r], in_sem.at[3 * s + a])
                    for a, (x, buf) in enumerate(((q_hbm, qb), (k_hbm, kb), (v_hbm, vb))) for r in range(rate)]

        def stores(pair, s):
            return [pltpu.make_async_copy(buf.at[s, r], x.at[pair, :, r, :], out_sem.at[2 * s + a])
                    for a, (x, buf) in enumerate(((o_hbm, ob), (l_hbm, lb))) for r in range(rate)]

        @pl.when(p == 0)
        def _():
            for c in loads(0, 0):
                c.start()

        @pl.when(p + 1 < NPAIR)
        def _():
            for c in loads(p + 1, 1 - slot):
                c.start()

        for c in loads(p, slot):
            c.wait()

        @pl.when(p >= 2)
        def _():
            for c in stores(p - 2, slot):
                c.wait()

        lo = _lo_mask((NBACK, PAIR))
        biases = _band_bias()

        def step(it, carry):
            b, r = it % nblk, it // nblk
            cur = pl.ds(pl.multiple_of(b * NBACK, NBACK), NBACK)
            prev = pl.ds(pl.multiple_of(jnp.maximum(b - 1, 0) * NBACK, NBACK), NBACK)
            q = qb[slot, r, cur, :] * (HEAD ** -0.5)
            kcat = jnp.concatenate([kb[slot, r, prev, :], kb[slot, r, cur, :]], axis=0).astype(BF16)
            vcat = jnp.concatenate([vb[slot, r, prev, :], vb[slot, r, cur, :]], axis=0).astype(BF16)
            o, ls = _attn_fwd_block(q, kcat, vcat, b == 0, lo, biases)
            ob[slot, r, cur, :] = o
            lb[slot, r, cur, :] = ls
            return carry

        lax.fori_loop(0, T // NBACK, step, 0, unroll=4)

        for c in stores(p, slot):
            c.start()

        @pl.when(p == NPAIR - 1)
        def _():
            for c in stores(p - 1, 1 - slot) + stores(p, slot):
                c.wait()

    by_residue = lambda a: a.reshape(NPAIR, n, rate, PAIR)
    in_specs, args = [HBM_SPEC] * 3, [by_residue(qn), by_residue(kn), by_residue(v)]
    if dep is not None:
        in_specs.append(HBM_SPEC)
        args.append(dep)
    o, l = pl.pallas_call(
        body, name=name, grid=(NPAIR,), in_specs=in_specs, out_specs=[HBM_SPEC] * 2,
        out_shape=[_sds((NPAIR, n, rate, PAIR), F32)] * 2,
        scratch_shapes=[pltpu.VMEM((2, rate, n, PAIR), F32)] * 5
        + [pltpu.SemaphoreType.DMA((6,)), pltpu.SemaphoreType.DMA((4,))],
        compiler_params=_cparams(1))(*args)
    return o.reshape(NPAIR, T, PAIR), l.reshape(NPAIR, T, PAIR)


def _merge_fwd(cat_ab, outs, lses, tm=512):
    def body(cat_in, o0, o1, o2, l0, l1, l2, cat_ref, lse_ref):
        del cat_in
        for p in range(NPAIR):
            a0, a1, a2 = l0[p], l1[p], l2[p]
            m = jnp.maximum(jnp.maximum(a0, a1), a2)
            w0, w1, w2 = jnp.exp(a0 - m), jnp.exp(a1 - m), jnp.exp(a2 - m)
            s = w0 + w1 + w2
            b = (w0 * o0[p] + w1 * o1[p] + w2 * o2[p]) / s
            cat_ref[:, p * PAIR:(p + 1) * PAIR] = b.astype(BF16)
            lse_ref[p] = m + jnp.log(s)

    pm = pl.BlockSpec((NPAIR, tm, PAIR), lambda i: (0, i, 0))
    return pl.pallas_call(
        body, name="merge_fwd", grid=(T // tm,),
        in_specs=[pl.BlockSpec(memory_space=pl.ANY)] + [pm] * 6,
        out_specs=[pl.BlockSpec((tm, 512), lambda i: (i, 1)), pm],
        out_shape=[_sds((T, D), BF16), _sds((NPAIR, T, PAIR), F32)],
        input_output_aliases={0: 0}, compiler_params=_cparams(1))(cat_ab, *outs, *lses)


def _d_cat_ab(dxb, w_ab_out, cat, dep, tm=512):
    def epilogue(d, rows, side, outs):
        (b_ref,), (da_ref, dbp_ref, e_ref) = side, outs
        da_ref[rows, :] = d[:, 0:512].astype(BF16)
        lo = _lo_mask((EPI_ROWS, PAIR))
        for p in range(NPAIR):
            db = d[:, 512 + p * PAIR:512 + (p + 1) * PAIR]
            b = b_ref[rows, p * PAIR:(p + 1) * PAIR].astype(F32)
            dbp_ref[p, rows, :] = db
            e_ref[p, rows, :] = _seg_mean(db * b, lo) * float(HEAD)

    pm = pl.BlockSpec((NPAIR, tm, PAIR), lambda i: (0, i, 0))
    return _mm_nt_rows(
        dxb, w_ab_out, "mm_d_cat_ab", epilogue, [cat], [pl.BlockSpec((tm, 512), lambda i: (i, 1))],
        [pl.BlockSpec((tm, 512), lambda i: (i, 0)), pm, pm],
        [_sds((T, 512), BF16), _sds((NPAIR, T, PAIR), F32), _sds((NPAIR, T, PAIR), F32)], dep=dep, tm=tm)


def _attn_bwd_block(q, db, ev, ls, kcat, vcat, first, lo, biases):
    scale = HEAD ** -0.5
    nt = (((1,), (1,)), ((), ()))
    tn = (((0,), (0,)), ((), ()))
    q = q * scale
    q2 = jnp.concatenate([jnp.where(lo, q, 0.0), jnp.where(lo, 0.0, q)], axis=0).astype(BF16)
    db2 = jnp.concatenate([jnp.where(lo, db, 0.0), jnp.where(lo, 0.0, db)], axis=0).astype(BF16)
    ls2 = jnp.concatenate([ls[:, 0:1], ls[:, HEAD:HEAD + 1]], axis=0)
    ev2 = jnp.concatenate([ev[:, 0:1], ev[:, HEAD:HEAD + 1]], axis=0)
    s = lax.dot_general(q2, kcat, nt, preferred_element_type=F32)
    pt = jnp.exp(s + jnp.where(first, biases[1], biases[0]) - ls2)
    dp = lax.dot_general(db2, vcat, nt, preferred_element_type=F32)
    ds = (pt * (dp - ev2)).astype(BF16)
    dq2 = jnp.dot(ds, kcat, preferred_element_type=F32) * scale
    dkc = lax.dot_general(ds, q2, tn, preferred_element_type=F32)
    dvc = lax.dot_general(pt.astype(BF16), db2, tn, preferred_element_type=F32)
    return jnp.where(lo, dq2[0:NBACK], dq2[NBACK:]), dkc, dvc


def _attn_bwd_loop(read, write, nblk):
    lo = _lo_mask((NBACK, PAIR))
    biases = _band_bias()

    def one(it, carry):
        dk_carry, dv_carry = carry
        rho = it // nblk
        b = it % nblk
        bp = jnp.maximum(b - 1, 0)
        kcat = jnp.concatenate([read(1, rho, bp), read(1, rho, b)], axis=0).astype(BF16)
        vcat = jnp.concatenate([read(2, rho, bp), read(2, rho, b)], axis=0).astype(BF16)
        dq, dkc, dvc = _attn_bwd_block(read(0, rho, b), read(3, rho, b), read(4, rho, b), read(5, rho, b), kcat, vcat,
                                       b == 0, lo, biases)
        write(0, rho, b, dq)
        write(1, rho, bp, dk_carry + dkc[0:NBACK])
        write(1, rho, b, dkc[NBACK:])
        write(2, rho, bp, dv_carry + dvc[0:NBACK])
        write(2, rho, b, dvc[NBACK:])
        return dkc[NBACK:], dvc[NBACK:]

    def step(i, carry):
        for u in range(ATTN_BWD_UNROLL):
            carry = one(i * ATTN_BWD_UNROLL + u, carry)
        return carry

    zero = jnp.zeros((NBACK, PAIR), F32)
    lax.fori_loop(0, T // NBACK // ATTN_BWD_UNROLL, step, (zero, zero))


def _attn_bwd(qn, kn, v, dbp, e, lse, rate, name):
    if rate > MAX_ROW_STRIDE:
        return _attn_bwd_gathered(qn, kn, v, dbp, e, lse, rate, name)
    window = NBACK * rate

    def body(*refs):
        rows = lambda rho, b: _rows(b * window + rho, rate)

        def write(j, rho, b, value):
            refs[6 + j][rows(rho, b), :] = value

        _attn_bwd_loop(lambda j, rho, b: refs[j][rows(rho, b), :], write, T // window)

    pm = pl.BlockSpec((None, T, PAIR), lambda p: (p, 0, 0))
    return pl.pallas_call(
        body, name=name, grid=(NPAIR,), in_specs=[pm] * 6, out_specs=[pm] * 3,
        out_shape=[_sds((NPAIR, T, PAIR), F32)] * 3, compiler_params=_cparams(1, 56))(qn, kn, v, dbp, e, lse)


def _attn_bwd_gathered(qn, kn, v, dbp, e, lse, rate, name):
    n = T // rate

    def body(*refs):
        ins, outs, in_bufs, out_bufs, (in_sem, out_sem) = refs[0:6], refs[6:9], refs[9:15], refs[15:18], refs[18:20]
        p = pl.program_id(0)
        slot = p % 2

        def loads(pair, s):
            return [pltpu.make_async_copy(x.at[pair, :, r, :], buf.at[s, r], in_sem.at[6 * s + a])
                    for a, (x, buf) in enumerate(zip(ins, in_bufs)) for r in range(rate)]

        def stores(pair, s):
            return [pltpu.make_async_copy(buf.at[s, r], x.at[pair, :, r, :], out_sem.at[3 * s + a])
                    for a, (x, buf) in enumerate(zip(outs, out_bufs)) for r in range(rate)]

        @pl.when(p == 0)
        def _():
            for c in loads(0, 0):
                c.start()

        @pl.when(p + 1 < NPAIR)
        def _():
            for c in loads(p + 1, 1 - slot):
                c.start()

        for c in loads(p, slot):
            c.wait()

        @pl.when(p >= 2)
        def _():
            for c in stores(p - 2, slot):
                c.wait()

        rows = lambda b: pl.ds(pl.multiple_of(b * NBACK, NBACK), NBACK)

        def write(j, rho, b, value):
            out_bufs[j][slot, rho, rows(b), :] = value

        _attn_bwd_loop(lambda j, rho, b: in_bufs[j][slot, rho, rows(b), :], write, n // NBACK)

        for c in stores(p, slot):
            c.start()

        @pl.when(p == NPAIR - 1)
        def _():
            for c in stores(p - 1, 1 - slot) + stores(p, slot):
                c.wait()

    by_residue = lambda a: a.reshape(NPAIR, n, rate, PAIR)
    res = pl.pallas_call(
        body, name=name, grid=(NPAIR,), in_specs=[HBM_SPEC] * 6, out_specs=[HBM_SPEC] * 3,
        out_shape=[_sds((NPAIR, n, rate, PAIR), F32)] * 3,
        scratch_shapes=[pltpu.VMEM((2, rate, n, PAIR), F32)] * 9
        + [pltpu.SemaphoreType.DMA((12,)), pltpu.SemaphoreType.DMA((6,))],
        compiler_params=_cparams(1, 56))(*[by_residue(a) for a in (qn, kn, v, dbp, e, lse)])
    return [r.reshape(NPAIR, T, PAIR) for r in res]


def _ab_in_bwd(pab, dcat, sgu_g, sgu_b, sgu_w, sgu_bias3, qg, kg, tabs, dqkv, rinvs, tm=256):
    def body(p_hbm, dcat_ref, g_ref, b_ref, w_ref, bias_ref, qg_ref, kg_ref, c_ref, s1_ref, s2_ref, *rest):
        dq_hbm, rinv_refs = rest[0:9], rest[9:15]
        o_ref, dwm_ref, dbias_ref, dsg_ref, dsb_ref, dgain_ref = rest[15:21]
        rings = rest[21:]
        i = pl.program_id(0)
        rows_of = lambda step: pl.ds(pl.multiple_of(step * tm, tm), tm)
        p_ref = _ring_tile(lambda step: p_hbm.at[rows_of(step), :], rings[0], rings[1], i, T // tm)
        dq_refs = [_ring_tile(lambda step, x=x: x.at[:, rows_of(step), :], rings[2 + 2 * n], rings[3 + 2 * n], i, T // tm)
                   for n, x in enumerate(dq_hbm)]

        @pl.when(i == 0)
        def _():
            dwm_ref[...] = jnp.zeros_like(dwm_ref)
            dbias_ref[...] = jnp.zeros_like(dbias_ref)
            dsg_ref[...] = jnp.zeros_like(dsg_ref)
            dsb_ref[...] = jnp.zeros_like(dsb_ref)
            dgain_ref[...] = jnp.zeros_like(dgain_ref)

        zu = p_ref[:, 0:512].astype(F32)
        zv = p_ref[:, 512:1024].astype(F32)
        u = _gelu(zu)
        v = _gelu(zv)
        mu = jnp.mean(v, axis=-1, keepdims=True)
        vc = v - mu
        rstd = lax.rsqrt(jnp.mean(vc * vc, axis=-1, keepdims=True) + EPS)
        xhat = vc * rstd
        vn = (xhat * g_ref[...] + b_ref[...]).astype(BF16)
        da = dcat_ref[...].astype(F32)
        tri = _tril_mask()
        du_parts = [[None] * 4 for _ in range(tm // 128)]
        dvn_parts = [[None] * 4 for _ in range(tm // 128)]
        for gi in range(4):
            wg = jnp.where(tri, w_ref[gi], 0.0).astype(BF16)
            bg = bias_ref[gi]
            for c in range(tm // 128):
                rs, cs = slice(c * 128, (c + 1) * 128), slice(gi * 128, (gi + 1) * 128)
                vblk = vn[rs, cs]
                mixed = jnp.dot(wg, vblk, preferred_element_type=F32) + bg
                dab = da[rs, cs]
                du_parts[c][gi] = dab * mixed
                dmixed = dab * u[rs, cs]
                dmb = dmixed.astype(BF16)
                dvn_parts[c][gi] = lax.dot_general(wg, dmb, (((0,), (0,)), ((), ())), preferred_element_type=F32)
                dwm = lax.dot_general(dmb, vblk, (((1,), (1,)), ((), ())), preferred_element_type=F32)
                dwm_ref[gi] += jnp.where(tri, dwm, 0.0)
                dbias_ref[gi] += dmixed
        du = jnp.concatenate([jnp.concatenate(r, axis=1) for r in du_parts], axis=0)
        dvn = jnp.concatenate([jnp.concatenate(r, axis=1) for r in dvn_parts], axis=0)
        dsg_ref[...] += jnp.sum(dvn * xhat, axis=0, keepdims=True)
        dsb_ref[...] += jnp.sum(dvn, axis=0, keepdims=True)
        dxh = dvn * g_ref[...]
        dv = rstd * (dxh - jnp.mean(dxh, axis=-1, keepdims=True)
                     - xhat * jnp.mean(dxh * xhat, axis=-1, keepdims=True))
        o_ref[:, 0:512] = (du * _gelu_grad(zu)).astype(BF16)
        o_ref[:, 512:1024] = (dv * _gelu_grad(zv)).astype(BF16)

        blocks = _head_blocks()
        c, s1, s2 = c_ref[...], s1_ref[...], s2_ref[...]
        for g in range(3):
            dq_ref, dk_ref, dv_ref = dq_refs[3 * g:3 * g + 3]
            for p in range(NPAIR):
                for which, gains, src in ((0, qg_ref, dq_ref), (1, kg_ref, dk_ref)):
                    col = (2 + 3 * which + g) * 512 + p * PAIR
                    xr = p_ref[:, col:col + PAIR].astype(F32)
                    rinv = rinv_refs[2 * g + which][p].astype(F32)
                    xh = xr * rinv
                    dn = _rope_t(src[p], c, s1, s2)
                    row = 2 * g + which
                    dgain_ref[row:row + 1, :] += jnp.sum(dn * xh, axis=0, keepdims=True)
                    dxh2 = dn * gains[g:g + 1, :]
                    dx = rinv * (dxh2 - xh * _seg_mean_mxu(dxh2 * xh, blocks))
                    o_ref[:, col:col + PAIR] = dx.astype(BF16)
                col = (8 + g) * 512 + p * PAIR
                o_ref[:, col:col + PAIR] = dv_ref[p].astype(BF16)

    pm = pl.BlockSpec((NPAIR, tm, PAIR), lambda i: (0, i, 0))
    tab = pl.BlockSpec((tm, PAIR), lambda i: (i, 0))
    gain = pl.BlockSpec((3, PAIR), lambda i: (0, 0))
    vec = pl.BlockSpec((1, 512), lambda i: (0, 0))
    full = pl.BlockSpec((tm, AB_IN), lambda i: (i, 0))
    w4 = pl.BlockSpec((4, 128, 128), lambda i: (0, 0, 0))
    return pl.pallas_call(
        body, name="ab_in_bwd", grid=(T // tm,),
        in_specs=[HBM_SPEC, pl.BlockSpec((tm, 512), lambda i: (i, 0)), vec, vec, w4,
                  pl.BlockSpec((4, 128, 1), lambda i: (0, 0, 0)), gain, gain, tab, tab, tab]
        + [HBM_SPEC] * 9 + [pm] * 6,
        out_specs=[full, w4, w4, vec, vec, pl.BlockSpec((8, PAIR), lambda i: (0, 0))],
        out_shape=[_sds((T, AB_IN), BF16), _sds((4, 128, 128), F32), _sds((4, 128, 128), F32),
                   _sds((1, 512), F32), _sds((1, 512), F32), _sds((8, PAIR), F32)],
        scratch_shapes=_ring_scratch((tm, AB_IN), BF16) + 9 * _ring_scratch((NPAIR, tm, PAIR), F32),
        compiler_params=_cparams(1))(pab, dcat, sgu_g, sgu_b, sgu_w, sgu_bias3, qg, kg, *tabs, *dqkv, *rinvs)


def _ln_stats(x):
    mu = jnp.mean(x, axis=-1, keepdims=True)
    xc = x - mu
    rstd = lax.rsqrt(jnp.mean(xc * xc, axis=-1, keepdims=True) + EPS)
    return xc * rstd, rstd


CONV_RC = 64


def _shifted_copies(src, dst, tm):
    dst[0] = src[...]
    for b in range(1, 8):
        dst[b, 0:tm + HALO - 8, :] = src[pl.ds(b, tm + HALO - 8), :]


def _offsets_by_phase(first):
    groups = {}
    for o in range(first, first + CONV_C_TAPS):
        groups.setdefault(o % 8, []).append(o)
    return sorted(groups.items())


def _window(shifted, b8, base, offsets, lanes):
    rows = 8 * (max(offsets) // 8) + CONV_RC
    return shifted[b8, pl.ds(base, rows), lanes].reshape(rows // 8, 8, 128)


def _cd_fwd(pcd, cw, cb, lg, lb, dw, tm=512):
    per = tm // HALO

    def body(p_ref, h_ref, cw_ref, cb_ref, lg_ref, lb_ref, dw_ref, cat_ref, c0_ref, c1_ref, dd_ref, y_ref,
             buf, buf2, sb):
        i = pl.program_id(0)
        live = jnp.where(i > 0, 1.0, 0.0)
        a = p_ref[:, 0:512].astype(F32)
        gt = p_ref[:, 512:1024].astype(F32)
        gb = p_ref[:, 1024:1536].astype(F32)
        gc = p_ref[:, 1536:2048].astype(F32)
        hv = p_ref[:, 2048:2560].astype(F32)
        c0 = a * _sigmoid(gt)
        dd = gc * hv
        buf[0:HALO, :] = h_ref[:, 0:512].astype(F32) * _sigmoid(h_ref[:, 512:1024].astype(F32)) * live
        buf[HALO:, :] = c0
        buf2[0:HALO, :] = h_ref[:, 1536:2048].astype(F32) * h_ref[:, 2048:2560].astype(F32) * live
        buf2[HALO:, :] = dd
        c0_ref[...] = c0.astype(BF16)
        dd_ref[...] = dd.astype(BF16)
        _shifted_copies(buf, sb, tm)

        def conv_rows(r, carry):
            base = pl.multiple_of(r * CONV_RC, CONV_RC)
            for c in range(4):
                lanes = slice(c * 128, (c + 1) * 128)
                acc = jnp.broadcast_to(cb_ref[:, lanes], (CONV_RC // 8, 8, 128))
                for b8, offsets in _offsets_by_phase(HALO - (CONV_C_TAPS - 1)):
                    win = _window(sb, b8, base, offsets, lanes)
                    for o in offsets:
                        j = o - (HALO - (CONV_C_TAPS - 1))
                        acc = acc + cw_ref[8 * j:8 * j + 8, lanes] * win[o // 8:o // 8 + CONV_RC // 8]
                c1_ref[pl.ds(base, CONV_RC), lanes] = acc.reshape(CONV_RC, 128)
            return carry

        lax.fori_loop(0, tm // CONV_RC, conv_rows, 0)
        xhat, _ = _ln_stats(c1_ref[...])
        c2 = xhat * lg_ref[...] + lb_ref[...]
        y = jnp.zeros((tm, 512), F32)
        for j in range(CONV_D_TAPS):
            y = y + dw_ref[j:j + 1, :] * buf2[pl.ds(HALO - (CONV_D_TAPS - 1) + j, tm), :]
        cat_ref[:, 0:512] = (c2 * _sigmoid(c2)).astype(BF16)
        cat_ref[:, 512:1024] = (gb * y).astype(BF16)
        y_ref[...] = y.astype(BF16)

    half = pl.BlockSpec((tm, 512), lambda i: (i, 0))
    vec = pl.BlockSpec((1, 512), lambda i: (0, 0))
    return pl.pallas_call(
        body, name="cd_fwd", grid=(T // tm,),
        in_specs=[pl.BlockSpec((tm, CD_IN), lambda i: (i, 0)),
                  pl.BlockSpec((HALO, CD_IN), lambda i: (jnp.maximum(i * per - 1, 0), 0)),
                  pl.BlockSpec((8 * 32, 512), lambda i: (0, 0)), vec, vec, vec, pl.BlockSpec((8, 512), lambda i: (0, 0))],
        out_specs=[pl.BlockSpec((tm, D), lambda i: (i, 0)), half, half, half, half],
        out_shape=[_sds((T, D), BF16), _sds((T, 512), BF16), _sds((T, 512), F32), _sds((T, 512), BF16),
                   _sds((T, 512), BF16)],
        scratch_shapes=[pltpu.VMEM((HALO + tm, 512), F32), pltpu.VMEM((HALO + tm, 512), F32),
                        pltpu.VMEM((8, HALO + tm, 512), F32)],
        compiler_params=_cparams(1))(pcd, pcd, cw, cb, lg, lb, dw)


def _d_cat_cd(dxb, w_cd_out, c1, pcd, y, lg, lb, dep, tm=512):
    def epilogue(d, rows, side, outs):
        c1_ref, gb_ref, y_ref, lg_ref, lb_ref = side
        dc1_ref, dy3_ref, dgb_ref, dlg_ref, dlb_ref, dcb_ref = outs
        dc, ddo = d[:, 0:512], d[:, 512:1024]
        xhat, rstd = _ln_stats(c1_ref[rows, :])
        c2 = xhat * lg_ref[...] + lb_ref[...]
        sg = _sigmoid(c2)
        dc2 = dc * sg * (1.0 + c2 * (1.0 - sg))
        dlg_ref[...] += jnp.sum(dc2 * xhat, axis=0, keepdims=True)
        dlb_ref[...] += jnp.sum(dc2, axis=0, keepdims=True)
        dxh = dc2 * lg_ref[...]
        dc1 = rstd * (dxh - jnp.mean(dxh, axis=-1, keepdims=True)
                      - xhat * jnp.mean(dxh * xhat, axis=-1, keepdims=True))
        dcb_ref[...] += jnp.sum(dc1, axis=0, keepdims=True)
        dc1_ref[rows, :] = dc1
        dgb_ref[rows, :] = (ddo * y_ref[rows, :].astype(F32)).astype(BF16)
        dy3_ref[rows, :] = ddo * gb_ref[rows, :].astype(F32)

    half = pl.BlockSpec((tm, 512), lambda i: (i, 0))
    vec = pl.BlockSpec((1, 512), lambda i: (0, 0))
    return _mm_nt_rows(
        dxb, w_cd_out, "mm_d_cat_cd", epilogue, [c1, pcd, y, lg, lb],
        [half, pl.BlockSpec((tm, 512), lambda i: (i, 2)), half, vec, vec], [half, half, half, vec, vec, vec],
        [_sds((T, 512), F32), _sds((T, 512), F32), _sds((T, 512), BF16),
         _sds((1, 512), F32), _sds((1, 512), F32), _sds((1, 512), F32)], sums=(3, 4, 5), dep=dep, tm=tm)


def _cd_bwd_conv(pcd, dc1, dy3, c0, dd, dgb, cw8, dw, tm=256):
    per = tm // HALO
    nblk = T // tm
    last32 = T // HALO - 1

    def body(p_ref, dc1_ref, dc1n_ref, dy3_ref, dy3n_ref, c0_ref, dd_ref, dgb_ref, cw_ref, dw_ref,
             o_ref, dcw_ref, ddw_ref, dbuf, d3buf, sd, dc0_buf):
        i = pl.program_id(0)
        has_next = jnp.where(i < nblk - 1, 1.0, 0.0)

        @pl.when(i == 0)
        def _():
            dcw_ref[...] = jnp.zeros_like(dcw_ref)
            ddw_ref[...] = jnp.zeros_like(ddw_ref)

        dbuf[0:tm, :] = dc1_ref[...]
        dbuf[tm:, :] = dc1n_ref[...] * has_next
        d3buf[0:tm, :] = dy3_ref[...]
        d3buf[tm:, :] = dy3n_ref[...] * has_next
        _shifted_copies(dbuf, sd, tm)
        n_tiles = tm // CONV_RC

        phases = _offsets_by_phase(0)

        def dc0_rows(r, carry):
            base = pl.multiple_of(r * CONV_RC, CONV_RC)
            for c in range(4):
                lanes = slice(c * 128, (c + 1) * 128)
                acc = jnp.zeros((CONV_RC // 8, 8, 128), F32)
                for b8, offsets in phases:
                    win = _window(sd, b8, base, offsets, lanes)
                    for o in offsets:
                        j = CONV_C_TAPS - 1 - o
                        acc = acc + cw_ref[8 * j:8 * j + 8, lanes] * win[o // 8:o // 8 + CONV_RC // 8]
                dc0_buf[pl.ds(base, CONV_RC), lanes] = acc.reshape(CONV_RC, 128)
            return carry

        lax.fori_loop(0, n_tiles, dc0_rows, 0)

        for c in range(4):
            lanes = slice(c * 128, (c + 1) * 128)
            for b8, offsets in phases:
                def dw_rows(r, accs, lanes=lanes, b8=b8, offsets=offsets):
                    base = pl.multiple_of(r * CONV_RC, CONV_RC)
                    xin = c0_ref[pl.ds(base, CONV_RC), lanes].astype(F32).reshape(CONV_RC // 8, 8, 128)
                    win = _window(sd, b8, base, offsets, lanes)
                    return tuple(acc + jnp.sum(xin * win[o // 8:o // 8 + CONV_RC // 8], axis=0)
                                 for acc, o in zip(accs, offsets))

                accs = lax.fori_loop(0, n_tiles, dw_rows, tuple(jnp.zeros((8, 128), F32) for _ in offsets))
                for acc, o in zip(accs, offsets):
                    j = CONV_C_TAPS - 1 - o
                    dcw_ref[j:j + 1, lanes] += jnp.sum(acc, axis=0, keepdims=True)

        dc0 = dc0_buf[...]
        ddin = dd_ref[...].astype(F32)
        ddd = jnp.zeros((tm, 512), F32)
        for j in range(CONV_D_TAPS):
            dy_shift = d3buf[pl.ds(CONV_D_TAPS - 1 - j, tm), :]
            ddd = ddd + dw_ref[j:j + 1, :] * dy_shift
            ddw_ref[j:j + 1, :] += jnp.sum(ddin * dy_shift, axis=0, keepdims=True)

        a = p_ref[:, 0:512].astype(F32)
        gt = p_ref[:, 512:1024].astype(F32)
        gc = p_ref[:, 1536:2048].astype(F32)
        hv = p_ref[:, 2048:2560].astype(F32)
        sg = _sigmoid(gt)
        o_ref[:, 0:512] = (dc0 * sg).astype(BF16)
        o_ref[:, 512:1024] = (dc0 * a * sg * (1.0 - sg)).astype(BF16)
        o_ref[:, 1024:1536] = dgb_ref[...]
        o_ref[:, 1536:2048] = (ddd * hv).astype(BF16)
        o_ref[:, 2048:2560] = (ddd * gc).astype(BF16)

    half = pl.BlockSpec((tm, 512), lambda i: (i, 0))
    nxt = pl.BlockSpec((HALO, 512), lambda i: (jnp.minimum((i + 1) * per, last32), 0))
    full = pl.BlockSpec((tm, CD_IN), lambda i: (i, 0))
    return pl.pallas_call(
        body, name="cd_bwd_conv", grid=(nblk,),
        in_specs=[full, half, nxt, half, nxt, half, half, half,
                  pl.BlockSpec((8 * 32, 512), lambda i: (0, 0)), pl.BlockSpec((8, 512), lambda i: (0, 0))],
        out_specs=[full, pl.BlockSpec((32, 512), lambda i: (0, 0)), pl.BlockSpec((8, 512), lambda i: (0, 0))],
        out_shape=[_sds((T, CD_IN), BF16), _sds((32, 512), F32), _sds((8, 512), F32)],
        scratch_shapes=[pltpu.VMEM((tm + HALO, 512), F32), pltpu.VMEM((tm + HALO, 512), F32),
                        pltpu.VMEM((8, tm + HALO, 512), F32), pltpu.VMEM((tm, 512), F32)],
        compiler_params=_cparams(1))(pcd, dc1, dc1, dy3, dy3, c0, dd, dgb, cw8, dw)


def _local_step(x, tgt, W, fetch=None, on_grad=None):
    W = dict(W)
    if fetch is None:
        fetch = lambda stage, after: {}
    if on_grad is None:
        on_grad = lambda key, arr: None
    tabs = _rope_tables()
    qg = jnp.tile(W["q_norm_g"], (1, 2))
    kg = jnp.tile(W["k_norm_g"], (1, 2))
    bias3 = W["sgu_bias"].reshape(4, 128, 1)
    G = {}

    h0 = _rms_fwd(x, W["ab_norm_g"], "rms_fwd_ab", dep=W.get("dep_first"))
    W.update(fetch("ab_in", h0))
    pab = _mm_nt(h0, W["wt_ab_in"], "mm_ab_in", dep=W.get("dep0"))
    cat_ab = _mix_a_fwd(pab, W["sgu_norm_g"], W["sgu_norm_b"], W["sgu_w"], bias3)
    qkv, rinvs = _prep_fwd(pab, qg, kg, tabs)
    outs, lses = [], []
    for g, rate in enumerate(DIL_RATES):
        o, l = _attn_fwd(qkv[3 * g], qkv[3 * g + 1], qkv[3 * g + 2], rate, f"attn_fwd_{g}", dep=W.get(f"dep_attn{g}"))
        outs.append(o)
        lses.append(l)
        W.update(fetch(f"attn{g}", o))
    cat_ab, lse = _merge_fwd(cat_ab, outs, lses)
    W.update(fetch("ab_out", lse))
    x1, h1 = _mm_nn(cat_ab, W["w_ab_out"], "mm_ab_out", mode="rms", resid=x, gain=W["ffn_norm_g"][0:1])
    pf0, act0 = _ffn_in(h1, W["wt_ffn_in0"], "ffn_in0")
    W.update(fetch("ffn_down0", act0))
    x2, h2 = _mm_nn(act0, W["w_ffn_down0"], "mm_ffn_down0", mode="rms", resid=x1, gain=W["cd_norm_g"],
                    dep=W.get("dep_down0"))
    W.update(fetch("cd_in", h2))
    pcd = _mm_nt(h2, W["wt_cd_in"], "mm_cd_in")
    cw8 = jnp.repeat(W["conv_c_w32"], 8, axis=0)
    cat_cd, c0, c1, dd, yv = _cd_fwd(pcd, cw8, W["conv_c_b"], W["c_ln_g"], W["c_ln_b"], W["conv_d_w8"])
    x3, h3 = _mm_nn(cat_cd, W["w_cd_out"], "mm_cd_out", mode="rms", resid=x2, gain=W["ffn_norm_g"][1:2])
    pf1, act1 = _ffn_in(h3, W["wt_ffn_in1"], "ffn_in1")
    dy, dyb, loss_cols = _mm_nn(act1, W["w_ffn_down1"], "mm_ffn_down1", mode="loss", resid=x3, tgt=tgt)

    def ffn_bwd(xin, h, pf, act, dres, dresb, layer):
        G[f"w_ffn_down{layer}"] = _mm_tn(act, dresb, f"mm_g_ffn_down{layer}")
        dep = on_grad(f"w_ffn_down{layer}", G[f"w_ffn_down{layer}"])
        dpf = _ffn_dact(dresb, W[f"w_ffn_down{layer}"], pf, f"ffn_dact{layer}", dep=dep)
        G[f"wt_ffn_in{layer}"] = _mm_tn(dpf, h, f"mm_g_ffn_in{layer}")
        dep = on_grad(f"wt_ffn_in{layer}", G[f"wt_ffn_in{layer}"])
        dx, dxb, G[f"ffn_norm_g{layer}"] = _mm_dh_rms_bwd(
            dpf, W[f"wt_ffn_in{layer}"], xin, W["ffn_norm_g"][layer:layer + 1], dres, f"mm_d_h_ffn{layer}", dep=dep)
        return dx, dxb

    dx3, dx3b = ffn_bwd(x3, h3, pf1, act1, dy, dyb, 1)

    G["w_cd_out"] = _mm_tn(cat_cd, dx3b, "mm_g_cd_out")
    dep = on_grad("w_cd_out", G["w_cd_out"])
    dc1, dy3, dgb, G["c_ln_g"], G["c_ln_b"], G["conv_c_b"] = _d_cat_cd(
        dx3b, W["w_cd_out"], c1, pcd, yv, W["c_ln_g"], W["c_ln_b"], dep)
    dpcd, G["conv_c_w32"], G["conv_d_w8"] = _cd_bwd_conv(pcd, dc1, dy3, c0, dd, dgb, cw8, W["conv_d_w8"])
    G["wt_cd_in"] = _mm_tn(dpcd, h2, "mm_g_cd_in")
    dep = on_grad("wt_cd_in", G["wt_cd_in"])
    dx2, dx2b, G["cd_norm_g"] = _mm_dh_rms_bwd(dpcd, W["wt_cd_in"], x2, W["cd_norm_g"], dx3, "mm_d_h_cd", dep=dep)

    dx1, dx1b = ffn_bwd(x1, h1, pf0, act0, dx2, dx2b, 0)

    G["w_ab_out"] = _mm_tn(cat_ab, dx1b, "mm_g_ab_out")
    dep = on_grad("w_ab_out", G["w_ab_out"])
    dcat_a, dbp, e = _d_cat_ab(dx1b, W["w_ab_out"], cat_ab, dep)
    dqkv = []
    for g, rate in enumerate(DIL_RATES):
        dqkv += _attn_bwd(qkv[3 * g], qkv[3 * g + 1], qkv[3 * g + 2], dbp, e, lse, rate, f"attn_bwd_{g}")
    dpab, G["sgu_w"], dbias_part, G["sgu_norm_g"], G["sgu_norm_b"], dgain = _ab_in_bwd(
        pab, dcat_a, W["sgu_norm_g"], W["sgu_norm_b"], W["sgu_w"], bias3, qg, kg, tabs, dqkv, rinvs)
    G["sgu_bias"] = jnp.sum(dbias_part, axis=-1)
    dgain = dgain[0:6, 0:HEAD] + dgain[0:6, HEAD:PAIR]
    G["q_norm_g"] = dgain[0::2]
    G["k_norm_g"] = dgain[1::2]
    G["loss_cols"] = loss_cols
    dep = on_grad("small", G)
    G["wt_ab_in"] = _mm_tn(dpab, h0, "mm_g_ab_in", dep=dep)
    dep = on_grad("wt_ab_in", G["wt_ab_in"])
    grad_x, G["ab_norm_g"] = _mm_dh_rms_bwd(dpab, W["wt_ab_in"], x, W["ab_norm_g"], dx1, "mm_d_h_ab", dep=dep,
                                            bf16_copy=False)
    return loss_cols, grad_x, G


def _my_place():
    return lax.axis_index("x"), lax.axis_index("y"), lax.axis_index("c")


def _dev_index(px, py, pc):
    return 4 * px + 2 * py + pc


def _flip(place, k):
    x, y, c = place
    return (1 - x if k & 4 else x, 1 - y if k & 2 else y, 1 - c if k & 1 else c)


def _landing(shape, dtype, own):
    buf = lax.empty(shape, dtype)
    for lead, part in own:
        buf = lax.dynamic_update_slice(buf, part.reshape((1,) * len(lead) + part.shape),
                                       tuple(lead) + (0,) * part.ndim)
    return buf


HBM_ONLY = pl.BlockSpec(memory_space=pltpu.HBM)
SEM_SPEC = pl.BlockSpec(memory_space=pltpu.SEMAPHORE)
IN_FLIGHT = pltpu.CompilerParams(has_side_effects=pltpu.SideEffectType.DATAFLOW_SIDE_EFFECTING)


def _in_hbm(a):
    return pltpu.with_memory_space_constraint(a, pltpu.HBM)


def _exchange_start(name, srcs, lands, items, dep=None):
    ns, nl, ni = len(srcs), len(lands), len(items)

    def body(*refs):
        S, L = refs[0:ns], refs[ns:ns + nl]
        first_out = ns + nl + (0 if dep is None else 1)
        send_sems, recv_sems, token = refs[first_out], refs[first_out + 1], refs[-1]
        me = _my_place()
        mi = _dev_index(*me)
        for i, (src, dst) in enumerate(items):
            for k in range(1, NDEV):
                peer = _flip(me, k)
                pltpu.make_async_remote_copy(
                    src_ref=src(S, _dev_index(*peer)), dst_ref=dst(L, mi), send_sem=send_sems.at[7 * i + k - 1],
                    recv_sem=recv_sems.at[7 * i + k - 1], device_id=peer, device_id_type=MESH).start()
        token[...] = jnp.zeros_like(token)

    thru = [pltpu.HBM(a.shape, a.dtype) for a in list(srcs) + list(lands)]
    args = [_in_hbm(a) for a in srcs] + [_in_hbm(a) for a in lands]
    in_specs = [HBM_ONLY] * (ns + nl)
    if dep is not None:
        args.append(dep)
        in_specs.append(HBM_SPEC)
    outs = pl.pallas_call(
        body, name=name, in_specs=in_specs,
        out_shape=(pltpu.SemaphoreType.DMA((7 * ni,)), pltpu.SemaphoreType.DMA((7 * ni,)), *thru, _sds((8, 128), F32)),
        out_specs=(SEM_SPEC, SEM_SPEC, *[HBM_ONLY] * (ns + nl), pl.BlockSpec(memory_space=pltpu.VMEM)),
        input_output_aliases={j: 2 + j for j in range(ns + nl)}, compiler_params=IN_FLIGHT)(*args)
    return dict(send=outs[0], recv=outs[1], srcs=list(outs[2:2 + ns]), lands=list(outs[2 + ns:2 + ns + nl]),
                token=outs[-1], items=items)


def _exchange_wait(name, states, after):
    after = list(after) if isinstance(after, (list, tuple)) else [after]
    counts = [(len(st["srcs"]), len(st["lands"]), len(st["items"])) for st in states]
    n_arrays = sum(c[0] + c[1] for c in counts)

    def body(*refs):
        me = _my_place()
        mi = _dev_index(*me)
        pos = 0
        sem_pos = n_arrays
        for st, (ns, nl, ni) in zip(states, counts):
            S, L = refs[pos:pos + ns], refs[pos + ns:pos + ns + nl]
            send_sems, recv_sems = refs[sem_pos], refs[sem_pos + 1]
            pos += ns + nl
            sem_pos += 2
            for i, (src, dst) in enumerate(st["items"]):
                for k in range(1, NDEV):
                    cp = pltpu.make_async_remote_copy(
                        src_ref=src(S, mi), dst_ref=dst(L, mi), send_sem=send_sems.at[7 * i + k - 1],
                        recv_sem=recv_sems.at[7 * i + k - 1], device_id=me, device_id_type=MESH)
                    cp.wait_send()
                    cp.wait_recv()

    arrays, sems = [], []
    for st in states:
        arrays += st["srcs"] + st["lands"]
        sems += [st["send"], st["recv"]]
    outs = pl.pallas_call(
        body, name=name, in_specs=[HBM_ONLY] * n_arrays + [SEM_SPEC] * len(sems) + [HBM_SPEC] * len(after),
        out_shape=tuple(pltpu.HBM(a.shape, a.dtype) for a in arrays), out_specs=tuple([HBM_ONLY] * n_arrays),
        input_output_aliases={j: j for j in range(n_arrays)}, compiler_params=IN_FLIGHT)(*arrays, *sems, *after)
    lands, pos = [], 0
    for ns, nl, _ in counts:
        lands.append(list(outs[pos + ns:pos + ns + nl]))
        pos += ns + nl
    return lands


def _place_and_neighbours():
    x, y, c = _my_place()
    return (x, y, c), (x, y, 1 - c), [(1 - x, y), (x, 1 - y), (1 - x, 1 - y)]


def _gather_start(name, srcs, lands, items, dep=None):
    ns, nl, ni = len(srcs), len(lands), len(items)

    def body(*refs):
        S, L = refs[0:ns], refs[ns:ns + nl]
        first_out = ns + nl + (0 if dep is None else 1)
        send_sems, recv_sems, token = refs[first_out], refs[first_out + 1], refs[-1]
        me, sib, chips = _place_and_neighbours()
        mi = _dev_index(*me)
        for i, (src, dst) in enumerate(items):
            for k, to in enumerate([sib] + [(*chip, me[2]) for chip in chips]):
                pltpu.make_async_remote_copy(
                    src_ref=src(S), dst_ref=dst(L, mi), send_sem=send_sems.at[4 * i + k],
                    recv_sem=recv_sems.at[4 * i + k], device_id=to, device_id_type=MESH).start()
        token[...] = jnp.zeros_like(token)

    thru = [pltpu.HBM(a.shape, a.dtype) for a in list(srcs) + list(lands)]
    args = [_in_hbm(a) for a in srcs] + [_in_hbm(a) for a in lands]
    in_specs = [HBM_ONLY] * (ns + nl)
    if dep is not None:
        args.append(dep)
        in_specs.append(HBM_SPEC)
    outs = pl.pallas_call(
        body, name=name, in_specs=in_specs,
        out_shape=(pltpu.SemaphoreType.DMA((4 * ni,)), pltpu.SemaphoreType.DMA((4 * ni,)), *thru, _sds((8, 128), F32)),
        out_specs=(SEM_SPEC, SEM_SPEC, *[HBM_ONLY] * (ns + nl), pl.BlockSpec(memory_space=pltpu.VMEM)),
        input_output_aliases={j: 2 + j for j in range(ns + nl)}, compiler_params=IN_FLIGHT)(*args)
    return dict(send=outs[0], recv=outs[1], srcs=list(outs[2:2 + ns]), lands=list(outs[2 + ns:2 + ns + nl]),
                token=outs[-1], items=items)


def _gather_forward(name, st, after):
    nl, ni = len(st["lands"]), len(st["items"])

    def body(*refs):
        L, recv_sems = refs[0:nl], refs[nl]
        fwd_send, fwd_recv, token = refs[-3:]
        me, sib, chips = _place_and_neighbours()
        for i, (_, dst) in enumerate(st["items"]):
            for j, chip in enumerate(chips):
                blk = dst(L, _dev_index(*chip, me[2]))
                pltpu.make_async_remote_copy(
                    src_ref=blk, dst_ref=blk, send_sem=fwd_send.at[3 * i + j], recv_sem=recv_sems.at[4 * i + 1 + j],
                    device_id=me, device_id_type=MESH).wait_recv()
                pltpu.make_async_remote_copy(
                    src_ref=blk, dst_ref=blk, send_sem=fwd_send.at[3 * i + j], recv_sem=fwd_recv.at[3 * i + j],
                    device_id=sib, device_id_type=MESH).start()
        token[...] = jnp.zeros_like(token)

    after = list(after) if isinstance(after, (list, tuple)) else [after]
    outs = pl.pallas_call(
        body, name=name, in_specs=[HBM_ONLY] * nl + [SEM_SPEC] + [HBM_SPEC] * len(after),
        out_shape=(*[pltpu.HBM(a.shape, a.dtype) for a in st["lands"]], pltpu.SemaphoreType.DMA((3 * ni,)),
                   pltpu.SemaphoreType.DMA((3 * ni,)), _sds((8, 128), F32)),
        out_specs=(*[HBM_ONLY] * nl, SEM_SPEC, SEM_SPEC, pl.BlockSpec(memory_space=pltpu.VMEM)),
        input_output_aliases={j: j for j in range(nl)}, compiler_params=IN_FLIGHT)(*st["lands"], st["recv"], *after)
    return dict(st, lands=list(outs[0:nl]), fwd_send=outs[nl], fwd_recv=outs[nl + 1], token=outs[-1])


def _gather_wait(name, st, after):
    ns, nl, ni = len(st["srcs"]), len(st["lands"]), len(st["items"])

    def body(*refs):
        S, L = refs[0:ns], refs[ns:ns + nl]
        send_sems, recv_sems, fwd_send, fwd_recv = refs[ns + nl:ns + nl + 4]
        me, sib, chips = _place_and_neighbours()
        mi = _dev_index(*me)
        for i, (src, dst) in enumerate(st["items"]):
            mine = dst(L, mi)
            for k in range(4):
                pltpu.make_async_remote_copy(
                    src_ref=src(S), dst_ref=mine, send_sem=send_sems.at[4 * i + k], recv_sem=recv_sems.at[4 * i + k],
                    device_id=me, device_id_type=MESH).wait_send()
            pltpu.make_async_remote_copy(
                src_ref=src(S), dst_ref=mine, send_sem=send_sems.at[4 * i], recv_sem=recv_sems.at[4 * i],
                device_id=me, device_id_type=MESH).wait_recv()
            for j in range(3):
                cp = pltpu.make_async_remote_copy(
                    src_ref=mine, dst_ref=mine, send_sem=fwd_send.at[3 * i + j], recv_sem=fwd_recv.at[3 * i + j],
                    device_id=me, device_id_type=MESH)
                cp.wait_send()
                cp.wait_recv()

    arrays = st["srcs"] + st["lands"]
    outs = pl.pallas_call(
        body, name=name, in_specs=[HBM_ONLY] * (ns + nl) + [SEM_SPEC] * 4 + [HBM_SPEC],
        out_shape=tuple(pltpu.HBM(a.shape, a.dtype) for a in arrays), out_specs=tuple([HBM_ONLY] * (ns + nl)),
        input_output_aliases={j: j for j in range(ns + nl)},
        compiler_params=IN_FLIGHT)(*arrays, st["send"], st["recv"], st["fwd_send"], st["fwd_recv"], after)
    return list(outs[ns:ns + nl])


def _sum_slots(land):
    def body(l_ref, o_ref):
        acc = l_ref[0]
        for d in range(1, NDEV):
            acc = acc + l_ref[d]
        o_ref[...] = acc

    vm = pl.BlockSpec(memory_space=pltpu.VMEM)
    return pl.pallas_call(body, name="sum_small", out_shape=_sds(land.shape[1:], F32), in_specs=[vm], out_specs=vm)(land)


def _adam_math(w, g, m, v):
    m2 = ADAM_B1 * m + (1.0 - ADAM_B1) * g
    v2 = ADAM_B2 * v + (1.0 - ADAM_B2) * (g * g)
    delta = -ADAM_LR * ((m2 * ADAM_C1) / (jnp.sqrt(v2 * ADAM_C2) + ADAM_EPS) + ADAM_WD * w)
    return delta, m2, v2


def _adam_layer(land, sel, w, m, v, layer, name, prev=None, tc=512):
    R = land.shape[2]

    def body(l_ref, w_ref, m_ref, v_ref, *rest):
        g_out, d_out, m_out, v_out = rest[-4:]
        g = l_ref[0].astype(F32)
        for d in range(1, NDEV):
            g = g + l_ref[d].astype(F32)
        delta, m2, v2 = _adam_math(w_ref[...], g, m_ref[...], v_ref[...])
        g_out[...] = g
        d_out[...] = delta
        m_out[...] = m2
        v_out[...] = v2

    wspec = pl.BlockSpec((None, R, tc), lambda i: (layer, 0, i))
    in_specs = [pl.BlockSpec((None, NDEV, R, tc), lambda i: (sel, 0, 0, i)), wspec, wspec, wspec]
    args = [land, w, m, v]
    aliases = {}
    if prev is not None:
        in_specs += [HBM_SPEC] * 4
        args += list(prev)
        aliases = {4 + j: j for j in range(4)}
    return pl.pallas_call(
        body, name=name, grid=(D // tc,), in_specs=in_specs, out_specs=[wspec] * 4,
        out_shape=[_sds(w.shape, F32)] * 4, input_output_aliases=aliases, compiler_params=_cparams(1))(*args)


def _adam_stacked(lands, sel, w, m, v, name):
    res = None
    for layer, land in enumerate(lands):
        res = _adam_layer(land, sel, w, m, v, layer, f"{name}{layer}", prev=res)
    return res


def _adam_small(ws, gs, ms, vs):
    n = len(ws)

    def body(*refs):
        w_r, g_r, m_r, v_r = refs[0:n], refs[n:2 * n], refs[2 * n:3 * n], refs[3 * n:4 * n]
        d_o, m_o, v_o = refs[4 * n:5 * n], refs[5 * n:6 * n], refs[6 * n:7 * n]
        for i in range(n):
            delta, m2, v2 = _adam_math(w_r[i][...], g_r[i][...], m_r[i][...], v_r[i][...])
            d_o[i][...] = delta
            m_o[i][...] = m2
            v_o[i][...] = v2

    vm = pl.BlockSpec(memory_space=pltpu.VMEM)
    shapes = [_sds(w.shape, F32) for w in ws]
    outs = pl.pallas_call(body, name="adam_small", in_specs=[vm] * (4 * n), out_specs=[vm] * (3 * n),
                          out_shape=shapes * 3)(*ws, *gs, *ms, *vs)
    return outs[0:n], outs[n:2 * n], outs[2 * n:3 * n]


def _adam_of_slots(land, w, m, v, name):
    def body(l_ref, w_ref, m_ref, v_ref, g_o, d_o, m_o, v_o):
        g = l_ref[0]
        for d in range(1, NDEV):
            g = g + l_ref[d]
        g_o[...] = g
        d_o[...], m_o[...], v_o[...] = _adam_math(w_ref[...], g, m_ref[...], v_ref[...])

    vm = pl.BlockSpec(memory_space=pltpu.VMEM)
    return pl.pallas_call(body, name=name, in_specs=[vm] * 4, out_specs=[vm] * 4,
                          out_shape=[_sds(w.shape, F32)] * 4)(land, w, m, v)


WEIGHT_NAMES = ("ab_norm_g", "ab_w_in", "sgu_norm_g", "sgu_norm_b", "sgu_w", "sgu_bias", "q_norm_g", "k_norm_g",
                "ab_w_out", "cd_norm_g", "cd_w_in", "conv_c_w", "conv_c_b", "c_ln_g", "c_ln_b", "conv_d_w",
                "cd_w_out", "ffn_norm_g", "ffn_w_gate", "ffn_w_up", "ffn_w_down")
SMALL_SHAPES = (("sgu_norm_g", (1, 512)), ("sgu_norm_b", (1, 512)), ("sgu_w", (512, 128)),
                ("sgu_bias", (4, 128)), ("q_norm_g", (3, 1, 64)), ("k_norm_g", (3, 1, 64)), ("cd_norm_g", (1, 128)),
                ("conv_c_w", (31, 1, 64)), ("conv_c_b", (1, 64)), ("c_ln_g", (1, 64)), ("c_ln_b", (1, 64)),
                ("conv_d_w", (3, 1, 64)), ("ffn_norm_g", (2, 1024)))
SHARD_C = 64


def _pack_rows(parts, rows):
    flat = jnp.concatenate([p.reshape(-1) for p in parts])
    return jnp.pad(flat, (0, rows * 128 - flat.shape[0])).reshape(rows, 128)


def kernel(x, ab_norm_g, ab_w_in, sgu_norm_g, sgu_norm_b, sgu_w, sgu_bias, q_norm_g, k_norm_g, ab_w_out, cd_norm_g, cd_w_in, conv_c_w, conv_c_b, c_ln_g, c_ln_b, conv_d_w, cd_w_out, ffn_norm_g, ffn_w_gate, ffn_w_up, ffn_w_down, loss_target, m_ab_norm_g, m_ab_w_in, m_sgu_norm_g, m_sgu_norm_b, m_sgu_w, m_sgu_bias, m_q_norm_g, m_k_norm_g, m_ab_w_out, m_cd_norm_g, m_cd_w_in, m_conv_c_w, m_conv_c_b, m_c_ln_g, m_c_ln_b, m_conv_d_w, m_cd_w_out, m_ffn_norm_g, m_ffn_w_gate, m_ffn_w_up, m_ffn_w_down, v_ab_norm_g, v_ab_w_in, v_sgu_norm_g, v_sgu_norm_b, v_sgu_w, v_sgu_bias, v_q_norm_g, v_k_norm_g, v_ab_w_out, v_cd_norm_g, v_cd_w_in, v_conv_c_w, v_conv_c_b, v_c_ln_g, v_c_ln_b, v_conv_d_w, v_cd_w_out, v_ffn_norm_g, v_ffn_w_gate, v_ffn_w_up, v_ffn_w_down):
    w = dict(zip(WEIGHT_NAMES, (ab_norm_g, ab_w_in, sgu_norm_g, sgu_norm_b, sgu_w, sgu_bias, q_norm_g, k_norm_g, ab_w_out, cd_norm_g, cd_w_in, conv_c_w, conv_c_b, c_ln_g, c_ln_b, conv_d_w, cd_w_out, ffn_norm_g, ffn_w_gate, ffn_w_up, ffn_w_down)))
    m = dict(zip(WEIGHT_NAMES, (m_ab_norm_g, m_ab_w_in, m_sgu_norm_g, m_sgu_norm_b, m_sgu_w, m_sgu_bias, m_q_norm_g, m_k_norm_g, m_ab_w_out, m_cd_norm_g, m_cd_w_in, m_conv_c_w, m_conv_c_b, m_c_ln_g, m_c_ln_b, m_conv_d_w, m_cd_w_out, m_ffn_norm_g, m_ffn_w_gate, m_ffn_w_up, m_ffn_w_down)))
    v = dict(zip(WEIGHT_NAMES, (v_ab_norm_g, v_ab_w_in, v_sgu_norm_g, v_sgu_norm_b, v_sgu_w, v_sgu_bias, v_q_norm_g, v_k_norm_g, v_ab_w_out, v_cd_norm_g, v_cd_w_in, v_conv_c_w, v_conv_c_b, v_c_ln_g, v_c_ln_b, v_conv_d_w, v_cd_w_out, v_ffn_norm_g, v_ffn_w_gate, v_ffn_w_up, v_ffn_w_down)))
    me = _dev_index(*_my_place())

    r_ff = DFF // NDEV
    one = lambda a: (lambda S, j: S[a])
    slot = lambda b: (lambda L, s: L[b].at[s])
    slot2 = lambda b, part: (lambda L, s: L[b].at[part, s])
    shard = lambda a: (lambda S: S[a])

    def later(a):
        return lax.optimization_barrier((a, gathers[0]["token"]))[0]

    def layer_shards(layer):
        return (later(w["ffn_w_gate"][layer]).T.astype(BF16), later(w["ffn_w_up"][layer]).T.astype(BF16),
                later(w["ffn_w_down"][layer]).astype(BF16))

    def gathered(own):
        return _landing((NDEV,) + own.shape, BF16, [((me,), own)])

    def gathered2(a, b):
        return _landing((2, NDEV) + a.shape, BF16, [((0, me), a), ((1, me), b)])

    ab_in_s = w["ab_w_in"][0].T.astype(BF16)
    gathers = {0: _gather_start("gather0_start", [ab_in_s], [gathered(ab_in_s)], [(shard(0), slot(0))])}

    def chan(flat, lo, taps):
        return flat[:, lo:lo + taps * SHARD_C].reshape(NDEV, taps, SHARD_C).transpose(1, 0, 2).reshape(taps, 512)

    def fetch(stage, after):
        if stage == "ab_in":
            ab_out_s = later(w["ab_w_out"][0]).astype(BF16)
            gate0, up0, down0 = layer_shards(0)
            small_s = _pack_rows([later(w[n]) for n in ("cd_norm_g", "conv_c_w", "conv_c_b", "c_ln_g", "c_ln_b",
                                                        "conv_d_w")], 24)
            lands1 = [gathered(ab_out_s), gathered2(gate0, up0), gathered(down0),
                      _landing((NDEV,) + small_s.shape, F32, [((me,), small_s)])]
            gathers[0] = _gather_forward("gather0_forward", gathers[0], [after] + lands1)
            l_ab_in, = _gather_wait("gather0_wait", gathers[0], gathers[0]["token"])
            gathers[1] = _gather_start(
                "gather1_start", [ab_out_s, gate0, up0, down0, small_s], lands1,
                [(shard(0), slot(0)), (shard(1), slot2(1, 0)), (shard(2), slot2(1, 1)), (shard(3), slot(2)),
                 (shard(4), slot(3))], dep=l_ab_in)
            return {"wt_ab_in": l_ab_in.reshape(AB_IN, D), "dep0": gathers[1]["token"]}
        if stage == "attn0":
            cd_in_s, cd_out_s = later(w["cd_w_in"][0]).T.astype(BF16), later(w["cd_w_out"][0]).astype(BF16)
            gate1, up1, down1 = layer_shards(1)
            gathers[2] = _gather_start(
                "gather2_start", [cd_in_s, cd_out_s, gate1, up1, down1],
                [gathered(cd_in_s), gathered(cd_out_s), gathered2(gate1, up1), gathered(down1)],
                [(shard(0), slot(0)), (shard(1), slot(1)), (shard(2), slot2(2, 0)), (shard(3), slot2(2, 1)),
                 (shard(4), slot(3))], dep=after)
            return {"dep_attn1": gathers[2]["token"]}
        if stage == "attn1":
            gathers[1] = _gather_forward("gather1_forward", gathers[1], after)
            return {"dep_attn2": gathers[1]["token"]}
        if stage == "ab_out":
            l_out, l_ffn, l_down, l_small = _gather_wait("gather1_wait", gathers[1], after)
            flat = l_small.reshape(NDEV, 24 * 128)
            return {
                "w_ab_out": l_out.reshape(D, D), "wt_ffn_in0": l_ffn.reshape(2 * DFF, D),
                "w_ffn_down0": l_down.reshape(DFF, D), "cd_norm_g": flat[:, 0:128].reshape(1, D),
                "conv_c_w32": jnp.pad(chan(flat, 128, CONV_C_TAPS), ((0, 1), (0, 0))),
                "conv_c_b": chan(flat, 2112, 1), "c_ln_g": chan(flat, 2176, 1), "c_ln_b": chan(flat, 2240, 1),
                "conv_d_w8": jnp.pad(chan(flat, 2304, CONV_D_TAPS), ((0, 8 - CONV_D_TAPS), (0, 0))),
            }
        if stage == "ffn_down0":
            gathers[2] = _gather_forward("gather2_forward", gathers[2], after)
            return {"dep_down0": gathers[2]["token"]}
        if stage == "cd_in":
            l_in, l_out, l_ffn, l_down = _gather_wait("gather2_wait", gathers[2], after)
            return {"wt_cd_in": l_in.reshape(CD_IN, D), "w_cd_out": l_out.reshape(D, D),
                    "wt_ffn_in1": l_ffn.reshape(2 * DFF, D), "w_ffn_down1": l_down.reshape(DFF, D)}
        return {}

    scatters = {}
    rides_with = {"w_ffn_down1": "wt_ffn_in1", "w_cd_out": "wt_cd_in", "w_ffn_down0": "wt_ffn_in0"}
    held = {}
    smalls = {}

    def small_exchange(name, block):
        land = _landing((NDEV,) + block.shape, F32, [((me,), block)])
        return _exchange_start(name, [block], [land], [(one(0), slot(0))])

    def on_grad(key, arr):
        if key == "small":
            parts = [arr["sgu_norm_g"], arr["sgu_norm_b"], arr["sgu_w"], arr["sgu_bias"], arr["q_norm_g"],
                     arr["k_norm_g"], arr["cd_norm_g"], arr["conv_c_w32"][:CONV_C_TAPS], arr["conv_c_b"], arr["c_ln_g"],
                     arr["c_ln_b"], arr["conv_d_w8"][:CONV_D_TAPS], arr["ffn_norm_g0"], arr["ffn_norm_g1"],
                     arr["loss_cols"]]
            smalls["sizes"] = [p.size for p in parts]
            rows = -(-sum(smalls["sizes"]) // 1024) * 8
            smalls["early"] = small_exchange("small_start", _pack_rows(parts, rows))
            return smalls["early"]["token"]
        if key in rides_with:
            held[rides_with[key]] = (key, arr)
            return None
        group = ([held.pop(key)] if key in held else []) + [(key, arr)]
        srcs, lands, items = [], [], []
        for n, (k, a) in enumerate(group):
            if k.startswith("wt_ffn_in"):
                src = a.reshape(2, NDEV, r_ff, D)
                own = lax.dynamic_slice_in_dim(src, me, 1, axis=1)
                lands.append(lax.dynamic_update_slice(lax.empty(src.shape, BF16), own, (0, me, 0, 0)))
                items += [((lambda S, j, n=n: S[n].at[0, j]), slot2(n, 0)), ((lambda S, j, n=n: S[n].at[1, j]), slot2(n, 1))]
            else:
                rows = a.shape[0] // NDEV
                src = a.reshape(NDEV, rows, D)
                own = lax.dynamic_index_in_dim(src, me, 0, keepdims=False)
                lands.append(_landing((1, NDEV, rows, D), BF16, [((0, me), own)]))
                items.append(((lambda S, j, n=n: S[n].at[j]), slot2(n, 0)))
            srcs.append(src)
        st = _exchange_start(f"scatter_{key}_start", srcs, lands, items)
        scatters[key] = (st, [k for k, _ in group])
        return st["token"]

    W = {
        "dep_first": gathers[0]["token"],
        "ab_norm_g": w["ab_norm_g"], "sgu_norm_g": w["sgu_norm_g"], "sgu_norm_b": w["sgu_norm_b"],
        "sgu_w": w["sgu_w"][0], "sgu_bias": w["sgu_bias"][0], "q_norm_g": w["q_norm_g"][0],
        "k_norm_g": w["k_norm_g"][0], "ffn_norm_g": w["ffn_norm_g"],
    }

    loss_cols, grad_x, G = _local_step(x[0], loss_target[0], W, fetch, on_grad)

    late_small = small_exchange("small_late_start", G["ab_norm_g"])
    landed = {}

    def wait_scatters(name, group_keys, others, after):
        res = _exchange_wait(name, [scatters[gk][0] for gk in group_keys] + others, after)
        for gk, lands in zip(group_keys, res):
            landed.update(zip(scatters[gk][1], lands))
        return [lands[0] for lands in res[len(group_keys):]]

    small_land, = wait_scatters("scatter_wait_early", ["wt_ffn_in1", "wt_cd_in", "wt_ffn_in0", "w_ab_out"],
                                [smalls["early"]], late_small["token"])

    grads, deltas, new_m, new_v = {}, {}, {}, {}
    done = []

    def put(name, res):
        grads[name], deltas[name], new_m[name], new_v[name] = res

    def adam(name, lands, sel, transposed):
        flip = (lambda a: jnp.swapaxes(a, 1, 2)) if transposed else (lambda a: a)
        res = _adam_stacked(lands, sel, flip(w[name]), flip(m[name]), flip(v[name]), f"adam_{name}")
        done.append(res[1])
        put(name, [flip(r) for r in res])

    ffn_in_lands = [landed["wt_ffn_in0"], landed["wt_ffn_in1"]]
    adam("cd_w_in", [landed["wt_cd_in"]], 0, True)
    adam("ffn_w_gate", ffn_in_lands, 0, True)
    adam("ffn_w_up", ffn_in_lands, 1, True)
    adam("cd_w_out", [landed["w_cd_out"]], 0, False)
    adam("ab_w_out", [landed["w_ab_out"]], 0, False)
    adam("ffn_w_down", [landed["w_ffn_down0"], landed["w_ffn_down1"]], 0, False)

    red = _sum_slots(small_land).reshape(-1)
    offs = [0]
    for s in smalls["sizes"]:
        offs.append(offs[-1] + s)
    seg = [red[offs[i]:offs[i + 1]] for i in range(len(smalls["sizes"]))]
    loss = jnp.sum(seg[14])

    def own_channels(full, taps):
        return lax.dynamic_slice_in_dim(full.reshape(taps, 512), me * SHARD_C, SHARD_C, axis=1)

    g_small = {
        "sgu_norm_g": seg[0].reshape(1, 512), "sgu_norm_b": seg[1].reshape(1, 512),
        "sgu_w": seg[2].reshape(512, 128), "sgu_bias": seg[3].reshape(4, 128), "q_norm_g": seg[4].reshape(3, 64),
        "k_norm_g": seg[5].reshape(3, 64),
        "cd_norm_g": lax.dynamic_slice_in_dim(seg[6].reshape(1, D), me * (D // NDEV), D // NDEV, axis=1),
        "conv_c_w": own_channels(seg[7], CONV_C_TAPS), "conv_c_b": own_channels(seg[8], 1),
        "c_ln_g": own_channels(seg[9], 1), "c_ln_b": own_channels(seg[10], 1),
        "conv_d_w": own_channels(seg[11], CONV_D_TAPS),
        "ffn_norm_g": jnp.concatenate([seg[12].reshape(1, D), seg[13].reshape(1, D)], axis=0),
    }

    def small_in(s, a):
        return jnp.swapaxes(a, 0, 1) if len(s) == 3 else a.reshape(s)

    def small_out(n, s, a):
        return jnp.swapaxes(a, 0, 1) if len(s) == 3 else a.reshape(w[n].shape)

    g_in = [g_small[n].reshape(s) for n, s in SMALL_SHAPES]
    d_s, m_s, v_s = _adam_small([small_in(s, w[n]) for n, s in SMALL_SHAPES], g_in,
                                [small_in(s, m[n]) for n, s in SMALL_SHAPES],
                                [small_in(s, v[n]) for n, s in SMALL_SHAPES])
    for i, (n, s) in enumerate(SMALL_SHAPES):
        grads[n], deltas[n] = small_out(n, s, g_in[i]), small_out(n, s, d_s[i])
        new_m[n], new_v[n] = small_out(n, s, m_s[i]), small_out(n, s, v_s[i])
    done.append(d_s[0])

    late_land, = wait_scatters("scatter_wait_last", ["wt_ab_in"], [late_small], list(done))
    put("ab_norm_g", _adam_of_slots(late_land, w["ab_norm_g"], m["ab_norm_g"], v["ab_norm_g"], "adam_ab_norm_g"))
    adam("ab_w_in", [landed["wt_ab_in"]], 0, True)

    return (loss, grad_x[None], *[grads[n] for n in WEIGHT_NAMES], *[deltas[n] for n in WEIGHT_NAMES],
            *[new_m[n] for n in WEIGHT_NAMES], *[new_v[n] for n in WEIGHT_NAMES])
```

```python
import jax
import jax.numpy as jnp
import numpy as np
from jax import lax
from jax.experimental import pallas as pl
from jax.experimental.pallas import tpu as pltpu

F32 = jnp.float32
BF16 = jnp.bfloat16

T = 4096
D = 1024
NDEV = 8
EPS = 1e-6
NEG_INF = -1e30
DFF = 2816
AB_IN = 5632
CD_IN = 2560
HEAD = 64
PAIR = 128
NPAIR = 4
NBACK = 128
DIL_RATES = (1, 4, 16)
ROPE_HALF = 8
ROPE_THETA = 500000.0
CONV_C_TAPS = 31
CONV_D_TAPS = 3
HALO = 32
ATTN_BWD_UNROLL = 4
MAX_ROW_STRIDE = 4

ADAM_LR = 0.001
ADAM_B1 = 0.9
ADAM_B2 = 0.999
ADAM_EPS = 1e-08
ADAM_WD = 0.01
ADAM_STEP = 10
ADAM_C1 = 1.0 / (1.0 - ADAM_B1 ** ADAM_STEP)
ADAM_C2 = 1.0 / (1.0 - ADAM_B2 ** ADAM_STEP)

VMEM_LIMIT_MB = 48
MESH = pl.DeviceIdType.MESH
HBM_SPEC = pl.BlockSpec(memory_space=pl.ANY)


def _cparams(ngrid, vmem_mb=VMEM_LIMIT_MB):
    return pltpu.CompilerParams(dimension_semantics=("arbitrary",) * ngrid,
                                vmem_limit_bytes=vmem_mb * 1024 * 1024)


def _pick(n, options):
    for o in options:
        if n % o == 0:
            return o
    raise ValueError(f"no tile for {n} in {options}")


def _sds(shape, dtype):
    return jax.ShapeDtypeStruct(shape, dtype)


def _sigmoid(x):
    return 1.0 / (1.0 + jnp.exp(-x))


def _sigmoid_bf16(x):
    return 0.5 * jnp.tanh(0.5 * x) + 0.5


def _gelu(z):
    return 0.5 * z * (1.0 + lax.erf(z * 0.7071067811865476))


def _gelu_grad(z):
    return 0.5 * (1.0 + lax.erf(z * 0.7071067811865476)) + z * jnp.exp(-0.5 * z * z) * 0.3989422804014327


def _mm_nt(a, wt, name, out_dtype=BF16, dep=None):
    M, K = a.shape
    N = wt.shape[0]
    tn = _pick(N, (512, 256))

    def body(a_ref, w_ref, *rest):
        o_ref = rest[-1]
        for r0 in range(0, M, 1024):
            o_ref[r0:r0 + 1024, :] = lax.dot_general(
                a_ref[r0:r0 + 1024, :], w_ref[...], (((1,), (1,)), ((), ())),
                preferred_element_type=F32).astype(o_ref.dtype)

    in_specs = [pl.BlockSpec((M, K), lambda j: (0, 0), pipeline_mode=pl.Buffered(1)),
                pl.BlockSpec((tn, K), lambda j: (j, 0))]
    args = [a, wt]
    if dep is not None:
        in_specs.append(HBM_SPEC)
        args.append(dep)
    return pl.pallas_call(
        body, name=name, grid=(N // tn,), in_specs=in_specs, out_specs=pl.BlockSpec((M, tn), lambda j: (0, j)),
        out_shape=_sds((M, N), out_dtype), compiler_params=_cparams(1))(*args)


EPI_ROWS = 256


def _mm_nt_rows(a, wt, name, epilogue, side, side_specs, out_specs, out_shape, sums=(), dep=None, tm=512):
    M, K = a.shape
    N = wt.shape[0]
    ns, no = len(side), len(out_shape)

    def body(a_ref, w_ref, *rest):
        side_refs, outs, acc = rest[0:ns], rest[-1 - no:-1], rest[-1]
        acc[...] = lax.dot_general(a_ref[...], w_ref[...], (((1,), (1,)), ((), ())), preferred_element_type=F32)

        @pl.when(pl.program_id(0) == 0)
        def _():
            for j in sums:
                outs[j][...] = jnp.zeros_like(outs[j])

        for r0 in range(0, tm, EPI_ROWS):
            rows = slice(r0, r0 + EPI_ROWS)
            epilogue(acc[rows, :].astype(BF16).astype(F32), rows, side_refs, outs)

    in_specs = [pl.BlockSpec((tm, K), lambda i: (i, 0)),
                pl.BlockSpec((N, K), lambda i: (0, 0), pipeline_mode=pl.Buffered(1))] + list(side_specs)
    args = [a, wt, *side]
    if dep is not None:
        in_specs.append(HBM_SPEC)
        args.append(dep)
    return pl.pallas_call(
        body, name=name, grid=(M // tm,), in_specs=in_specs, out_specs=list(out_specs), out_shape=list(out_shape),
        scratch_shapes=[pltpu.VMEM((tm, N), F32)], compiler_params=_cparams(1))(*args)


def _mm_nn(a, w, name, mode, resid, gain=None, tgt=None, dep=None, tm=512):
    M, K = a.shape
    N = w.shape[1]
    side = gain if mode == "rms" else tgt
    ring_a = K * 2 > N * 4

    def body(a_ref, w_ref, resid_ref, side_ref, *rest):
        outs, acc = rest[-5 if mode == "rms" else -6:-3], rest[-3]
        i = pl.program_id(0)
        tile = _ring_tile(lambda step: (a_ref if ring_a else resid_ref).at[pl.ds(pl.multiple_of(step * tm, tm), tm), :],
                          rest[-2], rest[-1], i, M // tm)
        if ring_a:
            a_ref = tile
        else:
            resid_ref = tile
        acc[...] = jnp.dot(a_ref[...], w_ref[...], preferred_element_type=F32)

        if mode == "loss":
            @pl.when(i == 0)
            def _():
                outs[2][...] = jnp.zeros_like(outs[2])

        for r0 in range(0, tm, EPI_ROWS):
            rows = slice(r0, r0 + EPI_ROWS)
            v = acc[rows, :] + resid_ref[rows, :]
            if mode == "rms":
                outs[0][rows, :] = v
                r = lax.rsqrt(jnp.mean(v * v, axis=-1, keepdims=True) + EPS)
                outs[1][rows, :] = (v * r * side_ref[...]).astype(BF16)
            else:
                d = v - side_ref[rows, :]
                outs[2][...] += jnp.sum(d * d, axis=0, keepdims=True) * (0.5 / N)
                dy = d * (1.0 / N)
                outs[0][rows, :] = dy
                outs[1][rows, :] = dy.astype(BF16)

    row = pl.BlockSpec((tm, N), lambda i: (i, 0))
    vec = pl.BlockSpec((1, N), lambda i: (0, 0))
    in_specs = [HBM_SPEC if ring_a else pl.BlockSpec((tm, K), lambda i: (i, 0)),
                pl.BlockSpec((K, N), lambda i: (0, 0), pipeline_mode=pl.Buffered(1)), row if ring_a else HBM_SPEC,
                vec if mode == "rms" else row]
    args = [a, w, resid, side]
    if dep is not None:
        in_specs.append(HBM_SPEC)
        args.append(dep)
    if mode == "rms":
        out_specs, out_shape = [row, row], [_sds((M, N), F32), _sds((M, N), BF16)]
    else:
        out_specs, out_shape = [row, row, vec], [_sds((M, N), F32), _sds((M, N), BF16), _sds((1, N), F32)]
    return pl.pallas_call(
        body, name=name, grid=(M // tm,), in_specs=in_specs, out_specs=out_specs, out_shape=out_shape,
        scratch_shapes=[pltpu.VMEM((tm, N), F32)] + (_ring_scratch((tm, K), a.dtype) if ring_a
                                                     else _ring_scratch((tm, N), F32)),
        compiler_params=_cparams(1))(*args)


def _mm_dh_rms_bwd(a, w, x, gain, dres, name, dep=None, tm=512, bf16_copy=True):
    parts = a.shape[0] if a.ndim == 3 else 1
    M, Kp = a.shape[-2], a.shape[-1]
    N = w.shape[1]
    nblk = M // tm
    assert nblk % 2 == 0

    def body(a_ref, w_ref, x_ref, g_ref, dres_ref, *rest):
        dg_ref, acc0, acc1 = rest[-3:]
        dx_ref = rest[-5] if bf16_copy else rest[-4]
        dxb_ref = rest[-4] if bf16_copy else None
        i = pl.program_id(0)

        def matmul(acc):
            if parts == 1:
                acc[...] = jnp.dot(a_ref[...], w_ref[...], preferred_element_type=F32)
            else:
                d = jnp.dot(a_ref[0], w_ref[0:Kp, :], preferred_element_type=F32)
                for p in range(1, parts):
                    d = d + jnp.dot(a_ref[p], w_ref[p * Kp:(p + 1) * Kp, :], preferred_element_type=F32)
                acc[...] = d

        def finish(acc):
            for r0 in range(0, tm, EPI_ROWS // 2):
                rows = slice(r0, r0 + EPI_ROWS // 2)
                v = acc[rows, :]
                xf = x_ref[rows, :]
                r = lax.rsqrt(jnp.mean(xf * xf, axis=-1, keepdims=True) + EPS)
                xhat = xf * r
                dg_ref[...] += jnp.sum(v * xhat, axis=0, keepdims=True)
                dxh = v * g_ref[...]
                tot = dres_ref[rows, :] + r * (dxh - xhat * jnp.mean(dxh * xhat, axis=-1, keepdims=True))
                dx_ref[rows, :] = tot
                if bf16_copy:
                    dxb_ref[rows, :] = tot.astype(BF16)

        @pl.when(i == 0)
        def _():
            dg_ref[...] = jnp.zeros_like(dg_ref)
            matmul(acc0)

        @pl.when((i > 0) & (i < nblk) & (i % 2 == 1))
        def _():
            matmul(acc1)
            finish(acc0)

        @pl.when((i > 0) & (i < nblk) & (i % 2 == 0))
        def _():
            matmul(acc0)
            finish(acc1)

        @pl.when(i == nblk)
        def _():
            finish(acc1)

    last = nblk - 1
    row = pl.BlockSpec((tm, N), lambda i: (jnp.maximum(i - 1, 0), 0))
    vec = pl.BlockSpec((1, N), lambda i: (0, 0))
    if a.ndim == 3:
        a_spec = pl.BlockSpec((parts, tm, Kp), lambda i: (0, jnp.minimum(i, last), 0))
    else:
        a_spec = pl.BlockSpec((tm, Kp), lambda i: (jnp.minimum(i, last), 0))
    w_spec = pl.BlockSpec((parts * Kp, N), lambda i: (0, 0), pipeline_mode=pl.Buffered(1))
    in_specs = [a_spec, w_spec, row, vec, row]
    args = [a, w, x, gain, dres]
    if dep is not None:
        in_specs.append(HBM_SPEC)
        args.append(dep)
    return pl.pallas_call(
        body, name=name, grid=(nblk + 1,), in_specs=in_specs,
        out_specs=[row, row, vec] if bf16_copy else [row, vec],
        out_shape=([_sds((M, N), F32), _sds((M, N), BF16), _sds((1, N), F32)] if bf16_copy
                   else [_sds((M, N), F32), _sds((1, N), F32)]),
        scratch_shapes=[pltpu.VMEM((tm, N), F32), pltpu.VMEM((tm, N), F32)], compiler_params=_cparams(1, 56))(*args)


def _mm_tn(a, b, name, out_dtype=BF16, tt=2048, dep=None):
    parts = a.shape[0] if a.ndim == 3 else 1
    Tt, Mp = a.shape[-2], a.shape[-1]
    N = b.shape[1]
    tn = _pick(Mp, (1408, 1280, 1024, 512))
    jper = Mp // tn
    nt = Tt // tt

    def body(a_ref, b_ref, *rest):
        o_ref, acc = rest[-2:]
        t = pl.program_id(1)

        @pl.when(t == 0)
        def _():
            acc[...] = jnp.zeros_like(acc)

        rows = pl.ds(pl.multiple_of(t * tt, tt), tt)
        acc[...] += lax.dot_general(a_ref[...], b_ref[rows, :], (((0,), (0,)), ((), ())),
                                    preferred_element_type=F32)

        @pl.when(t == nt - 1)
        def _():
            o_ref[...] = acc[...].astype(o_ref.dtype)

    if a.ndim == 3:
        a_spec = pl.BlockSpec((None, tt, tn), lambda j, t: (j // jper, t, j % jper))
    else:
        a_spec = pl.BlockSpec((tt, tn), lambda j, t: (t, j))
    in_specs = [a_spec, pl.BlockSpec((Tt, N), lambda j, t: (0, 0), pipeline_mode=pl.Buffered(1))]
    args = [a, b]
    if dep is not None:
        in_specs.append(HBM_SPEC)
        args.append(dep)
    return pl.pallas_call(
        body, name=name, grid=(parts * jper, nt), in_specs=in_specs,
        out_specs=pl.BlockSpec((tn, N), lambda j, t: (j, 0)),
        out_shape=_sds((parts * Mp, N), out_dtype), scratch_shapes=[pltpu.VMEM((tn, N), F32)],
        compiler_params=_cparams(2))(*args)


FFN_ROWS = 256
PREFETCH_SLOTS = 3


def _ring_tile(window, buf, sem, j, nj):
    def fetch(step):
        slot = step % PREFETCH_SLOTS
        return pltpu.make_async_copy(window(step), buf.at[slot], sem.at[slot])

    @pl.when(j == 0)
    def _():
        for s in range(PREFETCH_SLOTS - 1):
            fetch(s).start()

    @pl.when(j + PREFETCH_SLOTS - 1 < nj)
    def _():
        fetch(j + PREFETCH_SLOTS - 1).start()

    fetch(j).wait()
    return buf.at[j % PREFETCH_SLOTS]


def _ring_scratch(tile_shape, dtype):
    return [pltpu.VMEM((PREFETCH_SLOTS,) + tuple(tile_shape), dtype), pltpu.SemaphoreType.DMA((PREFETCH_SLOTS,))]


def _ffn_in(h, wt_in, name, tn=256):
    nj = DFF // tn

    def body(h_ref, wg_ref, wu_ref, p_ref, act_ref):
        nt = (((1,), (1,)), ((), ()))
        for r0 in range(0, T, FFN_ROWS):
            rows = slice(r0, r0 + FFN_ROWS)
            g = lax.dot_general(h_ref[rows, :], wg_ref[...], nt, preferred_element_type=F32).astype(BF16)
            u = lax.dot_general(h_ref[rows, :], wu_ref[...], nt, preferred_element_type=F32).astype(BF16)
            p_ref[0, rows, :] = g
            p_ref[1, rows, :] = u
            act_ref[rows, :] = g * _sigmoid_bf16(g) * u

    return pl.pallas_call(
        body, name=name, grid=(nj,),
        in_specs=[pl.BlockSpec((T, D), lambda j: (0, 0), pipeline_mode=pl.Buffered(1)),
                  pl.BlockSpec((tn, D), lambda j: (j, 0)), pl.BlockSpec((tn, D), lambda j: (j + nj, 0))],
        out_specs=[pl.BlockSpec((2, T, tn), lambda j: (0, 0, j)), pl.BlockSpec((T, tn), lambda j: (0, j))],
        out_shape=[_sds((2, T, DFF), BF16), _sds((T, DFF), BF16)], compiler_params=_cparams(1))(h, wt_in, wt_in)


def _ffn_dact(dyb, w_down, p3, name, tn=256, dep=None):
    nj = DFF // tn

    def body(dy_ref, w_ref, p_hbm, *rest):
        o_ref, p_buf, sem = rest[-3:]
        p_ref = _ring_tile(lambda step: p_hbm.at[:, :, pl.ds(pl.multiple_of(step * tn, tn), tn)], p_buf, sem,
                           pl.program_id(0), nj)
        for r0 in range(0, T, FFN_ROWS):
            rows = slice(r0, r0 + FFN_ROWS)
            da = lax.dot_general(dy_ref[rows, :], w_ref[...], (((1,), (1,)), ((), ())),
                                 preferred_element_type=F32).astype(BF16)
            g = p_ref[0, rows, :]
            u = p_ref[1, rows, :]
            sg = _sigmoid_bf16(g)
            gs = g * sg
            o_ref[0, rows, :] = (da * u) * (sg + gs * (1.0 - sg))
            o_ref[1, rows, :] = da * gs

    in_specs = [pl.BlockSpec((T, D), lambda j: (0, 0), pipeline_mode=pl.Buffered(1)),
                pl.BlockSpec((tn, D), lambda j: (j, 0)), HBM_SPEC]
    args = [dyb, w_down, p3]
    if dep is not None:
        in_specs.append(HBM_SPEC)
        args.append(dep)
    return pl.pallas_call(
        body, name=name, grid=(nj,), in_specs=in_specs, out_specs=pl.BlockSpec((2, T, tn), lambda j: (0, 0, j)),
        out_shape=_sds((2, T, DFF), BF16), scratch_shapes=_ring_scratch((2, T, tn), BF16),
        compiler_params=_cparams(1))(*args)


def _rms_fwd(x, g, name, tm=512, dep=None):
    def body(x_ref, g_ref, *rest):
        h_ref = rest[-1]
        xf = x_ref[...]
        r = lax.rsqrt(jnp.mean(xf * xf, axis=-1, keepdims=True) + EPS)
        h_ref[...] = (xf * r * g_ref[...]).astype(BF16)

    in_specs = [pl.BlockSpec((tm, D), lambda i: (i, 0)), pl.BlockSpec((1, D), lambda i: (0, 0))]
    args = [x, g]
    if dep is not None:
        in_specs.append(HBM_SPEC)
        args.append(dep)
    return pl.pallas_call(
        body, name=name, grid=(T // tm,), in_specs=in_specs, out_specs=pl.BlockSpec((tm, D), lambda i: (i, 0)),
        out_shape=_sds((T, D), BF16), compiler_params=_cparams(1))(*args)


def _tril_mask():
    r = lax.broadcasted_iota(jnp.int32, (128, 128), 0)
    c = lax.broadcasted_iota(jnp.int32, (128, 128), 1)
    return r >= c


def _mix_a_fwd(pab, sgu_g, sgu_b, sgu_w, sgu_bias3, tm=512):
    def body(zu_ref, zv_ref, g_ref, b_ref, w_ref, bias_ref, o_ref):
        u = _gelu(zu_ref[...].astype(F32))
        v = _gelu(zv_ref[...].astype(F32))
        mu = jnp.mean(v, axis=-1, keepdims=True)
        vc = v - mu
        rstd = lax.rsqrt(jnp.mean(vc * vc, axis=-1, keepdims=True) + EPS)
        vn = (vc * rstd * g_ref[...] + b_ref[...]).astype(BF16)
        tri = _tril_mask()
        for gi in range(4):
            wg = jnp.where(tri, w_ref[gi], 0.0).astype(BF16)
            bg = bias_ref[gi]
            for c in range(tm // 128):
                rs, cs = slice(c * 128, (c + 1) * 128), slice(gi * 128, (gi + 1) * 128)
                mixed = jnp.dot(wg, vn[rs, cs], preferred_element_type=F32) + bg
                o_ref[rs, cs] = (u[rs, cs] * mixed).astype(BF16)

    half = pl.BlockSpec((tm, 512), lambda i: (i, 0))
    return pl.pallas_call(
        body, name="mix_a_fwd", grid=(T // tm,),
        in_specs=[half, pl.BlockSpec((tm, 512), lambda i: (i, 1)),
                  pl.BlockSpec((1, 512), lambda i: (0, 0)), pl.BlockSpec((1, 512), lambda i: (0, 0)),
                  pl.BlockSpec((4, 128, 128), lambda i: (0, 0, 0)), pl.BlockSpec((4, 128, 1), lambda i: (0, 0, 0))],
        out_specs=half, out_shape=_sds((T, D), BF16), compiler_params=_cparams(1),
    )(pab, pab, sgu_g, sgu_b, sgu_w, sgu_bias3)


def _rope_tables():
    pos = np.arange(T, dtype=np.float32)
    inv_freq = np.float32(ROPE_THETA) ** (-np.arange(ROPE_HALF, dtype=np.float32) * np.float32(2.0 / (2 * ROPE_HALF)))
    ang = (pos[:, None] * inv_freq[None, :]).astype(np.float32)
    cos, sin = np.cos(ang), np.sin(ang)
    z8 = np.zeros((T, ROPE_HALF), np.float32)
    rest = np.zeros((T, HEAD - 2 * ROPE_HALF), np.float32)
    c64 = np.concatenate([cos, cos, rest + 1.0], axis=1)
    s1 = np.concatenate([z8, sin, rest], axis=1)
    s2 = np.concatenate([-sin, z8, rest], axis=1)
    return tuple(jnp.asarray(np.tile(t, (1, 2)).astype(np.float32)) for t in (c64, s1, s2))


def _lo_mask(shape):
    return lax.broadcasted_iota(jnp.int32, shape, 1) < HEAD


def _seg_mean(x, lo):
    s_all = jnp.sum(x, axis=-1, keepdims=True)
    s_lo = jnp.sum(jnp.where(lo, x, 0.0), axis=-1, keepdims=True)
    return jnp.where(lo, s_lo, s_all - s_lo) * (1.0 / HEAD)


def _head_blocks():
    r = lax.broadcasted_iota(jnp.int32, (PAIR, PAIR), 0) < HEAD
    c = lax.broadcasted_iota(jnp.int32, (PAIR, PAIR), 1) < HEAD
    return jnp.where(r == c, 1.0, 0.0).astype(BF16)


def _seg_mean_mxu(x, blocks):
    return jnp.dot(x.astype(BF16), blocks, preferred_element_type=F32) * (1.0 / HEAD)


def _rope(n, c, s1, s2):
    return n * c + pltpu.roll(n, ROPE_HALF, 1) * s1 + pltpu.roll(n, PAIR - ROPE_HALF, 1) * s2


def _rope_t(dy, c, s1, s2):
    return dy * c - pltpu.roll(dy, PAIR - ROPE_HALF, 1) * s2 - pltpu.roll(dy, ROPE_HALF, 1) * s1


def _prep_fwd(pab, qg, kg, tabs, tm=512):
    def body(p_hbm, qg_ref, kg_ref, c_ref, s1_ref, s2_ref, *rest):
        outs, (p_buf, sem) = rest[:-2], rest[-2:]
        p_ref = _ring_tile(lambda step: p_hbm.at[pl.ds(pl.multiple_of(step * tm, tm), tm), :], p_buf, sem,
                           pl.program_id(0), T // tm)
        blocks = _head_blocks()
        c, s1, s2 = c_ref[...], s1_ref[...], s2_ref[...]
        for g in range(3):
            qn_ref, kn_ref, v_ref = outs[3 * g:3 * g + 3]
            for p in range(NPAIR):
                for which, gains, dst in ((0, qg_ref, qn_ref), (1, kg_ref, kn_ref)):
                    col = (2 + 3 * which + g) * 512 + p * PAIR
                    xr = p_ref[:, col:col + PAIR].astype(F32)
                    rinv = lax.rsqrt(_seg_mean_mxu(xr * xr, blocks) + EPS)
                    outs[9 + 2 * g + which][p] = rinv.astype(BF16)
                    dst[p] = _rope(xr * rinv * gains[g:g + 1, :], c, s1, s2)
                col = (8 + g) * 512 + p * PAIR
                v_ref[p] = p_ref[:, col:col + PAIR].astype(F32)

    pm = pl.BlockSpec((NPAIR, tm, PAIR), lambda i: (0, i, 0))
    tab = pl.BlockSpec((tm, PAIR), lambda i: (i, 0))
    gain = pl.BlockSpec((3, PAIR), lambda i: (0, 0))
    res = pl.pallas_call(
        body, name="prep_fwd", grid=(T // tm,),
        in_specs=[HBM_SPEC, gain, gain, tab, tab, tab],
        out_specs=[pm] * 15, out_shape=[_sds((NPAIR, T, PAIR), F32)] * 9 + [_sds((NPAIR, T, PAIR), BF16)] * 6,
        scratch_shapes=_ring_scratch((tm, AB_IN), BF16), compiler_params=_cparams(1, 56))(pab, qg, kg, *tabs)
    return res[0:9], res[9:15]


def _res_index(it, rate):
    window = NBACK * rate
    b = it // rate
    rho = it % rate
    start = b * window + rho
    startp = jnp.maximum(start - window, rho)
    kmin = jnp.where(b > 0, 0, NBACK)
    return start, startp, kmin


def _rows(start, rate):
    if rate == 1:
        return pl.ds(pl.multiple_of(start, NBACK), NBACK)
    return pl.ds(start, NBACK, stride=rate)


def _band_bias():
    qs = lax.broadcasted_iota(jnp.int32, (2 * NBACK, 2 * NBACK), 0)
    kj = lax.broadcasted_iota(jnp.int32, (2 * NBACK, 2 * NBACK), 1)
    dist = (qs & (NBACK - 1)) + NBACK - kj
    both = (dist >= 0) & (dist <= NBACK)
    return jnp.where(both, 0.0, NEG_INF), jnp.where(both & (kj >= NBACK), 0.0, NEG_INF)


def _attn_fwd_block(q, kcat, vcat, first, lo, biases):
    vcat1 = jnp.concatenate([vcat, jnp.ones((2 * NBACK, PAIR), BF16)], axis=1)
    q2 = jnp.concatenate([jnp.where(lo, q, 0.0), jnp.where(lo, 0.0, q)], axis=0).astype(BF16)
    s = lax.dot_general(q2, kcat, (((1,), (1,)), ((), ())), preferred_element_type=F32)
    s = s + jnp.where(first, biases[1], biases[0])
    m = jnp.max(s, axis=-1, keepdims=True)
    ol = jnp.dot(jnp.exp(s - m).astype(BF16), vcat1, preferred_element_type=F32)
    o2 = ol[:, 0:PAIR] / ol[:, PAIR:]
    ls = m + jnp.log(ol[:, PAIR:])
    return jnp.where(lo, o2[0:NBACK], o2[NBACK:]), jnp.where(lo, ls[0:NBACK], ls[NBACK:])


def _attn_fwd(qn, kn, v, rate, name, dep=None):
    if rate > MAX_ROW_STRIDE:
        return _attn_fwd_gathered(qn, kn, v, rate, name, dep)

    def body(q_ref, k_ref, v_ref, *rest):
        o_ref, l_ref = rest[-2:]
        lo = _lo_mask((NBACK, PAIR))
        biases = _band_bias()

        def step(it, carry):
            start, startp, kmin = _res_index(it, rate)
            q = q_ref[_rows(start, rate), :] * (HEAD ** -0.5)
            kcat = jnp.concatenate([k_ref[_rows(startp, rate), :], k_ref[_rows(start, rate), :]], axis=0).astype(BF16)
            vcat = jnp.concatenate([v_ref[_rows(startp, rate), :], v_ref[_rows(start, rate), :]], axis=0).astype(BF16)
            o, ls = _attn_fwd_block(q, kcat, vcat, kmin != 0, lo, biases)
            o_ref[_rows(start, rate), :] = o
            l_ref[_rows(start, rate), :] = ls
            return carry

        lax.fori_loop(0, T // NBACK, step, 0, unroll=4)

    pm = pl.BlockSpec((None, T, PAIR), lambda p: (p, 0, 0))
    in_specs, args = [pm, pm, pm], [qn, kn, v]
    if dep is not None:
        in_specs.append(HBM_SPEC)
        args.append(dep)
    return pl.pallas_call(
        body, name=name, grid=(NPAIR,), in_specs=in_specs, out_specs=[pm, pm],
        out_shape=[_sds((NPAIR, T, PAIR), F32)] * 2, compiler_params=_cparams(1))(*args)


def _attn_fwd_gathered(qn, kn, v, rate, name, dep):
    n = T // rate
    nblk = n // NBACK

    def body(q_hbm, k_hbm, v_hbm, *rest):
        o_hbm, l_hbm, qb, kb, vb, ob, lb, in_sem, out_sem = rest[-9:]
        p = pl.program_id(0)
        slot = p % 2

        def loads(pair, s):
            return [pltpu.make_async_copy(x.at[pair, :, r, :], buf.at[s, r], in_sem.at[3 * s + a])
                    for a, (x, buf) in enumerate(((q_hbm, qb), (k_hbm, kb), (v_hbm, vb))) for r in range(rate)]

        def stores(pair, s):
            return [pltpu.make_async_copy(buf.at[s, r], x.at[pair, :, r, :], out_sem.at[2 * s + a])
                    for a, (x, buf) in enumerate(((o_hbm, ob), (l_hbm, lb))) for r in range(rate)]

        @pl.when(p == 0)
        def _():
            for c in loads(0, 0):
                c.start()

        @pl.when(p + 1 < NPAIR)
        def _():
            for c in loads(p + 1, 1 - slot):
                c.start()

        for c in loads(p, slot):
            c.wait()

        @pl.when(p >= 2)
        def _():
            for c in stores(p - 2, slot):
                c.wait()

        lo = _lo_mask((NBACK, PAIR))
        biases = _band_bias()

        def step(it, carry):
            b, r = it % nblk, it // nblk
            cur = pl.ds(pl.multiple_of(b * NBACK, NBACK), NBACK)
            prev = pl.ds(pl.multiple_of(jnp.maximum(b - 1, 0) * NBACK, NBACK), NBACK)
            q = qb[slot, r, cur, :] * (HEAD ** -0.5)
            kcat = jnp.concatenate([kb[slot, r, prev, :], kb[slot, r, cur, :]], axis=0).astype(BF16)
            vcat = jnp.concatenate([vb[slot, r, prev, :], vb[slot, r, cur, :]], axis=0).astype(BF16)
            o, ls = _attn_fwd_block(q, kcat, vcat, b == 0, lo, biases)
            ob[slot, r, cur, :] = o
            lb[slot, r, cur, :] = ls
            return carry

        lax.fori_loop(0, T // NBACK, step, 0, unroll=4)

        for c in stores(p, slot):
            c.start()

        @pl.when(p == NPAIR - 1)
        def _():
            for c in stores(p - 1, 1 - slot) + stores(p, slot):
                c.wait()

    by_residue = lambda a: a.reshape(NPAIR, n, rate, PAIR)
    in_specs, args = [HBM_SPEC] * 3, [by_residue(qn), by_residue(kn), by_residue(v)]
    if dep is not None:
        in_specs.append(HBM_SPEC)
        args.append(dep)
    o, l = pl.pallas_call(
        body, name=name, grid=(NPAIR,), in_specs=in_specs, out_specs=[HBM_SPEC] * 2,
        out_shape=[_sds((NPAIR, n, rate, PAIR), F32)] * 2,
        scratch_shapes=[pltpu.VMEM((2, rate, n, PAIR), F32)] * 5
        + [pltpu.SemaphoreType.DMA((6,)), pltpu.SemaphoreType.DMA((4,))],
        compiler_params=_cparams(1))(*args)
    return o.reshape(NPAIR, T, PAIR), l.reshape(NPAIR, T, PAIR)


def _merge_fwd(cat_ab, outs, lses, tm=512):
    def body(cat_in, o0, o1, o2, l0, l1, l2, cat_ref, lse_ref):
        del cat_in
        for p in range(NPAIR):
            a0, a1, a2 = l0[p], l1[p], l2[p]
            m = jnp.maximum(jnp.maximum(a0, a1), a2)
            w0, w1, w2 = jnp.exp(a0 - m), jnp.exp(a1 - m), jnp.exp(a2 - m)
            s = w0 + w1 + w2
            b = (w0 * o0[p] + w1 * o1[p] + w2 * o2[p]) / s
            cat_ref[:, p * PAIR:(p + 1) * PAIR] = b.astype(BF16)
            lse_ref[p] = m + jnp.log(s)

    pm = pl.BlockSpec((NPAIR, tm, PAIR), lambda i: (0, i, 0))
    return pl.pallas_call(
        body, name="merge_fwd", grid=(T // tm,),
        in_specs=[pl.BlockSpec(memory_space=pl.ANY)] + [pm] * 6,
        out_specs=[pl.BlockSpec((tm, 512), lambda i: (i, 1)), pm],
        out_shape=[_sds((T, D), BF16), _sds((NPAIR, T, PAIR), F32)],
        input_output_aliases={0: 0}, compiler_params=_cparams(1))(cat_ab, *outs, *lses)


def _d_cat_ab(dxb, w_ab_out, cat, dep, tm=512):
    def epilogue(d, rows, side, outs):
        (b_ref,), (da_ref, dbp_ref, e_ref) = side, outs
        da_ref[rows, :] = d[:, 0:512].astype(BF16)
        lo = _lo_mask((EPI_ROWS, PAIR))
        for p in range(NPAIR):
            db = d[:, 512 + p * PAIR:512 + (p + 1) * PAIR]
            b = b_ref[rows, p * PAIR:(p + 1) * PAIR].astype(F32)
            dbp_ref[p, rows, :] = db
            e_ref[p, rows, :] = _seg_mean(db * b, lo) * float(HEAD)

    pm = pl.BlockSpec((NPAIR, tm, PAIR), lambda i: (0, i, 0))
    return _mm_nt_rows(
        dxb, w_ab_out, "mm_d_cat_ab", epilogue, [cat], [pl.BlockSpec((tm, 512), lambda i: (i, 1))],
        [pl.BlockSpec((tm, 512), lambda i: (i, 0)), pm, pm],
        [_sds((T, 512), BF16), _sds((NPAIR, T, PAIR), F32), _sds((NPAIR, T, PAIR), F32)], dep=dep, tm=tm)


def _attn_bwd_block(q, db, ev, ls, kcat, vcat, first, lo, biases):
    scale = HEAD ** -0.5
    nt = (((1,), (1,)), ((), ()))
    tn = (((0,), (0,)), ((), ()))
    q = q * scale
    q2 = jnp.concatenate([jnp.where(lo, q, 0.0), jnp.where(lo, 0.0, q)], axis=0).astype(BF16)
    db2 = jnp.concatenate([jnp.where(lo, db, 0.0), jnp.where(lo, 0.0, db)], axis=0).astype(BF16)
    ls2 = jnp.concatenate([ls[:, 0:1], ls[:, HEAD:HEAD + 1]], axis=0)
    ev2 = jnp.concatenate([ev[:, 0:1], ev[:, HEAD:HEAD + 1]], axis=0)
    s = lax.dot_general(q2, kcat, nt, preferred_element_type=F32)
    pt = jnp.exp(s + jnp.where(first, biases[1], biases[0]) - ls2)
    dp = lax.dot_general(db2, vcat, nt, preferred_element_type=F32)
    ds = (pt * (dp - ev2)).astype(BF16)
    dq2 = jnp.dot(ds, kcat, preferred_element_type=F32) * scale
    dkc = lax.dot_general(ds, q2, tn, preferred_element_type=F32)
    dvc = lax.dot_general(pt.astype(BF16), db2, tn, preferred_element_type=F32)
    return jnp.where(lo, dq2[0:NBACK], dq2[NBACK:]), dkc, dvc


def _attn_bwd_loop(read, write, nblk):
    lo = _lo_mask((NBACK, PAIR))
    biases = _band_bias()

    def one(it, carry):
        dk_carry, dv_carry = carry
        rho = it // nblk
        b = it % nblk
        bp = jnp.maximum(b - 1, 0)
        kcat = jnp.concatenate([read(1, rho, bp), read(1, rho, b)], axis=0).astype(BF16)
        vcat = jnp.concatenate([read(2, rho, bp), read(2, rho, b)], axis=0).astype(BF16)
        dq, dkc, dvc = _attn_bwd_block(read(0, rho, b), read(3, rho, b), read(4, rho, b), read(5, rho, b), kcat, vcat,
                                       b == 0, lo, biases)
        write(0, rho, b, dq)
        write(1, rho, bp, dk_carry + dkc[0:NBACK])
        write(1, rho, b, dkc[NBACK:])
        write(2, rho, bp, dv_carry + dvc[0:NBACK])
        write(2, rho, b, dvc[NBACK:])
        return dkc[NBACK:], dvc[NBACK:]

    def step(i, carry):
        for u in range(ATTN_BWD_UNROLL):
            carry = one(i * ATTN_BWD_UNROLL + u, carry)
        return carry

    zero = jnp.zeros((NBACK, PAIR), F32)
    lax.fori_loop(0, T // NBACK // ATTN_BWD_UNROLL, step, (zero, zero))


def _attn_bwd(qn, kn, v, dbp, e, lse, rate, name):
    if rate > MAX_ROW_STRIDE:
        return _attn_bwd_gathered(qn, kn, v, dbp, e, lse, rate, name)
    window = NBACK * rate

    def body(*refs):
        rows = lambda rho, b: _rows(b * window + rho, rate)

        def write(j, rho, b, value):
            refs[6 + j][rows(rho, b), :] = value

        _attn_bwd_loop(lambda j, rho, b: refs[j][rows(rho, b), :], write, T // window)

    pm = pl.BlockSpec((None, T, PAIR), lambda p: (p, 0, 0))
    return pl.pallas_call(
        body, name=name, grid=(NPAIR,), in_specs=[pm] * 6, out_specs=[pm] * 3,
        out_shape=[_sds((NPAIR, T, PAIR), F32)] * 3, compiler_params=_cparams(1, 56))(qn, kn, v, dbp, e, lse)


def _attn_bwd_gathered(qn, kn, v, dbp, e, lse, rate, name):
    n = T // rate

    def body(*refs):
        ins, outs, in_bufs, out_bufs, (in_sem, out_sem) = refs[0:6], refs[6:9], refs[9:15], refs[15:18], refs[18:20]
        p = pl.program_id(0)
        slot = p % 2

        def loads(pair, s):
            return [pltpu.make_async_copy(x.at[pair, :, r, :], buf.at[s, r], in_sem.at[6 * s + a])
                    for a, (x, buf) in enumerate(zip(ins, in_bufs)) for r in range(rate)]

        def stores(pair, s):
            return [pltpu.make_async_copy(buf.at[s, r], x.at[pair, :, r, :], out_sem.at[3 * s + a])
                    for a, (x, buf) in enumerate(zip(outs, out_bufs)) for r in range(rate)]

        @pl.when(p == 0)
        def _():
            for c in loads(0, 0):
                c.start()

        @pl.when(p + 1 < NPAIR)
        def _():
            for c in loads(p + 1, 1 - slot):
                c.start()

        for c in loads(p, slot):
            c.wait()

        @pl.when(p >= 2)
        def _():
            for c in stores(p - 2, slot):
                c.wait()

        rows = lambda b: pl.ds(pl.multiple_of(b * NBACK, NBACK), NBACK)

        def write(j, rho, b, value):
            out_bufs[j][slot, rho, rows(b), :] = value

        _attn_bwd_loop(lambda j, rho, b: in_bufs[j][slot, rho, rows(b), :], write, n // NBACK)

        for c in stores(p, slot):
            c.start()

        @pl.when(p == NPAIR - 1)
        def _():
            for c in stores(p - 1, 1 - slot) + stores(p, slot):
                c.wait()

    by_residue = lambda a: a.reshape(NPAIR, n, rate, PAIR)
    res = pl.pallas_call(
        body, name=name, grid=(NPAIR,), in_specs=[HBM_SPEC] * 6, out_specs=[HBM_SPEC] * 3,
        out_shape=[_sds((NPAIR, n, rate, PAIR), F32)] * 3,
        scratch_shapes=[pltpu.VMEM((2, rate, n, PAIR), F32)] * 9
        + [pltpu.SemaphoreType.DMA((12,)), pltpu.SemaphoreType.DMA((6,))],
        compiler_params=_cparams(1, 56))(*[by_residue(a) for a in (qn, kn, v, dbp, e, lse)])
    return [r.reshape(NPAIR, T, PAIR) for r in res]


def _ab_in_bwd(pab, dcat, sgu_g, sgu_b, sgu_w, sgu_bias3, qg, kg, tabs, dqkv, rinvs, tm=256):
    def body(p_hbm, dcat_ref, g_ref, b_ref, w_ref, bias_ref, qg_ref, kg_ref, c_ref, s1_ref, s2_ref, *rest):
        dq_refs, rinv_refs = rest[0:9], rest[9:15]
        o_ref, dwm_ref, dbias_ref, dsg_ref, dsb_ref, dgain_ref = rest[15:21]
        p_buf, sem = rest[21:]
        i = pl.program_id(0)
        p_ref = _ring_tile(lambda step: p_hbm.at[pl.ds(pl.multiple_of(step * tm, tm), tm), :], p_buf, sem, i, T // tm)

        @pl.when(i == 0)
        def _():
            dwm_ref[...] = jnp.zeros_like(dwm_ref)
            dbias_ref[...] = jnp.zeros_like(dbias_ref)
            dsg_ref[...] = jnp.zeros_like(dsg_ref)
            dsb_ref[...] = jnp.zeros_like(dsb_ref)
            dgain_ref[...] = jnp.zeros_like(dgain_ref)

        zu = p_ref[:, 0:512].astype(F32)
        zv = p_ref[:, 512:1024].astype(F32)
        u = _gelu(zu)
        v = _gelu(zv)
        mu = jnp.mean(v, axis=-1, keepdims=True)
        vc = v - mu
        rstd = lax.rsqrt(jnp.mean(vc * vc, axis=-1, keepdims=True) + EPS)
        xhat = vc * rstd
        vn = (xhat * g_ref[...] + b_ref[...]).astype(BF16)
        da = dcat_ref[...].astype(F32)
        tri = _tril_mask()
        du_parts = [[None] * 4 for _ in range(tm // 128)]
        dvn_parts = [[None] * 4 for _ in range(tm // 128)]
        for gi in range(4):
            wg = jnp.where(tri, w_ref[gi], 0.0).astype(BF16)
            bg = bias_ref[gi]
            for c in range(tm // 128):
                rs, cs = slice(c * 128, (c + 1) * 128), slice(gi * 128, (gi + 1) * 128)
                vblk = vn[rs, cs]
                mixed = jnp.dot(wg, vblk, preferred_element_type=F32) + bg
                dab = da[rs, cs]
                du_parts[c][gi] = dab * mixed
                dmixed = dab * u[rs, cs]
                dmb = dmixed.astype(BF16)
                dvn_parts[c][gi] = lax.dot_general(wg, dmb, (((0,), (0,)), ((), ())), preferred_element_type=F32)
                dwm = lax.dot_general(dmb, vblk, (((1,), (1,)), ((), ())), preferred_element_type=F32)
                dwm_ref[gi] += jnp.where(tri, dwm, 0.0)
                dbias_ref[gi] += dmixed
        du = jnp.concatenate([jnp.concatenate(r, axis=1) for r in du_parts], axis=0)
        dvn = jnp.concatenate([jnp.concatenate(r, axis=1) for r in dvn_parts], axis=0)
        dsg_ref[...] += jnp.sum(dvn * xhat, axis=0, keepdims=True)
        dsb_ref[...] += jnp.sum(dvn, axis=0, keepdims=True)
        dxh = dvn * g_ref[...]
        dv = rstd * (dxh - jnp.mean(dxh, axis=-1, keepdims=True)
                     - xhat * jnp.mean(dxh * xhat, axis=-1, keepdims=True))
        o_ref[:, 0:512] = (du * _gelu_grad(zu)).astype(BF16)
        o_ref[:, 512:1024] = (dv * _gelu_grad(zv)).astype(BF16)

        blocks = _head_blocks()
        c, s1, s2 = c_ref[...], s1_ref[...], s2_ref[...]
        for g in range(3):
            dq_ref, dk_ref, dv_ref = dq_refs[3 * g:3 * g + 3]
            for p in range(NPAIR):
                for which, gains, src in ((0, qg_ref, dq_ref), (1, kg_ref, dk_ref)):
                    col = (2 + 3 * which + g) * 512 + p * PAIR
                    xr = p_ref[:, col:col + PAIR].astype(F32)
                    rinv = rinv_refs[2 * g + which][p].astype(F32)
                    xh = xr * rinv
                    dn = _rope_t(src[p], c, s1, s2)
                    row = 2 * g + which
                    dgain_ref[row:row + 1, :] += jnp.sum(dn * xh, axis=0, keepdims=True)
                    dxh2 = dn * gains[g:g + 1, :]
                    dx = rinv * (dxh2 - xh * _seg_mean_mxu(dxh2 * xh, blocks))
                    o_ref[:, col:col + PAIR] = dx.astype(BF16)
                col = (8 + g) * 512 + p * PAIR
                o_ref[:, col:col + PAIR] = dv_ref[p].astype(BF16)

    pm = pl.BlockSpec((NPAIR, tm, PAIR), lambda i: (0, i, 0))
    tab = pl.BlockSpec((tm, PAIR), lambda i: (i, 0))
    gain = pl.BlockSpec((3, PAIR), lambda i: (0, 0))
    vec = pl.BlockSpec((1, 512), lambda i: (0, 0))
    full = pl.BlockSpec((tm, AB_IN), lambda i: (i, 0))
    w4 = pl.BlockSpec((4, 128, 128), lambda i: (0, 0, 0))
    return pl.pallas_call(
        body, name="ab_in_bwd", grid=(T // tm,),
        in_specs=[HBM_SPEC, pl.BlockSpec((tm, 512), lambda i: (i, 0)), vec, vec, w4,
                  pl.BlockSpec((4, 128, 1), lambda i: (0, 0, 0)), gain, gain, tab, tab, tab] + [pm] * 15,
        out_specs=[full, w4, w4, vec, vec, pl.BlockSpec((8, PAIR), lambda i: (0, 0))],
        out_shape=[_sds((T, AB_IN), BF16), _sds((4, 128, 128), F32), _sds((4, 128, 128), F32),
                   _sds((1, 512), F32), _sds((1, 512), F32), _sds((8, PAIR), F32)],
        scratch_shapes=_ring_scratch((tm, AB_IN), BF16), compiler_params=_cparams(1))(pab, dcat, sgu_g, sgu_b, sgu_w, sgu_bias3, qg, kg, *tabs, *dqkv, *rinvs)


def _ln_stats(x):
    mu = jnp.mean(x, axis=-1, keepdims=True)
    xc = x - mu
    rstd = lax.rsqrt(jnp.mean(xc * xc, axis=-1, keepdims=True) + EPS)
    return xc * rstd, rstd


CONV_RC = 64


def _shifted_copies(src, dst, tm):
    dst[0] = src[...]
    for b in range(1, 8):
        dst[b, 0:tm + HALO - 8, :] = src[pl.ds(b, tm + HALO - 8), :]


def _offsets_by_phase(first):
    groups = {}
    for o in range(first, first + CONV_C_TAPS):
        groups.setdefault(o % 8, []).append(o)
    return sorted(groups.items())


def _window(shifted, b8, base, offsets, lanes):
    rows = 8 * (max(offsets) // 8) + CONV_RC
    return shifted[b8, pl.ds(base, rows), lanes].reshape(rows // 8, 8, 128)


def _cd_fwd(pcd, cw, cb, lg, lb, dw, tm=512):
    per = tm // HALO

    def body(p_ref, h_ref, cw_ref, cb_ref, lg_ref, lb_ref, dw_ref, cat_ref, c0_ref, c1_ref, dd_ref, y_ref,
             buf, buf2, sb):
        i = pl.program_id(0)
        live = jnp.where(i > 0, 1.0, 0.0)
        a = p_ref[:, 0:512].astype(F32)
        gt = p_ref[:, 512:1024].astype(F32)
        gb = p_ref[:, 1024:1536].astype(F32)
        gc = p_ref[:, 1536:2048].astype(F32)
        hv = p_ref[:, 2048:2560].astype(F32)
        c0 = a * _sigmoid(gt)
        dd = gc * hv
        buf[0:HALO, :] = h_ref[:, 0:512].astype(F32) * _sigmoid(h_ref[:, 512:1024].astype(F32)) * live
        buf[HALO:, :] = c0
        buf2[0:HALO, :] = h_ref[:, 1536:2048].astype(F32) * h_ref[:, 2048:2560].astype(F32) * live
        buf2[HALO:, :] = dd
        c0_ref[...] = c0.astype(BF16)
        dd_ref[...] = dd.astype(BF16)
        _shifted_copies(buf, sb, tm)

        def conv_rows(r, carry):
            base = pl.multiple_of(r * CONV_RC, CONV_RC)
            for c in range(4):
                lanes = slice(c * 128, (c + 1) * 128)
                acc = jnp.broadcast_to(cb_ref[:, lanes], (CONV_RC // 8, 8, 128))
                for b8, offsets in _offsets_by_phase(HALO - (CONV_C_TAPS - 1)):
                    win = _window(sb, b8, base, offsets, lanes)
                    for o in offsets:
                        j = o - (HALO - (CONV_C_TAPS - 1))
                        acc = acc + cw_ref[8 * j:8 * j + 8, lanes] * win[o // 8:o // 8 + CONV_RC // 8]
                c1_ref[pl.ds(base, CONV_RC), lanes] = acc.reshape(CONV_RC, 128)
            return carry

        lax.fori_loop(0, tm // CONV_RC, conv_rows, 0)
        xhat, _ = _ln_stats(c1_ref[...])
        c2 = xhat * lg_ref[...] + lb_ref[...]
        y = jnp.zeros((tm, 512), F32)
        for j in range(CONV_D_TAPS):
            y = y + dw_ref[j:j + 1, :] * buf2[pl.ds(HALO - (CONV_D_TAPS - 1) + j, tm), :]
        cat_ref[:, 0:512] = (c2 * _sigmoid(c2)).astype(BF16)
        cat_ref[:, 512:1024] = (gb * y).astype(BF16)
        y_ref[...] = y.astype(BF16)

    half = pl.BlockSpec((tm, 512), lambda i: (i, 0))
    vec = pl.BlockSpec((1, 512), lambda i: (0, 0))
    return pl.pallas_call(
        body, name="cd_fwd", grid=(T // tm,),
        in_specs=[pl.BlockSpec((tm, CD_IN), lambda i: (i, 0)),
                  pl.BlockSpec((HALO, CD_IN), lambda i: (jnp.maximum(i * per - 1, 0), 0)),
                  pl.BlockSpec((8 * 32, 512), lambda i: (0, 0)), vec, vec, vec, pl.BlockSpec((8, 512), lambda i: (0, 0))],
        out_specs=[pl.BlockSpec((tm, D), lambda i: (i, 0)), half, half, half, half],
        out_shape=[_sds((T, D), BF16), _sds((T, 512), BF16), _sds((T, 512), F32), _sds((T, 512), BF16),
                   _sds((T, 512), BF16)],
        scratch_shapes=[pltpu.VMEM((HALO + tm, 512), F32), pltpu.VMEM((HALO + tm, 512), F32),
                        pltpu.VMEM((8, HALO + tm, 512), F32)],
        compiler_params=_cparams(1))(pcd, pcd, cw, cb, lg, lb, dw)


def _d_cat_cd(dxb, w_cd_out, c1, pcd, y, lg, lb, dep, tm=512):
    def epilogue(d, rows, side, outs):
        c1_ref, gb_ref, y_ref, lg_ref, lb_ref = side
        dc1_ref, dy3_ref, dgb_ref, dlg_ref, dlb_ref, dcb_ref = outs
        dc, ddo = d[:, 0:512], d[:, 512:1024]
        xhat, rstd = _ln_stats(c1_ref[rows, :])
        c2 = xhat * lg_ref[...] + lb_ref[...]
        sg = _sigmoid(c2)
        dc2 = dc * sg * (1.0 + c2 * (1.0 - sg))
        dlg_ref[...] += jnp.sum(dc2 * xhat, axis=0, keepdims=True)
        dlb_ref[...] += jnp.sum(dc2, axis=0, keepdims=True)
        dxh = dc2 * lg_ref[...]
        dc1 = rstd * (dxh - jnp.mean(dxh, axis=-1, keepdims=True)
                      - xhat * jnp.mean(dxh * xhat, axis=-1, keepdims=True))
        dcb_ref[...] += jnp.sum(dc1, axis=0, keepdims=True)
        dc1_ref[rows, :] = dc1
        dgb_ref[rows, :] = (ddo * y_ref[rows, :].astype(F32)).astype(BF16)
        dy3_ref[rows, :] = ddo * gb_ref[rows, :].astype(F32)

    half = pl.BlockSpec((tm, 512), lambda i: (i, 0))
    vec = pl.BlockSpec((1, 512), lambda i: (0, 0))
    return _mm_nt_rows(
        dxb, w_cd_out, "mm_d_cat_cd", epilogue, [c1, pcd, y, lg, lb],
        [half, pl.BlockSpec((tm, 512), lambda i: (i, 2)), half, vec, vec], [half, half, half, vec, vec, vec],
        [_sds((T, 512), F32), _sds((T, 512), F32), _sds((T, 512), BF16),
         _sds((1, 512), F32), _sds((1, 512), F32), _sds((1, 512), F32)], sums=(3, 4, 5), dep=dep, tm=tm)


def _cd_bwd_conv(pcd, dc1, dy3, c0, dd, dgb, cw8, dw, tm=256):
    per = tm // HALO
    nblk = T // tm
    last32 = T // HALO - 1

    def body(p_ref, dc1_ref, dc1n_ref, dy3_ref, dy3n_ref, c0_ref, dd_ref, dgb_ref, cw_ref, dw_ref,
             o_ref, dcw_ref, ddw_ref, dbuf, d3buf, sd, dc0_buf):
        i = pl.program_id(0)
        has_next = jnp.where(i < nblk - 1, 1.0, 0.0)

        @pl.when(i == 0)
        def _():
            dcw_ref[...] = jnp.zeros_like(dcw_ref)
            ddw_ref[...] = jnp.zeros_like(ddw_ref)

        dbuf[0:tm, :] = dc1_ref[...]
        dbuf[tm:, :] = dc1n_ref[...] * has_next
        d3buf[0:tm, :] = dy3_ref[...]
        d3buf[tm:, :] = dy3n_ref[...] * has_next
        _shifted_copies(dbuf, sd, tm)
        n_tiles = tm // CONV_RC

        phases = _offsets_by_phase(0)

        def dc0_rows(r, carry):
            base = pl.multiple_of(r * CONV_RC, CONV_RC)
            for c in range(4):
                lanes = slice(c * 128, (c + 1) * 128)
                acc = jnp.zeros((CONV_RC // 8, 8, 128), F32)
                for b8, offsets in phases:
                    win = _window(sd, b8, base, offsets, lanes)
                    for o in offsets:
                        j = CONV_C_TAPS - 1 - o
                        acc = acc + cw_ref[8 * j:8 * j + 8, lanes] * win[o // 8:o // 8 + CONV_RC // 8]
                dc0_buf[pl.ds(base, CONV_RC), lanes] = acc.reshape(CONV_RC, 128)
            return carry

        lax.fori_loop(0, n_tiles, dc0_rows, 0)

        for c in range(4):
            lanes = slice(c * 128, (c + 1) * 128)
            for b8, offsets in phases:
                def dw_rows(r, accs, lanes=lanes, b8=b8, offsets=offsets):
                    base = pl.multiple_of(r * CONV_RC, CONV_RC)
                    xin = c0_ref[pl.ds(base, CONV_RC), lanes].astype(F32).reshape(CONV_RC // 8, 8, 128)
                    win = _window(sd, b8, base, offsets, lanes)
                    return tuple(acc + jnp.sum(xin * win[o // 8:o // 8 + CONV_RC // 8], axis=0)
                                 for acc, o in zip(accs, offsets))

                accs = lax.fori_loop(0, n_tiles, dw_rows, tuple(jnp.zeros((8, 128), F32) for _ in offsets))
                for acc, o in zip(accs, offsets):
                    j = CONV_C_TAPS - 1 - o
                    dcw_ref[j:j + 1, lanes] += jnp.sum(acc, axis=0, keepdims=True)

        dc0 = dc0_buf[...]
        ddin = dd_ref[...].astype(F32)
        ddd = jnp.zeros((tm, 512), F32)
        for j in range(CONV_D_TAPS):
            dy_shift = d3buf[pl.ds(CONV_D_TAPS - 1 - j, tm), :]
            ddd = ddd + dw_ref[j:j + 1, :] * dy_shift
            ddw_ref[j:j + 1, :] += jnp.sum(ddin * dy_shift, axis=0, keepdims=True)

        a = p_ref[:, 0:512].astype(F32)
        gt = p_ref[:, 512:1024].astype(F32)
        gc = p_ref[:, 1536:2048].astype(F32)
        hv = p_ref[:, 2048:2560].astype(F32)
        sg = _sigmoid(gt)
        o_ref[:, 0:512] = (dc0 * sg).astype(BF16)
        o_ref[:, 512:1024] = (dc0 * a * sg * (1.0 - sg)).astype(BF16)
        o_ref[:, 1024:1536] = dgb_ref[...]
        o_ref[:, 1536:2048] = (ddd * hv).astype(BF16)
        o_ref[:, 2048:2560] = (ddd * gc).astype(BF16)

    half = pl.BlockSpec((tm, 512), lambda i: (i, 0))
    nxt = pl.BlockSpec((HALO, 512), lambda i: (jnp.minimum((i + 1) * per, last32), 0))
    full = pl.BlockSpec((tm, CD_IN), lambda i: (i, 0))
    return pl.pallas_call(
        body, name="cd_bwd_conv", grid=(nblk,),
        in_specs=[full, half, nxt, half, nxt, half, half, half,
                  pl.BlockSpec((8 * 32, 512), lambda i: (0, 0)), pl.BlockSpec((8, 512), lambda i: (0, 0))],
        out_specs=[full, pl.BlockSpec((32, 512), lambda i: (0, 0)), pl.BlockSpec((8, 512), lambda i: (0, 0))],
        out_shape=[_sds((T, CD_IN), BF16), _sds((32, 512), F32), _sds((8, 512), F32)],
        scratch_shapes=[pltpu.VMEM((tm + HALO, 512), F32), pltpu.VMEM((tm + HALO, 512), F32),
                        pltpu.VMEM((8, tm + HALO, 512), F32), pltpu.VMEM((tm, 512), F32)],
        compiler_params=_cparams(1))(pcd, dc1, dc1, dy3, dy3, c0, dd, dgb, cw8, dw)


def _local_step(x, tgt, W, fetch=None, on_grad=None):
    W = dict(W)
    if fetch is None:
        fetch = lambda stage, after: {}
    if on_grad is None:
        on_grad = lambda key, arr: None
    tabs = _rope_tables()
    qg = jnp.tile(W["q_norm_g"], (1, 2))
    kg = jnp.tile(W["k_norm_g"], (1, 2))
    bias3 = W["sgu_bias"].reshape(4, 128, 1)
    G = {}

    h0 = _rms_fwd(x, W["ab_norm_g"], "rms_fwd_ab", dep=W.get("dep_first"))
    W.update(fetch("ab_in", h0))
    pab = _mm_nt(h0, W["wt_ab_in"], "mm_ab_in", dep=W.get("dep0"))
    cat_ab = _mix_a_fwd(pab, W["sgu_norm_g"], W["sgu_norm_b"], W["sgu_w"], bias3)
    qkv, rinvs = _prep_fwd(pab, qg, kg, tabs)
    outs, lses = [], []
    for g, rate in enumerate(DIL_RATES):
        o, l = _attn_fwd(qkv[3 * g], qkv[3 * g + 1], qkv[3 * g + 2], rate, f"attn_fwd_{g}", dep=W.get(f"dep_attn{g}"))
        outs.append(o)
        lses.append(l)
        W.update(fetch(f"attn{g}", o))
    cat_ab, lse = _merge_fwd(cat_ab, outs, lses)
    W.update(fetch("ab_out", lse))
    x1, h1 = _mm_nn(cat_ab, W["w_ab_out"], "mm_ab_out", mode="rms", resid=x, gain=W["ffn_norm_g"][0:1])
    pf0, act0 = _ffn_in(h1, W["wt_ffn_in0"], "ffn_in0")
    W.update(fetch("ffn_down0", act0))
    x2, h2 = _mm_nn(act0, W["w_ffn_down0"], "mm_ffn_down0", mode="rms", resid=x1, gain=W["cd_norm_g"],
                    dep=W.get("dep_down0"))
    W.update(fetch("cd_in", h2))
    pcd = _mm_nt(h2, W["wt_cd_in"], "mm_cd_in")
    cw8 = jnp.repeat(W["conv_c_w32"], 8, axis=0)
    cat_cd, c0, c1, dd, yv = _cd_fwd(pcd, cw8, W["conv_c_b"], W["c_ln_g"], W["c_ln_b"], W["conv_d_w8"])
    x3, h3 = _mm_nn(cat_cd, W["w_cd_out"], "mm_cd_out", mode="rms", resid=x2, gain=W["ffn_norm_g"][1:2])
    pf1, act1 = _ffn_in(h3, W["wt_ffn_in1"], "ffn_in1")
    dy, dyb, loss_cols = _mm_nn(act1, W["w_ffn_down1"], "mm_ffn_down1", mode="loss", resid=x3, tgt=tgt)

    def ffn_bwd(xin, h, pf, act, dres, dresb, layer):
        G[f"w_ffn_down{layer}"] = _mm_tn(act, dresb, f"mm_g_ffn_down{layer}")
        dep = on_grad(f"w_ffn_down{layer}", G[f"w_ffn_down{layer}"])
        dpf = _ffn_dact(dresb, W[f"w_ffn_down{layer}"], pf, f"ffn_dact{layer}", dep=dep)
        G[f"wt_ffn_in{layer}"] = _mm_tn(dpf, h, f"mm_g_ffn_in{layer}")
        dep = on_grad(f"wt_ffn_in{layer}", G[f"wt_ffn_in{layer}"])
        dx, dxb, G[f"ffn_norm_g{layer}"] = _mm_dh_rms_bwd(
            dpf, W[f"wt_ffn_in{layer}"], xin, W["ffn_norm_g"][layer:layer + 1], dres, f"mm_d_h_ffn{layer}", dep=dep)
        return dx, dxb

    dx3, dx3b = ffn_bwd(x3, h3, pf1, act1, dy, dyb, 1)

    G["w_cd_out"] = _mm_tn(cat_cd, dx3b, "mm_g_cd_out")
    dep = on_grad("w_cd_out", G["w_cd_out"])
    dc1, dy3, dgb, G["c_ln_g"], G["c_ln_b"], G["conv_c_b"] = _d_cat_cd(
        dx3b, W["w_cd_out"], c1, pcd, yv, W["c_ln_g"], W["c_ln_b"], dep)
    dpcd, G["conv_c_w32"], G["conv_d_w8"] = _cd_bwd_conv(pcd, dc1, dy3, c0, dd, dgb, cw8, W["conv_d_w8"])
    G["wt_cd_in"] = _mm_tn(dpcd, h2, "mm_g_cd_in")
    dep = on_grad("wt_cd_in", G["wt_cd_in"])
    dx2, dx2b, G["cd_norm_g"] = _mm_dh_rms_bwd(dpcd, W["wt_cd_in"], x2, W["cd_norm_g"], dx3, "mm_d_h_cd", dep=dep)

    dx1, dx1b = ffn_bwd(x1, h1, pf0, act0, dx2, dx2b, 0)

    G["w_ab_out"] = _mm_tn(cat_ab, dx1b, "mm_g_ab_out")
    dep = on_grad("w_ab_out", G["w_ab_out"])
    dcat_a, dbp, e = _d_cat_ab(dx1b, W["w_ab_out"], cat_ab, dep)
    dqkv = []
    for g, rate in enumerate(DIL_RATES):
        dqkv += _attn_bwd(qkv[3 * g], qkv[3 * g + 1], qkv[3 * g + 2], dbp, e, lse, rate, f"attn_bwd_{g}")
    dpab, G["sgu_w"], dbias_part, G["sgu_norm_g"], G["sgu_norm_b"], dgain = _ab_in_bwd(
        pab, dcat_a, W["sgu_norm_g"], W["sgu_norm_b"], W["sgu_w"], bias3, qg, kg, tabs, dqkv, rinvs)
    G["sgu_bias"] = jnp.sum(dbias_part, axis=-1)
    dgain = dgain[0:6, 0:HEAD] + dgain[0:6, HEAD:PAIR]
    G["q_norm_g"] = dgain[0::2]
    G["k_norm_g"] = dgain[1::2]
    G["loss_cols"] = loss_cols
    dep = on_grad("small", G)
    G["wt_ab_in"] = _mm_tn(dpab, h0, "mm_g_ab_in", dep=dep)
    dep = on_grad("wt_ab_in", G["wt_ab_in"])
    grad_x, G["ab_norm_g"] = _mm_dh_rms_bwd(dpab, W["wt_ab_in"], x, W["ab_norm_g"], dx1, "mm_d_h_ab", dep=dep,
                                            bf16_copy=False)
    return loss_cols, grad_x, G


def _my_place():
    return lax.axis_index("x"), lax.axis_index("y"), lax.axis_index("c")


def _dev_index(px, py, pc):
    return 4 * px + 2 * py + pc


def _flip(place, k):
    x, y, c = place
    return (1 - x if k & 4 else x, 1 - y if k & 2 else y, 1 - c if k & 1 else c)


def _landing(shape, dtype, own):
    buf = lax.empty(shape, dtype)
    for lead, part in own:
        buf = lax.dynamic_update_slice(buf, part.reshape((1,) * len(lead) + part.shape),
                                       tuple(lead) + (0,) * part.ndim)
    return buf


HBM_ONLY = pl.BlockSpec(memory_space=pltpu.HBM)
SEM_SPEC = pl.BlockSpec(memory_space=pltpu.SEMAPHORE)
IN_FLIGHT = pltpu.CompilerParams(has_side_effects=pltpu.SideEffectType.DATAFLOW_SIDE_EFFECTING)


def _in_hbm(a):
    return pltpu.with_memory_space_constraint(a, pltpu.HBM)


def _exchange_start(name, srcs, lands, items, dep=None):
    ns, nl, ni = len(srcs), len(lands), len(items)

    def body(*refs):
        S, L = refs[0:ns], refs[ns:ns + nl]
        first_out = ns + nl + (0 if dep is None else 1)
        send_sems, recv_sems, token = refs[first_out], refs[first_out + 1], refs[-1]
        me = _my_place()
        mi = _dev_index(*me)
        for i, (src, dst) in enumerate(items):
            for k in range(1, NDEV):
                peer = _flip(me, k)
                pltpu.make_async_remote_copy(
                    src_ref=src(S, _dev_index(*peer)), dst_ref=dst(L, mi), send_sem=send_sems.at[7 * i + k - 1],
                    recv_sem=recv_sems.at[7 * i + k - 1], device_id=peer, device_id_type=MESH).start()
        token[...] = jnp.zeros_like(token)

    thru = [pltpu.HBM(a.shape, a.dtype) for a in list(srcs) + list(lands)]
    args = [_in_hbm(a) for a in srcs] + [_in_hbm(a) for a in lands]
    in_specs = [HBM_ONLY] * (ns + nl)
    if dep is not None:
        args.append(dep)
        in_specs.append(HBM_SPEC)
    outs = pl.pallas_call(
        body, name=name, in_specs=in_specs,
        out_shape=(pltpu.SemaphoreType.DMA((7 * ni,)), pltpu.SemaphoreType.DMA((7 * ni,)), *thru, _sds((8, 128), F32)),
        out_specs=(SEM_SPEC, SEM_SPEC, *[HBM_ONLY] * (ns + nl), pl.BlockSpec(memory_space=pltpu.VMEM)),
        input_output_aliases={j: 2 + j for j in range(ns + nl)}, compiler_params=IN_FLIGHT)(*args)
    return dict(send=outs[0], recv=outs[1], srcs=list(outs[2:2 + ns]), lands=list(outs[2 + ns:2 + ns + nl]),
                token=outs[-1], items=items)


def _exchange_wait(name, states, after):
    after = list(after) if isinstance(after, (list, tuple)) else [after]
    counts = [(len(st["srcs"]), len(st["lands"]), len(st["items"])) for st in states]
    n_arrays = sum(c[0] + c[1] for c in counts)

    def body(*refs):
        me = _my_place()
        mi = _dev_index(*me)
        pos = 0
        sem_pos = n_arrays
        for st, (ns, nl, ni) in zip(states, counts):
            S, L = refs[pos:pos + ns], refs[pos + ns:pos + ns + nl]
            send_sems, recv_sems = refs[sem_pos], refs[sem_pos + 1]
            pos += ns + nl
            sem_pos += 2
            for i, (src, dst) in enumerate(st["items"]):
                for k in range(1, NDEV):
                    cp = pltpu.make_async_remote_copy(
                        src_ref=src(S, mi), dst_ref=dst(L, mi), send_sem=send_sems.at[7 * i + k - 1],
                        recv_sem=recv_sems.at[7 * i + k - 1], device_id=me, device_id_type=MESH)
                    cp.wait_send()
                    cp.wait_recv()

    arrays, sems = [], []
    for st in states:
        arrays += st["srcs"] + st["lands"]
        sems += [st["send"], st["recv"]]
    outs = pl.pallas_call(
        body, name=name, in_specs=[HBM_ONLY] * n_arrays + [SEM_SPEC] * len(sems) + [HBM_SPEC] * len(after),
        out_shape=tuple(pltpu.HBM(a.shape, a.dtype) for a in arrays), out_specs=tuple([HBM_ONLY] * n_arrays),
        input_output_aliases={j: j for j in range(n_arrays)}, compiler_params=IN_FLIGHT)(*arrays, *sems, *after)
    lands, pos = [], 0
    for ns, nl, _ in counts:
        lands.append(list(outs[pos + ns:pos + ns + nl]))
        pos += ns + nl
    return lands


def _place_and_neighbours():
    x, y, c = _my_place()
    return (x, y, c), (x, y, 1 - c), [(1 - x, y), (x, 1 - y), (1 - x, 1 - y)]


def _gather_start(name, srcs, lands, items, dep=None):
    ns, nl, ni = len(srcs), len(lands), len(items)

    def body(*refs):
        S, L = refs[0:ns], refs[ns:ns + nl]
        first_out = ns + nl + (0 if dep is None else 1)
        send_sems, recv_sems, token = refs[first_out], refs[first_out + 1], refs[-1]
        me, sib, chips = _place_and_neighbours()
        mi = _dev_index(*me)
        for i, (src, dst) in enumerate(items):
            for k, to in enumerate([sib] + [(*chip, me[2]) for chip in chips]):
                pltpu.make_async_remote_copy(
                    src_ref=src(S), dst_ref=dst(L, mi), send_sem=send_sems.at[4 * i + k],
                    recv_sem=recv_sems.at[4 * i + k], device_id=to, device_id_type=MESH).start()
        token[...] = jnp.zeros_like(token)

    thru = [pltpu.HBM(a.shape, a.dtype) for a in list(srcs) + list(lands)]
    args = [_in_hbm(a) for a in srcs] + [_in_hbm(a) for a in lands]
    in_specs = [HBM_ONLY] * (ns + nl)
    if dep is not None:
        args.append(dep)
        in_specs.append(HBM_SPEC)
    outs = pl.pallas_call(
        body, name=name, in_specs=in_specs,
        out_shape=(pltpu.SemaphoreType.DMA((4 * ni,)), pltpu.SemaphoreType.DMA((4 * ni,)), *thru, _sds((8, 128), F32)),
        out_specs=(SEM_SPEC, SEM_SPEC, *[HBM_ONLY] * (ns + nl), pl.BlockSpec(memory_space=pltpu.VMEM)),
        input_output_aliases={j: 2 + j for j in range(ns + nl)}, compiler_params=IN_FLIGHT)(*args)
    return dict(send=outs[0], recv=outs[1], srcs=list(outs[2:2 + ns]), lands=list(outs[2 + ns:2 + ns + nl]),
                token=outs[-1], items=items)


def _gather_forward(name, st, after):
    nl, ni = len(st["lands"]), len(st["items"])

    def body(*refs):
        L, recv_sems = refs[0:nl], refs[nl]
        fwd_send, fwd_recv, token = refs[-3:]
        me, sib, chips = _place_and_neighbours()
        for i, (_, dst) in enumerate(st["items"]):
            for j, chip in enumerate(chips):
                blk = dst(L, _dev_index(*chip, me[2]))
                pltpu.make_async_remote_copy(
                    src_ref=blk, dst_ref=blk, send_sem=fwd_send.at[3 * i + j], recv_sem=recv_sems.at[4 * i + 1 + j],
                    device_id=me, device_id_type=MESH).wait_recv()
                pltpu.make_async_remote_copy(
                    src_ref=blk, dst_ref=blk, send_sem=fwd_send.at[3 * i + j], recv_sem=fwd_recv.at[3 * i + j],
                    device_id=sib, device_id_type=MESH).start()
        token[...] = jnp.zeros_like(token)

    after = list(after) if isinstance(after, (list, tuple)) else [after]
    outs = pl.pallas_call(
        body, name=name, in_specs=[HBM_ONLY] * nl + [SEM_SPEC] + [HBM_SPEC] * len(after),
        out_shape=(*[pltpu.HBM(a.shape, a.dtype) for a in st["lands"]], pltpu.SemaphoreType.DMA((3 * ni,)),
                   pltpu.SemaphoreType.DMA((3 * ni,)), _sds((8, 128), F32)),
        out_specs=(*[HBM_ONLY] * nl, SEM_SPEC, SEM_SPEC, pl.BlockSpec(memory_space=pltpu.VMEM)),
        input_output_aliases={j: j for j in range(nl)}, compiler_params=IN_FLIGHT)(*st["lands"], st["recv"], *after)
    return dict(st, lands=list(outs[0:nl]), fwd_send=outs[nl], fwd_recv=outs[nl + 1], token=outs[-1])


def _gather_wait(name, st, after):
    ns, nl, ni = len(st["srcs"]), len(st["lands"]), len(st["items"])

    def body(*refs):
        S, L = refs[0:ns], refs[ns:ns + nl]
        send_sems, recv_sems, fwd_send, fwd_recv = refs[ns + nl:ns + nl + 4]
        me, sib, chips = _place_and_neighbours()
        mi = _dev_index(*me)
        for i, (src, dst) in enumerate(st["items"]):
            mine = dst(L, mi)
            for k in range(4):
                pltpu.make_async_remote_copy(
                    src_ref=src(S), dst_ref=mine, send_sem=send_sems.at[4 * i + k], recv_sem=recv_sems.at[4 * i + k],
                    device_id=me, device_id_type=MESH).wait_send()
            pltpu.make_async_remote_copy(
                src_ref=src(S), dst_ref=mine, send_sem=send_sems.at[4 * i], recv_sem=recv_sems.at[4 * i],
                device_id=me, device_id_type=MESH).wait_recv()
            for j in range(3):
                cp = pltpu.make_async_remote_copy(
                    src_ref=mine, dst_ref=mine, send_sem=fwd_send.at[3 * i + j], recv_sem=fwd_recv.at[3 * i + j],
                    device_id=me, device_id_type=MESH)
                cp.wait_send()
                cp.wait_recv()

    arrays = st["srcs"] + st["lands"]
    outs = pl.pallas_call(
        body, name=name, in_specs=[HBM_ONLY] * (ns + nl) + [SEM_SPEC] * 4 + [HBM_SPEC],
        out_shape=tuple(pltpu.HBM(a.shape, a.dtype) for a in arrays), out_specs=tuple([HBM_ONLY] * (ns + nl)),
        input_output_aliases={j: j for j in range(ns + nl)},
        compiler_params=IN_FLIGHT)(*arrays, st["send"], st["recv"], st["fwd_send"], st["fwd_recv"], after)
    return list(outs[ns:ns + nl])


def _sum_slots(land):
    def body(l_ref, o_ref):
        acc = l_ref[0]
        for d in range(1, NDEV):
            acc = acc + l_ref[d]
        o_ref[...] = acc

    vm = pl.BlockSpec(memory_space=pltpu.VMEM)
    return pl.pallas_call(body, name="sum_small", out_shape=_sds(land.shape[1:], F32), in_specs=[vm], out_specs=vm)(land)


def _adam_math(w, g, m, v):
    m2 = ADAM_B1 * m + (1.0 - ADAM_B1) * g
    v2 = ADAM_B2 * v + (1.0 - ADAM_B2) * (g * g)
    delta = -ADAM_LR * ((m2 * ADAM_C1) / (jnp.sqrt(v2 * ADAM_C2) + ADAM_EPS) + ADAM_WD * w)
    return delta, m2, v2


def _adam_layer(land, sel, w, m, v, layer, name, prev=None, tc=512):
    R = land.shape[2]

    def body(l_ref, w_ref, m_ref, v_ref, *rest):
        g_out, d_out, m_out, v_out = rest[-4:]
        g = l_ref[0].astype(F32)
        for d in range(1, NDEV):
            g = g + l_ref[d].astype(F32)
        delta, m2, v2 = _adam_math(w_ref[...], g, m_ref[...], v_ref[...])
        g_out[...] = g
        d_out[...] = delta
        m_out[...] = m2
        v_out[...] = v2

    wspec = pl.BlockSpec((None, R, tc), lambda i: (layer, 0, i))
    in_specs = [pl.BlockSpec((None, NDEV, R, tc), lambda i: (sel, 0, 0, i)), wspec, wspec, wspec]
    args = [land, w, m, v]
    aliases = {}
    if prev is not None:
        in_specs += [HBM_SPEC] * 4
        args += list(prev)
        aliases = {4 + j: j for j in range(4)}
    return pl.pallas_call(
        body, name=name, grid=(D // tc,), in_specs=in_specs, out_specs=[wspec] * 4,
        out_shape=[_sds(w.shape, F32)] * 4, input_output_aliases=aliases, compiler_params=_cparams(1))(*args)


def _adam_stacked(lands, sel, w, m, v, name):
    res = None
    for layer, land in enumerate(lands):
        res = _adam_layer(land, sel, w, m, v, layer, f"{name}{layer}", prev=res)
    return res


def _adam_small(ws, gs, ms, vs):
    n = len(ws)

    def body(*refs):
        w_r, g_r, m_r, v_r = refs[0:n], refs[n:2 * n], refs[2 * n:3 * n], refs[3 * n:4 * n]
        d_o, m_o, v_o = refs[4 * n:5 * n], refs[5 * n:6 * n], refs[6 * n:7 * n]
        for i in range(n):
            delta, m2, v2 = _adam_math(w_r[i][...], g_r[i][...], m_r[i][...], v_r[i][...])
            d_o[i][...] = delta
            m_o[i][...] = m2
            v_o[i][...] = v2

    vm = pl.BlockSpec(memory_space=pltpu.VMEM)
    shapes = [_sds(w.shape, F32) for w in ws]
    outs = pl.pallas_call(body, name="adam_small", in_specs=[vm] * (4 * n), out_specs=[vm] * (3 * n),
                          out_shape=shapes * 3)(*ws, *gs, *ms, *vs)
    return outs[0:n], outs[n:2 * n], outs[2 * n:3 * n]


def _adam_of_slots(land, w, m, v, name):
    def body(l_ref, w_ref, m_ref, v_ref, g_o, d_o, m_o, v_o):
        g = l_ref[0]
        for d in range(1, NDEV):
            g = g + l_ref[d]
        g_o[...] = g
        d_o[...], m_o[...], v_o[...] = _adam_math(w_ref[...], g, m_ref[...], v_ref[...])

    vm = pl.BlockSpec(memory_space=pltpu.VMEM)
    return pl.pallas_call(body, name=name, in_specs=[vm] * 4, out_specs=[vm] * 4,
                          out_shape=[_sds(w.shape, F32)] * 4)(land, w, m, v)


WEIGHT_NAMES = ("ab_norm_g", "ab_w_in", "sgu_norm_g", "sgu_norm_b", "sgu_w", "sgu_bias", "q_norm_g", "k_norm_g",
                "ab_w_out", "cd_norm_g", "cd_w_in", "conv_c_w", "conv_c_b", "c_ln_g", "c_ln_b", "conv_d_w",
                "cd_w_out", "ffn_norm_g", "ffn_w_gate", "ffn_w_up", "ffn_w_down")
SMALL_SHAPES = (("sgu_norm_g", (1, 512)), ("sgu_norm_b", (1, 512)), ("sgu_w", (512, 128)),
                ("sgu_bias", (4, 128)), ("q_norm_g", (3, 1, 64)), ("k_norm_g", (3, 1, 64)), ("cd_norm_g", (1, 128)),
                ("conv_c_w", (31, 1, 64)), ("conv_c_b", (1, 64)), ("c_ln_g", (1, 64)), ("c_ln_b", (1, 64)),
                ("conv_d_w", (3, 1, 64)), ("ffn_norm_g", (2, 1024)))
SHARD_C = 64


def _pack_rows(parts, rows):
    flat = jnp.concatenate([p.reshape(-1) for p in parts])
    return jnp.pad(flat, (0, rows * 128 - flat.shape[0])).reshape(rows, 128)


def kernel(x, ab_norm_g, ab_w_in, sgu_norm_g, sgu_norm_b, sgu_w, sgu_bias, q_norm_g, k_norm_g, ab_w_out, cd_norm_g, cd_w_in, conv_c_w, conv_c_b, c_ln_g, c_ln_b, conv_d_w, cd_w_out, ffn_norm_g, ffn_w_gate, ffn_w_up, ffn_w_down, loss_target, m_ab_norm_g, m_ab_w_in, m_sgu_norm_g, m_sgu_norm_b, m_sgu_w, m_sgu_bias, m_q_norm_g, m_k_norm_g, m_ab_w_out, m_cd_norm_g, m_cd_w_in, m_conv_c_w, m_conv_c_b, m_c_ln_g, m_c_ln_b, m_conv_d_w, m_cd_w_out, m_ffn_norm_g, m_ffn_w_gate, m_ffn_w_up, m_ffn_w_down, v_ab_norm_g, v_ab_w_in, v_sgu_norm_g, v_sgu_norm_b, v_sgu_w, v_sgu_bias, v_q_norm_g, v_k_norm_g, v_ab_w_out, v_cd_norm_g, v_cd_w_in, v_conv_c_w, v_conv_c_b, v_c_ln_g, v_c_ln_b, v_conv_d_w, v_cd_w_out, v_ffn_norm_g, v_ffn_w_gate, v_ffn_w_up, v_ffn_w_down):
    w = dict(zip(WEIGHT_NAMES, (ab_norm_g, ab_w_in, sgu_norm_g, sgu_norm_b, sgu_w, sgu_bias, q_norm_g, k_norm_g, ab_w_out, cd_norm_g, cd_w_in, conv_c_w, conv_c_b, c_ln_g, c_ln_b, conv_d_w, cd_w_out, ffn_norm_g, ffn_w_gate, ffn_w_up, ffn_w_down)))
    m = dict(zip(WEIGHT_NAMES, (m_ab_norm_g, m_ab_w_in, m_sgu_norm_g, m_sgu_norm_b, m_sgu_w, m_sgu_bias, m_q_norm_g, m_k_norm_g, m_ab_w_out, m_cd_norm_g, m_cd_w_in, m_conv_c_w, m_conv_c_b, m_c_ln_g, m_c_ln_b, m_conv_d_w, m_cd_w_out, m_ffn_norm_g, m_ffn_w_gate, m_ffn_w_up, m_ffn_w_down)))
    v = dict(zip(WEIGHT_NAMES, (v_ab_norm_g, v_ab_w_in, v_sgu_norm_g, v_sgu_norm_b, v_sgu_w, v_sgu_bias, v_q_norm_g, v_k_norm_g, v_ab_w_out, v_cd_norm_g, v_cd_w_in, v_conv_c_w, v_conv_c_b, v_c_ln_g, v_c_ln_b, v_conv_d_w, v_cd_w_out, v_ffn_norm_g, v_ffn_w_gate, v_ffn_w_up, v_ffn_w_down)))
    me = _dev_index(*_my_place())

    r_ff = DFF // NDEV
    one = lambda a: (lambda S, j: S[a])
    slot = lambda b: (lambda L, s: L[b].at[s])
    slot2 = lambda b, part: (lambda L, s: L[b].at[part, s])
    shard = lambda a: (lambda S: S[a])

    def later(a):
        return lax.optimization_barrier((a, gathers[0]["token"]))[0]

    def layer_shards(layer):
        return (later(w["ffn_w_gate"][layer]).T.astype(BF16), later(w["ffn_w_up"][layer]).T.astype(BF16),
                later(w["ffn_w_down"][layer]).astype(BF16))

    def gathered(own):
        return _landing((NDEV,) + own.shape, BF16, [((me,), own)])

    def gathered2(a, b):
        return _landing((2, NDEV) + a.shape, BF16, [((0, me), a), ((1, me), b)])

    ab_in_s = w["ab_w_in"][0].T.astype(BF16)
    gathers = {0: _gather_start("gather0_start", [ab_in_s], [gathered(ab_in_s)], [(shard(0), slot(0))])}

    def chan(flat, lo, taps):
        return flat[:, lo:lo + taps * SHARD_C].reshape(NDEV, taps, SHARD_C).transpose(1, 0, 2).reshape(taps, 512)

    def fetch(stage, after):
        if stage == "ab_in":
            ab_out_s = later(w["ab_w_out"][0]).astype(BF16)
            gate0, up0, down0 = layer_shards(0)
            small_s = _pack_rows([later(w[n]) for n in ("cd_norm_g", "conv_c_w", "conv_c_b", "c_ln_g", "c_ln_b",
                                                        "conv_d_w")], 24)
            lands1 = [gathered(ab_out_s), gathered2(gate0, up0), gathered(down0),
                      _landing((NDEV,) + small_s.shape, F32, [((me,), small_s)])]
            gathers[0] = _gather_forward("gather0_forward", gathers[0], [after] + lands1)
            l_ab_in, = _gather_wait("gather0_wait", gathers[0], gathers[0]["token"])
            gathers[1] = _gather_start(
                "gather1_start", [ab_out_s, gate0, up0, down0, small_s], lands1,
                [(shard(0), slot(0)), (shard(1), slot2(1, 0)), (shard(2), slot2(1, 1)), (shard(3), slot(2)),
                 (shard(4), slot(3))], dep=l_ab_in)
            return {"wt_ab_in": l_ab_in.reshape(AB_IN, D), "dep0": gathers[1]["token"]}
        if stage == "attn0":
            cd_in_s, cd_out_s = later(w["cd_w_in"][0]).T.astype(BF16), later(w["cd_w_out"][0]).astype(BF16)
            gate1, up1, down1 = layer_shards(1)
            gathers[2] = _gather_start(
                "gather2_start", [cd_in_s, cd_out_s, gate1, up1, down1],
                [gathered(cd_in_s), gathered(cd_out_s), gathered2(gate1, up1), gathered(down1)],
                [(shard(0), slot(0)), (shard(1), slot(1)), (shard(2), slot2(2, 0)), (shard(3), slot2(2, 1)),
                 (shard(4), slot(3))], dep=after)
            return {"dep_attn1": gathers[2]["token"]}
        if stage == "attn1":
            gathers[1] = _gather_forward("gather1_forward", gathers[1], after)
            return {"dep_attn2": gathers[1]["token"]}
        if stage == "ab_out":
            l_out, l_ffn, l_down, l_small = _gather_wait("gather1_wait", gathers[1], after)
            flat = l_small.reshape(NDEV, 24 * 128)
            return {
                "w_ab_out": l_out.reshape(D, D), "wt_ffn_in0": l_ffn.reshape(2 * DFF, D),
                "w_ffn_down0": l_down.reshape(DFF, D), "cd_norm_g": flat[:, 0:128].reshape(1, D),
                "conv_c_w32": jnp.pad(chan(flat, 128, CONV_C_TAPS), ((0, 1), (0, 0))),
                "conv_c_b": chan(flat, 2112, 1), "c_ln_g": chan(flat, 2176, 1), "c_ln_b": chan(flat, 2240, 1),
                "conv_d_w8": jnp.pad(chan(flat, 2304, CONV_D_TAPS), ((0, 8 - CONV_D_TAPS), (0, 0))),
            }
        if stage == "ffn_down0":
            gathers[2] = _gather_forward("gather2_forward", gathers[2], after)
            return {"dep_down0": gathers[2]["token"]}
        if stage == "cd_in":
            l_in, l_out, l_ffn, l_down = _gather_wait("gather2_wait", gathers[2], after)
            return {"wt_cd_in": l_in.reshape(CD_IN, D), "w_cd_out": l_out.reshape(D, D),
                    "wt_ffn_in1": l_ffn.reshape(2 * DFF, D), "w_ffn_down1": l_down.reshape(DFF, D)}
        return {}

    scatters = {}
    rides_with = {"w_ffn_down1": "wt_ffn_in1", "w_cd_out": "wt_cd_in", "w_ffn_down0": "wt_ffn_in0"}
    held = {}
    smalls = {}

    def small_exchange(name, block):
        land = _landing((NDEV,) + block.shape, F32, [((me,), block)])
        return _exchange_start(name, [block], [land], [(one(0), slot(0))])

    def on_grad(key, arr):
        if key == "small":
            parts = [arr["sgu_norm_g"], arr["sgu_norm_b"], arr["sgu_w"], arr["sgu_bias"], arr["q_norm_g"],
                     arr["k_norm_g"], arr["cd_norm_g"], arr["conv_c_w32"][:CONV_C_TAPS], arr["conv_c_b"], arr["c_ln_g"],
                     arr["c_ln_b"], arr["conv_d_w8"][:CONV_D_TAPS], arr["ffn_norm_g0"], arr["ffn_norm_g1"],
                     arr["loss_cols"]]
            smalls["sizes"] = [p.size for p in parts]
            rows = -(-sum(smalls["sizes"]) // 1024) * 8
            smalls["early"] = small_exchange("small_start", _pack_rows(parts, rows))
            return smalls["early"]["token"]
        if key in rides_with:
            held[rides_with[key]] = (key, arr)
            return None
        group = ([held.pop(key)] if key in held else []) + [(key, arr)]
        srcs, lands, items = [], [], []
        for n, (k, a) in enumerate(group):
            if k.startswith("wt_ffn_in"):
                src = a.reshape(2, NDEV, r_ff, D)
                own = lax.dynamic_slice_in_dim(src, me, 1, axis=1)
                lands.append(lax.dynamic_update_slice(lax.empty(src.shape, BF16), own, (0, me, 0, 0)))
                items += [((lambda S, j, n=n: S[n].at[0, j]), slot2(n, 0)), ((lambda S, j, n=n: S[n].at[1, j]), slot2(n, 1))]
            else:
                rows = a.shape[0] // NDEV
                src = a.reshape(NDEV, rows, D)
                own = lax.dynamic_index_in_dim(src, me, 0, keepdims=False)
                lands.append(_landing((1, NDEV, rows, D), BF16, [((0, me), own)]))
                items.append(((lambda S, j, n=n: S[n].at[j]), slot2(n, 0)))
            srcs.append(src)
        st = _exchange_start(f"scatter_{key}_start", srcs, lands, items)
        scatters[key] = (st, [k for k, _ in group])
        return st["token"]

    W = {
        "dep_first": gathers[0]["token"],
        "ab_norm_g": w["ab_norm_g"], "sgu_norm_g": w["sgu_norm_g"], "sgu_norm_b": w["sgu_norm_b"],
        "sgu_w": w["sgu_w"][0], "sgu_bias": w["sgu_bias"][0], "q_norm_g": w["q_norm_g"][0],
        "k_norm_g": w["k_norm_g"][0], "ffn_norm_g": w["ffn_norm_g"],
    }

    loss_cols, grad_x, G = _local_step(x[0], loss_target[0], W, fetch, on_grad)

    late_small = small_exchange("small_late_start", G["ab_norm_g"])
    landed = {}

    def wait_scatters(name, group_keys, others, after):
        res = _exchange_wait(name, [scatters[gk][0] for gk in group_keys] + others, after)
        for gk, lands in zip(group_keys, res):
            landed.update(zip(scatters[gk][1], lands))
        return [lands[0] for lands in res[len(group_keys):]]

    small_land, = wait_scatters("scatter_wait_early", ["wt_ffn_in1", "wt_cd_in", "wt_ffn_in0", "w_ab_out"],
                                [smalls["early"]], late_small["token"])

    grads, deltas, new_m, new_v = {}, {}, {}, {}
    done = []

    def put(name, res):
        grads[name], deltas[name], new_m[name], new_v[name] = res

    def adam(name, lands, sel, transposed):
        flip = (lambda a: jnp.swapaxes(a, 1, 2)) if transposed else (lambda a: a)
        res = _adam_stacked(lands, sel, flip(w[name]), flip(m[name]), flip(v[name]), f"adam_{name}")
        done.append(res[1])
        put(name, [flip(r) for r in res])

    ffn_in_lands = [landed["wt_ffn_in0"], landed["wt_ffn_in1"]]
    adam("cd_w_in", [landed["wt_cd_in"]], 0, True)
    adam("ffn_w_gate", ffn_in_lands, 0, True)
    adam("ffn_w_up", ffn_in_lands, 1, True)
    adam("cd_w_out", [landed["w_cd_out"]], 0, False)
    adam("ab_w_out", [landed["w_ab_out"]], 0, False)
    adam("ffn_w_down", [landed["w_ffn_down0"], landed["w_ffn_down1"]], 0, False)

    red = _sum_slots(small_land).reshape(-1)
    offs = [0]
    for s in smalls["sizes"]:
        offs.append(offs[-1] + s)
    seg = [red[offs[i]:offs[i + 1]] for i in range(len(smalls["sizes"]))]
    loss = jnp.sum(seg[14])

    def own_channels(full, taps):
        return lax.dynamic_slice_in_dim(full.reshape(taps, 512), me * SHARD_C, SHARD_C, axis=1)

    g_small = {
        "sgu_norm_g": seg[0].reshape(1, 512), "sgu_norm_b": seg[1].reshape(1, 512),
        "sgu_w": seg[2].reshape(512, 128), "sgu_bias": seg[3].reshape(4, 128), "q_norm_g": seg[4].reshape(3, 64),
        "k_norm_g": seg[5].reshape(3, 64),
        "cd_norm_g": lax.dynamic_slice_in_dim(seg[6].reshape(1, D), me * (D // NDEV), D // NDEV, axis=1),
        "conv_c_w": own_channels(seg[7], CONV_C_TAPS), "conv_c_b": own_channels(seg[8], 1),
        "c_ln_g": own_channels(seg[9], 1), "c_ln_b": own_channels(seg[10], 1),
        "conv_d_w": own_channels(seg[11], CONV_D_TAPS),
        "ffn_norm_g": jnp.concatenate([seg[12].reshape(1, D), seg[13].reshape(1, D)], axis=0),
    }

    def small_in(s, a):
        return jnp.swapaxes(a, 0, 1) if len(s) == 3 else a.reshape(s)

    def small_out(n, s, a):
        return jnp.swapaxes(a, 0, 1) if len(s) == 3 else a.reshape(w[n].shape)

    g_in = [g_small[n].reshape(s) for n, s in SMALL_SHAPES]
    d_s, m_s, v_s = _adam_small([small_in(s, w[n]) for n, s in SMALL_SHAPES], g_in,
                                [small_in(s, m[n]) for n, s in SMALL_SHAPES],
                                [small_in(s, v[n]) for n, s in SMALL_SHAPES])
    for i, (n, s) in enumerate(SMALL_SHAPES):
        grads[n], deltas[n] = small_out(n, s, g_in[i]), small_out(n, s, d_s[i])
        new_m[n], new_v[n] = small_out(n, s, m_s[i]), small_out(n, s, v_s[i])
    done.append(d_s[0])

    late_land, = wait_scatters("scatter_wait_last", ["wt_ab_in"], [late_small], list(done))
    put("ab_norm_g", _adam_of_slots(late_land, w["ab_norm_g"], m["ab_norm_g"], v["ab_norm_g"], "adam_ab_norm_g"))
    adam("ab_w_in", [landed["wt_ab_in"]], 0, True)

    return (loss, grad_x[None], *[grads[n] for n in WEIGHT_NAMES], *[deltas[n] for n in WEIGHT_NAMES],
            *[new_m[n] for n in WEIGHT_NAMES], *[new_v[n] for n in WEIGHT_NAMES])
```

```python
import jax
import jax.numpy as jnp
import numpy as np
from jax import lax
from jax.experimental import pallas as pl
from jax.experimental.pallas import tpu as pltpu

F32 = jnp.float32
BF16 = jnp.bfloat16

T = 4096
D = 1024
NDEV = 8
EPS = 1e-6
NEG_INF = -1e30
DFF = 2816
AB_IN = 5632
CD_IN = 2560
HEAD = 64
PAIR = 128
NPAIR = 4
NBACK = 128
DIL_RATES = (1, 4, 16)
ROPE_HALF = 8
ROPE_THETA = 500000.0
CONV_C_TAPS = 31
CONV_D_TAPS = 3
HALO = 32
ATTN_BWD_UNROLL = 4
MAX_ROW_STRIDE = 4

ADAM_LR = 0.001
ADAM_B1 = 0.9
ADAM_B2 = 0.999
ADAM_EPS = 1e-08
ADAM_WD = 0.01
ADAM_STEP = 10
ADAM_C1 = 1.0 / (1.0 - ADAM_B1 ** ADAM_STEP)
ADAM_C2 = 1.0 / (1.0 - ADAM_B2 ** ADAM_STEP)

VMEM_LIMIT_MB = 48
MESH = pl.DeviceIdType.MESH
HBM_SPEC = pl.BlockSpec(memory_space=pl.ANY)


def _cparams(ngrid, vmem_mb=VMEM_LIMIT_MB):
    return pltpu.CompilerParams(dimension_semantics=("arbitrary",) * ngrid,
                                vmem_limit_bytes=vmem_mb * 1024 * 1024)


def _pick(n, options):
    for o in options:
        if n % o == 0:
            return o
    raise ValueError(f"no tile for {n} in {options}")


def _sds(shape, dtype):
    return jax.ShapeDtypeStruct(shape, dtype)


def _sigmoid(x):
    return 1.0 / (1.0 + jnp.exp(-x))


def _sigmoid_bf16(x):
    return 0.5 * jnp.tanh(0.5 * x) + 0.5


def _gelu(z):
    return 0.5 * z * (1.0 + lax.erf(z * 0.7071067811865476))


def _gelu_grad(z):
    return 0.5 * (1.0 + lax.erf(z * 0.7071067811865476)) + z * jnp.exp(-0.5 * z * z) * 0.3989422804014327


def _mm_nt(a, wt, name, out_dtype=BF16, dep=None):
    M, K = a.shape
    N = wt.shape[0]
    tn = _pick(N, (512, 256))

    def body(a_ref, w_ref, *rest):
        o_ref = rest[-1]
        for r0 in range(0, M, 1024):
            o_ref[r0:r0 + 1024, :] = lax.dot_general(
                a_ref[r0:r0 + 1024, :], w_ref[...], (((1,), (1,)), ((), ())),
                preferred_element_type=F32).astype(o_ref.dtype)

    in_specs = [pl.BlockSpec((M, K), lambda j: (0, 0), pipeline_mode=pl.Buffered(1)),
                pl.BlockSpec((tn, K), lambda j: (j, 0))]
    args = [a, wt]
    if dep is not None:
        in_specs.append(HBM_SPEC)
        args.append(dep)
    return pl.pallas_call(
        body, name=name, grid=(N // tn,), in_specs=in_specs, out_specs=pl.BlockSpec((M, tn), lambda j: (0, j)),
        out_shape=_sds((M, N), out_dtype), compiler_params=_cparams(1))(*args)


EPI_ROWS = 256


def _mm_nt_rows(a, wt, name, epilogue, side, side_specs, out_specs, out_shape, sums=(), dep=None, tm=512):
    M, K = a.shape
    N = wt.shape[0]
    ns, no = len(side), len(out_shape)

    def body(a_ref, w_ref, *rest):
        side_refs, outs, acc = rest[0:ns], rest[-1 - no:-1], rest[-1]
        acc[...] = lax.dot_general(a_ref[...], w_ref[...], (((1,), (1,)), ((), ())), preferred_element_type=F32)

        @pl.when(pl.program_id(0) == 0)
        def _():
            for j in sums:
                outs[j][...] = jnp.zeros_like(outs[j])

        for r0 in range(0, tm, EPI_ROWS):
            rows = slice(r0, r0 + EPI_ROWS)
            epilogue(acc[rows, :].astype(BF16).astype(F32), rows, side_refs, outs)

    in_specs = [pl.BlockSpec((tm, K), lambda i: (i, 0)),
                pl.BlockSpec((N, K), lambda i: (0, 0), pipeline_mode=pl.Buffered(1))] + list(side_specs)
    args = [a, wt, *side]
    if dep is not None:
        in_specs.append(HBM_SPEC)
        args.append(dep)
    return pl.pallas_call(
        body, name=name, grid=(M // tm,), in_specs=in_specs, out_specs=list(out_specs), out_shape=list(out_shape),
        scratch_shapes=[pltpu.VMEM((tm, N), F32)], compiler_params=_cparams(1))(*args)


def _mm_nn(a, w, name, mode, resid, gain=None, tgt=None, dep=None, tm=512):
    M, K = a.shape
    N = w.shape[1]
    side = gain if mode == "rms" else tgt
    ring = N * 4 >= K * 2

    def body(a_ref, w_ref, resid_ref, side_ref, *rest):
        i = pl.program_id(0)
        if ring:
            resid_hbm = resid_ref
            resid_ref = _ring_tile(lambda step: resid_hbm.at[pl.ds(pl.multiple_of(step * tm, tm), tm), :],
                                   rest[-2], rest[-1], i, M // tm)
            rest = rest[:-2]
        outs, acc = rest[-3 if mode == "rms" else -4:-1], rest[-1]
        acc[...] = jnp.dot(a_ref[...], w_ref[...], preferred_element_type=F32)

        if mode == "loss":
            @pl.when(i == 0)
            def _():
                outs[2][...] = jnp.zeros_like(outs[2])

        for r0 in range(0, tm, EPI_ROWS):
            rows = slice(r0, r0 + EPI_ROWS)
            v = acc[rows, :] + resid_ref[rows, :]
            if mode == "rms":
                outs[0][rows, :] = v
                r = lax.rsqrt(jnp.mean(v * v, axis=-1, keepdims=True) + EPS)
                outs[1][rows, :] = (v * r * side_ref[...]).astype(BF16)
            else:
                d = v - side_ref[rows, :]
                outs[2][...] += jnp.sum(d * d, axis=0, keepdims=True) * (0.5 / N)
                dy = d * (1.0 / N)
                outs[0][rows, :] = dy
                outs[1][rows, :] = dy.astype(BF16)

    row = pl.BlockSpec((tm, N), lambda i: (i, 0))
    vec = pl.BlockSpec((1, N), lambda i: (0, 0))
    in_specs = [pl.BlockSpec((tm, K), lambda i: (i, 0)),
                pl.BlockSpec((K, N), lambda i: (0, 0), pipeline_mode=pl.Buffered(1)), HBM_SPEC if ring else row,
                vec if mode == "rms" else row]
    args = [a, w, resid, side]
    if dep is not None:
        in_specs.append(HBM_SPEC)
        args.append(dep)
    if mode == "rms":
        out_specs, out_shape = [row, row], [_sds((M, N), F32), _sds((M, N), BF16)]
    else:
        out_specs, out_shape = [row, row, vec], [_sds((M, N), F32), _sds((M, N), BF16), _sds((1, N), F32)]
    return pl.pallas_call(
        body, name=name, grid=(M // tm,), in_specs=in_specs, out_specs=out_specs, out_shape=out_shape,
        scratch_shapes=[pltpu.VMEM((tm, N), F32)] + (_ring_scratch((tm, N), F32) if ring else []),
        compiler_params=_cparams(1))(*args)


def _mm_dh_rms_bwd(a, w, x, gain, dres, name, dep=None, tm=512, bf16_copy=True):
    parts = a.shape[0] if a.ndim == 3 else 1
    M, Kp = a.shape[-2], a.shape[-1]
    N = w.shape[1]
    nblk = M // tm
    assert nblk % 2 == 0

    def body(a_ref, w_ref, x_ref, g_ref, dres_ref, *rest):
        dg_ref, acc0, acc1 = rest[-3:]
        dx_ref = rest[-5] if bf16_copy else rest[-4]
        dxb_ref = rest[-4] if bf16_copy else None
        i = pl.program_id(0)

        def matmul(acc):
            if parts == 1:
                acc[...] = jnp.dot(a_ref[...], w_ref[...], preferred_element_type=F32)
            else:
                d = jnp.dot(a_ref[0], w_ref[0:Kp, :], preferred_element_type=F32)
                for p in range(1, parts):
                    d = d + jnp.dot(a_ref[p], w_ref[p * Kp:(p + 1) * Kp, :], preferred_element_type=F32)
                acc[...] = d

        def finish(acc):
            for r0 in range(0, tm, EPI_ROWS // 2):
                rows = slice(r0, r0 + EPI_ROWS // 2)
                v = acc[rows, :]
                xf = x_ref[rows, :]
                r = lax.rsqrt(jnp.mean(xf * xf, axis=-1, keepdims=True) + EPS)
                xhat = xf * r
                dg_ref[...] += jnp.sum(v * xhat, axis=0, keepdims=True)
                dxh = v * g_ref[...]
                tot = dres_ref[rows, :] + r * (dxh - xhat * jnp.mean(dxh * xhat, axis=-1, keepdims=True))
                dx_ref[rows, :] = tot
                if bf16_copy:
                    dxb_ref[rows, :] = tot.astype(BF16)

        @pl.when(i == 0)
        def _():
            dg_ref[...] = jnp.zeros_like(dg_ref)
            matmul(acc0)

        @pl.when((i > 0) & (i < nblk) & (i % 2 == 1))
        def _():
            matmul(acc1)
            finish(acc0)

        @pl.when((i > 0) & (i < nblk) & (i % 2 == 0))
        def _():
            matmul(acc0)
            finish(acc1)

        @pl.when(i == nblk)
        def _():
            finish(acc1)

    last = nblk - 1
    row = pl.BlockSpec((tm, N), lambda i: (jnp.maximum(i - 1, 0), 0))
    vec = pl.BlockSpec((1, N), lambda i: (0, 0))
    if a.ndim == 3:
        a_spec = pl.BlockSpec((parts, tm, Kp), lambda i: (0, jnp.minimum(i, last), 0))
    else:
        a_spec = pl.BlockSpec((tm, Kp), lambda i: (jnp.minimum(i, last), 0))
    w_spec = pl.BlockSpec((parts * Kp, N), lambda i: (0, 0), pipeline_mode=pl.Buffered(1))
    in_specs = [a_spec, w_spec, row, vec, row]
    args = [a, w, x, gain, dres]
    if dep is not None:
        in_specs.append(HBM_SPEC)
        args.append(dep)
    return pl.pallas_call(
        body, name=name, grid=(nblk + 1,), in_specs=in_specs,
        out_specs=[row, row, vec] if bf16_copy else [row, vec],
        out_shape=([_sds((M, N), F32), _sds((M, N), BF16), _sds((1, N), F32)] if bf16_copy
                   else [_sds((M, N), F32), _sds((1, N), F32)]),
        scratch_shapes=[pltpu.VMEM((tm, N), F32), pltpu.VMEM((tm, N), F32)], compiler_params=_cparams(1, 56))(*args)


def _mm_tn(a, b, name, out_dtype=BF16, tt=2048, dep=None):
    parts = a.shape[0] if a.ndim == 3 else 1
    Tt, Mp = a.shape[-2], a.shape[-1]
    N = b.shape[1]
    tn = _pick(Mp, (1408, 1280, 1024, 512))
    jper = Mp // tn
    nt = Tt // tt

    def body(a_ref, b_ref, *rest):
        o_ref, acc = rest[-2:]
        t = pl.program_id(1)

        @pl.when(t == 0)
        def _():
            acc[...] = jnp.zeros_like(acc)

        rows = pl.ds(pl.multiple_of(t * tt, tt), tt)
        acc[...] += lax.dot_general(a_ref[...], b_ref[rows, :], (((0,), (0,)), ((), ())),
                                    preferred_element_type=F32)

        @pl.when(t == nt - 1)
        def _():
            o_ref[...] = acc[...].astype(o_ref.dtype)

    if a.ndim == 3:
        a_spec = pl.BlockSpec((None, tt, tn), lambda j, t: (j // jper, t, j % jper))
    else:
        a_spec = pl.BlockSpec((tt, tn), lambda j, t: (t, j))
    in_specs = [a_spec, pl.BlockSpec((Tt, N), lambda j, t: (0, 0), pipeline_mode=pl.Buffered(1))]
    args = [a, b]
    if dep is not None:
        in_specs.append(HBM_SPEC)
        args.append(dep)
    return pl.pallas_call(
        body, name=name, grid=(parts * jper, nt), in_specs=in_specs,
        out_specs=pl.BlockSpec((tn, N), lambda j, t: (j, 0)),
        out_shape=_sds((parts * Mp, N), out_dtype), scratch_shapes=[pltpu.VMEM((tn, N), F32)],
        compiler_params=_cparams(2))(*args)


FFN_ROWS = 256
PREFETCH_SLOTS = 3


def _ring_tile(window, buf, sem, j, nj):
    def fetch(step):
        slot = step % PREFETCH_SLOTS
        return pltpu.make_async_copy(window(step), buf.at[slot], sem.at[slot])

    @pl.when(j == 0)
    def _():
        for s in range(PREFETCH_SLOTS - 1):
            fetch(s).start()

    @pl.when(j + PREFETCH_SLOTS - 1 < nj)
    def _():
        fetch(j + PREFETCH_SLOTS - 1).start()

    fetch(j).wait()
    return buf.at[j % PREFETCH_SLOTS]


def _ring_scratch(tile_shape, dtype):
    return [pltpu.VMEM((PREFETCH_SLOTS,) + tuple(tile_shape), dtype), pltpu.SemaphoreType.DMA((PREFETCH_SLOTS,))]


def _ffn_in(h, wt_in, name, tn=256):
    nj = DFF // tn

    def body(h_ref, wg_ref, wu_ref, p_ref, act_ref):
        nt = (((1,), (1,)), ((), ()))
        for r0 in range(0, T, FFN_ROWS):
            rows = slice(r0, r0 + FFN_ROWS)
            g = lax.dot_general(h_ref[rows, :], wg_ref[...], nt, preferred_element_type=F32).astype(BF16)
            u = lax.dot_general(h_ref[rows, :], wu_ref[...], nt, preferred_element_type=F32).astype(BF16)
            p_ref[0, rows, :] = g
            p_ref[1, rows, :] = u
            act_ref[rows, :] = g * _sigmoid_bf16(g) * u

    return pl.pallas_call(
        body, name=name, grid=(nj,),
        in_specs=[pl.BlockSpec((T, D), lambda j: (0, 0), pipeline_mode=pl.Buffered(1)),
                  pl.BlockSpec((tn, D), lambda j: (j, 0)), pl.BlockSpec((tn, D), lambda j: (j + nj, 0))],
        out_specs=[pl.BlockSpec((2, T, tn), lambda j: (0, 0, j)), pl.BlockSpec((T, tn), lambda j: (0, j))],
        out_shape=[_sds((2, T, DFF), BF16), _sds((T, DFF), BF16)], compiler_params=_cparams(1))(h, wt_in, wt_in)


def _ffn_dact(dyb, w_down, p3, name, tn=256, dep=None):
    nj = DFF // tn

    def body(dy_ref, w_ref, p_hbm, *rest):
        o_ref, p_buf, sem = rest[-3:]
        p_ref = _ring_tile(lambda step: p_hbm.at[:, :, pl.ds(pl.multiple_of(step * tn, tn), tn)], p_buf, sem,
                           pl.program_id(0), nj)
        for r0 in range(0, T, FFN_ROWS):
            rows = slice(r0, r0 + FFN_ROWS)
            da = lax.dot_general(dy_ref[rows, :], w_ref[...], (((1,), (1,)), ((), ())),
                                 preferred_element_type=F32).astype(BF16)
            g = p_ref[0, rows, :]
            u = p_ref[1, rows, :]
            sg = _sigmoid_bf16(g)
            gs = g * sg
            o_ref[0, rows, :] = (da * u) * (sg + gs * (1.0 - sg))
            o_ref[1, rows, :] = da * gs

    in_specs = [pl.BlockSpec((T, D), lambda j: (0, 0), pipeline_mode=pl.Buffered(1)),
                pl.BlockSpec((tn, D), lambda j: (j, 0)), HBM_SPEC]
    args = [dyb, w_down, p3]
    if dep is not None:
        in_specs.append(HBM_SPEC)
        args.append(dep)
    return pl.pallas_call(
        body, name=name, grid=(nj,), in_specs=in_specs, out_specs=pl.BlockSpec((2, T, tn), lambda j: (0, 0, j)),
        out_shape=_sds((2, T, DFF), BF16), scratch_shapes=_ring_scratch((2, T, tn), BF16),
        compiler_params=_cparams(1))(*args)


def _rms_fwd(x, g, name, tm=512, dep=None):
    def body(x_ref, g_ref, *rest):
        h_ref = rest[-1]
        xf = x_ref[...]
        r = lax.rsqrt(jnp.mean(xf * xf, axis=-1, keepdims=True) + EPS)
        h_ref[...] = (xf * r * g_ref[...]).astype(BF16)

    in_specs = [pl.BlockSpec((tm, D), lambda i: (i, 0)), pl.BlockSpec((1, D), lambda i: (0, 0))]
    args = [x, g]
    if dep is not None:
        in_specs.append(HBM_SPEC)
        args.append(dep)
    return pl.pallas_call(
        body, name=name, grid=(T // tm,), in_specs=in_specs, out_specs=pl.BlockSpec((tm, D), lambda i: (i, 0)),
        out_shape=_sds((T, D), BF16), compiler_params=_cparams(1))(*args)


def _tril_mask():
    r = lax.broadcasted_iota(jnp.int32, (128, 128), 0)
    c = lax.broadcasted_iota(jnp.int32, (128, 128), 1)
    return r >= c


def _mix_a_fwd(pab, sgu_g, sgu_b, sgu_w, sgu_bias3, tm=512):
    def body(zu_ref, zv_ref, g_ref, b_ref, w_ref, bias_ref, o_ref):
        u = _gelu(zu_ref[...].astype(F32))
        v = _gelu(zv_ref[...].astype(F32))
        mu = jnp.mean(v, axis=-1, keepdims=True)
        vc = v - mu
        rstd = lax.rsqrt(jnp.mean(vc * vc, axis=-1, keepdims=True) + EPS)
        vn = (vc * rstd * g_ref[...] + b_ref[...]).astype(BF16)
        tri = _tril_mask()
        for gi in range(4):
            wg = jnp.where(tri, w_ref[gi], 0.0).astype(BF16)
            bg = bias_ref[gi]
            for c in range(tm // 128):
                rs, cs = slice(c * 128, (c + 1) * 128), slice(gi * 128, (gi + 1) * 128)
                mixed = jnp.dot(wg, vn[rs, cs], preferred_element_type=F32) + bg
                o_ref[rs, cs] = (u[rs, cs] * mixed).astype(BF16)

    half = pl.BlockSpec((tm, 512), lambda i: (i, 0))
    return pl.pallas_call(
        body, name="mix_a_fwd", grid=(T // tm,),
        in_specs=[half, pl.BlockSpec((tm, 512), lambda i: (i, 1)),
                  pl.BlockSpec((1, 512), lambda i: (0, 0)), pl.BlockSpec((1, 512), lambda i: (0, 0)),
                  pl.BlockSpec((4, 128, 128), lambda i: (0, 0, 0)), pl.BlockSpec((4, 128, 1), lambda i: (0, 0, 0))],
        out_specs=half, out_shape=_sds((T, D), BF16), compiler_params=_cparams(1),
    )(pab, pab, sgu_g, sgu_b, sgu_w, sgu_bias3)


def _rope_tables():
    pos = np.arange(T, dtype=np.float32)
    inv_freq = np.float32(ROPE_THETA) ** (-np.arange(ROPE_HALF, dtype=np.float32) * np.float32(2.0 / (2 * ROPE_HALF)))
    ang = (pos[:, None] * inv_freq[None, :]).astype(np.float32)
    cos, sin = np.cos(ang), np.sin(ang)
    z8 = np.zeros((T, ROPE_HALF), np.float32)
    rest = np.zeros((T, HEAD - 2 * ROPE_HALF), np.float32)
    c64 = np.concatenate([cos, cos, rest + 1.0], axis=1)
    s1 = np.concatenate([z8, sin, rest], axis=1)
    s2 = np.concatenate([-sin, z8, rest], axis=1)
    return tuple(jnp.asarray(np.tile(t, (1, 2)).astype(np.float32)) for t in (c64, s1, s2))


def _lo_mask(shape):
    return lax.broadcasted_iota(jnp.int32, shape, 1) < HEAD


def _seg_mean(x, lo):
    s_all = jnp.sum(x, axis=-1, keepdims=True)
    s_lo = jnp.sum(jnp.where(lo, x, 0.0), axis=-1, keepdims=True)
    return jnp.where(lo, s_lo, s_all - s_lo) * (1.0 / HEAD)


def _head_blocks():
    r = lax.broadcasted_iota(jnp.int32, (PAIR, PAIR), 0) < HEAD
    c = lax.broadcasted_iota(jnp.int32, (PAIR, PAIR), 1) < HEAD
    return jnp.where(r == c, 1.0, 0.0).astype(BF16)


def _seg_mean_mxu(x, blocks):
    return jnp.dot(x.astype(BF16), blocks, preferred_element_type=F32) * (1.0 / HEAD)


def _rope(n, c, s1, s2):
    return n * c + pltpu.roll(n, ROPE_HALF, 1) * s1 + pltpu.roll(n, PAIR - ROPE_HALF, 1) * s2


def _rope_t(dy, c, s1, s2):
    return dy * c - pltpu.roll(dy, PAIR - ROPE_HALF, 1) * s2 - pltpu.roll(dy, ROPE_HALF, 1) * s1


def _prep_fwd(pab, qg, kg, tabs, tm=512):
    def body(p_hbm, qg_ref, kg_ref, c_ref, s1_ref, s2_ref, *rest):
        outs, (p_buf, sem) = rest[:-2], rest[-2:]
        p_ref = _ring_tile(lambda step: p_hbm.at[pl.ds(pl.multiple_of(step * tm, tm), tm), :], p_buf, sem,
                           pl.program_id(0), T // tm)
        blocks = _head_blocks()
        c, s1, s2 = c_ref[...], s1_ref[...], s2_ref[...]
        for g in range(3):
            qn_ref, kn_ref, v_ref = outs[3 * g:3 * g + 3]
            for p in range(NPAIR):
                for which, gains, dst in ((0, qg_ref, qn_ref), (1, kg_ref, kn_ref)):
                    col = (2 + 3 * which + g) * 512 + p * PAIR
                    xr = p_ref[:, col:col + PAIR].astype(F32)
                    rinv = lax.rsqrt(_seg_mean_mxu(xr * xr, blocks) + EPS)
                    outs[9 + 2 * g + which][p] = rinv.astype(BF16)
                    dst[p] = _rope(xr * rinv * gains[g:g + 1, :], c, s1, s2)
                col = (8 + g) * 512 + p * PAIR
                v_ref[p] = p_ref[:, col:col + PAIR].astype(F32)

    pm = pl.BlockSpec((NPAIR, tm, PAIR), lambda i: (0, i, 0))
    tab = pl.BlockSpec((tm, PAIR), lambda i: (i, 0))
    gain = pl.BlockSpec((3, PAIR), lambda i: (0, 0))
    res = pl.pallas_call(
        body, name="prep_fwd", grid=(T // tm,),
        in_specs=[HBM_SPEC, gain, gain, tab, tab, tab],
        out_specs=[pm] * 15, out_shape=[_sds((NPAIR, T, PAIR), F32)] * 9 + [_sds((NPAIR, T, PAIR), BF16)] * 6,
        scratch_shapes=_ring_scratch((tm, AB_IN), BF16), compiler_params=_cparams(1, 56))(pab, qg, kg, *tabs)
    return res[0:9], res[9:15]


def _res_index(it, rate):
    window = NBACK * rate
    b = it // rate
    rho = it % rate
    start = b * window + rho
    startp = jnp.maximum(start - window, rho)
    kmin = jnp.where(b > 0, 0, NBACK)
    return start, startp, kmin


def _rows(start, rate):
    if rate == 1:
        return pl.ds(pl.multiple_of(start, NBACK), NBACK)
    return pl.ds(start, NBACK, stride=rate)


def _band_bias():
    qs = lax.broadcasted_iota(jnp.int32, (2 * NBACK, 2 * NBACK), 0)
    kj = lax.broadcasted_iota(jnp.int32, (2 * NBACK, 2 * NBACK), 1)
    dist = (qs & (NBACK - 1)) + NBACK - kj
    both = (dist >= 0) & (dist <= NBACK)
    return jnp.where(both, 0.0, NEG_INF), jnp.where(both & (kj >= NBACK), 0.0, NEG_INF)


def _attn_fwd_block(q, kcat, vcat, first, lo, biases):
    vcat1 = jnp.concatenate([vcat, jnp.ones((2 * NBACK, PAIR), BF16)], axis=1)
    q2 = jnp.concatenate([jnp.where(lo, q, 0.0), jnp.where(lo, 0.0, q)], axis=0).astype(BF16)
    s = lax.dot_general(q2, kcat, (((1,), (1,)), ((), ())), preferred_element_type=F32)
    s = s + jnp.where(first, biases[1], biases[0])
    m = jnp.max(s, axis=-1, keepdims=True)
    ol = jnp.dot(jnp.exp(s - m).astype(BF16), vcat1, preferred_element_type=F32)
    o2 = ol[:, 0:PAIR] / ol[:, PAIR:]
    ls = m + jnp.log(ol[:, PAIR:])
    return jnp.where(lo, o2[0:NBACK], o2[NBACK:]), jnp.where(lo, ls[0:NBACK], ls[NBACK:])


def _attn_fwd(qn, kn, v, rate, name, dep=None):
    if rate > MAX_ROW_STRIDE:
        return _attn_fwd_gathered(qn, kn, v, rate, name, dep)

    def body(q_ref, k_ref, v_ref, *rest):
        o_ref, l_ref = rest[-2:]
        lo = _lo_mask((NBACK, PAIR))
        biases = _band_bias()

        def step(it, carry):
            start, startp, kmin = _res_index(it, rate)
            q = q_ref[_rows(start, rate), :] * (HEAD ** -0.5)
            kcat = jnp.concatenate([k_ref[_rows(startp, rate), :], k_ref[_rows(start, rate), :]], axis=0).astype(BF16)
            vcat = jnp.concatenate([v_ref[_rows(startp, rate), :], v_ref[_rows(start, rate), :]], axis=0).astype(BF16)
            o, ls = _attn_fwd_block(q, kcat, vcat, kmin != 0, lo, biases)
            o_ref[_rows(start, rate), :] = o
            l_ref[_rows(start, rate), :] = ls
            return carry

        lax.fori_loop(0, T // NBACK, step, 0, unroll=4)

    pm = pl.BlockSpec((None, T, PAIR), lambda p: (p, 0, 0))
    in_specs, args = [pm, pm, pm], [qn, kn, v]
    if dep is not None:
        in_specs.append(HBM_SPEC)
        args.append(dep)
    return pl.pallas_call(
        body, name=name, grid=(NPAIR,), in_specs=in_specs, out_specs=[pm, pm],
        out_shape=[_sds((NPAIR, T, PAIR), F32)] * 2, compiler_params=_cparams(1))(*args)


def _attn_fwd_gathered(qn, kn, v, rate, name, dep):
    n = T // rate
    nblk = n // NBACK

    def body(q_hbm, k_hbm, v_hbm, *rest):
        o_hbm, l_hbm, qb, kb, vb, ob, lb, in_sem, out_sem = rest[-9:]
        p = pl.program_id(0)
        slot = p % 2

        def loads(pair, s):
            return [pltpu.make_async_copy(x.at[pair, :, r, :], buf.at[s, r], in_sem.at[3 * s + a])
                    for a, (x, buf) in enumerate(((q_hbm, qb), (k_hbm, kb), (v_hbm, vb))) for r in range(rate)]

        def stores(pair, s):
            return [pltpu.make_async_copy(buf.at[s, r], x.at[pair, :, r, :], out_sem.at[2 * s + a])
                    for a, (x, buf) in enumerate(((o_hbm, ob), (l_hbm, lb))) for r in range(rate)]

        @pl.when(p == 0)
        def _():
            for c in loads(0, 0):
                c.start()

        @pl.when(p + 1 < NPAIR)
        def _():
            for c in loads(p + 1, 1 - slot):
                c.start()

        for c in loads(p, slot):
            c.wait()

        @pl.when(p >= 2)
        def _():
            for c in stores(p - 2, slot):
                c.wait()

        lo = _lo_mask((NBACK, PAIR))
        biases = _band_bias()

        def step(it, carry):
            b, r = it % nblk, it // nblk
            cur = pl.ds(pl.multiple_of(b * NBACK, NBACK), NBACK)
            prev = pl.ds(pl.multiple_of(jnp.maximum(b - 1, 0) * NBACK, NBACK), NBACK)
            q = qb[slot, r, cur, :] * (HEAD ** -0.5)
            kcat = jnp.concatenate([kb[slot, r, prev, :], kb[slot, r, cur, :]], axis=0).astype(BF16)
            vcat = jnp.concatenate([vb[slot, r, prev, :], vb[slot, r, cur, :]], axis=0).astype(BF16)
            o, ls = _attn_fwd_block(q, kcat, vcat, b == 0, lo, biases)
            ob[slot, r, cur, :] = o
            lb[slot, r, cur, :] = ls
            return carry

        lax.fori_loop(0, T // NBACK, step, 0, unroll=4)

        for c in stores(p, slot):
            c.start()

        @pl.when(p == NPAIR - 1)
        def _():
            for c in stores(p - 1, 1 - slot) + stores(p, slot):
                c.wait()

    by_residue = lambda a: a.reshape(NPAIR, n, rate, PAIR)
    in_specs, args = [HBM_SPEC] * 3, [by_residue(qn), by_residue(kn), by_residue(v)]
    if dep is not None:
        in_specs.append(HBM_SPEC)
        args.append(dep)
    o, l = pl.pallas_call(
        body, name=name, grid=(NPAIR,), in_specs=in_specs, out_specs=[HBM_SPEC] * 2,
        out_shape=[_sds((NPAIR, n, rate, PAIR), F32)] * 2,
        scratch_shapes=[pltpu.VMEM((2, rate, n, PAIR), F32)] * 5
        + [pltpu.SemaphoreType.DMA((6,)), pltpu.SemaphoreType.DMA((4,))],
        compiler_params=_cparams(1))(*args)
    return o.reshape(NPAIR, T, PAIR), l.reshape(NPAIR, T, PAIR)


def _merge_fwd(cat_ab, outs, lses, tm=512):
    def body(cat_in, o0, o1, o2, l0, l1, l2, cat_ref, lse_ref):
        del cat_in
        for p in range(NPAIR):
            a0, a1, a2 = l0[p], l1[p], l2[p]
            m = jnp.maximum(jnp.maximum(a0, a1), a2)
            w0, w1, w2 = jnp.exp(a0 - m), jnp.exp(a1 - m), jnp.exp(a2 - m)
            s = w0 + w1 + w2
            b = (w0 * o0[p] + w1 * o1[p] + w2 * o2[p]) / s
            cat_ref[:, p * PAIR:(p + 1) * PAIR] = b.astype(BF16)
            lse_ref[p] = m + jnp.log(s)

    pm = pl.BlockSpec((NPAIR, tm, PAIR), lambda i: (0, i, 0))
    return pl.pallas_call(
        body, name="merge_fwd", grid=(T // tm,),
        in_specs=[pl.BlockSpec(memory_space=pl.ANY)] + [pm] * 6,
        out_specs=[pl.BlockSpec((tm, 512), lambda i: (i, 1)), pm],
        out_shape=[_sds((T, D), BF16), _sds((NPAIR, T, PAIR), F32)],
        input_output_aliases={0: 0}, compiler_params=_cparams(1))(cat_ab, *outs, *lses)


def _d_cat_ab(dxb, w_ab_out, cat, dep, tm=512):
    def epilogue(d, rows, side, outs):
        (b_ref,), (da_ref, dbp_ref, e_ref) = side, outs
        da_ref[rows, :] = d[:, 0:512].astype(BF16)
        lo = _lo_mask((EPI_ROWS, PAIR))
        for p in range(NPAIR):
            db = d[:, 512 + p * PAIR:512 + (p + 1) * PAIR]
            b = b_ref[rows, p * PAIR:(p + 1) * PAIR].astype(F32)
            dbp_ref[p, rows, :] = db
            e_ref[p, rows, :] = _seg_mean(db * b, lo) * float(HEAD)

    pm = pl.BlockSpec((NPAIR, tm, PAIR), lambda i: (0, i, 0))
    return _mm_nt_rows(
        dxb, w_ab_out, "mm_d_cat_ab", epilogue, [cat], [pl.BlockSpec((tm, 512), lambda i: (i, 1))],
        [pl.BlockSpec((tm, 512), lambda i: (i, 0)), pm, pm],
        [_sds((T, 512), BF16), _sds((NPAIR, T, PAIR), F32), _sds((NPAIR, T, PAIR), F32)], dep=dep, tm=tm)


def _attn_bwd_block(q, db, ev, ls, kcat, vcat, first, lo, biases):
    scale = HEAD ** -0.5
    nt = (((1,), (1,)), ((), ()))
    tn = (((0,), (0,)), ((), ()))
    q = q * scale
    q2 = jnp.concatenate([jnp.where(lo, q, 0.0), jnp.where(lo, 0.0, q)], axis=0).astype(BF16)
    db2 = jnp.concatenate([jnp.where(lo, db, 0.0), jnp.where(lo, 0.0, db)], axis=0).astype(BF16)
    ls2 = jnp.concatenate([ls[:, 0:1], ls[:, HEAD:HEAD + 1]], axis=0)
    ev2 = jnp.concatenate([ev[:, 0:1], ev[:, HEAD:HEAD + 1]], axis=0)
    s = lax.dot_general(q2, kcat, nt, preferred_element_type=F32)
    pt = jnp.exp(s + jnp.where(first, biases[1], biases[0]) - ls2)
    dp = lax.dot_general(db2, vcat, nt, preferred_element_type=F32)
    ds = (pt * (dp - ev2)).astype(BF16)
    dq2 = jnp.dot(ds, kcat, preferred_element_type=F32) * scale
    dkc = lax.dot_general(ds, q2, tn, preferred_element_type=F32)
    dvc = lax.dot_general(pt.astype(BF16), db2, tn, preferred_element_type=F32)
    return jnp.where(lo, dq2[0:NBACK], dq2[NBACK:]), dkc, dvc


def _attn_bwd_loop(read, write, nblk):
    lo = _lo_mask((NBACK, PAIR))
    biases = _band_bias()

    def one(it, carry):
        dk_carry, dv_carry = carry
        rho = it // nblk
        b = it % nblk
        bp = jnp.maximum(b - 1, 0)
        kcat = jnp.concatenate([read(1, rho, bp), read(1, rho, b)], axis=0).astype(BF16)
        vcat = jnp.concatenate([read(2, rho, bp), read(2, rho, b)], axis=0).astype(BF16)
        dq, dkc, dvc = _attn_bwd_block(read(0, rho, b), read(3, rho, b), read(4, rho, b), read(5, rho, b), kcat, vcat,
                                       b == 0, lo, biases)
        write(0, rho, b, dq)
        write(1, rho, bp, dk_carry + dkc[0:NBACK])
        write(1, rho, b, dkc[NBACK:])
        write(2, rho, bp, dv_carry + dvc[0:NBACK])
        write(2, rho, b, dvc[NBACK:])
        return dkc[NBACK:], dvc[NBACK:]

    def step(i, carry):
        for u in range(ATTN_BWD_UNROLL):
            carry = one(i * ATTN_BWD_UNROLL + u, carry)
        return carry

    zero = jnp.zeros((NBACK, PAIR), F32)
    lax.fori_loop(0, T // NBACK // ATTN_BWD_UNROLL, step, (zero, zero))


def _attn_bwd(qn, kn, v, dbp, e, lse, rate, name):
    if rate > MAX_ROW_STRIDE:
        return _attn_bwd_gathered(qn, kn, v, dbp, e, lse, rate, name)
    window = NBACK * rate

    def body(*refs):
        rows = lambda rho, b: _rows(b * window + rho, rate)

        def write(j, rho, b, value):
            refs[6 + j][rows(rho, b), :] = value

        _attn_bwd_loop(lambda j, rho, b: refs[j][rows(rho, b), :], write, T // window)

    pm = pl.BlockSpec((None, T, PAIR), lambda p: (p, 0, 0))
    return pl.pallas_call(
        body, name=name, grid=(NPAIR,), in_specs=[pm] * 6, out_specs=[pm] * 3,
        out_shape=[_sds((NPAIR, T, PAIR), F32)] * 3, compiler_params=_cparams(1, 56))(qn, kn, v, dbp, e, lse)


def _attn_bwd_gathered(qn, kn, v, dbp, e, lse, rate, name):
    n = T // rate

    def body(*refs):
        ins, outs, in_bufs, out_bufs, (in_sem, out_sem) = refs[0:6], refs[6:9], refs[9:15], refs[15:18], refs[18:20]
        p = pl.program_id(0)
        slot = p % 2

        def loads(pair, s):
            return [pltpu.make_async_copy(x.at[pair, :, r, :], buf.at[s, r], in_sem.at[6 * s + a])
                    for a, (x, buf) in enumerate(zip(ins, in_bufs)) for r in range(rate)]

        def stores(pair, s):
            return [pltpu.make_async_copy(buf.at[s, r], x.at[pair, :, r, :], out_sem.at[3 * s + a])
                    for a, (x, buf) in enumerate(zip(outs, out_bufs)) for r in range(rate)]

        @pl.when(p == 0)
        def _():
            for c in loads(0, 0):
                c.start()

        @pl.when(p + 1 < NPAIR)
        def _():
            for c in loads(p + 1, 1 - slot):
                c.start()

        for c in loads(p, slot):
            c.wait()

        @pl.when(p >= 2)
        def _():
            for c in stores(p - 2, slot):
                c.wait()

        rows = lambda b: pl.ds(pl.multiple_of(b * NBACK, NBACK), NBACK)

        def write(j, rho, b, value):
            out_bufs[j][slot, rho, rows(b), :] = value

        _attn_bwd_loop(lambda j, rho, b: in_bufs[j][slot, rho, rows(b), :], write, n // NBACK)

        for c in stores(p, slot):
            c.start()

        @pl.when(p == NPAIR - 1)
        def _():
            for c in stores(p - 1, 1 - slot) + stores(p, slot):
                c.wait()

    by_residue = lambda a: a.reshape(NPAIR, n, rate, PAIR)
    res = pl.pallas_call(
        body, name=name, grid=(NPAIR,), in_specs=[HBM_SPEC] * 6, out_specs=[HBM_SPEC] * 3,
        out_shape=[_sds((NPAIR, n, rate, PAIR), F32)] * 3,
        scratch_shapes=[pltpu.VMEM((2, rate, n, PAIR), F32)] * 9
        + [pltpu.SemaphoreType.DMA((12,)), pltpu.SemaphoreType.DMA((6,))],
        compiler_params=_cparams(1, 56))(*[by_residue(a) for a in (qn, kn, v, dbp, e, lse)])
    return [r.reshape(NPAIR, T, PAIR) for r in res]


def _ab_in_bwd(pab, dcat, sgu_g, sgu_b, sgu_w, sgu_bias3, qg, kg, tabs, dqkv, rinvs, tm=256):
    def body(p_hbm, dcat_ref, g_ref, b_ref, w_ref, bias_ref, qg_ref, kg_ref, c_ref, s1_ref, s2_ref, *rest):
        dq_refs, rinv_refs = rest[0:9], rest[9:15]
        o_ref, dwm_ref, dbias_ref, dsg_ref, dsb_ref, dgain_ref = rest[15:21]
        p_buf, sem = rest[21:]
        i = pl.program_id(0)
        p_ref = _ring_tile(lambda step: p_hbm.at[pl.ds(pl.multiple_of(step * tm, tm), tm), :], p_buf, sem, i, T // tm)

        @pl.when(i == 0)
        def _():
            dwm_ref[...] = jnp.zeros_like(dwm_ref)
            dbias_ref[...] = jnp.zeros_like(dbias_ref)
            dsg_ref[...] = jnp.zeros_like(dsg_ref)
            dsb_ref[...] = jnp.zeros_like(dsb_ref)
            dgain_ref[...] = jnp.zeros_like(dgain_ref)

        zu = p_ref[:, 0:512].astype(F32)
        zv = p_ref[:, 512:1024].astype(F32)
        u = _gelu(zu)
        v = _gelu(zv)
        mu = jnp.mean(v, axis=-1, keepdims=True)
        vc = v - mu
        rstd = lax.rsqrt(jnp.mean(vc * vc, axis=-1, keepdims=True) + EPS)
        xhat = vc * rstd
        vn = (xhat * g_ref[...] + b_ref[...]).astype(BF16)
        da = dcat_ref[...].astype(F32)
        tri = _tril_mask()
        du_parts = [[None] * 4 for _ in range(tm // 128)]
        dvn_parts = [[None] * 4 for _ in range(tm // 128)]
        for gi in range(4):
            wg = jnp.where(tri, w_ref[gi], 0.0).astype(BF16)
            bg = bias_ref[gi]
            for c in range(tm // 128):
                rs, cs = slice(c * 128, (c + 1) * 128), slice(gi * 128, (gi + 1) * 128)
                vblk = vn[rs, cs]
                mixed = jnp.dot(wg, vblk, preferred_element_type=F32) + bg
                dab = da[rs, cs]
                du_parts[c][gi] = dab * mixed
                dmixed = dab * u[rs, cs]
                dmb = dmixed.astype(BF16)
                dvn_parts[c][gi] = lax.dot_general(wg, dmb, (((0,), (0,)), ((), ())), preferred_element_type=F32)
                dwm = lax.dot_general(dmb, vblk, (((1,), (1,)), ((), ())), preferred_element_type=F32)
                dwm_ref[gi] += jnp.where(tri, dwm, 0.0)
                dbias_ref[gi] += dmixed
        du = jnp.concatenate([jnp.concatenate(r, axis=1) for r in du_parts], axis=0)
        dvn = jnp.concatenate([jnp.concatenate(r, axis=1) for r in dvn_parts], axis=0)
        dsg_ref[...] += jnp.sum(dvn * xhat, axis=0, keepdims=True)
        dsb_ref[...] += jnp.sum(dvn, axis=0, keepdims=True)
        dxh = dvn * g_ref[...]
        dv = rstd * (dxh - jnp.mean(dxh, axis=-1, keepdims=True)
                     - xhat * jnp.mean(dxh * xhat, axis=-1, keepdims=True))
        o_ref[:, 0:512] = (du * _gelu_grad(zu)).astype(BF16)
        o_ref[:, 512:1024] = (dv * _gelu_grad(zv)).astype(BF16)

        blocks = _head_blocks()
        c, s1, s2 = c_ref[...], s1_ref[...], s2_ref[...]
        for g in range(3):
            dq_ref, dk_ref, dv_ref = dq_refs[3 * g:3 * g + 3]
            for p in range(NPAIR):
                for which, gains, src in ((0, qg_ref, dq_ref), (1, kg_ref, dk_ref)):
                    col = (2 + 3 * which + g) * 512 + p * PAIR
                    xr = p_ref[:, col:col + PAIR].astype(F32)
                    rinv = rinv_refs[2 * g + which][p].astype(F32)
                    xh = xr * rinv
                    dn = _rope_t(src[p], c, s1, s2)
                    row = 2 * g + which
                    dgain_ref[row:row + 1, :] += jnp.sum(dn * xh, axis=0, keepdims=True)
                    dxh2 = dn * gains[g:g + 1, :]
                    dx = rinv * (dxh2 - xh * _seg_mean_mxu(dxh2 * xh, blocks))
                    o_ref[:, col:col + PAIR] = dx.astype(BF16)
                col = (8 + g) * 512 + p * PAIR
                o_ref[:, col:col + PAIR] = dv_ref[p].astype(BF16)

    pm = pl.BlockSpec((NPAIR, tm, PAIR), lambda i: (0, i, 0))
    tab = pl.BlockSpec((tm, PAIR), lambda i: (i, 0))
    gain = pl.BlockSpec((3, PAIR), lambda i: (0, 0))
    vec = pl.BlockSpec((1, 512), lambda i: (0, 0))
    full = pl.BlockSpec((tm, AB_IN), lambda i: (i, 0))
    w4 = pl.BlockSpec((4, 128, 128), lambda i: (0, 0, 0))
    return pl.pallas_call(
        body, name="ab_in_bwd", grid=(T // tm,),
        in_specs=[HBM_SPEC, pl.BlockSpec((tm, 512), lambda i: (i, 0)), vec, vec, w4,
                  pl.BlockSpec((4, 128, 1), lambda i: (0, 0, 0)), gain, gain, tab, tab, tab] + [pm] * 15,
        out_specs=[full, w4, w4, vec, vec, pl.BlockSpec((8, PAIR), lambda i: (0, 0))],
        out_shape=[_sds((T, AB_IN), BF16), _sds((4, 128, 128), F32), _sds((4, 128, 128), F32),
                   _sds((1, 512), F32), _sds((1, 512), F32), _sds((8, PAIR), F32)],
        scratch_shapes=_ring_scratch((tm, AB_IN), BF16), compiler_params=_cparams(1))(pab, dcat, sgu_g, sgu_b, sgu_w, sgu_bias3, qg, kg, *tabs, *dqkv, *rinvs)


def _ln_stats(x):
    mu = jnp.mean(x, axis=-1, keepdims=True)
    xc = x - mu
    rstd = lax.rsqrt(jnp.mean(xc * xc, axis=-1, keepdims=True) + EPS)
    return xc * rstd, rstd


CONV_RC = 64


def _shifted_copies(src, dst, tm):
    dst[0] = src[...]
    for b in range(1, 8):
        dst[b, 0:tm + HALO - 8, :] = src[pl.ds(b, tm + HALO - 8), :]


def _offsets_by_phase(first):
    groups = {}
    for o in range(first, first + CONV_C_TAPS):
        groups.setdefault(o % 8, []).append(o)
    return sorted(groups.items())


def _window(shifted, b8, base, offsets, lanes):
    rows = 8 * (max(offsets) // 8) + CONV_RC
    return shifted[b8, pl.ds(base, rows), lanes].reshape(rows // 8, 8, 128)


def _cd_fwd(pcd, cw, cb, lg, lb, dw, tm=512):
    per = tm // HALO

    def body(p_ref, h_ref, cw_ref, cb_ref, lg_ref, lb_ref, dw_ref, cat_ref, c0_ref, c1_ref, dd_ref, y_ref,
             buf, buf2, sb):
        i = pl.program_id(0)
        live = jnp.where(i > 0, 1.0, 0.0)
        a = p_ref[:, 0:512].astype(F32)
        gt = p_ref[:, 512:1024].astype(F32)
        gb = p_ref[:, 1024:1536].astype(F32)
        gc = p_ref[:, 1536:2048].astype(F32)
        hv = p_ref[:, 2048:2560].astype(F32)
        c0 = a * _sigmoid(gt)
        dd = gc * hv
        buf[0:HALO, :] = h_ref[:, 0:512].astype(F32) * _sigmoid(h_ref[:, 512:1024].astype(F32)) * live
        buf[HALO:, :] = c0
        buf2[0:HALO, :] = h_ref[:, 1536:2048].astype(F32) * h_ref[:, 2048:2560].astype(F32) * live
        buf2[HALO:, :] = dd
        c0_ref[...] = c0.astype(BF16)
        dd_ref[...] = dd.astype(BF16)
        _shifted_copies(buf, sb, tm)

        def conv_rows(r, carry):
            base = pl.multiple_of(r * CONV_RC, CONV_RC)
            for c in range(4):
                lanes = slice(c * 128, (c + 1) * 128)
                acc = jnp.broadcast_to(cb_ref[:, lanes], (CONV_RC // 8, 8, 128))
                for b8, offsets in _offsets_by_phase(HALO - (CONV_C_TAPS - 1)):
                    win = _window(sb, b8, base, offsets, lanes)
                    for o in offsets:
                        j = o - (HALO - (CONV_C_TAPS - 1))
                        acc = acc + cw_ref[8 * j:8 * j + 8, lanes] * win[o // 8:o // 8 + CONV_RC // 8]
                c1_ref[pl.ds(base, CONV_RC), lanes] = acc.reshape(CONV_RC, 128)
            return carry

        lax.fori_loop(0, tm // CONV_RC, conv_rows, 0)
        xhat, _ = _ln_stats(c1_ref[...])
        c2 = xhat * lg_ref[...] + lb_ref[...]
        y = jnp.zeros((tm, 512), F32)
        for j in range(CONV_D_TAPS):
            y = y + dw_ref[j:j + 1, :] * buf2[pl.ds(HALO - (CONV_D_TAPS - 1) + j, tm), :]
        cat_ref[:, 0:512] = (c2 * _sigmoid(c2)).astype(BF16)
        cat_ref[:, 512:1024] = (gb * y).astype(BF16)
        y_ref[...] = y.astype(BF16)

    half = pl.BlockSpec((tm, 512), lambda i: (i, 0))
    vec = pl.BlockSpec((1, 512), lambda i: (0, 0))
    return pl.pallas_call(
        body, name="cd_fwd", grid=(T // tm,),
        in_specs=[pl.BlockSpec((tm, CD_IN), lambda i: (i, 0)),
                  pl.BlockSpec((HALO, CD_IN), lambda i: (jnp.maximum(i * per - 1, 0), 0)),
                  pl.BlockSpec((8 * 32, 512), lambda i: (0, 0)), vec, vec, vec, pl.BlockSpec((8, 512), lambda i: (0, 0))],
        out_specs=[pl.BlockSpec((tm, D), lambda i: (i, 0)), half, half, half, half],
        out_shape=[_sds((T, D), BF16), _sds((T, 512), BF16), _sds((T, 512), F32), _sds((T, 512), BF16),
                   _sds((T, 512), BF16)],
        scratch_shapes=[pltpu.VMEM((HALO + tm, 512), F32), pltpu.VMEM((HALO + tm, 512), F32),
                        pltpu.VMEM((8, HALO + tm, 512), F32)],
        compiler_params=_cparams(1))(pcd, pcd, cw, cb, lg, lb, dw)


def _d_cat_cd(dxb, w_cd_out, c1, pcd, y, lg, lb, dep, tm=512):
    def epilogue(d, rows, side, outs):
        c1_ref, gb_ref, y_ref, lg_ref, lb_ref = side
        dc1_ref, dy3_ref, dgb_ref, dlg_ref, dlb_ref, dcb_ref = outs
        dc, ddo = d[:, 0:512], d[:, 512:1024]
        xhat, rstd = _ln_stats(c1_ref[rows, :])
        c2 = xhat * lg_ref[...] + lb_ref[...]
        sg = _sigmoid(c2)
        dc2 = dc * sg * (1.0 + c2 * (1.0 - sg))
        dlg_ref[...] += jnp.sum(dc2 * xhat, axis=0, keepdims=True)
        dlb_ref[...] += jnp.sum(dc2, axis=0, keepdims=True)
        dxh = dc2 * lg_ref[...]
        dc1 = rstd * (dxh - jnp.mean(dxh, axis=-1, keepdims=True)
                      - xhat * jnp.mean(dxh * xhat, axis=-1, keepdims=True))
        dcb_ref[...] += jnp.sum(dc1, axis=0, keepdims=True)
        dc1_ref[rows, :] = dc1
        dgb_ref[rows, :] = (ddo * y_ref[rows, :].astype(F32)).astype(BF16)
        dy3_ref[rows, :] = ddo * gb_ref[rows, :].astype(F32)

    half = pl.BlockSpec((tm, 512), lambda i: (i, 0))
    vec = pl.BlockSpec((1, 512), lambda i: (0, 0))
    return _mm_nt_rows(
        dxb, w_cd_out, "mm_d_cat_cd", epilogue, [c1, pcd, y, lg, lb],
        [half, pl.BlockSpec((tm, 512), lambda i: (i, 2)), half, vec, vec], [half, half, half, vec, vec, vec],
        [_sds((T, 512), F32), _sds((T, 512), F32), _sds((T, 512), BF16),
         _sds((1, 512), F32), _sds((1, 512), F32), _sds((1, 512), F32)], sums=(3, 4, 5), dep=dep, tm=tm)


def _cd_bwd_conv(pcd, dc1, dy3, c0, dd, dgb, cw8, dw, tm=256):
    per = tm // HALO
    nblk = T // tm
    last32 = T // HALO - 1

    def body(p_ref, dc1_ref, dc1n_ref, dy3_ref, dy3n_ref, c0_ref, dd_ref, dgb_ref, cw_ref, dw_ref,
             o_ref, dcw_ref, ddw_ref, dbuf, d3buf, sd, dc0_buf):
        i = pl.program_id(0)
        has_next = jnp.where(i < nblk - 1, 1.0, 0.0)

        @pl.when(i == 0)
        def _():
            dcw_ref[...] = jnp.zeros_like(dcw_ref)
            ddw_ref[...] = jnp.zeros_like(ddw_ref)

        dbuf[0:tm, :] = dc1_ref[...]
        dbuf[tm:, :] = dc1n_ref[...] * has_next
        d3buf[0:tm, :] = dy3_ref[...]
        d3buf[tm:, :] = dy3n_ref[...] * has_next
        _shifted_copies(dbuf, sd, tm)
        n_tiles = tm // CONV_RC

        phases = _offsets_by_phase(0)

        def dc0_rows(r, carry):
            base = pl.multiple_of(r * CONV_RC, CONV_RC)
            for c in range(4):
                lanes = slice(c * 128, (c + 1) * 128)
                acc = jnp.zeros((CONV_RC // 8, 8, 128), F32)
                for b8, offsets in phases:
                    win = _window(sd, b8, base, offsets, lanes)
                    for o in offsets:
                        j = CONV_C_TAPS - 1 - o
                        acc = acc + cw_ref[8 * j:8 * j + 8, lanes] * win[o // 8:o // 8 + CONV_RC // 8]
                dc0_buf[pl.ds(base, CONV_RC), lanes] = acc.reshape(CONV_RC, 128)
            return carry

        lax.fori_loop(0, n_tiles, dc0_rows, 0)

        for c in range(4):
            lanes = slice(c * 128, (c + 1) * 128)
            for b8, offsets in phases:
                def dw_rows(r, accs, lanes=lanes, b8=b8, offsets=offsets):
                    base = pl.multiple_of(r * CONV_RC, CONV_RC)
                    xin = c0_ref[pl.ds(base, CONV_RC), lanes].astype(F32).reshape(CONV_RC // 8, 8, 128)
                    win = _window(sd, b8, base, offsets, lanes)
                    return tuple(acc + jnp.sum(xin * win[o // 8:o // 8 + CONV_RC // 8], axis=0)
                                 for acc, o in zip(accs, offsets))

                accs = lax.fori_loop(0, n_tiles, dw_rows, tuple(jnp.zeros((8, 128), F32) for _ in offsets))
                for acc, o in zip(accs, offsets):
                    j = CONV_C_TAPS - 1 - o
                    dcw_ref[j:j + 1, lanes] += jnp.sum(acc, axis=0, keepdims=True)

        dc0 = dc0_buf[...]
        ddin = dd_ref[...].astype(F32)
        ddd = jnp.zeros((tm, 512), F32)
        for j in range(CONV_D_TAPS):
            dy_shift = d3buf[pl.ds(CONV_D_TAPS - 1 - j, tm), :]
            ddd = ddd + dw_ref[j:j + 1, :] * dy_shift
            ddw_ref[j:j + 1, :] += jnp.sum(ddin * dy_shift, axis=0, keepdims=True)

        a = p_ref[:, 0:512].astype(F32)
        gt = p_ref[:, 512:1024].astype(F32)
        gc = p_ref[:, 1536:2048].astype(F32)
        hv = p_ref[:, 2048:2560].astype(F32)
        sg = _sigmoid(gt)
        o_ref[:, 0:512] = (dc0 * sg).astype(BF16)
        o_ref[:, 512:1024] = (dc0 * a * sg * (1.0 - sg)).astype(BF16)
        o_ref[:, 1024:1536] = dgb_ref[...]
        o_ref[:, 1536:2048] = (ddd * hv).astype(BF16)
        o_ref[:, 2048:2560] = (ddd * gc).astype(BF16)

    half = pl.BlockSpec((tm, 512), lambda i: (i, 0))
    nxt = pl.BlockSpec((HALO, 512), lambda i: (jnp.minimum((i + 1) * per, last32), 0))
    full = pl.BlockSpec((tm, CD_IN), lambda i: (i, 0))
    return pl.pallas_call(
        body, name="cd_bwd_conv", grid=(nblk,),
        in_specs=[full, half, nxt, half, nxt, half, half, half,
                  pl.BlockSpec((8 * 32, 512), lambda i: (0, 0)), pl.BlockSpec((8, 512), lambda i: (0, 0))],
        out_specs=[full, pl.BlockSpec((32, 512), lambda i: (0, 0)), pl.BlockSpec((8, 512), lambda i: (0, 0))],
        out_shape=[_sds((T, CD_IN), BF16), _sds((32, 512), F32), _sds((8, 512), F32)],
        scratch_shapes=[pltpu.VMEM((tm + HALO, 512), F32), pltpu.VMEM((tm + HALO, 512), F32),
                        pltpu.VMEM((8, tm + HALO, 512), F32), pltpu.VMEM((tm, 512), F32)],
        compiler_params=_cparams(1))(pcd, dc1, dc1, dy3, dy3, c0, dd, dgb, cw8, dw)


def _local_step(x, tgt, W, fetch=None, on_grad=None):
    W = dict(W)
    if fetch is None:
        fetch = lambda stage, after: {}
    if on_grad is None:
        on_grad = lambda key, arr: None
    tabs = _rope_tables()
    qg = jnp.tile(W["q_norm_g"], (1, 2))
    kg = jnp.tile(W["k_norm_g"], (1, 2))
    bias3 = W["sgu_bias"].reshape(4, 128, 1)
    G = {}

    h0 = _rms_fwd(x, W["ab_norm_g"], "rms_fwd_ab", dep=W.get("dep_first"))
    W.update(fetch("ab_in", h0))
    pab = _mm_nt(h0, W["wt_ab_in"], "mm_ab_in", dep=W.get("dep0"))
    cat_ab = _mix_a_fwd(pab, W["sgu_norm_g"], W["sgu_norm_b"], W["sgu_w"], bias3)
    qkv, rinvs = _prep_fwd(pab, qg, kg, tabs)
    outs, lses = [], []
    for g, rate in enumerate(DIL_RATES):
        o, l = _attn_fwd(qkv[3 * g], qkv[3 * g + 1], qkv[3 * g + 2], rate, f"attn_fwd_{g}", dep=W.get(f"dep_attn{g}"))
        outs.append(o)
        lses.append(l)
        W.update(fetch(f"attn{g}", o))
    cat_ab, lse = _merge_fwd(cat_ab, outs, lses)
    W.update(fetch("ab_out", lse))
    x1, h1 = _mm_nn(cat_ab, W["w_ab_out"], "mm_ab_out", mode="rms", resid=x, gain=W["ffn_norm_g"][0:1])
    pf0, act0 = _ffn_in(h1, W["wt_ffn_in0"], "ffn_in0")
    W.update(fetch("ffn_down0", act0))
    x2, h2 = _mm_nn(act0, W["w_ffn_down0"], "mm_ffn_down0", mode="rms", resid=x1, gain=W["cd_norm_g"],
                    dep=W.get("dep_down0"))
    W.update(fetch("cd_in", h2))
    pcd = _mm_nt(h2, W["wt_cd_in"], "mm_cd_in")
    cw8 = jnp.repeat(W["conv_c_w32"], 8, axis=0)
    cat_cd, c0, c1, dd, yv = _cd_fwd(pcd, cw8, W["conv_c_b"], W["c_ln_g"], W["c_ln_b"], W["conv_d_w8"])
    x3, h3 = _mm_nn(cat_cd, W["w_cd_out"], "mm_cd_out", mode="rms", resid=x2, gain=W["ffn_norm_g"][1:2])
    pf1, act1 = _ffn_in(h3, W["wt_ffn_in1"], "ffn_in1")
    dy, dyb, loss_cols = _mm_nn(act1, W["w_ffn_down1"], "mm_ffn_down1", mode="loss", resid=x3, tgt=tgt)

    def ffn_bwd(xin, h, pf, act, dres, dresb, layer):
        G[f"w_ffn_down{layer}"] = _mm_tn(act, dresb, f"mm_g_ffn_down{layer}")
        dep = on_grad(f"w_ffn_down{layer}", G[f"w_ffn_down{layer}"])
        dpf = _ffn_dact(dresb, W[f"w_ffn_down{layer}"], pf, f"ffn_dact{layer}", dep=dep)
        G[f"wt_ffn_in{layer}"] = _mm_tn(dpf, h, f"mm_g_ffn_in{layer}")
        dep = on_grad(f"wt_ffn_in{layer}", G[f"wt_ffn_in{layer}"])
        dx, dxb, G[f"ffn_norm_g{layer}"] = _mm_dh_rms_bwd(
            dpf, W[f"wt_ffn_in{layer}"], xin, W["ffn_norm_g"][layer:layer + 1], dres, f"mm_d_h_ffn{layer}", dep=dep)
        return dx, dxb

    dx3, dx3b = ffn_bwd(x3, h3, pf1, act1, dy, dyb, 1)

    G["w_cd_out"] = _mm_tn(cat_cd, dx3b, "mm_g_cd_out")
    dep = on_grad("w_cd_out", G["w_cd_out"])
    dc1, dy3, dgb, G["c_ln_g"], G["c_ln_b"], G["conv_c_b"] = _d_cat_cd(
        dx3b, W["w_cd_out"], c1, pcd, yv, W["c_ln_g"], W["c_ln_b"], dep)
    dpcd, G["conv_c_w32"], G["conv_d_w8"] = _cd_bwd_conv(pcd, dc1, dy3, c0, dd, dgb, cw8, W["conv_d_w8"])
    G["wt_cd_in"] = _mm_tn(dpcd, h2, "mm_g_cd_in")
    dep = on_grad("wt_cd_in", G["wt_cd_in"])
    dx2, dx2b, G["cd_norm_g"] = _mm_dh_rms_bwd(dpcd, W["wt_cd_in"], x2, W["cd_norm_g"], dx3, "mm_d_h_cd", dep=dep)

    dx1, dx1b = ffn_bwd(x1, h1, pf0, act0, dx2, dx2b, 0)

    G["w_ab_out"] = _mm_tn(cat_ab, dx1b, "mm_g_ab_out")
    dep = on_grad("w_ab_out", G["w_ab_out"])
    dcat_a, dbp, e = _d_cat_ab(dx1b, W["w_ab_out"], cat_ab, dep)
    dqkv = []
    for g, rate in enumerate(DIL_RATES):
        dqkv += _attn_bwd(qkv[3 * g], qkv[3 * g + 1], qkv[3 * g + 2], dbp, e, lse, rate, f"attn_bwd_{g}")
    dpab, G["sgu_w"], dbias_part, G["sgu_norm_g"], G["sgu_norm_b"], dgain = _ab_in_bwd(
        pab, dcat_a, W["sgu_norm_g"], W["sgu_norm_b"], W["sgu_w"], bias3, qg, kg, tabs, dqkv, rinvs)
    G["sgu_bias"] = jnp.sum(dbias_part, axis=-1)
    dgain = dgain[0:6, 0:HEAD] + dgain[0:6, HEAD:PAIR]
    G["q_norm_g"] = dgain[0::2]
    G["k_norm_g"] = dgain[1::2]
    G["loss_cols"] = loss_cols
    dep = on_grad("small", G)
    G["wt_ab_in"] = _mm_tn(dpab, h0, "mm_g_ab_in", dep=dep)
    dep = on_grad("wt_ab_in", G["wt_ab_in"])
    grad_x, G["ab_norm_g"] = _mm_dh_rms_bwd(dpab, W["wt_ab_in"], x, W["ab_norm_g"], dx1, "mm_d_h_ab", dep=dep,
                                            bf16_copy=False)
    return loss_cols, grad_x, G


def _my_place():
    return lax.axis_index("x"), lax.axis_index("y"), lax.axis_index("c")


def _dev_index(px, py, pc):
    return 4 * px + 2 * py + pc


def _flip(place, k):
    x, y, c = place
    return (1 - x if k & 4 else x, 1 - y if k & 2 else y, 1 - c if k & 1 else c)


def _landing(shape, dtype, own):
    buf = lax.empty(shape, dtype)
    for lead, part in own:
        buf = lax.dynamic_update_slice(buf, part.reshape((1,) * len(lead) + part.shape),
                                       tuple(lead) + (0,) * part.ndim)
    return buf


HBM_ONLY = pl.BlockSpec(memory_space=pltpu.HBM)
SEM_SPEC = pl.BlockSpec(memory_space=pltpu.SEMAPHORE)
IN_FLIGHT = pltpu.CompilerParams(has_side_effects=pltpu.SideEffectType.DATAFLOW_SIDE_EFFECTING)


def _in_hbm(a):
    return pltpu.with_memory_space_constraint(a, pltpu.HBM)


def _exchange_start(name, srcs, lands, items, dep=None):
    ns, nl, ni = len(srcs), len(lands), len(items)

    def body(*refs):
        S, L = refs[0:ns], refs[ns:ns + nl]
        first_out = ns + nl + (0 if dep is None else 1)
        send_sems, recv_sems, token = refs[first_out], refs[first_out + 1], refs[-1]
        me = _my_place()
        mi = _dev_index(*me)
        for i, (src, dst) in enumerate(items):
            for k in range(1, NDEV):
                peer = _flip(me, k)
                pltpu.make_async_remote_copy(
                    src_ref=src(S, _dev_index(*peer)), dst_ref=dst(L, mi), send_sem=send_sems.at[7 * i + k - 1],
                    recv_sem=recv_sems.at[7 * i + k - 1], device_id=peer, device_id_type=MESH).start()
        token[...] = jnp.zeros_like(token)

    thru = [pltpu.HBM(a.shape, a.dtype) for a in list(srcs) + list(lands)]
    args = [_in_hbm(a) for a in srcs] + [_in_hbm(a) for a in lands]
    in_specs = [HBM_ONLY] * (ns + nl)
    if dep is not None:
        args.append(dep)
        in_specs.append(HBM_SPEC)
    outs = pl.pallas_call(
        body, name=name, in_specs=in_specs,
        out_shape=(pltpu.SemaphoreType.DMA((7 * ni,)), pltpu.SemaphoreType.DMA((7 * ni,)), *thru, _sds((8, 128), F32)),
        out_specs=(SEM_SPEC, SEM_SPEC, *[HBM_ONLY] * (ns + nl), pl.BlockSpec(memory_space=pltpu.VMEM)),
        input_output_aliases={j: 2 + j for j in range(ns + nl)}, compiler_params=IN_FLIGHT)(*args)
    return dict(send=outs[0], recv=outs[1], srcs=list(outs[2:2 + ns]), lands=list(outs[2 + ns:2 + ns + nl]),
                token=outs[-1], items=items)


def _exchange_wait(name, states, after):
    after = list(after) if isinstance(after, (list, tuple)) else [after]
    counts = [(len(st["srcs"]), len(st["lands"]), len(st["items"])) for st in states]
    n_arrays = sum(c[0] + c[1] for c in counts)

    def body(*refs):
        me = _my_place()
        mi = _dev_index(*me)
        pos = 0
        sem_pos = n_arrays
        for st, (ns, nl, ni) in zip(states, counts):
            S, L = refs[pos:pos + ns], refs[pos + ns:pos + ns + nl]
            send_sems, recv_sems = refs[sem_pos], refs[sem_pos + 1]
            pos += ns + nl
            sem_pos += 2
            for i, (src, dst) in enumerate(st["items"]):
                for k in range(1, NDEV):
                    cp = pltpu.make_async_remote_copy(
                        src_ref=src(S, mi), dst_ref=dst(L, mi), send_sem=send_sems.at[7 * i + k - 1],
                        recv_sem=recv_sems.at[7 * i + k - 1], device_id=me, device_id_type=MESH)
                    cp.wait_send()
                    cp.wait_recv()

    arrays, sems = [], []
    for st in states:
        arrays += st["srcs"] + st["lands"]
        sems += [st["send"], st["recv"]]
    outs = pl.pallas_call(
        body, name=name, in_specs=[HBM_ONLY] * n_arrays + [SEM_SPEC] * len(sems) + [HBM_SPEC] * len(after),
        out_shape=tuple(pltpu.HBM(a.shape, a.dtype) for a in arrays), out_specs=tuple([HBM_ONLY] * n_arrays),
        input_output_aliases={j: j for j in range(n_arrays)}, compiler_params=IN_FLIGHT)(*arrays, *sems, *after)
    lands, pos = [], 0
    for ns, nl, _ in counts:
        lands.append(list(outs[pos + ns:pos + ns + nl]))
        pos += ns + nl
    return lands


def _place_and_neighbours():
    x, y, c = _my_place()
    return (x, y, c), (x, y, 1 - c), [(1 - x, y), (x, 1 - y), (1 - x, 1 - y)]


def _gather_start(name, srcs, lands, items, dep=None):
    ns, nl, ni = len(srcs), len(lands), len(items)

    def body(*refs):
        S, L = refs[0:ns], refs[ns:ns + nl]
        first_out = ns + nl + (0 if dep is None else 1)
        send_sems, recv_sems, token = refs[first_out], refs[first_out + 1], refs[-1]
        me, sib, chips = _place_and_neighbours()
        mi = _dev_index(*me)
        for i, (src, dst) in enumerate(items):
            for k, to in enumerate([sib] + [(*chip, me[2]) for chip in chips]):
                pltpu.make_async_remote_copy(
                    src_ref=src(S), dst_ref=dst(L, mi), send_sem=send_sems.at[4 * i + k],
                    recv_sem=recv_sems.at[4 * i + k], device_id=to, device_id_type=MESH).start()
        token[...] = jnp.zeros_like(token)

    thru = [pltpu.HBM(a.shape, a.dtype) for a in list(srcs) + list(lands)]
    args = [_in_hbm(a) for a in srcs] + [_in_hbm(a) for a in lands]
    in_specs = [HBM_ONLY] * (ns + nl)
    if dep is not None:
        args.append(dep)
        in_specs.append(HBM_SPEC)
    outs = pl.pallas_call(
        body, name=name, in_specs=in_specs,
        out_shape=(pltpu.SemaphoreType.DMA((4 * ni,)), pltpu.SemaphoreType.DMA((4 * ni,)), *thru, _sds((8, 128), F32)),
        out_specs=(SEM_SPEC, SEM_SPEC, *[HBM_ONLY] * (ns + nl), pl.BlockSpec(memory_space=pltpu.VMEM)),
        input_output_aliases={j: 2 + j for j in range(ns + nl)}, compiler_params=IN_FLIGHT)(*args)
    return dict(send=outs[0], recv=outs[1], srcs=list(outs[2:2 + ns]), lands=list(outs[2 + ns:2 + ns + nl]),
                token=outs[-1], items=items)


def _gather_forward(name, st, after):
    nl, ni = len(st["lands"]), len(st["items"])

    def body(*refs):
        L, recv_sems = refs[0:nl], refs[nl]
        fwd_send, fwd_recv, token = refs[-3:]
        me, sib, chips = _place_and_neighbours()
        for i, (_, dst) in enumerate(st["items"]):
            for j, chip in enumerate(chips):
                blk = dst(L, _dev_index(*chip, me[2]))
                pltpu.make_async_remote_copy(
                    src_ref=blk, dst_ref=blk, send_sem=fwd_send.at[3 * i + j], recv_sem=recv_sems.at[4 * i + 1 + j],
                    device_id=me, device_id_type=MESH).wait_recv()
                pltpu.make_async_remote_copy(
                    src_ref=blk, dst_ref=blk, send_sem=fwd_send.at[3 * i + j], recv_sem=fwd_recv.at[3 * i + j],
                    device_id=sib, device_id_type=MESH).start()
        token[...] = jnp.zeros_like(token)

    after = list(after) if isinstance(after, (list, tuple)) else [after]
    outs = pl.pallas_call(
        body, name=name, in_specs=[HBM_ONLY] * nl + [SEM_SPEC] + [HBM_SPEC] * len(after),
        out_shape=(*[pltpu.HBM(a.shape, a.dtype) for a in st["lands"]], pltpu.SemaphoreType.DMA((3 * ni,)),
                   pltpu.SemaphoreType.DMA((3 * ni,)), _sds((8, 128), F32)),
        out_specs=(*[HBM_ONLY] * nl, SEM_SPEC, SEM_SPEC, pl.BlockSpec(memory_space=pltpu.VMEM)),
        input_output_aliases={j: j for j in range(nl)}, compiler_params=IN_FLIGHT)(*st["lands"], st["recv"], *after)
    return dict(st, lands=list(outs[0:nl]), fwd_send=outs[nl], fwd_recv=outs[nl + 1], token=outs[-1])


def _gather_wait(name, st, after):
    ns, nl, ni = len(st["srcs"]), len(st["lands"]), len(st["items"])

    def body(*refs):
        S, L = refs[0:ns], refs[ns:ns + nl]
        send_sems, recv_sems, fwd_send, fwd_recv = refs[ns + nl:ns + nl + 4]
        me, sib, chips = _place_and_neighbours()
        mi = _dev_index(*me)
        for i, (src, dst) in enumerate(st["items"]):
            mine = dst(L, mi)
            for k in range(4):
                pltpu.make_async_remote_copy(
                    src_ref=src(S), dst_ref=mine, send_sem=send_sems.at[4 * i + k], recv_sem=recv_sems.at[4 * i + k],
                    device_id=me, device_id_type=MESH).wait_send()
            pltpu.make_async_remote_copy(
                src_ref=src(S), dst_ref=mine, send_sem=send_sems.at[4 * i], recv_sem=recv_sems.at[4 * i],
                device_id=me, device_id_type=MESH).wait_recv()
            for j in range(3):
                cp = pltpu.make_async_remote_copy(
                    src_ref=mine, dst_ref=mine, send_sem=fwd_send.at[3 * i + j], recv_sem=fwd_recv.at[3 * i + j],
                    device_id=me, device_id_type=MESH)
                cp.wait_send()
                cp.wait_recv()

    arrays = st["srcs"] + st["lands"]
    outs = pl.pallas_call(
        body, name=name, in_specs=[HBM_ONLY] * (ns + nl) + [SEM_SPEC] * 4 + [HBM_SPEC],
        out_shape=tuple(pltpu.HBM(a.shape, a.dtype) for a in arrays), out_specs=tuple([HBM_ONLY] * (ns + nl)),
        input_output_aliases={j: j for j in range(ns + nl)},
        compiler_params=IN_FLIGHT)(*arrays, st["send"], st["recv"], st["fwd_send"], st["fwd_recv"], after)
    return list(outs[ns:ns + nl])


def _sum_slots(land):
    def body(l_ref, o_ref):
        acc = l_ref[0]
        for d in range(1, NDEV):
            acc = acc + l_ref[d]
        o_ref[...] = acc

    vm = pl.BlockSpec(memory_space=pltpu.VMEM)
    return pl.pallas_call(body, name="sum_small", out_shape=_sds(land.shape[1:], F32), in_specs=[vm], out_specs=vm)(land)


def _adam_math(w, g, m, v):
    m2 = ADAM_B1 * m + (1.0 - ADAM_B1) * g
    v2 = ADAM_B2 * v + (1.0 - ADAM_B2) * (g * g)
    delta = -ADAM_LR * ((m2 * ADAM_C1) / (jnp.sqrt(v2 * ADAM_C2) + ADAM_EPS) + ADAM_WD * w)
    return delta, m2, v2


def _adam_layer(land, sel, w, m, v, layer, name, prev=None, tc=512):
    R = land.shape[2]

    def body(l_ref, w_ref, m_ref, v_ref, *rest):
        g_out, d_out, m_out, v_out = rest[-4:]
        g = l_ref[0].astype(F32)
        for d in range(1, NDEV):
            g = g + l_ref[d].astype(F32)
        delta, m2, v2 = _adam_math(w_ref[...], g, m_ref[...], v_ref[...])
        g_out[...] = g
        d_out[...] = delta
        m_out[...] = m2
        v_out[...] = v2

    wspec = pl.BlockSpec((None, R, tc), lambda i: (layer, 0, i))
    in_specs = [pl.BlockSpec((None, NDEV, R, tc), lambda i: (sel, 0, 0, i)), wspec, wspec, wspec]
    args = [land, w, m, v]
    aliases = {}
    if prev is not None:
        in_specs += [HBM_SPEC] * 4
        args += list(prev)
        aliases = {4 + j: j for j in range(4)}
    return pl.pallas_call(
        body, name=name, grid=(D // tc,), in_specs=in_specs, out_specs=[wspec] * 4,
        out_shape=[_sds(w.shape, F32)] * 4, input_output_aliases=aliases, compiler_params=_cparams(1))(*args)


def _adam_stacked(lands, sel, w, m, v, name):
    res = None
    for layer, land in enumerate(lands):
        res = _adam_layer(land, sel, w, m, v, layer, f"{name}{layer}", prev=res)
    return res


def _adam_small(ws, gs, ms, vs):
    n = len(ws)

    def body(*refs):
        w_r, g_r, m_r, v_r = refs[0:n], refs[n:2 * n], refs[2 * n:3 * n], refs[3 * n:4 * n]
        d_o, m_o, v_o = refs[4 * n:5 * n], refs[5 * n:6 * n], refs[6 * n:7 * n]
        for i in range(n):
            delta, m2, v2 = _adam_math(w_r[i][...], g_r[i][...], m_r[i][...], v_r[i][...])
            d_o[i][...] = delta
            m_o[i][...] = m2
            v_o[i][...] = v2

    vm = pl.BlockSpec(memory_space=pltpu.VMEM)
    shapes = [_sds(w.shape, F32) for w in ws]
    outs = pl.pallas_call(body, name="adam_small", in_specs=[vm] * (4 * n), out_specs=[vm] * (3 * n),
                          out_shape=shapes * 3)(*ws, *gs, *ms, *vs)
    return outs[0:n], outs[n:2 * n], outs[2 * n:3 * n]


def _adam_of_slots(land, w, m, v, name):
    def body(l_ref, w_ref, m_ref, v_ref, g_o, d_o, m_o, v_o):
        g = l_ref[0]
        for d in range(1, NDEV):
            g = g + l_ref[d]
        g_o[...] = g
        d_o[...], m_o[...], v_o[...] = _adam_math(w_ref[...], g, m_ref[...], v_ref[...])

    vm = pl.BlockSpec(memory_space=pltpu.VMEM)
    return pl.pallas_call(body, name=name, in_specs=[vm] * 4, out_specs=[vm] * 4,
                          out_shape=[_sds(w.shape, F32)] * 4)(land, w, m, v)


WEIGHT_NAMES = ("ab_norm_g", "ab_w_in", "sgu_norm_g", "sgu_norm_b", "sgu_w", "sgu_bias", "q_norm_g", "k_norm_g",
                "ab_w_out", "cd_norm_g", "cd_w_in", "conv_c_w", "conv_c_b", "c_ln_g", "c_ln_b", "conv_d_w",
                "cd_w_out", "ffn_norm_g", "ffn_w_gate", "ffn_w_up", "ffn_w_down")
SMALL_SHAPES = (("sgu_norm_g", (1, 512)), ("sgu_norm_b", (1, 512)), ("sgu_w", (512, 128)),
                ("sgu_bias", (4, 128)), ("q_norm_g", (3, 1, 64)), ("k_norm_g", (3, 1, 64)), ("cd_norm_g", (1, 128)),
                ("conv_c_w", (31, 1, 64)), ("conv_c_b", (1, 64)), ("c_ln_g", (1, 64)), ("c_ln_b", (1, 64)),
                ("conv_d_w", (3, 1, 64)), ("ffn_norm_g", (2, 1024)))
SHARD_C = 64


def _pack_rows(parts, rows):
    flat = jnp.concatenate([p.reshape(-1) for p in parts])
    return jnp.pad(flat, (0, rows * 128 - flat.shape[0])).reshape(rows, 128)


def kernel(x, ab_norm_g, ab_w_in, sgu_norm_g, sgu_norm_b, sgu_w, sgu_bias, q_norm_g, k_norm_g, ab_w_out, cd_norm_g, cd_w_in, conv_c_w, conv_c_b, c_ln_g, c_ln_b, conv_d_w, cd_w_out, ffn_norm_g, ffn_w_gate, ffn_w_up, ffn_w_down, loss_target, m_ab_norm_g, m_ab_w_in, m_sgu_norm_g, m_sgu_norm_b, m_sgu_w, m_sgu_bias, m_q_norm_g, m_k_norm_g, m_ab_w_out, m_cd_norm_g, m_cd_w_in, m_conv_c_w, m_conv_c_b, m_c_ln_g, m_c_ln_b, m_conv_d_w, m_cd_w_out, m_ffn_norm_g, m_ffn_w_gate, m_ffn_w_up, m_ffn_w_down, v_ab_norm_g, v_ab_w_in, v_sgu_norm_g, v_sgu_norm_b, v_sgu_w, v_sgu_bias, v_q_norm_g, v_k_norm_g, v_ab_w_out, v_cd_norm_g, v_cd_w_in, v_conv_c_w, v_conv_c_b, v_c_ln_g, v_c_ln_b, v_conv_d_w, v_cd_w_out, v_ffn_norm_g, v_ffn_w_gate, v_ffn_w_up, v_ffn_w_down):
    w = dict(zip(WEIGHT_NAMES, (ab_norm_g, ab_w_in, sgu_norm_g, sgu_norm_b, sgu_w, sgu_bias, q_norm_g, k_norm_g, ab_w_out, cd_norm_g, cd_w_in, conv_c_w, conv_c_b, c_ln_g, c_ln_b, conv_d_w, cd_w_out, ffn_norm_g, ffn_w_gate, ffn_w_up, ffn_w_down)))
    m = dict(zip(WEIGHT_NAMES, (m_ab_norm_g, m_ab_w_in, m_sgu_norm_g, m_sgu_norm_b, m_sgu_w, m_sgu_bias, m_q_norm_g, m_k_norm_g, m_ab_w_out, m_cd_norm_g, m_cd_w_in, m_conv_c_w, m_conv_c_b, m_c_ln_g, m_c_ln_b, m_conv_d_w, m_cd_w_out, m_ffn_norm_g, m_ffn_w_gate, m_ffn_w_up, m_ffn_w_down)))
    v = dict(zip(WEIGHT_NAMES, (v_ab_norm_g, v_ab_w_in, v_sgu_norm_g, v_sgu_norm_b, v_sgu_w, v_sgu_bias, v_q_norm_g, v_k_norm_g, v_ab_w_out, v_cd_norm_g, v_cd_w_in, v_conv_c_w, v_conv_c_b, v_c_ln_g, v_c_ln_b, v_conv_d_w, v_cd_w_out, v_ffn_norm_g, v_ffn_w_gate, v_ffn_w_up, v_ffn_w_down)))
    me = _dev_index(*_my_place())

    r_ff = DFF // NDEV
    one = lambda a: (lambda S, j: S[a])
    slot = lambda b: (lambda L, s: L[b].at[s])
    slot2 = lambda b, part: (lambda L, s: L[b].at[part, s])
    shard = lambda a: (lambda S: S[a])

    def later(a):
        return lax.optimization_barrier((a, gathers[0]["token"]))[0]

    def layer_shards(layer):
        return (later(w["ffn_w_gate"][layer]).T.astype(BF16), later(w["ffn_w_up"][layer]).T.astype(BF16),
                later(w["ffn_w_down"][layer]).astype(BF16))

    def gathered(own):
        return _landing((NDEV,) + own.shape, BF16, [((me,), own)])

    def gathered2(a, b):
        return _landing((2, NDEV) + a.shape, BF16, [((0, me), a), ((1, me), b)])

    ab_in_s = w["ab_w_in"][0].T.astype(BF16)
    gathers = {0: _gather_start("gather0_start", [ab_in_s], [gathered(ab_in_s)], [(shard(0), slot(0))])}

    def chan(flat, lo, taps):
        return flat[:, lo:lo + taps * SHARD_C].reshape(NDEV, taps, SHARD_C).transpose(1, 0, 2).reshape(taps, 512)

    def fetch(stage, after):
        if stage == "ab_in":
            ab_out_s = later(w["ab_w_out"][0]).astype(BF16)
            gate0, up0, down0 = layer_shards(0)
            small_s = _pack_rows([later(w[n]) for n in ("cd_norm_g", "conv_c_w", "conv_c_b", "c_ln_g", "c_ln_b",
                                                        "conv_d_w")], 24)
            lands1 = [gathered(ab_out_s), gathered2(gate0, up0), gathered(down0),
                      _landing((NDEV,) + small_s.shape, F32, [((me,), small_s)])]
            gathers[0] = _gather_forward("gather0_forward", gathers[0], [after] + lands1)
            l_ab_in, = _gather_wait("gather0_wait", gathers[0], gathers[0]["token"])
            gathers[1] = _gather_start(
                "gather1_start", [ab_out_s, gate0, up0, down0, small_s], lands1,
                [(shard(0), slot(0)), (shard(1), slot2(1, 0)), (shard(2), slot2(1, 1)), (shard(3), slot(2)),
                 (shard(4), slot(3))], dep=l_ab_in)
            return {"wt_ab_in": l_ab_in.reshape(AB_IN, D), "dep0": gathers[1]["token"]}
        if stage == "attn0":
            cd_in_s, cd_out_s = later(w["cd_w_in"][0]).T.astype(BF16), later(w["cd_w_out"][0]).astype(BF16)
            gate1, up1, down1 = layer_shards(1)
            gathers[2] = _gather_start(
                "gather2_start", [cd_in_s, cd_out_s, gate1, up1, down1],
                [gathered(cd_in_s), gathered(cd_out_s), gathered2(gate1, up1), gathered(down1)],
                [(shard(0), slot(0)), (shard(1), slot(1)), (shard(2), slot2(2, 0)), (shard(3), slot2(2, 1)),
                 (shard(4), slot(3))], dep=after)
            return {"dep_attn1": gathers[2]["token"]}
        if stage == "attn1":
            gathers[1] = _gather_forward("gather1_forward", gathers[1], after)
            return {"dep_attn2": gathers[1]["token"]}
        if stage == "ab_out":
            l_out, l_ffn, l_down, l_small = _gather_wait("gather1_wait", gathers[1], after)
            flat = l_small.reshape(NDEV, 24 * 128)
            return {
                "w_ab_out": l_out.reshape(D, D), "wt_ffn_in0": l_ffn.reshape(2 * DFF, D),
                "w_ffn_down0": l_down.reshape(DFF, D), "cd_norm_g": flat[:, 0:128].reshape(1, D),
                "conv_c_w32": jnp.pad(chan(flat, 128, CONV_C_TAPS), ((0, 1), (0, 0))),
                "conv_c_b": chan(flat, 2112, 1), "c_ln_g": chan(flat, 2176, 1), "c_ln_b": chan(flat, 2240, 1),
                "conv_d_w8": jnp.pad(chan(flat, 2304, CONV_D_TAPS), ((0, 8 - CONV_D_TAPS), (0, 0))),
            }
        if stage == "ffn_down0":
            gathers[2] = _gather_forward("gather2_forward", gathers[2], after)
            return {"dep_down0": gathers[2]["token"]}
        if stage == "cd_in":
            l_in, l_out, l_ffn, l_down = _gather_wait("gather2_wait", gathers[2], after)
            return {"wt_cd_in": l_in.reshape(CD_IN, D), "w_cd_out": l_out.reshape(D, D),
                    "wt_ffn_in1": l_ffn.reshape(2 * DFF, D), "w_ffn_down1": l_down.reshape(DFF, D)}
        return {}

    scatters = {}
    rides_with = {"w_ffn_down1": "wt_ffn_in1", "w_cd_out": "wt_cd_in", "w_ffn_down0": "wt_ffn_in0"}
    held = {}
    smalls = {}

    def small_exchange(name, block):
        land = _landing((NDEV,) + block.shape, F32, [((me,), block)])
        return _exchange_start(name, [block], [land], [(one(0), slot(0))])

    def on_grad(key, arr):
        if key == "small":
            parts = [arr["sgu_norm_g"], arr["sgu_norm_b"], arr["sgu_w"], arr["sgu_bias"], arr["q_norm_g"],
                     arr["k_norm_g"], arr["cd_norm_g"], arr["conv_c_w32"][:CONV_C_TAPS], arr["conv_c_b"], arr["c_ln_g"],
                     arr["c_ln_b"], arr["conv_d_w8"][:CONV_D_TAPS], arr["ffn_norm_g0"], arr["ffn_norm_g1"],
                     arr["loss_cols"]]
            smalls["sizes"] = [p.size for p in parts]
            rows = -(-sum(smalls["sizes"]) // 1024) * 8
            smalls["early"] = small_exchange("small_start", _pack_rows(parts, rows))
            return smalls["early"]["token"]
        if key in rides_with:
            held[rides_with[key]] = (key, arr)
            return None
        group = ([held.pop(key)] if key in held else []) + [(key, arr)]
        srcs, lands, items = [], [], []
        for n, (k, a) in enumerate(group):
            if k.startswith("wt_ffn_in"):
                src = a.reshape(2, NDEV, r_ff, D)
                own = lax.dynamic_slice_in_dim(src, me, 1, axis=1)
                lands.append(lax.dynamic_update_slice(lax.empty(src.shape, BF16), own, (0, me, 0, 0)))
                items += [((lambda S, j, n=n: S[n].at[0, j]), slot2(n, 0)), ((lambda S, j, n=n: S[n].at[1, j]), slot2(n, 1))]
            else:
                rows = a.shape[0] // NDEV
                src = a.reshape(NDEV, rows, D)
                own = lax.dynamic_index_in_dim(src, me, 0, keepdims=False)
                lands.append(_landing((1, NDEV, rows, D), BF16, [((0, me), own)]))
                items.append(((lambda S, j, n=n: S[n].at[j]), slot2(n, 0)))
            srcs.append(src)
        st = _exchange_start(f"scatter_{key}_start", srcs, lands, items)
        scatters[key] = (st, [k for k, _ in group])
        return st["token"]

    W = {
        "dep_first": gathers[0]["token"],
        "ab_norm_g": w["ab_norm_g"], "sgu_norm_g": w["sgu_norm_g"], "sgu_norm_b": w["sgu_norm_b"],
        "sgu_w": w["sgu_w"][0], "sgu_bias": w["sgu_bias"][0], "q_norm_g": w["q_norm_g"][0],
        "k_norm_g": w["k_norm_g"][0], "ffn_norm_g": w["ffn_norm_g"],
    }

    loss_cols, grad_x, G = _local_step(x[0], loss_target[0], W, fetch, on_grad)

    late_small = small_exchange("small_late_start", G["ab_norm_g"])
    landed = {}

    def wait_scatters(name, group_keys, others, after):
        res = _exchange_wait(name, [scatters[gk][0] for gk in group_keys] + others, after)
        for gk, lands in zip(group_keys, res):
            landed.update(zip(scatters[gk][1], lands))
        return [lands[0] for lands in res[len(group_keys):]]

    small_land, = wait_scatters("scatter_wait_early", ["wt_ffn_in1", "wt_cd_in", "wt_ffn_in0", "w_ab_out"],
                                [smalls["early"]], late_small["token"])

    grads, deltas, new_m, new_v = {}, {}, {}, {}
    done = []

    def put(name, res):
        grads[name], deltas[name], new_m[name], new_v[name] = res

    def adam(name, lands, sel, transposed):
        flip = (lambda a: jnp.swapaxes(a, 1, 2)) if transposed else (lambda a: a)
        res = _adam_stacked(lands, sel, flip(w[name]), flip(m[name]), flip(v[name]), f"adam_{name}")
        done.append(res[1])
        put(name, [flip(r) for r in res])

    ffn_in_lands = [landed["wt_ffn_in0"], landed["wt_ffn_in1"]]
    adam("cd_w_in", [landed["wt_cd_in"]], 0, True)
    adam("ffn_w_gate", ffn_in_lands, 0, True)
    adam("ffn_w_up", ffn_in_lands, 1, True)
    adam("cd_w_out", [landed["w_cd_out"]], 0, False)
    adam("ab_w_out", [landed["w_ab_out"]], 0, False)
    adam("ffn_w_down", [landed["w_ffn_down0"], landed["w_ffn_down1"]], 0, False)

    red = _sum_slots(small_land).reshape(-1)
    offs = [0]
    for s in smalls["sizes"]:
        offs.append(offs[-1] + s)
    seg = [red[offs[i]:offs[i + 1]] for i in range(len(smalls["sizes"]))]
    loss = jnp.sum(seg[14])

    def own_channels(full, taps):
        return lax.dynamic_slice_in_dim(full.reshape(taps, 512), me * SHARD_C, SHARD_C, axis=1)

    g_small = {
        "sgu_norm_g": seg[0].reshape(1, 512), "sgu_norm_b": seg[1].reshape(1, 512),
        "sgu_w": seg[2].reshape(512, 128), "sgu_bias": seg[3].reshape(4, 128), "q_norm_g": seg[4].reshape(3, 64),
        "k_norm_g": seg[5].reshape(3, 64),
        "cd_norm_g": lax.dynamic_slice_in_dim(seg[6].reshape(1, D), me * (D // NDEV), D // NDEV, axis=1),
        "conv_c_w": own_channels(seg[7], CONV_C_TAPS), "conv_c_b": own_channels(seg[8], 1),
        "c_ln_g": own_channels(seg[9], 1), "c_ln_b": own_channels(seg[10], 1),
        "conv_d_w": own_channels(seg[11], CONV_D_TAPS),
        "ffn_norm_g": jnp.concatenate([seg[12].reshape(1, D), seg[13].reshape(1, D)], axis=0),
    }

    def small_in(s, a):
        return jnp.swapaxes(a, 0, 1) if len(s) == 3 else a.reshape(s)

    def small_out(n, s, a):
        return jnp.swapaxes(a, 0, 1) if len(s) == 3 else a.reshape(w[n].shape)

    g_in = [g_small[n].reshape(s) for n, s in SMALL_SHAPES]
    d_s, m_s, v_s = _adam_small([small_in(s, w[n]) for n, s in SMALL_SHAPES], g_in,
                                [small_in(s, m[n]) for n, s in SMALL_SHAPES],
                                [small_in(s, v[n]) for n, s in SMALL_SHAPES])
    for i, (n, s) in enumerate(SMALL_SHAPES):
        grads[n], deltas[n] = small_out(n, s, g_in[i]), small_out(n, s, d_s[i])
        new_m[n], new_v[n] = small_out(n, s, m_s[i]), small_out(n, s, v_s[i])
    done.append(d_s[0])

    late_land, = wait_scatters("scatter_wait_last", ["wt_ab_in"], [late_small], list(done))
    put("ab_norm_g", _adam_of_slots(late_land, w["ab_norm_g"], m["ab_norm_g"], v["ab_norm_g"], "adam_ab_norm_g"))
    adam("ab_w_in", [landed["wt_ab_in"]], 0, True)

    return (loss, grad_x[None], *[grads[n] for n in WEIGHT_NAMES], *[deltas[n] for n in WEIGHT_NAMES],
            *[new_m[n] for n in WEIGHT_NAMES], *[new_v[n] for n in WEIGHT_NAMES])
```
